```python
import math
import jax, jax.numpy as jnp
from jax import lax
import numpy as np

D_MODEL = 1024
BATCH = 8
SEQ = 4096
DEPTH = 4

HEAD_DIM = 64
N_HEADS = 8
N_KV_HEADS = 2
WINDOW = 128
ATTN_WIDTH = N_HEADS * HEAD_DIM
KV_WIDTH = N_KV_HEADS * HEAD_DIM
CONV_CH = D_MODEL // 2
CONV_WIDTH = 31
MIX_WIDTH = ATTN_WIDTH + CONV_CH
IN_WIDTH = ATTN_WIDTH + 2 * KV_WIDTH + 2 * CONV_CH
D_FF = 2816
FFN_RESIDUAL_WEIGHT = 0.5
EPS = 1e-6
NEG_INF = -1e30

kernel_name = "hybrid_swa_sink_alibi_conformer_conv_macaron"


def rms_norm(x, g):
    xf = x.astype(jnp.float32)
    y = xf * lax.rsqrt(jnp.mean(xf * xf, axis=-1, keepdims=True) + EPS)
    return (y * g.astype(jnp.float32)).astype(x.dtype)


def swiglu_ffn(h, w_in, w_out):
    gu = h @ w_in
    gate, up = jnp.split(gu, 2, axis=-1)
    return (jax.nn.silu(gate) * up) @ w_out


def alibi_slopes(n_heads):
    return jnp.exp2(-8.0 * jnp.arange(1, n_heads + 1, dtype=jnp.float32) / n_heads)


def sliding_window_sink_attention(q, k, v, sinks):
    B, S, H, hd = q.shape
    nb = S // WINDOW
    G = H // N_KV_HEADS
    qb = q.reshape(B, nb, WINDOW, N_KV_HEADS, G, hd).astype(jnp.float32)

    def band(t):
        cur = t.reshape(B, nb, WINDOW, N_KV_HEADS, hd)
        prev = jnp.pad(cur, ((0, 0), (1, 0), (0, 0), (0, 0), (0, 0)))[:, :-1]
        return jnp.concatenate([prev, cur], axis=2).astype(jnp.float32)

    kb, vb = band(k), band(v)
    scores = jnp.einsum('bnqkgd,bnskd->bkgnqs', qb, kb) * (1.0 / math.sqrt(hd))

    t_loc = jnp.arange(WINDOW)[:, None]
    s_loc = jnp.arange(2 * WINDOW)[None, :]
    dist = t_loc + WINDOW - s_loc
    in_window = (dist >= 0) & (dist < WINDOW)
    blk = jnp.arange(nb)[:, None, None]
    valid = in_window[None] & ((blk > 0) | (s_loc >= WINDOW)[None])

    slopes = alibi_slopes(H).reshape(N_KV_HEADS, G)
    bias = -slopes[:, :, None, None] * jnp.abs(dist).astype(jnp.float32)[None, None]
    scores = jnp.where(valid[None, None, None], scores + bias[:, :, None], NEG_INF)

    sink = sinks.astype(jnp.float32).reshape(N_KV_HEADS, G)[None, :, :, None, None]
    m = jnp.maximum(jnp.max(scores, axis=-1), sink)
    p = jnp.exp(scores - m[..., None])
    denom = jnp.sum(p, axis=-1) + jnp.exp(sink - m)
    p = p / denom[..., None]
    out = jnp.einsum('bkgnqs,bnskd->bnqkgd', p, vb)
    return out.reshape(B, S, H * hd).astype(q.dtype)


def conformer_conv(u, w_dw, b_dw, ln_g, ln_b):
    a, gate = jnp.split(u, 2, axis=-1)
    z = a * jax.nn.sigmoid(gate)
    C = z.shape[-1]
    y = lax.conv_general_dilated(
        z, w_dw.astype(z.dtype)[:, None, :],
        window_strides=(1,), padding=[(CONV_WIDTH - 1, 0)],
        dimension_numbers=('NWC', 'WIO', 'NWC'), feature_group_count=C)
    y = (y + b_dw).astype(jnp.float32)
    mu = jnp.mean(y, axis=-1, keepdims=True)
    var = jnp.mean(jnp.square(y - mu), axis=-1, keepdims=True)
    y = (y - mu) * lax.rsqrt(var + EPS) * ln_g.astype(jnp.float32) + ln_b.astype(jnp.float32)
    return jax.nn.silu(y).astype(u.dtype)


def _fwd_setup_inputs(seed: int = 0) -> dict:
    key = jax.random.key(seed)
    ks = jax.random.split(key, 20)
    f32 = jnp.float32

    def nrm(k, shape, scale):
        return jax.random.normal(k, shape, f32) * scale

    def gain(k, shape):
        return 1.0 + 0.05 * jax.random.normal(k, shape, f32)

    return {
        "x": jax.random.normal(ks[0], (BATCH, SEQ, D_MODEL), f32),
        "norm_ffn1": gain(ks[1], (DEPTH, D_MODEL)),
        "w_ffn1_in": nrm(ks[2], (DEPTH, D_MODEL, 2 * D_FF), D_MODEL ** -0.5),
        "w_ffn1_out": nrm(ks[3], (DEPTH, D_FF, D_MODEL), D_FF ** -0.5),
        "norm_mix": gain(ks[4], (DEPTH, D_MODEL)),
        "w_in": nrm(ks[5], (DEPTH, D_MODEL, IN_WIDTH), D_MODEL ** -0.5),
        "sinks": nrm(ks[6], (DEPTH, N_HEADS), 1.0),
        "w_dw": nrm(ks[7], (DEPTH, CONV_WIDTH, CONV_CH), CONV_WIDTH ** -0.5),
        "b_dw": nrm(ks[8], (DEPTH, CONV_CH), 0.02),
        "conv_ln_g": gain(ks[9], (DEPTH, CONV_CH)),
        "conv_ln_b": nrm(ks[10], (DEPTH, CONV_CH), 0.02),
        "w_out": nrm(ks[11], (DEPTH, MIX_WIDTH, D_MODEL), MIX_WIDTH ** -0.5),
        "norm_ffn2": gain(ks[12], (DEPTH, D_MODEL)),
        "w_ffn2_in": nrm(ks[13], (DEPTH, D_MODEL, 2 * D_FF), D_MODEL ** -0.5),
        "w_ffn2_out": nrm(ks[14], (DEPTH, D_FF, D_MODEL), D_FF ** -0.5),
        "final_norm": gain(ks[15], (D_MODEL,)),
    }


def _fwd_reference(x, norm_ffn1, w_ffn1_in, w_ffn1_out, norm_mix, w_in, sinks, w_dw, b_dw,
              conv_ln_g, conv_ln_b, w_out, norm_ffn2, w_ffn2_in, w_ffn2_out, final_norm):
    B, S, _ = x.shape
    split_pts = [ATTN_WIDTH, ATTN_WIDTH + KV_WIDTH, ATTN_WIDTH + 2 * KV_WIDTH]
    for l in range(DEPTH):
        h = rms_norm(x, norm_ffn1[l])
        x = x + FFN_RESIDUAL_WEIGHT * swiglu_ffn(h, w_ffn1_in[l], w_ffn1_out[l])

        h = rms_norm(x, norm_mix[l])
        proj = h @ w_in[l]
        q, k, v, u = jnp.split(proj, split_pts, axis=-1)
        attn = sliding_window_sink_attention(
            q.reshape(B, S, N_HEADS, HEAD_DIM),
            k.reshape(B, S, N_KV_HEADS, HEAD_DIM),
            v.reshape(B, S, N_KV_HEADS, HEAD_DIM),
            sinks[l])
        conv = conformer_conv(u, w_dw[l], b_dw[l], conv_ln_g[l], conv_ln_b[l])
        x = x + jnp.concatenate([attn, conv], axis=-1) @ w_out[l]

        h = rms_norm(x, norm_ffn2[l])
        x = x + FFN_RESIDUAL_WEIGHT * swiglu_ffn(h, w_ffn2_in[l], w_ffn2_out[l])
    return rms_norm(x, final_norm)


import jax as _jax
import jax.numpy as _jnp

TWIN_FORMAT = 'train_step'
FWD_PARAMS = ['x', 'norm_ffn1', 'w_ffn1_in', 'w_ffn1_out', 'norm_mix', 'w_in', 'sinks', 'w_dw', 'b_dw', 'conv_ln_g', 'conv_ln_b', 'w_out', 'norm_ffn2', 'w_ffn2_in', 'w_ffn2_out', 'final_norm']
TWIN_WEIGHTS = ['norm_ffn1', 'w_ffn1_in', 'w_ffn1_out', 'norm_mix', 'w_in', 'sinks', 'w_dw', 'b_dw', 'conv_ln_g', 'conv_ln_b', 'w_out', 'norm_ffn2', 'w_ffn2_in', 'w_ffn2_out', 'final_norm']
TWIN_DIFF_INPUT = 'x'
TWIN_INPUTS = ['x', 'norm_ffn1', 'w_ffn1_in', 'w_ffn1_out', 'norm_mix', 'w_in', 'sinks', 'w_dw', 'b_dw', 'conv_ln_g', 'conv_ln_b', 'w_out', 'norm_ffn2', 'w_ffn2_in', 'w_ffn2_out', 'final_norm', 'loss_target', 'm_norm_ffn1', 'm_w_ffn1_in', 'm_w_ffn1_out', 'm_norm_mix', 'm_w_in', 'm_sinks', 'm_w_dw', 'm_b_dw', 'm_conv_ln_g', 'm_conv_ln_b', 'm_w_out', 'm_norm_ffn2', 'm_w_ffn2_in', 'm_w_ffn2_out', 'm_final_norm', 'v_norm_ffn1', 'v_w_ffn1_in', 'v_w_ffn1_out', 'v_norm_mix', 'v_w_in', 'v_sinks', 'v_w_dw', 'v_b_dw', 'v_conv_ln_g', 'v_conv_ln_b', 'v_w_out', 'v_norm_ffn2', 'v_w_ffn2_in', 'v_w_ffn2_out', 'v_final_norm']
TWIN_OUTPUTS = ['loss', 'grad_x', 'grad_norm_ffn1', 'grad_w_ffn1_in', 'grad_w_ffn1_out', 'grad_norm_mix', 'grad_w_in', 'grad_sinks', 'grad_w_dw', 'grad_b_dw', 'grad_conv_ln_g', 'grad_conv_ln_b', 'grad_w_out', 'grad_norm_ffn2', 'grad_w_ffn2_in', 'grad_w_ffn2_out', 'grad_final_norm', 'delta_norm_ffn1', 'delta_w_ffn1_in', 'delta_w_ffn1_out', 'delta_norm_mix', 'delta_w_in', 'delta_sinks', 'delta_w_dw', 'delta_b_dw', 'delta_conv_ln_g', 'delta_conv_ln_b', 'delta_w_out', 'delta_norm_ffn2', 'delta_w_ffn2_in', 'delta_w_ffn2_out', 'delta_final_norm', 'new_m_norm_ffn1', 'new_m_w_ffn1_in', 'new_m_w_ffn1_out', 'new_m_norm_mix', 'new_m_w_in', 'new_m_sinks', 'new_m_w_dw', 'new_m_b_dw', 'new_m_conv_ln_g', 'new_m_conv_ln_b', 'new_m_w_out', 'new_m_norm_ffn2', 'new_m_w_ffn2_in', 'new_m_w_ffn2_out', 'new_m_final_norm', 'new_v_norm_ffn1', 'new_v_w_ffn1_in', 'new_v_w_ffn1_out', 'new_v_norm_mix', 'new_v_w_in', 'new_v_sinks', 'new_v_w_dw', 'new_v_b_dw', 'new_v_conv_ln_g', 'new_v_conv_ln_b', 'new_v_w_out', 'new_v_norm_ffn2', 'new_v_w_ffn2_in', 'new_v_w_ffn2_out', 'new_v_final_norm']
TWIN_LEAF_KINDS = {'loss': 'loss', 'grad_x': 'grad_x', 'grad_norm_ffn1': 'grad_w', 'grad_w_ffn1_in': 'grad_w', 'grad_w_ffn1_out': 'grad_w', 'grad_norm_mix': 'grad_w', 'grad_w_in': 'grad_w', 'grad_sinks': 'grad_w', 'grad_w_dw': 'grad_w', 'grad_b_dw': 'grad_w', 'grad_conv_ln_g': 'grad_w', 'grad_conv_ln_b': 'grad_w', 'grad_w_out': 'grad_w', 'grad_norm_ffn2': 'grad_w', 'grad_w_ffn2_in': 'grad_w', 'grad_w_ffn2_out': 'grad_w', 'grad_final_norm': 'grad_w', 'delta_norm_ffn1': 'delta_w', 'delta_w_ffn1_in': 'delta_w', 'delta_w_ffn1_out': 'delta_w', 'delta_norm_mix': 'delta_w', 'delta_w_in': 'delta_w', 'delta_sinks': 'delta_w', 'delta_w_dw': 'delta_w', 'delta_b_dw': 'delta_w', 'delta_conv_ln_g': 'delta_w', 'delta_conv_ln_b': 'delta_w', 'delta_w_out': 'delta_w', 'delta_norm_ffn2': 'delta_w', 'delta_w_ffn2_in': 'delta_w', 'delta_w_ffn2_out': 'delta_w', 'delta_final_norm': 'delta_w', 'new_m_norm_ffn1': 'new_m', 'new_m_w_ffn1_in': 'new_m', 'new_m_w_ffn1_out': 'new_m', 'new_m_norm_mix': 'new_m', 'new_m_w_in': 'new_m', 'new_m_sinks': 'new_m', 'new_m_w_dw': 'new_m', 'new_m_b_dw': 'new_m', 'new_m_conv_ln_g': 'new_m', 'new_m_conv_ln_b': 'new_m', 'new_m_w_out': 'new_m', 'new_m_norm_ffn2': 'new_m', 'new_m_w_ffn2_in': 'new_m', 'new_m_w_ffn2_out': 'new_m', 'new_m_final_norm': 'new_m', 'new_v_norm_ffn1': 'new_v', 'new_v_w_ffn1_in': 'new_v', 'new_v_w_ffn1_out': 'new_v', 'new_v_norm_mix': 'new_v', 'new_v_w_in': 'new_v', 'new_v_sinks': 'new_v', 'new_v_w_dw': 'new_v', 'new_v_b_dw': 'new_v', 'new_v_conv_ln_g': 'new_v', 'new_v_conv_ln_b': 'new_v', 'new_v_w_out': 'new_v', 'new_v_norm_ffn2': 'new_v', 'new_v_w_ffn2_in': 'new_v', 'new_v_w_ffn2_out': 'new_v', 'new_v_final_norm': 'new_v'}


def _forward(args):
    return _fwd_reference(*[args[k] for k in FWD_PARAMS])


def _output_shape():
    out = _jax.eval_shape(lambda: _forward(_fwd_setup_inputs(0)))
    return out.shape, out.dtype

N_MICROBATCH = 1
ADAM_LR = 0.001
ADAM_B1 = 0.9
ADAM_B2 = 0.999
ADAM_EPS = 1e-08
ADAM_WD = 0.01
ADAM_STEP = 10
PER_EXAMPLE_BATCH_AXIS = {'x': 0, 'loss_target': 0}
SHARED_INPUTS = []
_WEIGHT_DTYPES = {'norm_ffn1': _jnp.float32, 'w_ffn1_in': _jnp.float32, 'w_ffn1_out': _jnp.float32, 'norm_mix': _jnp.float32, 'w_in': _jnp.float32, 'sinks': _jnp.float32, 'w_dw': _jnp.float32, 'b_dw': _jnp.float32, 'conv_ln_g': _jnp.float32, 'conv_ln_b': _jnp.float32, 'w_out': _jnp.float32, 'norm_ffn2': _jnp.float32, 'w_ffn2_in': _jnp.float32, 'w_ffn2_out': _jnp.float32, 'final_norm': _jnp.float32}
MOMENT_SCALE = {'norm_ffn1': 7.335649e-02, 'w_ffn1_in': 3.094111e-02, 'w_ffn1_out': 5.066606e-02, 'norm_mix': 8.784608e-02, 'w_in': 6.739898e-02, 'sinks': 1.753116e-01, 'w_dw': 9.713023e-02, 'b_dw': 3.077114e-01, 'conv_ln_g': 1.470596e-01, 'conv_ln_b': 1.881729e-01, 'w_out': 9.024283e-02, 'norm_ffn2': 6.504324e-02, 'w_ffn2_in': 2.680039e-02, 'w_ffn2_out': 4.390243e-02, 'final_norm': 3.210151e+01}


def _to_microbatches(a, axis):
    t = _jnp.moveaxis(a, axis, 0)
    t = t.reshape((N_MICROBATCH, t.shape[0] // N_MICROBATCH) + t.shape[1:])
    return _jnp.moveaxis(t, 1, axis + 1)


def setup_inputs(seed: int = 0) -> dict:
    inp = _fwd_setup_inputs(seed)
    key = _jax.random.fold_in(_jax.random.key(seed), 7919)
    shape, _ = _output_shape()
    out = dict(inp)
    out["loss_target"] = _jax.random.normal(_jax.random.fold_in(key, 0), shape, _jnp.float32)
    for i, name in enumerate(TWIN_WEIGHTS):
        w = inp[name].astype(_jnp.float32)
        if MOMENT_SCALE is None:
            s = _jnp.sqrt(_jnp.mean(_jnp.square(w)) + 1e-30)
        else:
            s = MOMENT_SCALE[name]
        km, kv = _jax.random.split(_jax.random.fold_in(key, i + 1))
        out[name] = w
        out["m_" + name] = s * _jax.random.normal(km, w.shape, _jnp.float32)
        out["v_" + name] = (s * s) * _jax.random.uniform(kv, w.shape, _jnp.float32, 0.5, 1.5)
    if N_MICROBATCH > 1:
        for name, axis in PER_EXAMPLE_BATCH_AXIS.items():
            out[name] = _to_microbatches(out[name], axis)
    return {'x': out['x'], 'norm_ffn1': out['norm_ffn1'], 'w_ffn1_in': out['w_ffn1_in'], 'w_ffn1_out': out['w_ffn1_out'], 'norm_mix': out['norm_mix'], 'w_in': out['w_in'], 'sinks': out['sinks'], 'w_dw': out['w_dw'], 'b_dw': out['b_dw'], 'conv_ln_g': out['conv_ln_g'], 'conv_ln_b': out['conv_ln_b'], 'w_out': out['w_out'], 'norm_ffn2': out['norm_ffn2'], 'w_ffn2_in': out['w_ffn2_in'], 'w_ffn2_out': out['w_ffn2_out'], 'final_norm': out['final_norm'], 'loss_target': out['loss_target'], 'm_norm_ffn1': out['m_norm_ffn1'], 'm_w_ffn1_in': out['m_w_ffn1_in'], 'm_w_ffn1_out': out['m_w_ffn1_out'], 'm_norm_mix': out['m_norm_mix'], 'm_w_in': out['m_w_in'], 'm_sinks': out['m_sinks'], 'm_w_dw': out['m_w_dw'], 'm_b_dw': out['m_b_dw'], 'm_conv_ln_g': out['m_conv_ln_g'], 'm_conv_ln_b': out['m_conv_ln_b'], 'm_w_out': out['m_w_out'], 'm_norm_ffn2': out['m_norm_ffn2'], 'm_w_ffn2_in': out['m_w_ffn2_in'], 'm_w_ffn2_out': out['m_w_ffn2_out'], 'm_final_norm': out['m_final_norm'], 'v_norm_ffn1': out['v_norm_ffn1'], 'v_w_ffn1_in': out['v_w_ffn1_in'], 'v_w_ffn1_out': out['v_w_ffn1_out'], 'v_norm_mix': out['v_norm_mix'], 'v_w_in': out['v_w_in'], 'v_sinks': out['v_sinks'], 'v_w_dw': out['v_w_dw'], 'v_b_dw': out['v_b_dw'], 'v_conv_ln_g': out['v_conv_ln_g'], 'v_conv_ln_b': out['v_conv_ln_b'], 'v_w_out': out['v_w_out'], 'v_norm_ffn2': out['v_norm_ffn2'], 'v_w_ffn2_in': out['v_w_ffn2_in'], 'v_w_ffn2_out': out['v_w_ffn2_out'], 'v_final_norm': out['v_final_norm']}


def _loss(weights, diff, rest, loss_target):
    with _jax.named_scope("forward"):
        args = {**rest, TWIN_DIFF_INPUT: diff, **{k: w.astype(_WEIGHT_DTYPES[k]) for k, w in weights.items()}}
        y = _forward(args)
    with _jax.named_scope("loss_head"):
        err = _jnp.square(y.astype(_jnp.float32) - loss_target)
        return 0.5 * _jnp.sum(_jnp.mean(err, axis=-1)) if err.ndim else 0.5 * err


def _adamw(w, g, m, v):
    m = ADAM_B1 * m + (1.0 - ADAM_B1) * g
    v = ADAM_B2 * v + (1.0 - ADAM_B2) * _jnp.square(g)
    m_hat = m / (1.0 - ADAM_B1 ** ADAM_STEP)
    v_hat = v / (1.0 - ADAM_B2 ** ADAM_STEP)
    delta = -ADAM_LR * (m_hat / (_jnp.sqrt(v_hat) + ADAM_EPS) + ADAM_WD * w)
    return delta, m, v


def reference(x, norm_ffn1, w_ffn1_in, w_ffn1_out, norm_mix, w_in, sinks, w_dw, b_dw, conv_ln_g, conv_ln_b, w_out, norm_ffn2, w_ffn2_in, w_ffn2_out, final_norm, loss_target, m_norm_ffn1, m_w_ffn1_in, m_w_ffn1_out, m_norm_mix, m_w_in, m_sinks, m_w_dw, m_b_dw, m_conv_ln_g, m_conv_ln_b, m_w_out, m_norm_ffn2, m_w_ffn2_in, m_w_ffn2_out, m_final_norm, v_norm_ffn1, v_w_ffn1_in, v_w_ffn1_out, v_norm_mix, v_w_in, v_sinks, v_w_dw, v_b_dw, v_conv_ln_g, v_conv_ln_b, v_w_out, v_norm_ffn2, v_w_ffn2_in, v_w_ffn2_out, v_final_norm):
    given = dict(x=x, norm_ffn1=norm_ffn1, w_ffn1_in=w_ffn1_in, w_ffn1_out=w_ffn1_out, norm_mix=norm_mix, w_in=w_in, sinks=sinks, w_dw=w_dw, b_dw=b_dw, conv_ln_g=conv_ln_g, conv_ln_b=conv_ln_b, w_out=w_out, norm_ffn2=norm_ffn2, w_ffn2_in=w_ffn2_in, w_ffn2_out=w_ffn2_out, final_norm=final_norm, loss_target=loss_target, m_norm_ffn1=m_norm_ffn1, m_w_ffn1_in=m_w_ffn1_in, m_w_ffn1_out=m_w_ffn1_out, m_norm_mix=m_norm_mix, m_w_in=m_w_in, m_sinks=m_sinks, m_w_dw=m_w_dw, m_b_dw=m_b_dw, m_conv_ln_g=m_conv_ln_g, m_conv_ln_b=m_conv_ln_b, m_w_out=m_w_out, m_norm_ffn2=m_norm_ffn2, m_w_ffn2_in=m_w_ffn2_in, m_w_ffn2_out=m_w_ffn2_out, m_final_norm=m_final_norm, v_norm_ffn1=v_norm_ffn1, v_w_ffn1_in=v_w_ffn1_in, v_w_ffn1_out=v_w_ffn1_out, v_norm_mix=v_norm_mix, v_w_in=v_w_in, v_sinks=v_sinks, v_w_dw=v_w_dw, v_b_dw=v_b_dw, v_conv_ln_g=v_conv_ln_g, v_conv_ln_b=v_conv_ln_b, v_w_out=v_w_out, v_norm_ffn2=v_norm_ffn2, v_w_ffn2_in=v_w_ffn2_in, v_w_ffn2_out=v_w_ffn2_out, v_final_norm=v_final_norm)
    weights = {n: given[n] for n in TWIN_WEIGHTS}
    shared = {n: given[n] for n in SHARED_INPUTS}
    per_example = {n: given[n] for n in ['x']}
    grad_fn = _jax.value_and_grad(_loss, argnums=(0, 1))

    def one_microbatch(ex, loss_target):
        ex = dict(ex)
        diff = ex.pop(TWIN_DIFF_INPUT)
        return grad_fn(weights, diff, {**shared, **ex}, loss_target)

    if N_MICROBATCH == 1:
        loss, (grad_w, grad_x) = one_microbatch(per_example, given["loss_target"])
    else:
        def body(carry, xs):
            loss_sum, grad_sum = carry
            l_k, (gw_k, gx_k) = one_microbatch(xs[0], xs[1])
            with _jax.named_scope("update"):
                return (loss_sum + l_k, _jax.tree.map(_jnp.add, grad_sum, gw_k)), gx_k

        init = (_jnp.zeros((), _jnp.float32), _jax.tree.map(_jnp.zeros_like, weights))
        (loss, grad_w), grad_x = _jax.lax.scan(body, init, (per_example, given["loss_target"]))
    with _jax.named_scope("update"):
        delta_w, new_m, new_v = {}, {}, {}
        for n in TWIN_WEIGHTS:
            delta_w[n], new_m[n], new_v[n] = _adamw(weights[n], grad_w[n], given["m_" + n], given["v_" + n])
    return (loss, grad_x, *[grad_w[n] for n in TWIN_WEIGHTS], *[delta_w[n] for n in TWIN_WEIGHTS],
            *[new_m[n] for n in TWIN_WEIGHTS], *[new_v[n] for n in TWIN_WEIGHTS])
```

```python
import functools

import jax
import jax.numpy as jnp
from jax import lax
from jax.experimental import pallas as pl
from jax.experimental.pallas import tpu as pltpu

F32, BF16 = jnp.float32, jnp.bfloat16
EPS = 1e-6
NEG_INF = -1e30
HEAD_DIM = 64
N_HEADS = 8
N_KV = 2
GROUP = N_HEADS // N_KV
WINDOW = 128
ATTN_W = N_HEADS * HEAD_DIM
KV_W = N_KV * HEAD_DIM
CONV_W = 31
HALO = 32
CONV_ROWS = 32
SCALE = 1.0 / 8.0
ADAM_LR, ADAM_B1, ADAM_B2, ADAM_EPS, ADAM_WD, ADAM_STEP = 0.001, 0.9, 0.999, 1e-08, 0.01, 10
TM = 512
LANES = 128
VMEM_LIMIT = 52 * 1024 * 1024
MESH = pl.DeviceIdType.MESH
ANY = pl.BlockSpec(memory_space=pl.ANY)


def _cp(n):
    return pltpu.CompilerParams(dimension_semantics=("arbitrary",) * n, vmem_limit_bytes=VMEM_LIMIT)


def _dot(a, b):
    return jnp.dot(a, b, preferred_element_type=F32)


def _dot_nt(a, b):
    return lax.dot_general(a, b, (((1,), (1,)), ((), ())), preferred_element_type=F32)


def _dot_tn(a, b):
    return lax.dot_general(a, b, (((0,), (0,)), ((), ())), preferred_element_type=F32)


def _sigmoid(v):
    return 1.0 / (1.0 + jnp.exp(-v))


def _place():
    x, y, c = lax.axis_index("x"), lax.axis_index("y"), lax.axis_index("c")
    chips = [(1 - x, y), (x, 1 - y), (1 - x, 1 - y)]
    return x, y, c, chips


def _rcopy(src, dst, send_sems, recv_sems, k, dev):
    return pltpu.make_async_remote_copy(src_ref=src, dst_ref=dst, send_sem=send_sems.at[k],
                                        recv_sem=recv_sems.at[k], device_id=dev, device_id_type=MESH)


def _gather_layer(ws, wdw):
    nt = len(ws)

    def body(*refs):
        w_in, dw_in = refs[:nt], refs[nt]
        w_out, dw_out = refs[nt + 1:2 * nt + 1], refs[2 * nt + 1]
        send_sems, recv_sems, local_sems = refs[2 * nt + 2:]
        x, y, c, chips = _place()
        b = 2 * x + y
        sib = (x, y, 1 - c)

        def half(t, which):
            h = ws[t].shape[0] // 2
            return pl.ds(which * h, h)

        rc = functools.partial(_rcopy, send_sems=send_sems, recv_sems=recv_sems)
        local = []
        for t in range(nt + 1):
            src = w_in[t] if t < nt else dw_in
            dst = (w_out[t] if t < nt else dw_out).at[b]
            cp = pltpu.make_async_copy(src, dst, local_sems.at[t])
            cp.start()
            local.append(cp)
        sends = []
        for t in range(nt):
            for j, (px, py) in enumerate(chips):
                cp = rc(w_in[t].at[half(t, c)], w_out[t].at[b, half(t, c)], k=6 * t + j, dev=(px, py, c))
                cp.start()
                sends.append(cp)
        for j, (px, py) in enumerate(chips):
            cp = rc(dw_in, dw_out.at[b], k=6 * nt + j, dev=(px, py, c))
            cp.start()
            sends.append(cp)
        for t in range(nt):
            for j, (px, py) in enumerate(chips):
                blk = w_out[t].at[2 * px + py, half(t, c)]
                rc(blk, blk, k=6 * t + j, dev=sib).wait_recv()
                cp = rc(blk, blk, k=6 * t + 3 + j, dev=sib)
                cp.start()
                sends.append(cp)
        for j, (px, py) in enumerate(chips):
            blk = dw_out.at[2 * px + py]
            rc(blk, blk, k=6 * nt + j, dev=sib).wait_recv()
        for t in range(nt):
            for j, (px, py) in enumerate(chips):
                blk = w_out[t].at[2 * px + py, half(t, 1 - c)]
                rc(blk, blk, k=6 * t + 3 + j, dev=sib).wait_recv()
        for cp in sends:
            cp.wait_send()
        for cp in local:
            cp.wait()

    nsem = 6 * nt + 3
    out_shape = [jax.ShapeDtypeStruct((4,) + w.shape, w.dtype) for w in ws]
    out_shape.append(jax.ShapeDtypeStruct((4,) + wdw.shape, wdw.dtype))
    return pl.pallas_call(
        body, name="gather_layer", out_shape=out_shape,
        in_specs=[ANY] * (nt + 1), out_specs=[ANY] * (nt + 1),
        scratch_shapes=[pltpu.SemaphoreType.DMA((nsem,)), pltpu.SemaphoreType.DMA((nsem,)),
                        pltpu.SemaphoreType.DMA((nt + 1,))],
    )(*ws, wdw)


def _rms(xf, g):
    r = lax.rsqrt(jnp.mean(xf * xf, axis=-1, keepdims=True) + EPS)
    return xf * r, r


def _ffn_fwd(x, g, win, wout):
    T, D = x.shape
    FB = win.shape[2]
    tm = min(TM, T)

    def body(x_ref, g_ref, wg_ref, wu_ref, wo_ref, xo_ref, gu_ref, h_ref, acc_ref):
        j = pl.program_id(1)

        @pl.when(j == 0)
        def _():
            xh, _ = _rms(x_ref[...], None)
            h_ref[...] = (xh * g_ref[...]).astype(BF16)
            acc_ref[...] = jnp.zeros_like(acc_ref)

        h = h_ref[...]
        gate = _dot(h, wg_ref[...])
        up = _dot(h, wu_ref[...])
        gu_ref[0] = gate.astype(BF16)
        gu_ref[1] = up.astype(BF16)
        a = (gate * _sigmoid(gate) * up).astype(BF16)
        acc_ref[...] += _dot(a, wo_ref[...])

        @pl.when(j == 1)
        def _():
            xo_ref[...] = x_ref[...] + 0.5 * acc_ref[...]

    return pl.pallas_call(
        body, name="ffn_fwd", grid=(T // tm, 2),
        in_specs=[pl.BlockSpec((tm, D), lambda i, j: (i, 0)),
                  pl.BlockSpec((1, D), lambda i, j: (0, 0)),
                  pl.BlockSpec((None, D, FB), lambda i, j: (j, 0, 0)),
                  pl.BlockSpec((None, D, FB), lambda i, j: (j + 2, 0, 0)),
                  pl.BlockSpec((FB, D), lambda i, j: (j, 0))],
        out_specs=[pl.BlockSpec((tm, D), lambda i, j: (i, 0)),
                   pl.BlockSpec((2, tm, FB), lambda i, j: (0, i, j))],
        out_shape=[jax.ShapeDtypeStruct((T, D), F32), jax.ShapeDtypeStruct((2, T, 2 * FB), BF16)],
        scratch_shapes=[pltpu.VMEM((tm, D), BF16), pltpu.VMEM((tm, D), F32)],
        compiler_params=_cp(2),
    )(x, g, win, win, wout)


def _mixproj_fwd(x, g, w):
    T, D = x.shape
    W = w.shape[1]
    QKV = ATTN_W + 2 * KV_W
    tm = min(TM, T)

    def body(x_ref, g_ref, w_ref, qkv_ref, u_ref):
        xh, _ = _rms(x_ref[...], None)
        h = (xh * g_ref[...]).astype(BF16)
        qkv_ref[...] = _dot(h, w_ref[:, :QKV]).astype(BF16)
        u_ref[...] = _dot(h, w_ref[:, QKV:])

    return pl.pallas_call(
        body, name="mixproj_fwd", grid=(T // tm,),
        in_specs=[pl.BlockSpec((tm, D), lambda i: (i, 0)), pl.BlockSpec((1, D), lambda i: (0, 0)),
                  pl.BlockSpec((D, W), lambda i: (0, 0))],
        out_specs=[pl.BlockSpec((tm, QKV), lambda i: (i, 0)), pl.BlockSpec((tm, W - QKV), lambda i: (i, 0))],
        out_shape=[jax.ShapeDtypeStruct((T, QKV), BF16), jax.ShapeDtypeStruct((T, W - QKV), F32)],
        compiler_params=_cp(1),
    )(x, g, w)


def _attn_tables(n, g):
    rows, cols = GROUP * WINDOW, 2 * WINDOW
    row = lax.broadcasted_iota(jnp.int32, (rows, cols), 0)
    col = lax.broadcasted_iota(jnp.int32, (rows, cols), 1)
    dist = (row & (WINDOW - 1)) + WINDOW - col
    valid = (dist >= 0) & (dist < WINDOW) & ((n > 0) | (col >= WINDOW))
    hi = row >> 7
    slope = jnp.zeros((rows, cols), F32)
    for i in range(GROUP):
        slope = jnp.where(hi == i, 2.0 ** -(GROUP * g + i + 1), slope)
    bias = -slope * dist.astype(F32)
    return valid, bias


def _sink_col(sink_ref, g):
    hi = lax.broadcasted_iota(jnp.int32, (GROUP * WINDOW, 1), 0) >> 7
    col = jnp.zeros((GROUP * WINDOW, 1), F32)
    for i in range(GROUP):
        col = jnp.where(hi == i, sink_ref[0, GROUP * g + i], col)
    return col


def _stack_heads(ref, g):
    return jnp.concatenate([ref[:, (GROUP * g + i) * HEAD_DIM:(GROUP * g + i + 1) * HEAD_DIM]
                            for i in range(GROUP)], axis=0)


def _band(kvp_ref, kvc_ref, off):
    return jnp.concatenate([kvp_ref[:, off:off + HEAD_DIM], kvc_ref[:, off:off + HEAD_DIM]], axis=0)


def _attn_probs(qs, k, valid, bias, sink):
    s = _dot_nt(qs, k) * SCALE
    s = jnp.where(valid, s + bias, NEG_INF)
    m = jnp.maximum(jnp.max(s, axis=-1, keepdims=True), sink)
    p = jnp.exp(s - m)
    es = jnp.exp(sink - m)
    den = jnp.sum(p, axis=-1, keepdims=True) + es
    return p / den, es / den


def _attn_fwd(sinks, qkv):
    T = qkv.shape[0]
    nb = T // WINDOW

    def body(sink_ref, q_ref, kvp_ref, kvc_ref, o_ref):
        n = pl.program_id(0)
        for g in range(N_KV):
            valid, bias = _attn_tables(n, g)
            qs = _stack_heads(q_ref, g)
            k = _band(kvp_ref, kvc_ref, g * HEAD_DIM)
            v = _band(kvp_ref, kvc_ref, KV_W + g * HEAD_DIM)
            p, _ = _attn_probs(qs, k, valid, bias, _sink_col(sink_ref, g))
            o = _dot(p.astype(BF16), v)
            for i in range(GROUP):
                h = GROUP * g + i
                o_ref[:, h * HEAD_DIM:(h + 1) * HEAD_DIM] = o[i * WINDOW:(i + 1) * WINDOW].astype(BF16)

    return pl.pallas_call(
        body, name="attn_fwd", grid=(nb,),
        in_specs=[pl.BlockSpec(memory_space=pltpu.SMEM),
                  pl.BlockSpec((WINDOW, ATTN_W), lambda n: (n, 0)),
                  pl.BlockSpec((WINDOW, 2 * KV_W), lambda n: (jnp.maximum(n - 1, 0), 2)),
                  pl.BlockSpec((WINDOW, 2 * KV_W), lambda n: (n, 2))],
        out_specs=pl.BlockSpec((WINDOW, ATTN_W), lambda n: (n, 0)),
        out_shape=jax.ShapeDtypeStruct((T, ATTN_W), BF16),
        compiler_params=_cp(1),
    )(sinks, qkv, qkv, qkv)


def _shift_copies(src_ref, dst_ref, n):
    for b in range(1, 8):
        dst_ref[b - 1] = src_ref[b:b + n, :]


def _tap(src_ref, sh_ref, s, c0):
    a, b = divmod(s, 8)
    start = pl.multiple_of(c0 + 8 * a, 8)
    if b == 0:
        return src_ref[pl.ds(start, CONV_ROWS), :]
    return sh_ref[b - 1, pl.ds(start, CONV_ROWS), :]


def _glu_rows(u, ch):
    return u[:, :ch] * _sigmoid(u[:, ch:])


def _fill_z(zs_ref, zsh_ref, uc_ref, up_ref, i, ch, n):
    zs_ref[0:HALO] = jnp.where(i > 0, _glu_rows(up_ref[...], ch), 0.0)
    zs_ref[HALO:] = _glu_rows(uc_ref[...], ch)
    _shift_copies(zs_ref, zsh_ref, n - 8)


def _conv_fwd(u, w, b, lg, lb):
    T = u.shape[0]
    CH = u.shape[1] // 2
    tm = min(TM, T)
    n = tm + HALO
    hb = tm // HALO

    def body(uc_ref, up_ref, w_ref, b_ref, lg_ref, lb_ref, conv_ref, ypre_ref, zs_ref, zsh_ref):
        i = pl.program_id(0)
        _fill_z(zs_ref, zsh_ref, uc_ref, up_ref, i, CH, n)
        bias = b_ref[...]

        def chunk(ci, carry):
            c0 = pl.multiple_of(ci * CONV_ROWS, CONV_ROWS)
            acc = jnp.broadcast_to(bias, (CONV_ROWS, CH))
            for k in range(CONV_W):
                acc = acc + w_ref[k:k + 1, :] * _tap(zs_ref, zsh_ref, HALO - (CONV_W - 1) + k, c0)
            ypre_ref[pl.ds(c0, CONV_ROWS), :] = acc
            return carry

        lax.fori_loop(0, tm // CONV_ROWS, chunk, 0)
        y = ypre_ref[...]
        mu = jnp.mean(y, axis=-1, keepdims=True)
        d = y - mu
        var = jnp.mean(d * d, axis=-1, keepdims=True)
        o = d * lax.rsqrt(var + EPS) * lg_ref[...] + lb_ref[...]
        conv_ref[...] = (o * _sigmoid(o)).astype(BF16)

    vec = pl.BlockSpec((1, CH), lambda i: (0, 0))
    return pl.pallas_call(
        body, name="conv_fwd", grid=(T // tm,),
        in_specs=[pl.BlockSpec((tm, 2 * CH), lambda i: (i, 0)),
                  pl.BlockSpec((HALO, 2 * CH), lambda i: (jnp.maximum(i * hb - 1, 0), 0)),
                  pl.BlockSpec((CONV_W, CH), lambda i: (0, 0)), vec, vec, vec],
        out_specs=[pl.BlockSpec((tm, CH), lambda i: (i, 0)), pl.BlockSpec((tm, CH), lambda i: (i, 0))],
        out_shape=[jax.ShapeDtypeStruct((T, CH), BF16), jax.ShapeDtypeStruct((T, CH), F32)],
        scratch_shapes=[pltpu.VMEM((n, CH), F32), pltpu.VMEM((7, n - 8, CH), F32)],
        compiler_params=_cp(1),
    )(u, u, w, b, lg, lb)


def _mixout_fwd(x, attn, conv, wo):
    T, D = x.shape
    tm = min(TM, T)
    A = attn.shape[1]

    def body(x_ref, a_ref, c_ref, w_ref, xo_ref):
        xo_ref[...] = x_ref[...] + _dot(a_ref[...], w_ref[:A, :]) + _dot(c_ref[...], w_ref[A:, :])

    return pl.pallas_call(
        body, name="mixout_fwd", grid=(T // tm,),
        in_specs=[pl.BlockSpec((tm, D), lambda i: (i, 0)), pl.BlockSpec((tm, A), lambda i: (i, 0)),
                  pl.BlockSpec((tm, conv.shape[1]), lambda i: (i, 0)), pl.BlockSpec(wo.shape, lambda i: (0, 0))],
        out_specs=pl.BlockSpec((tm, D), lambda i: (i, 0)),
        out_shape=jax.ShapeDtypeStruct((T, D), F32),
        compiler_params=_cp(1),
    )(x, attn, conv, wo)


def _rms_bwd_rows(dh, xf, g):
    xh, r = _rms(xf, None)
    dxn = dh * g
    dx = r * (dxn - xh * jnp.mean(dxn * xh, axis=-1, keepdims=True))
    return dx, jnp.sum(dh * xh, axis=0, keepdims=True), xh * g


def _loss_head(x, g, tgt):
    T, D = x.shape
    tm = min(TM, T)

    def body(x_ref, g_ref, t_ref, loss_ref, dx_ref, dg_ref):
        @pl.when(pl.program_id(0) == 0)
        def _():
            loss_ref[...] = jnp.zeros_like(loss_ref)
            dg_ref[...] = jnp.zeros_like(dg_ref)

        xf = x_ref[...]
        g = g_ref[...]
        xh, _ = _rms(xf, None)
        e = xh * g - t_ref[...]
        loss_ref[...] += 0.5 * jnp.sum(jnp.mean(e * e, axis=-1, keepdims=True), axis=0, keepdims=True)
        dx, dg, _ = _rms_bwd_rows(e * (1.0 / D), xf, g)
        dx_ref[...] = dx
        dg_ref[...] += dg

    return pl.pallas_call(
        body, name="loss_head", grid=(T // tm,),
        in_specs=[pl.BlockSpec((tm, D), lambda i: (i, 0)), pl.BlockSpec((1, D), lambda i: (0, 0)),
                  pl.BlockSpec((tm, D), lambda i: (i, 0))],
        out_specs=[pl.BlockSpec((1, 1), lambda i: (0, 0)), pl.BlockSpec((tm, D), lambda i: (i, 0)),
                   pl.BlockSpec((1, D), lambda i: (0, 0))],
        out_shape=[jax.ShapeDtypeStruct((1, 1), F32), jax.ShapeDtypeStruct((T, D), F32),
                   jax.ShapeDtypeStruct((1, D), F32)],
        compiler_params=_cp(1),
    )(x, g, tgt)


def _ffn_bwd_act(dxo, gu, wout):
    T, D = dxo.shape
    FB = gu.shape[2] // 2
    tm = min(TM, T)

    def body(dxo_ref, gu_ref, wo_ref, dgu_ref, a_ref, dyb_ref):
        @pl.when(pl.program_id(1) == 0)
        def _():
            dyb_ref[...] = (0.5 * dxo_ref[...]).astype(BF16)

        da = _dot_nt(dyb_ref[...], wo_ref[...])
        gate = gu_ref[0].astype(F32)
        up = gu_ref[1].astype(F32)
        sg = _sigmoid(gate)
        s = gate * sg
        a_ref[...] = (s * up).astype(BF16)
        dgu_ref[0] = (da * up * (sg * (1.0 + gate * (1.0 - sg)))).astype(BF16)
        dgu_ref[1] = (da * s).astype(BF16)

    return pl.pallas_call(
        body, name="ffn_bwd_act", grid=(T // tm, 2),
        in_specs=[pl.BlockSpec((tm, D), lambda i, j: (i, 0)),
                  pl.BlockSpec((2, tm, FB), lambda i, j: (0, i, j)),
                  pl.BlockSpec((FB, D), lambda i, j: (j, 0))],
        out_specs=[pl.BlockSpec((2, tm, FB), lambda i, j: (0, i, j)),
                   pl.BlockSpec((tm, FB), lambda i, j: (i, j)),
                   pl.BlockSpec((tm, D), lambda i, j: (i, 0))],
        out_shape=[jax.ShapeDtypeStruct((2, T, 2 * FB), BF16), jax.ShapeDtypeStruct((T, 2 * FB), BF16),
                   jax.ShapeDtypeStruct((T, D), BF16)],
        compiler_params=_cp(2),
    )(dxo, gu, wout)


def _rms_matmul_bwd(name, dxo, x, g, dzs, ws, dz_specs, w_specs, nk):
    T, D = x.shape
    tm = min(TM, T)
    npair = len(dzs)

    def body(*refs):
        dxo_ref, x_ref, g_ref = refs[:3]
        dz_refs, w_refs = refs[3:3 + npair], refs[3 + npair:3 + 2 * npair]
        dxi_ref, dg_ref, hb_ref, acc_ref = refs[3 + 2 * npair:]
        i, k = pl.program_id(0), pl.program_id(1)

        @pl.when(k == 0)
        def _():
            acc_ref[...] = jnp.zeros_like(acc_ref)

        @pl.when((i == 0) & (k == 0))
        def _():
            dg_ref[...] = jnp.zeros_like(dg_ref)

        for p in range(npair):
            acc_ref[...] += _dot_nt(dz_refs[p][...], w_refs[p][...])

        @pl.when(k == nk - 1)
        def _():
            dx, dg, h = _rms_bwd_rows(acc_ref[...], x_ref[...], g_ref[...])
            dxi_ref[...] = dxo_ref[...] + dx
            dg_ref[...] += dg
            hb_ref[...] = h.astype(BF16)

    row = pl.BlockSpec((tm, D), lambda i, k: (i, 0))
    return pl.pallas_call(
        body, name=name, grid=(T // tm, nk),
        in_specs=[row, row, pl.BlockSpec((1, D), lambda i, k: (0, 0))] + list(dz_specs) + list(w_specs),
        out_specs=[row, pl.BlockSpec((1, D), lambda i, k: (0, 0)), row],
        out_shape=[jax.ShapeDtypeStruct((T, D), F32), jax.ShapeDtypeStruct((1, D), F32),
                   jax.ShapeDtypeStruct((T, D), BF16)],
        scratch_shapes=[pltpu.VMEM((tm, D), F32)],
        compiler_params=_cp(2),
    )(dxo, x, g, *dzs, *ws)


def _ffn_rms_bwd(dxo, x, g, dgu, win):
    tm = min(TM, x.shape[0])
    D, FB = win.shape[1], win.shape[2]
    return _rms_matmul_bwd(
        "ffn_rms_bwd", dxo, x, g, [dgu], [win],
        [pl.BlockSpec((None, tm, FB), lambda i, k: (k // 2, i, k % 2))],
        [pl.BlockSpec((None, D, FB), lambda i, k: (k, 0, 0))], 4)


def _mix_rms_bwd(dxo, x, g, dzs, ws):
    tm = min(TM, x.shape[0])
    return _rms_matmul_bwd(
        "mix_rms_bwd", dxo, x, g, dzs, ws,
        [pl.BlockSpec((tm, dz.shape[1]), lambda i, k: (i, 0)) for dz in dzs],
        [pl.BlockSpec(w.shape, lambda i, k: (0, 0)) for w in ws], 1)


def _wgrad(name, a, b, a_spec, b_spec, out_shape, out_spec, nblk, acc=None):
    T = a.shape[0]
    tk = min(TM, T)

    def body(*refs):
        a_ref, b_ref, o_ref = refs[0], refs[1], refs[-1]

        @pl.when(pl.program_id(1) == 0)
        def _():
            o_ref[...] = jnp.zeros_like(o_ref)

        o_ref[...] += _dot_tn(a_ref[...], b_ref[...]).reshape(o_ref.shape)

    ins, specs, alias = [a, b], [a_spec, b_spec], {}
    if acc is not None:
        ins.append(acc)
        specs.append(ANY)
        alias = {2: 0}
    return pl.pallas_call(
        body, name=name, grid=(nblk, T // tk), in_specs=specs, out_specs=out_spec,
        out_shape=jax.ShapeDtypeStruct(out_shape, F32), input_output_aliases=alias,
        compiler_params=_cp(2),
    )(*ins)


def _wgrad_ffn_in(hb, dgu, acc, l):
    T, D = hb.shape
    FB = dgu.shape[2] // 2
    tk = min(TM, T)
    return _wgrad("wgrad_ffn_in", hb, dgu,
                  pl.BlockSpec((tk, D), lambda b, k: (k, 0)),
                  pl.BlockSpec((None, tk, FB), lambda b, k: (b // 2, k, b % 2)),
                  acc.shape, pl.BlockSpec((None, None, D, FB), lambda b, k: (b, l, 0, 0)), 4, acc)


def _wgrad_ffn_out(a, dyb, acc, l):
    T, D = dyb.shape
    FB = a.shape[1] // 2
    tk = min(TM, T)
    return _wgrad("wgrad_ffn_out", a, dyb,
                  pl.BlockSpec((tk, FB), lambda b, k: (k, b)),
                  pl.BlockSpec((tk, D), lambda b, k: (k, 0)),
                  acc.shape, pl.BlockSpec((2, None, FB // 2, D), lambda b, k: (b, l, 0, 0)), 2, acc)


def _wgrad_plain(a, b):
    T, M = a.shape
    N = b.shape[1]
    tk = min(TM, T)
    return _wgrad("wgrad_plain", a, b, pl.BlockSpec((tk, M), lambda i, k: (k, 0)),
                  pl.BlockSpec((tk, N), lambda i, k: (k, 0)), (M, N),
                  pl.BlockSpec((M, N), lambda i, k: (0, 0)), 1)


def _mixout_bwd(dxo, wo):
    T, D = dxo.shape
    tm = min(TM, T)
    A = ATTN_W
    C = wo.shape[0] - A

    def body(dxo_ref, w_ref, dyb_ref, da_ref, dc_ref):
        dyb = dxo_ref[...].astype(BF16)
        dyb_ref[...] = dyb
        da_ref[...] = _dot_nt(dyb, w_ref[:A, :]).astype(BF16)
        dc_ref[...] = _dot_nt(dyb, w_ref[A:, :])

    return pl.pallas_call(
        body, name="mixout_bwd", grid=(T // tm,),
        in_specs=[pl.BlockSpec((tm, D), lambda i: (i, 0)), pl.BlockSpec(wo.shape, lambda i: (0, 0))],
        out_specs=[pl.BlockSpec((tm, D), lambda i: (i, 0)), pl.BlockSpec((tm, A), lambda i: (i, 0)),
                   pl.BlockSpec((tm, C), lambda i: (i, 0))],
        out_shape=[jax.ShapeDtypeStruct((T, D), BF16), jax.ShapeDtypeStruct((T, A), BF16),
                   jax.ShapeDtypeStruct((T, C), F32)],
        compiler_params=_cp(1),
    )(dxo, wo)


def _conv_bwd(dconv, ypre, u, w, lg, lb):
    T, CH = dconv.shape
    tm = min(TM, T)
    n = tm + HALO
    hb = tm // HALO
    nt = T // tm
    nchunk = tm // CONV_ROWS

    def body(dc_ref, dcn_ref, yp_ref, ypn_ref, uc_ref, up_ref, w_ref, lg_ref, lb_ref,
             du_ref, dw_ref, dvec_ref, zs_ref, zsh_ref, dy_ref, dysh_ref, dz_ref):
        i = pl.program_id(0)

        @pl.when(i == 0)
        def _():
            dw_ref[...] = jnp.zeros_like(dw_ref)
            dvec_ref[...] = jnp.zeros_like(dvec_ref)

        g, bb = lg_ref[...], lb_ref[...]

        def ln_bwd(dc, yp):
            mu = jnp.mean(yp, axis=-1, keepdims=True)
            d = yp - mu
            rs = lax.rsqrt(jnp.mean(d * d, axis=-1, keepdims=True) + EPS)
            yn = d * rs
            o = yn * g + bb
            sg = _sigmoid(o)
            do = dc * (sg * (1.0 + o * (1.0 - sg)))
            dyn = do * g
            dyp = rs * (dyn - jnp.mean(dyn, axis=-1, keepdims=True)
                        - yn * jnp.mean(dyn * yn, axis=-1, keepdims=True))
            return dyp, do, yn

        dyp, do, yn = ln_bwd(dc_ref[...], yp_ref[...])
        dvec_ref[0:1, :] += jnp.sum(dyp, axis=0, keepdims=True)
        dvec_ref[1:2, :] += jnp.sum(do * yn, axis=0, keepdims=True)
        dvec_ref[2:3, :] += jnp.sum(do, axis=0, keepdims=True)
        dy_ref[0:tm] = dyp
        dyh, _, _ = ln_bwd(dcn_ref[...], ypn_ref[...])
        dy_ref[tm:] = jnp.where(i < nt - 1, dyh, 0.0)
        _shift_copies(dy_ref, dysh_ref, n - 8)
        _fill_z(zs_ref, zsh_ref, uc_ref, up_ref, i, CH, n)

        def chunk(ci, carry):
            c0 = pl.multiple_of(ci * CONV_ROWS, CONV_ROWS)
            acc = jnp.zeros((CONV_ROWS, CH), F32)
            for k in range(CONV_W):
                acc = acc + w_ref[k:k + 1, :] * _tap(dy_ref, dysh_ref, CONV_W - 1 - k, c0)
            dz_ref[pl.ds(c0, CONV_ROWS), :] = acc
            return carry

        lax.fori_loop(0, nchunk, chunk, 0)

        for k in range(CONV_W):
            def red(ci, acc, k=k):
                c0 = pl.multiple_of(ci * CONV_ROWS, CONV_ROWS)
                prod = dy_ref[pl.ds(c0, CONV_ROWS), :] * _tap(zs_ref, zsh_ref, HALO - (CONV_W - 1) + k, c0)
                return acc + jnp.sum(prod.reshape(CONV_ROWS // 8, 8, CH), axis=0)

            acc = lax.fori_loop(0, nchunk, red, jnp.zeros((8, CH), F32))
            dw_ref[k:k + 1, :] += jnp.sum(acc, axis=0, keepdims=True)

        uc = uc_ref[...]
        a = uc[:, :CH]
        sg = _sigmoid(uc[:, CH:])
        dz = dz_ref[...]
        du_ref[:, :CH] = (dz * sg).astype(BF16)
        du_ref[:, CH:] = (dz * a * sg * (1.0 - sg)).astype(BF16)

    cur = lambda c: pl.BlockSpec((tm, c), lambda i: (i, 0))
    nxt = lambda c: pl.BlockSpec((HALO, c), lambda i: (jnp.minimum((i + 1) * hb, T // HALO - 1), 0))
    vec = pl.BlockSpec((1, CH), lambda i: (0, 0))
    return pl.pallas_call(
        body, name="conv_bwd", grid=(nt,),
        in_specs=[cur(CH), nxt(CH), cur(CH), nxt(CH), cur(2 * CH),
                  pl.BlockSpec((HALO, 2 * CH), lambda i: (jnp.maximum(i * hb - 1, 0), 0)),
                  pl.BlockSpec((CONV_W, CH), lambda i: (0, 0)), vec, vec],
        out_specs=[pl.BlockSpec((tm, 2 * CH), lambda i: (i, 0)), pl.BlockSpec((32, CH), lambda i: (0, 0)),
                   pl.BlockSpec((8, CH), lambda i: (0, 0))],
        out_shape=[jax.ShapeDtypeStruct((T, 2 * CH), BF16), jax.ShapeDtypeStruct((32, CH), F32),
                   jax.ShapeDtypeStruct((8, CH), F32)],
        scratch_shapes=[pltpu.VMEM((n, CH), F32), pltpu.VMEM((7, n - 8, CH), F32),
                        pltpu.VMEM((n, CH), F32), pltpu.VMEM((7, n - 8, CH), F32), pltpu.VMEM((tm, CH), F32)],
        compiler_params=_cp(1),
    )(dconv, dconv, ypre, ypre, u, u, w, lg, lb)


def _attn_bwd(sinks, qkv, dattn):
    T = qkv.shape[0]
    nb = T // WINDOW

    def body(sink_ref, q_ref, kvp_ref, kvc_ref, do_ref, dq_ref, dkv_ref, dsk_ref, carry_ref):
        n = pl.program_id(0)

        @pl.when(n == 0)
        def _():
            dsk_ref[...] = jnp.zeros_like(dsk_ref)
            carry_ref[...] = jnp.zeros_like(carry_ref)

        @pl.when(n < nb)
        def _():
            for g in range(N_KV):
                valid, bias = _attn_tables(n, g)
                qs = _stack_heads(q_ref, g)
                dos = _stack_heads(do_ref, g)
                k = _band(kvp_ref, kvc_ref, g * HEAD_DIM)
                v = _band(kvp_ref, kvc_ref, KV_W + g * HEAD_DIM)
                p, ps = _attn_probs(qs, k, valid, bias, _sink_col(sink_ref, g))
                dp = _dot_nt(dos, v)
                delta = jnp.sum(p * dp, axis=-1, keepdims=True)
                dsb = (p * (dp - delta)).astype(BF16)
                dsink = -ps * delta
                dqs = _dot(dsb, k) * SCALE
                dk = _dot_tn(dsb, qs) * SCALE
                dv = _dot_tn(p.astype(BF16), dos)
                for i in range(GROUP):
                    h = GROUP * g + i
                    dq_ref[:, h * HEAD_DIM:(h + 1) * HEAD_DIM] = dqs[i * WINDOW:(i + 1) * WINDOW].astype(BF16)
                    dsk_ref[h:h + 1, :] += jnp.sum(dsink[i * WINDOW:(i + 1) * WINDOW], axis=0, keepdims=True)
                for off, d in ((g * HEAD_DIM, dk), (KV_W + g * HEAD_DIM, dv)):
                    dkv_ref[:, off:off + HEAD_DIM] = (carry_ref[:, off:off + HEAD_DIM] + d[:WINDOW]).astype(BF16)
                    carry_ref[:, off:off + HEAD_DIM] = d[WINDOW:]

        @pl.when(n == nb)
        def _():
            dkv_ref[...] = carry_ref[...].astype(BF16)

    last = nb - 1
    return pl.pallas_call(
        body, name="attn_bwd", grid=(nb + 1,),
        in_specs=[pl.BlockSpec(memory_space=pltpu.SMEM),
                  pl.BlockSpec((WINDOW, ATTN_W), lambda n: (jnp.minimum(n, last), 0)),
                  pl.BlockSpec((WINDOW, 2 * KV_W), lambda n: (jnp.clip(n - 1, 0, last), 2)),
                  pl.BlockSpec((WINDOW, 2 * KV_W), lambda n: (jnp.minimum(n, last), 2)),
                  pl.BlockSpec((WINDOW, ATTN_W), lambda n: (jnp.minimum(n, last), 0))],
        out_specs=[pl.BlockSpec((WINDOW, ATTN_W), lambda n: (jnp.minimum(n, last), 0)),
                   pl.BlockSpec((WINDOW, 2 * KV_W), lambda n: (jnp.maximum(n - 1, 0), 0)),
                   pl.BlockSpec((8, LANES), lambda n: (0, 0))],
        out_shape=[jax.ShapeDtypeStruct((T, ATTN_W), BF16), jax.ShapeDtypeStruct((T, 2 * KV_W), BF16),
                   jax.ShapeDtypeStruct((8, LANES), F32)],
        scratch_shapes=[pltpu.VMEM((WINDOW, 2 * KV_W), F32)],
        compiler_params=_cp(1),
    )(sinks, qkv, qkv, qkv, dattn)


def _rows_block(h):
    for rb in (128, 176, 64, 32, 16):
        if h % rb == 0:
            return rb
    return h


def _rs_sibling(gs):
    nt = len(gs)

    def body(*refs):
        g_refs, s_refs = refs[:nt], refs[nt:2 * nt]
        send_sems, recv_sems = refs[2 * nt:]
        x, y, c, _ = _place()
        cps = []
        for t in range(nt):
            h = gs[t].shape[2] // 2
            cp = _rcopy(g_refs[t].at[:, :, pl.ds((1 - c) * h, h), :], s_refs[t], send_sems, recv_sems, t, (x, y, 1 - c))
            cp.start()
            cps.append(cp)
        for cp in cps:
            cp.wait()

    return pl.pallas_call(
        body, name="rs_sibling", in_specs=[ANY] * nt, out_specs=[ANY] * nt,
        out_shape=[jax.ShapeDtypeStruct(g.shape[:2] + (g.shape[2] // 2, g.shape[3]), F32) for g in gs],
        scratch_shapes=[pltpu.SemaphoreType.DMA((nt,)), pltpu.SemaphoreType.DMA((nt,))],
    )(*gs)


def _sum_halves(cidx, g, s):
    _, L, R, C = g.shape
    h = R // 2
    rb = _rows_block(h)
    nr = h // rb

    def body(c_ref, g_ref, s_ref, o_ref):
        o_ref[...] = (g_ref[...] + s_ref[...]).astype(BF16)

    blk = (None, None, rb, C)
    return pl.pallas_call(
        body, name="sum_halves", out_shape=jax.ShapeDtypeStruct(s.shape, BF16),
        grid_spec=pltpu.PrefetchScalarGridSpec(
            num_scalar_prefetch=1, grid=(4, L, nr),
            in_specs=[pl.BlockSpec(blk, lambda p, l, i, c: (p, l, c[0] * nr + i, 0)),
                      pl.BlockSpec(blk, lambda p, l, i, c: (p, l, i, 0))],
            out_specs=pl.BlockSpec(blk, lambda p, l, i, c: (p, l, i, 0))),
        compiler_params=_cp(3),
    )(cidx, g, s)


def _rs_chips(ps):
    nt = len(ps)

    def body(*refs):
        p_refs, q_refs = refs[:nt], refs[nt:2 * nt]
        send_sems, recv_sems, local_sems = refs[2 * nt:]
        x, y, c, chips = _place()
        b = 2 * x + y
        cps = []
        for t in range(nt):
            cp = pltpu.make_async_copy(p_refs[t].at[b], q_refs[t].at[b], local_sems.at[t])
            cp.start()
            cps.append(cp)
            for j, (px, py) in enumerate(chips):
                cp = _rcopy(p_refs[t].at[2 * px + py], q_refs[t].at[b], send_sems, recv_sems, 3 * t + j, (px, py, c))
                cp.start()
                cps.append(cp)
        for t in range(nt):
            for j, (px, py) in enumerate(chips):
                blk = q_refs[t].at[2 * px + py]
                _rcopy(blk, blk, send_sems, recv_sems, 3 * t + j, (px, py, c)).wait_recv()
        for t in range(nt):
            cps[4 * t].wait()
            for j in range(3):
                cps[4 * t + 1 + j].wait_send()

    return pl.pallas_call(
        body, name="rs_chips", in_specs=[ANY] * nt, out_specs=[ANY] * nt,
        out_shape=[jax.ShapeDtypeStruct(p.shape, p.dtype) for p in ps],
        scratch_shapes=[pltpu.SemaphoreType.DMA((3 * nt,)), pltpu.SemaphoreType.DMA((3 * nt,)),
                        pltpu.SemaphoreType.DMA((nt,))],
    )(*ps)


def _sum_chips(q):
    _, L, h, C = q.shape
    rb = _rows_block(h)

    def body(q_ref, o_ref):
        acc = q_ref[0].astype(F32)
        for s in range(1, 4):
            acc = acc + q_ref[s].astype(F32)
        o_ref[...] = acc

    return pl.pallas_call(
        body, name="sum_chips", grid=(L, h // rb),
        in_specs=[pl.BlockSpec((4, None, rb, C), lambda l, i: (0, l, i, 0))],
        out_specs=pl.BlockSpec((None, rb, C), lambda l, i: (l, i, 0)),
        out_shape=jax.ShapeDtypeStruct((L, h, C), F32),
        compiler_params=_cp(2),
    )(q)


def _rs_final(hs):
    nt = len(hs)

    def body(*refs):
        h_refs, o_refs = refs[:nt], refs[nt:2 * nt]
        send_sems, recv_sems, local_sems = refs[2 * nt:]
        x, y, c, _ = _place()
        cps = []
        for t in range(nt):
            h = hs[t].shape[1]
            mine = o_refs[t].at[:, pl.ds(c * h, h), :]
            lc = pltpu.make_async_copy(h_refs[t], mine, local_sems.at[t])
            lc.start()
            cp = _rcopy(h_refs[t], mine, send_sems, recv_sems, t, (x, y, 1 - c))
            cp.start()
            cps.append((lc, cp))
        for t in range(nt):
            h = hs[t].shape[1]
            other = o_refs[t].at[:, pl.ds((1 - c) * h, h), :]
            _rcopy(other, other, send_sems, recv_sems, t, (x, y, 1 - c)).wait_recv()
        for lc, cp in cps:
            lc.wait()
            cp.wait_send()

    return pl.pallas_call(
        body, name="rs_final", in_specs=[ANY] * nt, out_specs=[ANY] * nt,
        out_shape=[jax.ShapeDtypeStruct((h.shape[0], 2 * h.shape[1], h.shape[2]), F32) for h in hs],
        scratch_shapes=[pltpu.SemaphoreType.DMA((nt,)), pltpu.SemaphoreType.DMA((nt,)),
                        pltpu.SemaphoreType.DMA((nt,))],
    )(*hs)


def _adamw(g, w, m, v):
    L, R, C = g.shape
    rb = _rows_block(R)

    def body(g_ref, w_ref, m_ref, v_ref, d_ref, mo_ref, vo_ref):
        gg = g_ref[...]
        m2 = ADAM_B1 * m_ref[...] + (1.0 - ADAM_B1) * gg
        v2 = ADAM_B2 * v_ref[...] + (1.0 - ADAM_B2) * (gg * gg)
        mh = m2 / (1.0 - ADAM_B1 ** ADAM_STEP)
        vh = v2 / (1.0 - ADAM_B2 ** ADAM_STEP)
        d_ref[...] = -ADAM_LR * (mh / (jnp.sqrt(vh) + ADAM_EPS) + ADAM_WD * w_ref[...])
        mo_ref[...] = m2
        vo_ref[...] = v2

    spec = pl.BlockSpec((None, rb, C), lambda l, i: (l, i, 0))
    return pl.pallas_call(
        body, name="adamw", grid=(L, R // rb), in_specs=[spec] * 4, out_specs=[spec] * 3,
        out_shape=[jax.ShapeDtypeStruct(g.shape, F32)] * 3, compiler_params=_cp(2),
    )(g, w, m, v)


def _small_allreduce(p):
    R = p.shape[0]

    def body(p_ref, o_ref, buf_ref, send_sems, recv_sems):
        x, y, c, _ = _place()
        me = 4 * x + 2 * y + c
        flip = lambda a, f: 1 - a if f else a
        buf_ref[me] = p_ref[...]
        peers = [(flip(x, k >> 2 & 1), flip(y, k >> 1 & 1), flip(c, k & 1)) for k in range(1, 8)]
        cps = []
        for k, dev in enumerate(peers):
            cp = _rcopy(p_ref, buf_ref.at[me], send_sems, recv_sems, k, dev)
            cp.start()
            cps.append(cp)
        for k, (px, py, pc) in enumerate(peers):
            slot = buf_ref.at[4 * px + 2 * py + pc]
            _rcopy(slot, slot, send_sems, recv_sems, k, (px, py, pc)).wait_recv()
        for cp in cps:
            cp.wait_send()
        acc = buf_ref[0]
        for s in range(1, 8):
            acc = acc + buf_ref[s]
        o_ref[...] = acc

    vm = pl.BlockSpec(memory_space=pltpu.VMEM)
    return pl.pallas_call(
        body, name="small_allreduce", in_specs=[vm], out_specs=vm,
        out_shape=jax.ShapeDtypeStruct(p.shape, F32),
        scratch_shapes=[pltpu.VMEM((8, R, LANES), F32), pltpu.SemaphoreType.DMA((7,)), pltpu.SemaphoreType.DMA((7,))],
    )(p)


def _pack(arrs):
    flat = jnp.concatenate([a.reshape(-1) for a in arrs])
    pad = -flat.shape[0] % (8 * LANES)
    return jnp.pad(flat, (0, pad)).reshape(1, -1, LANES)


def _unpack(packed, like):
    flat = packed.reshape(-1)
    out, off = [], 0
    for a in like:
        out.append(flat[off:off + a.size].reshape(a.shape))
        off += a.size
    return out


def kernel(x, norm_ffn1, w_ffn1_in, w_ffn1_out, norm_mix, w_in, sinks, w_dw, b_dw, conv_ln_g, conv_ln_b, w_out, norm_ffn2, w_ffn2_in, w_ffn2_out, final_norm, loss_target, m_norm_ffn1, m_w_ffn1_in, m_w_ffn1_out, m_norm_mix, m_w_in, m_sinks, m_w_dw, m_b_dw, m_conv_ln_g, m_conv_ln_b, m_w_out, m_norm_ffn2, m_w_ffn2_in, m_w_ffn2_out, m_final_norm, v_norm_ffn1, v_w_ffn1_in, v_w_ffn1_out, v_norm_mix, v_w_in, v_sinks, v_w_dw, v_b_dw, v_conv_ln_g, v_conv_ln_b, v_w_out, v_norm_ffn2, v_w_ffn2_in, v_w_ffn2_out, v_final_norm):
    L, D = norm_ffn1.shape
    T = x.shape[1]
    FB = w_ffn1_in.shape[2]
    CH = b_dw.shape[1]
    QKV = ATTN_W + 2 * KV_W
    xs = x.reshape(T, D)
    tgt = loss_target.reshape(T, D)
    cx, cy, cc = lax.axis_index("x"), lax.axis_index("y"), lax.axis_index("c")
    chip = 2 * cx + cy

    W = []
    for l in range(L):
        ws = [w[l].astype(BF16) for w in (w_ffn1_in, w_ffn1_out, w_in, w_out, w_ffn2_in, w_ffn2_out)]
        g1i, g1o, gi, go, g2i, g2o, gdw = _gather_layer(ws, w_dw[l])
        wi = jnp.transpose(gi, (1, 0, 2)).reshape(D, -1)
        W.append(dict(
            f1i=g1i, f1o=g1o.reshape(2 * FB, D), f2i=g2i, f2o=g2o.reshape(2 * FB, D),
            wi=wi, wo=go.reshape(-1, D), wdw=jnp.transpose(gdw, (1, 0, 2)).reshape(CONV_W, CH)))

    row = lambda a, l: a[l].reshape(1, -1)

    saved = []
    for l in range(L):
        w = W[l]
        x0 = xs
        x1, gu1 = _ffn_fwd(x0, row(norm_ffn1, l), w["f1i"], w["f1o"])
        qkv, u = _mixproj_fwd(x1, row(norm_mix, l), w["wi"])
        attn = _attn_fwd(row(sinks, l), qkv)
        conv, ypre = _conv_fwd(u, w["wdw"], row(b_dw, l), row(conv_ln_g, l), row(conv_ln_b, l))
        x2 = _mixout_fwd(x1, attn, conv, w["wo"])
        xs, gu2 = _ffn_fwd(x2, row(norm_ffn2, l), w["f2i"], w["f2o"])
        saved.append((x0, gu1, x1, qkv, u, attn, conv, ypre, x2, gu2))

    loss_part, dx, d_final = _loss_head(xs, final_norm.reshape(1, D), tgt)
    loss = lax.psum(loss_part[0, 0], ("x", "y", "c"))

    acc_f1i = lax.empty((4, L, D, FB), F32)
    acc_f2i = lax.empty((4, L, D, FB), F32)
    acc_f1o = lax.empty((4, L, FB // 2, D), F32)
    acc_f2o = lax.empty((4, L, FB // 2, D), F32)
    g_wi, g_wo = [None] * L, [None] * L
    d_n1, d_nm, d_n2 = [None] * L, [None] * L, [None] * L
    d_sk, d_bdw, d_lg, d_lb, d_wdw = [None] * L, [None] * L, [None] * L, [None] * L, [None] * L
    for l in reversed(range(L)):
        w = W[l]
        x0, gu1, x1, qkv, u, attn, conv, ypre, x2, gu2 = saved[l]
        dgu, a, dyb = _ffn_bwd_act(dx, gu2, w["f2o"])
        dx, d_n2[l], hb = _ffn_rms_bwd(dx, x2, row(norm_ffn2, l), dgu, w["f2i"])
        acc_f2i = _wgrad_ffn_in(hb, dgu, acc_f2i, l)
        acc_f2o = _wgrad_ffn_out(a, dyb, acc_f2o, l)
        dyb, dattn, dconv = _mixout_bwd(dx, w["wo"])
        g_wo[l] = jnp.concatenate([_wgrad_plain(attn, dyb), _wgrad_plain(conv, dyb)], axis=0).reshape(4, -1, D)
        du, dwdw, dvec = _conv_bwd(dconv, ypre, u, w["wdw"], row(conv_ln_g, l), row(conv_ln_b, l))
        d_wdw[l], d_bdw[l], d_lg[l], d_lb[l] = dwdw[:CONV_W], dvec[0], dvec[1], dvec[2]
        dq, dkv, dsk = _attn_bwd(row(sinks, l), qkv, dattn)
        d_sk[l] = dsk[:, 0]
        wi = w["wi"]
        dx, d_nm[l], hb = _mix_rms_bwd(dx, x1, row(norm_mix, l), [dq, dkv, du],
                                       [wi[:, :ATTN_W], wi[:, ATTN_W:QKV], wi[:, QKV:]])
        gwi = jnp.concatenate([_wgrad_plain(hb, dq), _wgrad_plain(hb, dkv), _wgrad_plain(hb, du)], axis=1)
        g_wi[l] = jnp.transpose(gwi.reshape(D, 4, -1), (1, 0, 2))
        dgu, a, dyb = _ffn_bwd_act(dx, gu1, w["f1o"])
        dx, d_n1[l], hb = _ffn_rms_bwd(dx, x0, row(norm_ffn1, l), dgu, w["f1i"])
        acc_f1i = _wgrad_ffn_in(hb, dgu, acc_f1i, l)
        acc_f1o = _wgrad_ffn_out(a, dyb, acc_f1o, l)
    grad_x = dx.reshape(x.shape)

    gs = [acc_f1i, acc_f1o, jnp.stack(g_wi, axis=1), jnp.stack(g_wo, axis=1), acc_f2i, acc_f2o]
    cidx = cc.reshape(1).astype(jnp.int32)
    sib = _rs_sibling(gs)
    parts = [_sum_halves(cidx, g, s) for g, s in zip(gs, sib)]
    qs = _rs_chips(parts)
    halves = [_sum_chips(q) for q in qs]
    big_g = _rs_final(halves)
    big_w = (w_ffn1_in, w_ffn1_out, w_in, w_out, w_ffn2_in, w_ffn2_out)
    big_m = (m_w_ffn1_in, m_w_ffn1_out, m_w_in, m_w_out, m_w_ffn2_in, m_w_ffn2_out)
    big_v = (v_w_ffn1_in, v_w_ffn1_out, v_w_in, v_w_out, v_w_ffn2_in, v_w_ffn2_out)
    big_upd = [_adamw(g, w_, m_, v_) for g, w_, m_, v_ in zip(big_g, big_w, big_m, big_v)]

    small_g = [jnp.concatenate(d, axis=0) for d in (d_n1, d_nm, d_n2)] + [d_final, jnp.stack(d_sk)] + \
              [jnp.stack(d) for d in (d_bdw, d_lg, d_lb, d_wdw)]
    small_sum = _unpack(_small_allreduce(_pack(small_g)[0]), small_g)
    g_wdw = lax.dynamic_slice_in_dim(small_sum[8], chip * w_dw.shape[2], w_dw.shape[2], axis=2)
    small_g = [small_sum[0], small_sum[1], small_sum[2], small_sum[3].reshape(D), small_sum[4],
               small_sum[5], small_sum[6], small_sum[7], g_wdw]
    small_w = (norm_ffn1, norm_mix, norm_ffn2, final_norm, sinks, b_dw, conv_ln_g, conv_ln_b, w_dw)
    small_m = (m_norm_ffn1, m_norm_mix, m_norm_ffn2, m_final_norm, m_sinks, m_b_dw, m_conv_ln_g, m_conv_ln_b, m_w_dw)
    small_v = (v_norm_ffn1, v_norm_mix, v_norm_ffn2, v_final_norm, v_sinks, v_b_dw, v_conv_ln_g, v_conv_ln_b, v_w_dw)
    upd = _adamw(_pack(small_g), _pack(small_w), _pack(small_m), _pack(small_v))
    small_upd = [_unpack(u_, small_w) for u_ in upd]

    order = ("norm_ffn1", "w_ffn1_in", "w_ffn1_out", "norm_mix", "w_in", "sinks", "w_dw", "b_dw", "conv_ln_g",
             "conv_ln_b", "w_out", "norm_ffn2", "w_ffn2_in", "w_ffn2_out", "final_norm")
    small_names = ("norm_ffn1", "norm_mix", "norm_ffn2", "final_norm", "sinks", "b_dw", "conv_ln_g", "conv_ln_b", "w_dw")
    big_names = ("w_ffn1_in", "w_ffn1_out", "w_in", "w_out", "w_ffn2_in", "w_ffn2_out")
    grads, deltas, new_m, new_v = {}, {}, {}, {}
    for i, nme in enumerate(small_names):
        grads[nme], deltas[nme], new_m[nme], new_v[nme] = small_g[i], small_upd[0][i], small_upd[1][i], small_upd[2][i]
    for i, nme in enumerate(big_names):
        grads[nme] = big_g[i]
        deltas[nme], new_m[nme], new_v[nme] = big_upd[i]
    return (loss, grad_x, *[grads[n] for n in order], *[deltas[n] for n in order],
            *[new_m[n] for n in order], *[new_v[n] for n in order])
```

```python
import functools

import jax
import jax.numpy as jnp
from jax import lax
from jax.experimental import pallas as pl
from jax.experimental.pallas import tpu as pltpu

F32, BF16 = jnp.float32, jnp.bfloat16
EPS = 1e-6
NEG_INF = -1e30
HEAD_DIM = 64
N_HEADS = 8
N_KV = 2
GROUP = N_HEADS // N_KV
WINDOW = 128
ATTN_W = N_HEADS * HEAD_DIM
KV_W = N_KV * HEAD_DIM
CONV_W = 31
HALO = 32
CONV_ROWS = 32
SCALE = 1.0 / 8.0
ADAM_LR, ADAM_B1, ADAM_B2, ADAM_EPS, ADAM_WD, ADAM_STEP = 0.001, 0.9, 0.999, 1e-08, 0.01, 10
TM = 512
LANES = 128
VMEM_LIMIT = 52 * 1024 * 1024
MESH = pl.DeviceIdType.MESH
ANY = pl.BlockSpec(memory_space=pl.ANY)
HBM = pl.BlockSpec(memory_space=pltpu.HBM)
SEM = pl.BlockSpec(memory_space=pltpu.SEMAPHORE)
VMEM = pl.BlockSpec(memory_space=pltpu.VMEM)
EFFECT = pltpu.SideEffectType.DATAFLOW_SIDE_EFFECTING
TOKEN = jax.ShapeDtypeStruct((8, LANES), F32)


def _cp(n):
    return pltpu.CompilerParams(dimension_semantics=("arbitrary",) * n, vmem_limit_bytes=VMEM_LIMIT)


def _dot(a, b):
    return jnp.dot(a, b, preferred_element_type=F32)


def _dot_nt(a, b):
    return lax.dot_general(a, b, (((1,), (1,)), ((), ())), preferred_element_type=F32)


def _dot_tn(a, b):
    return lax.dot_general(a, b, (((0,), (0,)), ((), ())), preferred_element_type=F32)


def _sigmoid(v):
    return 1.0 / (1.0 + jnp.exp(-v))


def _place():
    x, y, c = lax.axis_index("x"), lax.axis_index("y"), lax.axis_index("c")
    chips = [(1 - x, y), (x, 1 - y), (1 - x, 1 - y)]
    return x, y, c, chips


def _rcopy(src, dst, send_sems, recv_sems, k, dev):
    return pltpu.make_async_remote_copy(src_ref=src, dst_ref=dst, send_sem=send_sems.at[k],
                                        recv_sem=recv_sems.at[k], device_id=dev, device_id_type=MESH)


def _hbm(a):
    return pltpu.with_memory_space_constraint(a, pltpu.HBM)


def _ici_start(name, srcs, lands, plan):
    n = len(srcs)

    def body(*refs):
        src, land = refs[:n], refs[n:2 * n]
        send_sems, recv_sems, token = refs[2 * n], refs[2 * n + 1], refs[-1]
        x, y, c, chips = _place()
        for t in range(n):
            for j, (px, py) in enumerate(chips):
                s, d, _ = plan(src[t], land[t], t, 2 * x + y, c, 2 * px + py)
                _rcopy(s, d, send_sems, recv_sems, 3 * t + j, (px, py, c)).start()
        token[...] = jnp.zeros_like(token)

    arrs = list(srcs) + list(lands)
    return pl.pallas_call(
        body, name=name,
        out_shape=(pltpu.SemaphoreType.DMA((3 * n,)), pltpu.SemaphoreType.DMA((3 * n,)),
                   *[pltpu.HBM(a.shape, a.dtype) for a in arrs], TOKEN),
        in_specs=[HBM] * (2 * n), out_specs=(SEM, SEM, *[HBM] * (2 * n), VMEM),
        input_output_aliases={i: 2 + i for i in range(2 * n)},
        compiler_params=pltpu.CompilerParams(has_side_effects=EFFECT),
    )(*[_hbm(a) for a in arrs])


def _ici_wait(name, started, n, plan, after):
    send_sems, recv_sems, thru = started[0], started[1], started[2:2 + 2 * n]

    def body(*refs):
        src, land = refs[:n], refs[n:2 * n]
        send_sems, recv_sems, token = refs[2 * n], refs[2 * n + 1], refs[-1]
        x, y, c, chips = _place()
        for t in range(n):
            for j, (px, py) in enumerate(chips):
                s, _, a = plan(src[t], land[t], t, 2 * x + y, c, 2 * px + py)
                cp = _rcopy(s, a, send_sems, recv_sems, 3 * t + j, (px, py, c))
                cp.wait_send()
                cp.wait_recv()
        token[...] = jnp.zeros_like(token)

    out = pl.pallas_call(
        body, name=name,
        out_shape=(*[pltpu.HBM(a.shape, a.dtype) for a in thru], TOKEN),
        in_specs=[HBM] * (2 * n) + [SEM, SEM, ANY], out_specs=(*[HBM] * (2 * n), VMEM),
        input_output_aliases={i: i for i in range(2 * n)},
        compiler_params=pltpu.CompilerParams(has_side_effects=EFFECT),
    )(*thru, send_sems, recv_sems, after)
    return out[:n], out[n:2 * n], out[-1]


def _half(ref_rows, which):
    h = ref_rows // 2
    return pl.ds(which * h, h)


def _gather_plan(src, land, t, b, c, pb):
    if len(src.shape) == 2 and src.shape[0] % 2 == 0:
        hs = _half(src.shape[0], c)
        return src.at[hs], land.at[b, hs], land.at[pb, hs]
    return src, land.at[b], land.at[pb]


def _gather_share(srcs, lands):
    n = len(srcs)

    def body(*refs):
        src, land_in, land = refs[:n], refs[n:2 * n], refs[2 * n:3 * n]
        send_sems, recv_sems, local_sems = refs[3 * n:]
        x, y, c, chips = _place()
        b, sib = 2 * x + y, (x, y, 1 - c)
        pend = []
        for t in range(n):
            cp = pltpu.make_async_copy(src[t], land[t].at[b], local_sems.at[t])
            cp.start()
            pend.append(cp.wait)
        for t in range(n - 1):
            for j, (px, py) in enumerate(chips):
                hs = _half(srcs[t].shape[0], c)
                cp = _rcopy(land_in[t].at[2 * px + py, hs], land[t].at[2 * px + py, hs], send_sems, recv_sems, 3 * t + j, sib)
                cp.start()
                pend.append(cp.wait_send)
        for t in range(n - 1):
            for j, (px, py) in enumerate(chips):
                other = land[t].at[2 * px + py, _half(srcs[t].shape[0], 1 - c)]
                _rcopy(other, other, send_sems, recv_sems, 3 * t + j, sib).wait_recv()
        for w in pend:
            w()

    ns = 3 * (n - 1)
    return pl.pallas_call(
        body, name="gather_share", out_shape=[jax.ShapeDtypeStruct(a.shape, a.dtype) for a in lands],
        in_specs=[ANY] * (2 * n), out_specs=[ANY] * n, input_output_aliases={n + t: t for t in range(n)},
        scratch_shapes=[pltpu.SemaphoreType.DMA((ns,)), pltpu.SemaphoreType.DMA((ns,)), pltpu.SemaphoreType.DMA((n,))],
    )(*srcs, *lands)


def _rs_plan(src, land, t, b, c, pb):
    return src.at[pb], land.at[b], land.at[pb]


def _rs_sibling(gs):
    nt = len(gs)

    def body(*refs):
        g_refs, s_refs = refs[:nt], refs[nt:2 * nt]
        send_sems, recv_sems = refs[2 * nt:]
        x, y, c, _ = _place()
        cps = []
        for t in range(nt):
            cp = _rcopy(g_refs[t].at[:, _half(gs[t].shape[1], 1 - c), :], s_refs[t], send_sems, recv_sems, t, (x, y, 1 - c))
            cp.start()
            cps.append(cp)
        for cp in cps:
            cp.wait()

    return pl.pallas_call(
        body, name="rs_sibling", in_specs=[ANY] * nt, out_specs=[ANY] * nt,
        out_shape=[jax.ShapeDtypeStruct((4, g.shape[1] // 2, g.shape[2]), F32) for g in gs],
        scratch_shapes=[pltpu.SemaphoreType.DMA((nt,)), pltpu.SemaphoreType.DMA((nt,))],
    )(*gs)


def _rows_block(h):
    for rb in (128, 176, 64, 32, 16):
        if h % rb == 0:
            return rb
    return h


def _sum_halves(cidx, g, s):
    _, R, C = g.shape
    rb = _rows_block(R // 2)
    nr = R // 2 // rb

    def body(c_ref, g_ref, s_ref, o_ref):
        o_ref[...] = (g_ref[...] + s_ref[...]).astype(BF16)

    blk = (None, rb, C)
    return pl.pallas_call(
        body, name="sum_halves", out_shape=jax.ShapeDtypeStruct(s.shape, BF16),
        grid_spec=pltpu.PrefetchScalarGridSpec(
            num_scalar_prefetch=1, grid=(4, nr),
            in_specs=[pl.BlockSpec(blk, lambda p, i, c: (p, c[0] * nr + i, 0)),
                      pl.BlockSpec(blk, lambda p, i, c: (p, i, 0))],
            out_specs=pl.BlockSpec(blk, lambda p, i, c: (p, i, 0))),
        compiler_params=_cp(2),
    )(cidx, g, s)


def _rs_share(ps, qs):
    nt = len(ps)

    def body(*refs):
        p_refs, q_out, qsib = refs[:nt], refs[2 * nt:3 * nt], refs[3 * nt:4 * nt]
        send_sems, recv_sems, local_sems = refs[4 * nt:]
        x, y, c, _ = _place()
        b = 2 * x + y
        local = []
        for t in range(nt):
            cp = pltpu.make_async_copy(p_refs[t].at[b], q_out[t].at[b], local_sems.at[t])
            cp.start()
            local.append(cp)
        cps = []
        for t in range(nt):
            local[t].wait()
            cp = _rcopy(q_out[t], qsib[t], send_sems, recv_sems, t, (x, y, 1 - c))
            cp.start()
            cps.append(cp)
        for cp in cps:
            cp.wait()

    shapes = [jax.ShapeDtypeStruct(q.shape, q.dtype) for q in qs]
    out = pl.pallas_call(
        body, name="rs_share", in_specs=[ANY] * (2 * nt), out_specs=[ANY] * (2 * nt), out_shape=shapes + shapes,
        input_output_aliases={nt + t: t for t in range(nt)},
        scratch_shapes=[pltpu.SemaphoreType.DMA((nt,)), pltpu.SemaphoreType.DMA((nt,)), pltpu.SemaphoreType.DMA((nt,))],
    )(*ps, *qs)
    return out[:nt], out[nt:]


def _adam_update(gg, w, m, v):
    m2 = ADAM_B1 * m + (1.0 - ADAM_B1) * gg
    v2 = ADAM_B2 * v + (1.0 - ADAM_B2) * (gg * gg)
    mh = m2 / (1.0 - ADAM_B1 ** ADAM_STEP)
    vh = v2 / (1.0 - ADAM_B2 ** ADAM_STEP)
    return -ADAM_LR * (mh / (jnp.sqrt(vh) + ADAM_EPS) + ADAM_WD * w), m2, v2


def _adamw_layer(cidx, q_own, q_sib, w, m, v, bufs, l):
    L, R, C = w.shape
    h = R // 2
    rb = _rows_block(h)
    nr = h // rb

    def body(c_ref, qo_ref, qs_ref, w_ref, m_ref, v_ref, *rest):
        g_ref, d_ref, mo_ref, vo_ref = rest[-4:]
        own = pl.program_id(0) == c_ref[0]
        gg = jnp.zeros((rb, C), F32)
        for s in range(4):
            gg = gg + jnp.where(own, qo_ref[s], qs_ref[s]).astype(F32)
        g_ref[...] = gg
        d_ref[...], mo_ref[...], vo_ref[...] = _adam_update(gg, w_ref[...], m_ref[...], v_ref[...])

    qspec = pl.BlockSpec((4, rb, C), lambda hh, i, c: (0, i, 0))
    wspec = pl.BlockSpec((None, rb, C), lambda hh, i, c: (l, hh * nr + i, 0))
    return pl.pallas_call(
        body, name="adamw_layer", out_shape=[jax.ShapeDtypeStruct(w.shape, F32)] * 4,
        grid_spec=pltpu.PrefetchScalarGridSpec(
            num_scalar_prefetch=1, grid=(2, nr),
            in_specs=[qspec, qspec, wspec, wspec, wspec] + [ANY] * 4, out_specs=[wspec] * 4),
        input_output_aliases={6 + k: k for k in range(4)},
        compiler_params=_cp(2),
    )(cidx, q_own, q_sib, w, m, v, *bufs)


def _adamw(g, w, m, v):
    L, R, C = g.shape
    rb = _rows_block(R)

    def body(g_ref, w_ref, m_ref, v_ref, d_ref, mo_ref, vo_ref):
        d_ref[...], mo_ref[...], vo_ref[...] = _adam_update(g_ref[...], w_ref[...], m_ref[...], v_ref[...])

    spec = pl.BlockSpec((None, rb, C), lambda l, i: (l, i, 0))
    return pl.pallas_call(
        body, name="adamw", grid=(L, R // rb), in_specs=[spec] * 4, out_specs=[spec] * 3,
        out_shape=[jax.ShapeDtypeStruct(g.shape, F32)] * 3, compiler_params=_cp(2),
    )(g, w, m, v)


def _small_allreduce(p):
    R = p.shape[0]

    def body(p_ref, o_ref, buf_ref, send_sems, recv_sems):
        x, y, c, _ = _place()
        me = 4 * x + 2 * y + c
        flip = lambda a, f: 1 - a if f else a
        buf_ref[me] = p_ref[...]
        peers = [(flip(x, k >> 2 & 1), flip(y, k >> 1 & 1), flip(c, k & 1)) for k in range(1, 8)]
        cps = []
        for k, dev in enumerate(peers):
            cp = _rcopy(p_ref, buf_ref.at[me], send_sems, recv_sems, k, dev)
            cp.start()
            cps.append(cp)
        for k, (px, py, pc) in enumerate(peers):
            slot = buf_ref.at[4 * px + 2 * py + pc]
            _rcopy(slot, slot, send_sems, recv_sems, k, (px, py, pc)).wait_recv()
        for cp in cps:
            cp.wait_send()
        acc = buf_ref[0]
        for s in range(1, 8):
            acc = acc + buf_ref[s]
        o_ref[...] = acc

    return pl.pallas_call(
        body, name="small_allreduce", in_specs=[VMEM], out_specs=VMEM,
        out_shape=jax.ShapeDtypeStruct(p.shape, F32),
        scratch_shapes=[pltpu.VMEM((8, R, LANES), F32), pltpu.SemaphoreType.DMA((7,)), pltpu.SemaphoreType.DMA((7,))],
    )(p)


def _rms(xf, g):
    r = lax.rsqrt(jnp.mean(xf * xf, axis=-1, keepdims=True) + EPS)
    return xf * r, r


def _ffn_fwd(x, g, win, wout):
    T, D = x.shape
    FB = win.shape[2]
    tm = min(TM, T)

    def body(x_ref, g_ref, wg_ref, wu_ref, wo_ref, xo_ref, gu_ref, h_ref, acc_ref):
        j = pl.program_id(1)

        @pl.when(j == 0)
        def _():
            xh, _ = _rms(x_ref[...], None)
            h_ref[...] = (xh * g_ref[...]).astype(BF16)
            acc_ref[...] = jnp.zeros_like(acc_ref)

        h = h_ref[...]
        gate = _dot(h, wg_ref[...])
        up = _dot(h, wu_ref[...])
        gu_ref[0] = gate.astype(BF16)
        gu_ref[1] = up.astype(BF16)
        a = (gate * _sigmoid(gate) * up).astype(BF16)
        acc_ref[...] += _dot(a, wo_ref[...])

        @pl.when(j == 1)
        def _():
            xo_ref[...] = x_ref[...] + 0.5 * acc_ref[...]

    return pl.pallas_call(
        body, name="ffn_fwd", grid=(T // tm, 2),
        in_specs=[pl.BlockSpec((tm, D), lambda i, j: (i, 0)),
                  pl.BlockSpec((1, D), lambda i, j: (0, 0)),
                  pl.BlockSpec((None, D, FB), lambda i, j: (j, 0, 0)),
                  pl.BlockSpec((None, D, FB), lambda i, j: (j + 2, 0, 0)),
                  pl.BlockSpec((FB, D), lambda i, j: (j, 0))],
        out_specs=[pl.BlockSpec((tm, D), lambda i, j: (i, 0)),
                   pl.BlockSpec((2, tm, FB), lambda i, j: (0, i, j))],
        out_shape=[jax.ShapeDtypeStruct((T, D), F32), jax.ShapeDtypeStruct((2, T, 2 * FB), BF16)],
        scratch_shapes=[pltpu.VMEM((tm, D), BF16), pltpu.VMEM((tm, D), F32)],
        compiler_params=_cp(2),
    )(x, g, win, win, wout)


def _mixproj_fwd(x, g, w):
    T, D = x.shape
    W = w.shape[1]
    QKV = ATTN_W + 2 * KV_W
    tm = min(TM, T)

    def body(x_ref, g_ref, w_ref, qkv_ref, u_ref):
        xh, _ = _rms(x_ref[...], None)
        h = (xh * g_ref[...]).astype(BF16)
        qkv_ref[...] = _dot(h, w_ref[:, :QKV]).astype(BF16)
        u_ref[...] = _dot(h, w_ref[:, QKV:])

    return pl.pallas_call(
        body, name="mixproj_fwd", grid=(T // tm,),
        in_specs=[pl.BlockSpec((tm, D), lambda i: (i, 0)), pl.BlockSpec((1, D), lambda i: (0, 0)),
                  pl.BlockSpec((D, W), lambda i: (0, 0))],
        out_specs=[pl.BlockSpec((tm, QKV), lambda i: (i, 0)), pl.BlockSpec((tm, W - QKV), lambda i: (i, 0))],
        out_shape=[jax.ShapeDtypeStruct((T, QKV), BF16), jax.ShapeDtypeStruct((T, W - QKV), F32)],
        compiler_params=_cp(1),
    )(x, g, w)


def _attn_tables(n, g):
    rows, cols = GROUP * WINDOW, 2 * WINDOW
    row = lax.broadcasted_iota(jnp.int32, (rows, cols), 0)
    col = lax.broadcasted_iota(jnp.int32, (rows, cols), 1)
    dist = (row & (WINDOW - 1)) + WINDOW - col
    valid = (dist >= 0) & (dist < WINDOW) & ((n > 0) | (col >= WINDOW))
    hi = row >> 7
    slope = jnp.zeros((rows, cols), F32)
    for i in range(GROUP):
        slope = jnp.where(hi == i, 2.0 ** -(GROUP * g + i + 1), slope)
    bias = -slope * dist.astype(F32)
    return valid, bias


def _sink_col(sink_ref, g):
    hi = lax.broadcasted_iota(jnp.int32, (GROUP * WINDOW, 1), 0) >> 7
    col = jnp.zeros((GROUP * WINDOW, 1), F32)
    for i in range(GROUP):
        col = jnp.where(hi == i, sink_ref[0, GROUP * g + i], col)
    return col


def _stack_heads(ref, g):
    return jnp.concatenate([ref[:, (GROUP * g + i) * HEAD_DIM:(GROUP * g + i + 1) * HEAD_DIM]
                            for i in range(GROUP)], axis=0)


def _band(kvp_ref, kvc_ref, off):
    return jnp.concatenate([kvp_ref[:, off:off + HEAD_DIM], kvc_ref[:, off:off + HEAD_DIM]], axis=0)


def _attn_probs(qs, k, valid, bias, sink):
    s = _dot_nt(qs, k) * SCALE
    s = jnp.where(valid, s + bias, NEG_INF)
    m = jnp.maximum(jnp.max(s, axis=-1, keepdims=True), sink)
    p = jnp.exp(s - m)
    es = jnp.exp(sink - m)
    den = jnp.sum(p, axis=-1, keepdims=True) + es
    return p / den, es / den


def _attn_fwd(sinks, qkv):
    T = qkv.shape[0]
    nb = T // WINDOW

    def body(sink_ref, q_ref, kvp_ref, kvc_ref, o_ref):
        n = pl.program_id(0)
        for g in range(N_KV):
            valid, bias = _attn_tables(n, g)
            qs = _stack_heads(q_ref, g)
            k = _band(kvp_ref, kvc_ref, g * HEAD_DIM)
            v = _band(kvp_ref, kvc_ref, KV_W + g * HEAD_DIM)
            p, _ = _attn_probs(qs, k, valid, bias, _sink_col(sink_ref, g))
            o = _dot(p.astype(BF16), v)
            for i in range(GROUP):
                h = GROUP * g + i
                o_ref[:, h * HEAD_DIM:(h + 1) * HEAD_DIM] = o[i * WINDOW:(i + 1) * WINDOW].astype(BF16)

    return pl.pallas_call(
        body, name="attn_fwd", grid=(nb,),
        in_specs=[pl.BlockSpec(memory_space=pltpu.SMEM),
                  pl.BlockSpec((WINDOW, ATTN_W), lambda n: (n, 0)),
                  pl.BlockSpec((WINDOW, 2 * KV_W), lambda n: (jnp.maximum(n - 1, 0), 2)),
                  pl.BlockSpec((WINDOW, 2 * KV_W), lambda n: (n, 2))],
        out_specs=pl.BlockSpec((WINDOW, ATTN_W), lambda n: (n, 0)),
        out_shape=jax.ShapeDtypeStruct((T, ATTN_W), BF16),
        compiler_params=_cp(1),
    )(sinks, qkv, qkv, qkv)


def _shift_copies(src_ref, dst_ref, n):
    for b in range(1, 8):
        dst_ref[b - 1] = src_ref[b:b + n, :]


def _tap(src_ref, sh_ref, s, c0):
    a, b = divmod(s, 8)
    start = pl.multiple_of(c0 + 8 * a, 8)
    if b == 0:
        return src_ref[pl.ds(start, CONV_ROWS), :]
    return sh_ref[b - 1, pl.ds(start, CONV_ROWS), :]


def _glu_rows(u, ch):
    return u[:, :ch] * _sigmoid(u[:, ch:])


def _fill_z(zs_ref, zsh_ref, uc_ref, up_ref, i, ch, n):
    zs_ref[0:HALO] = jnp.where(i > 0, _glu_rows(up_ref[...], ch), 0.0)
    zs_ref[HALO:] = _glu_rows(uc_ref[...], ch)
    _shift_copies(zs_ref, zsh_ref, n - 8)


def _conv_fwd(u, w, b, lg, lb):
    T = u.shape[0]
    CH = u.shape[1] // 2
    tm = min(TM, T)
    n = tm + HALO
    hb = tm // HALO

    def body(uc_ref, up_ref, w_ref, b_ref, lg_ref, lb_ref, conv_ref, ypre_ref, zs_ref, zsh_ref):
        i = pl.program_id(0)
        _fill_z(zs_ref, zsh_ref, uc_ref, up_ref, i, CH, n)
        bias = b_ref[...]

        def chunk(ci, carry):
            c0 = pl.multiple_of(ci * CONV_ROWS, CONV_ROWS)
            acc = jnp.broadcast_to(bias, (CONV_ROWS, CH))
            for k in range(CONV_W):
                acc = acc + w_ref[k:k + 1, :] * _tap(zs_ref, zsh_ref, HALO - (CONV_W - 1) + k, c0)
            ypre_ref[pl.ds(c0, CONV_ROWS), :] = acc
            return carry

        lax.fori_loop(0, tm // CONV_ROWS, chunk, 0)
        y = ypre_ref[...]
        mu = jnp.mean(y, axis=-1, keepdims=True)
        d = y - mu
        var = jnp.mean(d * d, axis=-1, keepdims=True)
        o = d * lax.rsqrt(var + EPS) * lg_ref[...] + lb_ref[...]
        conv_ref[...] = (o * _sigmoid(o)).astype(BF16)

    vec = pl.BlockSpec((1, CH), lambda i: (0, 0))
    return pl.pallas_call(
        body, name="conv_fwd", grid=(T // tm,),
        in_specs=[pl.BlockSpec((tm, 2 * CH), lambda i: (i, 0)),
                  pl.BlockSpec((HALO, 2 * CH), lambda i: (jnp.maximum(i * hb - 1, 0), 0)),
                  pl.BlockSpec((CONV_W, CH), lambda i: (0, 0)), vec, vec, vec],
        out_specs=[pl.BlockSpec((tm, CH), lambda i: (i, 0)), pl.BlockSpec((tm, CH), lambda i: (i, 0))],
        out_shape=[jax.ShapeDtypeStruct((T, CH), BF16), jax.ShapeDtypeStruct((T, CH), F32)],
        scratch_shapes=[pltpu.VMEM((n, CH), F32), pltpu.VMEM((7, n - 8, CH), F32)],
        compiler_params=_cp(1),
    )(u, u, w, b, lg, lb)


def _mixout_fwd(x, attn, conv, wo):
    T, D = x.shape
    tm = min(TM, T)
    A = attn.shape[1]

    def body(x_ref, a_ref, c_ref, w_ref, xo_ref):
        xo_ref[...] = x_ref[...] + _dot(a_ref[...], w_ref[:A, :]) + _dot(c_ref[...], w_ref[A:, :])

    return pl.pallas_call(
        body, name="mixout_fwd", grid=(T // tm,),
        in_specs=[pl.BlockSpec((tm, D), lambda i: (i, 0)), pl.BlockSpec((tm, A), lambda i: (i, 0)),
                  pl.BlockSpec((tm, conv.shape[1]), lambda i: (i, 0)), pl.BlockSpec(wo.shape, lambda i: (0, 0))],
        out_specs=pl.BlockSpec((tm, D), lambda i: (i, 0)),
        out_shape=jax.ShapeDtypeStruct((T, D), F32),
        compiler_params=_cp(1),
    )(x, attn, conv, wo)


def _rms_bwd_rows(dh, xf, g):
    xh, r = _rms(xf, None)
    dxn = dh * g
    dx = r * (dxn - xh * jnp.mean(dxn * xh, axis=-1, keepdims=True))
    return dx, jnp.sum(dh * xh, axis=0, keepdims=True), xh * g


def _loss_head(x, g, tgt):
    T, D = x.shape
    tm = min(TM, T)

    def body(x_ref, g_ref, t_ref, loss_ref, dx_ref, dg_ref):
        @pl.when(pl.program_id(0) == 0)
        def _():
            loss_ref[...] = jnp.zeros_like(loss_ref)
            dg_ref[...] = jnp.zeros_like(dg_ref)

        xf = x_ref[...]
        g = g_ref[...]
        xh, _ = _rms(xf, None)
        e = xh * g - t_ref[...]
        loss_ref[...] += 0.5 * jnp.sum(jnp.mean(e * e, axis=-1, keepdims=True), axis=0, keepdims=True)
        dx, dg, _ = _rms_bwd_rows(e * (1.0 / D), xf, g)
        dx_ref[...] = dx
        dg_ref[...] += dg

    return pl.pallas_call(
        body, name="loss_head", grid=(T // tm,),
        in_specs=[pl.BlockSpec((tm, D), lambda i: (i, 0)), pl.BlockSpec((1, D), lambda i: (0, 0)),
                  pl.BlockSpec((tm, D), lambda i: (i, 0))],
        out_specs=[pl.BlockSpec((1, 1), lambda i: (0, 0)), pl.BlockSpec((tm, D), lambda i: (i, 0)),
                   pl.BlockSpec((1, D), lambda i: (0, 0))],
        out_shape=[jax.ShapeDtypeStruct((1, 1), F32), jax.ShapeDtypeStruct((T, D), F32),
                   jax.ShapeDtypeStruct((1, D), F32)],
        compiler_params=_cp(1),
    )(x, g, tgt)


def _ffn_bwd_act(dxo, gu, wout, dep):
    T, D = dxo.shape
    FB = gu.shape[2] // 2
    tm = min(TM, T)

    def body(dxo_ref, gu_ref, wo_ref, dep_ref, dgu_ref, a_ref, dyb_ref):
        @pl.when(pl.program_id(1) == 0)
        def _():
            dyb_ref[...] = (0.5 * dxo_ref[...]).astype(BF16)

        da = _dot_nt(dyb_ref[...], wo_ref[...])
        gate = gu_ref[0].astype(F32)
        up = gu_ref[1].astype(F32)
        sg = _sigmoid(gate)
        s = gate * sg
        a_ref[...] = (s * up).astype(BF16)
        dgu_ref[0] = (da * up * (sg * (1.0 + gate * (1.0 - sg)))).astype(BF16)
        dgu_ref[1] = (da * s).astype(BF16)

    return pl.pallas_call(
        body, name="ffn_bwd_act", grid=(T // tm, 2),
        in_specs=[pl.BlockSpec((tm, D), lambda i, j: (i, 0)),
                  pl.BlockSpec((2, tm, FB), lambda i, j: (0, i, j)),
                  pl.BlockSpec((FB, D), lambda i, j: (j, 0)), ANY],
        out_specs=[pl.BlockSpec((2, tm, FB), lambda i, j: (0, i, j)),
                   pl.BlockSpec((tm, FB), lambda i, j: (i, j)),
                   pl.BlockSpec((tm, D), lambda i, j: (i, 0))],
        out_shape=[jax.ShapeDtypeStruct((2, T, 2 * FB), BF16), jax.ShapeDtypeStruct((T, 2 * FB), BF16),
                   jax.ShapeDtypeStruct((T, D), BF16)],
        compiler_params=_cp(2),
    )(dxo, gu, wout, dep)


def _rms_matmul_bwd(name, dxo, x, g, dzs, ws, dz_specs, w_specs, nk):
    T, D = x.shape
    tm = min(TM, T)
    npair = len(dzs)

    def body(*refs):
        dxo_ref, x_ref, g_ref = refs[:3]
        dz_refs, w_refs = refs[3:3 + npair], refs[3 + npair:3 + 2 * npair]
        dxi_ref, dg_ref, hb_ref, acc_ref = refs[3 + 2 * npair:]
        i, k = pl.program_id(0), pl.program_id(1)

        @pl.when(k == 0)
        def _():
            acc_ref[...] = jnp.zeros_like(acc_ref)

        @pl.when((i == 0) & (k == 0))
        def _():
            dg_ref[...] = jnp.zeros_like(dg_ref)

        for p in range(npair):
            acc_ref[...] += _dot_nt(dz_refs[p][...], w_refs[p][...])

        @pl.when(k == nk - 1)
        def _():
            dx, dg, h = _rms_bwd_rows(acc_ref[...], x_ref[...], g_ref[...])
            dxi_ref[...] = dxo_ref[...] + dx
            dg_ref[...] += dg
            hb_ref[...] = h.astype(BF16)

    row = pl.BlockSpec((tm, D), lambda i, k: (i, 0))
    return pl.pallas_call(
        body, name=name, grid=(T // tm, nk),
        in_specs=[row, row, pl.BlockSpec((1, D), lambda i, k: (0, 0))] + list(dz_specs) + list(w_specs),
        out_specs=[row, pl.BlockSpec((1, D), lambda i, k: (0, 0)), row],
        out_shape=[jax.ShapeDtypeStruct((T, D), F32), jax.ShapeDtypeStruct((1, D), F32),
                   jax.ShapeDtypeStruct((T, D), BF16)],
        scratch_shapes=[pltpu.VMEM((tm, D), F32)],
        compiler_params=_cp(2),
    )(dxo, x, g, *dzs, *ws)


def _ffn_rms_bwd(dxo, x, g, dgu, win):
    tm = min(TM, x.shape[0])
    D, FB = win.shape[1], win.shape[2]
    return _rms_matmul_bwd(
        "ffn_rms_bwd", dxo, x, g, [dgu], [win],
        [pl.BlockSpec((None, tm, FB), lambda i, k: (k // 2, i, k % 2))],
        [pl.BlockSpec((None, D, FB), lambda i, k: (k, 0, 0))], 4)


def _mix_rms_bwd(dxo, x, g, dzs, ws):
    tm = min(TM, x.shape[0])
    return _rms_matmul_bwd(
        "mix_rms_bwd", dxo, x, g, dzs, ws,
        [pl.BlockSpec((tm, dz.shape[1]), lambda i, k: (i, 0)) for dz in dzs],
        [pl.BlockSpec(w.shape, lambda i, k: (0, 0)) for w in ws], 1)


def _wgrad(name, a, b, a_spec, b_spec, out_shape, out_spec, nblk):
    T = a.shape[0]
    tk = min(TM, T)

    def body(a_ref, b_ref, o_ref):
        @pl.when(pl.program_id(1) == 0)
        def _():
            o_ref[...] = jnp.zeros_like(o_ref)

        o_ref[...] += _dot_tn(a_ref[...], b_ref[...]).reshape(o_ref.shape)

    return pl.pallas_call(
        body, name=name, grid=(nblk, T // tk), in_specs=[a_spec, b_spec], out_specs=out_spec,
        out_shape=jax.ShapeDtypeStruct(out_shape, F32), compiler_params=_cp(2),
    )(a, b)


def _wgrad_ffn_in(hb, dgu):
    T, D = hb.shape
    FB = dgu.shape[2] // 2
    tk = min(TM, T)
    return _wgrad("wgrad_ffn_in", hb, dgu,
                  pl.BlockSpec((tk, D), lambda b, k: (k, 0)),
                  pl.BlockSpec((None, tk, FB), lambda b, k: (b // 2, k, b % 2)),
                  (4, D, FB), pl.BlockSpec((None, D, FB), lambda b, k: (b, 0, 0)), 4)


def _wgrad_ffn_out(a, dyb):
    T, D = dyb.shape
    FB = a.shape[1] // 2
    tk = min(TM, T)
    return _wgrad("wgrad_ffn_out", a, dyb,
                  pl.BlockSpec((tk, FB), lambda b, k: (k, b)),
                  pl.BlockSpec((tk, D), lambda b, k: (k, 0)),
                  (4, FB // 2, D), pl.BlockSpec((2, FB // 2, D), lambda b, k: (b, 0, 0)), 2)


def _wgrad_plain(a, b):
    T, M = a.shape
    N = b.shape[1]
    tk = min(TM, T)
    return _wgrad("wgrad_plain", a, b, pl.BlockSpec((tk, M), lambda i, k: (k, 0)),
                  pl.BlockSpec((tk, N), lambda i, k: (k, 0)), (M, N),
                  pl.BlockSpec((M, N), lambda i, k: (0, 0)), 1)


def _mixout_bwd(dxo, wo):
    T, D = dxo.shape
    tm = min(TM, T)
    A = ATTN_W
    C = wo.shape[0] - A

    def body(dxo_ref, w_ref, dyb_ref, da_ref, dc_ref):
        dyb = dxo_ref[...].astype(BF16)
        dyb_ref[...] = dyb
        da_ref[...] = _dot_nt(dyb, w_ref[:A, :]).astype(BF16)
        dc_ref[...] = _dot_nt(dyb, w_ref[A:, :])

    return pl.pallas_call(
        body, name="mixout_bwd", grid=(T // tm,),
        in_specs=[pl.BlockSpec((tm, D), lambda i: (i, 0)), pl.BlockSpec(wo.shape, lambda i: (0, 0))],
        out_specs=[pl.BlockSpec((tm, D), lambda i: (i, 0)), pl.BlockSpec((tm, A), lambda i: (i, 0)),
                   pl.BlockSpec((tm, C), lambda i: (i, 0))],
        out_shape=[jax.ShapeDtypeStruct((T, D), BF16), jax.ShapeDtypeStruct((T, A), BF16),
                   jax.ShapeDtypeStruct((T, C), F32)],
        compiler_params=_cp(1),
    )(dxo, wo)


def _conv_bwd(dconv, ypre, u, w, lg, lb):
    T, CH = dconv.shape
    tm = min(TM, T)
    n = tm + HALO
    hb = tm // HALO
    nt = T // tm
    nchunk = tm // CONV_ROWS

    def body(dc_ref, dcn_ref, yp_ref, ypn_ref, uc_ref, up_ref, w_ref, lg_ref, lb_ref,
             du_ref, dw_ref, dvec_ref, zs_ref, zsh_ref, dy_ref, dysh_ref, dz_ref):
        i = pl.program_id(0)

        @pl.when(i == 0)
        def _():
            dw_ref[...] = jnp.zeros_like(dw_ref)
            dvec_ref[...] = jnp.zeros_like(dvec_ref)

        g, bb = lg_ref[...], lb_ref[...]

        def ln_bwd(dc, yp):
            mu = jnp.mean(yp, axis=-1, keepdims=True)
            d = yp - mu
            rs = lax.rsqrt(jnp.mean(d * d, axis=-1, keepdims=True) + EPS)
            yn = d * rs
            o = yn * g + bb
            sg = _sigmoid(o)
            do = dc * (sg * (1.0 + o * (1.0 - sg)))
            dyn = do * g
            dyp = rs * (dyn - jnp.mean(dyn, axis=-1, keepdims=True)
                        - yn * jnp.mean(dyn * yn, axis=-1, keepdims=True))
            return dyp, do, yn

        dyp, do, yn = ln_bwd(dc_ref[...], yp_ref[...])
        dvec_ref[0:1, :] += jnp.sum(dyp, axis=0, keepdims=True)
        dvec_ref[1:2, :] += jnp.sum(do * yn, axis=0, keepdims=True)
        dvec_ref[2:3, :] += jnp.sum(do, axis=0, keepdims=True)
        dy_ref[0:tm] = dyp
        dyh, _, _ = ln_bwd(dcn_ref[...], ypn_ref[...])
        dy_ref[tm:] = jnp.where(i < nt - 1, dyh, 0.0)
        _shift_copies(dy_ref, dysh_ref, n - 8)
        _fill_z(zs_ref, zsh_ref, uc_ref, up_ref, i, CH, n)

        def chunk(ci, carry):
            c0 = pl.multiple_of(ci * CONV_ROWS, CONV_ROWS)
            acc = jnp.zeros((CONV_ROWS, CH), F32)
            for k in range(CONV_W):
                acc = acc + w_ref[k:k + 1, :] * _tap(dy_ref, dysh_ref, CONV_W - 1 - k, c0)
            dz_ref[pl.ds(c0, CONV_ROWS), :] = acc
            return carry

        lax.fori_loop(0, nchunk, chunk, 0)

        for k in range(CONV_W):
            def red(ci, acc, k=k):
                c0 = pl.multiple_of(ci * CONV_ROWS, CONV_ROWS)
                prod = dy_ref[pl.ds(c0, CONV_ROWS), :] * _tap(zs_ref, zsh_ref, HALO - (CONV_W - 1) + k, c0)
                return acc + jnp.sum(prod.reshape(CONV_ROWS // 8, 8, CH), axis=0)

            acc = lax.fori_loop(0, nchunk, red, jnp.zeros((8, CH), F32))
            dw_ref[k:k + 1, :] += jnp.sum(acc, axis=0, keepdims=True)

        uc = uc_ref[...]
        a = uc[:, :CH]
        sg = _sigmoid(uc[:, CH:])
        dz = dz_ref[...]
        du_ref[:, :CH] = (dz * sg).astype(BF16)
        du_ref[:, CH:] = (dz * a * sg * (1.0 - sg)).astype(BF16)

    cur = lambda c: pl.BlockSpec((tm, c), lambda i: (i, 0))
    nxt = lambda c: pl.BlockSpec((HALO, c), lambda i: (jnp.minimum((i + 1) * hb, T // HALO - 1), 0))
    vec = pl.BlockSpec((1, CH), lambda i: (0, 0))
    return pl.pallas_call(
        body, name="conv_bwd", grid=(nt,),
        in_specs=[cur(CH), nxt(CH), cur(CH), nxt(CH), cur(2 * CH),
                  pl.BlockSpec((HALO, 2 * CH), lambda i: (jnp.maximum(i * hb - 1, 0), 0)),
                  pl.BlockSpec((CONV_W, CH), lambda i: (0, 0)), vec, vec],
        out_specs=[pl.BlockSpec((tm, 2 * CH), lambda i: (i, 0)), pl.BlockSpec((32, CH), lambda i: (0, 0)),
                   pl.BlockSpec((8, CH), lambda i: (0, 0))],
        out_shape=[jax.ShapeDtypeStruct((T, 2 * CH), BF16), jax.ShapeDtypeStruct((32, CH), F32),
                   jax.ShapeDtypeStruct((8, CH), F32)],
        scratch_shapes=[pltpu.VMEM((n, CH), F32), pltpu.VMEM((7, n - 8, CH), F32),
                        pltpu.VMEM((n, CH), F32), pltpu.VMEM((7, n - 8, CH), F32), pltpu.VMEM((tm, CH), F32)],
        compiler_params=_cp(1),
    )(dconv, dconv, ypre, ypre, u, u, w, lg, lb)


def _attn_bwd(sinks, qkv, dattn):
    T = qkv.shape[0]
    nb = T // WINDOW

    def body(sink_ref, q_ref, kvp_ref, kvc_ref, do_ref, dq_ref, dkv_ref, dsk_ref, carry_ref):
        n = pl.program_id(0)

        @pl.when(n == 0)
        def _():
            dsk_ref[...] = jnp.zeros_like(dsk_ref)
            carry_ref[...] = jnp.zeros_like(carry_ref)

        @pl.when(n < nb)
        def _():
            for g in range(N_KV):
                valid, bias = _attn_tables(n, g)
                qs = _stack_heads(q_ref, g)
                dos = _stack_heads(do_ref, g)
                k = _band(kvp_ref, kvc_ref, g * HEAD_DIM)
                v = _band(kvp_ref, kvc_ref, KV_W + g * HEAD_DIM)
                p, ps = _attn_probs(qs, k, valid, bias, _sink_col(sink_ref, g))
                dp = _dot_nt(dos, v)
                delta = jnp.sum(p * dp, axis=-1, keepdims=True)
                dsb = (p * (dp - delta)).astype(BF16)
                dsink = -ps * delta
                dqs = _dot(dsb, k) * SCALE
                dk = _dot_tn(dsb, qs) * SCALE
                dv = _dot_tn(p.astype(BF16), dos)
                for i in range(GROUP):
                    h = GROUP * g + i
                    dq_ref[:, h * HEAD_DIM:(h + 1) * HEAD_DIM] = dqs[i * WINDOW:(i + 1) * WINDOW].astype(BF16)
                    dsk_ref[h:h + 1, :] += jnp.sum(dsink[i * WINDOW:(i + 1) * WINDOW], axis=0, keepdims=True)
                for off, d in ((g * HEAD_DIM, dk), (KV_W + g * HEAD_DIM, dv)):
                    dkv_ref[:, off:off + HEAD_DIM] = (carry_ref[:, off:off + HEAD_DIM] + d[:WINDOW]).astype(BF16)
                    carry_ref[:, off:off + HEAD_DIM] = d[WINDOW:]

        @pl.when(n == nb)
        def _():
            dkv_ref[...] = carry_ref[...].astype(BF16)

    last = nb - 1
    return pl.pallas_call(
        body, name="attn_bwd", grid=(nb + 1,),
        in_specs=[pl.BlockSpec(memory_space=pltpu.SMEM),
                  pl.BlockSpec((WINDOW, ATTN_W), lambda n: (jnp.minimum(n, last), 0)),
                  pl.BlockSpec((WINDOW, 2 * KV_W), lambda n: (jnp.clip(n - 1, 0, last), 2)),
                  pl.BlockSpec((WINDOW, 2 * KV_W), lambda n: (jnp.minimum(n, last), 2)),
                  pl.BlockSpec((WINDOW, ATTN_W), lambda n: (jnp.minimum(n, last), 0))],
        out_specs=[pl.BlockSpec((WINDOW, ATTN_W), lambda n: (jnp.minimum(n, last), 0)),
                   pl.BlockSpec((WINDOW, 2 * KV_W), lambda n: (jnp.maximum(n - 1, 0), 0)),
                   pl.BlockSpec((8, LANES), lambda n: (0, 0))],
        out_shape=[jax.ShapeDtypeStruct((T, ATTN_W), BF16), jax.ShapeDtypeStruct((T, 2 * KV_W), BF16),
                   jax.ShapeDtypeStruct((8, LANES), F32)],
        scratch_shapes=[pltpu.VMEM((WINDOW, 2 * KV_W), F32)],
        compiler_params=_cp(1),
    )(sinks, qkv, qkv, qkv, dattn)


def _pack(arrs):
    flat = jnp.concatenate([a.reshape(-1) for a in arrs])
    pad = -flat.shape[0] % (8 * LANES)
    return jnp.pad(flat, (0, pad)).reshape(1, -1, LANES)


def _unpack(packed, like):
    flat = packed.reshape(-1)
    out, off = [], 0
    for a in like:
        out.append(flat[off:off + a.size].reshape(a.shape))
        off += a.size
    return out


def kernel(x, norm_ffn1, w_ffn1_in, w_ffn1_out, norm_mix, w_in, sinks, w_dw, b_dw, conv_ln_g, conv_ln_b, w_out, norm_ffn2, w_ffn2_in, w_ffn2_out, final_norm, loss_target, m_norm_ffn1, m_w_ffn1_in, m_w_ffn1_out, m_norm_mix, m_w_in, m_sinks, m_w_dw, m_b_dw, m_conv_ln_g, m_conv_ln_b, m_w_out, m_norm_ffn2, m_w_ffn2_in, m_w_ffn2_out, m_final_norm, v_norm_ffn1, v_w_ffn1_in, v_w_ffn1_out, v_norm_mix, v_w_in, v_sinks, v_w_dw, v_b_dw, v_conv_ln_g, v_conv_ln_b, v_w_out, v_norm_ffn2, v_w_ffn2_in, v_w_ffn2_out, v_final_norm):
    L, D = norm_ffn1.shape
    T = x.shape[1]
    FB = w_ffn1_in.shape[2]
    CH = b_dw.shape[1]
    QKV = ATTN_W + 2 * KV_W
    xs = x.reshape(T, D)
    tgt = loss_target.reshape(T, D)
    cx, cy, cc = lax.axis_index("x"), lax.axis_index("y"), lax.axis_index("c")
    chip = 2 * cx + cy
    cidx = cc.reshape(1).astype(jnp.int32)
    big_w = (w_ffn1_in, w_ffn1_out, w_in, w_out, w_ffn2_in, w_ffn2_out)
    big_m = (m_w_ffn1_in, m_w_ffn1_out, m_w_in, m_w_out, m_w_ffn2_in, m_w_ffn2_out)
    big_v = (v_w_ffn1_in, v_w_ffn1_out, v_w_in, v_w_out, v_w_ffn2_in, v_w_ffn2_out)
    NW = len(big_w) + 1

    def shards(l, tok):
        t16 = tok[0, 0].astype(BF16)
        return [w[l].astype(BF16) + t16 for w in big_w] + [w_dw[l] + tok[0, 0]]

    def gather_start(l, tok):
        srcs = shards(l, tok)
        lands = [lax.empty((4,) + s.shape, s.dtype) for s in srcs]
        return _ici_start("gather_start", srcs, lands, _gather_plan)

    row = lambda a, l: a[l].reshape(1, -1)

    saved, W = [], []
    started = gather_start(0, jnp.zeros((8, LANES), F32))
    after = xs
    for l in range(L):
        srcs, lands, tok = _ici_wait("gather_wait", started, NW, _gather_plan, after)
        if l + 1 < L:
            started = gather_start(l + 1, tok)
            tok = started[-1]
        g1i, g1o, gi, go, g2i, g2o, gdw = _gather_share(srcs, lands)
        w = dict(f1i=g1i, f1o=g1o.reshape(2 * FB, D), f2i=g2i, f2o=g2o.reshape(2 * FB, D),
                 wi=jnp.transpose(gi, (1, 0, 2)).reshape(D, -1), wo=go.reshape(-1, D),
                 wdw=jnp.transpose(gdw, (1, 0, 2)).reshape(CONV_W, CH))
        W.append(w)
        x0 = xs
        x1, gu1 = _ffn_fwd(x0, row(norm_ffn1, l) + tok[0, 0], w["f1i"], w["f1o"])
        qkv, u = _mixproj_fwd(x1, row(norm_mix, l), w["wi"])
        attn = _attn_fwd(row(sinks, l), qkv)
        conv, ypre = _conv_fwd(u, w["wdw"], row(b_dw, l), row(conv_ln_g, l), row(conv_ln_b, l))
        x2 = _mixout_fwd(x1, attn, conv, w["wo"])
        xs, gu2 = _ffn_fwd(x2, row(norm_ffn2, l), w["f2i"], w["f2o"])
        saved.append((x0, gu1, x1, qkv, u, attn, conv, ypre, x2, gu2))
        after = xs

    loss_part, dx, d_final = _loss_head(xs, final_norm.reshape(1, D), tgt)
    loss = lax.psum(loss_part[0, 0], ("x", "y", "c"))

    bufs = [[lax.empty(w_.shape, F32) for _ in range(4)] for w_ in big_w]
    d_n1, d_nm, d_n2 = [None] * L, [None] * L, [None] * L
    d_sk, d_bdw, d_lg, d_lb, d_wdw = [None] * L, [None] * L, [None] * L, [None] * L, [None] * L

    def finish(l, rs_started, after):
        parts, qs, _ = _ici_wait("rs_wait", rs_started, len(big_w), _rs_plan, after)
        q_own, q_sib = _rs_share(parts, qs)
        for t in range(len(parts)):
            bufs[t] = _adamw_layer(cidx, q_own[t], q_sib[t], big_w[t], big_m[t], big_v[t], bufs[t], l)

    pending = None
    tok = jnp.zeros((8, LANES), F32)
    for l in reversed(range(L)):
        w = W[l]
        x0, gu1, x1, qkv, u, attn, conv, ypre, x2, gu2 = saved[l]
        dgu, a, dyb = _ffn_bwd_act(dx, gu2, w["f2o"], tok)
        dx, d_n2[l], hb = _ffn_rms_bwd(dx, x2, row(norm_ffn2, l), dgu, w["f2i"])
        g_f2i, g_f2o = _wgrad_ffn_in(hb, dgu), _wgrad_ffn_out(a, dyb)
        dyb, dattn, dconv = _mixout_bwd(dx, w["wo"])
        g_wo = jnp.concatenate([_wgrad_plain(attn, dyb), _wgrad_plain(conv, dyb)], axis=0).reshape(4, -1, D)
        du, dwdw, dvec = _conv_bwd(dconv, ypre, u, w["wdw"], row(conv_ln_g, l), row(conv_ln_b, l))
        d_wdw[l], d_bdw[l], d_lg[l], d_lb[l] = dwdw[:CONV_W], dvec[0], dvec[1], dvec[2]
        dq, dkv, dsk = _attn_bwd(row(sinks, l), qkv, dattn)
        d_sk[l] = dsk[:, 0]
        wi = w["wi"]
        dx, d_nm[l], hb = _mix_rms_bwd(dx, x1, row(norm_mix, l), [dq, dkv, du],
                                       [wi[:, :ATTN_W], wi[:, ATTN_W:QKV], wi[:, QKV:]])
        gwi = jnp.concatenate([_wgrad_plain(hb, dq), _wgrad_plain(hb, dkv), _wgrad_plain(hb, du)], axis=1)
        g_wi = jnp.transpose(gwi.reshape(D, 4, -1), (1, 0, 2))
        dgu, a, dyb = _ffn_bwd_act(dx, gu1, w["f1o"], tok)
        dx, d_n1[l], hb = _ffn_rms_bwd(dx, x0, row(norm_ffn1, l), dgu, w["f1i"])
        g_f1i, g_f1o = _wgrad_ffn_in(hb, dgu), _wgrad_ffn_out(a, dyb)
        gs = [g_f1i, g_f1o, g_wi, g_wo, g_f2i, g_f2o]
        parts = [_sum_halves(cidx, g, s) for g, s in zip(gs, _rs_sibling(gs))]
        if pending is not None:
            finish(*pending, after=parts[0])
        lands = [lax.empty(p.shape, p.dtype) for p in parts]
        rs_started = _ici_start("rs_start", parts, lands, _rs_plan)
        tok = rs_started[-1]
        pending = (l, rs_started)
    finish(*pending, after=dx)
    grad_x = dx.reshape(x.shape)

    small_g = [jnp.concatenate(d, axis=0) for d in (d_n1, d_nm, d_n2)] + [d_final, jnp.stack(d_sk)] + \
              [jnp.stack(d) for d in (d_bdw, d_lg, d_lb, d_wdw)]
    small_sum = _unpack(_small_allreduce(_pack(small_g)[0]), small_g)
    g_wdw = lax.dynamic_slice_in_dim(small_sum[8], chip * w_dw.shape[2], w_dw.shape[2], axis=2)
    small_g = [small_sum[0], small_sum[1], small_sum[2], small_sum[3].reshape(D), small_sum[4],
               small_sum[5], small_sum[6], small_sum[7], g_wdw]
    small_w = (norm_ffn1, norm_mix, norm_ffn2, final_norm, sinks, b_dw, conv_ln_g, conv_ln_b, w_dw)
    small_m = (m_norm_ffn1, m_norm_mix, m_norm_ffn2, m_final_norm, m_sinks, m_b_dw, m_conv_ln_g, m_conv_ln_b, m_w_dw)
    small_v = (v_norm_ffn1, v_norm_mix, v_norm_ffn2, v_final_norm, v_sinks, v_b_dw, v_conv_ln_g, v_conv_ln_b, v_w_dw)
    upd = _adamw(_pack(small_g), _pack(small_w), _pack(small_m), _pack(small_v))
    small_upd = [_unpack(u_, small_w) for u_ in upd]

    order = ("norm_ffn1", "w_ffn1_in", "w_ffn1_out", "norm_mix", "w_in", "sinks", "w_dw", "b_dw", "conv_ln_g",
             "conv_ln_b", "w_out", "norm_ffn2", "w_ffn2_in", "w_ffn2_out", "final_norm")
    small_names = ("norm_ffn1", "norm_mix", "norm_ffn2", "final_norm", "sinks", "b_dw", "conv_ln_g", "conv_ln_b", "w_dw")
    big_names = ("w_ffn1_in", "w_ffn1_out", "w_in", "w_out", "w_ffn2_in", "w_ffn2_out")
    grads, deltas, new_m, new_v = {}, {}, {}, {}
    for i, nme in enumerate(small_names):
        grads[nme], deltas[nme], new_m[nme], new_v[nme] = small_g[i], small_upd[0][i], small_upd[1][i], small_upd[2][i]
    for i, nme in enumerate(big_names):
        grads[nme], deltas[nme], new_m[nme], new_v[nme] = bufs[i]
    return (loss, grad_x, *[grads[n] for n in order], *[deltas[n] for n in order],
            *[new_m[n] for n in order], *[new_v[n] for n in order])
```

```python
import functools

import jax
import jax.numpy as jnp
from jax import lax
from jax.experimental import pallas as pl
from jax.experimental.pallas import tpu as pltpu

F32, BF16 = jnp.float32, jnp.bfloat16
EPS = 1e-6
NEG_INF = -1e30
HEAD_DIM = 64
N_HEADS = 8
N_KV = 2
GROUP = N_HEADS // N_KV
WINDOW = 128
ATTN_W = N_HEADS * HEAD_DIM
KV_W = N_KV * HEAD_DIM
CONV_W = 31
HALO = 32
CONV_ROWS = 32
SCALE = 1.0 / 8.0
ADAM_LR, ADAM_B1, ADAM_B2, ADAM_EPS, ADAM_WD, ADAM_STEP = 0.001, 0.9, 0.999, 1e-08, 0.01, 10
TM = 512
LANES = 128
VMEM_LIMIT = 52 * 1024 * 1024
MESH = pl.DeviceIdType.MESH
ANY = pl.BlockSpec(memory_space=pl.ANY)
HBM = pl.BlockSpec(memory_space=pltpu.HBM)
SEM = pl.BlockSpec(memory_space=pltpu.SEMAPHORE)
VMEM = pl.BlockSpec(memory_space=pltpu.VMEM)
EFFECT = pltpu.SideEffectType.DATAFLOW_SIDE_EFFECTING
TOKEN = jax.ShapeDtypeStruct((8, LANES), F32)


def _cp(n):
    return pltpu.CompilerParams(dimension_semantics=("arbitrary",) * n, vmem_limit_bytes=VMEM_LIMIT)


def _dot(a, b):
    return jnp.dot(a, b, preferred_element_type=F32)


def _dot_nt(a, b):
    return lax.dot_general(a, b, (((1,), (1,)), ((), ())), preferred_element_type=F32)


def _dot_tn(a, b):
    return lax.dot_general(a, b, (((0,), (0,)), ((), ())), preferred_element_type=F32)


def _sigmoid(v):
    return 1.0 / (1.0 + jnp.exp(-v))


def _place():
    x, y, c = lax.axis_index("x"), lax.axis_index("y"), lax.axis_index("c")
    chips = [(1 - x, y), (x, 1 - y), (1 - x, 1 - y)]
    return x, y, c, chips


def _rcopy(src, dst, send_sems, recv_sems, k, dev):
    return pltpu.make_async_remote_copy(src_ref=src, dst_ref=dst, send_sem=send_sems.at[k],
                                        recv_sem=recv_sems.at[k], device_id=dev, device_id_type=MESH)


def _hbm(a):
    return pltpu.with_memory_space_constraint(a, pltpu.HBM)


def _targets(sibling):
    x, y, c, chips = _place()
    if sibling:
        return 2 * x + y, c, [((x, y, 1 - c), 2 * x + y)]
    return 2 * x + y, c, [((px, py, c), 2 * px + py) for px, py in chips]


def _xchg_start(name, srcs, lands, plan, sibling=False):
    n = len(srcs)
    npeer = 1 if sibling else 3

    def body(*refs):
        src, land = refs[:n], refs[n:2 * n]
        send_sems, recv_sems, token = refs[2 * n], refs[2 * n + 1], refs[-1]
        b, c, peers = _targets(sibling)
        for t in range(n):
            for j, (dev, pb) in enumerate(peers):
                s, d, _ = plan(src[t], land[t], t, b, c, pb)
                _rcopy(s, d, send_sems, recv_sems, npeer * t + j, dev).start()
        token[...] = jnp.zeros_like(token)

    arrs = list(srcs) + list(lands)
    return pl.pallas_call(
        body, name=name,
        out_shape=(pltpu.SemaphoreType.DMA((npeer * n,)), pltpu.SemaphoreType.DMA((npeer * n,)),
                   *[pltpu.HBM(a.shape, a.dtype) for a in arrs], TOKEN),
        in_specs=[HBM] * (2 * n), out_specs=(SEM, SEM, *[HBM] * (2 * n), VMEM),
        input_output_aliases={i: 2 + i for i in range(2 * n)},
        compiler_params=pltpu.CompilerParams(has_side_effects=EFFECT),
    )(*[_hbm(a) for a in arrs])


def _xchg_wait(name, started, n, plan, after, sibling=False):
    send_sems, recv_sems, thru = started[0], started[1], started[2:2 + 2 * n]
    npeer = 1 if sibling else 3

    def body(*refs):
        src, land = refs[:n], refs[n:2 * n]
        send_sems, recv_sems, token = refs[2 * n], refs[2 * n + 1], refs[-1]
        b, c, peers = _targets(sibling)
        for t in range(n):
            for j, (dev, pb) in enumerate(peers):
                s, _, a = plan(src[t], land[t], t, b, c, pb)
                cp = _rcopy(s, a, send_sems, recv_sems, npeer * t + j, dev)
                cp.wait_send()
                cp.wait_recv()
        token[...] = jnp.zeros_like(token)

    out = pl.pallas_call(
        body, name=name,
        out_shape=(*[pltpu.HBM(a.shape, a.dtype) for a in thru], TOKEN),
        in_specs=[HBM] * (2 * n) + [SEM, SEM, ANY], out_specs=(*[HBM] * (2 * n), VMEM),
        input_output_aliases={i: i for i in range(2 * n)},
        compiler_params=pltpu.CompilerParams(has_side_effects=EFFECT),
    )(*thru, send_sems, recv_sems, after)
    return out[:n], out[n:2 * n], out[-1]


def _half(ref_rows, which):
    h = ref_rows // 2
    return pl.ds(which * h, h)


def _gather_plan(src, land, t, b, c, pb):
    if len(src.shape) == 2 and src.shape[0] % 2 == 0:
        hs = _half(src.shape[0], c)
        return src.at[hs], land.at[b, hs], land.at[pb, hs]
    return src, land.at[b], land.at[pb]


def _gather_share(lands):
    n = len(lands)

    def body(*refs):
        land_in, land = refs[:n], refs[n:2 * n]
        send_sems, recv_sems = refs[2 * n:]
        x, y, c, chips = _place()
        sib = (x, y, 1 - c)
        sends = []
        for t in range(n):
            for j, (px, py) in enumerate(chips):
                hs = _half(lands[t].shape[1], c)
                cp = _rcopy(land_in[t].at[2 * px + py, hs], land[t].at[2 * px + py, hs], send_sems, recv_sems, 3 * t + j, sib)
                cp.start()
                sends.append(cp)
        for t in range(n):
            for j, (px, py) in enumerate(chips):
                other = land[t].at[2 * px + py, _half(lands[t].shape[1], 1 - c)]
                _rcopy(other, other, send_sems, recv_sems, 3 * t + j, sib).wait_recv()
        for cp in sends:
            cp.wait_send()

    return pl.pallas_call(
        body, name="gather_share", out_shape=[jax.ShapeDtypeStruct(a.shape, a.dtype) for a in lands],
        in_specs=[ANY] * n, out_specs=[ANY] * n, input_output_aliases={t: t for t in range(n)},
        scratch_shapes=[pltpu.SemaphoreType.DMA((3 * n,)), pltpu.SemaphoreType.DMA((3 * n,))],
    )(*lands)


def _rs_plan(src, land, t, b, c, pb):
    return src.at[pb], land.at[b], land.at[pb]


def _sib_plan(src, land, t, b, c, pb):
    return src.at[:, _half(src.shape[1], 1 - c), :], land, land


def _rows_block(h):
    for rb in (128, 176, 64, 32, 16):
        if h % rb == 0:
            return rb
    return h


def _sum_halves(cidx, g, s):
    _, R, C = g.shape
    rb = _rows_block(R // 2)
    nr = R // 2 // rb

    def body(c_ref, g_ref, s_ref, o_ref):
        o_ref[...] = (g_ref[...] + s_ref[...]).astype(BF16)

    blk = (None, rb, C)
    return pl.pallas_call(
        body, name="sum_halves", out_shape=jax.ShapeDtypeStruct(s.shape, BF16),
        grid_spec=pltpu.PrefetchScalarGridSpec(
            num_scalar_prefetch=1, grid=(4, nr),
            in_specs=[pl.BlockSpec(blk, lambda p, i, c: (p, c[0] * nr + i, 0)),
                      pl.BlockSpec(blk, lambda p, i, c: (p, i, 0))],
            out_specs=pl.BlockSpec(blk, lambda p, i, c: (p, i, 0))),
        compiler_params=_cp(2),
    )(cidx, g, s)


def _rs_share(qs):
    nt = len(qs)

    def body(*refs):
        q_refs, qsib = refs[:nt], refs[nt:2 * nt]
        send_sems, recv_sems = refs[2 * nt:]
        x, y, c, _ = _place()
        cps = []
        for t in range(nt):
            cp = _rcopy(q_refs[t], qsib[t], send_sems, recv_sems, t, (x, y, 1 - c))
            cp.start()
            cps.append(cp)
        for cp in cps:
            cp.wait()

    return pl.pallas_call(
        body, name="rs_share", in_specs=[ANY] * nt, out_specs=[ANY] * nt,
        out_shape=[jax.ShapeDtypeStruct(q.shape, q.dtype) for q in qs],
        scratch_shapes=[pltpu.SemaphoreType.DMA((nt,)), pltpu.SemaphoreType.DMA((nt,))],
    )(*qs)


def _adam_update(gg, w, m, v):
    m2 = ADAM_B1 * m + (1.0 - ADAM_B1) * gg
    v2 = ADAM_B2 * v + (1.0 - ADAM_B2) * (gg * gg)
    mh = m2 / (1.0 - ADAM_B1 ** ADAM_STEP)
    vh = v2 / (1.0 - ADAM_B2 ** ADAM_STEP)
    return -ADAM_LR * (mh / (jnp.sqrt(vh) + ADAM_EPS) + ADAM_WD * w), m2, v2


def _adamw_layer(cidx, q_own, q_sib, w, m, v, bufs, l):
    L, R, C = w.shape
    h = R // 2
    rb = _rows_block(h)
    nr = h // rb

    def body(c_ref, qo_ref, qs_ref, w_ref, m_ref, v_ref, *rest):
        g_ref, d_ref, mo_ref, vo_ref = rest[-4:]
        own = pl.program_id(0) == c_ref[0]
        gg = jnp.zeros((rb, C), F32)
        for s in range(4):
            gg = gg + jnp.where(own, qo_ref[s], qs_ref[s]).astype(F32)
        g_ref[...] = gg
        d_ref[...], mo_ref[...], vo_ref[...] = _adam_update(gg, w_ref[...], m_ref[...], v_ref[...])

    qspec = pl.BlockSpec((4, rb, C), lambda hh, i, c: (0, i, 0))
    wspec = pl.BlockSpec((None, rb, C), lambda hh, i, c: (l, hh * nr + i, 0))
    return pl.pallas_call(
        body, name="adamw_layer", out_shape=[jax.ShapeDtypeStruct(w.shape, F32)] * 4,
        grid_spec=pltpu.PrefetchScalarGridSpec(
            num_scalar_prefetch=1, grid=(2, nr),
            in_specs=[qspec, qspec, wspec, wspec, wspec] + [ANY] * 4, out_specs=[wspec] * 4),
        input_output_aliases={6 + k: k for k in range(4)},
        compiler_params=_cp(2),
    )(cidx, q_own, q_sib, w, m, v, *bufs)


def _adamw(g, w, m, v):
    L, R, C = g.shape
    rb = _rows_block(R)

    def body(g_ref, w_ref, m_ref, v_ref, d_ref, mo_ref, vo_ref):
        d_ref[...], mo_ref[...], vo_ref[...] = _adam_update(g_ref[...], w_ref[...], m_ref[...], v_ref[...])

    spec = pl.BlockSpec((None, rb, C), lambda l, i: (l, i, 0))
    return pl.pallas_call(
        body, name="adamw", grid=(L, R // rb), in_specs=[spec] * 4, out_specs=[spec] * 3,
        out_shape=[jax.ShapeDtypeStruct(g.shape, F32)] * 3, compiler_params=_cp(2),
    )(g, w, m, v)


def _small_allreduce(p):
    R = p.shape[0]

    def body(p_ref, o_ref, buf_ref, send_sems, recv_sems):
        x, y, c, _ = _place()
        me = 4 * x + 2 * y + c
        flip = lambda a, f: 1 - a if f else a
        buf_ref[me] = p_ref[...]
        peers = [(flip(x, k >> 2 & 1), flip(y, k >> 1 & 1), flip(c, k & 1)) for k in range(1, 8)]
        cps = []
        for k, dev in enumerate(peers):
            cp = _rcopy(p_ref, buf_ref.at[me], send_sems, recv_sems, k, dev)
            cp.start()
            cps.append(cp)
        for k, (px, py, pc) in enumerate(peers):
            slot = buf_ref.at[4 * px + 2 * py + pc]
            _rcopy(slot, slot, send_sems, recv_sems, k, (px, py, pc)).wait_recv()
        for cp in cps:
            cp.wait_send()
        acc = buf_ref[0]
        for s in range(1, 8):
            acc = acc + buf_ref[s]
        o_ref[...] = acc

    return pl.pallas_call(
        body, name="small_allreduce", in_specs=[VMEM], out_specs=VMEM,
        out_shape=jax.ShapeDtypeStruct(p.shape, F32),
        scratch_shapes=[pltpu.VMEM((8, R, LANES), F32), pltpu.SemaphoreType.DMA((7,)), pltpu.SemaphoreType.DMA((7,))],
    )(p)


def _rms(xf, g):
    r = lax.rsqrt(jnp.mean(xf * xf, axis=-1, keepdims=True) + EPS)
    return xf * r, r


def _ffn_fwd(x, g, win, wout):
    T, D = x.shape
    FB = win.shape[2]
    tm = min(TM, T)

    def body(x_ref, g_ref, wg_ref, wu_ref, wo_ref, xo_ref, gu_ref, h_ref, acc_ref):
        j = pl.program_id(1)

        @pl.when(j == 0)
        def _():
            xh, _ = _rms(x_ref[...], None)
            h_ref[...] = (xh * g_ref[...]).astype(BF16)
            acc_ref[...] = jnp.zeros_like(acc_ref)

        h = h_ref[...]
        gate = _dot(h, wg_ref[...])
        up = _dot(h, wu_ref[...])
        gu_ref[0] = gate.astype(BF16)
        gu_ref[1] = up.astype(BF16)
        a = (gate * _sigmoid(gate) * up).astype(BF16)
        acc_ref[...] += _dot(a, wo_ref[...])

        @pl.when(j == 1)
        def _():
            xo_ref[...] = x_ref[...] + 0.5 * acc_ref[...]

    return pl.pallas_call(
        body, name="ffn_fwd", grid=(T // tm, 2),
        in_specs=[pl.BlockSpec((tm, D), lambda i, j: (i, 0)),
                  pl.BlockSpec((1, D), lambda i, j: (0, 0)),
                  pl.BlockSpec((None, D, FB), lambda i, j: (j, 0, 0)),
                  pl.BlockSpec((None, D, FB), lambda i, j: (j + 2, 0, 0)),
                  pl.BlockSpec((FB, D), lambda i, j: (j, 0))],
        out_specs=[pl.BlockSpec((tm, D), lambda i, j: (i, 0)),
                   pl.BlockSpec((2, tm, FB), lambda i, j: (0, i, j))],
        out_shape=[jax.ShapeDtypeStruct((T, D), F32), jax.ShapeDtypeStruct((2, T, 2 * FB), BF16)],
        scratch_shapes=[pltpu.VMEM((tm, D), BF16), pltpu.VMEM((tm, D), F32)],
        compiler_params=_cp(2),
    )(x, g, win, win, wout)


def _mixproj_fwd(x, g, w):
    T, D = x.shape
    W = w.shape[1]
    QKV = ATTN_W + 2 * KV_W
    tm = min(TM, T)

    def body(x_ref, g_ref, w_ref, qkv_ref, u_ref):
        xh, _ = _rms(x_ref[...], None)
        h = (xh * g_ref[...]).astype(BF16)
        qkv_ref[...] = _dot(h, w_ref[:, :QKV]).astype(BF16)
        u_ref[...] = _dot(h, w_ref[:, QKV:])

    return pl.pallas_call(
        body, name="mixproj_fwd", grid=(T // tm,),
        in_specs=[pl.BlockSpec((tm, D), lambda i: (i, 0)), pl.BlockSpec((1, D), lambda i: (0, 0)),
                  pl.BlockSpec((D, W), lambda i: (0, 0))],
        out_specs=[pl.BlockSpec((tm, QKV), lambda i: (i, 0)), pl.BlockSpec((tm, W - QKV), lambda i: (i, 0))],
        out_shape=[jax.ShapeDtypeStruct((T, QKV), BF16), jax.ShapeDtypeStruct((T, W - QKV), F32)],
        compiler_params=_cp(1),
    )(x, g, w)


def _attn_tables(n, g):
    rows, cols = GROUP * WINDOW, 2 * WINDOW
    row = lax.broadcasted_iota(jnp.int32, (rows, cols), 0)
    col = lax.broadcasted_iota(jnp.int32, (rows, cols), 1)
    dist = (row & (WINDOW - 1)) + WINDOW - col
    valid = (dist >= 0) & (dist < WINDOW) & ((n > 0) | (col >= WINDOW))
    hi = row >> 7
    slope = jnp.zeros((rows, cols), F32)
    for i in range(GROUP):
        slope = jnp.where(hi == i, 2.0 ** -(GROUP * g + i + 1), slope)
    bias = -slope * dist.astype(F32)
    return valid, bias


def _sink_col(sink_ref, g):
    hi = lax.broadcasted_iota(jnp.int32, (GROUP * WINDOW, 1), 0) >> 7
    col = jnp.zeros((GROUP * WINDOW, 1), F32)
    for i in range(GROUP):
        col = jnp.where(hi == i, sink_ref[0, GROUP * g + i], col)
    return col


def _stack_heads(ref, g):
    return jnp.concatenate([ref[:, (GROUP * g + i) * HEAD_DIM:(GROUP * g + i + 1) * HEAD_DIM]
                            for i in range(GROUP)], axis=0)


def _band(kvp_ref, kvc_ref, off):
    return jnp.concatenate([kvp_ref[:, off:off + HEAD_DIM], kvc_ref[:, off:off + HEAD_DIM]], axis=0)


def _attn_probs(qs, k, valid, bias, sink):
    s = _dot_nt(qs, k) * SCALE
    s = jnp.where(valid, s + bias, NEG_INF)
    m = jnp.maximum(jnp.max(s, axis=-1, keepdims=True), sink)
    p = jnp.exp(s - m)
    es = jnp.exp(sink - m)
    den = jnp.sum(p, axis=-1, keepdims=True) + es
    return p / den, es / den


def _attn_fwd(sinks, qkv):
    T = qkv.shape[0]
    nb = T // WINDOW

    def body(sink_ref, q_ref, kvp_ref, kvc_ref, o_ref):
        n = pl.program_id(0)
        for g in range(N_KV):
            valid, bias = _attn_tables(n, g)
            qs = _stack_heads(q_ref, g)
            k = _band(kvp_ref, kvc_ref, g * HEAD_DIM)
            v = _band(kvp_ref, kvc_ref, KV_W + g * HEAD_DIM)
            p, _ = _attn_probs(qs, k, valid, bias, _sink_col(sink_ref, g))
            o = _dot(p.astype(BF16), v)
            for i in range(GROUP):
                h = GROUP * g + i
                o_ref[:, h * HEAD_DIM:(h + 1) * HEAD_DIM] = o[i * WINDOW:(i + 1) * WINDOW].astype(BF16)

    return pl.pallas_call(
        body, name="attn_fwd", grid=(nb,),
        in_specs=[pl.BlockSpec(memory_space=pltpu.SMEM),
                  pl.BlockSpec((WINDOW, ATTN_W), lambda n: (n, 0)),
                  pl.BlockSpec((WINDOW, 2 * KV_W), lambda n: (jnp.maximum(n - 1, 0), 2)),
                  pl.BlockSpec((WINDOW, 2 * KV_W), lambda n: (n, 2))],
        out_specs=pl.BlockSpec((WINDOW, ATTN_W), lambda n: (n, 0)),
        out_shape=jax.ShapeDtypeStruct((T, ATTN_W), BF16),
        compiler_params=_cp(1),
    )(sinks, qkv, qkv, qkv)


def _shift_copies(src_ref, dst_ref, n):
    for b in range(1, 8):
        dst_ref[b - 1] = src_ref[b:b + n, :]


def _tap(src_ref, sh_ref, s, c0):
    a, b = divmod(s, 8)
    start = pl.multiple_of(c0 + 8 * a, 8)
    if b == 0:
        return src_ref[pl.ds(start, CONV_ROWS), :]
    return sh_ref[b - 1, pl.ds(start, CONV_ROWS), :]


def _glu_rows(u, ch):
    return u[:, :ch] * _sigmoid(u[:, ch:])


def _fill_z(zs_ref, zsh_ref, uc_ref, up_ref, i, ch, n):
    zs_ref[0:HALO] = jnp.where(i > 0, _glu_rows(up_ref[...], ch), 0.0)
    zs_ref[HALO:] = _glu_rows(uc_ref[...], ch)
    _shift_copies(zs_ref, zsh_ref, n - 8)


def _conv_fwd(u, w, b, lg, lb):
    T = u.shape[0]
    CH = u.shape[1] // 2
    tm = min(TM, T)
    n = tm + HALO
    hb = tm // HALO

    def body(uc_ref, up_ref, w_ref, b_ref, lg_ref, lb_ref, conv_ref, ypre_ref, zs_ref, zsh_ref):
        i = pl.program_id(0)
        _fill_z(zs_ref, zsh_ref, uc_ref, up_ref, i, CH, n)
        bias = b_ref[...]

        def chunk(ci, carry):
            c0 = pl.multiple_of(ci * CONV_ROWS, CONV_ROWS)
            acc = jnp.broadcast_to(bias, (CONV_ROWS, CH))
            for k in range(CONV_W):
                acc = acc + w_ref[k:k + 1, :] * _tap(zs_ref, zsh_ref, HALO - (CONV_W - 1) + k, c0)
            ypre_ref[pl.ds(c0, CONV_ROWS), :] = acc
            return carry

        lax.fori_loop(0, tm // CONV_ROWS, chunk, 0)
        y = ypre_ref[...]
        mu = jnp.mean(y, axis=-1, keepdims=True)
        d = y - mu
        var = jnp.mean(d * d, axis=-1, keepdims=True)
        o = d * lax.rsqrt(var + EPS) * lg_ref[...] + lb_ref[...]
        conv_ref[...] = (o * _sigmoid(o)).astype(BF16)

    vec = pl.BlockSpec((1, CH), lambda i: (0, 0))
    return pl.pallas_call(
        body, name="conv_fwd", grid=(T // tm,),
        in_specs=[pl.BlockSpec((tm, 2 * CH), lambda i: (i, 0)),
                  pl.BlockSpec((HALO, 2 * CH), lambda i: (jnp.maximum(i * hb - 1, 0), 0)),
                  pl.BlockSpec((CONV_W, CH), lambda i: (0, 0)), vec, vec, vec],
        out_specs=[pl.BlockSpec((tm, CH), lambda i: (i, 0)), pl.BlockSpec((tm, CH), lambda i: (i, 0))],
        out_shape=[jax.ShapeDtypeStruct((T, CH), BF16), jax.ShapeDtypeStruct((T, CH), F32)],
        scratch_shapes=[pltpu.VMEM((n, CH), F32), pltpu.VMEM((7, n - 8, CH), F32)],
        compiler_params=_cp(1),
    )(u, u, w, b, lg, lb)


def _mixout_fwd(x, attn, conv, wo):
    T, D = x.shape
    tm = min(TM, T)
    A = attn.shape[1]

    def body(x_ref, a_ref, c_ref, w_ref, xo_ref):
        xo_ref[...] = x_ref[...] + _dot(a_ref[...], w_ref[:A, :]) + _dot(c_ref[...], w_ref[A:, :])

    return pl.pallas_call(
        body, name="mixout_fwd", grid=(T // tm,),
        in_specs=[pl.BlockSpec((tm, D), lambda i: (i, 0)), pl.BlockSpec((tm, A), lambda i: (i, 0)),
                  pl.BlockSpec((tm, conv.shape[1]), lambda i: (i, 0)), pl.BlockSpec(wo.shape, lambda i: (0, 0))],
        out_specs=pl.BlockSpec((tm, D), lambda i: (i, 0)),
        out_shape=jax.ShapeDtypeStruct((T, D), F32),
        compiler_params=_cp(1),
    )(x, attn, conv, wo)


def _rms_bwd_rows(dh, xf, g):
    xh, r = _rms(xf, None)
    dxn = dh * g
    dx = r * (dxn - xh * jnp.mean(dxn * xh, axis=-1, keepdims=True))
    return dx, jnp.sum(dh * xh, axis=0, keepdims=True), xh * g


def _loss_head(x, g, tgt):
    T, D = x.shape
    tm = min(TM, T)

    def body(x_ref, g_ref, t_ref, loss_ref, dx_ref, dg_ref):
        @pl.when(pl.program_id(0) == 0)
        def _():
            loss_ref[...] = jnp.zeros_like(loss_ref)
            dg_ref[...] = jnp.zeros_like(dg_ref)

        xf = x_ref[...]
        g = g_ref[...]
        xh, _ = _rms(xf, None)
        e = xh * g - t_ref[...]
        loss_ref[...] += 0.5 * jnp.sum(jnp.mean(e * e, axis=-1, keepdims=True), axis=0, keepdims=True)
        dx, dg, _ = _rms_bwd_rows(e * (1.0 / D), xf, g)
        dx_ref[...] = dx
        dg_ref[...] += dg

    return pl.pallas_call(
        body, name="loss_head", grid=(T // tm,),
        in_specs=[pl.BlockSpec((tm, D), lambda i: (i, 0)), pl.BlockSpec((1, D), lambda i: (0, 0)),
                  pl.BlockSpec((tm, D), lambda i: (i, 0))],
        out_specs=[pl.BlockSpec((1, 1), lambda i: (0, 0)), pl.BlockSpec((tm, D), lambda i: (i, 0)),
                   pl.BlockSpec((1, D), lambda i: (0, 0))],
        out_shape=[jax.ShapeDtypeStruct((1, 1), F32), jax.ShapeDtypeStruct((T, D), F32),
                   jax.ShapeDtypeStruct((1, D), F32)],
        compiler_params=_cp(1),
    )(x, g, tgt)


def _ffn_bwd_act(dxo, gu, wout, dep):
    T, D = dxo.shape
    FB = gu.shape[2] // 2
    tm = min(TM, T)

    def body(dxo_ref, gu_ref, wo_ref, dep_ref, dgu_ref, a_ref, dyb_ref):
        @pl.when(pl.program_id(1) == 0)
        def _():
            dyb_ref[...] = (0.5 * dxo_ref[...]).astype(BF16)

        da = _dot_nt(dyb_ref[...], wo_ref[...])
        gate = gu_ref[0].astype(F32)
        up = gu_ref[1].astype(F32)
        sg = _sigmoid(gate)
        s = gate * sg
        a_ref[...] = (s * up).astype(BF16)
        dgu_ref[0] = (da * up * (sg * (1.0 + gate * (1.0 - sg)))).astype(BF16)
        dgu_ref[1] = (da * s).astype(BF16)

    return pl.pallas_call(
        body, name="ffn_bwd_act", grid=(T // tm, 2),
        in_specs=[pl.BlockSpec((tm, D), lambda i, j: (i, 0)),
                  pl.BlockSpec((2, tm, FB), lambda i, j: (0, i, j)),
                  pl.BlockSpec((FB, D), lambda i, j: (j, 0)), ANY],
        out_specs=[pl.BlockSpec((2, tm, FB), lambda i, j: (0, i, j)),
                   pl.BlockSpec((tm, FB), lambda i, j: (i, j)),
                   pl.BlockSpec((tm, D), lambda i, j: (i, 0))],
        out_shape=[jax.ShapeDtypeStruct((2, T, 2 * FB), BF16), jax.ShapeDtypeStruct((T, 2 * FB), BF16),
                   jax.ShapeDtypeStruct((T, D), BF16)],
        compiler_params=_cp(2),
    )(dxo, gu, wout, dep)


def _rms_matmul_bwd(name, dxo, x, g, dzs, ws, dz_specs, w_specs, nk):
    T, D = x.shape
    tm = min(TM, T)
    npair = len(dzs)

    def body(*refs):
        dxo_ref, x_ref, g_ref = refs[:3]
        dz_refs, w_refs = refs[3:3 + npair], refs[3 + npair:3 + 2 * npair]
        dxi_ref, dg_ref, hb_ref, acc_ref = refs[3 + 2 * npair:]
        i, k = pl.program_id(0), pl.program_id(1)

        @pl.when(k == 0)
        def _():
            acc_ref[...] = jnp.zeros_like(acc_ref)

        @pl.when((i == 0) & (k == 0))
        def _():
            dg_ref[...] = jnp.zeros_like(dg_ref)

        for p in range(npair):
            acc_ref[...] += _dot_nt(dz_refs[p][...], w_refs[p][...])

        @pl.when(k == nk - 1)
        def _():
            dx, dg, h = _rms_bwd_rows(acc_ref[...], x_ref[...], g_ref[...])
            dxi_ref[...] = dxo_ref[...] + dx
            dg_ref[...] += dg
            hb_ref[...] = h.astype(BF16)

    row = pl.BlockSpec((tm, D), lambda i, k: (i, 0))
    return pl.pallas_call(
        body, name=name, grid=(T // tm, nk),
        in_specs=[row, row, pl.BlockSpec((1, D), lambda i, k: (0, 0))] + list(dz_specs) + list(w_specs),
        out_specs=[row, pl.BlockSpec((1, D), lambda i, k: (0, 0)), row],
        out_shape=[jax.ShapeDtypeStruct((T, D), F32), jax.ShapeDtypeStruct((1, D), F32),
                   jax.ShapeDtypeStruct((T, D), BF16)],
        scratch_shapes=[pltpu.VMEM((tm, D), F32)],
        compiler_params=_cp(2),
    )(dxo, x, g, *dzs, *ws)


def _ffn_rms_bwd(dxo, x, g, dgu, win):
    tm = min(TM, x.shape[0])
    D, FB = win.shape[1], win.shape[2]
    return _rms_matmul_bwd(
        "ffn_rms_bwd", dxo, x, g, [dgu], [win],
        [pl.BlockSpec((None, tm, FB), lambda i, k: (k // 2, i, k % 2))],
        [pl.BlockSpec((None, D, FB), lambda i, k: (k, 0, 0))], 4)


def _mix_rms_bwd(dxo, x, g, dzs, ws):
    tm = min(TM, x.shape[0])
    return _rms_matmul_bwd(
        "mix_rms_bwd", dxo, x, g, dzs, ws,
        [pl.BlockSpec((tm, dz.shape[1]), lambda i, k: (i, 0)) for dz in dzs],
        [pl.BlockSpec(w.shape, lambda i, k: (0, 0)) for w in ws], 1)


def _wgrad(name, a, b, a_spec, b_spec, out_shape, out_spec, nblk):
    T = a.shape[0]
    tk = min(TM, T)

    def body(a_ref, b_ref, o_ref):
        @pl.when(pl.program_id(1) == 0)
        def _():
            o_ref[...] = jnp.zeros_like(o_ref)

        o_ref[...] += _dot_tn(a_ref[...], b_ref[...]).reshape(o_ref.shape)

    return pl.pallas_call(
        body, name=name, grid=(nblk, T // tk), in_specs=[a_spec, b_spec], out_specs=out_spec,
        out_shape=jax.ShapeDtypeStruct(out_shape, F32), compiler_params=_cp(2),
    )(a, b)


def _wgrad_ffn_in(hb, dgu):
    T, D = hb.shape
    FB = dgu.shape[2] // 2
    tk = min(TM, T)
    return _wgrad("wgrad_ffn_in", hb, dgu,
                  pl.BlockSpec((tk, D), lambda b, k: (k, 0)),
                  pl.BlockSpec((None, tk, FB), lambda b, k: (b // 2, k, b % 2)),
                  (4, D, FB), pl.BlockSpec((None, D, FB), lambda b, k: (b, 0, 0)), 4)


def _wgrad_ffn_out(a, dyb):
    T, D = dyb.shape
    FB = a.shape[1] // 2
    tk = min(TM, T)
    return _wgrad("wgrad_ffn_out", a, dyb,
                  pl.BlockSpec((tk, FB), lambda b, k: (k, b)),
                  pl.BlockSpec((tk, D), lambda b, k: (k, 0)),
                  (4, FB // 2, D), pl.BlockSpec((2, FB // 2, D), lambda b, k: (b, 0, 0)), 2)


def _wgrad_plain(a, b):
    T, M = a.shape
    N = b.shape[1]
    tk = min(TM, T)
    return _wgrad("wgrad_plain", a, b, pl.BlockSpec((tk, M), lambda i, k: (k, 0)),
                  pl.BlockSpec((tk, N), lambda i, k: (k, 0)), (M, N),
                  pl.BlockSpec((M, N), lambda i, k: (0, 0)), 1)


def _mixout_bwd(dxo, wo):
    T, D = dxo.shape
    tm = min(TM, T)
    A = ATTN_W
    C = wo.shape[0] - A

    def body(dxo_ref, w_ref, dyb_ref, da_ref, dc_ref):
        dyb = dxo_ref[...].astype(BF16)
        dyb_ref[...] = dyb
        da_ref[...] = _dot_nt(dyb, w_ref[:A, :]).astype(BF16)
        dc_ref[...] = _dot_nt(dyb, w_ref[A:, :])

    return pl.pallas_call(
        body, name="mixout_bwd", grid=(T // tm,),
        in_specs=[pl.BlockSpec((tm, D), lambda i: (i, 0)), pl.BlockSpec(wo.shape, lambda i: (0, 0))],
        out_specs=[pl.BlockSpec((tm, D), lambda i: (i, 0)), pl.BlockSpec((tm, A), lambda i: (i, 0)),
                   pl.BlockSpec((tm, C), lambda i: (i, 0))],
        out_shape=[jax.ShapeDtypeStruct((T, D), BF16), jax.ShapeDtypeStruct((T, A), BF16),
                   jax.ShapeDtypeStruct((T, C), F32)],
        compiler_params=_cp(1),
    )(dxo, wo)


def _conv_bwd(dconv, ypre, u, w, lg, lb):
    T, CH = dconv.shape
    tm = min(TM, T)
    n = tm + HALO
    hb = tm // HALO
    nt = T // tm
    nchunk = tm // CONV_ROWS

    def body(dc_ref, dcn_ref, yp_ref, ypn_ref, uc_ref, up_ref, w_ref, lg_ref, lb_ref,
             du_ref, dw_ref, dvec_ref, zs_ref, zsh_ref, dy_ref, dysh_ref, dz_ref):
        i = pl.program_id(0)

        @pl.when(i == 0)
        def _():
            dw_ref[...] = jnp.zeros_like(dw_ref)
            dvec_ref[...] = jnp.zeros_like(dvec_ref)

        g, bb = lg_ref[...], lb_ref[...]

        def ln_bwd(dc, yp):
            mu = jnp.mean(yp, axis=-1, keepdims=True)
            d = yp - mu
            rs = lax.rsqrt(jnp.mean(d * d, axis=-1, keepdims=True) + EPS)
            yn = d * rs
            o = yn * g + bb
            sg = _sigmoid(o)
            do = dc * (sg * (1.0 + o * (1.0 - sg)))
            dyn = do * g
            dyp = rs * (dyn - jnp.mean(dyn, axis=-1, keepdims=True)
                        - yn * jnp.mean(dyn * yn, axis=-1, keepdims=True))
            return dyp, do, yn

        dyp, do, yn = ln_bwd(dc_ref[...], yp_ref[...])
        dvec_ref[0:1, :] += jnp.sum(dyp, axis=0, keepdims=True)
        dvec_ref[1:2, :] += jnp.sum(do * yn, axis=0, keepdims=True)
        dvec_ref[2:3, :] += jnp.sum(do, axis=0, keepdims=True)
        dy_ref[0:tm] = dyp
        dyh, _, _ = ln_bwd(dcn_ref[...], ypn_ref[...])
        dy_ref[tm:] = jnp.where(i < nt - 1, dyh, 0.0)
        _shift_copies(dy_ref, dysh_ref, n - 8)
        _fill_z(zs_ref, zsh_ref, uc_ref, up_ref, i, CH, n)

        def chunk(ci, carry):
            c0 = pl.multiple_of(ci * CONV_ROWS, CONV_ROWS)
            acc = jnp.zeros((CONV_ROWS, CH), F32)
            for k in range(CONV_W):
                acc = acc + w_ref[k:k + 1, :] * _tap(dy_ref, dysh_ref, CONV_W - 1 - k, c0)
            dz_ref[pl.ds(c0, CONV_ROWS), :] = acc
            return carry

        lax.fori_loop(0, nchunk, chunk, 0)

        for k in range(CONV_W):
            def red(ci, acc, k=k):
                c0 = pl.multiple_of(ci * CONV_ROWS, CONV_ROWS)
                prod = dy_ref[pl.ds(c0, CONV_ROWS), :] * _tap(zs_ref, zsh_ref, HALO - (CONV_W - 1) + k, c0)
                return acc + jnp.sum(prod.reshape(CONV_ROWS // 8, 8, CH), axis=0)

            acc = lax.fori_loop(0, nchunk, red, jnp.zeros((8, CH), F32))
            dw_ref[k:k + 1, :] += jnp.sum(acc, axis=0, keepdims=True)

        uc = uc_ref[...]
        a = uc[:, :CH]
        sg = _sigmoid(uc[:, CH:])
        dz = dz_ref[...]
        du_ref[:, :CH] = (dz * sg).astype(BF16)
        du_ref[:, CH:] = (dz * a * sg * (1.0 - sg)).astype(BF16)

    cur = lambda c: pl.BlockSpec((tm, c), lambda i: (i, 0))
    nxt = lambda c: pl.BlockSpec((HALO, c), lambda i: (jnp.minimum((i + 1) * hb, T // HALO - 1), 0))
    vec = pl.BlockSpec((1, CH), lambda i: (0, 0))
    return pl.pallas_call(
        body, name="conv_bwd", grid=(nt,),
        in_specs=[cur(CH), nxt(CH), cur(CH), nxt(CH), cur(2 * CH),
                  pl.BlockSpec((HALO, 2 * CH), lambda i: (jnp.maximum(i * hb - 1, 0), 0)),
                  pl.BlockSpec((CONV_W, CH), lambda i: (0, 0)), vec, vec],
        out_specs=[pl.BlockSpec((tm, 2 * CH), lambda i: (i, 0)), pl.BlockSpec((32, CH), lambda i: (0, 0)),
                   pl.BlockSpec((8, CH), lambda i: (0, 0))],
        out_shape=[jax.ShapeDtypeStruct((T, 2 * CH), BF16), jax.ShapeDtypeStruct((32, CH), F32),
                   jax.ShapeDtypeStruct((8, CH), F32)],
        scratch_shapes=[pltpu.VMEM((n, CH), F32), pltpu.VMEM((7, n - 8, CH), F32),
                        pltpu.VMEM((n, CH), F32), pltpu.VMEM((7, n - 8, CH), F32), pltpu.VMEM((tm, CH), F32)],
        compiler_params=_cp(1),
    )(dconv, dconv, ypre, ypre, u, u, w, lg, lb)


def _attn_bwd(sinks, qkv, dattn):
    T = qkv.shape[0]
    nb = T // WINDOW

    def body(sink_ref, q_ref, kvp_ref, kvc_ref, do_ref, dq_ref, dkv_ref, dsk_ref, carry_ref):
        n = pl.program_id(0)

        @pl.when(n == 0)
        def _():
            dsk_ref[...] = jnp.zeros_like(dsk_ref)
            carry_ref[...] = jnp.zeros_like(carry_ref)

        @pl.when(n < nb)
        def _():
            for g in range(N_KV):
                valid, bias = _attn_tables(n, g)
                qs = _stack_heads(q_ref, g)
                dos = _stack_heads(do_ref, g)
                k = _band(kvp_ref, kvc_ref, g * HEAD_DIM)
                v = _band(kvp_ref, kvc_ref, KV_W + g * HEAD_DIM)
                p, ps = _attn_probs(qs, k, valid, bias, _sink_col(sink_ref, g))
                dp = _dot_nt(dos, v)
                delta = jnp.sum(p * dp, axis=-1, keepdims=True)
                dsb = (p * (dp - delta)).astype(BF16)
                dsink = -ps * delta
                dqs = _dot(dsb, k) * SCALE
                dk = _dot_tn(dsb, qs) * SCALE
                dv = _dot_tn(p.astype(BF16), dos)
                for i in range(GROUP):
                    h = GROUP * g + i
                    dq_ref[:, h * HEAD_DIM:(h + 1) * HEAD_DIM] = dqs[i * WINDOW:(i + 1) * WINDOW].astype(BF16)
                    dsk_ref[h:h + 1, :] += jnp.sum(dsink[i * WINDOW:(i + 1) * WINDOW], axis=0, keepdims=True)
                for off, d in ((g * HEAD_DIM, dk), (KV_W + g * HEAD_DIM, dv)):
                    dkv_ref[:, off:off + HEAD_DIM] = (carry_ref[:, off:off + HEAD_DIM] + d[:WINDOW]).astype(BF16)
                    carry_ref[:, off:off + HEAD_DIM] = d[WINDOW:]

        @pl.when(n == nb)
        def _():
            dkv_ref[...] = carry_ref[...].astype(BF16)

    last = nb - 1
    return pl.pallas_call(
        body, name="attn_bwd", grid=(nb + 1,),
        in_specs=[pl.BlockSpec(memory_space=pltpu.SMEM),
                  pl.BlockSpec((WINDOW, ATTN_W), lambda n: (jnp.minimum(n, last), 0)),
                  pl.BlockSpec((WINDOW, 2 * KV_W), lambda n: (jnp.clip(n - 1, 0, last), 2)),
                  pl.BlockSpec((WINDOW, 2 * KV_W), lambda n: (jnp.minimum(n, last), 2)),
                  pl.BlockSpec((WINDOW, ATTN_W), lambda n: (jnp.minimum(n, last), 0))],
        out_specs=[pl.BlockSpec((WINDOW, ATTN_W), lambda n: (jnp.minimum(n, last), 0)),
                   pl.BlockSpec((WINDOW, 2 * KV_W), lambda n: (jnp.maximum(n - 1, 0), 0)),
                   pl.BlockSpec((8, LANES), lambda n: (0, 0))],
        out_shape=[jax.ShapeDtypeStruct((T, ATTN_W), BF16), jax.ShapeDtypeStruct((T, 2 * KV_W), BF16),
                   jax.ShapeDtypeStruct((8, LANES), F32)],
        scratch_shapes=[pltpu.VMEM((WINDOW, 2 * KV_W), F32)],
        compiler_params=_cp(1),
    )(sinks, qkv, qkv, qkv, dattn)


def _pack(arrs):
    flat = jnp.concatenate([a.reshape(-1) for a in arrs])
    pad = -flat.shape[0] % (8 * LANES)
    return jnp.pad(flat, (0, pad)).reshape(1, -1, LANES)


def _unpack(packed, like):
    flat = packed.reshape(-1)
    out, off = [], 0
    for a in like:
        out.append(flat[off:off + a.size].reshape(a.shape))
        off += a.size
    return out


def kernel(x, norm_ffn1, w_ffn1_in, w_ffn1_out, norm_mix, w_in, sinks, w_dw, b_dw, conv_ln_g, conv_ln_b, w_out, norm_ffn2, w_ffn2_in, w_ffn2_out, final_norm, loss_target, m_norm_ffn1, m_w_ffn1_in, m_w_ffn1_out, m_norm_mix, m_w_in, m_sinks, m_w_dw, m_b_dw, m_conv_ln_g, m_conv_ln_b, m_w_out, m_norm_ffn2, m_w_ffn2_in, m_w_ffn2_out, m_final_norm, v_norm_ffn1, v_w_ffn1_in, v_w_ffn1_out, v_norm_mix, v_w_in, v_sinks, v_w_dw, v_b_dw, v_conv_ln_g, v_conv_ln_b, v_w_out, v_norm_ffn2, v_w_ffn2_in, v_w_ffn2_out, v_final_norm):
    L, D = norm_ffn1.shape
    T = x.shape[1]
    FB = w_ffn1_in.shape[2]
    CH = b_dw.shape[1]
    QKV = ATTN_W + 2 * KV_W
    xs = x.reshape(T, D)
    tgt = loss_target.reshape(T, D)
    cx, cy, cc = lax.axis_index("x"), lax.axis_index("y"), lax.axis_index("c")
    chip = 2 * cx + cy
    cidx = cc.reshape(1).astype(jnp.int32)
    big_w = (w_ffn1_in, w_ffn1_out, w_in, w_out, w_ffn2_in, w_ffn2_out)
    big_m = (m_w_ffn1_in, m_w_ffn1_out, m_w_in, m_w_out, m_w_ffn2_in, m_w_ffn2_out)
    big_v = (v_w_ffn1_in, v_w_ffn1_out, v_w_in, v_w_out, v_w_ffn2_in, v_w_ffn2_out)
    NW = len(big_w) + 1

    def shards(l, tok):
        t16 = tok[0, 0].astype(BF16)
        return [w[l].astype(BF16) + t16 for w in big_w] + [w_dw[l] + tok[0, 0]]

    def own_slot(a):
        return lax.dynamic_update_index_in_dim(lax.empty((4,) + a.shape, a.dtype), a, chip, 0)

    def gather_start(l, tok):
        srcs = shards(l, tok)
        return _xchg_start("gather_start", srcs, [own_slot(s_) for s_ in srcs], _gather_plan)

    row = lambda a, l: a[l].reshape(1, -1)

    saved, W = [], []
    started = gather_start(0, jnp.zeros((8, LANES), F32))
    after = xs
    for l in range(L):
        _, lands, tok = _xchg_wait("gather_wait", started, NW, _gather_plan, after)
        if l + 1 < L:
            started = gather_start(l + 1, tok)
            tok = started[-1]
        g1i, g1o, gi, go, g2i, g2o = _gather_share(lands[:-1])
        w = dict(f1i=g1i, f1o=g1o.reshape(2 * FB, D), f2i=g2i, f2o=g2o.reshape(2 * FB, D),
                 wi=jnp.transpose(gi, (1, 0, 2)).reshape(D, -1), wo=go.reshape(-1, D),
                 wdw=jnp.transpose(lands[-1], (1, 0, 2)).reshape(CONV_W, CH))
        W.append(w)
        x0 = xs
        x1, gu1 = _ffn_fwd(x0, row(norm_ffn1, l) + tok[0, 0], w["f1i"], w["f1o"])
        qkv, u = _mixproj_fwd(x1, row(norm_mix, l), w["wi"])
        attn = _attn_fwd(row(sinks, l), qkv)
        conv, ypre = _conv_fwd(u, w["wdw"], row(b_dw, l), row(conv_ln_g, l), row(conv_ln_b, l))
        x2 = _mixout_fwd(x1, attn, conv, w["wo"])
        xs, gu2 = _ffn_fwd(x2, row(norm_ffn2, l), w["f2i"], w["f2o"])
        saved.append((x0, gu1, x1, qkv, u, attn, conv, ypre, x2, gu2))
        after = xs

    loss_part, dx, d_final = _loss_head(xs, final_norm.reshape(1, D), tgt)
    loss = lax.psum(loss_part[0, 0], ("x", "y", "c"))

    bufs = [[lax.empty(w_.shape, F32) for _ in range(4)] for w_ in big_w]
    d_n1, d_nm, d_n2 = [None] * L, [None] * L, [None] * L
    d_sk, d_bdw, d_lg, d_lb, d_wdw = [None] * L, [None] * L, [None] * L, [None] * L, [None] * L
    NB = len(big_w)

    def reduce_start(sib_started, after):
        gs, sibs, _ = _xchg_wait("sib_wait", sib_started, NB, _sib_plan, after, sibling=True)
        parts = [_sum_halves(cidx, g, s_) for g, s_ in zip(gs, sibs)]
        lands = [own_slot(lax.dynamic_index_in_dim(p, chip, 0, keepdims=False)) for p in parts]
        return _xchg_start("rs_start", parts, lands, _rs_plan)

    def finish(l, rs_started, after):
        _, qs, _ = _xchg_wait("rs_wait", rs_started, NB, _rs_plan, after)
        q_sib = _rs_share(qs)
        for t in range(NB):
            bufs[t] = _adamw_layer(cidx, qs[t], q_sib[t], big_w[t], big_m[t], big_v[t], bufs[t], l)

    sib_pending = rs_pending = None
    tok = jnp.zeros((8, LANES), F32)
    for l in reversed(range(L)):
        w = W[l]
        x0, gu1, x1, qkv, u, attn, conv, ypre, x2, gu2 = saved[l]
        dgu, a, dyb = _ffn_bwd_act(dx, gu2, w["f2o"], tok)
        dx, d_n2[l], hb = _ffn_rms_bwd(dx, x2, row(norm_ffn2, l), dgu, w["f2i"])
        g_f2i, g_f2o = _wgrad_ffn_in(hb, dgu), _wgrad_ffn_out(a, dyb)
        lg_row = row(conv_ln_g, l)
        if sib_pending is not None:
            rs_started = reduce_start(sib_pending[1], after=g_f2o)
            if rs_pending is not None:
                finish(*rs_pending, after=rs_started[-1])
            rs_pending = (sib_pending[0], rs_started)
            lg_row = lg_row + rs_started[-1][0, 0]
        dyb, dattn, dconv = _mixout_bwd(dx, w["wo"])
        g_wo = jnp.concatenate([_wgrad_plain(attn, dyb), _wgrad_plain(conv, dyb)], axis=0).reshape(4, -1, D)
        du, dwdw, dvec = _conv_bwd(dconv, ypre, u, w["wdw"], lg_row, row(conv_ln_b, l))
        d_wdw[l], d_bdw[l], d_lg[l], d_lb[l] = dwdw[:CONV_W], dvec[0], dvec[1], dvec[2]
        dq, dkv, dsk = _attn_bwd(row(sinks, l), qkv, dattn)
        d_sk[l] = dsk[:, 0]
        wi = w["wi"]
        dx, d_nm[l], hb = _mix_rms_bwd(dx, x1, row(norm_mix, l), [dq, dkv, du],
                                       [wi[:, :ATTN_W], wi[:, ATTN_W:QKV], wi[:, QKV:]])
        gwi = jnp.concatenate([_wgrad_plain(hb, dq), _wgrad_plain(hb, dkv), _wgrad_plain(hb, du)], axis=1)
        g_wi = jnp.transpose(gwi.reshape(D, 4, -1), (1, 0, 2))
        dgu, a, dyb = _ffn_bwd_act(dx, gu1, w["f1o"], tok)
        dx, d_n1[l], hb = _ffn_rms_bwd(dx, x0, row(norm_ffn1, l), dgu, w["f1i"])
        g_f1i, g_f1o = _wgrad_ffn_in(hb, dgu), _wgrad_ffn_out(a, dyb)
        gs = [g_f1i, g_f1o, g_wi, g_wo, g_f2i, g_f2o]
        sib_started = _xchg_start("sib_start", gs, [lax.empty((4, g.shape[1] // 2, g.shape[2]), F32) for g in gs],
                                  _sib_plan, sibling=True)
        tok = sib_started[-1]
        sib_pending = (l, sib_started)
    grad_x = dx.reshape(x.shape)
    after = tok
    if rs_pending is not None:
        finish(*rs_pending, after=tok)
        after = bufs[0][0]
    rs_started = reduce_start(sib_pending[1], after=after)

    small_g = [jnp.concatenate(d, axis=0) for d in (d_n1, d_nm, d_n2)] + [d_final, jnp.stack(d_sk)] + \
              [jnp.stack(d) for d in (d_bdw, d_lg, d_lb, d_wdw)]
    small_sum = _unpack(_small_allreduce(_pack(small_g)[0] + rs_started[-1][0, 0]), small_g)
    g_wdw = lax.dynamic_slice_in_dim(small_sum[8], chip * w_dw.shape[2], w_dw.shape[2], axis=2)
    small_g = [small_sum[0], small_sum[1], small_sum[2], small_sum[3].reshape(D), small_sum[4],
               small_sum[5], small_sum[6], small_sum[7], g_wdw]
    small_w = (norm_ffn1, norm_mix, norm_ffn2, final_norm, sinks, b_dw, conv_ln_g, conv_ln_b, w_dw)
    small_m = (m_norm_ffn1, m_norm_mix, m_norm_ffn2, m_final_norm, m_sinks, m_b_dw, m_conv_ln_g, m_conv_ln_b, m_w_dw)
    small_v = (v_norm_ffn1, v_norm_mix, v_norm_ffn2, v_final_norm, v_sinks, v_b_dw, v_conv_ln_g, v_conv_ln_b, v_w_dw)
    upd = _adamw(_pack(small_g), _pack(small_w), _pack(small_m), _pack(small_v))
    small_upd = [_unpack(u_, small_w) for u_ in upd]
    finish(sib_pending[0], rs_started, after=upd[0])

    order = ("norm_ffn1", "w_ffn1_in", "w_ffn1_out", "norm_mix", "w_in", "sinks", "w_dw", "b_dw", "conv_ln_g",
             "conv_ln_b", "w_out", "norm_ffn2", "w_ffn2_in", "w_ffn2_out", "final_norm")
    small_names = ("norm_ffn1", "norm_mix", "norm_ffn2", "final_norm", "sinks", "b_dw", "conv_ln_g", "conv_ln_b", "w_dw")
    big_names = ("w_ffn1_in", "w_ffn1_out", "w_in", "w_out", "w_ffn2_in", "w_ffn2_out")
    grads, deltas, new_m, new_v = {}, {}, {}, {}
    for i, nme in enumerate(small_names):
        grads[nme], deltas[nme], new_m[nme], new_v[nme] = small_g[i], small_upd[0][i], small_upd[1][i], small_upd[2][i]
    for i, nme in enumerate(big_names):
        grads[nme], deltas[nme], new_m[nme], new_v[nme] = bufs[i]
    return (loss, grad_x, *[grads[n] for n in order], *[deltas[n] for n in order],
            *[new_m[n] for n in order], *[new_v[n] for n in order])
```

```python
import functools

import jax
import jax.numpy as jnp
from jax import lax
from jax.experimental import pallas as pl
from jax.experimental.pallas import tpu as pltpu

F32, BF16 = jnp.float32, jnp.bfloat16
EPS = 1e-6
NEG_INF = -1e30
HEAD_DIM = 64
N_HEADS = 8
N_KV = 2
GROUP = N_HEADS // N_KV
WINDOW = 128
ATTN_W = N_HEADS * HEAD_DIM
KV_W = N_KV * HEAD_DIM
CONV_W = 31
HALO = 32
CONV_ROWS = 32
SCALE = 1.0 / 8.0
ADAM_LR, ADAM_B1, ADAM_B2, ADAM_EPS, ADAM_WD, ADAM_STEP = 0.001, 0.9, 0.999, 1e-08, 0.01, 10
TM = 512
TM_FFN_BWD = 256
LANES = 128
VMEM_LIMIT = 52 * 1024 * 1024
MESH = pl.DeviceIdType.MESH
ANY = pl.BlockSpec(memory_space=pl.ANY)
HBM = pl.BlockSpec(memory_space=pltpu.HBM)
SEM = pl.BlockSpec(memory_space=pltpu.SEMAPHORE)
VMEM = pl.BlockSpec(memory_space=pltpu.VMEM)
EFFECT = pltpu.SideEffectType.DATAFLOW_SIDE_EFFECTING
TOKEN = jax.ShapeDtypeStruct((8, LANES), F32)


def _cp(n):
    return pltpu.CompilerParams(dimension_semantics=("arbitrary",) * n, vmem_limit_bytes=VMEM_LIMIT)


def _dot(a, b):
    return jnp.dot(a, b, preferred_element_type=F32)


def _dot_nt(a, b):
    return lax.dot_general(a, b, (((1,), (1,)), ((), ())), preferred_element_type=F32)


def _dot_tn(a, b):
    return lax.dot_general(a, b, (((0,), (0,)), ((), ())), preferred_element_type=F32)


def _sigmoid(v):
    return 1.0 / (1.0 + jnp.exp(-v))


def _place():
    x, y, c = lax.axis_index("x"), lax.axis_index("y"), lax.axis_index("c")
    chips = [(1 - x, y), (x, 1 - y), (1 - x, 1 - y)]
    return x, y, c, chips


def _rcopy(src, dst, send_sems, recv_sems, k, dev):
    return pltpu.make_async_remote_copy(src_ref=src, dst_ref=dst, send_sem=send_sems.at[k],
                                        recv_sem=recv_sems.at[k], device_id=dev, device_id_type=MESH)


def _hbm(a):
    return pltpu.with_memory_space_constraint(a, pltpu.HBM)


def _targets(sibling):
    x, y, c, chips = _place()
    if sibling:
        return 2 * x + y, c, [((x, y, 1 - c), 2 * x + y)]
    return 2 * x + y, c, [((px, py, c), 2 * px + py) for px, py in chips]


def _xchg_start(name, srcs, lands, plan, sibling=False):
    n = len(srcs)
    npeer = 1 if sibling else 3

    def body(*refs):
        src, land = refs[:n], refs[n:2 * n]
        send_sems, recv_sems, token = refs[2 * n], refs[2 * n + 1], refs[-1]
        b, c, peers = _targets(sibling)
        for t in range(n):
            for j, (dev, pb) in enumerate(peers):
                s, d, _ = plan(src[t], land[t], t, b, c, pb)
                _rcopy(s, d, send_sems, recv_sems, npeer * t + j, dev).start()
        token[...] = jnp.zeros_like(token)

    arrs = list(srcs) + list(lands)
    return pl.pallas_call(
        body, name=name,
        out_shape=(pltpu.SemaphoreType.DMA((npeer * n,)), pltpu.SemaphoreType.DMA((npeer * n,)),
                   *[pltpu.HBM(a.shape, a.dtype) for a in arrs], TOKEN),
        in_specs=[HBM] * (2 * n), out_specs=(SEM, SEM, *[HBM] * (2 * n), VMEM),
        input_output_aliases={i: 2 + i for i in range(2 * n)},
        compiler_params=pltpu.CompilerParams(has_side_effects=EFFECT),
    )(*[_hbm(a) for a in arrs])


def _xchg_wait(name, started, n, plan, after, sibling=False):
    send_sems, recv_sems, thru = started[0], started[1], started[2:2 + 2 * n]
    npeer = 1 if sibling else 3

    def body(*refs):
        src, land = refs[:n], refs[n:2 * n]
        send_sems, recv_sems, token = refs[2 * n], refs[2 * n + 1], refs[-1]
        b, c, peers = _targets(sibling)
        for t in range(n):
            for j, (dev, pb) in enumerate(peers):
                s, _, a = plan(src[t], land[t], t, b, c, pb)
                cp = _rcopy(s, a, send_sems, recv_sems, npeer * t + j, dev)
                cp.wait_send()
                cp.wait_recv()
        token[...] = jnp.zeros_like(token)

    out = pl.pallas_call(
        body, name=name,
        out_shape=(*[pltpu.HBM(a.shape, a.dtype) for a in thru], TOKEN),
        in_specs=[HBM] * (2 * n) + [SEM, SEM, ANY], out_specs=(*[HBM] * (2 * n), VMEM),
        input_output_aliases={i: i for i in range(2 * n)},
        compiler_params=pltpu.CompilerParams(has_side_effects=EFFECT),
    )(*thru, send_sems, recv_sems, after)
    return out[:n], out[n:2 * n], out[-1]


def _half(ref_rows, which):
    h = ref_rows // 2
    return pl.ds(which * h, h)


def _gather_plan(src, land, t, b, c, pb):
    if len(src.shape) == 2 and src.shape[0] % 2 == 0:
        hs = _half(src.shape[0], c)
        return src.at[hs], land.at[b, hs], land.at[pb, hs]
    return src, land.at[b], land.at[pb]


def _gather_share(lands):
    n = len(lands)

    def body(*refs):
        land_in, land = refs[:n], refs[n:2 * n]
        send_sems, recv_sems = refs[2 * n:]
        x, y, c, chips = _place()
        sib = (x, y, 1 - c)
        sends = []
        for t in range(n):
            for j, (px, py) in enumerate(chips):
                hs = _half(lands[t].shape[1], c)
                cp = _rcopy(land_in[t].at[2 * px + py, hs], land[t].at[2 * px + py, hs], send_sems, recv_sems, 3 * t + j, sib)
                cp.start()
                sends.append(cp)
        for t in range(n):
            for j, (px, py) in enumerate(chips):
                other = land[t].at[2 * px + py, _half(lands[t].shape[1], 1 - c)]
                _rcopy(other, other, send_sems, recv_sems, 3 * t + j, sib).wait_recv()
        for cp in sends:
            cp.wait_send()

    return pl.pallas_call(
        body, name="gather_share", out_shape=[jax.ShapeDtypeStruct(a.shape, a.dtype) for a in lands],
        in_specs=[ANY] * n, out_specs=[ANY] * n, input_output_aliases={t: t for t in range(n)},
        scratch_shapes=[pltpu.SemaphoreType.DMA((3 * n,)), pltpu.SemaphoreType.DMA((3 * n,))],
    )(*lands)


def _rs_plan(src, land, t, b, c, pb):
    return src.at[pb], land.at[b], land.at[pb]


def _sib_plan(src, land, t, b, c, pb):
    return src.at[:, _half(src.shape[1], 1 - c), :], land, land


def _rows_block(h, cap=512):
    for rb in range(min(h, cap) // 16 * 16, 0, -16):
        if h % rb == 0:
            return rb
    return h


def _sum_halves(cidx, g, s):
    _, R, C = g.shape
    rb = _rows_block(R // 2)
    nr = R // 2 // rb

    def body(c_ref, g_ref, s_ref, o_ref):
        o_ref[...] = (g_ref[...] + s_ref[...]).astype(BF16)

    blk = (None, rb, C)
    return pl.pallas_call(
        body, name="sum_halves", out_shape=jax.ShapeDtypeStruct(s.shape, BF16),
        grid_spec=pltpu.PrefetchScalarGridSpec(
            num_scalar_prefetch=1, grid=(4, nr),
            in_specs=[pl.BlockSpec(blk, lambda p, i, c: (p, c[0] * nr + i, 0)),
                      pl.BlockSpec(blk, lambda p, i, c: (p, i, 0))],
            out_specs=pl.BlockSpec(blk, lambda p, i, c: (p, i, 0))),
        compiler_params=_cp(2),
    )(cidx, g, s)


def _rs_share(qs):
    nt = len(qs)

    def body(*refs):
        q_refs, qsib = refs[:nt], refs[nt:2 * nt]
        send_sems, recv_sems = refs[2 * nt:]
        x, y, c, _ = _place()
        cps = []
        for t in range(nt):
            cp = _rcopy(q_refs[t], qsib[t], send_sems, recv_sems, t, (x, y, 1 - c))
            cp.start()
            cps.append(cp)
        for cp in cps:
            cp.wait()

    return pl.pallas_call(
        body, name="rs_share", in_specs=[ANY] * nt, out_specs=[ANY] * nt,
        out_shape=[jax.ShapeDtypeStruct(q.shape, q.dtype) for q in qs],
        scratch_shapes=[pltpu.SemaphoreType.DMA((nt,)), pltpu.SemaphoreType.DMA((nt,))],
    )(*qs)


def _adam_update(gg, w, m, v):
    m2 = ADAM_B1 * m + (1.0 - ADAM_B1) * gg
    v2 = ADAM_B2 * v + (1.0 - ADAM_B2) * (gg * gg)
    mh = m2 / (1.0 - ADAM_B1 ** ADAM_STEP)
    vh = v2 / (1.0 - ADAM_B2 ** ADAM_STEP)
    return -ADAM_LR * (mh / (jnp.sqrt(vh) + ADAM_EPS) + ADAM_WD * w), m2, v2


def _adamw_layer(cidx, q_own, q_sib, w, m, v, bufs, l):
    L, R, C = w.shape
    h = R // 2
    rb = _rows_block(h, 256)
    nr = h // rb

    def body(c_ref, qo_ref, qs_ref, w_ref, m_ref, v_ref, *rest):
        g_ref, d_ref, mo_ref, vo_ref = rest[-4:]
        own = pl.program_id(0) == c_ref[0]
        gg = jnp.zeros((rb, C), F32)
        for s in range(4):
            gg = gg + jnp.where(own, qo_ref[s], qs_ref[s]).astype(F32)
        g_ref[...] = gg
        d_ref[...], mo_ref[...], vo_ref[...] = _adam_update(gg, w_ref[...], m_ref[...], v_ref[...])

    qspec = pl.BlockSpec((4, rb, C), lambda hh, i, c: (0, i, 0))
    wspec = pl.BlockSpec((None, rb, C), lambda hh, i, c: (l, hh * nr + i, 0))
    return pl.pallas_call(
        body, name="adamw_layer", out_shape=[jax.ShapeDtypeStruct(w.shape, F32)] * 4,
        grid_spec=pltpu.PrefetchScalarGridSpec(
            num_scalar_prefetch=1, grid=(2, nr),
            in_specs=[qspec, qspec, wspec, wspec, wspec] + [ANY] * 4, out_specs=[wspec] * 4),
        input_output_aliases={6 + k: k for k in range(4)},
        compiler_params=_cp(2),
    )(cidx, q_own, q_sib, w, m, v, *bufs)


def _adamw(g, w, m, v):
    L, R, C = g.shape
    rb = _rows_block(R)

    def body(g_ref, w_ref, m_ref, v_ref, d_ref, mo_ref, vo_ref):
        d_ref[...], mo_ref[...], vo_ref[...] = _adam_update(g_ref[...], w_ref[...], m_ref[...], v_ref[...])

    spec = pl.BlockSpec((None, rb, C), lambda l, i: (l, i, 0))
    return pl.pallas_call(
        body, name="adamw", grid=(L, R // rb), in_specs=[spec] * 4, out_specs=[spec] * 3,
        out_shape=[jax.ShapeDtypeStruct(g.shape, F32)] * 3, compiler_params=_cp(2),
    )(g, w, m, v)


def _small_allreduce(p):
    R = p.shape[0]

    def body(p_ref, o_ref, buf_ref, send_sems, recv_sems):
        x, y, c, _ = _place()
        me = 4 * x + 2 * y + c
        flip = lambda a, f: 1 - a if f else a
        buf_ref[me] = p_ref[...]
        peers = [(flip(x, k >> 2 & 1), flip(y, k >> 1 & 1), flip(c, k & 1)) for k in range(1, 8)]
        cps = []
        for k, dev in enumerate(peers):
            cp = _rcopy(p_ref, buf_ref.at[me], send_sems, recv_sems, k, dev)
            cp.start()
            cps.append(cp)
        for k, (px, py, pc) in enumerate(peers):
            slot = buf_ref.at[4 * px + 2 * py + pc]
            _rcopy(slot, slot, send_sems, recv_sems, k, (px, py, pc)).wait_recv()
        for cp in cps:
            cp.wait_send()
        acc = buf_ref[0]
        for s in range(1, 8):
            acc = acc + buf_ref[s]
        o_ref[...] = acc

    return pl.pallas_call(
        body, name="small_allreduce", in_specs=[VMEM], out_specs=VMEM,
        out_shape=jax.ShapeDtypeStruct(p.shape, F32),
        scratch_shapes=[pltpu.VMEM((8, R, LANES), F32), pltpu.SemaphoreType.DMA((7,)), pltpu.SemaphoreType.DMA((7,))],
    )(p)


def _rms(xf, g):
    r = lax.rsqrt(jnp.mean(xf * xf, axis=-1, keepdims=True) + EPS)
    return xf * r, r


def _ffn_fwd(x, g, win, wout):
    T, D = x.shape
    FB = win.shape[2]
    tm = min(TM, T)

    def body(x_ref, g_ref, wg_ref, wu_ref, wo_ref, xo_ref, gu_ref, h_ref, acc_ref):
        j = pl.program_id(1)

        @pl.when(j == 0)
        def _():
            xh, _ = _rms(x_ref[...], None)
            h_ref[...] = (xh * g_ref[...]).astype(BF16)
            acc_ref[...] = jnp.zeros_like(acc_ref)

        h = h_ref[...]
        gate = _dot(h, wg_ref[...])
        up = _dot(h, wu_ref[...])
        gu_ref[0] = gate.astype(BF16)
        gu_ref[1] = up.astype(BF16)
        a = (gate * _sigmoid(gate) * up).astype(BF16)
        acc_ref[...] += _dot(a, wo_ref[...])

        @pl.when(j == 1)
        def _():
            xo_ref[...] = x_ref[...] + 0.5 * acc_ref[...]

    return pl.pallas_call(
        body, name="ffn_fwd", grid=(T // tm, 2),
        in_specs=[pl.BlockSpec((tm, D), lambda i, j: (i, 0)),
                  pl.BlockSpec((1, D), lambda i, j: (0, 0)),
                  pl.BlockSpec((None, D, FB), lambda i, j: (j, 0, 0)),
                  pl.BlockSpec((None, D, FB), lambda i, j: (j + 2, 0, 0)),
                  pl.BlockSpec((FB, D), lambda i, j: (j, 0))],
        out_specs=[pl.BlockSpec((tm, D), lambda i, j: (i, 0)),
                   pl.BlockSpec((2, tm, FB), lambda i, j: (0, i, j))],
        out_shape=[jax.ShapeDtypeStruct((T, D), F32), jax.ShapeDtypeStruct((2, T, 2 * FB), BF16)],
        scratch_shapes=[pltpu.VMEM((tm, D), BF16), pltpu.VMEM((tm, D), F32)],
        compiler_params=_cp(2),
    )(x, g, win, win, wout)


def _mixproj_fwd(x, g, w):
    T, D = x.shape
    W = w.shape[1]
    QKV = ATTN_W + 2 * KV_W
    tm = min(TM, T)

    def body(x_ref, g_ref, w_ref, qkv_ref, u_ref):
        xh, _ = _rms(x_ref[...], None)
        h = (xh * g_ref[...]).astype(BF16)
        qkv_ref[...] = _dot(h, w_ref[:, :QKV]).astype(BF16)
        u_ref[...] = _dot(h, w_ref[:, QKV:])

    return pl.pallas_call(
        body, name="mixproj_fwd", grid=(T // tm,),
        in_specs=[pl.BlockSpec((tm, D), lambda i: (i, 0)), pl.BlockSpec((1, D), lambda i: (0, 0)),
                  pl.BlockSpec((D, W), lambda i: (0, 0))],
        out_specs=[pl.BlockSpec((tm, QKV), lambda i: (i, 0)), pl.BlockSpec((tm, W - QKV), lambda i: (i, 0))],
        out_shape=[jax.ShapeDtypeStruct((T, QKV), BF16), jax.ShapeDtypeStruct((T, W - QKV), F32)],
        compiler_params=_cp(1),
    )(x, g, w)


def _attn_tables(n, g):
    rows, cols = GROUP * WINDOW, 2 * WINDOW
    row = lax.broadcasted_iota(jnp.int32, (rows, cols), 0)
    col = lax.broadcasted_iota(jnp.int32, (rows, cols), 1)
    dist = (row & (WINDOW - 1)) + WINDOW - col
    valid = (dist >= 0) & (dist < WINDOW) & ((n > 0) | (col >= WINDOW))
    hi = row >> 7
    slope = jnp.zeros((rows, cols), F32)
    for i in range(GROUP):
        slope = jnp.where(hi == i, 2.0 ** -(GROUP * g + i + 1), slope)
    bias = -slope * dist.astype(F32)
    return valid, bias


def _sink_col(sink_ref, g):
    hi = lax.broadcasted_iota(jnp.int32, (GROUP * WINDOW, 1), 0) >> 7
    col = jnp.zeros((GROUP * WINDOW, 1), F32)
    for i in range(GROUP):
        col = jnp.where(hi == i, sink_ref[0, GROUP * g + i], col)
    return col


def _stack_heads(ref, g):
    return jnp.concatenate([ref[:, (GROUP * g + i) * HEAD_DIM:(GROUP * g + i + 1) * HEAD_DIM]
                            for i in range(GROUP)], axis=0)


def _band(kvp_ref, kvc_ref, off):
    return jnp.concatenate([kvp_ref[:, off:off + HEAD_DIM], kvc_ref[:, off:off + HEAD_DIM]], axis=0)


def _attn_probs(qs, k, valid, bias, sink):
    s = _dot_nt(qs, k) * SCALE
    s = jnp.where(valid, s + bias, NEG_INF)
    m = jnp.maximum(jnp.max(s, axis=-1, keepdims=True), sink)
    p = jnp.exp(s - m)
    es = jnp.exp(sink - m)
    den = jnp.sum(p, axis=-1, keepdims=True) + es
    return p / den, es / den


def _attn_fwd(sinks, qkv):
    T = qkv.shape[0]
    nb = T // WINDOW

    def body(sink_ref, q_ref, kvp_ref, kvc_ref, o_ref):
        n = pl.program_id(0)
        for g in range(N_KV):
            valid, bias = _attn_tables(n, g)
            qs = _stack_heads(q_ref, g)
            k = _band(kvp_ref, kvc_ref, g * HEAD_DIM)
            v = _band(kvp_ref, kvc_ref, KV_W + g * HEAD_DIM)
            p, _ = _attn_probs(qs, k, valid, bias, _sink_col(sink_ref, g))
            o = _dot(p.astype(BF16), v)
            for i in range(GROUP):
                h = GROUP * g + i
                o_ref[:, h * HEAD_DIM:(h + 1) * HEAD_DIM] = o[i * WINDOW:(i + 1) * WINDOW].astype(BF16)

    return pl.pallas_call(
        body, name="attn_fwd", grid=(nb,),
        in_specs=[pl.BlockSpec(memory_space=pltpu.SMEM),
                  pl.BlockSpec((WINDOW, ATTN_W), lambda n: (n, 0)),
                  pl.BlockSpec((WINDOW, 2 * KV_W), lambda n: (jnp.maximum(n - 1, 0), 2)),
                  pl.BlockSpec((WINDOW, 2 * KV_W), lambda n: (n, 2))],
        out_specs=pl.BlockSpec((WINDOW, ATTN_W), lambda n: (n, 0)),
        out_shape=jax.ShapeDtypeStruct((T, ATTN_W), BF16),
        compiler_params=_cp(1),
    )(sinks, qkv, qkv, qkv)


def _shift_copies(src_ref, dst_ref, n):
    for b in range(1, 8):
        dst_ref[b - 1] = src_ref[b:b + n, :]


def _tap(src_ref, sh_ref, s, c0):
    a, b = divmod(s, 8)
    start = pl.multiple_of(c0 + 8 * a, 8)
    if b == 0:
        return src_ref[pl.ds(start, CONV_ROWS), :]
    return sh_ref[b - 1, pl.ds(start, CONV_ROWS), :]


def _glu_rows(u, ch):
    return u[:, :ch] * _sigmoid(u[:, ch:])


def _fill_z(zs_ref, zsh_ref, uc_ref, up_ref, i, ch, n):
    zs_ref[0:HALO] = jnp.where(i > 0, _glu_rows(up_ref[...], ch), 0.0)
    zs_ref[HALO:] = _glu_rows(uc_ref[...], ch)
    _shift_copies(zs_ref, zsh_ref, n - 8)


def _conv_fwd(u, w, b, lg, lb):
    T = u.shape[0]
    CH = u.shape[1] // 2
    tm = min(TM, T)
    n = tm + HALO
    hb = tm // HALO

    def body(uc_ref, up_ref, w_ref, b_ref, lg_ref, lb_ref, conv_ref, ypre_ref, zs_ref, zsh_ref):
        i = pl.program_id(0)
        _fill_z(zs_ref, zsh_ref, uc_ref, up_ref, i, CH, n)
        bias = b_ref[...]

        def chunk(ci, carry):
            c0 = pl.multiple_of(ci * CONV_ROWS, CONV_ROWS)
            acc = jnp.broadcast_to(bias, (CONV_ROWS, CH))
            for k in range(CONV_W):
                acc = acc + w_ref[k:k + 1, :] * _tap(zs_ref, zsh_ref, HALO - (CONV_W - 1) + k, c0)
            ypre_ref[pl.ds(c0, CONV_ROWS), :] = acc
            return carry

        lax.fori_loop(0, tm // CONV_ROWS, chunk, 0)
        y = ypre_ref[...]
        mu = jnp.mean(y, axis=-1, keepdims=True)
        d = y - mu
        var = jnp.mean(d * d, axis=-1, keepdims=True)
        o = d * lax.rsqrt(var + EPS) * lg_ref[...] + lb_ref[...]
        conv_ref[...] = (o * _sigmoid(o)).astype(BF16)

    vec = pl.BlockSpec((1, CH), lambda i: (0, 0))
    return pl.pallas_call(
        body, name="conv_fwd", grid=(T // tm,),
        in_specs=[pl.BlockSpec((tm, 2 * CH), lambda i: (i, 0)),
                  pl.BlockSpec((HALO, 2 * CH), lambda i: (jnp.maximum(i * hb - 1, 0), 0)),
                  pl.BlockSpec((CONV_W, CH), lambda i: (0, 0)), vec, vec, vec],
        out_specs=[pl.BlockSpec((tm, CH), lambda i: (i, 0)), pl.BlockSpec((tm, CH), lambda i: (i, 0))],
        out_shape=[jax.ShapeDtypeStruct((T, CH), BF16), jax.ShapeDtypeStruct((T, CH), F32)],
        scratch_shapes=[pltpu.VMEM((n, CH), F32), pltpu.VMEM((7, n - 8, CH), F32)],
        compiler_params=_cp(1),
    )(u, u, w, b, lg, lb)


def _mixout_fwd(x, attn, conv, wo):
    T, D = x.shape
    tm = min(TM, T)
    A = attn.shape[1]

    def body(x_ref, a_ref, c_ref, w_ref, xo_ref):
        xo_ref[...] = x_ref[...] + _dot(a_ref[...], w_ref[:A, :]) + _dot(c_ref[...], w_ref[A:, :])

    return pl.pallas_call(
        body, name="mixout_fwd", grid=(T // tm,),
        in_specs=[pl.BlockSpec((tm, D), lambda i: (i, 0)), pl.BlockSpec((tm, A), lambda i: (i, 0)),
                  pl.BlockSpec((tm, conv.shape[1]), lambda i: (i, 0)), pl.BlockSpec(wo.shape, lambda i: (0, 0))],
        out_specs=pl.BlockSpec((tm, D), lambda i: (i, 0)),
        out_shape=jax.ShapeDtypeStruct((T, D), F32),
        compiler_params=_cp(1),
    )(x, attn, conv, wo)


def _rms_bwd_rows(dh, xf, g):
    xh, r = _rms(xf, None)
    dxn = dh * g
    dx = r * (dxn - xh * jnp.mean(dxn * xh, axis=-1, keepdims=True))
    return dx, jnp.sum(dh * xh, axis=0, keepdims=True), xh * g


def _loss_head(x, g, tgt):
    T, D = x.shape
    tm = min(TM, T)

    def body(x_ref, g_ref, t_ref, loss_ref, dx_ref, dg_ref):
        @pl.when(pl.program_id(0) == 0)
        def _():
            loss_ref[...] = jnp.zeros_like(loss_ref)
            dg_ref[...] = jnp.zeros_like(dg_ref)

        xf = x_ref[...]
        g = g_ref[...]
        xh, _ = _rms(xf, None)
        e = xh * g - t_ref[...]
        loss_ref[...] += 0.5 * jnp.sum(jnp.mean(e * e, axis=-1, keepdims=True), axis=0, keepdims=True)
        dx, dg, _ = _rms_bwd_rows(e * (1.0 / D), xf, g)
        dx_ref[...] = dx
        dg_ref[...] += dg

    return pl.pallas_call(
        body, name="loss_head", grid=(T // tm,),
        in_specs=[pl.BlockSpec((tm, D), lambda i: (i, 0)), pl.BlockSpec((1, D), lambda i: (0, 0)),
                  pl.BlockSpec((tm, D), lambda i: (i, 0))],
        out_specs=[pl.BlockSpec((1, 1), lambda i: (0, 0)), pl.BlockSpec((tm, D), lambda i: (i, 0)),
                   pl.BlockSpec((1, D), lambda i: (0, 0))],
        out_shape=[jax.ShapeDtypeStruct((1, 1), F32), jax.ShapeDtypeStruct((T, D), F32),
                   jax.ShapeDtypeStruct((1, D), F32)],
        compiler_params=_cp(1),
    )(x, g, tgt)


def _lane_chunks(n):
    lo = (n // LANES + 1) // 2 * LANES
    return ((0, lo), (lo, n - lo))


def _ffn_bwd(dxo, x, g, gu, win, wout, dep):
    T, D = x.shape
    FB = win.shape[2]
    tm = min(TM_FFN_BWD, T)

    def body(dxo_ref, x_ref, g_ref, gu_ref, win_hbm, wout_hbm, dep_ref,
             dxi_ref, dg_ref, hb_ref, dgu_ref, a_ref, dyb_ref, win_v, wout_v, sems):
        @pl.when(pl.program_id(0) == 0)
        def _():
            loads = [pltpu.make_async_copy(win_hbm.at[k], win_v.at[k], sems.at[k]) for k in range(4)]
            loads += [pltpu.make_async_copy(wout_hbm.at[pl.ds(k * FB, FB)], wout_v.at[pl.ds(k * FB, FB)], sems.at[4 + k])
                      for k in range(2)]
            for cp in loads:
                cp.start()
            for cp in loads:
                cp.wait()
            dg_ref[...] = jnp.zeros_like(dg_ref)

        dyb = (0.5 * dxo_ref[...]).astype(BF16)
        dyb_ref[...] = dyb
        dh = jnp.zeros((tm, D), F32)
        for blk in range(2):
            for lo, sz in _lane_chunks(FB):
                cols = pl.ds(blk * FB + lo, sz)
                da = _dot_nt(dyb, wout_v[cols, :])
                gate = gu_ref[0, :, cols].astype(F32)
                up = gu_ref[1, :, cols].astype(F32)
                sg = pl.reciprocal(1.0 + jnp.exp(-gate), approx=True)
                s = gate * sg
                a_ref[:, cols] = (s * up).astype(BF16)
                dgate = (da * up * (sg * (1.0 + gate * (1.0 - sg)))).astype(BF16)
                dup = (da * s).astype(BF16)
                dgu_ref[0, :, cols] = dgate
                dgu_ref[1, :, cols] = dup
                dh = dh + _dot_nt(dgate, win_v[blk, :, pl.ds(lo, sz)]) + _dot_nt(dup, win_v[2 + blk, :, pl.ds(lo, sz)])
        dx, dg, h = _rms_bwd_rows(dh, x_ref[...], g_ref[...])
        dxi_ref[...] = dxo_ref[...] + dx
        dg_ref[...] += dg
        hb_ref[...] = h.astype(BF16)

    row = pl.BlockSpec((tm, D), lambda i: (i, 0))
    act = pl.BlockSpec((2, tm, 2 * FB), lambda i: (0, i, 0))
    return pl.pallas_call(
        body, name="ffn_bwd", grid=(T // tm,),
        in_specs=[row, row, pl.BlockSpec((1, D), lambda i: (0, 0)), act, ANY, ANY, ANY],
        out_specs=[row, pl.BlockSpec((1, D), lambda i: (0, 0)), row, act,
                   pl.BlockSpec((tm, 2 * FB), lambda i: (i, 0)), row],
        out_shape=[jax.ShapeDtypeStruct((T, D), F32), jax.ShapeDtypeStruct((1, D), F32),
                   jax.ShapeDtypeStruct((T, D), BF16), jax.ShapeDtypeStruct((2, T, 2 * FB), BF16),
                   jax.ShapeDtypeStruct((T, 2 * FB), BF16), jax.ShapeDtypeStruct((T, D), BF16)],
        scratch_shapes=[pltpu.VMEM(win.shape, BF16), pltpu.VMEM(wout.shape, BF16), pltpu.SemaphoreType.DMA((6,))],
        compiler_params=_cp(1),
    )(dxo, x, g, gu, win, wout, dep)


def _rms_matmul_bwd(name, dxo, x, g, dzs, ws, dz_specs, w_specs, nk):
    T, D = x.shape
    tm = min(TM, T)
    npair = len(dzs)

    def body(*refs):
        dxo_ref, x_ref, g_ref = refs[:3]
        dz_refs, w_refs = refs[3:3 + npair], refs[3 + npair:3 + 2 * npair]
        dxi_ref, dg_ref, hb_ref, acc_ref = refs[3 + 2 * npair:]
        i, k = pl.program_id(0), pl.program_id(1)

        @pl.when(k == 0)
        def _():
            acc_ref[...] = jnp.zeros_like(acc_ref)

        @pl.when((i == 0) & (k == 0))
        def _():
            dg_ref[...] = jnp.zeros_like(dg_ref)

        for p in range(npair):
            acc_ref[...] += _dot_nt(dz_refs[p][...], w_refs[p][...])

        @pl.when(k == nk - 1)
        def _():
            dx, dg, h = _rms_bwd_rows(acc_ref[...], x_ref[...], g_ref[...])
            dxi_ref[...] = dxo_ref[...] + dx
            dg_ref[...] += dg
            hb_ref[...] = h.astype(BF16)

    row = pl.BlockSpec((tm, D), lambda i, k: (i, 0))
    return pl.pallas_call(
        body, name=name, grid=(T // tm, nk),
        in_specs=[row, row, pl.BlockSpec((1, D), lambda i, k: (0, 0))] + list(dz_specs) + list(w_specs),
        out_specs=[row, pl.BlockSpec((1, D), lambda i, k: (0, 0)), row],
        out_shape=[jax.ShapeDtypeStruct((T, D), F32), jax.ShapeDtypeStruct((1, D), F32),
                   jax.ShapeDtypeStruct((T, D), BF16)],
        scratch_shapes=[pltpu.VMEM((tm, D), F32)],
        compiler_params=_cp(2),
    )(dxo, x, g, *dzs, *ws)


def _mix_rms_bwd(dxo, x, g, dzs, ws):
    tm = min(TM, x.shape[0])
    return _rms_matmul_bwd(
        "mix_rms_bwd", dxo, x, g, dzs, ws,
        [pl.BlockSpec((tm, dz.shape[1]), lambda i, k: (i, 0)) for dz in dzs],
        [pl.BlockSpec(w.shape, lambda i, k: (0, 0)) for w in ws], 1)


def _wgrad(name, a, b, a_spec, b_spec, out_shape, out_spec, nblk):
    T = a.shape[0]
    tk = min(TM, T)

    def body(a_ref, b_ref, o_ref):
        @pl.when(pl.program_id(1) == 0)
        def _():
            o_ref[...] = jnp.zeros_like(o_ref)

        o_ref[...] += _dot_tn(a_ref[...], b_ref[...]).reshape(o_ref.shape)

    return pl.pallas_call(
        body, name=name, grid=(nblk, T // tk), in_specs=[a_spec, b_spec], out_specs=out_spec,
        out_shape=jax.ShapeDtypeStruct(out_shape, F32), compiler_params=_cp(2),
    )(a, b)


def _wgrad_ffn_in(hb, dgu):
    T, D = hb.shape
    FB = dgu.shape[2] // 2
    tk = min(TM, T)
    return _wgrad("wgrad_ffn_in", hb, dgu,
                  pl.BlockSpec((tk, D), lambda b, k: (k, 0)),
                  pl.BlockSpec((None, tk, FB), lambda b, k: (b // 2, k, b % 2)),
                  (4, D, FB), pl.BlockSpec((None, D, FB), lambda b, k: (b, 0, 0)), 4)


def _wgrad_ffn_out(a, dyb):
    T, D = dyb.shape
    FB = a.shape[1] // 2
    tk = min(TM, T)
    return _wgrad("wgrad_ffn_out", a, dyb,
                  pl.BlockSpec((tk, FB), lambda b, k: (k, b)),
                  pl.BlockSpec((tk, D), lambda b, k: (k, 0)),
                  (4, FB // 2, D), pl.BlockSpec((2, FB // 2, D), lambda b, k: (b, 0, 0)), 2)


def _wgrad_plain(a, b):
    T, M = a.shape
    N = b.shape[1]
    tk = min(TM, T)
    return _wgrad("wgrad_plain", a, b, pl.BlockSpec((tk, M), lambda i, k: (k, 0)),
                  pl.BlockSpec((tk, N), lambda i, k: (k, 0)), (M, N),
                  pl.BlockSpec((M, N), lambda i, k: (0, 0)), 1)


def _mixout_bwd(dxo, wo):
    T, D = dxo.shape
    tm = min(TM, T)
    A = ATTN_W
    C = wo.shape[0] - A

    def body(dxo_ref, w_ref, dyb_ref, da_ref, dc_ref):
        dyb = dxo_ref[...].astype(BF16)
        dyb_ref[...] = dyb
        da_ref[...] = _dot_nt(dyb, w_ref[:A, :]).astype(BF16)
        dc_ref[...] = _dot_nt(dyb, w_ref[A:, :])

    return pl.pallas_call(
        body, name="mixout_bwd", grid=(T // tm,),
        in_specs=[pl.BlockSpec((tm, D), lambda i: (i, 0)), pl.BlockSpec(wo.shape, lambda i: (0, 0))],
        out_specs=[pl.BlockSpec((tm, D), lambda i: (i, 0)), pl.BlockSpec((tm, A), lambda i: (i, 0)),
                   pl.BlockSpec((tm, C), lambda i: (i, 0))],
        out_shape=[jax.ShapeDtypeStruct((T, D), BF16), jax.ShapeDtypeStruct((T, A), BF16),
                   jax.ShapeDtypeStruct((T, C), F32)],
        compiler_params=_cp(1),
    )(dxo, wo)


def _conv_bwd(dconv, ypre, u, w, lg, lb):
    T, CH = dconv.shape
    tm = min(TM, T)
    n = tm + HALO
    hb = tm // HALO
    nt = T // tm
    nchunk = tm // CONV_ROWS

    def body(dc_ref, dcn_ref, yp_ref, ypn_ref, uc_ref, up_ref, w_ref, lg_ref, lb_ref,
             du_ref, dw_ref, dvec_ref, zs_ref, zsh_ref, dy_ref, dysh_ref, dz_ref):
        i = pl.program_id(0)

        @pl.when(i == 0)
        def _():
            dw_ref[...] = jnp.zeros_like(dw_ref)
            dvec_ref[...] = jnp.zeros_like(dvec_ref)

        g, bb = lg_ref[...], lb_ref[...]

        def ln_bwd(dc, yp):
            mu = jnp.mean(yp, axis=-1, keepdims=True)
            d = yp - mu
            rs = lax.rsqrt(jnp.mean(d * d, axis=-1, keepdims=True) + EPS)
            yn = d * rs
            o = yn * g + bb
            sg = _sigmoid(o)
            do = dc * (sg * (1.0 + o * (1.0 - sg)))
            dyn = do * g
            dyp = rs * (dyn - jnp.mean(dyn, axis=-1, keepdims=True)
                        - yn * jnp.mean(dyn * yn, axis=-1, keepdims=True))
            return dyp, do, yn

        dyp, do, yn = ln_bwd(dc_ref[...], yp_ref[...])
        dvec_ref[0:1, :] += jnp.sum(dyp, axis=0, keepdims=True)
        dvec_ref[1:2, :] += jnp.sum(do * yn, axis=0, keepdims=True)
        dvec_ref[2:3, :] += jnp.sum(do, axis=0, keepdims=True)
        dy_ref[0:tm] = dyp
        dyh, _, _ = ln_bwd(dcn_ref[...], ypn_ref[...])
        dy_ref[tm:] = jnp.where(i < nt - 1, dyh, 0.0)
        _shift_copies(dy_ref, dysh_ref, n - 8)
        _fill_z(zs_ref, zsh_ref, uc_ref, up_ref, i, CH, n)

        def chunk(ci, carry):
            c0 = pl.multiple_of(ci * CONV_ROWS, CONV_ROWS)
            acc = jnp.zeros((CONV_ROWS, CH), F32)
            for k in range(CONV_W):
                acc = acc + w_ref[k:k + 1, :] * _tap(dy_ref, dysh_ref, CONV_W - 1 - k, c0)
            dz_ref[pl.ds(c0, CONV_ROWS), :] = acc
            return carry

        lax.fori_loop(0, nchunk, chunk, 0)

        for k in range(CONV_W):
            def red(ci, acc, k=k):
                c0 = pl.multiple_of(ci * CONV_ROWS, CONV_ROWS)
                prod = dy_ref[pl.ds(c0, CONV_ROWS), :] * _tap(zs_ref, zsh_ref, HALO - (CONV_W - 1) + k, c0)
                return acc + jnp.sum(prod.reshape(CONV_ROWS // 8, 8, CH), axis=0)

            acc = lax.fori_loop(0, nchunk, red, jnp.zeros((8, CH), F32))
            dw_ref[k:k + 1, :] += jnp.sum(acc, axis=0, keepdims=True)

        uc = uc_ref[...]
        a = uc[:, :CH]
        sg = _sigmoid(uc[:, CH:])
        dz = dz_ref[...]
        du_ref[:, :CH] = (dz * sg).astype(BF16)
        du_ref[:, CH:] = (dz * a * sg * (1.0 - sg)).astype(BF16)

    cur = lambda c: pl.BlockSpec((tm, c), lambda i: (i, 0))
    nxt = lambda c: pl.BlockSpec((HALO, c), lambda i: (jnp.minimum((i + 1) * hb, T // HALO - 1), 0))
    vec = pl.BlockSpec((1, CH), lambda i: (0, 0))
    return pl.pallas_call(
        body, name="conv_bwd", grid=(nt,),
        in_specs=[cur(CH), nxt(CH), cur(CH), nxt(CH), cur(2 * CH),
                  pl.BlockSpec((HALO, 2 * CH), lambda i: (jnp.maximum(i * hb - 1, 0), 0)),
                  pl.BlockSpec((CONV_W, CH), lambda i: (0, 0)), vec, vec],
        out_specs=[pl.BlockSpec((tm, 2 * CH), lambda i: (i, 0)), pl.BlockSpec((32, CH), lambda i: (0, 0)),
                   pl.BlockSpec((8, CH), lambda i: (0, 0))],
        out_shape=[jax.ShapeDtypeStruct((T, 2 * CH), BF16), jax.ShapeDtypeStruct((32, CH), F32),
                   jax.ShapeDtypeStruct((8, CH), F32)],
        scratch_shapes=[pltpu.VMEM((n, CH), F32), pltpu.VMEM((7, n - 8, CH), F32),
                        pltpu.VMEM((n, CH), F32), pltpu.VMEM((7, n - 8, CH), F32), pltpu.VMEM((tm, CH), F32)],
        compiler_params=_cp(1),
    )(dconv, dconv, ypre, ypre, u, u, w, lg, lb)


def _attn_bwd(sinks, qkv, dattn):
    T = qkv.shape[0]
    nb = T // WINDOW

    def body(sink_ref, q_ref, kvp_ref, kvc_ref, do_ref, dq_ref, dkv_ref, dsk_ref, carry_ref):
        n = pl.program_id(0)

        @pl.when(n == 0)
        def _():
            dsk_ref[...] = jnp.zeros_like(dsk_ref)
            carry_ref[...] = jnp.zeros_like(carry_ref)

        @pl.when(n < nb)
        def _():
            for g in range(N_KV):
                valid, bias = _attn_tables(n, g)
                qs = _stack_heads(q_ref, g)
                dos = _stack_heads(do_ref, g)
                k = _band(kvp_ref, kvc_ref, g * HEAD_DIM)
                v = _band(kvp_ref, kvc_ref, KV_W + g * HEAD_DIM)
                p, ps = _attn_probs(qs, k, valid, bias, _sink_col(sink_ref, g))
                dp = _dot_nt(dos, v)
                delta = jnp.sum(p * dp, axis=-1, keepdims=True)
                dsb = (p * (dp - delta)).astype(BF16)
                dsink = -ps * delta
                dqs = _dot(dsb, k) * SCALE
                dk = _dot_tn(dsb, qs) * SCALE
                dv = _dot_tn(p.astype(BF16), dos)
                for i in range(GROUP):
                    h = GROUP * g + i
                    dq_ref[:, h * HEAD_DIM:(h + 1) * HEAD_DIM] = dqs[i * WINDOW:(i + 1) * WINDOW].astype(BF16)
                    dsk_ref[h:h + 1, :] += jnp.sum(dsink[i * WINDOW:(i + 1) * WINDOW], axis=0, keepdims=True)
                for off, d in ((g * HEAD_DIM, dk), (KV_W + g * HEAD_DIM, dv)):
                    dkv_ref[:, off:off + HEAD_DIM] = (carry_ref[:, off:off + HEAD_DIM] + d[:WINDOW]).astype(BF16)
                    carry_ref[:, off:off + HEAD_DIM] = d[WINDOW:]

        @pl.when(n == nb)
        def _():
            dkv_ref[...] = carry_ref[...].astype(BF16)

    last = nb - 1
    return pl.pallas_call(
        body, name="attn_bwd", grid=(nb + 1,),
        in_specs=[pl.BlockSpec(memory_space=pltpu.SMEM),
                  pl.BlockSpec((WINDOW, ATTN_W), lambda n: (jnp.minimum(n, last), 0)),
                  pl.BlockSpec((WINDOW, 2 * KV_W), lambda n: (jnp.clip(n - 1, 0, last), 2)),
                  pl.BlockSpec((WINDOW, 2 * KV_W), lambda n: (jnp.minimum(n, last), 2)),
                  pl.BlockSpec((WINDOW, ATTN_W), lambda n: (jnp.minimum(n, last), 0))],
        out_specs=[pl.BlockSpec((WINDOW, ATTN_W), lambda n: (jnp.minimum(n, last), 0)),
                   pl.BlockSpec((WINDOW, 2 * KV_W), lambda n: (jnp.maximum(n - 1, 0), 0)),
                   pl.BlockSpec((8, LANES), lambda n: (0, 0))],
        out_shape=[jax.ShapeDtypeStruct((T, ATTN_W), BF16), jax.ShapeDtypeStruct((T, 2 * KV_W), BF16),
                   jax.ShapeDtypeStruct((8, LANES), F32)],
        scratch_shapes=[pltpu.VMEM((WINDOW, 2 * KV_W), F32)],
        compiler_params=_cp(1),
    )(sinks, qkv, qkv, qkv, dattn)


def _pack(arrs):
    flat = jnp.concatenate([a.reshape(-1) for a in arrs])
    pad = -flat.shape[0] % (8 * LANES)
    return jnp.pad(flat, (0, pad)).reshape(1, -1, LANES)


def _unpack(packed, like):
    flat = packed.reshape(-1)
    out, off = [], 0
    for a in like:
        out.append(flat[off:off + a.size].reshape(a.shape))
        off += a.size
    return out


def kernel(x, norm_ffn1, w_ffn1_in, w_ffn1_out, norm_mix, w_in, sinks, w_dw, b_dw, conv_ln_g, conv_ln_b, w_out, norm_ffn2, w_ffn2_in, w_ffn2_out, final_norm, loss_target, m_norm_ffn1, m_w_ffn1_in, m_w_ffn1_out, m_norm_mix, m_w_in, m_sinks, m_w_dw, m_b_dw, m_conv_ln_g, m_conv_ln_b, m_w_out, m_norm_ffn2, m_w_ffn2_in, m_w_ffn2_out, m_final_norm, v_norm_ffn1, v_w_ffn1_in, v_w_ffn1_out, v_norm_mix, v_w_in, v_sinks, v_w_dw, v_b_dw, v_conv_ln_g, v_conv_ln_b, v_w_out, v_norm_ffn2, v_w_ffn2_in, v_w_ffn2_out, v_final_norm):
    L, D = norm_ffn1.shape
    T = x.shape[1]
    FB = w_ffn1_in.shape[2]
    CH = b_dw.shape[1]
    QKV = ATTN_W + 2 * KV_W
    xs = x.reshape(T, D)
    tgt = loss_target.reshape(T, D)
    cx, cy, cc = lax.axis_index("x"), lax.axis_index("y"), lax.axis_index("c")
    chip = 2 * cx + cy
    cidx = cc.reshape(1).astype(jnp.int32)
    big_w = (w_ffn1_in, w_ffn1_out, w_in, w_out, w_ffn2_in, w_ffn2_out)
    big_m = (m_w_ffn1_in, m_w_ffn1_out, m_w_in, m_w_out, m_w_ffn2_in, m_w_ffn2_out)
    big_v = (v_w_ffn1_in, v_w_ffn1_out, v_w_in, v_w_out, v_w_ffn2_in, v_w_ffn2_out)
    NW = len(big_w) + 1

    def shards(l, tok):
        t16 = tok[0, 0].astype(BF16)
        return [w[l].astype(BF16) + t16 for w in big_w] + [w_dw[l] + tok[0, 0]]

    def own_slot(a):
        return lax.dynamic_update_index_in_dim(lax.empty((4,) + a.shape, a.dtype), a, chip, 0)

    def gather_start(l, tok):
        srcs = shards(l, tok)
        return _xchg_start("gather_start", srcs, [own_slot(s_) for s_ in srcs], _gather_plan)

    row = lambda a, l: a[l].reshape(1, -1)

    saved, W = [], []
    started = gather_start(0, jnp.zeros((8, LANES), F32))
    after = xs
    for l in range(L):
        _, lands, tok = _xchg_wait("gather_wait", started, NW, _gather_plan, after)
        if l + 1 < L:
            started = gather_start(l + 1, tok)
            tok = started[-1]
        g1i, g1o, gi, go, g2i, g2o = _gather_share(lands[:-1])
        w = dict(f1i=g1i, f1o=g1o.reshape(2 * FB, D), f2i=g2i, f2o=g2o.reshape(2 * FB, D),
                 wi=jnp.transpose(gi, (1, 0, 2)).reshape(D, -1), wo=go.reshape(-1, D),
                 wdw=jnp.transpose(lands[-1], (1, 0, 2)).reshape(CONV_W, CH))
        W.append(w)
        x0 = xs
        x1, gu1 = _ffn_fwd(x0, row(norm_ffn1, l) + tok[0, 0], w["f1i"], w["f1o"])
        qkv, u = _mixproj_fwd(x1, row(norm_mix, l), w["wi"])
        attn = _attn_fwd(row(sinks, l), qkv)
        conv, ypre = _conv_fwd(u, w["wdw"], row(b_dw, l), row(conv_ln_g, l), row(conv_ln_b, l))
        x2 = _mixout_fwd(x1, attn, conv, w["wo"])
        xs, gu2 = _ffn_fwd(x2, row(norm_ffn2, l), w["f2i"], w["f2o"])
        saved.append((x0, gu1, x1, qkv, u, attn, conv, ypre, x2, gu2))
        after = xs

    loss_part, dx, d_final = _loss_head(xs, final_norm.reshape(1, D), tgt)
    loss = lax.psum(loss_part[0, 0], ("x", "y", "c"))

    bufs = [[lax.empty(w_.shape, F32) for _ in range(4)] for w_ in big_w]
    d_n1, d_nm, d_n2 = [None] * L, [None] * L, [None] * L
    d_sk, d_bdw, d_lg, d_lb, d_wdw = [None] * L, [None] * L, [None] * L, [None] * L, [None] * L
    NB = len(big_w)

    def reduce_start(sib_started, after):
        gs, sibs, _ = _xchg_wait("sib_wait", sib_started, NB, _sib_plan, after, sibling=True)
        parts = [_sum_halves(cidx, g, s_) for g, s_ in zip(gs, sibs)]
        lands = [own_slot(lax.dynamic_index_in_dim(p, chip, 0, keepdims=False)) for p in parts]
        return _xchg_start("rs_start", parts, lands, _rs_plan)

    def finish(l, rs_started, after):
        _, qs, _ = _xchg_wait("rs_wait", rs_started, NB, _rs_plan, after)
        q_sib = _rs_share(qs)
        for t in range(NB):
            bufs[t] = _adamw_layer(cidx, qs[t], q_sib[t], big_w[t], big_m[t], big_v[t], bufs[t], l)

    sib_pending = rs_pending = None
    tok = jnp.zeros((8, LANES), F32)
    for l in reversed(range(L)):
        w = W[l]
        x0, gu1, x1, qkv, u, attn, conv, ypre, x2, gu2 = saved[l]
        dx, d_n2[l], hb, dgu, a, dyb = _ffn_bwd(dx, x2, row(norm_ffn2, l), gu2, w["f2i"], w["f2o"], tok)
        g_f2i, g_f2o = _wgrad_ffn_in(hb, dgu), _wgrad_ffn_out(a, dyb)
        lg_row = row(conv_ln_g, l)
        if sib_pending is not None:
            rs_started = reduce_start(sib_pending[1], after=g_f2o)
            if rs_pending is not None:
                finish(*rs_pending, after=rs_started[-1])
            rs_pending = (sib_pending[0], rs_started)
            lg_row = lg_row + rs_started[-1][0, 0]
        dyb, dattn, dconv = _mixout_bwd(dx, w["wo"])
        g_wo = jnp.concatenate([_wgrad_plain(attn, dyb), _wgrad_plain(conv, dyb)], axis=0).reshape(4, -1, D)
        du, dwdw, dvec = _conv_bwd(dconv, ypre, u, w["wdw"], lg_row, row(conv_ln_b, l))
        d_wdw[l], d_bdw[l], d_lg[l], d_lb[l] = dwdw[:CONV_W], dvec[0], dvec[1], dvec[2]
        dq, dkv, dsk = _attn_bwd(row(sinks, l), qkv, dattn)
        d_sk[l] = dsk[:, 0]
        wi = w["wi"]
        dx, d_nm[l], hb = _mix_rms_bwd(dx, x1, row(norm_mix, l), [dq, dkv, du],
                                       [wi[:, :ATTN_W], wi[:, ATTN_W:QKV], wi[:, QKV:]])
        gwi = jnp.concatenate([_wgrad_plain(hb, dq), _wgrad_plain(hb, dkv), _wgrad_plain(hb, du)], axis=1)
        g_wi = jnp.transpose(gwi.reshape(D, 4, -1), (1, 0, 2))
        dx, d_n1[l], hb, dgu, a, dyb = _ffn_bwd(dx, x0, row(norm_ffn1, l), gu1, w["f1i"], w["f1o"], tok)
        g_f1i, g_f1o = _wgrad_ffn_in(hb, dgu), _wgrad_ffn_out(a, dyb)
        gs = [g_f1i, g_f1o, g_wi, g_wo, g_f2i, g_f2o]
        sib_started = _xchg_start("sib_start", gs, [lax.empty((4, g.shape[1] // 2, g.shape[2]), F32) for g in gs],
                                  _sib_plan, sibling=True)
        tok = sib_started[-1]
        sib_pending = (l, sib_started)
    grad_x = dx.reshape(x.shape)
    after = tok
    if rs_pending is not None:
        finish(*rs_pending, after=tok)
        after = bufs[0][0]
    rs_started = reduce_start(sib_pending[1], after=after)

    small_g = [jnp.concatenate(d, axis=0) for d in (d_n1, d_nm, d_n2)] + [d_final, jnp.stack(d_sk)] + \
              [jnp.stack(d) for d in (d_bdw, d_lg, d_lb, d_wdw)]
    small_sum = _unpack(_small_allreduce(_pack(small_g)[0] + rs_started[-1][0, 0]), small_g)
    g_wdw = lax.dynamic_slice_in_dim(small_sum[8], chip * w_dw.shape[2], w_dw.shape[2], axis=2)
    small_g = [small_sum[0], small_sum[1], small_sum[2], small_sum[3].reshape(D), small_sum[4],
               small_sum[5], small_sum[6], small_sum[7], g_wdw]
    small_w = (norm_ffn1, norm_mix, norm_ffn2, final_norm, sinks, b_dw, conv_ln_g, conv_ln_b, w_dw)
    small_m = (m_norm_ffn1, m_norm_mix, m_norm_ffn2, m_final_norm, m_sinks, m_b_dw, m_conv_ln_g, m_conv_ln_b, m_w_dw)
    small_v = (v_norm_ffn1, v_norm_mix, v_norm_ffn2, v_final_norm, v_sinks, v_b_dw, v_conv_ln_g, v_conv_ln_b, v_w_dw)
    upd = _adamw(_pack(small_g), _pack(small_w), _pack(small_m), _pack(small_v))
    small_upd = [_unpack(u_, small_w) for u_ in upd]
    finish(sib_pending[0], rs_started, after=upd[0])

    order = ("norm_ffn1", "w_ffn1_in", "w_ffn1_out", "norm_mix", "w_in", "sinks", "w_dw", "b_dw", "conv_ln_g",
             "conv_ln_b", "w_out", "norm_ffn2", "w_ffn2_in", "w_ffn2_out", "final_norm")
    small_names = ("norm_ffn1", "norm_mix", "norm_ffn2", "final_norm", "sinks", "b_dw", "conv_ln_g", "conv_ln_b", "w_dw")
    big_names = ("w_ffn1_in", "w_ffn1_out", "w_in", "w_out", "w_ffn2_in", "w_ffn2_out")
    grads, deltas, new_m, new_v = {}, {}, {}, {}
    for i, nme in enumerate(small_names):
        grads[nme], deltas[nme], new_m[nme], new_v[nme] = small_g[i], small_upd[0][i], small_upd[1][i], small_upd[2][i]
    for i, nme in enumerate(big_names):
        grads[nme], deltas[nme], new_m[nme], new_v[nme] = bufs[i]
    return (loss, grad_x, *[grads[n] for n in order], *[deltas[n] for n in order],
            *[new_m[n] for n in order], *[new_v[n] for n in order])
```

```python
import functools

import jax
import jax.numpy as jnp
from jax import lax
from jax.experimental import pallas as pl
from jax.experimental.pallas import tpu as pltpu

F32, BF16 = jnp.float32, jnp.bfloat16
EPS = 1e-6
NEG_INF = -1e30
HEAD_DIM = 64
N_HEADS = 8
N_KV = 2
GROUP = N_HEADS // N_KV
WINDOW = 128
ATTN_W = N_HEADS * HEAD_DIM
KV_W = N_KV * HEAD_DIM
CONV_W = 31
HALO = 32
CONV_ROWS = 32
SCALE = 1.0 / 8.0
ADAM_LR, ADAM_B1, ADAM_B2, ADAM_EPS, ADAM_WD, ADAM_STEP = 0.001, 0.9, 0.999, 1e-08, 0.01, 10
TM = 512
TM_FFN_BWD = 256
LANES = 128
VMEM_LIMIT = 52 * 1024 * 1024
MESH = pl.DeviceIdType.MESH
ANY = pl.BlockSpec(memory_space=pl.ANY)
HBM = pl.BlockSpec(memory_space=pltpu.HBM)
SEM = pl.BlockSpec(memory_space=pltpu.SEMAPHORE)
VMEM = pl.BlockSpec(memory_space=pltpu.VMEM)
EFFECT = pltpu.SideEffectType.DATAFLOW_SIDE_EFFECTING
TOKEN = jax.ShapeDtypeStruct((8, LANES), F32)


def _cp(n):
    return pltpu.CompilerParams(dimension_semantics=("arbitrary",) * n, vmem_limit_bytes=VMEM_LIMIT)


def _dot(a, b):
    return jnp.dot(a, b, preferred_element_type=F32)


def _dot_nt(a, b):
    return lax.dot_general(a, b, (((1,), (1,)), ((), ())), preferred_element_type=F32)


def _dot_tn(a, b):
    return lax.dot_general(a, b, (((0,), (0,)), ((), ())), preferred_element_type=F32)


def _sigmoid(v):
    return 1.0 / (1.0 + jnp.exp(-v))


def _place():
    x, y, c = lax.axis_index("x"), lax.axis_index("y"), lax.axis_index("c")
    chips = [(1 - x, y), (x, 1 - y), (1 - x, 1 - y)]
    return x, y, c, chips


def _rcopy(src, dst, send_sems, recv_sems, k, dev):
    return pltpu.make_async_remote_copy(src_ref=src, dst_ref=dst, send_sem=send_sems.at[k],
                                        recv_sem=recv_sems.at[k], device_id=dev, device_id_type=MESH)


def _hbm(a):
    return pltpu.with_memory_space_constraint(a, pltpu.HBM)


def _targets(sibling):
    x, y, c, chips = _place()
    if sibling:
        return 2 * x + y, c, [((x, y, 1 - c), 2 * x + y)]
    return 2 * x + y, c, [((px, py, c), 2 * px + py) for px, py in chips]


def _xchg_start(name, srcs, lands, plan, dep, sibling=False):
    n = len(srcs)
    npeer = 1 if sibling else 3

    def body(*refs):
        src, land = refs[:n], refs[n:2 * n]
        send_sems, recv_sems, token = refs[2 * n + 1], refs[2 * n + 2], refs[-1]
        b, c, peers = _targets(sibling)
        for t in range(n):
            for j, (dev, pb) in enumerate(peers):
                s, d, _ = plan(src[t], land[t], t, b, c, pb)
                _rcopy(s, d, send_sems, recv_sems, npeer * t + j, dev).start()
        token[...] = jnp.zeros_like(token)

    arrs = list(srcs) + list(lands)
    return pl.pallas_call(
        body, name=name,
        out_shape=(pltpu.SemaphoreType.DMA((npeer * n,)), pltpu.SemaphoreType.DMA((npeer * n,)),
                   *[pltpu.HBM(a.shape, a.dtype) for a in arrs], TOKEN),
        in_specs=[HBM] * (2 * n) + [ANY], out_specs=(SEM, SEM, *[HBM] * (2 * n), VMEM),
        input_output_aliases={i: 2 + i for i in range(2 * n)},
        compiler_params=pltpu.CompilerParams(has_side_effects=EFFECT),
    )(*[_hbm(a) for a in arrs], dep)


def _xchg_wait(name, started, n, plan, after, sibling=False):
    send_sems, recv_sems, thru = started[0], started[1], started[2:2 + 2 * n]
    npeer = 1 if sibling else 3

    def body(*refs):
        src, land = refs[:n], refs[n:2 * n]
        send_sems, recv_sems, token = refs[2 * n], refs[2 * n + 1], refs[-1]
        b, c, peers = _targets(sibling)
        for t in range(n):
            for j, (dev, pb) in enumerate(peers):
                s, _, a = plan(src[t], land[t], t, b, c, pb)
                cp = _rcopy(s, a, send_sems, recv_sems, npeer * t + j, dev)
                cp.wait_send()
                cp.wait_recv()
        token[...] = jnp.zeros_like(token)

    out = pl.pallas_call(
        body, name=name,
        out_shape=(*[pltpu.HBM(a.shape, a.dtype) for a in thru], TOKEN),
        in_specs=[HBM] * (2 * n) + [SEM, SEM] + [ANY] * len(after), out_specs=(*[HBM] * (2 * n), VMEM),
        input_output_aliases={i: i for i in range(2 * n)},
        compiler_params=pltpu.CompilerParams(has_side_effects=EFFECT),
    )(*thru, send_sems, recv_sems, *after)
    return out[:n], out[n:2 * n], out[-1]


def _half(ref_rows, which):
    h = ref_rows // 2
    return pl.ds(which * h, h)


def _gather_plan(src, land, t, b, c, pb):
    if len(src.shape) == 2 and src.shape[0] % 2 == 0:
        hs = _half(src.shape[0], c)
        return src.at[hs], land.at[b, hs], land.at[pb, hs]
    return src, land.at[b], land.at[pb]


def _gather_share(lands):
    n = len(lands)

    def body(*refs):
        land_in, land = refs[:n], refs[n:2 * n]
        send_sems, recv_sems = refs[2 * n:]
        x, y, c, chips = _place()
        sib = (x, y, 1 - c)
        sends = []
        for t in range(n):
            for j, (px, py) in enumerate(chips):
                hs = _half(lands[t].shape[1], c)
                cp = _rcopy(land_in[t].at[2 * px + py, hs], land[t].at[2 * px + py, hs], send_sems, recv_sems, 3 * t + j, sib)
                cp.start()
                sends.append(cp)
        for t in range(n):
            for j, (px, py) in enumerate(chips):
                other = land[t].at[2 * px + py, _half(lands[t].shape[1], 1 - c)]
                _rcopy(other, other, send_sems, recv_sems, 3 * t + j, sib).wait_recv()
        for cp in sends:
            cp.wait_send()

    return pl.pallas_call(
        body, name="gather_share", out_shape=[jax.ShapeDtypeStruct(a.shape, a.dtype) for a in lands],
        in_specs=[ANY] * n, out_specs=[ANY] * n, input_output_aliases={t: t for t in range(n)},
        scratch_shapes=[pltpu.SemaphoreType.DMA((3 * n,)), pltpu.SemaphoreType.DMA((3 * n,))],
    )(*lands)


def _rs_plan(src, land, t, b, c, pb):
    return src.at[pb], land.at[b], land.at[pb]


def _sib_plan(src, land, t, b, c, pb):
    return src.at[:, _half(src.shape[1], 1 - c), :], land, land


def _rows_block(h, cap=512):
    for rb in range(min(h, cap) // 16 * 16, 0, -16):
        if h % rb == 0:
            return rb
    return h


def _sum_halves(cidx, g, s):
    _, R, C = g.shape
    rb = _rows_block(R // 2)
    nr = R // 2 // rb

    def body(c_ref, g_ref, s_ref, o_ref):
        o_ref[...] = (g_ref[...] + s_ref[...]).astype(BF16)

    blk = (None, rb, C)
    return pl.pallas_call(
        body, name="sum_halves", out_shape=jax.ShapeDtypeStruct(s.shape, BF16),
        grid_spec=pltpu.PrefetchScalarGridSpec(
            num_scalar_prefetch=1, grid=(4, nr),
            in_specs=[pl.BlockSpec(blk, lambda p, i, c: (p, c[0] * nr + i, 0)),
                      pl.BlockSpec(blk, lambda p, i, c: (p, i, 0))],
            out_specs=pl.BlockSpec(blk, lambda p, i, c: (p, i, 0))),
        compiler_params=_cp(2),
    )(cidx, g, s)


def _rs_share(qs):
    nt = len(qs)

    def body(*refs):
        q_refs, qsib = refs[:nt], refs[nt:2 * nt]
        send_sems, recv_sems = refs[2 * nt:]
        x, y, c, _ = _place()
        cps = []
        for t in range(nt):
            cp = _rcopy(q_refs[t], qsib[t], send_sems, recv_sems, t, (x, y, 1 - c))
            cp.start()
            cps.append(cp)
        for cp in cps:
            cp.wait()

    return pl.pallas_call(
        body, name="rs_share", in_specs=[ANY] * nt, out_specs=[ANY] * nt,
        out_shape=[jax.ShapeDtypeStruct(q.shape, q.dtype) for q in qs],
        scratch_shapes=[pltpu.SemaphoreType.DMA((nt,)), pltpu.SemaphoreType.DMA((nt,))],
    )(*qs)


def _adam_update(gg, w, m, v):
    m2 = ADAM_B1 * m + (1.0 - ADAM_B1) * gg
    v2 = ADAM_B2 * v + (1.0 - ADAM_B2) * (gg * gg)
    mh = m2 / (1.0 - ADAM_B1 ** ADAM_STEP)
    vh = v2 / (1.0 - ADAM_B2 ** ADAM_STEP)
    return -ADAM_LR * (mh / (jnp.sqrt(vh) + ADAM_EPS) + ADAM_WD * w), m2, v2


def _adamw_layer(cidx, q_own, q_sib, w, m, v, bufs, l):
    L, R, C = w.shape
    h = R // 2
    rb = _rows_block(h, 256)
    nr = h // rb

    def body(c_ref, qo_ref, qs_ref, w_ref, m_ref, v_ref, *rest):
        g_ref, d_ref, mo_ref, vo_ref = rest[-4:]
        own = pl.program_id(0) == c_ref[0]
        gg = jnp.zeros((rb, C), F32)
        for s in range(4):
            gg = gg + jnp.where(own, qo_ref[s], qs_ref[s]).astype(F32)
        g_ref[...] = gg
        d_ref[...], mo_ref[...], vo_ref[...] = _adam_update(gg, w_ref[...], m_ref[...], v_ref[...])

    qspec = pl.BlockSpec((4, rb, C), lambda hh, i, c: (0, i, 0))
    wspec = pl.BlockSpec((None, rb, C), lambda hh, i, c: (l, hh * nr + i, 0))
    return pl.pallas_call(
        body, name="adamw_layer", out_shape=[jax.ShapeDtypeStruct(w.shape, F32)] * 4,
        grid_spec=pltpu.PrefetchScalarGridSpec(
            num_scalar_prefetch=1, grid=(2, nr),
            in_specs=[qspec, qspec, wspec, wspec, wspec] + [ANY] * 4, out_specs=[wspec] * 4),
        input_output_aliases={6 + k: k for k in range(4)},
        compiler_params=_cp(2),
    )(cidx, q_own, q_sib, w, m, v, *bufs)


def _adamw(g, w, m, v):
    L, R, C = g.shape
    rb = _rows_block(R)

    def body(g_ref, w_ref, m_ref, v_ref, d_ref, mo_ref, vo_ref):
        d_ref[...], mo_ref[...], vo_ref[...] = _adam_update(g_ref[...], w_ref[...], m_ref[...], v_ref[...])

    spec = pl.BlockSpec((None, rb, C), lambda l, i: (l, i, 0))
    return pl.pallas_call(
        body, name="adamw", grid=(L, R // rb), in_specs=[spec] * 4, out_specs=[spec] * 3,
        out_shape=[jax.ShapeDtypeStruct(g.shape, F32)] * 3, compiler_params=_cp(2),
    )(g, w, m, v)


def _small_allreduce(p):
    R = p.shape[0]

    def body(p_ref, o_ref, buf_ref, send_sems, recv_sems):
        x, y, c, _ = _place()
        me = 4 * x + 2 * y + c
        flip = lambda a, f: 1 - a if f else a
        buf_ref[me] = p_ref[...]
        peers = [(flip(x, k >> 2 & 1), flip(y, k >> 1 & 1), flip(c, k & 1)) for k in range(1, 8)]
        cps = []
        for k, dev in enumerate(peers):
            cp = _rcopy(p_ref, buf_ref.at[me], send_sems, recv_sems, k, dev)
            cp.start()
            cps.append(cp)
        for k, (px, py, pc) in enumerate(peers):
            slot = buf_ref.at[4 * px + 2 * py + pc]
            _rcopy(slot, slot, send_sems, recv_sems, k, (px, py, pc)).wait_recv()
        for cp in cps:
            cp.wait_send()
        acc = buf_ref[0]
        for s in range(1, 8):
            acc = acc + buf_ref[s]
        o_ref[...] = acc

    return pl.pallas_call(
        body, name="small_allreduce", in_specs=[VMEM], out_specs=VMEM,
        out_shape=jax.ShapeDtypeStruct(p.shape, F32),
        scratch_shapes=[pltpu.VMEM((8, R, LANES), F32), pltpu.SemaphoreType.DMA((7,)), pltpu.SemaphoreType.DMA((7,))],
    )(p)


def _rms(xf, g):
    r = lax.rsqrt(jnp.mean(xf * xf, axis=-1, keepdims=True) + EPS)
    return xf * r, r


def _ffn_fwd(x, g, win, wout):
    T, D = x.shape
    FB = win.shape[2]
    tm = min(TM, T)

    def body(x_ref, g_ref, wg_ref, wu_ref, wo_ref, xo_ref, gu_ref, h_ref, acc_ref):
        j = pl.program_id(1)

        @pl.when(j == 0)
        def _():
            xh, _ = _rms(x_ref[...], None)
            h_ref[...] = (xh * g_ref[...]).astype(BF16)
            acc_ref[...] = jnp.zeros_like(acc_ref)

        h = h_ref[...]
        gate = _dot(h, wg_ref[...])
        up = _dot(h, wu_ref[...])
        gu_ref[0] = gate.astype(BF16)
        gu_ref[1] = up.astype(BF16)
        a = (gate * _sigmoid(gate) * up).astype(BF16)
        acc_ref[...] += _dot(a, wo_ref[...])

        @pl.when(j == 1)
        def _():
            xo_ref[...] = x_ref[...] + 0.5 * acc_ref[...]

    return pl.pallas_call(
        body, name="ffn_fwd", grid=(T // tm, 2),
        in_specs=[pl.BlockSpec((tm, D), lambda i, j: (i, 0)),
                  pl.BlockSpec((1, D), lambda i, j: (0, 0)),
                  pl.BlockSpec((None, D, FB), lambda i, j: (j, 0, 0)),
                  pl.BlockSpec((None, D, FB), lambda i, j: (j + 2, 0, 0)),
                  pl.BlockSpec((FB, D), lambda i, j: (j, 0))],
        out_specs=[pl.BlockSpec((tm, D), lambda i, j: (i, 0)),
                   pl.BlockSpec((2, tm, FB), lambda i, j: (0, i, j))],
        out_shape=[jax.ShapeDtypeStruct((T, D), F32), jax.ShapeDtypeStruct((2, T, 2 * FB), BF16)],
        scratch_shapes=[pltpu.VMEM((tm, D), BF16), pltpu.VMEM((tm, D), F32)],
        compiler_params=_cp(2),
    )(x, g, win, win, wout)


def _mixproj_fwd(x, g, w):
    T, D = x.shape
    W = w.shape[1]
    QKV = ATTN_W + 2 * KV_W
    tm = min(TM, T)

    def body(x_ref, g_ref, w_ref, qkv_ref, u_ref):
        xh, _ = _rms(x_ref[...], None)
        h = (xh * g_ref[...]).astype(BF16)
        qkv_ref[...] = _dot(h, w_ref[:, :QKV]).astype(BF16)
        u_ref[...] = _dot(h, w_ref[:, QKV:])

    return pl.pallas_call(
        body, name="mixproj_fwd", grid=(T // tm,),
        in_specs=[pl.BlockSpec((tm, D), lambda i: (i, 0)), pl.BlockSpec((1, D), lambda i: (0, 0)),
                  pl.BlockSpec((D, W), lambda i: (0, 0))],
        out_specs=[pl.BlockSpec((tm, QKV), lambda i: (i, 0)), pl.BlockSpec((tm, W - QKV), lambda i: (i, 0))],
        out_shape=[jax.ShapeDtypeStruct((T, QKV), BF16), jax.ShapeDtypeStruct((T, W - QKV), F32)],
        compiler_params=_cp(1),
    )(x, g, w)


def _attn_tables(n, g):
    rows, cols = GROUP * WINDOW, 2 * WINDOW
    row = lax.broadcasted_iota(jnp.int32, (rows, cols), 0)
    col = lax.broadcasted_iota(jnp.int32, (rows, cols), 1)
    dist = (row & (WINDOW - 1)) + WINDOW - col
    valid = (dist >= 0) & (dist < WINDOW) & ((n > 0) | (col >= WINDOW))
    hi = row >> 7
    slope = jnp.zeros((rows, cols), F32)
    for i in range(GROUP):
        slope = jnp.where(hi == i, 2.0 ** -(GROUP * g + i + 1), slope)
    bias = -slope * dist.astype(F32)
    return valid, bias


def _sink_col(sink_ref, g):
    hi = lax.broadcasted_iota(jnp.int32, (GROUP * WINDOW, 1), 0) >> 7
    col = jnp.zeros((GROUP * WINDOW, 1), F32)
    for i in range(GROUP):
        col = jnp.where(hi == i, sink_ref[0, GROUP * g + i], col)
    return col


def _stack_heads(ref, g):
    return jnp.concatenate([ref[:, (GROUP * g + i) * HEAD_DIM:(GROUP * g + i + 1) * HEAD_DIM]
                            for i in range(GROUP)], axis=0)


def _band(kvp_ref, kvc_ref, off):
    return jnp.concatenate([kvp_ref[:, off:off + HEAD_DIM], kvc_ref[:, off:off + HEAD_DIM]], axis=0)


def _attn_probs(qs, k, valid, bias, sink):
    s = _dot_nt(qs, k) * SCALE
    s = jnp.where(valid, s + bias, NEG_INF)
    m = jnp.maximum(jnp.max(s, axis=-1, keepdims=True), sink)
    p = jnp.exp(s - m)
    es = jnp.exp(sink - m)
    den = jnp.sum(p, axis=-1, keepdims=True) + es
    return p / den, es / den


def _attn_fwd(sinks, qkv):
    T = qkv.shape[0]
    nb = T // WINDOW

    def body(sink_ref, q_ref, kvp_ref, kvc_ref, o_ref):
        n = pl.program_id(0)
        for g in range(N_KV):
            valid, bias = _attn_tables(n, g)
            qs = _stack_heads(q_ref, g)
            k = _band(kvp_ref, kvc_ref, g * HEAD_DIM)
            v = _band(kvp_ref, kvc_ref, KV_W + g * HEAD_DIM)
            p, _ = _attn_probs(qs, k, valid, bias, _sink_col(sink_ref, g))
            o = _dot(p.astype(BF16), v)
            for i in range(GROUP):
                h = GROUP * g + i
                o_ref[:, h * HEAD_DIM:(h + 1) * HEAD_DIM] = o[i * WINDOW:(i + 1) * WINDOW].astype(BF16)

    return pl.pallas_call(
        body, name="attn_fwd", grid=(nb,),
        in_specs=[pl.BlockSpec(memory_space=pltpu.SMEM),
                  pl.BlockSpec((WINDOW, ATTN_W), lambda n: (n, 0)),
                  pl.BlockSpec((WINDOW, 2 * KV_W), lambda n: (jnp.maximum(n - 1, 0), 2)),
                  pl.BlockSpec((WINDOW, 2 * KV_W), lambda n: (n, 2))],
        out_specs=pl.BlockSpec((WINDOW, ATTN_W), lambda n: (n, 0)),
        out_shape=jax.ShapeDtypeStruct((T, ATTN_W), BF16),
        compiler_params=_cp(1),
    )(sinks, qkv, qkv, qkv)


def _shift_copies(src_ref, dst_ref, n):
    for b in range(1, 8):
        dst_ref[b - 1] = src_ref[b:b + n, :]


def _tap(src_ref, sh_ref, s, c0):
    a, b = divmod(s, 8)
    start = pl.multiple_of(c0 + 8 * a, 8)
    if b == 0:
        return src_ref[pl.ds(start, CONV_ROWS), :]
    return sh_ref[b - 1, pl.ds(start, CONV_ROWS), :]


def _glu_rows(u, ch):
    return u[:, :ch] * _sigmoid(u[:, ch:])


def _fill_z(zs_ref, zsh_ref, uc_ref, up_ref, i, ch, n):
    zs_ref[0:HALO] = jnp.where(i > 0, _glu_rows(up_ref[...], ch), 0.0)
    zs_ref[HALO:] = _glu_rows(uc_ref[...], ch)
    _shift_copies(zs_ref, zsh_ref, n - 8)


def _conv_fwd(u, w, b, lg, lb):
    T = u.shape[0]
    CH = u.shape[1] // 2
    tm = min(TM, T)
    n = tm + HALO
    hb = tm // HALO

    def body(uc_ref, up_ref, w_ref, b_ref, lg_ref, lb_ref, conv_ref, ypre_ref, zs_ref, zsh_ref):
        i = pl.program_id(0)
        _fill_z(zs_ref, zsh_ref, uc_ref, up_ref, i, CH, n)
        bias = b_ref[...]

        def chunk(ci, carry):
            c0 = pl.multiple_of(ci * CONV_ROWS, CONV_ROWS)
            acc = jnp.broadcast_to(bias, (CONV_ROWS, CH))
            for k in range(CONV_W):
                acc = acc + w_ref[k:k + 1, :] * _tap(zs_ref, zsh_ref, HALO - (CONV_W - 1) + k, c0)
            ypre_ref[pl.ds(c0, CONV_ROWS), :] = acc
            return carry

        lax.fori_loop(0, tm // CONV_ROWS, chunk, 0)
        y = ypre_ref[...]
        mu = jnp.mean(y, axis=-1, keepdims=True)
        d = y - mu
        var = jnp.mean(d * d, axis=-1, keepdims=True)
        o = d * lax.rsqrt(var + EPS) * lg_ref[...] + lb_ref[...]
        conv_ref[...] = (o * _sigmoid(o)).astype(BF16)

    vec = pl.BlockSpec((1, CH), lambda i: (0, 0))
    return pl.pallas_call(
        body, name="conv_fwd", grid=(T // tm,),
        in_specs=[pl.BlockSpec((tm, 2 * CH), lambda i: (i, 0)),
                  pl.BlockSpec((HALO, 2 * CH), lambda i: (jnp.maximum(i * hb - 1, 0), 0)),
                  pl.BlockSpec((CONV_W, CH), lambda i: (0, 0)), vec, vec, vec],
        out_specs=[pl.BlockSpec((tm, CH), lambda i: (i, 0)), pl.BlockSpec((tm, CH), lambda i: (i, 0))],
        out_shape=[jax.ShapeDtypeStruct((T, CH), BF16), jax.ShapeDtypeStruct((T, CH), F32)],
        scratch_shapes=[pltpu.VMEM((n, CH), F32), pltpu.VMEM((7, n - 8, CH), F32)],
        compiler_params=_cp(1),
    )(u, u, w, b, lg, lb)


def _mixout_fwd(x, attn, conv, wo):
    T, D = x.shape
    tm = min(TM, T)
    A = attn.shape[1]

    def body(x_ref, a_ref, c_ref, w_ref, xo_ref):
        xo_ref[...] = x_ref[...] + _dot(a_ref[...], w_ref[:A, :]) + _dot(c_ref[...], w_ref[A:, :])

    return pl.pallas_call(
        body, name="mixout_fwd", grid=(T // tm,),
        in_specs=[pl.BlockSpec((tm, D), lambda i: (i, 0)), pl.BlockSpec((tm, A), lambda i: (i, 0)),
                  pl.BlockSpec((tm, conv.shape[1]), lambda i: (i, 0)), pl.BlockSpec(wo.shape, lambda i: (0, 0))],
        out_specs=pl.BlockSpec((tm, D), lambda i: (i, 0)),
        out_shape=jax.ShapeDtypeStruct((T, D), F32),
        compiler_params=_cp(1),
    )(x, attn, conv, wo)


def _rms_bwd_rows(dh, xf, g):
    xh, r = _rms(xf, None)
    dxn = dh * g
    dx = r * (dxn - xh * jnp.mean(dxn * xh, axis=-1, keepdims=True))
    return dx, jnp.sum(dh * xh, axis=0, keepdims=True), xh * g


def _loss_head(x, g, tgt):
    T, D = x.shape
    tm = min(TM, T)

    def body(x_ref, g_ref, t_ref, loss_ref, dx_ref, dg_ref):
        @pl.when(pl.program_id(0) == 0)
        def _():
            loss_ref[...] = jnp.zeros_like(loss_ref)
            dg_ref[...] = jnp.zeros_like(dg_ref)

        xf = x_ref[...]
        g = g_ref[...]
        xh, _ = _rms(xf, None)
        e = xh * g - t_ref[...]
        loss_ref[...] += 0.5 * jnp.sum(jnp.mean(e * e, axis=-1, keepdims=True), axis=0, keepdims=True)
        dx, dg, _ = _rms_bwd_rows(e * (1.0 / D), xf, g)
        dx_ref[...] = dx
        dg_ref[...] += dg

    return pl.pallas_call(
        body, name="loss_head", grid=(T // tm,),
        in_specs=[pl.BlockSpec((tm, D), lambda i: (i, 0)), pl.BlockSpec((1, D), lambda i: (0, 0)),
                  pl.BlockSpec((tm, D), lambda i: (i, 0))],
        out_specs=[pl.BlockSpec((1, 1), lambda i: (0, 0)), pl.BlockSpec((tm, D), lambda i: (i, 0)),
                   pl.BlockSpec((1, D), lambda i: (0, 0))],
        out_shape=[jax.ShapeDtypeStruct((1, 1), F32), jax.ShapeDtypeStruct((T, D), F32),
                   jax.ShapeDtypeStruct((1, D), F32)],
        compiler_params=_cp(1),
    )(x, g, tgt)


def _lane_chunks(n):
    lo = (n // LANES + 1) // 2 * LANES
    return ((0, lo), (lo, n - lo))


def _ffn_bwd(dxo, x, g, gu, win, wout, dep):
    T, D = x.shape
    FB = win.shape[2]
    tm = min(TM_FFN_BWD, T)

    def body(dxo_ref, x_ref, g_ref, gu_ref, win_hbm, wout_hbm, dep_ref,
             dxi_ref, dg_ref, hb_ref, dgu_ref, a_ref, dyb_ref, win_v, wout_v, sems):
        @pl.when(pl.program_id(0) == 0)
        def _():
            loads = [pltpu.make_async_copy(win_hbm.at[k], win_v.at[k], sems.at[k]) for k in range(4)]
            loads += [pltpu.make_async_copy(wout_hbm.at[pl.ds(k * FB, FB)], wout_v.at[pl.ds(k * FB, FB)], sems.at[4 + k])
                      for k in range(2)]
            for cp in loads:
                cp.start()
            for cp in loads:
                cp.wait()
            dg_ref[...] = jnp.zeros_like(dg_ref)

        dyb = (0.5 * dxo_ref[...]).astype(BF16)
        dyb_ref[...] = dyb
        dh = jnp.zeros((tm, D), F32)
        for blk in range(2):
            for lo, sz in _lane_chunks(FB):
                cols = pl.ds(blk * FB + lo, sz)
                da = _dot_nt(dyb, wout_v[cols, :])
                gate = gu_ref[0, :, cols].astype(F32)
                up = gu_ref[1, :, cols].astype(F32)
                sg = pl.reciprocal(1.0 + jnp.exp(-gate), approx=True)
                s = gate * sg
                a_ref[:, cols] = (s * up).astype(BF16)
                dgate = (da * up * (sg * (1.0 + gate * (1.0 - sg)))).astype(BF16)
                dup = (da * s).astype(BF16)
                dgu_ref[0, :, cols] = dgate
                dgu_ref[1, :, cols] = dup
                dh = dh + _dot_nt(dgate, win_v[blk, :, pl.ds(lo, sz)]) + _dot_nt(dup, win_v[2 + blk, :, pl.ds(lo, sz)])
        dx, dg, h = _rms_bwd_rows(dh, x_ref[...], g_ref[...])
        dxi_ref[...] = dxo_ref[...] + dx
        dg_ref[...] += dg
        hb_ref[...] = h.astype(BF16)

    row = pl.BlockSpec((tm, D), lambda i: (i, 0))
    act = pl.BlockSpec((2, tm, 2 * FB), lambda i: (0, i, 0))
    return pl.pallas_call(
        body, name="ffn_bwd", grid=(T // tm,),
        in_specs=[row, row, pl.BlockSpec((1, D), lambda i: (0, 0)), act, ANY, ANY, ANY],
        out_specs=[row, pl.BlockSpec((1, D), lambda i: (0, 0)), row, act,
                   pl.BlockSpec((tm, 2 * FB), lambda i: (i, 0)), row],
        out_shape=[jax.ShapeDtypeStruct((T, D), F32), jax.ShapeDtypeStruct((1, D), F32),
                   jax.ShapeDtypeStruct((T, D), BF16), jax.ShapeDtypeStruct((2, T, 2 * FB), BF16),
                   jax.ShapeDtypeStruct((T, 2 * FB), BF16), jax.ShapeDtypeStruct((T, D), BF16)],
        scratch_shapes=[pltpu.VMEM(win.shape, BF16), pltpu.VMEM(wout.shape, BF16), pltpu.SemaphoreType.DMA((6,))],
        compiler_params=_cp(1),
    )(dxo, x, g, gu, win, wout, dep)


def _rms_matmul_bwd(name, dxo, x, g, dzs, ws, dz_specs, w_specs, nk):
    T, D = x.shape
    tm = min(TM, T)
    npair = len(dzs)

    def body(*refs):
        dxo_ref, x_ref, g_ref = refs[:3]
        dz_refs, w_refs = refs[3:3 + npair], refs[3 + npair:3 + 2 * npair]
        dxi_ref, dg_ref, hb_ref, acc_ref = refs[3 + 2 * npair:]
        i, k = pl.program_id(0), pl.program_id(1)

        @pl.when(k == 0)
        def _():
            acc_ref[...] = jnp.zeros_like(acc_ref)

        @pl.when((i == 0) & (k == 0))
        def _():
            dg_ref[...] = jnp.zeros_like(dg_ref)

        for p in range(npair):
            acc_ref[...] += _dot_nt(dz_refs[p][...], w_refs[p][...])

        @pl.when(k == nk - 1)
        def _():
            dx, dg, h = _rms_bwd_rows(acc_ref[...], x_ref[...], g_ref[...])
            dxi_ref[...] = dxo_ref[...] + dx
            dg_ref[...] += dg
            hb_ref[...] = h.astype(BF16)

    row = pl.BlockSpec((tm, D), lambda i, k: (i, 0))
    return pl.pallas_call(
        body, name=name, grid=(T // tm, nk),
        in_specs=[row, row, pl.BlockSpec((1, D), lambda i, k: (0, 0))] + list(dz_specs) + list(w_specs),
        out_specs=[row, pl.BlockSpec((1, D), lambda i, k: (0, 0)), row],
        out_shape=[jax.ShapeDtypeStruct((T, D), F32), jax.ShapeDtypeStruct((1, D), F32),
                   jax.ShapeDtypeStruct((T, D), BF16)],
        scratch_shapes=[pltpu.VMEM((tm, D), F32)],
        compiler_params=_cp(2),
    )(dxo, x, g, *dzs, *ws)


def _mix_rms_bwd(dxo, x, g, dzs, ws):
    tm = min(TM, x.shape[0])
    return _rms_matmul_bwd(
        "mix_rms_bwd", dxo, x, g, dzs, ws,
        [pl.BlockSpec((tm, dz.shape[1]), lambda i, k: (i, 0)) for dz in dzs],
        [pl.BlockSpec(w.shape, lambda i, k: (0, 0)) for w in ws], 1)


def _wgrad(name, a, b, a_spec, b_spec, out_shape, out_spec, nblk):
    T = a.shape[0]
    tk = min(TM, T)

    def body(a_ref, b_ref, o_ref):
        @pl.when(pl.program_id(1) == 0)
        def _():
            o_ref[...] = jnp.zeros_like(o_ref)

        o_ref[...] += _dot_tn(a_ref[...], b_ref[...]).reshape(o_ref.shape)

    return pl.pallas_call(
        body, name=name, grid=(nblk, T // tk), in_specs=[a_spec, b_spec], out_specs=out_spec,
        out_shape=jax.ShapeDtypeStruct(out_shape, F32), compiler_params=_cp(2),
    )(a, b)


def _wgrad_ffn_in(hb, dgu):
    T, D = hb.shape
    FB = dgu.shape[2] // 2
    tk = min(TM, T)
    return _wgrad("wgrad_ffn_in", hb, dgu,
                  pl.BlockSpec((tk, D), lambda b, k: (k, 0)),
                  pl.BlockSpec((None, tk, FB), lambda b, k: (b // 2, k, b % 2)),
                  (4, D, FB), pl.BlockSpec((None, D, FB), lambda b, k: (b, 0, 0)), 4)


def _wgrad_ffn_out(a, dyb):
    T, D = dyb.shape
    FB = a.shape[1] // 2
    tk = min(TM, T)
    return _wgrad("wgrad_ffn_out", a, dyb,
                  pl.BlockSpec((tk, FB), lambda b, k: (k, b)),
                  pl.BlockSpec((tk, D), lambda b, k: (k, 0)),
                  (4, FB // 2, D), pl.BlockSpec((2, FB // 2, D), lambda b, k: (b, 0, 0)), 2)


def _wgrad_plain(a, b):
    T, M = a.shape
    N = b.shape[1]
    tk = min(TM, T)
    return _wgrad("wgrad_plain", a, b, pl.BlockSpec((tk, M), lambda i, k: (k, 0)),
                  pl.BlockSpec((tk, N), lambda i, k: (k, 0)), (M, N),
                  pl.BlockSpec((M, N), lambda i, k: (0, 0)), 1)


def _mixout_bwd(dxo, wo):
    T, D = dxo.shape
    tm = min(TM, T)
    A = ATTN_W
    C = wo.shape[0] - A

    def body(dxo_ref, w_ref, dyb_ref, da_ref, dc_ref):
        dyb = dxo_ref[...].astype(BF16)
        dyb_ref[...] = dyb
        da_ref[...] = _dot_nt(dyb, w_ref[:A, :]).astype(BF16)
        dc_ref[...] = _dot_nt(dyb, w_ref[A:, :])

    return pl.pallas_call(
        body, name="mixout_bwd", grid=(T // tm,),
        in_specs=[pl.BlockSpec((tm, D), lambda i: (i, 0)), pl.BlockSpec(wo.shape, lambda i: (0, 0))],
        out_specs=[pl.BlockSpec((tm, D), lambda i: (i, 0)), pl.BlockSpec((tm, A), lambda i: (i, 0)),
                   pl.BlockSpec((tm, C), lambda i: (i, 0))],
        out_shape=[jax.ShapeDtypeStruct((T, D), BF16), jax.ShapeDtypeStruct((T, A), BF16),
                   jax.ShapeDtypeStruct((T, C), F32)],
        compiler_params=_cp(1),
    )(dxo, wo)


def _conv_bwd(dconv, ypre, u, w, lg, lb):
    T, CH = dconv.shape
    tm = min(TM, T)
    n = tm + HALO
    hb = tm // HALO
    nt = T // tm
    nchunk = tm // CONV_ROWS

    def body(dc_ref, dcn_ref, yp_ref, ypn_ref, uc_ref, up_ref, w_ref, lg_ref, lb_ref,
             du_ref, dw_ref, dvec_ref, zs_ref, zsh_ref, dy_ref, dysh_ref, dz_ref):
        i = pl.program_id(0)

        @pl.when(i == 0)
        def _():
            dw_ref[...] = jnp.zeros_like(dw_ref)
            dvec_ref[...] = jnp.zeros_like(dvec_ref)

        g, bb = lg_ref[...], lb_ref[...]

        def ln_bwd(dc, yp):
            mu = jnp.mean(yp, axis=-1, keepdims=True)
            d = yp - mu
            rs = lax.rsqrt(jnp.mean(d * d, axis=-1, keepdims=True) + EPS)
            yn = d * rs
            o = yn * g + bb
            sg = _sigmoid(o)
            do = dc * (sg * (1.0 + o * (1.0 - sg)))
            dyn = do * g
            dyp = rs * (dyn - jnp.mean(dyn, axis=-1, keepdims=True)
                        - yn * jnp.mean(dyn * yn, axis=-1, keepdims=True))
            return dyp, do, yn

        dyp, do, yn = ln_bwd(dc_ref[...], yp_ref[...])
        dvec_ref[0:1, :] += jnp.sum(dyp, axis=0, keepdims=True)
        dvec_ref[1:2, :] += jnp.sum(do * yn, axis=0, keepdims=True)
        dvec_ref[2:3, :] += jnp.sum(do, axis=0, keepdims=True)
        dy_ref[0:tm] = dyp
        dyh, _, _ = ln_bwd(dcn_ref[...], ypn_ref[...])
        dy_ref[tm:] = jnp.where(i < nt - 1, dyh, 0.0)
        _shift_copies(dy_ref, dysh_ref, n - 8)
        _fill_z(zs_ref, zsh_ref, uc_ref, up_ref, i, CH, n)

        def chunk(ci, carry):
            c0 = pl.multiple_of(ci * CONV_ROWS, CONV_ROWS)
            acc = jnp.zeros((CONV_ROWS, CH), F32)
            for k in range(CONV_W):
                acc = acc + w_ref[k:k + 1, :] * _tap(dy_ref, dysh_ref, CONV_W - 1 - k, c0)
            dz_ref[pl.ds(c0, CONV_ROWS), :] = acc
            return carry

        lax.fori_loop(0, nchunk, chunk, 0)

        for k in range(CONV_W):
            def red(ci, acc, k=k):
                c0 = pl.multiple_of(ci * CONV_ROWS, CONV_ROWS)
                prod = dy_ref[pl.ds(c0, CONV_ROWS), :] * _tap(zs_ref, zsh_ref, HALO - (CONV_W - 1) + k, c0)
                return acc + jnp.sum(prod.reshape(CONV_ROWS // 8, 8, CH), axis=0)

            acc = lax.fori_loop(0, nchunk, red, jnp.zeros((8, CH), F32))
            dw_ref[k:k + 1, :] += jnp.sum(acc, axis=0, keepdims=True)

        uc = uc_ref[...]
        a = uc[:, :CH]
        sg = _sigmoid(uc[:, CH:])
        dz = dz_ref[...]
        du_ref[:, :CH] = (dz * sg).astype(BF16)
        du_ref[:, CH:] = (dz * a * sg * (1.0 - sg)).astype(BF16)

    cur = lambda c: pl.BlockSpec((tm, c), lambda i: (i, 0))
    nxt = lambda c: pl.BlockSpec((HALO, c), lambda i: (jnp.minimum((i + 1) * hb, T // HALO - 1), 0))
    vec = pl.BlockSpec((1, CH), lambda i: (0, 0))
    return pl.pallas_call(
        body, name="conv_bwd", grid=(nt,),
        in_specs=[cur(CH), nxt(CH), cur(CH), nxt(CH), cur(2 * CH),
                  pl.BlockSpec((HALO, 2 * CH), lambda i: (jnp.maximum(i * hb - 1, 0), 0)),
                  pl.BlockSpec((CONV_W, CH), lambda i: (0, 0)), vec, vec],
        out_specs=[pl.BlockSpec((tm, 2 * CH), lambda i: (i, 0)), pl.BlockSpec((32, CH), lambda i: (0, 0)),
                   pl.BlockSpec((8, CH), lambda i: (0, 0))],
        out_shape=[jax.ShapeDtypeStruct((T, 2 * CH), BF16), jax.ShapeDtypeStruct((32, CH), F32),
                   jax.ShapeDtypeStruct((8, CH), F32)],
        scratch_shapes=[pltpu.VMEM((n, CH), F32), pltpu.VMEM((7, n - 8, CH), F32),
                        pltpu.VMEM((n, CH), F32), pltpu.VMEM((7, n - 8, CH), F32), pltpu.VMEM((tm, CH), F32)],
        compiler_params=_cp(1),
    )(dconv, dconv, ypre, ypre, u, u, w, lg, lb)


def _attn_bwd(sinks, qkv, dattn):
    T = qkv.shape[0]
    nb = T // WINDOW

    def body(sink_ref, q_ref, kvp_ref, kvc_ref, do_ref, dq_ref, dkv_ref, dsk_ref, carry_ref):
        n = pl.program_id(0)

        @pl.when(n == 0)
        def _():
            dsk_ref[...] = jnp.zeros_like(dsk_ref)
            carry_ref[...] = jnp.zeros_like(carry_ref)

        @pl.when(n < nb)
        def _():
            for g in range(N_KV):
                valid, bias = _attn_tables(n, g)
                qs = _stack_heads(q_ref, g)
                dos = _stack_heads(do_ref, g)
                k = _band(kvp_ref, kvc_ref, g * HEAD_DIM)
                v = _band(kvp_ref, kvc_ref, KV_W + g * HEAD_DIM)
                p, ps = _attn_probs(qs, k, valid, bias, _sink_col(sink_ref, g))
                dp = _dot_nt(dos, v)
                delta = jnp.sum(p * dp, axis=-1, keepdims=True)
                dsb = (p * (dp - delta)).astype(BF16)
                dsink = -ps * delta
                dqs = _dot(dsb, k) * SCALE
                dk = _dot_tn(dsb, qs) * SCALE
                dv = _dot_tn(p.astype(BF16), dos)
                for i in range(GROUP):
                    h = GROUP * g + i
                    dq_ref[:, h * HEAD_DIM:(h + 1) * HEAD_DIM] = dqs[i * WINDOW:(i + 1) * WINDOW].astype(BF16)
                    dsk_ref[h:h + 1, :] += jnp.sum(dsink[i * WINDOW:(i + 1) * WINDOW], axis=0, keepdims=True)
                for off, d in ((g * HEAD_DIM, dk), (KV_W + g * HEAD_DIM, dv)):
                    dkv_ref[:, off:off + HEAD_DIM] = (carry_ref[:, off:off + HEAD_DIM] + d[:WINDOW]).astype(BF16)
                    carry_ref[:, off:off + HEAD_DIM] = d[WINDOW:]

        @pl.when(n == nb)
        def _():
            dkv_ref[...] = carry_ref[...].astype(BF16)

    last = nb - 1
    return pl.pallas_call(
        body, name="attn_bwd", grid=(nb + 1,),
        in_specs=[pl.BlockSpec(memory_space=pltpu.SMEM),
                  pl.BlockSpec((WINDOW, ATTN_W), lambda n: (jnp.minimum(n, last), 0)),
                  pl.BlockSpec((WINDOW, 2 * KV_W), lambda n: (jnp.clip(n - 1, 0, last), 2)),
                  pl.BlockSpec((WINDOW, 2 * KV_W), lambda n: (jnp.minimum(n, last), 2)),
                  pl.BlockSpec((WINDOW, ATTN_W), lambda n: (jnp.minimum(n, last), 0))],
        out_specs=[pl.BlockSpec((WINDOW, ATTN_W), lambda n: (jnp.minimum(n, last), 0)),
                   pl.BlockSpec((WINDOW, 2 * KV_W), lambda n: (jnp.maximum(n - 1, 0), 0)),
                   pl.BlockSpec((8, LANES), lambda n: (0, 0))],
        out_shape=[jax.ShapeDtypeStruct((T, ATTN_W), BF16), jax.ShapeDtypeStruct((T, 2 * KV_W), BF16),
                   jax.ShapeDtypeStruct((8, LANES), F32)],
        scratch_shapes=[pltpu.VMEM((WINDOW, 2 * KV_W), F32)],
        compiler_params=_cp(1),
    )(sinks, qkv, qkv, qkv, dattn)


def _pack(arrs):
    flat = jnp.concatenate([a.reshape(-1) for a in arrs])
    pad = -flat.shape[0] % (8 * LANES)
    return jnp.pad(flat, (0, pad)).reshape(1, -1, LANES)


def _unpack(packed, like):
    flat = packed.reshape(-1)
    out, off = [], 0
    for a in like:
        out.append(flat[off:off + a.size].reshape(a.shape))
        off += a.size
    return out


def kernel(x, norm_ffn1, w_ffn1_in, w_ffn1_out, norm_mix, w_in, sinks, w_dw, b_dw, conv_ln_g, conv_ln_b, w_out, norm_ffn2, w_ffn2_in, w_ffn2_out, final_norm, loss_target, m_norm_ffn1, m_w_ffn1_in, m_w_ffn1_out, m_norm_mix, m_w_in, m_sinks, m_w_dw, m_b_dw, m_conv_ln_g, m_conv_ln_b, m_w_out, m_norm_ffn2, m_w_ffn2_in, m_w_ffn2_out, m_final_norm, v_norm_ffn1, v_w_ffn1_in, v_w_ffn1_out, v_norm_mix, v_w_in, v_sinks, v_w_dw, v_b_dw, v_conv_ln_g, v_conv_ln_b, v_w_out, v_norm_ffn2, v_w_ffn2_in, v_w_ffn2_out, v_final_norm):
    L, D = norm_ffn1.shape
    T = x.shape[1]
    FB = w_ffn1_in.shape[2]
    CH = b_dw.shape[1]
    QKV = ATTN_W + 2 * KV_W
    xs = x.reshape(T, D)
    tgt = loss_target.reshape(T, D)
    cx, cy, cc = lax.axis_index("x"), lax.axis_index("y"), lax.axis_index("c")
    chip = 2 * cx + cy
    cidx = cc.reshape(1).astype(jnp.int32)
    big_w = (w_ffn1_in, w_ffn1_out, w_in, w_out, w_ffn2_in, w_ffn2_out)
    big_m = (m_w_ffn1_in, m_w_ffn1_out, m_w_in, m_w_out, m_w_ffn2_in, m_w_ffn2_out)
    big_v = (v_w_ffn1_in, v_w_ffn1_out, v_w_in, v_w_out, v_w_ffn2_in, v_w_ffn2_out)
    NW = len(big_w) + 1

    def shards(l, tok):
        return [(w_[l] + tok[0, 0]).astype(BF16) for w_ in big_w] + [w_dw[l] + tok[0, 0]]

    def own_slot(a):
        return lax.dynamic_update_index_in_dim(lax.empty((4,) + a.shape, a.dtype), a, chip, 0)

    def gather_start(srcs, tok):
        return _xchg_start("gather_start", srcs, [own_slot(s_) for s_ in srcs], _gather_plan, tok)

    row = lambda a, l: a[l].reshape(1, -1)

    saved, W = [], []
    zero_tok = jnp.zeros((8, LANES), F32)
    started = gather_start(shards(0, zero_tok), zero_tok)
    cast = [None] + [shards(l, started[-1]) for l in range(1, L)]
    after = [xs] + [a_ for c_ in cast[1:] for a_ in c_]
    for l in range(L):
        _, lands, tok = _xchg_wait("gather_wait", started, NW, _gather_plan, after)
        if l + 1 < L:
            started = gather_start(cast[l + 1], tok)
            tok = started[-1]
        g1i, g1o, gi, go, g2i, g2o = _gather_share(lands[:-1])
        w = dict(f1i=g1i, f1o=g1o.reshape(2 * FB, D), f2i=g2i, f2o=g2o.reshape(2 * FB, D),
                 wi=jnp.transpose(gi, (1, 0, 2)).reshape(D, -1), wo=go.reshape(-1, D),
                 wdw=jnp.transpose(lands[-1], (1, 0, 2)).reshape(CONV_W, CH))
        W.append(w)
        x0 = xs
        x1, gu1 = _ffn_fwd(x0, row(norm_ffn1, l) + tok[0, 0], w["f1i"], w["f1o"])
        qkv, u = _mixproj_fwd(x1, row(norm_mix, l), w["wi"])
        attn = _attn_fwd(row(sinks, l), qkv)
        conv, ypre = _conv_fwd(u, w["wdw"], row(b_dw, l), row(conv_ln_g, l), row(conv_ln_b, l))
        x2 = _mixout_fwd(x1, attn, conv, w["wo"])
        xs, gu2 = _ffn_fwd(x2, row(norm_ffn2, l), w["f2i"], w["f2o"])
        saved.append((x0, gu1, x1, qkv, u, attn, conv, ypre, x2, gu2))
        after = [xs]

    loss_part, dx, d_final = _loss_head(xs, final_norm.reshape(1, D), tgt)
    loss = lax.psum(loss_part[0, 0], ("x", "y", "c"))

    bufs = [[lax.empty(w_.shape, F32) for _ in range(4)] for w_ in big_w]
    d_n1, d_nm, d_n2 = [None] * L, [None] * L, [None] * L
    d_sk, d_bdw, d_lg, d_lb, d_wdw = [None] * L, [None] * L, [None] * L, [None] * L, [None] * L
    NB = len(big_w)

    def reduce_start(sib_started, after):
        gs, sibs, _ = _xchg_wait("sib_wait", sib_started, NB, _sib_plan, after, sibling=True)
        parts = [_sum_halves(cidx, g, s_) for g, s_ in zip(gs, sibs)]
        lands = [own_slot(lax.dynamic_index_in_dim(p, chip, 0, keepdims=False)) for p in parts]
        return _xchg_start("rs_start", parts, lands, _rs_plan, parts[0])

    def finish(l, rs_started, after):
        _, qs, _ = _xchg_wait("rs_wait", rs_started, NB, _rs_plan, after)
        q_sib = _rs_share(qs)
        for t in range(NB):
            bufs[t] = _adamw_layer(cidx, qs[t], q_sib[t], big_w[t], big_m[t], big_v[t], bufs[t], l)

    sib_pending, rs_list = None, []
    tok = zero_tok
    for l in reversed(range(L)):
        w = W[l]
        x0, gu1, x1, qkv, u, attn, conv, ypre, x2, gu2 = saved[l]
        dx, d_n2[l], hb, dgu, a, dyb = _ffn_bwd(dx, x2, row(norm_ffn2, l), gu2, w["f2i"], w["f2o"], tok)
        g_f2i, g_f2o = _wgrad_ffn_in(hb, dgu), _wgrad_ffn_out(a, dyb)
        lg_row = row(conv_ln_g, l)
        if sib_pending is not None:
            rs_started = reduce_start(sib_pending[1], after=[g_f2o])
            rs_list.append((sib_pending[0], rs_started))
            lg_row = lg_row + rs_started[-1][0, 0]
        dyb, dattn, dconv = _mixout_bwd(dx, w["wo"])
        g_wo = jnp.concatenate([_wgrad_plain(attn, dyb), _wgrad_plain(conv, dyb)], axis=0).reshape(4, -1, D)
        du, dwdw, dvec = _conv_bwd(dconv, ypre, u, w["wdw"], lg_row, row(conv_ln_b, l))
        d_wdw[l], d_bdw[l], d_lg[l], d_lb[l] = dwdw[:CONV_W], dvec[0], dvec[1], dvec[2]
        dq, dkv, dsk = _attn_bwd(row(sinks, l), qkv, dattn)
        d_sk[l] = dsk[:, 0]
        wi = w["wi"]
        dx, d_nm[l], hb = _mix_rms_bwd(dx, x1, row(norm_mix, l), [dq, dkv, du],
                                       [wi[:, :ATTN_W], wi[:, ATTN_W:QKV], wi[:, QKV:]])
        gwi = jnp.concatenate([_wgrad_plain(hb, dq), _wgrad_plain(hb, dkv), _wgrad_plain(hb, du)], axis=1)
        g_wi = jnp.transpose(gwi.reshape(D, 4, -1), (1, 0, 2))
        dx, d_n1[l], hb, dgu, a, dyb = _ffn_bwd(dx, x0, row(norm_ffn1, l), gu1, w["f1i"], w["f1o"], tok)
        g_f1i, g_f1o = _wgrad_ffn_in(hb, dgu), _wgrad_ffn_out(a, dyb)
        gs = [g_f1i, g_f1o, g_wi, g_wo, g_f2i, g_f2o]
        sib_started = _xchg_start("sib_start", gs, [lax.empty((4, g.shape[1] // 2, g.shape[2]), F32) for g in gs],
                                  _sib_plan, gs[0], sibling=True)
        tok = sib_started[-1]
        sib_pending = (l, sib_started)
    grad_x = dx.reshape(x.shape)
    rs_started = reduce_start(sib_pending[1], after=[tok])
    rs_list.append((sib_pending[0], rs_started))

    small_g = [jnp.concatenate(d, axis=0) for d in (d_n1, d_nm, d_n2)] + [d_final, jnp.stack(d_sk)] + \
              [jnp.stack(d) for d in (d_bdw, d_lg, d_lb, d_wdw)]
    small_sum = _unpack(_small_allreduce(_pack(small_g)[0] + rs_started[-1][0, 0]), small_g)
    g_wdw = lax.dynamic_slice_in_dim(small_sum[8], chip * w_dw.shape[2], w_dw.shape[2], axis=2)
    small_g = [small_sum[0], small_sum[1], small_sum[2], small_sum[3].reshape(D), small_sum[4],
               small_sum[5], small_sum[6], small_sum[7], g_wdw]
    small_w = (norm_ffn1, norm_mix, norm_ffn2, final_norm, sinks, b_dw, conv_ln_g, conv_ln_b, w_dw)
    small_m = (m_norm_ffn1, m_norm_mix, m_norm_ffn2, m_final_norm, m_sinks, m_b_dw, m_conv_ln_g, m_conv_ln_b, m_w_dw)
    small_v = (v_norm_ffn1, v_norm_mix, v_norm_ffn2, v_final_norm, v_sinks, v_b_dw, v_conv_ln_g, v_conv_ln_b, v_w_dw)
    upd = _adamw(_pack(small_g), _pack(small_w), _pack(small_m), _pack(small_v))
    small_upd = [_unpack(u_, small_w) for u_ in upd]
    after = upd[0]
    for l, st in rs_list:
        finish(l, st, after=[after])
        after = bufs[0][0]

    order = ("norm_ffn1", "w_ffn1_in", "w_ffn1_out", "norm_mix", "w_in", "sinks", "w_dw", "b_dw", "conv_ln_g",
             "conv_ln_b", "w_out", "norm_ffn2", "w_ffn2_in", "w_ffn2_out", "final_norm")
    small_names = ("norm_ffn1", "norm_mix", "norm_ffn2", "final_norm", "sinks", "b_dw", "conv_ln_g", "conv_ln_b", "w_dw")
    big_names = ("w_ffn1_in", "w_ffn1_out", "w_in", "w_out", "w_ffn2_in", "w_ffn2_out")
    grads, deltas, new_m, new_v = {}, {}, {}, {}
    for i, nme in enumerate(small_names):
        grads[nme], deltas[nme], new_m[nme], new_v[nme] = small_g[i], small_upd[0][i], small_upd[1][i], small_upd[2][i]
    for i, nme in enumerate(big_names):
        grads[nme], deltas[nme], new_m[nme], new_v[nme] = bufs[i]
    return (loss, grad_x, *[grads[n] for n in order], *[deltas[n] for n in order],
            *[new_m[n] for n in order], *[new_v[n] for n in order])
```

```python
import functools

import jax
import jax.numpy as jnp
from jax import lax
from jax.experimental import pallas as pl
from jax.experimental.pallas import tpu as pltpu

F32, BF16 = jnp.float32, jnp.bfloat16
EPS = 1e-6
NEG_INF = -1e30
HEAD_DIM = 64
N_HEADS = 8
N_KV = 2
GROUP = N_HEADS // N_KV
WINDOW = 128
ATTN_W = N_HEADS * HEAD_DIM
KV_W = N_KV * HEAD_DIM
CONV_W = 31
HALO = 32
CONV_ROWS = 32
SCALE = 1.0 / 8.0
ADAM_LR, ADAM_B1, ADAM_B2, ADAM_EPS, ADAM_WD, ADAM_STEP = 0.001, 0.9, 0.999, 1e-08, 0.01, 10
TM = 512
TM_FFN_BWD = 256
LANES = 128
VMEM_LIMIT = 52 * 1024 * 1024
MESH = pl.DeviceIdType.MESH
ANY = pl.BlockSpec(memory_space=pl.ANY)
HBM = pl.BlockSpec(memory_space=pltpu.HBM)
SEM = pl.BlockSpec(memory_space=pltpu.SEMAPHORE)
VMEM = pl.BlockSpec(memory_space=pltpu.VMEM)
EFFECT = pltpu.SideEffectType.DATAFLOW_SIDE_EFFECTING
TOKEN = jax.ShapeDtypeStruct((8, LANES), F32)


def _cp(n):
    return pltpu.CompilerParams(dimension_semantics=("arbitrary",) * n, vmem_limit_bytes=VMEM_LIMIT)


def _dot(a, b):
    return jnp.dot(a, b, preferred_element_type=F32)


def _dot_nt(a, b):
    return lax.dot_general(a, b, (((1,), (1,)), ((), ())), preferred_element_type=F32)


def _dot_tn(a, b):
    return lax.dot_general(a, b, (((0,), (0,)), ((), ())), preferred_element_type=F32)


def _sigmoid(v):
    return 1.0 / (1.0 + jnp.exp(-v))


def _place():
    x, y, c = lax.axis_index("x"), lax.axis_index("y"), lax.axis_index("c")
    chips = [(1 - x, y), (x, 1 - y), (1 - x, 1 - y)]
    return x, y, c, chips


def _rcopy(src, dst, send_sems, recv_sems, k, dev):
    return pltpu.make_async_remote_copy(src_ref=src, dst_ref=dst, send_sem=send_sems.at[k],
                                        recv_sem=recv_sems.at[k], device_id=dev, device_id_type=MESH)


def _hbm(a):
    return pltpu.with_memory_space_constraint(a, pltpu.HBM)


def _targets(sibling):
    x, y, c, chips = _place()
    if sibling:
        return 2 * x + y, c, [((x, y, 1 - c), 2 * x + y)]
    return 2 * x + y, c, [((px, py, c), 2 * px + py) for px, py in chips]


def _xchg_start(name, srcs, lands, plan, dep, sibling=False):
    n = len(srcs)
    npeer = 1 if sibling else 3

    def body(*refs):
        src, land = refs[:n], refs[n:2 * n]
        send_sems, recv_sems, token = refs[2 * n + 1], refs[2 * n + 2], refs[-1]
        b, c, peers = _targets(sibling)
        for t in range(n):
            for j, (dev, pb) in enumerate(peers):
                s, d, _ = plan(src[t], land[t], t, b, c, pb)
                _rcopy(s, d, send_sems, recv_sems, npeer * t + j, dev).start()
        token[...] = jnp.zeros_like(token)

    arrs = list(srcs) + list(lands)
    return pl.pallas_call(
        body, name=name,
        out_shape=(pltpu.SemaphoreType.DMA((npeer * n,)), pltpu.SemaphoreType.DMA((npeer * n,)),
                   *[pltpu.HBM(a.shape, a.dtype) for a in arrs], TOKEN),
        in_specs=[HBM] * (2 * n) + [ANY], out_specs=(SEM, SEM, *[HBM] * (2 * n), VMEM),
        input_output_aliases={i: 2 + i for i in range(2 * n)},
        compiler_params=pltpu.CompilerParams(has_side_effects=EFFECT),
    )(*[_hbm(a) for a in arrs], dep)


def _xchg_wait(name, started, n, plan, after, sibling=False):
    send_sems, recv_sems, thru = started[0], started[1], started[2:2 + 2 * n]
    npeer = 1 if sibling else 3

    def body(*refs):
        src, land = refs[:n], refs[n:2 * n]
        send_sems, recv_sems, token = refs[2 * n], refs[2 * n + 1], refs[-1]
        b, c, peers = _targets(sibling)
        for t in range(n):
            for j, (dev, pb) in enumerate(peers):
                s, _, a = plan(src[t], land[t], t, b, c, pb)
                cp = _rcopy(s, a, send_sems, recv_sems, npeer * t + j, dev)
                cp.wait_send()
                cp.wait_recv()
        token[...] = jnp.zeros_like(token)

    out = pl.pallas_call(
        body, name=name,
        out_shape=(*[pltpu.HBM(a.shape, a.dtype) for a in thru], TOKEN),
        in_specs=[HBM] * (2 * n) + [SEM, SEM] + [ANY] * len(after), out_specs=(*[HBM] * (2 * n), VMEM),
        input_output_aliases={i: i for i in range(2 * n)},
        compiler_params=pltpu.CompilerParams(has_side_effects=EFFECT),
    )(*thru, send_sems, recv_sems, *after)
    return out[:n], out[n:2 * n], out[-1]


def _half(ref_rows, which):
    h = ref_rows // 2
    return pl.ds(which * h, h)


def _gather_plan(src, land, t, b, c, pb):
    if len(src.shape) == 2 and src.shape[0] % 2 == 0:
        hs = _half(src.shape[0], c)
        return src.at[hs], land.at[b, hs], land.at[pb, hs]
    return src, land.at[b], land.at[pb]


def _gather_share(lands):
    n = len(lands)

    def body(*refs):
        land_in, land = refs[:n], refs[n:2 * n]
        send_sems, recv_sems = refs[2 * n:]
        x, y, c, chips = _place()
        sib = (x, y, 1 - c)
        sends = []
        for t in range(n):
            for j, (px, py) in enumerate(chips):
                hs = _half(lands[t].shape[1], c)
                cp = _rcopy(land_in[t].at[2 * px + py, hs], land[t].at[2 * px + py, hs], send_sems, recv_sems, 3 * t + j, sib)
                cp.start()
                sends.append(cp)
        for t in range(n):
            for j, (px, py) in enumerate(chips):
                other = land[t].at[2 * px + py, _half(lands[t].shape[1], 1 - c)]
                _rcopy(other, other, send_sems, recv_sems, 3 * t + j, sib).wait_recv()
        for cp in sends:
            cp.wait_send()

    return pl.pallas_call(
        body, name="gather_share", out_shape=[jax.ShapeDtypeStruct(a.shape, a.dtype) for a in lands],
        in_specs=[ANY] * n, out_specs=[ANY] * n, input_output_aliases={t: t for t in range(n)},
        scratch_shapes=[pltpu.SemaphoreType.DMA((3 * n,)), pltpu.SemaphoreType.DMA((3 * n,))],
    )(*lands)


def _rs_plan(src, land, t, b, c, pb):
    return src.at[pb], land.at[b], land.at[pb]


def _sib_plan(src, land, t, b, c, pb):
    return src.at[:, _half(src.shape[1], 1 - c), :], land, land


def _rows_block(h, cap=512):
    for rb in range(min(h, cap) // 16 * 16, 0, -16):
        if h % rb == 0:
            return rb
    return h


def _sum_halves(cidx, g, s):
    _, R, C = g.shape
    rb = _rows_block(R // 2)
    nr = R // 2 // rb

    def body(c_ref, g_ref, s_ref, o_ref):
        o_ref[...] = (g_ref[...] + s_ref[...]).astype(BF16)

    blk = (None, rb, C)
    return pl.pallas_call(
        body, name="sum_halves", out_shape=jax.ShapeDtypeStruct(s.shape, BF16),
        grid_spec=pltpu.PrefetchScalarGridSpec(
            num_scalar_prefetch=1, grid=(4, nr),
            in_specs=[pl.BlockSpec(blk, lambda p, i, c: (p, c[0] * nr + i, 0)),
                      pl.BlockSpec(blk, lambda p, i, c: (p, i, 0))],
            out_specs=pl.BlockSpec(blk, lambda p, i, c: (p, i, 0))),
        compiler_params=_cp(2),
    )(cidx, g, s)


def _rs_share(qs):
    nt = len(qs)

    def body(*refs):
        q_refs, qsib = refs[:nt], refs[nt:2 * nt]
        send_sems, recv_sems = refs[2 * nt:]
        x, y, c, _ = _place()
        cps = []
        for t in range(nt):
            cp = _rcopy(q_refs[t], qsib[t], send_sems, recv_sems, t, (x, y, 1 - c))
            cp.start()
            cps.append(cp)
        for cp in cps:
            cp.wait()

    return pl.pallas_call(
        body, name="rs_share", in_specs=[ANY] * nt, out_specs=[ANY] * nt,
        out_shape=[jax.ShapeDtypeStruct(q.shape, q.dtype) for q in qs],
        scratch_shapes=[pltpu.SemaphoreType.DMA((nt,)), pltpu.SemaphoreType.DMA((nt,))],
    )(*qs)


def _adam_update(gg, w, m, v):
    m2 = ADAM_B1 * m + (1.0 - ADAM_B1) * gg
    v2 = ADAM_B2 * v + (1.0 - ADAM_B2) * (gg * gg)
    mh = m2 / (1.0 - ADAM_B1 ** ADAM_STEP)
    vh = v2 / (1.0 - ADAM_B2 ** ADAM_STEP)
    return -ADAM_LR * (mh / (jnp.sqrt(vh) + ADAM_EPS) + ADAM_WD * w), m2, v2


def _adamw_layer(cidx, q_own, q_sib, w, m, v, bufs, l):
    L, R, C = w.shape
    h = R // 2
    rb = _rows_block(h, 256)
    nr = h // rb

    def body(c_ref, qo_ref, qs_ref, w_ref, m_ref, v_ref, *rest):
        g_ref, d_ref, mo_ref, vo_ref = rest[-4:]
        own = pl.program_id(0) == c_ref[0]
        gg = jnp.zeros((rb, C), F32)
        for s in range(4):
            gg = gg + jnp.where(own, qo_ref[s], qs_ref[s]).astype(F32)
        g_ref[...] = gg
        d_ref[...], mo_ref[...], vo_ref[...] = _adam_update(gg, w_ref[...], m_ref[...], v_ref[...])

    qspec = pl.BlockSpec((4, rb, C), lambda hh, i, c: (0, i, 0))
    wspec = pl.BlockSpec((None, rb, C), lambda hh, i, c: (l, hh * nr + i, 0))
    return pl.pallas_call(
        body, name="adamw_layer", out_shape=[jax.ShapeDtypeStruct(w.shape, F32)] * 4,
        grid_spec=pltpu.PrefetchScalarGridSpec(
            num_scalar_prefetch=1, grid=(2, nr),
            in_specs=[qspec, qspec, wspec, wspec, wspec] + [ANY] * 4, out_specs=[wspec] * 4),
        input_output_aliases={6 + k: k for k in range(4)},
        compiler_params=_cp(2),
    )(cidx, q_own, q_sib, w, m, v, *bufs)


def _adamw(g, w, m, v):
    L, R, C = g.shape
    rb = _rows_block(R)

    def body(g_ref, w_ref, m_ref, v_ref, d_ref, mo_ref, vo_ref):
        d_ref[...], mo_ref[...], vo_ref[...] = _adam_update(g_ref[...], w_ref[...], m_ref[...], v_ref[...])

    spec = pl.BlockSpec((None, rb, C), lambda l, i: (l, i, 0))
    return pl.pallas_call(
        body, name="adamw", grid=(L, R // rb), in_specs=[spec] * 4, out_specs=[spec] * 3,
        out_shape=[jax.ShapeDtypeStruct(g.shape, F32)] * 3, compiler_params=_cp(2),
    )(g, w, m, v)


def _small_allreduce(p, dep):
    R = p.shape[0]

    def body(p_ref, dep_ref, o_ref, buf_ref, send_sems, recv_sems):
        x, y, c, _ = _place()
        me = 4 * x + 2 * y + c
        flip = lambda a, f: 1 - a if f else a
        buf_ref[me] = p_ref[...]
        peers = [(flip(x, k >> 2 & 1), flip(y, k >> 1 & 1), flip(c, k & 1)) for k in range(1, 8)]
        cps = []
        for k, dev in enumerate(peers):
            cp = _rcopy(p_ref, buf_ref.at[me], send_sems, recv_sems, k, dev)
            cp.start()
            cps.append(cp)
        for k, (px, py, pc) in enumerate(peers):
            slot = buf_ref.at[4 * px + 2 * py + pc]
            _rcopy(slot, slot, send_sems, recv_sems, k, (px, py, pc)).wait_recv()
        for cp in cps:
            cp.wait_send()
        acc = buf_ref[0]
        for s in range(1, 8):
            acc = acc + buf_ref[s]
        o_ref[...] = acc

    return pl.pallas_call(
        body, name="small_allreduce", in_specs=[VMEM, ANY], out_specs=VMEM,
        out_shape=jax.ShapeDtypeStruct(p.shape, F32),
        scratch_shapes=[pltpu.VMEM((8, R, LANES), F32), pltpu.SemaphoreType.DMA((7,)), pltpu.SemaphoreType.DMA((7,))],
    )(p, dep)


def _rms(xf, g):
    r = lax.rsqrt(jnp.mean(xf * xf, axis=-1, keepdims=True) + EPS)
    return xf * r, r


def _ffn_fwd(x, g, win, wout):
    T, D = x.shape
    FB = win.shape[2]
    tm = min(TM, T)

    def body(x_ref, g_ref, wg_ref, wu_ref, wo_ref, xo_ref, gu_ref, h_ref, acc_ref):
        j = pl.program_id(1)

        @pl.when(j == 0)
        def _():
            xh, _ = _rms(x_ref[...], None)
            h_ref[...] = (xh * g_ref[...]).astype(BF16)
            acc_ref[...] = jnp.zeros_like(acc_ref)

        h = h_ref[...]
        gate = _dot(h, wg_ref[...])
        up = _dot(h, wu_ref[...])
        gu_ref[0] = gate.astype(BF16)
        gu_ref[1] = up.astype(BF16)
        a = (gate * _sigmoid(gate) * up).astype(BF16)
        acc_ref[...] += _dot(a, wo_ref[...])

        @pl.when(j == 1)
        def _():
            xo_ref[...] = x_ref[...] + 0.5 * acc_ref[...]

    return pl.pallas_call(
        body, name="ffn_fwd", grid=(T // tm, 2),
        in_specs=[pl.BlockSpec((tm, D), lambda i, j: (i, 0)),
                  pl.BlockSpec((1, D), lambda i, j: (0, 0)),
                  pl.BlockSpec((None, D, FB), lambda i, j: (j, 0, 0)),
                  pl.BlockSpec((None, D, FB), lambda i, j: (j + 2, 0, 0)),
                  pl.BlockSpec((FB, D), lambda i, j: (j, 0))],
        out_specs=[pl.BlockSpec((tm, D), lambda i, j: (i, 0)),
                   pl.BlockSpec((2, tm, FB), lambda i, j: (0, i, j))],
        out_shape=[jax.ShapeDtypeStruct((T, D), F32), jax.ShapeDtypeStruct((2, T, 2 * FB), BF16)],
        scratch_shapes=[pltpu.VMEM((tm, D), BF16), pltpu.VMEM((tm, D), F32)],
        compiler_params=_cp(2),
    )(x, g, win, win, wout)


def _mixproj_fwd(x, g, w):
    T, D = x.shape
    W = w.shape[1]
    QKV = ATTN_W + 2 * KV_W
    tm = min(TM, T)

    def body(x_ref, g_ref, w_ref, qkv_ref, u_ref):
        xh, _ = _rms(x_ref[...], None)
        h = (xh * g_ref[...]).astype(BF16)
        qkv_ref[...] = _dot(h, w_ref[:, :QKV]).astype(BF16)
        u_ref[...] = _dot(h, w_ref[:, QKV:])

    return pl.pallas_call(
        body, name="mixproj_fwd", grid=(T // tm,),
        in_specs=[pl.BlockSpec((tm, D), lambda i: (i, 0)), pl.BlockSpec((1, D), lambda i: (0, 0)),
                  pl.BlockSpec((D, W), lambda i: (0, 0))],
        out_specs=[pl.BlockSpec((tm, QKV), lambda i: (i, 0)), pl.BlockSpec((tm, W - QKV), lambda i: (i, 0))],
        out_shape=[jax.ShapeDtypeStruct((T, QKV), BF16), jax.ShapeDtypeStruct((T, W - QKV), F32)],
        compiler_params=_cp(1),
    )(x, g, w)


def _attn_bias_table():
    rows, cols = GROUP * WINDOW, 2 * WINDOW
    row = lax.broadcasted_iota(jnp.int32, (N_KV, rows, cols), 1)
    col = lax.broadcasted_iota(jnp.int32, (N_KV, rows, cols), 2)
    head = GROUP * lax.broadcasted_iota(jnp.int32, (N_KV, rows, cols), 0) + (row >> 7)
    dist = (row & (WINDOW - 1)) + WINDOW - col
    slope = jnp.exp2(-(head + 1).astype(F32))
    return jnp.where((dist >= 0) & (dist < WINDOW), -slope * dist.astype(F32), NEG_INF)


def _first_block_mask(n):
    col = lax.broadcasted_iota(jnp.int32, (GROUP * WINDOW, 2 * WINDOW), 1)
    return (n > 0) | (col >= WINDOW)


def _sink_col(sink_ref, g):
    hi = lax.broadcasted_iota(jnp.int32, (GROUP * WINDOW, 1), 0) >> 7
    col = jnp.zeros((GROUP * WINDOW, 1), F32)
    for i in range(GROUP):
        col = jnp.where(hi == i, sink_ref[0, GROUP * g + i], col)
    return col


def _stack_heads(ref, g):
    return jnp.concatenate([ref[:, (GROUP * g + i) * HEAD_DIM:(GROUP * g + i + 1) * HEAD_DIM]
                            for i in range(GROUP)], axis=0)


def _band(kvp_ref, kvc_ref, off):
    return jnp.concatenate([kvp_ref[:, off:off + HEAD_DIM], kvc_ref[:, off:off + HEAD_DIM]], axis=0)


def _attn_probs(qs, k, bias, seen, sink):
    s = jnp.where(seen, _dot_nt(qs, k) * SCALE + bias, NEG_INF)
    m = jnp.maximum(jnp.max(s, axis=-1, keepdims=True), sink)
    p = jnp.exp(s - m)
    es = jnp.exp(sink - m)
    den = jnp.sum(p, axis=-1, keepdims=True) + es
    return p / den, es / den


def _attn_fwd(sinks, tab, qkv):
    T = qkv.shape[0]
    nb = T // WINDOW

    def body(sink_ref, tab_ref, q_ref, kvp_ref, kvc_ref, o_ref):
        seen = _first_block_mask(pl.program_id(0))
        for g in range(N_KV):
            qs = _stack_heads(q_ref, g)
            k = _band(kvp_ref, kvc_ref, g * HEAD_DIM)
            v = _band(kvp_ref, kvc_ref, KV_W + g * HEAD_DIM)
            p, _ = _attn_probs(qs, k, tab_ref[g], seen, _sink_col(sink_ref, g))
            o = _dot(p.astype(BF16), v)
            for i in range(GROUP):
                h = GROUP * g + i
                o_ref[:, h * HEAD_DIM:(h + 1) * HEAD_DIM] = o[i * WINDOW:(i + 1) * WINDOW].astype(BF16)

    return pl.pallas_call(
        body, name="attn_fwd", grid=(nb,),
        in_specs=[pl.BlockSpec(memory_space=pltpu.SMEM),
                  pl.BlockSpec(tab.shape, lambda n: (0, 0, 0)),
                  pl.BlockSpec((WINDOW, ATTN_W), lambda n: (n, 0)),
                  pl.BlockSpec((WINDOW, 2 * KV_W), lambda n: (jnp.maximum(n - 1, 0), 2)),
                  pl.BlockSpec((WINDOW, 2 * KV_W), lambda n: (n, 2))],
        out_specs=pl.BlockSpec((WINDOW, ATTN_W), lambda n: (n, 0)),
        out_shape=jax.ShapeDtypeStruct((T, ATTN_W), BF16),
        compiler_params=_cp(1),
    )(sinks, tab, qkv, qkv, qkv)


def _shift_copies(src_ref, dst_ref, n):
    for b in range(1, 8):
        dst_ref[b - 1] = src_ref[b:b + n, :]


def _tap(src_ref, sh_ref, s, c0):
    a, b = divmod(s, 8)
    start = pl.multiple_of(c0 + 8 * a, 8)
    if b == 0:
        return src_ref[pl.ds(start, CONV_ROWS), :]
    return sh_ref[b - 1, pl.ds(start, CONV_ROWS), :]


def _glu_rows(u, ch):
    return u[:, :ch] * _sigmoid(u[:, ch:])


def _fill_z(zs_ref, zsh_ref, uc_ref, up_ref, i, ch, n):
    zs_ref[0:HALO] = jnp.where(i > 0, _glu_rows(up_ref[...], ch), 0.0)
    zs_ref[HALO:] = _glu_rows(uc_ref[...], ch)
    _shift_copies(zs_ref, zsh_ref, n - 8)


def _conv_fwd(u, w, b, lg, lb):
    T = u.shape[0]
    CH = u.shape[1] // 2
    tm = min(TM, T)
    n = tm + HALO
    hb = tm // HALO

    def body(uc_ref, up_ref, w_ref, b_ref, lg_ref, lb_ref, conv_ref, ypre_ref, zs_ref, zsh_ref):
        i = pl.program_id(0)
        _fill_z(zs_ref, zsh_ref, uc_ref, up_ref, i, CH, n)
        bias = b_ref[...]

        def chunk(ci, carry):
            c0 = pl.multiple_of(ci * CONV_ROWS, CONV_ROWS)
            acc = jnp.broadcast_to(bias, (CONV_ROWS, CH))
            for k in range(CONV_W):
                acc = acc + w_ref[k:k + 1, :] * _tap(zs_ref, zsh_ref, HALO - (CONV_W - 1) + k, c0)
            ypre_ref[pl.ds(c0, CONV_ROWS), :] = acc
            return carry

        lax.fori_loop(0, tm // CONV_ROWS, chunk, 0)
        y = ypre_ref[...]
        mu = jnp.mean(y, axis=-1, keepdims=True)
        d = y - mu
        var = jnp.mean(d * d, axis=-1, keepdims=True)
        o = d * lax.rsqrt(var + EPS) * lg_ref[...] + lb_ref[...]
        conv_ref[...] = (o * _sigmoid(o)).astype(BF16)

    vec = pl.BlockSpec((1, CH), lambda i: (0, 0))
    return pl.pallas_call(
        body, name="conv_fwd", grid=(T // tm,),
        in_specs=[pl.BlockSpec((tm, 2 * CH), lambda i: (i, 0)),
                  pl.BlockSpec((HALO, 2 * CH), lambda i: (jnp.maximum(i * hb - 1, 0), 0)),
                  pl.BlockSpec((CONV_W, CH), lambda i: (0, 0)), vec, vec, vec],
        out_specs=[pl.BlockSpec((tm, CH), lambda i: (i, 0)), pl.BlockSpec((tm, CH), lambda i: (i, 0))],
        out_shape=[jax.ShapeDtypeStruct((T, CH), BF16), jax.ShapeDtypeStruct((T, CH), F32)],
        scratch_shapes=[pltpu.VMEM((n, CH), F32), pltpu.VMEM((7, n - 8, CH), F32)],
        compiler_params=_cp(1),
    )(u, u, w, b, lg, lb)


def _mixout_fwd(x, attn, conv, wo):
    T, D = x.shape
    tm = min(TM, T)
    A = attn.shape[1]

    def body(x_ref, a_ref, c_ref, w_ref, xo_ref):
        xo_ref[...] = x_ref[...] + _dot(a_ref[...], w_ref[:A, :]) + _dot(c_ref[...], w_ref[A:, :])

    return pl.pallas_call(
        body, name="mixout_fwd", grid=(T // tm,),
        in_specs=[pl.BlockSpec((tm, D), lambda i: (i, 0)), pl.BlockSpec((tm, A), lambda i: (i, 0)),
                  pl.BlockSpec((tm, conv.shape[1]), lambda i: (i, 0)), pl.BlockSpec(wo.shape, lambda i: (0, 0))],
        out_specs=pl.BlockSpec((tm, D), lambda i: (i, 0)),
        out_shape=jax.ShapeDtypeStruct((T, D), F32),
        compiler_params=_cp(1),
    )(x, attn, conv, wo)


def _rms_bwd_rows(dh, xf, g):
    xh, r = _rms(xf, None)
    dxn = dh * g
    dx = r * (dxn - xh * jnp.mean(dxn * xh, axis=-1, keepdims=True))
    return dx, jnp.sum(dh * xh, axis=0, keepdims=True), xh * g


def _loss_head(x, g, tgt):
    T, D = x.shape
    tm = min(TM, T)

    def body(x_ref, g_ref, t_ref, loss_ref, dx_ref, dg_ref):
        @pl.when(pl.program_id(0) == 0)
        def _():
            loss_ref[...] = jnp.zeros_like(loss_ref)
            dg_ref[...] = jnp.zeros_like(dg_ref)

        xf = x_ref[...]
        g = g_ref[...]
        xh, _ = _rms(xf, None)
        e = xh * g - t_ref[...]
        loss_ref[...] += 0.5 * jnp.sum(jnp.mean(e * e, axis=-1, keepdims=True), axis=0, keepdims=True)
        dx, dg, _ = _rms_bwd_rows(e * (1.0 / D), xf, g)
        dx_ref[...] = dx
        dg_ref[...] += dg

    return pl.pallas_call(
        body, name="loss_head", grid=(T // tm,),
        in_specs=[pl.BlockSpec((tm, D), lambda i: (i, 0)), pl.BlockSpec((1, D), lambda i: (0, 0)),
                  pl.BlockSpec((tm, D), lambda i: (i, 0))],
        out_specs=[pl.BlockSpec((1, 1), lambda i: (0, 0)), pl.BlockSpec((tm, D), lambda i: (i, 0)),
                   pl.BlockSpec((1, D), lambda i: (0, 0))],
        out_shape=[jax.ShapeDtypeStruct((1, 1), F32), jax.ShapeDtypeStruct((T, D), F32),
                   jax.ShapeDtypeStruct((1, D), F32)],
        compiler_params=_cp(1),
    )(x, g, tgt)


def _lane_chunks(n):
    lo = (n // LANES + 1) // 2 * LANES
    return ((0, lo), (lo, n - lo))


def _ffn_bwd(dxo, x, g, gu, win, wout, dep):
    T, D = x.shape
    FB = win.shape[2]
    tm = min(TM_FFN_BWD, T)

    def body(dxo_ref, x_ref, g_ref, gu_ref, win_hbm, wout_hbm, dep_ref,
             dxi_ref, dg_ref, hb_ref, dgu_ref, a_ref, dyb_ref, win_v, wout_v, sems):
        @pl.when(pl.program_id(0) == 0)
        def _():
            loads = [pltpu.make_async_copy(win_hbm.at[k], win_v.at[k], sems.at[k]) for k in range(4)]
            loads += [pltpu.make_async_copy(wout_hbm.at[pl.ds(k * FB, FB)], wout_v.at[pl.ds(k * FB, FB)], sems.at[4 + k])
                      for k in range(2)]
            for cp in loads:
                cp.start()
            for cp in loads:
                cp.wait()
            dg_ref[...] = jnp.zeros_like(dg_ref)

        dyb = (0.5 * dxo_ref[...]).astype(BF16)
        dyb_ref[...] = dyb
        dh = jnp.zeros((tm, D), F32)
        for blk in range(2):
            for lo, sz in _lane_chunks(FB):
                cols = pl.ds(blk * FB + lo, sz)
                da = _dot_nt(dyb, wout_v[cols, :])
                gate = gu_ref[0, :, cols].astype(F32)
                up = gu_ref[1, :, cols].astype(F32)
                sg = pl.reciprocal(1.0 + jnp.exp(-gate), approx=True)
                s = gate * sg
                a_ref[:, cols] = (s * up).astype(BF16)
                dgate = (da * up * (sg * (1.0 + gate * (1.0 - sg)))).astype(BF16)
                dup = (da * s).astype(BF16)
                dgu_ref[0, :, cols] = dgate
                dgu_ref[1, :, cols] = dup
                dh = dh + _dot_nt(dgate, win_v[blk, :, pl.ds(lo, sz)]) + _dot_nt(dup, win_v[2 + blk, :, pl.ds(lo, sz)])
        dx, dg, h = _rms_bwd_rows(dh, x_ref[...], g_ref[...])
        dxi_ref[...] = dxo_ref[...] + dx
        dg_ref[...] += dg
        hb_ref[...] = h.astype(BF16)

    row = pl.BlockSpec((tm, D), lambda i: (i, 0))
    act = pl.BlockSpec((2, tm, 2 * FB), lambda i: (0, i, 0))
    return pl.pallas_call(
        body, name="ffn_bwd", grid=(T // tm,),
        in_specs=[row, row, pl.BlockSpec((1, D), lambda i: (0, 0)), act, ANY, ANY, ANY],
        out_specs=[row, pl.BlockSpec((1, D), lambda i: (0, 0)), row, act,
                   pl.BlockSpec((tm, 2 * FB), lambda i: (i, 0)), row],
        out_shape=[jax.ShapeDtypeStruct((T, D), F32), jax.ShapeDtypeStruct((1, D), F32),
                   jax.ShapeDtypeStruct((T, D), BF16), jax.ShapeDtypeStruct((2, T, 2 * FB), BF16),
                   jax.ShapeDtypeStruct((T, 2 * FB), BF16), jax.ShapeDtypeStruct((T, D), BF16)],
        scratch_shapes=[pltpu.VMEM(win.shape, BF16), pltpu.VMEM(wout.shape, BF16), pltpu.SemaphoreType.DMA((6,))],
        compiler_params=_cp(1),
    )(dxo, x, g, gu, win, wout, dep)


def _rms_matmul_bwd(name, dxo, x, g, dzs, ws, dz_specs, w_specs, nk):
    T, D = x.shape
    tm = min(TM, T)
    npair = len(dzs)

    def body(*refs):
        dxo_ref, x_ref, g_ref = refs[:3]
        dz_refs, w_refs = refs[3:3 + npair], refs[3 + npair:3 + 2 * npair]
        dxi_ref, dg_ref, hb_ref, acc_ref = refs[3 + 2 * npair:]
        i, k = pl.program_id(0), pl.program_id(1)

        @pl.when(k == 0)
        def _():
            acc_ref[...] = jnp.zeros_like(acc_ref)

        @pl.when((i == 0) & (k == 0))
        def _():
            dg_ref[...] = jnp.zeros_like(dg_ref)

        for p in range(npair):
            acc_ref[...] += _dot_nt(dz_refs[p][...], w_refs[p][...])

        @pl.when(k == nk - 1)
        def _():
            dx, dg, h = _rms_bwd_rows(acc_ref[...], x_ref[...], g_ref[...])
            dxi_ref[...] = dxo_ref[...] + dx
            dg_ref[...] += dg
            hb_ref[...] = h.astype(BF16)

    row = pl.BlockSpec((tm, D), lambda i, k: (i, 0))
    return pl.pallas_call(
        body, name=name, grid=(T // tm, nk),
        in_specs=[row, row, pl.BlockSpec((1, D), lambda i, k: (0, 0))] + list(dz_specs) + list(w_specs),
        out_specs=[row, pl.BlockSpec((1, D), lambda i, k: (0, 0)), row],
        out_shape=[jax.ShapeDtypeStruct((T, D), F32), jax.ShapeDtypeStruct((1, D), F32),
                   jax.ShapeDtypeStruct((T, D), BF16)],
        scratch_shapes=[pltpu.VMEM((tm, D), F32)],
        compiler_params=_cp(2),
    )(dxo, x, g, *dzs, *ws)


def _mix_rms_bwd(dxo, x, g, dzs, ws):
    tm = min(TM, x.shape[0])
    return _rms_matmul_bwd(
        "mix_rms_bwd", dxo, x, g, dzs, ws,
        [pl.BlockSpec((tm, dz.shape[1]), lambda i, k: (i, 0)) for dz in dzs],
        [pl.BlockSpec(w.shape, lambda i, k: (0, 0)) for w in ws], 1)


def _wgrad(name, a, b, a_spec, b_spec, out_shape, out_spec, nblk):
    T = a.shape[0]
    tk = min(TM, T)

    def body(a_ref, b_ref, o_ref):
        @pl.when(pl.program_id(1) == 0)
        def _():
            o_ref[...] = jnp.zeros_like(o_ref)

        o_ref[...] += _dot_tn(a_ref[...], b_ref[...]).reshape(o_ref.shape)

    return pl.pallas_call(
        body, name=name, grid=(nblk, T // tk), in_specs=[a_spec, b_spec], out_specs=out_spec,
        out_shape=jax.ShapeDtypeStruct(out_shape, F32), compiler_params=_cp(2),
    )(a, b)


def _wgrad_ffn_in(hb, dgu):
    T, D = hb.shape
    FB = dgu.shape[2] // 2
    tk = min(TM, T)
    return _wgrad("wgrad_ffn_in", hb, dgu,
                  pl.BlockSpec((tk, D), lambda b, k: (k, 0)),
                  pl.BlockSpec((None, tk, FB), lambda b, k: (b // 2, k, b % 2)),
                  (4, D, FB), pl.BlockSpec((None, D, FB), lambda b, k: (b, 0, 0)), 4)


def _wgrad_ffn_out(a, dyb):
    T, D = dyb.shape
    FB = a.shape[1] // 2
    tk = min(TM, T)
    return _wgrad("wgrad_ffn_out", a, dyb,
                  pl.BlockSpec((tk, FB), lambda b, k: (k, b)),
                  pl.BlockSpec((tk, D), lambda b, k: (k, 0)),
                  (4, FB // 2, D), pl.BlockSpec((2, FB // 2, D), lambda b, k: (b, 0, 0)), 2)


def _wgrad_cat(a_list, b_list):
    T = a_list[0].shape[0]
    tk = min(TM, T)
    na = len(a_list)
    M, N = sum(a.shape[1] for a in a_list), sum(b.shape[1] for b in b_list)

    def body(*refs):
        a_refs, b_refs, o_ref = refs[:na], refs[na:-1], refs[-1]

        @pl.when(pl.program_id(0) == 0)
        def _():
            o_ref[...] = jnp.zeros_like(o_ref)

        r0 = 0
        for a_ref in a_refs:
            c0 = 0
            for b_ref in b_refs:
                m, n = a_ref.shape[1], b_ref.shape[1]
                o_ref[r0:r0 + m, c0:c0 + n] += _dot_tn(a_ref[...], b_ref[...])
                c0 += n
            r0 += a_ref.shape[1]

    return pl.pallas_call(
        body, name="wgrad_cat", grid=(T // tk,),
        in_specs=[pl.BlockSpec((tk, v.shape[1]), lambda k: (k, 0)) for v in list(a_list) + list(b_list)],
        out_specs=pl.BlockSpec((M, N), lambda k: (0, 0)),
        out_shape=jax.ShapeDtypeStruct((M, N), F32), compiler_params=_cp(1),
    )(*a_list, *b_list)


def _mixout_bwd(dxo, wo):
    T, D = dxo.shape
    tm = min(TM, T)
    A = ATTN_W
    C = wo.shape[0] - A

    def body(dxo_ref, w_ref, dyb_ref, da_ref, dc_ref):
        dyb = dxo_ref[...].astype(BF16)
        dyb_ref[...] = dyb
        da_ref[...] = _dot_nt(dyb, w_ref[:A, :]).astype(BF16)
        dc_ref[...] = _dot_nt(dyb, w_ref[A:, :])

    return pl.pallas_call(
        body, name="mixout_bwd", grid=(T // tm,),
        in_specs=[pl.BlockSpec((tm, D), lambda i: (i, 0)), pl.BlockSpec(wo.shape, lambda i: (0, 0))],
        out_specs=[pl.BlockSpec((tm, D), lambda i: (i, 0)), pl.BlockSpec((tm, A), lambda i: (i, 0)),
                   pl.BlockSpec((tm, C), lambda i: (i, 0))],
        out_shape=[jax.ShapeDtypeStruct((T, D), BF16), jax.ShapeDtypeStruct((T, A), BF16),
                   jax.ShapeDtypeStruct((T, C), F32)],
        compiler_params=_cp(1),
    )(dxo, wo)


def _conv_bwd(dconv, ypre, u, w, lg, lb):
    T, CH = dconv.shape
    tm = min(TM, T)
    n = tm + HALO
    hb = tm // HALO
    nt = T // tm
    nchunk = tm // CONV_ROWS

    def body(dc_ref, dcn_ref, yp_ref, ypn_ref, uc_ref, up_ref, w_ref, lg_ref, lb_ref,
             du_ref, dw_ref, dvec_ref, zs_ref, zsh_ref, dy_ref, dysh_ref, dz_ref):
        i = pl.program_id(0)

        @pl.when(i == 0)
        def _():
            dw_ref[...] = jnp.zeros_like(dw_ref)
            dvec_ref[...] = jnp.zeros_like(dvec_ref)

        g, bb = lg_ref[...], lb_ref[...]

        def ln_bwd(dc, yp):
            mu = jnp.mean(yp, axis=-1, keepdims=True)
            d = yp - mu
            rs = lax.rsqrt(jnp.mean(d * d, axis=-1, keepdims=True) + EPS)
            yn = d * rs
            o = yn * g + bb
            sg = _sigmoid(o)
            do = dc * (sg * (1.0 + o * (1.0 - sg)))
            dyn = do * g
            dyp = rs * (dyn - jnp.mean(dyn, axis=-1, keepdims=True)
                        - yn * jnp.mean(dyn * yn, axis=-1, keepdims=True))
            return dyp, do, yn

        dyp, do, yn = ln_bwd(dc_ref[...], yp_ref[...])
        dvec_ref[0:1, :] += jnp.sum(dyp, axis=0, keepdims=True)
        dvec_ref[1:2, :] += jnp.sum(do * yn, axis=0, keepdims=True)
        dvec_ref[2:3, :] += jnp.sum(do, axis=0, keepdims=True)
        dy_ref[0:tm] = dyp
        dyh, _, _ = ln_bwd(dcn_ref[...], ypn_ref[...])
        dy_ref[tm:] = jnp.where(i < nt - 1, dyh, 0.0)
        _shift_copies(dy_ref, dysh_ref, n - 8)
        _fill_z(zs_ref, zsh_ref, uc_ref, up_ref, i, CH, n)

        def chunk(ci, carry):
            c0 = pl.multiple_of(ci * CONV_ROWS, CONV_ROWS)
            acc = jnp.zeros((CONV_ROWS, CH), F32)
            for k in range(CONV_W):
                acc = acc + w_ref[k:k + 1, :] * _tap(dy_ref, dysh_ref, CONV_W - 1 - k, c0)
            dz_ref[pl.ds(c0, CONV_ROWS), :] = acc
            return carry

        lax.fori_loop(0, nchunk, chunk, 0)

        for k in range(CONV_W):
            def red(ci, acc, k=k):
                c0 = pl.multiple_of(ci * CONV_ROWS, CONV_ROWS)
                prod = dy_ref[pl.ds(c0, CONV_ROWS), :] * _tap(zs_ref, zsh_ref, HALO - (CONV_W - 1) + k, c0)
                return acc + jnp.sum(prod.reshape(CONV_ROWS // 8, 8, CH), axis=0)

            acc = lax.fori_loop(0, nchunk, red, jnp.zeros((8, CH), F32))
            dw_ref[k:k + 1, :] += jnp.sum(acc, axis=0, keepdims=True)

        uc = uc_ref[...]
        a = uc[:, :CH]
        sg = _sigmoid(uc[:, CH:])
        dz = dz_ref[...]
        du_ref[:, :CH] = (dz * sg).astype(BF16)
        du_ref[:, CH:] = (dz * a * sg * (1.0 - sg)).astype(BF16)

    cur = lambda c: pl.BlockSpec((tm, c), lambda i: (i, 0))
    nxt = lambda c: pl.BlockSpec((HALO, c), lambda i: (jnp.minimum((i + 1) * hb, T // HALO - 1), 0))
    vec = pl.BlockSpec((1, CH), lambda i: (0, 0))
    return pl.pallas_call(
        body, name="conv_bwd", grid=(nt,),
        in_specs=[cur(CH), nxt(CH), cur(CH), nxt(CH), cur(2 * CH),
                  pl.BlockSpec((HALO, 2 * CH), lambda i: (jnp.maximum(i * hb - 1, 0), 0)),
                  pl.BlockSpec((CONV_W, CH), lambda i: (0, 0)), vec, vec],
        out_specs=[pl.BlockSpec((tm, 2 * CH), lambda i: (i, 0)), pl.BlockSpec((32, CH), lambda i: (0, 0)),
                   pl.BlockSpec((8, CH), lambda i: (0, 0))],
        out_shape=[jax.ShapeDtypeStruct((T, 2 * CH), BF16), jax.ShapeDtypeStruct((32, CH), F32),
                   jax.ShapeDtypeStruct((8, CH), F32)],
        scratch_shapes=[pltpu.VMEM((n, CH), F32), pltpu.VMEM((7, n - 8, CH), F32),
                        pltpu.VMEM((n, CH), F32), pltpu.VMEM((7, n - 8, CH), F32), pltpu.VMEM((tm, CH), F32)],
        compiler_params=_cp(1),
    )(dconv, dconv, ypre, ypre, u, u, w, lg, lb)


def _attn_bwd(sinks, tab, qkv, dattn):
    T = qkv.shape[0]
    nb = T // WINDOW

    def body(sink_ref, tab_ref, q_ref, kvp_ref, kvc_ref, do_ref, dq_ref, dkv_ref, dsk_ref, carry_ref):
        n = pl.program_id(0)

        @pl.when(n == 0)
        def _():
            dsk_ref[...] = jnp.zeros_like(dsk_ref)
            carry_ref[...] = jnp.zeros_like(carry_ref)

        @pl.when(n < nb)
        def _():
            seen = _first_block_mask(n)
            for g in range(N_KV):
                qs = _stack_heads(q_ref, g)
                dos = _stack_heads(do_ref, g)
                k = _band(kvp_ref, kvc_ref, g * HEAD_DIM)
                v = _band(kvp_ref, kvc_ref, KV_W + g * HEAD_DIM)
                p, ps = _attn_probs(qs, k, tab_ref[g], seen, _sink_col(sink_ref, g))
                dp = _dot_nt(dos, v)
                delta = jnp.sum(p * dp, axis=-1, keepdims=True)
                dsb = (p * (dp - delta)).astype(BF16)
                dsink = -ps * delta
                dqs = _dot(dsb, k) * SCALE
                dk = _dot_tn(dsb, qs) * SCALE
                dv = _dot_tn(p.astype(BF16), dos)
                for i in range(GROUP):
                    h = GROUP * g + i
                    dq_ref[:, h * HEAD_DIM:(h + 1) * HEAD_DIM] = dqs[i * WINDOW:(i + 1) * WINDOW].astype(BF16)
                    dsk_ref[h:h + 1, :] += jnp.sum(dsink[i * WINDOW:(i + 1) * WINDOW], axis=0, keepdims=True)
                for off, d in ((g * HEAD_DIM, dk), (KV_W + g * HEAD_DIM, dv)):
                    dkv_ref[:, off:off + HEAD_DIM] = (carry_ref[:, off:off + HEAD_DIM] + d[:WINDOW]).astype(BF16)
                    carry_ref[:, off:off + HEAD_DIM] = d[WINDOW:]

        @pl.when(n == nb)
        def _():
            dkv_ref[...] = carry_ref[...].astype(BF16)

    last = nb - 1
    return pl.pallas_call(
        body, name="attn_bwd", grid=(nb + 1,),
        in_specs=[pl.BlockSpec(memory_space=pltpu.SMEM),
                  pl.BlockSpec(tab.shape, lambda n: (0, 0, 0)),
                  pl.BlockSpec((WINDOW, ATTN_W), lambda n: (jnp.minimum(n, last), 0)),
                  pl.BlockSpec((WINDOW, 2 * KV_W), lambda n: (jnp.clip(n - 1, 0, last), 2)),
                  pl.BlockSpec((WINDOW, 2 * KV_W), lambda n: (jnp.minimum(n, last), 2)),
                  pl.BlockSpec((WINDOW, ATTN_W), lambda n: (jnp.minimum(n, last), 0))],
        out_specs=[pl.BlockSpec((WINDOW, ATTN_W), lambda n: (jnp.minimum(n, last), 0)),
                   pl.BlockSpec((WINDOW, 2 * KV_W), lambda n: (jnp.maximum(n - 1, 0), 0)),
                   pl.BlockSpec((8, LANES), lambda n: (0, 0))],
        out_shape=[jax.ShapeDtypeStruct((T, ATTN_W), BF16), jax.ShapeDtypeStruct((T, 2 * KV_W), BF16),
                   jax.ShapeDtypeStruct((8, LANES), F32)],
        scratch_shapes=[pltpu.VMEM((WINDOW, 2 * KV_W), F32)],
        compiler_params=_cp(1),
    )(sinks, tab, qkv, qkv, qkv, dattn)


def _pack(arrs):
    flat = jnp.concatenate([a.reshape(-1) for a in arrs])
    pad = -flat.shape[0] % (8 * LANES)
    return jnp.pad(flat, (0, pad)).reshape(1, -1, LANES)


def _unpack(packed, like):
    flat = packed.reshape(-1)
    out, off = [], 0
    for a in like:
        out.append(flat[off:off + a.size].reshape(a.shape))
        off += a.size
    return out


def kernel(x, norm_ffn1, w_ffn1_in, w_ffn1_out, norm_mix, w_in, sinks, w_dw, b_dw, conv_ln_g, conv_ln_b, w_out, norm_ffn2, w_ffn2_in, w_ffn2_out, final_norm, loss_target, m_norm_ffn1, m_w_ffn1_in, m_w_ffn1_out, m_norm_mix, m_w_in, m_sinks, m_w_dw, m_b_dw, m_conv_ln_g, m_conv_ln_b, m_w_out, m_norm_ffn2, m_w_ffn2_in, m_w_ffn2_out, m_final_norm, v_norm_ffn1, v_w_ffn1_in, v_w_ffn1_out, v_norm_mix, v_w_in, v_sinks, v_w_dw, v_b_dw, v_conv_ln_g, v_conv_ln_b, v_w_out, v_norm_ffn2, v_w_ffn2_in, v_w_ffn2_out, v_final_norm):
    L, D = norm_ffn1.shape
    T = x.shape[1]
    FB = w_ffn1_in.shape[2]
    CH = b_dw.shape[1]
    QKV = ATTN_W + 2 * KV_W
    xs = x.reshape(T, D)
    tgt = loss_target.reshape(T, D)
    cx, cy, cc = lax.axis_index("x"), lax.axis_index("y"), lax.axis_index("c")
    chip = 2 * cx + cy
    cidx = cc.reshape(1).astype(jnp.int32)
    big_w = (w_ffn1_in, w_ffn1_out, w_in, w_out, w_ffn2_in, w_ffn2_out)
    big_m = (m_w_ffn1_in, m_w_ffn1_out, m_w_in, m_w_out, m_w_ffn2_in, m_w_ffn2_out)
    big_v = (v_w_ffn1_in, v_w_ffn1_out, v_w_in, v_w_out, v_w_ffn2_in, v_w_ffn2_out)
    NW = len(big_w) + 1

    def shards(l, tok):
        return [(w_[l] + tok[0, 0]).astype(BF16) for w_ in big_w] + [w_dw[l] + tok[0, 0]]

    def own_slot(a):
        return lax.dynamic_update_index_in_dim(lax.empty((4,) + a.shape, a.dtype), a, chip, 0)

    def gather_start(srcs, tok):
        return _xchg_start("gather_start", srcs, [own_slot(s_) for s_ in srcs], _gather_plan, tok)

    row = lambda a, l: a[l].reshape(1, -1)
    tab = _attn_bias_table()

    saved, W = [], []
    zero_tok = jnp.zeros((8, LANES), F32)
    started = gather_start(shards(0, zero_tok), zero_tok)
    cast = [None] + [shards(l, started[-1]) for l in range(1, L)]
    after = [xs] + [a_ for c_ in cast[1:] for a_ in c_]
    for l in range(L):
        _, lands, tok = _xchg_wait("gather_wait", started, NW, _gather_plan, after)
        if l + 1 < L:
            started = gather_start(cast[l + 1], tok)
            tok = started[-1]
        g1i, g1o, gi, go, g2i, g2o = _gather_share(lands[:-1])
        w = dict(f1i=g1i, f1o=g1o.reshape(2 * FB, D), f2i=g2i, f2o=g2o.reshape(2 * FB, D),
                 wi=jnp.transpose(gi, (1, 0, 2)).reshape(D, -1), wo=go.reshape(-1, D),
                 wdw=jnp.transpose(lands[-1], (1, 0, 2)).reshape(CONV_W, CH))
        W.append(w)
        x0 = xs
        x1, gu1 = _ffn_fwd(x0, row(norm_ffn1, l) + tok[0, 0], w["f1i"], w["f1o"])
        qkv, u = _mixproj_fwd(x1, row(norm_mix, l), w["wi"])
        attn = _attn_fwd(row(sinks, l), tab, qkv)
        conv, ypre = _conv_fwd(u, w["wdw"], row(b_dw, l), row(conv_ln_g, l), row(conv_ln_b, l))
        x2 = _mixout_fwd(x1, attn, conv, w["wo"])
        xs, gu2 = _ffn_fwd(x2, row(norm_ffn2, l), w["f2i"], w["f2o"])
        saved.append((x0, gu1, x1, qkv, u, attn, conv, ypre, x2, gu2))
        after = [xs]

    loss_part, dx, d_final = _loss_head(xs, final_norm.reshape(1, D), tgt)
    loss = lax.psum(loss_part[0, 0], ("x", "y", "c"))

    bufs = [[lax.empty(w_.shape, F32) for _ in range(4)] for w_ in big_w]
    d_n1, d_nm, d_n2 = [None] * L, [None] * L, [None] * L
    d_sk, d_bdw, d_lg, d_lb, d_wdw = [None] * L, [None] * L, [None] * L, [None] * L, [None] * L
    NB = len(big_w)

    def reduce_start(sib_started, after):
        gs, sibs, _ = _xchg_wait("sib_wait", sib_started, NB, _sib_plan, after, sibling=True)
        parts = [_sum_halves(cidx, g, s_) for g, s_ in zip(gs, sibs)]
        lands = [own_slot(lax.dynamic_index_in_dim(p, chip, 0, keepdims=False)) for p in parts]
        return _xchg_start("rs_start", parts, lands, _rs_plan, zero_tok)

    def finish(l, rs_started, after):
        _, qs, _ = _xchg_wait("rs_wait", rs_started, NB, _rs_plan, after)
        q_sib = _rs_share(qs)
        for t in range(NB):
            bufs[t] = _adamw_layer(cidx, qs[t], q_sib[t], big_w[t], big_m[t], big_v[t], bufs[t], l)

    sib_pending, rs_list = None, []
    tok = zero_tok
    for l in reversed(range(L)):
        w = W[l]
        x0, gu1, x1, qkv, u, attn, conv, ypre, x2, gu2 = saved[l]
        dx, d_n2[l], hb, dgu, a, dyb = _ffn_bwd(dx, x2, row(norm_ffn2, l), gu2, w["f2i"], w["f2o"], tok)
        g_f2i, g_f2o = _wgrad_ffn_in(hb, dgu), _wgrad_ffn_out(a, dyb)
        lg_row = row(conv_ln_g, l)
        if sib_pending is not None:
            rs_started = reduce_start(sib_pending[1], after=[g_f2o])
            rs_list.append((sib_pending[0], rs_started))
            lg_row = lg_row + rs_started[-1][0, 0]
        dyb, dattn, dconv = _mixout_bwd(dx, w["wo"])
        g_wo = _wgrad_cat([attn, conv], [dyb]).reshape(4, -1, D)
        du, dwdw, dvec = _conv_bwd(dconv, ypre, u, w["wdw"], lg_row, row(conv_ln_b, l))
        d_wdw[l], d_bdw[l], d_lg[l], d_lb[l] = dwdw[:CONV_W], dvec[0], dvec[1], dvec[2]
        dq, dkv, dsk = _attn_bwd(row(sinks, l), tab, qkv, dattn)
        d_sk[l] = dsk[:, 0]
        wi = w["wi"]
        dx, d_nm[l], hb = _mix_rms_bwd(dx, x1, row(norm_mix, l), [dq, dkv, du],
                                       [wi[:, :ATTN_W], wi[:, ATTN_W:QKV], wi[:, QKV:]])
        g_wi = jnp.transpose(_wgrad_cat([hb], [dq, dkv, du]).reshape(D, 4, -1), (1, 0, 2))
        dx, d_n1[l], hb, dgu, a, dyb = _ffn_bwd(dx, x0, row(norm_ffn1, l), gu1, w["f1i"], w["f1o"], tok)
        g_f1i, g_f1o = _wgrad_ffn_in(hb, dgu), _wgrad_ffn_out(a, dyb)
        gs = [g_f1i, g_f1o, g_wi, g_wo, g_f2i, g_f2o]
        sib_started = _xchg_start("sib_start", gs, [lax.empty((4, g.shape[1] // 2, g.shape[2]), F32) for g in gs],
                                  _sib_plan, zero_tok, sibling=True)
        tok = sib_started[-1]
        sib_pending = (l, sib_started)
    grad_x = dx.reshape(x.shape)
    rs_started = reduce_start(sib_pending[1], after=[tok])
    rs_list.append((sib_pending[0], rs_started))

    after = rs_started[-1]
    for l, st in rs_list[:2]:
        finish(l, st, after=[after])
        after = bufs[0][0]
    small_g = [jnp.concatenate(d, axis=0) for d in (d_n1, d_nm, d_n2)] + [d_final, jnp.stack(d_sk)] + \
              [jnp.stack(d) for d in (d_bdw, d_lg, d_lb, d_wdw)]
    small_sum = _unpack(_small_allreduce(_pack(small_g)[0], after), small_g)
    g_wdw = lax.dynamic_slice_in_dim(small_sum[8], chip * w_dw.shape[2], w_dw.shape[2], axis=2)
    small_g = [small_sum[0], small_sum[1], small_sum[2], small_sum[3].reshape(D), small_sum[4],
               small_sum[5], small_sum[6], small_sum[7], g_wdw]
    small_w = (norm_ffn1, norm_mix, norm_ffn2, final_norm, sinks, b_dw, conv_ln_g, conv_ln_b, w_dw)
    small_m = (m_norm_ffn1, m_norm_mix, m_norm_ffn2, m_final_norm, m_sinks, m_b_dw, m_conv_ln_g, m_conv_ln_b, m_w_dw)
    small_v = (v_norm_ffn1, v_norm_mix, v_norm_ffn2, v_final_norm, v_sinks, v_b_dw, v_conv_ln_g, v_conv_ln_b, v_w_dw)
    upd = _adamw(_pack(small_g), _pack(small_w), _pack(small_m), _pack(small_v))
    small_upd = [_unpack(u_, small_w) for u_ in upd]
    after = upd[0]
    for l, st in rs_list[2:]:
        finish(l, st, after=[after])
        after = bufs[0][0]

    order = ("norm_ffn1", "w_ffn1_in", "w_ffn1_out", "norm_mix", "w_in", "sinks", "w_dw", "b_dw", "conv_ln_g",
             "conv_ln_b", "w_out", "norm_ffn2", "w_ffn2_in", "w_ffn2_out", "final_norm")
    small_names = ("norm_ffn1", "norm_mix", "norm_ffn2", "final_norm", "sinks", "b_dw", "conv_ln_g", "conv_ln_b", "w_dw")
    big_names = ("w_ffn1_in", "w_ffn1_out", "w_in", "w_out", "w_ffn2_in", "w_ffn2_out")
    grads, deltas, new_m, new_v = {}, {}, {}, {}
    for i, nme in enumerate(small_names):
        grads[nme], deltas[nme], new_m[nme], new_v[nme] = small_g[i], small_upd[0][i], small_upd[1][i], small_upd[2][i]
    for i, nme in enumerate(big_names):
        grads[nme], deltas[nme], new_m[nme], new_v[nme] = bufs[i]
    return (loss, grad_x, *[grads[n] for n in order], *[deltas[n] for n in order],
            *[new_m[n] for n in order], *[new_v[n] for n in order])
```

```python
import functools

import jax
import jax.numpy as jnp
from jax import lax
from jax.experimental import pallas as pl
from jax.experimental.pallas import tpu as pltpu

F32, BF16 = jnp.float32, jnp.bfloat16
EPS = 1e-6
NEG_INF = -1e30
HEAD_DIM = 64
N_HEADS = 8
N_KV = 2
GROUP = N_HEADS // N_KV
WINDOW = 128
ATTN_W = N_HEADS * HEAD_DIM
KV_W = N_KV * HEAD_DIM
CONV_W = 31
HALO = 32
CONV_ROWS = 32
SCALE = 1.0 / 8.0
ADAM_LR, ADAM_B1, ADAM_B2, ADAM_EPS, ADAM_WD, ADAM_STEP = 0.001, 0.9, 0.999, 1e-08, 0.01, 10
TM = 512
TM_FFN_BWD = 256
LANES = 128
VMEM_LIMIT = 52 * 1024 * 1024
MESH = pl.DeviceIdType.MESH
ANY = pl.BlockSpec(memory_space=pl.ANY)
HBM = pl.BlockSpec(memory_space=pltpu.HBM)
SEM = pl.BlockSpec(memory_space=pltpu.SEMAPHORE)
VMEM = pl.BlockSpec(memory_space=pltpu.VMEM)
EFFECT = pltpu.SideEffectType.DATAFLOW_SIDE_EFFECTING
TOKEN = jax.ShapeDtypeStruct((8, LANES), F32)


def _cp(n):
    return pltpu.CompilerParams(dimension_semantics=("arbitrary",) * n, vmem_limit_bytes=VMEM_LIMIT)


def _dot(a, b):
    return jnp.dot(a, b, preferred_element_type=F32)


def _dot_nt(a, b):
    return lax.dot_general(a, b, (((1,), (1,)), ((), ())), preferred_element_type=F32)


def _dot_tn(a, b):
    return lax.dot_general(a, b, (((0,), (0,)), ((), ())), preferred_element_type=F32)


def _sigmoid(v):
    return 1.0 / (1.0 + jnp.exp(-v))


def _place():
    x, y, c = lax.axis_index("x"), lax.axis_index("y"), lax.axis_index("c")
    chips = [(1 - x, y), (x, 1 - y), (1 - x, 1 - y)]
    return x, y, c, chips


def _rcopy(src, dst, send_sems, recv_sems, k, dev):
    return pltpu.make_async_remote_copy(src_ref=src, dst_ref=dst, send_sem=send_sems.at[k],
                                        recv_sem=recv_sems.at[k], device_id=dev, device_id_type=MESH)


def _hbm(a):
    return pltpu.with_memory_space_constraint(a, pltpu.HBM)


PEERS = {"chips": 3, "sibling": 1, "sibling3": 3, "all": 7}


def _targets(mode):
    x, y, c, chips = _place()
    b = 2 * x + y
    if mode == "chips":
        return b, c, [((px, py, c), 2 * px + py) for px, py in chips]
    if mode == "sibling":
        return b, c, [((x, y, 1 - c), b)]
    if mode == "sibling3":
        return b, c, [((x, y, 1 - c), 2 * px + py) for px, py in chips]
    flip = lambda v, f: 1 - v if f else v
    devs = [(flip(x, k >> 2 & 1), flip(y, k >> 1 & 1), flip(c, k & 1)) for k in range(1, 8)]
    return 4 * x + 2 * y + c, c, [(d, 4 * d[0] + 2 * d[1] + d[2]) for d in devs]


def _xchg_start(name, srcs, lands, plan, dep, mode="chips"):
    ns, nl, npeer = len(srcs), len(lands), PEERS[mode]

    def body(*refs):
        land = refs[ns:ns + nl]
        src = refs[:ns] if ns else land
        send_sems, recv_sems, token = refs[ns + nl + 1], refs[ns + nl + 2], refs[-1]
        me, c, peers = _targets(mode)
        for t in range(nl):
            for j, (dev, tag) in enumerate(peers):
                s, d, _ = plan(src[t], land[t], t, me, c, tag)
                _rcopy(s, d, send_sems, recv_sems, npeer * t + j, dev).start()
        token[...] = jnp.zeros_like(token)

    arrs = list(srcs) + list(lands)
    return pl.pallas_call(
        body, name=name,
        out_shape=(pltpu.SemaphoreType.DMA((npeer * nl,)), pltpu.SemaphoreType.DMA((npeer * nl,)),
                   *[pltpu.HBM(a.shape, a.dtype) for a in arrs], TOKEN),
        in_specs=[HBM] * (ns + nl) + [ANY], out_specs=(SEM, SEM, *[HBM] * (ns + nl), VMEM),
        input_output_aliases={i: 2 + i for i in range(ns + nl)},
        compiler_params=pltpu.CompilerParams(has_side_effects=EFFECT),
    )(*[_hbm(a) for a in arrs], dep)


def _xchg_wait(name, started, ns, nl, plan, after, mode="chips"):
    send_sems, recv_sems, thru = started[0], started[1], started[2:2 + ns + nl]
    npeer = PEERS[mode]

    def body(*refs):
        land = refs[ns:ns + nl]
        src = refs[:ns] if ns else land
        send_sems, recv_sems, token = refs[ns + nl], refs[ns + nl + 1], refs[-1]
        me, c, peers = _targets(mode)
        for t in range(nl):
            for j, (dev, tag) in enumerate(peers):
                s, _, a = plan(src[t], land[t], t, me, c, tag)
                cp = _rcopy(s, a, send_sems, recv_sems, npeer * t + j, dev)
                cp.wait_send()
                cp.wait_recv()
        token[...] = jnp.zeros_like(token)

    out = pl.pallas_call(
        body, name=name,
        out_shape=(*[pltpu.HBM(a.shape, a.dtype) for a in thru], TOKEN),
        in_specs=[HBM] * (ns + nl) + [SEM, SEM] + [ANY] * len(after), out_specs=(*[HBM] * (ns + nl), VMEM),
        input_output_aliases={i: i for i in range(ns + nl)},
        compiler_params=pltpu.CompilerParams(has_side_effects=EFFECT),
    )(*thru, send_sems, recv_sems, *after)
    return out[:ns], out[ns:ns + nl], out[-1]


def _half(ref_rows, which):
    h = ref_rows // 2
    return pl.ds(which * h, h)


def _gather_plan(src, land, t, b, c, pb):
    if len(src.shape) == 2 and src.shape[0] % 2 == 0:
        hs = _half(src.shape[0], c)
        return src.at[hs], land.at[b, hs], land.at[pb, hs]
    return src, land.at[b], land.at[pb]


def _gshare_plan(src, land, t, b, c, pb):
    return land.at[pb, _half(land.shape[1], c)], land.at[pb, _half(land.shape[1], c)], land.at[pb, _half(land.shape[1], 1 - c)]


def _rs_plan(src, land, t, b, c, pb):
    return src.at[pb], land.at[b], land.at[pb]


def _sib_plan(src, land, t, b, c, pb):
    return src.at[:, _half(src.shape[1], 1 - c), :], land, land


def _rows_block(h, cap=512):
    for rb in range(min(h, cap) // 16 * 16, 0, -16):
        if h % rb == 0:
            return rb
    return h


def _sum_halves(cidx, g, s):
    _, R, C = g.shape
    rb = _rows_block(R // 2)
    nr = R // 2 // rb

    def body(c_ref, g_ref, s_ref, o_ref):
        o_ref[...] = (g_ref[...] + s_ref[...]).astype(BF16)

    blk = (None, rb, C)
    return pl.pallas_call(
        body, name="sum_halves", out_shape=jax.ShapeDtypeStruct(s.shape, BF16),
        grid_spec=pltpu.PrefetchScalarGridSpec(
            num_scalar_prefetch=1, grid=(4, nr),
            in_specs=[pl.BlockSpec(blk, lambda p, i, c: (p, c[0] * nr + i, 0)),
                      pl.BlockSpec(blk, lambda p, i, c: (p, i, 0))],
            out_specs=pl.BlockSpec(blk, lambda p, i, c: (p, i, 0))),
        compiler_params=_cp(2),
    )(cidx, g, s)


def _whole_plan(src, land, t, me, c, tag):
    return src, land, land


def _slot_plan(src, land, t, me, c, tag):
    return src, land.at[me], land.at[tag]


def _adam_update(gg, w, m, v):
    m2 = ADAM_B1 * m + (1.0 - ADAM_B1) * gg
    v2 = ADAM_B2 * v + (1.0 - ADAM_B2) * (gg * gg)
    mh = m2 / (1.0 - ADAM_B1 ** ADAM_STEP)
    vh = v2 / (1.0 - ADAM_B2 ** ADAM_STEP)
    return -ADAM_LR * (mh / (jnp.sqrt(vh) + ADAM_EPS) + ADAM_WD * w), m2, v2


def _adamw_layer(cidx, q_own, q_sib, w, m, v, bufs, l):
    L, R, C = w.shape
    h = R // 2
    rb = _rows_block(h, 256)
    nr = h // rb

    def body(c_ref, qo_ref, qs_ref, w_ref, m_ref, v_ref, *rest):
        g_ref, d_ref, mo_ref, vo_ref = rest[-4:]
        own = pl.program_id(0) == c_ref[0]
        gg = jnp.zeros((rb, C), F32)
        for s in range(4):
            gg = gg + jnp.where(own, qo_ref[s], qs_ref[s]).astype(F32)
        g_ref[...] = gg
        d_ref[...], mo_ref[...], vo_ref[...] = _adam_update(gg, w_ref[...], m_ref[...], v_ref[...])

    qspec = pl.BlockSpec((4, rb, C), lambda hh, i, c: (0, i, 0))
    wspec = pl.BlockSpec((None, rb, C), lambda hh, i, c: (l, hh * nr + i, 0))
    return pl.pallas_call(
        body, name="adamw_layer", out_shape=[jax.ShapeDtypeStruct(w.shape, F32)] * 4,
        grid_spec=pltpu.PrefetchScalarGridSpec(
            num_scalar_prefetch=1, grid=(2, nr),
            in_specs=[qspec, qspec, wspec, wspec, wspec] + [ANY] * 4, out_specs=[wspec] * 4),
        input_output_aliases={6 + k: k for k in range(4)},
        compiler_params=_cp(2),
    )(cidx, q_own, q_sib, w, m, v, *bufs)


def _adamw(g, w, m, v):
    L, R, C = g.shape
    rb = _rows_block(R)

    def body(g_ref, w_ref, m_ref, v_ref, d_ref, mo_ref, vo_ref):
        d_ref[...], mo_ref[...], vo_ref[...] = _adam_update(g_ref[...], w_ref[...], m_ref[...], v_ref[...])

    spec = pl.BlockSpec((None, rb, C), lambda l, i: (l, i, 0))
    return pl.pallas_call(
        body, name="adamw", grid=(L, R // rb), in_specs=[spec] * 4, out_specs=[spec] * 3,
        out_shape=[jax.ShapeDtypeStruct(g.shape, F32)] * 3, compiler_params=_cp(2),
    )(g, w, m, v)


def _sum_slots(buf):
    def body(b_ref, o_ref):
        acc = b_ref[0]
        for k in range(1, 8):
            acc = acc + b_ref[k]
        o_ref[...] = acc

    return pl.pallas_call(body, name="sum_slots", in_specs=[VMEM], out_specs=VMEM,
                          out_shape=jax.ShapeDtypeStruct(buf.shape[1:], F32))(buf)


def _rms(xf, g):
    r = lax.rsqrt(jnp.mean(xf * xf, axis=-1, keepdims=True) + EPS)
    return xf * r, r


def _lane_chunks(n):
    lo = (n // LANES + 1) // 2 * LANES
    return ((0, lo), (lo, n - lo))


def _load_ffn_weights(win_hbm, wout_hbm, win_v, wout_v, sems):
    fb = win_v.shape[2]
    loads = [pltpu.make_async_copy(win_hbm.at[k], win_v.at[k], sems.at[k]) for k in range(4)]
    loads += [pltpu.make_async_copy(wout_hbm.at[pl.ds(k * fb, fb)], wout_v.at[pl.ds(k * fb, fb)], sems.at[4 + k])
              for k in range(2)]
    for cp in loads:
        cp.start()
    for cp in loads:
        cp.wait()


def _fast_sigmoid(v):
    return pl.reciprocal(1.0 + jnp.exp(-v), approx=True)


def _ffn_fwd(x, g, win, wout):
    T, D = x.shape
    FB = win.shape[2]
    tm = min(TM, T)

    def body(x_ref, g_ref, win_hbm, wout_hbm, xo_ref, gu_ref, win_v, wout_v, sems):
        @pl.when(pl.program_id(0) == 0)
        def _():
            _load_ffn_weights(win_hbm, wout_hbm, win_v, wout_v, sems)

        xf = x_ref[...]
        xh, _ = _rms(xf, None)
        h = (xh * g_ref[...]).astype(BF16)
        acc = jnp.zeros((tm, D), F32)
        for blk in range(2):
            for lo, sz in _lane_chunks(FB):
                cols = pl.ds(blk * FB + lo, sz)
                gate = _dot(h, win_v[blk, :, pl.ds(lo, sz)])
                up = _dot(h, win_v[2 + blk, :, pl.ds(lo, sz)])
                gu_ref[0, :, cols] = gate.astype(BF16)
                gu_ref[1, :, cols] = up.astype(BF16)
                a = (gate * _fast_sigmoid(gate) * up).astype(BF16)
                acc = acc + _dot(a, wout_v[cols, :])
        xo_ref[...] = xf + 0.5 * acc

    row = pl.BlockSpec((tm, D), lambda i: (i, 0))
    return pl.pallas_call(
        body, name="ffn_fwd", grid=(T // tm,),
        in_specs=[row, pl.BlockSpec((1, D), lambda i: (0, 0)), ANY, ANY],
        out_specs=[row, pl.BlockSpec((2, tm, 2 * FB), lambda i: (0, i, 0))],
        out_shape=[jax.ShapeDtypeStruct((T, D), F32), jax.ShapeDtypeStruct((2, T, 2 * FB), BF16)],
        scratch_shapes=[pltpu.VMEM(win.shape, BF16), pltpu.VMEM(wout.shape, BF16), pltpu.SemaphoreType.DMA((6,))],
        compiler_params=_cp(1),
    )(x, g, win, wout)


def _mixproj_fwd(x, g, w):
    T, D = x.shape
    W = w.shape[1]
    QKV = ATTN_W + 2 * KV_W
    tm = min(TM, T)

    def body(x_ref, g_ref, w_ref, qkv_ref, u_ref):
        xh, _ = _rms(x_ref[...], None)
        h = (xh * g_ref[...]).astype(BF16)
        qkv_ref[...] = _dot(h, w_ref[:, :QKV]).astype(BF16)
        u_ref[...] = _dot(h, w_ref[:, QKV:])

    return pl.pallas_call(
        body, name="mixproj_fwd", grid=(T // tm,),
        in_specs=[pl.BlockSpec((tm, D), lambda i: (i, 0)), pl.BlockSpec((1, D), lambda i: (0, 0)),
                  pl.BlockSpec((D, W), lambda i: (0, 0))],
        out_specs=[pl.BlockSpec((tm, QKV), lambda i: (i, 0)), pl.BlockSpec((tm, W - QKV), lambda i: (i, 0))],
        out_shape=[jax.ShapeDtypeStruct((T, QKV), BF16), jax.ShapeDtypeStruct((T, W - QKV), F32)],
        compiler_params=_cp(1),
    )(x, g, w)


def _attn_bias_table():
    rows, cols = GROUP * WINDOW, 2 * WINDOW
    row = lax.broadcasted_iota(jnp.int32, (N_KV, rows, cols), 1)
    col = lax.broadcasted_iota(jnp.int32, (N_KV, rows, cols), 2)
    head = GROUP * lax.broadcasted_iota(jnp.int32, (N_KV, rows, cols), 0) + (row >> 7)
    dist = (row & (WINDOW - 1)) + WINDOW - col
    slope = jnp.exp2(-(head + 1).astype(F32))
    return jnp.where((dist >= 0) & (dist < WINDOW), -slope * dist.astype(F32), NEG_INF)


def _first_block_mask(n):
    col = lax.broadcasted_iota(jnp.int32, (GROUP * WINDOW, 2 * WINDOW), 1)
    return (n > 0) | (col >= WINDOW)


def _sink_col(sink_ref, g):
    hi = lax.broadcasted_iota(jnp.int32, (GROUP * WINDOW, 1), 0) >> 7
    col = jnp.zeros((GROUP * WINDOW, 1), F32)
    for i in range(GROUP):
        col = jnp.where(hi == i, sink_ref[0, GROUP * g + i], col)
    return col


def _stack_heads(ref, g):
    return jnp.concatenate([ref[:, (GROUP * g + i) * HEAD_DIM:(GROUP * g + i + 1) * HEAD_DIM]
                            for i in range(GROUP)], axis=0)


def _band(kvp_ref, kvc_ref, off):
    return jnp.concatenate([kvp_ref[:, off:off + HEAD_DIM], kvc_ref[:, off:off + HEAD_DIM]], axis=0)


def _attn_probs(qs, k, bias, seen, sink):
    s = jnp.where(seen, _dot_nt(qs, k) * SCALE + bias, NEG_INF)
    m = jnp.maximum(jnp.max(s, axis=-1, keepdims=True), sink)
    p = jnp.exp(s - m)
    es = jnp.exp(sink - m)
    den = jnp.sum(p, axis=-1, keepdims=True) + es
    return p / den, es / den


def _attn_fwd(sinks, tab, qkv):
    T = qkv.shape[0]
    nb = T // WINDOW

    def body(sink_ref, tab_ref, q_ref, kvp_ref, kvc_ref, o_ref):
        seen = _first_block_mask(pl.program_id(0))
        for g in range(N_KV):
            qs = _stack_heads(q_ref, g)
            k = _band(kvp_ref, kvc_ref, g * HEAD_DIM)
            v = _band(kvp_ref, kvc_ref, KV_W + g * HEAD_DIM)
            p, _ = _attn_probs(qs, k, tab_ref[g], seen, _sink_col(sink_ref, g))
            o = _dot(p.astype(BF16), v)
            for i in range(GROUP):
                h = GROUP * g + i
                o_ref[:, h * HEAD_DIM:(h + 1) * HEAD_DIM] = o[i * WINDOW:(i + 1) * WINDOW].astype(BF16)

    return pl.pallas_call(
        body, name="attn_fwd", grid=(nb,),
        in_specs=[pl.BlockSpec(memory_space=pltpu.SMEM),
                  pl.BlockSpec(tab.shape, lambda n: (0, 0, 0)),
                  pl.BlockSpec((WINDOW, ATTN_W), lambda n: (n, 0)),
                  pl.BlockSpec((WINDOW, 2 * KV_W), lambda n: (jnp.maximum(n - 1, 0), 2)),
                  pl.BlockSpec((WINDOW, 2 * KV_W), lambda n: (n, 2))],
        out_specs=pl.BlockSpec((WINDOW, ATTN_W), lambda n: (n, 0)),
        out_shape=jax.ShapeDtypeStruct((T, ATTN_W), BF16),
        compiler_params=_cp(1),
    )(sinks, tab, qkv, qkv, qkv)


def _shift_copies(src_ref, dst_ref, n):
    for b in range(1, 8):
        dst_ref[b - 1] = src_ref[b:b + n, :]


def _tap(src_ref, sh_ref, s, c0):
    a, b = divmod(s, 8)
    start = pl.multiple_of(c0 + 8 * a, 8)
    if b == 0:
        return src_ref[pl.ds(start, CONV_ROWS), :]
    return sh_ref[b - 1, pl.ds(start, CONV_ROWS), :]


def _glu_rows(u, ch):
    return u[:, :ch] * _sigmoid(u[:, ch:])


def _fill_z(zs_ref, zsh_ref, uc_ref, up_ref, i, ch, n):
    zs_ref[0:HALO] = jnp.where(i > 0, _glu_rows(up_ref[...], ch), 0.0)
    zs_ref[HALO:] = _glu_rows(uc_ref[...], ch)
    _shift_copies(zs_ref, zsh_ref, n - 8)


def _conv_fwd(u, w, b, lg, lb):
    T = u.shape[0]
    CH = u.shape[1] // 2
    tm = min(TM, T)
    n = tm + HALO
    hb = tm // HALO

    def body(uc_ref, up_ref, w_ref, b_ref, lg_ref, lb_ref, conv_ref, ypre_ref, zs_ref, zsh_ref):
        i = pl.program_id(0)
        _fill_z(zs_ref, zsh_ref, uc_ref, up_ref, i, CH, n)
        bias = b_ref[...]

        def chunk(ci, carry):
            c0 = pl.multiple_of(ci * CONV_ROWS, CONV_ROWS)
            acc = jnp.broadcast_to(bias, (CONV_ROWS, CH))
            for k in range(CONV_W):
                acc = acc + w_ref[k:k + 1, :] * _tap(zs_ref, zsh_ref, HALO - (CONV_W - 1) + k, c0)
            ypre_ref[pl.ds(c0, CONV_ROWS), :] = acc
            return carry

        lax.fori_loop(0, tm // CONV_ROWS, chunk, 0)
        y = ypre_ref[...]
        mu = jnp.mean(y, axis=-1, keepdims=True)
        d = y - mu
        var = jnp.mean(d * d, axis=-1, keepdims=True)
        o = d * lax.rsqrt(var + EPS) * lg_ref[...] + lb_ref[...]
        conv_ref[...] = (o * _sigmoid(o)).astype(BF16)

    vec = pl.BlockSpec((1, CH), lambda i: (0, 0))
    return pl.pallas_call(
        body, name="conv_fwd", grid=(T // tm,),
        in_specs=[pl.BlockSpec((tm, 2 * CH), lambda i: (i, 0)),
                  pl.BlockSpec((HALO, 2 * CH), lambda i: (jnp.maximum(i * hb - 1, 0), 0)),
                  pl.BlockSpec((CONV_W, CH), lambda i: (0, 0)), vec, vec, vec],
        out_specs=[pl.BlockSpec((tm, CH), lambda i: (i, 0)), pl.BlockSpec((tm, CH), lambda i: (i, 0))],
        out_shape=[jax.ShapeDtypeStruct((T, CH), BF16), jax.ShapeDtypeStruct((T, CH), F32)],
        scratch_shapes=[pltpu.VMEM((n, CH), F32), pltpu.VMEM((7, n - 8, CH), F32)],
        compiler_params=_cp(1),
    )(u, u, w, b, lg, lb)


def _mixout_fwd(x, attn, conv, wo):
    T, D = x.shape
    tm = min(TM, T)
    A = attn.shape[1]

    def body(x_ref, a_ref, c_ref, w_ref, xo_ref):
        xo_ref[...] = x_ref[...] + _dot(a_ref[...], w_ref[:A, :]) + _dot(c_ref[...], w_ref[A:, :])

    return pl.pallas_call(
        body, name="mixout_fwd", grid=(T // tm,),
        in_specs=[pl.BlockSpec((tm, D), lambda i: (i, 0)), pl.BlockSpec((tm, A), lambda i: (i, 0)),
                  pl.BlockSpec((tm, conv.shape[1]), lambda i: (i, 0)), pl.BlockSpec(wo.shape, lambda i: (0, 0))],
        out_specs=pl.BlockSpec((tm, D), lambda i: (i, 0)),
        out_shape=jax.ShapeDtypeStruct((T, D), F32),
        compiler_params=_cp(1),
    )(x, attn, conv, wo)


def _rms_bwd_rows(dh, xf, g):
    xh, r = _rms(xf, None)
    dxn = dh * g
    dx = r * (dxn - xh * jnp.mean(dxn * xh, axis=-1, keepdims=True))
    return dx, jnp.sum(dh * xh, axis=0, keepdims=True), xh * g


def _loss_head(x, g, tgt):
    T, D = x.shape
    tm = min(TM, T)

    def body(x_ref, g_ref, t_ref, loss_ref, dx_ref, dg_ref):
        @pl.when(pl.program_id(0) == 0)
        def _():
            loss_ref[...] = jnp.zeros_like(loss_ref)
            dg_ref[...] = jnp.zeros_like(dg_ref)

        xf = x_ref[...]
        g = g_ref[...]
        xh, _ = _rms(xf, None)
        e = xh * g - t_ref[...]
        loss_ref[...] += 0.5 * jnp.sum(jnp.mean(e * e, axis=-1, keepdims=True), axis=0, keepdims=True)
        dx, dg, _ = _rms_bwd_rows(e * (1.0 / D), xf, g)
        dx_ref[...] = dx
        dg_ref[...] += dg

    return pl.pallas_call(
        body, name="loss_head", grid=(T // tm,),
        in_specs=[pl.BlockSpec((tm, D), lambda i: (i, 0)), pl.BlockSpec((1, D), lambda i: (0, 0)),
                  pl.BlockSpec((tm, D), lambda i: (i, 0))],
        out_specs=[pl.BlockSpec((1, 1), lambda i: (0, 0)), pl.BlockSpec((tm, D), lambda i: (i, 0)),
                   pl.BlockSpec((1, D), lambda i: (0, 0))],
        out_shape=[jax.ShapeDtypeStruct((1, 1), F32), jax.ShapeDtypeStruct((T, D), F32),
                   jax.ShapeDtypeStruct((1, D), F32)],
        compiler_params=_cp(1),
    )(x, g, tgt)


def _ffn_bwd(dxo, x, g, gu, win, wout, dep):
    T, D = x.shape
    FB = win.shape[2]
    tm = min(TM_FFN_BWD, T)

    def body(dxo_ref, x_ref, g_ref, gu_ref, win_hbm, wout_hbm, dep_ref,
             dxi_ref, dg_ref, hb_ref, dgu_ref, a_ref, dyb_ref, win_v, wout_v, sems):
        @pl.when(pl.program_id(0) == 0)
        def _():
            _load_ffn_weights(win_hbm, wout_hbm, win_v, wout_v, sems)
            dg_ref[...] = jnp.zeros_like(dg_ref)

        dyb = (0.5 * dxo_ref[...]).astype(BF16)
        dyb_ref[...] = dyb
        dh = jnp.zeros((tm, D), F32)
        for blk in range(2):
            for lo, sz in _lane_chunks(FB):
                cols = pl.ds(blk * FB + lo, sz)
                da = _dot_nt(dyb, wout_v[cols, :])
                gate = gu_ref[0, :, cols].astype(F32)
                up = gu_ref[1, :, cols].astype(F32)
                sg = _fast_sigmoid(gate)
                s = gate * sg
                a_ref[:, cols] = (s * up).astype(BF16)
                dgate = (da * up * (sg * (1.0 + gate * (1.0 - sg)))).astype(BF16)
                dup = (da * s).astype(BF16)
                dgu_ref[0, :, cols] = dgate
                dgu_ref[1, :, cols] = dup
                dh = dh + _dot_nt(dgate, win_v[blk, :, pl.ds(lo, sz)]) + _dot_nt(dup, win_v[2 + blk, :, pl.ds(lo, sz)])
        dx, dg, h = _rms_bwd_rows(dh, x_ref[...], g_ref[...])
        dxi_ref[...] = dxo_ref[...] + dx
        dg_ref[...] += dg
        hb_ref[...] = h.astype(BF16)

    row = pl.BlockSpec((tm, D), lambda i: (i, 0))
    act = pl.BlockSpec((2, tm, 2 * FB), lambda i: (0, i, 0))
    return pl.pallas_call(
        body, name="ffn_bwd", grid=(T // tm,),
        in_specs=[row, row, pl.BlockSpec((1, D), lambda i: (0, 0)), act, ANY, ANY, ANY],
        out_specs=[row, pl.BlockSpec((1, D), lambda i: (0, 0)), row, act,
                   pl.BlockSpec((tm, 2 * FB), lambda i: (i, 0)), row],
        out_shape=[jax.ShapeDtypeStruct((T, D), F32), jax.ShapeDtypeStruct((1, D), F32),
                   jax.ShapeDtypeStruct((T, D), BF16), jax.ShapeDtypeStruct((2, T, 2 * FB), BF16),
                   jax.ShapeDtypeStruct((T, 2 * FB), BF16), jax.ShapeDtypeStruct((T, D), BF16)],
        scratch_shapes=[pltpu.VMEM(win.shape, BF16), pltpu.VMEM(wout.shape, BF16), pltpu.SemaphoreType.DMA((6,))],
        compiler_params=_cp(1),
    )(dxo, x, g, gu, win, wout, dep)


def _rms_matmul_bwd(name, dxo, x, g, dzs, ws, dz_specs, w_specs, nk):
    T, D = x.shape
    tm = min(TM, T)
    npair = len(dzs)

    def body(*refs):
        dxo_ref, x_ref, g_ref = refs[:3]
        dz_refs, w_refs = refs[3:3 + npair], refs[3 + npair:3 + 2 * npair]
        dxi_ref, dg_ref, hb_ref, acc_ref = refs[3 + 2 * npair:]
        i, k = pl.program_id(0), pl.program_id(1)

        @pl.when(k == 0)
        def _():
            acc_ref[...] = jnp.zeros_like(acc_ref)

        @pl.when((i == 0) & (k == 0))
        def _():
            dg_ref[...] = jnp.zeros_like(dg_ref)

        for p in range(npair):
            acc_ref[...] += _dot_nt(dz_refs[p][...], w_refs[p][...])

        @pl.when(k == nk - 1)
        def _():
            dx, dg, h = _rms_bwd_rows(acc_ref[...], x_ref[...], g_ref[...])
            dxi_ref[...] = dxo_ref[...] + dx
            dg_ref[...] += dg
            hb_ref[...] = h.astype(BF16)

    row = pl.BlockSpec((tm, D), lambda i, k: (i, 0))
    return pl.pallas_call(
        body, name=name, grid=(T // tm, nk),
        in_specs=[row, row, pl.BlockSpec((1, D), lambda i, k: (0, 0))] + list(dz_specs) + list(w_specs),
        out_specs=[row, pl.BlockSpec((1, D), lambda i, k: (0, 0)), row],
        out_shape=[jax.ShapeDtypeStruct((T, D), F32), jax.ShapeDtypeStruct((1, D), F32),
                   jax.ShapeDtypeStruct((T, D), BF16)],
        scratch_shapes=[pltpu.VMEM((tm, D), F32)],
        compiler_params=_cp(2),
    )(dxo, x, g, *dzs, *ws)


def _mix_rms_bwd(dxo, x, g, dzs, ws):
    tm = min(TM, x.shape[0])
    return _rms_matmul_bwd(
        "mix_rms_bwd", dxo, x, g, dzs, ws,
        [pl.BlockSpec((tm, dz.shape[1]), lambda i, k: (i, 0)) for dz in dzs],
        [pl.BlockSpec(w.shape, lambda i, k: (0, 0)) for w in ws], 1)


def _wgrad(name, a, b, a_spec, b_spec, out_shape, out_spec, nblk):
    T = a.shape[0]
    tk = min(TM, T)

    def body(a_ref, b_ref, o_ref):
        @pl.when(pl.program_id(1) == 0)
        def _():
            o_ref[...] = jnp.zeros_like(o_ref)

        o_ref[...] += _dot_tn(a_ref[...], b_ref[...]).reshape(o_ref.shape)

    return pl.pallas_call(
        body, name=name, grid=(nblk, T // tk), in_specs=[a_spec, b_spec], out_specs=out_spec,
        out_shape=jax.ShapeDtypeStruct(out_shape, F32), compiler_params=_cp(2),
    )(a, b)


def _wgrad_ffn_in(hb, dgu):
    T, D = hb.shape
    FB = dgu.shape[2] // 2
    tk = min(TM, T)
    return _wgrad("wgrad_ffn_in", hb, dgu,
                  pl.BlockSpec((tk, D), lambda b, k: (k, 0)),
                  pl.BlockSpec((None, tk, FB), lambda b, k: (b // 2, k, b % 2)),
                  (4, D, FB), pl.BlockSpec((None, D, FB), lambda b, k: (b, 0, 0)), 4)


def _wgrad_ffn_out(a, dyb):
    T, D = dyb.shape
    FB = a.shape[1] // 2
    tk = min(TM, T)
    return _wgrad("wgrad_ffn_out", a, dyb,
                  pl.BlockSpec((tk, FB), lambda b, k: (k, b)),
                  pl.BlockSpec((tk, D), lambda b, k: (k, 0)),
                  (4, FB // 2, D), pl.BlockSpec((2, FB // 2, D), lambda b, k: (b, 0, 0)), 2)


def _wgrad_cat(a_list, b_list):
    T = a_list[0].shape[0]
    tk = min(TM, T)
    na = len(a_list)
    M, N = sum(a.shape[1] for a in a_list), sum(b.shape[1] for b in b_list)

    def body(*refs):
        a_refs, b_refs, o_ref = refs[:na], refs[na:-1], refs[-1]

        @pl.when(pl.program_id(0) == 0)
        def _():
            o_ref[...] = jnp.zeros_like(o_ref)

        r0 = 0
        for a_ref in a_refs:
            c0 = 0
            for b_ref in b_refs:
                m, n = a_ref.shape[1], b_ref.shape[1]
                o_ref[r0:r0 + m, c0:c0 + n] += _dot_tn(a_ref[...], b_ref[...])
                c0 += n
            r0 += a_ref.shape[1]

    return pl.pallas_call(
        body, name="wgrad_cat", grid=(T // tk,),
        in_specs=[pl.BlockSpec((tk, v.shape[1]), lambda k: (k, 0)) for v in list(a_list) + list(b_list)],
        out_specs=pl.BlockSpec((M, N), lambda k: (0, 0)),
        out_shape=jax.ShapeDtypeStruct((M, N), F32), compiler_params=_cp(1),
    )(*a_list, *b_list)


def _mixout_bwd(dxo, wo):
    T, D = dxo.shape
    tm = min(TM, T)
    A = ATTN_W
    C = wo.shape[0] - A

    def body(dxo_ref, w_ref, dyb_ref, da_ref, dc_ref):
        dyb = dxo_ref[...].astype(BF16)
        dyb_ref[...] = dyb
        da_ref[...] = _dot_nt(dyb, w_ref[:A, :]).astype(BF16)
        dc_ref[...] = _dot_nt(dyb, w_ref[A:, :])

    return pl.pallas_call(
        body, name="mixout_bwd", grid=(T // tm,),
        in_specs=[pl.BlockSpec((tm, D), lambda i: (i, 0)), pl.BlockSpec(wo.shape, lambda i: (0, 0))],
        out_specs=[pl.BlockSpec((tm, D), lambda i: (i, 0)), pl.BlockSpec((tm, A), lambda i: (i, 0)),
                   pl.BlockSpec((tm, C), lambda i: (i, 0))],
        out_shape=[jax.ShapeDtypeStruct((T, D), BF16), jax.ShapeDtypeStruct((T, A), BF16),
                   jax.ShapeDtypeStruct((T, C), F32)],
        compiler_params=_cp(1),
    )(dxo, wo)


def _conv_bwd(dconv, ypre, u, w, lg, lb):
    T, CH = dconv.shape
    tm = min(TM, T)
    n = tm + HALO
    hb = tm // HALO
    nt = T // tm
    nchunk = tm // CONV_ROWS

    def body(dc_ref, dcn_ref, yp_ref, ypn_ref, uc_ref, up_ref, w_ref, lg_ref, lb_ref,
             du_ref, dw_ref, dvec_ref, zs_ref, zsh_ref, dy_ref, dysh_ref, dz_ref):
        i = pl.program_id(0)

        @pl.when(i == 0)
        def _():
            dw_ref[...] = jnp.zeros_like(dw_ref)
            dvec_ref[...] = jnp.zeros_like(dvec_ref)

        g, bb = lg_ref[...], lb_ref[...]

        def ln_bwd(dc, yp):
            mu = jnp.mean(yp, axis=-1, keepdims=True)
            d = yp - mu
            rs = lax.rsqrt(jnp.mean(d * d, axis=-1, keepdims=True) + EPS)
            yn = d * rs
            o = yn * g + bb
            sg = _sigmoid(o)
            do = dc * (sg * (1.0 + o * (1.0 - sg)))
            dyn = do * g
            dyp = rs * (dyn - jnp.mean(dyn, axis=-1, keepdims=True)
                        - yn * jnp.mean(dyn * yn, axis=-1, keepdims=True))
            return dyp, do, yn

        dyp, do, yn = ln_bwd(dc_ref[...], yp_ref[...])
        dvec_ref[0:1, :] += jnp.sum(dyp, axis=0, keepdims=True)
        dvec_ref[1:2, :] += jnp.sum(do * yn, axis=0, keepdims=True)
        dvec_ref[2:3, :] += jnp.sum(do, axis=0, keepdims=True)
        dy_ref[0:tm] = dyp
        dyh, _, _ = ln_bwd(dcn_ref[...], ypn_ref[...])
        dy_ref[tm:] = jnp.where(i < nt - 1, dyh, 0.0)
        _shift_copies(dy_ref, dysh_ref, n - 8)
        _fill_z(zs_ref, zsh_ref, uc_ref, up_ref, i, CH, n)

        def chunk(ci, carry):
            c0 = pl.multiple_of(ci * CONV_ROWS, CONV_ROWS)
            acc = jnp.zeros((CONV_ROWS, CH), F32)
            for k in range(CONV_W):
                acc = acc + w_ref[k:k + 1, :] * _tap(dy_ref, dysh_ref, CONV_W - 1 - k, c0)
            dz_ref[pl.ds(c0, CONV_ROWS), :] = acc
            return carry

        lax.fori_loop(0, nchunk, chunk, 0)

        for k in range(CONV_W):
            def red(ci, acc, k=k):
                c0 = pl.multiple_of(ci * CONV_ROWS, CONV_ROWS)
                prod = dy_ref[pl.ds(c0, CONV_ROWS), :] * _tap(zs_ref, zsh_ref, HALO - (CONV_W - 1) + k, c0)
                return acc + jnp.sum(prod.reshape(CONV_ROWS // 8, 8, CH), axis=0)

            acc = lax.fori_loop(0, nchunk, red, jnp.zeros((8, CH), F32))
            dw_ref[k:k + 1, :] += jnp.sum(acc, axis=0, keepdims=True)

        uc = uc_ref[...]
        a = uc[:, :CH]
        sg = _sigmoid(uc[:, CH:])
        dz = dz_ref[...]
        du_ref[:, :CH] = (dz * sg).astype(BF16)
        du_ref[:, CH:] = (dz * a * sg * (1.0 - sg)).astype(BF16)

    cur = lambda c: pl.BlockSpec((tm, c), lambda i: (i, 0))
    nxt = lambda c: pl.BlockSpec((HALO, c), lambda i: (jnp.minimum((i + 1) * hb, T // HALO - 1), 0))
    vec = pl.BlockSpec((1, CH), lambda i: (0, 0))
    return pl.pallas_call(
        body, name="conv_bwd", grid=(nt,),
        in_specs=[cur(CH), nxt(CH), cur(CH), nxt(CH), cur(2 * CH),
                  pl.BlockSpec((HALO, 2 * CH), lambda i: (jnp.maximum(i * hb - 1, 0), 0)),
                  pl.BlockSpec((CONV_W, CH), lambda i: (0, 0)), vec, vec],
        out_specs=[pl.BlockSpec((tm, 2 * CH), lambda i: (i, 0)), pl.BlockSpec((32, CH), lambda i: (0, 0)),
                   pl.BlockSpec((8, CH), lambda i: (0, 0))],
        out_shape=[jax.ShapeDtypeStruct((T, 2 * CH), BF16), jax.ShapeDtypeStruct((32, CH), F32),
                   jax.ShapeDtypeStruct((8, CH), F32)],
        scratch_shapes=[pltpu.VMEM((n, CH), F32), pltpu.VMEM((7, n - 8, CH), F32),
                        pltpu.VMEM((n, CH), F32), pltpu.VMEM((7, n - 8, CH), F32), pltpu.VMEM((tm, CH), F32)],
        compiler_params=_cp(1),
    )(dconv, dconv, ypre, ypre, u, u, w, lg, lb)


def _attn_bwd(sinks, tab, qkv, dattn):
    T = qkv.shape[0]
    nb = T // WINDOW

    def body(sink_ref, tab_ref, q_ref, kvp_ref, kvc_ref, do_ref, dq_ref, dkv_ref, dsk_ref, carry_ref):
        n = pl.program_id(0)

        @pl.when(n == 0)
        def _():
            dsk_ref[...] = jnp.zeros_like(dsk_ref)
            carry_ref[...] = jnp.zeros_like(carry_ref)

        @pl.when(n < nb)
        def _():
            seen = _first_block_mask(n)
            for g in range(N_KV):
                qs = _stack_heads(q_ref, g)
                dos = _stack_heads(do_ref, g)
                k = _band(kvp_ref, kvc_ref, g * HEAD_DIM)
                v = _band(kvp_ref, kvc_ref, KV_W + g * HEAD_DIM)
                p, ps = _attn_probs(qs, k, tab_ref[g], seen, _sink_col(sink_ref, g))
                dp = _dot_nt(dos, v)
                delta = jnp.sum(p * dp, axis=-1, keepdims=True)
                dsb = (p * (dp - delta)).astype(BF16)
                dsink = -ps * delta
                dqs = _dot(dsb, k) * SCALE
                dk = _dot_tn(dsb, qs) * SCALE
                dv = _dot_tn(p.astype(BF16), dos)
                for i in range(GROUP):
                    h = GROUP * g + i
                    dq_ref[:, h * HEAD_DIM:(h + 1) * HEAD_DIM] = dqs[i * WINDOW:(i + 1) * WINDOW].astype(BF16)
                    dsk_ref[h:h + 1, :] += jnp.sum(dsink[i * WINDOW:(i + 1) * WINDOW], axis=0, keepdims=True)
                for off, d in ((g * HEAD_DIM, dk), (KV_W + g * HEAD_DIM, dv)):
                    dkv_ref[:, off:off + HEAD_DIM] = (carry_ref[:, off:off + HEAD_DIM] + d[:WINDOW]).astype(BF16)
                    carry_ref[:, off:off + HEAD_DIM] = d[WINDOW:]

        @pl.when(n == nb)
        def _():
            dkv_ref[...] = carry_ref[...].astype(BF16)

    last = nb - 1
    return pl.pallas_call(
        body, name="attn_bwd", grid=(nb + 1,),
        in_specs=[pl.BlockSpec(memory_space=pltpu.SMEM),
                  pl.BlockSpec(tab.shape, lambda n: (0, 0, 0)),
                  pl.BlockSpec((WINDOW, ATTN_W), lambda n: (jnp.minimum(n, last), 0)),
                  pl.BlockSpec((WINDOW, 2 * KV_W), lambda n: (jnp.clip(n - 1, 0, last), 2)),
                  pl.BlockSpec((WINDOW, 2 * KV_W), lambda n: (jnp.minimum(n, last), 2)),
                  pl.BlockSpec((WINDOW, ATTN_W), lambda n: (jnp.minimum(n, last), 0))],
        out_specs=[pl.BlockSpec((WINDOW, ATTN_W), lambda n: (jnp.minimum(n, last), 0)),
                   pl.BlockSpec((WINDOW, 2 * KV_W), lambda n: (jnp.maximum(n - 1, 0), 0)),
                   pl.BlockSpec((8, LANES), lambda n: (0, 0))],
        out_shape=[jax.ShapeDtypeStruct((T, ATTN_W), BF16), jax.ShapeDtypeStruct((T, 2 * KV_W), BF16),
                   jax.ShapeDtypeStruct((8, LANES), F32)],
        scratch_shapes=[pltpu.VMEM((WINDOW, 2 * KV_W), F32)],
        compiler_params=_cp(1),
    )(sinks, tab, qkv, qkv, qkv, dattn)


def _pack(arrs):
    flat = jnp.concatenate([a.reshape(-1) for a in arrs])
    pad = -flat.shape[0] % (8 * LANES)
    return jnp.pad(flat, (0, pad)).reshape(1, -1, LANES)


def _unpack(packed, like):
    flat = packed.reshape(-1)
    out, off = [], 0
    for a in like:
        out.append(flat[off:off + a.size].reshape(a.shape))
        off += a.size
    return out


def kernel(x, norm_ffn1, w_ffn1_in, w_ffn1_out, norm_mix, w_in, sinks, w_dw, b_dw, conv_ln_g, conv_ln_b, w_out, norm_ffn2, w_ffn2_in, w_ffn2_out, final_norm, loss_target, m_norm_ffn1, m_w_ffn1_in, m_w_ffn1_out, m_norm_mix, m_w_in, m_sinks, m_w_dw, m_b_dw, m_conv_ln_g, m_conv_ln_b, m_w_out, m_norm_ffn2, m_w_ffn2_in, m_w_ffn2_out, m_final_norm, v_norm_ffn1, v_w_ffn1_in, v_w_ffn1_out, v_norm_mix, v_w_in, v_sinks, v_w_dw, v_b_dw, v_conv_ln_g, v_conv_ln_b, v_w_out, v_norm_ffn2, v_w_ffn2_in, v_w_ffn2_out, v_final_norm):
    L, D = norm_ffn1.shape
    T = x.shape[1]
    FB = w_ffn1_in.shape[2]
    CH = b_dw.shape[1]
    QKV = ATTN_W + 2 * KV_W
    xs = x.reshape(T, D)
    tgt = loss_target.reshape(T, D)
    cx, cy, cc = lax.axis_index("x"), lax.axis_index("y"), lax.axis_index("c")
    chip = 2 * cx + cy
    cidx = cc.reshape(1).astype(jnp.int32)
    big_w = (w_ffn1_in, w_ffn1_out, w_in, w_out, w_ffn2_in, w_ffn2_out)
    big_m = (m_w_ffn1_in, m_w_ffn1_out, m_w_in, m_w_out, m_w_ffn2_in, m_w_ffn2_out)
    big_v = (v_w_ffn1_in, v_w_ffn1_out, v_w_in, v_w_out, v_w_ffn2_in, v_w_ffn2_out)
    NW = len(big_w) + 1

    def shards(l, tok):
        return [(w_[l] + tok[0, 0]).astype(BF16) for w_ in big_w] + [w_dw[l] + tok[0, 0]]

    def own_slot(a, slots=4, idx=chip):
        return lax.dynamic_update_index_in_dim(lax.empty((slots,) + a.shape, a.dtype), a, idx, 0)

    def gather_start(srcs, tok):
        return _xchg_start("gather_start", srcs, [own_slot(s_) for s_ in srcs], _gather_plan, tok)

    def gather_arrived(started, after):
        _, lands, tok = _xchg_wait("gather_wait", started, NW, NW, _gather_plan, after)
        return _xchg_start("gshare_start", [], lands[:-1], _gshare_plan, tok, "sibling3"), lands[-1]

    row = lambda a, l: a[l].reshape(1, -1)
    tab = _attn_bias_table()
    NB = len(big_w)

    saved, W = [], []
    zero_tok = jnp.zeros((8, LANES), F32)
    started = gather_start(shards(0, zero_tok), zero_tok)
    cast = [None] + [shards(l, started[-1]) for l in range(1, L)]
    shared, gdw = gather_arrived(started, [xs] + [a_ for c_ in cast[1:] for a_ in c_])
    after = [shared[-1]]
    for l in range(L):
        _, (g1i, g1o, gi, go, g2i, g2o), tok = _xchg_wait("gshare_wait", shared, 0, NB, _gshare_plan, after, "sibling3")
        if l + 1 < L:
            started = gather_start(cast[l + 1], tok)
            tok = started[-1]
        w = dict(f1i=g1i, f1o=g1o.reshape(2 * FB, D), f2i=g2i, f2o=g2o.reshape(2 * FB, D),
                 wi=jnp.transpose(gi, (1, 0, 2)).reshape(D, -1), wo=go.reshape(-1, D),
                 wdw=jnp.transpose(gdw, (1, 0, 2)).reshape(CONV_W, CH))
        W.append(w)
        x0 = xs
        x1, gu1 = _ffn_fwd(x0, row(norm_ffn1, l) + tok[0, 0], w["f1i"], w["f1o"])
        qkv, u = _mixproj_fwd(x1, row(norm_mix, l), w["wi"])
        attn = _attn_fwd(row(sinks, l), tab, qkv)
        conv, ypre = _conv_fwd(u, w["wdw"], row(b_dw, l), row(conv_ln_g, l), row(conv_ln_b, l))
        x2 = _mixout_fwd(x1, attn, conv, w["wo"])
        g2_row = row(norm_ffn2, l)
        if l + 1 < L:
            shared, gdw = gather_arrived(started, [x2])
            g2_row = g2_row + shared[-1][0, 0]
        xs, gu2 = _ffn_fwd(x2, g2_row, w["f2i"], w["f2o"])
        saved.append((x0, gu1, x1, qkv, u, attn, conv, ypre, x2, gu2))
        after = [xs]

    loss_part, dx, d_final = _loss_head(xs, final_norm.reshape(1, D), tgt)
    loss = lax.psum(loss_part[0, 0], ("x", "y", "c"))

    bufs = [[lax.empty(w_.shape, F32) for _ in range(4)] for w_ in big_w]
    d_n1, d_nm, d_n2 = [None] * L, [None] * L, [None] * L
    d_sk, d_bdw, d_lg, d_lb, d_wdw = [None] * L, [None] * L, [None] * L, [None] * L, [None] * L

    def reduce_start(sib_started, after):
        gs, sibs, _ = _xchg_wait("sib_wait", sib_started, NB, NB, _sib_plan, after, "sibling")
        parts = [_sum_halves(cidx, g, s_) for g, s_ in zip(gs, sibs)]
        lands = [own_slot(lax.dynamic_index_in_dim(p, chip, 0, keepdims=False)) for p in parts]
        return _xchg_start("rs_start", parts, lands, _rs_plan, zero_tok)

    def share_start(rs_started, after):
        _, qs, tok = _xchg_wait("rs_wait", rs_started, NB, NB, _rs_plan, after)
        return _xchg_start("qshare_start", qs, [lax.empty(q.shape, q.dtype) for q in qs], _whole_plan, tok, "sibling")

    def finish(l, shared, after):
        q_own, q_sib, _ = _xchg_wait("qshare_wait", shared, NB, NB, _whole_plan, after, "sibling")
        for t in range(NB):
            bufs[t] = _adamw_layer(cidx, q_own[t], q_sib[t], big_w[t], big_m[t], big_v[t], bufs[t], l)

    sib_pending, rs_list = None, []
    tok = zero_tok
    for l in reversed(range(L)):
        w = W[l]
        x0, gu1, x1, qkv, u, attn, conv, ypre, x2, gu2 = saved[l]
        dx, d_n2[l], hb, dgu, a, dyb = _ffn_bwd(dx, x2, row(norm_ffn2, l), gu2, w["f2i"], w["f2o"], tok)
        g_f2i, g_f2o = _wgrad_ffn_in(hb, dgu), _wgrad_ffn_out(a, dyb)
        lg_row = row(conv_ln_g, l)
        if sib_pending is not None:
            rs_started = reduce_start(sib_pending[1], after=[g_f2o])
            rs_list.append((sib_pending[0], rs_started))
            lg_row = lg_row + rs_started[-1][0, 0]
        dyb, dattn, dconv = _mixout_bwd(dx, w["wo"])
        g_wo = _wgrad_cat([attn, conv], [dyb]).reshape(4, -1, D)
        du, dwdw, dvec = _conv_bwd(dconv, ypre, u, w["wdw"], lg_row, row(conv_ln_b, l))
        d_wdw[l], d_bdw[l], d_lg[l], d_lb[l] = dwdw[:CONV_W], dvec[0], dvec[1], dvec[2]
        dq, dkv, dsk = _attn_bwd(row(sinks, l), tab, qkv, dattn)
        d_sk[l] = dsk[:, 0]
        wi = w["wi"]
        dx, d_nm[l], hb = _mix_rms_bwd(dx, x1, row(norm_mix, l), [dq, dkv, du],
                                       [wi[:, :ATTN_W], wi[:, ATTN_W:QKV], wi[:, QKV:]])
        g_wi = jnp.transpose(_wgrad_cat([hb], [dq, dkv, du]).reshape(D, 4, -1), (1, 0, 2))
        dx, d_n1[l], hb, dgu, a, dyb = _ffn_bwd(dx, x0, row(norm_ffn1, l), gu1, w["f1i"], w["f1o"], tok)
        g_f1i, g_f1o = _wgrad_ffn_in(hb, dgu), _wgrad_ffn_out(a, dyb)
        gs = [g_f1i, g_f1o, g_wi, g_wo, g_f2i, g_f2o]
        sib_started = _xchg_start("sib_start", gs, [lax.empty((4, g.shape[1] // 2, g.shape[2]), F32) for g in gs],
                                  _sib_plan, zero_tok, "sibling")
        tok = sib_started[-1]
        sib_pending = (l, sib_started)
    grad_x = dx.reshape(x.shape)

    small_g = [jnp.concatenate(d, axis=0) for d in (d_n1, d_nm, d_n2)] + [d_final, jnp.stack(d_sk)] + \
              [jnp.stack(d) for d in (d_bdw, d_lg, d_lb, d_wdw)]
    packed = _pack(small_g)[0]
    small_started = _xchg_start("small_start", [packed], [own_slot(packed, 8, 4 * cx + 2 * cy + cc)], _slot_plan, tok, "all")
    rs_started = reduce_start(sib_pending[1], after=[small_started[-1]])
    rs_list.append((sib_pending[0], rs_started))

    after = [rs_started[-1]]
    shares = []
    for l, st in rs_list[:-1]:
        shares.append((l, share_start(st, after)))
        after = [shares[-1][1][-1]]
    for l, sh in shares:
        finish(l, sh, after)
        after = [bufs[0][0]]
    _, (slots,), _ = _xchg_wait("small_wait", small_started, 1, 1, _slot_plan, after, "all")
    small_sum = _unpack(_sum_slots(slots), small_g)
    g_wdw = lax.dynamic_slice_in_dim(small_sum[8], chip * w_dw.shape[2], w_dw.shape[2], axis=2)
    small_g = [small_sum[0], small_sum[1], small_sum[2], small_sum[3].reshape(D), small_sum[4],
               small_sum[5], small_sum[6], small_sum[7], g_wdw]
    small_w = (norm_ffn1, norm_mix, norm_ffn2, final_norm, sinks, b_dw, conv_ln_g, conv_ln_b, w_dw)
    small_m = (m_norm_ffn1, m_norm_mix, m_norm_ffn2, m_final_norm, m_sinks, m_b_dw, m_conv_ln_g, m_conv_ln_b, m_w_dw)
    small_v = (v_norm_ffn1, v_norm_mix, v_norm_ffn2, v_final_norm, v_sinks, v_b_dw, v_conv_ln_g, v_conv_ln_b, v_w_dw)
    upd = _adamw(_pack(small_g), _pack(small_w), _pack(small_m), _pack(small_v))
    small_upd = [_unpack(u_, small_w) for u_ in upd]
    last = share_start(rs_list[-1][1], [upd[0]])
    finish(rs_list[-1][0], last, [last[-1]])

    order = ("norm_ffn1", "w_ffn1_in", "w_ffn1_out", "norm_mix", "w_in", "sinks", "w_dw", "b_dw", "conv_ln_g",
             "conv_ln_b", "w_out", "norm_ffn2", "w_ffn2_in", "w_ffn2_out", "final_norm")
    small_names = ("norm_ffn1", "norm_mix", "norm_ffn2", "final_norm", "sinks", "b_dw", "conv_ln_g", "conv_ln_b", "w_dw")
    big_names = ("w_ffn1_in", "w_ffn1_out", "w_in", "w_out", "w_ffn2_in", "w_ffn2_out")
    grads, deltas, new_m, new_v = {}, {}, {}, {}
    for i, nme in enumerate(small_names):
        grads[nme], deltas[nme], new_m[nme], new_v[nme] = small_g[i], small_upd[0][i], small_upd[1][i], small_upd[2][i]
    for i, nme in enumerate(big_names):
        grads[nme], deltas[nme], new_m[nme], new_v[nme] = bufs[i]
    return (loss, grad_x, *[grads[n] for n in order], *[deltas[n] for n in order],
            *[new_m[n] for n in order], *[new_v[n] for n in order])
```

```python
import functools

import jax
import jax.numpy as jnp
from jax import lax
from jax.experimental import pallas as pl
from jax.experimental.pallas import tpu as pltpu

F32, BF16 = jnp.float32, jnp.bfloat16
EPS = 1e-6
NEG_INF = -1e30
HEAD_DIM = 64
N_HEADS = 8
N_KV = 2
GROUP = N_HEADS // N_KV
WINDOW = 128
ATTN_W = N_HEADS * HEAD_DIM
KV_W = N_KV * HEAD_DIM
CONV_W = 31
HALO = 32
CONV_ROWS = 32
SCALE = 1.0 / 8.0
ADAM_LR, ADAM_B1, ADAM_B2, ADAM_EPS, ADAM_WD, ADAM_STEP = 0.001, 0.9, 0.999, 1e-08, 0.01, 10
TM = 512
TM_FFN_BWD = 256
TK_WGRAD = 1024
LANES = 128
VMEM_LIMIT = 52 * 1024 * 1024
MESH = pl.DeviceIdType.MESH
ANY = pl.BlockSpec(memory_space=pl.ANY)
HBM = pl.BlockSpec(memory_space=pltpu.HBM)
SEM = pl.BlockSpec(memory_space=pltpu.SEMAPHORE)
VMEM = pl.BlockSpec(memory_space=pltpu.VMEM)
EFFECT = pltpu.SideEffectType.DATAFLOW_SIDE_EFFECTING
TOKEN = jax.ShapeDtypeStruct((8, LANES), F32)


def _cp(n):
    return pltpu.CompilerParams(dimension_semantics=("arbitrary",) * n, vmem_limit_bytes=VMEM_LIMIT)


def _dot(a, b):
    return jnp.dot(a, b, preferred_element_type=F32)


def _dot_nt(a, b):
    return lax.dot_general(a, b, (((1,), (1,)), ((), ())), preferred_element_type=F32)


def _dot_tn(a, b):
    return lax.dot_general(a, b, (((0,), (0,)), ((), ())), preferred_element_type=F32)


def _sigmoid(v):
    return 1.0 / (1.0 + jnp.exp(-v))


def _place():
    x, y, c = lax.axis_index("x"), lax.axis_index("y"), lax.axis_index("c")
    chips = [(1 - x, y), (x, 1 - y), (1 - x, 1 - y)]
    return x, y, c, chips


def _rcopy(src, dst, send_sems, recv_sems, k, dev):
    return pltpu.make_async_remote_copy(src_ref=src, dst_ref=dst, send_sem=send_sems.at[k],
                                        recv_sem=recv_sems.at[k], device_id=dev, device_id_type=MESH)


def _hbm(a):
    return pltpu.with_memory_space_constraint(a, pltpu.HBM)


PEERS = {"chips": 3, "sibling": 1, "sibling3": 3, "all": 7}


def _targets(mode):
    x, y, c, chips = _place()
    b = 2 * x + y
    if mode == "chips":
        return b, c, [((px, py, c), 2 * px + py) for px, py in chips]
    if mode == "sibling":
        return b, c, [((x, y, 1 - c), b)]
    if mode == "sibling3":
        return b, c, [((x, y, 1 - c), 2 * px + py) for px, py in chips]
    flip = lambda v, f: 1 - v if f else v
    devs = [(flip(x, k >> 2 & 1), flip(y, k >> 1 & 1), flip(c, k & 1)) for k in range(1, 8)]
    return 4 * x + 2 * y + c, c, [(d, 4 * d[0] + 2 * d[1] + d[2]) for d in devs]


def _xchg_start(name, srcs, lands, plan, dep, mode="chips"):
    ns, nl, npeer = len(srcs), len(lands), PEERS[mode]

    def body(*refs):
        land = refs[ns:ns + nl]
        src = refs[:ns] if ns else land
        send_sems, recv_sems, token = refs[ns + nl + 1], refs[ns + nl + 2], refs[-1]
        me, c, peers = _targets(mode)
        for t in range(nl):
            for j, (dev, tag) in enumerate(peers):
                s, d, _ = plan(src[t], land[t], t, me, c, tag)
                _rcopy(s, d, send_sems, recv_sems, npeer * t + j, dev).start()
        token[...] = jnp.zeros_like(token)

    arrs = list(srcs) + list(lands)
    return pl.pallas_call(
        body, name=name,
        out_shape=(pltpu.SemaphoreType.DMA((npeer * nl,)), pltpu.SemaphoreType.DMA((npeer * nl,)),
                   *[pltpu.HBM(a.shape, a.dtype) for a in arrs], TOKEN),
        in_specs=[HBM] * (ns + nl) + [ANY], out_specs=(SEM, SEM, *[HBM] * (ns + nl), VMEM),
        input_output_aliases={i: 2 + i for i in range(ns + nl)},
        compiler_params=pltpu.CompilerParams(has_side_effects=EFFECT),
    )(*[_hbm(a) for a in arrs], dep)


def _xchg_wait(name, started, ns, nl, plan, after, mode="chips"):
    send_sems, recv_sems, thru = started[0], started[1], started[2:2 + ns + nl]
    npeer = PEERS[mode]

    def body(*refs):
        land = refs[ns:ns + nl]
        src = refs[:ns] if ns else land
        send_sems, recv_sems, token = refs[ns + nl], refs[ns + nl + 1], refs[-1]
        me, c, peers = _targets(mode)
        for t in range(nl):
            for j, (dev, tag) in enumerate(peers):
                s, _, a = plan(src[t], land[t], t, me, c, tag)
                cp = _rcopy(s, a, send_sems, recv_sems, npeer * t + j, dev)
                cp.wait_send()
                cp.wait_recv()
        token[...] = jnp.zeros_like(token)

    out = pl.pallas_call(
        body, name=name,
        out_shape=(*[pltpu.HBM(a.shape, a.dtype) for a in thru], TOKEN),
        in_specs=[HBM] * (ns + nl) + [SEM, SEM] + [ANY] * len(after), out_specs=(*[HBM] * (ns + nl), VMEM),
        input_output_aliases={i: i for i in range(ns + nl)},
        compiler_params=pltpu.CompilerParams(has_side_effects=EFFECT),
    )(*thru, send_sems, recv_sems, *after)
    return out[:ns], out[ns:ns + nl], out[-1]


def _half(ref_rows, which):
    h = ref_rows // 2
    return pl.ds(which * h, h)


def _gather_plan(src, land, t, b, c, pb):
    if len(src.shape) == 2 and src.shape[0] % 2 == 0:
        hs = _half(src.shape[0], c)
        return src.at[hs], land.at[b, hs], land.at[pb, hs]
    return src, land.at[b], land.at[pb]


def _gshare_plan(src, land, t, b, c, pb):
    return land.at[pb, _half(land.shape[1], c)], land.at[pb, _half(land.shape[1], c)], land.at[pb, _half(land.shape[1], 1 - c)]


def _rs_plan(src, land, t, b, c, pb):
    return src.at[pb], land.at[b], land.at[pb]


def _sib_plan(src, land, t, b, c, pb):
    return src.at[:, _half(src.shape[1], 1 - c), :], land, land


def _rows_block(h, cap=512):
    for rb in range(min(h, cap) // 16 * 16, 0, -16):
        if h % rb == 0:
            return rb
    return h


def _sum_halves(cidx, g, s):
    _, R, C = g.shape
    rb = _rows_block(R // 2)
    nr = R // 2 // rb

    def body(c_ref, g_ref, s_ref, o_ref):
        o_ref[...] = (g_ref[...] + s_ref[...]).astype(BF16)

    blk = (None, rb, C)
    return pl.pallas_call(
        body, name="sum_halves", out_shape=jax.ShapeDtypeStruct(s.shape, BF16),
        grid_spec=pltpu.PrefetchScalarGridSpec(
            num_scalar_prefetch=1, grid=(4, nr),
            in_specs=[pl.BlockSpec(blk, lambda p, i, c: (p, c[0] * nr + i, 0)),
                      pl.BlockSpec(blk, lambda p, i, c: (p, i, 0))],
            out_specs=pl.BlockSpec(blk, lambda p, i, c: (p, i, 0))),
        compiler_params=_cp(2),
    )(cidx, g, s)


def _whole_plan(src, land, t, me, c, tag):
    return src, land, land


def _slot_plan(src, land, t, me, c, tag):
    return src, land.at[me], land.at[tag]


def _adam_update(gg, w, m, v):
    m2 = ADAM_B1 * m + (1.0 - ADAM_B1) * gg
    v2 = ADAM_B2 * v + (1.0 - ADAM_B2) * (gg * gg)
    mh = m2 / (1.0 - ADAM_B1 ** ADAM_STEP)
    vh = v2 / (1.0 - ADAM_B2 ** ADAM_STEP)
    return -ADAM_LR * (mh / (jnp.sqrt(vh) + ADAM_EPS) + ADAM_WD * w), m2, v2


def _adamw_layer(cidx, q_own, q_sib, w, m, v, bufs, l):
    L, R, C = w.shape
    h = R // 2
    rb = _rows_block(h, 256)
    nr = h // rb

    def body(c_ref, qo_ref, qs_ref, w_ref, m_ref, v_ref, *rest):
        g_ref, d_ref, mo_ref, vo_ref = rest[-4:]
        own = pl.program_id(0) == c_ref[0]
        gg = jnp.zeros((rb, C), F32)
        for s in range(4):
            gg = gg + jnp.where(own, qo_ref[s], qs_ref[s]).astype(F32)
        g_ref[...] = gg
        d_ref[...], mo_ref[...], vo_ref[...] = _adam_update(gg, w_ref[...], m_ref[...], v_ref[...])

    qspec = pl.BlockSpec((4, rb, C), lambda hh, i, c: (0, i, 0))
    wspec = pl.BlockSpec((None, rb, C), lambda hh, i, c: (l, hh * nr + i, 0))
    return pl.pallas_call(
        body, name="adamw_layer", out_shape=[jax.ShapeDtypeStruct(w.shape, F32)] * 4,
        grid_spec=pltpu.PrefetchScalarGridSpec(
            num_scalar_prefetch=1, grid=(2, nr),
            in_specs=[qspec, qspec, wspec, wspec, wspec] + [ANY] * 4, out_specs=[wspec] * 4),
        input_output_aliases={6 + k: k for k in range(4)},
        compiler_params=_cp(2),
    )(cidx, q_own, q_sib, w, m, v, *bufs)


def _adamw(g, w, m, v):
    L, R, C = g.shape
    rb = _rows_block(R)

    def body(g_ref, w_ref, m_ref, v_ref, d_ref, mo_ref, vo_ref):
        d_ref[...], mo_ref[...], vo_ref[...] = _adam_update(g_ref[...], w_ref[...], m_ref[...], v_ref[...])

    spec = pl.BlockSpec((None, rb, C), lambda l, i: (l, i, 0))
    return pl.pallas_call(
        body, name="adamw", grid=(L, R // rb), in_specs=[spec] * 4, out_specs=[spec] * 3,
        out_shape=[jax.ShapeDtypeStruct(g.shape, F32)] * 3, compiler_params=_cp(2),
    )(g, w, m, v)


def _sum_slots(buf):
    def body(b_ref, o_ref):
        acc = b_ref[0]
        for k in range(1, 8):
            acc = acc + b_ref[k]
        o_ref[...] = acc

    return pl.pallas_call(body, name="sum_slots", in_specs=[VMEM], out_specs=VMEM,
                          out_shape=jax.ShapeDtypeStruct(buf.shape[1:], F32))(buf)


def _rms(xf, g):
    r = lax.rsqrt(jnp.mean(xf * xf, axis=-1, keepdims=True) + EPS)
    return xf * r, r


def _lane_chunks(n):
    lo = (n // LANES + 1) // 2 * LANES
    return ((0, lo), (lo, n - lo))


def _load_ffn_weights(win_hbm, wout_hbm, win_v, wout_v, sems):
    fb = win_v.shape[2]
    loads = [pltpu.make_async_copy(win_hbm.at[k], win_v.at[k], sems.at[k]) for k in range(4)]
    loads += [pltpu.make_async_copy(wout_hbm.at[pl.ds(k * fb, fb)], wout_v.at[pl.ds(k * fb, fb)], sems.at[4 + k])
              for k in range(2)]
    for cp in loads:
        cp.start()
    for cp in loads:
        cp.wait()


def _fast_sigmoid(v):
    return pl.reciprocal(1.0 + jnp.exp(-v), approx=True)


def _ffn_fwd(x, g, win, wout):
    T, D = x.shape
    FB = win.shape[2]
    tm = min(TM, T)

    def body(x_ref, g_ref, win_hbm, wout_hbm, xo_ref, gu_ref, win_v, wout_v, sems):
        @pl.when(pl.program_id(0) == 0)
        def _():
            _load_ffn_weights(win_hbm, wout_hbm, win_v, wout_v, sems)

        xf = x_ref[...]
        xh, _ = _rms(xf, None)
        h = (xh * g_ref[...]).astype(BF16)
        acc = jnp.zeros((tm, D), F32)
        for blk in range(2):
            for lo, sz in _lane_chunks(FB):
                cols = pl.ds(blk * FB + lo, sz)
                gate = _dot(h, win_v[blk, :, pl.ds(lo, sz)])
                up = _dot(h, win_v[2 + blk, :, pl.ds(lo, sz)])
                gu_ref[0, :, cols] = gate.astype(BF16)
                gu_ref[1, :, cols] = up.astype(BF16)
                a = (gate * _fast_sigmoid(gate) * up).astype(BF16)
                acc = acc + _dot(a, wout_v[cols, :])
        xo_ref[...] = xf + 0.5 * acc

    row = pl.BlockSpec((tm, D), lambda i: (i, 0))
    return pl.pallas_call(
        body, name="ffn_fwd", grid=(T // tm,),
        in_specs=[row, pl.BlockSpec((1, D), lambda i: (0, 0)), ANY, ANY],
        out_specs=[row, pl.BlockSpec((2, tm, 2 * FB), lambda i: (0, i, 0))],
        out_shape=[jax.ShapeDtypeStruct((T, D), F32), jax.ShapeDtypeStruct((2, T, 2 * FB), BF16)],
        scratch_shapes=[pltpu.VMEM(win.shape, BF16), pltpu.VMEM(wout.shape, BF16), pltpu.SemaphoreType.DMA((6,))],
        compiler_params=_cp(1),
    )(x, g, win, wout)


def _mixproj_fwd(x, g, w):
    T, D = x.shape
    W = w.shape[1]
    QKV = ATTN_W + 2 * KV_W
    tm = min(TM, T)

    def body(x_ref, g_ref, w_ref, qkv_ref, u_ref):
        xh, _ = _rms(x_ref[...], None)
        h = (xh * g_ref[...]).astype(BF16)
        qkv_ref[...] = _dot(h, w_ref[:, :QKV]).astype(BF16)
        u_ref[...] = _dot(h, w_ref[:, QKV:])

    return pl.pallas_call(
        body, name="mixproj_fwd", grid=(T // tm,),
        in_specs=[pl.BlockSpec((tm, D), lambda i: (i, 0)), pl.BlockSpec((1, D), lambda i: (0, 0)),
                  pl.BlockSpec((D, W), lambda i: (0, 0))],
        out_specs=[pl.BlockSpec((tm, QKV), lambda i: (i, 0)), pl.BlockSpec((tm, W - QKV), lambda i: (i, 0))],
        out_shape=[jax.ShapeDtypeStruct((T, QKV), BF16), jax.ShapeDtypeStruct((T, W - QKV), F32)],
        compiler_params=_cp(1),
    )(x, g, w)


def _attn_bias_table():
    rows, cols = GROUP * WINDOW, 2 * WINDOW
    row = lax.broadcasted_iota(jnp.int32, (N_KV, rows, cols), 1)
    col = lax.broadcasted_iota(jnp.int32, (N_KV, rows, cols), 2)
    head = GROUP * lax.broadcasted_iota(jnp.int32, (N_KV, rows, cols), 0) + (row >> 7)
    dist = (row & (WINDOW - 1)) + WINDOW - col
    slope = jnp.exp2(-(head + 1).astype(F32))
    return jnp.where((dist >= 0) & (dist < WINDOW), -slope * dist.astype(F32), NEG_INF)


def _first_block_mask(n):
    col = lax.broadcasted_iota(jnp.int32, (GROUP * WINDOW, 2 * WINDOW), 1)
    return (n > 0) | (col >= WINDOW)


def _sink_col(sink_ref, g):
    hi = lax.broadcasted_iota(jnp.int32, (GROUP * WINDOW, 1), 0) >> 7
    col = jnp.zeros((GROUP * WINDOW, 1), F32)
    for i in range(GROUP):
        col = jnp.where(hi == i, sink_ref[0, GROUP * g + i], col)
    return col


def _stack_heads(ref, g):
    return jnp.concatenate([ref[:, (GROUP * g + i) * HEAD_DIM:(GROUP * g + i + 1) * HEAD_DIM]
                            for i in range(GROUP)], axis=0)


def _band(kvp_ref, kvc_ref, off):
    return jnp.concatenate([kvp_ref[:, off:off + HEAD_DIM], kvc_ref[:, off:off + HEAD_DIM]], axis=0)


def _attn_probs(qs, k, bias, seen, sink):
    s = jnp.where(seen, _dot_nt(qs, k) * SCALE + bias, NEG_INF)
    m = jnp.maximum(jnp.max(s, axis=-1, keepdims=True), sink)
    p = jnp.exp(s - m)
    es = jnp.exp(sink - m)
    den = jnp.sum(p, axis=-1, keepdims=True) + es
    return p / den, es / den


def _attn_fwd(sinks, tab, qkv):
    T = qkv.shape[0]
    nb = T // WINDOW

    def body(sink_ref, tab_ref, q_ref, kvp_ref, kvc_ref, o_ref):
        seen = _first_block_mask(pl.program_id(0))
        for g in range(N_KV):
            qs = _stack_heads(q_ref, g)
            k = _band(kvp_ref, kvc_ref, g * HEAD_DIM)
            v = _band(kvp_ref, kvc_ref, KV_W + g * HEAD_DIM)
            p, _ = _attn_probs(qs, k, tab_ref[g], seen, _sink_col(sink_ref, g))
            o = _dot(p.astype(BF16), v)
            for i in range(GROUP):
                h = GROUP * g + i
                o_ref[:, h * HEAD_DIM:(h + 1) * HEAD_DIM] = o[i * WINDOW:(i + 1) * WINDOW].astype(BF16)

    return pl.pallas_call(
        body, name="attn_fwd", grid=(nb,),
        in_specs=[pl.BlockSpec(memory_space=pltpu.SMEM),
                  pl.BlockSpec(tab.shape, lambda n: (0, 0, 0)),
                  pl.BlockSpec((WINDOW, ATTN_W), lambda n: (n, 0)),
                  pl.BlockSpec((WINDOW, 2 * KV_W), lambda n: (jnp.maximum(n - 1, 0), 2)),
                  pl.BlockSpec((WINDOW, 2 * KV_W), lambda n: (n, 2))],
        out_specs=pl.BlockSpec((WINDOW, ATTN_W), lambda n: (n, 0)),
        out_shape=jax.ShapeDtypeStruct((T, ATTN_W), BF16),
        compiler_params=_cp(1),
    )(sinks, tab, qkv, qkv, qkv)


def _shift_copies(src_ref, dst_ref, n):
    for b in range(1, 8):
        dst_ref[b - 1] = src_ref[b:b + n, :]


def _tap(src_ref, sh_ref, s, c0):
    a, b = divmod(s, 8)
    start = pl.multiple_of(c0 + 8 * a, 8)
    if b == 0:
        return src_ref[pl.ds(start, CONV_ROWS), :]
    return sh_ref[b - 1, pl.ds(start, CONV_ROWS), :]


def _glu_rows(u, ch):
    return u[:, :ch] * _sigmoid(u[:, ch:])


def _fill_z(zs_ref, zsh_ref, uc_ref, up_ref, i, ch, n):
    zs_ref[0:HALO] = jnp.where(i > 0, _glu_rows(up_ref[...], ch), 0.0)
    zs_ref[HALO:] = _glu_rows(uc_ref[...], ch)
    _shift_copies(zs_ref, zsh_ref, n - 8)


def _conv_fwd(u, w, b, lg, lb):
    T = u.shape[0]
    CH = u.shape[1] // 2
    tm = min(TM, T)
    n = tm + HALO
    hb = tm // HALO

    def body(uc_ref, up_ref, w_ref, b_ref, lg_ref, lb_ref, conv_ref, ypre_ref, zs_ref, zsh_ref):
        i = pl.program_id(0)
        _fill_z(zs_ref, zsh_ref, uc_ref, up_ref, i, CH, n)
        bias = b_ref[...]

        def chunk(ci, carry):
            c0 = pl.multiple_of(ci * CONV_ROWS, CONV_ROWS)
            acc = jnp.broadcast_to(bias, (CONV_ROWS, CH))
            for k in range(CONV_W):
                acc = acc + w_ref[k:k + 1, :] * _tap(zs_ref, zsh_ref, HALO - (CONV_W - 1) + k, c0)
            ypre_ref[pl.ds(c0, CONV_ROWS), :] = acc
            return carry

        lax.fori_loop(0, tm // CONV_ROWS, chunk, 0)
        y = ypre_ref[...]
        mu = jnp.mean(y, axis=-1, keepdims=True)
        d = y - mu
        var = jnp.mean(d * d, axis=-1, keepdims=True)
        o = d * lax.rsqrt(var + EPS) * lg_ref[...] + lb_ref[...]
        conv_ref[...] = (o * _sigmoid(o)).astype(BF16)

    vec = pl.BlockSpec((1, CH), lambda i: (0, 0))
    return pl.pallas_call(
        body, name="conv_fwd", grid=(T // tm,),
        in_specs=[pl.BlockSpec((tm, 2 * CH), lambda i: (i, 0)),
                  pl.BlockSpec((HALO, 2 * CH), lambda i: (jnp.maximum(i * hb - 1, 0), 0)),
                  pl.BlockSpec((CONV_W, CH), lambda i: (0, 0)), vec, vec, vec],
        out_specs=[pl.BlockSpec((tm, CH), lambda i: (i, 0)), pl.BlockSpec((tm, CH), lambda i: (i, 0))],
        out_shape=[jax.ShapeDtypeStruct((T, CH), BF16), jax.ShapeDtypeStruct((T, CH), F32)],
        scratch_shapes=[pltpu.VMEM((n, CH), F32), pltpu.VMEM((7, n - 8, CH), F32)],
        compiler_params=_cp(1),
    )(u, u, w, b, lg, lb)


def _mixout_fwd(x, attn, conv, wo):
    T, D = x.shape
    tm = min(TM, T)
    A = attn.shape[1]

    def body(x_ref, a_ref, c_ref, w_ref, xo_ref):
        xo_ref[...] = x_ref[...] + _dot(a_ref[...], w_ref[:A, :]) + _dot(c_ref[...], w_ref[A:, :])

    return pl.pallas_call(
        body, name="mixout_fwd", grid=(T // tm,),
        in_specs=[pl.BlockSpec((tm, D), lambda i: (i, 0)), pl.BlockSpec((tm, A), lambda i: (i, 0)),
                  pl.BlockSpec((tm, conv.shape[1]), lambda i: (i, 0)), pl.BlockSpec(wo.shape, lambda i: (0, 0))],
        out_specs=pl.BlockSpec((tm, D), lambda i: (i, 0)),
        out_shape=jax.ShapeDtypeStruct((T, D), F32),
        compiler_params=_cp(1),
    )(x, attn, conv, wo)


def _rms_bwd_rows(dh, xf, g):
    xh, r = _rms(xf, None)
    dxn = dh * g
    dx = r * (dxn - xh * jnp.mean(dxn * xh, axis=-1, keepdims=True))
    return dx, jnp.sum(dh * xh, axis=0, keepdims=True), xh * g


def _loss_head(x, g, tgt):
    T, D = x.shape
    tm = min(TM, T)

    def body(x_ref, g_ref, t_ref, loss_ref, dx_ref, dg_ref):
        @pl.when(pl.program_id(0) == 0)
        def _():
            loss_ref[...] = jnp.zeros_like(loss_ref)
            dg_ref[...] = jnp.zeros_like(dg_ref)

        xf = x_ref[...]
        g = g_ref[...]
        xh, _ = _rms(xf, None)
        e = xh * g - t_ref[...]
        loss_ref[...] += 0.5 * jnp.sum(jnp.mean(e * e, axis=-1, keepdims=True), axis=0, keepdims=True)
        dx, dg, _ = _rms_bwd_rows(e * (1.0 / D), xf, g)
        dx_ref[...] = dx
        dg_ref[...] += dg

    return pl.pallas_call(
        body, name="loss_head", grid=(T // tm,),
        in_specs=[pl.BlockSpec((tm, D), lambda i: (i, 0)), pl.BlockSpec((1, D), lambda i: (0, 0)),
                  pl.BlockSpec((tm, D), lambda i: (i, 0))],
        out_specs=[pl.BlockSpec((1, 1), lambda i: (0, 0)), pl.BlockSpec((tm, D), lambda i: (i, 0)),
                   pl.BlockSpec((1, D), lambda i: (0, 0))],
        out_shape=[jax.ShapeDtypeStruct((1, 1), F32), jax.ShapeDtypeStruct((T, D), F32),
                   jax.ShapeDtypeStruct((1, D), F32)],
        compiler_params=_cp(1),
    )(x, g, tgt)


def _ffn_bwd(dxo, x, g, gu, win, wout, dep):
    T, D = x.shape
    FB = win.shape[2]
    tm = min(TM_FFN_BWD, T)

    def body(dxo_ref, x_ref, g_ref, gu_ref, win_hbm, wout_hbm, dep_ref,
             dxi_ref, dg_ref, hb_ref, dgu_ref, a_ref, dyb_ref, win_v, wout_v, sems):
        @pl.when(pl.program_id(0) == 0)
        def _():
            _load_ffn_weights(win_hbm, wout_hbm, win_v, wout_v, sems)
            dg_ref[...] = jnp.zeros_like(dg_ref)

        dyb = (0.5 * dxo_ref[...]).astype(BF16)
        dyb_ref[...] = dyb
        dh = jnp.zeros((tm, D), F32)
        for blk in range(2):
            for lo, sz in _lane_chunks(FB):
                cols = pl.ds(blk * FB + lo, sz)
                da = _dot_nt(dyb, wout_v[cols, :])
                gate = gu_ref[0, :, cols].astype(F32)
                up = gu_ref[1, :, cols].astype(F32)
                sg = _fast_sigmoid(gate)
                s = gate * sg
                a_ref[:, cols] = (s * up).astype(BF16)
                dgate = (da * up * (sg * (1.0 + gate * (1.0 - sg)))).astype(BF16)
                dup = (da * s).astype(BF16)
                dgu_ref[0, :, cols] = dgate
                dgu_ref[1, :, cols] = dup
                dh = dh + _dot_nt(dgate, win_v[blk, :, pl.ds(lo, sz)]) + _dot_nt(dup, win_v[2 + blk, :, pl.ds(lo, sz)])
        dx, dg, h = _rms_bwd_rows(dh, x_ref[...], g_ref[...])
        dxi_ref[...] = dxo_ref[...] + dx
        dg_ref[...] += dg
        hb_ref[...] = h.astype(BF16)

    row = pl.BlockSpec((tm, D), lambda i: (i, 0))
    act = pl.BlockSpec((2, tm, 2 * FB), lambda i: (0, i, 0))
    return pl.pallas_call(
        body, name="ffn_bwd", grid=(T // tm,),
        in_specs=[row, row, pl.BlockSpec((1, D), lambda i: (0, 0)), act, ANY, ANY, ANY],
        out_specs=[row, pl.BlockSpec((1, D), lambda i: (0, 0)), row, act,
                   pl.BlockSpec((tm, 2 * FB), lambda i: (i, 0)), row],
        out_shape=[jax.ShapeDtypeStruct((T, D), F32), jax.ShapeDtypeStruct((1, D), F32),
                   jax.ShapeDtypeStruct((T, D), BF16), jax.ShapeDtypeStruct((2, T, 2 * FB), BF16),
                   jax.ShapeDtypeStruct((T, 2 * FB), BF16), jax.ShapeDtypeStruct((T, D), BF16)],
        scratch_shapes=[pltpu.VMEM(win.shape, BF16), pltpu.VMEM(wout.shape, BF16), pltpu.SemaphoreType.DMA((6,))],
        compiler_params=_cp(1),
    )(dxo, x, g, gu, win, wout, dep)


def _rms_matmul_bwd(name, dxo, x, g, dzs, ws, dz_specs, w_specs, nk):
    T, D = x.shape
    tm = min(TM, T)
    npair = len(dzs)

    def body(*refs):
        dxo_ref, x_ref, g_ref = refs[:3]
        dz_refs, w_refs = refs[3:3 + npair], refs[3 + npair:3 + 2 * npair]
        dxi_ref, dg_ref, hb_ref, acc_ref = refs[3 + 2 * npair:]
        i, k = pl.program_id(0), pl.program_id(1)

        @pl.when(k == 0)
        def _():
            acc_ref[...] = jnp.zeros_like(acc_ref)

        @pl.when((i == 0) & (k == 0))
        def _():
            dg_ref[...] = jnp.zeros_like(dg_ref)

        for p in range(npair):
            acc_ref[...] += _dot_nt(dz_refs[p][...], w_refs[p][...])

        @pl.when(k == nk - 1)
        def _():
            dx, dg, h = _rms_bwd_rows(acc_ref[...], x_ref[...], g_ref[...])
            dxi_ref[...] = dxo_ref[...] + dx
            dg_ref[...] += dg
            hb_ref[...] = h.astype(BF16)

    row = pl.BlockSpec((tm, D), lambda i, k: (i, 0))
    return pl.pallas_call(
        body, name=name, grid=(T // tm, nk),
        in_specs=[row, row, pl.BlockSpec((1, D), lambda i, k: (0, 0))] + list(dz_specs) + list(w_specs),
        out_specs=[row, pl.BlockSpec((1, D), lambda i, k: (0, 0)), row],
        out_shape=[jax.ShapeDtypeStruct((T, D), F32), jax.ShapeDtypeStruct((1, D), F32),
                   jax.ShapeDtypeStruct((T, D), BF16)],
        scratch_shapes=[pltpu.VMEM((tm, D), F32)],
        compiler_params=_cp(2),
    )(dxo, x, g, *dzs, *ws)


def _mix_rms_bwd(dxo, x, g, dzs, ws):
    tm = min(TM, x.shape[0])
    return _rms_matmul_bwd(
        "mix_rms_bwd", dxo, x, g, dzs, ws,
        [pl.BlockSpec((tm, dz.shape[1]), lambda i, k: (i, 0)) for dz in dzs],
        [pl.BlockSpec(w.shape, lambda i, k: (0, 0)) for w in ws], 1)


def _wgrad(name, a, b, a_spec, b_spec, out_shape, out_spec, nblk):
    T = a.shape[0]
    tk = min(TK_WGRAD, T)

    def body(a_ref, b_ref, o_ref):
        @pl.when(pl.program_id(1) == 0)
        def _():
            o_ref[...] = jnp.zeros_like(o_ref)

        o_ref[...] += _dot_tn(a_ref[...], b_ref[...]).reshape(o_ref.shape)

    return pl.pallas_call(
        body, name=name, grid=(nblk, T // tk), in_specs=[a_spec, b_spec], out_specs=out_spec,
        out_shape=jax.ShapeDtypeStruct(out_shape, F32), compiler_params=_cp(2),
    )(a, b)


def _wgrad_ffn_in(hb, dgu):
    T, D = hb.shape
    FB = dgu.shape[2] // 2
    tk = min(TK_WGRAD, T)
    return _wgrad("wgrad_ffn_in", hb, dgu,
                  pl.BlockSpec((tk, D), lambda b, k: (k, 0)),
                  pl.BlockSpec((None, tk, FB), lambda b, k: (b // 2, k, b % 2)),
                  (4, D, FB), pl.BlockSpec((None, D, FB), lambda b, k: (b, 0, 0)), 4)


def _wgrad_ffn_out(a, dyb):
    T, D = dyb.shape
    FB = a.shape[1] // 2
    tk = min(TK_WGRAD, T)
    return _wgrad("wgrad_ffn_out", a, dyb,
                  pl.BlockSpec((tk, FB), lambda b, k: (k, b)),
                  pl.BlockSpec((tk, D), lambda b, k: (k, 0)),
                  (4, FB // 2, D), pl.BlockSpec((2, FB // 2, D), lambda b, k: (b, 0, 0)), 2)


def _wgrad_cat(a_list, b_list):
    T = a_list[0].shape[0]
    tk = min(TK_WGRAD, T)
    na = len(a_list)
    M, N = sum(a.shape[1] for a in a_list), sum(b.shape[1] for b in b_list)

    def body(*refs):
        a_refs, b_refs, o_ref = refs[:na], refs[na:-1], refs[-1]

        @pl.when(pl.program_id(0) == 0)
        def _():
            o_ref[...] = jnp.zeros_like(o_ref)

        r0 = 0
        for a_ref in a_refs:
            c0 = 0
            for b_ref in b_refs:
                m, n = a_ref.shape[1], b_ref.shape[1]
                o_ref[r0:r0 + m, c0:c0 + n] += _dot_tn(a_ref[...], b_ref[...])
                c0 += n
            r0 += a_ref.shape[1]

    return pl.pallas_call(
        body, name="wgrad_cat", grid=(T // tk,),
        in_specs=[pl.BlockSpec((tk, v.shape[1]), lambda k: (k, 0)) for v in list(a_list) + list(b_list)],
        out_specs=pl.BlockSpec((M, N), lambda k: (0, 0)),
        out_shape=jax.ShapeDtypeStruct((M, N), F32), compiler_params=_cp(1),
    )(*a_list, *b_list)


def _mixout_bwd(dxo, wo):
    T, D = dxo.shape
    tm = min(TM, T)
    A = ATTN_W
    C = wo.shape[0] - A

    def body(dxo_ref, w_ref, dyb_ref, da_ref, dc_ref):
        dyb = dxo_ref[...].astype(BF16)
        dyb_ref[...] = dyb
        da_ref[...] = _dot_nt(dyb, w_ref[:A, :]).astype(BF16)
        dc_ref[...] = _dot_nt(dyb, w_ref[A:, :])

    return pl.pallas_call(
        body, name="mixout_bwd", grid=(T // tm,),
        in_specs=[pl.BlockSpec((tm, D), lambda i: (i, 0)), pl.BlockSpec(wo.shape, lambda i: (0, 0))],
        out_specs=[pl.BlockSpec((tm, D), lambda i: (i, 0)), pl.BlockSpec((tm, A), lambda i: (i, 0)),
                   pl.BlockSpec((tm, C), lambda i: (i, 0))],
        out_shape=[jax.ShapeDtypeStruct((T, D), BF16), jax.ShapeDtypeStruct((T, A), BF16),
                   jax.ShapeDtypeStruct((T, C), F32)],
        compiler_params=_cp(1),
    )(dxo, wo)


def _conv_bwd(dconv, ypre, u, w, lg, lb):
    T, CH = dconv.shape
    tm = min(TM, T)
    n = tm + HALO
    hb = tm // HALO
    nt = T // tm
    nchunk = tm // CONV_ROWS

    def body(dc_ref, dcn_ref, yp_ref, ypn_ref, uc_ref, up_ref, w_ref, lg_ref, lb_ref,
             du_ref, dw_ref, dvec_ref, zs_ref, zsh_ref, dy_ref, dysh_ref, dz_ref, dwacc_ref):
        i = pl.program_id(0)

        @pl.when(i == 0)
        def _():
            dwacc_ref[...] = jnp.zeros_like(dwacc_ref)
            dvec_ref[...] = jnp.zeros_like(dvec_ref)

        g, bb = lg_ref[...], lb_ref[...]

        def ln_bwd(dc, yp):
            mu = jnp.mean(yp, axis=-1, keepdims=True)
            d = yp - mu
            rs = lax.rsqrt(jnp.mean(d * d, axis=-1, keepdims=True) + EPS)
            yn = d * rs
            o = yn * g + bb
            sg = _sigmoid(o)
            do = dc * (sg * (1.0 + o * (1.0 - sg)))
            dyn = do * g
            dyp = rs * (dyn - jnp.mean(dyn, axis=-1, keepdims=True)
                        - yn * jnp.mean(dyn * yn, axis=-1, keepdims=True))
            return dyp, do, yn

        dyp, do, yn = ln_bwd(dc_ref[...], yp_ref[...])
        dvec_ref[0:1, :] += jnp.sum(dyp, axis=0, keepdims=True)
        dvec_ref[1:2, :] += jnp.sum(do * yn, axis=0, keepdims=True)
        dvec_ref[2:3, :] += jnp.sum(do, axis=0, keepdims=True)
        dy_ref[0:tm] = dyp
        dyh, _, _ = ln_bwd(dcn_ref[...], ypn_ref[...])
        dy_ref[tm:] = jnp.where(i < nt - 1, dyh, 0.0)
        _shift_copies(dy_ref, dysh_ref, n - 8)
        _fill_z(zs_ref, zsh_ref, uc_ref, up_ref, i, CH, n)

        def chunk(ci, carry):
            c0 = pl.multiple_of(ci * CONV_ROWS, CONV_ROWS)
            acc = jnp.zeros((CONV_ROWS, CH), F32)
            for k in range(CONV_W):
                acc = acc + w_ref[k:k + 1, :] * _tap(dy_ref, dysh_ref, CONV_W - 1 - k, c0)
            dz_ref[pl.ds(c0, CONV_ROWS), :] = acc
            dyc = dy_ref[pl.ds(c0, CONV_ROWS), :]
            for k in range(CONV_W):
                prod = dyc * _tap(zs_ref, zsh_ref, HALO - (CONV_W - 1) + k, c0)
                dwacc_ref[k] += jnp.sum(prod.reshape(CONV_ROWS // 8, 8, CH), axis=0)
            return carry

        lax.fori_loop(0, nchunk, chunk, 0)

        @pl.when(i == nt - 1)
        def _():
            dw_ref[...] = jnp.sum(dwacc_ref[...], axis=1)

        uc = uc_ref[...]
        a = uc[:, :CH]
        sg = _sigmoid(uc[:, CH:])
        dz = dz_ref[...]
        du_ref[:, :CH] = (dz * sg).astype(BF16)
        du_ref[:, CH:] = (dz * a * sg * (1.0 - sg)).astype(BF16)

    cur = lambda c: pl.BlockSpec((tm, c), lambda i: (i, 0))
    nxt = lambda c: pl.BlockSpec((HALO, c), lambda i: (jnp.minimum((i + 1) * hb, T // HALO - 1), 0))
    vec = pl.BlockSpec((1, CH), lambda i: (0, 0))
    return pl.pallas_call(
        body, name="conv_bwd", grid=(nt,),
        in_specs=[cur(CH), nxt(CH), cur(CH), nxt(CH), cur(2 * CH),
                  pl.BlockSpec((HALO, 2 * CH), lambda i: (jnp.maximum(i * hb - 1, 0), 0)),
                  pl.BlockSpec((CONV_W, CH), lambda i: (0, 0)), vec, vec],
        out_specs=[pl.BlockSpec((tm, 2 * CH), lambda i: (i, 0)), pl.BlockSpec((32, CH), lambda i: (0, 0)),
                   pl.BlockSpec((8, CH), lambda i: (0, 0))],
        out_shape=[jax.ShapeDtypeStruct((T, 2 * CH), BF16), jax.ShapeDtypeStruct((32, CH), F32),
                   jax.ShapeDtypeStruct((8, CH), F32)],
        scratch_shapes=[pltpu.VMEM((n, CH), F32), pltpu.VMEM((7, n - 8, CH), F32),
                        pltpu.VMEM((n, CH), F32), pltpu.VMEM((7, n - 8, CH), F32), pltpu.VMEM((tm, CH), F32),
                        pltpu.VMEM((32, 8, CH), F32)],
        compiler_params=_cp(1),
    )(dconv, dconv, ypre, ypre, u, u, w, lg, lb)


def _attn_bwd(sinks, tab, qkv, dattn):
    T = qkv.shape[0]
    nb = T // WINDOW

    def body(sink_ref, tab_ref, q_ref, kvp_ref, kvc_ref, do_ref, dq_ref, dkv_ref, dsk_ref, carry_ref):
        n = pl.program_id(0)

        @pl.when(n == 0)
        def _():
            dsk_ref[...] = jnp.zeros_like(dsk_ref)
            carry_ref[...] = jnp.zeros_like(carry_ref)

        @pl.when(n < nb)
        def _():
            seen = _first_block_mask(n)
            for g in range(N_KV):
                qs = _stack_heads(q_ref, g)
                dos = _stack_heads(do_ref, g)
                k = _band(kvp_ref, kvc_ref, g * HEAD_DIM)
                v = _band(kvp_ref, kvc_ref, KV_W + g * HEAD_DIM)
                p, ps = _attn_probs(qs, k, tab_ref[g], seen, _sink_col(sink_ref, g))
                dp = _dot_nt(dos, v)
                delta = jnp.sum(p * dp, axis=-1, keepdims=True)
                dsb = (p * (dp - delta)).astype(BF16)
                dsink = -ps * delta
                dqs = _dot(dsb, k) * SCALE
                dk = _dot_tn(dsb, qs) * SCALE
                dv = _dot_tn(p.astype(BF16), dos)
                for i in range(GROUP):
                    h = GROUP * g + i
                    dq_ref[:, h * HEAD_DIM:(h + 1) * HEAD_DIM] = dqs[i * WINDOW:(i + 1) * WINDOW].astype(BF16)
                    dsk_ref[h:h + 1, :] += jnp.sum(dsink[i * WINDOW:(i + 1) * WINDOW], axis=0, keepdims=True)
                for off, d in ((g * HEAD_DIM, dk), (KV_W + g * HEAD_DIM, dv)):
                    dkv_ref[:, off:off + HEAD_DIM] = (carry_ref[:, off:off + HEAD_DIM] + d[:WINDOW]).astype(BF16)
                    carry_ref[:, off:off + HEAD_DIM] = d[WINDOW:]

        @pl.when(n == nb)
        def _():
            dkv_ref[...] = carry_ref[...].astype(BF16)

    last = nb - 1
    return pl.pallas_call(
        body, name="attn_bwd", grid=(nb + 1,),
        in_specs=[pl.BlockSpec(memory_space=pltpu.SMEM),
                  pl.BlockSpec(tab.shape, lambda n: (0, 0, 0)),
                  pl.BlockSpec((WINDOW, ATTN_W), lambda n: (jnp.minimum(n, last), 0)),
                  pl.BlockSpec((WINDOW, 2 * KV_W), lambda n: (jnp.clip(n - 1, 0, last), 2)),
                  pl.BlockSpec((WINDOW, 2 * KV_W), lambda n: (jnp.minimum(n, last), 2)),
                  pl.BlockSpec((WINDOW, ATTN_W), lambda n: (jnp.minimum(n, last), 0))],
        out_specs=[pl.BlockSpec((WINDOW, ATTN_W), lambda n: (jnp.minimum(n, last), 0)),
                   pl.BlockSpec((WINDOW, 2 * KV_W), lambda n: (jnp.maximum(n - 1, 0), 0)),
                   pl.BlockSpec((8, LANES), lambda n: (0, 0))],
        out_shape=[jax.ShapeDtypeStruct((T, ATTN_W), BF16), jax.ShapeDtypeStruct((T, 2 * KV_W), BF16),
                   jax.ShapeDtypeStruct((8, LANES), F32)],
        scratch_shapes=[pltpu.VMEM((WINDOW, 2 * KV_W), F32)],
        compiler_params=_cp(1),
    )(sinks, tab, qkv, qkv, qkv, dattn)


def _pack(arrs):
    flat = jnp.concatenate([a.reshape(-1) for a in arrs])
    pad = -flat.shape[0] % (8 * LANES)
    return jnp.pad(flat, (0, pad)).reshape(1, -1, LANES)


def _unpack(packed, like):
    flat = packed.reshape(-1)
    out, off = [], 0
    for a in like:
        out.append(flat[off:off + a.size].reshape(a.shape))
        off += a.size
    return out


def kernel(x, norm_ffn1, w_ffn1_in, w_ffn1_out, norm_mix, w_in, sinks, w_dw, b_dw, conv_ln_g, conv_ln_b, w_out, norm_ffn2, w_ffn2_in, w_ffn2_out, final_norm, loss_target, m_norm_ffn1, m_w_ffn1_in, m_w_ffn1_out, m_norm_mix, m_w_in, m_sinks, m_w_dw, m_b_dw, m_conv_ln_g, m_conv_ln_b, m_w_out, m_norm_ffn2, m_w_ffn2_in, m_w_ffn2_out, m_final_norm, v_norm_ffn1, v_w_ffn1_in, v_w_ffn1_out, v_norm_mix, v_w_in, v_sinks, v_w_dw, v_b_dw, v_conv_ln_g, v_conv_ln_b, v_w_out, v_norm_ffn2, v_w_ffn2_in, v_w_ffn2_out, v_final_norm):
    L, D = norm_ffn1.shape
    T = x.shape[1]
    FB = w_ffn1_in.shape[2]
    CH = b_dw.shape[1]
    QKV = ATTN_W + 2 * KV_W
    xs = x.reshape(T, D)
    tgt = loss_target.reshape(T, D)
    cx, cy, cc = lax.axis_index("x"), lax.axis_index("y"), lax.axis_index("c")
    chip = 2 * cx + cy
    cidx = cc.reshape(1).astype(jnp.int32)
    big_w = (w_ffn1_in, w_ffn1_out, w_in, w_out, w_ffn2_in, w_ffn2_out)
    big_m = (m_w_ffn1_in, m_w_ffn1_out, m_w_in, m_w_out, m_w_ffn2_in, m_w_ffn2_out)
    big_v = (v_w_ffn1_in, v_w_ffn1_out, v_w_in, v_w_out, v_w_ffn2_in, v_w_ffn2_out)
    NW = len(big_w) + 1

    def shards(l, tok):
        return [(w_[l] + tok[0, 0]).astype(BF16) for w_ in big_w] + [w_dw[l] + tok[0, 0]]

    def own_slot(a, slots=4, idx=chip):
        return lax.dynamic_update_index_in_dim(lax.empty((slots,) + a.shape, a.dtype), a, idx, 0)

    def gather_start(srcs, tok):
        return _xchg_start("gather_start", srcs, [own_slot(s_) for s_ in srcs], _gather_plan, tok)

    def gather_arrived(started, after):
        _, lands, tok = _xchg_wait("gather_wait", started, NW, NW, _gather_plan, after)
        return _xchg_start("gshare_start", [], lands[:-1], _gshare_plan, tok, "sibling3"), lands[-1]

    row = lambda a, l: a[l].reshape(1, -1)
    tab = _attn_bias_table()
    NB = len(big_w)

    saved, W = [], []
    zero_tok = jnp.zeros((8, LANES), F32)
    started = gather_start(shards(0, zero_tok), zero_tok)
    cast = [None] + [shards(l, started[-1]) for l in range(1, L)]
    shared, gdw = gather_arrived(started, [xs] + [a_ for c_ in cast[1:] for a_ in c_])
    after = [shared[-1]]
    for l in range(L):
        _, (g1i, g1o, gi, go, g2i, g2o), tok = _xchg_wait("gshare_wait", shared, 0, NB, _gshare_plan, after, "sibling3")
        if l + 1 < L:
            started = gather_start(cast[l + 1], tok)
            tok = started[-1]
        w = dict(f1i=g1i, f1o=g1o.reshape(2 * FB, D), f2i=g2i, f2o=g2o.reshape(2 * FB, D),
                 wi=jnp.transpose(gi, (1, 0, 2)).reshape(D, -1), wo=go.reshape(-1, D),
                 wdw=jnp.transpose(gdw, (1, 0, 2)).reshape(CONV_W, CH))
        W.append(w)
        x0 = xs
        x1, gu1 = _ffn_fwd(x0, row(norm_ffn1, l) + tok[0, 0], w["f1i"], w["f1o"])
        qkv, u = _mixproj_fwd(x1, row(norm_mix, l), w["wi"])
        attn = _attn_fwd(row(sinks, l), tab, qkv)
        conv, ypre = _conv_fwd(u, w["wdw"], row(b_dw, l), row(conv_ln_g, l), row(conv_ln_b, l))
        x2 = _mixout_fwd(x1, attn, conv, w["wo"])
        g2_row = row(norm_ffn2, l)
        if l + 1 < L:
            shared, gdw = gather_arrived(started, [x2])
            g2_row = g2_row + shared[-1][0, 0]
        xs, gu2 = _ffn_fwd(x2, g2_row, w["f2i"], w["f2o"])
        saved.append((x0, gu1, x1, qkv, u, attn, conv, ypre, x2, gu2))
        after = [xs]

    loss_part, dx, d_final = _loss_head(xs, final_norm.reshape(1, D), tgt)
    loss = lax.psum(loss_part[0, 0], ("x", "y", "c"))

    bufs = [[lax.empty(w_.shape, F32) for _ in range(4)] for w_ in big_w]
    d_n1, d_nm, d_n2 = [None] * L, [None] * L, [None] * L
    d_sk, d_bdw, d_lg, d_lb, d_wdw = [None] * L, [None] * L, [None] * L, [None] * L, [None] * L

    def reduce_start(sib_started, after):
        gs, sibs, _ = _xchg_wait("sib_wait", sib_started, NB, NB, _sib_plan, after, "sibling")
        parts = [_sum_halves(cidx, g, s_) for g, s_ in zip(gs, sibs)]
        lands = [own_slot(lax.dynamic_index_in_dim(p, chip, 0, keepdims=False)) for p in parts]
        return _xchg_start("rs_start", parts, lands, _rs_plan, zero_tok)

    def share_start(rs_started, after):
        _, qs, tok = _xchg_wait("rs_wait", rs_started, NB, NB, _rs_plan, after)
        return _xchg_start("qshare_start", qs, [lax.empty(q.shape, q.dtype) for q in qs], _whole_plan, tok, "sibling")

    def finish(l, shared, after):
        q_own, q_sib, _ = _xchg_wait("qshare_wait", shared, NB, NB, _whole_plan, after, "sibling")
        for t in range(NB):
            bufs[t] = _adamw_layer(cidx, q_own[t], q_sib[t], big_w[t], big_m[t], big_v[t], bufs[t], l)

    sib_pending, rs_list = None, []
    tok = zero_tok
    for l in reversed(range(L)):
        w = W[l]
        x0, gu1, x1, qkv, u, attn, conv, ypre, x2, gu2 = saved[l]
        dx, d_n2[l], hb, dgu, a, dyb = _ffn_bwd(dx, x2, row(norm_ffn2, l), gu2, w["f2i"], w["f2o"], tok)
        g_f2i, g_f2o = _wgrad_ffn_in(hb, dgu), _wgrad_ffn_out(a, dyb)
        lg_row = row(conv_ln_g, l)
        if sib_pending is not None:
            rs_started = reduce_start(sib_pending[1], after=[g_f2o])
            rs_list.append((sib_pending[0], rs_started))
            lg_row = lg_row + rs_started[-1][0, 0]
        dyb, dattn, dconv = _mixout_bwd(dx, w["wo"])
        g_wo = _wgrad_cat([attn, conv], [dyb]).reshape(4, -1, D)
        du, dwdw, dvec = _conv_bwd(dconv, ypre, u, w["wdw"], lg_row, row(conv_ln_b, l))
        d_wdw[l], d_bdw[l], d_lg[l], d_lb[l] = dwdw[:CONV_W], dvec[0], dvec[1], dvec[2]
        dq, dkv, dsk = _attn_bwd(row(sinks, l), tab, qkv, dattn)
        d_sk[l] = dsk[:, 0]
        wi = w["wi"]
        dx, d_nm[l], hb = _mix_rms_bwd(dx, x1, row(norm_mix, l), [dq, dkv, du],
                                       [wi[:, :ATTN_W], wi[:, ATTN_W:QKV], wi[:, QKV:]])
        g_wi = jnp.transpose(_wgrad_cat([hb], [dq, dkv, du]).reshape(D, 4, -1), (1, 0, 2))
        dx, d_n1[l], hb, dgu, a, dyb = _ffn_bwd(dx, x0, row(norm_ffn1, l), gu1, w["f1i"], w["f1o"], tok)
        g_f1i, g_f1o = _wgrad_ffn_in(hb, dgu), _wgrad_ffn_out(a, dyb)
        gs = [g_f1i, g_f1o, g_wi, g_wo, g_f2i, g_f2o]
        sib_started = _xchg_start("sib_start", gs, [lax.empty((4, g.shape[1] // 2, g.shape[2]), F32) for g in gs],
                                  _sib_plan, zero_tok, "sibling")
        tok = sib_started[-1]
        sib_pending = (l, sib_started)
    grad_x = dx.reshape(x.shape)

    small_g = [jnp.concatenate(d, axis=0) for d in (d_n1, d_nm, d_n2)] + [d_final, jnp.stack(d_sk)] + \
              [jnp.stack(d) for d in (d_bdw, d_lg, d_lb, d_wdw)]
    packed = _pack(small_g)[0]
    small_started = _xchg_start("small_start", [packed], [own_slot(packed, 8, 4 * cx + 2 * cy + cc)], _slot_plan, tok, "all")
    rs_started = reduce_start(sib_pending[1], after=[small_started[-1]])
    rs_list.append((sib_pending[0], rs_started))

    after = [rs_started[-1]]
    shares = []
    for l, st in rs_list[:-1]:
        shares.append((l, share_start(st, after)))
        after = [shares[-1][1][-1]]
    for l, sh in shares:
        finish(l, sh, after)
        after = [bufs[0][0]]
    _, (slots,), _ = _xchg_wait("small_wait", small_started, 1, 1, _slot_plan, after, "all")
    small_sum = _unpack(_sum_slots(slots), small_g)
    g_wdw = lax.dynamic_slice_in_dim(small_sum[8], chip * w_dw.shape[2], w_dw.shape[2], axis=2)
    small_g = [small_sum[0], small_sum[1], small_sum[2], small_sum[3].reshape(D), small_sum[4],
               small_sum[5], small_sum[6], small_sum[7], g_wdw]
    small_w = (norm_ffn1, norm_mix, norm_ffn2, final_norm, sinks, b_dw, conv_ln_g, conv_ln_b, w_dw)
    small_m = (m_norm_ffn1, m_norm_mix, m_norm_ffn2, m_final_norm, m_sinks, m_b_dw, m_conv_ln_g, m_conv_ln_b, m_w_dw)
    small_v = (v_norm_ffn1, v_norm_mix, v_norm_ffn2, v_final_norm, v_sinks, v_b_dw, v_conv_ln_g, v_conv_ln_b, v_w_dw)
    upd = _adamw(_pack(small_g), _pack(small_w), _pack(small_m), _pack(small_v))
    small_upd = [_unpack(u_, small_w) for u_ in upd]
    last = share_start(rs_list[-1][1], [upd[0]])
    finish(rs_list[-1][0], last, [last[-1]])

    order = ("norm_ffn1", "w_ffn1_in", "w_ffn1_out", "norm_mix", "w_in", "sinks", "w_dw", "b_dw", "conv_ln_g",
             "conv_ln_b", "w_out", "norm_ffn2", "w_ffn2_in", "w_ffn2_out", "final_norm")
    small_names = ("norm_ffn1", "norm_mix", "norm_ffn2", "final_norm", "sinks", "b_dw", "conv_ln_g", "conv_ln_b", "w_dw")
    big_names = ("w_ffn1_in", "w_ffn1_out", "w_in", "w_out", "w_ffn2_in", "w_ffn2_out")
    grads, deltas, new_m, new_v = {}, {}, {}, {}
    for i, nme in enumerate(small_names):
        grads[nme], deltas[nme], new_m[nme], new_v[nme] = small_g[i], small_upd[0][i], small_upd[1][i], small_upd[2][i]
    for i, nme in enumerate(big_names):
        grads[nme], deltas[nme], new_m[nme], new_v[nme] = bufs[i]
    return (loss, grad_x, *[grads[n] for n in order], *[deltas[n] for n in order],
            *[new_m[n] for n in order], *[new_v[n] for n in order])
```

```python
import functools

import jax
import jax.numpy as jnp
from jax import lax
from jax.experimental import pallas as pl
from jax.experimental.pallas import tpu as pltpu

F32, BF16 = jnp.float32, jnp.bfloat16
EPS = 1e-6
NEG_INF = -1e30
HEAD_DIM = 64
N_HEADS = 8
N_KV = 2
GROUP = N_HEADS // N_KV
WINDOW = 128
ATTN_W = N_HEADS * HEAD_DIM
KV_W = N_KV * HEAD_DIM
CONV_W = 31
HALO = 32
CONV_ROWS = 32
SCALE = 1.0 / 8.0
ADAM_LR, ADAM_B1, ADAM_B2, ADAM_EPS, ADAM_WD, ADAM_STEP = 0.001, 0.9, 0.999, 1e-08, 0.01, 10
TM = 512
TM_FFN_BWD = 256
TK_WGRAD = 2048
LANES = 128
VMEM_LIMIT = 52 * 1024 * 1024
MESH = pl.DeviceIdType.MESH
ANY = pl.BlockSpec(memory_space=pl.ANY)
HBM = pl.BlockSpec(memory_space=pltpu.HBM)
SEM = pl.BlockSpec(memory_space=pltpu.SEMAPHORE)
VMEM = pl.BlockSpec(memory_space=pltpu.VMEM)
EFFECT = pltpu.SideEffectType.DATAFLOW_SIDE_EFFECTING
TOKEN = jax.ShapeDtypeStruct((8, LANES), F32)


def _cp(n):
    return pltpu.CompilerParams(dimension_semantics=("arbitrary",) * n, vmem_limit_bytes=VMEM_LIMIT)


def _dot(a, b):
    return jnp.dot(a, b, preferred_element_type=F32)


def _dot_nt(a, b):
    return lax.dot_general(a, b, (((1,), (1,)), ((), ())), preferred_element_type=F32)


def _dot_tn(a, b):
    return lax.dot_general(a, b, (((0,), (0,)), ((), ())), preferred_element_type=F32)


def _sigmoid(v):
    return 1.0 / (1.0 + jnp.exp(-v))


def _place():
    x, y, c = lax.axis_index("x"), lax.axis_index("y"), lax.axis_index("c")
    chips = [(1 - x, y), (x, 1 - y), (1 - x, 1 - y)]
    return x, y, c, chips


def _rcopy(src, dst, send_sems, recv_sems, k, dev):
    return pltpu.make_async_remote_copy(src_ref=src, dst_ref=dst, send_sem=send_sems.at[k],
                                        recv_sem=recv_sems.at[k], device_id=dev, device_id_type=MESH)


def _hbm(a):
    return pltpu.with_memory_space_constraint(a, pltpu.HBM)


PEERS = {"chips": 3, "sibling": 1, "sibling3": 3, "all": 7}


def _targets(mode):
    x, y, c, chips = _place()
    b = 2 * x + y
    if mode == "chips":
        return b, c, [((px, py, c), 2 * px + py) for px, py in chips]
    if mode == "sibling":
        return b, c, [((x, y, 1 - c), b)]
    if mode == "sibling3":
        return b, c, [((x, y, 1 - c), 2 * px + py) for px, py in chips]
    flip = lambda v, f: 1 - v if f else v
    devs = [(flip(x, k >> 2 & 1), flip(y, k >> 1 & 1), flip(c, k & 1)) for k in range(1, 8)]
    return 4 * x + 2 * y + c, c, [(d, 4 * d[0] + 2 * d[1] + d[2]) for d in devs]


def _xchg_start(name, srcs, lands, plan, dep, mode="chips"):
    ns, nl, npeer = len(srcs), len(lands), PEERS[mode]

    def body(*refs):
        land = refs[ns:ns + nl]
        src = refs[:ns] if ns else land
        send_sems, recv_sems, token = refs[ns + nl + 1], refs[ns + nl + 2], refs[-1]
        me, c, peers = _targets(mode)
        for t in range(nl):
            for j, (dev, tag) in enumerate(peers):
                s, d, _ = plan(src[t], land[t], t, me, c, tag)
                _rcopy(s, d, send_sems, recv_sems, npeer * t + j, dev).start()
        token[...] = jnp.zeros_like(token)

    arrs = list(srcs) + list(lands)
    return pl.pallas_call(
        body, name=name,
        out_shape=(pltpu.SemaphoreType.DMA((npeer * nl,)), pltpu.SemaphoreType.DMA((npeer * nl,)),
                   *[pltpu.HBM(a.shape, a.dtype) for a in arrs], TOKEN),
        in_specs=[HBM] * (ns + nl) + [ANY], out_specs=(SEM, SEM, *[HBM] * (ns + nl), VMEM),
        input_output_aliases={i: 2 + i for i in range(ns + nl)},
        compiler_params=pltpu.CompilerParams(has_side_effects=EFFECT),
    )(*[_hbm(a) for a in arrs], dep)


def _xchg_wait(name, started, ns, nl, plan, after, mode="chips"):
    send_sems, recv_sems, thru = started[0], started[1], started[2:2 + ns + nl]
    npeer = PEERS[mode]

    def body(*refs):
        land = refs[ns:ns + nl]
        src = refs[:ns] if ns else land
        send_sems, recv_sems, token = refs[ns + nl], refs[ns + nl + 1], refs[-1]
        me, c, peers = _targets(mode)
        for t in range(nl):
            for j, (dev, tag) in enumerate(peers):
                s, _, a = plan(src[t], land[t], t, me, c, tag)
                cp = _rcopy(s, a, send_sems, recv_sems, npeer * t + j, dev)
                cp.wait_send()
                cp.wait_recv()
        token[...] = jnp.zeros_like(token)

    out = pl.pallas_call(
        body, name=name,
        out_shape=(*[pltpu.HBM(a.shape, a.dtype) for a in thru], TOKEN),
        in_specs=[HBM] * (ns + nl) + [SEM, SEM] + [ANY] * len(after), out_specs=(*[HBM] * (ns + nl), VMEM),
        input_output_aliases={i: i for i in range(ns + nl)},
        compiler_params=pltpu.CompilerParams(has_side_effects=EFFECT),
    )(*thru, send_sems, recv_sems, *after)
    return out[:ns], out[ns:ns + nl], out[-1]


def _half(ref_rows, which):
    h = ref_rows // 2
    return pl.ds(which * h, h)


def _gather_plan(src, land, t, b, c, pb):
    if len(src.shape) == 2 and src.shape[0] % 2 == 0:
        hs = _half(src.shape[0], c)
        return src.at[hs], land.at[b, hs], land.at[pb, hs]
    return src, land.at[b], land.at[pb]


def _gshare_plan(src, land, t, b, c, pb):
    return land.at[pb, _half(land.shape[1], c)], land.at[pb, _half(land.shape[1], c)], land.at[pb, _half(land.shape[1], 1 - c)]


def _rs_plan(src, land, t, b, c, pb):
    return src.at[pb], land.at[b], land.at[pb]


def _sib_plan(src, land, t, b, c, pb):
    return src.at[:, _half(src.shape[1], 1 - c), :], land, land


def _rows_block(h, cap=512):
    for rb in range(min(h, cap) // 16 * 16, 0, -16):
        if h % rb == 0:
            return rb
    return h


def _sum_halves(cidx, g, s):
    _, R, C = g.shape
    rb = _rows_block(R // 2)
    nr = R // 2 // rb

    def body(c_ref, g_ref, s_ref, o_ref):
        o_ref[...] = (g_ref[...] + s_ref[...]).astype(BF16)

    blk = (None, rb, C)
    return pl.pallas_call(
        body, name="sum_halves", out_shape=jax.ShapeDtypeStruct(s.shape, BF16),
        grid_spec=pltpu.PrefetchScalarGridSpec(
            num_scalar_prefetch=1, grid=(4, nr),
            in_specs=[pl.BlockSpec(blk, lambda p, i, c: (p, c[0] * nr + i, 0)),
                      pl.BlockSpec(blk, lambda p, i, c: (p, i, 0))],
            out_specs=pl.BlockSpec(blk, lambda p, i, c: (p, i, 0))),
        compiler_params=_cp(2),
    )(cidx, g, s)


def _whole_plan(src, land, t, me, c, tag):
    return src, land, land


def _slot_plan(src, land, t, me, c, tag):
    return src, land.at[me], land.at[tag]


def _adam_update(gg, w, m, v):
    m2 = ADAM_B1 * m + (1.0 - ADAM_B1) * gg
    v2 = ADAM_B2 * v + (1.0 - ADAM_B2) * (gg * gg)
    mh = m2 / (1.0 - ADAM_B1 ** ADAM_STEP)
    vh = v2 / (1.0 - ADAM_B2 ** ADAM_STEP)
    return -ADAM_LR * (mh / (jnp.sqrt(vh) + ADAM_EPS) + ADAM_WD * w), m2, v2


def _adamw_layer(cidx, q_own, q_sib, w, m, v, bufs, l):
    L, R, C = w.shape
    h = R // 2
    rb = _rows_block(h, 256)
    nr = h // rb

    def body(c_ref, qo_ref, qs_ref, w_ref, m_ref, v_ref, *rest):
        g_ref, d_ref, mo_ref, vo_ref = rest[-4:]
        own = pl.program_id(0) == c_ref[0]
        gg = jnp.zeros((rb, C), F32)
        for s in range(4):
            gg = gg + jnp.where(own, qo_ref[s], qs_ref[s]).astype(F32)
        g_ref[...] = gg
        d_ref[...], mo_ref[...], vo_ref[...] = _adam_update(gg, w_ref[...], m_ref[...], v_ref[...])

    qspec = pl.BlockSpec((4, rb, C), lambda hh, i, c: (0, i, 0))
    wspec = pl.BlockSpec((None, rb, C), lambda hh, i, c: (l, hh * nr + i, 0))
    return pl.pallas_call(
        body, name="adamw_layer", out_shape=[jax.ShapeDtypeStruct(w.shape, F32)] * 4,
        grid_spec=pltpu.PrefetchScalarGridSpec(
            num_scalar_prefetch=1, grid=(2, nr),
            in_specs=[qspec, qspec, wspec, wspec, wspec] + [ANY] * 4, out_specs=[wspec] * 4),
        input_output_aliases={6 + k: k for k in range(4)},
        compiler_params=_cp(2),
    )(cidx, q_own, q_sib, w, m, v, *bufs)


def _adamw(g, w, m, v):
    L, R, C = g.shape
    rb = _rows_block(R)

    def body(g_ref, w_ref, m_ref, v_ref, d_ref, mo_ref, vo_ref):
        d_ref[...], mo_ref[...], vo_ref[...] = _adam_update(g_ref[...], w_ref[...], m_ref[...], v_ref[...])

    spec = pl.BlockSpec((None, rb, C), lambda l, i: (l, i, 0))
    return pl.pallas_call(
        body, name="adamw", grid=(L, R // rb), in_specs=[spec] * 4, out_specs=[spec] * 3,
        out_shape=[jax.ShapeDtypeStruct(g.shape, F32)] * 3, compiler_params=_cp(2),
    )(g, w, m, v)


def _sum_slots(buf):
    def body(b_ref, o_ref):
        acc = b_ref[0]
        for k in range(1, 8):
            acc = acc + b_ref[k]
        o_ref[...] = acc

    return pl.pallas_call(body, name="sum_slots", in_specs=[VMEM], out_specs=VMEM,
                          out_shape=jax.ShapeDtypeStruct(buf.shape[1:], F32))(buf)


def _rms(xf, g):
    r = lax.rsqrt(jnp.mean(xf * xf, axis=-1, keepdims=True) + EPS)
    return xf * r, r


def _lane_chunks(n):
    lo = (n // LANES + 1) // 2 * LANES
    return ((0, lo), (lo, n - lo))


def _load_ffn_weights(win_hbm, wout_hbm, win_v, wout_v, sems):
    fb = win_v.shape[2]
    loads = [pltpu.make_async_copy(win_hbm.at[k], win_v.at[k], sems.at[k]) for k in range(4)]
    loads += [pltpu.make_async_copy(wout_hbm.at[pl.ds(k * fb, fb)], wout_v.at[pl.ds(k * fb, fb)], sems.at[4 + k])
              for k in range(2)]
    for cp in loads:
        cp.start()
    for cp in loads:
        cp.wait()


def _fast_sigmoid(v):
    return pl.reciprocal(1.0 + jnp.exp(-v), approx=True)


def _ffn_fwd(x, g, win, wout):
    T, D = x.shape
    FB = win.shape[2]
    tm = min(TM, T)

    def body(x_ref, g_ref, win_hbm, wout_hbm, xo_ref, gu_ref, win_v, wout_v, sems):
        @pl.when(pl.program_id(0) == 0)
        def _():
            _load_ffn_weights(win_hbm, wout_hbm, win_v, wout_v, sems)

        xf = x_ref[...]
        xh, _ = _rms(xf, None)
        h = (xh * g_ref[...]).astype(BF16)
        acc = jnp.zeros((tm, D), F32)
        for blk in range(2):
            for lo, sz in _lane_chunks(FB):
                cols = pl.ds(blk * FB + lo, sz)
                gate = _dot(h, win_v[blk, :, pl.ds(lo, sz)])
                up = _dot(h, win_v[2 + blk, :, pl.ds(lo, sz)])
                gu_ref[0, :, cols] = gate.astype(BF16)
                gu_ref[1, :, cols] = up.astype(BF16)
                a = (gate * _fast_sigmoid(gate) * up).astype(BF16)
                acc = acc + _dot(a, wout_v[cols, :])
        xo_ref[...] = xf + 0.5 * acc

    row = pl.BlockSpec((tm, D), lambda i: (i, 0))
    return pl.pallas_call(
        body, name="ffn_fwd", grid=(T // tm,),
        in_specs=[row, pl.BlockSpec((1, D), lambda i: (0, 0)), ANY, ANY],
        out_specs=[row, pl.BlockSpec((2, tm, 2 * FB), lambda i: (0, i, 0))],
        out_shape=[jax.ShapeDtypeStruct((T, D), F32), jax.ShapeDtypeStruct((2, T, 2 * FB), BF16)],
        scratch_shapes=[pltpu.VMEM(win.shape, BF16), pltpu.VMEM(wout.shape, BF16), pltpu.SemaphoreType.DMA((6,))],
        compiler_params=_cp(1),
    )(x, g, win, wout)


def _mixproj_fwd(x, g, w):
    T, D = x.shape
    W = w.shape[1]
    QKV = ATTN_W + 2 * KV_W
    tm = min(TM, T)

    def body(x_ref, g_ref, w_ref, qkv_ref, u_ref):
        xh, _ = _rms(x_ref[...], None)
        h = (xh * g_ref[...]).astype(BF16)
        qkv_ref[...] = _dot(h, w_ref[:, :QKV]).astype(BF16)
        u_ref[...] = _dot(h, w_ref[:, QKV:])

    return pl.pallas_call(
        body, name="mixproj_fwd", grid=(T // tm,),
        in_specs=[pl.BlockSpec((tm, D), lambda i: (i, 0)), pl.BlockSpec((1, D), lambda i: (0, 0)),
                  pl.BlockSpec((D, W), lambda i: (0, 0))],
        out_specs=[pl.BlockSpec((tm, QKV), lambda i: (i, 0)), pl.BlockSpec((tm, W - QKV), lambda i: (i, 0))],
        out_shape=[jax.ShapeDtypeStruct((T, QKV), BF16), jax.ShapeDtypeStruct((T, W - QKV), F32)],
        compiler_params=_cp(1),
    )(x, g, w)


def _attn_bias_table():
    rows, cols = GROUP * WINDOW, 2 * WINDOW
    row = lax.broadcasted_iota(jnp.int32, (N_KV, rows, cols), 1)
    col = lax.broadcasted_iota(jnp.int32, (N_KV, rows, cols), 2)
    head = GROUP * lax.broadcasted_iota(jnp.int32, (N_KV, rows, cols), 0) + (row >> 7)
    dist = (row & (WINDOW - 1)) + WINDOW - col
    slope = jnp.exp2(-(head + 1).astype(F32))
    return jnp.where((dist >= 0) & (dist < WINDOW), -slope * dist.astype(F32), NEG_INF)


def _first_block_mask(n):
    col = lax.broadcasted_iota(jnp.int32, (GROUP * WINDOW, 2 * WINDOW), 1)
    return (n > 0) | (col >= WINDOW)


def _sink_col(sink_ref, g):
    hi = lax.broadcasted_iota(jnp.int32, (GROUP * WINDOW, 1), 0) >> 7
    col = jnp.zeros((GROUP * WINDOW, 1), F32)
    for i in range(GROUP):
        col = jnp.where(hi == i, sink_ref[0, GROUP * g + i], col)
    return col


def _stack_heads(ref, g):
    return jnp.concatenate([ref[:, (GROUP * g + i) * HEAD_DIM:(GROUP * g + i + 1) * HEAD_DIM]
                            for i in range(GROUP)], axis=0)


def _band(kvp_ref, kvc_ref, off):
    return jnp.concatenate([kvp_ref[:, off:off + HEAD_DIM], kvc_ref[:, off:off + HEAD_DIM]], axis=0)


def _attn_probs(qs, k, bias, seen, sink):
    s = jnp.where(seen, _dot_nt(qs, k) * SCALE + bias, NEG_INF)
    m = jnp.maximum(jnp.max(s, axis=-1, keepdims=True), sink)
    p = jnp.exp(s - m)
    es = jnp.exp(sink - m)
    den = jnp.sum(p, axis=-1, keepdims=True) + es
    return p / den, es / den


def _attn_fwd(sinks, tab, qkv):
    T = qkv.shape[0]
    nb = T // WINDOW

    def body(sink_ref, tab_ref, q_ref, kvp_ref, kvc_ref, o_ref):
        seen = _first_block_mask(pl.program_id(0))
        for g in range(N_KV):
            qs = _stack_heads(q_ref, g)
            k = _band(kvp_ref, kvc_ref, g * HEAD_DIM)
            v = _band(kvp_ref, kvc_ref, KV_W + g * HEAD_DIM)
            p, _ = _attn_probs(qs, k, tab_ref[g], seen, _sink_col(sink_ref, g))
            o = _dot(p.astype(BF16), v)
            for i in range(GROUP):
                h = GROUP * g + i
                o_ref[:, h * HEAD_DIM:(h + 1) * HEAD_DIM] = o[i * WINDOW:(i + 1) * WINDOW].astype(BF16)

    return pl.pallas_call(
        body, name="attn_fwd", grid=(nb,),
        in_specs=[pl.BlockSpec(memory_space=pltpu.SMEM),
                  pl.BlockSpec(tab.shape, lambda n: (0, 0, 0)),
                  pl.BlockSpec((WINDOW, ATTN_W), lambda n: (n, 0)),
                  pl.BlockSpec((WINDOW, 2 * KV_W), lambda n: (jnp.maximum(n - 1, 0), 2)),
                  pl.BlockSpec((WINDOW, 2 * KV_W), lambda n: (n, 2))],
        out_specs=pl.BlockSpec((WINDOW, ATTN_W), lambda n: (n, 0)),
        out_shape=jax.ShapeDtypeStruct((T, ATTN_W), BF16),
        compiler_params=_cp(1),
    )(sinks, tab, qkv, qkv, qkv)


def _shift_copies(src_ref, dst_ref, n):
    for b in range(1, 8):
        dst_ref[b - 1] = src_ref[b:b + n, :]


def _tap(src_ref, sh_ref, s, c0):
    a, b = divmod(s, 8)
    start = pl.multiple_of(c0 + 8 * a, 8)
    if b == 0:
        return src_ref[pl.ds(start, CONV_ROWS), :]
    return sh_ref[b - 1, pl.ds(start, CONV_ROWS), :]


def _glu_rows(u, ch):
    return u[:, :ch] * _sigmoid(u[:, ch:])


def _fill_z(zs_ref, zsh_ref, uc_ref, up_ref, i, ch, n):
    zs_ref[0:HALO] = jnp.where(i > 0, _glu_rows(up_ref[...], ch), 0.0)
    zs_ref[HALO:] = _glu_rows(uc_ref[...], ch)
    _shift_copies(zs_ref, zsh_ref, n - 8)


def _conv_fwd(u, w, b, lg, lb):
    T = u.shape[0]
    CH = u.shape[1] // 2
    tm = min(TM, T)
    n = tm + HALO
    hb = tm // HALO

    def body(uc_ref, up_ref, w_ref, b_ref, lg_ref, lb_ref, conv_ref, ypre_ref, zs_ref, zsh_ref):
        i = pl.program_id(0)
        _fill_z(zs_ref, zsh_ref, uc_ref, up_ref, i, CH, n)
        bias = b_ref[...]

        def chunk(ci, carry):
            c0 = pl.multiple_of(ci * CONV_ROWS, CONV_ROWS)
            acc = jnp.broadcast_to(bias, (CONV_ROWS, CH))
            for k in range(CONV_W):
                acc = acc + w_ref[k:k + 1, :] * _tap(zs_ref, zsh_ref, HALO - (CONV_W - 1) + k, c0)
            ypre_ref[pl.ds(c0, CONV_ROWS), :] = acc
            return carry

        lax.fori_loop(0, tm // CONV_ROWS, chunk, 0)
        y = ypre_ref[...]
        mu = jnp.mean(y, axis=-1, keepdims=True)
        d = y - mu
        var = jnp.mean(d * d, axis=-1, keepdims=True)
        o = d * lax.rsqrt(var + EPS) * lg_ref[...] + lb_ref[...]
        conv_ref[...] = (o * _sigmoid(o)).astype(BF16)

    vec = pl.BlockSpec((1, CH), lambda i: (0, 0))
    return pl.pallas_call(
        body, name="conv_fwd", grid=(T // tm,),
        in_specs=[pl.BlockSpec((tm, 2 * CH), lambda i: (i, 0)),
                  pl.BlockSpec((HALO, 2 * CH), lambda i: (jnp.maximum(i * hb - 1, 0), 0)),
                  pl.BlockSpec((CONV_W, CH), lambda i: (0, 0)), vec, vec, vec],
        out_specs=[pl.BlockSpec((tm, CH), lambda i: (i, 0)), pl.BlockSpec((tm, CH), lambda i: (i, 0))],
        out_shape=[jax.ShapeDtypeStruct((T, CH), BF16), jax.ShapeDtypeStruct((T, CH), F32)],
        scratch_shapes=[pltpu.VMEM((n, CH), F32), pltpu.VMEM((7, n - 8, CH), F32)],
        compiler_params=_cp(1),
    )(u, u, w, b, lg, lb)


def _mixout_fwd(x, attn, conv, wo):
    T, D = x.shape
    tm = min(TM, T)
    A = attn.shape[1]

    def body(x_ref, a_ref, c_ref, w_ref, xo_ref):
        xo_ref[...] = x_ref[...] + _dot(a_ref[...], w_ref[:A, :]) + _dot(c_ref[...], w_ref[A:, :])

    return pl.pallas_call(
        body, name="mixout_fwd", grid=(T // tm,),
        in_specs=[pl.BlockSpec((tm, D), lambda i: (i, 0)), pl.BlockSpec((tm, A), lambda i: (i, 0)),
                  pl.BlockSpec((tm, conv.shape[1]), lambda i: (i, 0)), pl.BlockSpec(wo.shape, lambda i: (0, 0))],
        out_specs=pl.BlockSpec((tm, D), lambda i: (i, 0)),
        out_shape=jax.ShapeDtypeStruct((T, D), F32),
        compiler_params=_cp(1),
    )(x, attn, conv, wo)


def _rms_bwd_rows(dh, xf, g):
    xh, r = _rms(xf, None)
    dxn = dh * g
    dx = r * (dxn - xh * jnp.mean(dxn * xh, axis=-1, keepdims=True))
    return dx, jnp.sum(dh * xh, axis=0, keepdims=True), xh * g


def _loss_head(x, g, tgt):
    T, D = x.shape
    tm = min(TM, T)

    def body(x_ref, g_ref, t_ref, loss_ref, dx_ref, dg_ref):
        @pl.when(pl.program_id(0) == 0)
        def _():
            loss_ref[...] = jnp.zeros_like(loss_ref)
            dg_ref[...] = jnp.zeros_like(dg_ref)

        xf = x_ref[...]
        g = g_ref[...]
        xh, _ = _rms(xf, None)
        e = xh * g - t_ref[...]
        loss_ref[...] += 0.5 * jnp.sum(jnp.mean(e * e, axis=-1, keepdims=True), axis=0, keepdims=True)
        dx, dg, _ = _rms_bwd_rows(e * (1.0 / D), xf, g)
        dx_ref[...] = dx
        dg_ref[...] += dg

    return pl.pallas_call(
        body, name="loss_head", grid=(T // tm,),
        in_specs=[pl.BlockSpec((tm, D), lambda i: (i, 0)), pl.BlockSpec((1, D), lambda i: (0, 0)),
                  pl.BlockSpec((tm, D), lambda i: (i, 0))],
        out_specs=[pl.BlockSpec((1, 1), lambda i: (0, 0)), pl.BlockSpec((tm, D), lambda i: (i, 0)),
                   pl.BlockSpec((1, D), lambda i: (0, 0))],
        out_shape=[jax.ShapeDtypeStruct((1, 1), F32), jax.ShapeDtypeStruct((T, D), F32),
                   jax.ShapeDtypeStruct((1, D), F32)],
        compiler_params=_cp(1),
    )(x, g, tgt)


def _ffn_bwd(dxo, x, g, gu, win, wout, dep):
    T, D = x.shape
    FB = win.shape[2]
    tm = min(TM_FFN_BWD, T)

    def body(dxo_ref, x_ref, g_ref, gu_ref, win_hbm, wout_hbm, dep_ref,
             dxi_ref, dg_ref, hb_ref, dgu_ref, a_ref, dyb_ref, win_v, wout_v, sems):
        @pl.when(pl.program_id(0) == 0)
        def _():
            _load_ffn_weights(win_hbm, wout_hbm, win_v, wout_v, sems)
            dg_ref[...] = jnp.zeros_like(dg_ref)

        dyb = (0.5 * dxo_ref[...]).astype(BF16)
        dyb_ref[...] = dyb
        dh = jnp.zeros((tm, D), F32)
        for blk in range(2):
            for lo, sz in _lane_chunks(FB):
                cols = pl.ds(blk * FB + lo, sz)
                da = _dot_nt(dyb, wout_v[cols, :])
                gate = gu_ref[0, :, cols].astype(F32)
                up = gu_ref[1, :, cols].astype(F32)
                sg = _fast_sigmoid(gate)
                s = gate * sg
                a_ref[:, cols] = (s * up).astype(BF16)
                dgate = (da * up * (sg * (1.0 + gate * (1.0 - sg)))).astype(BF16)
                dup = (da * s).astype(BF16)
                dgu_ref[0, :, cols] = dgate
                dgu_ref[1, :, cols] = dup
                dh = dh + _dot_nt(dgate, win_v[blk, :, pl.ds(lo, sz)]) + _dot_nt(dup, win_v[2 + blk, :, pl.ds(lo, sz)])
        dx, dg, h = _rms_bwd_rows(dh, x_ref[...], g_ref[...])
        dxi_ref[...] = dxo_ref[...] + dx
        dg_ref[...] += dg
        hb_ref[...] = h.astype(BF16)

    row = pl.BlockSpec((tm, D), lambda i: (i, 0))
    act = pl.BlockSpec((2, tm, 2 * FB), lambda i: (0, i, 0))
    return pl.pallas_call(
        body, name="ffn_bwd", grid=(T // tm,),
        in_specs=[row, row, pl.BlockSpec((1, D), lambda i: (0, 0)), act, ANY, ANY, ANY],
        out_specs=[row, pl.BlockSpec((1, D), lambda i: (0, 0)), row, act,
                   pl.BlockSpec((tm, 2 * FB), lambda i: (i, 0)), row],
        out_shape=[jax.ShapeDtypeStruct((T, D), F32), jax.ShapeDtypeStruct((1, D), F32),
                   jax.ShapeDtypeStruct((T, D), BF16), jax.ShapeDtypeStruct((2, T, 2 * FB), BF16),
                   jax.ShapeDtypeStruct((T, 2 * FB), BF16), jax.ShapeDtypeStruct((T, D), BF16)],
        scratch_shapes=[pltpu.VMEM(win.shape, BF16), pltpu.VMEM(wout.shape, BF16), pltpu.SemaphoreType.DMA((6,))],
        compiler_params=_cp(1),
    )(dxo, x, g, gu, win, wout, dep)


def _rms_matmul_bwd(name, dxo, x, g, dzs, ws, dz_specs, w_specs, nk):
    T, D = x.shape
    tm = min(TM, T)
    npair = len(dzs)

    def body(*refs):
        dxo_ref, x_ref, g_ref = refs[:3]
        dz_refs, w_refs = refs[3:3 + npair], refs[3 + npair:3 + 2 * npair]
        dxi_ref, dg_ref, hb_ref, acc_ref = refs[3 + 2 * npair:]
        i, k = pl.program_id(0), pl.program_id(1)

        @pl.when(k == 0)
        def _():
            acc_ref[...] = jnp.zeros_like(acc_ref)

        @pl.when((i == 0) & (k == 0))
        def _():
            dg_ref[...] = jnp.zeros_like(dg_ref)

        for p in range(npair):
            acc_ref[...] += _dot_nt(dz_refs[p][...], w_refs[p][...])

        @pl.when(k == nk - 1)
        def _():
            dx, dg, h = _rms_bwd_rows(acc_ref[...], x_ref[...], g_ref[...])
            dxi_ref[...] = dxo_ref[...] + dx
            dg_ref[...] += dg
            hb_ref[...] = h.astype(BF16)

    row = pl.BlockSpec((tm, D), lambda i, k: (i, 0))
    return pl.pallas_call(
        body, name=name, grid=(T // tm, nk),
        in_specs=[row, row, pl.BlockSpec((1, D), lambda i, k: (0, 0))] + list(dz_specs) + list(w_specs),
        out_specs=[row, pl.BlockSpec((1, D), lambda i, k: (0, 0)), row],
        out_shape=[jax.ShapeDtypeStruct((T, D), F32), jax.ShapeDtypeStruct((1, D), F32),
                   jax.ShapeDtypeStruct((T, D), BF16)],
        scratch_shapes=[pltpu.VMEM((tm, D), F32)],
        compiler_params=_cp(2),
    )(dxo, x, g, *dzs, *ws)


def _mix_rms_bwd(dxo, x, g, dzs, ws):
    tm = min(TM, x.shape[0])
    return _rms_matmul_bwd(
        "mix_rms_bwd", dxo, x, g, dzs, ws,
        [pl.BlockSpec((tm, dz.shape[1]), lambda i, k: (i, 0)) for dz in dzs],
        [pl.BlockSpec(w.shape, lambda i, k: (0, 0)) for w in ws], 1)


def _wgrad(name, a, b, a_spec, b_spec, out_shape, out_spec, nblk):
    T = a.shape[0]
    tk = min(TK_WGRAD, T)

    def body(a_ref, b_ref, o_ref):
        @pl.when(pl.program_id(1) == 0)
        def _():
            o_ref[...] = jnp.zeros_like(o_ref)

        o_ref[...] += _dot_tn(a_ref[...], b_ref[...]).reshape(o_ref.shape)

    return pl.pallas_call(
        body, name=name, grid=(nblk, T // tk), in_specs=[a_spec, b_spec], out_specs=out_spec,
        out_shape=jax.ShapeDtypeStruct(out_shape, F32), compiler_params=_cp(2),
    )(a, b)


def _wgrad_ffn_in(hb, dgu):
    T, D = hb.shape
    FB = dgu.shape[2] // 2
    tk = min(TK_WGRAD, T)
    return _wgrad("wgrad_ffn_in", hb, dgu,
                  pl.BlockSpec((tk, D), lambda b, k: (k, 0)),
                  pl.BlockSpec((None, tk, FB), lambda b, k: (b // 2, k, b % 2)),
                  (4, D, FB), pl.BlockSpec((None, D, FB), lambda b, k: (b, 0, 0)), 4)


def _wgrad_ffn_out(a, dyb):
    T, D = dyb.shape
    FB = a.shape[1] // 2
    tk = min(TK_WGRAD, T)
    return _wgrad("wgrad_ffn_out", a, dyb,
                  pl.BlockSpec((tk, FB), lambda b, k: (k, b)),
                  pl.BlockSpec((tk, D), lambda b, k: (k, 0)),
                  (4, FB // 2, D), pl.BlockSpec((2, FB // 2, D), lambda b, k: (b, 0, 0)), 2)


def _wgrad_cat(a_list, b_list):
    T = a_list[0].shape[0]
    tk = min(TK_WGRAD, T)
    na = len(a_list)
    M, N = sum(a.shape[1] for a in a_list), sum(b.shape[1] for b in b_list)

    def body(*refs):
        a_refs, b_refs, o_ref = refs[:na], refs[na:-1], refs[-1]

        @pl.when(pl.program_id(0) == 0)
        def _():
            o_ref[...] = jnp.zeros_like(o_ref)

        r0 = 0
        for a_ref in a_refs:
            c0 = 0
            for b_ref in b_refs:
                m, n = a_ref.shape[1], b_ref.shape[1]
                o_ref[r0:r0 + m, c0:c0 + n] += _dot_tn(a_ref[...], b_ref[...])
                c0 += n
            r0 += a_ref.shape[1]

    return pl.pallas_call(
        body, name="wgrad_cat", grid=(T // tk,),
        in_specs=[pl.BlockSpec((tk, v.shape[1]), lambda k: (k, 0)) for v in list(a_list) + list(b_list)],
        out_specs=pl.BlockSpec((M, N), lambda k: (0, 0)),
        out_shape=jax.ShapeDtypeStruct((M, N), F32), compiler_params=_cp(1),
    )(*a_list, *b_list)


def _mixout_bwd(dxo, wo):
    T, D = dxo.shape
    tm = min(TM, T)
    A = ATTN_W
    C = wo.shape[0] - A

    def body(dxo_ref, w_ref, dyb_ref, da_ref, dc_ref):
        dyb = dxo_ref[...].astype(BF16)
        dyb_ref[...] = dyb
        da_ref[...] = _dot_nt(dyb, w_ref[:A, :]).astype(BF16)
        dc_ref[...] = _dot_nt(dyb, w_ref[A:, :])

    return pl.pallas_call(
        body, name="mixout_bwd", grid=(T // tm,),
        in_specs=[pl.BlockSpec((tm, D), lambda i: (i, 0)), pl.BlockSpec(wo.shape, lambda i: (0, 0))],
        out_specs=[pl.BlockSpec((tm, D), lambda i: (i, 0)), pl.BlockSpec((tm, A), lambda i: (i, 0)),
                   pl.BlockSpec((tm, C), lambda i: (i, 0))],
        out_shape=[jax.ShapeDtypeStruct((T, D), BF16), jax.ShapeDtypeStruct((T, A), BF16),
                   jax.ShapeDtypeStruct((T, C), F32)],
        compiler_params=_cp(1),
    )(dxo, wo)


def _conv_bwd(dconv, ypre, u, w, lg, lb):
    T, CH = dconv.shape
    tm = min(TM, T)
    n = tm + HALO
    hb = tm // HALO
    nt = T // tm
    nchunk = tm // CONV_ROWS

    def body(dc_ref, dcn_ref, yp_ref, ypn_ref, uc_ref, up_ref, w_ref, lg_ref, lb_ref,
             du_ref, dw_ref, dvec_ref, zs_ref, zsh_ref, dy_ref, dysh_ref, dz_ref, dwacc_ref):
        i = pl.program_id(0)

        @pl.when(i == 0)
        def _():
            dwacc_ref[...] = jnp.zeros_like(dwacc_ref)
            dvec_ref[...] = jnp.zeros_like(dvec_ref)

        g, bb = lg_ref[...], lb_ref[...]

        def ln_bwd(dc, yp):
            mu = jnp.mean(yp, axis=-1, keepdims=True)
            d = yp - mu
            rs = lax.rsqrt(jnp.mean(d * d, axis=-1, keepdims=True) + EPS)
            yn = d * rs
            o = yn * g + bb
            sg = _sigmoid(o)
            do = dc * (sg * (1.0 + o * (1.0 - sg)))
            dyn = do * g
            dyp = rs * (dyn - jnp.mean(dyn, axis=-1, keepdims=True)
                        - yn * jnp.mean(dyn * yn, axis=-1, keepdims=True))
            return dyp, do, yn

        dyp, do, yn = ln_bwd(dc_ref[...], yp_ref[...])
        dvec_ref[0:1, :] += jnp.sum(dyp, axis=0, keepdims=True)
        dvec_ref[1:2, :] += jnp.sum(do * yn, axis=0, keepdims=True)
        dvec_ref[2:3, :] += jnp.sum(do, axis=0, keepdims=True)
        dy_ref[0:tm] = dyp
        dyh, _, _ = ln_bwd(dcn_ref[...], ypn_ref[...])
        dy_ref[tm:] = jnp.where(i < nt - 1, dyh, 0.0)
        _shift_copies(dy_ref, dysh_ref, n - 8)
        _fill_z(zs_ref, zsh_ref, uc_ref, up_ref, i, CH, n)

        def chunk(ci, carry):
            c0 = pl.multiple_of(ci * CONV_ROWS, CONV_ROWS)
            acc = jnp.zeros((CONV_ROWS, CH), F32)
            for k in range(CONV_W):
                acc = acc + w_ref[k:k + 1, :] * _tap(dy_ref, dysh_ref, CONV_W - 1 - k, c0)
            dz_ref[pl.ds(c0, CONV_ROWS), :] = acc
            dyc = dy_ref[pl.ds(c0, CONV_ROWS), :]
            for k in range(CONV_W):
                prod = dyc * _tap(zs_ref, zsh_ref, HALO - (CONV_W - 1) + k, c0)
                dwacc_ref[k] += jnp.sum(prod.reshape(CONV_ROWS // 8, 8, CH), axis=0)
            return carry

        lax.fori_loop(0, nchunk, chunk, 0)

        @pl.when(i == nt - 1)
        def _():
            dw_ref[...] = jnp.sum(dwacc_ref[...], axis=1)

        uc = uc_ref[...]
        a = uc[:, :CH]
        sg = _sigmoid(uc[:, CH:])
        dz = dz_ref[...]
        du_ref[:, :CH] = (dz * sg).astype(BF16)
        du_ref[:, CH:] = (dz * a * sg * (1.0 - sg)).astype(BF16)

    cur = lambda c: pl.BlockSpec((tm, c), lambda i: (i, 0))
    nxt = lambda c: pl.BlockSpec((HALO, c), lambda i: (jnp.minimum((i + 1) * hb, T // HALO - 1), 0))
    vec = pl.BlockSpec((1, CH), lambda i: (0, 0))
    return pl.pallas_call(
        body, name="conv_bwd", grid=(nt,),
        in_specs=[cur(CH), nxt(CH), cur(CH), nxt(CH), cur(2 * CH),
                  pl.BlockSpec((HALO, 2 * CH), lambda i: (jnp.maximum(i * hb - 1, 0), 0)),
                  pl.BlockSpec((CONV_W, CH), lambda i: (0, 0)), vec, vec],
        out_specs=[pl.BlockSpec((tm, 2 * CH), lambda i: (i, 0)), pl.BlockSpec((32, CH), lambda i: (0, 0)),
                   pl.BlockSpec((8, CH), lambda i: (0, 0))],
        out_shape=[jax.ShapeDtypeStruct((T, 2 * CH), BF16), jax.ShapeDtypeStruct((32, CH), F32),
                   jax.ShapeDtypeStruct((8, CH), F32)],
        scratch_shapes=[pltpu.VMEM((n, CH), F32), pltpu.VMEM((7, n - 8, CH), F32),
                        pltpu.VMEM((n, CH), F32), pltpu.VMEM((7, n - 8, CH), F32), pltpu.VMEM((tm, CH), F32),
                        pltpu.VMEM((32, 8, CH), F32)],
        compiler_params=_cp(1),
    )(dconv, dconv, ypre, ypre, u, u, w, lg, lb)


def _attn_bwd(sinks, tab, qkv, dattn):
    T = qkv.shape[0]
    nb = T // WINDOW

    def body(sink_ref, tab_ref, q_ref, kvp_ref, kvc_ref, do_ref, dq_ref, dkv_ref, dsk_ref, carry_ref):
        n = pl.program_id(0)

        @pl.when(n == 0)
        def _():
            dsk_ref[...] = jnp.zeros_like(dsk_ref)
            carry_ref[...] = jnp.zeros_like(carry_ref)

        @pl.when(n < nb)
        def _():
            seen = _first_block_mask(n)
            for g in range(N_KV):
                qs = _stack_heads(q_ref, g)
                dos = _stack_heads(do_ref, g)
                k = _band(kvp_ref, kvc_ref, g * HEAD_DIM)
                v = _band(kvp_ref, kvc_ref, KV_W + g * HEAD_DIM)
                p, ps = _attn_probs(qs, k, tab_ref[g], seen, _sink_col(sink_ref, g))
                dp = _dot_nt(dos, v)
                delta = jnp.sum(p * dp, axis=-1, keepdims=True)
                dsb = (p * (dp - delta)).astype(BF16)
                dsink = -ps * delta
                dqs = _dot(dsb, k) * SCALE
                dk = _dot_tn(dsb, qs) * SCALE
                dv = _dot_tn(p.astype(BF16), dos)
                for i in range(GROUP):
                    h = GROUP * g + i
                    dq_ref[:, h * HEAD_DIM:(h + 1) * HEAD_DIM] = dqs[i * WINDOW:(i + 1) * WINDOW].astype(BF16)
                    dsk_ref[h:h + 1, :] += jnp.sum(dsink[i * WINDOW:(i + 1) * WINDOW], axis=0, keepdims=True)
                for off, d in ((g * HEAD_DIM, dk), (KV_W + g * HEAD_DIM, dv)):
                    dkv_ref[:, off:off + HEAD_DIM] = (carry_ref[:, off:off + HEAD_DIM] + d[:WINDOW]).astype(BF16)
                    carry_ref[:, off:off + HEAD_DIM] = d[WINDOW:]

        @pl.when(n == nb)
        def _():
            dkv_ref[...] = carry_ref[...].astype(BF16)

    last = nb - 1
    return pl.pallas_call(
        body, name="attn_bwd", grid=(nb + 1,),
        in_specs=[pl.BlockSpec(memory_space=pltpu.SMEM),
                  pl.BlockSpec(tab.shape, lambda n: (0, 0, 0)),
                  pl.BlockSpec((WINDOW, ATTN_W), lambda n: (jnp.minimum(n, last), 0)),
                  pl.BlockSpec((WINDOW, 2 * KV_W), lambda n: (jnp.clip(n - 1, 0, last), 2)),
                  pl.BlockSpec((WINDOW, 2 * KV_W), lambda n: (jnp.minimum(n, last), 2)),
                  pl.BlockSpec((WINDOW, ATTN_W), lambda n: (jnp.minimum(n, last), 0))],
        out_specs=[pl.BlockSpec((WINDOW, ATTN_W), lambda n: (jnp.minimum(n, last), 0)),
                   pl.BlockSpec((WINDOW, 2 * KV_W), lambda n: (jnp.maximum(n - 1, 0), 0)),
                   pl.BlockSpec((8, LANES), lambda n: (0, 0))],
        out_shape=[jax.ShapeDtypeStruct((T, ATTN_W), BF16), jax.ShapeDtypeStruct((T, 2 * KV_W), BF16),
                   jax.ShapeDtypeStruct((8, LANES), F32)],
        scratch_shapes=[pltpu.VMEM((WINDOW, 2 * KV_W), F32)],
        compiler_params=_cp(1),
    )(sinks, tab, qkv, qkv, qkv, dattn)


def _pack(arrs):
    flat = jnp.concatenate([a.reshape(-1) for a in arrs])
    pad = -flat.shape[0] % (8 * LANES)
    return jnp.pad(flat, (0, pad)).reshape(1, -1, LANES)


def _unpack(packed, like):
    flat = packed.reshape(-1)
    out, off = [], 0
    for a in like:
        out.append(flat[off:off + a.size].reshape(a.shape))
        off += a.size
    return out


def kernel(x, norm_ffn1, w_ffn1_in, w_ffn1_out, norm_mix, w_in, sinks, w_dw, b_dw, conv_ln_g, conv_ln_b, w_out, norm_ffn2, w_ffn2_in, w_ffn2_out, final_norm, loss_target, m_norm_ffn1, m_w_ffn1_in, m_w_ffn1_out, m_norm_mix, m_w_in, m_sinks, m_w_dw, m_b_dw, m_conv_ln_g, m_conv_ln_b, m_w_out, m_norm_ffn2, m_w_ffn2_in, m_w_ffn2_out, m_final_norm, v_norm_ffn1, v_w_ffn1_in, v_w_ffn1_out, v_norm_mix, v_w_in, v_sinks, v_w_dw, v_b_dw, v_conv_ln_g, v_conv_ln_b, v_w_out, v_norm_ffn2, v_w_ffn2_in, v_w_ffn2_out, v_final_norm):
    L, D = norm_ffn1.shape
    T = x.shape[1]
    FB = w_ffn1_in.shape[2]
    CH = b_dw.shape[1]
    QKV = ATTN_W + 2 * KV_W
    xs = x.reshape(T, D)
    tgt = loss_target.reshape(T, D)
    cx, cy, cc = lax.axis_index("x"), lax.axis_index("y"), lax.axis_index("c")
    chip = 2 * cx + cy
    cidx = cc.reshape(1).astype(jnp.int32)
    big_w = (w_ffn1_in, w_ffn1_out, w_in, w_out, w_ffn2_in, w_ffn2_out)
    big_m = (m_w_ffn1_in, m_w_ffn1_out, m_w_in, m_w_out, m_w_ffn2_in, m_w_ffn2_out)
    big_v = (v_w_ffn1_in, v_w_ffn1_out, v_w_in, v_w_out, v_w_ffn2_in, v_w_ffn2_out)
    NW = len(big_w) + 1

    def shards(l, tok):
        return [(w_[l] + tok[0, 0]).astype(BF16) for w_ in big_w] + [w_dw[l] + tok[0, 0]]

    def own_slot(a, slots=4, idx=chip):
        return lax.dynamic_update_index_in_dim(lax.empty((slots,) + a.shape, a.dtype), a, idx, 0)

    def gather_start(srcs, tok):
        return _xchg_start("gather_start", srcs, [own_slot(s_) for s_ in srcs], _gather_plan, tok)

    def gather_arrived(started, after):
        _, lands, tok = _xchg_wait("gather_wait", started, NW, NW, _gather_plan, after)
        return _xchg_start("gshare_start", [], lands[:-1], _gshare_plan, tok, "sibling3"), lands[-1]

    row = lambda a, l: a[l].reshape(1, -1)
    tab = _attn_bias_table()
    NB = len(big_w)

    saved, W = [], []
    zero_tok = jnp.zeros((8, LANES), F32)
    started = gather_start(shards(0, zero_tok), zero_tok)
    cast = [None] + [shards(l, started[-1]) for l in range(1, L)]
    shared, gdw = gather_arrived(started, [xs] + [a_ for c_ in cast[1:] for a_ in c_])
    after = [shared[-1]]
    for l in range(L):
        _, (g1i, g1o, gi, go, g2i, g2o), tok = _xchg_wait("gshare_wait", shared, 0, NB, _gshare_plan, after, "sibling3")
        if l + 1 < L:
            started = gather_start(cast[l + 1], tok)
            tok = started[-1]
        w = dict(f1i=g1i, f1o=g1o.reshape(2 * FB, D), f2i=g2i, f2o=g2o.reshape(2 * FB, D),
                 wi=jnp.transpose(gi, (1, 0, 2)).reshape(D, -1), wo=go.reshape(-1, D),
                 wdw=jnp.transpose(gdw, (1, 0, 2)).reshape(CONV_W, CH))
        W.append(w)
        x0 = xs
        x1, gu1 = _ffn_fwd(x0, row(norm_ffn1, l) + tok[0, 0], w["f1i"], w["f1o"])
        qkv, u = _mixproj_fwd(x1, row(norm_mix, l), w["wi"])
        attn = _attn_fwd(row(sinks, l), tab, qkv)
        conv, ypre = _conv_fwd(u, w["wdw"], row(b_dw, l), row(conv_ln_g, l), row(conv_ln_b, l))
        x2 = _mixout_fwd(x1, attn, conv, w["wo"])
        g2_row = row(norm_ffn2, l)
        if l + 1 < L:
            shared, gdw = gather_arrived(started, [x2])
            g2_row = g2_row + shared[-1][0, 0]
        xs, gu2 = _ffn_fwd(x2, g2_row, w["f2i"], w["f2o"])
        saved.append((x0, gu1, x1, qkv, u, attn, conv, ypre, x2, gu2))
        after = [xs]

    loss_part, dx, d_final = _loss_head(xs, final_norm.reshape(1, D), tgt)
    loss = lax.psum(loss_part[0, 0], ("x", "y", "c"))

    bufs = [[lax.empty(w_.shape, F32) for _ in range(4)] for w_ in big_w]
    d_n1, d_nm, d_n2 = [None] * L, [None] * L, [None] * L
    d_sk, d_bdw, d_lg, d_lb, d_wdw = [None] * L, [None] * L, [None] * L, [None] * L, [None] * L

    def reduce_start(sib_started, after):
        gs, sibs, _ = _xchg_wait("sib_wait", sib_started, NB, NB, _sib_plan, after, "sibling")
        parts = [_sum_halves(cidx, g, s_) for g, s_ in zip(gs, sibs)]
        lands = [own_slot(lax.dynamic_index_in_dim(p, chip, 0, keepdims=False)) for p in parts]
        return _xchg_start("rs_start", parts, lands, _rs_plan, zero_tok)

    def share_start(rs_started, after):
        _, qs, tok = _xchg_wait("rs_wait", rs_started, NB, NB, _rs_plan, after)
        return _xchg_start("qshare_start", qs, [lax.empty(q.shape, q.dtype) for q in qs], _whole_plan, tok, "sibling")

    def finish(l, shared, after):
        q_own, q_sib, _ = _xchg_wait("qshare_wait", shared, NB, NB, _whole_plan, after, "sibling")
        for t in range(NB):
            bufs[t] = _adamw_layer(cidx, q_own[t], q_sib[t], big_w[t], big_m[t], big_v[t], bufs[t], l)

    sib_pending = rs_pending = None
    shares = []
    tok = zero_tok
    for l in reversed(range(L)):
        w = W[l]
        x0, gu1, x1, qkv, u, attn, conv, ypre, x2, gu2 = saved[l]
        dx, d_n2[l], hb, dgu, a, dyb = _ffn_bwd(dx, x2, row(norm_ffn2, l), gu2, w["f2i"], w["f2o"], tok)
        g_f2i, g_f2o = _wgrad_ffn_in(hb, dgu), _wgrad_ffn_out(a, dyb)
        lg_row = row(conv_ln_g, l)
        if sib_pending is not None:
            rs_started = reduce_start(sib_pending[1], after=[g_f2o])
            if rs_pending is not None:
                shares.append((rs_pending[0], share_start(rs_pending[1], [rs_started[-1]])))
            rs_pending = (sib_pending[0], rs_started)
            lg_row = lg_row + rs_started[-1][0, 0]
        dyb, dattn, dconv = _mixout_bwd(dx, w["wo"])
        g_wo = _wgrad_cat([attn, conv], [dyb]).reshape(4, -1, D)
        du, dwdw, dvec = _conv_bwd(dconv, ypre, u, w["wdw"], lg_row, row(conv_ln_b, l))
        d_wdw[l], d_bdw[l], d_lg[l], d_lb[l] = dwdw[:CONV_W], dvec[0], dvec[1], dvec[2]
        dq, dkv, dsk = _attn_bwd(row(sinks, l), tab, qkv, dattn)
        d_sk[l] = dsk[:, 0]
        wi = w["wi"]
        dx, d_nm[l], hb = _mix_rms_bwd(dx, x1, row(norm_mix, l), [dq, dkv, du],
                                       [wi[:, :ATTN_W], wi[:, ATTN_W:QKV], wi[:, QKV:]])
        g_wi = jnp.transpose(_wgrad_cat([hb], [dq, dkv, du]).reshape(D, 4, -1), (1, 0, 2))
        dx, d_n1[l], hb, dgu, a, dyb = _ffn_bwd(dx, x0, row(norm_ffn1, l), gu1, w["f1i"], w["f1o"], tok)
        g_f1i, g_f1o = _wgrad_ffn_in(hb, dgu), _wgrad_ffn_out(a, dyb)
        gs = [g_f1i, g_f1o, g_wi, g_wo, g_f2i, g_f2o]
        sib_started = _xchg_start("sib_start", gs, [lax.empty((4, g.shape[1] // 2, g.shape[2]), F32) for g in gs],
                                  _sib_plan, zero_tok, "sibling")
        tok = sib_started[-1]
        sib_pending = (l, sib_started)
    grad_x = dx.reshape(x.shape)

    small_g = [jnp.concatenate(d, axis=0) for d in (d_n1, d_nm, d_n2)] + [d_final, jnp.stack(d_sk)] + \
              [jnp.stack(d) for d in (d_bdw, d_lg, d_lb, d_wdw)]
    packed = _pack(small_g)[0]
    small_started = _xchg_start("small_start", [packed], [own_slot(packed, 8, 4 * cx + 2 * cy + cc)], _slot_plan, tok, "all")

    after = [small_started[-1]]
    if rs_pending is not None:
        shares.append((rs_pending[0], share_start(rs_pending[1], after)))
        after = [shares[-1][1][-1]]
    if shares:
        finish(*shares.pop(0), after)
        after = [bufs[0][0]]
    rs_started = reduce_start(sib_pending[1], after)
    after = [rs_started[-1]]
    for l, sh in shares:
        finish(l, sh, after)
        after = [bufs[0][0]]
    _, (slots,), _ = _xchg_wait("small_wait", small_started, 1, 1, _slot_plan, after, "all")
    small_sum = _unpack(_sum_slots(slots), small_g)
    g_wdw = lax.dynamic_slice_in_dim(small_sum[8], chip * w_dw.shape[2], w_dw.shape[2], axis=2)
    small_g = [small_sum[0], small_sum[1], small_sum[2], small_sum[3].reshape(D), small_sum[4],
               small_sum[5], small_sum[6], small_sum[7], g_wdw]
    small_w = (norm_ffn1, norm_mix, norm_ffn2, final_norm, sinks, b_dw, conv_ln_g, conv_ln_b, w_dw)
    small_m = (m_norm_ffn1, m_norm_mix, m_norm_ffn2, m_final_norm, m_sinks, m_b_dw, m_conv_ln_g, m_conv_ln_b, m_w_dw)
    small_v = (v_norm_ffn1, v_norm_mix, v_norm_ffn2, v_final_norm, v_sinks, v_b_dw, v_conv_ln_g, v_conv_ln_b, v_w_dw)
    upd = _adamw(_pack(small_g), _pack(small_w), _pack(small_m), _pack(small_v))
    small_upd = [_unpack(u_, small_w) for u_ in upd]
    last = share_start(rs_started, [upd[0]])
    finish(sib_pending[0], last, [last[-1]])

    order = ("norm_ffn1", "w_ffn1_in", "w_ffn1_out", "norm_mix", "w_in", "sinks", "w_dw", "b_dw", "conv_ln_g",
             "conv_ln_b", "w_out", "norm_ffn2", "w_ffn2_in", "w_ffn2_out", "final_norm")
    small_names = ("norm_ffn1", "norm_mix", "norm_ffn2", "final_norm", "sinks", "b_dw", "conv_ln_g", "conv_ln_b", "w_dw")
    big_names = ("w_ffn1_in", "w_ffn1_out", "w_in", "w_out", "w_ffn2_in", "w_ffn2_out")
    grads, deltas, new_m, new_v = {}, {}, {}, {}
    for i, nme in enumerate(small_names):
        grads[nme], deltas[nme], new_m[nme], new_v[nme] = small_g[i], small_upd[0][i], small_upd[1][i], small_upd[2][i]
    for i, nme in enumerate(big_names):
        grads[nme], deltas[nme], new_m[nme], new_v[nme] = bufs[i]
    return (loss, grad_x, *[grads[n] for n in order], *[deltas[n] for n in order],
            *[new_m[n] for n in order], *[new_v[n] for n in order])
```

```python
import functools

import jax
import jax.numpy as jnp
from jax import lax
from jax.experimental import pallas as pl
from jax.experimental.pallas import tpu as pltpu

F32, BF16 = jnp.float32, jnp.bfloat16
EPS = 1e-6
NEG_INF = -1e30
HEAD_DIM = 64
N_HEADS = 8
N_KV = 2
GROUP = N_HEADS // N_KV
WINDOW = 128
ATTN_W = N_HEADS * HEAD_DIM
KV_W = N_KV * HEAD_DIM
CONV_W = 31
HALO = 32
CONV_ROWS = 32
SCALE = 1.0 / 8.0
ADAM_LR, ADAM_B1, ADAM_B2, ADAM_EPS, ADAM_WD, ADAM_STEP = 0.001, 0.9, 0.999, 1e-08, 0.01, 10
TM = 512
TM_FFN_BWD = 256
TK_WGRAD = 2048
LANES = 128
VMEM_LIMIT = 52 * 1024 * 1024
MESH = pl.DeviceIdType.MESH
ANY = pl.BlockSpec(memory_space=pl.ANY)
HBM = pl.BlockSpec(memory_space=pltpu.HBM)
SEM = pl.BlockSpec(memory_space=pltpu.SEMAPHORE)
VMEM = pl.BlockSpec(memory_space=pltpu.VMEM)
EFFECT = pltpu.SideEffectType.DATAFLOW_SIDE_EFFECTING
TOKEN = jax.ShapeDtypeStruct((8, LANES), F32)


def _cp(n):
    return pltpu.CompilerParams(dimension_semantics=("arbitrary",) * n, vmem_limit_bytes=VMEM_LIMIT)


def _dot(a, b):
    return jnp.dot(a, b, preferred_element_type=F32)


def _dot_nt(a, b):
    return lax.dot_general(a, b, (((1,), (1,)), ((), ())), preferred_element_type=F32)


def _dot_tn(a, b):
    return lax.dot_general(a, b, (((0,), (0,)), ((), ())), preferred_element_type=F32)


def _sigmoid(v):
    return 1.0 / (1.0 + jnp.exp(-v))


def _place():
    x, y, c = lax.axis_index("x"), lax.axis_index("y"), lax.axis_index("c")
    chips = [(1 - x, y), (x, 1 - y), (1 - x, 1 - y)]
    return x, y, c, chips


def _rcopy(src, dst, send_sems, recv_sems, k, dev):
    return pltpu.make_async_remote_copy(src_ref=src, dst_ref=dst, send_sem=send_sems.at[k],
                                        recv_sem=recv_sems.at[k], device_id=dev, device_id_type=MESH)


def _hbm(a):
    return pltpu.with_memory_space_constraint(a, pltpu.HBM)


PEERS = {"chips": 3, "sibling": 1, "sibling3": 3, "all": 7}


def _targets(mode):
    x, y, c, chips = _place()
    b = 2 * x + y
    if mode == "chips":
        return b, c, [((px, py, c), 2 * px + py) for px, py in chips]
    if mode == "sibling":
        return b, c, [((x, y, 1 - c), b)]
    if mode == "sibling3":
        return b, c, [((x, y, 1 - c), 2 * px + py) for px, py in chips]
    flip = lambda v, f: 1 - v if f else v
    devs = [(flip(x, k >> 2 & 1), flip(y, k >> 1 & 1), flip(c, k & 1)) for k in range(1, 8)]
    return 4 * x + 2 * y + c, c, [(d, 4 * d[0] + 2 * d[1] + d[2]) for d in devs]


def _xchg_start(name, srcs, lands, plan, dep, mode="chips"):
    ns, nl, npeer = len(srcs), len(lands), PEERS[mode]

    def body(*refs):
        land = refs[ns:ns + nl]
        src = refs[:ns] if ns else land
        send_sems, recv_sems, token = refs[ns + nl + 1], refs[ns + nl + 2], refs[-1]
        me, c, peers = _targets(mode)
        for t in range(nl):
            for j, (dev, tag) in enumerate(peers):
                s, d, _ = plan(src[t], land[t], t, me, c, tag)
                _rcopy(s, d, send_sems, recv_sems, npeer * t + j, dev).start()
        token[...] = jnp.zeros_like(token)

    arrs = list(srcs) + list(lands)
    return pl.pallas_call(
        body, name=name,
        out_shape=(pltpu.SemaphoreType.DMA((npeer * nl,)), pltpu.SemaphoreType.DMA((npeer * nl,)),
                   *[pltpu.HBM(a.shape, a.dtype) for a in arrs], TOKEN),
        in_specs=[HBM] * (ns + nl) + [ANY], out_specs=(SEM, SEM, *[HBM] * (ns + nl), VMEM),
        input_output_aliases={i: 2 + i for i in range(ns + nl)},
        compiler_params=pltpu.CompilerParams(has_side_effects=EFFECT),
    )(*[_hbm(a) for a in arrs], dep)


def _xchg_wait(name, started, ns, nl, plan, after, mode="chips"):
    send_sems, recv_sems, thru = started[0], started[1], started[2:2 + ns + nl]
    npeer = PEERS[mode]

    def body(*refs):
        land = refs[ns:ns + nl]
        src = refs[:ns] if ns else land
        send_sems, recv_sems, token = refs[ns + nl], refs[ns + nl + 1], refs[-1]
        me, c, peers = _targets(mode)
        for t in range(nl):
            for j, (dev, tag) in enumerate(peers):
                s, _, a = plan(src[t], land[t], t, me, c, tag)
                cp = _rcopy(s, a, send_sems, recv_sems, npeer * t + j, dev)
                cp.wait_send()
                cp.wait_recv()
        token[...] = jnp.zeros_like(token)

    out = pl.pallas_call(
        body, name=name,
        out_shape=(*[pltpu.HBM(a.shape, a.dtype) for a in thru], TOKEN),
        in_specs=[HBM] * (ns + nl) + [SEM, SEM] + [ANY] * len(after), out_specs=(*[HBM] * (ns + nl), VMEM),
        input_output_aliases={i: i for i in range(ns + nl)},
        compiler_params=pltpu.CompilerParams(has_side_effects=EFFECT),
    )(*thru, send_sems, recv_sems, *after)
    return out[:ns], out[ns:ns + nl], out[-1]


def _half(ref_rows, which):
    h = ref_rows // 2
    return pl.ds(which * h, h)


def _gather_plan(src, land, t, b, c, pb):
    if len(src.shape) == 2 and src.shape[0] % 2 == 0:
        hs = _half(src.shape[0], c)
        return src.at[hs], land.at[b, hs], land.at[pb, hs]
    return src, land.at[b], land.at[pb]


def _gshare_plan(src, land, t, b, c, pb):
    return land.at[pb, _half(land.shape[1], c)], land.at[pb, _half(land.shape[1], c)], land.at[pb, _half(land.shape[1], 1 - c)]


def _rs_plan(src, land, t, b, c, pb):
    return src.at[pb], land.at[b], land.at[pb]


def _sib_plan(src, land, t, b, c, pb):
    return src.at[:, _half(src.shape[1], 1 - c), :], land, land


def _rows_block(h, cap=512):
    for rb in range(min(h, cap) // 16 * 16, 0, -16):
        if h % rb == 0:
            return rb
    return h


def _sum_halves(cidx, g, s):
    _, R, C = g.shape
    rb = _rows_block(R // 2)
    nr = R // 2 // rb

    def body(c_ref, g_ref, s_ref, o_ref):
        o_ref[...] = (g_ref[...] + s_ref[...]).astype(BF16)

    blk = (None, rb, C)
    return pl.pallas_call(
        body, name="sum_halves", out_shape=jax.ShapeDtypeStruct(s.shape, BF16),
        grid_spec=pltpu.PrefetchScalarGridSpec(
            num_scalar_prefetch=1, grid=(4, nr),
            in_specs=[pl.BlockSpec(blk, lambda p, i, c: (p, c[0] * nr + i, 0)),
                      pl.BlockSpec(blk, lambda p, i, c: (p, i, 0))],
            out_specs=pl.BlockSpec(blk, lambda p, i, c: (p, i, 0))),
        compiler_params=_cp(2),
    )(cidx, g, s)


def _whole_plan(src, land, t, me, c, tag):
    return src, land, land


def _slot_plan(src, land, t, me, c, tag):
    return src, land.at[me], land.at[tag]


def _adam_update(gg, w, m, v):
    m2 = ADAM_B1 * m + (1.0 - ADAM_B1) * gg
    v2 = ADAM_B2 * v + (1.0 - ADAM_B2) * (gg * gg)
    mh = m2 / (1.0 - ADAM_B1 ** ADAM_STEP)
    vh = v2 / (1.0 - ADAM_B2 ** ADAM_STEP)
    return -ADAM_LR * (mh / (jnp.sqrt(vh) + ADAM_EPS) + ADAM_WD * w), m2, v2


def _adamw_layer(cidx, q_own, q_sib, w, m, v, bufs, l):
    L, R, C = w.shape
    h = R // 2
    rb = _rows_block(h, 256)
    nr = h // rb

    def body(c_ref, qo_ref, qs_ref, w_ref, m_ref, v_ref, *rest):
        g_ref, d_ref, mo_ref, vo_ref = rest[-4:]
        own = pl.program_id(0) == c_ref[0]
        gg = jnp.zeros((rb, C), F32)
        for s in range(4):
            gg = gg + jnp.where(own, qo_ref[s], qs_ref[s]).astype(F32)
        g_ref[...] = gg
        d_ref[...], mo_ref[...], vo_ref[...] = _adam_update(gg, w_ref[...], m_ref[...], v_ref[...])

    qspec = pl.BlockSpec((4, rb, C), lambda hh, i, c: (0, i, 0))
    wspec = pl.BlockSpec((None, rb, C), lambda hh, i, c: (l, hh * nr + i, 0))
    return pl.pallas_call(
        body, name="adamw_layer", out_shape=[jax.ShapeDtypeStruct(w.shape, F32)] * 4,
        grid_spec=pltpu.PrefetchScalarGridSpec(
            num_scalar_prefetch=1, grid=(2, nr),
            in_specs=[qspec, qspec, wspec, wspec, wspec] + [ANY] * 4, out_specs=[wspec] * 4),
        input_output_aliases={6 + k: k for k in range(4)},
        compiler_params=_cp(2),
    )(cidx, q_own, q_sib, w, m, v, *bufs)


def _adamw(g, w, m, v):
    L, R, C = g.shape
    rb = _rows_block(R)

    def body(g_ref, w_ref, m_ref, v_ref, d_ref, mo_ref, vo_ref):
        d_ref[...], mo_ref[...], vo_ref[...] = _adam_update(g_ref[...], w_ref[...], m_ref[...], v_ref[...])

    spec = pl.BlockSpec((None, rb, C), lambda l, i: (l, i, 0))
    return pl.pallas_call(
        body, name="adamw", grid=(L, R // rb), in_specs=[spec] * 4, out_specs=[spec] * 3,
        out_shape=[jax.ShapeDtypeStruct(g.shape, F32)] * 3, compiler_params=_cp(2),
    )(g, w, m, v)


def _sum_slots(buf):
    def body(b_ref, o_ref):
        acc = b_ref[0]
        for k in range(1, 8):
            acc = acc + b_ref[k]
        o_ref[...] = acc

    return pl.pallas_call(body, name="sum_slots", in_specs=[VMEM], out_specs=VMEM,
                          out_shape=jax.ShapeDtypeStruct(buf.shape[1:], F32))(buf)


def _rms(xf, g):
    r = lax.rsqrt(jnp.mean(xf * xf, axis=-1, keepdims=True) + EPS)
    return xf * r, r


def _lane_chunks(n):
    lo = (n // LANES + 1) // 2 * LANES
    return ((0, lo), (lo, n - lo))


def _load_ffn_weights(win_hbm, wout_hbm, win_v, wout_v, sems):
    fb = win_v.shape[2]
    loads = [pltpu.make_async_copy(win_hbm.at[k], win_v.at[k], sems.at[k]) for k in range(4)]
    loads += [pltpu.make_async_copy(wout_hbm.at[pl.ds(k * fb, fb)], wout_v.at[pl.ds(k * fb, fb)], sems.at[4 + k])
              for k in range(2)]
    for cp in loads:
        cp.start()
    for cp in loads:
        cp.wait()


def _fast_sigmoid(v):
    return pl.reciprocal(1.0 + jnp.exp(-v), approx=True)


def _ffn_fwd(x, g, win, wout):
    T, D = x.shape
    FB = win.shape[2]
    tm = min(TM, T)

    def body(x_ref, g_ref, win_hbm, wout_hbm, xo_ref, gu_ref, win_v, wout_v, sems):
        @pl.when(pl.program_id(0) == 0)
        def _():
            _load_ffn_weights(win_hbm, wout_hbm, win_v, wout_v, sems)

        xf = x_ref[...]
        xh, _ = _rms(xf, None)
        h = (xh * g_ref[...]).astype(BF16)
        acc = jnp.zeros((tm, D), F32)
        for blk in range(2):
            for lo, sz in _lane_chunks(FB):
                cols = pl.ds(blk * FB + lo, sz)
                gate = _dot(h, win_v[blk, :, pl.ds(lo, sz)])
                up = _dot(h, win_v[2 + blk, :, pl.ds(lo, sz)])
                gu_ref[0, :, cols] = gate.astype(BF16)
                gu_ref[1, :, cols] = up.astype(BF16)
                a = (gate * _fast_sigmoid(gate) * up).astype(BF16)
                acc = acc + _dot(a, wout_v[cols, :])
        xo_ref[...] = xf + 0.5 * acc

    row = pl.BlockSpec((tm, D), lambda i: (i, 0))
    return pl.pallas_call(
        body, name="ffn_fwd", grid=(T // tm,),
        in_specs=[row, pl.BlockSpec((1, D), lambda i: (0, 0)), ANY, ANY],
        out_specs=[row, pl.BlockSpec((2, tm, 2 * FB), lambda i: (0, i, 0))],
        out_shape=[jax.ShapeDtypeStruct((T, D), F32), jax.ShapeDtypeStruct((2, T, 2 * FB), BF16)],
        scratch_shapes=[pltpu.VMEM(win.shape, BF16), pltpu.VMEM(wout.shape, BF16), pltpu.SemaphoreType.DMA((6,))],
        compiler_params=_cp(1),
    )(x, g, win, wout)


def _mixproj_fwd(x, g, w):
    T, D = x.shape
    W = w.shape[1]
    QKV = ATTN_W + 2 * KV_W
    tm = min(TM, T)

    def body(x_ref, g_ref, w_ref, qkv_ref, u_ref):
        xh, _ = _rms(x_ref[...], None)
        h = (xh * g_ref[...]).astype(BF16)
        qkv_ref[...] = _dot(h, w_ref[:, :QKV]).astype(BF16)
        u_ref[...] = _dot(h, w_ref[:, QKV:])

    return pl.pallas_call(
        body, name="mixproj_fwd", grid=(T // tm,),
        in_specs=[pl.BlockSpec((tm, D), lambda i: (i, 0)), pl.BlockSpec((1, D), lambda i: (0, 0)),
                  pl.BlockSpec((D, W), lambda i: (0, 0))],
        out_specs=[pl.BlockSpec((tm, QKV), lambda i: (i, 0)), pl.BlockSpec((tm, W - QKV), lambda i: (i, 0))],
        out_shape=[jax.ShapeDtypeStruct((T, QKV), BF16), jax.ShapeDtypeStruct((T, W - QKV), F32)],
        compiler_params=_cp(1),
    )(x, g, w)


def _attn_bias_table():
    rows, cols = GROUP * WINDOW, 2 * WINDOW
    row = lax.broadcasted_iota(jnp.int32, (N_KV, rows, cols), 1)
    col = lax.broadcasted_iota(jnp.int32, (N_KV, rows, cols), 2)
    head = GROUP * lax.broadcasted_iota(jnp.int32, (N_KV, rows, cols), 0) + (row >> 7)
    dist = (row & (WINDOW - 1)) + WINDOW - col
    slope = jnp.exp2(-(head + 1).astype(F32))
    return jnp.where((dist >= 0) & (dist < WINDOW), -slope * dist.astype(F32), NEG_INF)


def _first_block_mask(n):
    col = lax.broadcasted_iota(jnp.int32, (GROUP * WINDOW, 2 * WINDOW), 1)
    return (n > 0) | (col >= WINDOW)


def _sink_col(sink_ref, g):
    hi = lax.broadcasted_iota(jnp.int32, (GROUP * WINDOW, 1), 0) >> 7
    col = jnp.zeros((GROUP * WINDOW, 1), F32)
    for i in range(GROUP):
        col = jnp.where(hi == i, sink_ref[0, GROUP * g + i], col)
    return col


def _stack_heads(ref, g):
    return jnp.concatenate([ref[:, (GROUP * g + i) * HEAD_DIM:(GROUP * g + i + 1) * HEAD_DIM]
                            for i in range(GROUP)], axis=0)


def _band(kvp_ref, kvc_ref, off):
    return jnp.concatenate([kvp_ref[:, off:off + HEAD_DIM], kvc_ref[:, off:off + HEAD_DIM]], axis=0)


def _attn_probs(qs, k, bias, seen, sink):
    s = jnp.where(seen, _dot_nt(qs, k) * SCALE + bias, NEG_INF)
    m = jnp.maximum(jnp.max(s, axis=-1, keepdims=True), sink)
    p = jnp.exp(s - m)
    es = jnp.exp(sink - m)
    den = jnp.sum(p, axis=-1, keepdims=True) + es
    return p / den, es / den


def _attn_fwd(sinks, tab, qkv):
    T = qkv.shape[0]
    nb = T // WINDOW

    def body(sink_ref, tab_ref, q_ref, kvp_ref, kvc_ref, o_ref):
        seen = _first_block_mask(pl.program_id(0))
        for g in range(N_KV):
            qs = _stack_heads(q_ref, g)
            k = _band(kvp_ref, kvc_ref, g * HEAD_DIM)
            v = _band(kvp_ref, kvc_ref, KV_W + g * HEAD_DIM)
            p, _ = _attn_probs(qs, k, tab_ref[g], seen, _sink_col(sink_ref, g))
            o = _dot(p.astype(BF16), v)
            for i in range(GROUP):
                h = GROUP * g + i
                o_ref[:, h * HEAD_DIM:(h + 1) * HEAD_DIM] = o[i * WINDOW:(i + 1) * WINDOW].astype(BF16)

    return pl.pallas_call(
        body, name="attn_fwd", grid=(nb,),
        in_specs=[pl.BlockSpec(memory_space=pltpu.SMEM),
                  pl.BlockSpec(tab.shape, lambda n: (0, 0, 0)),
                  pl.BlockSpec((WINDOW, ATTN_W), lambda n: (n, 0)),
                  pl.BlockSpec((WINDOW, 2 * KV_W), lambda n: (jnp.maximum(n - 1, 0), 2)),
                  pl.BlockSpec((WINDOW, 2 * KV_W), lambda n: (n, 2))],
        out_specs=pl.BlockSpec((WINDOW, ATTN_W), lambda n: (n, 0)),
        out_shape=jax.ShapeDtypeStruct((T, ATTN_W), BF16),
        compiler_params=_cp(1),
    )(sinks, tab, qkv, qkv, qkv)


def _shift_copies(src_ref, dst_ref, n):
    for b in range(1, 8):
        dst_ref[b - 1] = src_ref[b:b + n, :]


def _tap(src_ref, sh_ref, s, c0):
    a, b = divmod(s, 8)
    start = pl.multiple_of(c0 + 8 * a, 8)
    if b == 0:
        return src_ref[pl.ds(start, CONV_ROWS), :]
    return sh_ref[b - 1, pl.ds(start, CONV_ROWS), :]


def _glu_rows(u, ch):
    return u[:, :ch] * _sigmoid(u[:, ch:])


def _fill_z(zs_ref, zsh_ref, uc_ref, up_ref, i, ch, n):
    zs_ref[0:HALO] = jnp.where(i > 0, _glu_rows(up_ref[...], ch), 0.0)
    zs_ref[HALO:] = _glu_rows(uc_ref[...], ch)
    _shift_copies(zs_ref, zsh_ref, n - 8)


def _conv_fwd(u, w, b, lg, lb):
    T = u.shape[0]
    CH = u.shape[1] // 2
    tm = min(TM, T)
    n = tm + HALO
    hb = tm // HALO

    def body(uc_ref, up_ref, w_ref, b_ref, lg_ref, lb_ref, conv_ref, ypre_ref, zs_ref, zsh_ref):
        i = pl.program_id(0)
        _fill_z(zs_ref, zsh_ref, uc_ref, up_ref, i, CH, n)
        bias = b_ref[...]

        def chunk(ci, carry):
            c0 = pl.multiple_of(ci * CONV_ROWS, CONV_ROWS)
            acc = jnp.broadcast_to(bias, (CONV_ROWS, CH))
            for k in range(CONV_W):
                acc = acc + w_ref[k:k + 1, :] * _tap(zs_ref, zsh_ref, HALO - (CONV_W - 1) + k, c0)
            ypre_ref[pl.ds(c0, CONV_ROWS), :] = acc
            return carry

        lax.fori_loop(0, tm // CONV_ROWS, chunk, 0)
        y = ypre_ref[...]
        mu = jnp.mean(y, axis=-1, keepdims=True)
        d = y - mu
        var = jnp.mean(d * d, axis=-1, keepdims=True)
        o = d * lax.rsqrt(var + EPS) * lg_ref[...] + lb_ref[...]
        conv_ref[...] = (o * _sigmoid(o)).astype(BF16)

    vec = pl.BlockSpec((1, CH), lambda i: (0, 0))
    return pl.pallas_call(
        body, name="conv_fwd", grid=(T // tm,),
        in_specs=[pl.BlockSpec((tm, 2 * CH), lambda i: (i, 0)),
                  pl.BlockSpec((HALO, 2 * CH), lambda i: (jnp.maximum(i * hb - 1, 0), 0)),
                  pl.BlockSpec((CONV_W, CH), lambda i: (0, 0)), vec, vec, vec],
        out_specs=[pl.BlockSpec((tm, CH), lambda i: (i, 0)), pl.BlockSpec((tm, CH), lambda i: (i, 0))],
        out_shape=[jax.ShapeDtypeStruct((T, CH), BF16), jax.ShapeDtypeStruct((T, CH), F32)],
        scratch_shapes=[pltpu.VMEM((n, CH), F32), pltpu.VMEM((7, n - 8, CH), F32)],
        compiler_params=_cp(1),
    )(u, u, w, b, lg, lb)


def _mixout_fwd(x, attn, conv, wo):
    T, D = x.shape
    tm = min(TM, T)
    A = attn.shape[1]

    def body(x_ref, a_ref, c_ref, w_ref, xo_ref):
        xo_ref[...] = x_ref[...] + _dot(a_ref[...], w_ref[:A, :]) + _dot(c_ref[...], w_ref[A:, :])

    return pl.pallas_call(
        body, name="mixout_fwd", grid=(T // tm,),
        in_specs=[pl.BlockSpec((tm, D), lambda i: (i, 0)), pl.BlockSpec((tm, A), lambda i: (i, 0)),
                  pl.BlockSpec((tm, conv.shape[1]), lambda i: (i, 0)), pl.BlockSpec(wo.shape, lambda i: (0, 0))],
        out_specs=pl.BlockSpec((tm, D), lambda i: (i, 0)),
        out_shape=jax.ShapeDtypeStruct((T, D), F32),
        compiler_params=_cp(1),
    )(x, attn, conv, wo)


def _rms_bwd_rows(dh, xf, g):
    xh, r = _rms(xf, None)
    dxn = dh * g
    dx = r * (dxn - xh * jnp.mean(dxn * xh, axis=-1, keepdims=True))
    return dx, jnp.sum(dh * xh, axis=0, keepdims=True), xh * g


def _loss_head(x, g, tgt):
    T, D = x.shape
    tm = min(TM, T)

    def body(x_ref, g_ref, t_ref, loss_ref, dx_ref, dg_ref):
        @pl.when(pl.program_id(0) == 0)
        def _():
            loss_ref[...] = jnp.zeros_like(loss_ref)
            dg_ref[...] = jnp.zeros_like(dg_ref)

        xf = x_ref[...]
        g = g_ref[...]
        xh, _ = _rms(xf, None)
        e = xh * g - t_ref[...]
        loss_ref[...] += 0.5 * jnp.sum(jnp.mean(e * e, axis=-1, keepdims=True), axis=0, keepdims=True)
        dx, dg, _ = _rms_bwd_rows(e * (1.0 / D), xf, g)
        dx_ref[...] = dx
        dg_ref[...] += dg

    return pl.pallas_call(
        body, name="loss_head", grid=(T // tm,),
        in_specs=[pl.BlockSpec((tm, D), lambda i: (i, 0)), pl.BlockSpec((1, D), lambda i: (0, 0)),
                  pl.BlockSpec((tm, D), lambda i: (i, 0))],
        out_specs=[pl.BlockSpec((1, 1), lambda i: (0, 0)), pl.BlockSpec((tm, D), lambda i: (i, 0)),
                   pl.BlockSpec((1, D), lambda i: (0, 0))],
        out_shape=[jax.ShapeDtypeStruct((1, 1), F32), jax.ShapeDtypeStruct((T, D), F32),
                   jax.ShapeDtypeStruct((1, D), F32)],
        compiler_params=_cp(1),
    )(x, g, tgt)


def _ffn_bwd(dxo, x, g, gu, win, wout, dep):
    T, D = x.shape
    FB = win.shape[2]
    tm = min(TM_FFN_BWD, T)

    def body(dxo_ref, x_ref, g_ref, gu_ref, win_hbm, wout_hbm, dep_ref,
             dxi_ref, dg_ref, hb_ref, dgu_ref, a_ref, dyb_ref, win_v, wout_v, sems):
        @pl.when(pl.program_id(0) == 0)
        def _():
            _load_ffn_weights(win_hbm, wout_hbm, win_v, wout_v, sems)
            dg_ref[...] = jnp.zeros_like(dg_ref)

        dyb = (0.5 * dxo_ref[...]).astype(BF16)
        dyb_ref[...] = dyb
        dh = jnp.zeros((tm, D), F32)
        for blk in range(2):
            for lo, sz in _lane_chunks(FB):
                cols = pl.ds(blk * FB + lo, sz)
                da = _dot_nt(dyb, wout_v[cols, :])
                gate = gu_ref[0, :, cols].astype(F32)
                up = gu_ref[1, :, cols].astype(F32)
                sg = _fast_sigmoid(gate)
                s = gate * sg
                a_ref[:, cols] = (s * up).astype(BF16)
                dgate = (da * up * (sg * (1.0 + gate * (1.0 - sg)))).astype(BF16)
                dup = (da * s).astype(BF16)
                dgu_ref[0, :, cols] = dgate
                dgu_ref[1, :, cols] = dup
                dh = dh + _dot_nt(dgate, win_v[blk, :, pl.ds(lo, sz)]) + _dot_nt(dup, win_v[2 + blk, :, pl.ds(lo, sz)])
        dx, dg, h = _rms_bwd_rows(dh, x_ref[...], g_ref[...])
        dxi_ref[...] = dxo_ref[...] + dx
        dg_ref[...] += dg
        hb_ref[...] = h.astype(BF16)

    row = pl.BlockSpec((tm, D), lambda i: (i, 0))
    act = pl.BlockSpec((2, tm, 2 * FB), lambda i: (0, i, 0))
    return pl.pallas_call(
        body, name="ffn_bwd", grid=(T // tm,),
        in_specs=[row, row, pl.BlockSpec((1, D), lambda i: (0, 0)), act, ANY, ANY, ANY],
        out_specs=[row, pl.BlockSpec((1, D), lambda i: (0, 0)), row, act,
                   pl.BlockSpec((tm, 2 * FB), lambda i: (i, 0)), row],
        out_shape=[jax.ShapeDtypeStruct((T, D), F32), jax.ShapeDtypeStruct((1, D), F32),
                   jax.ShapeDtypeStruct((T, D), BF16), jax.ShapeDtypeStruct((2, T, 2 * FB), BF16),
                   jax.ShapeDtypeStruct((T, 2 * FB), BF16), jax.ShapeDtypeStruct((T, D), BF16)],
        scratch_shapes=[pltpu.VMEM(win.shape, BF16), pltpu.VMEM(wout.shape, BF16), pltpu.SemaphoreType.DMA((6,))],
        compiler_params=_cp(1),
    )(dxo, x, g, gu, win, wout, dep)


def _rms_matmul_bwd(name, dxo, x, g, dzs, ws, dz_specs, w_specs, nk):
    T, D = x.shape
    tm = min(TM, T)
    npair = len(dzs)

    def body(*refs):
        dxo_ref, x_ref, g_ref = refs[:3]
        dz_refs, w_refs = refs[3:3 + npair], refs[3 + npair:3 + 2 * npair]
        dxi_ref, dg_ref, hb_ref, acc_ref = refs[3 + 2 * npair:]
        i, k = pl.program_id(0), pl.program_id(1)

        @pl.when(k == 0)
        def _():
            acc_ref[...] = jnp.zeros_like(acc_ref)

        @pl.when((i == 0) & (k == 0))
        def _():
            dg_ref[...] = jnp.zeros_like(dg_ref)

        for p in range(npair):
            acc_ref[...] += _dot_nt(dz_refs[p][...], w_refs[p][...])

        @pl.when(k == nk - 1)
        def _():
            dx, dg, h = _rms_bwd_rows(acc_ref[...], x_ref[...], g_ref[...])
            dxi_ref[...] = dxo_ref[...] + dx
            dg_ref[...] += dg
            hb_ref[...] = h.astype(BF16)

    row = pl.BlockSpec((tm, D), lambda i, k: (i, 0))
    return pl.pallas_call(
        body, name=name, grid=(T // tm, nk),
        in_specs=[row, row, pl.BlockSpec((1, D), lambda i, k: (0, 0))] + list(dz_specs) + list(w_specs),
        out_specs=[row, pl.BlockSpec((1, D), lambda i, k: (0, 0)), row],
        out_shape=[jax.ShapeDtypeStruct((T, D), F32), jax.ShapeDtypeStruct((1, D), F32),
                   jax.ShapeDtypeStruct((T, D), BF16)],
        scratch_shapes=[pltpu.VMEM((tm, D), F32)],
        compiler_params=_cp(2),
    )(dxo, x, g, *dzs, *ws)


def _mix_rms_bwd(dxo, x, g, dzs, ws):
    tm = min(TM, x.shape[0])
    return _rms_matmul_bwd(
        "mix_rms_bwd", dxo, x, g, dzs, ws,
        [pl.BlockSpec((tm, dz.shape[1]), lambda i, k: (i, 0)) for dz in dzs],
        [pl.BlockSpec(w.shape, lambda i, k: (0, 0)) for w in ws], 1)


def _wgrad(name, a, b, a_spec, b_spec, out_shape, out_spec, nblk):
    T = a.shape[0]
    tk = min(TK_WGRAD, T)

    def body(a_ref, b_ref, o_ref):
        @pl.when(pl.program_id(1) == 0)
        def _():
            o_ref[...] = jnp.zeros_like(o_ref)

        o_ref[...] += _dot_tn(a_ref[...], b_ref[...]).reshape(o_ref.shape)

    return pl.pallas_call(
        body, name=name, grid=(nblk, T // tk), in_specs=[a_spec, b_spec], out_specs=out_spec,
        out_shape=jax.ShapeDtypeStruct(out_shape, F32), compiler_params=_cp(2),
    )(a, b)


def _wgrad_ffn_in(hb, dgu):
    T, D = hb.shape
    FB = dgu.shape[2] // 2
    tk = min(TK_WGRAD, T)
    return _wgrad("wgrad_ffn_in", hb, dgu,
                  pl.BlockSpec((tk, D), lambda b, k: (k, 0)),
                  pl.BlockSpec((None, tk, FB), lambda b, k: (b // 2, k, b % 2)),
                  (4, D, FB), pl.BlockSpec((None, D, FB), lambda b, k: (b, 0, 0)), 4)


def _wgrad_ffn_out(a, dyb):
    T, D = dyb.shape
    FB = a.shape[1] // 2
    tk = min(TK_WGRAD, T)
    return _wgrad("wgrad_ffn_out", a, dyb,
                  pl.BlockSpec((tk, FB), lambda b, k: (k, b)),
                  pl.BlockSpec((tk, D), lambda b, k: (k, 0)),
                  (4, FB // 2, D), pl.BlockSpec((2, FB // 2, D), lambda b, k: (b, 0, 0)), 2)


def _wgrad_cat(a_list, b_list):
    T = a_list[0].shape[0]
    tk = min(TK_WGRAD, T)
    na = len(a_list)
    M, N = sum(a.shape[1] for a in a_list), sum(b.shape[1] for b in b_list)

    def body(*refs):
        a_refs, b_refs, o_ref = refs[:na], refs[na:-1], refs[-1]

        @pl.when(pl.program_id(0) == 0)
        def _():
            o_ref[...] = jnp.zeros_like(o_ref)

        r0 = 0
        for a_ref in a_refs:
            c0 = 0
            for b_ref in b_refs:
                m, n = a_ref.shape[1], b_ref.shape[1]
                o_ref[r0:r0 + m, c0:c0 + n] += _dot_tn(a_ref[...], b_ref[...])
                c0 += n
            r0 += a_ref.shape[1]

    return pl.pallas_call(
        body, name="wgrad_cat", grid=(T // tk,),
        in_specs=[pl.BlockSpec((tk, v.shape[1]), lambda k: (k, 0)) for v in list(a_list) + list(b_list)],
        out_specs=pl.BlockSpec((M, N), lambda k: (0, 0)),
        out_shape=jax.ShapeDtypeStruct((M, N), F32), compiler_params=_cp(1),
    )(*a_list, *b_list)


def _mixout_bwd(dxo, wo):
    T, D = dxo.shape
    tm = min(TM, T)
    A = ATTN_W
    C = wo.shape[0] - A

    def body(dxo_ref, w_ref, dyb_ref, da_ref, dc_ref):
        dyb = dxo_ref[...].astype(BF16)
        dyb_ref[...] = dyb
        da_ref[...] = _dot_nt(dyb, w_ref[:A, :]).astype(BF16)
        dc_ref[...] = _dot_nt(dyb, w_ref[A:, :])

    return pl.pallas_call(
        body, name="mixout_bwd", grid=(T // tm,),
        in_specs=[pl.BlockSpec((tm, D), lambda i: (i, 0)), pl.BlockSpec(wo.shape, lambda i: (0, 0))],
        out_specs=[pl.BlockSpec((tm, D), lambda i: (i, 0)), pl.BlockSpec((tm, A), lambda i: (i, 0)),
                   pl.BlockSpec((tm, C), lambda i: (i, 0))],
        out_shape=[jax.ShapeDtypeStruct((T, D), BF16), jax.ShapeDtypeStruct((T, A), BF16),
                   jax.ShapeDtypeStruct((T, C), F32)],
        compiler_params=_cp(1),
    )(dxo, wo)


def _conv_bwd(dconv, ypre, u, w, lg, lb):
    T, CH = dconv.shape
    tm = min(TM, T)
    n = tm + HALO
    hb = tm // HALO
    nt = T // tm
    nchunk = tm // CONV_ROWS

    def body(dc_ref, dcn_ref, yp_ref, ypn_ref, uc_ref, up_ref, w_ref, lg_ref, lb_ref,
             du_ref, dw_ref, dvec_ref, zs_ref, zsh_ref, dy_ref, dysh_ref, dz_ref, dwacc_ref):
        i = pl.program_id(0)

        @pl.when(i == 0)
        def _():
            dwacc_ref[...] = jnp.zeros_like(dwacc_ref)
            dvec_ref[...] = jnp.zeros_like(dvec_ref)

        g, bb = lg_ref[...], lb_ref[...]

        def ln_bwd(dc, yp):
            mu = jnp.mean(yp, axis=-1, keepdims=True)
            d = yp - mu
            rs = lax.rsqrt(jnp.mean(d * d, axis=-1, keepdims=True) + EPS)
            yn = d * rs
            o = yn * g + bb
            sg = _sigmoid(o)
            do = dc * (sg * (1.0 + o * (1.0 - sg)))
            dyn = do * g
            dyp = rs * (dyn - jnp.mean(dyn, axis=-1, keepdims=True)
                        - yn * jnp.mean(dyn * yn, axis=-1, keepdims=True))
            return dyp, do, yn

        dyp, do, yn = ln_bwd(dc_ref[...], yp_ref[...])
        dvec_ref[0:1, :] += jnp.sum(dyp, axis=0, keepdims=True)
        dvec_ref[1:2, :] += jnp.sum(do * yn, axis=0, keepdims=True)
        dvec_ref[2:3, :] += jnp.sum(do, axis=0, keepdims=True)
        dy_ref[0:tm] = dyp
        dyh, _, _ = ln_bwd(dcn_ref[...], ypn_ref[...])
        dy_ref[tm:] = jnp.where(i < nt - 1, dyh, 0.0)
        _shift_copies(dy_ref, dysh_ref, n - 8)
        _fill_z(zs_ref, zsh_ref, uc_ref, up_ref, i, CH, n)

        def chunk(ci, carry):
            c0 = pl.multiple_of(ci * CONV_ROWS, CONV_ROWS)
            acc = jnp.zeros((CONV_ROWS, CH), F32)
            for k in range(CONV_W):
                acc = acc + w_ref[k:k + 1, :] * _tap(dy_ref, dysh_ref, CONV_W - 1 - k, c0)
            dz_ref[pl.ds(c0, CONV_ROWS), :] = acc
            dyc = dy_ref[pl.ds(c0, CONV_ROWS), :]
            for k in range(CONV_W):
                prod = dyc * _tap(zs_ref, zsh_ref, HALO - (CONV_W - 1) + k, c0)
                dwacc_ref[k] += jnp.sum(prod.reshape(CONV_ROWS // 8, 8, CH), axis=0)
            return carry

        lax.fori_loop(0, nchunk, chunk, 0)

        @pl.when(i == nt - 1)
        def _():
            dw_ref[...] = jnp.sum(dwacc_ref[...], axis=1)

        uc = uc_ref[...]
        a = uc[:, :CH]
        sg = _sigmoid(uc[:, CH:])
        dz = dz_ref[...]
        du_ref[:, :CH] = (dz * sg).astype(BF16)
        du_ref[:, CH:] = (dz * a * sg * (1.0 - sg)).astype(BF16)

    cur = lambda c: pl.BlockSpec((tm, c), lambda i: (i, 0))
    nxt = lambda c: pl.BlockSpec((HALO, c), lambda i: (jnp.minimum((i + 1) * hb, T // HALO - 1), 0))
    vec = pl.BlockSpec((1, CH), lambda i: (0, 0))
    return pl.pallas_call(
        body, name="conv_bwd", grid=(nt,),
        in_specs=[cur(CH), nxt(CH), cur(CH), nxt(CH), cur(2 * CH),
                  pl.BlockSpec((HALO, 2 * CH), lambda i: (jnp.maximum(i * hb - 1, 0), 0)),
                  pl.BlockSpec((CONV_W, CH), lambda i: (0, 0)), vec, vec],
        out_specs=[pl.BlockSpec((tm, 2 * CH), lambda i: (i, 0)), pl.BlockSpec((32, CH), lambda i: (0, 0)),
                   pl.BlockSpec((8, CH), lambda i: (0, 0))],
        out_shape=[jax.ShapeDtypeStruct((T, 2 * CH), BF16), jax.ShapeDtypeStruct((32, CH), F32),
                   jax.ShapeDtypeStruct((8, CH), F32)],
        scratch_shapes=[pltpu.VMEM((n, CH), F32), pltpu.VMEM((7, n - 8, CH), F32),
                        pltpu.VMEM((n, CH), F32), pltpu.VMEM((7, n - 8, CH), F32), pltpu.VMEM((tm, CH), F32),
                        pltpu.VMEM((32, 8, CH), F32)],
        compiler_params=_cp(1),
    )(dconv, dconv, ypre, ypre, u, u, w, lg, lb)


def _attn_bwd(sinks, tab, qkv, dattn):
    T = qkv.shape[0]
    nb = T // WINDOW

    def body(sink_ref, tab_ref, q_ref, kvp_ref, kvc_ref, do_ref, dq_ref, dkv_ref, dsk_ref, carry_ref):
        n = pl.program_id(0)

        @pl.when(n == 0)
        def _():
            dsk_ref[...] = jnp.zeros_like(dsk_ref)
            carry_ref[...] = jnp.zeros_like(carry_ref)

        @pl.when(n < nb)
        def _():
            seen = _first_block_mask(n)
            for g in range(N_KV):
                qs = _stack_heads(q_ref, g)
                dos = _stack_heads(do_ref, g)
                k = _band(kvp_ref, kvc_ref, g * HEAD_DIM)
                v = _band(kvp_ref, kvc_ref, KV_W + g * HEAD_DIM)
                p, ps = _attn_probs(qs, k, tab_ref[g], seen, _sink_col(sink_ref, g))
                dp = _dot_nt(dos, v)
                delta = jnp.sum(p * dp, axis=-1, keepdims=True)
                dsb = (p * (dp - delta)).astype(BF16)
                dsink = -ps * delta
                dqs = _dot(dsb, k) * SCALE
                dk = _dot_tn(dsb, qs) * SCALE
                dv = _dot_tn(p.astype(BF16), dos)
                for i in range(GROUP):
                    h = GROUP * g + i
                    dq_ref[:, h * HEAD_DIM:(h + 1) * HEAD_DIM] = dqs[i * WINDOW:(i + 1) * WINDOW].astype(BF16)
                    dsk_ref[h:h + 1, :] += jnp.sum(dsink[i * WINDOW:(i + 1) * WINDOW], axis=0, keepdims=True)
                for off, d in ((g * HEAD_DIM, dk), (KV_W + g * HEAD_DIM, dv)):
                    dkv_ref[:, off:off + HEAD_DIM] = (carry_ref[:, off:off + HEAD_DIM] + d[:WINDOW]).astype(BF16)
                    carry_ref[:, off:off + HEAD_DIM] = d[WINDOW:]

        @pl.when(n == nb)
        def _():
            dkv_ref[...] = carry_ref[...].astype(BF16)

    last = nb - 1
    return pl.pallas_call(
        body, name="attn_bwd", grid=(nb + 1,),
        in_specs=[pl.BlockSpec(memory_space=pltpu.SMEM),
                  pl.BlockSpec(tab.shape, lambda n: (0, 0, 0)),
                  pl.BlockSpec((WINDOW, ATTN_W), lambda n: (jnp.minimum(n, last), 0)),
                  pl.BlockSpec((WINDOW, 2 * KV_W), lambda n: (jnp.clip(n - 1, 0, last), 2)),
                  pl.BlockSpec((WINDOW, 2 * KV_W), lambda n: (jnp.minimum(n, last), 2)),
                  pl.BlockSpec((WINDOW, ATTN_W), lambda n: (jnp.minimum(n, last), 0))],
        out_specs=[pl.BlockSpec((WINDOW, ATTN_W), lambda n: (jnp.minimum(n, last), 0)),
                   pl.BlockSpec((WINDOW, 2 * KV_W), lambda n: (jnp.maximum(n - 1, 0), 0)),
                   pl.BlockSpec((8, LANES), lambda n: (0, 0))],
        out_shape=[jax.ShapeDtypeStruct((T, ATTN_W), BF16), jax.ShapeDtypeStruct((T, 2 * KV_W), BF16),
                   jax.ShapeDtypeStruct((8, LANES), F32)],
        scratch_shapes=[pltpu.VMEM((WINDOW, 2 * KV_W), F32)],
        compiler_params=_cp(1),
    )(sinks, tab, qkv, qkv, qkv, dattn)


def _pack(arrs):
    flat = jnp.concatenate([a.reshape(-1) for a in arrs])
    pad = -flat.shape[0] % (8 * LANES)
    return jnp.pad(flat, (0, pad)).reshape(1, -1, LANES)


def _unpack(packed, like):
    flat = packed.reshape(-1)
    out, off = [], 0
    for a in like:
        out.append(flat[off:off + a.size].reshape(a.shape))
        off += a.size
    return out


def kernel(x, norm_ffn1, w_ffn1_in, w_ffn1_out, norm_mix, w_in, sinks, w_dw, b_dw, conv_ln_g, conv_ln_b, w_out, norm_ffn2, w_ffn2_in, w_ffn2_out, final_norm, loss_target, m_norm_ffn1, m_w_ffn1_in, m_w_ffn1_out, m_norm_mix, m_w_in, m_sinks, m_w_dw, m_b_dw, m_conv_ln_g, m_conv_ln_b, m_w_out, m_norm_ffn2, m_w_ffn2_in, m_w_ffn2_out, m_final_norm, v_norm_ffn1, v_w_ffn1_in, v_w_ffn1_out, v_norm_mix, v_w_in, v_sinks, v_w_dw, v_b_dw, v_conv_ln_g, v_conv_ln_b, v_w_out, v_norm_ffn2, v_w_ffn2_in, v_w_ffn2_out, v_final_norm):
    L, D = norm_ffn1.shape
    T = x.shape[1]
    FB = w_ffn1_in.shape[2]
    CH = b_dw.shape[1]
    QKV = ATTN_W + 2 * KV_W
    xs = x.reshape(T, D)
    tgt = loss_target.reshape(T, D)
    cx, cy, cc = lax.axis_index("x"), lax.axis_index("y"), lax.axis_index("c")
    chip = 2 * cx + cy
    cidx = cc.reshape(1).astype(jnp.int32)
    big_w = (w_ffn1_in, w_ffn1_out, w_in, w_out, w_ffn2_in, w_ffn2_out)
    big_m = (m_w_ffn1_in, m_w_ffn1_out, m_w_in, m_w_out, m_w_ffn2_in, m_w_ffn2_out)
    big_v = (v_w_ffn1_in, v_w_ffn1_out, v_w_in, v_w_out, v_w_ffn2_in, v_w_ffn2_out)
    NW = len(big_w) + 1

    def shards(l, tok):
        return [(w_[l] + tok[0, 0]).astype(BF16) for w_ in big_w] + [w_dw[l] + tok[0, 0]]

    def own_slot(a, slots=4, idx=chip):
        return lax.dynamic_update_index_in_dim(lax.empty((slots,) + a.shape, a.dtype), a, idx, 0)

    def gather_start(srcs, tok):
        return _xchg_start("gather_start", srcs, [own_slot(s_) for s_ in srcs], _gather_plan, tok)

    def gather_arrived(started, after, n, taps):
        _, lands, tok = _xchg_wait("gather_wait", started, n, n, _gather_plan, after)
        return _xchg_start("gshare_start", [], lands[:-1] if taps else lands, _gshare_plan, tok, "sibling3"), lands[-1]

    def shared_weights(shared, after, n):
        _, mats, tok = _xchg_wait("gshare_wait", shared, 0, n, _gshare_plan, after, "sibling3")
        return mats, tok

    row = lambda a, l: a[l].reshape(1, -1)
    tab = _attn_bias_table()
    NB = len(big_w)

    saved, W = [], []
    zero_tok = jnp.zeros((8, LANES), F32)
    src0 = shards(0, zero_tok)
    started = gather_start(src0[:2], zero_tok)
    rest0 = gather_start(src0[2:], started[-1])
    cast = [None] + [shards(l, rest0[-1]) for l in range(1, L)]
    shared, _ = gather_arrived(started, [xs] + [a_ for c_ in cast[1:] for a_ in c_], 2, False)
    after = [shared[-1]]
    for l in range(L):
        mats, tok = shared_weights(shared, after, 2 if l == 0 else NB)
        started = None
        if l + 1 < L:
            started = gather_start(cast[l + 1], tok)
            tok = started[-1]
        x0 = xs
        x1, gu1 = _ffn_fwd(x0, row(norm_ffn1, l) + tok[0, 0], mats[0], mats[1].reshape(2 * FB, D))
        gm_row = row(norm_mix, l)
        if l == 0:
            shared, gdw = gather_arrived(rest0, [x1], NW - 2, True)
            rest, tok = shared_weights(shared, [shared[-1]], NB - 2)
            mats = list(mats) + list(rest)
            gm_row = gm_row + tok[0, 0]
        g1i, g1o, gi, go, g2i, g2o = mats
        w = dict(f1i=g1i, f1o=g1o.reshape(2 * FB, D), f2i=g2i, f2o=g2o.reshape(2 * FB, D),
                 wi=jnp.transpose(gi, (1, 0, 2)).reshape(D, -1), wo=go.reshape(-1, D),
                 wdw=jnp.transpose(gdw, (1, 0, 2)).reshape(CONV_W, CH))
        W.append(w)
        qkv, u = _mixproj_fwd(x1, gm_row, w["wi"])
        attn = _attn_fwd(row(sinks, l), tab, qkv)
        conv, ypre = _conv_fwd(u, w["wdw"], row(b_dw, l), row(conv_ln_g, l), row(conv_ln_b, l))
        x2 = _mixout_fwd(x1, attn, conv, w["wo"])
        g2_row = row(norm_ffn2, l)
        if started is not None:
            shared, gdw = gather_arrived(started, [x2], NW, True)
            g2_row = g2_row + shared[-1][0, 0]
        xs, gu2 = _ffn_fwd(x2, g2_row, w["f2i"], w["f2o"])
        saved.append((x0, gu1, x1, qkv, u, attn, conv, ypre, x2, gu2))
        after = [xs]

    loss_part, dx, d_final = _loss_head(xs, final_norm.reshape(1, D), tgt)
    loss = lax.psum(loss_part[0, 0], ("x", "y", "c"))

    bufs = [[lax.empty(w_.shape, F32) for _ in range(4)] for w_ in big_w]
    d_n1, d_nm, d_n2 = [None] * L, [None] * L, [None] * L
    d_sk, d_bdw, d_lg, d_lb, d_wdw = [None] * L, [None] * L, [None] * L, [None] * L, [None] * L

    def sib_start(gs):
        return _xchg_start("sib_start", gs, [lax.empty((4, g.shape[1] // 2, g.shape[2]), F32) for g in gs],
                           _sib_plan, zero_tok, "sibling")

    def reduce_start(sib_started, after, n):
        gs, sibs, _ = _xchg_wait("sib_wait", sib_started, n, n, _sib_plan, after, "sibling")
        parts = [_sum_halves(cidx, g, s_) for g, s_ in zip(gs, sibs)]
        lands = [own_slot(lax.dynamic_index_in_dim(p, chip, 0, keepdims=False)) for p in parts]
        return _xchg_start("rs_start", parts, lands, _rs_plan, zero_tok)

    def share_start(rs_started, after, n):
        _, qs, tok = _xchg_wait("rs_wait", rs_started, n, n, _rs_plan, after)
        return _xchg_start("qshare_start", qs, [lax.empty(q.shape, q.dtype) for q in qs], _whole_plan, tok, "sibling")

    def finish(l, shared, after, idxs):
        q_own, q_sib, _ = _xchg_wait("qshare_wait", shared, len(idxs), len(idxs), _whole_plan, after, "sibling")
        for k, t in enumerate(idxs):
            bufs[t] = _adamw_layer(cidx, q_own[k], q_sib[k], big_w[t], big_m[t], big_v[t], bufs[t], l)

    ALL = list(range(NB))
    EARLY, LATE = ALL[2:], ALL[:2]
    sib_pending = rs_pending = None
    shares = []
    tok = zero_tok
    for l in reversed(range(L)):
        w = W[l]
        x0, gu1, x1, qkv, u, attn, conv, ypre, x2, gu2 = saved[l]
        dx, d_n2[l], hb, dgu, a, dyb = _ffn_bwd(dx, x2, row(norm_ffn2, l), gu2, w["f2i"], w["f2o"], tok)
        g_f2i, g_f2o = _wgrad_ffn_in(hb, dgu), _wgrad_ffn_out(a, dyb)
        lg_row = row(conv_ln_g, l)
        if sib_pending is not None:
            rs_started = reduce_start(sib_pending[1], [g_f2o], NB)
            if rs_pending is not None:
                shares.append((rs_pending[0], share_start(rs_pending[1], [rs_started[-1]], NB)))
            rs_pending = (sib_pending[0], rs_started)
            lg_row = lg_row + rs_started[-1][0, 0]
        dyb, dattn, dconv = _mixout_bwd(dx, w["wo"])
        g_wo = _wgrad_cat([attn, conv], [dyb]).reshape(4, -1, D)
        du, dwdw, dvec = _conv_bwd(dconv, ypre, u, w["wdw"], lg_row, row(conv_ln_b, l))
        d_wdw[l], d_bdw[l], d_lg[l], d_lb[l] = dwdw[:CONV_W], dvec[0], dvec[1], dvec[2]
        dq, dkv, dsk = _attn_bwd(row(sinks, l), tab, qkv, dattn)
        d_sk[l] = dsk[:, 0]
        wi = w["wi"]
        dx, d_nm[l], hb = _mix_rms_bwd(dx, x1, row(norm_mix, l), [dq, dkv, du],
                                       [wi[:, :ATTN_W], wi[:, ATTN_W:QKV], wi[:, QKV:]])
        g_wi = jnp.transpose(_wgrad_cat([hb], [dq, dkv, du]).reshape(D, 4, -1), (1, 0, 2))
        if l == 0:
            sib_early = sib_start([g_wi, g_wo, g_f2i, g_f2o])
            tok = sib_early[-1]
        dx, d_n1[l], hb, dgu, a, dyb = _ffn_bwd(dx, x0, row(norm_ffn1, l), gu1, w["f1i"], w["f1o"], tok)
        if l == 0:
            rs_early = reduce_start(sib_early, [dx], len(EARLY))
        g_f1i, g_f1o = _wgrad_ffn_in(hb, dgu), _wgrad_ffn_out(a, dyb)
        sib_started = sib_start([g_f1i, g_f1o] if l == 0 else [g_f1i, g_f1o, g_wi, g_wo, g_f2i, g_f2o])
        tok = sib_started[-1]
        sib_pending = (l, sib_started)
    grad_x = dx.reshape(x.shape)

    small_g = [jnp.concatenate(d, axis=0) for d in (d_n1, d_nm, d_n2)] + [d_final, jnp.stack(d_sk)] + \
              [jnp.stack(d) for d in (d_bdw, d_lg, d_lb, d_wdw)]
    packed = _pack(small_g)[0]
    small_started = _xchg_start("small_start", [packed], [own_slot(packed, 8, 4 * cx + 2 * cy + cc)], _slot_plan, tok, "all")

    after = [small_started[-1]]
    if rs_pending is not None:
        shares.append((rs_pending[0], share_start(rs_pending[1], after, NB)))
        after = [shares[-1][1][-1]]
    if shares:
        finish(*shares.pop(0), after, ALL)
        after = [bufs[0][0]]
    rs_late = reduce_start(sib_pending[1], after, len(LATE))
    after = [rs_late[-1]]
    for l, sh in shares:
        finish(l, sh, after, ALL)
        after = [bufs[0][0]]
    _, (slots,), _ = _xchg_wait("small_wait", small_started, 1, 1, _slot_plan, after, "all")
    small_sum = _unpack(_sum_slots(slots), small_g)
    g_wdw = lax.dynamic_slice_in_dim(small_sum[8], chip * w_dw.shape[2], w_dw.shape[2], axis=2)
    small_g = [small_sum[0], small_sum[1], small_sum[2], small_sum[3].reshape(D), small_sum[4],
               small_sum[5], small_sum[6], small_sum[7], g_wdw]
    small_w = (norm_ffn1, norm_mix, norm_ffn2, final_norm, sinks, b_dw, conv_ln_g, conv_ln_b, w_dw)
    small_m = (m_norm_ffn1, m_norm_mix, m_norm_ffn2, m_final_norm, m_sinks, m_b_dw, m_conv_ln_g, m_conv_ln_b, m_w_dw)
    small_v = (v_norm_ffn1, v_norm_mix, v_norm_ffn2, v_final_norm, v_sinks, v_b_dw, v_conv_ln_g, v_conv_ln_b, v_w_dw)
    upd = _adamw(_pack(small_g), _pack(small_w), _pack(small_m), _pack(small_v))
    small_upd = [_unpack(u_, small_w) for u_ in upd]
    sh_early = share_start(rs_early, [upd[0]], len(EARLY))
    sh_late = share_start(rs_late, [sh_early[-1]], len(LATE))
    finish(0, sh_early, [sh_late[-1]], EARLY)
    finish(0, sh_late, [bufs[EARLY[0]][0]], LATE)

    order = ("norm_ffn1", "w_ffn1_in", "w_ffn1_out", "norm_mix", "w_in", "sinks", "w_dw", "b_dw", "conv_ln_g",
             "conv_ln_b", "w_out", "norm_ffn2", "w_ffn2_in", "w_ffn2_out", "final_norm")
    small_names = ("norm_ffn1", "norm_mix", "norm_ffn2", "final_norm", "sinks", "b_dw", "conv_ln_g", "conv_ln_b", "w_dw")
    big_names = ("w_ffn1_in", "w_ffn1_out", "w_in", "w_out", "w_ffn2_in", "w_ffn2_out")
    grads, deltas, new_m, new_v = {}, {}, {}, {}
    for i, nme in enumerate(small_names):
        grads[nme], deltas[nme], new_m[nme], new_v[nme] = small_g[i], small_upd[0][i], small_upd[1][i], small_upd[2][i]
    for i, nme in enumerate(big_names):
        grads[nme], deltas[nme], new_m[nme], new_v[nme] = bufs[i]
    return (loss, grad_x, *[grads[n] for n in order], *[deltas[n] for n in order],
            *[new_m[n] for n in order], *[new_v[n] for n in order])
```

```python
import functools

import jax
import jax.numpy as jnp
from jax import lax
from jax.experimental import pallas as pl
from jax.experimental.pallas import tpu as pltpu

F32, BF16 = jnp.float32, jnp.bfloat16
EPS = 1e-6
NEG_INF = -1e30
HEAD_DIM = 64
N_HEADS = 8
N_KV = 2
GROUP = N_HEADS // N_KV
WINDOW = 128
ATTN_W = N_HEADS * HEAD_DIM
KV_W = N_KV * HEAD_DIM
CONV_W = 31
HALO = 32
CONV_ROWS = 32
SCALE = 1.0 / 8.0
ADAM_LR, ADAM_B1, ADAM_B2, ADAM_EPS, ADAM_WD, ADAM_STEP = 0.001, 0.9, 0.999, 1e-08, 0.01, 10
TM = 512
TM_FFN_BWD = 256
TK_WGRAD = 2048
LANES = 128
VMEM_LIMIT = 52 * 1024 * 1024
MESH = pl.DeviceIdType.MESH
ANY = pl.BlockSpec(memory_space=pl.ANY)
HBM = pl.BlockSpec(memory_space=pltpu.HBM)
SEM = pl.BlockSpec(memory_space=pltpu.SEMAPHORE)
VMEM = pl.BlockSpec(memory_space=pltpu.VMEM)
EFFECT = pltpu.SideEffectType.DATAFLOW_SIDE_EFFECTING
TOKEN = jax.ShapeDtypeStruct((8, LANES), F32)


def _cp(n):
    return pltpu.CompilerParams(dimension_semantics=("arbitrary",) * n, vmem_limit_bytes=VMEM_LIMIT)


def _dot(a, b):
    return jnp.dot(a, b, preferred_element_type=F32)


def _dot_nt(a, b):
    return lax.dot_general(a, b, (((1,), (1,)), ((), ())), preferred_element_type=F32)


def _dot_tn(a, b):
    return lax.dot_general(a, b, (((0,), (0,)), ((), ())), preferred_element_type=F32)


def _sigmoid(v):
    return 1.0 / (1.0 + jnp.exp(-v))


def _place():
    x, y, c = lax.axis_index("x"), lax.axis_index("y"), lax.axis_index("c")
    chips = [(1 - x, y), (x, 1 - y), (1 - x, 1 - y)]
    return x, y, c, chips


def _rcopy(src, dst, send_sems, recv_sems, k, dev):
    return pltpu.make_async_remote_copy(src_ref=src, dst_ref=dst, send_sem=send_sems.at[k],
                                        recv_sem=recv_sems.at[k], device_id=dev, device_id_type=MESH)


def _hbm(a):
    return pltpu.with_memory_space_constraint(a, pltpu.HBM)


PEERS = {"chips": 3, "sibling": 1, "sibling3": 3, "all": 7}


def _targets(mode):
    x, y, c, chips = _place()
    b = 2 * x + y
    if mode == "chips":
        return b, c, [((px, py, c), 2 * px + py) for px, py in chips]
    if mode == "sibling":
        return b, c, [((x, y, 1 - c), b)]
    if mode == "sibling3":
        return b, c, [((x, y, 1 - c), 2 * px + py) for px, py in chips]
    flip = lambda v, f: 1 - v if f else v
    devs = [(flip(x, k >> 2 & 1), flip(y, k >> 1 & 1), flip(c, k & 1)) for k in range(1, 8)]
    return 4 * x + 2 * y + c, c, [(d, 4 * d[0] + 2 * d[1] + d[2]) for d in devs]


def _xchg_start(name, srcs, lands, plan, dep, mode="chips"):
    ns, nl, npeer = len(srcs), len(lands), PEERS[mode]

    def body(*refs):
        land = refs[ns:ns + nl]
        src = refs[:ns] if ns else land
        send_sems, recv_sems, token = refs[ns + nl + 1], refs[ns + nl + 2], refs[-1]
        me, c, peers = _targets(mode)
        for t in range(nl):
            for j, (dev, tag) in enumerate(peers):
                s, d, _ = plan(src[t], land[t], t, me, c, tag)
                _rcopy(s, d, send_sems, recv_sems, npeer * t + j, dev).start()
        token[...] = jnp.zeros_like(token)

    arrs = list(srcs) + list(lands)
    return pl.pallas_call(
        body, name=name,
        out_shape=(pltpu.SemaphoreType.DMA((npeer * nl,)), pltpu.SemaphoreType.DMA((npeer * nl,)),
                   *[pltpu.HBM(a.shape, a.dtype) for a in arrs], TOKEN),
        in_specs=[HBM] * (ns + nl) + [ANY], out_specs=(SEM, SEM, *[HBM] * (ns + nl), VMEM),
        input_output_aliases={i: 2 + i for i in range(ns + nl)},
        compiler_params=pltpu.CompilerParams(has_side_effects=EFFECT),
    )(*[_hbm(a) for a in arrs], dep)


def _xchg_wait(name, started, ns, nl, plan, after, mode="chips"):
    send_sems, recv_sems, thru = started[0], started[1], started[2:2 + ns + nl]
    npeer = PEERS[mode]

    def body(*refs):
        land = refs[ns:ns + nl]
        src = refs[:ns] if ns else land
        send_sems, recv_sems, token = refs[ns + nl], refs[ns + nl + 1], refs[-1]
        me, c, peers = _targets(mode)
        for t in range(nl):
            for j, (dev, tag) in enumerate(peers):
                s, _, a = plan(src[t], land[t], t, me, c, tag)
                cp = _rcopy(s, a, send_sems, recv_sems, npeer * t + j, dev)
                cp.wait_send()
                cp.wait_recv()
        token[...] = jnp.zeros_like(token)

    out = pl.pallas_call(
        body, name=name,
        out_shape=(*[pltpu.HBM(a.shape, a.dtype) for a in thru], TOKEN),
        in_specs=[HBM] * (ns + nl) + [SEM, SEM] + [ANY] * len(after), out_specs=(*[HBM] * (ns + nl), VMEM),
        input_output_aliases={i: i for i in range(ns + nl)},
        compiler_params=pltpu.CompilerParams(has_side_effects=EFFECT),
    )(*thru, send_sems, recv_sems, *after)
    return out[:ns], out[ns:ns + nl], out[-1]


def _half(ref_rows, which):
    h = ref_rows // 2
    return pl.ds(which * h, h)


def _gather_plan(src, land, t, b, c, pb):
    if len(src.shape) == 2 and src.shape[0] % 2 == 0:
        hs = _half(src.shape[0], c)
        return src.at[hs], land.at[b, hs], land.at[pb, hs]
    return src, land.at[b], land.at[pb]


def _gshare_plan(src, land, t, b, c, pb):
    return land.at[pb, _half(land.shape[1], c)], land.at[pb, _half(land.shape[1], c)], land.at[pb, _half(land.shape[1], 1 - c)]


def _rs_plan(src, land, t, b, c, pb):
    return src.at[pb], land.at[b], land.at[pb]


def _sib_plan(src, land, t, b, c, pb):
    return src.at[:, _half(src.shape[1], 1 - c), :], land, land


def _rows_block(h, cap=512):
    for rb in range(min(h, cap) // 16 * 16, 0, -16):
        if h % rb == 0:
            return rb
    return h


def _sum_halves(cidx, g, s):
    _, R, C = g.shape
    rb = _rows_block(R // 2)
    nr = R // 2 // rb

    def body(c_ref, g_ref, s_ref, o_ref):
        o_ref[...] = (g_ref[...] + s_ref[...]).astype(BF16)

    blk = (None, rb, C)
    return pl.pallas_call(
        body, name="sum_halves", out_shape=jax.ShapeDtypeStruct(s.shape, BF16),
        grid_spec=pltpu.PrefetchScalarGridSpec(
            num_scalar_prefetch=1, grid=(4, nr),
            in_specs=[pl.BlockSpec(blk, lambda p, i, c: (p, c[0] * nr + i, 0)),
                      pl.BlockSpec(blk, lambda p, i, c: (p, i, 0))],
            out_specs=pl.BlockSpec(blk, lambda p, i, c: (p, i, 0))),
        compiler_params=_cp(2),
    )(cidx, g, s)


def _whole_plan(src, land, t, me, c, tag):
    return src, land, land


def _slot_plan(src, land, t, me, c, tag):
    return src, land.at[me], land.at[tag]


def _adam_update(gg, w, m, v):
    m2 = ADAM_B1 * m + (1.0 - ADAM_B1) * gg
    v2 = ADAM_B2 * v + (1.0 - ADAM_B2) * (gg * gg)
    mh = m2 / (1.0 - ADAM_B1 ** ADAM_STEP)
    vh = v2 / (1.0 - ADAM_B2 ** ADAM_STEP)
    return -ADAM_LR * (mh / (jnp.sqrt(vh) + ADAM_EPS) + ADAM_WD * w), m2, v2


def _adamw_layer(cidx, q_own, q_sib, w, m, v, bufs, l):
    L, R, C = w.shape
    h = R // 2
    rb = _rows_block(h, 256)
    nr = h // rb

    def body(c_ref, qo_ref, qs_ref, w_ref, m_ref, v_ref, *rest):
        g_ref, d_ref, mo_ref, vo_ref = rest[-4:]
        own = pl.program_id(0) == c_ref[0]
        gg = jnp.zeros((rb, C), F32)
        for s in range(4):
            gg = gg + jnp.where(own, qo_ref[s], qs_ref[s]).astype(F32)
        g_ref[...] = gg
        d_ref[...], mo_ref[...], vo_ref[...] = _adam_update(gg, w_ref[...], m_ref[...], v_ref[...])

    qspec = pl.BlockSpec((4, rb, C), lambda hh, i, c: (0, i, 0))
    wspec = pl.BlockSpec((None, rb, C), lambda hh, i, c: (l, hh * nr + i, 0))
    return pl.pallas_call(
        body, name="adamw_layer", out_shape=[jax.ShapeDtypeStruct(w.shape, F32)] * 4,
        grid_spec=pltpu.PrefetchScalarGridSpec(
            num_scalar_prefetch=1, grid=(2, nr),
            in_specs=[qspec, qspec, wspec, wspec, wspec] + [ANY] * 4, out_specs=[wspec] * 4),
        input_output_aliases={6 + k: k for k in range(4)},
        compiler_params=_cp(2),
    )(cidx, q_own, q_sib, w, m, v, *bufs)


def _adamw(g, w, m, v):
    L, R, C = g.shape
    rb = _rows_block(R)

    def body(g_ref, w_ref, m_ref, v_ref, d_ref, mo_ref, vo_ref):
        d_ref[...], mo_ref[...], vo_ref[...] = _adam_update(g_ref[...], w_ref[...], m_ref[...], v_ref[...])

    spec = pl.BlockSpec((None, rb, C), lambda l, i: (l, i, 0))
    return pl.pallas_call(
        body, name="adamw", grid=(L, R // rb), in_specs=[spec] * 4, out_specs=[spec] * 3,
        out_shape=[jax.ShapeDtypeStruct(g.shape, F32)] * 3, compiler_params=_cp(2),
    )(g, w, m, v)


def _sum_slots(buf):
    def body(b_ref, o_ref):
        acc = b_ref[0]
        for k in range(1, 8):
            acc = acc + b_ref[k]
        o_ref[...] = acc

    return pl.pallas_call(body, name="sum_slots", in_specs=[VMEM], out_specs=VMEM,
                          out_shape=jax.ShapeDtypeStruct(buf.shape[1:], F32))(buf)


def _rms(xf, g):
    r = lax.rsqrt(jnp.mean(xf * xf, axis=-1, keepdims=True) + EPS)
    return xf * r, r


def _lane_chunks(n):
    lo = (n // LANES + 1) // 2 * LANES
    return ((0, lo), (lo, n - lo))


def _load_ffn_weights(win_hbm, wout_hbm, win_v, wout_v, sems):
    fb = win_v.shape[2]
    loads = [pltpu.make_async_copy(win_hbm.at[k], win_v.at[k], sems.at[k]) for k in range(4)]
    loads += [pltpu.make_async_copy(wout_hbm.at[pl.ds(k * fb, fb)], wout_v.at[pl.ds(k * fb, fb)], sems.at[4 + k])
              for k in range(2)]
    for cp in loads:
        cp.start()
    for cp in loads:
        cp.wait()


def _fast_sigmoid(v):
    return pl.reciprocal(1.0 + jnp.exp(-v), approx=True)


def _ffn_fwd(x, g, win, wout):
    T, D = x.shape
    FB = win.shape[2]
    tm = min(TM, T)

    def body(x_ref, g_ref, win_hbm, wout_hbm, xo_ref, gu_ref, win_v, wout_v, sems):
        @pl.when(pl.program_id(0) == 0)
        def _():
            _load_ffn_weights(win_hbm, wout_hbm, win_v, wout_v, sems)

        xf = x_ref[...]
        xh, _ = _rms(xf, None)
        h = (xh * g_ref[...]).astype(BF16)
        acc = jnp.zeros((tm, D), F32)
        for blk in range(2):
            for lo, sz in _lane_chunks(FB):
                cols = pl.ds(blk * FB + lo, sz)
                gate = _dot(h, win_v[blk, :, pl.ds(lo, sz)])
                up = _dot(h, win_v[2 + blk, :, pl.ds(lo, sz)])
                gu_ref[0, :, cols] = gate.astype(BF16)
                gu_ref[1, :, cols] = up.astype(BF16)
                a = (gate * _fast_sigmoid(gate) * up).astype(BF16)
                acc = acc + _dot(a, wout_v[cols, :])
        xo_ref[...] = xf + 0.5 * acc

    row = pl.BlockSpec((tm, D), lambda i: (i, 0))
    return pl.pallas_call(
        body, name="ffn_fwd", grid=(T // tm,),
        in_specs=[row, pl.BlockSpec((1, D), lambda i: (0, 0)), ANY, ANY],
        out_specs=[row, pl.BlockSpec((2, tm, 2 * FB), lambda i: (0, i, 0))],
        out_shape=[jax.ShapeDtypeStruct((T, D), F32), jax.ShapeDtypeStruct((2, T, 2 * FB), BF16)],
        scratch_shapes=[pltpu.VMEM(win.shape, BF16), pltpu.VMEM(wout.shape, BF16), pltpu.SemaphoreType.DMA((6,))],
        compiler_params=_cp(1),
    )(x, g, win, wout)


def _mixproj_fwd(x, g, w):
    T, D = x.shape
    W = w.shape[1]
    QKV = ATTN_W + 2 * KV_W
    tm = min(TM, T)

    def body(x_ref, g_ref, w_ref, qkv_ref, u_ref):
        xh, _ = _rms(x_ref[...], None)
        h = (xh * g_ref[...]).astype(BF16)
        qkv_ref[...] = _dot(h, w_ref[:, :QKV]).astype(BF16)
        u_ref[...] = _dot(h, w_ref[:, QKV:])

    return pl.pallas_call(
        body, name="mixproj_fwd", grid=(T // tm,),
        in_specs=[pl.BlockSpec((tm, D), lambda i: (i, 0)), pl.BlockSpec((1, D), lambda i: (0, 0)),
                  pl.BlockSpec((D, W), lambda i: (0, 0))],
        out_specs=[pl.BlockSpec((tm, QKV), lambda i: (i, 0)), pl.BlockSpec((tm, W - QKV), lambda i: (i, 0))],
        out_shape=[jax.ShapeDtypeStruct((T, QKV), BF16), jax.ShapeDtypeStruct((T, W - QKV), F32)],
        compiler_params=_cp(1),
    )(x, g, w)


def _attn_bias_table():
    rows, cols = GROUP * WINDOW, 2 * WINDOW
    row = lax.broadcasted_iota(jnp.int32, (N_KV, rows, cols), 1)
    col = lax.broadcasted_iota(jnp.int32, (N_KV, rows, cols), 2)
    head = GROUP * lax.broadcasted_iota(jnp.int32, (N_KV, rows, cols), 0) + (row >> 7)
    dist = (row & (WINDOW - 1)) + WINDOW - col
    slope = jnp.exp2(-(head + 1).astype(F32))
    return jnp.where((dist >= 0) & (dist < WINDOW), -slope * dist.astype(F32), NEG_INF)


def _first_block_mask(n):
    col = lax.broadcasted_iota(jnp.int32, (GROUP * WINDOW, 2 * WINDOW), 1)
    return (n > 0) | (col >= WINDOW)


def _sink_col(sink_ref, g):
    hi = lax.broadcasted_iota(jnp.int32, (GROUP * WINDOW, 1), 0) >> 7
    col = jnp.zeros((GROUP * WINDOW, 1), F32)
    for i in range(GROUP):
        col = jnp.where(hi == i, sink_ref[0, GROUP * g + i], col)
    return col


def _stack_heads(ref, g):
    return jnp.concatenate([ref[:, (GROUP * g + i) * HEAD_DIM:(GROUP * g + i + 1) * HEAD_DIM]
                            for i in range(GROUP)], axis=0)


def _band(kvp_ref, kvc_ref, off):
    return jnp.concatenate([kvp_ref[:, off:off + HEAD_DIM], kvc_ref[:, off:off + HEAD_DIM]], axis=0)


def _attn_probs(qs, k, bias, seen, sink):
    s = jnp.where(seen, _dot_nt(qs, k) * SCALE + bias, NEG_INF)
    m = jnp.maximum(jnp.max(s, axis=-1, keepdims=True), sink)
    p = jnp.exp(s - m)
    es = jnp.exp(sink - m)
    den = jnp.sum(p, axis=-1, keepdims=True) + es
    return p / den, es / den


def _attn_fwd(sinks, tab, qkv):
    T = qkv.shape[0]
    nb = T // WINDOW

    def body(sink_ref, tab_ref, q_ref, kvp_ref, kvc_ref, o_ref):
        seen = _first_block_mask(pl.program_id(0))
        for g in range(N_KV):
            qs = _stack_heads(q_ref, g)
            k = _band(kvp_ref, kvc_ref, g * HEAD_DIM)
            v = _band(kvp_ref, kvc_ref, KV_W + g * HEAD_DIM)
            p, _ = _attn_probs(qs, k, tab_ref[g], seen, _sink_col(sink_ref, g))
            o = _dot(p.astype(BF16), v)
            for i in range(GROUP):
                h = GROUP * g + i
                o_ref[:, h * HEAD_DIM:(h + 1) * HEAD_DIM] = o[i * WINDOW:(i + 1) * WINDOW].astype(BF16)

    return pl.pallas_call(
        body, name="attn_fwd", grid=(nb,),
        in_specs=[pl.BlockSpec(memory_space=pltpu.SMEM),
                  pl.BlockSpec(tab.shape, lambda n: (0, 0, 0)),
                  pl.BlockSpec((WINDOW, ATTN_W), lambda n: (n, 0)),
                  pl.BlockSpec((WINDOW, 2 * KV_W), lambda n: (jnp.maximum(n - 1, 0), 2)),
                  pl.BlockSpec((WINDOW, 2 * KV_W), lambda n: (n, 2))],
        out_specs=pl.BlockSpec((WINDOW, ATTN_W), lambda n: (n, 0)),
        out_shape=jax.ShapeDtypeStruct((T, ATTN_W), BF16),
        compiler_params=_cp(1),
    )(sinks, tab, qkv, qkv, qkv)


def _shift_copies(src_ref, dst_ref, n):
    for b in range(1, 8):
        dst_ref[b - 1] = src_ref[b:b + n, :]


def _tap(src_ref, sh_ref, s, c0):
    a, b = divmod(s, 8)
    start = pl.multiple_of(c0 + 8 * a, 8)
    if b == 0:
        return src_ref[pl.ds(start, CONV_ROWS), :]
    return sh_ref[b - 1, pl.ds(start, CONV_ROWS), :]


def _glu_rows(u, ch):
    return u[:, :ch] * _sigmoid(u[:, ch:])


def _fill_z(zs_ref, zsh_ref, uc_ref, up_ref, i, ch, n):
    zs_ref[0:HALO] = jnp.where(i > 0, _glu_rows(up_ref[...], ch), 0.0)
    zs_ref[HALO:] = _glu_rows(uc_ref[...], ch)
    _shift_copies(zs_ref, zsh_ref, n - 8)


def _conv_fwd(u, w, b, lg, lb):
    T = u.shape[0]
    CH = u.shape[1] // 2
    tm = min(TM, T)
    n = tm + HALO
    hb = tm // HALO

    def body(uc_ref, up_ref, w_ref, b_ref, lg_ref, lb_ref, conv_ref, ypre_ref, zs_ref, zsh_ref):
        i = pl.program_id(0)
        _fill_z(zs_ref, zsh_ref, uc_ref, up_ref, i, CH, n)
        bias = b_ref[...]

        def chunk(ci, carry):
            c0 = pl.multiple_of(ci * CONV_ROWS, CONV_ROWS)
            acc = jnp.broadcast_to(bias, (CONV_ROWS, CH))
            for k in range(CONV_W):
                acc = acc + w_ref[k:k + 1, :] * _tap(zs_ref, zsh_ref, HALO - (CONV_W - 1) + k, c0)
            ypre_ref[pl.ds(c0, CONV_ROWS), :] = acc
            return carry

        lax.fori_loop(0, tm // CONV_ROWS, chunk, 0)
        y = ypre_ref[...]
        mu = jnp.mean(y, axis=-1, keepdims=True)
        d = y - mu
        var = jnp.mean(d * d, axis=-1, keepdims=True)
        o = d * lax.rsqrt(var + EPS) * lg_ref[...] + lb_ref[...]
        conv_ref[...] = (o * _sigmoid(o)).astype(BF16)

    vec = pl.BlockSpec((1, CH), lambda i: (0, 0))
    return pl.pallas_call(
        body, name="conv_fwd", grid=(T // tm,),
        in_specs=[pl.BlockSpec((tm, 2 * CH), lambda i: (i, 0)),
                  pl.BlockSpec((HALO, 2 * CH), lambda i: (jnp.maximum(i * hb - 1, 0), 0)),
                  pl.BlockSpec((CONV_W, CH), lambda i: (0, 0)), vec, vec, vec],
        out_specs=[pl.BlockSpec((tm, CH), lambda i: (i, 0)), pl.BlockSpec((tm, CH), lambda i: (i, 0))],
        out_shape=[jax.ShapeDtypeStruct((T, CH), BF16), jax.ShapeDtypeStruct((T, CH), F32)],
        scratch_shapes=[pltpu.VMEM((n, CH), F32), pltpu.VMEM((7, n - 8, CH), F32)],
        compiler_params=_cp(1),
    )(u, u, w, b, lg, lb)


def _mixout_fwd(x, attn, conv, wo):
    T, D = x.shape
    tm = min(TM, T)
    A = attn.shape[1]

    def body(x_ref, a_ref, c_ref, w_ref, xo_ref):
        xo_ref[...] = x_ref[...] + _dot(a_ref[...], w_ref[:A, :]) + _dot(c_ref[...], w_ref[A:, :])

    return pl.pallas_call(
        body, name="mixout_fwd", grid=(T // tm,),
        in_specs=[pl.BlockSpec((tm, D), lambda i: (i, 0)), pl.BlockSpec((tm, A), lambda i: (i, 0)),
                  pl.BlockSpec((tm, conv.shape[1]), lambda i: (i, 0)), pl.BlockSpec(wo.shape, lambda i: (0, 0))],
        out_specs=pl.BlockSpec((tm, D), lambda i: (i, 0)),
        out_shape=jax.ShapeDtypeStruct((T, D), F32),
        compiler_params=_cp(1),
    )(x, attn, conv, wo)


def _rms_bwd_rows(dh, xf, g):
    xh, r = _rms(xf, None)
    dxn = dh * g
    dx = r * (dxn - xh * jnp.mean(dxn * xh, axis=-1, keepdims=True))
    return dx, jnp.sum(dh * xh, axis=0, keepdims=True), xh * g


def _loss_head(x, g, tgt):
    T, D = x.shape
    tm = min(TM, T)

    def body(x_ref, g_ref, t_ref, loss_ref, dx_ref, dg_ref):
        @pl.when(pl.program_id(0) == 0)
        def _():
            loss_ref[...] = jnp.zeros_like(loss_ref)
            dg_ref[...] = jnp.zeros_like(dg_ref)

        xf = x_ref[...]
        g = g_ref[...]
        xh, _ = _rms(xf, None)
        e = xh * g - t_ref[...]
        loss_ref[...] += 0.5 * jnp.sum(jnp.mean(e * e, axis=-1, keepdims=True), axis=0, keepdims=True)
        dx, dg, _ = _rms_bwd_rows(e * (1.0 / D), xf, g)
        dx_ref[...] = dx
        dg_ref[...] += dg

    return pl.pallas_call(
        body, name="loss_head", grid=(T // tm,),
        in_specs=[pl.BlockSpec((tm, D), lambda i: (i, 0)), pl.BlockSpec((1, D), lambda i: (0, 0)),
                  pl.BlockSpec((tm, D), lambda i: (i, 0))],
        out_specs=[pl.BlockSpec((1, 1), lambda i: (0, 0)), pl.BlockSpec((tm, D), lambda i: (i, 0)),
                   pl.BlockSpec((1, D), lambda i: (0, 0))],
        out_shape=[jax.ShapeDtypeStruct((1, 1), F32), jax.ShapeDtypeStruct((T, D), F32),
                   jax.ShapeDtypeStruct((1, D), F32)],
        compiler_params=_cp(1),
    )(x, g, tgt)


def _ffn_bwd(dxo, x, g, gu, win, wout, dep):
    T, D = x.shape
    FB = win.shape[2]
    tm = min(TM_FFN_BWD, T)

    def body(dxo_ref, x_ref, g_ref, gu_ref, win_hbm, wout_hbm, dep_ref,
             dxi_ref, dg_ref, hb_ref, dgu_ref, a_ref, dyb_ref, win_v, wout_v, sems):
        @pl.when(pl.program_id(0) == 0)
        def _():
            _load_ffn_weights(win_hbm, wout_hbm, win_v, wout_v, sems)
            dg_ref[...] = jnp.zeros_like(dg_ref)

        dyb = (0.5 * dxo_ref[...]).astype(BF16)
        dyb_ref[...] = dyb
        dh = jnp.zeros((tm, D), F32)
        for blk in range(2):
            for lo, sz in _lane_chunks(FB):
                cols = pl.ds(blk * FB + lo, sz)
                da = _dot_nt(dyb, wout_v[cols, :])
                gate = gu_ref[0, :, cols].astype(F32)
                up = gu_ref[1, :, cols].astype(F32)
                sg = _fast_sigmoid(gate)
                s = gate * sg
                a_ref[:, cols] = (s * up).astype(BF16)
                dgate = (da * up * (sg * (1.0 + gate * (1.0 - sg)))).astype(BF16)
                dup = (da * s).astype(BF16)
                dgu_ref[0, :, cols] = dgate
                dgu_ref[1, :, cols] = dup
                dh = dh + _dot_nt(dgate, win_v[blk, :, pl.ds(lo, sz)]) + _dot_nt(dup, win_v[2 + blk, :, pl.ds(lo, sz)])
        dx, dg, h = _rms_bwd_rows(dh, x_ref[...], g_ref[...])
        dxi_ref[...] = dxo_ref[...] + dx
        dg_ref[...] += dg
        hb_ref[...] = h.astype(BF16)

    row = pl.BlockSpec((tm, D), lambda i: (i, 0))
    act = pl.BlockSpec((2, tm, 2 * FB), lambda i: (0, i, 0))
    return pl.pallas_call(
        body, name="ffn_bwd", grid=(T // tm,),
        in_specs=[row, row, pl.BlockSpec((1, D), lambda i: (0, 0)), act, ANY, ANY, ANY],
        out_specs=[row, pl.BlockSpec((1, D), lambda i: (0, 0)), row, act,
                   pl.BlockSpec((tm, 2 * FB), lambda i: (i, 0)), row],
        out_shape=[jax.ShapeDtypeStruct((T, D), F32), jax.ShapeDtypeStruct((1, D), F32),
                   jax.ShapeDtypeStruct((T, D), BF16), jax.ShapeDtypeStruct((2, T, 2 * FB), BF16),
                   jax.ShapeDtypeStruct((T, 2 * FB), BF16), jax.ShapeDtypeStruct((T, D), BF16)],
        scratch_shapes=[pltpu.VMEM(win.shape, BF16), pltpu.VMEM(wout.shape, BF16), pltpu.SemaphoreType.DMA((6,))],
        compiler_params=_cp(1),
    )(dxo, x, g, gu, win, wout, dep)


def _rms_matmul_bwd(name, dxo, x, g, dzs, ws, dz_specs, w_specs, nk):
    T, D = x.shape
    tm = min(TM, T)
    npair = len(dzs)

    def body(*refs):
        dxo_ref, x_ref, g_ref = refs[:3]
        dz_refs, w_refs = refs[3:3 + npair], refs[3 + npair:3 + 2 * npair]
        dxi_ref, dg_ref, hb_ref, acc_ref = refs[3 + 2 * npair:]
        i, k = pl.program_id(0), pl.program_id(1)

        @pl.when(k == 0)
        def _():
            acc_ref[...] = jnp.zeros_like(acc_ref)

        @pl.when((i == 0) & (k == 0))
        def _():
            dg_ref[...] = jnp.zeros_like(dg_ref)

        for p in range(npair):
            acc_ref[...] += _dot_nt(dz_refs[p][...], w_refs[p][...])

        @pl.when(k == nk - 1)
        def _():
            dx, dg, h = _rms_bwd_rows(acc_ref[...], x_ref[...], g_ref[...])
            dxi_ref[...] = dxo_ref[...] + dx
            dg_ref[...] += dg
            hb_ref[...] = h.astype(BF16)

    row = pl.BlockSpec((tm, D), lambda i, k: (i, 0))
    return pl.pallas_call(
        body, name=name, grid=(T // tm, nk),
        in_specs=[row, row, pl.BlockSpec((1, D), lambda i, k: (0, 0))] + list(dz_specs) + list(w_specs),
        out_specs=[row, pl.BlockSpec((1, D), lambda i, k: (0, 0)), row],
        out_shape=[jax.ShapeDtypeStruct((T, D), F32), jax.ShapeDtypeStruct((1, D), F32),
                   jax.ShapeDtypeStruct((T, D), BF16)],
        scratch_shapes=[pltpu.VMEM((tm, D), F32)],
        compiler_params=_cp(2),
    )(dxo, x, g, *dzs, *ws)


def _mix_rms_bwd(dxo, x, g, dzs, ws):
    tm = min(TM, x.shape[0])
    return _rms_matmul_bwd(
        "mix_rms_bwd", dxo, x, g, dzs, ws,
        [pl.BlockSpec((tm, dz.shape[1]), lambda i, k: (i, 0)) for dz in dzs],
        [pl.BlockSpec(w.shape, lambda i, k: (0, 0)) for w in ws], 1)


def _wgrad(name, a, b, a_spec, b_spec, out_shape, out_spec, nblk, dep):
    T = a.shape[0]
    tk = min(TK_WGRAD, T)

    def body(a_ref, b_ref, dep_ref, o_ref):
        @pl.when(pl.program_id(1) == 0)
        def _():
            o_ref[...] = jnp.zeros_like(o_ref)

        o_ref[...] += _dot_tn(a_ref[...], b_ref[...]).reshape(o_ref.shape)

    return pl.pallas_call(
        body, name=name, grid=(nblk, T // tk), in_specs=[a_spec, b_spec, ANY], out_specs=out_spec,
        out_shape=jax.ShapeDtypeStruct(out_shape, F32), compiler_params=_cp(2),
    )(a, b, dep)


def _wgrad_ffn_in(hb, dgu, dep):
    T, D = hb.shape
    FB = dgu.shape[2] // 2
    tk = min(TK_WGRAD, T)
    return _wgrad("wgrad_ffn_in", hb, dgu,
                  pl.BlockSpec((tk, D), lambda b, k: (k, 0)),
                  pl.BlockSpec((None, tk, FB), lambda b, k: (b // 2, k, b % 2)),
                  (4, D, FB), pl.BlockSpec((None, D, FB), lambda b, k: (b, 0, 0)), 4, dep)


def _wgrad_ffn_out(a, dyb, dep):
    T, D = dyb.shape
    FB = a.shape[1] // 2
    tk = min(TK_WGRAD, T)
    return _wgrad("wgrad_ffn_out", a, dyb,
                  pl.BlockSpec((tk, FB), lambda b, k: (k, b)),
                  pl.BlockSpec((tk, D), lambda b, k: (k, 0)),
                  (4, FB // 2, D), pl.BlockSpec((2, FB // 2, D), lambda b, k: (b, 0, 0)), 2, dep)


def _wgrad_cat(a_list, b_list):
    T = a_list[0].shape[0]
    tk = min(TK_WGRAD, T)
    na = len(a_list)
    M, N = sum(a.shape[1] for a in a_list), sum(b.shape[1] for b in b_list)

    def body(*refs):
        a_refs, b_refs, o_ref = refs[:na], refs[na:-1], refs[-1]

        @pl.when(pl.program_id(0) == 0)
        def _():
            o_ref[...] = jnp.zeros_like(o_ref)

        r0 = 0
        for a_ref in a_refs:
            c0 = 0
            for b_ref in b_refs:
                m, n = a_ref.shape[1], b_ref.shape[1]
                o_ref[r0:r0 + m, c0:c0 + n] += _dot_tn(a_ref[...], b_ref[...])
                c0 += n
            r0 += a_ref.shape[1]

    return pl.pallas_call(
        body, name="wgrad_cat", grid=(T // tk,),
        in_specs=[pl.BlockSpec((tk, v.shape[1]), lambda k: (k, 0)) for v in list(a_list) + list(b_list)],
        out_specs=pl.BlockSpec((M, N), lambda k: (0, 0)),
        out_shape=jax.ShapeDtypeStruct((M, N), F32), compiler_params=_cp(1),
    )(*a_list, *b_list)


def _mixout_bwd(dxo, wo):
    T, D = dxo.shape
    tm = min(TM, T)
    A = ATTN_W
    C = wo.shape[0] - A

    def body(dxo_ref, w_ref, dyb_ref, da_ref, dc_ref):
        dyb = dxo_ref[...].astype(BF16)
        dyb_ref[...] = dyb
        da_ref[...] = _dot_nt(dyb, w_ref[:A, :]).astype(BF16)
        dc_ref[...] = _dot_nt(dyb, w_ref[A:, :])

    return pl.pallas_call(
        body, name="mixout_bwd", grid=(T // tm,),
        in_specs=[pl.BlockSpec((tm, D), lambda i: (i, 0)), pl.BlockSpec(wo.shape, lambda i: (0, 0))],
        out_specs=[pl.BlockSpec((tm, D), lambda i: (i, 0)), pl.BlockSpec((tm, A), lambda i: (i, 0)),
                   pl.BlockSpec((tm, C), lambda i: (i, 0))],
        out_shape=[jax.ShapeDtypeStruct((T, D), BF16), jax.ShapeDtypeStruct((T, A), BF16),
                   jax.ShapeDtypeStruct((T, C), F32)],
        compiler_params=_cp(1),
    )(dxo, wo)


def _conv_bwd(dconv, ypre, u, w, lg, lb):
    T, CH = dconv.shape
    tm = min(TM, T)
    n = tm + HALO
    hb = tm // HALO
    nt = T // tm
    nchunk = tm // CONV_ROWS

    def body(dc_ref, dcn_ref, yp_ref, ypn_ref, uc_ref, up_ref, w_ref, lg_ref, lb_ref,
             du_ref, dw_ref, dvec_ref, zs_ref, zsh_ref, dy_ref, dysh_ref, dz_ref, dwacc_ref):
        i = pl.program_id(0)

        @pl.when(i == 0)
        def _():
            dwacc_ref[...] = jnp.zeros_like(dwacc_ref)
            dvec_ref[...] = jnp.zeros_like(dvec_ref)

        g, bb = lg_ref[...], lb_ref[...]

        def ln_bwd(dc, yp):
            mu = jnp.mean(yp, axis=-1, keepdims=True)
            d = yp - mu
            rs = lax.rsqrt(jnp.mean(d * d, axis=-1, keepdims=True) + EPS)
            yn = d * rs
            o = yn * g + bb
            sg = _sigmoid(o)
            do = dc * (sg * (1.0 + o * (1.0 - sg)))
            dyn = do * g
            dyp = rs * (dyn - jnp.mean(dyn, axis=-1, keepdims=True)
                        - yn * jnp.mean(dyn * yn, axis=-1, keepdims=True))
            return dyp, do, yn

        dyp, do, yn = ln_bwd(dc_ref[...], yp_ref[...])
        dvec_ref[0:1, :] += jnp.sum(dyp, axis=0, keepdims=True)
        dvec_ref[1:2, :] += jnp.sum(do * yn, axis=0, keepdims=True)
        dvec_ref[2:3, :] += jnp.sum(do, axis=0, keepdims=True)
        dy_ref[0:tm] = dyp
        dyh, _, _ = ln_bwd(dcn_ref[...], ypn_ref[...])
        dy_ref[tm:] = jnp.where(i < nt - 1, dyh, 0.0)
        _shift_copies(dy_ref, dysh_ref, n - 8)
        _fill_z(zs_ref, zsh_ref, uc_ref, up_ref, i, CH, n)

        def chunk(ci, carry):
            c0 = pl.multiple_of(ci * CONV_ROWS, CONV_ROWS)
            acc = jnp.zeros((CONV_ROWS, CH), F32)
            for k in range(CONV_W):
                acc = acc + w_ref[k:k + 1, :] * _tap(dy_ref, dysh_ref, CONV_W - 1 - k, c0)
            dz_ref[pl.ds(c0, CONV_ROWS), :] = acc
            dyc = dy_ref[pl.ds(c0, CONV_ROWS), :]
            for k in range(CONV_W):
                prod = dyc * _tap(zs_ref, zsh_ref, HALO - (CONV_W - 1) + k, c0)
                dwacc_ref[k] += jnp.sum(prod.reshape(CONV_ROWS // 8, 8, CH), axis=0)
            return carry

        lax.fori_loop(0, nchunk, chunk, 0)

        @pl.when(i == nt - 1)
        def _():
            dw_ref[...] = jnp.sum(dwacc_ref[...], axis=1)

        uc = uc_ref[...]
        a = uc[:, :CH]
        sg = _sigmoid(uc[:, CH:])
        dz = dz_ref[...]
        du_ref[:, :CH] = (dz * sg).astype(BF16)
        du_ref[:, CH:] = (dz * a * sg * (1.0 - sg)).astype(BF16)

    cur = lambda c: pl.BlockSpec((tm, c), lambda i: (i, 0))
    nxt = lambda c: pl.BlockSpec((HALO, c), lambda i: (jnp.minimum((i + 1) * hb, T // HALO - 1), 0))
    vec = pl.BlockSpec((1, CH), lambda i: (0, 0))
    return pl.pallas_call(
        body, name="conv_bwd", grid=(nt,),
        in_specs=[cur(CH), nxt(CH), cur(CH), nxt(CH), cur(2 * CH),
                  pl.BlockSpec((HALO, 2 * CH), lambda i: (jnp.maximum(i * hb - 1, 0), 0)),
                  pl.BlockSpec((CONV_W, CH), lambda i: (0, 0)), vec, vec],
        out_specs=[pl.BlockSpec((tm, 2 * CH), lambda i: (i, 0)), pl.BlockSpec((32, CH), lambda i: (0, 0)),
                   pl.BlockSpec((8, CH), lambda i: (0, 0))],
        out_shape=[jax.ShapeDtypeStruct((T, 2 * CH), BF16), jax.ShapeDtypeStruct((32, CH), F32),
                   jax.ShapeDtypeStruct((8, CH), F32)],
        scratch_shapes=[pltpu.VMEM((n, CH), F32), pltpu.VMEM((7, n - 8, CH), F32),
                        pltpu.VMEM((n, CH), F32), pltpu.VMEM((7, n - 8, CH), F32), pltpu.VMEM((tm, CH), F32),
                        pltpu.VMEM((32, 8, CH), F32)],
        compiler_params=_cp(1),
    )(dconv, dconv, ypre, ypre, u, u, w, lg, lb)


def _attn_bwd(sinks, tab, qkv, dattn):
    T = qkv.shape[0]
    nb = T // WINDOW

    def body(sink_ref, tab_ref, q_ref, kvp_ref, kvc_ref, do_ref, dq_ref, dkv_ref, dsk_ref, carry_ref):
        n = pl.program_id(0)

        @pl.when(n == 0)
        def _():
            dsk_ref[...] = jnp.zeros_like(dsk_ref)
            carry_ref[...] = jnp.zeros_like(carry_ref)

        @pl.when(n < nb)
        def _():
            seen = _first_block_mask(n)
            for g in range(N_KV):
                qs = _stack_heads(q_ref, g)
                dos = _stack_heads(do_ref, g)
                k = _band(kvp_ref, kvc_ref, g * HEAD_DIM)
                v = _band(kvp_ref, kvc_ref, KV_W + g * HEAD_DIM)
                p, ps = _attn_probs(qs, k, tab_ref[g], seen, _sink_col(sink_ref, g))
                dp = _dot_nt(dos, v)
                delta = jnp.sum(p * dp, axis=-1, keepdims=True)
                dsb = (p * (dp - delta)).astype(BF16)
                dsink = -ps * delta
                dqs = _dot(dsb, k) * SCALE
                dk = _dot_tn(dsb, qs) * SCALE
                dv = _dot_tn(p.astype(BF16), dos)
                for i in range(GROUP):
                    h = GROUP * g + i
                    dq_ref[:, h * HEAD_DIM:(h + 1) * HEAD_DIM] = dqs[i * WINDOW:(i + 1) * WINDOW].astype(BF16)
                    dsk_ref[h:h + 1, :] += jnp.sum(dsink[i * WINDOW:(i + 1) * WINDOW], axis=0, keepdims=True)
                for off, d in ((g * HEAD_DIM, dk), (KV_W + g * HEAD_DIM, dv)):
                    dkv_ref[:, off:off + HEAD_DIM] = (carry_ref[:, off:off + HEAD_DIM] + d[:WINDOW]).astype(BF16)
                    carry_ref[:, off:off + HEAD_DIM] = d[WINDOW:]

        @pl.when(n == nb)
        def _():
            dkv_ref[...] = carry_ref[...].astype(BF16)

    last = nb - 1
    return pl.pallas_call(
        body, name="attn_bwd", grid=(nb + 1,),
        in_specs=[pl.BlockSpec(memory_space=pltpu.SMEM),
                  pl.BlockSpec(tab.shape, lambda n: (0, 0, 0)),
                  pl.BlockSpec((WINDOW, ATTN_W), lambda n: (jnp.minimum(n, last), 0)),
                  pl.BlockSpec((WINDOW, 2 * KV_W), lambda n: (jnp.clip(n - 1, 0, last), 2)),
                  pl.BlockSpec((WINDOW, 2 * KV_W), lambda n: (jnp.minimum(n, last), 2)),
                  pl.BlockSpec((WINDOW, ATTN_W), lambda n: (jnp.minimum(n, last), 0))],
        out_specs=[pl.BlockSpec((WINDOW, ATTN_W), lambda n: (jnp.minimum(n, last), 0)),
                   pl.BlockSpec((WINDOW, 2 * KV_W), lambda n: (jnp.maximum(n - 1, 0), 0)),
                   pl.BlockSpec((8, LANES), lambda n: (0, 0))],
        out_shape=[jax.ShapeDtypeStruct((T, ATTN_W), BF16), jax.ShapeDtypeStruct((T, 2 * KV_W), BF16),
                   jax.ShapeDtypeStruct((8, LANES), F32)],
        scratch_shapes=[pltpu.VMEM((WINDOW, 2 * KV_W), F32)],
        compiler_params=_cp(1),
    )(sinks, tab, qkv, qkv, qkv, dattn)


def _pack(arrs):
    flat = jnp.concatenate([a.reshape(-1) for a in arrs])
    pad = -flat.shape[0] % (8 * LANES)
    return jnp.pad(flat, (0, pad)).reshape(1, -1, LANES)


def _unpack(packed, like):
    flat = packed.reshape(-1)
    out, off = [], 0
    for a in like:
        out.append(flat[off:off + a.size].reshape(a.shape))
        off += a.size
    return out


def kernel(x, norm_ffn1, w_ffn1_in, w_ffn1_out, norm_mix, w_in, sinks, w_dw, b_dw, conv_ln_g, conv_ln_b, w_out, norm_ffn2, w_ffn2_in, w_ffn2_out, final_norm, loss_target, m_norm_ffn1, m_w_ffn1_in, m_w_ffn1_out, m_norm_mix, m_w_in, m_sinks, m_w_dw, m_b_dw, m_conv_ln_g, m_conv_ln_b, m_w_out, m_norm_ffn2, m_w_ffn2_in, m_w_ffn2_out, m_final_norm, v_norm_ffn1, v_w_ffn1_in, v_w_ffn1_out, v_norm_mix, v_w_in, v_sinks, v_w_dw, v_b_dw, v_conv_ln_g, v_conv_ln_b, v_w_out, v_norm_ffn2, v_w_ffn2_in, v_w_ffn2_out, v_final_norm):
    L, D = norm_ffn1.shape
    T = x.shape[1]
    FB = w_ffn1_in.shape[2]
    CH = b_dw.shape[1]
    QKV = ATTN_W + 2 * KV_W
    xs = x.reshape(T, D)
    tgt = loss_target.reshape(T, D)
    cx, cy, cc = lax.axis_index("x"), lax.axis_index("y"), lax.axis_index("c")
    chip = 2 * cx + cy
    cidx = cc.reshape(1).astype(jnp.int32)
    big_w = (w_ffn1_in, w_ffn1_out, w_in, w_out, w_ffn2_in, w_ffn2_out)
    big_m = (m_w_ffn1_in, m_w_ffn1_out, m_w_in, m_w_out, m_w_ffn2_in, m_w_ffn2_out)
    big_v = (v_w_ffn1_in, v_w_ffn1_out, v_w_in, v_w_out, v_w_ffn2_in, v_w_ffn2_out)
    NW = len(big_w) + 1

    def shards(l, tok):
        return [(w_[l] + tok[0, 0]).astype(BF16) for w_ in big_w] + [w_dw[l] + tok[0, 0]]

    def own_slot(a, slots=4, idx=chip):
        return lax.dynamic_update_index_in_dim(lax.empty((slots,) + a.shape, a.dtype), a, idx, 0)

    def gather_start(srcs, tok):
        return _xchg_start("gather_start", srcs, [own_slot(s_) for s_ in srcs], _gather_plan, tok)

    def gather_arrived(started, after, n, taps):
        _, lands, tok = _xchg_wait("gather_wait", started, n, n, _gather_plan, after)
        return _xchg_start("gshare_start", [], lands[:-1] if taps else lands, _gshare_plan, tok, "sibling3"), lands[-1]

    def shared_weights(shared, after, n):
        _, mats, tok = _xchg_wait("gshare_wait", shared, 0, n, _gshare_plan, after, "sibling3")
        return mats, tok

    row = lambda a, l: a[l].reshape(1, -1)
    tab = _attn_bias_table()
    NB = len(big_w)

    saved, W = [], []
    zero_tok = jnp.zeros((8, LANES), F32)
    src0 = shards(0, zero_tok)
    started = gather_start(src0[:2], zero_tok)
    rest0 = gather_start(src0[2:], started[-1])
    cast = [None] + [shards(l, rest0[-1]) for l in range(1, L)]
    shared, _ = gather_arrived(started, [xs] + [a_ for c_ in cast[1:] for a_ in c_], 2, False)
    after = [shared[-1]]
    for l in range(L):
        mats, tok = shared_weights(shared, after, 2 if l == 0 else NB)
        started = None
        if l + 1 < L:
            started = gather_start(cast[l + 1], tok)
            tok = started[-1]
        x0 = xs
        x1, gu1 = _ffn_fwd(x0, row(norm_ffn1, l) + tok[0, 0], mats[0], mats[1].reshape(2 * FB, D))
        gm_row = row(norm_mix, l)
        if l == 0:
            shared, gdw = gather_arrived(rest0, [x1], NW - 2, True)
            rest, tok = shared_weights(shared, [shared[-1]], NB - 2)
            mats = list(mats) + list(rest)
            gm_row = gm_row + tok[0, 0]
        g1i, g1o, gi, go, g2i, g2o = mats
        w = dict(f1i=g1i, f1o=g1o.reshape(2 * FB, D), f2i=g2i, f2o=g2o.reshape(2 * FB, D),
                 wi=jnp.transpose(gi, (1, 0, 2)).reshape(D, -1), wo=go.reshape(-1, D),
                 wdw=jnp.transpose(gdw, (1, 0, 2)).reshape(CONV_W, CH))
        W.append(w)
        qkv, u = _mixproj_fwd(x1, gm_row, w["wi"])
        attn = _attn_fwd(row(sinks, l), tab, qkv)
        conv, ypre = _conv_fwd(u, w["wdw"], row(b_dw, l), row(conv_ln_g, l), row(conv_ln_b, l))
        x2 = _mixout_fwd(x1, attn, conv, w["wo"])
        g2_row = row(norm_ffn2, l)
        if started is not None:
            shared, gdw = gather_arrived(started, [x2], NW, True)
            g2_row = g2_row + shared[-1][0, 0]
        xs, gu2 = _ffn_fwd(x2, g2_row, w["f2i"], w["f2o"])
        saved.append((x0, gu1, x1, qkv, u, attn, conv, ypre, x2, gu2))
        after = [xs]

    loss_part, dx, d_final = _loss_head(xs, final_norm.reshape(1, D), tgt)
    loss = lax.psum(loss_part[0, 0], ("x", "y", "c"))

    bufs = [[lax.empty(w_.shape, F32) for _ in range(4)] for w_ in big_w]
    d_n1, d_nm, d_n2 = [None] * L, [None] * L, [None] * L
    d_sk, d_bdw, d_lg, d_lb, d_wdw = [None] * L, [None] * L, [None] * L, [None] * L, [None] * L

    def sib_start(gs):
        return _xchg_start("sib_start", gs, [lax.empty((4, g.shape[1] // 2, g.shape[2]), F32) for g in gs],
                           _sib_plan, zero_tok, "sibling")

    def reduce_start(sib_started, after, n):
        gs, sibs, _ = _xchg_wait("sib_wait", sib_started, n, n, _sib_plan, after, "sibling")
        parts = [_sum_halves(cidx, g, s_) for g, s_ in zip(gs, sibs)]
        lands = [own_slot(lax.dynamic_index_in_dim(p, chip, 0, keepdims=False)) for p in parts]
        return _xchg_start("rs_start", parts, lands, _rs_plan, zero_tok)

    def share_start(rs_started, after, n):
        _, qs, tok = _xchg_wait("rs_wait", rs_started, n, n, _rs_plan, after)
        return _xchg_start("qshare_start", qs, [lax.empty(q.shape, q.dtype) for q in qs], _whole_plan, tok, "sibling")

    def finish(l, shared, after, idxs):
        q_own, q_sib, _ = _xchg_wait("qshare_wait", shared, len(idxs), len(idxs), _whole_plan, after, "sibling")
        for k, t in enumerate(idxs):
            bufs[t] = _adamw_layer(cidx, q_own[k], q_sib[k], big_w[t], big_m[t], big_v[t], bufs[t], l)

    ALL = list(range(NB))
    EARLY, LATE = ALL[2:], ALL[:2]
    sib_pending = rs_pending = None
    shares = []
    tok = zero_tok
    for l in reversed(range(L)):
        w = W[l]
        x0, gu1, x1, qkv, u, attn, conv, ypre, x2, gu2 = saved[l]
        dx, d_n2[l], hb, dgu, a, dyb = _ffn_bwd(dx, x2, row(norm_ffn2, l), gu2, w["f2i"], w["f2o"], tok)
        g_f2i, g_f2o = _wgrad_ffn_in(hb, dgu, tok), _wgrad_ffn_out(a, dyb, tok)
        lg_row = row(conv_ln_g, l)
        if sib_pending is not None:
            rs_started = reduce_start(sib_pending[1], [g_f2o], NB)
            if rs_pending is not None:
                shares.append((rs_pending[0], share_start(rs_pending[1], [rs_started[-1]], NB)))
            rs_pending = (sib_pending[0], rs_started)
            lg_row = lg_row + rs_started[-1][0, 0]
        dyb, dattn, dconv = _mixout_bwd(dx, w["wo"])
        g_wo = _wgrad_cat([attn, conv], [dyb]).reshape(4, -1, D)
        du, dwdw, dvec = _conv_bwd(dconv, ypre, u, w["wdw"], lg_row, row(conv_ln_b, l))
        d_wdw[l], d_bdw[l], d_lg[l], d_lb[l] = dwdw[:CONV_W], dvec[0], dvec[1], dvec[2]
        dq, dkv, dsk = _attn_bwd(row(sinks, l), tab, qkv, dattn)
        d_sk[l] = dsk[:, 0]
        wi = w["wi"]
        dx, d_nm[l], hb = _mix_rms_bwd(dx, x1, row(norm_mix, l), [dq, dkv, du],
                                       [wi[:, :ATTN_W], wi[:, ATTN_W:QKV], wi[:, QKV:]])
        g_wi = jnp.transpose(_wgrad_cat([hb], [dq, dkv, du]).reshape(D, 4, -1), (1, 0, 2))
        if l == 0:
            sib_early = sib_start([g_wi, g_wo, g_f2i, g_f2o])
            tok = sib_early[-1]
        dx, d_n1[l], hb, dgu, a, dyb = _ffn_bwd(dx, x0, row(norm_ffn1, l), gu1, w["f1i"], w["f1o"], tok)
        if l == 0:
            rs_early = reduce_start(sib_early, [dx], len(EARLY))
            tok = rs_early[-1]
        g_f1i, g_f1o = _wgrad_ffn_in(hb, dgu, tok), _wgrad_ffn_out(a, dyb, tok)
        sib_started = sib_start([g_f1i, g_f1o] if l == 0 else [g_f1i, g_f1o, g_wi, g_wo, g_f2i, g_f2o])
        tok = sib_started[-1]
        sib_pending = (l, sib_started)
    grad_x = dx.reshape(x.shape)

    small_g = [jnp.concatenate(d, axis=0) for d in (d_n1, d_nm, d_n2)] + [d_final, jnp.stack(d_sk)] + \
              [jnp.stack(d) for d in (d_bdw, d_lg, d_lb, d_wdw)]
    packed = _pack(small_g)[0]
    small_started = _xchg_start("small_start", [packed], [own_slot(packed, 8, 4 * cx + 2 * cy + cc)], _slot_plan, tok, "all")

    after = [small_started[-1]]
    if rs_pending is not None:
        shares.append((rs_pending[0], share_start(rs_pending[1], after, NB)))
        after = [shares[-1][1][-1]]
    if shares:
        finish(*shares.pop(0), after, ALL)
        after = [b_[0] for b_ in bufs]
    rs_late = reduce_start(sib_pending[1], after, len(LATE))
    after = [rs_late[-1]]
    for l, sh in shares:
        finish(l, sh, after, ALL)
        after = [b_[0] for b_ in bufs]
    _, (slots,), _ = _xchg_wait("small_wait", small_started, 1, 1, _slot_plan, after, "all")
    small_sum = _unpack(_sum_slots(slots), small_g)
    g_wdw = lax.dynamic_slice_in_dim(small_sum[8], chip * w_dw.shape[2], w_dw.shape[2], axis=2)
    small_g = [small_sum[0], small_sum[1], small_sum[2], small_sum[3].reshape(D), small_sum[4],
               small_sum[5], small_sum[6], small_sum[7], g_wdw]
    small_w = (norm_ffn1, norm_mix, norm_ffn2, final_norm, sinks, b_dw, conv_ln_g, conv_ln_b, w_dw)
    small_m = (m_norm_ffn1, m_norm_mix, m_norm_ffn2, m_final_norm, m_sinks, m_b_dw, m_conv_ln_g, m_conv_ln_b, m_w_dw)
    small_v = (v_norm_ffn1, v_norm_mix, v_norm_ffn2, v_final_norm, v_sinks, v_b_dw, v_conv_ln_g, v_conv_ln_b, v_w_dw)
    upd = _adamw(_pack(small_g), _pack(small_w), _pack(small_m), _pack(small_v))
    small_upd = [_unpack(u_, small_w) for u_ in upd]
    sh_early = share_start(rs_early, [upd[0]], len(EARLY))
    sh_late = share_start(rs_late, [sh_early[-1]], len(LATE))
    finish(0, sh_early, [sh_late[-1]], EARLY)
    finish(0, sh_late, [bufs[t][0] for t in EARLY], LATE)

    order = ("norm_ffn1", "w_ffn1_in", "w_ffn1_out", "norm_mix", "w_in", "sinks", "w_dw", "b_dw", "conv_ln_g",
             "conv_ln_b", "w_out", "norm_ffn2", "w_ffn2_in", "w_ffn2_out", "final_norm")
    small_names = ("norm_ffn1", "norm_mix", "norm_ffn2", "final_norm", "sinks", "b_dw", "conv_ln_g", "conv_ln_b", "w_dw")
    big_names = ("w_ffn1_in", "w_ffn1_out", "w_in", "w_out", "w_ffn2_in", "w_ffn2_out")
    grads, deltas, new_m, new_v = {}, {}, {}, {}
    for i, nme in enumerate(small_names):
        grads[nme], deltas[nme], new_m[nme], new_v[nme] = small_g[i], small_upd[0][i], small_upd[1][i], small_upd[2][i]
    for i, nme in enumerate(big_names):
        grads[nme], deltas[nme], new_m[nme], new_v[nme] = bufs[i]
    return (loss, grad_x, *[grads[n] for n in order], *[deltas[n] for n in order],
            *[new_m[n] for n in order], *[new_v[n] for n in order])
```

```python
import functools

import jax
import jax.numpy as jnp
from jax import lax
from jax.experimental import pallas as pl
from jax.experimental.pallas import tpu as pltpu

F32, BF16 = jnp.float32, jnp.bfloat16
EPS = 1e-6
NEG_INF = -1e30
HEAD_DIM = 64
N_HEADS = 8
N_KV = 2
GROUP = N_HEADS // N_KV
WINDOW = 128
ATTN_W = N_HEADS * HEAD_DIM
KV_W = N_KV * HEAD_DIM
CONV_W = 31
HALO = 32
CONV_ROWS = 32
SCALE = 1.0 / 8.0
ADAM_LR, ADAM_B1, ADAM_B2, ADAM_EPS, ADAM_WD, ADAM_STEP = 0.001, 0.9, 0.999, 1e-08, 0.01, 10
TM = 512
TM_FFN_BWD = 256
TK_WGRAD = 2048
LANES = 128
VMEM_LIMIT = 52 * 1024 * 1024
MESH = pl.DeviceIdType.MESH
ANY = pl.BlockSpec(memory_space=pl.ANY)
HBM = pl.BlockSpec(memory_space=pltpu.HBM)
SEM = pl.BlockSpec(memory_space=pltpu.SEMAPHORE)
VMEM = pl.BlockSpec(memory_space=pltpu.VMEM)
EFFECT = pltpu.SideEffectType.DATAFLOW_SIDE_EFFECTING
TOKEN = jax.ShapeDtypeStruct((8, LANES), F32)


def _cp(n):
    return pltpu.CompilerParams(dimension_semantics=("arbitrary",) * n, vmem_limit_bytes=VMEM_LIMIT)


def _dot(a, b):
    return jnp.dot(a, b, preferred_element_type=F32)


def _dot_nt(a, b):
    return lax.dot_general(a, b, (((1,), (1,)), ((), ())), preferred_element_type=F32)


def _dot_tn(a, b):
    return lax.dot_general(a, b, (((0,), (0,)), ((), ())), preferred_element_type=F32)


def _sigmoid(v):
    return 1.0 / (1.0 + jnp.exp(-v))


def _place():
    x, y, c = lax.axis_index("x"), lax.axis_index("y"), lax.axis_index("c")
    chips = [(1 - x, y), (x, 1 - y), (1 - x, 1 - y)]
    return x, y, c, chips


def _rcopy(src, dst, send_sems, recv_sems, k, dev):
    return pltpu.make_async_remote_copy(src_ref=src, dst_ref=dst, send_sem=send_sems.at[k],
                                        recv_sem=recv_sems.at[k], device_id=dev, device_id_type=MESH)


def _hbm(a):
    return pltpu.with_memory_space_constraint(a, pltpu.HBM)


PEERS = {"chips": 3, "sibling": 1, "sibling3": 3, "all": 7}


def _targets(mode):
    x, y, c, chips = _place()
    b = 2 * x + y
    if mode == "chips":
        return b, c, [((px, py, c), 2 * px + py) for px, py in chips]
    if mode == "sibling":
        return b, c, [((x, y, 1 - c), b)]
    if mode == "sibling3":
        return b, c, [((x, y, 1 - c), 2 * px + py) for px, py in chips]
    flip = lambda v, f: 1 - v if f else v
    devs = [(flip(x, k >> 2 & 1), flip(y, k >> 1 & 1), flip(c, k & 1)) for k in range(1, 8)]
    return 4 * x + 2 * y + c, c, [(d, 4 * d[0] + 2 * d[1] + d[2]) for d in devs]


def _xchg_start(name, srcs, lands, plan, dep, mode="chips"):
    ns, nl, npeer = len(srcs), len(lands), PEERS[mode]

    def body(*refs):
        land = refs[ns:ns + nl]
        src = refs[:ns] if ns else land
        send_sems, recv_sems, token = refs[ns + nl + 1], refs[ns + nl + 2], refs[-1]
        me, c, peers = _targets(mode)
        for t in range(nl):
            for j, (dev, tag) in enumerate(peers):
                s, d, _ = plan(src[t], land[t], t, me, c, tag)
                _rcopy(s, d, send_sems, recv_sems, npeer * t + j, dev).start()
        token[...] = jnp.zeros_like(token)

    arrs = list(srcs) + list(lands)
    return pl.pallas_call(
        body, name=name,
        out_shape=(pltpu.SemaphoreType.DMA((npeer * nl,)), pltpu.SemaphoreType.DMA((npeer * nl,)),
                   *[pltpu.HBM(a.shape, a.dtype) for a in arrs], TOKEN),
        in_specs=[HBM] * (ns + nl) + [ANY], out_specs=(SEM, SEM, *[HBM] * (ns + nl), VMEM),
        input_output_aliases={i: 2 + i for i in range(ns + nl)},
        compiler_params=pltpu.CompilerParams(has_side_effects=EFFECT),
    )(*[_hbm(a) for a in arrs], dep)


def _xchg_wait(name, started, ns, nl, plan, after, mode="chips"):
    send_sems, recv_sems, thru = started[0], started[1], started[2:2 + ns + nl]
    npeer = PEERS[mode]

    def body(*refs):
        land = refs[ns:ns + nl]
        src = refs[:ns] if ns else land
        send_sems, recv_sems, token = refs[ns + nl], refs[ns + nl + 1], refs[-1]
        me, c, peers = _targets(mode)
        for t in range(nl):
            for j, (dev, tag) in enumerate(peers):
                s, _, a = plan(src[t], land[t], t, me, c, tag)
                cp = _rcopy(s, a, send_sems, recv_sems, npeer * t + j, dev)
                cp.wait_send()
                cp.wait_recv()
        token[...] = jnp.zeros_like(token)

    out = pl.pallas_call(
        body, name=name,
        out_shape=(*[pltpu.HBM(a.shape, a.dtype) for a in thru], TOKEN),
        in_specs=[HBM] * (ns + nl) + [SEM, SEM] + [ANY] * len(after), out_specs=(*[HBM] * (ns + nl), VMEM),
        input_output_aliases={i: i for i in range(ns + nl)},
        compiler_params=pltpu.CompilerParams(has_side_effects=EFFECT),
    )(*thru, send_sems, recv_sems, *after)
    return out[:ns], out[ns:ns + nl], out[-1]


def _half(ref_rows, which):
    h = ref_rows // 2
    return pl.ds(which * h, h)


def _gather_plan(src, land, t, b, c, pb):
    if len(src.shape) == 2 and src.shape[0] % 2 == 0:
        hs = _half(src.shape[0], c)
        return src.at[hs], land.at[b, hs], land.at[pb, hs]
    return src, land.at[b], land.at[pb]


def _gshare_plan(src, land, t, b, c, pb):
    return land.at[pb, _half(land.shape[1], c)], land.at[pb, _half(land.shape[1], c)], land.at[pb, _half(land.shape[1], 1 - c)]


def _rs_plan(src, land, t, b, c, pb):
    return src.at[pb], land.at[b], land.at[pb]


def _sib_plan(src, land, t, b, c, pb):
    return src.at[:, _half(src.shape[1], 1 - c), :], land, land


def _rows_block(h, cap=512):
    for rb in range(min(h, cap) // 16 * 16, 0, -16):
        if h % rb == 0:
            return rb
    return h


def _sum_halves(cidx, g, s):
    _, R, C = g.shape
    rb = _rows_block(R // 2)
    nr = R // 2 // rb

    def body(c_ref, g_ref, s_ref, o_ref):
        o_ref[...] = (g_ref[...] + s_ref[...]).astype(BF16)

    blk = (None, rb, C)
    return pl.pallas_call(
        body, name="sum_halves", out_shape=jax.ShapeDtypeStruct(s.shape, BF16),
        grid_spec=pltpu.PrefetchScalarGridSpec(
            num_scalar_prefetch=1, grid=(4, nr),
            in_specs=[pl.BlockSpec(blk, lambda p, i, c: (p, c[0] * nr + i, 0)),
                      pl.BlockSpec(blk, lambda p, i, c: (p, i, 0))],
            out_specs=pl.BlockSpec(blk, lambda p, i, c: (p, i, 0))),
        compiler_params=_cp(2),
    )(cidx, g, s)


def _whole_plan(src, land, t, me, c, tag):
    return src, land, land


def _slot_plan(src, land, t, me, c, tag):
    return src, land.at[me], land.at[tag]


def _adam_update(gg, w, m, v):
    m2 = ADAM_B1 * m + (1.0 - ADAM_B1) * gg
    v2 = ADAM_B2 * v + (1.0 - ADAM_B2) * (gg * gg)
    mh = m2 / (1.0 - ADAM_B1 ** ADAM_STEP)
    vh = v2 / (1.0 - ADAM_B2 ** ADAM_STEP)
    return -ADAM_LR * (mh / (jnp.sqrt(vh) + ADAM_EPS) + ADAM_WD * w), m2, v2


def _adamw_layer(cidx, q_own, q_sib, w, m, v, bufs, l):
    L, R, C = w.shape
    h = R // 2
    rb = _rows_block(h, 256)
    nr = h // rb

    def body(c_ref, qo_ref, qs_ref, w_ref, m_ref, v_ref, *rest):
        g_ref, d_ref, mo_ref, vo_ref = rest[-4:]
        own = pl.program_id(0) == c_ref[0]
        gg = jnp.zeros((rb, C), F32)
        for s in range(4):
            gg = gg + jnp.where(own, qo_ref[s], qs_ref[s]).astype(F32)
        g_ref[...] = gg
        d_ref[...], mo_ref[...], vo_ref[...] = _adam_update(gg, w_ref[...], m_ref[...], v_ref[...])

    q_own_spec = pl.BlockSpec((4, rb, C), lambda hh, i, c: (0, jnp.where(hh == c[0], i, 0), 0))
    q_sib_spec = pl.BlockSpec((4, rb, C), lambda hh, i, c: (0, jnp.where(hh == c[0], 0, i), 0))
    wspec = pl.BlockSpec((None, rb, C), lambda hh, i, c: (l, hh * nr + i, 0))
    return pl.pallas_call(
        body, name="adamw_layer", out_shape=[jax.ShapeDtypeStruct(w.shape, F32)] * 4,
        grid_spec=pltpu.PrefetchScalarGridSpec(
            num_scalar_prefetch=1, grid=(2, nr),
            in_specs=[q_own_spec, q_sib_spec, wspec, wspec, wspec] + [ANY] * 4, out_specs=[wspec] * 4),
        input_output_aliases={6 + k: k for k in range(4)},
        compiler_params=_cp(2),
    )(cidx, q_own, q_sib, w, m, v, *bufs)


def _adamw(g, w, m, v):
    L, R, C = g.shape
    rb = _rows_block(R)

    def body(g_ref, w_ref, m_ref, v_ref, d_ref, mo_ref, vo_ref):
        d_ref[...], mo_ref[...], vo_ref[...] = _adam_update(g_ref[...], w_ref[...], m_ref[...], v_ref[...])

    spec = pl.BlockSpec((None, rb, C), lambda l, i: (l, i, 0))
    return pl.pallas_call(
        body, name="adamw", grid=(L, R // rb), in_specs=[spec] * 4, out_specs=[spec] * 3,
        out_shape=[jax.ShapeDtypeStruct(g.shape, F32)] * 3, compiler_params=_cp(2),
    )(g, w, m, v)


def _sum_slots(buf):
    def body(b_ref, o_ref):
        acc = b_ref[0]
        for k in range(1, 8):
            acc = acc + b_ref[k]
        o_ref[...] = acc

    return pl.pallas_call(body, name="sum_slots", in_specs=[VMEM], out_specs=VMEM,
                          out_shape=jax.ShapeDtypeStruct(buf.shape[1:], F32))(buf)


def _rms(xf, g):
    r = lax.rsqrt(jnp.mean(xf * xf, axis=-1, keepdims=True) + EPS)
    return xf * r, r


def _lane_chunks(n):
    lo = (n // LANES + 1) // 2 * LANES
    return ((0, lo), (lo, n - lo))


def _load_ffn_weights(win_hbm, wout_hbm, win_v, wout_v, sems):
    fb = win_v.shape[2]
    loads = [pltpu.make_async_copy(win_hbm.at[k], win_v.at[k], sems.at[k]) for k in range(4)]
    loads += [pltpu.make_async_copy(wout_hbm.at[pl.ds(k * fb, fb)], wout_v.at[pl.ds(k * fb, fb)], sems.at[4 + k])
              for k in range(2)]
    for cp in loads:
        cp.start()
    for cp in loads:
        cp.wait()


def _fast_sigmoid(v):
    return pl.reciprocal(1.0 + jnp.exp(-v), approx=True)


def _ffn_fwd(x, g, win, wout):
    T, D = x.shape
    FB = win.shape[2]
    tm = min(TM, T)

    def body(x_ref, g_ref, win_hbm, wout_hbm, xo_ref, gu_ref, win_v, wout_v, sems):
        @pl.when(pl.program_id(0) == 0)
        def _():
            _load_ffn_weights(win_hbm, wout_hbm, win_v, wout_v, sems)

        xf = x_ref[...]
        xh, _ = _rms(xf, None)
        h = (xh * g_ref[...]).astype(BF16)
        acc = jnp.zeros((tm, D), F32)
        for blk in range(2):
            for lo, sz in _lane_chunks(FB):
                cols = pl.ds(blk * FB + lo, sz)
                gate = _dot(h, win_v[blk, :, pl.ds(lo, sz)])
                up = _dot(h, win_v[2 + blk, :, pl.ds(lo, sz)])
                gu_ref[0, :, cols] = gate.astype(BF16)
                gu_ref[1, :, cols] = up.astype(BF16)
                a = (gate * _fast_sigmoid(gate) * up).astype(BF16)
                acc = acc + _dot(a, wout_v[cols, :])
        xo_ref[...] = xf + 0.5 * acc

    row = pl.BlockSpec((tm, D), lambda i: (i, 0))
    return pl.pallas_call(
        body, name="ffn_fwd", grid=(T // tm,),
        in_specs=[row, pl.BlockSpec((1, D), lambda i: (0, 0)), ANY, ANY],
        out_specs=[row, pl.BlockSpec((2, tm, 2 * FB), lambda i: (0, i, 0))],
        out_shape=[jax.ShapeDtypeStruct((T, D), F32), jax.ShapeDtypeStruct((2, T, 2 * FB), BF16)],
        scratch_shapes=[pltpu.VMEM(win.shape, BF16), pltpu.VMEM(wout.shape, BF16), pltpu.SemaphoreType.DMA((6,))],
        compiler_params=_cp(1),
    )(x, g, win, wout)


def _mixproj_fwd(x, g, wt):
    T, D = x.shape
    W = wt.shape[0]
    QKV = ATTN_W + 2 * KV_W
    tm = min(TM, T)

    def body(x_ref, g_ref, w_ref, qkv_ref, u_ref):
        xh, _ = _rms(x_ref[...], None)
        h = (xh * g_ref[...]).astype(BF16)
        qkv_ref[...] = _dot_nt(h, w_ref[:QKV, :]).astype(BF16)
        u_ref[...] = _dot_nt(h, w_ref[QKV:, :])

    return pl.pallas_call(
        body, name="mixproj_fwd", grid=(T // tm,),
        in_specs=[pl.BlockSpec((tm, D), lambda i: (i, 0)), pl.BlockSpec((1, D), lambda i: (0, 0)),
                  pl.BlockSpec((W, D), lambda i: (0, 0))],
        out_specs=[pl.BlockSpec((tm, QKV), lambda i: (i, 0)), pl.BlockSpec((tm, W - QKV), lambda i: (i, 0))],
        out_shape=[jax.ShapeDtypeStruct((T, QKV), BF16), jax.ShapeDtypeStruct((T, W - QKV), F32)],
        compiler_params=_cp(1),
    )(x, g, wt)


def _attn_bias_table():
    rows, cols = GROUP * WINDOW, 2 * WINDOW
    row = lax.broadcasted_iota(jnp.int32, (N_KV, rows, cols), 1)
    col = lax.broadcasted_iota(jnp.int32, (N_KV, rows, cols), 2)
    head = GROUP * lax.broadcasted_iota(jnp.int32, (N_KV, rows, cols), 0) + (row >> 7)
    dist = (row & (WINDOW - 1)) + WINDOW - col
    slope = jnp.exp2(-(head + 1).astype(F32))
    return jnp.where((dist >= 0) & (dist < WINDOW), -slope * dist.astype(F32), NEG_INF)


def _first_block_mask(n):
    col = lax.broadcasted_iota(jnp.int32, (GROUP * WINDOW, 2 * WINDOW), 1)
    return (n > 0) | (col >= WINDOW)


def _sink_col(sink_ref, g):
    hi = lax.broadcasted_iota(jnp.int32, (GROUP * WINDOW, 1), 0) >> 7
    col = jnp.zeros((GROUP * WINDOW, 1), F32)
    for i in range(GROUP):
        col = jnp.where(hi == i, sink_ref[0, GROUP * g + i], col)
    return col


def _stack_heads(ref, g):
    return jnp.concatenate([ref[:, (GROUP * g + i) * HEAD_DIM:(GROUP * g + i + 1) * HEAD_DIM]
                            for i in range(GROUP)], axis=0)


def _band(kvp_ref, kvc_ref, off):
    return jnp.concatenate([kvp_ref[:, off:off + HEAD_DIM], kvc_ref[:, off:off + HEAD_DIM]], axis=0)


def _attn_probs(qs, k, bias, seen, sink):
    s = jnp.where(seen, _dot_nt(qs, k) * SCALE + bias, NEG_INF)
    m = jnp.maximum(jnp.max(s, axis=-1, keepdims=True), sink)
    p = jnp.exp(s - m)
    es = jnp.exp(sink - m)
    den = jnp.sum(p, axis=-1, keepdims=True) + es
    return p / den, es / den


def _attn_fwd(sinks, tab, qkv):
    T = qkv.shape[0]
    nb = T // WINDOW

    def body(sink_ref, tab_ref, q_ref, kvp_ref, kvc_ref, o_ref):
        seen = _first_block_mask(pl.program_id(0))
        for g in range(N_KV):
            qs = _stack_heads(q_ref, g)
            k = _band(kvp_ref, kvc_ref, g * HEAD_DIM)
            v = _band(kvp_ref, kvc_ref, KV_W + g * HEAD_DIM)
            p, _ = _attn_probs(qs, k, tab_ref[g], seen, _sink_col(sink_ref, g))
            o = _dot(p.astype(BF16), v)
            for i in range(GROUP):
                h = GROUP * g + i
                o_ref[:, h * HEAD_DIM:(h + 1) * HEAD_DIM] = o[i * WINDOW:(i + 1) * WINDOW].astype(BF16)

    return pl.pallas_call(
        body, name="attn_fwd", grid=(nb,),
        in_specs=[pl.BlockSpec(memory_space=pltpu.SMEM),
                  pl.BlockSpec(tab.shape, lambda n: (0, 0, 0)),
                  pl.BlockSpec((WINDOW, ATTN_W), lambda n: (n, 0)),
                  pl.BlockSpec((WINDOW, 2 * KV_W), lambda n: (jnp.maximum(n - 1, 0), 2)),
                  pl.BlockSpec((WINDOW, 2 * KV_W), lambda n: (n, 2))],
        out_specs=pl.BlockSpec((WINDOW, ATTN_W), lambda n: (n, 0)),
        out_shape=jax.ShapeDtypeStruct((T, ATTN_W), BF16),
        compiler_params=_cp(1),
    )(sinks, tab, qkv, qkv, qkv)


def _shift_copies(src_ref, dst_ref, n):
    for b in range(1, 8):
        dst_ref[b - 1] = src_ref[b:b + n, :]


def _tap(src_ref, sh_ref, s, c0):
    a, b = divmod(s, 8)
    start = pl.multiple_of(c0 + 8 * a, 8)
    if b == 0:
        return src_ref[pl.ds(start, CONV_ROWS), :]
    return sh_ref[b - 1, pl.ds(start, CONV_ROWS), :]


def _glu_rows(u, ch):
    return u[:, :ch] * _sigmoid(u[:, ch:])


def _fill_z(zs_ref, zsh_ref, uc_ref, up_ref, i, ch, n):
    zs_ref[0:HALO] = jnp.where(i > 0, _glu_rows(up_ref[...], ch), 0.0)
    zs_ref[HALO:] = _glu_rows(uc_ref[...], ch)
    _shift_copies(zs_ref, zsh_ref, n - 8)


def _conv_fwd(u, w, b, lg, lb):
    T = u.shape[0]
    CH = u.shape[1] // 2
    tm = min(TM, T)
    n = tm + HALO
    hb = tm // HALO

    def body(uc_ref, up_ref, w_ref, b_ref, lg_ref, lb_ref, conv_ref, ypre_ref, zs_ref, zsh_ref):
        i = pl.program_id(0)
        _fill_z(zs_ref, zsh_ref, uc_ref, up_ref, i, CH, n)
        bias = b_ref[...]

        def chunk(ci, carry):
            c0 = pl.multiple_of(ci * CONV_ROWS, CONV_ROWS)
            acc = jnp.broadcast_to(bias, (CONV_ROWS, CH))
            for k in range(CONV_W):
                acc = acc + w_ref[k:k + 1, :] * _tap(zs_ref, zsh_ref, HALO - (CONV_W - 1) + k, c0)
            ypre_ref[pl.ds(c0, CONV_ROWS), :] = acc
            return carry

        lax.fori_loop(0, tm // CONV_ROWS, chunk, 0)
        y = ypre_ref[...]
        mu = jnp.mean(y, axis=-1, keepdims=True)
        d = y - mu
        var = jnp.mean(d * d, axis=-1, keepdims=True)
        o = d * lax.rsqrt(var + EPS) * lg_ref[...] + lb_ref[...]
        conv_ref[...] = (o * _sigmoid(o)).astype(BF16)

    vec = pl.BlockSpec((1, CH), lambda i: (0, 0))
    return pl.pallas_call(
        body, name="conv_fwd", grid=(T // tm,),
        in_specs=[pl.BlockSpec((tm, 2 * CH), lambda i: (i, 0)),
                  pl.BlockSpec((HALO, 2 * CH), lambda i: (jnp.maximum(i * hb - 1, 0), 0)),
                  pl.BlockSpec((CONV_W, CH), lambda i: (0, 0)), vec, vec, vec],
        out_specs=[pl.BlockSpec((tm, CH), lambda i: (i, 0)), pl.BlockSpec((tm, CH), lambda i: (i, 0))],
        out_shape=[jax.ShapeDtypeStruct((T, CH), BF16), jax.ShapeDtypeStruct((T, CH), F32)],
        scratch_shapes=[pltpu.VMEM((n, CH), F32), pltpu.VMEM((7, n - 8, CH), F32)],
        compiler_params=_cp(1),
    )(u, u, w, b, lg, lb)


def _mixout_fwd(x, attn, conv, wo):
    T, D = x.shape
    tm = min(TM, T)
    A = attn.shape[1]

    def body(x_ref, a_ref, c_ref, w_ref, xo_ref):
        xo_ref[...] = x_ref[...] + _dot(a_ref[...], w_ref[:A, :]) + _dot(c_ref[...], w_ref[A:, :])

    return pl.pallas_call(
        body, name="mixout_fwd", grid=(T // tm,),
        in_specs=[pl.BlockSpec((tm, D), lambda i: (i, 0)), pl.BlockSpec((tm, A), lambda i: (i, 0)),
                  pl.BlockSpec((tm, conv.shape[1]), lambda i: (i, 0)), pl.BlockSpec(wo.shape, lambda i: (0, 0))],
        out_specs=pl.BlockSpec((tm, D), lambda i: (i, 0)),
        out_shape=jax.ShapeDtypeStruct((T, D), F32),
        compiler_params=_cp(1),
    )(x, attn, conv, wo)


def _rms_bwd_rows(dh, xf, g):
    xh, r = _rms(xf, None)
    dxn = dh * g
    dx = r * (dxn - xh * jnp.mean(dxn * xh, axis=-1, keepdims=True))
    return dx, jnp.sum(dh * xh, axis=0, keepdims=True), xh * g


def _loss_head(x, g, tgt):
    T, D = x.shape
    tm = min(TM, T)

    def body(x_ref, g_ref, t_ref, loss_ref, dx_ref, dg_ref):
        @pl.when(pl.program_id(0) == 0)
        def _():
            loss_ref[...] = jnp.zeros_like(loss_ref)
            dg_ref[...] = jnp.zeros_like(dg_ref)

        xf = x_ref[...]
        g = g_ref[...]
        xh, _ = _rms(xf, None)
        e = xh * g - t_ref[...]
        loss_ref[...] += 0.5 * jnp.sum(jnp.mean(e * e, axis=-1, keepdims=True), axis=0, keepdims=True)
        dx, dg, _ = _rms_bwd_rows(e * (1.0 / D), xf, g)
        dx_ref[...] = dx
        dg_ref[...] += dg

    return pl.pallas_call(
        body, name="loss_head", grid=(T // tm,),
        in_specs=[pl.BlockSpec((tm, D), lambda i: (i, 0)), pl.BlockSpec((1, D), lambda i: (0, 0)),
                  pl.BlockSpec((tm, D), lambda i: (i, 0))],
        out_specs=[pl.BlockSpec((1, 1), lambda i: (0, 0)), pl.BlockSpec((tm, D), lambda i: (i, 0)),
                   pl.BlockSpec((1, D), lambda i: (0, 0))],
        out_shape=[jax.ShapeDtypeStruct((1, 1), F32), jax.ShapeDtypeStruct((T, D), F32),
                   jax.ShapeDtypeStruct((1, D), F32)],
        compiler_params=_cp(1),
    )(x, g, tgt)


def _ffn_bwd(dxo, x, g, gu, win, wout, dep):
    T, D = x.shape
    FB = win.shape[2]
    tm = min(TM_FFN_BWD, T)

    def body(dxo_ref, x_ref, g_ref, gu_ref, win_hbm, wout_hbm, dep_ref,
             dxi_ref, dg_ref, hb_ref, dgu_ref, a_ref, dyb_ref, win_v, wout_v, sems):
        @pl.when(pl.program_id(0) == 0)
        def _():
            _load_ffn_weights(win_hbm, wout_hbm, win_v, wout_v, sems)
            dg_ref[...] = jnp.zeros_like(dg_ref)

        dyb = (0.5 * dxo_ref[...]).astype(BF16)
        dyb_ref[...] = dyb
        dh = jnp.zeros((tm, D), F32)
        for blk in range(2):
            for lo, sz in _lane_chunks(FB):
                cols = pl.ds(blk * FB + lo, sz)
                da = _dot_nt(dyb, wout_v[cols, :])
                gate = gu_ref[0, :, cols].astype(F32)
                up = gu_ref[1, :, cols].astype(F32)
                sg = _fast_sigmoid(gate)
                s = gate * sg
                a_ref[:, cols] = (s * up).astype(BF16)
                dgate = (da * up * (sg * (1.0 + gate * (1.0 - sg)))).astype(BF16)
                dup = (da * s).astype(BF16)
                dgu_ref[0, :, cols] = dgate
                dgu_ref[1, :, cols] = dup
                dh = dh + _dot_nt(dgate, win_v[blk, :, pl.ds(lo, sz)]) + _dot_nt(dup, win_v[2 + blk, :, pl.ds(lo, sz)])
        dx, dg, h = _rms_bwd_rows(dh, x_ref[...], g_ref[...])
        dxi_ref[...] = dxo_ref[...] + dx
        dg_ref[...] += dg
        hb_ref[...] = h.astype(BF16)

    row = pl.BlockSpec((tm, D), lambda i: (i, 0))
    act = pl.BlockSpec((2, tm, 2 * FB), lambda i: (0, i, 0))
    return pl.pallas_call(
        body, name="ffn_bwd", grid=(T // tm,),
        in_specs=[row, row, pl.BlockSpec((1, D), lambda i: (0, 0)), act, ANY, ANY, ANY],
        out_specs=[row, pl.BlockSpec((1, D), lambda i: (0, 0)), row, act,
                   pl.BlockSpec((tm, 2 * FB), lambda i: (i, 0)), row],
        out_shape=[jax.ShapeDtypeStruct((T, D), F32), jax.ShapeDtypeStruct((1, D), F32),
                   jax.ShapeDtypeStruct((T, D), BF16), jax.ShapeDtypeStruct((2, T, 2 * FB), BF16),
                   jax.ShapeDtypeStruct((T, 2 * FB), BF16), jax.ShapeDtypeStruct((T, D), BF16)],
        scratch_shapes=[pltpu.VMEM(win.shape, BF16), pltpu.VMEM(wout.shape, BF16), pltpu.SemaphoreType.DMA((6,))],
        compiler_params=_cp(1),
    )(dxo, x, g, gu, win, wout, dep)


def _mix_rms_bwd(dxo, x, g, dzs, wts):
    T, D = x.shape
    tm = min(TM, T)
    npair = len(dzs)

    def body(*refs):
        dxo_ref, x_ref, g_ref = refs[:3]
        dz_refs, w_refs = refs[3:3 + npair], refs[3 + npair:3 + 2 * npair]
        dxi_ref, dg_ref, hb_ref = refs[3 + 2 * npair:]

        @pl.when(pl.program_id(0) == 0)
        def _():
            dg_ref[...] = jnp.zeros_like(dg_ref)

        dh = jnp.zeros((tm, D), F32)
        for p in range(npair):
            dh = dh + _dot(dz_refs[p][...], w_refs[p][...])
        dx, dg, h = _rms_bwd_rows(dh, x_ref[...], g_ref[...])
        dxi_ref[...] = dxo_ref[...] + dx
        dg_ref[...] += dg
        hb_ref[...] = h.astype(BF16)

    row = pl.BlockSpec((tm, D), lambda i: (i, 0))
    return pl.pallas_call(
        body, name="mix_rms_bwd", grid=(T // tm,),
        in_specs=[row, row, pl.BlockSpec((1, D), lambda i: (0, 0))]
                 + [pl.BlockSpec((tm, dz.shape[1]), lambda i: (i, 0)) for dz in dzs]
                 + [pl.BlockSpec(w.shape, lambda i: (0, 0)) for w in wts],
        out_specs=[row, pl.BlockSpec((1, D), lambda i: (0, 0)), row],
        out_shape=[jax.ShapeDtypeStruct((T, D), F32), jax.ShapeDtypeStruct((1, D), F32),
                   jax.ShapeDtypeStruct((T, D), BF16)],
        compiler_params=_cp(1),
    )(dxo, x, g, *dzs, *wts)


def _wgrad(name, a, b, a_spec, b_spec, out_shape, out_spec, nblk, dep):
    T = a.shape[0]
    tk = min(TK_WGRAD, T)

    def body(a_ref, b_ref, dep_ref, o_ref):
        @pl.when(pl.program_id(1) == 0)
        def _():
            o_ref[...] = jnp.zeros_like(o_ref)

        o_ref[...] += _dot_tn(a_ref[...], b_ref[...]).reshape(o_ref.shape)

    return pl.pallas_call(
        body, name=name, grid=(nblk, T // tk), in_specs=[a_spec, b_spec, ANY], out_specs=out_spec,
        out_shape=jax.ShapeDtypeStruct(out_shape, F32), compiler_params=_cp(2),
    )(a, b, dep)


def _wgrad_ffn_in(hb, dgu, dep):
    T, D = hb.shape
    FB = dgu.shape[2] // 2
    tk = min(TK_WGRAD, T)
    return _wgrad("wgrad_ffn_in", hb, dgu,
                  pl.BlockSpec((tk, D), lambda b, k: (k, 0)),
                  pl.BlockSpec((None, tk, FB), lambda b, k: (b // 2, k, b % 2)),
                  (4, D, FB), pl.BlockSpec((None, D, FB), lambda b, k: (b, 0, 0)), 4, dep)


def _wgrad_ffn_out(a, dyb, dep):
    T, D = dyb.shape
    FB = a.shape[1] // 2
    tk = min(TK_WGRAD, T)
    return _wgrad("wgrad_ffn_out", a, dyb,
                  pl.BlockSpec((tk, FB), lambda b, k: (k, b)),
                  pl.BlockSpec((tk, D), lambda b, k: (k, 0)),
                  (4, FB // 2, D), pl.BlockSpec((2, FB // 2, D), lambda b, k: (b, 0, 0)), 2, dep)


def _wgrad_cat(a_list, b_list):
    T = a_list[0].shape[0]
    tk = min(TK_WGRAD, T)
    na = len(a_list)
    M, N = sum(a.shape[1] for a in a_list), sum(b.shape[1] for b in b_list)

    def body(*refs):
        a_refs, b_refs, o_ref = refs[:na], refs[na:-1], refs[-1]

        @pl.when(pl.program_id(0) == 0)
        def _():
            o_ref[...] = jnp.zeros_like(o_ref)

        r0 = 0
        for a_ref in a_refs:
            c0 = 0
            for b_ref in b_refs:
                m, n = a_ref.shape[1], b_ref.shape[1]
                o_ref[r0:r0 + m, c0:c0 + n] += _dot_tn(a_ref[...], b_ref[...])
                c0 += n
            r0 += a_ref.shape[1]

    return pl.pallas_call(
        body, name="wgrad_cat", grid=(T // tk,),
        in_specs=[pl.BlockSpec((tk, v.shape[1]), lambda k: (k, 0)) for v in list(a_list) + list(b_list)],
        out_specs=pl.BlockSpec((M, N), lambda k: (0, 0)),
        out_shape=jax.ShapeDtypeStruct((M, N), F32), compiler_params=_cp(1),
    )(*a_list, *b_list)


def _mixout_bwd(dxo, wo):
    T, D = dxo.shape
    tm = min(TM, T)
    A = ATTN_W
    C = wo.shape[0] - A

    def body(dxo_ref, w_ref, dyb_ref, da_ref, dc_ref):
        dyb = dxo_ref[...].astype(BF16)
        dyb_ref[...] = dyb
        da_ref[...] = _dot_nt(dyb, w_ref[:A, :]).astype(BF16)
        dc_ref[...] = _dot_nt(dyb, w_ref[A:, :])

    return pl.pallas_call(
        body, name="mixout_bwd", grid=(T // tm,),
        in_specs=[pl.BlockSpec((tm, D), lambda i: (i, 0)), pl.BlockSpec(wo.shape, lambda i: (0, 0))],
        out_specs=[pl.BlockSpec((tm, D), lambda i: (i, 0)), pl.BlockSpec((tm, A), lambda i: (i, 0)),
                   pl.BlockSpec((tm, C), lambda i: (i, 0))],
        out_shape=[jax.ShapeDtypeStruct((T, D), BF16), jax.ShapeDtypeStruct((T, A), BF16),
                   jax.ShapeDtypeStruct((T, C), F32)],
        compiler_params=_cp(1),
    )(dxo, wo)


def _conv_bwd(dconv, ypre, u, w, lg, lb):
    T, CH = dconv.shape
    tm = min(TM, T)
    n = tm + HALO
    hb = tm // HALO
    nt = T // tm
    nchunk = tm // CONV_ROWS

    def body(dc_ref, dcn_ref, yp_ref, ypn_ref, uc_ref, up_ref, w_ref, lg_ref, lb_ref,
             du_ref, dw_ref, dvec_ref, zs_ref, zsh_ref, dy_ref, dysh_ref, dz_ref, dwacc_ref):
        i = pl.program_id(0)

        @pl.when(i == 0)
        def _():
            dwacc_ref[...] = jnp.zeros_like(dwacc_ref)
            dvec_ref[...] = jnp.zeros_like(dvec_ref)

        g, bb = lg_ref[...], lb_ref[...]

        def ln_bwd(dc, yp):
            mu = jnp.mean(yp, axis=-1, keepdims=True)
            d = yp - mu
            rs = lax.rsqrt(jnp.mean(d * d, axis=-1, keepdims=True) + EPS)
            yn = d * rs
            o = yn * g + bb
            sg = _sigmoid(o)
            do = dc * (sg * (1.0 + o * (1.0 - sg)))
            dyn = do * g
            dyp = rs * (dyn - jnp.mean(dyn, axis=-1, keepdims=True)
                        - yn * jnp.mean(dyn * yn, axis=-1, keepdims=True))
            return dyp, do, yn

        dyp, do, yn = ln_bwd(dc_ref[...], yp_ref[...])
        dvec_ref[0:1, :] += jnp.sum(dyp, axis=0, keepdims=True)
        dvec_ref[1:2, :] += jnp.sum(do * yn, axis=0, keepdims=True)
        dvec_ref[2:3, :] += jnp.sum(do, axis=0, keepdims=True)
        dy_ref[0:tm] = dyp
        dyh, _, _ = ln_bwd(dcn_ref[...], ypn_ref[...])
        dy_ref[tm:] = jnp.where(i < nt - 1, dyh, 0.0)
        _shift_copies(dy_ref, dysh_ref, n - 8)
        _fill_z(zs_ref, zsh_ref, uc_ref, up_ref, i, CH, n)

        def chunk(ci, carry):
            c0 = pl.multiple_of(ci * CONV_ROWS, CONV_ROWS)
            acc = jnp.zeros((CONV_ROWS, CH), F32)
            for k in range(CONV_W):
                acc = acc + w_ref[k:k + 1, :] * _tap(dy_ref, dysh_ref, CONV_W - 1 - k, c0)
            dz_ref[pl.ds(c0, CONV_ROWS), :] = acc
            dyc = dy_ref[pl.ds(c0, CONV_ROWS), :]
            for k in range(CONV_W):
                prod = dyc * _tap(zs_ref, zsh_ref, HALO - (CONV_W - 1) + k, c0)
                dwacc_ref[k] += jnp.sum(prod.reshape(CONV_ROWS // 8, 8, CH), axis=0)
            return carry

        lax.fori_loop(0, nchunk, chunk, 0)

        @pl.when(i == nt - 1)
        def _():
            dw_ref[...] = jnp.sum(dwacc_ref[...], axis=1)

        uc = uc_ref[...]
        a = uc[:, :CH]
        sg = _sigmoid(uc[:, CH:])
        dz = dz_ref[...]
        du_ref[:, :CH] = (dz * sg).astype(BF16)
        du_ref[:, CH:] = (dz * a * sg * (1.0 - sg)).astype(BF16)

    cur = lambda c: pl.BlockSpec((tm, c), lambda i: (i, 0))
    nxt = lambda c: pl.BlockSpec((HALO, c), lambda i: (jnp.minimum((i + 1) * hb, T // HALO - 1), 0))
    vec = pl.BlockSpec((1, CH), lambda i: (0, 0))
    return pl.pallas_call(
        body, name="conv_bwd", grid=(nt,),
        in_specs=[cur(CH), nxt(CH), cur(CH), nxt(CH), cur(2 * CH),
                  pl.BlockSpec((HALO, 2 * CH), lambda i: (jnp.maximum(i * hb - 1, 0), 0)),
                  pl.BlockSpec((CONV_W, CH), lambda i: (0, 0)), vec, vec],
        out_specs=[pl.BlockSpec((tm, 2 * CH), lambda i: (i, 0)), pl.BlockSpec((32, CH), lambda i: (0, 0)),
                   pl.BlockSpec((8, CH), lambda i: (0, 0))],
        out_shape=[jax.ShapeDtypeStruct((T, 2 * CH), BF16), jax.ShapeDtypeStruct((32, CH), F32),
                   jax.ShapeDtypeStruct((8, CH), F32)],
        scratch_shapes=[pltpu.VMEM((n, CH), F32), pltpu.VMEM((7, n - 8, CH), F32),
                        pltpu.VMEM((n, CH), F32), pltpu.VMEM((7, n - 8, CH), F32), pltpu.VMEM((tm, CH), F32),
                        pltpu.VMEM((32, 8, CH), F32)],
        compiler_params=_cp(1),
    )(dconv, dconv, ypre, ypre, u, u, w, lg, lb)


def _attn_bwd(sinks, tab, qkv, dattn):
    T = qkv.shape[0]
    nb = T // WINDOW

    def body(sink_ref, tab_ref, q_ref, kvp_ref, kvc_ref, do_ref, dq_ref, dkv_ref, dsk_ref, carry_ref):
        n = pl.program_id(0)

        @pl.when(n == 0)
        def _():
            dsk_ref[...] = jnp.zeros_like(dsk_ref)
            carry_ref[...] = jnp.zeros_like(carry_ref)

        @pl.when(n < nb)
        def _():
            seen = _first_block_mask(n)
            for g in range(N_KV):
                qs = _stack_heads(q_ref, g)
                dos = _stack_heads(do_ref, g)
                k = _band(kvp_ref, kvc_ref, g * HEAD_DIM)
                v = _band(kvp_ref, kvc_ref, KV_W + g * HEAD_DIM)
                p, ps = _attn_probs(qs, k, tab_ref[g], seen, _sink_col(sink_ref, g))
                dp = _dot_nt(dos, v)
                delta = jnp.sum(p * dp, axis=-1, keepdims=True)
                dsb = (p * (dp - delta)).astype(BF16)
                dsink = -ps * delta
                dqs = _dot(dsb, k) * SCALE
                dk = _dot_tn(dsb, qs) * SCALE
                dv = _dot_tn(p.astype(BF16), dos)
                for i in range(GROUP):
                    h = GROUP * g + i
                    dq_ref[:, h * HEAD_DIM:(h + 1) * HEAD_DIM] = dqs[i * WINDOW:(i + 1) * WINDOW].astype(BF16)
                    dsk_ref[h:h + 1, :] += jnp.sum(dsink[i * WINDOW:(i + 1) * WINDOW], axis=0, keepdims=True)
                for off, d in ((g * HEAD_DIM, dk), (KV_W + g * HEAD_DIM, dv)):
                    dkv_ref[:, off:off + HEAD_DIM] = (carry_ref[:, off:off + HEAD_DIM] + d[:WINDOW]).astype(BF16)
                    carry_ref[:, off:off + HEAD_DIM] = d[WINDOW:]

        @pl.when(n == nb)
        def _():
            dkv_ref[...] = carry_ref[...].astype(BF16)

    last = nb - 1
    return pl.pallas_call(
        body, name="attn_bwd", grid=(nb + 1,),
        in_specs=[pl.BlockSpec(memory_space=pltpu.SMEM),
                  pl.BlockSpec(tab.shape, lambda n: (0, 0, 0)),
                  pl.BlockSpec((WINDOW, ATTN_W), lambda n: (jnp.minimum(n, last), 0)),
                  pl.BlockSpec((WINDOW, 2 * KV_W), lambda n: (jnp.clip(n - 1, 0, last), 2)),
                  pl.BlockSpec((WINDOW, 2 * KV_W), lambda n: (jnp.minimum(n, last), 2)),
                  pl.BlockSpec((WINDOW, ATTN_W), lambda n: (jnp.minimum(n, last), 0))],
        out_specs=[pl.BlockSpec((WINDOW, ATTN_W), lambda n: (jnp.minimum(n, last), 0)),
                   pl.BlockSpec((WINDOW, 2 * KV_W), lambda n: (jnp.maximum(n - 1, 0), 0)),
                   pl.BlockSpec((8, LANES), lambda n: (0, 0))],
        out_shape=[jax.ShapeDtypeStruct((T, ATTN_W), BF16), jax.ShapeDtypeStruct((T, 2 * KV_W), BF16),
                   jax.ShapeDtypeStruct((8, LANES), F32)],
        scratch_shapes=[pltpu.VMEM((WINDOW, 2 * KV_W), F32)],
        compiler_params=_cp(1),
    )(sinks, tab, qkv, qkv, qkv, dattn)


def _pack(arrs):
    flat = jnp.concatenate([a.reshape(-1) for a in arrs])
    pad = -flat.shape[0] % (8 * LANES)
    return jnp.pad(flat, (0, pad)).reshape(1, -1, LANES)


def _unpack(packed, like):
    flat = packed.reshape(-1)
    out, off = [], 0
    for a in like:
        out.append(flat[off:off + a.size].reshape(a.shape))
        off += a.size
    return out


def kernel(x, norm_ffn1, w_ffn1_in, w_ffn1_out, norm_mix, w_in, sinks, w_dw, b_dw, conv_ln_g, conv_ln_b, w_out, norm_ffn2, w_ffn2_in, w_ffn2_out, final_norm, loss_target, m_norm_ffn1, m_w_ffn1_in, m_w_ffn1_out, m_norm_mix, m_w_in, m_sinks, m_w_dw, m_b_dw, m_conv_ln_g, m_conv_ln_b, m_w_out, m_norm_ffn2, m_w_ffn2_in, m_w_ffn2_out, m_final_norm, v_norm_ffn1, v_w_ffn1_in, v_w_ffn1_out, v_norm_mix, v_w_in, v_sinks, v_w_dw, v_b_dw, v_conv_ln_g, v_conv_ln_b, v_w_out, v_norm_ffn2, v_w_ffn2_in, v_w_ffn2_out, v_final_norm):
    L, D = norm_ffn1.shape
    T = x.shape[1]
    FB = w_ffn1_in.shape[2]
    CH = b_dw.shape[1]
    QKV = ATTN_W + 2 * KV_W
    xs = x.reshape(T, D)
    tgt = loss_target.reshape(T, D)
    cx, cy, cc = lax.axis_index("x"), lax.axis_index("y"), lax.axis_index("c")
    chip = 2 * cx + cy
    cidx = cc.reshape(1).astype(jnp.int32)
    tr = lambda a_: jnp.transpose(a_, (0, 2, 1))
    big_w = (w_ffn1_in, w_ffn1_out, tr(w_in), w_out, w_ffn2_in, w_ffn2_out)
    big_m = (m_w_ffn1_in, m_w_ffn1_out, tr(m_w_in), m_w_out, m_w_ffn2_in, m_w_ffn2_out)
    big_v = (v_w_ffn1_in, v_w_ffn1_out, tr(v_w_in), v_w_out, v_w_ffn2_in, v_w_ffn2_out)
    NW = len(big_w) + 1

    def shards(l, tok):
        return [(w_[l] + tok[0, 0]).astype(BF16) for w_ in big_w] + [w_dw[l] + tok[0, 0]]

    def own_slot(a, slots=4, idx=chip):
        return lax.dynamic_update_index_in_dim(lax.empty((slots,) + a.shape, a.dtype), a, idx, 0)

    def gather_start(srcs, tok):
        return _xchg_start("gather_start", srcs, [own_slot(s_) for s_ in srcs], _gather_plan, tok)

    def gather_arrived(started, after, n, taps):
        _, lands, tok = _xchg_wait("gather_wait", started, n, n, _gather_plan, after)
        return _xchg_start("gshare_start", [], lands[:-1] if taps else lands, _gshare_plan, tok, "sibling3"), lands[-1]

    def shared_weights(shared, after, n):
        _, mats, tok = _xchg_wait("gshare_wait", shared, 0, n, _gshare_plan, after, "sibling3")
        return mats, tok

    row = lambda a, l: a[l].reshape(1, -1)
    tab = _attn_bias_table()
    NB = len(big_w)

    saved, W = [], []
    zero_tok = jnp.zeros((8, LANES), F32)
    src0 = shards(0, zero_tok)
    started = gather_start(src0[:2], zero_tok)
    rest0 = gather_start(src0[2:], started[-1])
    cast = [None] + [shards(l, rest0[-1]) for l in range(1, L)]
    shared, _ = gather_arrived(started, [xs] + [a_ for c_ in cast[1:] for a_ in c_], 2, False)
    after = [shared[-1]]
    for l in range(L):
        mats, tok = shared_weights(shared, after, 2 if l == 0 else NB)
        started = None
        if l + 1 < L:
            started = gather_start(cast[l + 1], tok)
            tok = started[-1]
        x0 = xs
        x1, gu1 = _ffn_fwd(x0, row(norm_ffn1, l) + tok[0, 0], mats[0], mats[1].reshape(2 * FB, D))
        gm_row = row(norm_mix, l)
        if l == 0:
            shared, gdw = gather_arrived(rest0, [x1], NW - 2, True)
            rest, tok = shared_weights(shared, [shared[-1]], NB - 2)
            mats = list(mats) + list(rest)
            gm_row = gm_row + tok[0, 0]
        g1i, g1o, gi, go, g2i, g2o = mats
        w = dict(f1i=g1i, f1o=g1o.reshape(2 * FB, D), f2i=g2i, f2o=g2o.reshape(2 * FB, D),
                 wit=gi.reshape(-1, D), wo=go.reshape(-1, D),
                 wdw=jnp.transpose(gdw, (1, 0, 2)).reshape(CONV_W, CH))
        W.append(w)
        qkv, u = _mixproj_fwd(x1, gm_row, w["wit"])
        attn = _attn_fwd(row(sinks, l), tab, qkv)
        conv, ypre = _conv_fwd(u, w["wdw"], row(b_dw, l), row(conv_ln_g, l), row(conv_ln_b, l))
        x2 = _mixout_fwd(x1, attn, conv, w["wo"])
        g2_row = row(norm_ffn2, l)
        if started is not None:
            shared, gdw = gather_arrived(started, [x2], NW, True)
            g2_row = g2_row + shared[-1][0, 0]
        xs, gu2 = _ffn_fwd(x2, g2_row, w["f2i"], w["f2o"])
        saved.append((x0, gu1, x1, qkv, u, attn, conv, ypre, x2, gu2))
        after = [xs]

    loss_part, dx, d_final = _loss_head(xs, final_norm.reshape(1, D), tgt)
    loss = lax.psum(loss_part[0, 0], ("x", "y", "c"))

    bufs = [[lax.empty(w_.shape, F32) for _ in range(4)] for w_ in big_w]
    d_n1, d_nm, d_n2 = [None] * L, [None] * L, [None] * L
    d_sk, d_bdw, d_lg, d_lb, d_wdw = [None] * L, [None] * L, [None] * L, [None] * L, [None] * L

    def sib_start(gs):
        return _xchg_start("sib_start", gs, [lax.empty((4, g.shape[1] // 2, g.shape[2]), F32) for g in gs],
                           _sib_plan, zero_tok, "sibling")

    def reduce_start(sib_started, after, n):
        gs, sibs, _ = _xchg_wait("sib_wait", sib_started, n, n, _sib_plan, after, "sibling")
        parts = [_sum_halves(cidx, g, s_) for g, s_ in zip(gs, sibs)]
        lands = [own_slot(lax.dynamic_index_in_dim(p, chip, 0, keepdims=False)) for p in parts]
        return _xchg_start("rs_start", parts, lands, _rs_plan, zero_tok)

    def share_start(rs_started, after, n):
        _, qs, tok = _xchg_wait("rs_wait", rs_started, n, n, _rs_plan, after)
        return _xchg_start("qshare_start", qs, [lax.empty(q.shape, q.dtype) for q in qs], _whole_plan, tok, "sibling")

    def finish(l, shared, after, idxs):
        q_own, q_sib, _ = _xchg_wait("qshare_wait", shared, len(idxs), len(idxs), _whole_plan, after, "sibling")
        for k, t in enumerate(idxs):
            bufs[t] = _adamw_layer(cidx, q_own[k], q_sib[k], big_w[t], big_m[t], big_v[t], bufs[t], l)

    ALL = list(range(NB))
    EARLY, LATE = ALL[2:], ALL[:2]
    sib_pending = rs_pending = None
    shares = []
    tok = zero_tok
    for l in reversed(range(L)):
        w = W[l]
        x0, gu1, x1, qkv, u, attn, conv, ypre, x2, gu2 = saved[l]
        dx, d_n2[l], hb, dgu, a, dyb = _ffn_bwd(dx, x2, row(norm_ffn2, l), gu2, w["f2i"], w["f2o"], tok)
        g_f2i, g_f2o = _wgrad_ffn_in(hb, dgu, tok), _wgrad_ffn_out(a, dyb, tok)
        lg_row = row(conv_ln_g, l)
        if sib_pending is not None:
            rs_started = reduce_start(sib_pending[1], [g_f2o], NB)
            if rs_pending is not None:
                shares.append((rs_pending[0], share_start(rs_pending[1], [rs_started[-1]], NB)))
            rs_pending = (sib_pending[0], rs_started)
            lg_row = lg_row + rs_started[-1][0, 0]
        dyb, dattn, dconv = _mixout_bwd(dx, w["wo"])
        g_wo = _wgrad_cat([attn, conv], [dyb]).reshape(4, -1, D)
        du, dwdw, dvec = _conv_bwd(dconv, ypre, u, w["wdw"], lg_row, row(conv_ln_b, l))
        d_wdw[l], d_bdw[l], d_lg[l], d_lb[l] = dwdw[:CONV_W], dvec[0], dvec[1], dvec[2]
        dq, dkv, dsk = _attn_bwd(row(sinks, l), tab, qkv, dattn)
        d_sk[l] = dsk[:, 0]
        wit = w["wit"]
        dx, d_nm[l], hb = _mix_rms_bwd(dx, x1, row(norm_mix, l), [dq, dkv, du],
                                       [wit[:ATTN_W], wit[ATTN_W:QKV], wit[QKV:]])
        g_wi = _wgrad_cat([dq, dkv, du], [hb]).reshape(4, -1, D)
        if l == 0:
            sib_early = sib_start([g_wi, g_wo, g_f2i, g_f2o])
            tok = sib_early[-1]
        dx, d_n1[l], hb, dgu, a, dyb = _ffn_bwd(dx, x0, row(norm_ffn1, l), gu1, w["f1i"], w["f1o"], tok)
        if l == 0:
            rs_early = reduce_start(sib_early, [dx], len(EARLY))
            tok = rs_early[-1]
        g_f1i, g_f1o = _wgrad_ffn_in(hb, dgu, tok), _wgrad_ffn_out(a, dyb, tok)
        sib_started = sib_start([g_f1i, g_f1o] if l == 0 else [g_f1i, g_f1o, g_wi, g_wo, g_f2i, g_f2o])
        tok = sib_started[-1]
        sib_pending = (l, sib_started)
    grad_x = dx.reshape(x.shape)

    small_g = [jnp.concatenate(d, axis=0) for d in (d_n1, d_nm, d_n2)] + [d_final, jnp.stack(d_sk)] + \
              [jnp.stack(d) for d in (d_bdw, d_lg, d_lb, d_wdw)]
    packed = _pack(small_g)[0]
    small_started = _xchg_start("small_start", [packed], [own_slot(packed, 8, 4 * cx + 2 * cy + cc)], _slot_plan, tok, "all")

    after = [small_started[-1]]
    if rs_pending is not None:
        shares.append((rs_pending[0], share_start(rs_pending[1], after, NB)))
        after = [shares[-1][1][-1]]
    if shares:
        finish(*shares.pop(0), after, ALL)
        after = [b_[0] for b_ in bufs]
    rs_late = reduce_start(sib_pending[1], after, len(LATE))
    after = [rs_late[-1]]
    for l, sh in shares:
        finish(l, sh, after, ALL)
        after = [b_[0] for b_ in bufs]
    _, (slots,), _ = _xchg_wait("small_wait", small_started, 1, 1, _slot_plan, after, "all")
    small_sum = _unpack(_sum_slots(slots), small_g)
    g_wdw = lax.dynamic_slice_in_dim(small_sum[8], chip * w_dw.shape[2], w_dw.shape[2], axis=2)
    small_g = [small_sum[0], small_sum[1], small_sum[2], small_sum[3].reshape(D), small_sum[4],
               small_sum[5], small_sum[6], small_sum[7], g_wdw]
    small_w = (norm_ffn1, norm_mix, norm_ffn2, final_norm, sinks, b_dw, conv_ln_g, conv_ln_b, w_dw)
    small_m = (m_norm_ffn1, m_norm_mix, m_norm_ffn2, m_final_norm, m_sinks, m_b_dw, m_conv_ln_g, m_conv_ln_b, m_w_dw)
    small_v = (v_norm_ffn1, v_norm_mix, v_norm_ffn2, v_final_norm, v_sinks, v_b_dw, v_conv_ln_g, v_conv_ln_b, v_w_dw)
    upd = _adamw(_pack(small_g), _pack(small_w), _pack(small_m), _pack(small_v))
    small_upd = [_unpack(u_, small_w) for u_ in upd]
    sh_early = share_start(rs_early, [upd[0]], len(EARLY))
    sh_late = share_start(rs_late, [sh_early[-1]], len(LATE))
    finish(0, sh_early, [sh_late[-1]], EARLY)
    finish(0, sh_late, [bufs[t][0] for t in EARLY], LATE)

    order = ("norm_ffn1", "w_ffn1_in", "w_ffn1_out", "norm_mix", "w_in", "sinks", "w_dw", "b_dw", "conv_ln_g",
             "conv_ln_b", "w_out", "norm_ffn2", "w_ffn2_in", "w_ffn2_out", "final_norm")
    small_names = ("norm_ffn1", "norm_mix", "norm_ffn2", "final_norm", "sinks", "b_dw", "conv_ln_g", "conv_ln_b", "w_dw")
    big_names = ("w_ffn1_in", "w_ffn1_out", "w_in", "w_out", "w_ffn2_in", "w_ffn2_out")
    grads, deltas, new_m, new_v = {}, {}, {}, {}
    for i, nme in enumerate(small_names):
        grads[nme], deltas[nme], new_m[nme], new_v[nme] = small_g[i], small_upd[0][i], small_upd[1][i], small_upd[2][i]
    for i, nme in enumerate(big_names):
        grads[nme], deltas[nme], new_m[nme], new_v[nme] = [tr(b_) for b_ in bufs[i]] if nme == "w_in" else bufs[i]
    return (loss, grad_x, *[grads[n] for n in order], *[deltas[n] for n in order],
            *[new_m[n] for n in order], *[new_v[n] for n in order])
```

```python
import functools

import jax
import jax.numpy as jnp
from jax import lax
from jax.experimental import pallas as pl
from jax.experimental.pallas import tpu as pltpu

F32, BF16 = jnp.float32, jnp.bfloat16
EPS = 1e-6
NEG_INF = -1e30
HEAD_DIM = 64
N_HEADS = 8
N_KV = 2
GROUP = N_HEADS // N_KV
WINDOW = 128
ATTN_W = N_HEADS * HEAD_DIM
KV_W = N_KV * HEAD_DIM
CONV_W = 31
HALO = 32
CONV_ROWS = 32
SCALE = 1.0 / 8.0
ADAM_LR, ADAM_B1, ADAM_B2, ADAM_EPS, ADAM_WD, ADAM_STEP = 0.001, 0.9, 0.999, 1e-08, 0.01, 10
TM = 512
TM_FFN_BWD = 256
TK_WGRAD = 2048
TM_MIX = 1024
LANES = 128
VMEM_LIMIT = 52 * 1024 * 1024
MESH = pl.DeviceIdType.MESH
ANY = pl.BlockSpec(memory_space=pl.ANY)
HBM = pl.BlockSpec(memory_space=pltpu.HBM)
SEM = pl.BlockSpec(memory_space=pltpu.SEMAPHORE)
VMEM = pl.BlockSpec(memory_space=pltpu.VMEM)
EFFECT = pltpu.SideEffectType.DATAFLOW_SIDE_EFFECTING
TOKEN = jax.ShapeDtypeStruct((8, LANES), F32)


def _cp(n):
    return pltpu.CompilerParams(dimension_semantics=("arbitrary",) * n, vmem_limit_bytes=VMEM_LIMIT)


def _dot(a, b):
    return jnp.dot(a, b, preferred_element_type=F32)


def _dot_nt(a, b):
    return lax.dot_general(a, b, (((1,), (1,)), ((), ())), preferred_element_type=F32)


def _dot_tn(a, b):
    return lax.dot_general(a, b, (((0,), (0,)), ((), ())), preferred_element_type=F32)


def _place():
    x, y, c = lax.axis_index("x"), lax.axis_index("y"), lax.axis_index("c")
    chips = [(1 - x, y), (x, 1 - y), (1 - x, 1 - y)]
    return x, y, c, chips


def _rcopy(src, dst, send_sems, recv_sems, k, dev):
    return pltpu.make_async_remote_copy(src_ref=src, dst_ref=dst, send_sem=send_sems.at[k],
                                        recv_sem=recv_sems.at[k], device_id=dev, device_id_type=MESH)


def _hbm(a):
    return pltpu.with_memory_space_constraint(a, pltpu.HBM)


PEERS = {"chips": 3, "sibling": 1, "sibling3": 3, "all": 7}


def _targets(mode):
    x, y, c, chips = _place()
    b = 2 * x + y
    if mode == "chips":
        return b, c, [((px, py, c), 2 * px + py) for px, py in chips]
    if mode == "sibling":
        return b, c, [((x, y, 1 - c), b)]
    if mode == "sibling3":
        return b, c, [((x, y, 1 - c), 2 * px + py) for px, py in chips]
    flip = lambda v, f: 1 - v if f else v
    devs = [(flip(x, k >> 2 & 1), flip(y, k >> 1 & 1), flip(c, k & 1)) for k in range(1, 8)]
    return 4 * x + 2 * y + c, c, [(d, 4 * d[0] + 2 * d[1] + d[2]) for d in devs]


def _xchg_start(name, srcs, lands, plan, dep, mode="chips"):
    ns, nl, npeer = len(srcs), len(lands), PEERS[mode]

    def body(*refs):
        land = refs[ns:ns + nl]
        src = refs[:ns] if ns else land
        send_sems, recv_sems, token = refs[ns + nl + 1], refs[ns + nl + 2], refs[-1]
        me, c, peers = _targets(mode)
        for t in range(nl):
            for j, (dev, tag) in enumerate(peers):
                s, d, _ = plan(src[t], land[t], t, me, c, tag)
                _rcopy(s, d, send_sems, recv_sems, npeer * t + j, dev).start()
        token[...] = jnp.zeros_like(token)

    arrs = list(srcs) + list(lands)
    return pl.pallas_call(
        body, name=name,
        out_shape=(pltpu.SemaphoreType.DMA((npeer * nl,)), pltpu.SemaphoreType.DMA((npeer * nl,)),
                   *[pltpu.HBM(a.shape, a.dtype) for a in arrs], TOKEN),
        in_specs=[HBM] * (ns + nl) + [ANY], out_specs=(SEM, SEM, *[HBM] * (ns + nl), VMEM),
        input_output_aliases={i: 2 + i for i in range(ns + nl)},
        compiler_params=pltpu.CompilerParams(has_side_effects=EFFECT),
    )(*[_hbm(a) for a in arrs], dep)


def _xchg_wait(name, started, ns, nl, plan, after, mode="chips"):
    send_sems, recv_sems, thru = started[0], started[1], started[2:2 + ns + nl]
    npeer = PEERS[mode]

    def body(*refs):
        land = refs[ns:ns + nl]
        src = refs[:ns] if ns else land
        send_sems, recv_sems, token = refs[ns + nl], refs[ns + nl + 1], refs[-1]
        me, c, peers = _targets(mode)
        for t in range(nl):
            for j, (dev, tag) in enumerate(peers):
                s, _, a = plan(src[t], land[t], t, me, c, tag)
                cp = _rcopy(s, a, send_sems, recv_sems, npeer * t + j, dev)
                cp.wait_send()
                cp.wait_recv()
        token[...] = jnp.zeros_like(token)

    out = pl.pallas_call(
        body, name=name,
        out_shape=(*[pltpu.HBM(a.shape, a.dtype) for a in thru], TOKEN),
        in_specs=[HBM] * (ns + nl) + [SEM, SEM] + [ANY] * len(after), out_specs=(*[HBM] * (ns + nl), VMEM),
        input_output_aliases={i: i for i in range(ns + nl)},
        compiler_params=pltpu.CompilerParams(has_side_effects=EFFECT),
    )(*thru, send_sems, recv_sems, *after)
    return out[:ns], out[ns:ns + nl], out[-1]


def _half(ref_rows, which):
    h = ref_rows // 2
    return pl.ds(which * h, h)


def _gather_plan(src, land, t, b, c, pb):
    if len(src.shape) == 2 and src.shape[0] % 2 == 0:
        hs = _half(src.shape[0], c)
        return src.at[hs], land.at[b, hs], land.at[pb, hs]
    return src, land.at[b], land.at[pb]


def _gshare_plan(src, land, t, b, c, pb):
    return land.at[pb, _half(land.shape[1], c)], land.at[pb, _half(land.shape[1], c)], land.at[pb, _half(land.shape[1], 1 - c)]


def _rs_plan(src, land, t, b, c, pb):
    return src.at[pb], land.at[b], land.at[pb]


def _sib_plan(src, land, t, b, c, pb):
    return src.at[:, _half(src.shape[1], 1 - c), :], land, land


def _rows_block(h, cap=512):
    for rb in range(min(h, cap) // 16 * 16, 0, -16):
        if h % rb == 0:
            return rb
    return h


def _sum_halves(cidx, g, s):
    _, R, C = g.shape
    rb = _rows_block(R // 2)
    nr = R // 2 // rb

    def body(c_ref, g_ref, s_ref, o_ref):
        o_ref[...] = (g_ref[...] + s_ref[...]).astype(BF16)

    blk = (None, rb, C)
    return pl.pallas_call(
        body, name="sum_halves", out_shape=jax.ShapeDtypeStruct(s.shape, BF16),
        grid_spec=pltpu.PrefetchScalarGridSpec(
            num_scalar_prefetch=1, grid=(4, nr),
            in_specs=[pl.BlockSpec(blk, lambda p, i, c: (p, c[0] * nr + i, 0)),
                      pl.BlockSpec(blk, lambda p, i, c: (p, i, 0))],
            out_specs=pl.BlockSpec(blk, lambda p, i, c: (p, i, 0))),
        compiler_params=_cp(2),
    )(cidx, g, s)


def _whole_plan(src, land, t, me, c, tag):
    return src, land, land


def _slot_plan(src, land, t, me, c, tag):
    return src, land.at[me], land.at[tag]


def _adam_update(gg, w, m, v):
    m2 = ADAM_B1 * m + (1.0 - ADAM_B1) * gg
    v2 = ADAM_B2 * v + (1.0 - ADAM_B2) * (gg * gg)
    mh = m2 / (1.0 - ADAM_B1 ** ADAM_STEP)
    vh = v2 / (1.0 - ADAM_B2 ** ADAM_STEP)
    return -ADAM_LR * (mh / (jnp.sqrt(vh) + ADAM_EPS) + ADAM_WD * w), m2, v2


def _adamw_layer(cidx, q_own, q_sib, w, m, v, bufs, l):
    L, R, C = w.shape
    h = R // 2
    rb = _rows_block(h, 256)
    nr = h // rb

    def body(c_ref, qo_ref, qs_ref, w_ref, m_ref, v_ref, *rest):
        g_ref, d_ref, mo_ref, vo_ref = rest[-4:]
        own = pl.program_id(0) == c_ref[0]
        gg = jnp.zeros((rb, C), F32)
        for s in range(4):
            gg = gg + jnp.where(own, qo_ref[s], qs_ref[s]).astype(F32)
        g_ref[...] = gg
        d_ref[...], mo_ref[...], vo_ref[...] = _adam_update(gg, w_ref[...], m_ref[...], v_ref[...])

    q_own_spec = pl.BlockSpec((4, rb, C), lambda hh, i, c: (0, jnp.where(hh == c[0], i, 0), 0))
    q_sib_spec = pl.BlockSpec((4, rb, C), lambda hh, i, c: (0, jnp.where(hh == c[0], 0, i), 0))
    wspec = pl.BlockSpec((None, rb, C), lambda hh, i, c: (l, hh * nr + i, 0))
    return pl.pallas_call(
        body, name="adamw_layer", out_shape=[jax.ShapeDtypeStruct(w.shape, F32)] * 4,
        grid_spec=pltpu.PrefetchScalarGridSpec(
            num_scalar_prefetch=1, grid=(2, nr),
            in_specs=[q_own_spec, q_sib_spec, wspec, wspec, wspec] + [ANY] * 4, out_specs=[wspec] * 4),
        input_output_aliases={6 + k: k for k in range(4)},
        compiler_params=_cp(2),
    )(cidx, q_own, q_sib, w, m, v, *bufs)


def _adamw(g, w, m, v):
    L, R, C = g.shape
    rb = _rows_block(R)

    def body(g_ref, w_ref, m_ref, v_ref, d_ref, mo_ref, vo_ref):
        d_ref[...], mo_ref[...], vo_ref[...] = _adam_update(g_ref[...], w_ref[...], m_ref[...], v_ref[...])

    spec = pl.BlockSpec((None, rb, C), lambda l, i: (l, i, 0))
    return pl.pallas_call(
        body, name="adamw", grid=(L, R // rb), in_specs=[spec] * 4, out_specs=[spec] * 3,
        out_shape=[jax.ShapeDtypeStruct(g.shape, F32)] * 3, compiler_params=_cp(2),
    )(g, w, m, v)


def _sum_slots(buf):
    def body(b_ref, o_ref):
        acc = b_ref[0]
        for k in range(1, 8):
            acc = acc + b_ref[k]
        o_ref[...] = acc

    return pl.pallas_call(body, name="sum_slots", in_specs=[VMEM], out_specs=VMEM,
                          out_shape=jax.ShapeDtypeStruct(buf.shape[1:], F32))(buf)


def _rms(xf, g):
    r = lax.rsqrt(jnp.mean(xf * xf, axis=-1, keepdims=True) + EPS)
    return xf * r, r


def _lane_chunks(n):
    lo = (n // LANES + 1) // 2 * LANES
    return ((0, lo), (lo, n - lo))


def _load_ffn_weights(win_hbm, wout_hbm, win_v, wout_v, sems):
    fb = win_v.shape[2]
    loads = [pltpu.make_async_copy(win_hbm.at[k], win_v.at[k], sems.at[k]) for k in range(4)]
    loads += [pltpu.make_async_copy(wout_hbm.at[pl.ds(k * fb, fb)], wout_v.at[pl.ds(k * fb, fb)], sems.at[4 + k])
              for k in range(2)]
    for cp in loads:
        cp.start()
    for cp in loads:
        cp.wait()


def _fast_sigmoid(v):
    return pl.reciprocal(1.0 + jnp.exp(-v), approx=True)


def _ffn_fwd(x, g, win, wout):
    T, D = x.shape
    FB = win.shape[2]
    tm = min(TM, T)

    def body(x_ref, g_ref, win_hbm, wout_hbm, xo_ref, gu_ref, win_v, wout_v, sems):
        @pl.when(pl.program_id(0) == 0)
        def _():
            _load_ffn_weights(win_hbm, wout_hbm, win_v, wout_v, sems)

        xf = x_ref[...]
        xh, _ = _rms(xf, None)
        h = (xh * g_ref[...]).astype(BF16)
        acc = jnp.zeros((tm, D), F32)
        for blk in range(2):
            for lo, sz in _lane_chunks(FB):
                cols = pl.ds(blk * FB + lo, sz)
                gate = _dot(h, win_v[blk, :, pl.ds(lo, sz)])
                up = _dot(h, win_v[2 + blk, :, pl.ds(lo, sz)])
                gu_ref[0, :, cols] = gate.astype(BF16)
                gu_ref[1, :, cols] = up.astype(BF16)
                a = (gate * _fast_sigmoid(gate) * up).astype(BF16)
                acc = acc + _dot(a, wout_v[cols, :])
        xo_ref[...] = xf + 0.5 * acc

    row = pl.BlockSpec((tm, D), lambda i: (i, 0))
    return pl.pallas_call(
        body, name="ffn_fwd", grid=(T // tm,),
        in_specs=[row, pl.BlockSpec((1, D), lambda i: (0, 0)), ANY, ANY],
        out_specs=[row, pl.BlockSpec((2, tm, 2 * FB), lambda i: (0, i, 0))],
        out_shape=[jax.ShapeDtypeStruct((T, D), F32), jax.ShapeDtypeStruct((2, T, 2 * FB), BF16)],
        scratch_shapes=[pltpu.VMEM(win.shape, BF16), pltpu.VMEM(wout.shape, BF16), pltpu.SemaphoreType.DMA((6,))],
        compiler_params=_cp(1),
    )(x, g, win, wout)


def _mixproj_fwd(x, g, wt):
    T, D = x.shape
    W = wt.shape[0]
    QKV = ATTN_W + 2 * KV_W
    tm = min(TM_MIX, T)

    def body(x_ref, g_ref, w_ref, qkv_ref, u_ref):
        xh, _ = _rms(x_ref[...], None)
        h = (xh * g_ref[...]).astype(BF16)
        qkv_ref[...] = _dot_nt(h, w_ref[:QKV, :]).astype(BF16)
        u_ref[...] = _dot_nt(h, w_ref[QKV:, :])

    return pl.pallas_call(
        body, name="mixproj_fwd", grid=(T // tm,),
        in_specs=[pl.BlockSpec((tm, D), lambda i: (i, 0)), pl.BlockSpec((1, D), lambda i: (0, 0)),
                  pl.BlockSpec((W, D), lambda i: (0, 0))],
        out_specs=[pl.BlockSpec((tm, QKV), lambda i: (i, 0)), pl.BlockSpec((tm, W - QKV), lambda i: (i, 0))],
        out_shape=[jax.ShapeDtypeStruct((T, QKV), BF16), jax.ShapeDtypeStruct((T, W - QKV), F32)],
        compiler_params=_cp(1),
    )(x, g, wt)


def _attn_bias_table():
    rows, cols = GROUP * WINDOW, 2 * WINDOW
    row = lax.broadcasted_iota(jnp.int32, (N_KV, rows, cols), 1)
    col = lax.broadcasted_iota(jnp.int32, (N_KV, rows, cols), 2)
    head = GROUP * lax.broadcasted_iota(jnp.int32, (N_KV, rows, cols), 0) + (row >> 7)
    dist = (row & (WINDOW - 1)) + WINDOW - col
    slope = jnp.exp2(-(head + 1).astype(F32))
    return jnp.where((dist >= 0) & (dist < WINDOW), -slope * dist.astype(F32), NEG_INF)


def _first_block_mask(n):
    col = lax.broadcasted_iota(jnp.int32, (GROUP * WINDOW, 2 * WINDOW), 1)
    return (n > 0) | (col >= WINDOW)


def _sink_col(sink_ref, g):
    hi = lax.broadcasted_iota(jnp.int32, (GROUP * WINDOW, 1), 0) >> 7
    col = jnp.zeros((GROUP * WINDOW, 1), F32)
    for i in range(GROUP):
        col = jnp.where(hi == i, sink_ref[0, GROUP * g + i], col)
    return col


def _stack_heads(ref, g):
    return jnp.concatenate([ref[:, (GROUP * g + i) * HEAD_DIM:(GROUP * g + i + 1) * HEAD_DIM]
                            for i in range(GROUP)], axis=0)


def _band(kvp_ref, kvc_ref, off):
    return jnp.concatenate([kvp_ref[:, off:off + HEAD_DIM], kvc_ref[:, off:off + HEAD_DIM]], axis=0)


def _attn_probs(qs, k, bias, seen, sink):
    s = jnp.where(seen, _dot_nt(qs, k) * SCALE + bias, NEG_INF)
    m = jnp.maximum(jnp.max(s, axis=-1, keepdims=True), sink)
    p = jnp.exp(s - m)
    es = jnp.exp(sink - m)
    inv = 1.0 / (jnp.sum(p, axis=-1, keepdims=True) + es)
    return p * inv, es * inv


def _attn_fwd(sinks, tab, qkv):
    T = qkv.shape[0]
    nb = T // WINDOW

    def body(sink_ref, tab_ref, q_ref, kvp_ref, kvc_ref, o_ref):
        seen = _first_block_mask(pl.program_id(0))
        for g in range(N_KV):
            qs = _stack_heads(q_ref, g)
            k = _band(kvp_ref, kvc_ref, g * HEAD_DIM)
            v = _band(kvp_ref, kvc_ref, KV_W + g * HEAD_DIM)
            p, _ = _attn_probs(qs, k, tab_ref[g], seen, _sink_col(sink_ref, g))
            o = _dot(p.astype(BF16), v)
            for i in range(GROUP):
                h = GROUP * g + i
                o_ref[:, h * HEAD_DIM:(h + 1) * HEAD_DIM] = o[i * WINDOW:(i + 1) * WINDOW].astype(BF16)

    return pl.pallas_call(
        body, name="attn_fwd", grid=(nb,),
        in_specs=[pl.BlockSpec(memory_space=pltpu.SMEM),
                  pl.BlockSpec(tab.shape, lambda n: (0, 0, 0)),
                  pl.BlockSpec((WINDOW, ATTN_W), lambda n: (n, 0)),
                  pl.BlockSpec((WINDOW, 2 * KV_W), lambda n: (jnp.maximum(n - 1, 0), 2)),
                  pl.BlockSpec((WINDOW, 2 * KV_W), lambda n: (n, 2))],
        out_specs=pl.BlockSpec((WINDOW, ATTN_W), lambda n: (n, 0)),
        out_shape=jax.ShapeDtypeStruct((T, ATTN_W), BF16),
        compiler_params=_cp(1),
    )(sinks, tab, qkv, qkv, qkv)


def _shift_copies(src_ref, dst_ref, n):
    for b in range(1, 8):
        dst_ref[b - 1] = src_ref[b:b + n, :]


def _tap(src_ref, sh_ref, s, c0):
    a, b = divmod(s, 8)
    start = pl.multiple_of(c0 + 8 * a, 8)
    if b == 0:
        return src_ref[pl.ds(start, CONV_ROWS), :]
    return sh_ref[b - 1, pl.ds(start, CONV_ROWS), :]


def _glu_rows(u, ch):
    return u[:, :ch] * _fast_sigmoid(u[:, ch:])


def _fill_z(zs_ref, zsh_ref, uc_ref, up_ref, i, ch, n):
    zs_ref[0:HALO] = jnp.where(i > 0, _glu_rows(up_ref[...], ch), 0.0)
    zs_ref[HALO:] = _glu_rows(uc_ref[...], ch)
    _shift_copies(zs_ref, zsh_ref, n - 8)


def _conv_fwd(u, w, b, lg, lb):
    T = u.shape[0]
    CH = u.shape[1] // 2
    tm = min(TM, T)
    n = tm + HALO
    hb = tm // HALO

    def body(uc_ref, up_ref, w_ref, b_ref, lg_ref, lb_ref, conv_ref, ypre_ref, zs_ref, zsh_ref):
        i = pl.program_id(0)
        _fill_z(zs_ref, zsh_ref, uc_ref, up_ref, i, CH, n)
        bias = b_ref[...]

        def chunk(ci, carry):
            c0 = pl.multiple_of(ci * CONV_ROWS, CONV_ROWS)
            acc = jnp.broadcast_to(bias, (CONV_ROWS, CH))
            for k in range(CONV_W):
                acc = acc + w_ref[k:k + 1, :] * _tap(zs_ref, zsh_ref, HALO - (CONV_W - 1) + k, c0)
            ypre_ref[pl.ds(c0, CONV_ROWS), :] = acc
            return carry

        lax.fori_loop(0, tm // CONV_ROWS, chunk, 0)
        y = ypre_ref[...]
        mu = jnp.mean(y, axis=-1, keepdims=True)
        d = y - mu
        var = jnp.mean(d * d, axis=-1, keepdims=True)
        o = d * lax.rsqrt(var + EPS) * lg_ref[...] + lb_ref[...]
        conv_ref[...] = (o * _fast_sigmoid(o)).astype(BF16)

    vec = pl.BlockSpec((1, CH), lambda i: (0, 0))
    return pl.pallas_call(
        body, name="conv_fwd", grid=(T // tm,),
        in_specs=[pl.BlockSpec((tm, 2 * CH), lambda i: (i, 0)),
                  pl.BlockSpec((HALO, 2 * CH), lambda i: (jnp.maximum(i * hb - 1, 0), 0)),
                  pl.BlockSpec((CONV_W, CH), lambda i: (0, 0)), vec, vec, vec],
        out_specs=[pl.BlockSpec((tm, CH), lambda i: (i, 0)), pl.BlockSpec((tm, CH), lambda i: (i, 0))],
        out_shape=[jax.ShapeDtypeStruct((T, CH), BF16), jax.ShapeDtypeStruct((T, CH), F32)],
        scratch_shapes=[pltpu.VMEM((n, CH), F32), pltpu.VMEM((7, n - 8, CH), F32)],
        compiler_params=_cp(1),
    )(u, u, w, b, lg, lb)


def _mixout_fwd(x, attn, conv, wo):
    T, D = x.shape
    tm = min(TM_MIX, T)
    A = attn.shape[1]

    def body(x_ref, a_ref, c_ref, w_ref, xo_ref):
        xo_ref[...] = x_ref[...] + _dot(a_ref[...], w_ref[:A, :]) + _dot(c_ref[...], w_ref[A:, :])

    return pl.pallas_call(
        body, name="mixout_fwd", grid=(T // tm,),
        in_specs=[pl.BlockSpec((tm, D), lambda i: (i, 0)), pl.BlockSpec((tm, A), lambda i: (i, 0)),
                  pl.BlockSpec((tm, conv.shape[1]), lambda i: (i, 0)), pl.BlockSpec(wo.shape, lambda i: (0, 0))],
        out_specs=pl.BlockSpec((tm, D), lambda i: (i, 0)),
        out_shape=jax.ShapeDtypeStruct((T, D), F32),
        compiler_params=_cp(1),
    )(x, attn, conv, wo)


def _rms_bwd_rows(dh, xf, g):
    xh, r = _rms(xf, None)
    dxn = dh * g
    dx = r * (dxn - xh * jnp.mean(dxn * xh, axis=-1, keepdims=True))
    return dx, jnp.sum(dh * xh, axis=0, keepdims=True), xh * g


def _loss_head(x, g, tgt):
    T, D = x.shape
    tm = min(TM, T)

    def body(x_ref, g_ref, t_ref, loss_ref, dx_ref, dg_ref):
        @pl.when(pl.program_id(0) == 0)
        def _():
            loss_ref[...] = jnp.zeros_like(loss_ref)
            dg_ref[...] = jnp.zeros_like(dg_ref)

        xf = x_ref[...]
        g = g_ref[...]
        xh, _ = _rms(xf, None)
        e = xh * g - t_ref[...]
        loss_ref[...] += 0.5 * jnp.sum(jnp.mean(e * e, axis=-1, keepdims=True), axis=0, keepdims=True)
        dx, dg, _ = _rms_bwd_rows(e * (1.0 / D), xf, g)
        dx_ref[...] = dx
        dg_ref[...] += dg

    return pl.pallas_call(
        body, name="loss_head", grid=(T // tm,),
        in_specs=[pl.BlockSpec((tm, D), lambda i: (i, 0)), pl.BlockSpec((1, D), lambda i: (0, 0)),
                  pl.BlockSpec((tm, D), lambda i: (i, 0))],
        out_specs=[pl.BlockSpec((1, 1), lambda i: (0, 0)), pl.BlockSpec((tm, D), lambda i: (i, 0)),
                   pl.BlockSpec((1, D), lambda i: (0, 0))],
        out_shape=[jax.ShapeDtypeStruct((1, 1), F32), jax.ShapeDtypeStruct((T, D), F32),
                   jax.ShapeDtypeStruct((1, D), F32)],
        compiler_params=_cp(1),
    )(x, g, tgt)


def _ffn_bwd(dxo, x, g, gu, win, wout, dep):
    T, D = x.shape
    FB = win.shape[2]
    tm = min(TM_FFN_BWD, T)

    def body(dxo_ref, x_ref, g_ref, gu_ref, win_hbm, wout_hbm, dep_ref,
             dxi_ref, dg_ref, hb_ref, dgu_ref, a_ref, dyb_ref, win_v, wout_v, sems):
        @pl.when(pl.program_id(0) == 0)
        def _():
            _load_ffn_weights(win_hbm, wout_hbm, win_v, wout_v, sems)
            dg_ref[...] = jnp.zeros_like(dg_ref)

        dyb = (0.5 * dxo_ref[...]).astype(BF16)
        dyb_ref[...] = dyb
        dh = jnp.zeros((tm, D), F32)
        for blk in range(2):
            for lo, sz in _lane_chunks(FB):
                cols = pl.ds(blk * FB + lo, sz)
                da = _dot_nt(dyb, wout_v[cols, :])
                gate = gu_ref[0, :, cols].astype(F32)
                up = gu_ref[1, :, cols].astype(F32)
                sg = _fast_sigmoid(gate)
                s = gate * sg
                a_ref[:, cols] = (s * up).astype(BF16)
                dgate = (da * up * (sg + s * (1.0 - sg))).astype(BF16)
                dup = (da * s).astype(BF16)
                dgu_ref[0, :, cols] = dgate
                dgu_ref[1, :, cols] = dup
                dh = dh + _dot_nt(dgate, win_v[blk, :, pl.ds(lo, sz)]) + _dot_nt(dup, win_v[2 + blk, :, pl.ds(lo, sz)])
        dx, dg, h = _rms_bwd_rows(dh, x_ref[...], g_ref[...])
        dxi_ref[...] = dxo_ref[...] + dx
        dg_ref[...] += dg
        hb_ref[...] = h.astype(BF16)

    row = pl.BlockSpec((tm, D), lambda i: (i, 0))
    act = pl.BlockSpec((2, tm, 2 * FB), lambda i: (0, i, 0))
    return pl.pallas_call(
        body, name="ffn_bwd", grid=(T // tm,),
        in_specs=[row, row, pl.BlockSpec((1, D), lambda i: (0, 0)), act, ANY, ANY, ANY],
        out_specs=[row, pl.BlockSpec((1, D), lambda i: (0, 0)), row, act,
                   pl.BlockSpec((tm, 2 * FB), lambda i: (i, 0)), row],
        out_shape=[jax.ShapeDtypeStruct((T, D), F32), jax.ShapeDtypeStruct((1, D), F32),
                   jax.ShapeDtypeStruct((T, D), BF16), jax.ShapeDtypeStruct((2, T, 2 * FB), BF16),
                   jax.ShapeDtypeStruct((T, 2 * FB), BF16), jax.ShapeDtypeStruct((T, D), BF16)],
        scratch_shapes=[pltpu.VMEM(win.shape, BF16), pltpu.VMEM(wout.shape, BF16), pltpu.SemaphoreType.DMA((6,))],
        compiler_params=_cp(1),
    )(dxo, x, g, gu, win, wout, dep)


def _mix_rms_bwd(dxo, x, g, dzs, wts):
    T, D = x.shape
    tm = min(TM, T)
    npair = len(dzs)

    def body(*refs):
        dxo_ref, x_ref, g_ref = refs[:3]
        dz_refs, w_refs = refs[3:3 + npair], refs[3 + npair:3 + 2 * npair]
        dxi_ref, dg_ref, hb_ref = refs[3 + 2 * npair:]

        @pl.when(pl.program_id(0) == 0)
        def _():
            dg_ref[...] = jnp.zeros_like(dg_ref)

        dh = jnp.zeros((tm, D), F32)
        for p in range(npair):
            dh = dh + _dot(dz_refs[p][...], w_refs[p][...])
        dx, dg, h = _rms_bwd_rows(dh, x_ref[...], g_ref[...])
        dxi_ref[...] = dxo_ref[...] + dx
        dg_ref[...] += dg
        hb_ref[...] = h.astype(BF16)

    row = pl.BlockSpec((tm, D), lambda i: (i, 0))
    return pl.pallas_call(
        body, name="mix_rms_bwd", grid=(T // tm,),
        in_specs=[row, row, pl.BlockSpec((1, D), lambda i: (0, 0))]
                 + [pl.BlockSpec((tm, dz.shape[1]), lambda i: (i, 0)) for dz in dzs]
                 + [pl.BlockSpec(w.shape, lambda i: (0, 0)) for w in wts],
        out_specs=[row, pl.BlockSpec((1, D), lambda i: (0, 0)), row],
        out_shape=[jax.ShapeDtypeStruct((T, D), F32), jax.ShapeDtypeStruct((1, D), F32),
                   jax.ShapeDtypeStruct((T, D), BF16)],
        compiler_params=_cp(1),
    )(dxo, x, g, *dzs, *wts)


def _wgrad(name, a, b, a_spec, b_spec, out_shape, out_spec, nblk, dep):
    T = a.shape[0]
    tk = min(TK_WGRAD, T)

    def body(a_ref, b_ref, dep_ref, o_ref):
        @pl.when(pl.program_id(1) == 0)
        def _():
            o_ref[...] = jnp.zeros_like(o_ref)

        o_ref[...] += _dot_tn(a_ref[...], b_ref[...]).reshape(o_ref.shape)

    return pl.pallas_call(
        body, name=name, grid=(nblk, T // tk), in_specs=[a_spec, b_spec, ANY], out_specs=out_spec,
        out_shape=jax.ShapeDtypeStruct(out_shape, F32), compiler_params=_cp(2),
    )(a, b, dep)


def _wgrad_ffn_in(hb, dgu, dep):
    T, D = hb.shape
    FB = dgu.shape[2] // 2
    tk = min(TK_WGRAD, T)
    return _wgrad("wgrad_ffn_in", hb, dgu,
                  pl.BlockSpec((tk, D), lambda b, k: (k, 0)),
                  pl.BlockSpec((None, tk, FB), lambda b, k: (b // 2, k, b % 2)),
                  (4, D, FB), pl.BlockSpec((None, D, FB), lambda b, k: (b, 0, 0)), 4, dep)


def _wgrad_ffn_out(a, dyb, dep):
    T, D = dyb.shape
    FB = a.shape[1] // 2
    tk = min(TK_WGRAD, T)
    return _wgrad("wgrad_ffn_out", a, dyb,
                  pl.BlockSpec((tk, FB), lambda b, k: (k, b)),
                  pl.BlockSpec((tk, D), lambda b, k: (k, 0)),
                  (4, FB // 2, D), pl.BlockSpec((2, FB // 2, D), lambda b, k: (b, 0, 0)), 2, dep)


def _wgrad_cat(a_list, b_list):
    T = a_list[0].shape[0]
    tk = min(TK_WGRAD, T)
    na = len(a_list)
    M, N = sum(a.shape[1] for a in a_list), sum(b.shape[1] for b in b_list)

    def body(*refs):
        a_refs, b_refs, o_ref = refs[:na], refs[na:-1], refs[-1]

        @pl.when(pl.program_id(0) == 0)
        def _():
            o_ref[...] = jnp.zeros_like(o_ref)

        r0 = 0
        for a_ref in a_refs:
            c0 = 0
            for b_ref in b_refs:
                m, n = a_ref.shape[1], b_ref.shape[1]
                o_ref[r0:r0 + m, c0:c0 + n] += _dot_tn(a_ref[...], b_ref[...])
                c0 += n
            r0 += a_ref.shape[1]

    return pl.pallas_call(
        body, name="wgrad_cat", grid=(T // tk,),
        in_specs=[pl.BlockSpec((tk, v.shape[1]), lambda k: (k, 0)) for v in list(a_list) + list(b_list)],
        out_specs=pl.BlockSpec((M, N), lambda k: (0, 0)),
        out_shape=jax.ShapeDtypeStruct((M, N), F32), compiler_params=_cp(1),
    )(*a_list, *b_list)


def _mixout_bwd(dxo, wo):
    T, D = dxo.shape
    tm = min(TM_MIX, T)
    A = ATTN_W
    C = wo.shape[0] - A

    def body(dxo_ref, w_ref, dyb_ref, da_ref, dc_ref):
        dyb = dxo_ref[...].astype(BF16)
        dyb_ref[...] = dyb
        da_ref[...] = _dot_nt(dyb, w_ref[:A, :]).astype(BF16)
        dc_ref[...] = _dot_nt(dyb, w_ref[A:, :])

    return pl.pallas_call(
        body, name="mixout_bwd", grid=(T // tm,),
        in_specs=[pl.BlockSpec((tm, D), lambda i: (i, 0)), pl.BlockSpec(wo.shape, lambda i: (0, 0))],
        out_specs=[pl.BlockSpec((tm, D), lambda i: (i, 0)), pl.BlockSpec((tm, A), lambda i: (i, 0)),
                   pl.BlockSpec((tm, C), lambda i: (i, 0))],
        out_shape=[jax.ShapeDtypeStruct((T, D), BF16), jax.ShapeDtypeStruct((T, A), BF16),
                   jax.ShapeDtypeStruct((T, C), F32)],
        compiler_params=_cp(1),
    )(dxo, wo)


def _conv_bwd(dconv, ypre, u, w, lg, lb):
    T, CH = dconv.shape
    tm = min(TM, T)
    n = tm + HALO
    hb = tm // HALO
    nt = T // tm
    nchunk = tm // CONV_ROWS

    def body(dc_ref, dcn_ref, yp_ref, ypn_ref, uc_ref, up_ref, w_ref, lg_ref, lb_ref,
             du_ref, dw_ref, dvec_ref, zs_ref, zsh_ref, dy_ref, dysh_ref, dz_ref, dwacc_ref):
        i = pl.program_id(0)

        @pl.when(i == 0)
        def _():
            dwacc_ref[...] = jnp.zeros_like(dwacc_ref)
            dvec_ref[...] = jnp.zeros_like(dvec_ref)

        g, bb = lg_ref[...], lb_ref[...]

        def ln_bwd(dc, yp):
            mu = jnp.mean(yp, axis=-1, keepdims=True)
            d = yp - mu
            rs = lax.rsqrt(jnp.mean(d * d, axis=-1, keepdims=True) + EPS)
            yn = d * rs
            o = yn * g + bb
            sg = _fast_sigmoid(o)
            do = dc * (sg * (1.0 + o * (1.0 - sg)))
            dyn = do * g
            dyp = rs * (dyn - jnp.mean(dyn, axis=-1, keepdims=True)
                        - yn * jnp.mean(dyn * yn, axis=-1, keepdims=True))
            return dyp, do, yn

        dyp, do, yn = ln_bwd(dc_ref[...], yp_ref[...])
        dvec_ref[0:1, :] += jnp.sum(dyp, axis=0, keepdims=True)
        dvec_ref[1:2, :] += jnp.sum(do * yn, axis=0, keepdims=True)
        dvec_ref[2:3, :] += jnp.sum(do, axis=0, keepdims=True)
        dy_ref[0:tm] = dyp
        dyh, _, _ = ln_bwd(dcn_ref[...], ypn_ref[...])
        dy_ref[tm:] = jnp.where(i < nt - 1, dyh, 0.0)
        _shift_copies(dy_ref, dysh_ref, n - 8)
        _fill_z(zs_ref, zsh_ref, uc_ref, up_ref, i, CH, n)

        def chunk(ci, carry):
            c0 = pl.multiple_of(ci * CONV_ROWS, CONV_ROWS)
            acc = jnp.zeros((CONV_ROWS, CH), F32)
            for k in range(CONV_W):
                acc = acc + w_ref[k:k + 1, :] * _tap(dy_ref, dysh_ref, CONV_W - 1 - k, c0)
            dz_ref[pl.ds(c0, CONV_ROWS), :] = acc
            dyc = dy_ref[pl.ds(c0, CONV_ROWS), :]
            for k in range(CONV_W):
                prod = dyc * _tap(zs_ref, zsh_ref, HALO - (CONV_W - 1) + k, c0)
                dwacc_ref[k] += jnp.sum(prod.reshape(CONV_ROWS // 8, 8, CH), axis=0)
            return carry

        lax.fori_loop(0, nchunk, chunk, 0)

        @pl.when(i == nt - 1)
        def _():
            dw_ref[...] = jnp.sum(dwacc_ref[...], axis=1)

        uc = uc_ref[...]
        a = uc[:, :CH]
        sg = _fast_sigmoid(uc[:, CH:])
        dz = dz_ref[...]
        du_ref[:, :CH] = (dz * sg).astype(BF16)
        du_ref[:, CH:] = (dz * a * sg * (1.0 - sg)).astype(BF16)

    cur = lambda c: pl.BlockSpec((tm, c), lambda i: (i, 0))
    nxt = lambda c: pl.BlockSpec((HALO, c), lambda i: (jnp.minimum((i + 1) * hb, T // HALO - 1), 0))
    vec = pl.BlockSpec((1, CH), lambda i: (0, 0))
    return pl.pallas_call(
        body, name="conv_bwd", grid=(nt,),
        in_specs=[cur(CH), nxt(CH), cur(CH), nxt(CH), cur(2 * CH),
                  pl.BlockSpec((HALO, 2 * CH), lambda i: (jnp.maximum(i * hb - 1, 0), 0)),
                  pl.BlockSpec((CONV_W, CH), lambda i: (0, 0)), vec, vec],
        out_specs=[pl.BlockSpec((tm, 2 * CH), lambda i: (i, 0)), pl.BlockSpec((32, CH), lambda i: (0, 0)),
                   pl.BlockSpec((8, CH), lambda i: (0, 0))],
        out_shape=[jax.ShapeDtypeStruct((T, 2 * CH), BF16), jax.ShapeDtypeStruct((32, CH), F32),
                   jax.ShapeDtypeStruct((8, CH), F32)],
        scratch_shapes=[pltpu.VMEM((n, CH), F32), pltpu.VMEM((7, n - 8, CH), F32),
                        pltpu.VMEM((n, CH), F32), pltpu.VMEM((7, n - 8, CH), F32), pltpu.VMEM((tm, CH), F32),
                        pltpu.VMEM((32, 8, CH), F32)],
        compiler_params=_cp(1),
    )(dconv, dconv, ypre, ypre, u, u, w, lg, lb)


def _attn_bwd(sinks, tab, qkv, dattn):
    T = qkv.shape[0]
    nb = T // WINDOW

    def body(sink_ref, tab_ref, q_ref, kvp_ref, kvc_ref, do_ref, dq_ref, dkv_ref, dsk_ref, carry_ref):
        n = pl.program_id(0)

        @pl.when(n == 0)
        def _():
            dsk_ref[...] = jnp.zeros_like(dsk_ref)
            carry_ref[...] = jnp.zeros_like(carry_ref)

        @pl.when(n < nb)
        def _():
            seen = _first_block_mask(n)
            for g in range(N_KV):
                qs = _stack_heads(q_ref, g)
                dos = _stack_heads(do_ref, g)
                k = _band(kvp_ref, kvc_ref, g * HEAD_DIM)
                v = _band(kvp_ref, kvc_ref, KV_W + g * HEAD_DIM)
                p, ps = _attn_probs(qs, k, tab_ref[g], seen, _sink_col(sink_ref, g))
                dp = _dot_nt(dos, v)
                delta = jnp.sum(p * dp, axis=-1, keepdims=True)
                dsb = (p * (dp - delta)).astype(BF16)
                dsink = -ps * delta
                dqs = _dot(dsb, k) * SCALE
                dk = _dot_tn(dsb, qs) * SCALE
                dv = _dot_tn(p.astype(BF16), dos)
                for i in range(GROUP):
                    h = GROUP * g + i
                    dq_ref[:, h * HEAD_DIM:(h + 1) * HEAD_DIM] = dqs[i * WINDOW:(i + 1) * WINDOW].astype(BF16)
                    dsk_ref[h:h + 1, :] += jnp.sum(dsink[i * WINDOW:(i + 1) * WINDOW], axis=0, keepdims=True)
                for off, d in ((g * HEAD_DIM, dk), (KV_W + g * HEAD_DIM, dv)):
                    dkv_ref[:, off:off + HEAD_DIM] = (carry_ref[:, off:off + HEAD_DIM] + d[:WINDOW]).astype(BF16)
                    carry_ref[:, off:off + HEAD_DIM] = d[WINDOW:]

        @pl.when(n == nb)
        def _():
            dkv_ref[...] = carry_ref[...].astype(BF16)

    last = nb - 1
    return pl.pallas_call(
        body, name="attn_bwd", grid=(nb + 1,),
        in_specs=[pl.BlockSpec(memory_space=pltpu.SMEM),
                  pl.BlockSpec(tab.shape, lambda n: (0, 0, 0)),
                  pl.BlockSpec((WINDOW, ATTN_W), lambda n: (jnp.minimum(n, last), 0)),
                  pl.BlockSpec((WINDOW, 2 * KV_W), lambda n: (jnp.clip(n - 1, 0, last), 2)),
                  pl.BlockSpec((WINDOW, 2 * KV_W), lambda n: (jnp.minimum(n, last), 2)),
                  pl.BlockSpec((WINDOW, ATTN_W), lambda n: (jnp.minimum(n, last), 0))],
        out_specs=[pl.BlockSpec((WINDOW, ATTN_W), lambda n: (jnp.minimum(n, last), 0)),
                   pl.BlockSpec((WINDOW, 2 * KV_W), lambda n: (jnp.maximum(n - 1, 0), 0)),
                   pl.BlockSpec((8, LANES), lambda n: (0, 0))],
        out_shape=[jax.ShapeDtypeStruct((T, ATTN_W), BF16), jax.ShapeDtypeStruct((T, 2 * KV_W), BF16),
                   jax.ShapeDtypeStruct((8, LANES), F32)],
        scratch_shapes=[pltpu.VMEM((WINDOW, 2 * KV_W), F32)],
        compiler_params=_cp(1),
    )(sinks, tab, qkv, qkv, qkv, dattn)


def _pack(arrs):
    flat = jnp.concatenate([a.reshape(-1) for a in arrs])
    pad = -flat.shape[0] % (8 * LANES)
    return jnp.pad(flat, (0, pad)).reshape(1, -1, LANES)


def _unpack(packed, like):
    flat = packed.reshape(-1)
    out, off = [], 0
    for a in like:
        out.append(flat[off:off + a.size].reshape(a.shape))
        off += a.size
    return out


def kernel(x, norm_ffn1, w_ffn1_in, w_ffn1_out, norm_mix, w_in, sinks, w_dw, b_dw, conv_ln_g, conv_ln_b, w_out, norm_ffn2, w_ffn2_in, w_ffn2_out, final_norm, loss_target, m_norm_ffn1, m_w_ffn1_in, m_w_ffn1_out, m_norm_mix, m_w_in, m_sinks, m_w_dw, m_b_dw, m_conv_ln_g, m_conv_ln_b, m_w_out, m_norm_ffn2, m_w_ffn2_in, m_w_ffn2_out, m_final_norm, v_norm_ffn1, v_w_ffn1_in, v_w_ffn1_out, v_norm_mix, v_w_in, v_sinks, v_w_dw, v_b_dw, v_conv_ln_g, v_conv_ln_b, v_w_out, v_norm_ffn2, v_w_ffn2_in, v_w_ffn2_out, v_final_norm):
    L, D = norm_ffn1.shape
    T = x.shape[1]
    FB = w_ffn1_in.shape[2]
    CH = b_dw.shape[1]
    QKV = ATTN_W + 2 * KV_W
    xs = x.reshape(T, D)
    tgt = loss_target.reshape(T, D)
    cx, cy, cc = lax.axis_index("x"), lax.axis_index("y"), lax.axis_index("c")
    chip = 2 * cx + cy
    cidx = cc.reshape(1).astype(jnp.int32)
    tr = lambda a_: jnp.transpose(a_, (0, 2, 1))
    big_w = (w_ffn1_in, w_ffn1_out, tr(w_in), w_out, w_ffn2_in, w_ffn2_out)
    big_m = (m_w_ffn1_in, m_w_ffn1_out, tr(m_w_in), m_w_out, m_w_ffn2_in, m_w_ffn2_out)
    big_v = (v_w_ffn1_in, v_w_ffn1_out, tr(v_w_in), v_w_out, v_w_ffn2_in, v_w_ffn2_out)
    NW = len(big_w) + 1

    def shards(l, tok):
        return [(w_[l] + tok[0, 0]).astype(BF16) for w_ in big_w] + [w_dw[l] + tok[0, 0]]

    def own_slot(a, slots=4, idx=chip):
        return lax.dynamic_update_index_in_dim(lax.empty((slots,) + a.shape, a.dtype), a, idx, 0)

    def gather_start(srcs, tok):
        return _xchg_start("gather_start", srcs, [own_slot(s_) for s_ in srcs], _gather_plan, tok)

    def gather_arrived(started, after, n, taps):
        _, lands, tok = _xchg_wait("gather_wait", started, n, n, _gather_plan, after)
        return _xchg_start("gshare_start", [], lands[:-1] if taps else lands, _gshare_plan, tok, "sibling3"), lands[-1]

    def shared_weights(shared, after, n):
        _, mats, tok = _xchg_wait("gshare_wait", shared, 0, n, _gshare_plan, after, "sibling3")
        return mats, tok

    row = lambda a, l: a[l].reshape(1, -1)
    tab = _attn_bias_table()
    NB = len(big_w)

    saved, W = [], []
    zero_tok = jnp.zeros((8, LANES), F32)
    src0 = shards(0, zero_tok)
    started = gather_start(src0[:2], zero_tok)
    rest0 = gather_start(src0[2:], started[-1])
    cast = [None] + [shards(l, rest0[-1]) for l in range(1, L)]
    shared, _ = gather_arrived(started, [xs] + [a_ for c_ in cast[1:] for a_ in c_], 2, False)
    after = [shared[-1]]
    for l in range(L):
        mats, tok = shared_weights(shared, after, 2 if l == 0 else NB)
        started = None
        if l + 1 < L:
            started = gather_start(cast[l + 1], tok)
            tok = started[-1]
        x0 = xs
        x1, gu1 = _ffn_fwd(x0, row(norm_ffn1, l) + tok[0, 0], mats[0], mats[1].reshape(2 * FB, D))
        gm_row = row(norm_mix, l)
        if l == 0:
            shared, gdw = gather_arrived(rest0, [x1], NW - 2, True)
            rest, tok = shared_weights(shared, [shared[-1]], NB - 2)
            mats = list(mats) + list(rest)
            gm_row = gm_row + tok[0, 0]
        g1i, g1o, gi, go, g2i, g2o = mats
        w = dict(f1i=g1i, f1o=g1o.reshape(2 * FB, D), f2i=g2i, f2o=g2o.reshape(2 * FB, D),
                 wit=gi.reshape(-1, D), wo=go.reshape(-1, D),
                 wdw=jnp.transpose(gdw, (1, 0, 2)).reshape(CONV_W, CH))
        W.append(w)
        qkv, u = _mixproj_fwd(x1, gm_row, w["wit"])
        attn = _attn_fwd(row(sinks, l), tab, qkv)
        conv, ypre = _conv_fwd(u, w["wdw"], row(b_dw, l), row(conv_ln_g, l), row(conv_ln_b, l))
        x2 = _mixout_fwd(x1, attn, conv, w["wo"])
        g2_row = row(norm_ffn2, l)
        if started is not None:
            shared, gdw = gather_arrived(started, [x2], NW, True)
            g2_row = g2_row + shared[-1][0, 0]
        xs, gu2 = _ffn_fwd(x2, g2_row, w["f2i"], w["f2o"])
        saved.append((x0, gu1, x1, qkv, u, attn, conv, ypre, x2, gu2))
        after = [xs]

    loss_part, dx, d_final = _loss_head(xs, final_norm.reshape(1, D), tgt)
    loss = lax.psum(loss_part[0, 0], ("x", "y", "c"))

    bufs = [[lax.empty(w_.shape, F32) for _ in range(4)] for w_ in big_w]
    d_n1, d_nm, d_n2 = [None] * L, [None] * L, [None] * L
    d_sk, d_bdw, d_lg, d_lb, d_wdw = [None] * L, [None] * L, [None] * L, [None] * L, [None] * L

    def sib_start(gs):
        return _xchg_start("sib_start", gs, [lax.empty((4, g.shape[1] // 2, g.shape[2]), F32) for g in gs],
                           _sib_plan, zero_tok, "sibling")

    def reduce_start(sib_started, after, n):
        gs, sibs, _ = _xchg_wait("sib_wait", sib_started, n, n, _sib_plan, after, "sibling")
        parts = [_sum_halves(cidx, g, s_) for g, s_ in zip(gs, sibs)]
        lands = [own_slot(lax.dynamic_index_in_dim(p, chip, 0, keepdims=False)) for p in parts]
        return _xchg_start("rs_start", parts, lands, _rs_plan, zero_tok)

    def share_start(rs_started, after, n):
        _, qs, tok = _xchg_wait("rs_wait", rs_started, n, n, _rs_plan, after)
        return _xchg_start("qshare_start", qs, [lax.empty(q.shape, q.dtype) for q in qs], _whole_plan, tok, "sibling")

    def finish(l, shared, after, idxs):
        q_own, q_sib, _ = _xchg_wait("qshare_wait", shared, len(idxs), len(idxs), _whole_plan, after, "sibling")
        for k, t in enumerate(idxs):
            bufs[t] = _adamw_layer(cidx, q_own[k], q_sib[k], big_w[t], big_m[t], big_v[t], bufs[t], l)

    ALL = list(range(NB))
    EARLY, LATE = ALL[2:], ALL[:2]
    sib_pending = rs_pending = None
    shares = []
    tok = zero_tok
    for l in reversed(range(L)):
        w = W[l]
        x0, gu1, x1, qkv, u, attn, conv, ypre, x2, gu2 = saved[l]
        dx, d_n2[l], hb, dgu, a, dyb = _ffn_bwd(dx, x2, row(norm_ffn2, l), gu2, w["f2i"], w["f2o"], tok)
        g_f2i, g_f2o = _wgrad_ffn_in(hb, dgu, tok), _wgrad_ffn_out(a, dyb, tok)
        lg_row = row(conv_ln_g, l)
        if sib_pending is not None:
            rs_started = reduce_start(sib_pending[1], [g_f2o], NB)
            if rs_pending is not None:
                shares.append((rs_pending[0], share_start(rs_pending[1], [rs_started[-1]], NB)))
            rs_pending = (sib_pending[0], rs_started)
            lg_row = lg_row + rs_started[-1][0, 0]
        dyb, dattn, dconv = _mixout_bwd(dx, w["wo"])
        g_wo = _wgrad_cat([attn, conv], [dyb]).reshape(4, -1, D)
        du, dwdw, dvec = _conv_bwd(dconv, ypre, u, w["wdw"], lg_row, row(conv_ln_b, l))
        d_wdw[l], d_bdw[l], d_lg[l], d_lb[l] = dwdw[:CONV_W], dvec[0], dvec[1], dvec[2]
        dq, dkv, dsk = _attn_bwd(row(sinks, l), tab, qkv, dattn)
        d_sk[l] = dsk[:, 0]
        wit = w["wit"]
        dx, d_nm[l], hb = _mix_rms_bwd(dx, x1, row(norm_mix, l), [dq, dkv, du],
                                       [wit[:ATTN_W], wit[ATTN_W:QKV], wit[QKV:]])
        g_wi = _wgrad_cat([dq, dkv, du], [hb]).reshape(4, -1, D)
        if l == 0:
            sib_early = sib_start([g_wi, g_wo, g_f2i, g_f2o])
            tok = sib_early[-1]
        dx, d_n1[l], hb, dgu, a, dyb = _ffn_bwd(dx, x0, row(norm_ffn1, l), gu1, w["f1i"], w["f1o"], tok)
        if l == 0:
            rs_early = reduce_start(sib_early, [dx], len(EARLY))
            tok = rs_early[-1]
        g_f1i, g_f1o = _wgrad_ffn_in(hb, dgu, tok), _wgrad_ffn_out(a, dyb, tok)
        sib_started = sib_start([g_f1i, g_f1o] if l == 0 else [g_f1i, g_f1o, g_wi, g_wo, g_f2i, g_f2o])
        tok = sib_started[-1]
        sib_pending = (l, sib_started)
    grad_x = dx.reshape(x.shape)

    small_g = [jnp.concatenate(d, axis=0) for d in (d_n1, d_nm, d_n2)] + [d_final, jnp.stack(d_sk)] + \
              [jnp.stack(d) for d in (d_bdw, d_lg, d_lb, d_wdw)]
    packed = _pack(small_g)[0]
    small_started = _xchg_start("small_start", [packed], [own_slot(packed, 8, 4 * cx + 2 * cy + cc)], _slot_plan, tok, "all")

    after = [small_started[-1]]
    if rs_pending is not None:
        shares.append((rs_pending[0], share_start(rs_pending[1], after, NB)))
        after = [shares[-1][1][-1]]
    if shares:
        finish(*shares.pop(0), after, ALL)
        after = [b_[0] for b_ in bufs]
    rs_late = reduce_start(sib_pending[1], after, len(LATE))
    after = [rs_late[-1]]
    for l, sh in shares:
        finish(l, sh, after, ALL)
        after = [b_[0] for b_ in bufs]
    _, (slots,), _ = _xchg_wait("small_wait", small_started, 1, 1, _slot_plan, after, "all")
    small_sum = _unpack(_sum_slots(slots), small_g)
    g_wdw = lax.dynamic_slice_in_dim(small_sum[8], chip * w_dw.shape[2], w_dw.shape[2], axis=2)
    small_g = [small_sum[0], small_sum[1], small_sum[2], small_sum[3].reshape(D), small_sum[4],
               small_sum[5], small_sum[6], small_sum[7], g_wdw]
    small_w = (norm_ffn1, norm_mix, norm_ffn2, final_norm, sinks, b_dw, conv_ln_g, conv_ln_b, w_dw)
    small_m = (m_norm_ffn1, m_norm_mix, m_norm_ffn2, m_final_norm, m_sinks, m_b_dw, m_conv_ln_g, m_conv_ln_b, m_w_dw)
    small_v = (v_norm_ffn1, v_norm_mix, v_norm_ffn2, v_final_norm, v_sinks, v_b_dw, v_conv_ln_g, v_conv_ln_b, v_w_dw)
    upd = _adamw(_pack(small_g), _pack(small_w), _pack(small_m), _pack(small_v))
    small_upd = [_unpack(u_, small_w) for u_ in upd]
    sh_early = share_start(rs_early, [upd[0]], len(EARLY))
    sh_late = share_start(rs_late, [sh_early[-1]], len(LATE))
    finish(0, sh_early, [sh_late[-1]], EARLY)
    finish(0, sh_late, [bufs[t][0] for t in EARLY], LATE)

    order = ("norm_ffn1", "w_ffn1_in", "w_ffn1_out", "norm_mix", "w_in", "sinks", "w_dw", "b_dw", "conv_ln_g",
             "conv_ln_b", "w_out", "norm_ffn2", "w_ffn2_in", "w_ffn2_out", "final_norm")
    small_names = ("norm_ffn1", "norm_mix", "norm_ffn2", "final_norm", "sinks", "b_dw", "conv_ln_g", "conv_ln_b", "w_dw")
    big_names = ("w_ffn1_in", "w_ffn1_out", "w_in", "w_out", "w_ffn2_in", "w_ffn2_out")
    grads, deltas, new_m, new_v = {}, {}, {}, {}
    for i, nme in enumerate(small_names):
        grads[nme], deltas[nme], new_m[nme], new_v[nme] = small_g[i], small_upd[0][i], small_upd[1][i], small_upd[2][i]
    for i, nme in enumerate(big_names):
        grads[nme], deltas[nme], new_m[nme], new_v[nme] = [tr(b_) for b_ in bufs[i]] if nme == "w_in" else bufs[i]
    return (loss, grad_x, *[grads[n] for n in order], *[deltas[n] for n in order],
            *[new_m[n] for n in order], *[new_v[n] for n in order])
```

```python
import functools

import jax
import jax.numpy as jnp
from jax import lax
from jax.experimental import pallas as pl
from jax.experimental.pallas import tpu as pltpu

F32, BF16 = jnp.float32, jnp.bfloat16
EPS = 1e-6
NEG_INF = -1e30
HEAD_DIM = 64
N_HEADS = 8
N_KV = 2
GROUP = N_HEADS // N_KV
WINDOW = 128
ATTN_W = N_HEADS * HEAD_DIM
KV_W = N_KV * HEAD_DIM
CONV_W = 31
HALO = 32
CONV_ROWS = 32
SCALE = 1.0 / 8.0
ADAM_LR, ADAM_B1, ADAM_B2, ADAM_EPS, ADAM_WD, ADAM_STEP = 0.001, 0.9, 0.999, 1e-08, 0.01, 10
TM = 512
TM_FFN_BWD = 256
TK_WGRAD = 2048
TM_MIX = 1024
LANES = 128
VMEM_LIMIT = 52 * 1024 * 1024
MESH = pl.DeviceIdType.MESH
ANY = pl.BlockSpec(memory_space=pl.ANY)
HBM = pl.BlockSpec(memory_space=pltpu.HBM)
SEM = pl.BlockSpec(memory_space=pltpu.SEMAPHORE)
VMEM = pl.BlockSpec(memory_space=pltpu.VMEM)
EFFECT = pltpu.SideEffectType.DATAFLOW_SIDE_EFFECTING
TOKEN = jax.ShapeDtypeStruct((8, LANES), F32)


def _cp(n):
    return pltpu.CompilerParams(dimension_semantics=("arbitrary",) * n, vmem_limit_bytes=VMEM_LIMIT)


def _dot(a, b):
    return jnp.dot(a, b, preferred_element_type=F32)


def _dot_nt(a, b):
    return lax.dot_general(a, b, (((1,), (1,)), ((), ())), preferred_element_type=F32)


def _dot_tn(a, b):
    return lax.dot_general(a, b, (((0,), (0,)), ((), ())), preferred_element_type=F32)


def _place():
    x, y, c = lax.axis_index("x"), lax.axis_index("y"), lax.axis_index("c")
    chips = [(1 - x, y), (x, 1 - y), (1 - x, 1 - y)]
    return x, y, c, chips


def _rcopy(src, dst, send_sems, recv_sems, k, dev):
    return pltpu.make_async_remote_copy(src_ref=src, dst_ref=dst, send_sem=send_sems.at[k],
                                        recv_sem=recv_sems.at[k], device_id=dev, device_id_type=MESH)


def _hbm(a):
    return pltpu.with_memory_space_constraint(a, pltpu.HBM)


PEERS = {"chips": 3, "sibling": 1, "sibling3": 3, "all": 7}


def _targets(mode):
    x, y, c, chips = _place()
    b = 2 * x + y
    if mode == "chips":
        return b, c, [((px, py, c), 2 * px + py) for px, py in chips]
    if mode == "sibling":
        return b, c, [((x, y, 1 - c), b)]
    if mode == "sibling3":
        return b, c, [((x, y, 1 - c), 2 * px + py) for px, py in chips]
    flip = lambda v, f: 1 - v if f else v
    devs = [(flip(x, k >> 2 & 1), flip(y, k >> 1 & 1), flip(c, k & 1)) for k in range(1, 8)]
    return 4 * x + 2 * y + c, c, [(d, 4 * d[0] + 2 * d[1] + d[2]) for d in devs]


def _xchg_start(name, srcs, lands, plan, dep, mode="chips"):
    ns, nl, npeer = len(srcs), len(lands), PEERS[mode]

    def body(*refs):
        land = refs[ns:ns + nl]
        src = refs[:ns] if ns else land
        send_sems, recv_sems, token = refs[ns + nl + 1], refs[ns + nl + 2], refs[-1]
        me, c, peers = _targets(mode)
        for t in range(nl):
            for j, (dev, tag) in enumerate(peers):
                s, d, _ = plan(src[t], land[t], t, me, c, tag)
                _rcopy(s, d, send_sems, recv_sems, npeer * t + j, dev).start()
        token[...] = jnp.zeros_like(token)

    arrs = list(srcs) + list(lands)
    return pl.pallas_call(
        body, name=name,
        out_shape=(pltpu.SemaphoreType.DMA((npeer * nl,)), pltpu.SemaphoreType.DMA((npeer * nl,)),
                   *[pltpu.HBM(a.shape, a.dtype) for a in arrs], TOKEN),
        in_specs=[HBM] * (ns + nl) + [ANY], out_specs=(SEM, SEM, *[HBM] * (ns + nl), VMEM),
        input_output_aliases={i: 2 + i for i in range(ns + nl)},
        compiler_params=pltpu.CompilerParams(has_side_effects=EFFECT),
    )(*[_hbm(a) for a in arrs], dep)


def _xchg_wait(name, started, ns, nl, plan, after, mode="chips"):
    send_sems, recv_sems, thru = started[0], started[1], started[2:2 + ns + nl]
    npeer = PEERS[mode]

    def body(*refs):
        land = refs[ns:ns + nl]
        src = refs[:ns] if ns else land
        send_sems, recv_sems, token = refs[ns + nl], refs[ns + nl + 1], refs[-1]
        me, c, peers = _targets(mode)
        for t in range(nl):
            for j, (dev, tag) in enumerate(peers):
                s, _, a = plan(src[t], land[t], t, me, c, tag)
                cp = _rcopy(s, a, send_sems, recv_sems, npeer * t + j, dev)
                cp.wait_send()
                cp.wait_recv()
        token[...] = jnp.zeros_like(token)

    out = pl.pallas_call(
        body, name=name,
        out_shape=(*[pltpu.HBM(a.shape, a.dtype) for a in thru], TOKEN),
        in_specs=[HBM] * (ns + nl) + [SEM, SEM] + [ANY] * len(after), out_specs=(*[HBM] * (ns + nl), VMEM),
        input_output_aliases={i: i for i in range(ns + nl)},
        compiler_params=pltpu.CompilerParams(has_side_effects=EFFECT),
    )(*thru, send_sems, recv_sems, *after)
    return out[:ns], out[ns:ns + nl], out[-1]


def _half(ref_rows, which):
    h = ref_rows // 2
    return pl.ds(which * h, h)


def _gather_plan(src, land, t, b, c, pb):
    if len(src.shape) == 2 and src.shape[0] % 2 == 0:
        hs = _half(src.shape[0], c)
        return src.at[hs], land.at[b, hs], land.at[pb, hs]
    return src, land.at[b], land.at[pb]


def _gshare_plan(src, land, t, b, c, pb):
    return land.at[pb, _half(land.shape[1], c)], land.at[pb, _half(land.shape[1], c)], land.at[pb, _half(land.shape[1], 1 - c)]


def _rs_plan(src, land, t, b, c, pb):
    return src.at[pb], land.at[b], land.at[pb]


def _sib_plan(src, land, t, b, c, pb):
    return src.at[:, _half(src.shape[1], 1 - c), :], land, land


def _rows_block(h, cap=512):
    for rb in range(min(h, cap) // 16 * 16, 0, -16):
        if h % rb == 0:
            return rb
    return h


def _sum_halves(cidx, g, s):
    _, R, C = g.shape
    rb = _rows_block(R // 2, 256)
    nr = R // 2 // rb

    def body(c_ref, g_ref, s_ref, o_ref):
        o_ref[...] = (g_ref[...] + s_ref[...]).astype(BF16)

    blk = (None, rb, C)
    return pl.pallas_call(
        body, name="sum_halves", out_shape=jax.ShapeDtypeStruct(s.shape, BF16),
        grid_spec=pltpu.PrefetchScalarGridSpec(
            num_scalar_prefetch=1, grid=(4, nr),
            in_specs=[pl.BlockSpec(blk, lambda p, i, c: (p, c[0] * nr + i, 0)),
                      pl.BlockSpec(blk, lambda p, i, c: (p, i, 0))],
            out_specs=pl.BlockSpec(blk, lambda p, i, c: (p, i, 0))),
        compiler_params=_cp(2),
    )(cidx, g, s)


def _whole_plan(src, land, t, me, c, tag):
    return src, land, land


def _slot_plan(src, land, t, me, c, tag):
    return src, land.at[me], land.at[tag]


def _adam_update(gg, w, m, v):
    m2 = ADAM_B1 * m + (1.0 - ADAM_B1) * gg
    v2 = ADAM_B2 * v + (1.0 - ADAM_B2) * (gg * gg)
    mh = m2 / (1.0 - ADAM_B1 ** ADAM_STEP)
    vh = v2 / (1.0 - ADAM_B2 ** ADAM_STEP)
    return -ADAM_LR * (mh / (jnp.sqrt(vh) + ADAM_EPS) + ADAM_WD * w), m2, v2


def _adamw_layer(cidx, q_own, q_sib, w, m, v, bufs, l):
    L, R, C = w.shape
    h = R // 2
    rb = _rows_block(h, 128)
    nr = h // rb

    def body(c_ref, qo_ref, qs_ref, w_ref, m_ref, v_ref, *rest):
        g_ref, d_ref, mo_ref, vo_ref = rest[-4:]
        own = pl.program_id(0) == c_ref[0]
        gg = jnp.zeros((rb, C), F32)
        for s in range(4):
            gg = gg + jnp.where(own, qo_ref[s], qs_ref[s]).astype(F32)
        g_ref[...] = gg
        d_ref[...], mo_ref[...], vo_ref[...] = _adam_update(gg, w_ref[...], m_ref[...], v_ref[...])

    q_own_spec = pl.BlockSpec((4, rb, C), lambda hh, i, c: (0, jnp.where(hh == c[0], i, 0), 0))
    q_sib_spec = pl.BlockSpec((4, rb, C), lambda hh, i, c: (0, jnp.where(hh == c[0], 0, i), 0))
    wspec = pl.BlockSpec((None, rb, C), lambda hh, i, c: (l, hh * nr + i, 0))
    return pl.pallas_call(
        body, name="adamw_layer", out_shape=[jax.ShapeDtypeStruct(w.shape, F32)] * 4,
        grid_spec=pltpu.PrefetchScalarGridSpec(
            num_scalar_prefetch=1, grid=(2, nr),
            in_specs=[q_own_spec, q_sib_spec, wspec, wspec, wspec] + [ANY] * 4, out_specs=[wspec] * 4),
        input_output_aliases={6 + k: k for k in range(4)},
        compiler_params=_cp(2),
    )(cidx, q_own, q_sib, w, m, v, *bufs)


def _adamw(g, w, m, v):
    L, R, C = g.shape
    rb = _rows_block(R)

    def body(g_ref, w_ref, m_ref, v_ref, d_ref, mo_ref, vo_ref):
        d_ref[...], mo_ref[...], vo_ref[...] = _adam_update(g_ref[...], w_ref[...], m_ref[...], v_ref[...])

    spec = pl.BlockSpec((None, rb, C), lambda l, i: (l, i, 0))
    return pl.pallas_call(
        body, name="adamw", grid=(L, R // rb), in_specs=[spec] * 4, out_specs=[spec] * 3,
        out_shape=[jax.ShapeDtypeStruct(g.shape, F32)] * 3, compiler_params=_cp(2),
    )(g, w, m, v)


def _sum_slots(buf):
    def body(b_ref, o_ref):
        acc = b_ref[0]
        for k in range(1, 8):
            acc = acc + b_ref[k]
        o_ref[...] = acc

    return pl.pallas_call(body, name="sum_slots", in_specs=[VMEM], out_specs=VMEM,
                          out_shape=jax.ShapeDtypeStruct(buf.shape[1:], F32))(buf)


def _rms(xf, g):
    r = lax.rsqrt(jnp.mean(xf * xf, axis=-1, keepdims=True) + EPS)
    return xf * r, r


def _lane_chunks(n):
    lo = (n // LANES + 1) // 2 * LANES
    return ((0, lo), (lo, n - lo))


def _load_ffn_weights(win_hbm, wout_hbm, win_v, wout_v, sems):
    fb = win_v.shape[2]
    loads = [pltpu.make_async_copy(win_hbm.at[k], win_v.at[k], sems.at[k]) for k in range(4)]
    loads += [pltpu.make_async_copy(wout_hbm.at[pl.ds(k * fb, fb)], wout_v.at[pl.ds(k * fb, fb)], sems.at[4 + k])
              for k in range(2)]
    for cp in loads:
        cp.start()
    for cp in loads:
        cp.wait()


def _fast_sigmoid(v):
    return pl.reciprocal(1.0 + jnp.exp(-v), approx=True)


def _ffn_fwd(x, g, win, wout):
    T, D = x.shape
    FB = win.shape[2]
    tm = min(TM, T)

    def body(x_ref, g_ref, win_hbm, wout_hbm, xo_ref, gu_ref, win_v, wout_v, sems):
        @pl.when(pl.program_id(0) == 0)
        def _():
            _load_ffn_weights(win_hbm, wout_hbm, win_v, wout_v, sems)

        xf = x_ref[...]
        xh, _ = _rms(xf, None)
        h = (xh * g_ref[...]).astype(BF16)
        acc = jnp.zeros((tm, D), F32)
        for blk in range(2):
            for lo, sz in _lane_chunks(FB):
                cols = pl.ds(blk * FB + lo, sz)
                gate = _dot(h, win_v[blk, :, pl.ds(lo, sz)])
                up = _dot(h, win_v[2 + blk, :, pl.ds(lo, sz)])
                gu_ref[0, :, cols] = gate.astype(BF16)
                gu_ref[1, :, cols] = up.astype(BF16)
                a = (gate * _fast_sigmoid(gate) * up).astype(BF16)
                acc = acc + _dot(a, wout_v[cols, :])
        xo_ref[...] = xf + 0.5 * acc

    row = pl.BlockSpec((tm, D), lambda i: (i, 0))
    return pl.pallas_call(
        body, name="ffn_fwd", grid=(T // tm,),
        in_specs=[row, pl.BlockSpec((1, D), lambda i: (0, 0)), ANY, ANY],
        out_specs=[row, pl.BlockSpec((2, tm, 2 * FB), lambda i: (0, i, 0))],
        out_shape=[jax.ShapeDtypeStruct((T, D), F32), jax.ShapeDtypeStruct((2, T, 2 * FB), BF16)],
        scratch_shapes=[pltpu.VMEM(win.shape, BF16), pltpu.VMEM(wout.shape, BF16), pltpu.SemaphoreType.DMA((6,))],
        compiler_params=_cp(1),
    )(x, g, win, wout)


def _mixproj_fwd(x, g, wt):
    T, D = x.shape
    W = wt.shape[0]
    QKV = ATTN_W + 2 * KV_W
    tm = min(TM_MIX, T)

    def body(x_ref, g_ref, w_ref, qkv_ref, u_ref):
        xh, _ = _rms(x_ref[...], None)
        h = (xh * g_ref[...]).astype(BF16)
        qkv_ref[...] = _dot_nt(h, w_ref[:QKV, :]).astype(BF16)
        u_ref[...] = _dot_nt(h, w_ref[QKV:, :])

    return pl.pallas_call(
        body, name="mixproj_fwd", grid=(T // tm,),
        in_specs=[pl.BlockSpec((tm, D), lambda i: (i, 0)), pl.BlockSpec((1, D), lambda i: (0, 0)),
                  pl.BlockSpec((W, D), lambda i: (0, 0))],
        out_specs=[pl.BlockSpec((tm, QKV), lambda i: (i, 0)), pl.BlockSpec((tm, W - QKV), lambda i: (i, 0))],
        out_shape=[jax.ShapeDtypeStruct((T, QKV), BF16), jax.ShapeDtypeStruct((T, W - QKV), F32)],
        compiler_params=_cp(1),
    )(x, g, wt)


def _attn_bias_table():
    rows, cols = GROUP * WINDOW, 2 * WINDOW
    row = lax.broadcasted_iota(jnp.int32, (N_KV, rows, cols), 1)
    col = lax.broadcasted_iota(jnp.int32, (N_KV, rows, cols), 2)
    head = GROUP * lax.broadcasted_iota(jnp.int32, (N_KV, rows, cols), 0) + (row >> 7)
    dist = (row & (WINDOW - 1)) + WINDOW - col
    slope = jnp.exp2(-(head + 1).astype(F32))
    return jnp.where((dist >= 0) & (dist < WINDOW), -slope * dist.astype(F32), NEG_INF)


def _first_block_mask(n):
    col = lax.broadcasted_iota(jnp.int32, (GROUP * WINDOW, 2 * WINDOW), 1)
    return (n > 0) | (col >= WINDOW)


def _sink_col(sink_ref, g):
    hi = lax.broadcasted_iota(jnp.int32, (GROUP * WINDOW, 1), 0) >> 7
    col = jnp.zeros((GROUP * WINDOW, 1), F32)
    for i in range(GROUP):
        col = jnp.where(hi == i, sink_ref[0, GROUP * g + i], col)
    return col


def _stack_heads(ref, g):
    return jnp.concatenate([ref[:, (GROUP * g + i) * HEAD_DIM:(GROUP * g + i + 1) * HEAD_DIM]
                            for i in range(GROUP)], axis=0)


def _band(kvp_ref, kvc_ref, off):
    return jnp.concatenate([kvp_ref[:, off:off + HEAD_DIM], kvc_ref[:, off:off + HEAD_DIM]], axis=0)


def _attn_probs(qs, k, bias, seen, sink):
    s = jnp.where(seen, _dot_nt(qs, k) * SCALE + bias, NEG_INF)
    m = jnp.maximum(jnp.max(s, axis=-1, keepdims=True), sink)
    p = jnp.exp(s - m)
    es = jnp.exp(sink - m)
    inv = 1.0 / (jnp.sum(p, axis=-1, keepdims=True) + es)
    return p * inv, es * inv


def _attn_fwd(sinks, tab, qkv):
    T = qkv.shape[0]
    nb = T // WINDOW

    def body(sink_ref, tab_ref, q_ref, kvp_ref, kvc_ref, o_ref):
        seen = _first_block_mask(pl.program_id(0))
        for g in range(N_KV):
            qs = _stack_heads(q_ref, g)
            k = _band(kvp_ref, kvc_ref, g * HEAD_DIM)
            v = _band(kvp_ref, kvc_ref, KV_W + g * HEAD_DIM)
            p, _ = _attn_probs(qs, k, tab_ref[g], seen, _sink_col(sink_ref, g))
            o = _dot(p.astype(BF16), v)
            for i in range(GROUP):
                h = GROUP * g + i
                o_ref[:, h * HEAD_DIM:(h + 1) * HEAD_DIM] = o[i * WINDOW:(i + 1) * WINDOW].astype(BF16)

    return pl.pallas_call(
        body, name="attn_fwd", grid=(nb,),
        in_specs=[pl.BlockSpec(memory_space=pltpu.SMEM),
                  pl.BlockSpec(tab.shape, lambda n: (0, 0, 0)),
                  pl.BlockSpec((WINDOW, ATTN_W), lambda n: (n, 0)),
                  pl.BlockSpec((WINDOW, 2 * KV_W), lambda n: (jnp.maximum(n - 1, 0), 2)),
                  pl.BlockSpec((WINDOW, 2 * KV_W), lambda n: (n, 2))],
        out_specs=pl.BlockSpec((WINDOW, ATTN_W), lambda n: (n, 0)),
        out_shape=jax.ShapeDtypeStruct((T, ATTN_W), BF16),
        compiler_params=_cp(1),
    )(sinks, tab, qkv, qkv, qkv)


def _shift_copies(src_ref, dst_ref, n):
    for b in range(1, 8):
        dst_ref[b - 1] = src_ref[b:b + n, :]


def _tap(src_ref, sh_ref, s, c0):
    a, b = divmod(s, 8)
    start = pl.multiple_of(c0 + 8 * a, 8)
    if b == 0:
        return src_ref[pl.ds(start, CONV_ROWS), :]
    return sh_ref[b - 1, pl.ds(start, CONV_ROWS), :]


def _glu_rows(u, ch):
    return u[:, :ch] * _fast_sigmoid(u[:, ch:])


def _fill_z(zs_ref, zsh_ref, uc_ref, up_ref, i, ch, n):
    zs_ref[0:HALO] = jnp.where(i > 0, _glu_rows(up_ref[...], ch), 0.0)
    zs_ref[HALO:] = _glu_rows(uc_ref[...], ch)
    _shift_copies(zs_ref, zsh_ref, n - 8)


def _conv_fwd(u, w, b, lg, lb):
    T = u.shape[0]
    CH = u.shape[1] // 2
    tm = min(TM, T)
    n = tm + HALO
    hb = tm // HALO

    def body(uc_ref, up_ref, w_ref, b_ref, lg_ref, lb_ref, conv_ref, ypre_ref, zs_ref, zsh_ref):
        i = pl.program_id(0)
        _fill_z(zs_ref, zsh_ref, uc_ref, up_ref, i, CH, n)
        bias = b_ref[...]

        def chunk(ci, carry):
            c0 = pl.multiple_of(ci * CONV_ROWS, CONV_ROWS)
            acc = jnp.broadcast_to(bias, (CONV_ROWS, CH))
            for k in range(CONV_W):
                acc = acc + w_ref[k:k + 1, :] * _tap(zs_ref, zsh_ref, HALO - (CONV_W - 1) + k, c0)
            ypre_ref[pl.ds(c0, CONV_ROWS), :] = acc
            return carry

        lax.fori_loop(0, tm // CONV_ROWS, chunk, 0)
        y = ypre_ref[...]
        mu = jnp.mean(y, axis=-1, keepdims=True)
        d = y - mu
        var = jnp.mean(d * d, axis=-1, keepdims=True)
        o = d * lax.rsqrt(var + EPS) * lg_ref[...] + lb_ref[...]
        conv_ref[...] = (o * _fast_sigmoid(o)).astype(BF16)

    vec = pl.BlockSpec((1, CH), lambda i: (0, 0))
    return pl.pallas_call(
        body, name="conv_fwd", grid=(T // tm,),
        in_specs=[pl.BlockSpec((tm, 2 * CH), lambda i: (i, 0)),
                  pl.BlockSpec((HALO, 2 * CH), lambda i: (jnp.maximum(i * hb - 1, 0), 0)),
                  pl.BlockSpec((CONV_W, CH), lambda i: (0, 0)), vec, vec, vec],
        out_specs=[pl.BlockSpec((tm, CH), lambda i: (i, 0)), pl.BlockSpec((tm, CH), lambda i: (i, 0))],
        out_shape=[jax.ShapeDtypeStruct((T, CH), BF16), jax.ShapeDtypeStruct((T, CH), F32)],
        scratch_shapes=[pltpu.VMEM((n, CH), F32), pltpu.VMEM((7, n - 8, CH), F32)],
        compiler_params=_cp(1),
    )(u, u, w, b, lg, lb)


def _mixout_fwd(x, attn, conv, wo):
    T, D = x.shape
    tm = min(TM_MIX, T)
    A = attn.shape[1]

    def body(x_ref, a_ref, c_ref, w_ref, xo_ref):
        xo_ref[...] = x_ref[...] + _dot(a_ref[...], w_ref[:A, :]) + _dot(c_ref[...], w_ref[A:, :])

    return pl.pallas_call(
        body, name="mixout_fwd", grid=(T // tm,),
        in_specs=[pl.BlockSpec((tm, D), lambda i: (i, 0)), pl.BlockSpec((tm, A), lambda i: (i, 0)),
                  pl.BlockSpec((tm, conv.shape[1]), lambda i: (i, 0)), pl.BlockSpec(wo.shape, lambda i: (0, 0))],
        out_specs=pl.BlockSpec((tm, D), lambda i: (i, 0)),
        out_shape=jax.ShapeDtypeStruct((T, D), F32),
        compiler_params=_cp(1),
    )(x, attn, conv, wo)


def _rms_bwd_rows(dh, xf, g):
    xh, r = _rms(xf, None)
    dxn = dh * g
    dx = r * (dxn - xh * jnp.mean(dxn * xh, axis=-1, keepdims=True))
    return dx, jnp.sum(dh * xh, axis=0, keepdims=True), xh * g


def _loss_head(x, g, tgt):
    T, D = x.shape
    tm = min(TM, T)

    def body(x_ref, g_ref, t_ref, loss_ref, dx_ref, dg_ref):
        @pl.when(pl.program_id(0) == 0)
        def _():
            loss_ref[...] = jnp.zeros_like(loss_ref)
            dg_ref[...] = jnp.zeros_like(dg_ref)

        xf = x_ref[...]
        g = g_ref[...]
        xh, _ = _rms(xf, None)
        e = xh * g - t_ref[...]
        loss_ref[...] += 0.5 * jnp.sum(jnp.mean(e * e, axis=-1, keepdims=True), axis=0, keepdims=True)
        dx, dg, _ = _rms_bwd_rows(e * (1.0 / D), xf, g)
        dx_ref[...] = dx
        dg_ref[...] += dg

    return pl.pallas_call(
        body, name="loss_head", grid=(T // tm,),
        in_specs=[pl.BlockSpec((tm, D), lambda i: (i, 0)), pl.BlockSpec((1, D), lambda i: (0, 0)),
                  pl.BlockSpec((tm, D), lambda i: (i, 0))],
        out_specs=[pl.BlockSpec((1, 1), lambda i: (0, 0)), pl.BlockSpec((tm, D), lambda i: (i, 0)),
                   pl.BlockSpec((1, D), lambda i: (0, 0))],
        out_shape=[jax.ShapeDtypeStruct((1, 1), F32), jax.ShapeDtypeStruct((T, D), F32),
                   jax.ShapeDtypeStruct((1, D), F32)],
        compiler_params=_cp(1),
    )(x, g, tgt)


def _ffn_bwd(dxo, x, g, gu, win, wout, dep):
    T, D = x.shape
    FB = win.shape[2]
    tm = min(TM_FFN_BWD, T)

    def body(dxo_ref, x_ref, g_ref, gu_ref, win_hbm, wout_hbm, dep_ref,
             dxi_ref, dg_ref, hb_ref, dgu_ref, a_ref, dyb_ref, win_v, wout_v, sems):
        @pl.when(pl.program_id(0) == 0)
        def _():
            _load_ffn_weights(win_hbm, wout_hbm, win_v, wout_v, sems)
            dg_ref[...] = jnp.zeros_like(dg_ref)

        dyb = (0.5 * dxo_ref[...]).astype(BF16)
        dyb_ref[...] = dyb
        dh = jnp.zeros((tm, D), F32)
        for blk in range(2):
            for lo, sz in _lane_chunks(FB):
                cols = pl.ds(blk * FB + lo, sz)
                da = _dot_nt(dyb, wout_v[cols, :])
                gate = gu_ref[0, :, cols].astype(F32)
                up = gu_ref[1, :, cols].astype(F32)
                sg = _fast_sigmoid(gate)
                s = gate * sg
                a_ref[:, cols] = (s * up).astype(BF16)
                dgate = (da * up * (sg + s * (1.0 - sg))).astype(BF16)
                dup = (da * s).astype(BF16)
                dgu_ref[0, :, cols] = dgate
                dgu_ref[1, :, cols] = dup
                dh = dh + _dot_nt(dgate, win_v[blk, :, pl.ds(lo, sz)]) + _dot_nt(dup, win_v[2 + blk, :, pl.ds(lo, sz)])
        dx, dg, h = _rms_bwd_rows(dh, x_ref[...], g_ref[...])
        dxi_ref[...] = dxo_ref[...] + dx
        dg_ref[...] += dg
        hb_ref[...] = h.astype(BF16)

    row = pl.BlockSpec((tm, D), lambda i: (i, 0))
    act = pl.BlockSpec((2, tm, 2 * FB), lambda i: (0, i, 0))
    return pl.pallas_call(
        body, name="ffn_bwd", grid=(T // tm,),
        in_specs=[row, row, pl.BlockSpec((1, D), lambda i: (0, 0)), act, ANY, ANY, ANY],
        out_specs=[row, pl.BlockSpec((1, D), lambda i: (0, 0)), row, act,
                   pl.BlockSpec((tm, 2 * FB), lambda i: (i, 0)), row],
        out_shape=[jax.ShapeDtypeStruct((T, D), F32), jax.ShapeDtypeStruct((1, D), F32),
                   jax.ShapeDtypeStruct((T, D), BF16), jax.ShapeDtypeStruct((2, T, 2 * FB), BF16),
                   jax.ShapeDtypeStruct((T, 2 * FB), BF16), jax.ShapeDtypeStruct((T, D), BF16)],
        scratch_shapes=[pltpu.VMEM(win.shape, BF16), pltpu.VMEM(wout.shape, BF16), pltpu.SemaphoreType.DMA((6,))],
        compiler_params=_cp(1),
    )(dxo, x, g, gu, win, wout, dep)


def _mix_rms_bwd(dxo, x, g, dzs, wts):
    T, D = x.shape
    tm = min(TM, T)
    npair = len(dzs)

    def body(*refs):
        dxo_ref, x_ref, g_ref = refs[:3]
        dz_refs, w_refs = refs[3:3 + npair], refs[3 + npair:3 + 2 * npair]
        dxi_ref, dg_ref, hb_ref = refs[3 + 2 * npair:]

        @pl.when(pl.program_id(0) == 0)
        def _():
            dg_ref[...] = jnp.zeros_like(dg_ref)

        dh = jnp.zeros((tm, D), F32)
        for p in range(npair):
            dh = dh + _dot(dz_refs[p][...], w_refs[p][...])
        dx, dg, h = _rms_bwd_rows(dh, x_ref[...], g_ref[...])
        dxi_ref[...] = dxo_ref[...] + dx
        dg_ref[...] += dg
        hb_ref[...] = h.astype(BF16)

    row = pl.BlockSpec((tm, D), lambda i: (i, 0))
    return pl.pallas_call(
        body, name="mix_rms_bwd", grid=(T // tm,),
        in_specs=[row, row, pl.BlockSpec((1, D), lambda i: (0, 0))]
                 + [pl.BlockSpec((tm, dz.shape[1]), lambda i: (i, 0)) for dz in dzs]
                 + [pl.BlockSpec(w.shape, lambda i: (0, 0)) for w in wts],
        out_specs=[row, pl.BlockSpec((1, D), lambda i: (0, 0)), row],
        out_shape=[jax.ShapeDtypeStruct((T, D), F32), jax.ShapeDtypeStruct((1, D), F32),
                   jax.ShapeDtypeStruct((T, D), BF16)],
        compiler_params=_cp(1),
    )(dxo, x, g, *dzs, *wts)


def _wgrad(name, a, b, a_spec, b_spec, out_shape, out_spec, nblk, dep):
    T = a.shape[0]
    tk = min(TK_WGRAD, T)

    def body(a_ref, b_ref, dep_ref, o_ref):
        @pl.when(pl.program_id(1) == 0)
        def _():
            o_ref[...] = jnp.zeros_like(o_ref)

        o_ref[...] += _dot_tn(a_ref[...], b_ref[...]).reshape(o_ref.shape)

    return pl.pallas_call(
        body, name=name, grid=(nblk, T // tk), in_specs=[a_spec, b_spec, ANY], out_specs=out_spec,
        out_shape=jax.ShapeDtypeStruct(out_shape, F32), compiler_params=_cp(2),
    )(a, b, dep)


def _wgrad_ffn_in(hb, dgu, dep):
    T, D = hb.shape
    FB = dgu.shape[2] // 2
    tk = min(TK_WGRAD, T)
    return _wgrad("wgrad_ffn_in", hb, dgu,
                  pl.BlockSpec((tk, D), lambda b, k: (k, 0)),
                  pl.BlockSpec((None, tk, FB), lambda b, k: (b // 2, k, b % 2)),
                  (4, D, FB), pl.BlockSpec((None, D, FB), lambda b, k: (b, 0, 0)), 4, dep)


def _wgrad_ffn_out(a, dyb, dep):
    T, D = dyb.shape
    FB = a.shape[1] // 2
    tk = min(TK_WGRAD, T)
    return _wgrad("wgrad_ffn_out", a, dyb,
                  pl.BlockSpec((tk, FB), lambda b, k: (k, b)),
                  pl.BlockSpec((tk, D), lambda b, k: (k, 0)),
                  (4, FB // 2, D), pl.BlockSpec((2, FB // 2, D), lambda b, k: (b, 0, 0)), 2, dep)


def _wgrad_cat(a_list, b_list):
    T = a_list[0].shape[0]
    tk = min(TK_WGRAD, T)
    na = len(a_list)
    M, N = sum(a.shape[1] for a in a_list), sum(b.shape[1] for b in b_list)

    def body(*refs):
        a_refs, b_refs, o_ref = refs[:na], refs[na:-1], refs[-1]

        @pl.when(pl.program_id(0) == 0)
        def _():
            o_ref[...] = jnp.zeros_like(o_ref)

        r0 = 0
        for a_ref in a_refs:
            c0 = 0
            for b_ref in b_refs:
                m, n = a_ref.shape[1], b_ref.shape[1]
                o_ref[r0:r0 + m, c0:c0 + n] += _dot_tn(a_ref[...], b_ref[...])
                c0 += n
            r0 += a_ref.shape[1]

    return pl.pallas_call(
        body, name="wgrad_cat", grid=(T // tk,),
        in_specs=[pl.BlockSpec((tk, v.shape[1]), lambda k: (k, 0)) for v in list(a_list) + list(b_list)],
        out_specs=pl.BlockSpec((M, N), lambda k: (0, 0)),
        out_shape=jax.ShapeDtypeStruct((M, N), F32), compiler_params=_cp(1),
    )(*a_list, *b_list)


def _mixout_bwd(dxo, wo):
    T, D = dxo.shape
    tm = min(TM_MIX, T)
    A = ATTN_W
    C = wo.shape[0] - A

    def body(dxo_ref, w_ref, dyb_ref, da_ref, dc_ref):
        dyb = dxo_ref[...].astype(BF16)
        dyb_ref[...] = dyb
        da_ref[...] = _dot_nt(dyb, w_ref[:A, :]).astype(BF16)
        dc_ref[...] = _dot_nt(dyb, w_ref[A:, :])

    return pl.pallas_call(
        body, name="mixout_bwd", grid=(T // tm,),
        in_specs=[pl.BlockSpec((tm, D), lambda i: (i, 0)), pl.BlockSpec(wo.shape, lambda i: (0, 0))],
        out_specs=[pl.BlockSpec((tm, D), lambda i: (i, 0)), pl.BlockSpec((tm, A), lambda i: (i, 0)),
                   pl.BlockSpec((tm, C), lambda i: (i, 0))],
        out_shape=[jax.ShapeDtypeStruct((T, D), BF16), jax.ShapeDtypeStruct((T, A), BF16),
                   jax.ShapeDtypeStruct((T, C), F32)],
        compiler_params=_cp(1),
    )(dxo, wo)


def _conv_bwd(dconv, ypre, u, w, lg, lb):
    T, CH = dconv.shape
    tm = min(TM, T)
    n = tm + HALO
    hb = tm // HALO
    nt = T // tm
    nchunk = tm // CONV_ROWS

    def body(dc_ref, dcn_ref, yp_ref, ypn_ref, uc_ref, up_ref, w_ref, lg_ref, lb_ref,
             du_ref, dw_ref, dvec_ref, zs_ref, zsh_ref, dy_ref, dysh_ref, dz_ref, dwacc_ref):
        i = pl.program_id(0)

        @pl.when(i == 0)
        def _():
            dwacc_ref[...] = jnp.zeros_like(dwacc_ref)
            dvec_ref[...] = jnp.zeros_like(dvec_ref)

        g, bb = lg_ref[...], lb_ref[...]

        def ln_bwd(dc, yp):
            mu = jnp.mean(yp, axis=-1, keepdims=True)
            d = yp - mu
            rs = lax.rsqrt(jnp.mean(d * d, axis=-1, keepdims=True) + EPS)
            yn = d * rs
            o = yn * g + bb
            sg = _fast_sigmoid(o)
            do = dc * (sg * (1.0 + o * (1.0 - sg)))
            dyn = do * g
            dyp = rs * (dyn - jnp.mean(dyn, axis=-1, keepdims=True)
                        - yn * jnp.mean(dyn * yn, axis=-1, keepdims=True))
            return dyp, do, yn

        dyp, do, yn = ln_bwd(dc_ref[...], yp_ref[...])
        dvec_ref[0:1, :] += jnp.sum(dyp, axis=0, keepdims=True)
        dvec_ref[1:2, :] += jnp.sum(do * yn, axis=0, keepdims=True)
        dvec_ref[2:3, :] += jnp.sum(do, axis=0, keepdims=True)
        dy_ref[0:tm] = dyp
        dyh, _, _ = ln_bwd(dcn_ref[...], ypn_ref[...])
        dy_ref[tm:] = jnp.where(i < nt - 1, dyh, 0.0)
        _shift_copies(dy_ref, dysh_ref, n - 8)
        _fill_z(zs_ref, zsh_ref, uc_ref, up_ref, i, CH, n)

        def chunk(ci, carry):
            c0 = pl.multiple_of(ci * CONV_ROWS, CONV_ROWS)
            acc = jnp.zeros((CONV_ROWS, CH), F32)
            for k in range(CONV_W):
                acc = acc + w_ref[k:k + 1, :] * _tap(dy_ref, dysh_ref, CONV_W - 1 - k, c0)
            dz_ref[pl.ds(c0, CONV_ROWS), :] = acc
            dyc = dy_ref[pl.ds(c0, CONV_ROWS), :]
            for k in range(CONV_W):
                prod = dyc * _tap(zs_ref, zsh_ref, HALO - (CONV_W - 1) + k, c0)
                dwacc_ref[k] += jnp.sum(prod.reshape(CONV_ROWS // 8, 8, CH), axis=0)
            return carry

        lax.fori_loop(0, nchunk, chunk, 0)

        @pl.when(i == nt - 1)
        def _():
            dw_ref[...] = jnp.sum(dwacc_ref[...], axis=1)

        uc = uc_ref[...]
        a = uc[:, :CH]
        sg = _fast_sigmoid(uc[:, CH:])
        dz = dz_ref[...]
        du_ref[:, :CH] = (dz * sg).astype(BF16)
        du_ref[:, CH:] = (dz * a * sg * (1.0 - sg)).astype(BF16)

    cur = lambda c: pl.BlockSpec((tm, c), lambda i: (i, 0))
    nxt = lambda c: pl.BlockSpec((HALO, c), lambda i: (jnp.minimum((i + 1) * hb, T // HALO - 1), 0))
    vec = pl.BlockSpec((1, CH), lambda i: (0, 0))
    return pl.pallas_call(
        body, name="conv_bwd", grid=(nt,),
        in_specs=[cur(CH), nxt(CH), cur(CH), nxt(CH), cur(2 * CH),
                  pl.BlockSpec((HALO, 2 * CH), lambda i: (jnp.maximum(i * hb - 1, 0), 0)),
                  pl.BlockSpec((CONV_W, CH), lambda i: (0, 0)), vec, vec],
        out_specs=[pl.BlockSpec((tm, 2 * CH), lambda i: (i, 0)), pl.BlockSpec((32, CH), lambda i: (0, 0)),
                   pl.BlockSpec((8, CH), lambda i: (0, 0))],
        out_shape=[jax.ShapeDtypeStruct((T, 2 * CH), BF16), jax.ShapeDtypeStruct((32, CH), F32),
                   jax.ShapeDtypeStruct((8, CH), F32)],
        scratch_shapes=[pltpu.VMEM((n, CH), F32), pltpu.VMEM((7, n - 8, CH), F32),
                        pltpu.VMEM((n, CH), F32), pltpu.VMEM((7, n - 8, CH), F32), pltpu.VMEM((tm, CH), F32),
                        pltpu.VMEM((32, 8, CH), F32)],
        compiler_params=_cp(1),
    )(dconv, dconv, ypre, ypre, u, u, w, lg, lb)


def _attn_bwd(sinks, tab, qkv, dattn):
    T = qkv.shape[0]
    nb = T // WINDOW

    def body(sink_ref, tab_ref, q_ref, kvp_ref, kvc_ref, do_ref, dq_ref, dkv_ref, dsk_ref, carry_ref):
        n = pl.program_id(0)

        @pl.when(n == 0)
        def _():
            dsk_ref[...] = jnp.zeros_like(dsk_ref)
            carry_ref[...] = jnp.zeros_like(carry_ref)

        @pl.when(n < nb)
        def _():
            seen = _first_block_mask(n)
            for g in range(N_KV):
                qs = _stack_heads(q_ref, g)
                dos = _stack_heads(do_ref, g)
                k = _band(kvp_ref, kvc_ref, g * HEAD_DIM)
                v = _band(kvp_ref, kvc_ref, KV_W + g * HEAD_DIM)
                p, ps = _attn_probs(qs, k, tab_ref[g], seen, _sink_col(sink_ref, g))
                dp = _dot_nt(dos, v)
                delta = jnp.sum(p * dp, axis=-1, keepdims=True)
                dsb = (p * (dp - delta)).astype(BF16)
                dsink = -ps * delta
                dqs = _dot(dsb, k) * SCALE
                dk = _dot_tn(dsb, qs) * SCALE
                dv = _dot_tn(p.astype(BF16), dos)
                for i in range(GROUP):
                    h = GROUP * g + i
                    dq_ref[:, h * HEAD_DIM:(h + 1) * HEAD_DIM] = dqs[i * WINDOW:(i + 1) * WINDOW].astype(BF16)
                    dsk_ref[h:h + 1, :] += jnp.sum(dsink[i * WINDOW:(i + 1) * WINDOW], axis=0, keepdims=True)
                for off, d in ((g * HEAD_DIM, dk), (KV_W + g * HEAD_DIM, dv)):
                    dkv_ref[:, off:off + HEAD_DIM] = (carry_ref[:, off:off + HEAD_DIM] + d[:WINDOW]).astype(BF16)
                    carry_ref[:, off:off + HEAD_DIM] = d[WINDOW:]

        @pl.when(n == nb)
        def _():
            dkv_ref[...] = carry_ref[...].astype(BF16)

    last = nb - 1
    return pl.pallas_call(
        body, name="attn_bwd", grid=(nb + 1,),
        in_specs=[pl.BlockSpec(memory_space=pltpu.SMEM),
                  pl.BlockSpec(tab.shape, lambda n: (0, 0, 0)),
                  pl.BlockSpec((WINDOW, ATTN_W), lambda n: (jnp.minimum(n, last), 0)),
                  pl.BlockSpec((WINDOW, 2 * KV_W), lambda n: (jnp.clip(n - 1, 0, last), 2)),
                  pl.BlockSpec((WINDOW, 2 * KV_W), lambda n: (jnp.minimum(n, last), 2)),
                  pl.BlockSpec((WINDOW, ATTN_W), lambda n: (jnp.minimum(n, last), 0))],
        out_specs=[pl.BlockSpec((WINDOW, ATTN_W), lambda n: (jnp.minimum(n, last), 0)),
                   pl.BlockSpec((WINDOW, 2 * KV_W), lambda n: (jnp.maximum(n - 1, 0), 0)),
                   pl.BlockSpec((8, LANES), lambda n: (0, 0))],
        out_shape=[jax.ShapeDtypeStruct((T, ATTN_W), BF16), jax.ShapeDtypeStruct((T, 2 * KV_W), BF16),
                   jax.ShapeDtypeStruct((8, LANES), F32)],
        scratch_shapes=[pltpu.VMEM((WINDOW, 2 * KV_W), F32)],
        compiler_params=_cp(1),
    )(sinks, tab, qkv, qkv, qkv, dattn)


def _pack(arrs):
    flat = jnp.concatenate([a.reshape(-1) for a in arrs])
    pad = -flat.shape[0] % (8 * LANES)
    return jnp.pad(flat, (0, pad)).reshape(1, -1, LANES)


def _unpack(packed, like):
    flat = packed.reshape(-1)
    out, off = [], 0
    for a in like:
        out.append(flat[off:off + a.size].reshape(a.shape))
        off += a.size
    return out


def kernel(x, norm_ffn1, w_ffn1_in, w_ffn1_out, norm_mix, w_in, sinks, w_dw, b_dw, conv_ln_g, conv_ln_b, w_out, norm_ffn2, w_ffn2_in, w_ffn2_out, final_norm, loss_target, m_norm_ffn1, m_w_ffn1_in, m_w_ffn1_out, m_norm_mix, m_w_in, m_sinks, m_w_dw, m_b_dw, m_conv_ln_g, m_conv_ln_b, m_w_out, m_norm_ffn2, m_w_ffn2_in, m_w_ffn2_out, m_final_norm, v_norm_ffn1, v_w_ffn1_in, v_w_ffn1_out, v_norm_mix, v_w_in, v_sinks, v_w_dw, v_b_dw, v_conv_ln_g, v_conv_ln_b, v_w_out, v_norm_ffn2, v_w_ffn2_in, v_w_ffn2_out, v_final_norm):
    L, D = norm_ffn1.shape
    T = x.shape[1]
    FB = w_ffn1_in.shape[2]
    CH = b_dw.shape[1]
    QKV = ATTN_W + 2 * KV_W
    xs = x.reshape(T, D)
    tgt = loss_target.reshape(T, D)
    cx, cy, cc = lax.axis_index("x"), lax.axis_index("y"), lax.axis_index("c")
    chip = 2 * cx + cy
    cidx = cc.reshape(1).astype(jnp.int32)
    tr = lambda a_: jnp.transpose(a_, (0, 2, 1))
    big_w = (w_ffn1_in, w_ffn1_out, tr(w_in), w_out, w_ffn2_in, w_ffn2_out)
    big_m = (m_w_ffn1_in, m_w_ffn1_out, tr(m_w_in), m_w_out, m_w_ffn2_in, m_w_ffn2_out)
    big_v = (v_w_ffn1_in, v_w_ffn1_out, tr(v_w_in), v_w_out, v_w_ffn2_in, v_w_ffn2_out)
    NW = len(big_w) + 1

    def shards(l, tok):
        return [(w_[l] + tok[0, 0]).astype(BF16) for w_ in big_w] + [w_dw[l] + tok[0, 0]]

    def own_slot(a, slots=4, idx=chip):
        return lax.dynamic_update_index_in_dim(lax.empty((slots,) + a.shape, a.dtype), a, idx, 0)

    def gather_start(srcs, tok):
        return _xchg_start("gather_start", srcs, [own_slot(s_) for s_ in srcs], _gather_plan, tok)

    def gather_arrived(started, after, n, taps):
        _, lands, tok = _xchg_wait("gather_wait", started, n, n, _gather_plan, after)
        return _xchg_start("gshare_start", [], lands[:-1] if taps else lands, _gshare_plan, tok, "sibling3"), lands[-1]

    def shared_weights(shared, after, n):
        _, mats, tok = _xchg_wait("gshare_wait", shared, 0, n, _gshare_plan, after, "sibling3")
        return mats, tok

    row = lambda a, l: a[l].reshape(1, -1)
    tab = _attn_bias_table()
    NB = len(big_w)

    saved, W = [], []
    zero_tok = jnp.zeros((8, LANES), F32)
    src0 = shards(0, zero_tok)
    started = gather_start(src0[:2], zero_tok)
    rest0 = gather_start(src0[2:], started[-1])
    cast = [None] + [shards(l, rest0[-1]) for l in range(1, L)]
    shared, _ = gather_arrived(started, [xs] + [a_ for c_ in cast[1:] for a_ in c_], 2, False)
    after = [shared[-1]]
    for l in range(L):
        mats, tok = shared_weights(shared, after, 2 if l == 0 else NB)
        started = None
        if l + 1 < L:
            started = gather_start(cast[l + 1], tok)
            tok = started[-1]
        x0 = xs
        x1, gu1 = _ffn_fwd(x0, row(norm_ffn1, l) + tok[0, 0], mats[0], mats[1].reshape(2 * FB, D))
        gm_row = row(norm_mix, l)
        if l == 0:
            shared, gdw = gather_arrived(rest0, [x1], NW - 2, True)
            rest, tok = shared_weights(shared, [shared[-1]], NB - 2)
            mats = list(mats) + list(rest)
            gm_row = gm_row + tok[0, 0]
        g1i, g1o, gi, go, g2i, g2o = mats
        w = dict(f1i=g1i, f1o=g1o.reshape(2 * FB, D), f2i=g2i, f2o=g2o.reshape(2 * FB, D),
                 wit=gi.reshape(-1, D), wo=go.reshape(-1, D),
                 wdw=jnp.transpose(gdw, (1, 0, 2)).reshape(CONV_W, CH))
        W.append(w)
        qkv, u = _mixproj_fwd(x1, gm_row, w["wit"])
        attn = _attn_fwd(row(sinks, l), tab, qkv)
        conv, ypre = _conv_fwd(u, w["wdw"], row(b_dw, l), row(conv_ln_g, l), row(conv_ln_b, l))
        x2 = _mixout_fwd(x1, attn, conv, w["wo"])
        g2_row = row(norm_ffn2, l)
        if started is not None:
            shared, gdw = gather_arrived(started, [x2], NW, True)
            g2_row = g2_row + shared[-1][0, 0]
        xs, gu2 = _ffn_fwd(x2, g2_row, w["f2i"], w["f2o"])
        saved.append((x0, gu1, x1, qkv, u, attn, conv, ypre, x2, gu2))
        after = [xs]

    loss_part, dx, d_final = _loss_head(xs, final_norm.reshape(1, D), tgt)
    loss = lax.psum(loss_part[0, 0], ("x", "y", "c"))

    bufs = [[lax.empty(w_.shape, F32) for _ in range(4)] for w_ in big_w]
    d_n1, d_nm, d_n2 = [None] * L, [None] * L, [None] * L
    d_sk, d_bdw, d_lg, d_lb, d_wdw = [None] * L, [None] * L, [None] * L, [None] * L, [None] * L

    def sib_start(gs):
        return _xchg_start("sib_start", gs, [lax.empty((4, g.shape[1] // 2, g.shape[2]), F32) for g in gs],
                           _sib_plan, zero_tok, "sibling")

    def reduce_start(sib_started, after, n):
        gs, sibs, _ = _xchg_wait("sib_wait", sib_started, n, n, _sib_plan, after, "sibling")
        parts = [_sum_halves(cidx, g, s_) for g, s_ in zip(gs, sibs)]
        lands = [own_slot(lax.dynamic_index_in_dim(p, chip, 0, keepdims=False)) for p in parts]
        return _xchg_start("rs_start", parts, lands, _rs_plan, zero_tok)

    def share_start(rs_started, after, n):
        _, qs, tok = _xchg_wait("rs_wait", rs_started, n, n, _rs_plan, after)
        return _xchg_start("qshare_start", qs, [lax.empty(q.shape, q.dtype) for q in qs], _whole_plan, tok, "sibling")

    def finish(l, shared, after, idxs):
        q_own, q_sib, _ = _xchg_wait("qshare_wait", shared, len(idxs), len(idxs), _whole_plan, after, "sibling")
        for k, t in enumerate(idxs):
            bufs[t] = _adamw_layer(cidx, q_own[k], q_sib[k], big_w[t], big_m[t], big_v[t], bufs[t], l)

    ALL = list(range(NB))
    EARLY, LATE = ALL[2:], ALL[:2]
    sib_pending = rs_pending = None
    shares = []
    tok = zero_tok
    for l in reversed(range(L)):
        w = W[l]
        x0, gu1, x1, qkv, u, attn, conv, ypre, x2, gu2 = saved[l]
        dx, d_n2[l], hb, dgu, a, dyb = _ffn_bwd(dx, x2, row(norm_ffn2, l), gu2, w["f2i"], w["f2o"], tok)
        g_f2i, g_f2o = _wgrad_ffn_in(hb, dgu, tok), _wgrad_ffn_out(a, dyb, tok)
        lg_row = row(conv_ln_g, l)
        if sib_pending is not None:
            rs_started = reduce_start(sib_pending[1], [g_f2o], NB)
            if rs_pending is not None:
                shares.append((rs_pending[0], share_start(rs_pending[1], [rs_started[-1]], NB)))
            rs_pending = (sib_pending[0], rs_started)
            lg_row = lg_row + rs_started[-1][0, 0]
        dyb, dattn, dconv = _mixout_bwd(dx, w["wo"])
        g_wo = _wgrad_cat([attn, conv], [dyb]).reshape(4, -1, D)
        du, dwdw, dvec = _conv_bwd(dconv, ypre, u, w["wdw"], lg_row, row(conv_ln_b, l))
        d_wdw[l], d_bdw[l], d_lg[l], d_lb[l] = dwdw[:CONV_W], dvec[0], dvec[1], dvec[2]
        dq, dkv, dsk = _attn_bwd(row(sinks, l), tab, qkv, dattn)
        d_sk[l] = dsk[:, 0]
        wit = w["wit"]
        dx, d_nm[l], hb = _mix_rms_bwd(dx, x1, row(norm_mix, l), [dq, dkv, du],
                                       [wit[:ATTN_W], wit[ATTN_W:QKV], wit[QKV:]])
        g_wi = _wgrad_cat([dq, dkv, du], [hb]).reshape(4, -1, D)
        if l == 0:
            sib_early = sib_start([g_wi, g_wo, g_f2i, g_f2o])
            tok = sib_early[-1]
        dx, d_n1[l], hb, dgu, a, dyb = _ffn_bwd(dx, x0, row(norm_ffn1, l), gu1, w["f1i"], w["f1o"], tok)
        if l == 0:
            rs_early = reduce_start(sib_early, [dx], len(EARLY))
            tok = rs_early[-1]
        g_f1i, g_f1o = _wgrad_ffn_in(hb, dgu, tok), _wgrad_ffn_out(a, dyb, tok)
        sib_started = sib_start([g_f1i, g_f1o] if l == 0 else [g_f1i, g_f1o, g_wi, g_wo, g_f2i, g_f2o])
        tok = sib_started[-1]
        sib_pending = (l, sib_started)
    grad_x = dx.reshape(x.shape)

    small_g = [jnp.concatenate(d, axis=0) for d in (d_n1, d_nm, d_n2)] + [d_final, jnp.stack(d_sk)] + \
              [jnp.stack(d) for d in (d_bdw, d_lg, d_lb, d_wdw)]
    packed = _pack(small_g)[0]
    small_started = _xchg_start("small_start", [packed], [own_slot(packed, 8, 4 * cx + 2 * cy + cc)], _slot_plan, tok, "all")

    after = [small_started[-1]]
    if rs_pending is not None:
        shares.append((rs_pending[0], share_start(rs_pending[1], after, NB)))
        after = [shares[-1][1][-1]]
    if shares:
        finish(*shares.pop(0), after, ALL)
        after = [b_[0] for b_ in bufs]
    rs_late = reduce_start(sib_pending[1], after, len(LATE))
    after = [rs_late[-1]]
    for l, sh in shares:
        finish(l, sh, after, ALL)
        after = [b_[0] for b_ in bufs]
    _, (slots,), _ = _xchg_wait("small_wait", small_started, 1, 1, _slot_plan, after, "all")
    small_sum = _unpack(_sum_slots(slots), small_g)
    g_wdw = lax.dynamic_slice_in_dim(small_sum[8], chip * w_dw.shape[2], w_dw.shape[2], axis=2)
    small_g = [small_sum[0], small_sum[1], small_sum[2], small_sum[3].reshape(D), small_sum[4],
               small_sum[5], small_sum[6], small_sum[7], g_wdw]
    small_w = (norm_ffn1, norm_mix, norm_ffn2, final_norm, sinks, b_dw, conv_ln_g, conv_ln_b, w_dw)
    small_m = (m_norm_ffn1, m_norm_mix, m_norm_ffn2, m_final_norm, m_sinks, m_b_dw, m_conv_ln_g, m_conv_ln_b, m_w_dw)
    small_v = (v_norm_ffn1, v_norm_mix, v_norm_ffn2, v_final_norm, v_sinks, v_b_dw, v_conv_ln_g, v_conv_ln_b, v_w_dw)
    upd = _adamw(_pack(small_g), _pack(small_w), _pack(small_m), _pack(small_v))
    small_upd = [_unpack(u_, small_w) for u_ in upd]
    sh_early = share_start(rs_early, [upd[0]], len(EARLY))
    sh_late = share_start(rs_late, [sh_early[-1]], len(LATE))
    finish(0, sh_early, [sh_late[-1]], EARLY)
    finish(0, sh_late, [bufs[t][0] for t in EARLY], LATE)

    order = ("norm_ffn1", "w_ffn1_in", "w_ffn1_out", "norm_mix", "w_in", "sinks", "w_dw", "b_dw", "conv_ln_g",
             "conv_ln_b", "w_out", "norm_ffn2", "w_ffn2_in", "w_ffn2_out", "final_norm")
    small_names = ("norm_ffn1", "norm_mix", "norm_ffn2", "final_norm", "sinks", "b_dw", "conv_ln_g", "conv_ln_b", "w_dw")
    big_names = ("w_ffn1_in", "w_ffn1_out", "w_in", "w_out", "w_ffn2_in", "w_ffn2_out")
    grads, deltas, new_m, new_v = {}, {}, {}, {}
    for i, nme in enumerate(small_names):
        grads[nme], deltas[nme], new_m[nme], new_v[nme] = small_g[i], small_upd[0][i], small_upd[1][i], small_upd[2][i]
    for i, nme in enumerate(big_names):
        grads[nme], deltas[nme], new_m[nme], new_v[nme] = [tr(b_) for b_ in bufs[i]] if nme == "w_in" else bufs[i]
    return (loss, grad_x, *[grads[n] for n in order], *[deltas[n] for n in order],
            *[new_m[n] for n in order], *[new_v[n] for n in order])
```

```python
import functools

import jax
import jax.numpy as jnp
from jax import lax
from jax.experimental import pallas as pl
from jax.experimental.pallas import tpu as pltpu

F32, BF16 = jnp.float32, jnp.bfloat16
EPS = 1e-6
NEG_INF = -1e30
HEAD_DIM = 64
N_HEADS = 8
N_KV = 2
GROUP = N_HEADS // N_KV
WINDOW = 128
ATTN_W = N_HEADS * HEAD_DIM
KV_W = N_KV * HEAD_DIM
CONV_W = 31
HALO = 32
CONV_ROWS = 32
SCALE = 1.0 / 8.0
ADAM_LR, ADAM_B1, ADAM_B2, ADAM_EPS, ADAM_WD, ADAM_STEP = 0.001, 0.9, 0.999, 1e-08, 0.01, 10
TM = 512
TM_FFN_BWD = 256
TK_WGRAD = 2048
TM_MIX = 1024
LANES = 128
VMEM_LIMIT = 52 * 1024 * 1024
MESH = pl.DeviceIdType.MESH
ANY = pl.BlockSpec(memory_space=pl.ANY)
HBM = pl.BlockSpec(memory_space=pltpu.HBM)
SEM = pl.BlockSpec(memory_space=pltpu.SEMAPHORE)
VMEM = pl.BlockSpec(memory_space=pltpu.VMEM)
EFFECT = pltpu.SideEffectType.DATAFLOW_SIDE_EFFECTING
TOKEN = jax.ShapeDtypeStruct((8, LANES), F32)


def _cp(n):
    return pltpu.CompilerParams(dimension_semantics=("arbitrary",) * n, vmem_limit_bytes=VMEM_LIMIT)


def _dot(a, b):
    return jnp.dot(a, b, preferred_element_type=F32)


def _dot_nt(a, b):
    return lax.dot_general(a, b, (((1,), (1,)), ((), ())), preferred_element_type=F32)


def _dot_tn(a, b):
    return lax.dot_general(a, b, (((0,), (0,)), ((), ())), preferred_element_type=F32)


def _place():
    x, y, c = lax.axis_index("x"), lax.axis_index("y"), lax.axis_index("c")
    chips = [(1 - x, y), (x, 1 - y), (1 - x, 1 - y)]
    return x, y, c, chips


def _rcopy(src, dst, send_sems, recv_sems, k, dev):
    return pltpu.make_async_remote_copy(src_ref=src, dst_ref=dst, send_sem=send_sems.at[k],
                                        recv_sem=recv_sems.at[k], device_id=dev, device_id_type=MESH)


def _hbm(a):
    return pltpu.with_memory_space_constraint(a, pltpu.HBM)


PEERS = {"chips": 3, "sibling": 1, "sibling3": 3, "all": 7}


def _targets(mode):
    x, y, c, chips = _place()
    b = 2 * x + y
    if mode == "chips":
        return b, c, [((px, py, c), 2 * px + py) for px, py in chips]
    if mode == "sibling":
        return b, c, [((x, y, 1 - c), b)]
    if mode == "sibling3":
        return b, c, [((x, y, 1 - c), 2 * px + py) for px, py in chips]
    flip = lambda v, f: 1 - v if f else v
    devs = [(flip(x, k >> 2 & 1), flip(y, k >> 1 & 1), flip(c, k & 1)) for k in range(1, 8)]
    return 4 * x + 2 * y + c, c, [(d, 4 * d[0] + 2 * d[1] + d[2]) for d in devs]


def _xchg_start(name, srcs, lands, plan, dep, mode="chips"):
    ns, nl, npeer = len(srcs), len(lands), PEERS[mode]

    def body(*refs):
        land = refs[ns:ns + nl]
        src = refs[:ns] if ns else land
        send_sems, recv_sems, token = refs[ns + nl + 1], refs[ns + nl + 2], refs[-1]
        me, c, peers = _targets(mode)
        for t in range(nl):
            for j, (dev, tag) in enumerate(peers):
                s, d, _ = plan(src[t], land[t], t, me, c, tag)
                _rcopy(s, d, send_sems, recv_sems, npeer * t + j, dev).start()
        token[...] = jnp.zeros_like(token)

    arrs = list(srcs) + list(lands)
    return pl.pallas_call(
        body, name=name,
        out_shape=(pltpu.SemaphoreType.DMA((npeer * nl,)), pltpu.SemaphoreType.DMA((npeer * nl,)),
                   *[pltpu.HBM(a.shape, a.dtype) for a in arrs], TOKEN),
        in_specs=[HBM] * (ns + nl) + [ANY], out_specs=(SEM, SEM, *[HBM] * (ns + nl), VMEM),
        input_output_aliases={i: 2 + i for i in range(ns + nl)},
        compiler_params=pltpu.CompilerParams(has_side_effects=EFFECT),
    )(*[_hbm(a) for a in arrs], dep)


def _xchg_wait(name, started, ns, nl, plan, after, mode="chips"):
    send_sems, recv_sems, thru = started[0], started[1], started[2:2 + ns + nl]
    npeer = PEERS[mode]

    def body(*refs):
        land = refs[ns:ns + nl]
        src = refs[:ns] if ns else land
        send_sems, recv_sems, token = refs[ns + nl], refs[ns + nl + 1], refs[-1]
        me, c, peers = _targets(mode)
        for t in range(nl):
            for j, (dev, tag) in enumerate(peers):
                s, _, a = plan(src[t], land[t], t, me, c, tag)
                cp = _rcopy(s, a, send_sems, recv_sems, npeer * t + j, dev)
                cp.wait_send()
                cp.wait_recv()
        token[...] = jnp.zeros_like(token)

    out = pl.pallas_call(
        body, name=name,
        out_shape=(*[pltpu.HBM(a.shape, a.dtype) for a in thru], TOKEN),
        in_specs=[HBM] * (ns + nl) + [SEM, SEM] + [ANY] * len(after), out_specs=(*[HBM] * (ns + nl), VMEM),
        input_output_aliases={i: i for i in range(ns + nl)},
        compiler_params=pltpu.CompilerParams(has_side_effects=EFFECT),
    )(*thru, send_sems, recv_sems, *after)
    return out[:ns], out[ns:ns + nl], out[-1]


def _half(ref_rows, which):
    h = ref_rows // 2
    return pl.ds(which * h, h)


def _gather_plan(src, land, t, b, c, pb):
    if len(src.shape) == 2 and src.shape[0] % 2 == 0:
        hs = _half(src.shape[0], c)
        return src.at[hs], land.at[b, hs], land.at[pb, hs]
    return src, land.at[b], land.at[pb]


def _gshare_plan(src, land, t, b, c, pb):
    return land.at[pb, _half(land.shape[1], c)], land.at[pb, _half(land.shape[1], c)], land.at[pb, _half(land.shape[1], 1 - c)]


def _rs_plan(src, land, t, b, c, pb):
    return src.at[pb], land.at[b], land.at[pb]


def _sib_plan(src, land, t, b, c, pb):
    return src.at[:, _half(src.shape[1], 1 - c), :], land, land


def _rows_block(h, cap=512):
    for rb in range(min(h, cap) // 16 * 16, 0, -16):
        if h % rb == 0:
            return rb
    return h


def _sum_halves(cidx, g, s):
    _, R, C = g.shape
    rb = _rows_block(R // 2)
    nr = R // 2 // rb

    def body(c_ref, g_ref, s_ref, o_ref):
        o_ref[...] = (g_ref[...].astype(F32) + s_ref[...].astype(F32)).astype(BF16)

    blk = (None, rb, C)
    return pl.pallas_call(
        body, name="sum_halves", out_shape=jax.ShapeDtypeStruct(s.shape, BF16),
        grid_spec=pltpu.PrefetchScalarGridSpec(
            num_scalar_prefetch=1, grid=(4, nr),
            in_specs=[pl.BlockSpec(blk, lambda p, i, c: (p, c[0] * nr + i, 0)),
                      pl.BlockSpec(blk, lambda p, i, c: (p, i, 0))],
            out_specs=pl.BlockSpec(blk, lambda p, i, c: (p, i, 0))),
        compiler_params=_cp(2),
    )(cidx, g, s)


def _whole_plan(src, land, t, me, c, tag):
    return src, land, land


def _slot_plan(src, land, t, me, c, tag):
    return src, land.at[me], land.at[tag]


def _adam_update(gg, w, m, v):
    m2 = ADAM_B1 * m + (1.0 - ADAM_B1) * gg
    v2 = ADAM_B2 * v + (1.0 - ADAM_B2) * (gg * gg)
    mh = m2 / (1.0 - ADAM_B1 ** ADAM_STEP)
    vh = v2 / (1.0 - ADAM_B2 ** ADAM_STEP)
    return -ADAM_LR * (mh / (jnp.sqrt(vh) + ADAM_EPS) + ADAM_WD * w), m2, v2


def _adamw_layer(cidx, q_own, q_sib, w, m, v, bufs, l):
    L, R, C = w.shape
    h = R // 2
    rb = _rows_block(h, 256)
    nr = h // rb

    def body(c_ref, qo_ref, qs_ref, w_ref, m_ref, v_ref, *rest):
        g_ref, d_ref, mo_ref, vo_ref = rest[-4:]
        own = pl.program_id(0) == c_ref[0]
        gg = jnp.zeros((rb, C), F32)
        for s in range(4):
            gg = gg + jnp.where(own, qo_ref[s], qs_ref[s]).astype(F32)
        g_ref[...] = gg
        d_ref[...], mo_ref[...], vo_ref[...] = _adam_update(gg, w_ref[...], m_ref[...], v_ref[...])

    q_own_spec = pl.BlockSpec((4, rb, C), lambda hh, i, c: (0, jnp.where(hh == c[0], i, 0), 0))
    q_sib_spec = pl.BlockSpec((4, rb, C), lambda hh, i, c: (0, jnp.where(hh == c[0], 0, i), 0))
    wspec = pl.BlockSpec((None, rb, C), lambda hh, i, c: (l, hh * nr + i, 0))
    return pl.pallas_call(
        body, name="adamw_layer", out_shape=[jax.ShapeDtypeStruct(w.shape, F32)] * 4,
        grid_spec=pltpu.PrefetchScalarGridSpec(
            num_scalar_prefetch=1, grid=(2, nr),
            in_specs=[q_own_spec, q_sib_spec, wspec, wspec, wspec] + [ANY] * 4, out_specs=[wspec] * 4),
        input_output_aliases={6 + k: k for k in range(4)},
        compiler_params=_cp(2),
    )(cidx, q_own, q_sib, w, m, v, *bufs)


def _adamw(g, w, m, v):
    L, R, C = g.shape
    rb = _rows_block(R)

    def body(g_ref, w_ref, m_ref, v_ref, d_ref, mo_ref, vo_ref):
        d_ref[...], mo_ref[...], vo_ref[...] = _adam_update(g_ref[...], w_ref[...], m_ref[...], v_ref[...])

    spec = pl.BlockSpec((None, rb, C), lambda l, i: (l, i, 0))
    return pl.pallas_call(
        body, name="adamw", grid=(L, R // rb), in_specs=[spec] * 4, out_specs=[spec] * 3,
        out_shape=[jax.ShapeDtypeStruct(g.shape, F32)] * 3, compiler_params=_cp(2),
    )(g, w, m, v)


def _sum_slots(buf):
    def body(b_ref, o_ref):
        acc = b_ref[0]
        for k in range(1, 8):
            acc = acc + b_ref[k]
        o_ref[...] = acc

    return pl.pallas_call(body, name="sum_slots", in_specs=[VMEM], out_specs=VMEM,
                          out_shape=jax.ShapeDtypeStruct(buf.shape[1:], F32))(buf)


def _rms(xf, g):
    r = lax.rsqrt(jnp.mean(xf * xf, axis=-1, keepdims=True) + EPS)
    return xf * r, r


def _lane_chunks(n):
    lo = (n // LANES + 1) // 2 * LANES
    return ((0, lo), (lo, n - lo))


def _load_ffn_weights(win_hbm, wout_hbm, win_v, wout_v, sems):
    fb = win_v.shape[2]
    loads = [pltpu.make_async_copy(win_hbm.at[k], win_v.at[k], sems.at[k]) for k in range(4)]
    loads += [pltpu.make_async_copy(wout_hbm.at[pl.ds(k * fb, fb)], wout_v.at[pl.ds(k * fb, fb)], sems.at[4 + k])
              for k in range(2)]
    for cp in loads:
        cp.start()
    for cp in loads:
        cp.wait()


def _fast_sigmoid(v):
    return pl.reciprocal(1.0 + jnp.exp(-v), approx=True)


def _ffn_fwd(x, g, win, wout):
    T, D = x.shape
    FB = win.shape[2]
    tm = min(TM, T)

    def body(x_ref, g_ref, win_hbm, wout_hbm, xo_ref, gu_ref, win_v, wout_v, sems):
        @pl.when(pl.program_id(0) == 0)
        def _():
            _load_ffn_weights(win_hbm, wout_hbm, win_v, wout_v, sems)

        xf = x_ref[...]
        xh, _ = _rms(xf, None)
        h = (xh * g_ref[...]).astype(BF16)
        acc = jnp.zeros((tm, D), F32)
        for blk in range(2):
            for lo, sz in _lane_chunks(FB):
                cols = pl.ds(blk * FB + lo, sz)
                gate = _dot(h, win_v[blk, :, pl.ds(lo, sz)])
                up = _dot(h, win_v[2 + blk, :, pl.ds(lo, sz)])
                gu_ref[0, :, cols] = gate.astype(BF16)
                gu_ref[1, :, cols] = up.astype(BF16)
                a = (gate * _fast_sigmoid(gate) * up).astype(BF16)
                acc = acc + _dot(a, wout_v[cols, :])
        xo_ref[...] = xf + 0.5 * acc

    row = pl.BlockSpec((tm, D), lambda i: (i, 0))
    return pl.pallas_call(
        body, name="ffn_fwd", grid=(T // tm,),
        in_specs=[row, pl.BlockSpec((1, D), lambda i: (0, 0)), ANY, ANY],
        out_specs=[row, pl.BlockSpec((2, tm, 2 * FB), lambda i: (0, i, 0))],
        out_shape=[jax.ShapeDtypeStruct((T, D), F32), jax.ShapeDtypeStruct((2, T, 2 * FB), BF16)],
        scratch_shapes=[pltpu.VMEM(win.shape, BF16), pltpu.VMEM(wout.shape, BF16), pltpu.SemaphoreType.DMA((6,))],
        compiler_params=_cp(1),
    )(x, g, win, wout)


def _mixproj_fwd(x, g, wt):
    T, D = x.shape
    W = wt.shape[0]
    QKV = ATTN_W + 2 * KV_W
    tm = min(TM_MIX, T)

    def body(x_ref, g_ref, w_ref, qkv_ref, u_ref):
        xh, _ = _rms(x_ref[...], None)
        h = (xh * g_ref[...]).astype(BF16)
        qkv_ref[...] = _dot_nt(h, w_ref[:QKV, :]).astype(BF16)
        u_ref[...] = _dot_nt(h, w_ref[QKV:, :])

    return pl.pallas_call(
        body, name="mixproj_fwd", grid=(T // tm,),
        in_specs=[pl.BlockSpec((tm, D), lambda i: (i, 0)), pl.BlockSpec((1, D), lambda i: (0, 0)),
                  pl.BlockSpec((W, D), lambda i: (0, 0))],
        out_specs=[pl.BlockSpec((tm, QKV), lambda i: (i, 0)), pl.BlockSpec((tm, W - QKV), lambda i: (i, 0))],
        out_shape=[jax.ShapeDtypeStruct((T, QKV), BF16), jax.ShapeDtypeStruct((T, W - QKV), F32)],
        compiler_params=_cp(1),
    )(x, g, wt)


def _attn_bias_table():
    rows, cols = GROUP * WINDOW, 2 * WINDOW
    row = lax.broadcasted_iota(jnp.int32, (N_KV, rows, cols), 1)
    col = lax.broadcasted_iota(jnp.int32, (N_KV, rows, cols), 2)
    head = GROUP * lax.broadcasted_iota(jnp.int32, (N_KV, rows, cols), 0) + (row >> 7)
    dist = (row & (WINDOW - 1)) + WINDOW - col
    slope = jnp.exp2(-(head + 1).astype(F32))
    return jnp.where((dist >= 0) & (dist < WINDOW), -slope * dist.astype(F32), NEG_INF)


def _first_block_mask(n):
    col = lax.broadcasted_iota(jnp.int32, (GROUP * WINDOW, 2 * WINDOW), 1)
    return (n > 0) | (col >= WINDOW)


def _sink_col(sink_ref, g):
    hi = lax.broadcasted_iota(jnp.int32, (GROUP * WINDOW, 1), 0) >> 7
    col = jnp.zeros((GROUP * WINDOW, 1), F32)
    for i in range(GROUP):
        col = jnp.where(hi == i, sink_ref[0, GROUP * g + i], col)
    return col


def _stack_heads(ref, g):
    return jnp.concatenate([ref[:, (GROUP * g + i) * HEAD_DIM:(GROUP * g + i + 1) * HEAD_DIM]
                            for i in range(GROUP)], axis=0)


def _band(kvp_ref, kvc_ref, off):
    return jnp.concatenate([kvp_ref[:, off:off + HEAD_DIM], kvc_ref[:, off:off + HEAD_DIM]], axis=0)


def _attn_probs(qs, k, bias, seen, sink):
    s = jnp.where(seen, _dot_nt(qs, k) * SCALE + bias, NEG_INF)
    m = jnp.maximum(jnp.max(s, axis=-1, keepdims=True), sink)
    p = jnp.exp(s - m)
    es = jnp.exp(sink - m)
    inv = 1.0 / (jnp.sum(p, axis=-1, keepdims=True) + es)
    return p * inv, es * inv


def _attn_fwd(sinks, tab, qkv):
    T = qkv.shape[0]
    nb = T // WINDOW

    def body(sink_ref, tab_ref, q_ref, kvp_ref, kvc_ref, o_ref):
        seen = _first_block_mask(pl.program_id(0))
        for g in range(N_KV):
            qs = _stack_heads(q_ref, g)
            k = _band(kvp_ref, kvc_ref, g * HEAD_DIM)
            v = _band(kvp_ref, kvc_ref, KV_W + g * HEAD_DIM)
            p, _ = _attn_probs(qs, k, tab_ref[g], seen, _sink_col(sink_ref, g))
            o = _dot(p.astype(BF16), v)
            for i in range(GROUP):
                h = GROUP * g + i
                o_ref[:, h * HEAD_DIM:(h + 1) * HEAD_DIM] = o[i * WINDOW:(i + 1) * WINDOW].astype(BF16)

    return pl.pallas_call(
        body, name="attn_fwd", grid=(nb,),
        in_specs=[pl.BlockSpec(memory_space=pltpu.SMEM),
                  pl.BlockSpec(tab.shape, lambda n: (0, 0, 0)),
                  pl.BlockSpec((WINDOW, ATTN_W), lambda n: (n, 0)),
                  pl.BlockSpec((WINDOW, 2 * KV_W), lambda n: (jnp.maximum(n - 1, 0), 2)),
                  pl.BlockSpec((WINDOW, 2 * KV_W), lambda n: (n, 2))],
        out_specs=pl.BlockSpec((WINDOW, ATTN_W), lambda n: (n, 0)),
        out_shape=jax.ShapeDtypeStruct((T, ATTN_W), BF16),
        compiler_params=_cp(1),
    )(sinks, tab, qkv, qkv, qkv)


def _shift_copies(src_ref, dst_ref, n):
    for b in range(1, 8):
        dst_ref[b - 1] = src_ref[b:b + n, :]


def _tap(src_ref, sh_ref, s, c0):
    a, b = divmod(s, 8)
    start = pl.multiple_of(c0 + 8 * a, 8)
    if b == 0:
        return src_ref[pl.ds(start, CONV_ROWS), :]
    return sh_ref[b - 1, pl.ds(start, CONV_ROWS), :]


def _glu_rows(u, ch):
    return u[:, :ch] * _fast_sigmoid(u[:, ch:])


def _fill_z(zs_ref, zsh_ref, uc_ref, up_ref, i, ch, n):
    zs_ref[0:HALO] = jnp.where(i > 0, _glu_rows(up_ref[...], ch), 0.0)
    zs_ref[HALO:] = _glu_rows(uc_ref[...], ch)
    _shift_copies(zs_ref, zsh_ref, n - 8)


def _conv_fwd(u, w, b, lg, lb):
    T = u.shape[0]
    CH = u.shape[1] // 2
    tm = min(TM, T)
    n = tm + HALO
    hb = tm // HALO

    def body(uc_ref, up_ref, w_ref, b_ref, lg_ref, lb_ref, conv_ref, ypre_ref, zs_ref, zsh_ref):
        i = pl.program_id(0)
        _fill_z(zs_ref, zsh_ref, uc_ref, up_ref, i, CH, n)
        bias = b_ref[...]

        def chunk(ci, carry):
            c0 = pl.multiple_of(ci * CONV_ROWS, CONV_ROWS)
            acc = jnp.broadcast_to(bias, (CONV_ROWS, CH))
            for k in range(CONV_W):
                acc = acc + w_ref[k:k + 1, :] * _tap(zs_ref, zsh_ref, HALO - (CONV_W - 1) + k, c0)
            ypre_ref[pl.ds(c0, CONV_ROWS), :] = acc
            return carry

        lax.fori_loop(0, tm // CONV_ROWS, chunk, 0)
        y = ypre_ref[...]
        mu = jnp.mean(y, axis=-1, keepdims=True)
        d = y - mu
        var = jnp.mean(d * d, axis=-1, keepdims=True)
        o = d * lax.rsqrt(var + EPS) * lg_ref[...] + lb_ref[...]
        conv_ref[...] = (o * _fast_sigmoid(o)).astype(BF16)

    vec = pl.BlockSpec((1, CH), lambda i: (0, 0))
    return pl.pallas_call(
        body, name="conv_fwd", grid=(T // tm,),
        in_specs=[pl.BlockSpec((tm, 2 * CH), lambda i: (i, 0)),
                  pl.BlockSpec((HALO, 2 * CH), lambda i: (jnp.maximum(i * hb - 1, 0), 0)),
                  pl.BlockSpec((CONV_W, CH), lambda i: (0, 0)), vec, vec, vec],
        out_specs=[pl.BlockSpec((tm, CH), lambda i: (i, 0)), pl.BlockSpec((tm, CH), lambda i: (i, 0))],
        out_shape=[jax.ShapeDtypeStruct((T, CH), BF16), jax.ShapeDtypeStruct((T, CH), F32)],
        scratch_shapes=[pltpu.VMEM((n, CH), F32), pltpu.VMEM((7, n - 8, CH), F32)],
        compiler_params=_cp(1),
    )(u, u, w, b, lg, lb)


def _mixout_fwd(x, attn, conv, wo):
    T, D = x.shape
    tm = min(TM_MIX, T)
    A = attn.shape[1]

    def body(x_ref, a_ref, c_ref, w_ref, xo_ref):
        xo_ref[...] = x_ref[...] + _dot(a_ref[...], w_ref[:A, :]) + _dot(c_ref[...], w_ref[A:, :])

    return pl.pallas_call(
        body, name="mixout_fwd", grid=(T // tm,),
        in_specs=[pl.BlockSpec((tm, D), lambda i: (i, 0)), pl.BlockSpec((tm, A), lambda i: (i, 0)),
                  pl.BlockSpec((tm, conv.shape[1]), lambda i: (i, 0)), pl.BlockSpec(wo.shape, lambda i: (0, 0))],
        out_specs=pl.BlockSpec((tm, D), lambda i: (i, 0)),
        out_shape=jax.ShapeDtypeStruct((T, D), F32),
        compiler_params=_cp(1),
    )(x, attn, conv, wo)


def _rms_bwd_rows(dh, xf, g):
    xh, r = _rms(xf, None)
    dxn = dh * g
    dx = r * (dxn - xh * jnp.mean(dxn * xh, axis=-1, keepdims=True))
    return dx, jnp.sum(dh * xh, axis=0, keepdims=True), xh * g


def _loss_head(x, g, tgt):
    T, D = x.shape
    tm = min(TM, T)

    def body(x_ref, g_ref, t_ref, loss_ref, dx_ref, dg_ref):
        @pl.when(pl.program_id(0) == 0)
        def _():
            loss_ref[...] = jnp.zeros_like(loss_ref)
            dg_ref[...] = jnp.zeros_like(dg_ref)

        xf = x_ref[...]
        g = g_ref[...]
        xh, _ = _rms(xf, None)
        e = xh * g - t_ref[...]
        loss_ref[...] += 0.5 * jnp.sum(jnp.mean(e * e, axis=-1, keepdims=True), axis=0, keepdims=True)
        dx, dg, _ = _rms_bwd_rows(e * (1.0 / D), xf, g)
        dx_ref[...] = dx
        dg_ref[...] += dg

    return pl.pallas_call(
        body, name="loss_head", grid=(T // tm,),
        in_specs=[pl.BlockSpec((tm, D), lambda i: (i, 0)), pl.BlockSpec((1, D), lambda i: (0, 0)),
                  pl.BlockSpec((tm, D), lambda i: (i, 0))],
        out_specs=[pl.BlockSpec((1, 1), lambda i: (0, 0)), pl.BlockSpec((tm, D), lambda i: (i, 0)),
                   pl.BlockSpec((1, D), lambda i: (0, 0))],
        out_shape=[jax.ShapeDtypeStruct((1, 1), F32), jax.ShapeDtypeStruct((T, D), F32),
                   jax.ShapeDtypeStruct((1, D), F32)],
        compiler_params=_cp(1),
    )(x, g, tgt)


def _ffn_bwd(dxo, x, g, gu, win, wout, dep):
    T, D = x.shape
    FB = win.shape[2]
    tm = min(TM_FFN_BWD, T)

    def body(dxo_ref, x_ref, g_ref, gu_ref, win_hbm, wout_hbm, dep_ref,
             dxi_ref, dg_ref, hb_ref, dgu_ref, a_ref, dyb_ref, win_v, wout_v, sems):
        @pl.when(pl.program_id(0) == 0)
        def _():
            _load_ffn_weights(win_hbm, wout_hbm, win_v, wout_v, sems)
            dg_ref[...] = jnp.zeros_like(dg_ref)

        dyb = (0.5 * dxo_ref[...]).astype(BF16)
        dyb_ref[...] = dyb
        dh = jnp.zeros((tm, D), F32)
        for blk in range(2):
            for lo, sz in _lane_chunks(FB):
                cols = pl.ds(blk * FB + lo, sz)
                da = _dot_nt(dyb, wout_v[cols, :])
                gate = gu_ref[0, :, cols].astype(F32)
                up = gu_ref[1, :, cols].astype(F32)
                sg = _fast_sigmoid(gate)
                s = gate * sg
                a_ref[:, cols] = (s * up).astype(BF16)
                dgate = (da * up * (sg + s * (1.0 - sg))).astype(BF16)
                dup = (da * s).astype(BF16)
                dgu_ref[0, :, cols] = dgate
                dgu_ref[1, :, cols] = dup
                dh = dh + _dot_nt(dgate, win_v[blk, :, pl.ds(lo, sz)]) + _dot_nt(dup, win_v[2 + blk, :, pl.ds(lo, sz)])
        dx, dg, h = _rms_bwd_rows(dh, x_ref[...], g_ref[...])
        dxi_ref[...] = dxo_ref[...] + dx
        dg_ref[...] += dg
        hb_ref[...] = h.astype(BF16)

    row = pl.BlockSpec((tm, D), lambda i: (i, 0))
    act = pl.BlockSpec((2, tm, 2 * FB), lambda i: (0, i, 0))
    return pl.pallas_call(
        body, name="ffn_bwd", grid=(T // tm,),
        in_specs=[row, row, pl.BlockSpec((1, D), lambda i: (0, 0)), act, ANY, ANY, ANY],
        out_specs=[row, pl.BlockSpec((1, D), lambda i: (0, 0)), row, act,
                   pl.BlockSpec((tm, 2 * FB), lambda i: (i, 0)), row],
        out_shape=[jax.ShapeDtypeStruct((T, D), F32), jax.ShapeDtypeStruct((1, D), F32),
                   jax.ShapeDtypeStruct((T, D), BF16), jax.ShapeDtypeStruct((2, T, 2 * FB), BF16),
                   jax.ShapeDtypeStruct((T, 2 * FB), BF16), jax.ShapeDtypeStruct((T, D), BF16)],
        scratch_shapes=[pltpu.VMEM(win.shape, BF16), pltpu.VMEM(wout.shape, BF16), pltpu.SemaphoreType.DMA((6,))],
        compiler_params=_cp(1),
    )(dxo, x, g, gu, win, wout, dep)


def _mix_rms_bwd(dxo, x, g, dzs, wts):
    T, D = x.shape
    tm = min(TM, T)
    npair = len(dzs)

    def body(*refs):
        dxo_ref, x_ref, g_ref = refs[:3]
        dz_refs, w_refs = refs[3:3 + npair], refs[3 + npair:3 + 2 * npair]
        dxi_ref, dg_ref, hb_ref = refs[3 + 2 * npair:]

        @pl.when(pl.program_id(0) == 0)
        def _():
            dg_ref[...] = jnp.zeros_like(dg_ref)

        dh = jnp.zeros((tm, D), F32)
        for p in range(npair):
            dh = dh + _dot(dz_refs[p][...], w_refs[p][...])
        dx, dg, h = _rms_bwd_rows(dh, x_ref[...], g_ref[...])
        dxi_ref[...] = dxo_ref[...] + dx
        dg_ref[...] += dg
        hb_ref[...] = h.astype(BF16)

    row = pl.BlockSpec((tm, D), lambda i: (i, 0))
    return pl.pallas_call(
        body, name="mix_rms_bwd", grid=(T // tm,),
        in_specs=[row, row, pl.BlockSpec((1, D), lambda i: (0, 0))]
                 + [pl.BlockSpec((tm, dz.shape[1]), lambda i: (i, 0)) for dz in dzs]
                 + [pl.BlockSpec(w.shape, lambda i: (0, 0)) for w in wts],
        out_specs=[row, pl.BlockSpec((1, D), lambda i: (0, 0)), row],
        out_shape=[jax.ShapeDtypeStruct((T, D), F32), jax.ShapeDtypeStruct((1, D), F32),
                   jax.ShapeDtypeStruct((T, D), BF16)],
        compiler_params=_cp(1),
    )(dxo, x, g, *dzs, *wts)


def _wgrad(name, a, b, a_spec, b_spec, out_shape, out_spec, nblk, dep, acc_shape):
    T = a.shape[0]
    tk = min(TK_WGRAD, T)
    nk = T // tk

    def body(a_ref, b_ref, dep_ref, o_ref, acc_ref):
        k = pl.program_id(1)

        @pl.when(k == 0)
        def _():
            acc_ref[...] = jnp.zeros_like(acc_ref)

        acc_ref[...] += _dot_tn(a_ref[...], b_ref[...])

        @pl.when(k == nk - 1)
        def _():
            o_ref[...] = acc_ref[...].reshape(o_ref.shape).astype(BF16)

    return pl.pallas_call(
        body, name=name, grid=(nblk, nk), in_specs=[a_spec, b_spec, ANY], out_specs=out_spec,
        out_shape=jax.ShapeDtypeStruct(out_shape, BF16), scratch_shapes=[pltpu.VMEM(acc_shape, F32)],
        compiler_params=_cp(2),
    )(a, b, dep)


def _wgrad_ffn_in(hb, dgu, dep):
    T, D = hb.shape
    FB = dgu.shape[2] // 2
    tk = min(TK_WGRAD, T)
    return _wgrad("wgrad_ffn_in", hb, dgu,
                  pl.BlockSpec((tk, D), lambda b, k: (k, 0)),
                  pl.BlockSpec((None, tk, FB), lambda b, k: (b // 2, k, b % 2)),
                  (4, D, FB), pl.BlockSpec((None, D, FB), lambda b, k: (b, 0, 0)), 4, dep, (D, FB))


def _wgrad_ffn_out(a, dyb, dep):
    T, D = dyb.shape
    FB = a.shape[1] // 2
    tk = min(TK_WGRAD, T)
    return _wgrad("wgrad_ffn_out", a, dyb,
                  pl.BlockSpec((tk, FB), lambda b, k: (k, b)),
                  pl.BlockSpec((tk, D), lambda b, k: (k, 0)),
                  (4, FB // 2, D), pl.BlockSpec((2, FB // 2, D), lambda b, k: (b, 0, 0)), 2, dep, (FB, D))


def _wgrad_cat(a_list, b_list):
    T = a_list[0].shape[0]
    tk = min(TK_WGRAD, T)
    nk = T // tk
    na = len(a_list)
    M, N = sum(a.shape[1] for a in a_list), sum(b.shape[1] for b in b_list)

    def body(*refs):
        a_refs, b_refs, o_ref, acc_ref = refs[:na], refs[na:-2], refs[-2], refs[-1]
        k = pl.program_id(0)

        @pl.when(k == 0)
        def _():
            acc_ref[...] = jnp.zeros_like(acc_ref)

        r0 = 0
        for a_ref in a_refs:
            c0 = 0
            for b_ref in b_refs:
                m, n = a_ref.shape[1], b_ref.shape[1]
                acc_ref[r0:r0 + m, c0:c0 + n] += _dot_tn(a_ref[...], b_ref[...])
                c0 += n
            r0 += a_ref.shape[1]

        @pl.when(k == nk - 1)
        def _():
            o_ref[...] = acc_ref[...].astype(BF16)

    return pl.pallas_call(
        body, name="wgrad_cat", grid=(nk,),
        in_specs=[pl.BlockSpec((tk, v.shape[1]), lambda k: (k, 0)) for v in list(a_list) + list(b_list)],
        out_specs=pl.BlockSpec((M, N), lambda k: (0, 0)),
        out_shape=jax.ShapeDtypeStruct((M, N), BF16), scratch_shapes=[pltpu.VMEM((M, N), F32)],
        compiler_params=_cp(1),
    )(*a_list, *b_list)


def _mixout_bwd(dxo, wo):
    T, D = dxo.shape
    tm = min(TM_MIX, T)
    A = ATTN_W
    C = wo.shape[0] - A

    def body(dxo_ref, w_ref, dyb_ref, da_ref, dc_ref):
        dyb = dxo_ref[...].astype(BF16)
        dyb_ref[...] = dyb
        da_ref[...] = _dot_nt(dyb, w_ref[:A, :]).astype(BF16)
        dc_ref[...] = _dot_nt(dyb, w_ref[A:, :])

    return pl.pallas_call(
        body, name="mixout_bwd", grid=(T // tm,),
        in_specs=[pl.BlockSpec((tm, D), lambda i: (i, 0)), pl.BlockSpec(wo.shape, lambda i: (0, 0))],
        out_specs=[pl.BlockSpec((tm, D), lambda i: (i, 0)), pl.BlockSpec((tm, A), lambda i: (i, 0)),
                   pl.BlockSpec((tm, C), lambda i: (i, 0))],
        out_shape=[jax.ShapeDtypeStruct((T, D), BF16), jax.ShapeDtypeStruct((T, A), BF16),
                   jax.ShapeDtypeStruct((T, C), F32)],
        compiler_params=_cp(1),
    )(dxo, wo)


def _conv_bwd(dconv, ypre, u, w, lg, lb):
    T, CH = dconv.shape
    tm = min(TM, T)
    n = tm + HALO
    hb = tm // HALO
    nt = T // tm
    nchunk = tm // CONV_ROWS

    def body(dc_ref, dcn_ref, yp_ref, ypn_ref, uc_ref, up_ref, w_ref, lg_ref, lb_ref,
             du_ref, dw_ref, dvec_ref, zs_ref, zsh_ref, dy_ref, dysh_ref, dz_ref, dwacc_ref):
        i = pl.program_id(0)

        @pl.when(i == 0)
        def _():
            dwacc_ref[...] = jnp.zeros_like(dwacc_ref)
            dvec_ref[...] = jnp.zeros_like(dvec_ref)

        g, bb = lg_ref[...], lb_ref[...]

        def ln_bwd(dc, yp):
            mu = jnp.mean(yp, axis=-1, keepdims=True)
            d = yp - mu
            rs = lax.rsqrt(jnp.mean(d * d, axis=-1, keepdims=True) + EPS)
            yn = d * rs
            o = yn * g + bb
            sg = _fast_sigmoid(o)
            do = dc * (sg * (1.0 + o * (1.0 - sg)))
            dyn = do * g
            dyp = rs * (dyn - jnp.mean(dyn, axis=-1, keepdims=True)
                        - yn * jnp.mean(dyn * yn, axis=-1, keepdims=True))
            return dyp, do, yn

        dyp, do, yn = ln_bwd(dc_ref[...], yp_ref[...])
        dvec_ref[0:1, :] += jnp.sum(dyp, axis=0, keepdims=True)
        dvec_ref[1:2, :] += jnp.sum(do * yn, axis=0, keepdims=True)
        dvec_ref[2:3, :] += jnp.sum(do, axis=0, keepdims=True)
        dy_ref[0:tm] = dyp
        dyh, _, _ = ln_bwd(dcn_ref[...], ypn_ref[...])
        dy_ref[tm:] = jnp.where(i < nt - 1, dyh, 0.0)
        _shift_copies(dy_ref, dysh_ref, n - 8)
        _fill_z(zs_ref, zsh_ref, uc_ref, up_ref, i, CH, n)

        def chunk(ci, carry):
            c0 = pl.multiple_of(ci * CONV_ROWS, CONV_ROWS)
            acc = jnp.zeros((CONV_ROWS, CH), F32)
            for k in range(CONV_W):
                acc = acc + w_ref[k:k + 1, :] * _tap(dy_ref, dysh_ref, CONV_W - 1 - k, c0)
            dz_ref[pl.ds(c0, CONV_ROWS), :] = acc
            dyc = dy_ref[pl.ds(c0, CONV_ROWS), :]
            for k in range(CONV_W):
                prod = dyc * _tap(zs_ref, zsh_ref, HALO - (CONV_W - 1) + k, c0)
                dwacc_ref[k] += jnp.sum(prod.reshape(CONV_ROWS // 8, 8, CH), axis=0)
            return carry

        lax.fori_loop(0, nchunk, chunk, 0)

        @pl.when(i == nt - 1)
        def _():
            dw_ref[...] = jnp.sum(dwacc_ref[...], axis=1)

        uc = uc_ref[...]
        a = uc[:, :CH]
        sg = _fast_sigmoid(uc[:, CH:])
        dz = dz_ref[...]
        du_ref[:, :CH] = (dz * sg).astype(BF16)
        du_ref[:, CH:] = (dz * a * sg * (1.0 - sg)).astype(BF16)

    cur = lambda c: pl.BlockSpec((tm, c), lambda i: (i, 0))
    nxt = lambda c: pl.BlockSpec((HALO, c), lambda i: (jnp.minimum((i + 1) * hb, T // HALO - 1), 0))
    vec = pl.BlockSpec((1, CH), lambda i: (0, 0))
    return pl.pallas_call(
        body, name="conv_bwd", grid=(nt,),
        in_specs=[cur(CH), nxt(CH), cur(CH), nxt(CH), cur(2 * CH),
                  pl.BlockSpec((HALO, 2 * CH), lambda i: (jnp.maximum(i * hb - 1, 0), 0)),
                  pl.BlockSpec((CONV_W, CH), lambda i: (0, 0)), vec, vec],
        out_specs=[pl.BlockSpec((tm, 2 * CH), lambda i: (i, 0)), pl.BlockSpec((32, CH), lambda i: (0, 0)),
                   pl.BlockSpec((8, CH), lambda i: (0, 0))],
        out_shape=[jax.ShapeDtypeStruct((T, 2 * CH), BF16), jax.ShapeDtypeStruct((32, CH), F32),
                   jax.ShapeDtypeStruct((8, CH), F32)],
        scratch_shapes=[pltpu.VMEM((n, CH), F32), pltpu.VMEM((7, n - 8, CH), F32),
                        pltpu.VMEM((n, CH), F32), pltpu.VMEM((7, n - 8, CH), F32), pltpu.VMEM((tm, CH), F32),
                        pltpu.VMEM((32, 8, CH), F32)],
        compiler_params=_cp(1),
    )(dconv, dconv, ypre, ypre, u, u, w, lg, lb)


def _attn_bwd(sinks, tab, qkv, dattn):
    T = qkv.shape[0]
    nb = T // WINDOW

    def body(sink_ref, tab_ref, q_ref, kvp_ref, kvc_ref, do_ref, dq_ref, dkv_ref, dsk_ref, carry_ref):
        n = pl.program_id(0)

        @pl.when(n == 0)
        def _():
            dsk_ref[...] = jnp.zeros_like(dsk_ref)
            carry_ref[...] = jnp.zeros_like(carry_ref)

        @pl.when(n < nb)
        def _():
            seen = _first_block_mask(n)
            for g in range(N_KV):
                qs = _stack_heads(q_ref, g)
                dos = _stack_heads(do_ref, g)
                k = _band(kvp_ref, kvc_ref, g * HEAD_DIM)
                v = _band(kvp_ref, kvc_ref, KV_W + g * HEAD_DIM)
                p, ps = _attn_probs(qs, k, tab_ref[g], seen, _sink_col(sink_ref, g))
                dp = _dot_nt(dos, v)
                delta = jnp.sum(p * dp, axis=-1, keepdims=True)
                dsb = (p * (dp - delta)).astype(BF16)
                dsink = -ps * delta
                dqs = _dot(dsb, k) * SCALE
                dk = _dot_tn(dsb, qs) * SCALE
                dv = _dot_tn(p.astype(BF16), dos)
                for i in range(GROUP):
                    h = GROUP * g + i
                    dq_ref[:, h * HEAD_DIM:(h + 1) * HEAD_DIM] = dqs[i * WINDOW:(i + 1) * WINDOW].astype(BF16)
                    dsk_ref[h:h + 1, :] += jnp.sum(dsink[i * WINDOW:(i + 1) * WINDOW], axis=0, keepdims=True)
                for off, d in ((g * HEAD_DIM, dk), (KV_W + g * HEAD_DIM, dv)):
                    dkv_ref[:, off:off + HEAD_DIM] = (carry_ref[:, off:off + HEAD_DIM] + d[:WINDOW]).astype(BF16)
                    carry_ref[:, off:off + HEAD_DIM] = d[WINDOW:]

        @pl.when(n == nb)
        def _():
            dkv_ref[...] = carry_ref[...].astype(BF16)

    last = nb - 1
    return pl.pallas_call(
        body, name="attn_bwd", grid=(nb + 1,),
        in_specs=[pl.BlockSpec(memory_space=pltpu.SMEM),
                  pl.BlockSpec(tab.shape, lambda n: (0, 0, 0)),
                  pl.BlockSpec((WINDOW, ATTN_W), lambda n: (jnp.minimum(n, last), 0)),
                  pl.BlockSpec((WINDOW, 2 * KV_W), lambda n: (jnp.clip(n - 1, 0, last), 2)),
                  pl.BlockSpec((WINDOW, 2 * KV_W), lambda n: (jnp.minimum(n, last), 2)),
                  pl.BlockSpec((WINDOW, ATTN_W), lambda n: (jnp.minimum(n, last), 0))],
        out_specs=[pl.BlockSpec((WINDOW, ATTN_W), lambda n: (jnp.minimum(n, last), 0)),
                   pl.BlockSpec((WINDOW, 2 * KV_W), lambda n: (jnp.maximum(n - 1, 0), 0)),
                   pl.BlockSpec((8, LANES), lambda n: (0, 0))],
        out_shape=[jax.ShapeDtypeStruct((T, ATTN_W), BF16), jax.ShapeDtypeStruct((T, 2 * KV_W), BF16),
                   jax.ShapeDtypeStruct((8, LANES), F32)],
        scratch_shapes=[pltpu.VMEM((WINDOW, 2 * KV_W), F32)],
        compiler_params=_cp(1),
    )(sinks, tab, qkv, qkv, qkv, dattn)


def _pack(arrs):
    flat = jnp.concatenate([a.reshape(-1) for a in arrs])
    pad = -flat.shape[0] % (8 * LANES)
    return jnp.pad(flat, (0, pad)).reshape(1, -1, LANES)


def _unpack(packed, like):
    flat = packed.reshape(-1)
    out, off = [], 0
    for a in like:
        out.append(flat[off:off + a.size].reshape(a.shape))
        off += a.size
    return out


def kernel(x, norm_ffn1, w_ffn1_in, w_ffn1_out, norm_mix, w_in, sinks, w_dw, b_dw, conv_ln_g, conv_ln_b, w_out, norm_ffn2, w_ffn2_in, w_ffn2_out, final_norm, loss_target, m_norm_ffn1, m_w_ffn1_in, m_w_ffn1_out, m_norm_mix, m_w_in, m_sinks, m_w_dw, m_b_dw, m_conv_ln_g, m_conv_ln_b, m_w_out, m_norm_ffn2, m_w_ffn2_in, m_w_ffn2_out, m_final_norm, v_norm_ffn1, v_w_ffn1_in, v_w_ffn1_out, v_norm_mix, v_w_in, v_sinks, v_w_dw, v_b_dw, v_conv_ln_g, v_conv_ln_b, v_w_out, v_norm_ffn2, v_w_ffn2_in, v_w_ffn2_out, v_final_norm):
    L, D = norm_ffn1.shape
    T = x.shape[1]
    FB = w_ffn1_in.shape[2]
    CH = b_dw.shape[1]
    QKV = ATTN_W + 2 * KV_W
    xs = x.reshape(T, D)
    tgt = loss_target.reshape(T, D)
    cx, cy, cc = lax.axis_index("x"), lax.axis_index("y"), lax.axis_index("c")
    chip = 2 * cx + cy
    cidx = cc.reshape(1).astype(jnp.int32)
    tr = lambda a_: jnp.transpose(a_, (0, 2, 1))
    big_w = (w_ffn1_in, w_ffn1_out, tr(w_in), w_out, w_ffn2_in, w_ffn2_out)
    big_m = (m_w_ffn1_in, m_w_ffn1_out, tr(m_w_in), m_w_out, m_w_ffn2_in, m_w_ffn2_out)
    big_v = (v_w_ffn1_in, v_w_ffn1_out, tr(v_w_in), v_w_out, v_w_ffn2_in, v_w_ffn2_out)
    NW = len(big_w) + 1

    def shards(l, tok):
        return [(w_[l] + tok[0, 0]).astype(BF16) for w_ in big_w] + [w_dw[l] + tok[0, 0]]

    def own_slot(a, slots=4, idx=chip):
        return lax.dynamic_update_index_in_dim(lax.empty((slots,) + a.shape, a.dtype), a, idx, 0)

    def gather_start(srcs, tok):
        return _xchg_start("gather_start", srcs, [own_slot(s_) for s_ in srcs], _gather_plan, tok)

    def gather_arrived(started, after, n, taps):
        _, lands, tok = _xchg_wait("gather_wait", started, n, n, _gather_plan, after)
        return _xchg_start("gshare_start", [], lands[:-1] if taps else lands, _gshare_plan, tok, "sibling3"), lands[-1]

    def shared_weights(shared, after, n):
        _, mats, tok = _xchg_wait("gshare_wait", shared, 0, n, _gshare_plan, after, "sibling3")
        return mats, tok

    row = lambda a, l: a[l].reshape(1, -1)
    tab = _attn_bias_table()
    NB = len(big_w)

    saved, W = [], []
    zero_tok = jnp.zeros((8, LANES), F32)
    src0 = shards(0, zero_tok)
    started = gather_start(src0[:2], zero_tok)
    rest0 = gather_start(src0[2:], started[-1])
    cast = [None] + [shards(l, rest0[-1]) for l in range(1, L)]
    shared, _ = gather_arrived(started, [xs] + [a_ for c_ in cast[1:] for a_ in c_], 2, False)
    after = [shared[-1]]
    for l in range(L):
        mats, tok = shared_weights(shared, after, 2 if l == 0 else NB)
        started = None
        if l + 1 < L:
            started = gather_start(cast[l + 1], tok)
            tok = started[-1]
        x0 = xs
        x1, gu1 = _ffn_fwd(x0, row(norm_ffn1, l) + tok[0, 0], mats[0], mats[1].reshape(2 * FB, D))
        gm_row = row(norm_mix, l)
        if l == 0:
            shared, gdw = gather_arrived(rest0, [x1], NW - 2, True)
            rest, tok = shared_weights(shared, [shared[-1]], NB - 2)
            mats = list(mats) + list(rest)
            gm_row = gm_row + tok[0, 0]
        g1i, g1o, gi, go, g2i, g2o = mats
        w = dict(f1i=g1i, f1o=g1o.reshape(2 * FB, D), f2i=g2i, f2o=g2o.reshape(2 * FB, D),
                 wit=gi.reshape(-1, D), wo=go.reshape(-1, D),
                 wdw=jnp.transpose(gdw, (1, 0, 2)).reshape(CONV_W, CH))
        W.append(w)
        qkv, u = _mixproj_fwd(x1, gm_row, w["wit"])
        attn = _attn_fwd(row(sinks, l), tab, qkv)
        conv, ypre = _conv_fwd(u, w["wdw"], row(b_dw, l), row(conv_ln_g, l), row(conv_ln_b, l))
        x2 = _mixout_fwd(x1, attn, conv, w["wo"])
        g2_row = row(norm_ffn2, l)
        if started is not None:
            shared, gdw = gather_arrived(started, [x2], NW, True)
            g2_row = g2_row + shared[-1][0, 0]
        xs, gu2 = _ffn_fwd(x2, g2_row, w["f2i"], w["f2o"])
        saved.append((x0, gu1, x1, qkv, u, attn, conv, ypre, x2, gu2))
        after = [xs]

    loss_part, dx, d_final = _loss_head(xs, final_norm.reshape(1, D), tgt)
    loss = lax.psum(loss_part[0, 0], ("x", "y", "c"))

    bufs = [[lax.empty(w_.shape, F32) for _ in range(4)] for w_ in big_w]
    d_n1, d_nm, d_n2 = [None] * L, [None] * L, [None] * L
    d_sk, d_bdw, d_lg, d_lb, d_wdw = [None] * L, [None] * L, [None] * L, [None] * L, [None] * L

    def sib_start(gs):
        return _xchg_start("sib_start", gs, [lax.empty((4, g.shape[1] // 2, g.shape[2]), g.dtype) for g in gs],
                           _sib_plan, zero_tok, "sibling")

    def reduce_start(sib_started, after, n):
        gs, sibs, _ = _xchg_wait("sib_wait", sib_started, n, n, _sib_plan, after, "sibling")
        parts = [_sum_halves(cidx, g, s_) for g, s_ in zip(gs, sibs)]
        lands = [own_slot(lax.dynamic_index_in_dim(p, chip, 0, keepdims=False)) for p in parts]
        return _xchg_start("rs_start", parts, lands, _rs_plan, zero_tok)

    def share_start(rs_started, after, n):
        _, qs, tok = _xchg_wait("rs_wait", rs_started, n, n, _rs_plan, after)
        return _xchg_start("qshare_start", qs, [lax.empty(q.shape, q.dtype) for q in qs], _whole_plan, tok, "sibling")

    def finish(l, shared, after, idxs):
        q_own, q_sib, _ = _xchg_wait("qshare_wait", shared, len(idxs), len(idxs), _whole_plan, after, "sibling")
        for k, t in enumerate(idxs):
            bufs[t] = _adamw_layer(cidx, q_own[k], q_sib[k], big_w[t], big_m[t], big_v[t], bufs[t], l)

    ALL = list(range(NB))
    EARLY, LATE = ALL[2:], ALL[:2]
    sib_pending = rs_pending = None
    shares = []
    tok = zero_tok
    for l in reversed(range(L)):
        w = W[l]
        x0, gu1, x1, qkv, u, attn, conv, ypre, x2, gu2 = saved[l]
        dx, d_n2[l], hb, dgu, a, dyb = _ffn_bwd(dx, x2, row(norm_ffn2, l), gu2, w["f2i"], w["f2o"], tok)
        g_f2i, g_f2o = _wgrad_ffn_in(hb, dgu, tok), _wgrad_ffn_out(a, dyb, tok)
        lg_row = row(conv_ln_g, l)
        if sib_pending is not None:
            rs_started = reduce_start(sib_pending[1], [g_f2o], NB)
            if rs_pending is not None:
                shares.append((rs_pending[0], share_start(rs_pending[1], [rs_started[-1]], NB)))
            rs_pending = (sib_pending[0], rs_started)
            lg_row = lg_row + rs_started[-1][0, 0]
        dyb, dattn, dconv = _mixout_bwd(dx, w["wo"])
        g_wo = _wgrad_cat([attn, conv], [dyb]).reshape(4, -1, D)
        du, dwdw, dvec = _conv_bwd(dconv, ypre, u, w["wdw"], lg_row, row(conv_ln_b, l))
        d_wdw[l], d_bdw[l], d_lg[l], d_lb[l] = dwdw[:CONV_W], dvec[0], dvec[1], dvec[2]
        dq, dkv, dsk = _attn_bwd(row(sinks, l), tab, qkv, dattn)
        d_sk[l] = dsk[:, 0]
        wit = w["wit"]
        dx, d_nm[l], hb = _mix_rms_bwd(dx, x1, row(norm_mix, l), [dq, dkv, du],
                                       [wit[:ATTN_W], wit[ATTN_W:QKV], wit[QKV:]])
        g_wi = _wgrad_cat([dq, dkv, du], [hb]).reshape(4, -1, D)
        if l == 0:
            sib_early = sib_start([g_wi, g_wo, g_f2i, g_f2o])
            tok = sib_early[-1]
        dx, d_n1[l], hb, dgu, a, dyb = _ffn_bwd(dx, x0, row(norm_ffn1, l), gu1, w["f1i"], w["f1o"], tok)
        if l == 0:
            rs_early = reduce_start(sib_early, [dx], len(EARLY))
            tok = rs_early[-1]
        g_f1i, g_f1o = _wgrad_ffn_in(hb, dgu, tok), _wgrad_ffn_out(a, dyb, tok)
        sib_started = sib_start([g_f1i, g_f1o] if l == 0 else [g_f1i, g_f1o, g_wi, g_wo, g_f2i, g_f2o])
        tok = sib_started[-1]
        sib_pending = (l, sib_started)
    grad_x = dx.reshape(x.shape)

    small_g = [jnp.concatenate(d, axis=0) for d in (d_n1, d_nm, d_n2)] + [d_final, jnp.stack(d_sk)] + \
              [jnp.stack(d) for d in (d_bdw, d_lg, d_lb, d_wdw)]
    packed = _pack(small_g)[0]
    small_started = _xchg_start("small_start", [packed], [own_slot(packed, 8, 4 * cx + 2 * cy + cc)], _slot_plan, tok, "all")

    after = [small_started[-1]]
    if rs_pending is not None:
        shares.append((rs_pending[0], share_start(rs_pending[1], after, NB)))
        after = [shares[-1][1][-1]]
    if shares:
        finish(*shares.pop(0), after, ALL)
        after = [b_[0] for b_ in bufs]
    rs_late = reduce_start(sib_pending[1], after, len(LATE))
    after = [rs_late[-1]]
    for l, sh in shares:
        finish(l, sh, after, ALL)
        after = [b_[0] for b_ in bufs]
    _, (slots,), _ = _xchg_wait("small_wait", small_started, 1, 1, _slot_plan, after, "all")
    small_sum = _unpack(_sum_slots(slots), small_g)
    g_wdw = lax.dynamic_slice_in_dim(small_sum[8], chip * w_dw.shape[2], w_dw.shape[2], axis=2)
    small_g = [small_sum[0], small_sum[1], small_sum[2], small_sum[3].reshape(D), small_sum[4],
               small_sum[5], small_sum[6], small_sum[7], g_wdw]
    small_w = (norm_ffn1, norm_mix, norm_ffn2, final_norm, sinks, b_dw, conv_ln_g, conv_ln_b, w_dw)
    small_m = (m_norm_ffn1, m_norm_mix, m_norm_ffn2, m_final_norm, m_sinks, m_b_dw, m_conv_ln_g, m_conv_ln_b, m_w_dw)
    small_v = (v_norm_ffn1, v_norm_mix, v_norm_ffn2, v_final_norm, v_sinks, v_b_dw, v_conv_ln_g, v_conv_ln_b, v_w_dw)
    upd = _adamw(_pack(small_g), _pack(small_w), _pack(small_m), _pack(small_v))
    small_upd = [_unpack(u_, small_w) for u_ in upd]
    sh_early = share_start(rs_early, [upd[0]], len(EARLY))
    sh_late = share_start(rs_late, [sh_early[-1]], len(LATE))
    finish(0, sh_early, [sh_late[-1]], EARLY)
    finish(0, sh_late, [bufs[t][0] for t in EARLY], LATE)

    order = ("norm_ffn1", "w_ffn1_in", "w_ffn1_out", "norm_mix", "w_in", "sinks", "w_dw", "b_dw", "conv_ln_g",
             "conv_ln_b", "w_out", "norm_ffn2", "w_ffn2_in", "w_ffn2_out", "final_norm")
    small_names = ("norm_ffn1", "norm_mix", "norm_ffn2", "final_norm", "sinks", "b_dw", "conv_ln_g", "conv_ln_b", "w_dw")
    big_names = ("w_ffn1_in", "w_ffn1_out", "w_in", "w_out", "w_ffn2_in", "w_ffn2_out")
    grads, deltas, new_m, new_v = {}, {}, {}, {}
    for i, nme in enumerate(small_names):
        grads[nme], deltas[nme], new_m[nme], new_v[nme] = small_g[i], small_upd[0][i], small_upd[1][i], small_upd[2][i]
    for i, nme in enumerate(big_names):
        grads[nme], deltas[nme], new_m[nme], new_v[nme] = [tr(b_) for b_ in bufs[i]] if nme == "w_in" else bufs[i]
    return (loss, grad_x, *[grads[n] for n in order], *[deltas[n] for n in order],
            *[new_m[n] for n in order], *[new_v[n] for n in order])
```

```python
import functools

import jax
import jax.numpy as jnp
from jax import lax
from jax.experimental import pallas as pl
from jax.experimental.pallas import tpu as pltpu

F32, BF16 = jnp.float32, jnp.bfloat16
EPS = 1e-6
NEG_INF = -1e30
HEAD_DIM = 64
N_HEADS = 8
N_KV = 2
GROUP = N_HEADS // N_KV
WINDOW = 128
ATTN_W = N_HEADS * HEAD_DIM
KV_W = N_KV * HEAD_DIM
CONV_W = 31
HALO = 32
CONV_ROWS = 32
SCALE = 1.0 / 8.0
ADAM_LR, ADAM_B1, ADAM_B2, ADAM_EPS, ADAM_WD, ADAM_STEP = 0.001, 0.9, 0.999, 1e-08, 0.01, 10
TM = 512
TM_FFN_BWD = 256
TK_WGRAD = 2048
TM_MIX = 1024
LANES = 128
VMEM_LIMIT = 52 * 1024 * 1024
MESH = pl.DeviceIdType.MESH
ANY = pl.BlockSpec(memory_space=pl.ANY)
HBM = pl.BlockSpec(memory_space=pltpu.HBM)
SEM = pl.BlockSpec(memory_space=pltpu.SEMAPHORE)
VMEM = pl.BlockSpec(memory_space=pltpu.VMEM)
EFFECT = pltpu.SideEffectType.DATAFLOW_SIDE_EFFECTING
TOKEN = jax.ShapeDtypeStruct((8, LANES), F32)


def _cp(n):
    return pltpu.CompilerParams(dimension_semantics=("arbitrary",) * n, vmem_limit_bytes=VMEM_LIMIT)


def _dot(a, b):
    return jnp.dot(a, b, preferred_element_type=F32)


def _dot_nt(a, b):
    return lax.dot_general(a, b, (((1,), (1,)), ((), ())), preferred_element_type=F32)


def _dot_tn(a, b):
    return lax.dot_general(a, b, (((0,), (0,)), ((), ())), preferred_element_type=F32)


def _place():
    x, y, c = lax.axis_index("x"), lax.axis_index("y"), lax.axis_index("c")
    chips = [(1 - x, y), (x, 1 - y), (1 - x, 1 - y)]
    return x, y, c, chips


def _rcopy(src, dst, send_sems, recv_sems, k, dev):
    return pltpu.make_async_remote_copy(src_ref=src, dst_ref=dst, send_sem=send_sems.at[k],
                                        recv_sem=recv_sems.at[k], device_id=dev, device_id_type=MESH)


def _hbm(a):
    return pltpu.with_memory_space_constraint(a, pltpu.HBM)


PEERS = {"chips": 3, "sibling": 1, "sibling3": 3, "all": 7}


def _targets(mode):
    x, y, c, chips = _place()
    b = 2 * x + y
    if mode == "chips":
        return b, c, [((px, py, c), 2 * px + py) for px, py in chips]
    if mode == "sibling":
        return b, c, [((x, y, 1 - c), b)]
    if mode == "sibling3":
        return b, c, [((x, y, 1 - c), 2 * px + py) for px, py in chips]
    flip = lambda v, f: 1 - v if f else v
    devs = [(flip(x, k >> 2 & 1), flip(y, k >> 1 & 1), flip(c, k & 1)) for k in range(1, 8)]
    return 4 * x + 2 * y + c, c, [(d, 4 * d[0] + 2 * d[1] + d[2]) for d in devs]


def _xchg_start(name, srcs, lands, plan, dep, mode="chips"):
    ns, nl, npeer = len(srcs), len(lands), PEERS[mode]

    def body(*refs):
        land = refs[ns:ns + nl]
        src = refs[:ns] if ns else land
        send_sems, recv_sems, token = refs[ns + nl + 1], refs[ns + nl + 2], refs[-1]
        me, c, peers = _targets(mode)
        for t in range(nl):
            for j, (dev, tag) in enumerate(peers):
                s, d, _ = plan(src[t], land[t], t, me, c, tag)
                _rcopy(s, d, send_sems, recv_sems, npeer * t + j, dev).start()
        token[...] = jnp.zeros_like(token)

    arrs = list(srcs) + list(lands)
    return pl.pallas_call(
        body, name=name,
        out_shape=(pltpu.SemaphoreType.DMA((npeer * nl,)), pltpu.SemaphoreType.DMA((npeer * nl,)),
                   *[pltpu.HBM(a.shape, a.dtype) for a in arrs], TOKEN),
        in_specs=[HBM] * (ns + nl) + [ANY], out_specs=(SEM, SEM, *[HBM] * (ns + nl), VMEM),
        input_output_aliases={i: 2 + i for i in range(ns + nl)},
        compiler_params=pltpu.CompilerParams(has_side_effects=EFFECT),
    )(*[_hbm(a) for a in arrs], dep)


def _xchg_wait(name, started, ns, nl, plan, after, mode="chips"):
    send_sems, recv_sems, thru = started[0], started[1], started[2:2 + ns + nl]
    npeer = PEERS[mode]

    def body(*refs):
        land = refs[ns:ns + nl]
        src = refs[:ns] if ns else land
        send_sems, recv_sems, token = refs[ns + nl], refs[ns + nl + 1], refs[-1]
        me, c, peers = _targets(mode)
        for t in range(nl):
            for j, (dev, tag) in enumerate(peers):
                s, _, a = plan(src[t], land[t], t, me, c, tag)
                cp = _rcopy(s, a, send_sems, recv_sems, npeer * t + j, dev)
                cp.wait_send()
                cp.wait_recv()
        token[...] = jnp.zeros_like(token)

    out = pl.pallas_call(
        body, name=name,
        out_shape=(*[pltpu.HBM(a.shape, a.dtype) for a in thru], TOKEN),
        in_specs=[HBM] * (ns + nl) + [SEM, SEM] + [ANY] * len(after), out_specs=(*[HBM] * (ns + nl), VMEM),
        input_output_aliases={i: i for i in range(ns + nl)},
        compiler_params=pltpu.CompilerParams(has_side_effects=EFFECT),
    )(*thru, send_sems, recv_sems, *after)
    return out[:ns], out[ns:ns + nl], out[-1]


def _half(ref_rows, which):
    h = ref_rows // 2
    return pl.ds(which * h, h)


def _gather_plan(src, land, t, b, c, pb):
    if land.shape[1] % 2 == 0:
        hs = _half(land.shape[1], c)
        return land.at[b, hs], land.at[b, hs], land.at[pb, hs]
    return land.at[b], land.at[b], land.at[pb]


def _gshare_plan(src, land, t, b, c, pb):
    return land.at[pb, _half(land.shape[1], c)], land.at[pb, _half(land.shape[1], c)], land.at[pb, _half(land.shape[1], 1 - c)]


def _rs_plan(src, land, t, b, c, pb):
    return src.at[pb], land.at[b], land.at[pb]


def _sib_plan(src, land, t, b, c, pb):
    return src.at[:, _half(src.shape[1], 1 - c), :], land, land


def _rows_block(h, cap=512):
    for rb in range(min(h, cap) // 16 * 16, 0, -16):
        if h % rb == 0:
            return rb
    return h


def _sum_halves(cidx, g, s):
    _, R, C = g.shape
    rb = _rows_block(R // 2)
    nr = R // 2 // rb

    def body(c_ref, g_ref, s_ref, o_ref):
        o_ref[...] = (g_ref[...].astype(F32) + s_ref[...].astype(F32)).astype(BF16)

    blk = (None, rb, C)
    return pl.pallas_call(
        body, name="sum_halves", out_shape=jax.ShapeDtypeStruct(s.shape, BF16),
        grid_spec=pltpu.PrefetchScalarGridSpec(
            num_scalar_prefetch=1, grid=(4, nr),
            in_specs=[pl.BlockSpec(blk, lambda p, i, c: (p, c[0] * nr + i, 0)),
                      pl.BlockSpec(blk, lambda p, i, c: (p, i, 0))],
            out_specs=pl.BlockSpec(blk, lambda p, i, c: (p, i, 0))),
        compiler_params=_cp(2),
    )(cidx, g, s)


def _whole_plan(src, land, t, me, c, tag):
    return src, land, land


def _slot_plan(src, land, t, me, c, tag):
    return src, land.at[me], land.at[tag]


def _adam_update(gg, w, m, v):
    m2 = ADAM_B1 * m + (1.0 - ADAM_B1) * gg
    v2 = ADAM_B2 * v + (1.0 - ADAM_B2) * (gg * gg)
    mh = m2 / (1.0 - ADAM_B1 ** ADAM_STEP)
    vh = v2 / (1.0 - ADAM_B2 ** ADAM_STEP)
    return -ADAM_LR * (mh / (jnp.sqrt(vh) + ADAM_EPS) + ADAM_WD * w), m2, v2


def _adamw_layer(cidx, q_own, q_sib, w, m, v, bufs, l):
    L, R, C = w.shape
    h = R // 2
    rb = _rows_block(h, 256)
    nr = h // rb

    def body(c_ref, qo_ref, qs_ref, w_ref, m_ref, v_ref, *rest):
        g_ref, d_ref, mo_ref, vo_ref = rest[-4:]
        own = pl.program_id(0) == c_ref[0]
        gg = jnp.zeros((rb, C), F32)
        for s in range(4):
            gg = gg + jnp.where(own, qo_ref[s], qs_ref[s]).astype(F32)
        g_ref[...] = gg
        d_ref[...], mo_ref[...], vo_ref[...] = _adam_update(gg, w_ref[...], m_ref[...], v_ref[...])

    q_own_spec = pl.BlockSpec((4, rb, C), lambda hh, i, c: (0, jnp.where(hh == c[0], i, 0), 0))
    q_sib_spec = pl.BlockSpec((4, rb, C), lambda hh, i, c: (0, jnp.where(hh == c[0], 0, i), 0))
    wspec = pl.BlockSpec((None, rb, C), lambda hh, i, c: (l, hh * nr + i, 0))
    return pl.pallas_call(
        body, name="adamw_layer", out_shape=[jax.ShapeDtypeStruct(w.shape, F32)] * 4,
        grid_spec=pltpu.PrefetchScalarGridSpec(
            num_scalar_prefetch=1, grid=(2, nr),
            in_specs=[q_own_spec, q_sib_spec, wspec, wspec, wspec] + [ANY] * 4, out_specs=[wspec] * 4),
        input_output_aliases={6 + k: k for k in range(4)},
        compiler_params=_cp(2),
    )(cidx, q_own, q_sib, w, m, v, *bufs)


def _adamw(g, w, m, v):
    L, R, C = g.shape
    rb = _rows_block(R)

    def body(g_ref, w_ref, m_ref, v_ref, d_ref, mo_ref, vo_ref):
        d_ref[...], mo_ref[...], vo_ref[...] = _adam_update(g_ref[...], w_ref[...], m_ref[...], v_ref[...])

    spec = pl.BlockSpec((None, rb, C), lambda l, i: (l, i, 0))
    return pl.pallas_call(
        body, name="adamw", grid=(L, R // rb), in_specs=[spec] * 4, out_specs=[spec] * 3,
        out_shape=[jax.ShapeDtypeStruct(g.shape, F32)] * 3, compiler_params=_cp(2),
    )(g, w, m, v)


def _sum_slots(buf):
    def body(b_ref, o_ref):
        acc = b_ref[0]
        for k in range(1, 8):
            acc = acc + b_ref[k]
        o_ref[...] = acc

    return pl.pallas_call(body, name="sum_slots", in_specs=[VMEM], out_specs=VMEM,
                          out_shape=jax.ShapeDtypeStruct(buf.shape[1:], F32))(buf)


def _rms(xf, g):
    r = lax.rsqrt(jnp.mean(xf * xf, axis=-1, keepdims=True) + EPS)
    return xf * r, r


def _lane_chunks(n):
    lo = (n // LANES + 1) // 2 * LANES
    return ((0, lo), (lo, n - lo))


def _load_ffn_weights(win_hbm, wout_hbm, win_v, wout_v, sems):
    fb = win_v.shape[2]
    loads = [pltpu.make_async_copy(win_hbm.at[k], win_v.at[k], sems.at[k]) for k in range(4)]
    loads += [pltpu.make_async_copy(wout_hbm.at[pl.ds(k * fb, fb)], wout_v.at[pl.ds(k * fb, fb)], sems.at[4 + k])
              for k in range(2)]
    for cp in loads:
        cp.start()
    for cp in loads:
        cp.wait()


def _fast_sigmoid(v):
    return pl.reciprocal(1.0 + jnp.exp(-v), approx=True)


def _ffn_fwd(x, g, win, wout):
    T, D = x.shape
    FB = win.shape[2]
    tm = min(TM, T)

    def body(x_ref, g_ref, win_hbm, wout_hbm, xo_ref, gu_ref, win_v, wout_v, sems):
        @pl.when(pl.program_id(0) == 0)
        def _():
            _load_ffn_weights(win_hbm, wout_hbm, win_v, wout_v, sems)

        xf = x_ref[...]
        xh, _ = _rms(xf, None)
        h = (xh * g_ref[...]).astype(BF16)
        acc = jnp.zeros((tm, D), F32)
        for blk in range(2):
            for lo, sz in _lane_chunks(FB):
                cols = pl.ds(blk * FB + lo, sz)
                gate = _dot(h, win_v[blk, :, pl.ds(lo, sz)])
                up = _dot(h, win_v[2 + blk, :, pl.ds(lo, sz)])
                gu_ref[0, :, cols] = gate.astype(BF16)
                gu_ref[1, :, cols] = up.astype(BF16)
                a = (gate * _fast_sigmoid(gate) * up).astype(BF16)
                acc = acc + _dot(a, wout_v[cols, :])
        xo_ref[...] = xf + 0.5 * acc

    row = pl.BlockSpec((tm, D), lambda i: (i, 0))
    return pl.pallas_call(
        body, name="ffn_fwd", grid=(T // tm,),
        in_specs=[row, pl.BlockSpec((1, D), lambda i: (0, 0)), ANY, ANY],
        out_specs=[row, pl.BlockSpec((2, tm, 2 * FB), lambda i: (0, i, 0))],
        out_shape=[jax.ShapeDtypeStruct((T, D), F32), jax.ShapeDtypeStruct((2, T, 2 * FB), BF16)],
        scratch_shapes=[pltpu.VMEM(win.shape, BF16), pltpu.VMEM(wout.shape, BF16), pltpu.SemaphoreType.DMA((6,))],
        compiler_params=_cp(1),
    )(x, g, win, wout)


def _mixproj_fwd(x, g, wt):
    T, D = x.shape
    W = wt.shape[0]
    QKV = ATTN_W + 2 * KV_W
    tm = min(TM_MIX, T)

    def body(x_ref, g_ref, w_ref, qkv_ref, u_ref):
        xh, _ = _rms(x_ref[...], None)
        h = (xh * g_ref[...]).astype(BF16)
        qkv_ref[...] = _dot_nt(h, w_ref[:QKV, :]).astype(BF16)
        u_ref[...] = _dot_nt(h, w_ref[QKV:, :])

    return pl.pallas_call(
        body, name="mixproj_fwd", grid=(T // tm,),
        in_specs=[pl.BlockSpec((tm, D), lambda i: (i, 0)), pl.BlockSpec((1, D), lambda i: (0, 0)),
                  pl.BlockSpec((W, D), lambda i: (0, 0))],
        out_specs=[pl.BlockSpec((tm, QKV), lambda i: (i, 0)), pl.BlockSpec((tm, W - QKV), lambda i: (i, 0))],
        out_shape=[jax.ShapeDtypeStruct((T, QKV), BF16), jax.ShapeDtypeStruct((T, W - QKV), F32)],
        compiler_params=_cp(1),
    )(x, g, wt)


def _attn_bias_table():
    rows, cols = GROUP * WINDOW, 2 * WINDOW
    row = lax.broadcasted_iota(jnp.int32, (N_KV, rows, cols), 1)
    col = lax.broadcasted_iota(jnp.int32, (N_KV, rows, cols), 2)
    head = GROUP * lax.broadcasted_iota(jnp.int32, (N_KV, rows, cols), 0) + (row >> 7)
    dist = (row & (WINDOW - 1)) + WINDOW - col
    slope = jnp.exp2(-(head + 1).astype(F32))
    return jnp.where((dist >= 0) & (dist < WINDOW), -slope * dist.astype(F32), NEG_INF)


def _first_block_mask(n):
    col = lax.broadcasted_iota(jnp.int32, (GROUP * WINDOW, 2 * WINDOW), 1)
    return (n > 0) | (col >= WINDOW)


def _sink_col(sink_ref, g):
    hi = lax.broadcasted_iota(jnp.int32, (GROUP * WINDOW, 1), 0) >> 7
    col = jnp.zeros((GROUP * WINDOW, 1), F32)
    for i in range(GROUP):
        col = jnp.where(hi == i, sink_ref[0, GROUP * g + i], col)
    return col


def _stack_heads(ref, g):
    return jnp.concatenate([ref[:, (GROUP * g + i) * HEAD_DIM:(GROUP * g + i + 1) * HEAD_DIM]
                            for i in range(GROUP)], axis=0)


def _band(kvp_ref, kvc_ref, off):
    return jnp.concatenate([kvp_ref[:, off:off + HEAD_DIM], kvc_ref[:, off:off + HEAD_DIM]], axis=0)


def _attn_probs(qs, k, bias, seen, sink):
    s = jnp.where(seen, _dot_nt(qs, k) * SCALE + bias, NEG_INF)
    m = jnp.maximum(jnp.max(s, axis=-1, keepdims=True), sink)
    p = jnp.exp(s - m)
    es = jnp.exp(sink - m)
    inv = 1.0 / (jnp.sum(p, axis=-1, keepdims=True) + es)
    return p * inv, es * inv


def _attn_fwd(sinks, tab, qkv):
    T = qkv.shape[0]
    nb = T // WINDOW

    def body(sink_ref, tab_ref, q_ref, kvp_ref, kvc_ref, o_ref):
        seen = _first_block_mask(pl.program_id(0))
        for g in range(N_KV):
            qs = _stack_heads(q_ref, g)
            k = _band(kvp_ref, kvc_ref, g * HEAD_DIM)
            v = _band(kvp_ref, kvc_ref, KV_W + g * HEAD_DIM)
            p, _ = _attn_probs(qs, k, tab_ref[g], seen, _sink_col(sink_ref, g))
            o = _dot(p.astype(BF16), v)
            for i in range(GROUP):
                h = GROUP * g + i
                o_ref[:, h * HEAD_DIM:(h + 1) * HEAD_DIM] = o[i * WINDOW:(i + 1) * WINDOW].astype(BF16)

    return pl.pallas_call(
        body, name="attn_fwd", grid=(nb,),
        in_specs=[pl.BlockSpec(memory_space=pltpu.SMEM),
                  pl.BlockSpec(tab.shape, lambda n: (0, 0, 0)),
                  pl.BlockSpec((WINDOW, ATTN_W), lambda n: (n, 0)),
                  pl.BlockSpec((WINDOW, 2 * KV_W), lambda n: (jnp.maximum(n - 1, 0), 2)),
                  pl.BlockSpec((WINDOW, 2 * KV_W), lambda n: (n, 2))],
        out_specs=pl.BlockSpec((WINDOW, ATTN_W), lambda n: (n, 0)),
        out_shape=jax.ShapeDtypeStruct((T, ATTN_W), BF16),
        compiler_params=_cp(1),
    )(sinks, tab, qkv, qkv, qkv)


def _shift_copies(src_ref, dst_ref, n):
    for b in range(1, 8):
        dst_ref[b - 1] = src_ref[b:b + n, :]


def _tap(src_ref, sh_ref, s, c0):
    a, b = divmod(s, 8)
    start = pl.multiple_of(c0 + 8 * a, 8)
    if b == 0:
        return src_ref[pl.ds(start, CONV_ROWS), :]
    return sh_ref[b - 1, pl.ds(start, CONV_ROWS), :]


def _glu_rows(u, ch):
    return u[:, :ch] * _fast_sigmoid(u[:, ch:])


def _fill_z(zs_ref, zsh_ref, uc_ref, up_ref, i, ch, n):
    zs_ref[0:HALO] = jnp.where(i > 0, _glu_rows(up_ref[...], ch), 0.0)
    zs_ref[HALO:] = _glu_rows(uc_ref[...], ch)
    _shift_copies(zs_ref, zsh_ref, n - 8)


def _conv_fwd(u, w, b, lg, lb):
    T = u.shape[0]
    CH = u.shape[1] // 2
    tm = min(TM, T)
    n = tm + HALO
    hb = tm // HALO

    def body(uc_ref, up_ref, w_ref, b_ref, lg_ref, lb_ref, conv_ref, ypre_ref, zs_ref, zsh_ref):
        i = pl.program_id(0)
        _fill_z(zs_ref, zsh_ref, uc_ref, up_ref, i, CH, n)
        bias = b_ref[...]

        def chunk(ci, carry):
            c0 = pl.multiple_of(ci * CONV_ROWS, CONV_ROWS)
            acc = jnp.broadcast_to(bias, (CONV_ROWS, CH))
            for k in range(CONV_W):
                acc = acc + w_ref[k:k + 1, :] * _tap(zs_ref, zsh_ref, HALO - (CONV_W - 1) + k, c0)
            ypre_ref[pl.ds(c0, CONV_ROWS), :] = acc
            return carry

        lax.fori_loop(0, tm // CONV_ROWS, chunk, 0)
        y = ypre_ref[...]
        mu = jnp.mean(y, axis=-1, keepdims=True)
        d = y - mu
        var = jnp.mean(d * d, axis=-1, keepdims=True)
        o = d * lax.rsqrt(var + EPS) * lg_ref[...] + lb_ref[...]
        conv_ref[...] = (o * _fast_sigmoid(o)).astype(BF16)

    vec = pl.BlockSpec((1, CH), lambda i: (0, 0))
    return pl.pallas_call(
        body, name="conv_fwd", grid=(T // tm,),
        in_specs=[pl.BlockSpec((tm, 2 * CH), lambda i: (i, 0)),
                  pl.BlockSpec((HALO, 2 * CH), lambda i: (jnp.maximum(i * hb - 1, 0), 0)),
                  pl.BlockSpec((CONV_W, CH), lambda i: (0, 0)), vec, vec, vec],
        out_specs=[pl.BlockSpec((tm, CH), lambda i: (i, 0)), pl.BlockSpec((tm, CH), lambda i: (i, 0))],
        out_shape=[jax.ShapeDtypeStruct((T, CH), BF16), jax.ShapeDtypeStruct((T, CH), F32)],
        scratch_shapes=[pltpu.VMEM((n, CH), F32), pltpu.VMEM((7, n - 8, CH), F32)],
        compiler_params=_cp(1),
    )(u, u, w, b, lg, lb)


def _mixout_fwd(x, attn, conv, wo):
    T, D = x.shape
    tm = min(TM_MIX, T)
    A = attn.shape[1]

    def body(x_ref, a_ref, c_ref, w_ref, xo_ref):
        xo_ref[...] = x_ref[...] + _dot(a_ref[...], w_ref[:A, :]) + _dot(c_ref[...], w_ref[A:, :])

    return pl.pallas_call(
        body, name="mixout_fwd", grid=(T // tm,),
        in_specs=[pl.BlockSpec((tm, D), lambda i: (i, 0)), pl.BlockSpec((tm, A), lambda i: (i, 0)),
                  pl.BlockSpec((tm, conv.shape[1]), lambda i: (i, 0)), pl.BlockSpec(wo.shape, lambda i: (0, 0))],
        out_specs=pl.BlockSpec((tm, D), lambda i: (i, 0)),
        out_shape=jax.ShapeDtypeStruct((T, D), F32),
        compiler_params=_cp(1),
    )(x, attn, conv, wo)


def _rms_bwd_rows(dh, xf, g):
    xh, r = _rms(xf, None)
    dxn = dh * g
    dx = r * (dxn - xh * jnp.mean(dxn * xh, axis=-1, keepdims=True))
    return dx, jnp.sum(dh * xh, axis=0, keepdims=True), xh * g


def _loss_head(x, g, tgt):
    T, D = x.shape
    tm = min(TM, T)

    def body(x_ref, g_ref, t_ref, loss_ref, dx_ref, dg_ref):
        @pl.when(pl.program_id(0) == 0)
        def _():
            loss_ref[...] = jnp.zeros_like(loss_ref)
            dg_ref[...] = jnp.zeros_like(dg_ref)

        xf = x_ref[...]
        g = g_ref[...]
        xh, _ = _rms(xf, None)
        e = xh * g - t_ref[...]
        loss_ref[...] += 0.5 * jnp.sum(jnp.mean(e * e, axis=-1, keepdims=True), axis=0, keepdims=True)
        dx, dg, _ = _rms_bwd_rows(e * (1.0 / D), xf, g)
        dx_ref[...] = dx
        dg_ref[...] += dg

    return pl.pallas_call(
        body, name="loss_head", grid=(T // tm,),
        in_specs=[pl.BlockSpec((tm, D), lambda i: (i, 0)), pl.BlockSpec((1, D), lambda i: (0, 0)),
                  pl.BlockSpec((tm, D), lambda i: (i, 0))],
        out_specs=[pl.BlockSpec((1, 1), lambda i: (0, 0)), pl.BlockSpec((tm, D), lambda i: (i, 0)),
                   pl.BlockSpec((1, D), lambda i: (0, 0))],
        out_shape=[jax.ShapeDtypeStruct((1, 1), F32), jax.ShapeDtypeStruct((T, D), F32),
                   jax.ShapeDtypeStruct((1, D), F32)],
        compiler_params=_cp(1),
    )(x, g, tgt)


def _ffn_bwd(dxo, x, g, gu, win, wout, dep):
    T, D = x.shape
    FB = win.shape[2]
    tm = min(TM_FFN_BWD, T)

    def body(dxo_ref, x_ref, g_ref, gu_ref, win_hbm, wout_hbm, dep_ref,
             dxi_ref, dg_ref, hb_ref, dgu_ref, a_ref, dyb_ref, win_v, wout_v, sems):
        @pl.when(pl.program_id(0) == 0)
        def _():
            _load_ffn_weights(win_hbm, wout_hbm, win_v, wout_v, sems)
            dg_ref[...] = jnp.zeros_like(dg_ref)

        dyb = (0.5 * dxo_ref[...]).astype(BF16)
        dyb_ref[...] = dyb
        dh = jnp.zeros((tm, D), F32)
        for blk in range(2):
            for lo, sz in _lane_chunks(FB):
                cols = pl.ds(blk * FB + lo, sz)
                da = _dot_nt(dyb, wout_v[cols, :])
                gate = gu_ref[0, :, cols].astype(F32)
                up = gu_ref[1, :, cols].astype(F32)
                sg = _fast_sigmoid(gate)
                s = gate * sg
                a_ref[:, cols] = (s * up).astype(BF16)
                dgate = (da * up * (sg + s * (1.0 - sg))).astype(BF16)
                dup = (da * s).astype(BF16)
                dgu_ref[0, :, cols] = dgate
                dgu_ref[1, :, cols] = dup
                dh = dh + _dot_nt(dgate, win_v[blk, :, pl.ds(lo, sz)]) + _dot_nt(dup, win_v[2 + blk, :, pl.ds(lo, sz)])
        dx, dg, h = _rms_bwd_rows(dh, x_ref[...], g_ref[...])
        dxi_ref[...] = dxo_ref[...] + dx
        dg_ref[...] += dg
        hb_ref[...] = h.astype(BF16)

    row = pl.BlockSpec((tm, D), lambda i: (i, 0))
    act = pl.BlockSpec((2, tm, 2 * FB), lambda i: (0, i, 0))
    return pl.pallas_call(
        body, name="ffn_bwd", grid=(T // tm,),
        in_specs=[row, row, pl.BlockSpec((1, D), lambda i: (0, 0)), act, ANY, ANY, ANY],
        out_specs=[row, pl.BlockSpec((1, D), lambda i: (0, 0)), row, act,
                   pl.BlockSpec((tm, 2 * FB), lambda i: (i, 0)), row],
        out_shape=[jax.ShapeDtypeStruct((T, D), F32), jax.ShapeDtypeStruct((1, D), F32),
                   jax.ShapeDtypeStruct((T, D), BF16), jax.ShapeDtypeStruct((2, T, 2 * FB), BF16),
                   jax.ShapeDtypeStruct((T, 2 * FB), BF16), jax.ShapeDtypeStruct((T, D), BF16)],
        scratch_shapes=[pltpu.VMEM(win.shape, BF16), pltpu.VMEM(wout.shape, BF16), pltpu.SemaphoreType.DMA((6,))],
        compiler_params=_cp(1),
    )(dxo, x, g, gu, win, wout, dep)


def _mix_rms_bwd(dxo, x, g, dzs, wts):
    T, D = x.shape
    tm = min(TM, T)
    npair = len(dzs)

    def body(*refs):
        dxo_ref, x_ref, g_ref = refs[:3]
        dz_refs, w_refs = refs[3:3 + npair], refs[3 + npair:3 + 2 * npair]
        dxi_ref, dg_ref, hb_ref = refs[3 + 2 * npair:]

        @pl.when(pl.program_id(0) == 0)
        def _():
            dg_ref[...] = jnp.zeros_like(dg_ref)

        dh = jnp.zeros((tm, D), F32)
        for p in range(npair):
            dh = dh + _dot(dz_refs[p][...], w_refs[p][...])
        dx, dg, h = _rms_bwd_rows(dh, x_ref[...], g_ref[...])
        dxi_ref[...] = dxo_ref[...] + dx
        dg_ref[...] += dg
        hb_ref[...] = h.astype(BF16)

    row = pl.BlockSpec((tm, D), lambda i: (i, 0))
    return pl.pallas_call(
        body, name="mix_rms_bwd", grid=(T // tm,),
        in_specs=[row, row, pl.BlockSpec((1, D), lambda i: (0, 0))]
                 + [pl.BlockSpec((tm, dz.shape[1]), lambda i: (i, 0)) for dz in dzs]
                 + [pl.BlockSpec(w.shape, lambda i: (0, 0)) for w in wts],
        out_specs=[row, pl.BlockSpec((1, D), lambda i: (0, 0)), row],
        out_shape=[jax.ShapeDtypeStruct((T, D), F32), jax.ShapeDtypeStruct((1, D), F32),
                   jax.ShapeDtypeStruct((T, D), BF16)],
        compiler_params=_cp(1),
    )(dxo, x, g, *dzs, *wts)


def _wgrad(name, a, b, a_spec, b_spec, out_shape, out_spec, nblk, dep, acc_shape):
    T = a.shape[0]
    tk = min(TK_WGRAD, T)
    nk = T // tk

    def body(a_ref, b_ref, dep_ref, o_ref, acc_ref):
        k = pl.program_id(1)

        @pl.when(k == 0)
        def _():
            acc_ref[...] = jnp.zeros_like(acc_ref)

        acc_ref[...] += _dot_tn(a_ref[...], b_ref[...])

        @pl.when(k == nk - 1)
        def _():
            o_ref[...] = acc_ref[...].reshape(o_ref.shape).astype(BF16)

    return pl.pallas_call(
        body, name=name, grid=(nblk, nk), in_specs=[a_spec, b_spec, ANY], out_specs=out_spec,
        out_shape=jax.ShapeDtypeStruct(out_shape, BF16), scratch_shapes=[pltpu.VMEM(acc_shape, F32)],
        compiler_params=_cp(2),
    )(a, b, dep)


def _wgrad_ffn_in(hb, dgu, dep):
    T, D = hb.shape
    FB = dgu.shape[2] // 2
    tk = min(TK_WGRAD, T)
    return _wgrad("wgrad_ffn_in", hb, dgu,
                  pl.BlockSpec((tk, D), lambda b, k: (k, 0)),
                  pl.BlockSpec((None, tk, FB), lambda b, k: (b // 2, k, b % 2)),
                  (4, D, FB), pl.BlockSpec((None, D, FB), lambda b, k: (b, 0, 0)), 4, dep, (D, FB))


def _wgrad_ffn_out(a, dyb, dep):
    T, D = dyb.shape
    FB = a.shape[1] // 2
    tk = min(TK_WGRAD, T)
    return _wgrad("wgrad_ffn_out", a, dyb,
                  pl.BlockSpec((tk, FB), lambda b, k: (k, b)),
                  pl.BlockSpec((tk, D), lambda b, k: (k, 0)),
                  (4, FB // 2, D), pl.BlockSpec((2, FB // 2, D), lambda b, k: (b, 0, 0)), 2, dep, (FB, D))


def _wgrad_cat(a_list, b_list):
    T = a_list[0].shape[0]
    tk = min(TK_WGRAD, T)
    nk = T // tk
    na = len(a_list)
    M, N = sum(a.shape[1] for a in a_list), sum(b.shape[1] for b in b_list)

    def body(*refs):
        a_refs, b_refs, o_ref, acc_ref = refs[:na], refs[na:-2], refs[-2], refs[-1]
        k = pl.program_id(0)

        @pl.when(k == 0)
        def _():
            acc_ref[...] = jnp.zeros_like(acc_ref)

        r0 = 0
        for a_ref in a_refs:
            c0 = 0
            for b_ref in b_refs:
                m, n = a_ref.shape[1], b_ref.shape[1]
                acc_ref[r0:r0 + m, c0:c0 + n] += _dot_tn(a_ref[...], b_ref[...])
                c0 += n
            r0 += a_ref.shape[1]

        @pl.when(k == nk - 1)
        def _():
            o_ref[...] = acc_ref[...].astype(BF16)

    return pl.pallas_call(
        body, name="wgrad_cat", grid=(nk,),
        in_specs=[pl.BlockSpec((tk, v.shape[1]), lambda k: (k, 0)) for v in list(a_list) + list(b_list)],
        out_specs=pl.BlockSpec((M, N), lambda k: (0, 0)),
        out_shape=jax.ShapeDtypeStruct((M, N), BF16), scratch_shapes=[pltpu.VMEM((M, N), F32)],
        compiler_params=_cp(1),
    )(*a_list, *b_list)


def _mixout_bwd(dxo, wo):
    T, D = dxo.shape
    tm = min(TM_MIX, T)
    A = ATTN_W
    C = wo.shape[0] - A

    def body(dxo_ref, w_ref, dyb_ref, da_ref, dc_ref):
        dyb = dxo_ref[...].astype(BF16)
        dyb_ref[...] = dyb
        da_ref[...] = _dot_nt(dyb, w_ref[:A, :]).astype(BF16)
        dc_ref[...] = _dot_nt(dyb, w_ref[A:, :])

    return pl.pallas_call(
        body, name="mixout_bwd", grid=(T // tm,),
        in_specs=[pl.BlockSpec((tm, D), lambda i: (i, 0)), pl.BlockSpec(wo.shape, lambda i: (0, 0))],
        out_specs=[pl.BlockSpec((tm, D), lambda i: (i, 0)), pl.BlockSpec((tm, A), lambda i: (i, 0)),
                   pl.BlockSpec((tm, C), lambda i: (i, 0))],
        out_shape=[jax.ShapeDtypeStruct((T, D), BF16), jax.ShapeDtypeStruct((T, A), BF16),
                   jax.ShapeDtypeStruct((T, C), F32)],
        compiler_params=_cp(1),
    )(dxo, wo)


def _conv_bwd(dconv, ypre, u, w, lg, lb):
    T, CH = dconv.shape
    tm = min(TM, T)
    n = tm + HALO
    hb = tm // HALO
    nt = T // tm
    nchunk = tm // CONV_ROWS

    def body(dc_ref, dcn_ref, yp_ref, ypn_ref, uc_ref, up_ref, w_ref, lg_ref, lb_ref,
             du_ref, dw_ref, dvec_ref, zs_ref, zsh_ref, dy_ref, dysh_ref, dz_ref, dwacc_ref):
        i = pl.program_id(0)

        @pl.when(i == 0)
        def _():
            dwacc_ref[...] = jnp.zeros_like(dwacc_ref)
            dvec_ref[...] = jnp.zeros_like(dvec_ref)

        g, bb = lg_ref[...], lb_ref[...]

        def ln_bwd(dc, yp):
            mu = jnp.mean(yp, axis=-1, keepdims=True)
            d = yp - mu
            rs = lax.rsqrt(jnp.mean(d * d, axis=-1, keepdims=True) + EPS)
            yn = d * rs
            o = yn * g + bb
            sg = _fast_sigmoid(o)
            do = dc * (sg * (1.0 + o * (1.0 - sg)))
            dyn = do * g
            dyp = rs * (dyn - jnp.mean(dyn, axis=-1, keepdims=True)
                        - yn * jnp.mean(dyn * yn, axis=-1, keepdims=True))
            return dyp, do, yn

        dyp, do, yn = ln_bwd(dc_ref[...], yp_ref[...])
        dvec_ref[0:1, :] += jnp.sum(dyp, axis=0, keepdims=True)
        dvec_ref[1:2, :] += jnp.sum(do * yn, axis=0, keepdims=True)
        dvec_ref[2:3, :] += jnp.sum(do, axis=0, keepdims=True)
        dy_ref[0:tm] = dyp
        dyh, _, _ = ln_bwd(dcn_ref[...], ypn_ref[...])
        dy_ref[tm:] = jnp.where(i < nt - 1, dyh, 0.0)
        _shift_copies(dy_ref, dysh_ref, n - 8)
        _fill_z(zs_ref, zsh_ref, uc_ref, up_ref, i, CH, n)

        def chunk(ci, carry):
            c0 = pl.multiple_of(ci * CONV_ROWS, CONV_ROWS)
            acc = jnp.zeros((CONV_ROWS, CH), F32)
            for k in range(CONV_W):
                acc = acc + w_ref[k:k + 1, :] * _tap(dy_ref, dysh_ref, CONV_W - 1 - k, c0)
            dz_ref[pl.ds(c0, CONV_ROWS), :] = acc
            dyc = dy_ref[pl.ds(c0, CONV_ROWS), :]
            for k in range(CONV_W):
                prod = dyc * _tap(zs_ref, zsh_ref, HALO - (CONV_W - 1) + k, c0)
                dwacc_ref[k] += jnp.sum(prod.reshape(CONV_ROWS // 8, 8, CH), axis=0)
            return carry

        lax.fori_loop(0, nchunk, chunk, 0)

        @pl.when(i == nt - 1)
        def _():
            dw_ref[...] = jnp.sum(dwacc_ref[...], axis=1)

        uc = uc_ref[...]
        a = uc[:, :CH]
        sg = _fast_sigmoid(uc[:, CH:])
        dz = dz_ref[...]
        du_ref[:, :CH] = (dz * sg).astype(BF16)
        du_ref[:, CH:] = (dz * a * sg * (1.0 - sg)).astype(BF16)

    cur = lambda c: pl.BlockSpec((tm, c), lambda i: (i, 0))
    nxt = lambda c: pl.BlockSpec((HALO, c), lambda i: (jnp.minimum((i + 1) * hb, T // HALO - 1), 0))
    vec = pl.BlockSpec((1, CH), lambda i: (0, 0))
    return pl.pallas_call(
        body, name="conv_bwd", grid=(nt,),
        in_specs=[cur(CH), nxt(CH), cur(CH), nxt(CH), cur(2 * CH),
                  pl.BlockSpec((HALO, 2 * CH), lambda i: (jnp.maximum(i * hb - 1, 0), 0)),
                  pl.BlockSpec((CONV_W, CH), lambda i: (0, 0)), vec, vec],
        out_specs=[pl.BlockSpec((tm, 2 * CH), lambda i: (i, 0)), pl.BlockSpec((32, CH), lambda i: (0, 0)),
                   pl.BlockSpec((8, CH), lambda i: (0, 0))],
        out_shape=[jax.ShapeDtypeStruct((T, 2 * CH), BF16), jax.ShapeDtypeStruct((32, CH), F32),
                   jax.ShapeDtypeStruct((8, CH), F32)],
        scratch_shapes=[pltpu.VMEM((n, CH), F32), pltpu.VMEM((7, n - 8, CH), F32),
                        pltpu.VMEM((n, CH), F32), pltpu.VMEM((7, n - 8, CH), F32), pltpu.VMEM((tm, CH), F32),
                        pltpu.VMEM((32, 8, CH), F32)],
        compiler_params=_cp(1),
    )(dconv, dconv, ypre, ypre, u, u, w, lg, lb)


def _attn_bwd(sinks, tab, qkv, dattn):
    T = qkv.shape[0]
    nb = T // WINDOW

    def body(sink_ref, tab_ref, q_ref, kvp_ref, kvc_ref, do_ref, dq_ref, dkv_ref, dsk_ref, carry_ref):
        n = pl.program_id(0)

        @pl.when(n == 0)
        def _():
            dsk_ref[...] = jnp.zeros_like(dsk_ref)
            carry_ref[...] = jnp.zeros_like(carry_ref)

        @pl.when(n < nb)
        def _():
            seen = _first_block_mask(n)
            for g in range(N_KV):
                qs = _stack_heads(q_ref, g)
                dos = _stack_heads(do_ref, g)
                k = _band(kvp_ref, kvc_ref, g * HEAD_DIM)
                v = _band(kvp_ref, kvc_ref, KV_W + g * HEAD_DIM)
                p, ps = _attn_probs(qs, k, tab_ref[g], seen, _sink_col(sink_ref, g))
                dp = _dot_nt(dos, v)
                delta = jnp.sum(p * dp, axis=-1, keepdims=True)
                dsb = (p * (dp - delta)).astype(BF16)
                dsink = -ps * delta
                dqs = _dot(dsb, k) * SCALE
                dk = _dot_tn(dsb, qs) * SCALE
                dv = _dot_tn(p.astype(BF16), dos)
                for i in range(GROUP):
                    h = GROUP * g + i
                    dq_ref[:, h * HEAD_DIM:(h + 1) * HEAD_DIM] = dqs[i * WINDOW:(i + 1) * WINDOW].astype(BF16)
                    dsk_ref[h:h + 1, :] += jnp.sum(dsink[i * WINDOW:(i + 1) * WINDOW], axis=0, keepdims=True)
                for off, d in ((g * HEAD_DIM, dk), (KV_W + g * HEAD_DIM, dv)):
                    dkv_ref[:, off:off + HEAD_DIM] = (carry_ref[:, off:off + HEAD_DIM] + d[:WINDOW]).astype(BF16)
                    carry_ref[:, off:off + HEAD_DIM] = d[WINDOW:]

        @pl.when(n == nb)
        def _():
            dkv_ref[...] = carry_ref[...].astype(BF16)

    last = nb - 1
    return pl.pallas_call(
        body, name="attn_bwd", grid=(nb + 1,),
        in_specs=[pl.BlockSpec(memory_space=pltpu.SMEM),
                  pl.BlockSpec(tab.shape, lambda n: (0, 0, 0)),
                  pl.BlockSpec((WINDOW, ATTN_W), lambda n: (jnp.minimum(n, last), 0)),
                  pl.BlockSpec((WINDOW, 2 * KV_W), lambda n: (jnp.clip(n - 1, 0, last), 2)),
                  pl.BlockSpec((WINDOW, 2 * KV_W), lambda n: (jnp.minimum(n, last), 2)),
                  pl.BlockSpec((WINDOW, ATTN_W), lambda n: (jnp.minimum(n, last), 0))],
        out_specs=[pl.BlockSpec((WINDOW, ATTN_W), lambda n: (jnp.minimum(n, last), 0)),
                   pl.BlockSpec((WINDOW, 2 * KV_W), lambda n: (jnp.maximum(n - 1, 0), 0)),
                   pl.BlockSpec((8, LANES), lambda n: (0, 0))],
        out_shape=[jax.ShapeDtypeStruct((T, ATTN_W), BF16), jax.ShapeDtypeStruct((T, 2 * KV_W), BF16),
                   jax.ShapeDtypeStruct((8, LANES), F32)],
        scratch_shapes=[pltpu.VMEM((WINDOW, 2 * KV_W), F32)],
        compiler_params=_cp(1),
    )(sinks, tab, qkv, qkv, qkv, dattn)


def _pack(arrs):
    flat = jnp.concatenate([a.reshape(-1) for a in arrs])
    pad = -flat.shape[0] % (8 * LANES)
    return jnp.pad(flat, (0, pad)).reshape(1, -1, LANES)


def _unpack(packed, like):
    flat = packed.reshape(-1)
    out, off = [], 0
    for a in like:
        out.append(flat[off:off + a.size].reshape(a.shape))
        off += a.size
    return out


def kernel(x, norm_ffn1, w_ffn1_in, w_ffn1_out, norm_mix, w_in, sinks, w_dw, b_dw, conv_ln_g, conv_ln_b, w_out, norm_ffn2, w_ffn2_in, w_ffn2_out, final_norm, loss_target, m_norm_ffn1, m_w_ffn1_in, m_w_ffn1_out, m_norm_mix, m_w_in, m_sinks, m_w_dw, m_b_dw, m_conv_ln_g, m_conv_ln_b, m_w_out, m_norm_ffn2, m_w_ffn2_in, m_w_ffn2_out, m_final_norm, v_norm_ffn1, v_w_ffn1_in, v_w_ffn1_out, v_norm_mix, v_w_in, v_sinks, v_w_dw, v_b_dw, v_conv_ln_g, v_conv_ln_b, v_w_out, v_norm_ffn2, v_w_ffn2_in, v_w_ffn2_out, v_final_norm):
    L, D = norm_ffn1.shape
    T = x.shape[1]
    FB = w_ffn1_in.shape[2]
    CH = b_dw.shape[1]
    QKV = ATTN_W + 2 * KV_W
    xs = x.reshape(T, D)
    tgt = loss_target.reshape(T, D)
    cx, cy, cc = lax.axis_index("x"), lax.axis_index("y"), lax.axis_index("c")
    chip = 2 * cx + cy
    cidx = cc.reshape(1).astype(jnp.int32)
    tr = lambda a_: jnp.transpose(a_, (0, 2, 1))
    big_w = (w_ffn1_in, w_ffn1_out, tr(w_in), w_out, w_ffn2_in, w_ffn2_out)
    big_m = (m_w_ffn1_in, m_w_ffn1_out, tr(m_w_in), m_w_out, m_w_ffn2_in, m_w_ffn2_out)
    big_v = (v_w_ffn1_in, v_w_ffn1_out, tr(v_w_in), v_w_out, v_w_ffn2_in, v_w_ffn2_out)
    NW = len(big_w) + 1

    def own_slot(a, slots=4, idx=chip):
        return lax.dynamic_update_index_in_dim(lax.empty((slots,) + a.shape, a.dtype), a, idx, 0)

    def shards(l, tok):
        return [own_slot((w_[l] + tok[0, 0]).astype(BF16)) for w_ in big_w] + [own_slot(w_dw[l] + tok[0, 0])]

    def gather_start(lands, tok):
        return _xchg_start("gather_start", [], lands, _gather_plan, tok)

    def gather_arrived(started, after, n, taps):
        _, lands, tok = _xchg_wait("gather_wait", started, 0, n, _gather_plan, after)
        return _xchg_start("gshare_start", [], lands[:-1] if taps else lands, _gshare_plan, tok, "sibling3"), lands[-1]

    def shared_weights(shared, after, n):
        _, mats, tok = _xchg_wait("gshare_wait", shared, 0, n, _gshare_plan, after, "sibling3")
        return mats, tok

    row = lambda a, l: a[l].reshape(1, -1)
    tab = _attn_bias_table()
    NB = len(big_w)

    saved, W = [], []
    zero_tok = jnp.zeros((8, LANES), F32)
    src0 = shards(0, zero_tok)
    started = gather_start(src0[:2], zero_tok)
    rest0 = gather_start(src0[2:], started[-1])
    cast = [None] + [shards(l, rest0[-1]) for l in range(1, L)]
    shared, _ = gather_arrived(started, [xs] + [a_ for c_ in cast[1:] for a_ in c_], 2, False)
    after = [shared[-1]]
    for l in range(L):
        mats, tok = shared_weights(shared, after, 2 if l == 0 else NB)
        started = None
        if l + 1 < L:
            started = gather_start(cast[l + 1], tok)
            tok = started[-1]
        x0 = xs
        x1, gu1 = _ffn_fwd(x0, row(norm_ffn1, l) + tok[0, 0], mats[0], mats[1].reshape(2 * FB, D))
        gm_row = row(norm_mix, l)
        if l == 0:
            shared, gdw = gather_arrived(rest0, [x1], NW - 2, True)
            rest, tok = shared_weights(shared, [shared[-1]], NB - 2)
            mats = list(mats) + list(rest)
            gm_row = gm_row + tok[0, 0]
        g1i, g1o, gi, go, g2i, g2o = mats
        w = dict(f1i=g1i, f1o=g1o.reshape(2 * FB, D), f2i=g2i, f2o=g2o.reshape(2 * FB, D),
                 wit=gi.reshape(-1, D), wo=go.reshape(-1, D),
                 wdw=jnp.transpose(gdw, (1, 0, 2)).reshape(CONV_W, CH))
        W.append(w)
        qkv, u = _mixproj_fwd(x1, gm_row, w["wit"])
        attn = _attn_fwd(row(sinks, l), tab, qkv)
        conv, ypre = _conv_fwd(u, w["wdw"], row(b_dw, l), row(conv_ln_g, l), row(conv_ln_b, l))
        x2 = _mixout_fwd(x1, attn, conv, w["wo"])
        g2_row = row(norm_ffn2, l)
        if started is not None and l > 0:
            shared, gdw = gather_arrived(started, [x2], NW, True)
            g2_row = g2_row + shared[-1][0, 0]
        xs, gu2 = _ffn_fwd(x2, g2_row, w["f2i"], w["f2o"])
        if started is not None and l == 0:
            shared, gdw = gather_arrived(started, [xs], NW, True)
        saved.append((x0, gu1, x1, qkv, u, attn, conv, ypre, x2, gu2))
        after = [xs]

    loss_part, dx, d_final = _loss_head(xs, final_norm.reshape(1, D), tgt)
    loss = lax.psum(loss_part[0, 0], ("x", "y", "c"))

    bufs = [[lax.empty(w_.shape, F32) for _ in range(4)] for w_ in big_w]
    d_n1, d_nm, d_n2 = [None] * L, [None] * L, [None] * L
    d_sk, d_bdw, d_lg, d_lb, d_wdw = [None] * L, [None] * L, [None] * L, [None] * L, [None] * L

    def sib_start(gs):
        return _xchg_start("sib_start", gs, [lax.empty((4, g.shape[1] // 2, g.shape[2]), g.dtype) for g in gs],
                           _sib_plan, zero_tok, "sibling")

    def reduce_start(sib_started, after, n):
        gs, sibs, _ = _xchg_wait("sib_wait", sib_started, n, n, _sib_plan, after, "sibling")
        parts = [_sum_halves(cidx, g, s_) for g, s_ in zip(gs, sibs)]
        lands = [own_slot(lax.dynamic_index_in_dim(p, chip, 0, keepdims=False)) for p in parts]
        return _xchg_start("rs_start", parts, lands, _rs_plan, zero_tok)

    def share_start(rs_started, after, n):
        _, qs, tok = _xchg_wait("rs_wait", rs_started, n, n, _rs_plan, after)
        return _xchg_start("qshare_start", qs, [lax.empty(q.shape, q.dtype) for q in qs], _whole_plan, tok, "sibling")

    def finish(l, shared, after, idxs):
        q_own, q_sib, _ = _xchg_wait("qshare_wait", shared, len(idxs), len(idxs), _whole_plan, after, "sibling")
        for k, t in enumerate(idxs):
            bufs[t] = _adamw_layer(cidx, q_own[k], q_sib[k], big_w[t], big_m[t], big_v[t], bufs[t], l)

    ALL = list(range(NB))
    EARLY, LATE = ALL[2:], ALL[:2]
    sib_pending = rs_pending = None
    shares = []
    tok = zero_tok
    for l in reversed(range(L)):
        w = W[l]
        x0, gu1, x1, qkv, u, attn, conv, ypre, x2, gu2 = saved[l]
        dx, d_n2[l], hb, dgu, a, dyb = _ffn_bwd(dx, x2, row(norm_ffn2, l), gu2, w["f2i"], w["f2o"], tok)
        g_f2i, g_f2o = _wgrad_ffn_in(hb, dgu, tok), _wgrad_ffn_out(a, dyb, tok)
        lg_row = row(conv_ln_g, l)
        if sib_pending is not None:
            rs_started = reduce_start(sib_pending[1], [g_f2o], NB)
            if rs_pending is not None:
                shares.append((rs_pending[0], share_start(rs_pending[1], [rs_started[-1]], NB)))
            rs_pending = (sib_pending[0], rs_started)
            lg_row = lg_row + rs_started[-1][0, 0]
        dyb, dattn, dconv = _mixout_bwd(dx, w["wo"])
        g_wo = _wgrad_cat([attn, conv], [dyb]).reshape(4, -1, D)
        du, dwdw, dvec = _conv_bwd(dconv, ypre, u, w["wdw"], lg_row, row(conv_ln_b, l))
        d_wdw[l], d_bdw[l], d_lg[l], d_lb[l] = dwdw[:CONV_W], dvec[0], dvec[1], dvec[2]
        dq, dkv, dsk = _attn_bwd(row(sinks, l), tab, qkv, dattn)
        d_sk[l] = dsk[:, 0]
        wit = w["wit"]
        dx, d_nm[l], hb = _mix_rms_bwd(dx, x1, row(norm_mix, l), [dq, dkv, du],
                                       [wit[:ATTN_W], wit[ATTN_W:QKV], wit[QKV:]])
        g_wi = _wgrad_cat([dq, dkv, du], [hb]).reshape(4, -1, D)
        if l == 0:
            sib_early = sib_start([g_wi, g_wo, g_f2i, g_f2o])
            tok = sib_early[-1]
        dx, d_n1[l], hb, dgu, a, dyb = _ffn_bwd(dx, x0, row(norm_ffn1, l), gu1, w["f1i"], w["f1o"], tok)
        if l == 0:
            rs_early = reduce_start(sib_early, [dx], len(EARLY))
            tok = rs_early[-1]
        g_f1i, g_f1o = _wgrad_ffn_in(hb, dgu, tok), _wgrad_ffn_out(a, dyb, tok)
        sib_started = sib_start([g_f1i, g_f1o] if l == 0 else [g_f1i, g_f1o, g_wi, g_wo, g_f2i, g_f2o])
        tok = sib_started[-1]
        sib_pending = (l, sib_started)
    grad_x = dx.reshape(x.shape)

    small_g = [jnp.concatenate(d, axis=0) for d in (d_n1, d_nm, d_n2)] + [d_final, jnp.stack(d_sk)] + \
              [jnp.stack(d) for d in (d_bdw, d_lg, d_lb, d_wdw)]
    packed = _pack(small_g)[0]
    small_started = _xchg_start("small_start", [packed], [own_slot(packed, 8, 4 * cx + 2 * cy + cc)], _slot_plan, tok, "all")

    after = [small_started[-1]]
    if rs_pending is not None:
        shares.append((rs_pending[0], share_start(rs_pending[1], after, NB)))
        after = [shares[-1][1][-1]]
    if shares:
        finish(*shares.pop(0), after, ALL)
        after = [b_[0] for b_ in bufs]
    rs_late = reduce_start(sib_pending[1], after, len(LATE))
    after = [rs_late[-1]]
    for l, sh in shares:
        finish(l, sh, after, ALL)
        after = [b_[0] for b_ in bufs]
    _, (slots,), _ = _xchg_wait("small_wait", small_started, 1, 1, _slot_plan, after, "all")
    small_sum = _unpack(_sum_slots(slots), small_g)
    g_wdw = lax.dynamic_slice_in_dim(small_sum[8], chip * w_dw.shape[2], w_dw.shape[2], axis=2)
    small_g = [small_sum[0], small_sum[1], small_sum[2], small_sum[3].reshape(D), small_sum[4],
               small_sum[5], small_sum[6], small_sum[7], g_wdw]
    small_w = (norm_ffn1, norm_mix, norm_ffn2, final_norm, sinks, b_dw, conv_ln_g, conv_ln_b, w_dw)
    small_m = (m_norm_ffn1, m_norm_mix, m_norm_ffn2, m_final_norm, m_sinks, m_b_dw, m_conv_ln_g, m_conv_ln_b, m_w_dw)
    small_v = (v_norm_ffn1, v_norm_mix, v_norm_ffn2, v_final_norm, v_sinks, v_b_dw, v_conv_ln_g, v_conv_ln_b, v_w_dw)
    upd = _adamw(_pack(small_g), _pack(small_w), _pack(small_m), _pack(small_v))
    small_upd = [_unpack(u_, small_w) for u_ in upd]
    sh_early = share_start(rs_early, [upd[0]], len(EARLY))
    sh_late = share_start(rs_late, [sh_early[-1]], len(LATE))
    finish(0, sh_early, [sh_late[-1]], EARLY)
    finish(0, sh_late, [bufs[t][0] for t in EARLY], LATE)

    order = ("norm_ffn1", "w_ffn1_in", "w_ffn1_out", "norm_mix", "w_in", "sinks", "w_dw", "b_dw", "conv_ln_g",
             "conv_ln_b", "w_out", "norm_ffn2", "w_ffn2_in", "w_ffn2_out", "final_norm")
    small_names = ("norm_ffn1", "norm_mix", "norm_ffn2", "final_norm", "sinks", "b_dw", "conv_ln_g", "conv_ln_b", "w_dw")
    big_names = ("w_ffn1_in", "w_ffn1_out", "w_in", "w_out", "w_ffn2_in", "w_ffn2_out")
    grads, deltas, new_m, new_v = {}, {}, {}, {}
    for i, nme in enumerate(small_names):
        grads[nme], deltas[nme], new_m[nme], new_v[nme] = small_g[i], small_upd[0][i], small_upd[1][i], small_upd[2][i]
    for i, nme in enumerate(big_names):
        grads[nme], deltas[nme], new_m[nme], new_v[nme] = [tr(b_) for b_ in bufs[i]] if nme == "w_in" else bufs[i]
    return (loss, grad_x, *[grads[n] for n in order], *[deltas[n] for n in order],
            *[new_m[n] for n in order], *[new_v[n] for n in order])
```

```python
import functools

import jax
import jax.numpy as jnp
from jax import lax
from jax.experimental import pallas as pl
from jax.experimental.pallas import tpu as pltpu

F32, BF16 = jnp.float32, jnp.bfloat16
EPS = 1e-6
NEG_INF = -1e30
HEAD_DIM = 64
N_HEADS = 8
N_KV = 2
GROUP = N_HEADS // N_KV
WINDOW = 128
ATTN_W = N_HEADS * HEAD_DIM
KV_W = N_KV * HEAD_DIM
CONV_W = 31
HALO = 32
CONV_ROWS = 32
SCALE = 1.0 / 8.0
ADAM_LR, ADAM_B1, ADAM_B2, ADAM_EPS, ADAM_WD, ADAM_STEP = 0.001, 0.9, 0.999, 1e-08, 0.01, 10
TM = 512
TM_FFN_BWD = 256
TK_WGRAD = 2048
TM_MIX = 1024
LANES = 128
VMEM_LIMIT = 52 * 1024 * 1024
MESH = pl.DeviceIdType.MESH
ANY = pl.BlockSpec(memory_space=pl.ANY)
HBM = pl.BlockSpec(memory_space=pltpu.HBM)
SEM = pl.BlockSpec(memory_space=pltpu.SEMAPHORE)
VMEM = pl.BlockSpec(memory_space=pltpu.VMEM)
EFFECT = pltpu.SideEffectType.DATAFLOW_SIDE_EFFECTING
TOKEN = jax.ShapeDtypeStruct((8, LANES), F32)


def _cp(n):
    return pltpu.CompilerParams(dimension_semantics=("arbitrary",) * n, vmem_limit_bytes=VMEM_LIMIT)


def _dot(a, b):
    return jnp.dot(a, b, preferred_element_type=F32)


def _dot_nt(a, b):
    return lax.dot_general(a, b, (((1,), (1,)), ((), ())), preferred_element_type=F32)


def _dot_tn(a, b):
    return lax.dot_general(a, b, (((0,), (0,)), ((), ())), preferred_element_type=F32)


def _place():
    x, y, c = lax.axis_index("x"), lax.axis_index("y"), lax.axis_index("c")
    chips = [(1 - x, y), (x, 1 - y), (1 - x, 1 - y)]
    return x, y, c, chips


def _rcopy(src, dst, send_sems, recv_sems, k, dev):
    return pltpu.make_async_remote_copy(src_ref=src, dst_ref=dst, send_sem=send_sems.at[k],
                                        recv_sem=recv_sems.at[k], device_id=dev, device_id_type=MESH)


def _hbm(a):
    return pltpu.with_memory_space_constraint(a, pltpu.HBM)


PEERS = {"chips": 3, "sibling": 1, "sibling3": 3, "all": 7}


def _targets(mode):
    x, y, c, chips = _place()
    b = 2 * x + y
    if mode == "chips":
        return b, c, [((px, py, c), 2 * px + py) for px, py in chips]
    if mode == "sibling":
        return b, c, [((x, y, 1 - c), b)]
    if mode == "sibling3":
        return b, c, [((x, y, 1 - c), 2 * px + py) for px, py in chips]
    flip = lambda v, f: 1 - v if f else v
    devs = [(flip(x, k >> 2 & 1), flip(y, k >> 1 & 1), flip(c, k & 1)) for k in range(1, 8)]
    return 4 * x + 2 * y + c, c, [(d, 4 * d[0] + 2 * d[1] + d[2]) for d in devs]


def _xchg_start(name, srcs, lands, plan, dep, mode="chips"):
    ns, nl, npeer = len(srcs), len(lands), PEERS[mode]

    def body(*refs):
        land = refs[ns:ns + nl]
        src = refs[:ns] if ns else land
        send_sems, recv_sems, token = refs[ns + nl + 1], refs[ns + nl + 2], refs[-1]
        me, c, peers = _targets(mode)
        for t in range(nl):
            for j, (dev, tag) in enumerate(peers):
                s, d, _ = plan(src[t], land[t], t, me, c, tag)
                _rcopy(s, d, send_sems, recv_sems, npeer * t + j, dev).start()
        token[...] = jnp.zeros_like(token)

    arrs = list(srcs) + list(lands)
    return pl.pallas_call(
        body, name=name,
        out_shape=(pltpu.SemaphoreType.DMA((npeer * nl,)), pltpu.SemaphoreType.DMA((npeer * nl,)),
                   *[pltpu.HBM(a.shape, a.dtype) for a in arrs], TOKEN),
        in_specs=[HBM] * (ns + nl) + [ANY], out_specs=(SEM, SEM, *[HBM] * (ns + nl), VMEM),
        input_output_aliases={i: 2 + i for i in range(ns + nl)},
        compiler_params=pltpu.CompilerParams(has_side_effects=EFFECT),
    )(*[_hbm(a) for a in arrs], dep)


def _xchg_wait(name, started, ns, nl, plan, after, mode="chips"):
    send_sems, recv_sems, thru = started[0], started[1], started[2:2 + ns + nl]
    npeer = PEERS[mode]

    def body(*refs):
        land = refs[ns:ns + nl]
        src = refs[:ns] if ns else land
        send_sems, recv_sems, token = refs[ns + nl], refs[ns + nl + 1], refs[-1]
        me, c, peers = _targets(mode)
        for t in range(nl):
            for j, (dev, tag) in enumerate(peers):
                s, _, a = plan(src[t], land[t], t, me, c, tag)
                cp = _rcopy(s, a, send_sems, recv_sems, npeer * t + j, dev)
                cp.wait_send()
                cp.wait_recv()
        token[...] = jnp.zeros_like(token)

    out = pl.pallas_call(
        body, name=name,
        out_shape=(*[pltpu.HBM(a.shape, a.dtype) for a in thru], TOKEN),
        in_specs=[HBM] * (ns + nl) + [SEM, SEM] + [ANY] * len(after), out_specs=(*[HBM] * (ns + nl), VMEM),
        input_output_aliases={i: i for i in range(ns + nl)},
        compiler_params=pltpu.CompilerParams(has_side_effects=EFFECT),
    )(*thru, send_sems, recv_sems, *after)
    return out[:ns], out[ns:ns + nl], out[-1]


def _half(ref_rows, which):
    h = ref_rows // 2
    return pl.ds(which * h, h)


def _gather_plan(src, land, t, b, c, pb):
    if land.shape[1] % 2 == 0:
        hs = _half(land.shape[1], c)
        return land.at[b, hs], land.at[b, hs], land.at[pb, hs]
    return land.at[b], land.at[b], land.at[pb]


def _gshare_plan(src, land, t, b, c, pb):
    return land.at[pb, _half(land.shape[1], c)], land.at[pb, _half(land.shape[1], c)], land.at[pb, _half(land.shape[1], 1 - c)]


def _rs_plan(src, land, t, b, c, pb):
    return src.at[pb], land.at[b], land.at[pb]


def _sib_plan(src, land, t, b, c, pb):
    return src.at[:, _half(src.shape[1], 1 - c), :], land, land


def _rows_block(h, cap=512):
    for rb in range(min(h, cap) // 16 * 16, 0, -16):
        if h % rb == 0:
            return rb
    return h


def _sum_halves(cidx, g, s):
    _, R, C = g.shape
    rb = _rows_block(R // 2)
    nr = R // 2 // rb

    def body(c_ref, g_ref, s_ref, o_ref):
        o_ref[...] = (g_ref[...].astype(F32) + s_ref[...].astype(F32)).astype(BF16)

    blk = (None, rb, C)
    return pl.pallas_call(
        body, name="sum_halves", out_shape=jax.ShapeDtypeStruct(s.shape, BF16),
        grid_spec=pltpu.PrefetchScalarGridSpec(
            num_scalar_prefetch=1, grid=(4, nr),
            in_specs=[pl.BlockSpec(blk, lambda p, i, c: (p, c[0] * nr + i, 0)),
                      pl.BlockSpec(blk, lambda p, i, c: (p, i, 0))],
            out_specs=pl.BlockSpec(blk, lambda p, i, c: (p, i, 0))),
        compiler_params=_cp(2),
    )(cidx, g, s)


def _whole_plan(src, land, t, me, c, tag):
    return src, land, land


def _slot_plan(src, land, t, me, c, tag):
    return src, land.at[me], land.at[tag]


def _adam_update(gg, w, m, v):
    m2 = ADAM_B1 * m + (1.0 - ADAM_B1) * gg
    v2 = ADAM_B2 * v + (1.0 - ADAM_B2) * (gg * gg)
    mh = m2 / (1.0 - ADAM_B1 ** ADAM_STEP)
    vh = v2 / (1.0 - ADAM_B2 ** ADAM_STEP)
    return -ADAM_LR * (mh / (jnp.sqrt(vh) + ADAM_EPS) + ADAM_WD * w), m2, v2


def _adamw_layer(cidx, q_own, q_sib, w, m, v, bufs, l):
    L, R, C = w.shape
    h = R // 2
    rb = _rows_block(h, 256)
    nr = h // rb

    def body(c_ref, qo_ref, qs_ref, w_ref, m_ref, v_ref, *rest):
        g_ref, d_ref, mo_ref, vo_ref = rest[-4:]
        own = pl.program_id(0) == c_ref[0]
        gg = jnp.zeros((rb, C), F32)
        for s in range(4):
            gg = gg + jnp.where(own, qo_ref[s], qs_ref[s]).astype(F32)
        g_ref[...] = gg
        d_ref[...], mo_ref[...], vo_ref[...] = _adam_update(gg, w_ref[...], m_ref[...], v_ref[...])

    q_own_spec = pl.BlockSpec((4, rb, C), lambda hh, i, c: (0, jnp.where(hh == c[0], i, 0), 0))
    q_sib_spec = pl.BlockSpec((4, rb, C), lambda hh, i, c: (0, jnp.where(hh == c[0], 0, i), 0))
    wspec = pl.BlockSpec((None, rb, C), lambda hh, i, c: (l, hh * nr + i, 0))
    return pl.pallas_call(
        body, name="adamw_layer", out_shape=[jax.ShapeDtypeStruct(w.shape, F32)] * 4,
        grid_spec=pltpu.PrefetchScalarGridSpec(
            num_scalar_prefetch=1, grid=(2, nr),
            in_specs=[q_own_spec, q_sib_spec, wspec, wspec, wspec] + [ANY] * 4, out_specs=[wspec] * 4),
        input_output_aliases={6 + k: k for k in range(4)},
        compiler_params=_cp(2),
    )(cidx, q_own, q_sib, w, m, v, *bufs)


def _adamw(g, w, m, v):
    L, R, C = g.shape
    rb = _rows_block(R)

    def body(g_ref, w_ref, m_ref, v_ref, d_ref, mo_ref, vo_ref):
        d_ref[...], mo_ref[...], vo_ref[...] = _adam_update(g_ref[...], w_ref[...], m_ref[...], v_ref[...])

    spec = pl.BlockSpec((None, rb, C), lambda l, i: (l, i, 0))
    return pl.pallas_call(
        body, name="adamw", grid=(L, R // rb), in_specs=[spec] * 4, out_specs=[spec] * 3,
        out_shape=[jax.ShapeDtypeStruct(g.shape, F32)] * 3, compiler_params=_cp(2),
    )(g, w, m, v)


def _sum_slots(buf):
    def body(b_ref, o_ref):
        acc = b_ref[0]
        for k in range(1, 8):
            acc = acc + b_ref[k]
        o_ref[...] = acc

    return pl.pallas_call(body, name="sum_slots", in_specs=[VMEM], out_specs=VMEM,
                          out_shape=jax.ShapeDtypeStruct(buf.shape[1:], F32))(buf)


def _rms(xf, g):
    r = lax.rsqrt(jnp.mean(xf * xf, axis=-1, keepdims=True) + EPS)
    return xf * r, r


def _lane_chunks(n):
    lo = (n // LANES + 1) // 2 * LANES
    return ((0, lo), (lo, n - lo))


def _load_ffn_weights(win_hbm, wout_hbm, win_v, wout_v, sems):
    fb = win_v.shape[2]
    loads = [pltpu.make_async_copy(win_hbm.at[k], win_v.at[k], sems.at[k]) for k in range(4)]
    loads += [pltpu.make_async_copy(wout_hbm.at[pl.ds(k * fb, fb)], wout_v.at[pl.ds(k * fb, fb)], sems.at[4 + k])
              for k in range(2)]
    for cp in loads:
        cp.start()
    for cp in loads:
        cp.wait()


def _fast_sigmoid(v):
    return pl.reciprocal(1.0 + jnp.exp(-v), approx=True)


def _ffn_fwd(x, g, win, wout):
    T, D = x.shape
    FB = win.shape[2]
    tm = min(TM, T)

    def body(x_ref, g_ref, win_hbm, wout_hbm, xo_ref, gu_ref, win_v, wout_v, sems):
        @pl.when(pl.program_id(0) == 0)
        def _():
            _load_ffn_weights(win_hbm, wout_hbm, win_v, wout_v, sems)

        xf = x_ref[...]
        xh, _ = _rms(xf, None)
        h = (xh * g_ref[...]).astype(BF16)
        acc = jnp.zeros((tm, D), F32)
        for blk in range(2):
            for lo, sz in _lane_chunks(FB):
                cols = pl.ds(blk * FB + lo, sz)
                gate = _dot(h, win_v[blk, :, pl.ds(lo, sz)])
                up = _dot(h, win_v[2 + blk, :, pl.ds(lo, sz)])
                gu_ref[0, :, cols] = gate.astype(BF16)
                gu_ref[1, :, cols] = up.astype(BF16)
                a = (gate * _fast_sigmoid(gate) * up).astype(BF16)
                acc = acc + _dot(a, wout_v[cols, :])
        xo_ref[...] = xf + 0.5 * acc

    row = pl.BlockSpec((tm, D), lambda i: (i, 0))
    return pl.pallas_call(
        body, name="ffn_fwd", grid=(T // tm,),
        in_specs=[row, pl.BlockSpec((1, D), lambda i: (0, 0)), ANY, ANY],
        out_specs=[row, pl.BlockSpec((2, tm, 2 * FB), lambda i: (0, i, 0))],
        out_shape=[jax.ShapeDtypeStruct((T, D), F32), jax.ShapeDtypeStruct((2, T, 2 * FB), BF16)],
        scratch_shapes=[pltpu.VMEM(win.shape, BF16), pltpu.VMEM(wout.shape, BF16), pltpu.SemaphoreType.DMA((6,))],
        compiler_params=_cp(1),
    )(x, g, win, wout)


def _mixproj_fwd(x, g, wt):
    T, D = x.shape
    W = wt.shape[0]
    QKV = ATTN_W + 2 * KV_W
    tm = min(TM_MIX, T)

    def body(x_ref, g_ref, w_ref, qkv_ref, u_ref):
        xh, _ = _rms(x_ref[...], None)
        h = (xh * g_ref[...]).astype(BF16)
        qkv_ref[...] = _dot_nt(h, w_ref[:QKV, :]).astype(BF16)
        u_ref[...] = _dot_nt(h, w_ref[QKV:, :])

    return pl.pallas_call(
        body, name="mixproj_fwd", grid=(T // tm,),
        in_specs=[pl.BlockSpec((tm, D), lambda i: (i, 0)), pl.BlockSpec((1, D), lambda i: (0, 0)),
                  pl.BlockSpec((W, D), lambda i: (0, 0))],
        out_specs=[pl.BlockSpec((tm, QKV), lambda i: (i, 0)), pl.BlockSpec((tm, W - QKV), lambda i: (i, 0))],
        out_shape=[jax.ShapeDtypeStruct((T, QKV), BF16), jax.ShapeDtypeStruct((T, W - QKV), F32)],
        compiler_params=_cp(1),
    )(x, g, wt)


def _attn_bias_table():
    rows, cols = GROUP * WINDOW, 2 * WINDOW
    row = lax.broadcasted_iota(jnp.int32, (N_KV, rows, cols), 1)
    col = lax.broadcasted_iota(jnp.int32, (N_KV, rows, cols), 2)
    head = GROUP * lax.broadcasted_iota(jnp.int32, (N_KV, rows, cols), 0) + (row >> 7)
    dist = (row & (WINDOW - 1)) + WINDOW - col
    slope = jnp.exp2(-(head + 1).astype(F32))
    return jnp.where((dist >= 0) & (dist < WINDOW), -slope * dist.astype(F32), NEG_INF)


def _first_block_mask(n):
    col = lax.broadcasted_iota(jnp.int32, (GROUP * WINDOW, 2 * WINDOW), 1)
    return (n > 0) | (col >= WINDOW)


def _sink_col(sink_ref, g):
    hi = lax.broadcasted_iota(jnp.int32, (GROUP * WINDOW, 1), 0) >> 7
    col = jnp.zeros((GROUP * WINDOW, 1), F32)
    for i in range(GROUP):
        col = jnp.where(hi == i, sink_ref[0, GROUP * g + i], col)
    return col


def _stack_heads(ref, g):
    return jnp.concatenate([ref[:, (GROUP * g + i) * HEAD_DIM:(GROUP * g + i + 1) * HEAD_DIM]
                            for i in range(GROUP)], axis=0)


def _band(kvp_ref, kvc_ref, off):
    return jnp.concatenate([kvp_ref[:, off:off + HEAD_DIM], kvc_ref[:, off:off + HEAD_DIM]], axis=0)


def _attn_probs(qs, k, bias, seen, sink):
    s = jnp.where(seen, _dot_nt(qs, k) * SCALE + bias, NEG_INF)
    m = jnp.maximum(jnp.max(s, axis=-1, keepdims=True), sink)
    p = jnp.exp(s - m)
    es = jnp.exp(sink - m)
    inv = 1.0 / (jnp.sum(p, axis=-1, keepdims=True) + es)
    return p * inv, es * inv


def _attn_fwd(sinks, tab, qkv):
    T = qkv.shape[0]
    nb = T // WINDOW

    def body(sink_ref, tab_ref, q_ref, kvp_ref, kvc_ref, o_ref):
        seen = _first_block_mask(pl.program_id(0))
        for g in range(N_KV):
            qs = _stack_heads(q_ref, g)
            k = _band(kvp_ref, kvc_ref, g * HEAD_DIM)
            v = _band(kvp_ref, kvc_ref, KV_W + g * HEAD_DIM)
            p, _ = _attn_probs(qs, k, tab_ref[g], seen, _sink_col(sink_ref, g))
            o = _dot(p.astype(BF16), v)
            for i in range(GROUP):
                h = GROUP * g + i
                o_ref[:, h * HEAD_DIM:(h + 1) * HEAD_DIM] = o[i * WINDOW:(i + 1) * WINDOW].astype(BF16)

    return pl.pallas_call(
        body, name="attn_fwd", grid=(nb,),
        in_specs=[pl.BlockSpec(memory_space=pltpu.SMEM),
                  pl.BlockSpec(tab.shape, lambda n: (0, 0, 0)),
                  pl.BlockSpec((WINDOW, ATTN_W), lambda n: (n, 0)),
                  pl.BlockSpec((WINDOW, 2 * KV_W), lambda n: (jnp.maximum(n - 1, 0), 2)),
                  pl.BlockSpec((WINDOW, 2 * KV_W), lambda n: (n, 2))],
        out_specs=pl.BlockSpec((WINDOW, ATTN_W), lambda n: (n, 0)),
        out_shape=jax.ShapeDtypeStruct((T, ATTN_W), BF16),
        compiler_params=_cp(1),
    )(sinks, tab, qkv, qkv, qkv)


def _shift_copies(src_ref, dst_ref, n):
    for b in range(1, 8):
        dst_ref[b - 1] = src_ref[b:b + n, :]


def _tap(src_ref, sh_ref, s, c0):
    a, b = divmod(s, 8)
    start = pl.multiple_of(c0 + 8 * a, 8)
    if b == 0:
        return src_ref[pl.ds(start, CONV_ROWS), :]
    return sh_ref[b - 1, pl.ds(start, CONV_ROWS), :]


def _glu_rows(u, ch):
    return u[:, :ch] * _fast_sigmoid(u[:, ch:])


def _fill_z(zs_ref, zsh_ref, uc_ref, up_ref, i, ch, n):
    zs_ref[0:HALO] = jnp.where(i > 0, _glu_rows(up_ref[...], ch), 0.0)
    zs_ref[HALO:] = _glu_rows(uc_ref[...], ch)
    _shift_copies(zs_ref, zsh_ref, n - 8)


def _conv_fwd(u, w, b, lg, lb):
    T = u.shape[0]
    CH = u.shape[1] // 2
    tm = min(TM, T)
    n = tm + HALO
    hb = tm // HALO

    def body(uc_ref, up_ref, w_ref, b_ref, lg_ref, lb_ref, conv_ref, ypre_ref, zs_ref, zsh_ref):
        i = pl.program_id(0)
        _fill_z(zs_ref, zsh_ref, uc_ref, up_ref, i, CH, n)
        bias = b_ref[...]

        def chunk(ci, carry):
            c0 = pl.multiple_of(ci * CONV_ROWS, CONV_ROWS)
            acc = jnp.broadcast_to(bias, (CONV_ROWS, CH))
            for k in range(CONV_W):
                acc = acc + w_ref[k:k + 1, :] * _tap(zs_ref, zsh_ref, HALO - (CONV_W - 1) + k, c0)
            ypre_ref[pl.ds(c0, CONV_ROWS), :] = acc
            return carry

        lax.fori_loop(0, tm // CONV_ROWS, chunk, 0)
        y = ypre_ref[...]
        mu = jnp.mean(y, axis=-1, keepdims=True)
        d = y - mu
        var = jnp.mean(d * d, axis=-1, keepdims=True)
        o = d * lax.rsqrt(var + EPS) * lg_ref[...] + lb_ref[...]
        conv_ref[...] = (o * _fast_sigmoid(o)).astype(BF16)

    vec = pl.BlockSpec((1, CH), lambda i: (0, 0))
    return pl.pallas_call(
        body, name="conv_fwd", grid=(T // tm,),
        in_specs=[pl.BlockSpec((tm, 2 * CH), lambda i: (i, 0)),
                  pl.BlockSpec((HALO, 2 * CH), lambda i: (jnp.maximum(i * hb - 1, 0), 0)),
                  pl.BlockSpec((CONV_W, CH), lambda i: (0, 0)), vec, vec, vec],
        out_specs=[pl.BlockSpec((tm, CH), lambda i: (i, 0)), pl.BlockSpec((tm, CH), lambda i: (i, 0))],
        out_shape=[jax.ShapeDtypeStruct((T, CH), BF16), jax.ShapeDtypeStruct((T, CH), F32)],
        scratch_shapes=[pltpu.VMEM((n, CH), F32), pltpu.VMEM((7, n - 8, CH), F32)],
        compiler_params=_cp(1),
    )(u, u, w, b, lg, lb)


def _mixout_fwd(x, attn, conv, wo):
    T, D = x.shape
    tm = min(TM_MIX, T)
    A = attn.shape[1]

    def body(x_ref, a_ref, c_ref, w_ref, xo_ref):
        xo_ref[...] = x_ref[...] + _dot(a_ref[...], w_ref[:A, :]) + _dot(c_ref[...], w_ref[A:, :])

    return pl.pallas_call(
        body, name="mixout_fwd", grid=(T // tm,),
        in_specs=[pl.BlockSpec((tm, D), lambda i: (i, 0)), pl.BlockSpec((tm, A), lambda i: (i, 0)),
                  pl.BlockSpec((tm, conv.shape[1]), lambda i: (i, 0)), pl.BlockSpec(wo.shape, lambda i: (0, 0))],
        out_specs=pl.BlockSpec((tm, D), lambda i: (i, 0)),
        out_shape=jax.ShapeDtypeStruct((T, D), F32),
        compiler_params=_cp(1),
    )(x, attn, conv, wo)


def _rms_bwd_rows(dh, xf, g):
    xh, r = _rms(xf, None)
    dxn = dh * g
    dx = r * (dxn - xh * jnp.mean(dxn * xh, axis=-1, keepdims=True))
    return dx, jnp.sum(dh * xh, axis=0, keepdims=True), xh * g


def _loss_head(x, g, tgt):
    T, D = x.shape
    tm = min(TM, T)

    def body(x_ref, g_ref, t_ref, loss_ref, dx_ref, dg_ref):
        @pl.when(pl.program_id(0) == 0)
        def _():
            loss_ref[...] = jnp.zeros_like(loss_ref)
            dg_ref[...] = jnp.zeros_like(dg_ref)

        xf = x_ref[...]
        g = g_ref[...]
        xh, _ = _rms(xf, None)
        e = xh * g - t_ref[...]
        loss_ref[...] += 0.5 * jnp.sum(jnp.mean(e * e, axis=-1, keepdims=True), axis=0, keepdims=True)
        dx, dg, _ = _rms_bwd_rows(e * (1.0 / D), xf, g)
        dx_ref[...] = dx
        dg_ref[...] += dg

    return pl.pallas_call(
        body, name="loss_head", grid=(T // tm,),
        in_specs=[pl.BlockSpec((tm, D), lambda i: (i, 0)), pl.BlockSpec((1, D), lambda i: (0, 0)),
                  pl.BlockSpec((tm, D), lambda i: (i, 0))],
        out_specs=[pl.BlockSpec((1, 1), lambda i: (0, 0)), pl.BlockSpec((tm, D), lambda i: (i, 0)),
                   pl.BlockSpec((1, D), lambda i: (0, 0))],
        out_shape=[jax.ShapeDtypeStruct((1, 1), F32), jax.ShapeDtypeStruct((T, D), F32),
                   jax.ShapeDtypeStruct((1, D), F32)],
        compiler_params=_cp(1),
    )(x, g, tgt)


def _ffn_bwd(dxo, x, g, gu, win, wout, dep):
    T, D = x.shape
    FB = win.shape[2]
    tm = min(TM_FFN_BWD, T)

    def body(dxo_ref, x_ref, g_ref, gu_ref, win_hbm, wout_hbm, dep_ref,
             dxi_ref, dg_ref, hb_ref, dgu_ref, a_ref, dyb_ref, win_v, wout_v, sems):
        @pl.when(pl.program_id(0) == 0)
        def _():
            _load_ffn_weights(win_hbm, wout_hbm, win_v, wout_v, sems)
            dg_ref[...] = jnp.zeros_like(dg_ref)

        dyb = (0.5 * dxo_ref[...]).astype(BF16)
        dyb_ref[...] = dyb
        dh = jnp.zeros((tm, D), F32)
        for blk in range(2):
            for lo, sz in _lane_chunks(FB):
                cols = pl.ds(blk * FB + lo, sz)
                da = _dot_nt(dyb, wout_v[cols, :])
                gate = gu_ref[0, :, cols].astype(F32)
                up = gu_ref[1, :, cols].astype(F32)
                sg = _fast_sigmoid(gate)
                s = gate * sg
                a_ref[:, cols] = (s * up).astype(BF16)
                dgate = (da * up * (sg + s * (1.0 - sg))).astype(BF16)
                dup = (da * s).astype(BF16)
                dgu_ref[0, :, cols] = dgate
                dgu_ref[1, :, cols] = dup
                dh = dh + _dot_nt(dgate, win_v[blk, :, pl.ds(lo, sz)]) + _dot_nt(dup, win_v[2 + blk, :, pl.ds(lo, sz)])
        dx, dg, h = _rms_bwd_rows(dh, x_ref[...], g_ref[...])
        dxi_ref[...] = dxo_ref[...] + dx
        dg_ref[...] += dg
        hb_ref[...] = h.astype(BF16)

    row = pl.BlockSpec((tm, D), lambda i: (i, 0))
    act = pl.BlockSpec((2, tm, 2 * FB), lambda i: (0, i, 0))
    return pl.pallas_call(
        body, name="ffn_bwd", grid=(T // tm,),
        in_specs=[row, row, pl.BlockSpec((1, D), lambda i: (0, 0)), act, ANY, ANY, ANY],
        out_specs=[row, pl.BlockSpec((1, D), lambda i: (0, 0)), row, act,
                   pl.BlockSpec((tm, 2 * FB), lambda i: (i, 0)), row],
        out_shape=[jax.ShapeDtypeStruct((T, D), F32), jax.ShapeDtypeStruct((1, D), F32),
                   jax.ShapeDtypeStruct((T, D), BF16), jax.ShapeDtypeStruct((2, T, 2 * FB), BF16),
                   jax.ShapeDtypeStruct((T, 2 * FB), BF16), jax.ShapeDtypeStruct((T, D), BF16)],
        scratch_shapes=[pltpu.VMEM(win.shape, BF16), pltpu.VMEM(wout.shape, BF16), pltpu.SemaphoreType.DMA((6,))],
        compiler_params=_cp(1),
    )(dxo, x, g, gu, win, wout, dep)


def _mix_rms_bwd(dxo, x, g, dzs, wts):
    T, D = x.shape
    tm = min(TM, T)
    npair = len(dzs)

    def body(*refs):
        dxo_ref, x_ref, g_ref = refs[:3]
        dz_refs, w_refs = refs[3:3 + npair], refs[3 + npair:3 + 2 * npair]
        dxi_ref, dg_ref, hb_ref = refs[3 + 2 * npair:]

        @pl.when(pl.program_id(0) == 0)
        def _():
            dg_ref[...] = jnp.zeros_like(dg_ref)

        dh = jnp.zeros((tm, D), F32)
        for p in range(npair):
            dh = dh + _dot(dz_refs[p][...], w_refs[p][...])
        dx, dg, h = _rms_bwd_rows(dh, x_ref[...], g_ref[...])
        dxi_ref[...] = dxo_ref[...] + dx
        dg_ref[...] += dg
        hb_ref[...] = h.astype(BF16)

    row = pl.BlockSpec((tm, D), lambda i: (i, 0))
    return pl.pallas_call(
        body, name="mix_rms_bwd", grid=(T // tm,),
        in_specs=[row, row, pl.BlockSpec((1, D), lambda i: (0, 0))]
                 + [pl.BlockSpec((tm, dz.shape[1]), lambda i: (i, 0)) for dz in dzs]
                 + [pl.BlockSpec(w.shape, lambda i: (0, 0)) for w in wts],
        out_specs=[row, pl.BlockSpec((1, D), lambda i: (0, 0)), row],
        out_shape=[jax.ShapeDtypeStruct((T, D), F32), jax.ShapeDtypeStruct((1, D), F32),
                   jax.ShapeDtypeStruct((T, D), BF16)],
        compiler_params=_cp(1),
    )(dxo, x, g, *dzs, *wts)


def _wgrad(name, a, b, a_spec, b_spec, out_shape, out_spec, nblk, dep, acc_shape):
    T = a.shape[0]
    tk = min(TK_WGRAD, T)
    nk = T // tk

    def body(a_ref, b_ref, dep_ref, o_ref, acc_ref):
        k = pl.program_id(1)

        @pl.when(k == 0)
        def _():
            acc_ref[...] = jnp.zeros_like(acc_ref)

        acc_ref[...] += _dot_tn(a_ref[...], b_ref[...])

        @pl.when(k == nk - 1)
        def _():
            o_ref[...] = acc_ref[...].reshape(o_ref.shape).astype(BF16)

    return pl.pallas_call(
        body, name=name, grid=(nblk, nk), in_specs=[a_spec, b_spec, ANY], out_specs=out_spec,
        out_shape=jax.ShapeDtypeStruct(out_shape, BF16), scratch_shapes=[pltpu.VMEM(acc_shape, F32)],
        compiler_params=_cp(2),
    )(a, b, dep)


def _wgrad_ffn_in(hb, dgu, dep):
    T, D = hb.shape
    FB = dgu.shape[2] // 2
    tk = min(TK_WGRAD, T)
    return _wgrad("wgrad_ffn_in", hb, dgu,
                  pl.BlockSpec((tk, D), lambda b, k: (k, 0)),
                  pl.BlockSpec((None, tk, FB), lambda b, k: (b // 2, k, b % 2)),
                  (4, D, FB), pl.BlockSpec((None, D, FB), lambda b, k: (b, 0, 0)), 4, dep, (D, FB))


def _wgrad_ffn_out(a, dyb, dep):
    T, D = dyb.shape
    FB = a.shape[1] // 2
    tk = min(TK_WGRAD, T)
    return _wgrad("wgrad_ffn_out", a, dyb,
                  pl.BlockSpec((tk, FB), lambda b, k: (k, b)),
                  pl.BlockSpec((tk, D), lambda b, k: (k, 0)),
                  (4, FB // 2, D), pl.BlockSpec((2, FB // 2, D), lambda b, k: (b, 0, 0)), 2, dep, (FB, D))


def _wgrad_cat(a_list, b_list):
    T = a_list[0].shape[0]
    tk = min(TK_WGRAD, T)
    nk = T // tk
    na = len(a_list)
    M, N = sum(a.shape[1] for a in a_list), sum(b.shape[1] for b in b_list)

    def body(*refs):
        a_refs, b_refs, o_ref, acc_ref = refs[:na], refs[na:-2], refs[-2], refs[-1]
        k = pl.program_id(0)

        @pl.when(k == 0)
        def _():
            acc_ref[...] = jnp.zeros_like(acc_ref)

        r0 = 0
        for a_ref in a_refs:
            c0 = 0
            for b_ref in b_refs:
                m, n = a_ref.shape[1], b_ref.shape[1]
                acc_ref[r0:r0 + m, c0:c0 + n] += _dot_tn(a_ref[...], b_ref[...])
                c0 += n
            r0 += a_ref.shape[1]

        @pl.when(k == nk - 1)
        def _():
            o_ref[...] = acc_ref[...].astype(BF16)

    return pl.pallas_call(
        body, name="wgrad_cat", grid=(nk,),
        in_specs=[pl.BlockSpec((tk, v.shape[1]), lambda k: (k, 0)) for v in list(a_list) + list(b_list)],
        out_specs=pl.BlockSpec((M, N), lambda k: (0, 0)),
        out_shape=jax.ShapeDtypeStruct((M, N), BF16), scratch_shapes=[pltpu.VMEM((M, N), F32)],
        compiler_params=_cp(1),
    )(*a_list, *b_list)


def _mixout_bwd(dxo, wo):
    T, D = dxo.shape
    tm = min(TM_MIX, T)
    A = ATTN_W
    C = wo.shape[0] - A

    def body(dxo_ref, w_ref, dyb_ref, da_ref, dc_ref):
        dyb = dxo_ref[...].astype(BF16)
        dyb_ref[...] = dyb
        da_ref[...] = _dot_nt(dyb, w_ref[:A, :]).astype(BF16)
        dc_ref[...] = _dot_nt(dyb, w_ref[A:, :])

    return pl.pallas_call(
        body, name="mixout_bwd", grid=(T // tm,),
        in_specs=[pl.BlockSpec((tm, D), lambda i: (i, 0)), pl.BlockSpec(wo.shape, lambda i: (0, 0))],
        out_specs=[pl.BlockSpec((tm, D), lambda i: (i, 0)), pl.BlockSpec((tm, A), lambda i: (i, 0)),
                   pl.BlockSpec((tm, C), lambda i: (i, 0))],
        out_shape=[jax.ShapeDtypeStruct((T, D), BF16), jax.ShapeDtypeStruct((T, A), BF16),
                   jax.ShapeDtypeStruct((T, C), F32)],
        compiler_params=_cp(1),
    )(dxo, wo)


def _conv_bwd(dconv, ypre, u, w, lg, lb):
    T, CH = dconv.shape
    tm = min(TM, T)
    n = tm + HALO
    hb = tm // HALO
    nt = T // tm
    nchunk = tm // CONV_ROWS

    def body(dc_ref, dcn_ref, yp_ref, ypn_ref, uc_ref, up_ref, w_ref, lg_ref, lb_ref,
             du_ref, dw_ref, dvec_ref, zs_ref, zsh_ref, dy_ref, dysh_ref, dz_ref, dwacc_ref):
        i = pl.program_id(0)

        @pl.when(i == 0)
        def _():
            dwacc_ref[...] = jnp.zeros_like(dwacc_ref)
            dvec_ref[...] = jnp.zeros_like(dvec_ref)

        g, bb = lg_ref[...], lb_ref[...]

        def ln_bwd(dc, yp):
            mu = jnp.mean(yp, axis=-1, keepdims=True)
            d = yp - mu
            rs = lax.rsqrt(jnp.mean(d * d, axis=-1, keepdims=True) + EPS)
            yn = d * rs
            o = yn * g + bb
            sg = _fast_sigmoid(o)
            do = dc * (sg * (1.0 + o * (1.0 - sg)))
            dyn = do * g
            dyp = rs * (dyn - jnp.mean(dyn, axis=-1, keepdims=True)
                        - yn * jnp.mean(dyn * yn, axis=-1, keepdims=True))
            return dyp, do, yn

        dyp, do, yn = ln_bwd(dc_ref[...], yp_ref[...])
        dvec_ref[0:1, :] += jnp.sum(dyp, axis=0, keepdims=True)
        dvec_ref[1:2, :] += jnp.sum(do * yn, axis=0, keepdims=True)
        dvec_ref[2:3, :] += jnp.sum(do, axis=0, keepdims=True)
        dy_ref[0:tm] = dyp
        dyh, _, _ = ln_bwd(dcn_ref[...], ypn_ref[...])
        dy_ref[tm:] = jnp.where(i < nt - 1, dyh, 0.0)
        _shift_copies(dy_ref, dysh_ref, n - 8)
        _fill_z(zs_ref, zsh_ref, uc_ref, up_ref, i, CH, n)

        def chunk(ci, carry):
            c0 = pl.multiple_of(ci * CONV_ROWS, CONV_ROWS)
            acc = jnp.zeros((CONV_ROWS, CH), F32)
            for k in range(CONV_W):
                acc = acc + w_ref[k:k + 1, :] * _tap(dy_ref, dysh_ref, CONV_W - 1 - k, c0)
            dz_ref[pl.ds(c0, CONV_ROWS), :] = acc
            dyc = dy_ref[pl.ds(c0, CONV_ROWS), :]
            for k in range(CONV_W):
                prod = dyc * _tap(zs_ref, zsh_ref, HALO - (CONV_W - 1) + k, c0)
                dwacc_ref[k] += jnp.sum(prod.reshape(CONV_ROWS // 8, 8, CH), axis=0)
            return carry

        lax.fori_loop(0, nchunk, chunk, 0)

        @pl.when(i == nt - 1)
        def _():
            dw_ref[...] = jnp.sum(dwacc_ref[...], axis=1)

        uc = uc_ref[...]
        a = uc[:, :CH]
        sg = _fast_sigmoid(uc[:, CH:])
        dz = dz_ref[...]
        du_ref[:, :CH] = (dz * sg).astype(BF16)
        du_ref[:, CH:] = (dz * a * sg * (1.0 - sg)).astype(BF16)

    cur = lambda c: pl.BlockSpec((tm, c), lambda i: (i, 0))
    nxt = lambda c: pl.BlockSpec((HALO, c), lambda i: (jnp.minimum((i + 1) * hb, T // HALO - 1), 0))
    vec = pl.BlockSpec((1, CH), lambda i: (0, 0))
    return pl.pallas_call(
        body, name="conv_bwd", grid=(nt,),
        in_specs=[cur(CH), nxt(CH), cur(CH), nxt(CH), cur(2 * CH),
                  pl.BlockSpec((HALO, 2 * CH), lambda i: (jnp.maximum(i * hb - 1, 0), 0)),
                  pl.BlockSpec((CONV_W, CH), lambda i: (0, 0)), vec, vec],
        out_specs=[pl.BlockSpec((tm, 2 * CH), lambda i: (i, 0)), pl.BlockSpec((32, CH), lambda i: (0, 0)),
                   pl.BlockSpec((8, CH), lambda i: (0, 0))],
        out_shape=[jax.ShapeDtypeStruct((T, 2 * CH), BF16), jax.ShapeDtypeStruct((32, CH), F32),
                   jax.ShapeDtypeStruct((8, CH), F32)],
        scratch_shapes=[pltpu.VMEM((n, CH), F32), pltpu.VMEM((7, n - 8, CH), F32),
                        pltpu.VMEM((n, CH), F32), pltpu.VMEM((7, n - 8, CH), F32), pltpu.VMEM((tm, CH), F32),
                        pltpu.VMEM((32, 8, CH), F32)],
        compiler_params=_cp(1),
    )(dconv, dconv, ypre, ypre, u, u, w, lg, lb)


def _attn_bwd(sinks, tab, qkv, dattn):
    T = qkv.shape[0]
    nb = T // WINDOW

    def body(sink_ref, tab_ref, q_ref, kvp_ref, kvc_ref, do_ref, dq_ref, dkv_ref, dsk_ref, carry_ref):
        n = pl.program_id(0)

        @pl.when(n == 0)
        def _():
            dsk_ref[...] = jnp.zeros_like(dsk_ref)
            carry_ref[...] = jnp.zeros_like(carry_ref)

        @pl.when(n < nb)
        def _():
            seen = _first_block_mask(n)
            for g in range(N_KV):
                qs = _stack_heads(q_ref, g)
                dos = _stack_heads(do_ref, g)
                k = _band(kvp_ref, kvc_ref, g * HEAD_DIM)
                v = _band(kvp_ref, kvc_ref, KV_W + g * HEAD_DIM)
                p, ps = _attn_probs(qs, k, tab_ref[g], seen, _sink_col(sink_ref, g))
                dp = _dot_nt(dos, v)
                delta = jnp.sum(p * dp, axis=-1, keepdims=True)
                dsb = (p * (dp - delta)).astype(BF16)
                dsink = -ps * delta
                dqs = _dot(dsb, k) * SCALE
                dk = _dot_tn(dsb, qs) * SCALE
                dv = _dot_tn(p.astype(BF16), dos)
                for i in range(GROUP):
                    h = GROUP * g + i
                    dq_ref[:, h * HEAD_DIM:(h + 1) * HEAD_DIM] = dqs[i * WINDOW:(i + 1) * WINDOW].astype(BF16)
                    dsk_ref[h:h + 1, :] += jnp.sum(dsink[i * WINDOW:(i + 1) * WINDOW], axis=0, keepdims=True)
                for off, d in ((g * HEAD_DIM, dk), (KV_W + g * HEAD_DIM, dv)):
                    dkv_ref[:, off:off + HEAD_DIM] = (carry_ref[:, off:off + HEAD_DIM] + d[:WINDOW]).astype(BF16)
                    carry_ref[:, off:off + HEAD_DIM] = d[WINDOW:]

        @pl.when(n == nb)
        def _():
            dkv_ref[...] = carry_ref[...].astype(BF16)

    last = nb - 1
    return pl.pallas_call(
        body, name="attn_bwd", grid=(nb + 1,),
        in_specs=[pl.BlockSpec(memory_space=pltpu.SMEM),
                  pl.BlockSpec(tab.shape, lambda n: (0, 0, 0)),
                  pl.BlockSpec((WINDOW, ATTN_W), lambda n: (jnp.minimum(n, last), 0)),
                  pl.BlockSpec((WINDOW, 2 * KV_W), lambda n: (jnp.clip(n - 1, 0, last), 2)),
                  pl.BlockSpec((WINDOW, 2 * KV_W), lambda n: (jnp.minimum(n, last), 2)),
                  pl.BlockSpec((WINDOW, ATTN_W), lambda n: (jnp.minimum(n, last), 0))],
        out_specs=[pl.BlockSpec((WINDOW, ATTN_W), lambda n: (jnp.minimum(n, last), 0)),
                   pl.BlockSpec((WINDOW, 2 * KV_W), lambda n: (jnp.maximum(n - 1, 0), 0)),
                   pl.BlockSpec((8, LANES), lambda n: (0, 0))],
        out_shape=[jax.ShapeDtypeStruct((T, ATTN_W), BF16), jax.ShapeDtypeStruct((T, 2 * KV_W), BF16),
                   jax.ShapeDtypeStruct((8, LANES), F32)],
        scratch_shapes=[pltpu.VMEM((WINDOW, 2 * KV_W), F32)],
        compiler_params=_cp(1),
    )(sinks, tab, qkv, qkv, qkv, dattn)


def _pack(arrs):
    flat = jnp.concatenate([a.reshape(-1) for a in arrs])
    pad = -flat.shape[0] % (8 * LANES)
    return jnp.pad(flat, (0, pad)).reshape(1, -1, LANES)


def _unpack(packed, like):
    flat = packed.reshape(-1)
    out, off = [], 0
    for a in like:
        out.append(flat[off:off + a.size].reshape(a.shape))
        off += a.size
    return out


def kernel(x, norm_ffn1, w_ffn1_in, w_ffn1_out, norm_mix, w_in, sinks, w_dw, b_dw, conv_ln_g, conv_ln_b, w_out, norm_ffn2, w_ffn2_in, w_ffn2_out, final_norm, loss_target, m_norm_ffn1, m_w_ffn1_in, m_w_ffn1_out, m_norm_mix, m_w_in, m_sinks, m_w_dw, m_b_dw, m_conv_ln_g, m_conv_ln_b, m_w_out, m_norm_ffn2, m_w_ffn2_in, m_w_ffn2_out, m_final_norm, v_norm_ffn1, v_w_ffn1_in, v_w_ffn1_out, v_norm_mix, v_w_in, v_sinks, v_w_dw, v_b_dw, v_conv_ln_g, v_conv_ln_b, v_w_out, v_norm_ffn2, v_w_ffn2_in, v_w_ffn2_out, v_final_norm):
    L, D = norm_ffn1.shape
    T = x.shape[1]
    FB = w_ffn1_in.shape[2]
    CH = b_dw.shape[1]
    QKV = ATTN_W + 2 * KV_W
    xs = x.reshape(T, D)
    tgt = loss_target.reshape(T, D)
    cx, cy, cc = lax.axis_index("x"), lax.axis_index("y"), lax.axis_index("c")
    chip = 2 * cx + cy
    cidx = cc.reshape(1).astype(jnp.int32)
    tr = lambda a_: jnp.transpose(a_, (0, 2, 1))
    big_w = (w_ffn1_in, w_ffn1_out, tr(w_in), w_out, w_ffn2_in, w_ffn2_out)
    big_m = (m_w_ffn1_in, m_w_ffn1_out, tr(m_w_in), m_w_out, m_w_ffn2_in, m_w_ffn2_out)
    big_v = (v_w_ffn1_in, v_w_ffn1_out, tr(v_w_in), v_w_out, v_w_ffn2_in, v_w_ffn2_out)
    NW = len(big_w) + 1

    def own_slot(a, slots=4, idx=chip):
        return lax.dynamic_update_index_in_dim(lax.empty((slots,) + a.shape, a.dtype), a, idx, 0)

    def shards(l, tok):
        return [own_slot((w_[l] + tok[0, 0]).astype(BF16)) for w_ in big_w] + [own_slot(w_dw[l] + tok[0, 0])]

    def gather_start(lands, tok):
        return _xchg_start("gather_start", [], lands, _gather_plan, tok)

    def gather_arrived(started, after, n, taps):
        _, lands, tok = _xchg_wait("gather_wait", started, 0, n, _gather_plan, after)
        return _xchg_start("gshare_start", [], lands[:-1] if taps else lands, _gshare_plan, tok, "sibling3"), lands[-1]

    def shared_weights(shared, after, n):
        _, mats, tok = _xchg_wait("gshare_wait", shared, 0, n, _gshare_plan, after, "sibling3")
        return mats, tok

    row = lambda a, l: a[l].reshape(1, -1)
    tab = _attn_bias_table()
    NB = len(big_w)

    saved, W = [], []
    zero_tok = jnp.zeros((8, LANES), F32)
    src0 = shards(0, zero_tok)
    started = gather_start(src0[:2], zero_tok)
    rest0 = gather_start(src0[2:], started[-1])
    cast = [None] + [shards(l, rest0[-1]) for l in range(1, L)]
    shared, _ = gather_arrived(started, [xs] + [a_ for c_ in cast[1:] for a_ in c_], 2, False)
    after = [shared[-1]]
    for l in range(L):
        mats, tok = shared_weights(shared, after, 2 if l == 0 else NB)
        started = None
        if l + 1 < L:
            started = gather_start(cast[l + 1], tok)
            tok = started[-1]
        x0 = xs
        x1, gu1 = _ffn_fwd(x0, row(norm_ffn1, l) + tok[0, 0], mats[0], mats[1].reshape(2 * FB, D))
        gm_row = row(norm_mix, l)
        if l == 0:
            shared, gdw = gather_arrived(rest0, [x1], NW - 2, True)
            rest, tok = shared_weights(shared, [shared[-1]], NB - 2)
            mats = list(mats) + list(rest)
            gm_row = gm_row + tok[0, 0]
        g1i, g1o, gi, go, g2i, g2o = mats
        w = dict(f1i=g1i, f1o=g1o.reshape(2 * FB, D), f2i=g2i, f2o=g2o.reshape(2 * FB, D),
                 wit=gi.reshape(-1, D), wo=go.reshape(-1, D),
                 wdw=jnp.transpose(gdw, (1, 0, 2)).reshape(CONV_W, CH))
        W.append(w)
        qkv, u = _mixproj_fwd(x1, gm_row, w["wit"])
        attn = _attn_fwd(row(sinks, l), tab, qkv)
        conv, ypre = _conv_fwd(u, w["wdw"], row(b_dw, l), row(conv_ln_g, l), row(conv_ln_b, l))
        x2 = _mixout_fwd(x1, attn, conv, w["wo"])
        g2_row = row(norm_ffn2, l)
        if started is not None and l > 0:
            shared, gdw = gather_arrived(started, [x2], NW, True)
            g2_row = g2_row + shared[-1][0, 0]
        xs, gu2 = _ffn_fwd(x2, g2_row, w["f2i"], w["f2o"])
        if started is not None and l == 0:
            shared, gdw = gather_arrived(started, [xs], NW, True)
        saved.append((x0, gu1, x1, qkv, u, attn, conv, ypre, x2, gu2))
        after = [xs]

    loss_part, dx, d_final = _loss_head(xs, final_norm.reshape(1, D), tgt)
    loss = lax.psum(loss_part[0, 0], ("x", "y", "c"))

    bufs = [[lax.empty(w_.shape, F32) for _ in range(4)] for w_ in big_w]
    d_n1, d_nm, d_n2 = [None] * L, [None] * L, [None] * L
    d_sk, d_bdw, d_lg, d_lb, d_wdw = [None] * L, [None] * L, [None] * L, [None] * L, [None] * L

    def sib_start(gs):
        return _xchg_start("sib_start", gs, [lax.empty((4, g.shape[1] // 2, g.shape[2]), g.dtype) for g in gs],
                           _sib_plan, zero_tok, "sibling")

    def reduce_start(sib_started, after, n):
        gs, sibs, _ = _xchg_wait("sib_wait", sib_started, n, n, _sib_plan, after, "sibling")
        parts = [_sum_halves(cidx, g, s_) for g, s_ in zip(gs, sibs)]
        lands = [own_slot(lax.dynamic_index_in_dim(p, chip, 0, keepdims=False)) for p in parts]
        return _xchg_start("rs_start", parts, lands, _rs_plan, zero_tok)

    def share_start(rs_started, after, n):
        _, qs, tok = _xchg_wait("rs_wait", rs_started, n, n, _rs_plan, after)
        return _xchg_start("qshare_start", qs, [lax.empty(q.shape, q.dtype) for q in qs], _whole_plan, tok, "sibling")

    def finish(l, shared, after, idxs):
        q_own, q_sib, _ = _xchg_wait("qshare_wait", shared, len(idxs), len(idxs), _whole_plan, after, "sibling")
        for k, t in enumerate(idxs):
            bufs[t] = _adamw_layer(cidx, q_own[k], q_sib[k], big_w[t], big_m[t], big_v[t], bufs[t], l)

    ALL = list(range(NB))
    EARLY, LATE = ALL[2:], ALL[:2]
    sib_pending = rs_pending = None
    shares = []
    tok = zero_tok
    for l in reversed(range(L)):
        w = W[l]
        x0, gu1, x1, qkv, u, attn, conv, ypre, x2, gu2 = saved[l]
        dx, d_n2[l], hb, dgu, a, dyb = _ffn_bwd(dx, x2, row(norm_ffn2, l), gu2, w["f2i"], w["f2o"], tok)
        g_f2i, g_f2o = _wgrad_ffn_in(hb, dgu, tok), _wgrad_ffn_out(a, dyb, tok)
        lg_row = row(conv_ln_g, l)
        if sib_pending is not None:
            rs_started = reduce_start(sib_pending[1], [g_f2o], NB)
            if rs_pending is not None:
                shares.append((rs_pending[0], share_start(rs_pending[1], [rs_started[-1]], NB)))
            rs_pending = (sib_pending[0], rs_started)
            lg_row = lg_row + rs_started[-1][0, 0]
        dyb, dattn, dconv = _mixout_bwd(dx, w["wo"])
        g_wo = _wgrad_cat([attn, conv], [dyb]).reshape(4, -1, D)
        du, dwdw, dvec = _conv_bwd(dconv, ypre, u, w["wdw"], lg_row, row(conv_ln_b, l))
        d_wdw[l], d_bdw[l], d_lg[l], d_lb[l] = dwdw[:CONV_W], dvec[0], dvec[1], dvec[2]
        dq, dkv, dsk = _attn_bwd(row(sinks, l), tab, qkv, dattn)
        d_sk[l] = dsk[:, 0]
        wit = w["wit"]
        dx, d_nm[l], hb = _mix_rms_bwd(dx, x1, row(norm_mix, l), [dq, dkv, du],
                                       [wit[:ATTN_W], wit[ATTN_W:QKV], wit[QKV:]])
        g_wi = _wgrad_cat([dq, dkv, du], [hb]).reshape(4, -1, D)
        if l == 0:
            sib_early = sib_start([g_wi, g_wo, g_f2i, g_f2o])
            tok = sib_early[-1]
        dx, d_n1[l], hb, dgu, a, dyb = _ffn_bwd(dx, x0, row(norm_ffn1, l), gu1, w["f1i"], w["f1o"], tok)
        if l == 0:
            rs_early = reduce_start(sib_early, [dx], len(EARLY))
            tok = rs_early[-1]
        g_f1i, g_f1o = _wgrad_ffn_in(hb, dgu, tok), _wgrad_ffn_out(a, dyb, tok)
        sib_started = sib_start([g_f1i, g_f1o] if l == 0 else [g_f1i, g_f1o, g_wi, g_wo, g_f2i, g_f2o])
        tok = sib_started[-1]
        sib_pending = (l, sib_started)
    grad_x = dx.reshape(x.shape)

    small_g = [jnp.concatenate(d, axis=0) for d in (d_n1, d_nm, d_n2)] + [d_final, jnp.stack(d_sk)] + \
              [jnp.stack(d) for d in (d_bdw, d_lg, d_lb, d_wdw)]
    packed = _pack(small_g)[0]
    small_started = _xchg_start("small_start", [packed], [own_slot(packed, 8, 4 * cx + 2 * cy + cc)], _slot_plan, tok, "all")

    after = [small_started[-1]]
    if rs_pending is not None:
        shares.append((rs_pending[0], share_start(rs_pending[1], after, NB)))
        after = [shares[-1][1][-1]]
    if shares:
        finish(*shares.pop(0), after, ALL)
        after = [b_[0] for b_ in bufs]
    rs_late = reduce_start(sib_pending[1], after, len(LATE))
    sh_early = share_start(rs_early, [rs_late[-1]], len(EARLY))
    after = [sh_early[-1]]
    sh_late = None
    for l, sh in shares:
        finish(l, sh, after, ALL)
        after = [b_[0] for b_ in bufs]
        if sh_late is None:
            sh_late = share_start(rs_late, after, len(LATE))
            after = [sh_late[-1]]
    if sh_late is None:
        sh_late = share_start(rs_late, after, len(LATE))
        after = [sh_late[-1]]
    _, (slots,), _ = _xchg_wait("small_wait", small_started, 1, 1, _slot_plan, after, "all")
    small_sum = _unpack(_sum_slots(slots), small_g)
    g_wdw = lax.dynamic_slice_in_dim(small_sum[8], chip * w_dw.shape[2], w_dw.shape[2], axis=2)
    small_g = [small_sum[0], small_sum[1], small_sum[2], small_sum[3].reshape(D), small_sum[4],
               small_sum[5], small_sum[6], small_sum[7], g_wdw]
    small_w = (norm_ffn1, norm_mix, norm_ffn2, final_norm, sinks, b_dw, conv_ln_g, conv_ln_b, w_dw)
    small_m = (m_norm_ffn1, m_norm_mix, m_norm_ffn2, m_final_norm, m_sinks, m_b_dw, m_conv_ln_g, m_conv_ln_b, m_w_dw)
    small_v = (v_norm_ffn1, v_norm_mix, v_norm_ffn2, v_final_norm, v_sinks, v_b_dw, v_conv_ln_g, v_conv_ln_b, v_w_dw)
    upd = _adamw(_pack(small_g), _pack(small_w), _pack(small_m), _pack(small_v))
    small_upd = [_unpack(u_, small_w) for u_ in upd]
    finish(0, sh_early, [upd[0]], EARLY)
    finish(0, sh_late, [bufs[t][0] for t in EARLY], LATE)

    order = ("norm_ffn1", "w_ffn1_in", "w_ffn1_out", "norm_mix", "w_in", "sinks", "w_dw", "b_dw", "conv_ln_g",
             "conv_ln_b", "w_out", "norm_ffn2", "w_ffn2_in", "w_ffn2_out", "final_norm")
    small_names = ("norm_ffn1", "norm_mix", "norm_ffn2", "final_norm", "sinks", "b_dw", "conv_ln_g", "conv_ln_b", "w_dw")
    big_names = ("w_ffn1_in", "w_ffn1_out", "w_in", "w_out", "w_ffn2_in", "w_ffn2_out")
    grads, deltas, new_m, new_v = {}, {}, {}, {}
    for i, nme in enumerate(small_names):
        grads[nme], deltas[nme], new_m[nme], new_v[nme] = small_g[i], small_upd[0][i], small_upd[1][i], small_upd[2][i]
    for i, nme in enumerate(big_names):
        grads[nme], deltas[nme], new_m[nme], new_v[nme] = [tr(b_) for b_ in bufs[i]] if nme == "w_in" else bufs[i]
    return (loss, grad_x, *[grads[n] for n in order], *[deltas[n] for n in order],
            *[new_m[n] for n in order], *[new_v[n] for n in order])
```

```python
import functools

import jax
import jax.numpy as jnp
from jax import lax
from jax.experimental import pallas as pl
from jax.experimental.pallas import tpu as pltpu

F32, BF16 = jnp.float32, jnp.bfloat16
EPS = 1e-6
NEG_INF = -1e30
HEAD_DIM = 64
N_HEADS = 8
N_KV = 2
GROUP = N_HEADS // N_KV
WINDOW = 128
ATTN_W = N_HEADS * HEAD_DIM
KV_W = N_KV * HEAD_DIM
CONV_W = 31
HALO = 32
CONV_ROWS = 32
SCALE = 1.0 / 8.0
ADAM_LR, ADAM_B1, ADAM_B2, ADAM_EPS, ADAM_WD, ADAM_STEP = 0.001, 0.9, 0.999, 1e-08, 0.01, 10
TM = 512
TM_FFN_BWD = 256
TK_WGRAD = 2048
TM_MIX = 1024
LANES = 128
VMEM_LIMIT = 52 * 1024 * 1024
MESH = pl.DeviceIdType.MESH
ANY = pl.BlockSpec(memory_space=pl.ANY)
HBM = pl.BlockSpec(memory_space=pltpu.HBM)
SEM = pl.BlockSpec(memory_space=pltpu.SEMAPHORE)
VMEM = pl.BlockSpec(memory_space=pltpu.VMEM)
EFFECT = pltpu.SideEffectType.DATAFLOW_SIDE_EFFECTING
TOKEN = jax.ShapeDtypeStruct((8, LANES), F32)


def _cp(n):
    return pltpu.CompilerParams(dimension_semantics=("arbitrary",) * n, vmem_limit_bytes=VMEM_LIMIT)


def _dot(a, b):
    return jnp.dot(a, b, preferred_element_type=F32)


def _dot_nt(a, b):
    return lax.dot_general(a, b, (((1,), (1,)), ((), ())), preferred_element_type=F32)


def _dot_tn(a, b):
    return lax.dot_general(a, b, (((0,), (0,)), ((), ())), preferred_element_type=F32)


def _place():
    x, y, c = lax.axis_index("x"), lax.axis_index("y"), lax.axis_index("c")
    chips = [(1 - x, y), (x, 1 - y), (1 - x, 1 - y)]
    return x, y, c, chips


def _rcopy(src, dst, send_sems, recv_sems, k, dev):
    return pltpu.make_async_remote_copy(src_ref=src, dst_ref=dst, send_sem=send_sems.at[k],
                                        recv_sem=recv_sems.at[k], device_id=dev, device_id_type=MESH)


def _hbm(a):
    return pltpu.with_memory_space_constraint(a, pltpu.HBM)


PEERS = {"chips": 3, "sibling": 1, "sibling3": 3, "all": 7}


def _targets(mode):
    x, y, c, chips = _place()
    b = 2 * x + y
    if mode == "chips":
        return b, c, [((px, py, c), 2 * px + py) for px, py in chips]
    if mode == "sibling":
        return b, c, [((x, y, 1 - c), b)]
    if mode == "sibling3":
        return b, c, [((x, y, 1 - c), 2 * px + py) for px, py in chips]
    flip = lambda v, f: 1 - v if f else v
    devs = [(flip(x, k >> 2 & 1), flip(y, k >> 1 & 1), flip(c, k & 1)) for k in range(1, 8)]
    return 4 * x + 2 * y + c, c, [(d, 4 * d[0] + 2 * d[1] + d[2]) for d in devs]


def _xchg_start(name, srcs, lands, plan, dep, mode="chips"):
    ns, nl, npeer = len(srcs), len(lands), PEERS[mode]

    def body(*refs):
        land = refs[ns:ns + nl]
        src = refs[:ns] if ns else land
        send_sems, recv_sems, token = refs[ns + nl + 1], refs[ns + nl + 2], refs[-1]
        me, c, peers = _targets(mode)
        for t in range(nl):
            for j, (dev, tag) in enumerate(peers):
                s, d, _ = plan(src[t], land[t], t, me, c, tag)
                _rcopy(s, d, send_sems, recv_sems, npeer * t + j, dev).start()
        token[...] = jnp.zeros_like(token)

    arrs = list(srcs) + list(lands)
    return pl.pallas_call(
        body, name=name,
        out_shape=(pltpu.SemaphoreType.DMA((npeer * nl,)), pltpu.SemaphoreType.DMA((npeer * nl,)),
                   *[pltpu.HBM(a.shape, a.dtype) for a in arrs], TOKEN),
        in_specs=[HBM] * (ns + nl) + [ANY], out_specs=(SEM, SEM, *[HBM] * (ns + nl), VMEM),
        input_output_aliases={i: 2 + i for i in range(ns + nl)},
        compiler_params=pltpu.CompilerParams(has_side_effects=EFFECT),
    )(*[_hbm(a) for a in arrs], dep)


def _xchg_wait(name, started, ns, nl, plan, after, mode="chips"):
    send_sems, recv_sems, thru = started[0], started[1], started[2:2 + ns + nl]
    npeer = PEERS[mode]

    def body(*refs):
        land = refs[ns:ns + nl]
        src = refs[:ns] if ns else land
        send_sems, recv_sems, token = refs[ns + nl], refs[ns + nl + 1], refs[-1]
        me, c, peers = _targets(mode)
        for t in range(nl):
            for j, (dev, tag) in enumerate(peers):
                s, _, a = plan(src[t], land[t], t, me, c, tag)
                cp = _rcopy(s, a, send_sems, recv_sems, npeer * t + j, dev)
                cp.wait_send()
                cp.wait_recv()
        token[...] = jnp.zeros_like(token)

    out = pl.pallas_call(
        body, name=name,
        out_shape=(*[pltpu.HBM(a.shape, a.dtype) for a in thru], TOKEN),
        in_specs=[HBM] * (ns + nl) + [SEM, SEM] + [ANY] * len(after), out_specs=(*[HBM] * (ns + nl), VMEM),
        input_output_aliases={i: i for i in range(ns + nl)},
        compiler_params=pltpu.CompilerParams(has_side_effects=EFFECT),
    )(*thru, send_sems, recv_sems, *after)
    return out[:ns], out[ns:ns + nl], out[-1]


def _half(ref_rows, which):
    h = ref_rows // 2
    return pl.ds(which * h, h)


def _gather_plan(src, land, t, b, c, pb):
    if land.shape[1] % 2 == 0:
        hs = _half(land.shape[1], c)
        return land.at[b, hs], land.at[b, hs], land.at[pb, hs]
    return land.at[b], land.at[b], land.at[pb]


def _gshare_plan(src, land, t, b, c, pb):
    return land.at[pb, _half(land.shape[1], c)], land.at[pb, _half(land.shape[1], c)], land.at[pb, _half(land.shape[1], 1 - c)]


def _rs_plan(src, land, t, b, c, pb):
    return src.at[pb], land.at[b], land.at[pb]


def _sib_plan(src, land, t, b, c, pb):
    return src.at[:, _half(src.shape[1], 1 - c), :], land, land


def _rows_block(h, cap=512):
    for rb in range(min(h, cap) // 16 * 16, 0, -16):
        if h % rb == 0:
            return rb
    return h


def _sum_halves(cidx, g, s):
    _, R, C = g.shape
    rb = _rows_block(R // 2)
    nr = R // 2 // rb

    def body(c_ref, g_ref, s_ref, o_ref):
        o_ref[...] = (g_ref[...].astype(F32) + s_ref[...].astype(F32)).astype(BF16)

    blk = (None, rb, C)
    return pl.pallas_call(
        body, name="sum_halves", out_shape=jax.ShapeDtypeStruct(s.shape, BF16),
        grid_spec=pltpu.PrefetchScalarGridSpec(
            num_scalar_prefetch=1, grid=(4, nr),
            in_specs=[pl.BlockSpec(blk, lambda p, i, c: (p, c[0] * nr + i, 0)),
                      pl.BlockSpec(blk, lambda p, i, c: (p, i, 0))],
            out_specs=pl.BlockSpec(blk, lambda p, i, c: (p, i, 0))),
        compiler_params=_cp(2),
    )(cidx, g, s)


def _whole_plan(src, land, t, me, c, tag):
    return src, land, land


def _slot_plan(src, land, t, me, c, tag):
    return src, land.at[me], land.at[tag]


def _adam_update(gg, w, m, v):
    m2 = ADAM_B1 * m + (1.0 - ADAM_B1) * gg
    v2 = ADAM_B2 * v + (1.0 - ADAM_B2) * (gg * gg)
    mh = m2 / (1.0 - ADAM_B1 ** ADAM_STEP)
    vh = v2 / (1.0 - ADAM_B2 ** ADAM_STEP)
    return -ADAM_LR * (mh / (jnp.sqrt(vh) + ADAM_EPS) + ADAM_WD * w), m2, v2


def _adamw_layer(cidx, q_own, q_sib, w, m, v, bufs, l):
    L, R, C = w.shape
    h = R // 2
    rb = _rows_block(h, 256)
    nr = h // rb

    def body(c_ref, qo_ref, qs_ref, w_ref, m_ref, v_ref, *rest):
        g_ref, d_ref, mo_ref, vo_ref = rest[-4:]
        own = pl.program_id(0) == c_ref[0]
        gg = jnp.zeros((rb, C), F32)
        for s in range(4):
            gg = gg + jnp.where(own, qo_ref[s], qs_ref[s]).astype(F32)
        g_ref[...] = gg
        d_ref[...], mo_ref[...], vo_ref[...] = _adam_update(gg, w_ref[...], m_ref[...], v_ref[...])

    q_own_spec = pl.BlockSpec((4, rb, C), lambda hh, i, c: (0, jnp.where(hh == c[0], i, 0), 0))
    q_sib_spec = pl.BlockSpec((4, rb, C), lambda hh, i, c: (0, jnp.where(hh == c[0], 0, i), 0))
    wspec = pl.BlockSpec((None, rb, C), lambda hh, i, c: (l, hh * nr + i, 0))
    return pl.pallas_call(
        body, name="adamw_layer", out_shape=[jax.ShapeDtypeStruct(w.shape, F32)] * 4,
        grid_spec=pltpu.PrefetchScalarGridSpec(
            num_scalar_prefetch=1, grid=(2, nr),
            in_specs=[q_own_spec, q_sib_spec, wspec, wspec, wspec] + [ANY] * 4, out_specs=[wspec] * 4),
        input_output_aliases={6 + k: k for k in range(4)},
        compiler_params=_cp(2),
    )(cidx, q_own, q_sib, w, m, v, *bufs)


def _adamw(g, w, m, v):
    L, R, C = g.shape
    rb = _rows_block(R)

    def body(g_ref, w_ref, m_ref, v_ref, d_ref, mo_ref, vo_ref):
        d_ref[...], mo_ref[...], vo_ref[...] = _adam_update(g_ref[...], w_ref[...], m_ref[...], v_ref[...])

    spec = pl.BlockSpec((None, rb, C), lambda l, i: (l, i, 0))
    return pl.pallas_call(
        body, name="adamw", grid=(L, R // rb), in_specs=[spec] * 4, out_specs=[spec] * 3,
        out_shape=[jax.ShapeDtypeStruct(g.shape, F32)] * 3, compiler_params=_cp(2),
    )(g, w, m, v)


def _sum_slots(buf):
    def body(b_ref, o_ref):
        acc = b_ref[0]
        for k in range(1, 8):
            acc = acc + b_ref[k]
        o_ref[...] = acc

    return pl.pallas_call(body, name="sum_slots", in_specs=[VMEM], out_specs=VMEM,
                          out_shape=jax.ShapeDtypeStruct(buf.shape[1:], F32))(buf)


def _rms(xf, g):
    r = lax.rsqrt(jnp.mean(xf * xf, axis=-1, keepdims=True) + EPS)
    return xf * r, r


def _lane_chunks(n):
    lo = (n // LANES + 1) // 2 * LANES
    return ((0, lo), (lo, n - lo))


def _load_ffn_weights(win_hbm, wout_hbm, win_v, wout_v, sems):
    fb = win_v.shape[2]
    loads = [pltpu.make_async_copy(win_hbm.at[k], win_v.at[k], sems.at[k]) for k in range(4)]
    loads += [pltpu.make_async_copy(wout_hbm.at[pl.ds(k * fb, fb)], wout_v.at[pl.ds(k * fb, fb)], sems.at[4 + k])
              for k in range(2)]
    for cp in loads:
        cp.start()
    for cp in loads:
        cp.wait()


def _fast_sigmoid(v):
    return pl.reciprocal(1.0 + jnp.exp(-v), approx=True)


def _ffn_fwd(x, g, win, wout):
    T, D = x.shape
    FB = win.shape[2]
    tm = min(TM, T)

    def body(x_ref, g_ref, win_hbm, wout_hbm, xo_ref, gu_ref, win_v, wout_v, sems):
        @pl.when(pl.program_id(0) == 0)
        def _():
            _load_ffn_weights(win_hbm, wout_hbm, win_v, wout_v, sems)

        xf = x_ref[...]
        xh, _ = _rms(xf, None)
        h = (xh * g_ref[...]).astype(BF16)
        acc = jnp.zeros((tm, D), F32)
        for blk in range(2):
            for lo, sz in _lane_chunks(FB):
                cols = pl.ds(blk * FB + lo, sz)
                gate = _dot(h, win_v[blk, :, pl.ds(lo, sz)])
                up = _dot(h, win_v[2 + blk, :, pl.ds(lo, sz)])
                gu_ref[0, :, cols] = gate.astype(BF16)
                gu_ref[1, :, cols] = up.astype(BF16)
                a = (gate * _fast_sigmoid(gate) * up).astype(BF16)
                acc = acc + _dot(a, wout_v[cols, :])
        xo_ref[...] = xf + 0.5 * acc

    row = pl.BlockSpec((tm, D), lambda i: (i, 0))
    return pl.pallas_call(
        body, name="ffn_fwd", grid=(T // tm,),
        in_specs=[row, pl.BlockSpec((1, D), lambda i: (0, 0)), ANY, ANY],
        out_specs=[row, pl.BlockSpec((2, tm, 2 * FB), lambda i: (0, i, 0))],
        out_shape=[jax.ShapeDtypeStruct((T, D), F32), jax.ShapeDtypeStruct((2, T, 2 * FB), BF16)],
        scratch_shapes=[pltpu.VMEM(win.shape, BF16), pltpu.VMEM(wout.shape, BF16), pltpu.SemaphoreType.DMA((6,))],
        compiler_params=_cp(1),
    )(x, g, win, wout)


def _mixproj_fwd(x, g, wt):
    T, D = x.shape
    W = wt.shape[0]
    QKV = ATTN_W + 2 * KV_W
    tm = min(TM_MIX, T)

    def body(x_ref, g_ref, w_ref, qkv_ref, u_ref):
        xh, _ = _rms(x_ref[...], None)
        h = (xh * g_ref[...]).astype(BF16)
        qkv_ref[...] = _dot_nt(h, w_ref[:QKV, :]).astype(BF16)
        u_ref[...] = _dot_nt(h, w_ref[QKV:, :])

    return pl.pallas_call(
        body, name="mixproj_fwd", grid=(T // tm,),
        in_specs=[pl.BlockSpec((tm, D), lambda i: (i, 0)), pl.BlockSpec((1, D), lambda i: (0, 0)),
                  pl.BlockSpec((W, D), lambda i: (0, 0))],
        out_specs=[pl.BlockSpec((tm, QKV), lambda i: (i, 0)), pl.BlockSpec((tm, W - QKV), lambda i: (i, 0))],
        out_shape=[jax.ShapeDtypeStruct((T, QKV), BF16), jax.ShapeDtypeStruct((T, W - QKV), F32)],
        compiler_params=_cp(1),
    )(x, g, wt)


def _attn_bias_table():
    rows, cols = GROUP * WINDOW, 2 * WINDOW
    row = lax.broadcasted_iota(jnp.int32, (N_KV, rows, cols), 1)
    col = lax.broadcasted_iota(jnp.int32, (N_KV, rows, cols), 2)
    head = GROUP * lax.broadcasted_iota(jnp.int32, (N_KV, rows, cols), 0) + (row >> 7)
    dist = (row & (WINDOW - 1)) + WINDOW - col
    slope = jnp.exp2(-(head + 1).astype(F32))
    return jnp.where((dist >= 0) & (dist < WINDOW), -slope * dist.astype(F32), NEG_INF)


def _first_block_mask(n):
    col = lax.broadcasted_iota(jnp.int32, (GROUP * WINDOW, 2 * WINDOW), 1)
    return (n > 0) | (col >= WINDOW)


def _sink_col(sink_ref, g):
    hi = lax.broadcasted_iota(jnp.int32, (GROUP * WINDOW, 1), 0) >> 7
    col = jnp.zeros((GROUP * WINDOW, 1), F32)
    for i in range(GROUP):
        col = jnp.where(hi == i, sink_ref[0, GROUP * g + i], col)
    return col


def _stack_heads(ref, g):
    return jnp.concatenate([ref[:, (GROUP * g + i) * HEAD_DIM:(GROUP * g + i + 1) * HEAD_DIM]
                            for i in range(GROUP)], axis=0)


def _band(kvp_ref, kvc_ref, off):
    return jnp.concatenate([kvp_ref[:, off:off + HEAD_DIM], kvc_ref[:, off:off + HEAD_DIM]], axis=0)


def _attn_probs(qs, k, bias, seen, sink):
    s = jnp.where(seen, _dot_nt(qs, k) * SCALE + bias, NEG_INF)
    m = jnp.maximum(jnp.max(s, axis=-1, keepdims=True), sink)
    p = jnp.exp(s - m)
    es = jnp.exp(sink - m)
    inv = 1.0 / (jnp.sum(p, axis=-1, keepdims=True) + es)
    return p * inv, es * inv


def _attn_fwd(sinks, tab, qkv):
    T = qkv.shape[0]
    nb = T // WINDOW

    def body(sink_ref, tab_ref, q_ref, kvp_ref, kvc_ref, o_ref):
        seen = _first_block_mask(pl.program_id(0))
        for g in range(N_KV):
            qs = _stack_heads(q_ref, g)
            k = _band(kvp_ref, kvc_ref, g * HEAD_DIM)
            v = _band(kvp_ref, kvc_ref, KV_W + g * HEAD_DIM)
            p, _ = _attn_probs(qs, k, tab_ref[g], seen, _sink_col(sink_ref, g))
            o = _dot(p.astype(BF16), v)
            for i in range(GROUP):
                h = GROUP * g + i
                o_ref[:, h * HEAD_DIM:(h + 1) * HEAD_DIM] = o[i * WINDOW:(i + 1) * WINDOW].astype(BF16)

    return pl.pallas_call(
        body, name="attn_fwd", grid=(nb,),
        in_specs=[pl.BlockSpec(memory_space=pltpu.SMEM),
                  pl.BlockSpec(tab.shape, lambda n: (0, 0, 0)),
                  pl.BlockSpec((WINDOW, ATTN_W), lambda n: (n, 0)),
                  pl.BlockSpec((WINDOW, 2 * KV_W), lambda n: (jnp.maximum(n - 1, 0), 2)),
                  pl.BlockSpec((WINDOW, 2 * KV_W), lambda n: (n, 2))],
        out_specs=pl.BlockSpec((WINDOW, ATTN_W), lambda n: (n, 0)),
        out_shape=jax.ShapeDtypeStruct((T, ATTN_W), BF16),
        compiler_params=_cp(1),
    )(sinks, tab, qkv, qkv, qkv)


def _shift_copies(src_ref, dst_ref, n):
    for b in range(1, 8):
        dst_ref[b - 1] = src_ref[b:b + n, :]


def _tap(src_ref, sh_ref, s, c0):
    a, b = divmod(s, 8)
    start = pl.multiple_of(c0 + 8 * a, 8)
    if b == 0:
        return src_ref[pl.ds(start, CONV_ROWS), :]
    return sh_ref[b - 1, pl.ds(start, CONV_ROWS), :]


def _glu_rows(u, ch):
    return u[:, :ch] * _fast_sigmoid(u[:, ch:])


def _fill_z(zs_ref, zsh_ref, uc_ref, up_ref, i, ch, n):
    zs_ref[0:HALO] = jnp.where(i > 0, _glu_rows(up_ref[...], ch), 0.0)
    zs_ref[HALO:] = _glu_rows(uc_ref[...], ch)
    _shift_copies(zs_ref, zsh_ref, n - 8)


def _conv_fwd(u, w, b, lg, lb):
    T = u.shape[0]
    CH = u.shape[1] // 2
    tm = min(TM, T)
    n = tm + HALO
    hb = tm // HALO

    def body(uc_ref, up_ref, w_ref, b_ref, lg_ref, lb_ref, conv_ref, ypre_ref, zs_ref, zsh_ref):
        i = pl.program_id(0)
        _fill_z(zs_ref, zsh_ref, uc_ref, up_ref, i, CH, n)
        bias = b_ref[...]

        def chunk(ci, carry):
            c0 = pl.multiple_of(ci * CONV_ROWS, CONV_ROWS)
            acc = jnp.broadcast_to(bias, (CONV_ROWS, CH))
            for k in range(CONV_W):
                acc = acc + w_ref[k:k + 1, :] * _tap(zs_ref, zsh_ref, HALO - (CONV_W - 1) + k, c0)
            ypre_ref[pl.ds(c0, CONV_ROWS), :] = acc
            return carry

        lax.fori_loop(0, tm // CONV_ROWS, chunk, 0)
        y = ypre_ref[...]
        mu = jnp.mean(y, axis=-1, keepdims=True)
        d = y - mu
        var = jnp.mean(d * d, axis=-1, keepdims=True)
        o = d * lax.rsqrt(var + EPS) * lg_ref[...] + lb_ref[...]
        conv_ref[...] = (o * _fast_sigmoid(o)).astype(BF16)

    vec = pl.BlockSpec((1, CH), lambda i: (0, 0))
    return pl.pallas_call(
        body, name="conv_fwd", grid=(T // tm,),
        in_specs=[pl.BlockSpec((tm, 2 * CH), lambda i: (i, 0)),
                  pl.BlockSpec((HALO, 2 * CH), lambda i: (jnp.maximum(i * hb - 1, 0), 0)),
                  pl.BlockSpec((CONV_W, CH), lambda i: (0, 0)), vec, vec, vec],
        out_specs=[pl.BlockSpec((tm, CH), lambda i: (i, 0)), pl.BlockSpec((tm, CH), lambda i: (i, 0))],
        out_shape=[jax.ShapeDtypeStruct((T, CH), BF16), jax.ShapeDtypeStruct((T, CH), F32)],
        scratch_shapes=[pltpu.VMEM((n, CH), F32), pltpu.VMEM((7, n - 8, CH), F32)],
        compiler_params=_cp(1),
    )(u, u, w, b, lg, lb)


def _mixout_fwd(x, attn, conv, wo):
    T, D = x.shape
    tm = min(TM_MIX, T)
    A = attn.shape[1]

    def body(x_ref, a_ref, c_ref, w_ref, xo_ref):
        xo_ref[...] = x_ref[...] + _dot(a_ref[...], w_ref[:A, :]) + _dot(c_ref[...], w_ref[A:, :])

    return pl.pallas_call(
        body, name="mixout_fwd", grid=(T // tm,),
        in_specs=[pl.BlockSpec((tm, D), lambda i: (i, 0)), pl.BlockSpec((tm, A), lambda i: (i, 0)),
                  pl.BlockSpec((tm, conv.shape[1]), lambda i: (i, 0)), pl.BlockSpec(wo.shape, lambda i: (0, 0))],
        out_specs=pl.BlockSpec((tm, D), lambda i: (i, 0)),
        out_shape=jax.ShapeDtypeStruct((T, D), F32),
        compiler_params=_cp(1),
    )(x, attn, conv, wo)


def _rms_bwd_rows(dh, xf, g):
    xh, r = _rms(xf, None)
    dxn = dh * g
    dx = r * (dxn - xh * jnp.mean(dxn * xh, axis=-1, keepdims=True))
    return dx, jnp.sum(dh * xh, axis=0, keepdims=True), xh * g


def _loss_head(x, g, tgt):
    T, D = x.shape
    tm = min(TM, T)

    def body(x_ref, g_ref, t_ref, loss_ref, dx_ref, dg_ref):
        @pl.when(pl.program_id(0) == 0)
        def _():
            loss_ref[...] = jnp.zeros_like(loss_ref)
            dg_ref[...] = jnp.zeros_like(dg_ref)

        xf = x_ref[...]
        g = g_ref[...]
        xh, _ = _rms(xf, None)
        e = xh * g - t_ref[...]
        loss_ref[...] += 0.5 * jnp.sum(jnp.mean(e * e, axis=-1, keepdims=True), axis=0, keepdims=True)
        dx, dg, _ = _rms_bwd_rows(e * (1.0 / D), xf, g)
        dx_ref[...] = dx
        dg_ref[...] += dg

    return pl.pallas_call(
        body, name="loss_head", grid=(T // tm,),
        in_specs=[pl.BlockSpec((tm, D), lambda i: (i, 0)), pl.BlockSpec((1, D), lambda i: (0, 0)),
                  pl.BlockSpec((tm, D), lambda i: (i, 0))],
        out_specs=[pl.BlockSpec((1, 1), lambda i: (0, 0)), pl.BlockSpec((tm, D), lambda i: (i, 0)),
                   pl.BlockSpec((1, D), lambda i: (0, 0))],
        out_shape=[jax.ShapeDtypeStruct((1, 1), F32), jax.ShapeDtypeStruct((T, D), F32),
                   jax.ShapeDtypeStruct((1, D), F32)],
        compiler_params=_cp(1),
    )(x, g, tgt)


def _ffn_bwd(dxo, x, g, gu, win, wout, dep):
    T, D = x.shape
    FB = win.shape[2]
    tm = min(TM_FFN_BWD, T)

    def body(dxo_ref, x_ref, g_ref, gu_ref, win_hbm, wout_hbm, dep_ref,
             dxi_ref, dg_ref, hb_ref, dgu_ref, a_ref, dyb_ref, win_v, wout_v, sems):
        @pl.when(pl.program_id(0) == 0)
        def _():
            _load_ffn_weights(win_hbm, wout_hbm, win_v, wout_v, sems)
            dg_ref[...] = jnp.zeros_like(dg_ref)

        dyb = (0.5 * dxo_ref[...]).astype(BF16)
        dyb_ref[...] = dyb
        dh = jnp.zeros((tm, D), F32)
        for blk in range(2):
            for lo, sz in _lane_chunks(FB):
                cols = pl.ds(blk * FB + lo, sz)
                da = _dot_nt(dyb, wout_v[cols, :])
                gate = gu_ref[0, :, cols].astype(F32)
                up = gu_ref[1, :, cols].astype(F32)
                sg = _fast_sigmoid(gate)
                s = gate * sg
                a_ref[:, cols] = (s * up).astype(BF16)
                dgate = (da * up * (sg + s * (1.0 - sg))).astype(BF16)
                dup = (da * s).astype(BF16)
                dgu_ref[0, :, cols] = dgate
                dgu_ref[1, :, cols] = dup
                dh = dh + _dot_nt(dgate, win_v[blk, :, pl.ds(lo, sz)]) + _dot_nt(dup, win_v[2 + blk, :, pl.ds(lo, sz)])
        dx, dg, h = _rms_bwd_rows(dh, x_ref[...], g_ref[...])
        dxi_ref[...] = dxo_ref[...] + dx
        dg_ref[...] += dg
        hb_ref[...] = h.astype(BF16)

    row = pl.BlockSpec((tm, D), lambda i: (i, 0))
    act = pl.BlockSpec((2, tm, 2 * FB), lambda i: (0, i, 0))
    return pl.pallas_call(
        body, name="ffn_bwd", grid=(T // tm,),
        in_specs=[row, row, pl.BlockSpec((1, D), lambda i: (0, 0)), act, ANY, ANY, ANY],
        out_specs=[row, pl.BlockSpec((1, D), lambda i: (0, 0)), row, act,
                   pl.BlockSpec((tm, 2 * FB), lambda i: (i, 0)), row],
        out_shape=[jax.ShapeDtypeStruct((T, D), F32), jax.ShapeDtypeStruct((1, D), F32),
                   jax.ShapeDtypeStruct((T, D), BF16), jax.ShapeDtypeStruct((2, T, 2 * FB), BF16),
                   jax.ShapeDtypeStruct((T, 2 * FB), BF16), jax.ShapeDtypeStruct((T, D), BF16)],
        scratch_shapes=[pltpu.VMEM(win.shape, BF16), pltpu.VMEM(wout.shape, BF16), pltpu.SemaphoreType.DMA((6,))],
        compiler_params=_cp(1),
    )(dxo, x, g, gu, win, wout, dep)


def _mix_rms_bwd(dxo, x, g, dzs, wts):
    T, D = x.shape
    tm = min(TM, T)
    npair = len(dzs)

    def body(*refs):
        dxo_ref, x_ref, g_ref = refs[:3]
        dz_refs, w_refs = refs[3:3 + npair], refs[3 + npair:3 + 2 * npair]
        dxi_ref, dg_ref, hb_ref = refs[3 + 2 * npair:]

        @pl.when(pl.program_id(0) == 0)
        def _():
            dg_ref[...] = jnp.zeros_like(dg_ref)

        dh = jnp.zeros((tm, D), F32)
        for p in range(npair):
            dh = dh + _dot(dz_refs[p][...], w_refs[p][...])
        dx, dg, h = _rms_bwd_rows(dh, x_ref[...], g_ref[...])
        dxi_ref[...] = dxo_ref[...] + dx
        dg_ref[...] += dg
        hb_ref[...] = h.astype(BF16)

    row = pl.BlockSpec((tm, D), lambda i: (i, 0))
    return pl.pallas_call(
        body, name="mix_rms_bwd", grid=(T // tm,),
        in_specs=[row, row, pl.BlockSpec((1, D), lambda i: (0, 0))]
                 + [pl.BlockSpec((tm, dz.shape[1]), lambda i: (i, 0)) for dz in dzs]
                 + [pl.BlockSpec(w.shape, lambda i: (0, 0)) for w in wts],
        out_specs=[row, pl.BlockSpec((1, D), lambda i: (0, 0)), row],
        out_shape=[jax.ShapeDtypeStruct((T, D), F32), jax.ShapeDtypeStruct((1, D), F32),
                   jax.ShapeDtypeStruct((T, D), BF16)],
        compiler_params=_cp(1),
    )(dxo, x, g, *dzs, *wts)


def _wgrad(name, a, b, a_spec, b_spec, out_shape, out_spec, nblk, dep, acc_shape):
    T = a.shape[0]
    tk = min(TK_WGRAD, T)
    nk = T // tk

    def body(a_ref, b_ref, dep_ref, o_ref, acc_ref):
        k = pl.program_id(1)

        @pl.when(k == 0)
        def _():
            acc_ref[...] = jnp.zeros_like(acc_ref)

        acc_ref[...] += _dot_tn(a_ref[...], b_ref[...])

        @pl.when(k == nk - 1)
        def _():
            o_ref[...] = acc_ref[...].reshape(o_ref.shape).astype(BF16)

    return pl.pallas_call(
        body, name=name, grid=(nblk, nk), in_specs=[a_spec, b_spec, ANY], out_specs=out_spec,
        out_shape=jax.ShapeDtypeStruct(out_shape, BF16), scratch_shapes=[pltpu.VMEM(acc_shape, F32)],
        compiler_params=_cp(2),
    )(a, b, dep)


def _wgrad_ffn_in(hb, dgu, dep):
    T, D = hb.shape
    FB = dgu.shape[2] // 2
    tk = min(TK_WGRAD, T)
    return _wgrad("wgrad_ffn_in", hb, dgu,
                  pl.BlockSpec((tk, D), lambda b, k: (k, 0)),
                  pl.BlockSpec((None, tk, FB), lambda b, k: (b // 2, k, b % 2)),
                  (4, D, FB), pl.BlockSpec((None, D, FB), lambda b, k: (b, 0, 0)), 4, dep, (D, FB))


def _wgrad_ffn_out(a, dyb, dep):
    T, D = dyb.shape
    FB = a.shape[1] // 2
    tk = min(TK_WGRAD, T)
    return _wgrad("wgrad_ffn_out", a, dyb,
                  pl.BlockSpec((tk, FB), lambda b, k: (k, b)),
                  pl.BlockSpec((tk, D), lambda b, k: (k, 0)),
                  (4, FB // 2, D), pl.BlockSpec((2, FB // 2, D), lambda b, k: (b, 0, 0)), 2, dep, (FB, D))


def _wgrad_cat(a_list, b_list):
    T = a_list[0].shape[0]
    tk = min(TK_WGRAD, T)
    nk = T // tk
    na = len(a_list)
    M, N = sum(a.shape[1] for a in a_list), sum(b.shape[1] for b in b_list)

    def body(*refs):
        a_refs, b_refs, o_ref, acc_ref = refs[:na], refs[na:-2], refs[-2], refs[-1]
        k = pl.program_id(0)

        @pl.when(k == 0)
        def _():
            acc_ref[...] = jnp.zeros_like(acc_ref)

        r0 = 0
        for a_ref in a_refs:
            c0 = 0
            for b_ref in b_refs:
                m, n = a_ref.shape[1], b_ref.shape[1]
                acc_ref[r0:r0 + m, c0:c0 + n] += _dot_tn(a_ref[...], b_ref[...])
                c0 += n
            r0 += a_ref.shape[1]

        @pl.when(k == nk - 1)
        def _():
            o_ref[...] = acc_ref[...].astype(BF16)

    return pl.pallas_call(
        body, name="wgrad_cat", grid=(nk,),
        in_specs=[pl.BlockSpec((tk, v.shape[1]), lambda k: (k, 0)) for v in list(a_list) + list(b_list)],
        out_specs=pl.BlockSpec((M, N), lambda k: (0, 0)),
        out_shape=jax.ShapeDtypeStruct((M, N), BF16), scratch_shapes=[pltpu.VMEM((M, N), F32)],
        compiler_params=_cp(1),
    )(*a_list, *b_list)


def _mixout_bwd(dxo, wo):
    T, D = dxo.shape
    tm = min(TM_MIX, T)
    A = ATTN_W
    C = wo.shape[0] - A

    def body(dxo_ref, w_ref, dyb_ref, da_ref, dc_ref):
        dyb = dxo_ref[...].astype(BF16)
        dyb_ref[...] = dyb
        da_ref[...] = _dot_nt(dyb, w_ref[:A, :]).astype(BF16)
        dc_ref[...] = _dot_nt(dyb, w_ref[A:, :])

    return pl.pallas_call(
        body, name="mixout_bwd", grid=(T // tm,),
        in_specs=[pl.BlockSpec((tm, D), lambda i: (i, 0)), pl.BlockSpec(wo.shape, lambda i: (0, 0))],
        out_specs=[pl.BlockSpec((tm, D), lambda i: (i, 0)), pl.BlockSpec((tm, A), lambda i: (i, 0)),
                   pl.BlockSpec((tm, C), lambda i: (i, 0))],
        out_shape=[jax.ShapeDtypeStruct((T, D), BF16), jax.ShapeDtypeStruct((T, A), BF16),
                   jax.ShapeDtypeStruct((T, C), F32)],
        compiler_params=_cp(1),
    )(dxo, wo)


def _conv_bwd(dconv, ypre, u, w, lg, lb):
    T, CH = dconv.shape
    tm = min(TM, T)
    n = tm + HALO
    hb = tm // HALO
    nt = T // tm
    nchunk = tm // CONV_ROWS

    def body(dc_ref, dcn_ref, yp_ref, ypn_ref, uc_ref, up_ref, w_ref, lg_ref, lb_ref,
             du_ref, dw_ref, dvec_ref, zs_ref, zsh_ref, dy_ref, dysh_ref, dz_ref, dwacc_ref):
        i = pl.program_id(0)

        @pl.when(i == 0)
        def _():
            dwacc_ref[...] = jnp.zeros_like(dwacc_ref)
            dvec_ref[...] = jnp.zeros_like(dvec_ref)

        g, bb = lg_ref[...], lb_ref[...]

        def ln_bwd(dc, yp):
            mu = jnp.mean(yp, axis=-1, keepdims=True)
            d = yp - mu
            rs = lax.rsqrt(jnp.mean(d * d, axis=-1, keepdims=True) + EPS)
            yn = d * rs
            o = yn * g + bb
            sg = _fast_sigmoid(o)
            do = dc * (sg * (1.0 + o * (1.0 - sg)))
            dyn = do * g
            dyp = rs * (dyn - jnp.mean(dyn, axis=-1, keepdims=True)
                        - yn * jnp.mean(dyn * yn, axis=-1, keepdims=True))
            return dyp, do, yn

        dyp, do, yn = ln_bwd(dc_ref[...], yp_ref[...])
        dvec_ref[0:1, :] += jnp.sum(dyp, axis=0, keepdims=True)
        dvec_ref[1:2, :] += jnp.sum(do * yn, axis=0, keepdims=True)
        dvec_ref[2:3, :] += jnp.sum(do, axis=0, keepdims=True)
        dy_ref[0:tm] = dyp
        dyh, _, _ = ln_bwd(dcn_ref[...], ypn_ref[...])
        dy_ref[tm:] = jnp.where(i < nt - 1, dyh, 0.0)
        _shift_copies(dy_ref, dysh_ref, n - 8)
        _fill_z(zs_ref, zsh_ref, uc_ref, up_ref, i, CH, n)

        def chunk(ci, carry):
            c0 = pl.multiple_of(ci * CONV_ROWS, CONV_ROWS)
            acc = jnp.zeros((CONV_ROWS, CH), F32)
            for k in range(CONV_W):
                acc = acc + w_ref[k:k + 1, :] * _tap(dy_ref, dysh_ref, CONV_W - 1 - k, c0)
            dz_ref[pl.ds(c0, CONV_ROWS), :] = acc
            dyc = dy_ref[pl.ds(c0, CONV_ROWS), :]
            for k in range(CONV_W):
                prod = dyc * _tap(zs_ref, zsh_ref, HALO - (CONV_W - 1) + k, c0)
                dwacc_ref[k] += jnp.sum(prod.reshape(CONV_ROWS // 8, 8, CH), axis=0)
            return carry

        lax.fori_loop(0, nchunk, chunk, 0)

        @pl.when(i == nt - 1)
        def _():
            dw_ref[...] = jnp.sum(dwacc_ref[...], axis=1)

        uc = uc_ref[...]
        a = uc[:, :CH]
        sg = _fast_sigmoid(uc[:, CH:])
        dz = dz_ref[...]
        du_ref[:, :CH] = (dz * sg).astype(BF16)
        du_ref[:, CH:] = (dz * a * sg * (1.0 - sg)).astype(BF16)

    cur = lambda c: pl.BlockSpec((tm, c), lambda i: (i, 0))
    nxt = lambda c: pl.BlockSpec((HALO, c), lambda i: (jnp.minimum((i + 1) * hb, T // HALO - 1), 0))
    vec = pl.BlockSpec((1, CH), lambda i: (0, 0))
    return pl.pallas_call(
        body, name="conv_bwd", grid=(nt,),
        in_specs=[cur(CH), nxt(CH), cur(CH), nxt(CH), cur(2 * CH),
                  pl.BlockSpec((HALO, 2 * CH), lambda i: (jnp.maximum(i * hb - 1, 0), 0)),
                  pl.BlockSpec((CONV_W, CH), lambda i: (0, 0)), vec, vec],
        out_specs=[pl.BlockSpec((tm, 2 * CH), lambda i: (i, 0)), pl.BlockSpec((32, CH), lambda i: (0, 0)),
                   pl.BlockSpec((8, CH), lambda i: (0, 0))],
        out_shape=[jax.ShapeDtypeStruct((T, 2 * CH), BF16), jax.ShapeDtypeStruct((32, CH), F32),
                   jax.ShapeDtypeStruct((8, CH), F32)],
        scratch_shapes=[pltpu.VMEM((n, CH), F32), pltpu.VMEM((7, n - 8, CH), F32),
                        pltpu.VMEM((n, CH), F32), pltpu.VMEM((7, n - 8, CH), F32), pltpu.VMEM((tm, CH), F32),
                        pltpu.VMEM((32, 8, CH), F32)],
        compiler_params=_cp(1),
    )(dconv, dconv, ypre, ypre, u, u, w, lg, lb)


def _attn_bwd(sinks, tab, qkv, dattn):
    T = qkv.shape[0]
    nb = T // WINDOW

    def body(sink_ref, tab_ref, q_ref, kvp_ref, kvc_ref, do_ref, dq_ref, dkv_ref, dsk_ref, carry_ref):
        n = pl.program_id(0)

        @pl.when(n == 0)
        def _():
            dsk_ref[...] = jnp.zeros_like(dsk_ref)
            carry_ref[...] = jnp.zeros_like(carry_ref)

        @pl.when(n < nb)
        def _():
            seen = _first_block_mask(n)
            for g in range(N_KV):
                qs = _stack_heads(q_ref, g)
                dos = _stack_heads(do_ref, g)
                k = _band(kvp_ref, kvc_ref, g * HEAD_DIM)
                v = _band(kvp_ref, kvc_ref, KV_W + g * HEAD_DIM)
                p, ps = _attn_probs(qs, k, tab_ref[g], seen, _sink_col(sink_ref, g))
                dp = _dot_nt(dos, v)
                delta = jnp.sum(p * dp, axis=-1, keepdims=True)
                dsb = (p * (dp - delta)).astype(BF16)
                dsink = -ps * delta
                dqs = _dot(dsb, k) * SCALE
                dk = _dot_tn(dsb, qs) * SCALE
                dv = _dot_tn(p.astype(BF16), dos)
                for i in range(GROUP):
                    h = GROUP * g + i
                    dq_ref[:, h * HEAD_DIM:(h + 1) * HEAD_DIM] = dqs[i * WINDOW:(i + 1) * WINDOW].astype(BF16)
                    dsk_ref[h:h + 1, :] += jnp.sum(dsink[i * WINDOW:(i + 1) * WINDOW], axis=0, keepdims=True)
                for off, d in ((g * HEAD_DIM, dk), (KV_W + g * HEAD_DIM, dv)):
                    dkv_ref[:, off:off + HEAD_DIM] = (carry_ref[:, off:off + HEAD_DIM] + d[:WINDOW]).astype(BF16)
                    carry_ref[:, off:off + HEAD_DIM] = d[WINDOW:]

        @pl.when(n == nb)
        def _():
            dkv_ref[...] = carry_ref[...].astype(BF16)

    last = nb - 1
    return pl.pallas_call(
        body, name="attn_bwd", grid=(nb + 1,),
        in_specs=[pl.BlockSpec(memory_space=pltpu.SMEM),
                  pl.BlockSpec(tab.shape, lambda n: (0, 0, 0)),
                  pl.BlockSpec((WINDOW, ATTN_W), lambda n: (jnp.minimum(n, last), 0)),
                  pl.BlockSpec((WINDOW, 2 * KV_W), lambda n: (jnp.clip(n - 1, 0, last), 2)),
                  pl.BlockSpec((WINDOW, 2 * KV_W), lambda n: (jnp.minimum(n, last), 2)),
                  pl.BlockSpec((WINDOW, ATTN_W), lambda n: (jnp.minimum(n, last), 0))],
        out_specs=[pl.BlockSpec((WINDOW, ATTN_W), lambda n: (jnp.minimum(n, last), 0)),
                   pl.BlockSpec((WINDOW, 2 * KV_W), lambda n: (jnp.maximum(n - 1, 0), 0)),
                   pl.BlockSpec((8, LANES), lambda n: (0, 0))],
        out_shape=[jax.ShapeDtypeStruct((T, ATTN_W), BF16), jax.ShapeDtypeStruct((T, 2 * KV_W), BF16),
                   jax.ShapeDtypeStruct((8, LANES), F32)],
        scratch_shapes=[pltpu.VMEM((WINDOW, 2 * KV_W), F32)],
        compiler_params=_cp(1),
    )(sinks, tab, qkv, qkv, qkv, dattn)


def _pack(arrs):
    flat = jnp.concatenate([a.reshape(-1) for a in arrs])
    pad = -flat.shape[0] % (8 * LANES)
    return jnp.pad(flat, (0, pad)).reshape(1, -1, LANES)


def _unpack(packed, like):
    flat = packed.reshape(-1)
    out, off = [], 0
    for a in like:
        out.append(flat[off:off + a.size].reshape(a.shape))
        off += a.size
    return out


def kernel(x, norm_ffn1, w_ffn1_in, w_ffn1_out, norm_mix, w_in, sinks, w_dw, b_dw, conv_ln_g, conv_ln_b, w_out, norm_ffn2, w_ffn2_in, w_ffn2_out, final_norm, loss_target, m_norm_ffn1, m_w_ffn1_in, m_w_ffn1_out, m_norm_mix, m_w_in, m_sinks, m_w_dw, m_b_dw, m_conv_ln_g, m_conv_ln_b, m_w_out, m_norm_ffn2, m_w_ffn2_in, m_w_ffn2_out, m_final_norm, v_norm_ffn1, v_w_ffn1_in, v_w_ffn1_out, v_norm_mix, v_w_in, v_sinks, v_w_dw, v_b_dw, v_conv_ln_g, v_conv_ln_b, v_w_out, v_norm_ffn2, v_w_ffn2_in, v_w_ffn2_out, v_final_norm):
    L, D = norm_ffn1.shape
    T = x.shape[1]
    FB = w_ffn1_in.shape[2]
    CH = b_dw.shape[1]
    QKV = ATTN_W + 2 * KV_W
    xs = x.reshape(T, D)
    tgt = loss_target.reshape(T, D)
    cx, cy, cc = lax.axis_index("x"), lax.axis_index("y"), lax.axis_index("c")
    chip = 2 * cx + cy
    cidx = cc.reshape(1).astype(jnp.int32)
    tr = lambda a_: jnp.transpose(a_, (0, 2, 1))
    big_w = (w_ffn1_in, w_ffn1_out, tr(w_in), w_out, w_ffn2_in, w_ffn2_out)
    big_m = (m_w_ffn1_in, m_w_ffn1_out, tr(m_w_in), m_w_out, m_w_ffn2_in, m_w_ffn2_out)
    big_v = (v_w_ffn1_in, v_w_ffn1_out, tr(v_w_in), v_w_out, v_w_ffn2_in, v_w_ffn2_out)
    NW = len(big_w) + 1

    def own_slot(a, slots=4, idx=chip):
        return lax.dynamic_update_index_in_dim(lax.empty((slots,) + a.shape, a.dtype), a, idx, 0)

    def shards(l, tok):
        return [own_slot((w_[l] + tok[0, 0]).astype(BF16)) for w_ in big_w] + [own_slot(w_dw[l] + tok[0, 0])]

    def gather_start(lands, tok):
        return _xchg_start("gather_start", [], lands, _gather_plan, tok)

    def gather_arrived(started, after, n, taps):
        _, lands, tok = _xchg_wait("gather_wait", started, 0, n, _gather_plan, after)
        return _xchg_start("gshare_start", [], lands[:-1] if taps else lands, _gshare_plan, tok, "sibling3"), lands[-1]

    def shared_weights(shared, after, n):
        _, mats, tok = _xchg_wait("gshare_wait", shared, 0, n, _gshare_plan, after, "sibling3")
        return mats, tok

    row = lambda a, l: a[l].reshape(1, -1)
    tab = _attn_bias_table()
    NB = len(big_w)

    saved, W = [], []
    zero_tok = jnp.zeros((8, LANES), F32)
    src0 = shards(0, zero_tok)
    started = gather_start(src0[:2], zero_tok)
    rest0 = gather_start(src0[2:], started[-1])
    cast = [None] + [shards(l, rest0[-1]) for l in range(1, L)]
    shared, _ = gather_arrived(started, [xs] + [a_ for c_ in cast[1:] for a_ in c_], 2, False)
    after = [shared[-1]]
    for l in range(L):
        mats, tok = shared_weights(shared, after, 2 if l == 0 else NB)
        started = None
        if l + 1 < L:
            started = gather_start(cast[l + 1], tok)
            tok = started[-1]
        x0 = xs
        x1, gu1 = _ffn_fwd(x0, row(norm_ffn1, l) + tok[0, 0], mats[0], mats[1].reshape(2 * FB, D))
        gm_row = row(norm_mix, l)
        if l == 0:
            shared, gdw = gather_arrived(rest0, [x1], NW - 2, True)
            rest, tok = shared_weights(shared, [shared[-1]], NB - 2)
            mats = list(mats) + list(rest)
            gm_row = gm_row + tok[0, 0]
        g1i, g1o, gi, go, g2i, g2o = mats
        w = dict(f1i=g1i, f1o=g1o.reshape(2 * FB, D), f2i=g2i, f2o=g2o.reshape(2 * FB, D),
                 wit=gi.reshape(-1, D), wo=go.reshape(-1, D),
                 wdw=jnp.transpose(gdw, (1, 0, 2)).reshape(CONV_W, CH))
        W.append(w)
        qkv, u = _mixproj_fwd(x1, gm_row, w["wit"])
        attn = _attn_fwd(row(sinks, l), tab, qkv)
        conv, ypre = _conv_fwd(u, w["wdw"], row(b_dw, l), row(conv_ln_g, l), row(conv_ln_b, l))
        x2 = _mixout_fwd(x1, attn, conv, w["wo"])
        g2_row = row(norm_ffn2, l)
        if started is not None and l > 0:
            shared, gdw = gather_arrived(started, [x2], NW, True)
            g2_row = g2_row + shared[-1][0, 0]
        xs, gu2 = _ffn_fwd(x2, g2_row, w["f2i"], w["f2o"])
        if started is not None and l == 0:
            shared, gdw = gather_arrived(started, [xs], NW, True)
        saved.append((x0, gu1, x1, qkv, u, attn, conv, ypre, x2, gu2))
        after = [xs]

    loss_part, dx, d_final = _loss_head(xs, final_norm.reshape(1, D), tgt)
    loss = lax.psum(loss_part[0, 0], ("x", "y", "c"))

    bufs = [[lax.empty(w_.shape, F32) for _ in range(4)] for w_ in big_w]
    d_n1, d_nm, d_n2 = [None] * L, [None] * L, [None] * L
    d_sk, d_bdw, d_lg, d_lb, d_wdw = [None] * L, [None] * L, [None] * L, [None] * L, [None] * L

    def sib_start(gs):
        return _xchg_start("sib_start", gs, [lax.empty((4, g.shape[1] // 2, g.shape[2]), g.dtype) for g in gs],
                           _sib_plan, zero_tok, "sibling")

    def reduce_start(sib_started, after, n):
        gs, sibs, _ = _xchg_wait("sib_wait", sib_started, n, n, _sib_plan, after, "sibling")
        parts = [_sum_halves(cidx, g, s_) for g, s_ in zip(gs, sibs)]
        lands = [own_slot(lax.dynamic_index_in_dim(p, chip, 0, keepdims=False)) for p in parts]
        return _xchg_start("rs_start", parts, lands, _rs_plan, zero_tok)

    def share_start(rs_started, after, n):
        _, qs, tok = _xchg_wait("rs_wait", rs_started, n, n, _rs_plan, after)
        return _xchg_start("qshare_start", qs, [lax.empty(q.shape, q.dtype) for q in qs], _whole_plan, tok, "sibling")

    def finish(l, shared, after, idxs):
        q_own, q_sib, _ = _xchg_wait("qshare_wait", shared, len(idxs), len(idxs), _whole_plan, after, "sibling")
        for k, t in enumerate(idxs):
            bufs[t] = _adamw_layer(cidx, q_own[k], q_sib[k], big_w[t], big_m[t], big_v[t], bufs[t], l)

    ALL = list(range(NB))
    EARLY, LATE = ALL[2:], ALL[:2]
    sib_pending = rs_pending = None
    shares = []
    tok = zero_tok
    for l in reversed(range(L)):
        w = W[l]
        x0, gu1, x1, qkv, u, attn, conv, ypre, x2, gu2 = saved[l]
        dx, d_n2[l], hb, dgu, a, dyb = _ffn_bwd(dx, x2, row(norm_ffn2, l), gu2, w["f2i"], w["f2o"], tok)
        g_f2i, g_f2o = _wgrad_ffn_in(hb, dgu, tok), _wgrad_ffn_out(a, dyb, tok)
        lg_row = row(conv_ln_g, l)
        if sib_pending is not None:
            rs_started = reduce_start(sib_pending[1], [g_f2o], NB)
            if rs_pending is not None:
                shares.append((rs_pending[0], share_start(rs_pending[1], [rs_started[-1]], NB)))
                lg_row = lg_row + shares[-1][1][-1][0, 0]
            rs_pending = (sib_pending[0], rs_started)
            lg_row = lg_row + rs_started[-1][0, 0]
        dyb, dattn, dconv = _mixout_bwd(dx, w["wo"])
        g_wo = _wgrad_cat([attn, conv], [dyb]).reshape(4, -1, D)
        du, dwdw, dvec = _conv_bwd(dconv, ypre, u, w["wdw"], lg_row, row(conv_ln_b, l))
        d_wdw[l], d_bdw[l], d_lg[l], d_lb[l] = dwdw[:CONV_W], dvec[0], dvec[1], dvec[2]
        dq, dkv, dsk = _attn_bwd(row(sinks, l), tab, qkv, dattn)
        d_sk[l] = dsk[:, 0]
        wit = w["wit"]
        dx, d_nm[l], hb = _mix_rms_bwd(dx, x1, row(norm_mix, l), [dq, dkv, du],
                                       [wit[:ATTN_W], wit[ATTN_W:QKV], wit[QKV:]])
        g_wi = _wgrad_cat([dq, dkv, du], [hb]).reshape(4, -1, D)
        if l == 0:
            sib_early = sib_start([g_wi, g_wo, g_f2i, g_f2o])
            tok = sib_early[-1]
        dx, d_n1[l], hb, dgu, a, dyb = _ffn_bwd(dx, x0, row(norm_ffn1, l), gu1, w["f1i"], w["f1o"], tok)
        if l == 0:
            rs_early = reduce_start(sib_early, [dx], len(EARLY))
            tok = rs_early[-1]
        g_f1i, g_f1o = _wgrad_ffn_in(hb, dgu, tok), _wgrad_ffn_out(a, dyb, tok)
        sib_started = sib_start([g_f1i, g_f1o] if l == 0 else [g_f1i, g_f1o, g_wi, g_wo, g_f2i, g_f2o])
        tok = sib_started[-1]
        sib_pending = (l, sib_started)
    grad_x = dx.reshape(x.shape)

    small_g = [jnp.concatenate(d, axis=0) for d in (d_n1, d_nm, d_n2)] + [d_final, jnp.stack(d_sk)] + \
              [jnp.stack(d) for d in (d_bdw, d_lg, d_lb, d_wdw)]
    packed = _pack(small_g)[0]
    small_started = _xchg_start("small_start", [packed], [own_slot(packed, 8, 4 * cx + 2 * cy + cc)], _slot_plan, tok, "all")

    after = [small_started[-1]]
    if rs_pending is not None:
        shares.append((rs_pending[0], share_start(rs_pending[1], after, NB)))
        after = [shares[-1][1][-1]]
    if shares:
        finish(*shares.pop(0), after, ALL)
        after = [b_[0] for b_ in bufs]
    rs_late = reduce_start(sib_pending[1], after, len(LATE))
    sh_early = share_start(rs_early, [rs_late[-1]], len(EARLY))
    after = [sh_early[-1]]
    sh_late = None
    for l, sh in shares:
        finish(l, sh, after, ALL)
        after = [b_[0] for b_ in bufs]
        if sh_late is None:
            sh_late = share_start(rs_late, after, len(LATE))
            after = [sh_late[-1]]
    if sh_late is None:
        sh_late = share_start(rs_late, after, len(LATE))
        after = [sh_late[-1]]
    _, (slots,), _ = _xchg_wait("small_wait", small_started, 1, 1, _slot_plan, after, "all")
    small_sum = _unpack(_sum_slots(slots), small_g)
    g_wdw = lax.dynamic_slice_in_dim(small_sum[8], chip * w_dw.shape[2], w_dw.shape[2], axis=2)
    small_g = [small_sum[0], small_sum[1], small_sum[2], small_sum[3].reshape(D), small_sum[4],
               small_sum[5], small_sum[6], small_sum[7], g_wdw]
    small_w = (norm_ffn1, norm_mix, norm_ffn2, final_norm, sinks, b_dw, conv_ln_g, conv_ln_b, w_dw)
    small_m = (m_norm_ffn1, m_norm_mix, m_norm_ffn2, m_final_norm, m_sinks, m_b_dw, m_conv_ln_g, m_conv_ln_b, m_w_dw)
    small_v = (v_norm_ffn1, v_norm_mix, v_norm_ffn2, v_final_norm, v_sinks, v_b_dw, v_conv_ln_g, v_conv_ln_b, v_w_dw)
    upd = _adamw(_pack(small_g), _pack(small_w), _pack(small_m), _pack(small_v))
    small_upd = [_unpack(u_, small_w) for u_ in upd]
    finish(0, sh_early, [upd[0]], EARLY)
    finish(0, sh_late, [bufs[t][0] for t in EARLY], LATE)

    order = ("norm_ffn1", "w_ffn1_in", "w_ffn1_out", "norm_mix", "w_in", "sinks", "w_dw", "b_dw", "conv_ln_g",
             "conv_ln_b", "w_out", "norm_ffn2", "w_ffn2_in", "w_ffn2_out", "final_norm")
    small_names = ("norm_ffn1", "norm_mix", "norm_ffn2", "final_norm", "sinks", "b_dw", "conv_ln_g", "conv_ln_b", "w_dw")
    big_names = ("w_ffn1_in", "w_ffn1_out", "w_in", "w_out", "w_ffn2_in", "w_ffn2_out")
    grads, deltas, new_m, new_v = {}, {}, {}, {}
    for i, nme in enumerate(small_names):
        grads[nme], deltas[nme], new_m[nme], new_v[nme] = small_g[i], small_upd[0][i], small_upd[1][i], small_upd[2][i]
    for i, nme in enumerate(big_names):
        grads[nme], deltas[nme], new_m[nme], new_v[nme] = [tr(b_) for b_ in bufs[i]] if nme == "w_in" else bufs[i]
    return (loss, grad_x, *[grads[n] for n in order], *[deltas[n] for n in order],
            *[new_m[n] for n in order], *[new_v[n] for n in order])
```

```python
import functools

import jax
import jax.numpy as jnp
from jax import lax
from jax.experimental import pallas as pl
from jax.experimental.pallas import tpu as pltpu

F32, BF16 = jnp.float32, jnp.bfloat16
EPS = 1e-6
NEG_INF = -1e30
HEAD_DIM = 64
N_HEADS = 8
N_KV = 2
GROUP = N_HEADS // N_KV
WINDOW = 128
ATTN_W = N_HEADS * HEAD_DIM
KV_W = N_KV * HEAD_DIM
CONV_W = 31
HALO = 32
CONV_ROWS = 32
SCALE = 1.0 / 8.0
ADAM_LR, ADAM_B1, ADAM_B2, ADAM_EPS, ADAM_WD, ADAM_STEP = 0.001, 0.9, 0.999, 1e-08, 0.01, 10
TM = 512
TM_FFN_BWD = 256
TK_WGRAD = 2048
TM_MIX = 1024
LANES = 128
VMEM_LIMIT = 52 * 1024 * 1024
MESH = pl.DeviceIdType.MESH
ANY = pl.BlockSpec(memory_space=pl.ANY)
HBM = pl.BlockSpec(memory_space=pltpu.HBM)
SEM = pl.BlockSpec(memory_space=pltpu.SEMAPHORE)
VMEM = pl.BlockSpec(memory_space=pltpu.VMEM)
EFFECT = pltpu.SideEffectType.DATAFLOW_SIDE_EFFECTING
TOKEN = jax.ShapeDtypeStruct((8, LANES), F32)


def _cp(n):
    return pltpu.CompilerParams(dimension_semantics=("arbitrary",) * n, vmem_limit_bytes=VMEM_LIMIT)


def _dot(a, b):
    return jnp.dot(a, b, preferred_element_type=F32)


def _dot_nt(a, b):
    return lax.dot_general(a, b, (((1,), (1,)), ((), ())), preferred_element_type=F32)


def _dot_tn(a, b):
    return lax.dot_general(a, b, (((0,), (0,)), ((), ())), preferred_element_type=F32)


def _place():
    x, y, c = lax.axis_index("x"), lax.axis_index("y"), lax.axis_index("c")
    chips = [(1 - x, y), (x, 1 - y), (1 - x, 1 - y)]
    return x, y, c, chips


def _rcopy(src, dst, send_sems, recv_sems, k, dev):
    return pltpu.make_async_remote_copy(src_ref=src, dst_ref=dst, send_sem=send_sems.at[k],
                                        recv_sem=recv_sems.at[k], device_id=dev, device_id_type=MESH)


def _hbm(a):
    return pltpu.with_memory_space_constraint(a, pltpu.HBM)


PEERS = {"chips": 3, "sibling": 1, "sibling3": 3, "all": 7}


def _targets(mode):
    x, y, c, chips = _place()
    b = 2 * x + y
    if mode == "chips":
        return b, c, [((px, py, c), 2 * px + py) for px, py in chips]
    if mode == "sibling":
        return b, c, [((x, y, 1 - c), b)]
    if mode == "sibling3":
        return b, c, [((x, y, 1 - c), 2 * px + py) for px, py in chips]
    flip = lambda v, f: 1 - v if f else v
    devs = [(flip(x, k >> 2 & 1), flip(y, k >> 1 & 1), flip(c, k & 1)) for k in range(1, 8)]
    return 4 * x + 2 * y + c, c, [(d, 4 * d[0] + 2 * d[1] + d[2]) for d in devs]


def _xchg_start(name, srcs, lands, plan, dep, mode="chips"):
    ns, nl, npeer = len(srcs), len(lands), PEERS[mode]

    def body(*refs):
        land = refs[ns:ns + nl]
        src = refs[:ns] if ns else land
        send_sems, recv_sems, token = refs[ns + nl + 1], refs[ns + nl + 2], refs[-1]
        me, c, peers = _targets(mode)
        for t in range(nl):
            for j, (dev, tag) in enumerate(peers):
                s, d, _ = plan(src[t], land[t], t, me, c, tag)
                _rcopy(s, d, send_sems, recv_sems, npeer * t + j, dev).start()
        token[...] = jnp.zeros_like(token)

    arrs = list(srcs) + list(lands)
    return pl.pallas_call(
        body, name=name,
        out_shape=(pltpu.SemaphoreType.DMA((npeer * nl,)), pltpu.SemaphoreType.DMA((npeer * nl,)),
                   *[pltpu.HBM(a.shape, a.dtype) for a in arrs], TOKEN),
        in_specs=[HBM] * (ns + nl) + [ANY], out_specs=(SEM, SEM, *[HBM] * (ns + nl), VMEM),
        input_output_aliases={i: 2 + i for i in range(ns + nl)},
        compiler_params=pltpu.CompilerParams(has_side_effects=EFFECT),
    )(*[_hbm(a) for a in arrs], dep)


def _xchg_wait(name, started, ns, nl, plan, after, mode="chips"):
    send_sems, recv_sems, thru = started[0], started[1], started[2:2 + ns + nl]
    npeer = PEERS[mode]

    def body(*refs):
        land = refs[ns:ns + nl]
        src = refs[:ns] if ns else land
        send_sems, recv_sems, token = refs[ns + nl], refs[ns + nl + 1], refs[-1]
        me, c, peers = _targets(mode)
        for t in range(nl):
            for j, (dev, tag) in enumerate(peers):
                s, _, a = plan(src[t], land[t], t, me, c, tag)
                cp = _rcopy(s, a, send_sems, recv_sems, npeer * t + j, dev)
                cp.wait_send()
                cp.wait_recv()
        token[...] = jnp.zeros_like(token)

    out = pl.pallas_call(
        body, name=name,
        out_shape=(*[pltpu.HBM(a.shape, a.dtype) for a in thru], TOKEN),
        in_specs=[HBM] * (ns + nl) + [SEM, SEM] + [ANY] * len(after), out_specs=(*[HBM] * (ns + nl), VMEM),
        input_output_aliases={i: i for i in range(ns + nl)},
        compiler_params=pltpu.CompilerParams(has_side_effects=EFFECT),
    )(*thru, send_sems, recv_sems, *after)
    return out[:ns], out[ns:ns + nl], out[-1]


def _half(ref_rows, which):
    h = ref_rows // 2
    return pl.ds(which * h, h)


def _gather_plan(src, land, t, b, c, pb):
    if land.shape[1] % 2 == 0:
        hs = _half(land.shape[1], c)
        return land.at[b, hs], land.at[b, hs], land.at[pb, hs]
    return land.at[b], land.at[b], land.at[pb]


def _gshare_plan(src, land, t, b, c, pb):
    return land.at[pb, _half(land.shape[1], c)], land.at[pb, _half(land.shape[1], c)], land.at[pb, _half(land.shape[1], 1 - c)]


def _rs_plan(src, land, t, me, c, tag):
    h = src.shape[1] // 2
    return src.at[tag // 2, pl.ds((tag % 2) * h, h), :], land.at[me], land.at[tag]


def _rows_block(h, cap=512):
    for rb in range(min(h, cap) // 16 * 16, 0, -16):
        if h % rb == 0:
            return rb
    return h


def _whole_plan(src, land, t, me, c, tag):
    return src, land, land


def _slot_plan(src, land, t, me, c, tag):
    return src, land.at[me], land.at[tag]


def _adam_update(gg, w, m, v):
    m2 = ADAM_B1 * m + (1.0 - ADAM_B1) * gg
    v2 = ADAM_B2 * v + (1.0 - ADAM_B2) * (gg * gg)
    mh = m2 / (1.0 - ADAM_B1 ** ADAM_STEP)
    vh = v2 / (1.0 - ADAM_B2 ** ADAM_STEP)
    return -ADAM_LR * (mh / (jnp.sqrt(vh) + ADAM_EPS) + ADAM_WD * w), m2, v2


def _adamw_layer(cidx, q_own, q_sib, w, m, v, bufs, l):
    L, R, C = w.shape
    h = R // 2
    rb = _rows_block(h, 256)
    nr = h // rb

    def body(c_ref, qo_ref, qs_ref, w_ref, m_ref, v_ref, *rest):
        g_ref, d_ref, mo_ref, vo_ref = rest[-4:]
        own = pl.program_id(0) == c_ref[0]
        gg = jnp.zeros((rb, C), F32)
        for s in range(8):
            gg = gg + jnp.where(own, qo_ref[s], qs_ref[s]).astype(F32)
        g_ref[...] = gg
        d_ref[...], mo_ref[...], vo_ref[...] = _adam_update(gg, w_ref[...], m_ref[...], v_ref[...])

    q_own_spec = pl.BlockSpec((8, rb, C), lambda hh, i, c: (0, jnp.where(hh == c[0], i, 0), 0))
    q_sib_spec = pl.BlockSpec((8, rb, C), lambda hh, i, c: (0, jnp.where(hh == c[0], 0, i), 0))
    wspec = pl.BlockSpec((None, rb, C), lambda hh, i, c: (l, hh * nr + i, 0))
    return pl.pallas_call(
        body, name="adamw_layer", out_shape=[jax.ShapeDtypeStruct(w.shape, F32)] * 4,
        grid_spec=pltpu.PrefetchScalarGridSpec(
            num_scalar_prefetch=1, grid=(2, nr),
            in_specs=[q_own_spec, q_sib_spec, wspec, wspec, wspec] + [ANY] * 4, out_specs=[wspec] * 4),
        input_output_aliases={6 + k: k for k in range(4)},
        compiler_params=_cp(2),
    )(cidx, q_own, q_sib, w, m, v, *bufs)


def _adamw(g, w, m, v):
    L, R, C = g.shape
    rb = _rows_block(R)

    def body(g_ref, w_ref, m_ref, v_ref, d_ref, mo_ref, vo_ref):
        d_ref[...], mo_ref[...], vo_ref[...] = _adam_update(g_ref[...], w_ref[...], m_ref[...], v_ref[...])

    spec = pl.BlockSpec((None, rb, C), lambda l, i: (l, i, 0))
    return pl.pallas_call(
        body, name="adamw", grid=(L, R // rb), in_specs=[spec] * 4, out_specs=[spec] * 3,
        out_shape=[jax.ShapeDtypeStruct(g.shape, F32)] * 3, compiler_params=_cp(2),
    )(g, w, m, v)


def _sum_slots(buf):
    def body(b_ref, o_ref):
        acc = b_ref[0]
        for k in range(1, 8):
            acc = acc + b_ref[k]
        o_ref[...] = acc

    return pl.pallas_call(body, name="sum_slots", in_specs=[VMEM], out_specs=VMEM,
                          out_shape=jax.ShapeDtypeStruct(buf.shape[1:], F32))(buf)


def _rms(xf, g):
    r = lax.rsqrt(jnp.mean(xf * xf, axis=-1, keepdims=True) + EPS)
    return xf * r, r


def _lane_chunks(n):
    lo = (n // LANES + 1) // 2 * LANES
    return ((0, lo), (lo, n - lo))


def _load_ffn_weights(win_hbm, wout_hbm, win_v, wout_v, sems):
    fb = win_v.shape[2]
    loads = [pltpu.make_async_copy(win_hbm.at[k], win_v.at[k], sems.at[k]) for k in range(4)]
    loads += [pltpu.make_async_copy(wout_hbm.at[pl.ds(k * fb, fb)], wout_v.at[pl.ds(k * fb, fb)], sems.at[4 + k])
              for k in range(2)]
    for cp in loads:
        cp.start()
    for cp in loads:
        cp.wait()


def _fast_sigmoid(v):
    return pl.reciprocal(1.0 + jnp.exp(-v), approx=True)


def _ffn_fwd(x, g, win, wout):
    T, D = x.shape
    FB = win.shape[2]
    tm = min(TM, T)

    def body(x_ref, g_ref, win_hbm, wout_hbm, xo_ref, gu_ref, win_v, wout_v, sems):
        @pl.when(pl.program_id(0) == 0)
        def _():
            _load_ffn_weights(win_hbm, wout_hbm, win_v, wout_v, sems)

        xf = x_ref[...]
        xh, _ = _rms(xf, None)
        h = (xh * g_ref[...]).astype(BF16)
        acc = jnp.zeros((tm, D), F32)
        for blk in range(2):
            for lo, sz in _lane_chunks(FB):
                cols = pl.ds(blk * FB + lo, sz)
                gate = _dot(h, win_v[blk, :, pl.ds(lo, sz)])
                up = _dot(h, win_v[2 + blk, :, pl.ds(lo, sz)])
                gu_ref[0, :, cols] = gate.astype(BF16)
                gu_ref[1, :, cols] = up.astype(BF16)
                a = (gate * _fast_sigmoid(gate) * up).astype(BF16)
                acc = acc + _dot(a, wout_v[cols, :])
        xo_ref[...] = xf + 0.5 * acc

    row = pl.BlockSpec((tm, D), lambda i: (i, 0))
    return pl.pallas_call(
        body, name="ffn_fwd", grid=(T // tm,),
        in_specs=[row, pl.BlockSpec((1, D), lambda i: (0, 0)), ANY, ANY],
        out_specs=[row, pl.BlockSpec((2, tm, 2 * FB), lambda i: (0, i, 0))],
        out_shape=[jax.ShapeDtypeStruct((T, D), F32), jax.ShapeDtypeStruct((2, T, 2 * FB), BF16)],
        scratch_shapes=[pltpu.VMEM(win.shape, BF16), pltpu.VMEM(wout.shape, BF16), pltpu.SemaphoreType.DMA((6,))],
        compiler_params=_cp(1),
    )(x, g, win, wout)


def _mixproj_fwd(x, g, wt):
    T, D = x.shape
    W = wt.shape[0]
    QKV = ATTN_W + 2 * KV_W
    tm = min(TM_MIX, T)

    def body(x_ref, g_ref, w_ref, qkv_ref, u_ref):
        xh, _ = _rms(x_ref[...], None)
        h = (xh * g_ref[...]).astype(BF16)
        qkv_ref[...] = _dot_nt(h, w_ref[:QKV, :]).astype(BF16)
        u_ref[...] = _dot_nt(h, w_ref[QKV:, :])

    return pl.pallas_call(
        body, name="mixproj_fwd", grid=(T // tm,),
        in_specs=[pl.BlockSpec((tm, D), lambda i: (i, 0)), pl.BlockSpec((1, D), lambda i: (0, 0)),
                  pl.BlockSpec((W, D), lambda i: (0, 0))],
        out_specs=[pl.BlockSpec((tm, QKV), lambda i: (i, 0)), pl.BlockSpec((tm, W - QKV), lambda i: (i, 0))],
        out_shape=[jax.ShapeDtypeStruct((T, QKV), BF16), jax.ShapeDtypeStruct((T, W - QKV), F32)],
        compiler_params=_cp(1),
    )(x, g, wt)


def _attn_bias_table():
    rows, cols = GROUP * WINDOW, 2 * WINDOW
    row = lax.broadcasted_iota(jnp.int32, (N_KV, rows, cols), 1)
    col = lax.broadcasted_iota(jnp.int32, (N_KV, rows, cols), 2)
    head = GROUP * lax.broadcasted_iota(jnp.int32, (N_KV, rows, cols), 0) + (row >> 7)
    dist = (row & (WINDOW - 1)) + WINDOW - col
    slope = jnp.exp2(-(head + 1).astype(F32))
    return jnp.where((dist >= 0) & (dist < WINDOW), -slope * dist.astype(F32), NEG_INF)


def _first_block_mask(n):
    col = lax.broadcasted_iota(jnp.int32, (GROUP * WINDOW, 2 * WINDOW), 1)
    return (n > 0) | (col >= WINDOW)


def _sink_col(sink_ref, g):
    hi = lax.broadcasted_iota(jnp.int32, (GROUP * WINDOW, 1), 0) >> 7
    col = jnp.zeros((GROUP * WINDOW, 1), F32)
    for i in range(GROUP):
        col = jnp.where(hi == i, sink_ref[0, GROUP * g + i], col)
    return col


def _stack_heads(ref, g):
    return jnp.concatenate([ref[:, (GROUP * g + i) * HEAD_DIM:(GROUP * g + i + 1) * HEAD_DIM]
                            for i in range(GROUP)], axis=0)


def _band(kvp_ref, kvc_ref, off):
    return jnp.concatenate([kvp_ref[:, off:off + HEAD_DIM], kvc_ref[:, off:off + HEAD_DIM]], axis=0)


def _attn_probs(qs, k, bias, seen, sink):
    s = jnp.where(seen, _dot_nt(qs, k) * SCALE + bias, NEG_INF)
    m = jnp.maximum(jnp.max(s, axis=-1, keepdims=True), sink)
    p = jnp.exp(s - m)
    es = jnp.exp(sink - m)
    inv = 1.0 / (jnp.sum(p, axis=-1, keepdims=True) + es)
    return p * inv, es * inv


def _attn_fwd(sinks, tab, qkv):
    T = qkv.shape[0]
    nb = T // WINDOW

    def body(sink_ref, tab_ref, q_ref, kvp_ref, kvc_ref, o_ref):
        seen = _first_block_mask(pl.program_id(0))
        for g in range(N_KV):
            qs = _stack_heads(q_ref, g)
            k = _band(kvp_ref, kvc_ref, g * HEAD_DIM)
            v = _band(kvp_ref, kvc_ref, KV_W + g * HEAD_DIM)
            p, _ = _attn_probs(qs, k, tab_ref[g], seen, _sink_col(sink_ref, g))
            o = _dot(p.astype(BF16), v)
            for i in range(GROUP):
                h = GROUP * g + i
                o_ref[:, h * HEAD_DIM:(h + 1) * HEAD_DIM] = o[i * WINDOW:(i + 1) * WINDOW].astype(BF16)

    return pl.pallas_call(
        body, name="attn_fwd", grid=(nb,),
        in_specs=[pl.BlockSpec(memory_space=pltpu.SMEM),
                  pl.BlockSpec(tab.shape, lambda n: (0, 0, 0)),
                  pl.BlockSpec((WINDOW, ATTN_W), lambda n: (n, 0)),
                  pl.BlockSpec((WINDOW, 2 * KV_W), lambda n: (jnp.maximum(n - 1, 0), 2)),
                  pl.BlockSpec((WINDOW, 2 * KV_W), lambda n: (n, 2))],
        out_specs=pl.BlockSpec((WINDOW, ATTN_W), lambda n: (n, 0)),
        out_shape=jax.ShapeDtypeStruct((T, ATTN_W), BF16),
        compiler_params=_cp(1),
    )(sinks, tab, qkv, qkv, qkv)


def _shift_copies(src_ref, dst_ref, n):
    for b in range(1, 8):
        dst_ref[b - 1] = src_ref[b:b + n, :]


def _tap(src_ref, sh_ref, s, c0):
    a, b = divmod(s, 8)
    start = pl.multiple_of(c0 + 8 * a, 8)
    if b == 0:
        return src_ref[pl.ds(start, CONV_ROWS), :]
    return sh_ref[b - 1, pl.ds(start, CONV_ROWS), :]


def _glu_rows(u, ch):
    return u[:, :ch] * _fast_sigmoid(u[:, ch:])


def _fill_z(zs_ref, zsh_ref, uc_ref, up_ref, i, ch, n):
    zs_ref[0:HALO] = jnp.where(i > 0, _glu_rows(up_ref[...], ch), 0.0)
    zs_ref[HALO:] = _glu_rows(uc_ref[...], ch)
    _shift_copies(zs_ref, zsh_ref, n - 8)


def _conv_fwd(u, w, b, lg, lb):
    T = u.shape[0]
    CH = u.shape[1] // 2
    tm = min(TM, T)
    n = tm + HALO
    hb = tm // HALO

    def body(uc_ref, up_ref, w_ref, b_ref, lg_ref, lb_ref, conv_ref, ypre_ref, zs_ref, zsh_ref):
        i = pl.program_id(0)
        _fill_z(zs_ref, zsh_ref, uc_ref, up_ref, i, CH, n)
        bias = b_ref[...]

        def chunk(ci, carry):
            c0 = pl.multiple_of(ci * CONV_ROWS, CONV_ROWS)
            acc = jnp.broadcast_to(bias, (CONV_ROWS, CH))
            for k in range(CONV_W):
                acc = acc + w_ref[k:k + 1, :] * _tap(zs_ref, zsh_ref, HALO - (CONV_W - 1) + k, c0)
            ypre_ref[pl.ds(c0, CONV_ROWS), :] = acc
            return carry

        lax.fori_loop(0, tm // CONV_ROWS, chunk, 0)
        y = ypre_ref[...]
        mu = jnp.mean(y, axis=-1, keepdims=True)
        d = y - mu
        var = jnp.mean(d * d, axis=-1, keepdims=True)
        o = d * lax.rsqrt(var + EPS) * lg_ref[...] + lb_ref[...]
        conv_ref[...] = (o * _fast_sigmoid(o)).astype(BF16)

    vec = pl.BlockSpec((1, CH), lambda i: (0, 0))
    return pl.pallas_call(
        body, name="conv_fwd", grid=(T // tm,),
        in_specs=[pl.BlockSpec((tm, 2 * CH), lambda i: (i, 0)),
                  pl.BlockSpec((HALO, 2 * CH), lambda i: (jnp.maximum(i * hb - 1, 0), 0)),
                  pl.BlockSpec((CONV_W, CH), lambda i: (0, 0)), vec, vec, vec],
        out_specs=[pl.BlockSpec((tm, CH), lambda i: (i, 0)), pl.BlockSpec((tm, CH), lambda i: (i, 0))],
        out_shape=[jax.ShapeDtypeStruct((T, CH), BF16), jax.ShapeDtypeStruct((T, CH), F32)],
        scratch_shapes=[pltpu.VMEM((n, CH), F32), pltpu.VMEM((7, n - 8, CH), F32)],
        compiler_params=_cp(1),
    )(u, u, w, b, lg, lb)


def _mixout_fwd(x, attn, conv, wo):
    T, D = x.shape
    tm = min(TM_MIX, T)
    A = attn.shape[1]

    def body(x_ref, a_ref, c_ref, w_ref, xo_ref):
        xo_ref[...] = x_ref[...] + _dot(a_ref[...], w_ref[:A, :]) + _dot(c_ref[...], w_ref[A:, :])

    return pl.pallas_call(
        body, name="mixout_fwd", grid=(T // tm,),
        in_specs=[pl.BlockSpec((tm, D), lambda i: (i, 0)), pl.BlockSpec((tm, A), lambda i: (i, 0)),
                  pl.BlockSpec((tm, conv.shape[1]), lambda i: (i, 0)), pl.BlockSpec(wo.shape, lambda i: (0, 0))],
        out_specs=pl.BlockSpec((tm, D), lambda i: (i, 0)),
        out_shape=jax.ShapeDtypeStruct((T, D), F32),
        compiler_params=_cp(1),
    )(x, attn, conv, wo)


def _rms_bwd_rows(dh, xf, g):
    xh, r = _rms(xf, None)
    dxn = dh * g
    dx = r * (dxn - xh * jnp.mean(dxn * xh, axis=-1, keepdims=True))
    return dx, jnp.sum(dh * xh, axis=0, keepdims=True), xh * g


def _loss_head(x, g, tgt):
    T, D = x.shape
    tm = min(TM, T)

    def body(x_ref, g_ref, t_ref, loss_ref, dx_ref, dg_ref):
        @pl.when(pl.program_id(0) == 0)
        def _():
            loss_ref[...] = jnp.zeros_like(loss_ref)
            dg_ref[...] = jnp.zeros_like(dg_ref)

        xf = x_ref[...]
        g = g_ref[...]
        xh, _ = _rms(xf, None)
        e = xh * g - t_ref[...]
        loss_ref[...] += 0.5 * jnp.sum(jnp.mean(e * e, axis=-1, keepdims=True), axis=0, keepdims=True)
        dx, dg, _ = _rms_bwd_rows(e * (1.0 / D), xf, g)
        dx_ref[...] = dx
        dg_ref[...] += dg

    return pl.pallas_call(
        body, name="loss_head", grid=(T // tm,),
        in_specs=[pl.BlockSpec((tm, D), lambda i: (i, 0)), pl.BlockSpec((1, D), lambda i: (0, 0)),
                  pl.BlockSpec((tm, D), lambda i: (i, 0))],
        out_specs=[pl.BlockSpec((1, 1), lambda i: (0, 0)), pl.BlockSpec((tm, D), lambda i: (i, 0)),
                   pl.BlockSpec((1, D), lambda i: (0, 0))],
        out_shape=[jax.ShapeDtypeStruct((1, 1), F32), jax.ShapeDtypeStruct((T, D), F32),
                   jax.ShapeDtypeStruct((1, D), F32)],
        compiler_params=_cp(1),
    )(x, g, tgt)


def _ffn_bwd(dxo, x, g, gu, win, wout, dep):
    T, D = x.shape
    FB = win.shape[2]
    tm = min(TM_FFN_BWD, T)

    def body(dxo_ref, x_ref, g_ref, gu_ref, win_hbm, wout_hbm, dep_ref,
             dxi_ref, dg_ref, hb_ref, dgu_ref, a_ref, dyb_ref, win_v, wout_v, sems):
        @pl.when(pl.program_id(0) == 0)
        def _():
            _load_ffn_weights(win_hbm, wout_hbm, win_v, wout_v, sems)
            dg_ref[...] = jnp.zeros_like(dg_ref)

        dyb = (0.5 * dxo_ref[...]).astype(BF16)
        dyb_ref[...] = dyb
        dh = jnp.zeros((tm, D), F32)
        for blk in range(2):
            for lo, sz in _lane_chunks(FB):
                cols = pl.ds(blk * FB + lo, sz)
                da = _dot_nt(dyb, wout_v[cols, :])
                gate = gu_ref[0, :, cols].astype(F32)
                up = gu_ref[1, :, cols].astype(F32)
                sg = _fast_sigmoid(gate)
                s = gate * sg
                a_ref[:, cols] = (s * up).astype(BF16)
                dgate = (da * up * (sg + s * (1.0 - sg))).astype(BF16)
                dup = (da * s).astype(BF16)
                dgu_ref[0, :, cols] = dgate
                dgu_ref[1, :, cols] = dup
                dh = dh + _dot_nt(dgate, win_v[blk, :, pl.ds(lo, sz)]) + _dot_nt(dup, win_v[2 + blk, :, pl.ds(lo, sz)])
        dx, dg, h = _rms_bwd_rows(dh, x_ref[...], g_ref[...])
        dxi_ref[...] = dxo_ref[...] + dx
        dg_ref[...] += dg
        hb_ref[...] = h.astype(BF16)

    row = pl.BlockSpec((tm, D), lambda i: (i, 0))
    act = pl.BlockSpec((2, tm, 2 * FB), lambda i: (0, i, 0))
    return pl.pallas_call(
        body, name="ffn_bwd", grid=(T // tm,),
        in_specs=[row, row, pl.BlockSpec((1, D), lambda i: (0, 0)), act, ANY, ANY, ANY],
        out_specs=[row, pl.BlockSpec((1, D), lambda i: (0, 0)), row, act,
                   pl.BlockSpec((tm, 2 * FB), lambda i: (i, 0)), row],
        out_shape=[jax.ShapeDtypeStruct((T, D), F32), jax.ShapeDtypeStruct((1, D), F32),
                   jax.ShapeDtypeStruct((T, D), BF16), jax.ShapeDtypeStruct((2, T, 2 * FB), BF16),
                   jax.ShapeDtypeStruct((T, 2 * FB), BF16), jax.ShapeDtypeStruct((T, D), BF16)],
        scratch_shapes=[pltpu.VMEM(win.shape, BF16), pltpu.VMEM(wout.shape, BF16), pltpu.SemaphoreType.DMA((6,))],
        compiler_params=_cp(1),
    )(dxo, x, g, gu, win, wout, dep)


def _mix_rms_bwd(dxo, x, g, dzs, wts):
    T, D = x.shape
    tm = min(TM, T)
    npair = len(dzs)

    def body(*refs):
        dxo_ref, x_ref, g_ref = refs[:3]
        dz_refs, w_refs = refs[3:3 + npair], refs[3 + npair:3 + 2 * npair]
        dxi_ref, dg_ref, hb_ref = refs[3 + 2 * npair:]

        @pl.when(pl.program_id(0) == 0)
        def _():
            dg_ref[...] = jnp.zeros_like(dg_ref)

        dh = jnp.zeros((tm, D), F32)
        for p in range(npair):
            dh = dh + _dot(dz_refs[p][...], w_refs[p][...])
        dx, dg, h = _rms_bwd_rows(dh, x_ref[...], g_ref[...])
        dxi_ref[...] = dxo_ref[...] + dx
        dg_ref[...] += dg
        hb_ref[...] = h.astype(BF16)

    row = pl.BlockSpec((tm, D), lambda i: (i, 0))
    return pl.pallas_call(
        body, name="mix_rms_bwd", grid=(T // tm,),
        in_specs=[row, row, pl.BlockSpec((1, D), lambda i: (0, 0))]
                 + [pl.BlockSpec((tm, dz.shape[1]), lambda i: (i, 0)) for dz in dzs]
                 + [pl.BlockSpec(w.shape, lambda i: (0, 0)) for w in wts],
        out_specs=[row, pl.BlockSpec((1, D), lambda i: (0, 0)), row],
        out_shape=[jax.ShapeDtypeStruct((T, D), F32), jax.ShapeDtypeStruct((1, D), F32),
                   jax.ShapeDtypeStruct((T, D), BF16)],
        compiler_params=_cp(1),
    )(dxo, x, g, *dzs, *wts)


def _wgrad(name, a, b, a_spec, b_spec, out_shape, out_spec, nblk, dep, acc_shape):
    T = a.shape[0]
    tk = min(TK_WGRAD, T)
    nk = T // tk

    def body(a_ref, b_ref, dep_ref, o_ref, acc_ref):
        k = pl.program_id(1)

        @pl.when(k == 0)
        def _():
            acc_ref[...] = jnp.zeros_like(acc_ref)

        acc_ref[...] += _dot_tn(a_ref[...], b_ref[...])

        @pl.when(k == nk - 1)
        def _():
            o_ref[...] = acc_ref[...].reshape(o_ref.shape).astype(BF16)

    return pl.pallas_call(
        body, name=name, grid=(nblk, nk), in_specs=[a_spec, b_spec, ANY], out_specs=out_spec,
        out_shape=jax.ShapeDtypeStruct(out_shape, BF16), scratch_shapes=[pltpu.VMEM(acc_shape, F32)],
        compiler_params=_cp(2),
    )(a, b, dep)


def _wgrad_ffn_in(hb, dgu, dep):
    T, D = hb.shape
    FB = dgu.shape[2] // 2
    tk = min(TK_WGRAD, T)
    return _wgrad("wgrad_ffn_in", hb, dgu,
                  pl.BlockSpec((tk, D), lambda b, k: (k, 0)),
                  pl.BlockSpec((None, tk, FB), lambda b, k: (b // 2, k, b % 2)),
                  (4, D, FB), pl.BlockSpec((None, D, FB), lambda b, k: (b, 0, 0)), 4, dep, (D, FB))


def _wgrad_ffn_out(a, dyb, dep):
    T, D = dyb.shape
    FB = a.shape[1] // 2
    tk = min(TK_WGRAD, T)
    return _wgrad("wgrad_ffn_out", a, dyb,
                  pl.BlockSpec((tk, FB), lambda b, k: (k, b)),
                  pl.BlockSpec((tk, D), lambda b, k: (k, 0)),
                  (4, FB // 2, D), pl.BlockSpec((2, FB // 2, D), lambda b, k: (b, 0, 0)), 2, dep, (FB, D))


def _wgrad_cat(a_list, b_list):
    T = a_list[0].shape[0]
    tk = min(TK_WGRAD, T)
    nk = T // tk
    na = len(a_list)
    M, N = sum(a.shape[1] for a in a_list), sum(b.shape[1] for b in b_list)

    def body(*refs):
        a_refs, b_refs, o_ref, acc_ref = refs[:na], refs[na:-2], refs[-2], refs[-1]
        k = pl.program_id(0)

        @pl.when(k == 0)
        def _():
            acc_ref[...] = jnp.zeros_like(acc_ref)

        r0 = 0
        for a_ref in a_refs:
            c0 = 0
            for b_ref in b_refs:
                m, n = a_ref.shape[1], b_ref.shape[1]
                acc_ref[r0:r0 + m, c0:c0 + n] += _dot_tn(a_ref[...], b_ref[...])
                c0 += n
            r0 += a_ref.shape[1]

        @pl.when(k == nk - 1)
        def _():
            o_ref[...] = acc_ref[...].astype(BF16)

    return pl.pallas_call(
        body, name="wgrad_cat", grid=(nk,),
        in_specs=[pl.BlockSpec((tk, v.shape[1]), lambda k: (k, 0)) for v in list(a_list) + list(b_list)],
        out_specs=pl.BlockSpec((M, N), lambda k: (0, 0)),
        out_shape=jax.ShapeDtypeStruct((M, N), BF16), scratch_shapes=[pltpu.VMEM((M, N), F32)],
        compiler_params=_cp(1),
    )(*a_list, *b_list)


def _mixout_bwd(dxo, wo):
    T, D = dxo.shape
    tm = min(TM_MIX, T)
    A = ATTN_W
    C = wo.shape[0] - A

    def body(dxo_ref, w_ref, dyb_ref, da_ref, dc_ref):
        dyb = dxo_ref[...].astype(BF16)
        dyb_ref[...] = dyb
        da_ref[...] = _dot_nt(dyb, w_ref[:A, :]).astype(BF16)
        dc_ref[...] = _dot_nt(dyb, w_ref[A:, :])

    return pl.pallas_call(
        body, name="mixout_bwd", grid=(T // tm,),
        in_specs=[pl.BlockSpec((tm, D), lambda i: (i, 0)), pl.BlockSpec(wo.shape, lambda i: (0, 0))],
        out_specs=[pl.BlockSpec((tm, D), lambda i: (i, 0)), pl.BlockSpec((tm, A), lambda i: (i, 0)),
                   pl.BlockSpec((tm, C), lambda i: (i, 0))],
        out_shape=[jax.ShapeDtypeStruct((T, D), BF16), jax.ShapeDtypeStruct((T, A), BF16),
                   jax.ShapeDtypeStruct((T, C), F32)],
        compiler_params=_cp(1),
    )(dxo, wo)


def _conv_bwd(dconv, ypre, u, w, lg, lb):
    T, CH = dconv.shape
    tm = min(TM, T)
    n = tm + HALO
    hb = tm // HALO
    nt = T // tm
    nchunk = tm // CONV_ROWS

    def body(dc_ref, dcn_ref, yp_ref, ypn_ref, uc_ref, up_ref, w_ref, lg_ref, lb_ref,
             du_ref, dw_ref, dvec_ref, zs_ref, zsh_ref, dy_ref, dysh_ref, dz_ref, dwacc_ref):
        i = pl.program_id(0)

        @pl.when(i == 0)
        def _():
            dwacc_ref[...] = jnp.zeros_like(dwacc_ref)
            dvec_ref[...] = jnp.zeros_like(dvec_ref)

        g, bb = lg_ref[...], lb_ref[...]

        def ln_bwd(dc, yp):
            mu = jnp.mean(yp, axis=-1, keepdims=True)
            d = yp - mu
            rs = lax.rsqrt(jnp.mean(d * d, axis=-1, keepdims=True) + EPS)
            yn = d * rs
            o = yn * g + bb
            sg = _fast_sigmoid(o)
            do = dc * (sg * (1.0 + o * (1.0 - sg)))
            dyn = do * g
            dyp = rs * (dyn - jnp.mean(dyn, axis=-1, keepdims=True)
                        - yn * jnp.mean(dyn * yn, axis=-1, keepdims=True))
            return dyp, do, yn

        dyp, do, yn = ln_bwd(dc_ref[...], yp_ref[...])
        dvec_ref[0:1, :] += jnp.sum(dyp, axis=0, keepdims=True)
        dvec_ref[1:2, :] += jnp.sum(do * yn, axis=0, keepdims=True)
        dvec_ref[2:3, :] += jnp.sum(do, axis=0, keepdims=True)
        dy_ref[0:tm] = dyp
        dyh, _, _ = ln_bwd(dcn_ref[...], ypn_ref[...])
        dy_ref[tm:] = jnp.where(i < nt - 1, dyh, 0.0)
        _shift_copies(dy_ref, dysh_ref, n - 8)
        _fill_z(zs_ref, zsh_ref, uc_ref, up_ref, i, CH, n)

        def chunk(ci, carry):
            c0 = pl.multiple_of(ci * CONV_ROWS, CONV_ROWS)
            acc = jnp.zeros((CONV_ROWS, CH), F32)
            for k in range(CONV_W):
                acc = acc + w_ref[k:k + 1, :] * _tap(dy_ref, dysh_ref, CONV_W - 1 - k, c0)
            dz_ref[pl.ds(c0, CONV_ROWS), :] = acc
            dyc = dy_ref[pl.ds(c0, CONV_ROWS), :]
            for k in range(CONV_W):
                prod = dyc * _tap(zs_ref, zsh_ref, HALO - (CONV_W - 1) + k, c0)
                dwacc_ref[k] += jnp.sum(prod.reshape(CONV_ROWS // 8, 8, CH), axis=0)
            return carry

        lax.fori_loop(0, nchunk, chunk, 0)

        @pl.when(i == nt - 1)
        def _():
            dw_ref[...] = jnp.sum(dwacc_ref[...], axis=1)

        uc = uc_ref[...]
        a = uc[:, :CH]
        sg = _fast_sigmoid(uc[:, CH:])
        dz = dz_ref[...]
        du_ref[:, :CH] = (dz * sg).astype(BF16)
        du_ref[:, CH:] = (dz * a * sg * (1.0 - sg)).astype(BF16)

    cur = lambda c: pl.BlockSpec((tm, c), lambda i: (i, 0))
    nxt = lambda c: pl.BlockSpec((HALO, c), lambda i: (jnp.minimum((i + 1) * hb, T // HALO - 1), 0))
    vec = pl.BlockSpec((1, CH), lambda i: (0, 0))
    return pl.pallas_call(
        body, name="conv_bwd", grid=(nt,),
        in_specs=[cur(CH), nxt(CH), cur(CH), nxt(CH), cur(2 * CH),
                  pl.BlockSpec((HALO, 2 * CH), lambda i: (jnp.maximum(i * hb - 1, 0), 0)),
                  pl.BlockSpec((CONV_W, CH), lambda i: (0, 0)), vec, vec],
        out_specs=[pl.BlockSpec((tm, 2 * CH), lambda i: (i, 0)), pl.BlockSpec((32, CH), lambda i: (0, 0)),
                   pl.BlockSpec((8, CH), lambda i: (0, 0))],
        out_shape=[jax.ShapeDtypeStruct((T, 2 * CH), BF16), jax.ShapeDtypeStruct((32, CH), F32),
                   jax.ShapeDtypeStruct((8, CH), F32)],
        scratch_shapes=[pltpu.VMEM((n, CH), F32), pltpu.VMEM((7, n - 8, CH), F32),
                        pltpu.VMEM((n, CH), F32), pltpu.VMEM((7, n - 8, CH), F32), pltpu.VMEM((tm, CH), F32),
                        pltpu.VMEM((32, 8, CH), F32)],
        compiler_params=_cp(1),
    )(dconv, dconv, ypre, ypre, u, u, w, lg, lb)


def _attn_bwd(sinks, tab, qkv, dattn):
    T = qkv.shape[0]
    nb = T // WINDOW

    def body(sink_ref, tab_ref, q_ref, kvp_ref, kvc_ref, do_ref, dq_ref, dkv_ref, dsk_ref, carry_ref):
        n = pl.program_id(0)

        @pl.when(n == 0)
        def _():
            dsk_ref[...] = jnp.zeros_like(dsk_ref)
            carry_ref[...] = jnp.zeros_like(carry_ref)

        @pl.when(n < nb)
        def _():
            seen = _first_block_mask(n)
            for g in range(N_KV):
                qs = _stack_heads(q_ref, g)
                dos = _stack_heads(do_ref, g)
                k = _band(kvp_ref, kvc_ref, g * HEAD_DIM)
                v = _band(kvp_ref, kvc_ref, KV_W + g * HEAD_DIM)
                p, ps = _attn_probs(qs, k, tab_ref[g], seen, _sink_col(sink_ref, g))
                dp = _dot_nt(dos, v)
                delta = jnp.sum(p * dp, axis=-1, keepdims=True)
                dsb = (p * (dp - delta)).astype(BF16)
                dsink = -ps * delta
                dqs = _dot(dsb, k) * SCALE
                dk = _dot_tn(dsb, qs) * SCALE
                dv = _dot_tn(p.astype(BF16), dos)
                for i in range(GROUP):
                    h = GROUP * g + i
                    dq_ref[:, h * HEAD_DIM:(h + 1) * HEAD_DIM] = dqs[i * WINDOW:(i + 1) * WINDOW].astype(BF16)
                    dsk_ref[h:h + 1, :] += jnp.sum(dsink[i * WINDOW:(i + 1) * WINDOW], axis=0, keepdims=True)
                for off, d in ((g * HEAD_DIM, dk), (KV_W + g * HEAD_DIM, dv)):
                    dkv_ref[:, off:off + HEAD_DIM] = (carry_ref[:, off:off + HEAD_DIM] + d[:WINDOW]).astype(BF16)
                    carry_ref[:, off:off + HEAD_DIM] = d[WINDOW:]

        @pl.when(n == nb)
        def _():
            dkv_ref[...] = carry_ref[...].astype(BF16)

    last = nb - 1
    return pl.pallas_call(
        body, name="attn_bwd", grid=(nb + 1,),
        in_specs=[pl.BlockSpec(memory_space=pltpu.SMEM),
                  pl.BlockSpec(tab.shape, lambda n: (0, 0, 0)),
                  pl.BlockSpec((WINDOW, ATTN_W), lambda n: (jnp.minimum(n, last), 0)),
                  pl.BlockSpec((WINDOW, 2 * KV_W), lambda n: (jnp.clip(n - 1, 0, last), 2)),
                  pl.BlockSpec((WINDOW, 2 * KV_W), lambda n: (jnp.minimum(n, last), 2)),
                  pl.BlockSpec((WINDOW, ATTN_W), lambda n: (jnp.minimum(n, last), 0))],
        out_specs=[pl.BlockSpec((WINDOW, ATTN_W), lambda n: (jnp.minimum(n, last), 0)),
                   pl.BlockSpec((WINDOW, 2 * KV_W), lambda n: (jnp.maximum(n - 1, 0), 0)),
                   pl.BlockSpec((8, LANES), lambda n: (0, 0))],
        out_shape=[jax.ShapeDtypeStruct((T, ATTN_W), BF16), jax.ShapeDtypeStruct((T, 2 * KV_W), BF16),
                   jax.ShapeDtypeStruct((8, LANES), F32)],
        scratch_shapes=[pltpu.VMEM((WINDOW, 2 * KV_W), F32)],
        compiler_params=_cp(1),
    )(sinks, tab, qkv, qkv, qkv, dattn)


def _pack(arrs):
    flat = jnp.concatenate([a.reshape(-1) for a in arrs])
    pad = -flat.shape[0] % (8 * LANES)
    return jnp.pad(flat, (0, pad)).reshape(1, -1, LANES)


def _unpack(packed, like):
    flat = packed.reshape(-1)
    out, off = [], 0
    for a in like:
        out.append(flat[off:off + a.size].reshape(a.shape))
        off += a.size
    return out


def kernel(x, norm_ffn1, w_ffn1_in, w_ffn1_out, norm_mix, w_in, sinks, w_dw, b_dw, conv_ln_g, conv_ln_b, w_out, norm_ffn2, w_ffn2_in, w_ffn2_out, final_norm, loss_target, m_norm_ffn1, m_w_ffn1_in, m_w_ffn1_out, m_norm_mix, m_w_in, m_sinks, m_w_dw, m_b_dw, m_conv_ln_g, m_conv_ln_b, m_w_out, m_norm_ffn2, m_w_ffn2_in, m_w_ffn2_out, m_final_norm, v_norm_ffn1, v_w_ffn1_in, v_w_ffn1_out, v_norm_mix, v_w_in, v_sinks, v_w_dw, v_b_dw, v_conv_ln_g, v_conv_ln_b, v_w_out, v_norm_ffn2, v_w_ffn2_in, v_w_ffn2_out, v_final_norm):
    L, D = norm_ffn1.shape
    T = x.shape[1]
    FB = w_ffn1_in.shape[2]
    CH = b_dw.shape[1]
    QKV = ATTN_W + 2 * KV_W
    xs = x.reshape(T, D)
    tgt = loss_target.reshape(T, D)
    cx, cy, cc = lax.axis_index("x"), lax.axis_index("y"), lax.axis_index("c")
    chip = 2 * cx + cy
    cidx = cc.reshape(1).astype(jnp.int32)
    tr = lambda a_: jnp.transpose(a_, (0, 2, 1))
    big_w = (w_ffn1_in, w_ffn1_out, tr(w_in), w_out, w_ffn2_in, w_ffn2_out)
    big_m = (m_w_ffn1_in, m_w_ffn1_out, tr(m_w_in), m_w_out, m_w_ffn2_in, m_w_ffn2_out)
    big_v = (v_w_ffn1_in, v_w_ffn1_out, tr(v_w_in), v_w_out, v_w_ffn2_in, v_w_ffn2_out)
    NW = len(big_w) + 1

    def own_slot(a, slots=4, idx=chip):
        return lax.dynamic_update_index_in_dim(lax.empty((slots,) + a.shape, a.dtype), a, idx, 0)

    def shards(l, tok):
        return [own_slot((w_[l] + tok[0, 0]).astype(BF16)) for w_ in big_w] + [own_slot(w_dw[l] + tok[0, 0])]

    def gather_start(lands, tok):
        return _xchg_start("gather_start", [], lands, _gather_plan, tok)

    def gather_arrived(started, after, n, taps):
        _, lands, tok = _xchg_wait("gather_wait", started, 0, n, _gather_plan, after)
        return _xchg_start("gshare_start", [], lands[:-1] if taps else lands, _gshare_plan, tok, "sibling3"), lands[-1]

    def shared_weights(shared, after, n):
        _, mats, tok = _xchg_wait("gshare_wait", shared, 0, n, _gshare_plan, after, "sibling3")
        return mats, tok

    row = lambda a, l: a[l].reshape(1, -1)
    tab = _attn_bias_table()
    NB = len(big_w)

    saved, W = [], []
    zero_tok = jnp.zeros((8, LANES), F32)
    src0 = shards(0, zero_tok)
    started = gather_start(src0[:2], zero_tok)
    rest0 = gather_start(src0[2:], started[-1])
    cast = [None] + [shards(l, rest0[-1]) for l in range(1, L)]
    shared, _ = gather_arrived(started, [xs] + [a_ for c_ in cast[1:] for a_ in c_], 2, False)
    after = [shared[-1]]
    for l in range(L):
        mats, tok = shared_weights(shared, after, 2 if l == 0 else NB)
        started = None
        if l + 1 < L:
            started = gather_start(cast[l + 1], tok)
            tok = started[-1]
        x0 = xs
        x1, gu1 = _ffn_fwd(x0, row(norm_ffn1, l) + tok[0, 0], mats[0], mats[1].reshape(2 * FB, D))
        gm_row = row(norm_mix, l)
        if l == 0:
            shared, gdw = gather_arrived(rest0, [x1], NW - 2, True)
            rest, tok = shared_weights(shared, [shared[-1]], NB - 2)
            mats = list(mats) + list(rest)
            gm_row = gm_row + tok[0, 0]
        g1i, g1o, gi, go, g2i, g2o = mats
        w = dict(f1i=g1i, f1o=g1o.reshape(2 * FB, D), f2i=g2i, f2o=g2o.reshape(2 * FB, D),
                 wit=gi.reshape(-1, D), wo=go.reshape(-1, D),
                 wdw=jnp.transpose(gdw, (1, 0, 2)).reshape(CONV_W, CH))
        W.append(w)
        qkv, u = _mixproj_fwd(x1, gm_row, w["wit"])
        attn = _attn_fwd(row(sinks, l), tab, qkv)
        conv, ypre = _conv_fwd(u, w["wdw"], row(b_dw, l), row(conv_ln_g, l), row(conv_ln_b, l))
        x2 = _mixout_fwd(x1, attn, conv, w["wo"])
        g2_row = row(norm_ffn2, l)
        if started is not None and l > 0:
            shared, gdw = gather_arrived(started, [x2], NW, True)
            g2_row = g2_row + shared[-1][0, 0]
        xs, gu2 = _ffn_fwd(x2, g2_row, w["f2i"], w["f2o"])
        if started is not None and l == 0:
            shared, gdw = gather_arrived(started, [xs], NW, True)
        saved.append((x0, gu1, x1, qkv, u, attn, conv, ypre, x2, gu2))
        after = [xs]

    loss_part, dx, d_final = _loss_head(xs, final_norm.reshape(1, D), tgt)
    loss = lax.psum(loss_part[0, 0], ("x", "y", "c"))

    bufs = [[lax.empty(w_.shape, F32) for _ in range(4)] for w_ in big_w]
    d_n1, d_nm, d_n2 = [None] * L, [None] * L, [None] * L
    d_sk, d_bdw, d_lg, d_lb, d_wdw = [None] * L, [None] * L, [None] * L, [None] * L, [None] * L

    me_idx = 4 * cx + 2 * cy + cc

    def reduce_start(gs):
        lands = []
        for g in gs:
            h = g.shape[1] // 2
            mine = lax.dynamic_slice(g, (chip, cc * h, 0), (1, h, g.shape[2]))[0]
            lands.append(own_slot(mine, 8, me_idx))
        return _xchg_start("rs_start", gs, lands, _rs_plan, zero_tok, "all")

    def share_start(rs_started, after, n):
        _, qs, tok = _xchg_wait("rs_wait", rs_started, n, n, _rs_plan, after, "all")
        return _xchg_start("qshare_start", qs, [lax.empty(q.shape, q.dtype) for q in qs], _whole_plan, tok, "sibling")

    def finish(l, shared, after, idxs):
        q_own, q_sib, _ = _xchg_wait("qshare_wait", shared, len(idxs), len(idxs), _whole_plan, after, "sibling")
        for k, t in enumerate(idxs):
            bufs[t] = _adamw_layer(cidx, q_own[k], q_sib[k], big_w[t], big_m[t], big_v[t], bufs[t], l)

    ALL = list(range(NB))
    EARLY, LATE = ALL[2:], ALL[:2]
    rs_list, shares = [], []
    tok = zero_tok
    for l in reversed(range(L)):
        w = W[l]
        x0, gu1, x1, qkv, u, attn, conv, ypre, x2, gu2 = saved[l]
        dx, d_n2[l], hb, dgu, a, dyb = _ffn_bwd(dx, x2, row(norm_ffn2, l), gu2, w["f2i"], w["f2o"], tok)
        g_f2i, g_f2o = _wgrad_ffn_in(hb, dgu, tok), _wgrad_ffn_out(a, dyb, tok)
        lg_row = row(conv_ln_g, l)
        if len(rs_list) >= 2:
            pl_, st_ = rs_list[-2]
            shares.append((pl_, share_start(st_, [g_f2o], NB)))
            lg_row = lg_row + shares[-1][1][-1][0, 0]
        dyb, dattn, dconv = _mixout_bwd(dx, w["wo"])
        g_wo = _wgrad_cat([attn, conv], [dyb]).reshape(4, -1, D)
        du, dwdw, dvec = _conv_bwd(dconv, ypre, u, w["wdw"], lg_row, row(conv_ln_b, l))
        d_wdw[l], d_bdw[l], d_lg[l], d_lb[l] = dwdw[:CONV_W], dvec[0], dvec[1], dvec[2]
        dq, dkv, dsk = _attn_bwd(row(sinks, l), tab, qkv, dattn)
        d_sk[l] = dsk[:, 0]
        wit = w["wit"]
        dx, d_nm[l], hb = _mix_rms_bwd(dx, x1, row(norm_mix, l), [dq, dkv, du],
                                       [wit[:ATTN_W], wit[ATTN_W:QKV], wit[QKV:]])
        g_wi = _wgrad_cat([dq, dkv, du], [hb]).reshape(4, -1, D)
        if l == 0:
            rs_early = reduce_start([g_wi, g_wo, g_f2i, g_f2o])
            tok = rs_early[-1]
        dx, d_n1[l], hb, dgu, a, dyb = _ffn_bwd(dx, x0, row(norm_ffn1, l), gu1, w["f1i"], w["f1o"], tok)
        g_f1i, g_f1o = _wgrad_ffn_in(hb, dgu, tok), _wgrad_ffn_out(a, dyb, tok)
        rs_started = reduce_start([g_f1i, g_f1o] if l == 0 else [g_f1i, g_f1o, g_wi, g_wo, g_f2i, g_f2o])
        tok = rs_started[-1]
        rs_list.append((l, rs_started))
    grad_x = dx.reshape(x.shape)

    small_g = [jnp.concatenate(d, axis=0) for d in (d_n1, d_nm, d_n2)] + [d_final, jnp.stack(d_sk)] + \
              [jnp.stack(d) for d in (d_bdw, d_lg, d_lb, d_wdw)]
    packed = _pack(small_g)[0]
    small_started = _xchg_start("small_start", [packed], [own_slot(packed, 8, 4 * cx + 2 * cy + cc)], _slot_plan, tok, "all")

    rs_late = rs_list.pop()[1]
    after = [small_started[-1]]
    if len(rs_list) > len(shares):
        pl_, st_ = rs_list[len(shares)]
        shares.append((pl_, share_start(st_, after, NB)))
        after = [shares[-1][1][-1]]
    if shares:
        finish(*shares.pop(0), after, ALL)
        after = [b_[0] for b_ in bufs]
    sh_early = share_start(rs_early, after, len(EARLY))
    after = [sh_early[-1]]
    sh_late = None
    for l, sh in shares:
        finish(l, sh, after, ALL)
        after = [b_[0] for b_ in bufs]
        if sh_late is None:
            sh_late = share_start(rs_late, after, len(LATE))
            after = [sh_late[-1]]
    if sh_late is None:
        sh_late = share_start(rs_late, after, len(LATE))
        after = [sh_late[-1]]
    _, (slots,), _ = _xchg_wait("small_wait", small_started, 1, 1, _slot_plan, after, "all")
    small_sum = _unpack(_sum_slots(slots), small_g)
    g_wdw = lax.dynamic_slice_in_dim(small_sum[8], chip * w_dw.shape[2], w_dw.shape[2], axis=2)
    small_g = [small_sum[0], small_sum[1], small_sum[2], small_sum[3].reshape(D), small_sum[4],
               small_sum[5], small_sum[6], small_sum[7], g_wdw]
    small_w = (norm_ffn1, norm_mix, norm_ffn2, final_norm, sinks, b_dw, conv_ln_g, conv_ln_b, w_dw)
    small_m = (m_norm_ffn1, m_norm_mix, m_norm_ffn2, m_final_norm, m_sinks, m_b_dw, m_conv_ln_g, m_conv_ln_b, m_w_dw)
    small_v = (v_norm_ffn1, v_norm_mix, v_norm_ffn2, v_final_norm, v_sinks, v_b_dw, v_conv_ln_g, v_conv_ln_b, v_w_dw)
    upd = _adamw(_pack(small_g), _pack(small_w), _pack(small_m), _pack(small_v))
    small_upd = [_unpack(u_, small_w) for u_ in upd]
    finish(0, sh_early, [upd[0]], EARLY)
    finish(0, sh_late, [bufs[t][0] for t in EARLY], LATE)

    order = ("norm_ffn1", "w_ffn1_in", "w_ffn1_out", "norm_mix", "w_in", "sinks", "w_dw", "b_dw", "conv_ln_g",
             "conv_ln_b", "w_out", "norm_ffn2", "w_ffn2_in", "w_ffn2_out", "final_norm")
    small_names = ("norm_ffn1", "norm_mix", "norm_ffn2", "final_norm", "sinks", "b_dw", "conv_ln_g", "conv_ln_b", "w_dw")
    big_names = ("w_ffn1_in", "w_ffn1_out", "w_in", "w_out", "w_ffn2_in", "w_ffn2_out")
    grads, deltas, new_m, new_v = {}, {}, {}, {}
    for i, nme in enumerate(small_names):
        grads[nme], deltas[nme], new_m[nme], new_v[nme] = small_g[i], small_upd[0][i], small_upd[1][i], small_upd[2][i]
    for i, nme in enumerate(big_names):
        grads[nme], deltas[nme], new_m[nme], new_v[nme] = [tr(b_) for b_ in bufs[i]] if nme == "w_in" else bufs[i]
    return (loss, grad_x, *[grads[n] for n in order], *[deltas[n] for n in order],
            *[new_m[n] for n in order], *[new_v[n] for n in order])
```

```python
import functools

import jax
import jax.numpy as jnp
from jax import lax
from jax.experimental import pallas as pl
from jax.experimental.pallas import tpu as pltpu

F32, BF16 = jnp.float32, jnp.bfloat16
EPS = 1e-6
NEG_INF = -1e30
HEAD_DIM = 64
N_HEADS = 8
N_KV = 2
GROUP = N_HEADS // N_KV
WINDOW = 128
ATTN_W = N_HEADS * HEAD_DIM
KV_W = N_KV * HEAD_DIM
CONV_W = 31
HALO = 32
CONV_ROWS = 32
SCALE = 1.0 / 8.0
ADAM_LR, ADAM_B1, ADAM_B2, ADAM_EPS, ADAM_WD, ADAM_STEP = 0.001, 0.9, 0.999, 1e-08, 0.01, 10
TM = 512
TM_FFN_BWD = 256
TK_WGRAD = 2048
TM_MIX = 1024
LANES = 128
VMEM_LIMIT = 52 * 1024 * 1024
MESH = pl.DeviceIdType.MESH
ANY = pl.BlockSpec(memory_space=pl.ANY)
HBM = pl.BlockSpec(memory_space=pltpu.HBM)
SEM = pl.BlockSpec(memory_space=pltpu.SEMAPHORE)
VMEM = pl.BlockSpec(memory_space=pltpu.VMEM)
EFFECT = pltpu.SideEffectType.DATAFLOW_SIDE_EFFECTING
TOKEN = jax.ShapeDtypeStruct((8, LANES), F32)


def _cp(n):
    return pltpu.CompilerParams(dimension_semantics=("arbitrary",) * n, vmem_limit_bytes=VMEM_LIMIT)


def _dot(a, b):
    return jnp.dot(a, b, preferred_element_type=F32)


def _dot_nt(a, b):
    return lax.dot_general(a, b, (((1,), (1,)), ((), ())), preferred_element_type=F32)


def _dot_tn(a, b):
    return lax.dot_general(a, b, (((0,), (0,)), ((), ())), preferred_element_type=F32)


def _place():
    x, y, c = lax.axis_index("x"), lax.axis_index("y"), lax.axis_index("c")
    chips = [(1 - x, y), (x, 1 - y), (1 - x, 1 - y)]
    return x, y, c, chips


def _rcopy(src, dst, send_sems, recv_sems, k, dev):
    return pltpu.make_async_remote_copy(src_ref=src, dst_ref=dst, send_sem=send_sems.at[k],
                                        recv_sem=recv_sems.at[k], device_id=dev, device_id_type=MESH)


def _hbm(a):
    return pltpu.with_memory_space_constraint(a, pltpu.HBM)


PEERS = {"chips": 3, "sibling": 1, "sibling3": 3, "all": 7}


def _targets(mode):
    x, y, c, chips = _place()
    b = 2 * x + y
    if mode == "chips":
        return b, c, [((px, py, c), 2 * px + py) for px, py in chips]
    if mode == "sibling":
        return b, c, [((x, y, 1 - c), b)]
    if mode == "sibling3":
        return b, c, [((x, y, 1 - c), 2 * px + py) for px, py in chips]
    flip = lambda v, f: 1 - v if f else v
    devs = [(flip(x, k >> 2 & 1), flip(y, k >> 1 & 1), flip(c, k & 1)) for k in range(1, 8)]
    return 4 * x + 2 * y + c, c, [(d, 4 * d[0] + 2 * d[1] + d[2]) for d in devs]


def _xchg_start(name, srcs, lands, plan, dep, mode="chips"):
    ns, nl, npeer = len(srcs), len(lands), PEERS[mode]

    def body(*refs):
        land = refs[ns:ns + nl]
        src = refs[:ns] if ns else land
        send_sems, recv_sems, token = refs[ns + nl + 1], refs[ns + nl + 2], refs[-1]
        me, c, peers = _targets(mode)
        for t in range(nl):
            for j, (dev, tag) in enumerate(peers):
                s, d, _ = plan(src[t], land[t], t, me, c, tag)
                _rcopy(s, d, send_sems, recv_sems, npeer * t + j, dev).start()
        token[...] = jnp.zeros_like(token)

    arrs = list(srcs) + list(lands)
    return pl.pallas_call(
        body, name=name,
        out_shape=(pltpu.SemaphoreType.DMA((npeer * nl,)), pltpu.SemaphoreType.DMA((npeer * nl,)),
                   *[pltpu.HBM(a.shape, a.dtype) for a in arrs], TOKEN),
        in_specs=[HBM] * (ns + nl) + [ANY], out_specs=(SEM, SEM, *[HBM] * (ns + nl), VMEM),
        input_output_aliases={i: 2 + i for i in range(ns + nl)},
        compiler_params=pltpu.CompilerParams(has_side_effects=EFFECT),
    )(*[_hbm(a) for a in arrs], dep)


def _xchg_wait(name, started, ns, nl, plan, after, mode="chips"):
    send_sems, recv_sems, thru = started[0], started[1], started[2:2 + ns + nl]
    npeer = PEERS[mode]

    def body(*refs):
        land = refs[ns:ns + nl]
        src = refs[:ns] if ns else land
        send_sems, recv_sems, token = refs[ns + nl], refs[ns + nl + 1], refs[-1]
        me, c, peers = _targets(mode)
        for t in range(nl):
            for j, (dev, tag) in enumerate(peers):
                s, _, a = plan(src[t], land[t], t, me, c, tag)
                cp = _rcopy(s, a, send_sems, recv_sems, npeer * t + j, dev)
                cp.wait_send()
                cp.wait_recv()
        token[...] = jnp.zeros_like(token)

    out = pl.pallas_call(
        body, name=name,
        out_shape=(*[pltpu.HBM(a.shape, a.dtype) for a in thru], TOKEN),
        in_specs=[HBM] * (ns + nl) + [SEM, SEM] + [ANY] * len(after), out_specs=(*[HBM] * (ns + nl), VMEM),
        input_output_aliases={i: i for i in range(ns + nl)},
        compiler_params=pltpu.CompilerParams(has_side_effects=EFFECT),
    )(*thru, send_sems, recv_sems, *after)
    return out[:ns], out[ns:ns + nl], out[-1]


def _half(ref_rows, which):
    h = ref_rows // 2
    return pl.ds(which * h, h)


def _gather_plan(src, land, t, b, c, pb):
    if land.shape[1] % 2 == 0:
        hs = _half(land.shape[1], c)
        return land.at[b, hs], land.at[b, hs], land.at[pb, hs]
    return land.at[b], land.at[b], land.at[pb]


def _gshare_plan(src, land, t, b, c, pb):
    return land.at[pb, _half(land.shape[1], c)], land.at[pb, _half(land.shape[1], c)], land.at[pb, _half(land.shape[1], 1 - c)]


def _rs_plan(src, land, t, me, c, tag):
    h = src.shape[1] // 2
    return src.at[tag // 2, pl.ds((tag % 2) * h, h), :], land.at[me], land.at[tag]


def _rows_block(h, cap=512):
    for rb in range(min(h, cap) // 16 * 16, 0, -16):
        if h % rb == 0:
            return rb
    return h


def _whole_plan(src, land, t, me, c, tag):
    return src, land, land


def _slot_plan(src, land, t, me, c, tag):
    return src, land.at[me], land.at[tag]


def _adam_update(gg, w, m, v):
    m2 = ADAM_B1 * m + (1.0 - ADAM_B1) * gg
    v2 = ADAM_B2 * v + (1.0 - ADAM_B2) * (gg * gg)
    mh = m2 / (1.0 - ADAM_B1 ** ADAM_STEP)
    vh = v2 / (1.0 - ADAM_B2 ** ADAM_STEP)
    return -ADAM_LR * (mh / (jnp.sqrt(vh) + ADAM_EPS) + ADAM_WD * w), m2, v2


def _adamw_layer(cidx, q_own, q_sib, w, m, v, bufs, l):
    L, R, C = w.shape
    h = R // 2
    rb = _rows_block(h, 256)
    nr = h // rb

    def body(c_ref, qo_ref, qs_ref, w_ref, m_ref, v_ref, *rest):
        g_ref, d_ref, mo_ref, vo_ref = rest[-4:]
        own = pl.program_id(0) == c_ref[0]
        gg = jnp.zeros((rb, C), F32)
        for s in range(8):
            gg = gg + jnp.where(own, qo_ref[s], qs_ref[s]).astype(F32)
        g_ref[...] = gg
        d_ref[...], mo_ref[...], vo_ref[...] = _adam_update(gg, w_ref[...], m_ref[...], v_ref[...])

    q_own_spec = pl.BlockSpec((8, rb, C), lambda hh, i, c: (0, jnp.where(hh == c[0], i, 0), 0))
    q_sib_spec = pl.BlockSpec((8, rb, C), lambda hh, i, c: (0, jnp.where(hh == c[0], 0, i), 0))
    wspec = pl.BlockSpec((None, rb, C), lambda hh, i, c: (l, hh * nr + i, 0))
    return pl.pallas_call(
        body, name="adamw_layer", out_shape=[jax.ShapeDtypeStruct(w.shape, F32)] * 4,
        grid_spec=pltpu.PrefetchScalarGridSpec(
            num_scalar_prefetch=1, grid=(2, nr),
            in_specs=[q_own_spec, q_sib_spec, wspec, wspec, wspec] + [ANY] * 4, out_specs=[wspec] * 4),
        input_output_aliases={6 + k: k for k in range(4)},
        compiler_params=_cp(2),
    )(cidx, q_own, q_sib, w, m, v, *bufs)


def _adamw(g, w, m, v):
    L, R, C = g.shape
    rb = _rows_block(R)

    def body(g_ref, w_ref, m_ref, v_ref, d_ref, mo_ref, vo_ref):
        d_ref[...], mo_ref[...], vo_ref[...] = _adam_update(g_ref[...], w_ref[...], m_ref[...], v_ref[...])

    spec = pl.BlockSpec((None, rb, C), lambda l, i: (l, i, 0))
    return pl.pallas_call(
        body, name="adamw", grid=(L, R // rb), in_specs=[spec] * 4, out_specs=[spec] * 3,
        out_shape=[jax.ShapeDtypeStruct(g.shape, F32)] * 3, compiler_params=_cp(2),
    )(g, w, m, v)


def _sum_slots(buf):
    def body(b_ref, o_ref):
        acc = b_ref[0]
        for k in range(1, 8):
            acc = acc + b_ref[k]
        o_ref[...] = acc

    return pl.pallas_call(body, name="sum_slots", in_specs=[VMEM], out_specs=VMEM,
                          out_shape=jax.ShapeDtypeStruct(buf.shape[1:], F32))(buf)


def _rms(xf, g):
    r = lax.rsqrt(jnp.mean(xf * xf, axis=-1, keepdims=True) + EPS)
    return xf * r, r


def _lane_chunks(n):
    lo = (n // LANES + 1) // 2 * LANES
    return ((0, lo), (lo, n - lo))


def _load_ffn_weights(win_hbm, wout_hbm, win_v, wout_v, sems):
    fb = win_v.shape[2]
    loads = [pltpu.make_async_copy(win_hbm.at[k], win_v.at[k], sems.at[k]) for k in range(4)]
    loads += [pltpu.make_async_copy(wout_hbm.at[pl.ds(k * fb, fb)], wout_v.at[pl.ds(k * fb, fb)], sems.at[4 + k])
              for k in range(2)]
    for cp in loads:
        cp.start()
    for cp in loads:
        cp.wait()


def _fast_sigmoid(v):
    return pl.reciprocal(1.0 + jnp.exp(-v), approx=True)


def _ffn_fwd(x, g, win, wout):
    T, D = x.shape
    FB = win.shape[2]
    tm = min(TM, T)

    def body(x_ref, g_ref, win_hbm, wout_hbm, xo_ref, gu_ref, win_v, wout_v, sems):
        @pl.when(pl.program_id(0) == 0)
        def _():
            _load_ffn_weights(win_hbm, wout_hbm, win_v, wout_v, sems)

        xf = x_ref[...]
        xh, _ = _rms(xf, None)
        h = (xh * g_ref[...]).astype(BF16)
        acc = jnp.zeros((tm, D), F32)
        for blk in range(2):
            for lo, sz in _lane_chunks(FB):
                cols = pl.ds(blk * FB + lo, sz)
                gate = _dot(h, win_v[blk, :, pl.ds(lo, sz)])
                up = _dot(h, win_v[2 + blk, :, pl.ds(lo, sz)])
                gu_ref[0, :, cols] = gate.astype(BF16)
                gu_ref[1, :, cols] = up.astype(BF16)
                a = (gate * _fast_sigmoid(gate) * up).astype(BF16)
                acc = acc + _dot(a, wout_v[cols, :])
        xo_ref[...] = xf + 0.5 * acc

    row = pl.BlockSpec((tm, D), lambda i: (i, 0))
    return pl.pallas_call(
        body, name="ffn_fwd", grid=(T // tm,),
        in_specs=[row, pl.BlockSpec((1, D), lambda i: (0, 0)), ANY, ANY],
        out_specs=[row, pl.BlockSpec((2, tm, 2 * FB), lambda i: (0, i, 0))],
        out_shape=[jax.ShapeDtypeStruct((T, D), F32), jax.ShapeDtypeStruct((2, T, 2 * FB), BF16)],
        scratch_shapes=[pltpu.VMEM(win.shape, BF16), pltpu.VMEM(wout.shape, BF16), pltpu.SemaphoreType.DMA((6,))],
        compiler_params=_cp(1),
    )(x, g, win, wout)


def _mixproj_fwd(x, g, wt):
    T, D = x.shape
    W = wt.shape[0]
    QKV = ATTN_W + 2 * KV_W
    tm = min(TM_MIX, T)

    def body(x_ref, g_ref, w_ref, qkv_ref, u_ref):
        xh, _ = _rms(x_ref[...], None)
        h = (xh * g_ref[...]).astype(BF16)
        qkv_ref[...] = _dot_nt(h, w_ref[:QKV, :]).astype(BF16)
        u_ref[...] = _dot_nt(h, w_ref[QKV:, :])

    return pl.pallas_call(
        body, name="mixproj_fwd", grid=(T // tm,),
        in_specs=[pl.BlockSpec((tm, D), lambda i: (i, 0)), pl.BlockSpec((1, D), lambda i: (0, 0)),
                  pl.BlockSpec((W, D), lambda i: (0, 0))],
        out_specs=[pl.BlockSpec((tm, QKV), lambda i: (i, 0)), pl.BlockSpec((tm, W - QKV), lambda i: (i, 0))],
        out_shape=[jax.ShapeDtypeStruct((T, QKV), BF16), jax.ShapeDtypeStruct((T, W - QKV), F32)],
        compiler_params=_cp(1),
    )(x, g, wt)


def _attn_bias_table():
    rows, cols = GROUP * WINDOW, 2 * WINDOW
    shape = (2, N_KV, rows, cols)
    first = lax.broadcasted_iota(jnp.int32, shape, 0) == 0
    row = lax.broadcasted_iota(jnp.int32, shape, 2)
    col = lax.broadcasted_iota(jnp.int32, shape, 3)
    head = GROUP * lax.broadcasted_iota(jnp.int32, shape, 1) + (row >> 7)
    dist = (row & (WINDOW - 1)) + WINDOW - col
    slope = jnp.exp2(-(head + 1).astype(F32))
    seen = (dist >= 0) & (dist < WINDOW) & ~(first & (col < WINDOW))
    return jnp.where(seen, -slope * dist.astype(F32), NEG_INF)


def _sink_col(sink_ref, g):
    hi = lax.broadcasted_iota(jnp.int32, (GROUP * WINDOW, 1), 0) >> 7
    col = jnp.zeros((GROUP * WINDOW, 1), F32)
    for i in range(GROUP):
        col = jnp.where(hi == i, sink_ref[0, GROUP * g + i], col)
    return col


def _stack_heads(ref, g):
    return jnp.concatenate([ref[:, (GROUP * g + i) * HEAD_DIM:(GROUP * g + i + 1) * HEAD_DIM]
                            for i in range(GROUP)], axis=0)


def _band(kvp_ref, kvc_ref, off):
    return jnp.concatenate([kvp_ref[:, off:off + HEAD_DIM], kvc_ref[:, off:off + HEAD_DIM]], axis=0)


def _attn_probs(qs, k, bias, sink):
    s = _dot_nt(qs, k) * SCALE + bias
    m = jnp.maximum(jnp.max(s, axis=-1, keepdims=True), sink)
    p = jnp.exp(s - m)
    es = jnp.exp(sink - m)
    inv = 1.0 / (jnp.sum(p, axis=-1, keepdims=True) + es)
    return p * inv, es * inv


def _attn_fwd(sinks, tab, qkv):
    T = qkv.shape[0]
    nb = T // WINDOW

    def body(sink_ref, tab_ref, q_ref, kvp_ref, kvc_ref, o_ref):
        other = jnp.minimum(pl.program_id(0), 1)
        for g in range(N_KV):
            qs = _stack_heads(q_ref, g)
            k = _band(kvp_ref, kvc_ref, g * HEAD_DIM)
            v = _band(kvp_ref, kvc_ref, KV_W + g * HEAD_DIM)
            p, _ = _attn_probs(qs, k, tab_ref[other, g], _sink_col(sink_ref, g))
            o = _dot(p.astype(BF16), v)
            for i in range(GROUP):
                h = GROUP * g + i
                o_ref[:, h * HEAD_DIM:(h + 1) * HEAD_DIM] = o[i * WINDOW:(i + 1) * WINDOW].astype(BF16)

    return pl.pallas_call(
        body, name="attn_fwd", grid=(nb,),
        in_specs=[pl.BlockSpec(memory_space=pltpu.SMEM),
                  pl.BlockSpec(tab.shape, lambda n: (0, 0, 0, 0)),
                  pl.BlockSpec((WINDOW, ATTN_W), lambda n: (n, 0)),
                  pl.BlockSpec((WINDOW, 2 * KV_W), lambda n: (jnp.maximum(n - 1, 0), 2)),
                  pl.BlockSpec((WINDOW, 2 * KV_W), lambda n: (n, 2))],
        out_specs=pl.BlockSpec((WINDOW, ATTN_W), lambda n: (n, 0)),
        out_shape=jax.ShapeDtypeStruct((T, ATTN_W), BF16),
        compiler_params=_cp(1),
    )(sinks, tab, qkv, qkv, qkv)


def _shift_copies(src_ref, dst_ref, n):
    for b in range(1, 8):
        dst_ref[b - 1] = src_ref[b:b + n, :]


def _tap(src_ref, sh_ref, s, c0):
    a, b = divmod(s, 8)
    start = pl.multiple_of(c0 + 8 * a, 8)
    if b == 0:
        return src_ref[pl.ds(start, CONV_ROWS), :]
    return sh_ref[b - 1, pl.ds(start, CONV_ROWS), :]


def _glu_rows(u, ch):
    return u[:, :ch] * _fast_sigmoid(u[:, ch:])


def _fill_z(zs_ref, zsh_ref, uc_ref, up_ref, i, ch, n):
    zs_ref[0:HALO] = jnp.where(i > 0, _glu_rows(up_ref[...], ch), 0.0)
    zs_ref[HALO:] = _glu_rows(uc_ref[...], ch)
    _shift_copies(zs_ref, zsh_ref, n - 8)


def _conv_fwd(u, w, b, lg, lb):
    T = u.shape[0]
    CH = u.shape[1] // 2
    tm = min(TM, T)
    n = tm + HALO
    hb = tm // HALO

    def body(uc_ref, up_ref, w_ref, b_ref, lg_ref, lb_ref, conv_ref, ypre_ref, zs_ref, zsh_ref):
        i = pl.program_id(0)
        _fill_z(zs_ref, zsh_ref, uc_ref, up_ref, i, CH, n)
        bias = b_ref[...]

        def chunk(ci, carry):
            c0 = pl.multiple_of(ci * CONV_ROWS, CONV_ROWS)
            acc = jnp.broadcast_to(bias, (CONV_ROWS, CH))
            for k in range(CONV_W):
                acc = acc + w_ref[k:k + 1, :] * _tap(zs_ref, zsh_ref, HALO - (CONV_W - 1) + k, c0)
            ypre_ref[pl.ds(c0, CONV_ROWS), :] = acc
            return carry

        lax.fori_loop(0, tm // CONV_ROWS, chunk, 0)
        y = ypre_ref[...]
        mu = jnp.mean(y, axis=-1, keepdims=True)
        d = y - mu
        var = jnp.mean(d * d, axis=-1, keepdims=True)
        o = d * lax.rsqrt(var + EPS) * lg_ref[...] + lb_ref[...]
        conv_ref[...] = (o * _fast_sigmoid(o)).astype(BF16)

    vec = pl.BlockSpec((1, CH), lambda i: (0, 0))
    return pl.pallas_call(
        body, name="conv_fwd", grid=(T // tm,),
        in_specs=[pl.BlockSpec((tm, 2 * CH), lambda i: (i, 0)),
                  pl.BlockSpec((HALO, 2 * CH), lambda i: (jnp.maximum(i * hb - 1, 0), 0)),
                  pl.BlockSpec((CONV_W, CH), lambda i: (0, 0)), vec, vec, vec],
        out_specs=[pl.BlockSpec((tm, CH), lambda i: (i, 0)), pl.BlockSpec((tm, CH), lambda i: (i, 0))],
        out_shape=[jax.ShapeDtypeStruct((T, CH), BF16), jax.ShapeDtypeStruct((T, CH), F32)],
        scratch_shapes=[pltpu.VMEM((n, CH), F32), pltpu.VMEM((7, n - 8, CH), F32)],
        compiler_params=_cp(1),
    )(u, u, w, b, lg, lb)


def _mixout_fwd(x, attn, conv, wo):
    T, D = x.shape
    tm = min(TM_MIX, T)
    A = attn.shape[1]

    def body(x_ref, a_ref, c_ref, w_ref, xo_ref):
        xo_ref[...] = x_ref[...] + _dot(a_ref[...], w_ref[:A, :]) + _dot(c_ref[...], w_ref[A:, :])

    return pl.pallas_call(
        body, name="mixout_fwd", grid=(T // tm,),
        in_specs=[pl.BlockSpec((tm, D), lambda i: (i, 0)), pl.BlockSpec((tm, A), lambda i: (i, 0)),
                  pl.BlockSpec((tm, conv.shape[1]), lambda i: (i, 0)), pl.BlockSpec(wo.shape, lambda i: (0, 0))],
        out_specs=pl.BlockSpec((tm, D), lambda i: (i, 0)),
        out_shape=jax.ShapeDtypeStruct((T, D), F32),
        compiler_params=_cp(1),
    )(x, attn, conv, wo)


def _rms_bwd_rows(dh, xf, g):
    xh, r = _rms(xf, None)
    dxn = dh * g
    dx = r * (dxn - xh * jnp.mean(dxn * xh, axis=-1, keepdims=True))
    return dx, jnp.sum(dh * xh, axis=0, keepdims=True), xh * g


def _loss_head(x, g, tgt):
    T, D = x.shape
    tm = min(TM, T)

    def body(x_ref, g_ref, t_ref, loss_ref, dx_ref, dg_ref):
        @pl.when(pl.program_id(0) == 0)
        def _():
            loss_ref[...] = jnp.zeros_like(loss_ref)
            dg_ref[...] = jnp.zeros_like(dg_ref)

        xf = x_ref[...]
        g = g_ref[...]
        xh, _ = _rms(xf, None)
        e = xh * g - t_ref[...]
        loss_ref[...] += 0.5 * jnp.sum(jnp.mean(e * e, axis=-1, keepdims=True), axis=0, keepdims=True)
        dx, dg, _ = _rms_bwd_rows(e * (1.0 / D), xf, g)
        dx_ref[...] = dx
        dg_ref[...] += dg

    return pl.pallas_call(
        body, name="loss_head", grid=(T // tm,),
        in_specs=[pl.BlockSpec((tm, D), lambda i: (i, 0)), pl.BlockSpec((1, D), lambda i: (0, 0)),
                  pl.BlockSpec((tm, D), lambda i: (i, 0))],
        out_specs=[pl.BlockSpec((1, 1), lambda i: (0, 0)), pl.BlockSpec((tm, D), lambda i: (i, 0)),
                   pl.BlockSpec((1, D), lambda i: (0, 0))],
        out_shape=[jax.ShapeDtypeStruct((1, 1), F32), jax.ShapeDtypeStruct((T, D), F32),
                   jax.ShapeDtypeStruct((1, D), F32)],
        compiler_params=_cp(1),
    )(x, g, tgt)


def _ffn_bwd(dxo, x, g, gu, win, wout, dep):
    T, D = x.shape
    FB = win.shape[2]
    tm = min(TM_FFN_BWD, T)

    def body(dxo_ref, x_ref, g_ref, gu_ref, win_hbm, wout_hbm, dep_ref,
             dxi_ref, dg_ref, hb_ref, dgu_ref, a_ref, dyb_ref, win_v, wout_v, sems):
        @pl.when(pl.program_id(0) == 0)
        def _():
            _load_ffn_weights(win_hbm, wout_hbm, win_v, wout_v, sems)
            dg_ref[...] = jnp.zeros_like(dg_ref)

        dyb = (0.5 * dxo_ref[...]).astype(BF16)
        dyb_ref[...] = dyb
        dh = jnp.zeros((tm, D), F32)
        for blk in range(2):
            for lo, sz in _lane_chunks(FB):
                cols = pl.ds(blk * FB + lo, sz)
                da = _dot_nt(dyb, wout_v[cols, :])
                gate = gu_ref[0, :, cols].astype(F32)
                up = gu_ref[1, :, cols].astype(F32)
                sg = _fast_sigmoid(gate)
                s = gate * sg
                a_ref[:, cols] = (s * up).astype(BF16)
                dgate = (da * up * (sg + s * (1.0 - sg))).astype(BF16)
                dup = (da * s).astype(BF16)
                dgu_ref[0, :, cols] = dgate
                dgu_ref[1, :, cols] = dup
                dh = dh + _dot_nt(dgate, win_v[blk, :, pl.ds(lo, sz)]) + _dot_nt(dup, win_v[2 + blk, :, pl.ds(lo, sz)])
        dx, dg, h = _rms_bwd_rows(dh, x_ref[...], g_ref[...])
        dxi_ref[...] = dxo_ref[...] + dx
        dg_ref[...] += dg
        hb_ref[...] = h.astype(BF16)

    row = pl.BlockSpec((tm, D), lambda i: (i, 0))
    act = pl.BlockSpec((2, tm, 2 * FB), lambda i: (0, i, 0))
    return pl.pallas_call(
        body, name="ffn_bwd", grid=(T // tm,),
        in_specs=[row, row, pl.BlockSpec((1, D), lambda i: (0, 0)), act, ANY, ANY, ANY],
        out_specs=[row, pl.BlockSpec((1, D), lambda i: (0, 0)), row, act,
                   pl.BlockSpec((tm, 2 * FB), lambda i: (i, 0)), row],
        out_shape=[jax.ShapeDtypeStruct((T, D), F32), jax.ShapeDtypeStruct((1, D), F32),
                   jax.ShapeDtypeStruct((T, D), BF16), jax.ShapeDtypeStruct((2, T, 2 * FB), BF16),
                   jax.ShapeDtypeStruct((T, 2 * FB), BF16), jax.ShapeDtypeStruct((T, D), BF16)],
        scratch_shapes=[pltpu.VMEM(win.shape, BF16), pltpu.VMEM(wout.shape, BF16), pltpu.SemaphoreType.DMA((6,))],
        compiler_params=_cp(1),
    )(dxo, x, g, gu, win, wout, dep)


def _mix_rms_bwd(dxo, x, g, dzs, wts):
    T, D = x.shape
    tm = min(TM, T)
    npair = len(dzs)

    def body(*refs):
        dxo_ref, x_ref, g_ref = refs[:3]
        dz_refs, w_refs = refs[3:3 + npair], refs[3 + npair:3 + 2 * npair]
        dxi_ref, dg_ref, hb_ref = refs[3 + 2 * npair:]

        @pl.when(pl.program_id(0) == 0)
        def _():
            dg_ref[...] = jnp.zeros_like(dg_ref)

        dh = jnp.zeros((tm, D), F32)
        for p in range(npair):
            dh = dh + _dot(dz_refs[p][...], w_refs[p][...])
        dx, dg, h = _rms_bwd_rows(dh, x_ref[...], g_ref[...])
        dxi_ref[...] = dxo_ref[...] + dx
        dg_ref[...] += dg
        hb_ref[...] = h.astype(BF16)

    row = pl.BlockSpec((tm, D), lambda i: (i, 0))
    return pl.pallas_call(
        body, name="mix_rms_bwd", grid=(T // tm,),
        in_specs=[row, row, pl.BlockSpec((1, D), lambda i: (0, 0))]
                 + [pl.BlockSpec((tm, dz.shape[1]), lambda i: (i, 0)) for dz in dzs]
                 + [pl.BlockSpec(w.shape, lambda i: (0, 0)) for w in wts],
        out_specs=[row, pl.BlockSpec((1, D), lambda i: (0, 0)), row],
        out_shape=[jax.ShapeDtypeStruct((T, D), F32), jax.ShapeDtypeStruct((1, D), F32),
                   jax.ShapeDtypeStruct((T, D), BF16)],
        compiler_params=_cp(1),
    )(dxo, x, g, *dzs, *wts)


def _wgrad(name, a, b, a_spec, b_spec, out_shape, out_spec, nblk, dep, acc_shape):
    T = a.shape[0]
    tk = min(TK_WGRAD, T)
    nk = T // tk

    def body(a_ref, b_ref, dep_ref, o_ref, acc_ref):
        k = pl.program_id(1)

        @pl.when(k == 0)
        def _():
            acc_ref[...] = jnp.zeros_like(acc_ref)

        acc_ref[...] += _dot_tn(a_ref[...], b_ref[...])

        @pl.when(k == nk - 1)
        def _():
            o_ref[...] = acc_ref[...].reshape(o_ref.shape).astype(BF16)

    return pl.pallas_call(
        body, name=name, grid=(nblk, nk), in_specs=[a_spec, b_spec, ANY], out_specs=out_spec,
        out_shape=jax.ShapeDtypeStruct(out_shape, BF16), scratch_shapes=[pltpu.VMEM(acc_shape, F32)],
        compiler_params=_cp(2),
    )(a, b, dep)


def _wgrad_ffn_in(hb, dgu, dep):
    T, D = hb.shape
    FB = dgu.shape[2] // 2
    tk = min(TK_WGRAD, T)
    return _wgrad("wgrad_ffn_in", hb, dgu,
                  pl.BlockSpec((tk, D), lambda b, k: (k, 0)),
                  pl.BlockSpec((None, tk, FB), lambda b, k: (b // 2, k, b % 2)),
                  (4, D, FB), pl.BlockSpec((None, D, FB), lambda b, k: (b, 0, 0)), 4, dep, (D, FB))


def _wgrad_ffn_out(a, dyb, dep):
    T, D = dyb.shape
    FB = a.shape[1] // 2
    tk = min(TK_WGRAD, T)
    return _wgrad("wgrad_ffn_out", a, dyb,
                  pl.BlockSpec((tk, FB), lambda b, k: (k, b)),
                  pl.BlockSpec((tk, D), lambda b, k: (k, 0)),
                  (4, FB // 2, D), pl.BlockSpec((2, FB // 2, D), lambda b, k: (b, 0, 0)), 2, dep, (FB, D))


def _wgrad_cat(a_list, b_list):
    T = a_list[0].shape[0]
    tk = min(TK_WGRAD, T)
    nk = T // tk
    na = len(a_list)
    M, N = sum(a.shape[1] for a in a_list), sum(b.shape[1] for b in b_list)

    def body(*refs):
        a_refs, b_refs, o_ref, acc_ref = refs[:na], refs[na:-2], refs[-2], refs[-1]
        k = pl.program_id(0)

        @pl.when(k == 0)
        def _():
            acc_ref[...] = jnp.zeros_like(acc_ref)

        r0 = 0
        for a_ref in a_refs:
            c0 = 0
            for b_ref in b_refs:
                m, n = a_ref.shape[1], b_ref.shape[1]
                acc_ref[r0:r0 + m, c0:c0 + n] += _dot_tn(a_ref[...], b_ref[...])
                c0 += n
            r0 += a_ref.shape[1]

        @pl.when(k == nk - 1)
        def _():
            o_ref[...] = acc_ref[...].astype(BF16)

    return pl.pallas_call(
        body, name="wgrad_cat", grid=(nk,),
        in_specs=[pl.BlockSpec((tk, v.shape[1]), lambda k: (k, 0)) for v in list(a_list) + list(b_list)],
        out_specs=pl.BlockSpec((M, N), lambda k: (0, 0)),
        out_shape=jax.ShapeDtypeStruct((M, N), BF16), scratch_shapes=[pltpu.VMEM((M, N), F32)],
        compiler_params=_cp(1),
    )(*a_list, *b_list)


def _mixout_bwd(dxo, wo):
    T, D = dxo.shape
    tm = min(TM_MIX, T)
    A = ATTN_W
    C = wo.shape[0] - A

    def body(dxo_ref, w_ref, dyb_ref, da_ref, dc_ref):
        dyb = dxo_ref[...].astype(BF16)
        dyb_ref[...] = dyb
        da_ref[...] = _dot_nt(dyb, w_ref[:A, :]).astype(BF16)
        dc_ref[...] = _dot_nt(dyb, w_ref[A:, :])

    return pl.pallas_call(
        body, name="mixout_bwd", grid=(T // tm,),
        in_specs=[pl.BlockSpec((tm, D), lambda i: (i, 0)), pl.BlockSpec(wo.shape, lambda i: (0, 0))],
        out_specs=[pl.BlockSpec((tm, D), lambda i: (i, 0)), pl.BlockSpec((tm, A), lambda i: (i, 0)),
                   pl.BlockSpec((tm, C), lambda i: (i, 0))],
        out_shape=[jax.ShapeDtypeStruct((T, D), BF16), jax.ShapeDtypeStruct((T, A), BF16),
                   jax.ShapeDtypeStruct((T, C), F32)],
        compiler_params=_cp(1),
    )(dxo, wo)


def _conv_bwd(dconv, ypre, u, w, lg, lb):
    T, CH = dconv.shape
    tm = min(TM, T)
    n = tm + HALO
    hb = tm // HALO
    nt = T // tm
    nchunk = tm // CONV_ROWS

    def body(dc_ref, dcn_ref, yp_ref, ypn_ref, uc_ref, up_ref, w_ref, lg_ref, lb_ref,
             du_ref, dw_ref, dvec_ref, zs_ref, zsh_ref, dy_ref, dysh_ref, dz_ref, dwacc_ref):
        i = pl.program_id(0)

        @pl.when(i == 0)
        def _():
            dwacc_ref[...] = jnp.zeros_like(dwacc_ref)
            dvec_ref[...] = jnp.zeros_like(dvec_ref)

        g, bb = lg_ref[...], lb_ref[...]

        def ln_bwd(dc, yp):
            mu = jnp.mean(yp, axis=-1, keepdims=True)
            d = yp - mu
            rs = lax.rsqrt(jnp.mean(d * d, axis=-1, keepdims=True) + EPS)
            yn = d * rs
            o = yn * g + bb
            sg = _fast_sigmoid(o)
            do = dc * (sg * (1.0 + o * (1.0 - sg)))
            dyn = do * g
            dyp = rs * (dyn - jnp.mean(dyn, axis=-1, keepdims=True)
                        - yn * jnp.mean(dyn * yn, axis=-1, keepdims=True))
            return dyp, do, yn

        dyp, do, yn = ln_bwd(dc_ref[...], yp_ref[...])
        dvec_ref[0:1, :] += jnp.sum(dyp, axis=0, keepdims=True)
        dvec_ref[1:2, :] += jnp.sum(do * yn, axis=0, keepdims=True)
        dvec_ref[2:3, :] += jnp.sum(do, axis=0, keepdims=True)
        dy_ref[0:tm] = dyp
        dyh, _, _ = ln_bwd(dcn_ref[...], ypn_ref[...])
        dy_ref[tm:] = jnp.where(i < nt - 1, dyh, 0.0)
        _shift_copies(dy_ref, dysh_ref, n - 8)
        _fill_z(zs_ref, zsh_ref, uc_ref, up_ref, i, CH, n)

        def chunk(ci, carry):
            c0 = pl.multiple_of(ci * CONV_ROWS, CONV_ROWS)
            acc = jnp.zeros((CONV_ROWS, CH), F32)
            for k in range(CONV_W):
                acc = acc + w_ref[k:k + 1, :] * _tap(dy_ref, dysh_ref, CONV_W - 1 - k, c0)
            dz_ref[pl.ds(c0, CONV_ROWS), :] = acc
            dyc = dy_ref[pl.ds(c0, CONV_ROWS), :]
            for k in range(CONV_W):
                prod = dyc * _tap(zs_ref, zsh_ref, HALO - (CONV_W - 1) + k, c0)
                dwacc_ref[k] += jnp.sum(prod.reshape(CONV_ROWS // 8, 8, CH), axis=0)
            return carry

        lax.fori_loop(0, nchunk, chunk, 0)

        @pl.when(i == nt - 1)
        def _():
            dw_ref[...] = jnp.sum(dwacc_ref[...], axis=1)

        uc = uc_ref[...]
        a = uc[:, :CH]
        sg = _fast_sigmoid(uc[:, CH:])
        dz = dz_ref[...]
        du_ref[:, :CH] = (dz * sg).astype(BF16)
        du_ref[:, CH:] = (dz * a * sg * (1.0 - sg)).astype(BF16)

    cur = lambda c: pl.BlockSpec((tm, c), lambda i: (i, 0))
    nxt = lambda c: pl.BlockSpec((HALO, c), lambda i: (jnp.minimum((i + 1) * hb, T // HALO - 1), 0))
    vec = pl.BlockSpec((1, CH), lambda i: (0, 0))
    return pl.pallas_call(
        body, name="conv_bwd", grid=(nt,),
        in_specs=[cur(CH), nxt(CH), cur(CH), nxt(CH), cur(2 * CH),
                  pl.BlockSpec((HALO, 2 * CH), lambda i: (jnp.maximum(i * hb - 1, 0), 0)),
                  pl.BlockSpec((CONV_W, CH), lambda i: (0, 0)), vec, vec],
        out_specs=[pl.BlockSpec((tm, 2 * CH), lambda i: (i, 0)), pl.BlockSpec((32, CH), lambda i: (0, 0)),
                   pl.BlockSpec((8, CH), lambda i: (0, 0))],
        out_shape=[jax.ShapeDtypeStruct((T, 2 * CH), BF16), jax.ShapeDtypeStruct((32, CH), F32),
                   jax.ShapeDtypeStruct((8, CH), F32)],
        scratch_shapes=[pltpu.VMEM((n, CH), F32), pltpu.VMEM((7, n - 8, CH), F32),
                        pltpu.VMEM((n, CH), F32), pltpu.VMEM((7, n - 8, CH), F32), pltpu.VMEM((tm, CH), F32),
                        pltpu.VMEM((32, 8, CH), F32)],
        compiler_params=_cp(1),
    )(dconv, dconv, ypre, ypre, u, u, w, lg, lb)


def _attn_bwd(sinks, tab, qkv, dattn):
    T = qkv.shape[0]
    nb = T // WINDOW

    def body(sink_ref, tab_ref, q_ref, kvp_ref, kvc_ref, do_ref, dq_ref, dkv_ref, dsk_ref, carry_ref):
        n = pl.program_id(0)

        @pl.when(n == 0)
        def _():
            dsk_ref[...] = jnp.zeros_like(dsk_ref)
            carry_ref[...] = jnp.zeros_like(carry_ref)

        @pl.when(n < nb)
        def _():
            other = jnp.minimum(n, 1)
            for g in range(N_KV):
                qs = _stack_heads(q_ref, g)
                dos = _stack_heads(do_ref, g)
                k = _band(kvp_ref, kvc_ref, g * HEAD_DIM)
                v = _band(kvp_ref, kvc_ref, KV_W + g * HEAD_DIM)
                p, ps = _attn_probs(qs, k, tab_ref[other, g], _sink_col(sink_ref, g))
                dp = _dot_nt(dos, v)
                delta = jnp.sum(p * dp, axis=-1, keepdims=True)
                dsb = (p * (dp - delta)).astype(BF16)
                dsink = -ps * delta
                dqs = _dot(dsb, k) * SCALE
                dk = _dot_tn(dsb, qs) * SCALE
                dv = _dot_tn(p.astype(BF16), dos)
                for i in range(GROUP):
                    h = GROUP * g + i
                    dq_ref[:, h * HEAD_DIM:(h + 1) * HEAD_DIM] = dqs[i * WINDOW:(i + 1) * WINDOW].astype(BF16)
                    dsk_ref[h:h + 1, :] += jnp.sum(dsink[i * WINDOW:(i + 1) * WINDOW], axis=0, keepdims=True)
                for off, d in ((g * HEAD_DIM, dk), (KV_W + g * HEAD_DIM, dv)):
                    dkv_ref[:, off:off + HEAD_DIM] = (carry_ref[:, off:off + HEAD_DIM] + d[:WINDOW]).astype(BF16)
                    carry_ref[:, off:off + HEAD_DIM] = d[WINDOW:]

        @pl.when(n == nb)
        def _():
            dkv_ref[...] = carry_ref[...].astype(BF16)

    last = nb - 1
    return pl.pallas_call(
        body, name="attn_bwd", grid=(nb + 1,),
        in_specs=[pl.BlockSpec(memory_space=pltpu.SMEM),
                  pl.BlockSpec(tab.shape, lambda n: (0, 0, 0, 0)),
                  pl.BlockSpec((WINDOW, ATTN_W), lambda n: (jnp.minimum(n, last), 0)),
                  pl.BlockSpec((WINDOW, 2 * KV_W), lambda n: (jnp.clip(n - 1, 0, last), 2)),
                  pl.BlockSpec((WINDOW, 2 * KV_W), lambda n: (jnp.minimum(n, last), 2)),
                  pl.BlockSpec((WINDOW, ATTN_W), lambda n: (jnp.minimum(n, last), 0))],
        out_specs=[pl.BlockSpec((WINDOW, ATTN_W), lambda n: (jnp.minimum(n, last), 0)),
                   pl.BlockSpec((WINDOW, 2 * KV_W), lambda n: (jnp.maximum(n - 1, 0), 0)),
                   pl.BlockSpec((8, LANES), lambda n: (0, 0))],
        out_shape=[jax.ShapeDtypeStruct((T, ATTN_W), BF16), jax.ShapeDtypeStruct((T, 2 * KV_W), BF16),
                   jax.ShapeDtypeStruct((8, LANES), F32)],
        scratch_shapes=[pltpu.VMEM((WINDOW, 2 * KV_W), F32)],
        compiler_params=_cp(1),
    )(sinks, tab, qkv, qkv, qkv, dattn)


def _pack(arrs):
    flat = jnp.concatenate([a.reshape(-1) for a in arrs])
    pad = -flat.shape[0] % (8 * LANES)
    return jnp.pad(flat, (0, pad)).reshape(1, -1, LANES)


def _unpack(packed, like):
    flat = packed.reshape(-1)
    out, off = [], 0
    for a in like:
        out.append(flat[off:off + a.size].reshape(a.shape))
        off += a.size
    return out


def kernel(x, norm_ffn1, w_ffn1_in, w_ffn1_out, norm_mix, w_in, sinks, w_dw, b_dw, conv_ln_g, conv_ln_b, w_out, norm_ffn2, w_ffn2_in, w_ffn2_out, final_norm, loss_target, m_norm_ffn1, m_w_ffn1_in, m_w_ffn1_out, m_norm_mix, m_w_in, m_sinks, m_w_dw, m_b_dw, m_conv_ln_g, m_conv_ln_b, m_w_out, m_norm_ffn2, m_w_ffn2_in, m_w_ffn2_out, m_final_norm, v_norm_ffn1, v_w_ffn1_in, v_w_ffn1_out, v_norm_mix, v_w_in, v_sinks, v_w_dw, v_b_dw, v_conv_ln_g, v_conv_ln_b, v_w_out, v_norm_ffn2, v_w_ffn2_in, v_w_ffn2_out, v_final_norm):
    L, D = norm_ffn1.shape
    T = x.shape[1]
    FB = w_ffn1_in.shape[2]
    CH = b_dw.shape[1]
    QKV = ATTN_W + 2 * KV_W
    xs = x.reshape(T, D)
    tgt = loss_target.reshape(T, D)
    cx, cy, cc = lax.axis_index("x"), lax.axis_index("y"), lax.axis_index("c")
    chip = 2 * cx + cy
    cidx = cc.reshape(1).astype(jnp.int32)
    tr = lambda a_: jnp.transpose(a_, (0, 2, 1))
    big_w = (w_ffn1_in, w_ffn1_out, tr(w_in), w_out, w_ffn2_in, w_ffn2_out)
    big_m = (m_w_ffn1_in, m_w_ffn1_out, tr(m_w_in), m_w_out, m_w_ffn2_in, m_w_ffn2_out)
    big_v = (v_w_ffn1_in, v_w_ffn1_out, tr(v_w_in), v_w_out, v_w_ffn2_in, v_w_ffn2_out)
    NW = len(big_w) + 1

    def own_slot(a, slots=4, idx=chip):
        return lax.dynamic_update_index_in_dim(lax.empty((slots,) + a.shape, a.dtype), a, idx, 0)

    def shards(l, tok):
        return [own_slot((w_[l] + tok[0, 0]).astype(BF16)) for w_ in big_w] + [own_slot(w_dw[l] + tok[0, 0])]

    def gather_start(lands, tok):
        return _xchg_start("gather_start", [], lands, _gather_plan, tok)

    def gather_arrived(started, after, n, taps):
        _, lands, tok = _xchg_wait("gather_wait", started, 0, n, _gather_plan, after)
        return _xchg_start("gshare_start", [], lands[:-1] if taps else lands, _gshare_plan, tok, "sibling3"), lands[-1]

    def shared_weights(shared, after, n):
        _, mats, tok = _xchg_wait("gshare_wait", shared, 0, n, _gshare_plan, after, "sibling3")
        return mats, tok

    row = lambda a, l: a[l].reshape(1, -1)
    tab = _attn_bias_table()
    NB = len(big_w)

    saved, W = [], []
    zero_tok = jnp.zeros((8, LANES), F32)
    src0 = shards(0, zero_tok)
    started = gather_start(src0[:2], zero_tok)
    rest0 = gather_start(src0[2:], started[-1])
    cast = [None] + [shards(l, rest0[-1]) for l in range(1, L)]
    shared, _ = gather_arrived(started, [xs] + [a_ for c_ in cast[1:] for a_ in c_], 2, False)
    after = [shared[-1]]
    for l in range(L):
        mats, tok = shared_weights(shared, after, 2 if l == 0 else NB)
        started = None
        if l + 1 < L:
            started = gather_start(cast[l + 1], tok)
            tok = started[-1]
        x0 = xs
        x1, gu1 = _ffn_fwd(x0, row(norm_ffn1, l) + tok[0, 0], mats[0], mats[1].reshape(2 * FB, D))
        gm_row = row(norm_mix, l)
        if l == 0:
            shared, gdw = gather_arrived(rest0, [x1], NW - 2, True)
            rest, tok = shared_weights(shared, [shared[-1]], NB - 2)
            mats = list(mats) + list(rest)
            gm_row = gm_row + tok[0, 0]
        g1i, g1o, gi, go, g2i, g2o = mats
        w = dict(f1i=g1i, f1o=g1o.reshape(2 * FB, D), f2i=g2i, f2o=g2o.reshape(2 * FB, D),
                 wit=gi.reshape(-1, D), wo=go.reshape(-1, D),
                 wdw=jnp.transpose(gdw, (1, 0, 2)).reshape(CONV_W, CH))
        W.append(w)
        qkv, u = _mixproj_fwd(x1, gm_row, w["wit"])
        attn = _attn_fwd(row(sinks, l), tab, qkv)
        conv, ypre = _conv_fwd(u, w["wdw"], row(b_dw, l), row(conv_ln_g, l), row(conv_ln_b, l))
        x2 = _mixout_fwd(x1, attn, conv, w["wo"])
        g2_row = row(norm_ffn2, l)
        if started is not None and l > 0:
            shared, gdw = gather_arrived(started, [x2], NW, True)
            g2_row = g2_row + shared[-1][0, 0]
        xs, gu2 = _ffn_fwd(x2, g2_row, w["f2i"], w["f2o"])
        if started is not None and l == 0:
            shared, gdw = gather_arrived(started, [xs], NW, True)
        saved.append((x0, gu1, x1, qkv, u, attn, conv, ypre, x2, gu2))
        after = [xs]

    loss_part, dx, d_final = _loss_head(xs, final_norm.reshape(1, D), tgt)
    loss = lax.psum(loss_part[0, 0], ("x", "y", "c"))

    bufs = [[lax.empty(w_.shape, F32) for _ in range(4)] for w_ in big_w]
    d_n1, d_nm, d_n2 = [None] * L, [None] * L, [None] * L
    d_sk, d_bdw, d_lg, d_lb, d_wdw = [None] * L, [None] * L, [None] * L, [None] * L, [None] * L

    me_idx = 4 * cx + 2 * cy + cc

    def reduce_start(gs):
        lands = []
        for g in gs:
            h = g.shape[1] // 2
            mine = lax.dynamic_slice(g, (chip, cc * h, 0), (1, h, g.shape[2]))[0]
            lands.append(own_slot(mine, 8, me_idx))
        return _xchg_start("rs_start", gs, lands, _rs_plan, zero_tok, "all")

    def share_start(rs_started, after, n):
        _, qs, tok = _xchg_wait("rs_wait", rs_started, n, n, _rs_plan, after, "all")
        return _xchg_start("qshare_start", qs, [lax.empty(q.shape, q.dtype) for q in qs], _whole_plan, tok, "sibling")

    def finish(l, shared, after, idxs):
        q_own, q_sib, _ = _xchg_wait("qshare_wait", shared, len(idxs), len(idxs), _whole_plan, after, "sibling")
        for k, t in enumerate(idxs):
            bufs[t] = _adamw_layer(cidx, q_own[k], q_sib[k], big_w[t], big_m[t], big_v[t], bufs[t], l)

    ALL = list(range(NB))
    EARLY, LATE = ALL[2:], ALL[:2]
    rs_list, shares = [], []
    tok = zero_tok
    for l in reversed(range(L)):
        w = W[l]
        x0, gu1, x1, qkv, u, attn, conv, ypre, x2, gu2 = saved[l]
        dx, d_n2[l], hb, dgu, a, dyb = _ffn_bwd(dx, x2, row(norm_ffn2, l), gu2, w["f2i"], w["f2o"], tok)
        g_f2i, g_f2o = _wgrad_ffn_in(hb, dgu, tok), _wgrad_ffn_out(a, dyb, tok)
        lg_row = row(conv_ln_g, l)
        if len(rs_list) >= 2:
            pl_, st_ = rs_list[-2]
            shares.append((pl_, share_start(st_, [g_f2o], NB)))
            lg_row = lg_row + shares[-1][1][-1][0, 0]
        dyb, dattn, dconv = _mixout_bwd(dx, w["wo"])
        g_wo = _wgrad_cat([attn, conv], [dyb]).reshape(4, -1, D)
        du, dwdw, dvec = _conv_bwd(dconv, ypre, u, w["wdw"], lg_row, row(conv_ln_b, l))
        d_wdw[l], d_bdw[l], d_lg[l], d_lb[l] = dwdw[:CONV_W], dvec[0], dvec[1], dvec[2]
        dq, dkv, dsk = _attn_bwd(row(sinks, l), tab, qkv, dattn)
        d_sk[l] = dsk[:, 0]
        wit = w["wit"]
        dx, d_nm[l], hb = _mix_rms_bwd(dx, x1, row(norm_mix, l), [dq, dkv, du],
                                       [wit[:ATTN_W], wit[ATTN_W:QKV], wit[QKV:]])
        g_wi = _wgrad_cat([dq, dkv, du], [hb]).reshape(4, -1, D)
        if l == 0:
            rs_early = reduce_start([g_wi, g_wo, g_f2i, g_f2o])
            tok = rs_early[-1]
        dx, d_n1[l], hb, dgu, a, dyb = _ffn_bwd(dx, x0, row(norm_ffn1, l), gu1, w["f1i"], w["f1o"], tok)
        g_f1i, g_f1o = _wgrad_ffn_in(hb, dgu, tok), _wgrad_ffn_out(a, dyb, tok)
        rs_started = reduce_start([g_f1i, g_f1o] if l == 0 else [g_f1i, g_f1o, g_wi, g_wo, g_f2i, g_f2o])
        tok = rs_started[-1]
        rs_list.append((l, rs_started))
    grad_x = dx.reshape(x.shape)

    small_g = [jnp.concatenate(d, axis=0) for d in (d_n1, d_nm, d_n2)] + [d_final, jnp.stack(d_sk)] + \
              [jnp.stack(d) for d in (d_bdw, d_lg, d_lb, d_wdw)]
    packed = _pack(small_g)[0]
    small_started = _xchg_start("small_start", [packed], [own_slot(packed, 8, 4 * cx + 2 * cy + cc)], _slot_plan, tok, "all")

    rs_late = rs_list.pop()[1]
    after = [small_started[-1]]
    if len(rs_list) > len(shares):
        pl_, st_ = rs_list[len(shares)]
        shares.append((pl_, share_start(st_, after, NB)))
        after = [shares[-1][1][-1]]
    if shares:
        finish(*shares.pop(0), after, ALL)
        after = [b_[0] for b_ in bufs]
    sh_early = share_start(rs_early, after, len(EARLY))
    after = [sh_early[-1]]
    sh_late = None
    for l, sh in shares:
        finish(l, sh, after, ALL)
        after = [b_[0] for b_ in bufs]
        if sh_late is None:
            sh_late = share_start(rs_late, after, len(LATE))
            after = [sh_late[-1]]
    if sh_late is None:
        sh_late = share_start(rs_late, after, len(LATE))
        after = [sh_late[-1]]
    _, (slots,), _ = _xchg_wait("small_wait", small_started, 1, 1, _slot_plan, after, "all")
    small_sum = _unpack(_sum_slots(slots), small_g)
    g_wdw = lax.dynamic_slice_in_dim(small_sum[8], chip * w_dw.shape[2], w_dw.shape[2], axis=2)
    small_g = [small_sum[0], small_sum[1], small_sum[2], small_sum[3].reshape(D), small_sum[4],
               small_sum[5], small_sum[6], small_sum[7], g_wdw]
    small_w = (norm_ffn1, norm_mix, norm_ffn2, final_norm, sinks, b_dw, conv_ln_g, conv_ln_b, w_dw)
    small_m = (m_norm_ffn1, m_norm_mix, m_norm_ffn2, m_final_norm, m_sinks, m_b_dw, m_conv_ln_g, m_conv_ln_b, m_w_dw)
    small_v = (v_norm_ffn1, v_norm_mix, v_norm_ffn2, v_final_norm, v_sinks, v_b_dw, v_conv_ln_g, v_conv_ln_b, v_w_dw)
    upd = _adamw(_pack(small_g), _pack(small_w), _pack(small_m), _pack(small_v))
    small_upd = [_unpack(u_, small_w) for u_ in upd]
    finish(0, sh_early, [upd[0]], EARLY)
    finish(0, sh_late, [bufs[t][0] for t in EARLY], LATE)

    order = ("norm_ffn1", "w_ffn1_in", "w_ffn1_out", "norm_mix", "w_in", "sinks", "w_dw", "b_dw", "conv_ln_g",
             "conv_ln_b", "w_out", "norm_ffn2", "w_ffn2_in", "w_ffn2_out", "final_norm")
    small_names = ("norm_ffn1", "norm_mix", "norm_ffn2", "final_norm", "sinks", "b_dw", "conv_ln_g", "conv_ln_b", "w_dw")
    big_names = ("w_ffn1_in", "w_ffn1_out", "w_in", "w_out", "w_ffn2_in", "w_ffn2_out")
    grads, deltas, new_m, new_v = {}, {}, {}, {}
    for i, nme in enumerate(small_names):
        grads[nme], deltas[nme], new_m[nme], new_v[nme] = small_g[i], small_upd[0][i], small_upd[1][i], small_upd[2][i]
    for i, nme in enumerate(big_names):
        grads[nme], deltas[nme], new_m[nme], new_v[nme] = [tr(b_) for b_ in bufs[i]] if nme == "w_in" else bufs[i]
    return (loss, grad_x, *[grads[n] for n in order], *[deltas[n] for n in order],
            *[new_m[n] for n in order], *[new_v[n] for n in order])
```

```python
import functools

import jax
import jax.numpy as jnp
from jax import lax
from jax.experimental import pallas as pl
from jax.experimental.pallas import tpu as pltpu

F32, BF16 = jnp.float32, jnp.bfloat16
EPS = 1e-6
NEG_INF = -1e30
HEAD_DIM = 64
N_HEADS = 8
N_KV = 2
GROUP = N_HEADS // N_KV
WINDOW = 128
ATTN_W = N_HEADS * HEAD_DIM
KV_W = N_KV * HEAD_DIM
CONV_W = 31
HALO = 32
CONV_ROWS = 32
SCALE = 1.0 / 8.0
ADAM_LR, ADAM_B1, ADAM_B2, ADAM_EPS, ADAM_WD, ADAM_STEP = 0.001, 0.9, 0.999, 1e-08, 0.01, 10
TM = 512
TM_FFN_BWD = 256
TK_WGRAD = 2048
TM_MIX = 1024
LANES = 128
VMEM_LIMIT = 52 * 1024 * 1024
MESH = pl.DeviceIdType.MESH
ANY = pl.BlockSpec(memory_space=pl.ANY)
HBM = pl.BlockSpec(memory_space=pltpu.HBM)
SEM = pl.BlockSpec(memory_space=pltpu.SEMAPHORE)
VMEM = pl.BlockSpec(memory_space=pltpu.VMEM)
EFFECT = pltpu.SideEffectType.DATAFLOW_SIDE_EFFECTING
TOKEN = jax.ShapeDtypeStruct((8, LANES), F32)


def _cp(n):
    return pltpu.CompilerParams(dimension_semantics=("arbitrary",) * n, vmem_limit_bytes=VMEM_LIMIT)


def _dot(a, b):
    return jnp.dot(a, b, preferred_element_type=F32)


def _dot_nt(a, b):
    return lax.dot_general(a, b, (((1,), (1,)), ((), ())), preferred_element_type=F32)


def _dot_tn(a, b):
    return lax.dot_general(a, b, (((0,), (0,)), ((), ())), preferred_element_type=F32)


def _place():
    x, y, c = lax.axis_index("x"), lax.axis_index("y"), lax.axis_index("c")
    chips = [(1 - x, y), (x, 1 - y), (1 - x, 1 - y)]
    return x, y, c, chips


def _rcopy(src, dst, send_sems, recv_sems, k, dev):
    return pltpu.make_async_remote_copy(src_ref=src, dst_ref=dst, send_sem=send_sems.at[k],
                                        recv_sem=recv_sems.at[k], device_id=dev, device_id_type=MESH)


def _hbm(a):
    return pltpu.with_memory_space_constraint(a, pltpu.HBM)


PEERS = {"chips": 3, "sibling": 1, "sibling3": 3, "all": 7}


def _targets(mode):
    x, y, c, chips = _place()
    b = 2 * x + y
    if mode == "chips":
        return b, c, [((px, py, c), 2 * px + py) for px, py in chips]
    if mode == "sibling":
        return b, c, [((x, y, 1 - c), b)]
    if mode == "sibling3":
        return b, c, [((x, y, 1 - c), 2 * px + py) for px, py in chips]
    flip = lambda v, f: 1 - v if f else v
    devs = [(flip(x, k >> 2 & 1), flip(y, k >> 1 & 1), flip(c, k & 1)) for k in range(1, 8)]
    return 4 * x + 2 * y + c, c, [(d, 4 * d[0] + 2 * d[1] + d[2]) for d in devs]


def _xchg_start(name, srcs, lands, plan, dep, mode="chips"):
    ns, nl, npeer = len(srcs), len(lands), PEERS[mode]

    def body(*refs):
        land = refs[ns:ns + nl]
        src = refs[:ns] if ns else land
        send_sems, recv_sems, token = refs[ns + nl + 1], refs[ns + nl + 2], refs[-1]
        me, c, peers = _targets(mode)
        for t in range(nl):
            for j, (dev, tag) in enumerate(peers):
                s, d, _ = plan(src[t], land[t], t, me, c, tag)
                _rcopy(s, d, send_sems, recv_sems, npeer * t + j, dev).start()
        token[...] = jnp.zeros_like(token)

    arrs = list(srcs) + list(lands)
    return pl.pallas_call(
        body, name=name,
        out_shape=(pltpu.SemaphoreType.DMA((npeer * nl,)), pltpu.SemaphoreType.DMA((npeer * nl,)),
                   *[pltpu.HBM(a.shape, a.dtype) for a in arrs], TOKEN),
        in_specs=[HBM] * (ns + nl) + [ANY], out_specs=(SEM, SEM, *[HBM] * (ns + nl), VMEM),
        input_output_aliases={i: 2 + i for i in range(ns + nl)},
        compiler_params=pltpu.CompilerParams(has_side_effects=EFFECT),
    )(*[_hbm(a) for a in arrs], dep)


def _xchg_wait(name, started, ns, nl, plan, after, mode="chips"):
    send_sems, recv_sems, thru = started[0], started[1], started[2:2 + ns + nl]
    npeer = PEERS[mode]

    def body(*refs):
        land = refs[ns:ns + nl]
        src = refs[:ns] if ns else land
        send_sems, recv_sems, token = refs[ns + nl], refs[ns + nl + 1], refs[-1]
        me, c, peers = _targets(mode)
        for t in range(nl):
            for j, (dev, tag) in enumerate(peers):
                s, _, a = plan(src[t], land[t], t, me, c, tag)
                cp = _rcopy(s, a, send_sems, recv_sems, npeer * t + j, dev)
                cp.wait_send()
                cp.wait_recv()
        token[...] = jnp.zeros_like(token)

    out = pl.pallas_call(
        body, name=name,
        out_shape=(*[pltpu.HBM(a.shape, a.dtype) for a in thru], TOKEN),
        in_specs=[HBM] * (ns + nl) + [SEM, SEM] + [ANY] * len(after), out_specs=(*[HBM] * (ns + nl), VMEM),
        input_output_aliases={i: i for i in range(ns + nl)},
        compiler_params=pltpu.CompilerParams(has_side_effects=EFFECT),
    )(*thru, send_sems, recv_sems, *after)
    return out[:ns], out[ns:ns + nl], out[-1]


def _half(ref_rows, which):
    h = ref_rows // 2
    return pl.ds(which * h, h)


def _gather_plan(src, land, t, b, c, pb):
    if land.shape[1] % 2 == 0:
        hs = _half(land.shape[1], c)
        return land.at[b, hs], land.at[b, hs], land.at[pb, hs]
    return land.at[b], land.at[b], land.at[pb]


def _gshare_plan(src, land, t, b, c, pb):
    return land.at[pb, _half(land.shape[1], c)], land.at[pb, _half(land.shape[1], c)], land.at[pb, _half(land.shape[1], 1 - c)]


def _rs_plan(src, land, t, me, c, tag):
    h = src.shape[1] // 2
    return src.at[tag // 2, pl.ds((tag % 2) * h, h), :], land.at[me], land.at[tag]


def _rows_block(h, cap=512):
    for rb in range(min(h, cap) // 16 * 16, 0, -16):
        if h % rb == 0:
            return rb
    return h


def _whole_plan(src, land, t, me, c, tag):
    return src, land, land


def _slot_plan(src, land, t, me, c, tag):
    return src, land.at[me], land.at[tag]


def _adam_update(gg, w, m, v):
    m2 = ADAM_B1 * m + (1.0 - ADAM_B1) * gg
    v2 = ADAM_B2 * v + (1.0 - ADAM_B2) * (gg * gg)
    mh = m2 / (1.0 - ADAM_B1 ** ADAM_STEP)
    vh = v2 / (1.0 - ADAM_B2 ** ADAM_STEP)
    return -ADAM_LR * (mh / (jnp.sqrt(vh) + ADAM_EPS) + ADAM_WD * w), m2, v2


def _adamw_layer(cidx, q_own, q_sib, w, m, v, bufs, l):
    L, R, C = w.shape
    h = R // 2
    rb = _rows_block(h, 256)
    nr = h // rb

    def body(c_ref, qo_ref, qs_ref, w_ref, m_ref, v_ref, *rest):
        g_ref, d_ref, mo_ref, vo_ref = rest[-4:]
        own = pl.program_id(0) == c_ref[0]
        gg = jnp.zeros((rb, C), F32)
        for s in range(8):
            gg = gg + jnp.where(own, qo_ref[s], qs_ref[s]).astype(F32)
        g_ref[...] = gg
        d_ref[...], mo_ref[...], vo_ref[...] = _adam_update(gg, w_ref[...], m_ref[...], v_ref[...])

    q_own_spec = pl.BlockSpec((8, rb, C), lambda hh, i, c: (0, jnp.where(hh == c[0], i, 0), 0))
    q_sib_spec = pl.BlockSpec((8, rb, C), lambda hh, i, c: (0, jnp.where(hh == c[0], 0, i), 0))
    wspec = pl.BlockSpec((None, rb, C), lambda hh, i, c: (l, hh * nr + i, 0))
    return pl.pallas_call(
        body, name="adamw_layer", out_shape=[jax.ShapeDtypeStruct(w.shape, F32)] * 4,
        grid_spec=pltpu.PrefetchScalarGridSpec(
            num_scalar_prefetch=1, grid=(2, nr),
            in_specs=[q_own_spec, q_sib_spec, wspec, wspec, wspec] + [ANY] * 4, out_specs=[wspec] * 4),
        input_output_aliases={6 + k: k for k in range(4)},
        compiler_params=_cp(2),
    )(cidx, q_own, q_sib, w, m, v, *bufs)


def _adamw(g, w, m, v):
    L, R, C = g.shape
    rb = _rows_block(R)

    def body(g_ref, w_ref, m_ref, v_ref, d_ref, mo_ref, vo_ref):
        d_ref[...], mo_ref[...], vo_ref[...] = _adam_update(g_ref[...], w_ref[...], m_ref[...], v_ref[...])

    spec = pl.BlockSpec((None, rb, C), lambda l, i: (l, i, 0))
    return pl.pallas_call(
        body, name="adamw", grid=(L, R // rb), in_specs=[spec] * 4, out_specs=[spec] * 3,
        out_shape=[jax.ShapeDtypeStruct(g.shape, F32)] * 3, compiler_params=_cp(2),
    )(g, w, m, v)


def _sum_slots(buf):
    def body(b_ref, o_ref):
        acc = b_ref[0]
        for k in range(1, 8):
            acc = acc + b_ref[k]
        o_ref[...] = acc

    return pl.pallas_call(body, name="sum_slots", in_specs=[VMEM], out_specs=VMEM,
                          out_shape=jax.ShapeDtypeStruct(buf.shape[1:], F32))(buf)


def _rms(xf, g):
    r = lax.rsqrt(jnp.mean(xf * xf, axis=-1, keepdims=True) + EPS)
    return xf * r, r


def _lane_chunks(n):
    lo = (n // LANES + 1) // 2 * LANES
    return ((0, lo), (lo, n - lo))


def _load_ffn_weights(win_hbm, wout_hbm, win_v, wout_v, sems):
    fb = win_v.shape[2]
    loads = [pltpu.make_async_copy(win_hbm.at[k], win_v.at[k], sems.at[k]) for k in range(4)]
    loads += [pltpu.make_async_copy(wout_hbm.at[pl.ds(k * fb, fb)], wout_v.at[pl.ds(k * fb, fb)], sems.at[4 + k])
              for k in range(2)]
    for cp in loads:
        cp.start()
    for cp in loads:
        cp.wait()


def _fast_sigmoid(v):
    return pl.reciprocal(1.0 + jnp.exp(-v), approx=True)


def _ffn_fwd(x, g, win, wout):
    T, D = x.shape
    FB = win.shape[2]
    tm = min(TM, T)

    def body(x_ref, g_ref, win_hbm, wout_hbm, xo_ref, gu_ref, win_v, wout_v, sems):
        @pl.when(pl.program_id(0) == 0)
        def _():
            _load_ffn_weights(win_hbm, wout_hbm, win_v, wout_v, sems)

        xf = x_ref[...]
        xh, _ = _rms(xf, None)
        h = (xh * g_ref[...]).astype(BF16)
        acc = jnp.zeros((tm, D), F32)
        for blk in range(2):
            for lo, sz in _lane_chunks(FB):
                cols = pl.ds(blk * FB + lo, sz)
                gate = _dot(h, win_v[blk, :, pl.ds(lo, sz)])
                up = _dot(h, win_v[2 + blk, :, pl.ds(lo, sz)])
                gu_ref[0, :, cols] = gate.astype(BF16)
                gu_ref[1, :, cols] = up.astype(BF16)
                a = (gate * _fast_sigmoid(gate) * up).astype(BF16)
                acc = acc + _dot(a, wout_v[cols, :])
        xo_ref[...] = xf + 0.5 * acc

    row = pl.BlockSpec((tm, D), lambda i: (i, 0))
    return pl.pallas_call(
        body, name="ffn_fwd", grid=(T // tm,),
        in_specs=[row, pl.BlockSpec((1, D), lambda i: (0, 0)), ANY, ANY],
        out_specs=[row, pl.BlockSpec((2, tm, 2 * FB), lambda i: (0, i, 0))],
        out_shape=[jax.ShapeDtypeStruct((T, D), F32), jax.ShapeDtypeStruct((2, T, 2 * FB), BF16)],
        scratch_shapes=[pltpu.VMEM(win.shape, BF16), pltpu.VMEM(wout.shape, BF16), pltpu.SemaphoreType.DMA((6,))],
        compiler_params=_cp(1),
    )(x, g, win, wout)


def _mixproj_fwd(x, g, wt):
    T, D = x.shape
    W = wt.shape[0]
    QKV = ATTN_W + 2 * KV_W
    tm = min(TM_MIX, T)

    def body(x_ref, g_ref, w_ref, qkv_ref, u_ref):
        xh, _ = _rms(x_ref[...], None)
        h = (xh * g_ref[...]).astype(BF16)
        qkv_ref[...] = _dot_nt(h, w_ref[:QKV, :]).astype(BF16)
        u_ref[...] = _dot_nt(h, w_ref[QKV:, :])

    return pl.pallas_call(
        body, name="mixproj_fwd", grid=(T // tm,),
        in_specs=[pl.BlockSpec((tm, D), lambda i: (i, 0)), pl.BlockSpec((1, D), lambda i: (0, 0)),
                  pl.BlockSpec((W, D), lambda i: (0, 0))],
        out_specs=[pl.BlockSpec((tm, QKV), lambda i: (i, 0)), pl.BlockSpec((tm, W - QKV), lambda i: (i, 0))],
        out_shape=[jax.ShapeDtypeStruct((T, QKV), BF16), jax.ShapeDtypeStruct((T, W - QKV), F32)],
        compiler_params=_cp(1),
    )(x, g, wt)


def _attn_bias_table():
    rows, cols = GROUP * WINDOW, 2 * WINDOW
    row = lax.broadcasted_iota(jnp.int32, (N_KV, rows, cols), 1)
    col = lax.broadcasted_iota(jnp.int32, (N_KV, rows, cols), 2)
    head = GROUP * lax.broadcasted_iota(jnp.int32, (N_KV, rows, cols), 0) + (row >> 7)
    dist = (row & (WINDOW - 1)) + WINDOW - col
    slope = jnp.exp2(-(head + 1).astype(F32))
    return jnp.where((dist >= 0) & (dist < WINDOW), -slope * dist.astype(F32), NEG_INF)


def _first_block_mask(n):
    col = lax.broadcasted_iota(jnp.int32, (GROUP * WINDOW, 2 * WINDOW), 1)
    return (n > 0) | (col >= WINDOW)


def _sink_col(sink_ref, g):
    hi = lax.broadcasted_iota(jnp.int32, (GROUP * WINDOW, 1), 0) >> 7
    col = jnp.zeros((GROUP * WINDOW, 1), F32)
    for i in range(GROUP):
        col = jnp.where(hi == i, sink_ref[0, GROUP * g + i], col)
    return col


def _stack_heads(ref, g):
    return jnp.concatenate([ref[:, (GROUP * g + i) * HEAD_DIM:(GROUP * g + i + 1) * HEAD_DIM]
                            for i in range(GROUP)], axis=0)


def _band(kvp_ref, kvc_ref, off):
    return jnp.concatenate([kvp_ref[:, off:off + HEAD_DIM], kvc_ref[:, off:off + HEAD_DIM]], axis=0)


def _attn_probs(qs, k, bias, seen, sink):
    s = jnp.where(seen, _dot_nt(qs, k) * SCALE + bias, NEG_INF)
    m = jnp.maximum(jnp.max(s, axis=-1, keepdims=True), sink)
    p = jnp.exp(s - m)
    es = jnp.exp(sink - m)
    inv = 1.0 / (jnp.sum(p, axis=-1, keepdims=True) + es)
    return p * inv, es * inv


def _attn_fwd(sinks, tab, qkv):
    T = qkv.shape[0]
    nb = T // WINDOW

    def body(sink_ref, tab_ref, q_ref, kvp_ref, kvc_ref, o_ref):
        seen = _first_block_mask(pl.program_id(0))
        for g in range(N_KV):
            qs = _stack_heads(q_ref, g)
            k = _band(kvp_ref, kvc_ref, g * HEAD_DIM)
            v = _band(kvp_ref, kvc_ref, KV_W + g * HEAD_DIM)
            p, _ = _attn_probs(qs, k, tab_ref[g], seen, _sink_col(sink_ref, g))
            o = _dot(p.astype(BF16), v)
            for i in range(GROUP):
                h = GROUP * g + i
                o_ref[:, h * HEAD_DIM:(h + 1) * HEAD_DIM] = o[i * WINDOW:(i + 1) * WINDOW].astype(BF16)

    return pl.pallas_call(
        body, name="attn_fwd", grid=(nb,),
        in_specs=[pl.BlockSpec(memory_space=pltpu.SMEM),
                  pl.BlockSpec(tab.shape, lambda n: (0, 0, 0)),
                  pl.BlockSpec((WINDOW, ATTN_W), lambda n: (n, 0)),
                  pl.BlockSpec((WINDOW, 2 * KV_W), lambda n: (jnp.maximum(n - 1, 0), 2)),
                  pl.BlockSpec((WINDOW, 2 * KV_W), lambda n: (n, 2))],
        out_specs=pl.BlockSpec((WINDOW, ATTN_W), lambda n: (n, 0)),
        out_shape=jax.ShapeDtypeStruct((T, ATTN_W), BF16),
        compiler_params=_cp(1),
    )(sinks, tab, qkv, qkv, qkv)


def _shift_copies(src_ref, dst_ref, n):
    for b in range(1, 8):
        dst_ref[b - 1] = src_ref[b:b + n, :]


def _tap(src_ref, sh_ref, s, c0):
    a, b = divmod(s, 8)
    start = pl.multiple_of(c0 + 8 * a, 8)
    if b == 0:
        return src_ref[pl.ds(start, CONV_ROWS), :]
    return sh_ref[b - 1, pl.ds(start, CONV_ROWS), :]


def _glu_rows(u, ch):
    return u[:, :ch] * _fast_sigmoid(u[:, ch:])


def _fill_z(zs_ref, zsh_ref, uc_ref, up_ref, i, ch, n):
    zs_ref[0:HALO] = jnp.where(i > 0, _glu_rows(up_ref[...], ch), 0.0)
    zs_ref[HALO:] = _glu_rows(uc_ref[...], ch)
    _shift_copies(zs_ref, zsh_ref, n - 8)


def _conv_fwd(u, w, b, lg, lb):
    T = u.shape[0]
    CH = u.shape[1] // 2
    tm = min(TM, T)
    n = tm + HALO
    hb = tm // HALO

    def body(uc_ref, up_ref, w_ref, b_ref, lg_ref, lb_ref, conv_ref, ypre_ref, zs_ref, zsh_ref):
        i = pl.program_id(0)
        _fill_z(zs_ref, zsh_ref, uc_ref, up_ref, i, CH, n)
        bias = b_ref[...]

        def chunk(ci, carry):
            c0 = pl.multiple_of(ci * CONV_ROWS, CONV_ROWS)
            acc = jnp.broadcast_to(bias, (CONV_ROWS, CH))
            for k in range(CONV_W):
                acc = acc + w_ref[k:k + 1, :] * _tap(zs_ref, zsh_ref, HALO - (CONV_W - 1) + k, c0)
            ypre_ref[pl.ds(c0, CONV_ROWS), :] = acc
            return carry

        lax.fori_loop(0, tm // CONV_ROWS, chunk, 0)
        y = ypre_ref[...]
        mu = jnp.mean(y, axis=-1, keepdims=True)
        d = y - mu
        var = jnp.mean(d * d, axis=-1, keepdims=True)
        o = d * lax.rsqrt(var + EPS) * lg_ref[...] + lb_ref[...]
        conv_ref[...] = (o * _fast_sigmoid(o)).astype(BF16)

    vec = pl.BlockSpec((1, CH), lambda i: (0, 0))
    return pl.pallas_call(
        body, name="conv_fwd", grid=(T // tm,),
        in_specs=[pl.BlockSpec((tm, 2 * CH), lambda i: (i, 0)),
                  pl.BlockSpec((HALO, 2 * CH), lambda i: (jnp.maximum(i * hb - 1, 0), 0)),
                  pl.BlockSpec((CONV_W, CH), lambda i: (0, 0)), vec, vec, vec],
        out_specs=[pl.BlockSpec((tm, CH), lambda i: (i, 0)), pl.BlockSpec((tm, CH), lambda i: (i, 0))],
        out_shape=[jax.ShapeDtypeStruct((T, CH), BF16), jax.ShapeDtypeStruct((T, CH), F32)],
        scratch_shapes=[pltpu.VMEM((n, CH), F32), pltpu.VMEM((7, n - 8, CH), F32)],
        compiler_params=_cp(1),
    )(u, u, w, b, lg, lb)


def _mixout_fwd(x, attn, conv, wo):
    T, D = x.shape
    tm = min(TM_MIX, T)
    A = attn.shape[1]

    def body(x_ref, a_ref, c_ref, w_ref, xo_ref):
        xo_ref[...] = x_ref[...] + _dot(a_ref[...], w_ref[:A, :]) + _dot(c_ref[...], w_ref[A:, :])

    return pl.pallas_call(
        body, name="mixout_fwd", grid=(T // tm,),
        in_specs=[pl.BlockSpec((tm, D), lambda i: (i, 0)), pl.BlockSpec((tm, A), lambda i: (i, 0)),
                  pl.BlockSpec((tm, conv.shape[1]), lambda i: (i, 0)), pl.BlockSpec(wo.shape, lambda i: (0, 0))],
        out_specs=pl.BlockSpec((tm, D), lambda i: (i, 0)),
        out_shape=jax.ShapeDtypeStruct((T, D), F32),
        compiler_params=_cp(1),
    )(x, attn, conv, wo)


def _rms_bwd_rows(dh, xf, g):
    xh, r = _rms(xf, None)
    dxn = dh * g
    dx = r * (dxn - xh * jnp.mean(dxn * xh, axis=-1, keepdims=True))
    return dx, jnp.sum(dh * xh, axis=0, keepdims=True), xh * g


def _loss_head(x, g, tgt):
    T, D = x.shape
    tm = min(TM, T)

    def body(x_ref, g_ref, t_ref, loss_ref, dx_ref, dg_ref):
        @pl.when(pl.program_id(0) == 0)
        def _():
            loss_ref[...] = jnp.zeros_like(loss_ref)
            dg_ref[...] = jnp.zeros_like(dg_ref)

        xf = x_ref[...]
        g = g_ref[...]
        xh, _ = _rms(xf, None)
        e = xh * g - t_ref[...]
        loss_ref[...] += 0.5 * jnp.sum(jnp.mean(e * e, axis=-1, keepdims=True), axis=0, keepdims=True)
        dx, dg, _ = _rms_bwd_rows(e * (1.0 / D), xf, g)
        dx_ref[...] = dx
        dg_ref[...] += dg

    return pl.pallas_call(
        body, name="loss_head", grid=(T // tm,),
        in_specs=[pl.BlockSpec((tm, D), lambda i: (i, 0)), pl.BlockSpec((1, D), lambda i: (0, 0)),
                  pl.BlockSpec((tm, D), lambda i: (i, 0))],
        out_specs=[pl.BlockSpec((1, 1), lambda i: (0, 0)), pl.BlockSpec((tm, D), lambda i: (i, 0)),
                   pl.BlockSpec((1, D), lambda i: (0, 0))],
        out_shape=[jax.ShapeDtypeStruct((1, 1), F32), jax.ShapeDtypeStruct((T, D), F32),
                   jax.ShapeDtypeStruct((1, D), F32)],
        compiler_params=_cp(1),
    )(x, g, tgt)


def _ffn_bwd(dxo, x, g, gu, win, wout, dep):
    T, D = x.shape
    FB = win.shape[2]
    tm = min(TM_FFN_BWD, T)

    def body(dxo_ref, x_ref, g_ref, gu_ref, win_hbm, wout_hbm, dep_ref,
             dxi_ref, dg_ref, hb_ref, dgu_ref, a_ref, dyb_ref, win_v, wout_v, sems):
        @pl.when(pl.program_id(0) == 0)
        def _():
            _load_ffn_weights(win_hbm, wout_hbm, win_v, wout_v, sems)
            dg_ref[...] = jnp.zeros_like(dg_ref)

        dyb = (0.5 * dxo_ref[...]).astype(BF16)
        dyb_ref[...] = dyb
        dh = jnp.zeros((tm, D), F32)
        for blk in range(2):
            cols = pl.ds(blk * FB, FB)
            da = _dot_nt(dyb, wout_v[cols, :])
            gate = gu_ref[0, :, cols].astype(F32)
            up = gu_ref[1, :, cols].astype(F32)
            sg = _fast_sigmoid(gate)
            s = gate * sg
            a_ref[:, cols] = (s * up).astype(BF16)
            dgate = (da * up * (sg + s * (1.0 - sg))).astype(BF16)
            dup = (da * s).astype(BF16)
            dgu_ref[0, :, cols] = dgate
            dgu_ref[1, :, cols] = dup
            dh = dh + _dot_nt(dgate, win_v[blk]) + _dot_nt(dup, win_v[2 + blk])
        dx, dg, h = _rms_bwd_rows(dh, x_ref[...], g_ref[...])
        dxi_ref[...] = dxo_ref[...] + dx
        dg_ref[...] += dg
        hb_ref[...] = h.astype(BF16)

    row = pl.BlockSpec((tm, D), lambda i: (i, 0))
    act = pl.BlockSpec((2, tm, 2 * FB), lambda i: (0, i, 0))
    return pl.pallas_call(
        body, name="ffn_bwd", grid=(T // tm,),
        in_specs=[row, row, pl.BlockSpec((1, D), lambda i: (0, 0)), act, ANY, ANY, ANY],
        out_specs=[row, pl.BlockSpec((1, D), lambda i: (0, 0)), row, act,
                   pl.BlockSpec((tm, 2 * FB), lambda i: (i, 0)), row],
        out_shape=[jax.ShapeDtypeStruct((T, D), F32), jax.ShapeDtypeStruct((1, D), F32),
                   jax.ShapeDtypeStruct((T, D), BF16), jax.ShapeDtypeStruct((2, T, 2 * FB), BF16),
                   jax.ShapeDtypeStruct((T, 2 * FB), BF16), jax.ShapeDtypeStruct((T, D), BF16)],
        scratch_shapes=[pltpu.VMEM(win.shape, BF16), pltpu.VMEM(wout.shape, BF16), pltpu.SemaphoreType.DMA((6,))],
        compiler_params=_cp(1),
    )(dxo, x, g, gu, win, wout, dep)


def _mix_rms_bwd(dxo, x, g, dzs, wts):
    T, D = x.shape
    tm = min(TM, T)
    npair = len(dzs)

    def body(*refs):
        dxo_ref, x_ref, g_ref = refs[:3]
        dz_refs, w_refs = refs[3:3 + npair], refs[3 + npair:3 + 2 * npair]
        dxi_ref, dg_ref, hb_ref = refs[3 + 2 * npair:]

        @pl.when(pl.program_id(0) == 0)
        def _():
            dg_ref[...] = jnp.zeros_like(dg_ref)

        dh = jnp.zeros((tm, D), F32)
        for p in range(npair):
            dh = dh + _dot(dz_refs[p][...], w_refs[p][...])
        dx, dg, h = _rms_bwd_rows(dh, x_ref[...], g_ref[...])
        dxi_ref[...] = dxo_ref[...] + dx
        dg_ref[...] += dg
        hb_ref[...] = h.astype(BF16)

    row = pl.BlockSpec((tm, D), lambda i: (i, 0))
    return pl.pallas_call(
        body, name="mix_rms_bwd", grid=(T // tm,),
        in_specs=[row, row, pl.BlockSpec((1, D), lambda i: (0, 0))]
                 + [pl.BlockSpec((tm, dz.shape[1]), lambda i: (i, 0)) for dz in dzs]
                 + [pl.BlockSpec(w.shape, lambda i: (0, 0)) for w in wts],
        out_specs=[row, pl.BlockSpec((1, D), lambda i: (0, 0)), row],
        out_shape=[jax.ShapeDtypeStruct((T, D), F32), jax.ShapeDtypeStruct((1, D), F32),
                   jax.ShapeDtypeStruct((T, D), BF16)],
        compiler_params=_cp(1),
    )(dxo, x, g, *dzs, *wts)


def _wgrad(name, a, b, a_spec, b_spec, out_shape, out_spec, nblk, dep, acc_shape):
    T = a.shape[0]
    tk = min(TK_WGRAD, T)
    nk = T // tk

    def body(a_ref, b_ref, dep_ref, o_ref, acc_ref):
        k = pl.program_id(1)

        @pl.when(k == 0)
        def _():
            acc_ref[...] = jnp.zeros_like(acc_ref)

        acc_ref[...] += _dot_tn(a_ref[...], b_ref[...])

        @pl.when(k == nk - 1)
        def _():
            o_ref[...] = acc_ref[...].reshape(o_ref.shape).astype(BF16)

    return pl.pallas_call(
        body, name=name, grid=(nblk, nk), in_specs=[a_spec, b_spec, ANY], out_specs=out_spec,
        out_shape=jax.ShapeDtypeStruct(out_shape, BF16), scratch_shapes=[pltpu.VMEM(acc_shape, F32)],
        compiler_params=_cp(2),
    )(a, b, dep)


def _wgrad_ffn_in(hb, dgu, dep):
    T, D = hb.shape
    FB = dgu.shape[2] // 2
    tk = min(TK_WGRAD, T)
    return _wgrad("wgrad_ffn_in", hb, dgu,
                  pl.BlockSpec((tk, D), lambda b, k: (k, 0)),
                  pl.BlockSpec((None, tk, FB), lambda b, k: (b // 2, k, b % 2)),
                  (4, D, FB), pl.BlockSpec((None, D, FB), lambda b, k: (b, 0, 0)), 4, dep, (D, FB))


def _wgrad_ffn_out(a, dyb, dep):
    T, D = dyb.shape
    FB = a.shape[1] // 2
    tk = min(TK_WGRAD, T)
    return _wgrad("wgrad_ffn_out", a, dyb,
                  pl.BlockSpec((tk, FB), lambda b, k: (k, b)),
                  pl.BlockSpec((tk, D), lambda b, k: (k, 0)),
                  (4, FB // 2, D), pl.BlockSpec((2, FB // 2, D), lambda b, k: (b, 0, 0)), 2, dep, (FB, D))


def _wgrad_cat(a_list, b_list):
    T = a_list[0].shape[0]
    tk = min(TK_WGRAD, T)
    nk = T // tk
    na = len(a_list)
    M, N = sum(a.shape[1] for a in a_list), sum(b.shape[1] for b in b_list)

    def body(*refs):
        a_refs, b_refs, o_ref, acc_ref = refs[:na], refs[na:-2], refs[-2], refs[-1]
        k = pl.program_id(0)

        @pl.when(k == 0)
        def _():
            acc_ref[...] = jnp.zeros_like(acc_ref)

        r0 = 0
        for a_ref in a_refs:
            c0 = 0
            for b_ref in b_refs:
                m, n = a_ref.shape[1], b_ref.shape[1]
                acc_ref[r0:r0 + m, c0:c0 + n] += _dot_tn(a_ref[...], b_ref[...])
                c0 += n
            r0 += a_ref.shape[1]

        @pl.when(k == nk - 1)
        def _():
            o_ref[...] = acc_ref[...].astype(BF16)

    return pl.pallas_call(
        body, name="wgrad_cat", grid=(nk,),
        in_specs=[pl.BlockSpec((tk, v.shape[1]), lambda k: (k, 0)) for v in list(a_list) + list(b_list)],
        out_specs=pl.BlockSpec((M, N), lambda k: (0, 0)),
        out_shape=jax.ShapeDtypeStruct((M, N), BF16), scratch_shapes=[pltpu.VMEM((M, N), F32)],
        compiler_params=_cp(1),
    )(*a_list, *b_list)


def _mixout_bwd(dxo, wo):
    T, D = dxo.shape
    tm = min(TM_MIX, T)
    A = ATTN_W
    C = wo.shape[0] - A

    def body(dxo_ref, w_ref, dyb_ref, da_ref, dc_ref):
        dyb = dxo_ref[...].astype(BF16)
        dyb_ref[...] = dyb
        da_ref[...] = _dot_nt(dyb, w_ref[:A, :]).astype(BF16)
        dc_ref[...] = _dot_nt(dyb, w_ref[A:, :])

    return pl.pallas_call(
        body, name="mixout_bwd", grid=(T // tm,),
        in_specs=[pl.BlockSpec((tm, D), lambda i: (i, 0)), pl.BlockSpec(wo.shape, lambda i: (0, 0))],
        out_specs=[pl.BlockSpec((tm, D), lambda i: (i, 0)), pl.BlockSpec((tm, A), lambda i: (i, 0)),
                   pl.BlockSpec((tm, C), lambda i: (i, 0))],
        out_shape=[jax.ShapeDtypeStruct((T, D), BF16), jax.ShapeDtypeStruct((T, A), BF16),
                   jax.ShapeDtypeStruct((T, C), F32)],
        compiler_params=_cp(1),
    )(dxo, wo)


def _conv_bwd(dconv, ypre, u, w, lg, lb):
    T, CH = dconv.shape
    tm = min(TM, T)
    n = tm + HALO
    hb = tm // HALO
    nt = T // tm
    nchunk = tm // CONV_ROWS

    def body(dc_ref, dcn_ref, yp_ref, ypn_ref, uc_ref, up_ref, w_ref, lg_ref, lb_ref,
             du_ref, dw_ref, dvec_ref, zs_ref, zsh_ref, dy_ref, dysh_ref, dz_ref, dwacc_ref):
        i = pl.program_id(0)

        @pl.when(i == 0)
        def _():
            dwacc_ref[...] = jnp.zeros_like(dwacc_ref)
            dvec_ref[...] = jnp.zeros_like(dvec_ref)

        g, bb = lg_ref[...], lb_ref[...]

        def ln_bwd(dc, yp):
            mu = jnp.mean(yp, axis=-1, keepdims=True)
            d = yp - mu
            rs = lax.rsqrt(jnp.mean(d * d, axis=-1, keepdims=True) + EPS)
            yn = d * rs
            o = yn * g + bb
            sg = _fast_sigmoid(o)
            do = dc * (sg * (1.0 + o * (1.0 - sg)))
            dyn = do * g
            dyp = rs * (dyn - jnp.mean(dyn, axis=-1, keepdims=True)
                        - yn * jnp.mean(dyn * yn, axis=-1, keepdims=True))
            return dyp, do, yn

        dyp, do, yn = ln_bwd(dc_ref[...], yp_ref[...])
        dvec_ref[0:1, :] += jnp.sum(dyp, axis=0, keepdims=True)
        dvec_ref[1:2, :] += jnp.sum(do * yn, axis=0, keepdims=True)
        dvec_ref[2:3, :] += jnp.sum(do, axis=0, keepdims=True)
        dy_ref[0:tm] = dyp
        dyh, _, _ = ln_bwd(dcn_ref[...], ypn_ref[...])
        dy_ref[tm:] = jnp.where(i < nt - 1, dyh, 0.0)
        _shift_copies(dy_ref, dysh_ref, n - 8)
        _fill_z(zs_ref, zsh_ref, uc_ref, up_ref, i, CH, n)

        def chunk(ci, carry):
            c0 = pl.multiple_of(ci * CONV_ROWS, CONV_ROWS)
            acc = jnp.zeros((CONV_ROWS, CH), F32)
            for k in range(CONV_W):
                acc = acc + w_ref[k:k + 1, :] * _tap(dy_ref, dysh_ref, CONV_W - 1 - k, c0)
            dz_ref[pl.ds(c0, CONV_ROWS), :] = acc
            dyc = dy_ref[pl.ds(c0, CONV_ROWS), :]
            for k in range(CONV_W):
                prod = dyc * _tap(zs_ref, zsh_ref, HALO - (CONV_W - 1) + k, c0)
                dwacc_ref[k] += jnp.sum(prod.reshape(CONV_ROWS // 8, 8, CH), axis=0)
            return carry

        lax.fori_loop(0, nchunk, chunk, 0)

        @pl.when(i == nt - 1)
        def _():
            dw_ref[...] = jnp.sum(dwacc_ref[...], axis=1)

        uc = uc_ref[...]
        a = uc[:, :CH]
        sg = _fast_sigmoid(uc[:, CH:])
        dz = dz_ref[...]
        du_ref[:, :CH] = (dz * sg).astype(BF16)
        du_ref[:, CH:] = (dz * a * sg * (1.0 - sg)).astype(BF16)

    cur = lambda c: pl.BlockSpec((tm, c), lambda i: (i, 0))
    nxt = lambda c: pl.BlockSpec((HALO, c), lambda i: (jnp.minimum((i + 1) * hb, T // HALO - 1), 0))
    vec = pl.BlockSpec((1, CH), lambda i: (0, 0))
    return pl.pallas_call(
        body, name="conv_bwd", grid=(nt,),
        in_specs=[cur(CH), nxt(CH), cur(CH), nxt(CH), cur(2 * CH),
                  pl.BlockSpec((HALO, 2 * CH), lambda i: (jnp.maximum(i * hb - 1, 0), 0)),
                  pl.BlockSpec((CONV_W, CH), lambda i: (0, 0)), vec, vec],
        out_specs=[pl.BlockSpec((tm, 2 * CH), lambda i: (i, 0)), pl.BlockSpec((32, CH), lambda i: (0, 0)),
                   pl.BlockSpec((8, CH), lambda i: (0, 0))],
        out_shape=[jax.ShapeDtypeStruct((T, 2 * CH), BF16), jax.ShapeDtypeStruct((32, CH), F32),
                   jax.ShapeDtypeStruct((8, CH), F32)],
        scratch_shapes=[pltpu.VMEM((n, CH), F32), pltpu.VMEM((7, n - 8, CH), F32),
                        pltpu.VMEM((n, CH), F32), pltpu.VMEM((7, n - 8, CH), F32), pltpu.VMEM((tm, CH), F32),
                        pltpu.VMEM((32, 8, CH), F32)],
        compiler_params=_cp(1),
    )(dconv, dconv, ypre, ypre, u, u, w, lg, lb)


def _attn_bwd(sinks, tab, qkv, dattn):
    T = qkv.shape[0]
    nb = T // WINDOW

    def body(sink_ref, tab_ref, q_ref, kvp_ref, kvc_ref, do_ref, dq_ref, dkv_ref, dsk_ref, carry_ref):
        n = pl.program_id(0)

        @pl.when(n == 0)
        def _():
            dsk_ref[...] = jnp.zeros_like(dsk_ref)
            carry_ref[...] = jnp.zeros_like(carry_ref)

        @pl.when(n < nb)
        def _():
            seen = _first_block_mask(n)
            for g in range(N_KV):
                qs = _stack_heads(q_ref, g)
                dos = _stack_heads(do_ref, g)
                k = _band(kvp_ref, kvc_ref, g * HEAD_DIM)
                v = _band(kvp_ref, kvc_ref, KV_W + g * HEAD_DIM)
                p, ps = _attn_probs(qs, k, tab_ref[g], seen, _sink_col(sink_ref, g))
                dp = _dot_nt(dos, v)
                delta = jnp.sum(p * dp, axis=-1, keepdims=True)
                dsb = (p * (dp - delta)).astype(BF16)
                dsink = -ps * delta
                dqs = _dot(dsb, k) * SCALE
                dk = _dot_tn(dsb, qs) * SCALE
                dv = _dot_tn(p.astype(BF16), dos)
                for i in range(GROUP):
                    h = GROUP * g + i
                    dq_ref[:, h * HEAD_DIM:(h + 1) * HEAD_DIM] = dqs[i * WINDOW:(i + 1) * WINDOW].astype(BF16)
                    dsk_ref[h:h + 1, :] += jnp.sum(dsink[i * WINDOW:(i + 1) * WINDOW], axis=0, keepdims=True)
                for off, d in ((g * HEAD_DIM, dk), (KV_W + g * HEAD_DIM, dv)):
                    dkv_ref[:, off:off + HEAD_DIM] = (carry_ref[:, off:off + HEAD_DIM] + d[:WINDOW]).astype(BF16)
                    carry_ref[:, off:off + HEAD_DIM] = d[WINDOW:]

        @pl.when(n == nb)
        def _():
            dkv_ref[...] = carry_ref[...].astype(BF16)

    last = nb - 1
    return pl.pallas_call(
        body, name="attn_bwd", grid=(nb + 1,),
        in_specs=[pl.BlockSpec(memory_space=pltpu.SMEM),
                  pl.BlockSpec(tab.shape, lambda n: (0, 0, 0)),
                  pl.BlockSpec((WINDOW, ATTN_W), lambda n: (jnp.minimum(n, last), 0)),
                  pl.BlockSpec((WINDOW, 2 * KV_W), lambda n: (jnp.clip(n - 1, 0, last), 2)),
                  pl.BlockSpec((WINDOW, 2 * KV_W), lambda n: (jnp.minimum(n, last), 2)),
                  pl.BlockSpec((WINDOW, ATTN_W), lambda n: (jnp.minimum(n, last), 0))],
        out_specs=[pl.BlockSpec((WINDOW, ATTN_W), lambda n: (jnp.minimum(n, last), 0)),
                   pl.BlockSpec((WINDOW, 2 * KV_W), lambda n: (jnp.maximum(n - 1, 0), 0)),
                   pl.BlockSpec((8, LANES), lambda n: (0, 0))],
        out_shape=[jax.ShapeDtypeStruct((T, ATTN_W), BF16), jax.ShapeDtypeStruct((T, 2 * KV_W), BF16),
                   jax.ShapeDtypeStruct((8, LANES), F32)],
        scratch_shapes=[pltpu.VMEM((WINDOW, 2 * KV_W), F32)],
        compiler_params=_cp(1),
    )(sinks, tab, qkv, qkv, qkv, dattn)


def _pack(arrs):
    flat = jnp.concatenate([a.reshape(-1) for a in arrs])
    pad = -flat.shape[0] % (8 * LANES)
    return jnp.pad(flat, (0, pad)).reshape(1, -1, LANES)


def _unpack(packed, like):
    flat = packed.reshape(-1)
    out, off = [], 0
    for a in like:
        out.append(flat[off:off + a.size].reshape(a.shape))
        off += a.size
    return out


def kernel(x, norm_ffn1, w_ffn1_in, w_ffn1_out, norm_mix, w_in, sinks, w_dw, b_dw, conv_ln_g, conv_ln_b, w_out, norm_ffn2, w_ffn2_in, w_ffn2_out, final_norm, loss_target, m_norm_ffn1, m_w_ffn1_in, m_w_ffn1_out, m_norm_mix, m_w_in, m_sinks, m_w_dw, m_b_dw, m_conv_ln_g, m_conv_ln_b, m_w_out, m_norm_ffn2, m_w_ffn2_in, m_w_ffn2_out, m_final_norm, v_norm_ffn1, v_w_ffn1_in, v_w_ffn1_out, v_norm_mix, v_w_in, v_sinks, v_w_dw, v_b_dw, v_conv_ln_g, v_conv_ln_b, v_w_out, v_norm_ffn2, v_w_ffn2_in, v_w_ffn2_out, v_final_norm):
    L, D = norm_ffn1.shape
    T = x.shape[1]
    FB = w_ffn1_in.shape[2]
    CH = b_dw.shape[1]
    QKV = ATTN_W + 2 * KV_W
    xs = x.reshape(T, D)
    tgt = loss_target.reshape(T, D)
    cx, cy, cc = lax.axis_index("x"), lax.axis_index("y"), lax.axis_index("c")
    chip = 2 * cx + cy
    cidx = cc.reshape(1).astype(jnp.int32)
    tr = lambda a_: jnp.transpose(a_, (0, 2, 1))
    big_w = (w_ffn1_in, w_ffn1_out, tr(w_in), w_out, w_ffn2_in, w_ffn2_out)
    big_m = (m_w_ffn1_in, m_w_ffn1_out, tr(m_w_in), m_w_out, m_w_ffn2_in, m_w_ffn2_out)
    big_v = (v_w_ffn1_in, v_w_ffn1_out, tr(v_w_in), v_w_out, v_w_ffn2_in, v_w_ffn2_out)
    NW = len(big_w) + 1

    def own_slot(a, slots=4, idx=chip):
        return lax.dynamic_update_index_in_dim(lax.empty((slots,) + a.shape, a.dtype), a, idx, 0)

    def shards(l, tok):
        return [own_slot((w_[l] + tok[0, 0]).astype(BF16)) for w_ in big_w] + [own_slot(w_dw[l] + tok[0, 0])]

    def gather_start(lands, tok):
        return _xchg_start("gather_start", [], lands, _gather_plan, tok)

    def gather_arrived(started, after, n, taps):
        _, lands, tok = _xchg_wait("gather_wait", started, 0, n, _gather_plan, after)
        return _xchg_start("gshare_start", [], lands[:-1] if taps else lands, _gshare_plan, tok, "sibling3"), lands[-1]

    def shared_weights(shared, after, n):
        _, mats, tok = _xchg_wait("gshare_wait", shared, 0, n, _gshare_plan, after, "sibling3")
        return mats, tok

    row = lambda a, l: a[l].reshape(1, -1)
    tab = _attn_bias_table()
    NB = len(big_w)

    saved, W = [], []
    zero_tok = jnp.zeros((8, LANES), F32)
    src0 = shards(0, zero_tok)
    started = gather_start(src0[:2], zero_tok)
    rest0 = gather_start(src0[2:], started[-1])
    cast = [None] + [shards(l, rest0[-1]) for l in range(1, L)]
    shared, _ = gather_arrived(started, [xs] + [a_ for c_ in cast[1:] for a_ in c_], 2, False)
    after = [shared[-1]]
    for l in range(L):
        mats, tok = shared_weights(shared, after, 2 if l == 0 else NB)
        started = None
        if l + 1 < L:
            started = gather_start(cast[l + 1], tok)
            tok = started[-1]
        x0 = xs
        x1, gu1 = _ffn_fwd(x0, row(norm_ffn1, l) + tok[0, 0], mats[0], mats[1].reshape(2 * FB, D))
        gm_row = row(norm_mix, l)
        if l == 0:
            shared, gdw = gather_arrived(rest0, [x1], NW - 2, True)
            rest, tok = shared_weights(shared, [shared[-1]], NB - 2)
            mats = list(mats) + list(rest)
            gm_row = gm_row + tok[0, 0]
        g1i, g1o, gi, go, g2i, g2o = mats
        w = dict(f1i=g1i, f1o=g1o.reshape(2 * FB, D), f2i=g2i, f2o=g2o.reshape(2 * FB, D),
                 wit=gi.reshape(-1, D), wo=go.reshape(-1, D),
                 wdw=jnp.transpose(gdw, (1, 0, 2)).reshape(CONV_W, CH))
        W.append(w)
        qkv, u = _mixproj_fwd(x1, gm_row, w["wit"])
        attn = _attn_fwd(row(sinks, l), tab, qkv)
        conv, ypre = _conv_fwd(u, w["wdw"], row(b_dw, l), row(conv_ln_g, l), row(conv_ln_b, l))
        x2 = _mixout_fwd(x1, attn, conv, w["wo"])
        g2_row = row(norm_ffn2, l)
        if started is not None and l > 0:
            shared, gdw = gather_arrived(started, [x2], NW, True)
            g2_row = g2_row + shared[-1][0, 0]
        xs, gu2 = _ffn_fwd(x2, g2_row, w["f2i"], w["f2o"])
        if started is not None and l == 0:
            shared, gdw = gather_arrived(started, [xs], NW, True)
        saved.append((x0, gu1, x1, qkv, u, attn, conv, ypre, x2, gu2))
        after = [xs]

    loss_part, dx, d_final = _loss_head(xs, final_norm.reshape(1, D), tgt)
    loss = lax.psum(loss_part[0, 0], ("x", "y", "c"))

    bufs = [[lax.empty(w_.shape, F32) for _ in range(4)] for w_ in big_w]
    d_n1, d_nm, d_n2 = [None] * L, [None] * L, [None] * L
    d_sk, d_bdw, d_lg, d_lb, d_wdw = [None] * L, [None] * L, [None] * L, [None] * L, [None] * L

    me_idx = 4 * cx + 2 * cy + cc

    def reduce_start(gs):
        lands = []
        for g in gs:
            h = g.shape[1] // 2
            mine = lax.dynamic_slice(g, (chip, cc * h, 0), (1, h, g.shape[2]))[0]
            lands.append(own_slot(mine, 8, me_idx))
        return _xchg_start("rs_start", gs, lands, _rs_plan, zero_tok, "all")

    def share_start(rs_started, after, n):
        _, qs, tok = _xchg_wait("rs_wait", rs_started, n, n, _rs_plan, after, "all")
        return _xchg_start("qshare_start", qs, [lax.empty(q.shape, q.dtype) for q in qs], _whole_plan, tok, "sibling")

    def finish(l, shared, after, idxs):
        q_own, q_sib, _ = _xchg_wait("qshare_wait", shared, len(idxs), len(idxs), _whole_plan, after, "sibling")
        for k, t in enumerate(idxs):
            bufs[t] = _adamw_layer(cidx, q_own[k], q_sib[k], big_w[t], big_m[t], big_v[t], bufs[t], l)

    ALL = list(range(NB))
    EARLY, LATE = ALL[2:], ALL[:2]
    rs_list, shares = [], []
    tok = zero_tok
    for l in reversed(range(L)):
        w = W[l]
        x0, gu1, x1, qkv, u, attn, conv, ypre, x2, gu2 = saved[l]
        dx, d_n2[l], hb, dgu, a, dyb = _ffn_bwd(dx, x2, row(norm_ffn2, l), gu2, w["f2i"], w["f2o"], tok)
        g_f2i, g_f2o = _wgrad_ffn_in(hb, dgu, tok), _wgrad_ffn_out(a, dyb, tok)
        lg_row = row(conv_ln_g, l)
        if len(rs_list) >= 2:
            pl_, st_ = rs_list[-2]
            shares.append((pl_, share_start(st_, [g_f2o], NB)))
            lg_row = lg_row + shares[-1][1][-1][0, 0]
        dyb, dattn, dconv = _mixout_bwd(dx, w["wo"])
        g_wo = _wgrad_cat([attn, conv], [dyb]).reshape(4, -1, D)
        du, dwdw, dvec = _conv_bwd(dconv, ypre, u, w["wdw"], lg_row, row(conv_ln_b, l))
        d_wdw[l], d_bdw[l], d_lg[l], d_lb[l] = dwdw[:CONV_W], dvec[0], dvec[1], dvec[2]
        dq, dkv, dsk = _attn_bwd(row(sinks, l), tab, qkv, dattn)
        d_sk[l] = dsk[:, 0]
        wit = w["wit"]
        dx, d_nm[l], hb = _mix_rms_bwd(dx, x1, row(norm_mix, l), [dq, dkv, du],
                                       [wit[:ATTN_W], wit[ATTN_W:QKV], wit[QKV:]])
        g_wi = _wgrad_cat([dq, dkv, du], [hb]).reshape(4, -1, D)
        if l == 0:
            rs_early = reduce_start([g_wi, g_wo, g_f2i, g_f2o])
            tok = rs_early[-1]
        dx, d_n1[l], hb, dgu, a, dyb = _ffn_bwd(dx, x0, row(norm_ffn1, l), gu1, w["f1i"], w["f1o"], tok)
        g_f1i, g_f1o = _wgrad_ffn_in(hb, dgu, tok), _wgrad_ffn_out(a, dyb, tok)
        rs_started = reduce_start([g_f1i, g_f1o] if l == 0 else [g_f1i, g_f1o, g_wi, g_wo, g_f2i, g_f2o])
        tok = rs_started[-1]
        rs_list.append((l, rs_started))
    grad_x = dx.reshape(x.shape)

    small_g = [jnp.concatenate(d, axis=0) for d in (d_n1, d_nm, d_n2)] + [d_final, jnp.stack(d_sk)] + \
              [jnp.stack(d) for d in (d_bdw, d_lg, d_lb, d_wdw)]
    packed = _pack(small_g)[0]
    small_started = _xchg_start("small_start", [packed], [own_slot(packed, 8, 4 * cx + 2 * cy + cc)], _slot_plan, tok, "all")

    rs_late = rs_list.pop()[1]
    after = [small_started[-1]]
    if len(rs_list) > len(shares):
        pl_, st_ = rs_list[len(shares)]
        shares.append((pl_, share_start(st_, after, NB)))
        after = [shares[-1][1][-1]]
    if shares:
        finish(*shares.pop(0), after, ALL)
        after = [b_[0] for b_ in bufs]
    sh_early = share_start(rs_early, after, len(EARLY))
    after = [sh_early[-1]]
    sh_late = None
    for l, sh in shares:
        finish(l, sh, after, ALL)
        after = [b_[0] for b_ in bufs]
        if sh_late is None:
            sh_late = share_start(rs_late, after, len(LATE))
            after = [sh_late[-1]]
    if sh_late is None:
        sh_late = share_start(rs_late, after, len(LATE))
        after = [sh_late[-1]]
    _, (slots,), _ = _xchg_wait("small_wait", small_started, 1, 1, _slot_plan, after, "all")
    small_sum = _unpack(_sum_slots(slots), small_g)
    g_wdw = lax.dynamic_slice_in_dim(small_sum[8], chip * w_dw.shape[2], w_dw.shape[2], axis=2)
    small_g = [small_sum[0], small_sum[1], small_sum[2], small_sum[3].reshape(D), small_sum[4],
               small_sum[5], small_sum[6], small_sum[7], g_wdw]
    small_w = (norm_ffn1, norm_mix, norm_ffn2, final_norm, sinks, b_dw, conv_ln_g, conv_ln_b, w_dw)
    small_m = (m_norm_ffn1, m_norm_mix, m_norm_ffn2, m_final_norm, m_sinks, m_b_dw, m_conv_ln_g, m_conv_ln_b, m_w_dw)
    small_v = (v_norm_ffn1, v_norm_mix, v_norm_ffn2, v_final_norm, v_sinks, v_b_dw, v_conv_ln_g, v_conv_ln_b, v_w_dw)
    upd = _adamw(_pack(small_g), _pack(small_w), _pack(small_m), _pack(small_v))
    small_upd = [_unpack(u_, small_w) for u_ in upd]
    finish(0, sh_early, [upd[0]], EARLY)
    finish(0, sh_late, [bufs[t][0] for t in EARLY], LATE)

    order = ("norm_ffn1", "w_ffn1_in", "w_ffn1_out", "norm_mix", "w_in", "sinks", "w_dw", "b_dw", "conv_ln_g",
             "conv_ln_b", "w_out", "norm_ffn2", "w_ffn2_in", "w_ffn2_out", "final_norm")
    small_names = ("norm_ffn1", "norm_mix", "norm_ffn2", "final_norm", "sinks", "b_dw", "conv_ln_g", "conv_ln_b", "w_dw")
    big_names = ("w_ffn1_in", "w_ffn1_out", "w_in", "w_out", "w_ffn2_in", "w_ffn2_out")
    grads, deltas, new_m, new_v = {}, {}, {}, {}
    for i, nme in enumerate(small_names):
        grads[nme], deltas[nme], new_m[nme], new_v[nme] = small_g[i], small_upd[0][i], small_upd[1][i], small_upd[2][i]
    for i, nme in enumerate(big_names):
        grads[nme], deltas[nme], new_m[nme], new_v[nme] = [tr(b_) for b_ in bufs[i]] if nme == "w_in" else bufs[i]
    return (loss, grad_x, *[grads[n] for n in order], *[deltas[n] for n in order],
            *[new_m[n] for n in order], *[new_v[n] for n in order])
```

```python
import jax
import jax.numpy as jnp
from jax import lax
from jax.experimental import pallas as pl
from jax.experimental.pallas import tpu as pltpu

F32, BF16 = jnp.float32, jnp.bfloat16
EPS = 1e-6
NEG_INF = -1e30
HEAD_DIM = 64
N_HEADS = 8
N_KV = 2
GROUP = N_HEADS // N_KV
WINDOW = 128
ATTN_W = N_HEADS * HEAD_DIM
KV_W = N_KV * HEAD_DIM
CONV_W = 31
HALO = 32
CONV_ROWS = 32
SCALE = 1.0 / 8.0
ADAM_LR, ADAM_B1, ADAM_B2, ADAM_EPS, ADAM_WD, ADAM_STEP = 0.001, 0.9, 0.999, 1e-08, 0.01, 10
TM = 512
TM_FFN_BWD = 256
TK_WGRAD = 2048
TM_MIX = 1024
LANES = 128
VMEM_LIMIT = 52 * 1024 * 1024
MESH = pl.DeviceIdType.MESH
ANY = pl.BlockSpec(memory_space=pl.ANY)
HBM = pl.BlockSpec(memory_space=pltpu.HBM)
SEM = pl.BlockSpec(memory_space=pltpu.SEMAPHORE)
VMEM = pl.BlockSpec(memory_space=pltpu.VMEM)
EFFECT = pltpu.SideEffectType.DATAFLOW_SIDE_EFFECTING
TOKEN = jax.ShapeDtypeStruct((8, LANES), F32)


def _cp(n):
    return pltpu.CompilerParams(dimension_semantics=("arbitrary",) * n, vmem_limit_bytes=VMEM_LIMIT)


def _dot(a, b):
    return jnp.dot(a, b, preferred_element_type=F32)


def _dot_nt(a, b):
    return lax.dot_general(a, b, (((1,), (1,)), ((), ())), preferred_element_type=F32)


def _dot_tn(a, b):
    return lax.dot_general(a, b, (((0,), (0,)), ((), ())), preferred_element_type=F32)


def _place():
    x, y, c = lax.axis_index("x"), lax.axis_index("y"), lax.axis_index("c")
    chips = [(1 - x, y), (x, 1 - y), (1 - x, 1 - y)]
    return x, y, c, chips


def _rcopy(src, dst, send_sems, recv_sems, k, dev):
    return pltpu.make_async_remote_copy(src_ref=src, dst_ref=dst, send_sem=send_sems.at[k],
                                        recv_sem=recv_sems.at[k], device_id=dev, device_id_type=MESH)


def _hbm(a):
    return pltpu.with_memory_space_constraint(a, pltpu.HBM)


PEERS = {"chips": 3, "sibling": 1, "sibling3": 3, "all": 7}


def _targets(mode):
    x, y, c, chips = _place()
    b = 2 * x + y
    if mode == "chips":
        return b, c, [((px, py, c), 2 * px + py) for px, py in chips]
    if mode == "sibling":
        return b, c, [((x, y, 1 - c), b)]
    if mode == "sibling3":
        return b, c, [((x, y, 1 - c), 2 * px + py) for px, py in chips]
    flip = lambda v, f: 1 - v if f else v
    devs = [(flip(x, k >> 2 & 1), flip(y, k >> 1 & 1), flip(c, k & 1)) for k in range(1, 8)]
    return 4 * x + 2 * y + c, c, [(d, 4 * d[0] + 2 * d[1] + d[2]) for d in devs]


def _xchg_start(name, srcs, lands, plan, dep, mode="chips"):
    ns, nl, npeer = len(srcs), len(lands), PEERS[mode]

    def body(*refs):
        land = refs[ns:ns + nl]
        src = refs[:ns] if ns else land
        send_sems, recv_sems, token = refs[ns + nl + 1], refs[ns + nl + 2], refs[-1]
        me, c, peers = _targets(mode)
        for t in range(nl):
            for j, (dev, tag) in enumerate(peers):
                s, d, _ = plan(src[t], land[t], t, me, c, tag)
                _rcopy(s, d, send_sems, recv_sems, npeer * t + j, dev).start()
        token[...] = jnp.zeros_like(token)

    arrs = list(srcs) + list(lands)
    return pl.pallas_call(
        body, name=name,
        out_shape=(pltpu.SemaphoreType.DMA((npeer * nl,)), pltpu.SemaphoreType.DMA((npeer * nl,)),
                   *[pltpu.HBM(a.shape, a.dtype) for a in arrs], TOKEN),
        in_specs=[HBM] * (ns + nl) + [ANY], out_specs=(SEM, SEM, *[HBM] * (ns + nl), VMEM),
        input_output_aliases={i: 2 + i for i in range(ns + nl)},
        compiler_params=pltpu.CompilerParams(has_side_effects=EFFECT),
    )(*[_hbm(a) for a in arrs], dep)


def _xchg_wait(name, started, ns, nl, plan, after, mode="chips"):
    send_sems, recv_sems, thru = started[0], started[1], started[2:2 + ns + nl]
    npeer = PEERS[mode]

    def body(*refs):
        land = refs[ns:ns + nl]
        src = refs[:ns] if ns else land
        send_sems, recv_sems, token = refs[ns + nl], refs[ns + nl + 1], refs[-1]
        me, c, peers = _targets(mode)
        for t in range(nl):
            for j, (dev, tag) in enumerate(peers):
                s, _, a = plan(src[t], land[t], t, me, c, tag)
                cp = _rcopy(s, a, send_sems, recv_sems, npeer * t + j, dev)
                cp.wait_send()
                cp.wait_recv()
        token[...] = jnp.zeros_like(token)

    out = pl.pallas_call(
        body, name=name,
        out_shape=(*[pltpu.HBM(a.shape, a.dtype) for a in thru], TOKEN),
        in_specs=[HBM] * (ns + nl) + [SEM, SEM] + [ANY] * len(after), out_specs=(*[HBM] * (ns + nl), VMEM),
        input_output_aliases={i: i for i in range(ns + nl)},
        compiler_params=pltpu.CompilerParams(has_side_effects=EFFECT),
    )(*thru, send_sems, recv_sems, *after)
    return out[:ns], out[ns:ns + nl], out[-1]


def _half(ref_rows, which):
    h = ref_rows // 2
    return pl.ds(which * h, h)


def _gather_plan(src, land, t, b, c, pb):
    if land.shape[1] % 2 == 0:
        hs = _half(land.shape[1], c)
        return land.at[b, hs], land.at[b, hs], land.at[pb, hs]
    return land.at[b], land.at[b], land.at[pb]


def _gshare_plan(src, land, t, b, c, pb):
    return land.at[pb, _half(land.shape[1], c)], land.at[pb, _half(land.shape[1], c)], land.at[pb, _half(land.shape[1], 1 - c)]


def _rs_plan(src, land, t, me, c, tag):
    h = src.shape[1] // 2
    return src.at[tag // 2, pl.ds((tag % 2) * h, h), :], land.at[me], land.at[tag]


def _rows_block(h, cap=512):
    for rb in range(min(h, cap) // 16 * 16, 0, -16):
        if h % rb == 0:
            return rb
    return h


def _whole_plan(src, land, t, me, c, tag):
    return src, land, land


def _slot_plan(src, land, t, me, c, tag):
    return src, land.at[me], land.at[tag]


def _adam_update(gg, w, m, v):
    m2 = ADAM_B1 * m + (1.0 - ADAM_B1) * gg
    v2 = ADAM_B2 * v + (1.0 - ADAM_B2) * (gg * gg)
    mh = m2 / (1.0 - ADAM_B1 ** ADAM_STEP)
    vh = v2 / (1.0 - ADAM_B2 ** ADAM_STEP)
    return -ADAM_LR * (mh / (jnp.sqrt(vh) + ADAM_EPS) + ADAM_WD * w), m2, v2


def _adamw_layer(cidx, q_own, q_sib, w, m, v, bufs, l):
    L, R, C = w.shape
    h = R // 2
    rb = _rows_block(h, 256)
    nr = h // rb

    def body(c_ref, qo_ref, qs_ref, w_ref, m_ref, v_ref, *rest):
        g_ref, d_ref, mo_ref, vo_ref = rest[-4:]
        own = pl.program_id(0) == c_ref[0]
        gg = jnp.zeros((rb, C), F32)
        for s in range(8):
            gg = gg + jnp.where(own, qo_ref[s], qs_ref[s]).astype(F32)
        g_ref[...] = gg
        d_ref[...], mo_ref[...], vo_ref[...] = _adam_update(gg, w_ref[...], m_ref[...], v_ref[...])

    q_own_spec = pl.BlockSpec((8, rb, C), lambda hh, i, c: (0, jnp.where(hh == c[0], i, 0), 0))
    q_sib_spec = pl.BlockSpec((8, rb, C), lambda hh, i, c: (0, jnp.where(hh == c[0], 0, i), 0))
    wspec = pl.BlockSpec((None, rb, C), lambda hh, i, c: (l, hh * nr + i, 0))
    return pl.pallas_call(
        body, name="adamw_layer", out_shape=[jax.ShapeDtypeStruct(w.shape, F32)] * 4,
        grid_spec=pltpu.PrefetchScalarGridSpec(
            num_scalar_prefetch=1, grid=(2, nr),
            in_specs=[q_own_spec, q_sib_spec, wspec, wspec, wspec] + [ANY] * 4, out_specs=[wspec] * 4),
        input_output_aliases={6 + k: k for k in range(4)},
        compiler_params=_cp(2),
    )(cidx, q_own, q_sib, w, m, v, *bufs)


def _adamw(g, w, m, v):
    L, R, C = g.shape
    rb = _rows_block(R)

    def body(g_ref, w_ref, m_ref, v_ref, d_ref, mo_ref, vo_ref):
        d_ref[...], mo_ref[...], vo_ref[...] = _adam_update(g_ref[...], w_ref[...], m_ref[...], v_ref[...])

    spec = pl.BlockSpec((None, rb, C), lambda l, i: (l, i, 0))
    return pl.pallas_call(
        body, name="adamw", grid=(L, R // rb), in_specs=[spec] * 4, out_specs=[spec] * 3,
        out_shape=[jax.ShapeDtypeStruct(g.shape, F32)] * 3, compiler_params=_cp(2),
    )(g, w, m, v)


def _sum_slots(buf):
    def body(b_ref, o_ref):
        acc = b_ref[0]
        for k in range(1, 8):
            acc = acc + b_ref[k]
        o_ref[...] = acc

    return pl.pallas_call(body, name="sum_slots", in_specs=[VMEM], out_specs=VMEM,
                          out_shape=jax.ShapeDtypeStruct(buf.shape[1:], F32))(buf)


def _rms(xf, g):
    r = lax.rsqrt(jnp.mean(xf * xf, axis=-1, keepdims=True) + EPS)
    return xf * r, r


def _lane_chunks(n):
    lo = (n // LANES + 1) // 2 * LANES
    return ((0, lo), (lo, n - lo))


def _load_ffn_weights(win_hbm, wout_hbm, win_v, wout_v, sems):
    fb = win_v.shape[2]
    loads = [pltpu.make_async_copy(win_hbm.at[k], win_v.at[k], sems.at[k]) for k in range(4)]
    loads += [pltpu.make_async_copy(wout_hbm.at[pl.ds(k * fb, fb)], wout_v.at[pl.ds(k * fb, fb)], sems.at[4 + k])
              for k in range(2)]
    for cp in loads:
        cp.start()
    for cp in loads:
        cp.wait()


def _fast_sigmoid(v):
    return pl.reciprocal(1.0 + jnp.exp(-v), approx=True)


def _ffn_fwd(x, g, win, wout):
    T, D = x.shape
    FB = win.shape[2]
    tm = min(TM, T)

    def body(x_ref, g_ref, win_hbm, wout_hbm, xo_ref, gu_ref, win_v, wout_v, sems):
        @pl.when(pl.program_id(0) == 0)
        def _():
            _load_ffn_weights(win_hbm, wout_hbm, win_v, wout_v, sems)

        xf = x_ref[...]
        xh, _ = _rms(xf, None)
        h = (xh * g_ref[...]).astype(BF16)
        acc = jnp.zeros((tm, D), F32)
        for blk in range(2):
            for lo, sz in _lane_chunks(FB):
                cols = pl.ds(blk * FB + lo, sz)
                gate = _dot(h, win_v[blk, :, pl.ds(lo, sz)])
                up = _dot(h, win_v[2 + blk, :, pl.ds(lo, sz)])
                gu_ref[0, :, cols] = gate.astype(BF16)
                gu_ref[1, :, cols] = up.astype(BF16)
                a = (gate * _fast_sigmoid(gate) * up).astype(BF16)
                acc = acc + _dot(a, wout_v[cols, :])
        xo_ref[...] = xf + 0.5 * acc

    row = pl.BlockSpec((tm, D), lambda i: (i, 0))
    return pl.pallas_call(
        body, name="ffn_fwd", grid=(T // tm,),
        in_specs=[row, pl.BlockSpec((1, D), lambda i: (0, 0)), ANY, ANY],
        out_specs=[row, pl.BlockSpec((2, tm, 2 * FB), lambda i: (0, i, 0))],
        out_shape=[jax.ShapeDtypeStruct((T, D), F32), jax.ShapeDtypeStruct((2, T, 2 * FB), BF16)],
        scratch_shapes=[pltpu.VMEM(win.shape, BF16), pltpu.VMEM(wout.shape, BF16), pltpu.SemaphoreType.DMA((6,))],
        compiler_params=_cp(1),
    )(x, g, win, wout)


def _mixproj_fwd(x, g, wt):
    T, D = x.shape
    W = wt.shape[0]
    QKV = ATTN_W + 2 * KV_W
    tm = min(TM_MIX, T)

    def body(x_ref, g_ref, w_ref, qkv_ref, u_ref):
        xh, _ = _rms(x_ref[...], None)
        h = (xh * g_ref[...]).astype(BF16)
        qkv_ref[...] = _dot_nt(h, w_ref[:QKV, :]).astype(BF16)
        u_ref[...] = _dot_nt(h, w_ref[QKV:, :])

    return pl.pallas_call(
        body, name="mixproj_fwd", grid=(T // tm,),
        in_specs=[pl.BlockSpec((tm, D), lambda i: (i, 0)), pl.BlockSpec((1, D), lambda i: (0, 0)),
                  pl.BlockSpec((W, D), lambda i: (0, 0))],
        out_specs=[pl.BlockSpec((tm, QKV), lambda i: (i, 0)), pl.BlockSpec((tm, W - QKV), lambda i: (i, 0))],
        out_shape=[jax.ShapeDtypeStruct((T, QKV), BF16), jax.ShapeDtypeStruct((T, W - QKV), F32)],
        compiler_params=_cp(1),
    )(x, g, wt)


def _attn_bias_table():
    rows, cols = GROUP * WINDOW, 2 * WINDOW
    row = lax.broadcasted_iota(jnp.int32, (N_KV, rows, cols), 1)
    col = lax.broadcasted_iota(jnp.int32, (N_KV, rows, cols), 2)
    head = GROUP * lax.broadcasted_iota(jnp.int32, (N_KV, rows, cols), 0) + (row >> 7)
    dist = (row & (WINDOW - 1)) + WINDOW - col
    slope = jnp.exp2(-(head + 1).astype(F32))
    return jnp.where((dist >= 0) & (dist < WINDOW), -slope * dist.astype(F32), NEG_INF)


def _first_block_mask(n):
    col = lax.broadcasted_iota(jnp.int32, (GROUP * WINDOW, 2 * WINDOW), 1)
    return (n > 0) | (col >= WINDOW)


def _sink_col(sink_ref, g):
    hi = lax.broadcasted_iota(jnp.int32, (GROUP * WINDOW, 1), 0) >> 7
    col = jnp.zeros((GROUP * WINDOW, 1), F32)
    for i in range(GROUP):
        col = jnp.where(hi == i, sink_ref[0, GROUP * g + i], col)
    return col


def _stack_heads(ref, g):
    return jnp.concatenate([ref[:, (GROUP * g + i) * HEAD_DIM:(GROUP * g + i + 1) * HEAD_DIM]
                            for i in range(GROUP)], axis=0)


def _band(kvp_ref, kvc_ref, off):
    return jnp.concatenate([kvp_ref[:, off:off + HEAD_DIM], kvc_ref[:, off:off + HEAD_DIM]], axis=0)


def _attn_probs(qs, k, bias, seen, sink):
    s = jnp.where(seen, _dot_nt(qs, k) * SCALE + bias, NEG_INF)
    m = jnp.maximum(jnp.max(s, axis=-1, keepdims=True), sink)
    p = jnp.exp(s - m)
    es = jnp.exp(sink - m)
    inv = 1.0 / (jnp.sum(p, axis=-1, keepdims=True) + es)
    return p * inv, es * inv


def _attn_fwd(sinks, tab, qkv):
    T = qkv.shape[0]
    nb = T // WINDOW

    def body(sink_ref, tab_ref, q_ref, kvp_ref, kvc_ref, o_ref):
        seen = _first_block_mask(pl.program_id(0))
        for g in range(N_KV):
            qs = _stack_heads(q_ref, g)
            k = _band(kvp_ref, kvc_ref, g * HEAD_DIM)
            v = _band(kvp_ref, kvc_ref, KV_W + g * HEAD_DIM)
            p, _ = _attn_probs(qs, k, tab_ref[g], seen, _sink_col(sink_ref, g))
            o = _dot(p.astype(BF16), v)
            for i in range(GROUP):
                h = GROUP * g + i
                o_ref[:, h * HEAD_DIM:(h + 1) * HEAD_DIM] = o[i * WINDOW:(i + 1) * WINDOW].astype(BF16)

    return pl.pallas_call(
        body, name="attn_fwd", grid=(nb,),
        in_specs=[pl.BlockSpec(memory_space=pltpu.SMEM),
                  pl.BlockSpec(tab.shape, lambda n: (0, 0, 0)),
                  pl.BlockSpec((WINDOW, ATTN_W), lambda n: (n, 0)),
                  pl.BlockSpec((WINDOW, 2 * KV_W), lambda n: (jnp.maximum(n - 1, 0), 2)),
                  pl.BlockSpec((WINDOW, 2 * KV_W), lambda n: (n, 2))],
        out_specs=pl.BlockSpec((WINDOW, ATTN_W), lambda n: (n, 0)),
        out_shape=jax.ShapeDtypeStruct((T, ATTN_W), BF16),
        compiler_params=_cp(1),
    )(sinks, tab, qkv, qkv, qkv)


def _shift_copies(src_ref, dst_ref, n):
    for b in range(1, 8):
        dst_ref[b - 1] = src_ref[b:b + n, :]


def _tap(src_ref, sh_ref, s, c0):
    a, b = divmod(s, 8)
    start = pl.multiple_of(c0 + 8 * a, 8)
    if b == 0:
        return src_ref[pl.ds(start, CONV_ROWS), :]
    return sh_ref[b - 1, pl.ds(start, CONV_ROWS), :]


def _glu_rows(u, ch):
    return u[:, :ch] * _fast_sigmoid(u[:, ch:])


def _fill_z(zs_ref, zsh_ref, uc_ref, up_ref, i, ch, n):
    zs_ref[0:HALO] = jnp.where(i > 0, _glu_rows(up_ref[...], ch), 0.0)
    zs_ref[HALO:] = _glu_rows(uc_ref[...], ch)
    _shift_copies(zs_ref, zsh_ref, n - 8)


def _conv_fwd(u, w, b, lg, lb):
    T = u.shape[0]
    CH = u.shape[1] // 2
    tm = min(TM, T)
    n = tm + HALO
    hb = tm // HALO

    def body(uc_ref, up_ref, w_ref, b_ref, lg_ref, lb_ref, conv_ref, ypre_ref, zs_ref, zsh_ref):
        i = pl.program_id(0)
        _fill_z(zs_ref, zsh_ref, uc_ref, up_ref, i, CH, n)
        bias = b_ref[...]

        def chunk(ci, carry):
            c0 = pl.multiple_of(ci * CONV_ROWS, CONV_ROWS)
            acc = jnp.broadcast_to(bias, (CONV_ROWS, CH))
            for k in range(CONV_W):
                acc = acc + w_ref[k:k + 1, :] * _tap(zs_ref, zsh_ref, HALO - (CONV_W - 1) + k, c0)
            ypre_ref[pl.ds(c0, CONV_ROWS), :] = acc
            return carry

        lax.fori_loop(0, tm // CONV_ROWS, chunk, 0)
        y = ypre_ref[...]
        mu = jnp.mean(y, axis=-1, keepdims=True)
        d = y - mu
        var = jnp.mean(d * d, axis=-1, keepdims=True)
        o = d * lax.rsqrt(var + EPS) * lg_ref[...] + lb_ref[...]
        conv_ref[...] = (o * _fast_sigmoid(o)).astype(BF16)

    vec = pl.BlockSpec((1, CH), lambda i: (0, 0))
    return pl.pallas_call(
        body, name="conv_fwd", grid=(T // tm,),
        in_specs=[pl.BlockSpec((tm, 2 * CH), lambda i: (i, 0)),
                  pl.BlockSpec((HALO, 2 * CH), lambda i: (jnp.maximum(i * hb - 1, 0), 0)),
                  pl.BlockSpec((CONV_W, CH), lambda i: (0, 0)), vec, vec, vec],
        out_specs=[pl.BlockSpec((tm, CH), lambda i: (i, 0)), pl.BlockSpec((tm, CH), lambda i: (i, 0))],
        out_shape=[jax.ShapeDtypeStruct((T, CH), BF16), jax.ShapeDtypeStruct((T, CH), F32)],
        scratch_shapes=[pltpu.VMEM((n, CH), F32), pltpu.VMEM((7, n - 8, CH), F32)],
        compiler_params=_cp(1),
    )(u, u, w, b, lg, lb)


def _mixout_fwd(x, attn, conv, wo):
    T, D = x.shape
    tm = min(TM_MIX, T)
    A = attn.shape[1]

    def body(x_ref, a_ref, c_ref, w_ref, xo_ref):
        xo_ref[...] = x_ref[...] + _dot(a_ref[...], w_ref[:A, :]) + _dot(c_ref[...], w_ref[A:, :])

    return pl.pallas_call(
        body, name="mixout_fwd", grid=(T // tm,),
        in_specs=[pl.BlockSpec((tm, D), lambda i: (i, 0)), pl.BlockSpec((tm, A), lambda i: (i, 0)),
                  pl.BlockSpec((tm, conv.shape[1]), lambda i: (i, 0)), pl.BlockSpec(wo.shape, lambda i: (0, 0))],
        out_specs=pl.BlockSpec((tm, D), lambda i: (i, 0)),
        out_shape=jax.ShapeDtypeStruct((T, D), F32),
        compiler_params=_cp(1),
    )(x, attn, conv, wo)


def _rms_bwd_rows(dh, xf, g):
    xh, r = _rms(xf, None)
    dxn = dh * g
    dx = r * (dxn - xh * jnp.mean(dxn * xh, axis=-1, keepdims=True))
    return dx, jnp.sum(dh * xh, axis=0, keepdims=True), xh * g


def _loss_head(x, g, tgt):
    T, D = x.shape
    tm = min(TM, T)

    def body(x_ref, g_ref, t_ref, loss_ref, dx_ref, dg_ref):
        @pl.when(pl.program_id(0) == 0)
        def _():
            loss_ref[...] = jnp.zeros_like(loss_ref)
            dg_ref[...] = jnp.zeros_like(dg_ref)

        xf = x_ref[...]
        g = g_ref[...]
        xh, _ = _rms(xf, None)
        e = xh * g - t_ref[...]
        loss_ref[...] += 0.5 * jnp.sum(jnp.mean(e * e, axis=-1, keepdims=True), axis=0, keepdims=True)
        dx, dg, _ = _rms_bwd_rows(e * (1.0 / D), xf, g)
        dx_ref[...] = dx
        dg_ref[...] += dg

    return pl.pallas_call(
        body, name="loss_head", grid=(T // tm,),
        in_specs=[pl.BlockSpec((tm, D), lambda i: (i, 0)), pl.BlockSpec((1, D), lambda i: (0, 0)),
                  pl.BlockSpec((tm, D), lambda i: (i, 0))],
        out_specs=[pl.BlockSpec((1, 1), lambda i: (0, 0)), pl.BlockSpec((tm, D), lambda i: (i, 0)),
                   pl.BlockSpec((1, D), lambda i: (0, 0))],
        out_shape=[jax.ShapeDtypeStruct((1, 1), F32), jax.ShapeDtypeStruct((T, D), F32),
                   jax.ShapeDtypeStruct((1, D), F32)],
        compiler_params=_cp(1),
    )(x, g, tgt)


def _ffn_bwd(dxo, x, g, gu, win, wout, dep):
    T, D = x.shape
    FB = win.shape[2]
    tm = min(TM_FFN_BWD, T)

    def body(dxo_ref, x_ref, g_ref, gu_ref, win_hbm, wout_hbm, dep_ref,
             dxi_ref, dg_ref, hb_ref, dgu_ref, a_ref, dyb_ref, win_v, wout_v, sems):
        @pl.when(pl.program_id(0) == 0)
        def _():
            _load_ffn_weights(win_hbm, wout_hbm, win_v, wout_v, sems)
            dg_ref[...] = jnp.zeros_like(dg_ref)

        dyb = (0.5 * dxo_ref[...]).astype(BF16)
        dyb_ref[...] = dyb
        dh = jnp.zeros((tm, D), F32)
        for blk in range(2):
            cols = pl.ds(blk * FB, FB)
            da = _dot_nt(dyb, wout_v[cols, :])
            gate = gu_ref[0, :, cols].astype(F32)
            up = gu_ref[1, :, cols].astype(F32)
            sg = _fast_sigmoid(gate)
            s = gate * sg
            a_ref[:, cols] = (s * up).astype(BF16)
            dgate = (da * up * (sg + s * (1.0 - sg))).astype(BF16)
            dup = (da * s).astype(BF16)
            dgu_ref[0, :, cols] = dgate
            dgu_ref[1, :, cols] = dup
            dh = dh + _dot_nt(dgate, win_v[blk]) + _dot_nt(dup, win_v[2 + blk])
        dx, dg, h = _rms_bwd_rows(dh, x_ref[...], g_ref[...])
        dxi_ref[...] = dxo_ref[...] + dx
        dg_ref[...] += dg
        hb_ref[...] = h.astype(BF16)

    row = pl.BlockSpec((tm, D), lambda i: (i, 0))
    act = pl.BlockSpec((2, tm, 2 * FB), lambda i: (0, i, 0))
    return pl.pallas_call(
        body, name="ffn_bwd", grid=(T // tm,),
        in_specs=[row, row, pl.BlockSpec((1, D), lambda i: (0, 0)), act, ANY, ANY, ANY],
        out_specs=[row, pl.BlockSpec((1, D), lambda i: (0, 0)), row, act,
                   pl.BlockSpec((tm, 2 * FB), lambda i: (i, 0)), row],
        out_shape=[jax.ShapeDtypeStruct((T, D), F32), jax.ShapeDtypeStruct((1, D), F32),
                   jax.ShapeDtypeStruct((T, D), BF16), jax.ShapeDtypeStruct((2, T, 2 * FB), BF16),
                   jax.ShapeDtypeStruct((T, 2 * FB), BF16), jax.ShapeDtypeStruct((T, D), BF16)],
        scratch_shapes=[pltpu.VMEM(win.shape, BF16), pltpu.VMEM(wout.shape, BF16), pltpu.SemaphoreType.DMA((6,))],
        compiler_params=_cp(1),
    )(dxo, x, g, gu, win, wout, dep)


def _mix_rms_bwd(dxo, x, g, dzs, wt):
    T, D = x.shape
    tm = min(TM, T)
    npair = len(dzs)

    def body(*refs):
        dxo_ref, x_ref, g_ref = refs[:3]
        dz_refs, w_ref = refs[3:3 + npair], refs[3 + npair]
        dxi_ref, dg_ref, hb_ref = refs[4 + npair:]

        @pl.when(pl.program_id(0) == 0)
        def _():
            dg_ref[...] = jnp.zeros_like(dg_ref)

        dh = jnp.zeros((tm, D), F32)
        k0 = 0
        for dz_ref in dz_refs:
            kp = dz_ref.shape[1]
            dh = dh + _dot(dz_ref[...], w_ref[k0:k0 + kp, :])
            k0 += kp
        dx, dg, h = _rms_bwd_rows(dh, x_ref[...], g_ref[...])
        dxi_ref[...] = dxo_ref[...] + dx
        dg_ref[...] += dg
        hb_ref[...] = h.astype(BF16)

    row = pl.BlockSpec((tm, D), lambda i: (i, 0))
    return pl.pallas_call(
        body, name="mix_rms_bwd", grid=(T // tm,),
        in_specs=[row, row, pl.BlockSpec((1, D), lambda i: (0, 0))]
                 + [pl.BlockSpec((tm, dz.shape[1]), lambda i: (i, 0)) for dz in dzs]
                 + [pl.BlockSpec(wt.shape, lambda i: (0, 0))],
        out_specs=[row, pl.BlockSpec((1, D), lambda i: (0, 0)), row],
        out_shape=[jax.ShapeDtypeStruct((T, D), F32), jax.ShapeDtypeStruct((1, D), F32),
                   jax.ShapeDtypeStruct((T, D), BF16)],
        compiler_params=_cp(1),
    )(dxo, x, g, *dzs, wt)


def _wgrad(name, a, b, a_spec, b_spec, out_shape, out_spec, nblk, dep, acc_shape):
    T = a.shape[0]
    tk = min(TK_WGRAD, T)
    nk = T // tk

    def body(a_ref, b_ref, dep_ref, o_ref, acc_ref):
        k = pl.program_id(1)

        @pl.when(k == 0)
        def _():
            acc_ref[...] = jnp.zeros_like(acc_ref)

        acc_ref[...] += _dot_tn(a_ref[...], b_ref[...])

        @pl.when(k == nk - 1)
        def _():
            o_ref[...] = acc_ref[...].reshape(o_ref.shape).astype(BF16)

    return pl.pallas_call(
        body, name=name, grid=(nblk, nk), in_specs=[a_spec, b_spec, ANY], out_specs=out_spec,
        out_shape=jax.ShapeDtypeStruct(out_shape, BF16), scratch_shapes=[pltpu.VMEM(acc_shape, F32)],
        compiler_params=_cp(2),
    )(a, b, dep)


def _wgrad_ffn_in(hb, dgu, dep):
    T, D = hb.shape
    FB = dgu.shape[2] // 2
    tk = min(TK_WGRAD, T)
    return _wgrad("wgrad_ffn_in", hb, dgu,
                  pl.BlockSpec((tk, D), lambda b, k: (k, 0)),
                  pl.BlockSpec((None, tk, FB), lambda b, k: (b // 2, k, b % 2)),
                  (4, D, FB), pl.BlockSpec((None, D, FB), lambda b, k: (b, 0, 0)), 4, dep, (D, FB))


def _wgrad_ffn_out(a, dyb, dep):
    T, D = dyb.shape
    FB = a.shape[1] // 2
    tk = min(TK_WGRAD, T)
    return _wgrad("wgrad_ffn_out", a, dyb,
                  pl.BlockSpec((tk, FB), lambda b, k: (k, b)),
                  pl.BlockSpec((tk, D), lambda b, k: (k, 0)),
                  (4, FB // 2, D), pl.BlockSpec((2, FB // 2, D), lambda b, k: (b, 0, 0)), 2, dep, (FB, D))


def _wgrad_cat(a_list, b_list):
    T = a_list[0].shape[0]
    tk = min(TK_WGRAD, T)
    nk = T // tk
    na = len(a_list)
    M, N = sum(a.shape[1] for a in a_list), sum(b.shape[1] for b in b_list)

    def body(*refs):
        a_refs, b_refs, o_ref, acc_ref = refs[:na], refs[na:-2], refs[-2], refs[-1]
        k = pl.program_id(0)

        @pl.when(k == 0)
        def _():
            acc_ref[...] = jnp.zeros_like(acc_ref)

        r0 = 0
        for a_ref in a_refs:
            c0 = 0
            for b_ref in b_refs:
                m, n = a_ref.shape[1], b_ref.shape[1]
                acc_ref[r0:r0 + m, c0:c0 + n] += _dot_tn(a_ref[...], b_ref[...])
                c0 += n
            r0 += a_ref.shape[1]

        @pl.when(k == nk - 1)
        def _():
            o_ref[...] = acc_ref[...].astype(BF16)

    return pl.pallas_call(
        body, name="wgrad_cat", grid=(nk,),
        in_specs=[pl.BlockSpec((tk, v.shape[1]), lambda k: (k, 0)) for v in list(a_list) + list(b_list)],
        out_specs=pl.BlockSpec((M, N), lambda k: (0, 0)),
        out_shape=jax.ShapeDtypeStruct((M, N), BF16), scratch_shapes=[pltpu.VMEM((M, N), F32)],
        compiler_params=_cp(1),
    )(*a_list, *b_list)


def _mixout_bwd(dxo, wo):
    T, D = dxo.shape
    tm = min(TM_MIX, T)
    A = ATTN_W
    C = wo.shape[0] - A

    def body(dxo_ref, w_ref, dyb_ref, da_ref, dc_ref):
        dyb = dxo_ref[...].astype(BF16)
        dyb_ref[...] = dyb
        da_ref[...] = _dot_nt(dyb, w_ref[:A, :]).astype(BF16)
        dc_ref[...] = _dot_nt(dyb, w_ref[A:, :])

    return pl.pallas_call(
        body, name="mixout_bwd", grid=(T // tm,),
        in_specs=[pl.BlockSpec((tm, D), lambda i: (i, 0)), pl.BlockSpec(wo.shape, lambda i: (0, 0))],
        out_specs=[pl.BlockSpec((tm, D), lambda i: (i, 0)), pl.BlockSpec((tm, A), lambda i: (i, 0)),
                   pl.BlockSpec((tm, C), lambda i: (i, 0))],
        out_shape=[jax.ShapeDtypeStruct((T, D), BF16), jax.ShapeDtypeStruct((T, A), BF16),
                   jax.ShapeDtypeStruct((T, C), F32)],
        compiler_params=_cp(1),
    )(dxo, wo)


def _conv_bwd(dconv, ypre, u, w, lg, lb):
    T, CH = dconv.shape
    tm = min(TM, T)
    n = tm + HALO
    hb = tm // HALO
    nt = T // tm
    nchunk = tm // CONV_ROWS

    def body(dc_ref, dcn_ref, yp_ref, ypn_ref, uc_ref, up_ref, w_ref, lg_ref, lb_ref,
             du_ref, dw_ref, dvec_ref, zs_ref, zsh_ref, dy_ref, dysh_ref, dz_ref, dwacc_ref):
        i = pl.program_id(0)

        @pl.when(i == 0)
        def _():
            dwacc_ref[...] = jnp.zeros_like(dwacc_ref)
            dvec_ref[...] = jnp.zeros_like(dvec_ref)

        g, bb = lg_ref[...], lb_ref[...]

        def ln_bwd(dc, yp):
            mu = jnp.mean(yp, axis=-1, keepdims=True)
            d = yp - mu
            rs = lax.rsqrt(jnp.mean(d * d, axis=-1, keepdims=True) + EPS)
            yn = d * rs
            o = yn * g + bb
            sg = _fast_sigmoid(o)
            do = dc * (sg * (1.0 + o * (1.0 - sg)))
            dyn = do * g
            dyp = rs * (dyn - jnp.mean(dyn, axis=-1, keepdims=True)
                        - yn * jnp.mean(dyn * yn, axis=-1, keepdims=True))
            return dyp, do, yn

        dyp, do, yn = ln_bwd(dc_ref[...], yp_ref[...])
        dvec_ref[0:1, :] += jnp.sum(dyp, axis=0, keepdims=True)
        dvec_ref[1:2, :] += jnp.sum(do * yn, axis=0, keepdims=True)
        dvec_ref[2:3, :] += jnp.sum(do, axis=0, keepdims=True)
        dy_ref[0:tm] = dyp
        dyh, _, _ = ln_bwd(dcn_ref[...], ypn_ref[...])
        dy_ref[tm:] = jnp.where(i < nt - 1, dyh, 0.0)
        _shift_copies(dy_ref, dysh_ref, n - 8)
        _fill_z(zs_ref, zsh_ref, uc_ref, up_ref, i, CH, n)

        def chunk(ci, carry):
            c0 = pl.multiple_of(ci * CONV_ROWS, CONV_ROWS)
            acc = jnp.zeros((CONV_ROWS, CH), F32)
            for k in range(CONV_W):
                acc = acc + w_ref[k:k + 1, :] * _tap(dy_ref, dysh_ref, CONV_W - 1 - k, c0)
            dz_ref[pl.ds(c0, CONV_ROWS), :] = acc
            dyc = dy_ref[pl.ds(c0, CONV_ROWS), :]
            for k in range(CONV_W):
                prod = dyc * _tap(zs_ref, zsh_ref, HALO - (CONV_W - 1) + k, c0)
                dwacc_ref[k] += jnp.sum(prod.reshape(CONV_ROWS // 8, 8, CH), axis=0)
            return carry

        lax.fori_loop(0, nchunk, chunk, 0)

        @pl.when(i == nt - 1)
        def _():
            dw_ref[...] = jnp.sum(dwacc_ref[...], axis=1)

        uc = uc_ref[...]
        a = uc[:, :CH]
        sg = _fast_sigmoid(uc[:, CH:])
        dz = dz_ref[...]
        du_ref[:, :CH] = (dz * sg).astype(BF16)
        du_ref[:, CH:] = (dz * a * sg * (1.0 - sg)).astype(BF16)

    cur = lambda c: pl.BlockSpec((tm, c), lambda i: (i, 0))
    nxt = lambda c: pl.BlockSpec((HALO, c), lambda i: (jnp.minimum((i + 1) * hb, T // HALO - 1), 0))
    vec = pl.BlockSpec((1, CH), lambda i: (0, 0))
    return pl.pallas_call(
        body, name="conv_bwd", grid=(nt,),
        in_specs=[cur(CH), nxt(CH), cur(CH), nxt(CH), cur(2 * CH),
                  pl.BlockSpec((HALO, 2 * CH), lambda i: (jnp.maximum(i * hb - 1, 0), 0)),
                  pl.BlockSpec((CONV_W, CH), lambda i: (0, 0)), vec, vec],
        out_specs=[pl.BlockSpec((tm, 2 * CH), lambda i: (i, 0)), pl.BlockSpec((32, CH), lambda i: (0, 0)),
                   pl.BlockSpec((8, CH), lambda i: (0, 0))],
        out_shape=[jax.ShapeDtypeStruct((T, 2 * CH), BF16), jax.ShapeDtypeStruct((32, CH), F32),
                   jax.ShapeDtypeStruct((8, CH), F32)],
        scratch_shapes=[pltpu.VMEM((n, CH), F32), pltpu.VMEM((7, n - 8, CH), F32),
                        pltpu.VMEM((n, CH), F32), pltpu.VMEM((7, n - 8, CH), F32), pltpu.VMEM((tm, CH), F32),
                        pltpu.VMEM((32, 8, CH), F32)],
        compiler_params=_cp(1),
    )(dconv, dconv, ypre, ypre, u, u, w, lg, lb)


def _attn_bwd(sinks, tab, qkv, dattn):
    T = qkv.shape[0]
    nb = T // WINDOW

    def body(sink_ref, tab_ref, q_ref, kvp_ref, kvc_ref, do_ref, dq_ref, dkv_ref, dsk_ref, carry_ref):
        n = pl.program_id(0)

        @pl.when(n == 0)
        def _():
            dsk_ref[...] = jnp.zeros_like(dsk_ref)
            carry_ref[...] = jnp.zeros_like(carry_ref)

        @pl.when(n < nb)
        def _():
            seen = _first_block_mask(n)
            for g in range(N_KV):
                qs = _stack_heads(q_ref, g)
                dos = _stack_heads(do_ref, g)
                k = _band(kvp_ref, kvc_ref, g * HEAD_DIM)
                v = _band(kvp_ref, kvc_ref, KV_W + g * HEAD_DIM)
                p, ps = _attn_probs(qs, k, tab_ref[g], seen, _sink_col(sink_ref, g))
                dp = _dot_nt(dos, v)
                delta = jnp.sum(p * dp, axis=-1, keepdims=True)
                dsb = (p * (dp - delta)).astype(BF16)
                dsink = -ps * delta
                dqs = _dot(dsb, k) * SCALE
                dk = _dot_tn(dsb, qs) * SCALE
                dv = _dot_tn(p.astype(BF16), dos)
                for i in range(GROUP):
                    h = GROUP * g + i
                    dq_ref[:, h * HEAD_DIM:(h + 1) * HEAD_DIM] = dqs[i * WINDOW:(i + 1) * WINDOW].astype(BF16)
                    dsk_ref[h:h + 1, :] += jnp.sum(dsink[i * WINDOW:(i + 1) * WINDOW], axis=0, keepdims=True)
                for off, d in ((g * HEAD_DIM, dk), (KV_W + g * HEAD_DIM, dv)):
                    dkv_ref[:, off:off + HEAD_DIM] = (carry_ref[:, off:off + HEAD_DIM] + d[:WINDOW]).astype(BF16)
                    carry_ref[:, off:off + HEAD_DIM] = d[WINDOW:]

        @pl.when(n == nb)
        def _():
            dkv_ref[...] = carry_ref[...].astype(BF16)

    last = nb - 1
    return pl.pallas_call(
        body, name="attn_bwd", grid=(nb + 1,),
        in_specs=[pl.BlockSpec(memory_space=pltpu.SMEM),
                  pl.BlockSpec(tab.shape, lambda n: (0, 0, 0)),
                  pl.BlockSpec((WINDOW, ATTN_W), lambda n: (jnp.minimum(n, last), 0)),
                  pl.BlockSpec((WINDOW, 2 * KV_W), lambda n: (jnp.clip(n - 1, 0, last), 2)),
                  pl.BlockSpec((WINDOW, 2 * KV_W), lambda n: (jnp.minimum(n, last), 2)),
                  pl.BlockSpec((WINDOW, ATTN_W), lambda n: (jnp.minimum(n, last), 0))],
        out_specs=[pl.BlockSpec((WINDOW, ATTN_W), lambda n: (jnp.minimum(n, last), 0)),
                   pl.BlockSpec((WINDOW, 2 * KV_W), lambda n: (jnp.maximum(n - 1, 0), 0)),
                   pl.BlockSpec((8, LANES), lambda n: (0, 0))],
        out_shape=[jax.ShapeDtypeStruct((T, ATTN_W), BF16), jax.ShapeDtypeStruct((T, 2 * KV_W), BF16),
                   jax.ShapeDtypeStruct((8, LANES), F32)],
        scratch_shapes=[pltpu.VMEM((WINDOW, 2 * KV_W), F32)],
        compiler_params=_cp(1),
    )(sinks, tab, qkv, qkv, qkv, dattn)


def _pack(arrs):
    flat = jnp.concatenate([a.reshape(-1) for a in arrs])
    pad = -flat.shape[0] % (8 * LANES)
    return jnp.pad(flat, (0, pad)).reshape(1, -1, LANES)


def _unpack(packed, like):
    flat = packed.reshape(-1)
    out, off = [], 0
    for a in like:
        out.append(flat[off:off + a.size].reshape(a.shape))
        off += a.size
    return out


def kernel(x, norm_ffn1, w_ffn1_in, w_ffn1_out, norm_mix, w_in, sinks, w_dw, b_dw, conv_ln_g, conv_ln_b, w_out, norm_ffn2, w_ffn2_in, w_ffn2_out, final_norm, loss_target, m_norm_ffn1, m_w_ffn1_in, m_w_ffn1_out, m_norm_mix, m_w_in, m_sinks, m_w_dw, m_b_dw, m_conv_ln_g, m_conv_ln_b, m_w_out, m_norm_ffn2, m_w_ffn2_in, m_w_ffn2_out, m_final_norm, v_norm_ffn1, v_w_ffn1_in, v_w_ffn1_out, v_norm_mix, v_w_in, v_sinks, v_w_dw, v_b_dw, v_conv_ln_g, v_conv_ln_b, v_w_out, v_norm_ffn2, v_w_ffn2_in, v_w_ffn2_out, v_final_norm):
    L, D = norm_ffn1.shape
    T = x.shape[1]
    FB = w_ffn1_in.shape[2]
    CH = b_dw.shape[1]
    QKV = ATTN_W + 2 * KV_W
    xs = x.reshape(T, D)
    tgt = loss_target.reshape(T, D)
    cx, cy, cc = lax.axis_index("x"), lax.axis_index("y"), lax.axis_index("c")
    chip = 2 * cx + cy
    cidx = cc.reshape(1).astype(jnp.int32)
    tr = lambda a_: jnp.transpose(a_, (0, 2, 1))
    big_w = (w_ffn1_in, w_ffn1_out, tr(w_in), w_out, w_ffn2_in, w_ffn2_out)
    big_m = (m_w_ffn1_in, m_w_ffn1_out, tr(m_w_in), m_w_out, m_w_ffn2_in, m_w_ffn2_out)
    big_v = (v_w_ffn1_in, v_w_ffn1_out, tr(v_w_in), v_w_out, v_w_ffn2_in, v_w_ffn2_out)
    NW = len(big_w) + 1

    def own_slot(a, slots=4, idx=chip):
        return lax.dynamic_update_index_in_dim(lax.empty((slots,) + a.shape, a.dtype), a, idx, 0)

    def shards(l, tok):
        return [own_slot((w_[l] + tok[0, 0]).astype(BF16)) for w_ in big_w] + [own_slot(w_dw[l] + tok[0, 0])]

    def gather_start(lands, tok):
        return _xchg_start("gather_start", [], lands, _gather_plan, tok)

    def gather_arrived(started, after, n, taps):
        _, lands, tok = _xchg_wait("gather_wait", started, 0, n, _gather_plan, after)
        return _xchg_start("gshare_start", [], lands[:-1] if taps else lands, _gshare_plan, tok, "sibling3"), lands[-1]

    def shared_weights(shared, after, n):
        _, mats, tok = _xchg_wait("gshare_wait", shared, 0, n, _gshare_plan, after, "sibling3")
        return mats, tok

    row = lambda a, l: a[l].reshape(1, -1)
    tab = _attn_bias_table()
    NB = len(big_w)

    saved, W = [], []
    zero_tok = jnp.zeros((8, LANES), F32)
    src0 = shards(0, zero_tok)
    started = gather_start(src0[:2], zero_tok)
    rest0 = gather_start(src0[2:], started[-1])
    cast = [None] + [shards(l, rest0[-1]) for l in range(1, L)]
    shared, _ = gather_arrived(started, [xs] + [a_ for c_ in cast[1:] for a_ in c_], 2, False)
    after = [shared[-1]]
    for l in range(L):
        mats, tok = shared_weights(shared, after, 2 if l == 0 else NB)
        started = None
        if l + 1 < L:
            started = gather_start(cast[l + 1], tok)
            tok = started[-1]
        x0 = xs
        x1, gu1 = _ffn_fwd(x0, row(norm_ffn1, l) + tok[0, 0], mats[0], mats[1].reshape(2 * FB, D))
        gm_row = row(norm_mix, l)
        if l == 0:
            shared, gdw = gather_arrived(rest0, [x1], NW - 2, True)
            rest, tok = shared_weights(shared, [shared[-1]], NB - 2)
            mats = list(mats) + list(rest)
            gm_row = gm_row + tok[0, 0]
        g1i, g1o, gi, go, g2i, g2o = mats
        w = dict(f1i=g1i, f1o=g1o.reshape(2 * FB, D), f2i=g2i, f2o=g2o.reshape(2 * FB, D),
                 wit=gi.reshape(-1, D), wo=go.reshape(-1, D),
                 wdw=jnp.transpose(gdw, (1, 0, 2)).reshape(CONV_W, CH))
        W.append(w)
        qkv, u = _mixproj_fwd(x1, gm_row, w["wit"])
        attn = _attn_fwd(row(sinks, l), tab, qkv)
        conv, ypre = _conv_fwd(u, w["wdw"], row(b_dw, l), row(conv_ln_g, l), row(conv_ln_b, l))
        x2 = _mixout_fwd(x1, attn, conv, w["wo"])
        g2_row = row(norm_ffn2, l)
        if started is not None and l > 0:
            shared, gdw = gather_arrived(started, [x2], NW, True)
            g2_row = g2_row + shared[-1][0, 0]
        xs, gu2 = _ffn_fwd(x2, g2_row, w["f2i"], w["f2o"])
        if started is not None and l == 0:
            shared, gdw = gather_arrived(started, [xs], NW, True)
        saved.append((x0, gu1, x1, qkv, u, attn, conv, ypre, x2, gu2))
        after = [xs]

    loss_part, dx, d_final = _loss_head(xs, final_norm.reshape(1, D), tgt)
    loss = lax.psum(loss_part[0, 0], ("x", "y", "c"))

    bufs = [[lax.empty(w_.shape, F32) for _ in range(4)] for w_ in big_w]
    d_n1, d_nm, d_n2 = [None] * L, [None] * L, [None] * L
    d_sk, d_bdw, d_lg, d_lb, d_wdw = [None] * L, [None] * L, [None] * L, [None] * L, [None] * L

    me_idx = 4 * cx + 2 * cy + cc

    def reduce_start(gs):
        lands = []
        for g in gs:
            h = g.shape[1] // 2
            mine = lax.dynamic_slice(g, (chip, cc * h, 0), (1, h, g.shape[2]))[0]
            lands.append(own_slot(mine, 8, me_idx))
        return _xchg_start("rs_start", gs, lands, _rs_plan, zero_tok, "all")

    def share_start(rs_started, after, n):
        _, qs, tok = _xchg_wait("rs_wait", rs_started, n, n, _rs_plan, after, "all")
        return _xchg_start("qshare_start", qs, [lax.empty(q.shape, q.dtype) for q in qs], _whole_plan, tok, "sibling")

    def finish(l, shared, after, idxs):
        q_own, q_sib, _ = _xchg_wait("qshare_wait", shared, len(idxs), len(idxs), _whole_plan, after, "sibling")
        for k, t in enumerate(idxs):
            bufs[t] = _adamw_layer(cidx, q_own[k], q_sib[k], big_w[t], big_m[t], big_v[t], bufs[t], l)

    ALL = list(range(NB))
    EARLY, LATE = ALL[2:], ALL[:2]
    rs_list, shares = [], []
    tok = zero_tok
    for l in reversed(range(L)):
        w = W[l]
        x0, gu1, x1, qkv, u, attn, conv, ypre, x2, gu2 = saved[l]
        dx, d_n2[l], hb, dgu, a, dyb = _ffn_bwd(dx, x2, row(norm_ffn2, l), gu2, w["f2i"], w["f2o"], tok)
        g_f2i, g_f2o = _wgrad_ffn_in(hb, dgu, tok), _wgrad_ffn_out(a, dyb, tok)
        lg_row = row(conv_ln_g, l)
        if len(rs_list) >= 2:
            pl_, st_ = rs_list[-2]
            shares.append((pl_, share_start(st_, [g_f2o], NB)))
            lg_row = lg_row + shares[-1][1][-1][0, 0]
        dyb, dattn, dconv = _mixout_bwd(dx, w["wo"])
        g_wo = _wgrad_cat([attn, conv], [dyb]).reshape(4, -1, D)
        du, dwdw, dvec = _conv_bwd(dconv, ypre, u, w["wdw"], lg_row, row(conv_ln_b, l))
        d_wdw[l], d_bdw[l], d_lg[l], d_lb[l] = dwdw[:CONV_W], dvec[0], dvec[1], dvec[2]
        dq, dkv, dsk = _attn_bwd(row(sinks, l), tab, qkv, dattn)
        d_sk[l] = dsk[:, 0]
        dx, d_nm[l], hb = _mix_rms_bwd(dx, x1, row(norm_mix, l), [dq, dkv, du], w["wit"])
        g_wi = _wgrad_cat([dq, dkv, du], [hb]).reshape(4, -1, D)
        if l == 0:
            rs_early = reduce_start([g_wi, g_wo, g_f2i, g_f2o])
            tok = rs_early[-1]
        dx, d_n1[l], hb, dgu, a, dyb = _ffn_bwd(dx, x0, row(norm_ffn1, l), gu1, w["f1i"], w["f1o"], tok)
        g_f1i, g_f1o = _wgrad_ffn_in(hb, dgu, tok), _wgrad_ffn_out(a, dyb, tok)
        rs_started = reduce_start([g_f1i, g_f1o] if l == 0 else [g_f1i, g_f1o, g_wi, g_wo, g_f2i, g_f2o])
        tok = rs_started[-1]
        rs_list.append((l, rs_started))
    grad_x = dx.reshape(x.shape)

    small_g = [jnp.concatenate(d, axis=0) for d in (d_n1, d_nm, d_n2)] + [d_final, jnp.stack(d_sk)] + \
              [jnp.stack(d) for d in (d_bdw, d_lg, d_lb, d_wdw)]
    packed = _pack(small_g)[0]
    small_started = _xchg_start("small_start", [packed], [own_slot(packed, 8, 4 * cx + 2 * cy + cc)], _slot_plan, tok, "all")

    rs_late = rs_list.pop()[1]
    after = [small_started[-1]]
    if len(rs_list) > len(shares):
        pl_, st_ = rs_list[len(shares)]
        shares.append((pl_, share_start(st_, after, NB)))
        after = [shares[-1][1][-1]]
    if shares:
        finish(*shares.pop(0), after, ALL)
        after = [b_[0] for b_ in bufs]
    sh_early = share_start(rs_early, after, len(EARLY))
    after = [sh_early[-1]]
    sh_late = None
    for l, sh in shares:
        finish(l, sh, after, ALL)
        after = [b_[0] for b_ in bufs]
        if sh_late is None:
            sh_late = share_start(rs_late, after, len(LATE))
            after = [sh_late[-1]]
    if sh_late is None:
        sh_late = share_start(rs_late, after, len(LATE))
        after = [sh_late[-1]]
    _, (slots,), _ = _xchg_wait("small_wait", small_started, 1, 1, _slot_plan, after, "all")
    small_sum = _unpack(_sum_slots(slots), small_g)
    g_wdw = lax.dynamic_slice_in_dim(small_sum[8], chip * w_dw.shape[2], w_dw.shape[2], axis=2)
    small_g = [small_sum[0], small_sum[1], small_sum[2], small_sum[3].reshape(D), small_sum[4],
               small_sum[5], small_sum[6], small_sum[7], g_wdw]
    small_w = (norm_ffn1, norm_mix, norm_ffn2, final_norm, sinks, b_dw, conv_ln_g, conv_ln_b, w_dw)
    small_m = (m_norm_ffn1, m_norm_mix, m_norm_ffn2, m_final_norm, m_sinks, m_b_dw, m_conv_ln_g, m_conv_ln_b, m_w_dw)
    small_v = (v_norm_ffn1, v_norm_mix, v_norm_ffn2, v_final_norm, v_sinks, v_b_dw, v_conv_ln_g, v_conv_ln_b, v_w_dw)
    upd = _adamw(_pack(small_g), _pack(small_w), _pack(small_m), _pack(small_v))
    small_upd = [_unpack(u_, small_w) for u_ in upd]
    finish(0, sh_early, [upd[0]], EARLY)
    finish(0, sh_late, [bufs[t][0] for t in EARLY], LATE)

    order = ("norm_ffn1", "w_ffn1_in", "w_ffn1_out", "norm_mix", "w_in", "sinks", "w_dw", "b_dw", "conv_ln_g",
             "conv_ln_b", "w_out", "norm_ffn2", "w_ffn2_in", "w_ffn2_out", "final_norm")
    small_names = ("norm_ffn1", "norm_mix", "norm_ffn2", "final_norm", "sinks", "b_dw", "conv_ln_g", "conv_ln_b", "w_dw")
    big_names = ("w_ffn1_in", "w_ffn1_out", "w_in", "w_out", "w_ffn2_in", "w_ffn2_out")
    grads, deltas, new_m, new_v = {}, {}, {}, {}
    for i, nme in enumerate(small_names):
        grads[nme], deltas[nme], new_m[nme], new_v[nme] = small_g[i], small_upd[0][i], small_upd[1][i], small_upd[2][i]
    for i, nme in enumerate(big_names):
        grads[nme], deltas[nme], new_m[nme], new_v[nme] = [tr(b_) for b_ in bufs[i]] if nme == "w_in" else bufs[i]
    return (loss, grad_x, *[grads[n] for n in order], *[deltas[n] for n in order],
            *[new_m[n] for n in order], *[new_v[n] for n in order])
```

```python
import jax
import jax.numpy as jnp
from jax import lax
from jax.experimental import pallas as pl
from jax.experimental.pallas import tpu as pltpu

F32, BF16 = jnp.float32, jnp.bfloat16
EPS = 1e-6
NEG_INF = -1e30
HEAD_DIM = 64
N_HEADS = 8
N_KV = 2
GROUP = N_HEADS // N_KV
WINDOW = 128
ATTN_W = N_HEADS * HEAD_DIM
KV_W = N_KV * HEAD_DIM
CONV_W = 31
HALO = 32
CONV_ROWS = 32
SCALE = 1.0 / 8.0
ADAM_LR, ADAM_B1, ADAM_B2, ADAM_EPS, ADAM_WD, ADAM_STEP = 0.001, 0.9, 0.999, 1e-08, 0.01, 10
TM = 512
TM_FFN_BWD = 256
TK_WGRAD = 2048
TM_MIX = 1024
LANES = 128
VMEM_LIMIT = 52 * 1024 * 1024
MESH = pl.DeviceIdType.MESH
ANY = pl.BlockSpec(memory_space=pl.ANY)
HBM = pl.BlockSpec(memory_space=pltpu.HBM)
SEM = pl.BlockSpec(memory_space=pltpu.SEMAPHORE)
VMEM = pl.BlockSpec(memory_space=pltpu.VMEM)
EFFECT = pltpu.SideEffectType.DATAFLOW_SIDE_EFFECTING
TOKEN = jax.ShapeDtypeStruct((8, LANES), F32)


def _cp(n):
    return pltpu.CompilerParams(dimension_semantics=("arbitrary",) * n, vmem_limit_bytes=VMEM_LIMIT)


def _dot(a, b):
    return jnp.dot(a, b, preferred_element_type=F32)


def _dot_nt(a, b):
    return lax.dot_general(a, b, (((1,), (1,)), ((), ())), preferred_element_type=F32)


def _dot_tn(a, b):
    return lax.dot_general(a, b, (((0,), (0,)), ((), ())), preferred_element_type=F32)


def _place():
    x, y, c = lax.axis_index("x"), lax.axis_index("y"), lax.axis_index("c")
    chips = [(1 - x, y), (x, 1 - y), (1 - x, 1 - y)]
    return x, y, c, chips


def _rcopy(src, dst, send_sems, recv_sems, k, dev):
    return pltpu.make_async_remote_copy(src_ref=src, dst_ref=dst, send_sem=send_sems.at[k],
                                        recv_sem=recv_sems.at[k], device_id=dev, device_id_type=MESH)


def _hbm(a):
    return pltpu.with_memory_space_constraint(a, pltpu.HBM)


PEERS = {"chips": 3, "sibling": 1, "sibling3": 3, "all": 7}


def _targets(mode):
    x, y, c, chips = _place()
    b = 2 * x + y
    if mode == "chips":
        return b, c, [((px, py, c), 2 * px + py) for px, py in chips]
    if mode == "sibling":
        return b, c, [((x, y, 1 - c), b)]
    if mode == "sibling3":
        return b, c, [((x, y, 1 - c), 2 * px + py) for px, py in chips]
    flip = lambda v, f: 1 - v if f else v
    devs = [(flip(x, k >> 2 & 1), flip(y, k >> 1 & 1), flip(c, k & 1)) for k in range(1, 8)]
    return 4 * x + 2 * y + c, c, [(d, 4 * d[0] + 2 * d[1] + d[2]) for d in devs]


def _xchg_start(name, srcs, lands, plan, dep, mode="chips"):
    ns, nl, npeer = len(srcs), len(lands), PEERS[mode]

    def body(*refs):
        land = refs[ns:ns + nl]
        src = refs[:ns] if ns else land
        send_sems, recv_sems, token = refs[ns + nl + 1], refs[ns + nl + 2], refs[-1]
        me, c, peers = _targets(mode)
        for t in range(nl):
            for j, (dev, tag) in enumerate(peers):
                s, d, _ = plan(src[t], land[t], t, me, c, tag)
                _rcopy(s, d, send_sems, recv_sems, npeer * t + j, dev).start()
        token[...] = jnp.zeros_like(token)

    arrs = list(srcs) + list(lands)
    return pl.pallas_call(
        body, name=name,
        out_shape=(pltpu.SemaphoreType.DMA((npeer * nl,)), pltpu.SemaphoreType.DMA((npeer * nl,)),
                   *[pltpu.HBM(a.shape, a.dtype) for a in arrs], TOKEN),
        in_specs=[HBM] * (ns + nl) + [ANY], out_specs=(SEM, SEM, *[HBM] * (ns + nl), VMEM),
        input_output_aliases={i: 2 + i for i in range(ns + nl)},
        compiler_params=pltpu.CompilerParams(has_side_effects=EFFECT),
    )(*[_hbm(a) for a in arrs], dep)


def _xchg_wait(name, started, ns, nl, plan, after, mode="chips", latest=None):
    send_sems, recv_sems = started[0], started[1]
    thru = started[2:2 + ns + nl] if latest is None else latest
    npeer = PEERS[mode]

    def body(*refs):
        land = refs[ns:ns + nl]
        src = refs[:ns] if ns else land
        send_sems, recv_sems, token = refs[ns + nl], refs[ns + nl + 1], refs[-1]
        me, c, peers = _targets(mode)
        for t in range(nl):
            for j, (dev, tag) in enumerate(peers):
                s, _, a = plan(src[t], land[t], t, me, c, tag)
                cp = _rcopy(s, a, send_sems, recv_sems, npeer * t + j, dev)
                cp.wait_send()
                cp.wait_recv()
        token[...] = jnp.zeros_like(token)

    out = pl.pallas_call(
        body, name=name,
        out_shape=(*[pltpu.HBM(a.shape, a.dtype) for a in thru], TOKEN),
        in_specs=[HBM] * (ns + nl) + [SEM, SEM] + [ANY] * len(after), out_specs=(*[HBM] * (ns + nl), VMEM),
        input_output_aliases={i: i for i in range(ns + nl)},
        compiler_params=pltpu.CompilerParams(has_side_effects=EFFECT),
    )(*thru, send_sems, recv_sems, *after)
    return out[:ns], out[ns:ns + nl], out[-1]


def _half(ref_rows, which):
    h = ref_rows // 2
    return pl.ds(which * h, h)


def _gather_plan(src, land, t, b, c, pb):
    if land.shape[1] % 2 == 0:
        hs = _half(land.shape[1], c)
        return land.at[b, hs], land.at[b, hs], land.at[pb, hs]
    return land.at[b], land.at[b], land.at[pb]


def _gshare_plan(src, land, t, b, c, pb):
    return land.at[pb, _half(land.shape[1], c)], land.at[pb, _half(land.shape[1], c)], land.at[pb, _half(land.shape[1], 1 - c)]


def _rs_plan(src, land, t, me, c, tag):
    h = src.shape[1] // 2
    return src.at[tag // 2, pl.ds((tag % 2) * h, h), :], land.at[me], land.at[tag]


def _rows_block(h, cap=512):
    for rb in range(min(h, cap) // 16 * 16, 0, -16):
        if h % rb == 0:
            return rb
    return h


def _oshare_plan(l):
    def plan(src, land, t, me, c, tag):
        return land.at[l, _half(land.shape[1], c)], land.at[l, _half(land.shape[1], c)], land.at[l, _half(land.shape[1], 1 - c)]
    return plan


def _slot_plan(src, land, t, me, c, tag):
    return src, land.at[me], land.at[tag]


def _adam_update(gg, w, m, v):
    m2 = ADAM_B1 * m + (1.0 - ADAM_B1) * gg
    v2 = ADAM_B2 * v + (1.0 - ADAM_B2) * (gg * gg)
    mh = m2 / (1.0 - ADAM_B1 ** ADAM_STEP)
    vh = v2 / (1.0 - ADAM_B2 ** ADAM_STEP)
    return -ADAM_LR * (mh / (jnp.sqrt(vh) + ADAM_EPS) + ADAM_WD * w), m2, v2


def _adamw_layer(cidx, q, w, m, v, bufs, l):
    L, R, C = w.shape
    h = R // 2
    rb = _rows_block(h, 256)
    nr = h // rb

    def body(c_ref, q_ref, w_ref, m_ref, v_ref, *rest):
        g_ref, d_ref, mo_ref, vo_ref = rest[-4:]
        gg = q_ref[0].astype(F32)
        for s in range(1, 8):
            gg = gg + q_ref[s].astype(F32)
        g_ref[...] = gg
        d_ref[...], mo_ref[...], vo_ref[...] = _adam_update(gg, w_ref[...], m_ref[...], v_ref[...])

    wspec = pl.BlockSpec((None, rb, C), lambda i, c: (l, c[0] * nr + i, 0))
    return pl.pallas_call(
        body, name="adamw_layer", out_shape=[jax.ShapeDtypeStruct(w.shape, F32)] * 4,
        grid_spec=pltpu.PrefetchScalarGridSpec(
            num_scalar_prefetch=1, grid=(nr,),
            in_specs=[pl.BlockSpec((8, rb, C), lambda i, c: (0, i, 0)), wspec, wspec, wspec] + [ANY] * 4,
            out_specs=[wspec] * 4),
        input_output_aliases={5 + k: k for k in range(4)},
        compiler_params=_cp(1),
    )(cidx, q, w, m, v, *bufs)


def _adamw(g, w, m, v):
    L, R, C = g.shape
    rb = _rows_block(R)

    def body(g_ref, w_ref, m_ref, v_ref, d_ref, mo_ref, vo_ref):
        d_ref[...], mo_ref[...], vo_ref[...] = _adam_update(g_ref[...], w_ref[...], m_ref[...], v_ref[...])

    spec = pl.BlockSpec((None, rb, C), lambda l, i: (l, i, 0))
    return pl.pallas_call(
        body, name="adamw", grid=(L, R // rb), in_specs=[spec] * 4, out_specs=[spec] * 3,
        out_shape=[jax.ShapeDtypeStruct(g.shape, F32)] * 3, compiler_params=_cp(2),
    )(g, w, m, v)


def _sum_slots(buf):
    def body(b_ref, o_ref):
        acc = b_ref[0]
        for k in range(1, 8):
            acc = acc + b_ref[k]
        o_ref[...] = acc

    return pl.pallas_call(body, name="sum_slots", in_specs=[VMEM], out_specs=VMEM,
                          out_shape=jax.ShapeDtypeStruct(buf.shape[1:], F32))(buf)


def _rms(xf, g):
    r = lax.rsqrt(jnp.mean(xf * xf, axis=-1, keepdims=True) + EPS)
    return xf * r, r


def _lane_chunks(n):
    lo = (n // LANES + 1) // 2 * LANES
    return ((0, lo), (lo, n - lo))


def _load_ffn_weights(win_hbm, wout_hbm, win_v, wout_v, sems):
    fb = win_v.shape[2]
    loads = [pltpu.make_async_copy(win_hbm.at[k], win_v.at[k], sems.at[k]) for k in range(4)]
    loads += [pltpu.make_async_copy(wout_hbm.at[pl.ds(k * fb, fb)], wout_v.at[pl.ds(k * fb, fb)], sems.at[4 + k])
              for k in range(2)]
    for cp in loads:
        cp.start()
    for cp in loads:
        cp.wait()


def _fast_sigmoid(v):
    return pl.reciprocal(1.0 + jnp.exp(-v), approx=True)


def _ffn_fwd(x, g, win, wout):
    T, D = x.shape
    FB = win.shape[2]
    tm = min(TM, T)

    def body(x_ref, g_ref, win_hbm, wout_hbm, xo_ref, gu_ref, win_v, wout_v, sems):
        @pl.when(pl.program_id(0) == 0)
        def _():
            _load_ffn_weights(win_hbm, wout_hbm, win_v, wout_v, sems)

        xf = x_ref[...]
        xh, _ = _rms(xf, None)
        h = (xh * g_ref[...]).astype(BF16)
        acc = jnp.zeros((tm, D), F32)
        for blk in range(2):
            for lo, sz in _lane_chunks(FB):
                cols = pl.ds(blk * FB + lo, sz)
                gate = _dot(h, win_v[blk, :, pl.ds(lo, sz)])
                up = _dot(h, win_v[2 + blk, :, pl.ds(lo, sz)])
                gu_ref[0, :, cols] = gate.astype(BF16)
                gu_ref[1, :, cols] = up.astype(BF16)
                a = (gate * _fast_sigmoid(gate) * up).astype(BF16)
                acc = acc + _dot(a, wout_v[cols, :])
        xo_ref[...] = xf + 0.5 * acc

    row = pl.BlockSpec((tm, D), lambda i: (i, 0))
    return pl.pallas_call(
        body, name="ffn_fwd", grid=(T // tm,),
        in_specs=[row, pl.BlockSpec((1, D), lambda i: (0, 0)), ANY, ANY],
        out_specs=[row, pl.BlockSpec((2, tm, 2 * FB), lambda i: (0, i, 0))],
        out_shape=[jax.ShapeDtypeStruct((T, D), F32), jax.ShapeDtypeStruct((2, T, 2 * FB), BF16)],
        scratch_shapes=[pltpu.VMEM(win.shape, BF16), pltpu.VMEM(wout.shape, BF16), pltpu.SemaphoreType.DMA((6,))],
        compiler_params=_cp(1),
    )(x, g, win, wout)


def _mixproj_fwd(x, g, wt):
    T, D = x.shape
    W = wt.shape[0]
    QKV = ATTN_W + 2 * KV_W
    tm = min(TM_MIX, T)

    def body(x_ref, g_ref, w_ref, qkv_ref, u_ref):
        xh, _ = _rms(x_ref[...], None)
        h = (xh * g_ref[...]).astype(BF16)
        qkv_ref[...] = _dot_nt(h, w_ref[:QKV, :]).astype(BF16)
        u_ref[...] = _dot_nt(h, w_ref[QKV:, :])

    return pl.pallas_call(
        body, name="mixproj_fwd", grid=(T // tm,),
        in_specs=[pl.BlockSpec((tm, D), lambda i: (i, 0)), pl.BlockSpec((1, D), lambda i: (0, 0)),
                  pl.BlockSpec((W, D), lambda i: (0, 0))],
        out_specs=[pl.BlockSpec((tm, QKV), lambda i: (i, 0)), pl.BlockSpec((tm, W - QKV), lambda i: (i, 0))],
        out_shape=[jax.ShapeDtypeStruct((T, QKV), BF16), jax.ShapeDtypeStruct((T, W - QKV), F32)],
        compiler_params=_cp(1),
    )(x, g, wt)


def _attn_bias_table():
    rows, cols = GROUP * WINDOW, 2 * WINDOW
    row = lax.broadcasted_iota(jnp.int32, (N_KV, rows, cols), 1)
    col = lax.broadcasted_iota(jnp.int32, (N_KV, rows, cols), 2)
    head = GROUP * lax.broadcasted_iota(jnp.int32, (N_KV, rows, cols), 0) + (row >> 7)
    dist = (row & (WINDOW - 1)) + WINDOW - col
    slope = jnp.exp2(-(head + 1).astype(F32))
    return jnp.where((dist >= 0) & (dist < WINDOW), -slope * dist.astype(F32), NEG_INF)


def _first_block_mask(n):
    col = lax.broadcasted_iota(jnp.int32, (GROUP * WINDOW, 2 * WINDOW), 1)
    return (n > 0) | (col >= WINDOW)


def _sink_col(sink_ref, g):
    hi = lax.broadcasted_iota(jnp.int32, (GROUP * WINDOW, 1), 0) >> 7
    col = jnp.zeros((GROUP * WINDOW, 1), F32)
    for i in range(GROUP):
        col = jnp.where(hi == i, sink_ref[0, GROUP * g + i], col)
    return col


def _stack_heads(ref, g):
    return jnp.concatenate([ref[:, (GROUP * g + i) * HEAD_DIM:(GROUP * g + i + 1) * HEAD_DIM]
                            for i in range(GROUP)], axis=0)


def _band(kvp_ref, kvc_ref, off):
    return jnp.concatenate([kvp_ref[:, off:off + HEAD_DIM], kvc_ref[:, off:off + HEAD_DIM]], axis=0)


def _attn_probs(qs, k, bias, seen, sink):
    s = jnp.where(seen, _dot_nt(qs, k) * SCALE + bias, NEG_INF)
    m = jnp.maximum(jnp.max(s, axis=-1, keepdims=True), sink)
    p = jnp.exp(s - m)
    es = jnp.exp(sink - m)
    inv = 1.0 / (jnp.sum(p, axis=-1, keepdims=True) + es)
    return p * inv, es * inv


def _attn_fwd(sinks, tab, qkv):
    T = qkv.shape[0]
    nb = T // WINDOW

    def body(sink_ref, tab_ref, q_ref, kvp_ref, kvc_ref, o_ref):
        seen = _first_block_mask(pl.program_id(0))
        for g in range(N_KV):
            qs = _stack_heads(q_ref, g)
            k = _band(kvp_ref, kvc_ref, g * HEAD_DIM)
            v = _band(kvp_ref, kvc_ref, KV_W + g * HEAD_DIM)
            p, _ = _attn_probs(qs, k, tab_ref[g], seen, _sink_col(sink_ref, g))
            o = _dot(p.astype(BF16), v)
            for i in range(GROUP):
                h = GROUP * g + i
                o_ref[:, h * HEAD_DIM:(h + 1) * HEAD_DIM] = o[i * WINDOW:(i + 1) * WINDOW].astype(BF16)

    return pl.pallas_call(
        body, name="attn_fwd", grid=(nb,),
        in_specs=[pl.BlockSpec(memory_space=pltpu.SMEM),
                  pl.BlockSpec(tab.shape, lambda n: (0, 0, 0)),
                  pl.BlockSpec((WINDOW, ATTN_W), lambda n: (n, 0)),
                  pl.BlockSpec((WINDOW, 2 * KV_W), lambda n: (jnp.maximum(n - 1, 0), 2)),
                  pl.BlockSpec((WINDOW, 2 * KV_W), lambda n: (n, 2))],
        out_specs=pl.BlockSpec((WINDOW, ATTN_W), lambda n: (n, 0)),
        out_shape=jax.ShapeDtypeStruct((T, ATTN_W), BF16),
        compiler_params=_cp(1),
    )(sinks, tab, qkv, qkv, qkv)


def _shift_copies(src_ref, dst_ref, n):
    for b in range(1, 8):
        dst_ref[b - 1] = src_ref[b:b + n, :]


def _tap(src_ref, sh_ref, s, c0):
    a, b = divmod(s, 8)
    start = pl.multiple_of(c0 + 8 * a, 8)
    if b == 0:
        return src_ref[pl.ds(start, CONV_ROWS), :]
    return sh_ref[b - 1, pl.ds(start, CONV_ROWS), :]


def _glu_rows(u, ch):
    return u[:, :ch] * _fast_sigmoid(u[:, ch:])


def _fill_z(zs_ref, zsh_ref, uc_ref, up_ref, i, ch, n):
    zs_ref[0:HALO] = jnp.where(i > 0, _glu_rows(up_ref[...], ch), 0.0)
    zs_ref[HALO:] = _glu_rows(uc_ref[...], ch)
    _shift_copies(zs_ref, zsh_ref, n - 8)


def _conv_fwd(u, w, b, lg, lb):
    T = u.shape[0]
    CH = u.shape[1] // 2
    tm = min(TM, T)
    n = tm + HALO
    hb = tm // HALO

    def body(uc_ref, up_ref, w_ref, b_ref, lg_ref, lb_ref, conv_ref, ypre_ref, zs_ref, zsh_ref):
        i = pl.program_id(0)
        _fill_z(zs_ref, zsh_ref, uc_ref, up_ref, i, CH, n)
        bias = b_ref[...]

        def chunk(ci, carry):
            c0 = pl.multiple_of(ci * CONV_ROWS, CONV_ROWS)
            acc = jnp.broadcast_to(bias, (CONV_ROWS, CH))
            for k in range(CONV_W):
                acc = acc + w_ref[k:k + 1, :] * _tap(zs_ref, zsh_ref, HALO - (CONV_W - 1) + k, c0)
            ypre_ref[pl.ds(c0, CONV_ROWS), :] = acc
            return carry

        lax.fori_loop(0, tm // CONV_ROWS, chunk, 0)
        y = ypre_ref[...]
        mu = jnp.mean(y, axis=-1, keepdims=True)
        d = y - mu
        var = jnp.mean(d * d, axis=-1, keepdims=True)
        o = d * lax.rsqrt(var + EPS) * lg_ref[...] + lb_ref[...]
        conv_ref[...] = (o * _fast_sigmoid(o)).astype(BF16)

    vec = pl.BlockSpec((1, CH), lambda i: (0, 0))
    return pl.pallas_call(
        body, name="conv_fwd", grid=(T // tm,),
        in_specs=[pl.BlockSpec((tm, 2 * CH), lambda i: (i, 0)),
                  pl.BlockSpec((HALO, 2 * CH), lambda i: (jnp.maximum(i * hb - 1, 0), 0)),
                  pl.BlockSpec((CONV_W, CH), lambda i: (0, 0)), vec, vec, vec],
        out_specs=[pl.BlockSpec((tm, CH), lambda i: (i, 0)), pl.BlockSpec((tm, CH), lambda i: (i, 0))],
        out_shape=[jax.ShapeDtypeStruct((T, CH), BF16), jax.ShapeDtypeStruct((T, CH), F32)],
        scratch_shapes=[pltpu.VMEM((n, CH), F32), pltpu.VMEM((7, n - 8, CH), F32)],
        compiler_params=_cp(1),
    )(u, u, w, b, lg, lb)


def _mixout_fwd(x, attn, conv, wo):
    T, D = x.shape
    tm = min(TM_MIX, T)
    A = attn.shape[1]

    def body(x_ref, a_ref, c_ref, w_ref, xo_ref):
        xo_ref[...] = x_ref[...] + _dot(a_ref[...], w_ref[:A, :]) + _dot(c_ref[...], w_ref[A:, :])

    return pl.pallas_call(
        body, name="mixout_fwd", grid=(T // tm,),
        in_specs=[pl.BlockSpec((tm, D), lambda i: (i, 0)), pl.BlockSpec((tm, A), lambda i: (i, 0)),
                  pl.BlockSpec((tm, conv.shape[1]), lambda i: (i, 0)), pl.BlockSpec(wo.shape, lambda i: (0, 0))],
        out_specs=pl.BlockSpec((tm, D), lambda i: (i, 0)),
        out_shape=jax.ShapeDtypeStruct((T, D), F32),
        compiler_params=_cp(1),
    )(x, attn, conv, wo)


def _rms_bwd_rows(dh, xf, g):
    xh, r = _rms(xf, None)
    dxn = dh * g
    dx = r * (dxn - xh * jnp.mean(dxn * xh, axis=-1, keepdims=True))
    return dx, jnp.sum(dh * xh, axis=0, keepdims=True), xh * g


def _loss_head(x, g, tgt):
    T, D = x.shape
    tm = min(TM, T)

    def body(x_ref, g_ref, t_ref, loss_ref, dx_ref, dg_ref):
        @pl.when(pl.program_id(0) == 0)
        def _():
            loss_ref[...] = jnp.zeros_like(loss_ref)
            dg_ref[...] = jnp.zeros_like(dg_ref)

        xf = x_ref[...]
        g = g_ref[...]
        xh, _ = _rms(xf, None)
        e = xh * g - t_ref[...]
        loss_ref[...] += 0.5 * jnp.sum(jnp.mean(e * e, axis=-1, keepdims=True), axis=0, keepdims=True)
        dx, dg, _ = _rms_bwd_rows(e * (1.0 / D), xf, g)
        dx_ref[...] = dx
        dg_ref[...] += dg

    return pl.pallas_call(
        body, name="loss_head", grid=(T // tm,),
        in_specs=[pl.BlockSpec((tm, D), lambda i: (i, 0)), pl.BlockSpec((1, D), lambda i: (0, 0)),
                  pl.BlockSpec((tm, D), lambda i: (i, 0))],
        out_specs=[pl.BlockSpec((1, 1), lambda i: (0, 0)), pl.BlockSpec((tm, D), lambda i: (i, 0)),
                   pl.BlockSpec((1, D), lambda i: (0, 0))],
        out_shape=[jax.ShapeDtypeStruct((1, 1), F32), jax.ShapeDtypeStruct((T, D), F32),
                   jax.ShapeDtypeStruct((1, D), F32)],
        compiler_params=_cp(1),
    )(x, g, tgt)


def _ffn_bwd(dxo, x, g, gu, win, wout, dep):
    T, D = x.shape
    FB = win.shape[2]
    tm = min(TM_FFN_BWD, T)

    def body(dxo_ref, x_ref, g_ref, gu_ref, win_hbm, wout_hbm, dep_ref,
             dxi_ref, dg_ref, hb_ref, dgu_ref, a_ref, dyb_ref, win_v, wout_v, sems):
        @pl.when(pl.program_id(0) == 0)
        def _():
            _load_ffn_weights(win_hbm, wout_hbm, win_v, wout_v, sems)
            dg_ref[...] = jnp.zeros_like(dg_ref)

        dyb = (0.5 * dxo_ref[...]).astype(BF16)
        dyb_ref[...] = dyb
        dh = jnp.zeros((tm, D), F32)
        for blk in range(2):
            cols = pl.ds(blk * FB, FB)
            da = _dot_nt(dyb, wout_v[cols, :])
            gate = gu_ref[0, :, cols].astype(F32)
            up = gu_ref[1, :, cols].astype(F32)
            sg = _fast_sigmoid(gate)
            s = gate * sg
            a_ref[:, cols] = (s * up).astype(BF16)
            dgate = (da * up * (sg + s * (1.0 - sg))).astype(BF16)
            dup = (da * s).astype(BF16)
            dgu_ref[0, :, cols] = dgate
            dgu_ref[1, :, cols] = dup
            dh = dh + _dot_nt(dgate, win_v[blk]) + _dot_nt(dup, win_v[2 + blk])
        dx, dg, h = _rms_bwd_rows(dh, x_ref[...], g_ref[...])
        dxi_ref[...] = dxo_ref[...] + dx
        dg_ref[...] += dg
        hb_ref[...] = h.astype(BF16)

    row = pl.BlockSpec((tm, D), lambda i: (i, 0))
    act = pl.BlockSpec((2, tm, 2 * FB), lambda i: (0, i, 0))
    return pl.pallas_call(
        body, name="ffn_bwd", grid=(T // tm,),
        in_specs=[row, row, pl.BlockSpec((1, D), lambda i: (0, 0)), act, ANY, ANY, ANY],
        out_specs=[row, pl.BlockSpec((1, D), lambda i: (0, 0)), row, act,
                   pl.BlockSpec((tm, 2 * FB), lambda i: (i, 0)), row],
        out_shape=[jax.ShapeDtypeStruct((T, D), F32), jax.ShapeDtypeStruct((1, D), F32),
                   jax.ShapeDtypeStruct((T, D), BF16), jax.ShapeDtypeStruct((2, T, 2 * FB), BF16),
                   jax.ShapeDtypeStruct((T, 2 * FB), BF16), jax.ShapeDtypeStruct((T, D), BF16)],
        scratch_shapes=[pltpu.VMEM(win.shape, BF16), pltpu.VMEM(wout.shape, BF16), pltpu.SemaphoreType.DMA((6,))],
        compiler_params=_cp(1),
    )(dxo, x, g, gu, win, wout, dep)


def _mix_rms_bwd(dxo, x, g, dzs, wt):
    T, D = x.shape
    tm = min(TM, T)
    npair = len(dzs)

    def body(*refs):
        dxo_ref, x_ref, g_ref = refs[:3]
        dz_refs, w_ref = refs[3:3 + npair], refs[3 + npair]
        dxi_ref, dg_ref, hb_ref = refs[4 + npair:]

        @pl.when(pl.program_id(0) == 0)
        def _():
            dg_ref[...] = jnp.zeros_like(dg_ref)

        dh = jnp.zeros((tm, D), F32)
        k0 = 0
        for dz_ref in dz_refs:
            kp = dz_ref.shape[1]
            dh = dh + _dot(dz_ref[...], w_ref[k0:k0 + kp, :])
            k0 += kp
        dx, dg, h = _rms_bwd_rows(dh, x_ref[...], g_ref[...])
        dxi_ref[...] = dxo_ref[...] + dx
        dg_ref[...] += dg
        hb_ref[...] = h.astype(BF16)

    row = pl.BlockSpec((tm, D), lambda i: (i, 0))
    return pl.pallas_call(
        body, name="mix_rms_bwd", grid=(T // tm,),
        in_specs=[row, row, pl.BlockSpec((1, D), lambda i: (0, 0))]
                 + [pl.BlockSpec((tm, dz.shape[1]), lambda i: (i, 0)) for dz in dzs]
                 + [pl.BlockSpec(wt.shape, lambda i: (0, 0))],
        out_specs=[row, pl.BlockSpec((1, D), lambda i: (0, 0)), row],
        out_shape=[jax.ShapeDtypeStruct((T, D), F32), jax.ShapeDtypeStruct((1, D), F32),
                   jax.ShapeDtypeStruct((T, D), BF16)],
        compiler_params=_cp(1),
    )(dxo, x, g, *dzs, wt)


def _wgrad(name, a, b, a_spec, b_spec, out_shape, out_spec, nblk, dep, acc_shape):
    T = a.shape[0]
    tk = min(TK_WGRAD, T)
    nk = T // tk

    def body(a_ref, b_ref, dep_ref, o_ref, acc_ref):
        k = pl.program_id(1)

        @pl.when(k == 0)
        def _():
            acc_ref[...] = jnp.zeros_like(acc_ref)

        acc_ref[...] += _dot_tn(a_ref[...], b_ref[...])

        @pl.when(k == nk - 1)
        def _():
            o_ref[...] = acc_ref[...].reshape(o_ref.shape).astype(BF16)

    return pl.pallas_call(
        body, name=name, grid=(nblk, nk), in_specs=[a_spec, b_spec, ANY], out_specs=out_spec,
        out_shape=jax.ShapeDtypeStruct(out_shape, BF16), scratch_shapes=[pltpu.VMEM(acc_shape, F32)],
        compiler_params=_cp(2),
    )(a, b, dep)


def _wgrad_ffn_in(hb, dgu, dep):
    T, D = hb.shape
    FB = dgu.shape[2] // 2
    tk = min(TK_WGRAD, T)
    return _wgrad("wgrad_ffn_in", hb, dgu,
                  pl.BlockSpec((tk, D), lambda b, k: (k, 0)),
                  pl.BlockSpec((None, tk, FB), lambda b, k: (b // 2, k, b % 2)),
                  (4, D, FB), pl.BlockSpec((None, D, FB), lambda b, k: (b, 0, 0)), 4, dep, (D, FB))


def _wgrad_ffn_out(a, dyb, dep):
    T, D = dyb.shape
    FB = a.shape[1] // 2
    tk = min(TK_WGRAD, T)
    return _wgrad("wgrad_ffn_out", a, dyb,
                  pl.BlockSpec((tk, FB), lambda b, k: (k, b)),
                  pl.BlockSpec((tk, D), lambda b, k: (k, 0)),
                  (4, FB // 2, D), pl.BlockSpec((2, FB // 2, D), lambda b, k: (b, 0, 0)), 2, dep, (FB, D))


def _wgrad_cat(a_list, b_list):
    T = a_list[0].shape[0]
    tk = min(TK_WGRAD, T)
    nk = T // tk
    na = len(a_list)
    M, N = sum(a.shape[1] for a in a_list), sum(b.shape[1] for b in b_list)

    def body(*refs):
        a_refs, b_refs, o_ref, acc_ref = refs[:na], refs[na:-2], refs[-2], refs[-1]
        k = pl.program_id(0)

        @pl.when(k == 0)
        def _():
            acc_ref[...] = jnp.zeros_like(acc_ref)

        r0 = 0
        for a_ref in a_refs:
            c0 = 0
            for b_ref in b_refs:
                m, n = a_ref.shape[1], b_ref.shape[1]
                acc_ref[r0:r0 + m, c0:c0 + n] += _dot_tn(a_ref[...], b_ref[...])
                c0 += n
            r0 += a_ref.shape[1]

        @pl.when(k == nk - 1)
        def _():
            o_ref[...] = acc_ref[...].astype(BF16)

    return pl.pallas_call(
        body, name="wgrad_cat", grid=(nk,),
        in_specs=[pl.BlockSpec((tk, v.shape[1]), lambda k: (k, 0)) for v in list(a_list) + list(b_list)],
        out_specs=pl.BlockSpec((M, N), lambda k: (0, 0)),
        out_shape=jax.ShapeDtypeStruct((M, N), BF16), scratch_shapes=[pltpu.VMEM((M, N), F32)],
        compiler_params=_cp(1),
    )(*a_list, *b_list)


def _mixout_bwd(dxo, wo):
    T, D = dxo.shape
    tm = min(TM_MIX, T)
    A = ATTN_W
    C = wo.shape[0] - A

    def body(dxo_ref, w_ref, dyb_ref, da_ref, dc_ref):
        dyb = dxo_ref[...].astype(BF16)
        dyb_ref[...] = dyb
        da_ref[...] = _dot_nt(dyb, w_ref[:A, :]).astype(BF16)
        dc_ref[...] = _dot_nt(dyb, w_ref[A:, :])

    return pl.pallas_call(
        body, name="mixout_bwd", grid=(T // tm,),
        in_specs=[pl.BlockSpec((tm, D), lambda i: (i, 0)), pl.BlockSpec(wo.shape, lambda i: (0, 0))],
        out_specs=[pl.BlockSpec((tm, D), lambda i: (i, 0)), pl.BlockSpec((tm, A), lambda i: (i, 0)),
                   pl.BlockSpec((tm, C), lambda i: (i, 0))],
        out_shape=[jax.ShapeDtypeStruct((T, D), BF16), jax.ShapeDtypeStruct((T, A), BF16),
                   jax.ShapeDtypeStruct((T, C), F32)],
        compiler_params=_cp(1),
    )(dxo, wo)


def _conv_bwd(dconv, ypre, u, w, lg, lb):
    T, CH = dconv.shape
    tm = min(TM, T)
    n = tm + HALO
    hb = tm // HALO
    nt = T // tm
    nchunk = tm // CONV_ROWS

    def body(dc_ref, dcn_ref, yp_ref, ypn_ref, uc_ref, up_ref, w_ref, lg_ref, lb_ref,
             du_ref, dw_ref, dvec_ref, zs_ref, zsh_ref, dy_ref, dysh_ref, dz_ref, dwacc_ref):
        i = pl.program_id(0)

        @pl.when(i == 0)
        def _():
            dwacc_ref[...] = jnp.zeros_like(dwacc_ref)
            dvec_ref[...] = jnp.zeros_like(dvec_ref)

        g, bb = lg_ref[...], lb_ref[...]

        def ln_bwd(dc, yp):
            mu = jnp.mean(yp, axis=-1, keepdims=True)
            d = yp - mu
            rs = lax.rsqrt(jnp.mean(d * d, axis=-1, keepdims=True) + EPS)
            yn = d * rs
            o = yn * g + bb
            sg = _fast_sigmoid(o)
            do = dc * (sg * (1.0 + o * (1.0 - sg)))
            dyn = do * g
            dyp = rs * (dyn - jnp.mean(dyn, axis=-1, keepdims=True)
                        - yn * jnp.mean(dyn * yn, axis=-1, keepdims=True))
            return dyp, do, yn

        dyp, do, yn = ln_bwd(dc_ref[...], yp_ref[...])
        dvec_ref[0:1, :] += jnp.sum(dyp, axis=0, keepdims=True)
        dvec_ref[1:2, :] += jnp.sum(do * yn, axis=0, keepdims=True)
        dvec_ref[2:3, :] += jnp.sum(do, axis=0, keepdims=True)
        dy_ref[0:tm] = dyp
        dyh, _, _ = ln_bwd(dcn_ref[...], ypn_ref[...])
        dy_ref[tm:] = jnp.where(i < nt - 1, dyh, 0.0)
        _shift_copies(dy_ref, dysh_ref, n - 8)
        _fill_z(zs_ref, zsh_ref, uc_ref, up_ref, i, CH, n)

        def chunk(ci, carry):
            c0 = pl.multiple_of(ci * CONV_ROWS, CONV_ROWS)
            acc = jnp.zeros((CONV_ROWS, CH), F32)
            for k in range(CONV_W):
                acc = acc + w_ref[k:k + 1, :] * _tap(dy_ref, dysh_ref, CONV_W - 1 - k, c0)
            dz_ref[pl.ds(c0, CONV_ROWS), :] = acc
            dyc = dy_ref[pl.ds(c0, CONV_ROWS), :]
            for k in range(CONV_W):
                prod = dyc * _tap(zs_ref, zsh_ref, HALO - (CONV_W - 1) + k, c0)
                dwacc_ref[k] += jnp.sum(prod.reshape(CONV_ROWS // 8, 8, CH), axis=0)
            return carry

        lax.fori_loop(0, nchunk, chunk, 0)

        @pl.when(i == nt - 1)
        def _():
            dw_ref[...] = jnp.sum(dwacc_ref[...], axis=1)

        uc = uc_ref[...]
        a = uc[:, :CH]
        sg = _fast_sigmoid(uc[:, CH:])
        dz = dz_ref[...]
        du_ref[:, :CH] = (dz * sg).astype(BF16)
        du_ref[:, CH:] = (dz * a * sg * (1.0 - sg)).astype(BF16)

    cur = lambda c: pl.BlockSpec((tm, c), lambda i: (i, 0))
    nxt = lambda c: pl.BlockSpec((HALO, c), lambda i: (jnp.minimum((i + 1) * hb, T // HALO - 1), 0))
    vec = pl.BlockSpec((1, CH), lambda i: (0, 0))
    return pl.pallas_call(
        body, name="conv_bwd", grid=(nt,),
        in_specs=[cur(CH), nxt(CH), cur(CH), nxt(CH), cur(2 * CH),
                  pl.BlockSpec((HALO, 2 * CH), lambda i: (jnp.maximum(i * hb - 1, 0), 0)),
                  pl.BlockSpec((CONV_W, CH), lambda i: (0, 0)), vec, vec],
        out_specs=[pl.BlockSpec((tm, 2 * CH), lambda i: (i, 0)), pl.BlockSpec((32, CH), lambda i: (0, 0)),
                   pl.BlockSpec((8, CH), lambda i: (0, 0))],
        out_shape=[jax.ShapeDtypeStruct((T, 2 * CH), BF16), jax.ShapeDtypeStruct((32, CH), F32),
                   jax.ShapeDtypeStruct((8, CH), F32)],
        scratch_shapes=[pltpu.VMEM((n, CH), F32), pltpu.VMEM((7, n - 8, CH), F32),
                        pltpu.VMEM((n, CH), F32), pltpu.VMEM((7, n - 8, CH), F32), pltpu.VMEM((tm, CH), F32),
                        pltpu.VMEM((32, 8, CH), F32)],
        compiler_params=_cp(1),
    )(dconv, dconv, ypre, ypre, u, u, w, lg, lb)


def _attn_bwd(sinks, tab, qkv, dattn):
    T = qkv.shape[0]
    nb = T // WINDOW

    def body(sink_ref, tab_ref, q_ref, kvp_ref, kvc_ref, do_ref, dq_ref, dkv_ref, dsk_ref, carry_ref):
        n = pl.program_id(0)

        @pl.when(n == 0)
        def _():
            dsk_ref[...] = jnp.zeros_like(dsk_ref)
            carry_ref[...] = jnp.zeros_like(carry_ref)

        @pl.when(n < nb)
        def _():
            seen = _first_block_mask(n)
            for g in range(N_KV):
                qs = _stack_heads(q_ref, g)
                dos = _stack_heads(do_ref, g)
                k = _band(kvp_ref, kvc_ref, g * HEAD_DIM)
                v = _band(kvp_ref, kvc_ref, KV_W + g * HEAD_DIM)
                p, ps = _attn_probs(qs, k, tab_ref[g], seen, _sink_col(sink_ref, g))
                dp = _dot_nt(dos, v)
                delta = jnp.sum(p * dp, axis=-1, keepdims=True)
                dsb = (p * (dp - delta)).astype(BF16)
                dsink = -ps * delta
                dqs = _dot(dsb, k) * SCALE
                dk = _dot_tn(dsb, qs) * SCALE
                dv = _dot_tn(p.astype(BF16), dos)
                for i in range(GROUP):
                    h = GROUP * g + i
                    dq_ref[:, h * HEAD_DIM:(h + 1) * HEAD_DIM] = dqs[i * WINDOW:(i + 1) * WINDOW].astype(BF16)
                    dsk_ref[h:h + 1, :] += jnp.sum(dsink[i * WINDOW:(i + 1) * WINDOW], axis=0, keepdims=True)
                for off, d in ((g * HEAD_DIM, dk), (KV_W + g * HEAD_DIM, dv)):
                    dkv_ref[:, off:off + HEAD_DIM] = (carry_ref[:, off:off + HEAD_DIM] + d[:WINDOW]).astype(BF16)
                    carry_ref[:, off:off + HEAD_DIM] = d[WINDOW:]

        @pl.when(n == nb)
        def _():
            dkv_ref[...] = carry_ref[...].astype(BF16)

    last = nb - 1
    return pl.pallas_call(
        body, name="attn_bwd", grid=(nb + 1,),
        in_specs=[pl.BlockSpec(memory_space=pltpu.SMEM),
                  pl.BlockSpec(tab.shape, lambda n: (0, 0, 0)),
                  pl.BlockSpec((WINDOW, ATTN_W), lambda n: (jnp.minimum(n, last), 0)),
                  pl.BlockSpec((WINDOW, 2 * KV_W), lambda n: (jnp.clip(n - 1, 0, last), 2)),
                  pl.BlockSpec((WINDOW, 2 * KV_W), lambda n: (jnp.minimum(n, last), 2)),
                  pl.BlockSpec((WINDOW, ATTN_W), lambda n: (jnp.minimum(n, last), 0))],
        out_specs=[pl.BlockSpec((WINDOW, ATTN_W), lambda n: (jnp.minimum(n, last), 0)),
                   pl.BlockSpec((WINDOW, 2 * KV_W), lambda n: (jnp.maximum(n - 1, 0), 0)),
                   pl.BlockSpec((8, LANES), lambda n: (0, 0))],
        out_shape=[jax.ShapeDtypeStruct((T, ATTN_W), BF16), jax.ShapeDtypeStruct((T, 2 * KV_W), BF16),
                   jax.ShapeDtypeStruct((8, LANES), F32)],
        scratch_shapes=[pltpu.VMEM((WINDOW, 2 * KV_W), F32)],
        compiler_params=_cp(1),
    )(sinks, tab, qkv, qkv, qkv, dattn)


def _pack(arrs):
    flat = jnp.concatenate([a.reshape(-1) for a in arrs])
    pad = -flat.shape[0] % (8 * LANES)
    return jnp.pad(flat, (0, pad)).reshape(1, -1, LANES)


def _unpack(packed, like):
    flat = packed.reshape(-1)
    out, off = [], 0
    for a in like:
        out.append(flat[off:off + a.size].reshape(a.shape))
        off += a.size
    return out


def kernel(x, norm_ffn1, w_ffn1_in, w_ffn1_out, norm_mix, w_in, sinks, w_dw, b_dw, conv_ln_g, conv_ln_b, w_out, norm_ffn2, w_ffn2_in, w_ffn2_out, final_norm, loss_target, m_norm_ffn1, m_w_ffn1_in, m_w_ffn1_out, m_norm_mix, m_w_in, m_sinks, m_w_dw, m_b_dw, m_conv_ln_g, m_conv_ln_b, m_w_out, m_norm_ffn2, m_w_ffn2_in, m_w_ffn2_out, m_final_norm, v_norm_ffn1, v_w_ffn1_in, v_w_ffn1_out, v_norm_mix, v_w_in, v_sinks, v_w_dw, v_b_dw, v_conv_ln_g, v_conv_ln_b, v_w_out, v_norm_ffn2, v_w_ffn2_in, v_w_ffn2_out, v_final_norm):
    L, D = norm_ffn1.shape
    T = x.shape[1]
    FB = w_ffn1_in.shape[2]
    CH = b_dw.shape[1]
    QKV = ATTN_W + 2 * KV_W
    xs = x.reshape(T, D)
    tgt = loss_target.reshape(T, D)
    cx, cy, cc = lax.axis_index("x"), lax.axis_index("y"), lax.axis_index("c")
    chip = 2 * cx + cy
    cidx = cc.reshape(1).astype(jnp.int32)
    tr = lambda a_: jnp.transpose(a_, (0, 2, 1))
    big_w = (w_ffn1_in, w_ffn1_out, tr(w_in), w_out, w_ffn2_in, w_ffn2_out)
    big_m = (m_w_ffn1_in, m_w_ffn1_out, tr(m_w_in), m_w_out, m_w_ffn2_in, m_w_ffn2_out)
    big_v = (v_w_ffn1_in, v_w_ffn1_out, tr(v_w_in), v_w_out, v_w_ffn2_in, v_w_ffn2_out)
    NW = len(big_w) + 1

    def own_slot(a, slots=4, idx=chip):
        return lax.dynamic_update_index_in_dim(lax.empty((slots,) + a.shape, a.dtype), a, idx, 0)

    def shards(l, tok):
        return [own_slot((w_[l] + tok[0, 0]).astype(BF16)) for w_ in big_w] + [own_slot(w_dw[l] + tok[0, 0])]

    def gather_start(lands, tok):
        return _xchg_start("gather_start", [], lands, _gather_plan, tok)

    def gather_arrived(started, after, n, taps):
        _, lands, tok = _xchg_wait("gather_wait", started, 0, n, _gather_plan, after)
        return _xchg_start("gshare_start", [], lands[:-1] if taps else lands, _gshare_plan, tok, "sibling3"), lands[-1]

    def shared_weights(shared, after, n):
        _, mats, tok = _xchg_wait("gshare_wait", shared, 0, n, _gshare_plan, after, "sibling3")
        return mats, tok

    row = lambda a, l: a[l].reshape(1, -1)
    tab = _attn_bias_table()
    NB = len(big_w)

    saved, W = [], []
    zero_tok = jnp.zeros((8, LANES), F32)
    src0 = shards(0, zero_tok)
    started = gather_start(src0[:2], zero_tok)
    rest0 = gather_start(src0[2:], started[-1])
    cast = [None] + [shards(l, rest0[-1]) for l in range(1, L)]
    shared, _ = gather_arrived(started, [xs] + [a_ for c_ in cast[1:] for a_ in c_], 2, False)
    after = [shared[-1]]
    for l in range(L):
        mats, tok = shared_weights(shared, after, 2 if l == 0 else NB)
        started = None
        if l + 1 < L:
            started = gather_start(cast[l + 1], tok)
            tok = started[-1]
        x0 = xs
        x1, gu1 = _ffn_fwd(x0, row(norm_ffn1, l) + tok[0, 0], mats[0], mats[1].reshape(2 * FB, D))
        gm_row = row(norm_mix, l)
        if l == 0:
            shared, gdw = gather_arrived(rest0, [x1], NW - 2, True)
            rest, tok = shared_weights(shared, [shared[-1]], NB - 2)
            mats = list(mats) + list(rest)
            gm_row = gm_row + tok[0, 0]
        g1i, g1o, gi, go, g2i, g2o = mats
        w = dict(f1i=g1i, f1o=g1o.reshape(2 * FB, D), f2i=g2i, f2o=g2o.reshape(2 * FB, D),
                 wit=gi.reshape(-1, D), wo=go.reshape(-1, D),
                 wdw=jnp.transpose(gdw, (1, 0, 2)).reshape(CONV_W, CH))
        W.append(w)
        qkv, u = _mixproj_fwd(x1, gm_row, w["wit"])
        attn = _attn_fwd(row(sinks, l), tab, qkv)
        conv, ypre = _conv_fwd(u, w["wdw"], row(b_dw, l), row(conv_ln_g, l), row(conv_ln_b, l))
        x2 = _mixout_fwd(x1, attn, conv, w["wo"])
        g2_row = row(norm_ffn2, l)
        if started is not None and l > 0:
            shared, gdw = gather_arrived(started, [x2], NW, True)
            g2_row = g2_row + shared[-1][0, 0]
        xs, gu2 = _ffn_fwd(x2, g2_row, w["f2i"], w["f2o"])
        if started is not None and l == 0:
            shared, gdw = gather_arrived(started, [xs], NW, True)
        saved.append((x0, gu1, x1, qkv, u, attn, conv, ypre, x2, gu2))
        after = [xs]

    loss_part, dx, d_final = _loss_head(xs, final_norm.reshape(1, D), tgt)
    loss = lax.psum(loss_part[0, 0], ("x", "y", "c"))

    bufs = [[lax.empty(w_.shape, F32) for _ in range(4)] for w_ in big_w]
    d_n1, d_nm, d_n2 = [None] * L, [None] * L, [None] * L
    d_sk, d_bdw, d_lg, d_lb, d_wdw = [None] * L, [None] * L, [None] * L, [None] * L, [None] * L

    me_idx = 4 * cx + 2 * cy + cc

    def reduce_start(gs):
        lands = []
        for g in gs:
            h = g.shape[1] // 2
            mine = lax.dynamic_slice(g, (chip, cc * h, 0), (1, h, g.shape[2]))[0]
            lands.append(own_slot(mine, 8, me_idx))
        return _xchg_start("rs_start", gs, lands, _rs_plan, zero_tok, "all")

    sent = []

    def finish(l, rs_started, after, idxs):
        _, qs, tok = _xchg_wait("rs_wait", rs_started, len(idxs), len(idxs), _rs_plan, after, "all")
        for k, t in enumerate(idxs):
            bufs[t] = _adamw_layer(cidx, qs[k], big_w[t], big_m[t], big_v[t], bufs[t], l)
        flat = [b_ for t in idxs for b_ in bufs[t]]
        st = _xchg_start("oshare_start", [], flat, _oshare_plan(l), tok, "sibling")
        for k, t in enumerate(idxs):
            bufs[t] = list(st[2 + 4 * k:6 + 4 * k])
        sent.append((st, idxs, _oshare_plan(l)))

    ALL = list(range(NB))
    EARLY, LATE = ALL[2:], ALL[:2]
    rs_list = []
    tok = zero_tok
    for l in reversed(range(L)):
        w = W[l]
        x0, gu1, x1, qkv, u, attn, conv, ypre, x2, gu2 = saved[l]
        dx, d_n2[l], hb, dgu, a, dyb = _ffn_bwd(dx, x2, row(norm_ffn2, l), gu2, w["f2i"], w["f2o"], tok)
        g_f2i, g_f2o = _wgrad_ffn_in(hb, dgu, tok), _wgrad_ffn_out(a, dyb, tok)
        lg_row = row(conv_ln_g, l)
        dyb, dattn, dconv = _mixout_bwd(dx, w["wo"])
        g_wo = _wgrad_cat([attn, conv], [dyb]).reshape(4, -1, D)
        du, dwdw, dvec = _conv_bwd(dconv, ypre, u, w["wdw"], lg_row, row(conv_ln_b, l))
        d_wdw[l], d_bdw[l], d_lg[l], d_lb[l] = dwdw[:CONV_W], dvec[0], dvec[1], dvec[2]
        dq, dkv, dsk = _attn_bwd(row(sinks, l), tab, qkv, dattn)
        d_sk[l] = dsk[:, 0]
        dx, d_nm[l], hb = _mix_rms_bwd(dx, x1, row(norm_mix, l), [dq, dkv, du], w["wit"])
        g_wi = _wgrad_cat([dq, dkv, du], [hb]).reshape(4, -1, D)
        if l == 0:
            rs_early = reduce_start([g_wi, g_wo, g_f2i, g_f2o])
            tok = rs_early[-1]
        dx, d_n1[l], hb, dgu, a, dyb = _ffn_bwd(dx, x0, row(norm_ffn1, l), gu1, w["f1i"], w["f1o"], tok)
        g_f1i, g_f1o = _wgrad_ffn_in(hb, dgu, tok), _wgrad_ffn_out(a, dyb, tok)
        rs_started = reduce_start([g_f1i, g_f1o] if l == 0 else [g_f1i, g_f1o, g_wi, g_wo, g_f2i, g_f2o])
        tok = rs_started[-1]
        rs_list.append((l, rs_started))
    grad_x = dx.reshape(x.shape)

    small_g = [jnp.concatenate(d, axis=0) for d in (d_n1, d_nm, d_n2)] + [d_final, jnp.stack(d_sk)] + \
              [jnp.stack(d) for d in (d_bdw, d_lg, d_lb, d_wdw)]
    packed = _pack(small_g)[0]
    small_started = _xchg_start("small_start", [packed], [own_slot(packed, 8, 4 * cx + 2 * cy + cc)], _slot_plan, tok, "all")

    rs_late = rs_list.pop()[1]
    after = [small_started[-1]]
    for l, st in rs_list:
        finish(l, st, after, ALL)
        after = [b_[0] for b_ in bufs]
    _, (slots,), _ = _xchg_wait("small_wait", small_started, 1, 1, _slot_plan, after, "all")
    small_sum = _unpack(_sum_slots(slots), small_g)
    g_wdw = lax.dynamic_slice_in_dim(small_sum[8], chip * w_dw.shape[2], w_dw.shape[2], axis=2)
    small_g = [small_sum[0], small_sum[1], small_sum[2], small_sum[3].reshape(D), small_sum[4],
               small_sum[5], small_sum[6], small_sum[7], g_wdw]
    small_w = (norm_ffn1, norm_mix, norm_ffn2, final_norm, sinks, b_dw, conv_ln_g, conv_ln_b, w_dw)
    small_m = (m_norm_ffn1, m_norm_mix, m_norm_ffn2, m_final_norm, m_sinks, m_b_dw, m_conv_ln_g, m_conv_ln_b, m_w_dw)
    small_v = (v_norm_ffn1, v_norm_mix, v_norm_ffn2, v_final_norm, v_sinks, v_b_dw, v_conv_ln_g, v_conv_ln_b, v_w_dw)
    upd = _adamw(_pack(small_g), _pack(small_w), _pack(small_m), _pack(small_v))
    small_upd = [_unpack(u_, small_w) for u_ in upd]
    finish(0, rs_early, [upd[0]], EARLY)
    finish(0, rs_late, [b_[0] for b_ in bufs], LATE)
    after = []
    for st, idxs, plan in sent:
        flat = [b_ for t in idxs for b_ in bufs[t]]
        _, flat, tok = _xchg_wait("oshare_wait", st, 0, len(flat), plan, after, "sibling", latest=flat)
        for k, t in enumerate(idxs):
            bufs[t] = list(flat[4 * k:4 * k + 4])
        after = [tok]

    order = ("norm_ffn1", "w_ffn1_in", "w_ffn1_out", "norm_mix", "w_in", "sinks", "w_dw", "b_dw", "conv_ln_g",
             "conv_ln_b", "w_out", "norm_ffn2", "w_ffn2_in", "w_ffn2_out", "final_norm")
    small_names = ("norm_ffn1", "norm_mix", "norm_ffn2", "final_norm", "sinks", "b_dw", "conv_ln_g", "conv_ln_b", "w_dw")
    big_names = ("w_ffn1_in", "w_ffn1_out", "w_in", "w_out", "w_ffn2_in", "w_ffn2_out")
    grads, deltas, new_m, new_v = {}, {}, {}, {}
    for i, nme in enumerate(small_names):
        grads[nme], deltas[nme], new_m[nme], new_v[nme] = small_g[i], small_upd[0][i], small_upd[1][i], small_upd[2][i]
    for i, nme in enumerate(big_names):
        grads[nme], deltas[nme], new_m[nme], new_v[nme] = [tr(b_) for b_ in bufs[i]] if nme == "w_in" else bufs[i]
    return (loss, grad_x, *[grads[n] for n in order], *[deltas[n] for n in order],
            *[new_m[n] for n in order], *[new_v[n] for n in order])
```

```python
import jax
import jax.numpy as jnp
from jax import lax
from jax.experimental import pallas as pl
from jax.experimental.pallas import tpu as pltpu

F32, BF16 = jnp.float32, jnp.bfloat16
EPS = 1e-6
NEG_INF = -1e30
HEAD_DIM = 64
N_HEADS = 8
N_KV = 2
GROUP = N_HEADS // N_KV
WINDOW = 128
ATTN_W = N_HEADS * HEAD_DIM
KV_W = N_KV * HEAD_DIM
CONV_W = 31
HALO = 32
CONV_ROWS = 32
SCALE = 1.0 / 8.0
ADAM_LR, ADAM_B1, ADAM_B2, ADAM_EPS, ADAM_WD, ADAM_STEP = 0.001, 0.9, 0.999, 1e-08, 0.01, 10
TM = 512
TM_FFN_BWD = 256
TK_WGRAD = 2048
TM_MIX = 1024
LANES = 128
VMEM_LIMIT = 52 * 1024 * 1024
MESH = pl.DeviceIdType.MESH
ANY = pl.BlockSpec(memory_space=pl.ANY)
HBM = pl.BlockSpec(memory_space=pltpu.HBM)
SEM = pl.BlockSpec(memory_space=pltpu.SEMAPHORE)
VMEM = pl.BlockSpec(memory_space=pltpu.VMEM)
EFFECT = pltpu.SideEffectType.DATAFLOW_SIDE_EFFECTING
TOKEN = jax.ShapeDtypeStruct((8, LANES), F32)


def _cp(n):
    return pltpu.CompilerParams(dimension_semantics=("arbitrary",) * n, vmem_limit_bytes=VMEM_LIMIT)


def _dot(a, b):
    return jnp.dot(a, b, preferred_element_type=F32)


def _dot_nt(a, b):
    return lax.dot_general(a, b, (((1,), (1,)), ((), ())), preferred_element_type=F32)


def _dot_tn(a, b):
    return lax.dot_general(a, b, (((0,), (0,)), ((), ())), preferred_element_type=F32)


def _place():
    x, y, c = lax.axis_index("x"), lax.axis_index("y"), lax.axis_index("c")
    chips = [(1 - x, y), (x, 1 - y), (1 - x, 1 - y)]
    return x, y, c, chips


def _rcopy(src, dst, send_sems, recv_sems, k, dev):
    return pltpu.make_async_remote_copy(src_ref=src, dst_ref=dst, send_sem=send_sems.at[k],
                                        recv_sem=recv_sems.at[k], device_id=dev, device_id_type=MESH)


def _hbm(a):
    return pltpu.with_memory_space_constraint(a, pltpu.HBM)


PEERS = {"chips": 3, "sibling": 1, "sibling3": 3, "all": 7}


def _targets(mode):
    x, y, c, chips = _place()
    b = 2 * x + y
    if mode == "chips":
        return b, c, [((px, py, c), 2 * px + py) for px, py in chips]
    if mode == "sibling":
        return b, c, [((x, y, 1 - c), b)]
    if mode == "sibling3":
        return b, c, [((x, y, 1 - c), 2 * px + py) for px, py in chips]
    flip = lambda v, f: 1 - v if f else v
    devs = [(flip(x, k >> 2 & 1), flip(y, k >> 1 & 1), flip(c, k & 1)) for k in range(1, 8)]
    return 4 * x + 2 * y + c, c, [(d, 4 * d[0] + 2 * d[1] + d[2]) for d in devs]


def _xchg_start(name, srcs, lands, plan, dep, mode="chips"):
    ns, nl, npeer = len(srcs), len(lands), PEERS[mode]

    def body(*refs):
        land = refs[ns:ns + nl]
        src = refs[:ns] if ns else land
        send_sems, recv_sems, token = refs[ns + nl + 1], refs[ns + nl + 2], refs[-1]
        me, c, peers = _targets(mode)
        for t in range(nl):
            for j, (dev, tag) in enumerate(peers):
                s, d, _ = plan(src[t], land[t], t, me, c, tag)
                _rcopy(s, d, send_sems, recv_sems, npeer * t + j, dev).start()
        token[...] = jnp.zeros_like(token)

    arrs = list(srcs) + list(lands)
    return pl.pallas_call(
        body, name=name,
        out_shape=(pltpu.SemaphoreType.DMA((npeer * nl,)), pltpu.SemaphoreType.DMA((npeer * nl,)),
                   *[pltpu.HBM(a.shape, a.dtype) for a in arrs], TOKEN),
        in_specs=[HBM] * (ns + nl) + [ANY], out_specs=(SEM, SEM, *[HBM] * (ns + nl), VMEM),
        input_output_aliases={i: 2 + i for i in range(ns + nl)},
        compiler_params=pltpu.CompilerParams(has_side_effects=EFFECT),
    )(*[_hbm(a) for a in arrs], dep)


def _xchg_wait(name, started, ns, nl, plan, after, mode="chips", latest=None):
    send_sems, recv_sems = started[0], started[1]
    thru = started[2:2 + ns + nl] if latest is None else latest
    npeer = PEERS[mode]

    def body(*refs):
        land = refs[ns:ns + nl]
        src = refs[:ns] if ns else land
        send_sems, recv_sems, token = refs[ns + nl], refs[ns + nl + 1], refs[-1]
        me, c, peers = _targets(mode)
        for t in range(nl):
            for j, (dev, tag) in enumerate(peers):
                s, _, a = plan(src[t], land[t], t, me, c, tag)
                cp = _rcopy(s, a, send_sems, recv_sems, npeer * t + j, dev)
                cp.wait_send()
                cp.wait_recv()
        token[...] = jnp.zeros_like(token)

    out = pl.pallas_call(
        body, name=name,
        out_shape=(*[pltpu.HBM(a.shape, a.dtype) for a in thru], TOKEN),
        in_specs=[HBM] * (ns + nl) + [SEM, SEM] + [ANY] * len(after), out_specs=(*[HBM] * (ns + nl), VMEM),
        input_output_aliases={i: i for i in range(ns + nl)},
        compiler_params=pltpu.CompilerParams(has_side_effects=EFFECT),
    )(*thru, send_sems, recv_sems, *after)
    return out[:ns], out[ns:ns + nl], out[-1]


def _half(ref_rows, which):
    h = ref_rows // 2
    return pl.ds(which * h, h)


def _gather_plan(src, land, t, b, c, pb):
    if land.shape[1] % 2 == 0:
        hs = _half(land.shape[1], c)
        return land.at[b, hs], land.at[b, hs], land.at[pb, hs]
    return land.at[b], land.at[b], land.at[pb]


def _gshare_plan(src, land, t, b, c, pb):
    return land.at[pb, _half(land.shape[1], c)], land.at[pb, _half(land.shape[1], c)], land.at[pb, _half(land.shape[1], 1 - c)]


def _rs_plan(src, land, t, me, c, tag):
    h = src.shape[1] // 2
    return src.at[tag // 2, pl.ds((tag % 2) * h, h), :], land.at[me], land.at[tag]


def _rows_block(h, cap=512):
    for rb in range(min(h, cap) // 16 * 16, 0, -16):
        if h % rb == 0:
            return rb
    return h


def _oshare_plan(l):
    def plan(src, land, t, me, c, tag):
        return land.at[l, _half(land.shape[1], c)], land.at[l, _half(land.shape[1], c)], land.at[l, _half(land.shape[1], 1 - c)]
    return plan


def _slot_plan(src, land, t, me, c, tag):
    return src, land.at[me], land.at[tag]


def _hand_on(arrs):
    n = len(arrs)

    def body(*refs):
        pass

    return pl.pallas_call(
        body, name="hand_on", in_specs=[ANY] * n, out_specs=[ANY] * n,
        out_shape=[jax.ShapeDtypeStruct(a.shape, a.dtype) for a in arrs],
        input_output_aliases={i: i for i in range(n)},
    )(*arrs)


def _adam_update(gg, w, m, v):
    m2 = ADAM_B1 * m + (1.0 - ADAM_B1) * gg
    v2 = ADAM_B2 * v + (1.0 - ADAM_B2) * (gg * gg)
    mh = m2 / (1.0 - ADAM_B1 ** ADAM_STEP)
    vh = v2 / (1.0 - ADAM_B2 ** ADAM_STEP)
    return -ADAM_LR * (mh / (jnp.sqrt(vh) + ADAM_EPS) + ADAM_WD * w), m2, v2


def _adamw_layer(cidx, q, w, m, v, bufs, l):
    L, R, C = w.shape
    h = R // 2
    rb = _rows_block(h, 256)
    nr = h // rb

    def body(c_ref, q_ref, w_ref, m_ref, v_ref, *rest):
        g_ref, d_ref, mo_ref, vo_ref = rest[-4:]
        gg = q_ref[0].astype(F32)
        for s in range(1, 8):
            gg = gg + q_ref[s].astype(F32)
        g_ref[...] = gg
        d_ref[...], mo_ref[...], vo_ref[...] = _adam_update(gg, w_ref[...], m_ref[...], v_ref[...])

    wspec = pl.BlockSpec((None, rb, C), lambda i, c: (l, c[0] * nr + i, 0))
    return pl.pallas_call(
        body, name="adamw_layer", out_shape=[jax.ShapeDtypeStruct(w.shape, F32)] * 4,
        grid_spec=pltpu.PrefetchScalarGridSpec(
            num_scalar_prefetch=1, grid=(nr,),
            in_specs=[pl.BlockSpec((8, rb, C), lambda i, c: (0, i, 0)), wspec, wspec, wspec] + [ANY] * 4,
            out_specs=[wspec] * 4),
        input_output_aliases={5 + k: k for k in range(4)},
        compiler_params=_cp(1),
    )(cidx, q, w, m, v, *bufs)


def _adamw(g, w, m, v):
    L, R, C = g.shape
    rb = _rows_block(R)

    def body(g_ref, w_ref, m_ref, v_ref, d_ref, mo_ref, vo_ref):
        d_ref[...], mo_ref[...], vo_ref[...] = _adam_update(g_ref[...], w_ref[...], m_ref[...], v_ref[...])

    spec = pl.BlockSpec((None, rb, C), lambda l, i: (l, i, 0))
    return pl.pallas_call(
        body, name="adamw", grid=(L, R // rb), in_specs=[spec] * 4, out_specs=[spec] * 3,
        out_shape=[jax.ShapeDtypeStruct(g.shape, F32)] * 3, compiler_params=_cp(2),
    )(g, w, m, v)


def _sum_slots(buf):
    def body(b_ref, o_ref):
        acc = b_ref[0]
        for k in range(1, 8):
            acc = acc + b_ref[k]
        o_ref[...] = acc

    return pl.pallas_call(body, name="sum_slots", in_specs=[VMEM], out_specs=VMEM,
                          out_shape=jax.ShapeDtypeStruct(buf.shape[1:], F32))(buf)


def _rms(xf, g):
    r = lax.rsqrt(jnp.mean(xf * xf, axis=-1, keepdims=True) + EPS)
    return xf * r, r


def _lane_chunks(n):
    lo = (n // LANES + 1) // 2 * LANES
    return ((0, lo), (lo, n - lo))


def _load_ffn_weights(win_hbm, wout_hbm, win_v, wout_v, sems):
    fb = win_v.shape[2]
    loads = [pltpu.make_async_copy(win_hbm.at[k], win_v.at[k], sems.at[k]) for k in range(4)]
    loads += [pltpu.make_async_copy(wout_hbm.at[pl.ds(k * fb, fb)], wout_v.at[pl.ds(k * fb, fb)], sems.at[4 + k])
              for k in range(2)]
    for cp in loads:
        cp.start()
    for cp in loads:
        cp.wait()


def _fast_sigmoid(v):
    return pl.reciprocal(1.0 + jnp.exp(-v), approx=True)


def _ffn_fwd(x, g, win, wout):
    T, D = x.shape
    FB = win.shape[2]
    tm = min(TM, T)

    def body(x_ref, g_ref, win_hbm, wout_hbm, xo_ref, gu_ref, win_v, wout_v, sems):
        @pl.when(pl.program_id(0) == 0)
        def _():
            _load_ffn_weights(win_hbm, wout_hbm, win_v, wout_v, sems)

        xf = x_ref[...]
        xh, _ = _rms(xf, None)
        h = (xh * g_ref[...]).astype(BF16)
        acc = jnp.zeros((tm, D), F32)
        for blk in range(2):
            for lo, sz in _lane_chunks(FB):
                cols = pl.ds(blk * FB + lo, sz)
                gate = _dot(h, win_v[blk, :, pl.ds(lo, sz)])
                up = _dot(h, win_v[2 + blk, :, pl.ds(lo, sz)])
                gu_ref[0, :, cols] = gate.astype(BF16)
                gu_ref[1, :, cols] = up.astype(BF16)
                a = (gate * _fast_sigmoid(gate) * up).astype(BF16)
                acc = acc + _dot(a, wout_v[cols, :])
        xo_ref[...] = xf + 0.5 * acc

    row = pl.BlockSpec((tm, D), lambda i: (i, 0))
    return pl.pallas_call(
        body, name="ffn_fwd", grid=(T // tm,),
        in_specs=[row, pl.BlockSpec((1, D), lambda i: (0, 0)), ANY, ANY],
        out_specs=[row, pl.BlockSpec((2, tm, 2 * FB), lambda i: (0, i, 0))],
        out_shape=[jax.ShapeDtypeStruct((T, D), F32), jax.ShapeDtypeStruct((2, T, 2 * FB), BF16)],
        scratch_shapes=[pltpu.VMEM(win.shape, BF16), pltpu.VMEM(wout.shape, BF16), pltpu.SemaphoreType.DMA((6,))],
        compiler_params=_cp(1),
    )(x, g, win, wout)


def _mixproj_fwd(x, g, wt):
    T, D = x.shape
    W = wt.shape[0]
    QKV = ATTN_W + 2 * KV_W
    tm = min(TM_MIX, T)

    def body(x_ref, g_ref, w_ref, qkv_ref, u_ref):
        xh, _ = _rms(x_ref[...], None)
        h = (xh * g_ref[...]).astype(BF16)
        qkv_ref[...] = _dot_nt(h, w_ref[:QKV, :]).astype(BF16)
        u_ref[...] = _dot_nt(h, w_ref[QKV:, :])

    return pl.pallas_call(
        body, name="mixproj_fwd", grid=(T // tm,),
        in_specs=[pl.BlockSpec((tm, D), lambda i: (i, 0)), pl.BlockSpec((1, D), lambda i: (0, 0)),
                  pl.BlockSpec((W, D), lambda i: (0, 0))],
        out_specs=[pl.BlockSpec((tm, QKV), lambda i: (i, 0)), pl.BlockSpec((tm, W - QKV), lambda i: (i, 0))],
        out_shape=[jax.ShapeDtypeStruct((T, QKV), BF16), jax.ShapeDtypeStruct((T, W - QKV), F32)],
        compiler_params=_cp(1),
    )(x, g, wt)


def _attn_bias_table():
    rows, cols = GROUP * WINDOW, 2 * WINDOW
    row = lax.broadcasted_iota(jnp.int32, (N_KV, rows, cols), 1)
    col = lax.broadcasted_iota(jnp.int32, (N_KV, rows, cols), 2)
    head = GROUP * lax.broadcasted_iota(jnp.int32, (N_KV, rows, cols), 0) + (row >> 7)
    dist = (row & (WINDOW - 1)) + WINDOW - col
    slope = jnp.exp2(-(head + 1).astype(F32))
    return jnp.where((dist >= 0) & (dist < WINDOW), -slope * dist.astype(F32), NEG_INF)


def _first_block_mask(n):
    col = lax.broadcasted_iota(jnp.int32, (GROUP * WINDOW, 2 * WINDOW), 1)
    return (n > 0) | (col >= WINDOW)


def _sink_col(sink_ref, g):
    hi = lax.broadcasted_iota(jnp.int32, (GROUP * WINDOW, 1), 0) >> 7
    col = jnp.zeros((GROUP * WINDOW, 1), F32)
    for i in range(GROUP):
        col = jnp.where(hi == i, sink_ref[0, GROUP * g + i], col)
    return col


def _stack_heads(ref, g):
    return jnp.concatenate([ref[:, (GROUP * g + i) * HEAD_DIM:(GROUP * g + i + 1) * HEAD_DIM]
                            for i in range(GROUP)], axis=0)


def _band(kvp_ref, kvc_ref, off):
    return jnp.concatenate([kvp_ref[:, off:off + HEAD_DIM], kvc_ref[:, off:off + HEAD_DIM]], axis=0)


def _attn_probs(qs, k, bias, seen, sink):
    s = jnp.where(seen, _dot_nt(qs, k) * SCALE + bias, NEG_INF)
    m = jnp.maximum(jnp.max(s, axis=-1, keepdims=True), sink)
    p = jnp.exp(s - m)
    es = jnp.exp(sink - m)
    inv = 1.0 / (jnp.sum(p, axis=-1, keepdims=True) + es)
    return p * inv, es * inv


def _attn_fwd(sinks, tab, qkv):
    T = qkv.shape[0]
    nb = T // WINDOW

    def body(sink_ref, tab_ref, q_ref, kvp_ref, kvc_ref, o_ref):
        seen = _first_block_mask(pl.program_id(0))
        for g in range(N_KV):
            qs = _stack_heads(q_ref, g)
            k = _band(kvp_ref, kvc_ref, g * HEAD_DIM)
            v = _band(kvp_ref, kvc_ref, KV_W + g * HEAD_DIM)
            p, _ = _attn_probs(qs, k, tab_ref[g], seen, _sink_col(sink_ref, g))
            o = _dot(p.astype(BF16), v)
            for i in range(GROUP):
                h = GROUP * g + i
                o_ref[:, h * HEAD_DIM:(h + 1) * HEAD_DIM] = o[i * WINDOW:(i + 1) * WINDOW].astype(BF16)

    return pl.pallas_call(
        body, name="attn_fwd", grid=(nb,),
        in_specs=[pl.BlockSpec(memory_space=pltpu.SMEM),
                  pl.BlockSpec(tab.shape, lambda n: (0, 0, 0)),
                  pl.BlockSpec((WINDOW, ATTN_W), lambda n: (n, 0)),
                  pl.BlockSpec((WINDOW, 2 * KV_W), lambda n: (jnp.maximum(n - 1, 0), 2)),
                  pl.BlockSpec((WINDOW, 2 * KV_W), lambda n: (n, 2))],
        out_specs=pl.BlockSpec((WINDOW, ATTN_W), lambda n: (n, 0)),
        out_shape=jax.ShapeDtypeStruct((T, ATTN_W), BF16),
        compiler_params=_cp(1),
    )(sinks, tab, qkv, qkv, qkv)


def _shift_copies(src_ref, dst_ref, n):
    for b in range(1, 8):
        dst_ref[b - 1] = src_ref[b:b + n, :]


def _tap(src_ref, sh_ref, s, c0):
    a, b = divmod(s, 8)
    start = pl.multiple_of(c0 + 8 * a, 8)
    if b == 0:
        return src_ref[pl.ds(start, CONV_ROWS), :]
    return sh_ref[b - 1, pl.ds(start, CONV_ROWS), :]


def _glu_rows(u, ch):
    return u[:, :ch] * _fast_sigmoid(u[:, ch:])


def _fill_z(zs_ref, zsh_ref, uc_ref, up_ref, i, ch, n):
    zs_ref[0:HALO] = jnp.where(i > 0, _glu_rows(up_ref[...], ch), 0.0)
    zs_ref[HALO:] = _glu_rows(uc_ref[...], ch)
    _shift_copies(zs_ref, zsh_ref, n - 8)


def _conv_fwd(u, w, b, lg, lb):
    T = u.shape[0]
    CH = u.shape[1] // 2
    tm = min(TM, T)
    n = tm + HALO
    hb = tm // HALO

    def body(uc_ref, up_ref, w_ref, b_ref, lg_ref, lb_ref, conv_ref, ypre_ref, zs_ref, zsh_ref):
        i = pl.program_id(0)
        _fill_z(zs_ref, zsh_ref, uc_ref, up_ref, i, CH, n)
        bias = b_ref[...]

        def chunk(ci, carry):
            c0 = pl.multiple_of(ci * CONV_ROWS, CONV_ROWS)
            acc = jnp.broadcast_to(bias, (CONV_ROWS, CH))
            for k in range(CONV_W):
                acc = acc + w_ref[k:k + 1, :] * _tap(zs_ref, zsh_ref, HALO - (CONV_W - 1) + k, c0)
            ypre_ref[pl.ds(c0, CONV_ROWS), :] = acc
            return carry

        lax.fori_loop(0, tm // CONV_ROWS, chunk, 0)
        y = ypre_ref[...]
        mu = jnp.mean(y, axis=-1, keepdims=True)
        d = y - mu
        var = jnp.mean(d * d, axis=-1, keepdims=True)
        o = d * lax.rsqrt(var + EPS) * lg_ref[...] + lb_ref[...]
        conv_ref[...] = (o * _fast_sigmoid(o)).astype(BF16)

    vec = pl.BlockSpec((1, CH), lambda i: (0, 0))
    return pl.pallas_call(
        body, name="conv_fwd", grid=(T // tm,),
        in_specs=[pl.BlockSpec((tm, 2 * CH), lambda i: (i, 0)),
                  pl.BlockSpec((HALO, 2 * CH), lambda i: (jnp.maximum(i * hb - 1, 0), 0)),
                  pl.BlockSpec((CONV_W, CH), lambda i: (0, 0)), vec, vec, vec],
        out_specs=[pl.BlockSpec((tm, CH), lambda i: (i, 0)), pl.BlockSpec((tm, CH), lambda i: (i, 0))],
        out_shape=[jax.ShapeDtypeStruct((T, CH), BF16), jax.ShapeDtypeStruct((T, CH), F32)],
        scratch_shapes=[pltpu.VMEM((n, CH), F32), pltpu.VMEM((7, n - 8, CH), F32)],
        compiler_params=_cp(1),
    )(u, u, w, b, lg, lb)


def _mixout_fwd(x, attn, conv, wo):
    T, D = x.shape
    tm = min(TM_MIX, T)
    A = attn.shape[1]

    def body(x_ref, a_ref, c_ref, w_ref, xo_ref):
        xo_ref[...] = x_ref[...] + _dot(a_ref[...], w_ref[:A, :]) + _dot(c_ref[...], w_ref[A:, :])

    return pl.pallas_call(
        body, name="mixout_fwd", grid=(T // tm,),
        in_specs=[pl.BlockSpec((tm, D), lambda i: (i, 0)), pl.BlockSpec((tm, A), lambda i: (i, 0)),
                  pl.BlockSpec((tm, conv.shape[1]), lambda i: (i, 0)), pl.BlockSpec(wo.shape, lambda i: (0, 0))],
        out_specs=pl.BlockSpec((tm, D), lambda i: (i, 0)),
        out_shape=jax.ShapeDtypeStruct((T, D), F32),
        compiler_params=_cp(1),
    )(x, attn, conv, wo)


def _rms_bwd_rows(dh, xf, g):
    xh, r = _rms(xf, None)
    dxn = dh * g
    dx = r * (dxn - xh * jnp.mean(dxn * xh, axis=-1, keepdims=True))
    return dx, jnp.sum(dh * xh, axis=0, keepdims=True), xh * g


def _loss_head(x, g, tgt):
    T, D = x.shape
    tm = min(TM, T)

    def body(x_ref, g_ref, t_ref, loss_ref, dx_ref, dg_ref):
        @pl.when(pl.program_id(0) == 0)
        def _():
            loss_ref[...] = jnp.zeros_like(loss_ref)
            dg_ref[...] = jnp.zeros_like(dg_ref)

        xf = x_ref[...]
        g = g_ref[...]
        xh, _ = _rms(xf, None)
        e = xh * g - t_ref[...]
        loss_ref[...] += 0.5 * jnp.sum(jnp.mean(e * e, axis=-1, keepdims=True), axis=0, keepdims=True)
        dx, dg, _ = _rms_bwd_rows(e * (1.0 / D), xf, g)
        dx_ref[...] = dx
        dg_ref[...] += dg

    return pl.pallas_call(
        body, name="loss_head", grid=(T // tm,),
        in_specs=[pl.BlockSpec((tm, D), lambda i: (i, 0)), pl.BlockSpec((1, D), lambda i: (0, 0)),
                  pl.BlockSpec((tm, D), lambda i: (i, 0))],
        out_specs=[pl.BlockSpec((1, 1), lambda i: (0, 0)), pl.BlockSpec((tm, D), lambda i: (i, 0)),
                   pl.BlockSpec((1, D), lambda i: (0, 0))],
        out_shape=[jax.ShapeDtypeStruct((1, 1), F32), jax.ShapeDtypeStruct((T, D), F32),
                   jax.ShapeDtypeStruct((1, D), F32)],
        compiler_params=_cp(1),
    )(x, g, tgt)


def _ffn_bwd(dxo, x, g, gu, win, wout, dep):
    T, D = x.shape
    FB = win.shape[2]
    tm = min(TM_FFN_BWD, T)

    def body(dxo_ref, x_ref, g_ref, gu_ref, win_hbm, wout_hbm, dep_ref,
             dxi_ref, dg_ref, hb_ref, dgu_ref, a_ref, dyb_ref, win_v, wout_v, sems):
        @pl.when(pl.program_id(0) == 0)
        def _():
            _load_ffn_weights(win_hbm, wout_hbm, win_v, wout_v, sems)
            dg_ref[...] = jnp.zeros_like(dg_ref)

        dyb = (0.5 * dxo_ref[...]).astype(BF16)
        dyb_ref[...] = dyb
        dh = jnp.zeros((tm, D), F32)
        for blk in range(2):
            cols = pl.ds(blk * FB, FB)
            da = _dot_nt(dyb, wout_v[cols, :])
            gate = gu_ref[0, :, cols].astype(F32)
            up = gu_ref[1, :, cols].astype(F32)
            sg = _fast_sigmoid(gate)
            s = gate * sg
            a_ref[:, cols] = (s * up).astype(BF16)
            dgate = (da * up * (sg + s * (1.0 - sg))).astype(BF16)
            dup = (da * s).astype(BF16)
            dgu_ref[0, :, cols] = dgate
            dgu_ref[1, :, cols] = dup
            dh = dh + _dot_nt(dgate, win_v[blk]) + _dot_nt(dup, win_v[2 + blk])
        dx, dg, h = _rms_bwd_rows(dh, x_ref[...], g_ref[...])
        dxi_ref[...] = dxo_ref[...] + dx
        dg_ref[...] += dg
        hb_ref[...] = h.astype(BF16)

    row = pl.BlockSpec((tm, D), lambda i: (i, 0))
    act = pl.BlockSpec((2, tm, 2 * FB), lambda i: (0, i, 0))
    return pl.pallas_call(
        body, name="ffn_bwd", grid=(T // tm,),
        in_specs=[row, row, pl.BlockSpec((1, D), lambda i: (0, 0)), act, ANY, ANY, ANY],
        out_specs=[row, pl.BlockSpec((1, D), lambda i: (0, 0)), row, act,
                   pl.BlockSpec((tm, 2 * FB), lambda i: (i, 0)), row],
        out_shape=[jax.ShapeDtypeStruct((T, D), F32), jax.ShapeDtypeStruct((1, D), F32),
                   jax.ShapeDtypeStruct((T, D), BF16), jax.ShapeDtypeStruct((2, T, 2 * FB), BF16),
                   jax.ShapeDtypeStruct((T, 2 * FB), BF16), jax.ShapeDtypeStruct((T, D), BF16)],
        scratch_shapes=[pltpu.VMEM(win.shape, BF16), pltpu.VMEM(wout.shape, BF16), pltpu.SemaphoreType.DMA((6,))],
        compiler_params=_cp(1),
    )(dxo, x, g, gu, win, wout, dep)


def _mix_rms_bwd(dxo, x, g, dzs, wt):
    T, D = x.shape
    tm = min(TM, T)
    npair = len(dzs)

    def body(*refs):
        dxo_ref, x_ref, g_ref = refs[:3]
        dz_refs, w_ref = refs[3:3 + npair], refs[3 + npair]
        dxi_ref, dg_ref, hb_ref = refs[4 + npair:]

        @pl.when(pl.program_id(0) == 0)
        def _():
            dg_ref[...] = jnp.zeros_like(dg_ref)

        dh = jnp.zeros((tm, D), F32)
        k0 = 0
        for dz_ref in dz_refs:
            kp = dz_ref.shape[1]
            dh = dh + _dot(dz_ref[...], w_ref[k0:k0 + kp, :])
            k0 += kp
        dx, dg, h = _rms_bwd_rows(dh, x_ref[...], g_ref[...])
        dxi_ref[...] = dxo_ref[...] + dx
        dg_ref[...] += dg
        hb_ref[...] = h.astype(BF16)

    row = pl.BlockSpec((tm, D), lambda i: (i, 0))
    return pl.pallas_call(
        body, name="mix_rms_bwd", grid=(T // tm,),
        in_specs=[row, row, pl.BlockSpec((1, D), lambda i: (0, 0))]
                 + [pl.BlockSpec((tm, dz.shape[1]), lambda i: (i, 0)) for dz in dzs]
                 + [pl.BlockSpec(wt.shape, lambda i: (0, 0))],
        out_specs=[row, pl.BlockSpec((1, D), lambda i: (0, 0)), row],
        out_shape=[jax.ShapeDtypeStruct((T, D), F32), jax.ShapeDtypeStruct((1, D), F32),
                   jax.ShapeDtypeStruct((T, D), BF16)],
        compiler_params=_cp(1),
    )(dxo, x, g, *dzs, wt)


def _wgrad(name, a, b, a_spec, b_spec, out_shape, out_spec, nblk, dep, acc_shape):
    T = a.shape[0]
    tk = min(TK_WGRAD, T)
    nk = T // tk

    def body(a_ref, b_ref, dep_ref, o_ref, acc_ref):
        k = pl.program_id(1)

        @pl.when(k == 0)
        def _():
            acc_ref[...] = jnp.zeros_like(acc_ref)

        acc_ref[...] += _dot_tn(a_ref[...], b_ref[...])

        @pl.when(k == nk - 1)
        def _():
            o_ref[...] = acc_ref[...].reshape(o_ref.shape).astype(BF16)

    return pl.pallas_call(
        body, name=name, grid=(nblk, nk), in_specs=[a_spec, b_spec, ANY], out_specs=out_spec,
        out_shape=jax.ShapeDtypeStruct(out_shape, BF16), scratch_shapes=[pltpu.VMEM(acc_shape, F32)],
        compiler_params=_cp(2),
    )(a, b, dep)


def _wgrad_ffn_in(hb, dgu, dep):
    T, D = hb.shape
    FB = dgu.shape[2] // 2
    tk = min(TK_WGRAD, T)
    return _wgrad("wgrad_ffn_in", hb, dgu,
                  pl.BlockSpec((tk, D), lambda b, k: (k, 0)),
                  pl.BlockSpec((None, tk, FB), lambda b, k: (b // 2, k, b % 2)),
                  (4, D, FB), pl.BlockSpec((None, D, FB), lambda b, k: (b, 0, 0)), 4, dep, (D, FB))


def _wgrad_ffn_out(a, dyb, dep):
    T, D = dyb.shape
    FB = a.shape[1] // 2
    tk = min(TK_WGRAD, T)
    return _wgrad("wgrad_ffn_out", a, dyb,
                  pl.BlockSpec((tk, FB), lambda b, k: (k, b)),
                  pl.BlockSpec((tk, D), lambda b, k: (k, 0)),
                  (4, FB // 2, D), pl.BlockSpec((2, FB // 2, D), lambda b, k: (b, 0, 0)), 2, dep, (FB, D))


def _wgrad_cat(a_list, b_list):
    T = a_list[0].shape[0]
    tk = min(TK_WGRAD, T)
    nk = T // tk
    na = len(a_list)
    M, N = sum(a.shape[1] for a in a_list), sum(b.shape[1] for b in b_list)

    def body(*refs):
        a_refs, b_refs, o_ref, acc_ref = refs[:na], refs[na:-2], refs[-2], refs[-1]
        k = pl.program_id(0)

        @pl.when(k == 0)
        def _():
            acc_ref[...] = jnp.zeros_like(acc_ref)

        r0 = 0
        for a_ref in a_refs:
            c0 = 0
            for b_ref in b_refs:
                m, n = a_ref.shape[1], b_ref.shape[1]
                acc_ref[r0:r0 + m, c0:c0 + n] += _dot_tn(a_ref[...], b_ref[...])
                c0 += n
            r0 += a_ref.shape[1]

        @pl.when(k == nk - 1)
        def _():
            o_ref[...] = acc_ref[...].astype(BF16)

    return pl.pallas_call(
        body, name="wgrad_cat", grid=(nk,),
        in_specs=[pl.BlockSpec((tk, v.shape[1]), lambda k: (k, 0)) for v in list(a_list) + list(b_list)],
        out_specs=pl.BlockSpec((M, N), lambda k: (0, 0)),
        out_shape=jax.ShapeDtypeStruct((M, N), BF16), scratch_shapes=[pltpu.VMEM((M, N), F32)],
        compiler_params=_cp(1),
    )(*a_list, *b_list)


def _mixout_bwd(dxo, wo):
    T, D = dxo.shape
    tm = min(TM_MIX, T)
    A = ATTN_W
    C = wo.shape[0] - A

    def body(dxo_ref, w_ref, dyb_ref, da_ref, dc_ref):
        dyb = dxo_ref[...].astype(BF16)
        dyb_ref[...] = dyb
        da_ref[...] = _dot_nt(dyb, w_ref[:A, :]).astype(BF16)
        dc_ref[...] = _dot_nt(dyb, w_ref[A:, :])

    return pl.pallas_call(
        body, name="mixout_bwd", grid=(T // tm,),
        in_specs=[pl.BlockSpec((tm, D), lambda i: (i, 0)), pl.BlockSpec(wo.shape, lambda i: (0, 0))],
        out_specs=[pl.BlockSpec((tm, D), lambda i: (i, 0)), pl.BlockSpec((tm, A), lambda i: (i, 0)),
                   pl.BlockSpec((tm, C), lambda i: (i, 0))],
        out_shape=[jax.ShapeDtypeStruct((T, D), BF16), jax.ShapeDtypeStruct((T, A), BF16),
                   jax.ShapeDtypeStruct((T, C), F32)],
        compiler_params=_cp(1),
    )(dxo, wo)


def _conv_bwd(dconv, ypre, u, w, lg, lb):
    T, CH = dconv.shape
    tm = min(TM, T)
    n = tm + HALO
    hb = tm // HALO
    nt = T // tm
    nchunk = tm // CONV_ROWS

    def body(dc_ref, dcn_ref, yp_ref, ypn_ref, uc_ref, up_ref, w_ref, lg_ref, lb_ref,
             du_ref, dw_ref, dvec_ref, zs_ref, zsh_ref, dy_ref, dysh_ref, dz_ref, dwacc_ref):
        i = pl.program_id(0)

        @pl.when(i == 0)
        def _():
            dwacc_ref[...] = jnp.zeros_like(dwacc_ref)
            dvec_ref[...] = jnp.zeros_like(dvec_ref)

        g, bb = lg_ref[...], lb_ref[...]

        def ln_bwd(dc, yp):
            mu = jnp.mean(yp, axis=-1, keepdims=True)
            d = yp - mu
            rs = lax.rsqrt(jnp.mean(d * d, axis=-1, keepdims=True) + EPS)
            yn = d * rs
            o = yn * g + bb
            sg = _fast_sigmoid(o)
            do = dc * (sg * (1.0 + o * (1.0 - sg)))
            dyn = do * g
            dyp = rs * (dyn - jnp.mean(dyn, axis=-1, keepdims=True)
                        - yn * jnp.mean(dyn * yn, axis=-1, keepdims=True))
            return dyp, do, yn

        dyp, do, yn = ln_bwd(dc_ref[...], yp_ref[...])
        dvec_ref[0:1, :] += jnp.sum(dyp, axis=0, keepdims=True)
        dvec_ref[1:2, :] += jnp.sum(do * yn, axis=0, keepdims=True)
        dvec_ref[2:3, :] += jnp.sum(do, axis=0, keepdims=True)
        dy_ref[0:tm] = dyp
        dyh, _, _ = ln_bwd(dcn_ref[...], ypn_ref[...])
        dy_ref[tm:] = jnp.where(i < nt - 1, dyh, 0.0)
        _shift_copies(dy_ref, dysh_ref, n - 8)
        _fill_z(zs_ref, zsh_ref, uc_ref, up_ref, i, CH, n)

        def chunk(ci, carry):
            c0 = pl.multiple_of(ci * CONV_ROWS, CONV_ROWS)
            acc = jnp.zeros((CONV_ROWS, CH), F32)
            for k in range(CONV_W):
                acc = acc + w_ref[k:k + 1, :] * _tap(dy_ref, dysh_ref, CONV_W - 1 - k, c0)
            dz_ref[pl.ds(c0, CONV_ROWS), :] = acc
            dyc = dy_ref[pl.ds(c0, CONV_ROWS), :]
            for k in range(CONV_W):
                prod = dyc * _tap(zs_ref, zsh_ref, HALO - (CONV_W - 1) + k, c0)
                dwacc_ref[k] += jnp.sum(prod.reshape(CONV_ROWS // 8, 8, CH), axis=0)
            return carry

        lax.fori_loop(0, nchunk, chunk, 0)

        @pl.when(i == nt - 1)
        def _():
            dw_ref[...] = jnp.sum(dwacc_ref[...], axis=1)

        uc = uc_ref[...]
        a = uc[:, :CH]
        sg = _fast_sigmoid(uc[:, CH:])
        dz = dz_ref[...]
        du_ref[:, :CH] = (dz * sg).astype(BF16)
        du_ref[:, CH:] = (dz * a * sg * (1.0 - sg)).astype(BF16)

    cur = lambda c: pl.BlockSpec((tm, c), lambda i: (i, 0))
    nxt = lambda c: pl.BlockSpec((HALO, c), lambda i: (jnp.minimum((i + 1) * hb, T // HALO - 1), 0))
    vec = pl.BlockSpec((1, CH), lambda i: (0, 0))
    return pl.pallas_call(
        body, name="conv_bwd", grid=(nt,),
        in_specs=[cur(CH), nxt(CH), cur(CH), nxt(CH), cur(2 * CH),
                  pl.BlockSpec((HALO, 2 * CH), lambda i: (jnp.maximum(i * hb - 1, 0), 0)),
                  pl.BlockSpec((CONV_W, CH), lambda i: (0, 0)), vec, vec],
        out_specs=[pl.BlockSpec((tm, 2 * CH), lambda i: (i, 0)), pl.BlockSpec((32, CH), lambda i: (0, 0)),
                   pl.BlockSpec((8, CH), lambda i: (0, 0))],
        out_shape=[jax.ShapeDtypeStruct((T, 2 * CH), BF16), jax.ShapeDtypeStruct((32, CH), F32),
                   jax.ShapeDtypeStruct((8, CH), F32)],
        scratch_shapes=[pltpu.VMEM((n, CH), F32), pltpu.VMEM((7, n - 8, CH), F32),
                        pltpu.VMEM((n, CH), F32), pltpu.VMEM((7, n - 8, CH), F32), pltpu.VMEM((tm, CH), F32),
                        pltpu.VMEM((32, 8, CH), F32)],
        compiler_params=_cp(1),
    )(dconv, dconv, ypre, ypre, u, u, w, lg, lb)


def _attn_bwd(sinks, tab, qkv, dattn):
    T = qkv.shape[0]
    nb = T // WINDOW

    def body(sink_ref, tab_ref, q_ref, kvp_ref, kvc_ref, do_ref, dq_ref, dkv_ref, dsk_ref, carry_ref):
        n = pl.program_id(0)

        @pl.when(n == 0)
        def _():
            dsk_ref[...] = jnp.zeros_like(dsk_ref)
            carry_ref[...] = jnp.zeros_like(carry_ref)

        @pl.when(n < nb)
        def _():
            seen = _first_block_mask(n)
            for g in range(N_KV):
                qs = _stack_heads(q_ref, g)
                dos = _stack_heads(do_ref, g)
                k = _band(kvp_ref, kvc_ref, g * HEAD_DIM)
                v = _band(kvp_ref, kvc_ref, KV_W + g * HEAD_DIM)
                p, ps = _attn_probs(qs, k, tab_ref[g], seen, _sink_col(sink_ref, g))
                dp = _dot_nt(dos, v)
                delta = jnp.sum(p * dp, axis=-1, keepdims=True)
                dsb = (p * (dp - delta)).astype(BF16)
                dsink = -ps * delta
                dqs = _dot(dsb, k) * SCALE
                dk = _dot_tn(dsb, qs) * SCALE
                dv = _dot_tn(p.astype(BF16), dos)
                for i in range(GROUP):
                    h = GROUP * g + i
                    dq_ref[:, h * HEAD_DIM:(h + 1) * HEAD_DIM] = dqs[i * WINDOW:(i + 1) * WINDOW].astype(BF16)
                    dsk_ref[h:h + 1, :] += jnp.sum(dsink[i * WINDOW:(i + 1) * WINDOW], axis=0, keepdims=True)
                for off, d in ((g * HEAD_DIM, dk), (KV_W + g * HEAD_DIM, dv)):
                    dkv_ref[:, off:off + HEAD_DIM] = (carry_ref[:, off:off + HEAD_DIM] + d[:WINDOW]).astype(BF16)
                    carry_ref[:, off:off + HEAD_DIM] = d[WINDOW:]

        @pl.when(n == nb)
        def _():
            dkv_ref[...] = carry_ref[...].astype(BF16)

    last = nb - 1
    return pl.pallas_call(
        body, name="attn_bwd", grid=(nb + 1,),
        in_specs=[pl.BlockSpec(memory_space=pltpu.SMEM),
                  pl.BlockSpec(tab.shape, lambda n: (0, 0, 0)),
                  pl.BlockSpec((WINDOW, ATTN_W), lambda n: (jnp.minimum(n, last), 0)),
                  pl.BlockSpec((WINDOW, 2 * KV_W), lambda n: (jnp.clip(n - 1, 0, last), 2)),
                  pl.BlockSpec((WINDOW, 2 * KV_W), lambda n: (jnp.minimum(n, last), 2)),
                  pl.BlockSpec((WINDOW, ATTN_W), lambda n: (jnp.minimum(n, last), 0))],
        out_specs=[pl.BlockSpec((WINDOW, ATTN_W), lambda n: (jnp.minimum(n, last), 0)),
                   pl.BlockSpec((WINDOW, 2 * KV_W), lambda n: (jnp.maximum(n - 1, 0), 0)),
                   pl.BlockSpec((8, LANES), lambda n: (0, 0))],
        out_shape=[jax.ShapeDtypeStruct((T, ATTN_W), BF16), jax.ShapeDtypeStruct((T, 2 * KV_W), BF16),
                   jax.ShapeDtypeStruct((8, LANES), F32)],
        scratch_shapes=[pltpu.VMEM((WINDOW, 2 * KV_W), F32)],
        compiler_params=_cp(1),
    )(sinks, tab, qkv, qkv, qkv, dattn)


def _pack(arrs):
    flat = jnp.concatenate([a.reshape(-1) for a in arrs])
    pad = -flat.shape[0] % (8 * LANES)
    return jnp.pad(flat, (0, pad)).reshape(1, -1, LANES)


def _unpack(packed, like):
    flat = packed.reshape(-1)
    out, off = [], 0
    for a in like:
        out.append(flat[off:off + a.size].reshape(a.shape))
        off += a.size
    return out


def kernel(x, norm_ffn1, w_ffn1_in, w_ffn1_out, norm_mix, w_in, sinks, w_dw, b_dw, conv_ln_g, conv_ln_b, w_out, norm_ffn2, w_ffn2_in, w_ffn2_out, final_norm, loss_target, m_norm_ffn1, m_w_ffn1_in, m_w_ffn1_out, m_norm_mix, m_w_in, m_sinks, m_w_dw, m_b_dw, m_conv_ln_g, m_conv_ln_b, m_w_out, m_norm_ffn2, m_w_ffn2_in, m_w_ffn2_out, m_final_norm, v_norm_ffn1, v_w_ffn1_in, v_w_ffn1_out, v_norm_mix, v_w_in, v_sinks, v_w_dw, v_b_dw, v_conv_ln_g, v_conv_ln_b, v_w_out, v_norm_ffn2, v_w_ffn2_in, v_w_ffn2_out, v_final_norm):
    L, D = norm_ffn1.shape
    T = x.shape[1]
    FB = w_ffn1_in.shape[2]
    CH = b_dw.shape[1]
    QKV = ATTN_W + 2 * KV_W
    xs = x.reshape(T, D)
    tgt = loss_target.reshape(T, D)
    cx, cy, cc = lax.axis_index("x"), lax.axis_index("y"), lax.axis_index("c")
    chip = 2 * cx + cy
    cidx = cc.reshape(1).astype(jnp.int32)
    tr = lambda a_: jnp.transpose(a_, (0, 2, 1))
    big_w = (w_ffn1_in, w_ffn1_out, tr(w_in), w_out, w_ffn2_in, w_ffn2_out)
    big_m = (m_w_ffn1_in, m_w_ffn1_out, tr(m_w_in), m_w_out, m_w_ffn2_in, m_w_ffn2_out)
    big_v = (v_w_ffn1_in, v_w_ffn1_out, tr(v_w_in), v_w_out, v_w_ffn2_in, v_w_ffn2_out)
    NW = len(big_w) + 1

    def own_slot(a, slots=4, idx=chip):
        return lax.dynamic_update_index_in_dim(lax.empty((slots,) + a.shape, a.dtype), a, idx, 0)

    def shards(l, tok):
        return [own_slot((w_[l] + tok[0, 0]).astype(BF16)) for w_ in big_w] + [own_slot(w_dw[l] + tok[0, 0])]

    def gather_start(lands, tok):
        return _xchg_start("gather_start", [], lands, _gather_plan, tok)

    def gather_arrived(started, after, n, taps):
        _, lands, tok = _xchg_wait("gather_wait", started, 0, n, _gather_plan, after)
        return _xchg_start("gshare_start", [], lands[:-1] if taps else lands, _gshare_plan, tok, "sibling3"), lands[-1]

    def shared_weights(shared, after, n):
        _, mats, tok = _xchg_wait("gshare_wait", shared, 0, n, _gshare_plan, after, "sibling3")
        return mats, tok

    row = lambda a, l: a[l].reshape(1, -1)
    tab = _attn_bias_table()
    NB = len(big_w)

    saved, W = [], []
    zero_tok = jnp.zeros((8, LANES), F32)
    src0 = shards(0, zero_tok)
    started = gather_start(src0[:2], zero_tok)
    rest0 = gather_start(src0[2:], started[-1])
    cast = [None] + [shards(l, rest0[-1]) for l in range(1, L)]
    shared, _ = gather_arrived(started, [xs] + [a_ for c_ in cast[1:] for a_ in c_], 2, False)
    after = [shared[-1]]
    for l in range(L):
        mats, tok = shared_weights(shared, after, 2 if l == 0 else NB)
        started = None
        if l + 1 < L:
            started = gather_start(cast[l + 1], tok)
            tok = started[-1]
        x0 = xs
        x1, gu1 = _ffn_fwd(x0, row(norm_ffn1, l) + tok[0, 0], mats[0], mats[1].reshape(2 * FB, D))
        gm_row = row(norm_mix, l)
        if l == 0:
            shared, gdw = gather_arrived(rest0, [x1], NW - 2, True)
            rest, tok = shared_weights(shared, [shared[-1]], NB - 2)
            mats = list(mats) + list(rest)
            gm_row = gm_row + tok[0, 0]
        g1i, g1o, gi, go, g2i, g2o = mats
        w = dict(f1i=g1i, f1o=g1o.reshape(2 * FB, D), f2i=g2i, f2o=g2o.reshape(2 * FB, D),
                 wit=gi.reshape(-1, D), wo=go.reshape(-1, D),
                 wdw=jnp.transpose(gdw, (1, 0, 2)).reshape(CONV_W, CH))
        W.append(w)
        qkv, u = _mixproj_fwd(x1, gm_row, w["wit"])
        attn = _attn_fwd(row(sinks, l), tab, qkv)
        conv, ypre = _conv_fwd(u, w["wdw"], row(b_dw, l), row(conv_ln_g, l), row(conv_ln_b, l))
        x2 = _mixout_fwd(x1, attn, conv, w["wo"])
        g2_row = row(norm_ffn2, l)
        if started is not None and l > 0:
            shared, gdw = gather_arrived(started, [x2], NW, True)
            g2_row = g2_row + shared[-1][0, 0]
        xs, gu2 = _ffn_fwd(x2, g2_row, w["f2i"], w["f2o"])
        if started is not None and l == 0:
            shared, gdw = gather_arrived(started, [xs], NW, True)
        saved.append((x0, gu1, x1, qkv, u, attn, conv, ypre, x2, gu2))
        after = [xs]

    loss_part, dx, d_final = _loss_head(xs, final_norm.reshape(1, D), tgt)
    loss = lax.psum(loss_part[0, 0], ("x", "y", "c"))

    bufs = [[lax.empty(w_.shape, F32) for _ in range(4)] for w_ in big_w]
    d_n1, d_nm, d_n2 = [None] * L, [None] * L, [None] * L
    d_sk, d_bdw, d_lg, d_lb, d_wdw = [None] * L, [None] * L, [None] * L, [None] * L, [None] * L

    me_idx = 4 * cx + 2 * cy + cc

    def reduce_start(gs):
        lands = []
        for g in gs:
            h = g.shape[1] // 2
            mine = lax.dynamic_slice(g, (chip, cc * h, 0), (1, h, g.shape[2]))[0]
            lands.append(own_slot(mine, 8, me_idx))
        return _xchg_start("rs_start", gs, lands, _rs_plan, zero_tok, "all")

    sent = []

    def finish(l, rs_started, after, idxs):
        _, qs, tok = _xchg_wait("rs_wait", rs_started, len(idxs), len(idxs), _rs_plan, after, "all")
        for k, t in enumerate(idxs):
            bufs[t] = _adamw_layer(cidx, qs[k], big_w[t], big_m[t], big_v[t], bufs[t], l)
        flat = [b_ for t in idxs for b_ in bufs[t]]
        st = _xchg_start("oshare_start", [], flat, _oshare_plan(l), tok, "sibling")
        for k, t in enumerate(idxs):
            bufs[t] = list(st[2 + 4 * k:6 + 4 * k])
        sent.append((st, idxs, _oshare_plan(l)))

    ALL = list(range(NB))
    EARLY, LATE = ALL[2:], ALL[:2]
    rs_list = []
    tok = zero_tok
    for l in reversed(range(L)):
        w = W[l]
        x0, gu1, x1, qkv, u, attn, conv, ypre, x2, gu2 = saved[l]
        dx, d_n2[l], hb, dgu, a, dyb = _ffn_bwd(dx, x2, row(norm_ffn2, l), gu2, w["f2i"], w["f2o"], tok)
        g_f2i, g_f2o = _wgrad_ffn_in(hb, dgu, tok), _wgrad_ffn_out(a, dyb, tok)
        lg_row = row(conv_ln_g, l)
        dyb, dattn, dconv = _mixout_bwd(dx, w["wo"])
        g_wo = _wgrad_cat([attn, conv], [dyb]).reshape(4, -1, D)
        du, dwdw, dvec = _conv_bwd(dconv, ypre, u, w["wdw"], lg_row, row(conv_ln_b, l))
        d_wdw[l], d_bdw[l], d_lg[l], d_lb[l] = dwdw[:CONV_W], dvec[0], dvec[1], dvec[2]
        dq, dkv, dsk = _attn_bwd(row(sinks, l), tab, qkv, dattn)
        d_sk[l] = dsk[:, 0]
        dx, d_nm[l], hb = _mix_rms_bwd(dx, x1, row(norm_mix, l), [dq, dkv, du], w["wit"])
        g_wi = _wgrad_cat([dq, dkv, du], [hb]).reshape(4, -1, D)
        if l == 0:
            rs_early = reduce_start([g_wi, g_wo, g_f2i, g_f2o])
            tok = rs_early[-1]
        dx, d_n1[l], hb, dgu, a, dyb = _ffn_bwd(dx, x0, row(norm_ffn1, l), gu1, w["f1i"], w["f1o"], tok)
        g_f1i, g_f1o = _wgrad_ffn_in(hb, dgu, tok), _wgrad_ffn_out(a, dyb, tok)
        rs_started = reduce_start([g_f1i, g_f1o] if l == 0 else [g_f1i, g_f1o, g_wi, g_wo, g_f2i, g_f2o])
        tok = rs_started[-1]
        rs_list.append((l, rs_started))
    grad_x = dx.reshape(x.shape)

    small_g = [jnp.concatenate(d, axis=0) for d in (d_n1, d_nm, d_n2)] + [d_final, jnp.stack(d_sk)] + \
              [jnp.stack(d) for d in (d_bdw, d_lg, d_lb, d_wdw)]
    packed = _pack(small_g)[0]
    small_started = _xchg_start("small_start", [packed], [own_slot(packed, 8, 4 * cx + 2 * cy + cc)], _slot_plan, tok, "all")

    rs_late = rs_list.pop()[1]
    after = [small_started[-1]]
    for l, st in rs_list:
        finish(l, st, after, ALL)
        after = [b_[0] for b_ in bufs]
    _, (slots,), _ = _xchg_wait("small_wait", small_started, 1, 1, _slot_plan, after, "all")
    small_sum = _unpack(_sum_slots(slots), small_g)
    g_wdw = lax.dynamic_slice_in_dim(small_sum[8], chip * w_dw.shape[2], w_dw.shape[2], axis=2)
    small_g = [small_sum[0], small_sum[1], small_sum[2], small_sum[3].reshape(D), small_sum[4],
               small_sum[5], small_sum[6], small_sum[7], g_wdw]
    small_w = (norm_ffn1, norm_mix, norm_ffn2, final_norm, sinks, b_dw, conv_ln_g, conv_ln_b, w_dw)
    small_m = (m_norm_ffn1, m_norm_mix, m_norm_ffn2, m_final_norm, m_sinks, m_b_dw, m_conv_ln_g, m_conv_ln_b, m_w_dw)
    small_v = (v_norm_ffn1, v_norm_mix, v_norm_ffn2, v_final_norm, v_sinks, v_b_dw, v_conv_ln_g, v_conv_ln_b, v_w_dw)
    upd = _adamw(_pack(small_g), _pack(small_w), _pack(small_m), _pack(small_v))
    small_upd = [_unpack(u_, small_w) for u_ in upd]
    finish(0, rs_early, [upd[0]], EARLY)
    finish(0, rs_late, [b_[0] for b_ in bufs], LATE)
    after = []
    for st, idxs, plan in sent:
        flat = [b_ for t in idxs for b_ in bufs[t]]
        _, flat, tok = _xchg_wait("oshare_wait", st, 0, len(flat), plan, after, "sibling", latest=flat)
        for k, t in enumerate(idxs):
            bufs[t] = list(flat[4 * k:4 * k + 4])
        after = [tok]
    flat = _hand_on([b_ for t in ALL for b_ in bufs[t]])
    for t in ALL:
        bufs[t] = list(flat[4 * t:4 * t + 4])

    order = ("norm_ffn1", "w_ffn1_in", "w_ffn1_out", "norm_mix", "w_in", "sinks", "w_dw", "b_dw", "conv_ln_g",
             "conv_ln_b", "w_out", "norm_ffn2", "w_ffn2_in", "w_ffn2_out", "final_norm")
    small_names = ("norm_ffn1", "norm_mix", "norm_ffn2", "final_norm", "sinks", "b_dw", "conv_ln_g", "conv_ln_b", "w_dw")
    big_names = ("w_ffn1_in", "w_ffn1_out", "w_in", "w_out", "w_ffn2_in", "w_ffn2_out")
    grads, deltas, new_m, new_v = {}, {}, {}, {}
    for i, nme in enumerate(small_names):
        grads[nme], deltas[nme], new_m[nme], new_v[nme] = small_g[i], small_upd[0][i], small_upd[1][i], small_upd[2][i]
    for i, nme in enumerate(big_names):
        grads[nme], deltas[nme], new_m[nme], new_v[nme] = [tr(b_) for b_ in bufs[i]] if nme == "w_in" else bufs[i]
    return (loss, grad_x, *[grads[n] for n in order], *[deltas[n] for n in order],
            *[new_m[n] for n in order], *[new_v[n] for n in order])
```

```python
import jax
import jax.numpy as jnp
from jax import lax
from jax.experimental import pallas as pl
from jax.experimental.pallas import tpu as pltpu

F32, BF16 = jnp.float32, jnp.bfloat16
EPS = 1e-6
NEG_INF = -1e30
HEAD_DIM = 64
N_HEADS = 8
N_KV = 2
GROUP = N_HEADS // N_KV
WINDOW = 128
ATTN_W = N_HEADS * HEAD_DIM
KV_W = N_KV * HEAD_DIM
CONV_W = 31
HALO = 32
CONV_ROWS = 32
SCALE = 1.0 / 8.0
ADAM_LR, ADAM_B1, ADAM_B2, ADAM_EPS, ADAM_WD, ADAM_STEP = 0.001, 0.9, 0.999, 1e-08, 0.01, 10
TM = 512
TM_FFN_BWD = 256
TK_WGRAD = 2048
TM_MIX = 1024
LANES = 128
VMEM_LIMIT = 52 * 1024 * 1024
MESH = pl.DeviceIdType.MESH
ANY = pl.BlockSpec(memory_space=pl.ANY)
HBM = pl.BlockSpec(memory_space=pltpu.HBM)
SEM = pl.BlockSpec(memory_space=pltpu.SEMAPHORE)
VMEM = pl.BlockSpec(memory_space=pltpu.VMEM)
EFFECT = pltpu.SideEffectType.DATAFLOW_SIDE_EFFECTING
TOKEN = jax.ShapeDtypeStruct((8, LANES), F32)


def _cp(n):
    return pltpu.CompilerParams(dimension_semantics=("arbitrary",) * n, vmem_limit_bytes=VMEM_LIMIT)


def _dot(a, b):
    return jnp.dot(a, b, preferred_element_type=F32)


def _dot_nt(a, b):
    return lax.dot_general(a, b, (((1,), (1,)), ((), ())), preferred_element_type=F32)


def _dot_tn(a, b):
    return lax.dot_general(a, b, (((0,), (0,)), ((), ())), preferred_element_type=F32)


def _place():
    x, y, c = lax.axis_index("x"), lax.axis_index("y"), lax.axis_index("c")
    chips = [(1 - x, y), (x, 1 - y), (1 - x, 1 - y)]
    return x, y, c, chips


def _rcopy(src, dst, send_sems, recv_sems, k, dev):
    return pltpu.make_async_remote_copy(src_ref=src, dst_ref=dst, send_sem=send_sems.at[k],
                                        recv_sem=recv_sems.at[k], device_id=dev, device_id_type=MESH)


def _hbm(a):
    return pltpu.with_memory_space_constraint(a, pltpu.HBM)


PEERS = {"chips": 3, "sibling": 1, "sibling3": 3, "all": 7}


def _targets(mode):
    x, y, c, chips = _place()
    b = 2 * x + y
    if mode == "chips":
        return b, c, [((px, py, c), 2 * px + py) for px, py in chips]
    if mode == "sibling":
        return b, c, [((x, y, 1 - c), b)]
    if mode == "sibling3":
        return b, c, [((x, y, 1 - c), 2 * px + py) for px, py in chips]
    flip = lambda v, f: 1 - v if f else v
    devs = [(flip(x, k >> 2 & 1), flip(y, k >> 1 & 1), flip(c, k & 1)) for k in range(1, 8)]
    return 4 * x + 2 * y + c, c, [(d, 4 * d[0] + 2 * d[1] + d[2]) for d in devs]


def _xchg_start(name, srcs, lands, plan, dep, mode="chips"):
    ns, nl, npeer = len(srcs), len(lands), PEERS[mode]

    def body(*refs):
        land = refs[ns:ns + nl]
        src = refs[:ns] if ns else land
        send_sems, recv_sems, token = refs[ns + nl + 1], refs[ns + nl + 2], refs[-1]
        me, c, peers = _targets(mode)
        for t in range(nl):
            for j, (dev, tag) in enumerate(peers):
                s, d, _ = plan(src[t], land[t], t, me, c, tag)
                _rcopy(s, d, send_sems, recv_sems, npeer * t + j, dev).start()
        token[...] = jnp.zeros_like(token)

    arrs = list(srcs) + list(lands)
    return pl.pallas_call(
        body, name=name,
        out_shape=(pltpu.SemaphoreType.DMA((npeer * nl,)), pltpu.SemaphoreType.DMA((npeer * nl,)),
                   *[pltpu.HBM(a.shape, a.dtype) for a in arrs], TOKEN),
        in_specs=[HBM] * (ns + nl) + [ANY], out_specs=(SEM, SEM, *[HBM] * (ns + nl), VMEM),
        input_output_aliases={i: 2 + i for i in range(ns + nl)},
        compiler_params=pltpu.CompilerParams(has_side_effects=EFFECT),
    )(*[_hbm(a) for a in arrs], dep)


def _xchg_wait(name, started, ns, nl, plan, after, mode="chips"):
    send_sems, recv_sems, thru = started[0], started[1], started[2:2 + ns + nl]
    npeer = PEERS[mode]

    def body(*refs):
        land = refs[ns:ns + nl]
        src = refs[:ns] if ns else land
        send_sems, recv_sems, token = refs[ns + nl], refs[ns + nl + 1], refs[-1]
        me, c, peers = _targets(mode)
        for t in range(nl):
            for j, (dev, tag) in enumerate(peers):
                s, _, a = plan(src[t], land[t], t, me, c, tag)
                cp = _rcopy(s, a, send_sems, recv_sems, npeer * t + j, dev)
                cp.wait_send()
                cp.wait_recv()
        token[...] = jnp.zeros_like(token)

    out = pl.pallas_call(
        body, name=name,
        out_shape=(*[pltpu.HBM(a.shape, a.dtype) for a in thru], TOKEN),
        in_specs=[HBM] * (ns + nl) + [SEM, SEM] + [ANY] * len(after), out_specs=(*[HBM] * (ns + nl), VMEM),
        input_output_aliases={i: i for i in range(ns + nl)},
        compiler_params=pltpu.CompilerParams(has_side_effects=EFFECT),
    )(*thru, send_sems, recv_sems, *after)
    return out[:ns], out[ns:ns + nl], out[-1]


def _half(ref_rows, which):
    h = ref_rows // 2
    return pl.ds(which * h, h)


def _gather_plan(src, land, t, b, c, pb):
    if land.shape[1] % 2 == 0:
        hs = _half(land.shape[1], c)
        return land.at[b, hs], land.at[b, hs], land.at[pb, hs]
    return land.at[b], land.at[b], land.at[pb]


def _gshare_plan(src, land, t, b, c, pb):
    return land.at[pb, _half(land.shape[1], c)], land.at[pb, _half(land.shape[1], c)], land.at[pb, _half(land.shape[1], 1 - c)]


def _rs_plan(src, land, t, me, c, tag):
    h = src.shape[1] // 2
    return src.at[tag // 2, pl.ds((tag % 2) * h, h), :], land.at[me], land.at[tag]


def _rows_block(h, cap=512):
    for rb in range(min(h, cap) // 16 * 16, 0, -16):
        if h % rb == 0:
            return rb
    return h


def _whole_plan(src, land, t, me, c, tag):
    return src, land, land


def _slot_plan(src, land, t, me, c, tag):
    return src, land.at[me], land.at[tag]


def _adam_update(gg, w, m, v):
    m2 = ADAM_B1 * m + (1.0 - ADAM_B1) * gg
    v2 = ADAM_B2 * v + (1.0 - ADAM_B2) * (gg * gg)
    mh = m2 / (1.0 - ADAM_B1 ** ADAM_STEP)
    vh = v2 / (1.0 - ADAM_B2 ** ADAM_STEP)
    return -ADAM_LR * (mh / (jnp.sqrt(vh) + ADAM_EPS) + ADAM_WD * w), m2, v2


def _adamw_layer(cidx, q_own, q_sib, w, m, v, bufs, l):
    L, R, C = w.shape
    h = R // 2
    rb = _rows_block(h, 256)
    nr = h // rb

    def body(c_ref, qo_ref, qs_ref, w_ref, m_ref, v_ref, *rest):
        g_ref, d_ref, mo_ref, vo_ref = rest[-4:]
        own = pl.program_id(0) == c_ref[0]

        def update(q_ref):
            gg = q_ref[0].astype(F32)
            for s in range(1, 8):
                gg = gg + q_ref[s].astype(F32)
            g_ref[...] = gg
            d_ref[...], mo_ref[...], vo_ref[...] = _adam_update(gg, w_ref[...], m_ref[...], v_ref[...])

        @pl.when(own)
        def _():
            update(qo_ref)

        @pl.when(jnp.logical_not(own))
        def _():
            update(qs_ref)

    q_own_spec = pl.BlockSpec((8, rb, C), lambda hh, i, c: (0, jnp.where(hh == c[0], i, 0), 0))
    q_sib_spec = pl.BlockSpec((8, rb, C), lambda hh, i, c: (0, jnp.where(hh == c[0], 0, i), 0))
    wspec = pl.BlockSpec((None, rb, C), lambda hh, i, c: (l, hh * nr + i, 0))
    return pl.pallas_call(
        body, name="adamw_layer", out_shape=[jax.ShapeDtypeStruct(w.shape, F32)] * 4,
        grid_spec=pltpu.PrefetchScalarGridSpec(
            num_scalar_prefetch=1, grid=(2, nr),
            in_specs=[q_own_spec, q_sib_spec, wspec, wspec, wspec] + [ANY] * 4, out_specs=[wspec] * 4),
        input_output_aliases={6 + k: k for k in range(4)},
        compiler_params=_cp(2),
    )(cidx, q_own, q_sib, w, m, v, *bufs)


def _adamw(g, w, m, v):
    L, R, C = g.shape
    rb = _rows_block(R)

    def body(g_ref, w_ref, m_ref, v_ref, d_ref, mo_ref, vo_ref):
        d_ref[...], mo_ref[...], vo_ref[...] = _adam_update(g_ref[...], w_ref[...], m_ref[...], v_ref[...])

    spec = pl.BlockSpec((None, rb, C), lambda l, i: (l, i, 0))
    return pl.pallas_call(
        body, name="adamw", grid=(L, R // rb), in_specs=[spec] * 4, out_specs=[spec] * 3,
        out_shape=[jax.ShapeDtypeStruct(g.shape, F32)] * 3, compiler_params=_cp(2),
    )(g, w, m, v)


def _sum_slots(buf):
    def body(b_ref, o_ref):
        acc = b_ref[0]
        for k in range(1, 8):
            acc = acc + b_ref[k]
        o_ref[...] = acc

    return pl.pallas_call(body, name="sum_slots", in_specs=[VMEM], out_specs=VMEM,
                          out_shape=jax.ShapeDtypeStruct(buf.shape[1:], F32))(buf)


def _rms(xf, g):
    r = lax.rsqrt(jnp.mean(xf * xf, axis=-1, keepdims=True) + EPS)
    return xf * r, r


def _lane_chunks(n):
    lo = (n // LANES + 1) // 2 * LANES
    return ((0, lo), (lo, n - lo))


def _load_ffn_weights(win_hbm, wout_hbm, win_v, wout_v, sems):
    fb = win_v.shape[2]
    loads = [pltpu.make_async_copy(win_hbm.at[k], win_v.at[k], sems.at[k]) for k in range(4)]
    loads += [pltpu.make_async_copy(wout_hbm.at[pl.ds(k * fb, fb)], wout_v.at[pl.ds(k * fb, fb)], sems.at[4 + k])
              for k in range(2)]
    for cp in loads:
        cp.start()
    for cp in loads:
        cp.wait()


def _fast_sigmoid(v):
    return pl.reciprocal(1.0 + jnp.exp(-v), approx=True)


def _ffn_fwd(x, g, win, wout):
    T, D = x.shape
    FB = win.shape[2]
    tm = min(TM, T)

    def body(x_ref, g_ref, win_hbm, wout_hbm, xo_ref, gu_ref, win_v, wout_v, sems):
        @pl.when(pl.program_id(0) == 0)
        def _():
            _load_ffn_weights(win_hbm, wout_hbm, win_v, wout_v, sems)

        xf = x_ref[...]
        xh, _ = _rms(xf, None)
        h = (xh * g_ref[...]).astype(BF16)
        acc = jnp.zeros((tm, D), F32)
        for blk in range(2):
            for lo, sz in _lane_chunks(FB):
                cols = pl.ds(blk * FB + lo, sz)
                gate = _dot(h, win_v[blk, :, pl.ds(lo, sz)])
                up = _dot(h, win_v[2 + blk, :, pl.ds(lo, sz)])
                gu_ref[0, :, cols] = gate.astype(BF16)
                gu_ref[1, :, cols] = up.astype(BF16)
                a = (gate * _fast_sigmoid(gate) * up).astype(BF16)
                acc = acc + _dot(a, wout_v[cols, :])
        xo_ref[...] = xf + 0.5 * acc

    row = pl.BlockSpec((tm, D), lambda i: (i, 0))
    return pl.pallas_call(
        body, name="ffn_fwd", grid=(T // tm,),
        in_specs=[row, pl.BlockSpec((1, D), lambda i: (0, 0)), ANY, ANY],
        out_specs=[row, pl.BlockSpec((2, tm, 2 * FB), lambda i: (0, i, 0))],
        out_shape=[jax.ShapeDtypeStruct((T, D), F32), jax.ShapeDtypeStruct((2, T, 2 * FB), BF16)],
        scratch_shapes=[pltpu.VMEM(win.shape, BF16), pltpu.VMEM(wout.shape, BF16), pltpu.SemaphoreType.DMA((6,))],
        compiler_params=_cp(1),
    )(x, g, win, wout)


def _mixproj_fwd(x, g, wt):
    T, D = x.shape
    W = wt.shape[0]
    QKV = ATTN_W + 2 * KV_W
    tm = min(TM_MIX, T)

    def body(x_ref, g_ref, w_ref, qkv_ref, u_ref):
        xh, _ = _rms(x_ref[...], None)
        h = (xh * g_ref[...]).astype(BF16)
        qkv_ref[...] = _dot_nt(h, w_ref[:QKV, :]).astype(BF16)
        u_ref[...] = _dot_nt(h, w_ref[QKV:, :])

    return pl.pallas_call(
        body, name="mixproj_fwd", grid=(T // tm,),
        in_specs=[pl.BlockSpec((tm, D), lambda i: (i, 0)), pl.BlockSpec((1, D), lambda i: (0, 0)),
                  pl.BlockSpec((W, D), lambda i: (0, 0))],
        out_specs=[pl.BlockSpec((tm, QKV), lambda i: (i, 0)), pl.BlockSpec((tm, W - QKV), lambda i: (i, 0))],
        out_shape=[jax.ShapeDtypeStruct((T, QKV), BF16), jax.ShapeDtypeStruct((T, W - QKV), F32)],
        compiler_params=_cp(1),
    )(x, g, wt)


def _attn_bias_table():
    rows, cols = GROUP * WINDOW, 2 * WINDOW
    row = lax.broadcasted_iota(jnp.int32, (N_KV, rows, cols), 1)
    col = lax.broadcasted_iota(jnp.int32, (N_KV, rows, cols), 2)
    head = GROUP * lax.broadcasted_iota(jnp.int32, (N_KV, rows, cols), 0) + (row >> 7)
    dist = (row & (WINDOW - 1)) + WINDOW - col
    slope = jnp.exp2(-(head + 1).astype(F32))
    return jnp.where((dist >= 0) & (dist < WINDOW), -slope * dist.astype(F32), NEG_INF)


def _first_block_mask(n):
    col = lax.broadcasted_iota(jnp.int32, (GROUP * WINDOW, 2 * WINDOW), 1)
    return (n > 0) | (col >= WINDOW)


def _sink_col(sink_ref, g):
    hi = lax.broadcasted_iota(jnp.int32, (GROUP * WINDOW, 1), 0) >> 7
    col = jnp.zeros((GROUP * WINDOW, 1), F32)
    for i in range(GROUP):
        col = jnp.where(hi == i, sink_ref[0, GROUP * g + i], col)
    return col


def _stack_heads(ref, g):
    return jnp.concatenate([ref[:, (GROUP * g + i) * HEAD_DIM:(GROUP * g + i + 1) * HEAD_DIM]
                            for i in range(GROUP)], axis=0)


def _band(kvp_ref, kvc_ref, off):
    return jnp.concatenate([kvp_ref[:, off:off + HEAD_DIM], kvc_ref[:, off:off + HEAD_DIM]], axis=0)


def _attn_probs(qs, k, bias, seen, sink):
    s = jnp.where(seen, _dot_nt(qs, k) * SCALE + bias, NEG_INF)
    m = jnp.maximum(jnp.max(s, axis=-1, keepdims=True), sink)
    p = jnp.exp(s - m)
    es = jnp.exp(sink - m)
    inv = 1.0 / (jnp.sum(p, axis=-1, keepdims=True) + es)
    return p * inv, es * inv


def _attn_fwd(sinks, tab, qkv):
    T = qkv.shape[0]
    nb = T // WINDOW

    def body(sink_ref, tab_ref, q_ref, kvp_ref, kvc_ref, o_ref):
        seen = _first_block_mask(pl.program_id(0))
        for g in range(N_KV):
            qs = _stack_heads(q_ref, g)
            k = _band(kvp_ref, kvc_ref, g * HEAD_DIM)
            v = _band(kvp_ref, kvc_ref, KV_W + g * HEAD_DIM)
            p, _ = _attn_probs(qs, k, tab_ref[g], seen, _sink_col(sink_ref, g))
            o = _dot(p.astype(BF16), v)
            for i in range(GROUP):
                h = GROUP * g + i
                o_ref[:, h * HEAD_DIM:(h + 1) * HEAD_DIM] = o[i * WINDOW:(i + 1) * WINDOW].astype(BF16)

    return pl.pallas_call(
        body, name="attn_fwd", grid=(nb,),
        in_specs=[pl.BlockSpec(memory_space=pltpu.SMEM),
                  pl.BlockSpec(tab.shape, lambda n: (0, 0, 0)),
                  pl.BlockSpec((WINDOW, ATTN_W), lambda n: (n, 0)),
                  pl.BlockSpec((WINDOW, 2 * KV_W), lambda n: (jnp.maximum(n - 1, 0), 2)),
                  pl.BlockSpec((WINDOW, 2 * KV_W), lambda n: (n, 2))],
        out_specs=pl.BlockSpec((WINDOW, ATTN_W), lambda n: (n, 0)),
        out_shape=jax.ShapeDtypeStruct((T, ATTN_W), BF16),
        compiler_params=_cp(1),
    )(sinks, tab, qkv, qkv, qkv)


def _shift_copies(src_ref, dst_ref, n):
    for b in range(1, 8):
        dst_ref[b - 1] = src_ref[b:b + n, :]


def _tap(src_ref, sh_ref, s, c0):
    a, b = divmod(s, 8)
    start = pl.multiple_of(c0 + 8 * a, 8)
    if b == 0:
        return src_ref[pl.ds(start, CONV_ROWS), :]
    return sh_ref[b - 1, pl.ds(start, CONV_ROWS), :]


def _glu_rows(u, ch):
    return u[:, :ch] * _fast_sigmoid(u[:, ch:])


def _fill_z(zs_ref, zsh_ref, uc_ref, up_ref, i, ch, n):
    zs_ref[0:HALO] = jnp.where(i > 0, _glu_rows(up_ref[...], ch), 0.0)
    zs_ref[HALO:] = _glu_rows(uc_ref[...], ch)
    _shift_copies(zs_ref, zsh_ref, n - 8)


def _conv_fwd(u, w, b, lg, lb):
    T = u.shape[0]
    CH = u.shape[1] // 2
    tm = min(TM, T)
    n = tm + HALO
    hb = tm // HALO

    def body(uc_ref, up_ref, w_ref, b_ref, lg_ref, lb_ref, conv_ref, ypre_ref, zs_ref, zsh_ref):
        i = pl.program_id(0)
        _fill_z(zs_ref, zsh_ref, uc_ref, up_ref, i, CH, n)
        bias = b_ref[...]

        def chunk(ci, carry):
            c0 = pl.multiple_of(ci * CONV_ROWS, CONV_ROWS)
            acc = jnp.broadcast_to(bias, (CONV_ROWS, CH))
            for k in range(CONV_W):
                acc = acc + w_ref[k:k + 1, :] * _tap(zs_ref, zsh_ref, HALO - (CONV_W - 1) + k, c0)
            ypre_ref[pl.ds(c0, CONV_ROWS), :] = acc
            return carry

        lax.fori_loop(0, tm // CONV_ROWS, chunk, 0)
        y = ypre_ref[...]
        mu = jnp.mean(y, axis=-1, keepdims=True)
        d = y - mu
        var = jnp.mean(d * d, axis=-1, keepdims=True)
        o = d * lax.rsqrt(var + EPS) * lg_ref[...] + lb_ref[...]
        conv_ref[...] = (o * _fast_sigmoid(o)).astype(BF16)

    vec = pl.BlockSpec((1, CH), lambda i: (0, 0))
    return pl.pallas_call(
        body, name="conv_fwd", grid=(T // tm,),
        in_specs=[pl.BlockSpec((tm, 2 * CH), lambda i: (i, 0)),
                  pl.BlockSpec((HALO, 2 * CH), lambda i: (jnp.maximum(i * hb - 1, 0), 0)),
                  pl.BlockSpec((CONV_W, CH), lambda i: (0, 0)), vec, vec, vec],
        out_specs=[pl.BlockSpec((tm, CH), lambda i: (i, 0)), pl.BlockSpec((tm, CH), lambda i: (i, 0))],
        out_shape=[jax.ShapeDtypeStruct((T, CH), BF16), jax.ShapeDtypeStruct((T, CH), F32)],
        scratch_shapes=[pltpu.VMEM((n, CH), F32), pltpu.VMEM((7, n - 8, CH), F32)],
        compiler_params=_cp(1),
    )(u, u, w, b, lg, lb)


def _mixout_fwd(x, attn, conv, wo):
    T, D = x.shape
    tm = min(TM_MIX, T)
    A = attn.shape[1]

    def body(x_ref, a_ref, c_ref, w_ref, xo_ref):
        xo_ref[...] = x_ref[...] + _dot(a_ref[...], w_ref[:A, :]) + _dot(c_ref[...], w_ref[A:, :])

    return pl.pallas_call(
        body, name="mixout_fwd", grid=(T // tm,),
        in_specs=[pl.BlockSpec((tm, D), lambda i: (i, 0)), pl.BlockSpec((tm, A), lambda i: (i, 0)),
                  pl.BlockSpec((tm, conv.shape[1]), lambda i: (i, 0)), pl.BlockSpec(wo.shape, lambda i: (0, 0))],
        out_specs=pl.BlockSpec((tm, D), lambda i: (i, 0)),
        out_shape=jax.ShapeDtypeStruct((T, D), F32),
        compiler_params=_cp(1),
    )(x, attn, conv, wo)


def _rms_bwd_rows(dh, xf, g):
    xh, r = _rms(xf, None)
    dxn = dh * g
    dx = r * (dxn - xh * jnp.mean(dxn * xh, axis=-1, keepdims=True))
    return dx, jnp.sum(dh * xh, axis=0, keepdims=True), xh * g


def _loss_head(x, g, tgt):
    T, D = x.shape
    tm = min(TM, T)

    def body(x_ref, g_ref, t_ref, loss_ref, dx_ref, dg_ref):
        @pl.when(pl.program_id(0) == 0)
        def _():
            loss_ref[...] = jnp.zeros_like(loss_ref)
            dg_ref[...] = jnp.zeros_like(dg_ref)

        xf = x_ref[...]
        g = g_ref[...]
        xh, _ = _rms(xf, None)
        e = xh * g - t_ref[...]
        loss_ref[...] += 0.5 * jnp.sum(jnp.mean(e * e, axis=-1, keepdims=True), axis=0, keepdims=True)
        dx, dg, _ = _rms_bwd_rows(e * (1.0 / D), xf, g)
        dx_ref[...] = dx
        dg_ref[...] += dg

    return pl.pallas_call(
        body, name="loss_head", grid=(T // tm,),
        in_specs=[pl.BlockSpec((tm, D), lambda i: (i, 0)), pl.BlockSpec((1, D), lambda i: (0, 0)),
                  pl.BlockSpec((tm, D), lambda i: (i, 0))],
        out_specs=[pl.BlockSpec((1, 1), lambda i: (0, 0)), pl.BlockSpec((tm, D), lambda i: (i, 0)),
                   pl.BlockSpec((1, D), lambda i: (0, 0))],
        out_shape=[jax.ShapeDtypeStruct((1, 1), F32), jax.ShapeDtypeStruct((T, D), F32),
                   jax.ShapeDtypeStruct((1, D), F32)],
        compiler_params=_cp(1),
    )(x, g, tgt)


def _ffn_bwd(dxo, x, g, gu, win, wout, dep):
    T, D = x.shape
    FB = win.shape[2]
    tm = min(TM_FFN_BWD, T)

    def body(dxo_ref, x_ref, g_ref, gu_ref, win_hbm, wout_hbm, dep_ref,
             dxi_ref, dg_ref, hb_ref, dgu_ref, a_ref, dyb_ref, win_v, wout_v, sems):
        @pl.when(pl.program_id(0) == 0)
        def _():
            _load_ffn_weights(win_hbm, wout_hbm, win_v, wout_v, sems)
            dg_ref[...] = jnp.zeros_like(dg_ref)

        dyb = (0.5 * dxo_ref[...]).astype(BF16)
        dyb_ref[...] = dyb
        dh = jnp.zeros((tm, D), F32)
        for blk in range(2):
            cols = pl.ds(blk * FB, FB)
            da = _dot_nt(dyb, wout_v[cols, :])
            gate = gu_ref[0, :, cols].astype(F32)
            up = gu_ref[1, :, cols].astype(F32)
            sg = _fast_sigmoid(gate)
            s = gate * sg
            a_ref[:, cols] = (s * up).astype(BF16)
            dgate = (da * up * (sg + s * (1.0 - sg))).astype(BF16)
            dup = (da * s).astype(BF16)
            dgu_ref[0, :, cols] = dgate
            dgu_ref[1, :, cols] = dup
            dh = dh + _dot_nt(dgate, win_v[blk]) + _dot_nt(dup, win_v[2 + blk])
        dx, dg, h = _rms_bwd_rows(dh, x_ref[...], g_ref[...])
        dxi_ref[...] = dxo_ref[...] + dx
        dg_ref[...] += dg
        hb_ref[...] = h.astype(BF16)

    row = pl.BlockSpec((tm, D), lambda i: (i, 0))
    act = pl.BlockSpec((2, tm, 2 * FB), lambda i: (0, i, 0))
    return pl.pallas_call(
        body, name="ffn_bwd", grid=(T // tm,),
        in_specs=[row, row, pl.BlockSpec((1, D), lambda i: (0, 0)), act, ANY, ANY, ANY],
        out_specs=[row, pl.BlockSpec((1, D), lambda i: (0, 0)), row, act,
                   pl.BlockSpec((tm, 2 * FB), lambda i: (i, 0)), row],
        out_shape=[jax.ShapeDtypeStruct((T, D), F32), jax.ShapeDtypeStruct((1, D), F32),
                   jax.ShapeDtypeStruct((T, D), BF16), jax.ShapeDtypeStruct((2, T, 2 * FB), BF16),
                   jax.ShapeDtypeStruct((T, 2 * FB), BF16), jax.ShapeDtypeStruct((T, D), BF16)],
        scratch_shapes=[pltpu.VMEM(win.shape, BF16), pltpu.VMEM(wout.shape, BF16), pltpu.SemaphoreType.DMA((6,))],
        compiler_params=_cp(1),
    )(dxo, x, g, gu, win, wout, dep)


def _mix_rms_bwd(dxo, x, g, dzs, wt):
    T, D = x.shape
    tm = min(TM, T)
    npair = len(dzs)

    def body(*refs):
        dxo_ref, x_ref, g_ref = refs[:3]
        dz_refs, w_ref = refs[3:3 + npair], refs[3 + npair]
        dxi_ref, dg_ref, hb_ref = refs[4 + npair:]

        @pl.when(pl.program_id(0) == 0)
        def _():
            dg_ref[...] = jnp.zeros_like(dg_ref)

        dh = jnp.zeros((tm, D), F32)
        k0 = 0
        for dz_ref in dz_refs:
            kp = dz_ref.shape[1]
            dh = dh + _dot(dz_ref[...], w_ref[k0:k0 + kp, :])
            k0 += kp
        dx, dg, h = _rms_bwd_rows(dh, x_ref[...], g_ref[...])
        dxi_ref[...] = dxo_ref[...] + dx
        dg_ref[...] += dg
        hb_ref[...] = h.astype(BF16)

    row = pl.BlockSpec((tm, D), lambda i: (i, 0))
    return pl.pallas_call(
        body, name="mix_rms_bwd", grid=(T // tm,),
        in_specs=[row, row, pl.BlockSpec((1, D), lambda i: (0, 0))]
                 + [pl.BlockSpec((tm, dz.shape[1]), lambda i: (i, 0)) for dz in dzs]
                 + [pl.BlockSpec(wt.shape, lambda i: (0, 0))],
        out_specs=[row, pl.BlockSpec((1, D), lambda i: (0, 0)), row],
        out_shape=[jax.ShapeDtypeStruct((T, D), F32), jax.ShapeDtypeStruct((1, D), F32),
                   jax.ShapeDtypeStruct((T, D), BF16)],
        compiler_params=_cp(1),
    )(dxo, x, g, *dzs, wt)


def _wgrad(name, a, b, a_spec, b_spec, out_shape, out_spec, nblk, dep, acc_shape):
    T = a.shape[0]
    tk = min(TK_WGRAD, T)
    nk = T // tk

    def body(a_ref, b_ref, dep_ref, o_ref, acc_ref):
        k = pl.program_id(1)

        @pl.when(k == 0)
        def _():
            acc_ref[...] = jnp.zeros_like(acc_ref)

        acc_ref[...] += _dot_tn(a_ref[...], b_ref[...])

        @pl.when(k == nk - 1)
        def _():
            o_ref[...] = acc_ref[...].reshape(o_ref.shape).astype(BF16)

    return pl.pallas_call(
        body, name=name, grid=(nblk, nk), in_specs=[a_spec, b_spec, ANY], out_specs=out_spec,
        out_shape=jax.ShapeDtypeStruct(out_shape, BF16), scratch_shapes=[pltpu.VMEM(acc_shape, F32)],
        compiler_params=_cp(2),
    )(a, b, dep)


def _wgrad_ffn_in(hb, dgu, dep):
    T, D = hb.shape
    FB = dgu.shape[2] // 2
    tk = min(TK_WGRAD, T)
    return _wgrad("wgrad_ffn_in", hb, dgu,
                  pl.BlockSpec((tk, D), lambda b, k: (k, 0)),
                  pl.BlockSpec((None, tk, FB), lambda b, k: (b // 2, k, b % 2)),
                  (4, D, FB), pl.BlockSpec((None, D, FB), lambda b, k: (b, 0, 0)), 4, dep, (D, FB))


def _wgrad_ffn_out(a, dyb, dep):
    T, D = dyb.shape
    FB = a.shape[1] // 2
    tk = min(TK_WGRAD, T)
    return _wgrad("wgrad_ffn_out", a, dyb,
                  pl.BlockSpec((tk, FB), lambda b, k: (k, b)),
                  pl.BlockSpec((tk, D), lambda b, k: (k, 0)),
                  (4, FB // 2, D), pl.BlockSpec((2, FB // 2, D), lambda b, k: (b, 0, 0)), 2, dep, (FB, D))


def _wgrad_cat(a_list, b_list):
    T = a_list[0].shape[0]
    tk = min(TK_WGRAD, T)
    nk = T // tk
    na = len(a_list)
    M, N = sum(a.shape[1] for a in a_list), sum(b.shape[1] for b in b_list)

    def body(*refs):
        a_refs, b_refs, o_ref, acc_ref = refs[:na], refs[na:-2], refs[-2], refs[-1]
        k = pl.program_id(0)

        @pl.when(k == 0)
        def _():
            acc_ref[...] = jnp.zeros_like(acc_ref)

        r0 = 0
        for a_ref in a_refs:
            c0 = 0
            for b_ref in b_refs:
                m, n = a_ref.shape[1], b_ref.shape[1]
                acc_ref[r0:r0 + m, c0:c0 + n] += _dot_tn(a_ref[...], b_ref[...])
                c0 += n
            r0 += a_ref.shape[1]

        @pl.when(k == nk - 1)
        def _():
            o_ref[...] = acc_ref[...].astype(BF16)

    return pl.pallas_call(
        body, name="wgrad_cat", grid=(nk,),
        in_specs=[pl.BlockSpec((tk, v.shape[1]), lambda k: (k, 0)) for v in list(a_list) + list(b_list)],
        out_specs=pl.BlockSpec((M, N), lambda k: (0, 0)),
        out_shape=jax.ShapeDtypeStruct((M, N), BF16), scratch_shapes=[pltpu.VMEM((M, N), F32)],
        compiler_params=_cp(1),
    )(*a_list, *b_list)


def _mixout_bwd(dxo, wo):
    T, D = dxo.shape
    tm = min(TM_MIX, T)
    A = ATTN_W
    C = wo.shape[0] - A

    def body(dxo_ref, w_ref, dyb_ref, da_ref, dc_ref):
        dyb = dxo_ref[...].astype(BF16)
        dyb_ref[...] = dyb
        da_ref[...] = _dot_nt(dyb, w_ref[:A, :]).astype(BF16)
        dc_ref[...] = _dot_nt(dyb, w_ref[A:, :])

    return pl.pallas_call(
        body, name="mixout_bwd", grid=(T // tm,),
        in_specs=[pl.BlockSpec((tm, D), lambda i: (i, 0)), pl.BlockSpec(wo.shape, lambda i: (0, 0))],
        out_specs=[pl.BlockSpec((tm, D), lambda i: (i, 0)), pl.BlockSpec((tm, A), lambda i: (i, 0)),
                   pl.BlockSpec((tm, C), lambda i: (i, 0))],
        out_shape=[jax.ShapeDtypeStruct((T, D), BF16), jax.ShapeDtypeStruct((T, A), BF16),
                   jax.ShapeDtypeStruct((T, C), F32)],
        compiler_params=_cp(1),
    )(dxo, wo)


def _conv_bwd(dconv, ypre, u, w, lg, lb):
    T, CH = dconv.shape
    tm = min(TM, T)
    n = tm + HALO
    hb = tm // HALO
    nt = T // tm
    nchunk = tm // CONV_ROWS

    def body(dc_ref, dcn_ref, yp_ref, ypn_ref, uc_ref, up_ref, w_ref, lg_ref, lb_ref,
             du_ref, dw_ref, dvec_ref, zs_ref, zsh_ref, dy_ref, dysh_ref, dz_ref, dwacc_ref):
        i = pl.program_id(0)

        @pl.when(i == 0)
        def _():
            dwacc_ref[...] = jnp.zeros_like(dwacc_ref)
            dvec_ref[...] = jnp.zeros_like(dvec_ref)

        g, bb = lg_ref[...], lb_ref[...]

        def ln_bwd(dc, yp):
            mu = jnp.mean(yp, axis=-1, keepdims=True)
            d = yp - mu
            rs = lax.rsqrt(jnp.mean(d * d, axis=-1, keepdims=True) + EPS)
            yn = d * rs
            o = yn * g + bb
            sg = _fast_sigmoid(o)
            do = dc * (sg * (1.0 + o * (1.0 - sg)))
            dyn = do * g
            dyp = rs * (dyn - jnp.mean(dyn, axis=-1, keepdims=True)
                        - yn * jnp.mean(dyn * yn, axis=-1, keepdims=True))
            return dyp, do, yn

        dyp, do, yn = ln_bwd(dc_ref[...], yp_ref[...])
        dvec_ref[0:1, :] += jnp.sum(dyp, axis=0, keepdims=True)
        dvec_ref[1:2, :] += jnp.sum(do * yn, axis=0, keepdims=True)
        dvec_ref[2:3, :] += jnp.sum(do, axis=0, keepdims=True)
        dy_ref[0:tm] = dyp
        dyh, _, _ = ln_bwd(dcn_ref[...], ypn_ref[...])
        dy_ref[tm:] = jnp.where(i < nt - 1, dyh, 0.0)
        _shift_copies(dy_ref, dysh_ref, n - 8)
        _fill_z(zs_ref, zsh_ref, uc_ref, up_ref, i, CH, n)

        def chunk(ci, carry):
            c0 = pl.multiple_of(ci * CONV_ROWS, CONV_ROWS)
            acc = jnp.zeros((CONV_ROWS, CH), F32)
            for k in range(CONV_W):
                acc = acc + w_ref[k:k + 1, :] * _tap(dy_ref, dysh_ref, CONV_W - 1 - k, c0)
            dz_ref[pl.ds(c0, CONV_ROWS), :] = acc
            dyc = dy_ref[pl.ds(c0, CONV_ROWS), :]
            for k in range(CONV_W):
                prod = dyc * _tap(zs_ref, zsh_ref, HALO - (CONV_W - 1) + k, c0)
                dwacc_ref[k] += jnp.sum(prod.reshape(CONV_ROWS // 8, 8, CH), axis=0)
            return carry

        lax.fori_loop(0, nchunk, chunk, 0)

        @pl.when(i == nt - 1)
        def _():
            dw_ref[...] = jnp.sum(dwacc_ref[...], axis=1)

        uc = uc_ref[...]
        a = uc[:, :CH]
        sg = _fast_sigmoid(uc[:, CH:])
        dz = dz_ref[...]
        du_ref[:, :CH] = (dz * sg).astype(BF16)
        du_ref[:, CH:] = (dz * a * sg * (1.0 - sg)).astype(BF16)

    cur = lambda c: pl.BlockSpec((tm, c), lambda i: (i, 0))
    nxt = lambda c: pl.BlockSpec((HALO, c), lambda i: (jnp.minimum((i + 1) * hb, T // HALO - 1), 0))
    vec = pl.BlockSpec((1, CH), lambda i: (0, 0))
    return pl.pallas_call(
        body, name="conv_bwd", grid=(nt,),
        in_specs=[cur(CH), nxt(CH), cur(CH), nxt(CH), cur(2 * CH),
                  pl.BlockSpec((HALO, 2 * CH), lambda i: (jnp.maximum(i * hb - 1, 0), 0)),
                  pl.BlockSpec((CONV_W, CH), lambda i: (0, 0)), vec, vec],
        out_specs=[pl.BlockSpec((tm, 2 * CH), lambda i: (i, 0)), pl.BlockSpec((32, CH), lambda i: (0, 0)),
                   pl.BlockSpec((8, CH), lambda i: (0, 0))],
        out_shape=[jax.ShapeDtypeStruct((T, 2 * CH), BF16), jax.ShapeDtypeStruct((32, CH), F32),
                   jax.ShapeDtypeStruct((8, CH), F32)],
        scratch_shapes=[pltpu.VMEM((n, CH), F32), pltpu.VMEM((7, n - 8, CH), F32),
                        pltpu.VMEM((n, CH), F32), pltpu.VMEM((7, n - 8, CH), F32), pltpu.VMEM((tm, CH), F32),
                        pltpu.VMEM((32, 8, CH), F32)],
        compiler_params=_cp(1),
    )(dconv, dconv, ypre, ypre, u, u, w, lg, lb)


def _attn_bwd(sinks, tab, qkv, dattn):
    T = qkv.shape[0]
    nb = T // WINDOW

    def body(sink_ref, tab_ref, q_ref, kvp_ref, kvc_ref, do_ref, dq_ref, dkv_ref, dsk_ref, carry_ref):
        n = pl.program_id(0)

        @pl.when(n == 0)
        def _():
            dsk_ref[...] = jnp.zeros_like(dsk_ref)
            carry_ref[...] = jnp.zeros_like(carry_ref)

        @pl.when(n < nb)
        def _():
            seen = _first_block_mask(n)
            for g in range(N_KV):
                qs = _stack_heads(q_ref, g)
                dos = _stack_heads(do_ref, g)
                k = _band(kvp_ref, kvc_ref, g * HEAD_DIM)
                v = _band(kvp_ref, kvc_ref, KV_W + g * HEAD_DIM)
                p, ps = _attn_probs(qs, k, tab_ref[g], seen, _sink_col(sink_ref, g))
                dp = _dot_nt(dos, v)
                delta = jnp.sum(p * dp, axis=-1, keepdims=True)
                dsb = (p * (dp - delta)).astype(BF16)
                dsink = -ps * delta
                dqs = _dot(dsb, k) * SCALE
                dk = _dot_tn(dsb, qs) * SCALE
                dv = _dot_tn(p.astype(BF16), dos)
                for i in range(GROUP):
                    h = GROUP * g + i
                    dq_ref[:, h * HEAD_DIM:(h + 1) * HEAD_DIM] = dqs[i * WINDOW:(i + 1) * WINDOW].astype(BF16)
                    dsk_ref[h:h + 1, :] += jnp.sum(dsink[i * WINDOW:(i + 1) * WINDOW], axis=0, keepdims=True)
                for off, d in ((g * HEAD_DIM, dk), (KV_W + g * HEAD_DIM, dv)):
                    dkv_ref[:, off:off + HEAD_DIM] = (carry_ref[:, off:off + HEAD_DIM] + d[:WINDOW]).astype(BF16)
                    carry_ref[:, off:off + HEAD_DIM] = d[WINDOW:]

        @pl.when(n == nb)
        def _():
            dkv_ref[...] = carry_ref[...].astype(BF16)

    last = nb - 1
    return pl.pallas_call(
        body, name="attn_bwd", grid=(nb + 1,),
        in_specs=[pl.BlockSpec(memory_space=pltpu.SMEM),
                  pl.BlockSpec(tab.shape, lambda n: (0, 0, 0)),
                  pl.BlockSpec((WINDOW, ATTN_W), lambda n: (jnp.minimum(n, last), 0)),
                  pl.BlockSpec((WINDOW, 2 * KV_W), lambda n: (jnp.clip(n - 1, 0, last), 2)),
                  pl.BlockSpec((WINDOW, 2 * KV_W), lambda n: (jnp.minimum(n, last), 2)),
                  pl.BlockSpec((WINDOW, ATTN_W), lambda n: (jnp.minimum(n, last), 0))],
        out_specs=[pl.BlockSpec((WINDOW, ATTN_W), lambda n: (jnp.minimum(n, last), 0)),
                   pl.BlockSpec((WINDOW, 2 * KV_W), lambda n: (jnp.maximum(n - 1, 0), 0)),
                   pl.BlockSpec((8, LANES), lambda n: (0, 0))],
        out_shape=[jax.ShapeDtypeStruct((T, ATTN_W), BF16), jax.ShapeDtypeStruct((T, 2 * KV_W), BF16),
                   jax.ShapeDtypeStruct((8, LANES), F32)],
        scratch_shapes=[pltpu.VMEM((WINDOW, 2 * KV_W), F32)],
        compiler_params=_cp(1),
    )(sinks, tab, qkv, qkv, qkv, dattn)


def _pack(arrs):
    flat = jnp.concatenate([a.reshape(-1) for a in arrs])
    pad = -flat.shape[0] % (8 * LANES)
    return jnp.pad(flat, (0, pad)).reshape(1, -1, LANES)


def _unpack(packed, like):
    flat = packed.reshape(-1)
    out, off = [], 0
    for a in like:
        out.append(flat[off:off + a.size].reshape(a.shape))
        off += a.size
    return out


def kernel(x, norm_ffn1, w_ffn1_in, w_ffn1_out, norm_mix, w_in, sinks, w_dw, b_dw, conv_ln_g, conv_ln_b, w_out, norm_ffn2, w_ffn2_in, w_ffn2_out, final_norm, loss_target, m_norm_ffn1, m_w_ffn1_in, m_w_ffn1_out, m_norm_mix, m_w_in, m_sinks, m_w_dw, m_b_dw, m_conv_ln_g, m_conv_ln_b, m_w_out, m_norm_ffn2, m_w_ffn2_in, m_w_ffn2_out, m_final_norm, v_norm_ffn1, v_w_ffn1_in, v_w_ffn1_out, v_norm_mix, v_w_in, v_sinks, v_w_dw, v_b_dw, v_conv_ln_g, v_conv_ln_b, v_w_out, v_norm_ffn2, v_w_ffn2_in, v_w_ffn2_out, v_final_norm):
    L, D = norm_ffn1.shape
    T = x.shape[1]
    FB = w_ffn1_in.shape[2]
    CH = b_dw.shape[1]
    QKV = ATTN_W + 2 * KV_W
    xs = x.reshape(T, D)
    tgt = loss_target.reshape(T, D)
    cx, cy, cc = lax.axis_index("x"), lax.axis_index("y"), lax.axis_index("c")
    chip = 2 * cx + cy
    cidx = cc.reshape(1).astype(jnp.int32)
    tr = lambda a_: jnp.transpose(a_, (0, 2, 1))
    big_w = (w_ffn1_in, w_ffn1_out, tr(w_in), w_out, w_ffn2_in, w_ffn2_out)
    big_m = (m_w_ffn1_in, m_w_ffn1_out, tr(m_w_in), m_w_out, m_w_ffn2_in, m_w_ffn2_out)
    big_v = (v_w_ffn1_in, v_w_ffn1_out, tr(v_w_in), v_w_out, v_w_ffn2_in, v_w_ffn2_out)
    NW = len(big_w) + 1

    def own_slot(a, slots=4, idx=chip):
        return lax.dynamic_update_index_in_dim(lax.empty((slots,) + a.shape, a.dtype), a, idx, 0)

    def shards(l, tok):
        return [own_slot((w_[l] + tok[0, 0]).astype(BF16)) for w_ in big_w] + [own_slot(w_dw[l] + tok[0, 0])]

    def gather_start(lands, tok):
        return _xchg_start("gather_start", [], lands, _gather_plan, tok)

    def gather_arrived(started, after, n, taps):
        _, lands, tok = _xchg_wait("gather_wait", started, 0, n, _gather_plan, after)
        return _xchg_start("gshare_start", [], lands[:-1] if taps else lands, _gshare_plan, tok, "sibling3"), lands[-1]

    def shared_weights(shared, after, n):
        _, mats, tok = _xchg_wait("gshare_wait", shared, 0, n, _gshare_plan, after, "sibling3")
        return mats, tok

    row = lambda a, l: a[l].reshape(1, -1)
    tab = _attn_bias_table()
    NB = len(big_w)

    saved, W = [], []
    zero_tok = jnp.zeros((8, LANES), F32)
    src0 = shards(0, zero_tok)
    started = gather_start(src0[:2], zero_tok)
    rest0 = gather_start(src0[2:], started[-1])
    cast = [None] + [shards(l, rest0[-1]) for l in range(1, L)]
    shared, _ = gather_arrived(started, [xs] + [a_ for c_ in cast[1:] for a_ in c_], 2, False)
    after = [shared[-1]]
    for l in range(L):
        mats, tok = shared_weights(shared, after, 2 if l == 0 else NB)
        started = None
        if l + 1 < L:
            started = gather_start(cast[l + 1], tok)
            tok = started[-1]
        x0 = xs
        x1, gu1 = _ffn_fwd(x0, row(norm_ffn1, l) + tok[0, 0], mats[0], mats[1].reshape(2 * FB, D))
        gm_row = row(norm_mix, l)
        if l == 0:
            shared, gdw = gather_arrived(rest0, [x1], NW - 2, True)
            rest, tok = shared_weights(shared, [shared[-1]], NB - 2)
            mats = list(mats) + list(rest)
            gm_row = gm_row + tok[0, 0]
        g1i, g1o, gi, go, g2i, g2o = mats
        w = dict(f1i=g1i, f1o=g1o.reshape(2 * FB, D), f2i=g2i, f2o=g2o.reshape(2 * FB, D),
                 wit=gi.reshape(-1, D), wo=go.reshape(-1, D),
                 wdw=jnp.transpose(gdw, (1, 0, 2)).reshape(CONV_W, CH))
        W.append(w)
        qkv, u = _mixproj_fwd(x1, gm_row, w["wit"])
        attn = _attn_fwd(row(sinks, l), tab, qkv)
        conv, ypre = _conv_fwd(u, w["wdw"], row(b_dw, l), row(conv_ln_g, l), row(conv_ln_b, l))
        x2 = _mixout_fwd(x1, attn, conv, w["wo"])
        g2_row = row(norm_ffn2, l)
        if started is not None and l > 0:
            shared, gdw = gather_arrived(started, [x2], NW, True)
            g2_row = g2_row + shared[-1][0, 0]
        xs, gu2 = _ffn_fwd(x2, g2_row, w["f2i"], w["f2o"])
        if started is not None and l == 0:
            shared, gdw = gather_arrived(started, [xs], NW, True)
        saved.append((x0, gu1, x1, qkv, u, attn, conv, ypre, x2, gu2))
        after = [xs]

    loss_part, dx, d_final = _loss_head(xs, final_norm.reshape(1, D), tgt)
    loss = lax.psum(loss_part[0, 0], ("x", "y", "c"))

    bufs = [[lax.empty(w_.shape, F32) for _ in range(4)] for w_ in big_w]
    d_n1, d_nm, d_n2 = [None] * L, [None] * L, [None] * L
    d_sk, d_bdw, d_lg, d_lb, d_wdw = [None] * L, [None] * L, [None] * L, [None] * L, [None] * L

    me_idx = 4 * cx + 2 * cy + cc

    def reduce_start(gs):
        lands = []
        for g in gs:
            h = g.shape[1] // 2
            mine = lax.dynamic_slice(g, (chip, cc * h, 0), (1, h, g.shape[2]))[0]
            lands.append(own_slot(mine, 8, me_idx))
        return _xchg_start("rs_start", gs, lands, _rs_plan, zero_tok, "all")

    def share_start(rs_started, after, n):
        _, qs, tok = _xchg_wait("rs_wait", rs_started, n, n, _rs_plan, after, "all")
        return _xchg_start("qshare_start", qs, [lax.empty(q.shape, q.dtype) for q in qs], _whole_plan, tok, "sibling")

    def finish(l, shared, after, idxs):
        q_own, q_sib, _ = _xchg_wait("qshare_wait", shared, len(idxs), len(idxs), _whole_plan, after, "sibling")
        for k, t in enumerate(idxs):
            bufs[t] = _adamw_layer(cidx, q_own[k], q_sib[k], big_w[t], big_m[t], big_v[t], bufs[t], l)

    ALL = list(range(NB))
    EARLY, LATE = ALL[2:], ALL[:2]
    rs_list, shares = [], []
    tok = zero_tok
    for l in reversed(range(L)):
        w = W[l]
        x0, gu1, x1, qkv, u, attn, conv, ypre, x2, gu2 = saved[l]
        dx, d_n2[l], hb, dgu, a, dyb = _ffn_bwd(dx, x2, row(norm_ffn2, l), gu2, w["f2i"], w["f2o"], tok)
        g_f2i, g_f2o = _wgrad_ffn_in(hb, dgu, tok), _wgrad_ffn_out(a, dyb, tok)
        lg_row = row(conv_ln_g, l)
        if len(rs_list) >= 2:
            pl_, st_ = rs_list[-2]
            shares.append((pl_, share_start(st_, [g_f2o], NB)))
            lg_row = lg_row + shares[-1][1][-1][0, 0]
        dyb, dattn, dconv = _mixout_bwd(dx, w["wo"])
        g_wo = _wgrad_cat([attn, conv], [dyb]).reshape(4, -1, D)
        du, dwdw, dvec = _conv_bwd(dconv, ypre, u, w["wdw"], lg_row, row(conv_ln_b, l))
        d_wdw[l], d_bdw[l], d_lg[l], d_lb[l] = dwdw[:CONV_W], dvec[0], dvec[1], dvec[2]
        dq, dkv, dsk = _attn_bwd(row(sinks, l), tab, qkv, dattn)
        d_sk[l] = dsk[:, 0]
        dx, d_nm[l], hb = _mix_rms_bwd(dx, x1, row(norm_mix, l), [dq, dkv, du], w["wit"])
        g_wi = _wgrad_cat([dq, dkv, du], [hb]).reshape(4, -1, D)
        if l == 0:
            rs_early = reduce_start([g_wi, g_wo, g_f2i, g_f2o])
            tok = rs_early[-1]
        dx, d_n1[l], hb, dgu, a, dyb = _ffn_bwd(dx, x0, row(norm_ffn1, l), gu1, w["f1i"], w["f1o"], tok)
        g_f1i, g_f1o = _wgrad_ffn_in(hb, dgu, tok), _wgrad_ffn_out(a, dyb, tok)
        rs_started = reduce_start([g_f1i, g_f1o] if l == 0 else [g_f1i, g_f1o, g_wi, g_wo, g_f2i, g_f2o])
        tok = rs_started[-1]
        rs_list.append((l, rs_started))
    grad_x = dx.reshape(x.shape)

    small_g = [jnp.concatenate(d, axis=0) for d in (d_n1, d_nm, d_n2)] + [d_final, jnp.stack(d_sk)] + \
              [jnp.stack(d) for d in (d_bdw, d_lg, d_lb, d_wdw)]
    packed = _pack(small_g)[0]
    small_started = _xchg_start("small_start", [packed], [own_slot(packed, 8, 4 * cx + 2 * cy + cc)], _slot_plan, tok, "all")

    rs_late = rs_list.pop()[1]
    after = [small_started[-1]]
    if len(rs_list) > len(shares):
        pl_, st_ = rs_list[len(shares)]
        shares.append((pl_, share_start(st_, after, NB)))
        after = [shares[-1][1][-1]]
    if shares:
        finish(*shares.pop(0), after, ALL)
        after = [b_[0] for b_ in bufs]
    sh_early = share_start(rs_early, after, len(EARLY))
    after = [sh_early[-1]]
    sh_late = None
    for l, sh in shares:
        finish(l, sh, after, ALL)
        after = [b_[0] for b_ in bufs]
        if sh_late is None:
            sh_late = share_start(rs_late, after, len(LATE))
            after = [sh_late[-1]]
    if sh_late is None:
        sh_late = share_start(rs_late, after, len(LATE))
        after = [sh_late[-1]]
    _, (slots,), _ = _xchg_wait("small_wait", small_started, 1, 1, _slot_plan, after, "all")
    small_sum = _unpack(_sum_slots(slots), small_g)
    g_wdw = lax.dynamic_slice_in_dim(small_sum[8], chip * w_dw.shape[2], w_dw.shape[2], axis=2)
    small_g = [small_sum[0], small_sum[1], small_sum[2], small_sum[3].reshape(D), small_sum[4],
               small_sum[5], small_sum[6], small_sum[7], g_wdw]
    small_w = (norm_ffn1, norm_mix, norm_ffn2, final_norm, sinks, b_dw, conv_ln_g, conv_ln_b, w_dw)
    small_m = (m_norm_ffn1, m_norm_mix, m_norm_ffn2, m_final_norm, m_sinks, m_b_dw, m_conv_ln_g, m_conv_ln_b, m_w_dw)
    small_v = (v_norm_ffn1, v_norm_mix, v_norm_ffn2, v_final_norm, v_sinks, v_b_dw, v_conv_ln_g, v_conv_ln_b, v_w_dw)
    upd = _adamw(_pack(small_g), _pack(small_w), _pack(small_m), _pack(small_v))
    small_upd = [_unpack(u_, small_w) for u_ in upd]
    finish(0, sh_early, [upd[0]], EARLY)
    finish(0, sh_late, [bufs[t][0] for t in EARLY], LATE)

    order = ("norm_ffn1", "w_ffn1_in", "w_ffn1_out", "norm_mix", "w_in", "sinks", "w_dw", "b_dw", "conv_ln_g",
             "conv_ln_b", "w_out", "norm_ffn2", "w_ffn2_in", "w_ffn2_out", "final_norm")
    small_names = ("norm_ffn1", "norm_mix", "norm_ffn2", "final_norm", "sinks", "b_dw", "conv_ln_g", "conv_ln_b", "w_dw")
    big_names = ("w_ffn1_in", "w_ffn1_out", "w_in", "w_out", "w_ffn2_in", "w_ffn2_out")
    grads, deltas, new_m, new_v = {}, {}, {}, {}
    for i, nme in enumerate(small_names):
        grads[nme], deltas[nme], new_m[nme], new_v[nme] = small_g[i], small_upd[0][i], small_upd[1][i], small_upd[2][i]
    for i, nme in enumerate(big_names):
        grads[nme], deltas[nme], new_m[nme], new_v[nme] = [tr(b_) for b_ in bufs[i]] if nme == "w_in" else bufs[i]
    return (loss, grad_x, *[grads[n] for n in order], *[deltas[n] for n in order],
            *[new_m[n] for n in order], *[new_v[n] for n in order])
```

```python
import jax
import jax.numpy as jnp
from jax import lax
from jax.experimental import pallas as pl
from jax.experimental.pallas import tpu as pltpu

F32, BF16 = jnp.float32, jnp.bfloat16
EPS = 1e-6
NEG_INF = -1e30
HEAD_DIM = 64
N_HEADS = 8
N_KV = 2
GROUP = N_HEADS // N_KV
WINDOW = 128
ATTN_W = N_HEADS * HEAD_DIM
KV_W = N_KV * HEAD_DIM
CONV_W = 31
HALO = 32
CONV_ROWS = 32
SCALE = 1.0 / 8.0
ADAM_LR, ADAM_B1, ADAM_B2, ADAM_EPS, ADAM_WD, ADAM_STEP = 0.001, 0.9, 0.999, 1e-08, 0.01, 10
TM = 512
TM_FFN_BWD = 256
TK_WGRAD = 2048
TM_MIX = 1024
LANES = 128
VMEM_LIMIT = 52 * 1024 * 1024
MESH = pl.DeviceIdType.MESH
ANY = pl.BlockSpec(memory_space=pl.ANY)
HBM = pl.BlockSpec(memory_space=pltpu.HBM)
SEM = pl.BlockSpec(memory_space=pltpu.SEMAPHORE)
VMEM = pl.BlockSpec(memory_space=pltpu.VMEM)
EFFECT = pltpu.SideEffectType.DATAFLOW_SIDE_EFFECTING
TOKEN = jax.ShapeDtypeStruct((8, LANES), F32)


def _cp(n):
    return pltpu.CompilerParams(dimension_semantics=("arbitrary",) * n, vmem_limit_bytes=VMEM_LIMIT)


def _dot(a, b):
    return jnp.dot(a, b, preferred_element_type=F32)


def _dot_nt(a, b):
    return lax.dot_general(a, b, (((1,), (1,)), ((), ())), preferred_element_type=F32)


def _dot_tn(a, b):
    return lax.dot_general(a, b, (((0,), (0,)), ((), ())), preferred_element_type=F32)


def _place():
    x, y, c = lax.axis_index("x"), lax.axis_index("y"), lax.axis_index("c")
    chips = [(1 - x, y), (x, 1 - y), (1 - x, 1 - y)]
    return x, y, c, chips


def _rcopy(src, dst, send_sems, recv_sems, k, dev):
    return pltpu.make_async_remote_copy(src_ref=src, dst_ref=dst, send_sem=send_sems.at[k],
                                        recv_sem=recv_sems.at[k], device_id=dev, device_id_type=MESH)


def _hbm(a):
    return pltpu.with_memory_space_constraint(a, pltpu.HBM)


PEERS = {"chips": 3, "sibling": 1, "sibling3": 3, "all": 7}


def _targets(mode):
    x, y, c, chips = _place()
    b = 2 * x + y
    if mode == "chips":
        return b, c, [((px, py, c), 2 * px + py) for px, py in chips]
    if mode == "sibling":
        return b, c, [((x, y, 1 - c), b)]
    if mode == "sibling3":
        return b, c, [((x, y, 1 - c), 2 * px + py) for px, py in chips]
    flip = lambda v, f: 1 - v if f else v
    devs = [(flip(x, k >> 2 & 1), flip(y, k >> 1 & 1), flip(c, k & 1)) for k in range(1, 8)]
    return 4 * x + 2 * y + c, c, [(d, 4 * d[0] + 2 * d[1] + d[2]) for d in devs]


def _xchg_start(name, srcs, lands, plan, dep, mode="chips"):
    ns, nl, npeer = len(srcs), len(lands), PEERS[mode]

    def body(*refs):
        land = refs[ns:ns + nl]
        src = refs[:ns] if ns else land
        send_sems, recv_sems, token = refs[ns + nl + 1], refs[ns + nl + 2], refs[-1]
        me, c, peers = _targets(mode)
        for t in range(nl):
            for j, (dev, tag) in enumerate(peers):
                s, d, _ = plan(src[t], land[t], t, me, c, tag)
                _rcopy(s, d, send_sems, recv_sems, npeer * t + j, dev).start()
        token[...] = jnp.zeros_like(token)

    arrs = list(srcs) + list(lands)
    return pl.pallas_call(
        body, name=name,
        out_shape=(pltpu.SemaphoreType.DMA((npeer * nl,)), pltpu.SemaphoreType.DMA((npeer * nl,)),
                   *[pltpu.HBM(a.shape, a.dtype) for a in arrs], TOKEN),
        in_specs=[HBM] * (ns + nl) + [ANY], out_specs=(SEM, SEM, *[HBM] * (ns + nl), VMEM),
        input_output_aliases={i: 2 + i for i in range(ns + nl)},
        compiler_params=pltpu.CompilerParams(has_side_effects=EFFECT),
    )(*[_hbm(a) for a in arrs], dep)


def _xchg_wait(name, started, ns, nl, plan, after, mode="chips"):
    send_sems, recv_sems, thru = started[0], started[1], started[2:2 + ns + nl]
    npeer = PEERS[mode]

    def body(*refs):
        land = refs[ns:ns + nl]
        src = refs[:ns] if ns else land
        send_sems, recv_sems, token = refs[ns + nl], refs[ns + nl + 1], refs[-1]
        me, c, peers = _targets(mode)
        for t in range(nl):
            for j, (dev, tag) in enumerate(peers):
                s, _, a = plan(src[t], land[t], t, me, c, tag)
                cp = _rcopy(s, a, send_sems, recv_sems, npeer * t + j, dev)
                cp.wait_send()
                cp.wait_recv()
        token[...] = jnp.zeros_like(token)

    out = pl.pallas_call(
        body, name=name,
        out_shape=(*[pltpu.HBM(a.shape, a.dtype) for a in thru], TOKEN),
        in_specs=[HBM] * (ns + nl) + [SEM, SEM] + [ANY] * len(after), out_specs=(*[HBM] * (ns + nl), VMEM),
        input_output_aliases={i: i for i in range(ns + nl)},
        compiler_params=pltpu.CompilerParams(has_side_effects=EFFECT),
    )(*thru, send_sems, recv_sems, *after)
    return out[:ns], out[ns:ns + nl], out[-1]


def _half(ref_rows, which):
    h = ref_rows // 2
    return pl.ds(which * h, h)


def _gather_plan(src, land, t, b, c, pb):
    if land.shape[1] % 2 == 0:
        hs = _half(land.shape[1], c)
        return land.at[b, hs], land.at[b, hs], land.at[pb, hs]
    return land.at[b], land.at[b], land.at[pb]


def _gshare_plan(src, land, t, b, c, pb):
    return land.at[pb, _half(land.shape[1], c)], land.at[pb, _half(land.shape[1], c)], land.at[pb, _half(land.shape[1], 1 - c)]


def _rs_plan(src, land, t, me, c, tag):
    h = src.shape[1] // 2
    return src.at[tag // 2, pl.ds((tag % 2) * h, h), :], land.at[me], land.at[tag]


def _rows_block(h, cap=512):
    for rb in range(min(h, cap) // 16 * 16, 0, -16):
        if h % rb == 0:
            return rb
    return h


def _whole_plan(src, land, t, me, c, tag):
    return src, land, land


def _slot_plan(src, land, t, me, c, tag):
    return src, land.at[me], land.at[tag]


def _adam_update(gg, w, m, v):
    m2 = ADAM_B1 * m + (1.0 - ADAM_B1) * gg
    v2 = ADAM_B2 * v + (1.0 - ADAM_B2) * (gg * gg)
    mh = m2 / (1.0 - ADAM_B1 ** ADAM_STEP)
    vh = v2 / (1.0 - ADAM_B2 ** ADAM_STEP)
    return -ADAM_LR * (mh / (jnp.sqrt(vh) + ADAM_EPS) + ADAM_WD * w), m2, v2


def _adamw_layer(cidx, q_own, q_sib, w, m, v, bufs, l):
    L, R, C = w.shape
    h = R // 2
    rb = _rows_block(h, 256)
    nr = h // rb

    def body(c_ref, qo_ref, qs_ref, w_ref, m_ref, v_ref, *rest):
        g_ref, d_ref, mo_ref, vo_ref = rest[-4:]
        own = pl.program_id(0) == c_ref[0]

        def update(q_ref):
            gg = q_ref[0].astype(F32)
            for s in range(1, 8):
                gg = gg + q_ref[s].astype(F32)
            g_ref[...] = gg
            d_ref[...], mo_ref[...], vo_ref[...] = _adam_update(gg, w_ref[...], m_ref[...], v_ref[...])

        @pl.when(own)
        def _():
            update(qo_ref)

        @pl.when(jnp.logical_not(own))
        def _():
            update(qs_ref)

    q_own_spec = pl.BlockSpec(
        (8, rb, C), lambda hh, i, c: (0, jnp.where(hh == c[0], i, jnp.where(hh < c[0], 0, nr - 1)), 0))
    q_sib_spec = pl.BlockSpec(
        (8, rb, C), lambda hh, i, c: (0, jnp.where(hh != c[0], i, jnp.where(hh < 1 - c[0], 0, nr - 1)), 0))
    wspec = pl.BlockSpec((None, rb, C), lambda hh, i, c: (l, hh * nr + i, 0))
    return pl.pallas_call(
        body, name="adamw_layer", out_shape=[jax.ShapeDtypeStruct(w.shape, F32)] * 4,
        grid_spec=pltpu.PrefetchScalarGridSpec(
            num_scalar_prefetch=1, grid=(2, nr),
            in_specs=[q_own_spec, q_sib_spec, wspec, wspec, wspec] + [ANY] * 4, out_specs=[wspec] * 4),
        input_output_aliases={6 + k: k for k in range(4)},
        compiler_params=_cp(2),
    )(cidx, q_own, q_sib, w, m, v, *bufs)


def _adamw(g, w, m, v):
    L, R, C = g.shape
    rb = _rows_block(R)

    def body(g_ref, w_ref, m_ref, v_ref, d_ref, mo_ref, vo_ref):
        d_ref[...], mo_ref[...], vo_ref[...] = _adam_update(g_ref[...], w_ref[...], m_ref[...], v_ref[...])

    spec = pl.BlockSpec((None, rb, C), lambda l, i: (l, i, 0))
    return pl.pallas_call(
        body, name="adamw", grid=(L, R // rb), in_specs=[spec] * 4, out_specs=[spec] * 3,
        out_shape=[jax.ShapeDtypeStruct(g.shape, F32)] * 3, compiler_params=_cp(2),
    )(g, w, m, v)


def _sum_slots(buf):
    def body(b_ref, o_ref):
        acc = b_ref[0]
        for k in range(1, 8):
            acc = acc + b_ref[k]
        o_ref[...] = acc

    return pl.pallas_call(body, name="sum_slots", in_specs=[VMEM], out_specs=VMEM,
                          out_shape=jax.ShapeDtypeStruct(buf.shape[1:], F32))(buf)


def _rms(xf, g):
    r = lax.rsqrt(jnp.mean(xf * xf, axis=-1, keepdims=True) + EPS)
    return xf * r, r


def _lane_chunks(n):
    lo = (n // LANES + 1) // 2 * LANES
    return ((0, lo), (lo, n - lo))


def _load_ffn_weights(win_hbm, wout_hbm, win_v, wout_v, sems):
    fb = win_v.shape[2]
    loads = [pltpu.make_async_copy(win_hbm.at[k], win_v.at[k], sems.at[k]) for k in range(4)]
    loads += [pltpu.make_async_copy(wout_hbm.at[pl.ds(k * fb, fb)], wout_v.at[pl.ds(k * fb, fb)], sems.at[4 + k])
              for k in range(2)]
    for cp in loads:
        cp.start()
    for cp in loads:
        cp.wait()


def _fast_sigmoid(v):
    return pl.reciprocal(1.0 + jnp.exp(-v), approx=True)


def _ffn_fwd(x, g, win, wout):
    T, D = x.shape
    FB = win.shape[2]
    tm = min(TM, T)

    def body(x_ref, g_ref, win_hbm, wout_hbm, xo_ref, gu_ref, win_v, wout_v, sems):
        @pl.when(pl.program_id(0) == 0)
        def _():
            _load_ffn_weights(win_hbm, wout_hbm, win_v, wout_v, sems)

        xf = x_ref[...]
        xh, _ = _rms(xf, None)
        h = (xh * g_ref[...]).astype(BF16)
        acc = jnp.zeros((tm, D), F32)
        for blk in range(2):
            for lo, sz in _lane_chunks(FB):
                cols = pl.ds(blk * FB + lo, sz)
                gate = _dot(h, win_v[blk, :, pl.ds(lo, sz)])
                up = _dot(h, win_v[2 + blk, :, pl.ds(lo, sz)])
                gu_ref[0, :, cols] = gate.astype(BF16)
                gu_ref[1, :, cols] = up.astype(BF16)
                a = (gate * _fast_sigmoid(gate) * up).astype(BF16)
                acc = acc + _dot(a, wout_v[cols, :])
        xo_ref[...] = xf + 0.5 * acc

    row = pl.BlockSpec((tm, D), lambda i: (i, 0))
    return pl.pallas_call(
        body, name="ffn_fwd", grid=(T // tm,),
        in_specs=[row, pl.BlockSpec((1, D), lambda i: (0, 0)), ANY, ANY],
        out_specs=[row, pl.BlockSpec((2, tm, 2 * FB), lambda i: (0, i, 0))],
        out_shape=[jax.ShapeDtypeStruct((T, D), F32), jax.ShapeDtypeStruct((2, T, 2 * FB), BF16)],
        scratch_shapes=[pltpu.VMEM(win.shape, BF16), pltpu.VMEM(wout.shape, BF16), pltpu.SemaphoreType.DMA((6,))],
        compiler_params=_cp(1),
    )(x, g, win, wout)


def _mixproj_fwd(x, g, wt):
    T, D = x.shape
    W = wt.shape[0]
    QKV = ATTN_W + 2 * KV_W
    tm = min(TM_MIX, T)

    def body(x_ref, g_ref, w_ref, qkv_ref, u_ref):
        xh, _ = _rms(x_ref[...], None)
        h = (xh * g_ref[...]).astype(BF16)
        qkv_ref[...] = _dot_nt(h, w_ref[:QKV, :]).astype(BF16)
        u_ref[...] = _dot_nt(h, w_ref[QKV:, :])

    return pl.pallas_call(
        body, name="mixproj_fwd", grid=(T // tm,),
        in_specs=[pl.BlockSpec((tm, D), lambda i: (i, 0)), pl.BlockSpec((1, D), lambda i: (0, 0)),
                  pl.BlockSpec((W, D), lambda i: (0, 0))],
        out_specs=[pl.BlockSpec((tm, QKV), lambda i: (i, 0)), pl.BlockSpec((tm, W - QKV), lambda i: (i, 0))],
        out_shape=[jax.ShapeDtypeStruct((T, QKV), BF16), jax.ShapeDtypeStruct((T, W - QKV), F32)],
        compiler_params=_cp(1),
    )(x, g, wt)


def _attn_bias_table():
    rows, cols = GROUP * WINDOW, 2 * WINDOW
    row = lax.broadcasted_iota(jnp.int32, (N_KV, rows, cols), 1)
    col = lax.broadcasted_iota(jnp.int32, (N_KV, rows, cols), 2)
    head = GROUP * lax.broadcasted_iota(jnp.int32, (N_KV, rows, cols), 0) + (row >> 7)
    dist = (row & (WINDOW - 1)) + WINDOW - col
    slope = jnp.exp2(-(head + 1).astype(F32))
    return jnp.where((dist >= 0) & (dist < WINDOW), -slope * dist.astype(F32), NEG_INF)


def _first_block_mask(n):
    col = lax.broadcasted_iota(jnp.int32, (GROUP * WINDOW, 2 * WINDOW), 1)
    return (n > 0) | (col >= WINDOW)


def _sink_col(sink_ref, g):
    hi = lax.broadcasted_iota(jnp.int32, (GROUP * WINDOW, 1), 0) >> 7
    col = jnp.zeros((GROUP * WINDOW, 1), F32)
    for i in range(GROUP):
        col = jnp.where(hi == i, sink_ref[0, GROUP * g + i], col)
    return col


def _stack_heads(ref, g):
    return jnp.concatenate([ref[:, (GROUP * g + i) * HEAD_DIM:(GROUP * g + i + 1) * HEAD_DIM]
                            for i in range(GROUP)], axis=0)


def _band(kvp_ref, kvc_ref, off):
    return jnp.concatenate([kvp_ref[:, off:off + HEAD_DIM], kvc_ref[:, off:off + HEAD_DIM]], axis=0)


def _attn_probs(qs, k, bias, seen, sink):
    s = jnp.where(seen, _dot_nt(qs, k) * SCALE + bias, NEG_INF)
    m = jnp.maximum(jnp.max(s, axis=-1, keepdims=True), sink)
    p = jnp.exp(s - m)
    es = jnp.exp(sink - m)
    inv = 1.0 / (jnp.sum(p, axis=-1, keepdims=True) + es)
    return p * inv, es * inv


def _attn_fwd(sinks, tab, qkv):
    T = qkv.shape[0]
    nb = T // WINDOW

    def body(sink_ref, tab_ref, q_ref, kvp_ref, kvc_ref, o_ref):
        seen = _first_block_mask(pl.program_id(0))
        for g in range(N_KV):
            qs = _stack_heads(q_ref, g)
            k = _band(kvp_ref, kvc_ref, g * HEAD_DIM)
            v = _band(kvp_ref, kvc_ref, KV_W + g * HEAD_DIM)
            p, _ = _attn_probs(qs, k, tab_ref[g], seen, _sink_col(sink_ref, g))
            o = _dot(p.astype(BF16), v)
            for i in range(GROUP):
                h = GROUP * g + i
                o_ref[:, h * HEAD_DIM:(h + 1) * HEAD_DIM] = o[i * WINDOW:(i + 1) * WINDOW].astype(BF16)

    return pl.pallas_call(
        body, name="attn_fwd", grid=(nb,),
        in_specs=[pl.BlockSpec(memory_space=pltpu.SMEM),
                  pl.BlockSpec(tab.shape, lambda n: (0, 0, 0)),
                  pl.BlockSpec((WINDOW, ATTN_W), lambda n: (n, 0)),
                  pl.BlockSpec((WINDOW, 2 * KV_W), lambda n: (jnp.maximum(n - 1, 0), 2)),
                  pl.BlockSpec((WINDOW, 2 * KV_W), lambda n: (n, 2))],
        out_specs=pl.BlockSpec((WINDOW, ATTN_W), lambda n: (n, 0)),
        out_shape=jax.ShapeDtypeStruct((T, ATTN_W), BF16),
        compiler_params=_cp(1),
    )(sinks, tab, qkv, qkv, qkv)


def _shift_copies(src_ref, dst_ref, n):
    for b in range(1, 8):
        dst_ref[b - 1] = src_ref[b:b + n, :]


def _tap(src_ref, sh_ref, s, c0):
    a, b = divmod(s, 8)
    start = pl.multiple_of(c0 + 8 * a, 8)
    if b == 0:
        return src_ref[pl.ds(start, CONV_ROWS), :]
    return sh_ref[b - 1, pl.ds(start, CONV_ROWS), :]


def _glu_rows(u, ch):
    return u[:, :ch] * _fast_sigmoid(u[:, ch:])


def _fill_z(zs_ref, zsh_ref, uc_ref, up_ref, i, ch, n):
    zs_ref[0:HALO] = jnp.where(i > 0, _glu_rows(up_ref[...], ch), 0.0)
    zs_ref[HALO:] = _glu_rows(uc_ref[...], ch)
    _shift_copies(zs_ref, zsh_ref, n - 8)


def _conv_fwd(u, w, b, lg, lb):
    T = u.shape[0]
    CH = u.shape[1] // 2
    tm = min(TM, T)
    n = tm + HALO
    hb = tm // HALO

    def body(uc_ref, up_ref, w_ref, b_ref, lg_ref, lb_ref, conv_ref, ypre_ref, zs_ref, zsh_ref):
        i = pl.program_id(0)
        _fill_z(zs_ref, zsh_ref, uc_ref, up_ref, i, CH, n)
        bias = b_ref[...]

        def chunk(ci, carry):
            c0 = pl.multiple_of(ci * CONV_ROWS, CONV_ROWS)
            acc = jnp.broadcast_to(bias, (CONV_ROWS, CH))
            for k in range(CONV_W):
                acc = acc + w_ref[k:k + 1, :] * _tap(zs_ref, zsh_ref, HALO - (CONV_W - 1) + k, c0)
            ypre_ref[pl.ds(c0, CONV_ROWS), :] = acc
            return carry

        lax.fori_loop(0, tm // CONV_ROWS, chunk, 0)
        y = ypre_ref[...]
        mu = jnp.mean(y, axis=-1, keepdims=True)
        d = y - mu
        var = jnp.mean(d * d, axis=-1, keepdims=True)
        o = d * lax.rsqrt(var + EPS) * lg_ref[...] + lb_ref[...]
        conv_ref[...] = (o * _fast_sigmoid(o)).astype(BF16)

    vec = pl.BlockSpec((1, CH), lambda i: (0, 0))
    return pl.pallas_call(
        body, name="conv_fwd", grid=(T // tm,),
        in_specs=[pl.BlockSpec((tm, 2 * CH), lambda i: (i, 0)),
                  pl.BlockSpec((HALO, 2 * CH), lambda i: (jnp.maximum(i * hb - 1, 0), 0)),
                  pl.BlockSpec((CONV_W, CH), lambda i: (0, 0)), vec, vec, vec],
        out_specs=[pl.BlockSpec((tm, CH), lambda i: (i, 0)), pl.BlockSpec((tm, CH), lambda i: (i, 0))],
        out_shape=[jax.ShapeDtypeStruct((T, CH), BF16), jax.ShapeDtypeStruct((T, CH), F32)],
        scratch_shapes=[pltpu.VMEM((n, CH), F32), pltpu.VMEM((7, n - 8, CH), F32)],
        compiler_params=_cp(1),
    )(u, u, w, b, lg, lb)


def _mixout_fwd(x, attn, conv, wo):
    T, D = x.shape
    tm = min(TM_MIX, T)
    A = attn.shape[1]

    def body(x_ref, a_ref, c_ref, w_ref, xo_ref):
        xo_ref[...] = x_ref[...] + _dot(a_ref[...], w_ref[:A, :]) + _dot(c_ref[...], w_ref[A:, :])

    return pl.pallas_call(
        body, name="mixout_fwd", grid=(T // tm,),
        in_specs=[pl.BlockSpec((tm, D), lambda i: (i, 0)), pl.BlockSpec((tm, A), lambda i: (i, 0)),
                  pl.BlockSpec((tm, conv.shape[1]), lambda i: (i, 0)), pl.BlockSpec(wo.shape, lambda i: (0, 0))],
        out_specs=pl.BlockSpec((tm, D), lambda i: (i, 0)),
        out_shape=jax.ShapeDtypeStruct((T, D), F32),
        compiler_params=_cp(1),
    )(x, attn, conv, wo)


def _rms_bwd_rows(dh, xf, g):
    xh, r = _rms(xf, None)
    dxn = dh * g
    dx = r * (dxn - xh * jnp.mean(dxn * xh, axis=-1, keepdims=True))
    return dx, jnp.sum(dh * xh, axis=0, keepdims=True), xh * g


def _loss_head(x, g, tgt):
    T, D = x.shape
    tm = min(TM, T)

    def body(x_ref, g_ref, t_ref, loss_ref, dx_ref, dg_ref):
        @pl.when(pl.program_id(0) == 0)
        def _():
            loss_ref[...] = jnp.zeros_like(loss_ref)
            dg_ref[...] = jnp.zeros_like(dg_ref)

        xf = x_ref[...]
        g = g_ref[...]
        xh, _ = _rms(xf, None)
        e = xh * g - t_ref[...]
        loss_ref[...] += 0.5 * jnp.sum(jnp.mean(e * e, axis=-1, keepdims=True), axis=0, keepdims=True)
        dx, dg, _ = _rms_bwd_rows(e * (1.0 / D), xf, g)
        dx_ref[...] = dx
        dg_ref[...] += dg

    return pl.pallas_call(
        body, name="loss_head", grid=(T // tm,),
        in_specs=[pl.BlockSpec((tm, D), lambda i: (i, 0)), pl.BlockSpec((1, D), lambda i: (0, 0)),
                  pl.BlockSpec((tm, D), lambda i: (i, 0))],
        out_specs=[pl.BlockSpec((1, 1), lambda i: (0, 0)), pl.BlockSpec((tm, D), lambda i: (i, 0)),
                   pl.BlockSpec((1, D), lambda i: (0, 0))],
        out_shape=[jax.ShapeDtypeStruct((1, 1), F32), jax.ShapeDtypeStruct((T, D), F32),
                   jax.ShapeDtypeStruct((1, D), F32)],
        compiler_params=_cp(1),
    )(x, g, tgt)


def _ffn_bwd(dxo, x, g, gu, win, wout, dep):
    T, D = x.shape
    FB = win.shape[2]
    tm = min(TM_FFN_BWD, T)

    def body(dxo_ref, x_ref, g_ref, gu_ref, win_hbm, wout_hbm, dep_ref,
             dxi_ref, dg_ref, hb_ref, dgu_ref, a_ref, dyb_ref, win_v, wout_v, sems):
        @pl.when(pl.program_id(0) == 0)
        def _():
            _load_ffn_weights(win_hbm, wout_hbm, win_v, wout_v, sems)
            dg_ref[...] = jnp.zeros_like(dg_ref)

        dyb = (0.5 * dxo_ref[...]).astype(BF16)
        dyb_ref[...] = dyb
        dh = jnp.zeros((tm, D), F32)
        for blk in range(2):
            cols = pl.ds(blk * FB, FB)
            da = _dot_nt(dyb, wout_v[cols, :])
            gate = gu_ref[0, :, cols].astype(F32)
            up = gu_ref[1, :, cols].astype(F32)
            sg = _fast_sigmoid(gate)
            s = gate * sg
            a_ref[:, cols] = (s * up).astype(BF16)
            dgate = (da * up * (sg + s * (1.0 - sg))).astype(BF16)
            dup = (da * s).astype(BF16)
            dgu_ref[0, :, cols] = dgate
            dgu_ref[1, :, cols] = dup
            dh = dh + _dot_nt(dgate, win_v[blk]) + _dot_nt(dup, win_v[2 + blk])
        dx, dg, h = _rms_bwd_rows(dh, x_ref[...], g_ref[...])
        dxi_ref[...] = dxo_ref[...] + dx
        dg_ref[...] += dg
        hb_ref[...] = h.astype(BF16)

    row = pl.BlockSpec((tm, D), lambda i: (i, 0))
    act = pl.BlockSpec((2, tm, 2 * FB), lambda i: (0, i, 0))
    return pl.pallas_call(
        body, name="ffn_bwd", grid=(T // tm,),
        in_specs=[row, row, pl.BlockSpec((1, D), lambda i: (0, 0)), act, ANY, ANY, ANY],
        out_specs=[row, pl.BlockSpec((1, D), lambda i: (0, 0)), row, act,
                   pl.BlockSpec((tm, 2 * FB), lambda i: (i, 0)), row],
        out_shape=[jax.ShapeDtypeStruct((T, D), F32), jax.ShapeDtypeStruct((1, D), F32),
                   jax.ShapeDtypeStruct((T, D), BF16), jax.ShapeDtypeStruct((2, T, 2 * FB), BF16),
                   jax.ShapeDtypeStruct((T, 2 * FB), BF16), jax.ShapeDtypeStruct((T, D), BF16)],
        scratch_shapes=[pltpu.VMEM(win.shape, BF16), pltpu.VMEM(wout.shape, BF16), pltpu.SemaphoreType.DMA((6,))],
        compiler_params=_cp(1),
    )(dxo, x, g, gu, win, wout, dep)


def _mix_rms_bwd(dxo, x, g, dzs, wt):
    T, D = x.shape
    tm = min(TM, T)
    npair = len(dzs)

    def body(*refs):
        dxo_ref, x_ref, g_ref = refs[:3]
        dz_refs, w_ref = refs[3:3 + npair], refs[3 + npair]
        dxi_ref, dg_ref, hb_ref = refs[4 + npair:]

        @pl.when(pl.program_id(0) == 0)
        def _():
            dg_ref[...] = jnp.zeros_like(dg_ref)

        dh = jnp.zeros((tm, D), F32)
        k0 = 0
        for dz_ref in dz_refs:
            kp = dz_ref.shape[1]
            dh = dh + _dot(dz_ref[...], w_ref[k0:k0 + kp, :])
            k0 += kp
        dx, dg, h = _rms_bwd_rows(dh, x_ref[...], g_ref[...])
        dxi_ref[...] = dxo_ref[...] + dx
        dg_ref[...] += dg
        hb_ref[...] = h.astype(BF16)

    row = pl.BlockSpec((tm, D), lambda i: (i, 0))
    return pl.pallas_call(
        body, name="mix_rms_bwd", grid=(T // tm,),
        in_specs=[row, row, pl.BlockSpec((1, D), lambda i: (0, 0))]
                 + [pl.BlockSpec((tm, dz.shape[1]), lambda i: (i, 0)) for dz in dzs]
                 + [pl.BlockSpec(wt.shape, lambda i: (0, 0))],
        out_specs=[row, pl.BlockSpec((1, D), lambda i: (0, 0)), row],
        out_shape=[jax.ShapeDtypeStruct((T, D), F32), jax.ShapeDtypeStruct((1, D), F32),
                   jax.ShapeDtypeStruct((T, D), BF16)],
        compiler_params=_cp(1),
    )(dxo, x, g, *dzs, wt)


def _wgrad(name, a, b, a_spec, b_spec, out_shape, out_spec, nblk, dep, acc_shape):
    T = a.shape[0]
    tk = min(TK_WGRAD, T)
    nk = T // tk

    def body(a_ref, b_ref, dep_ref, o_ref, acc_ref):
        k = pl.program_id(1)

        @pl.when(k == 0)
        def _():
            acc_ref[...] = jnp.zeros_like(acc_ref)

        acc_ref[...] += _dot_tn(a_ref[...], b_ref[...])

        @pl.when(k == nk - 1)
        def _():
            o_ref[...] = acc_ref[...].reshape(o_ref.shape).astype(BF16)

    return pl.pallas_call(
        body, name=name, grid=(nblk, nk), in_specs=[a_spec, b_spec, ANY], out_specs=out_spec,
        out_shape=jax.ShapeDtypeStruct(out_shape, BF16), scratch_shapes=[pltpu.VMEM(acc_shape, F32)],
        compiler_params=_cp(2),
    )(a, b, dep)


def _wgrad_ffn_in(hb, dgu, dep):
    T, D = hb.shape
    FB = dgu.shape[2] // 2
    tk = min(TK_WGRAD, T)
    return _wgrad("wgrad_ffn_in", hb, dgu,
                  pl.BlockSpec((tk, D), lambda b, k: (k, 0)),
                  pl.BlockSpec((None, tk, FB), lambda b, k: (b // 2, k, b % 2)),
                  (4, D, FB), pl.BlockSpec((None, D, FB), lambda b, k: (b, 0, 0)), 4, dep, (D, FB))


def _wgrad_ffn_out(a, dyb, dep):
    T, D = dyb.shape
    FB = a.shape[1] // 2
    tk = min(TK_WGRAD, T)
    return _wgrad("wgrad_ffn_out", a, dyb,
                  pl.BlockSpec((tk, FB), lambda b, k: (k, b)),
                  pl.BlockSpec((tk, D), lambda b, k: (k, 0)),
                  (4, FB // 2, D), pl.BlockSpec((2, FB // 2, D), lambda b, k: (b, 0, 0)), 2, dep, (FB, D))


def _wgrad_cat(a_list, b_list):
    T = a_list[0].shape[0]
    tk = min(TK_WGRAD, T)
    nk = T // tk
    na = len(a_list)
    M, N = sum(a.shape[1] for a in a_list), sum(b.shape[1] for b in b_list)

    def body(*refs):
        a_refs, b_refs, o_ref, acc_ref = refs[:na], refs[na:-2], refs[-2], refs[-1]
        k = pl.program_id(0)

        @pl.when(k == 0)
        def _():
            acc_ref[...] = jnp.zeros_like(acc_ref)

        r0 = 0
        for a_ref in a_refs:
            c0 = 0
            for b_ref in b_refs:
                m, n = a_ref.shape[1], b_ref.shape[1]
                acc_ref[r0:r0 + m, c0:c0 + n] += _dot_tn(a_ref[...], b_ref[...])
                c0 += n
            r0 += a_ref.shape[1]

        @pl.when(k == nk - 1)
        def _():
            o_ref[...] = acc_ref[...].astype(BF16)

    return pl.pallas_call(
        body, name="wgrad_cat", grid=(nk,),
        in_specs=[pl.BlockSpec((tk, v.shape[1]), lambda k: (k, 0)) for v in list(a_list) + list(b_list)],
        out_specs=pl.BlockSpec((M, N), lambda k: (0, 0)),
        out_shape=jax.ShapeDtypeStruct((M, N), BF16), scratch_shapes=[pltpu.VMEM((M, N), F32)],
        compiler_params=_cp(1),
    )(*a_list, *b_list)


def _mixout_bwd(dxo, wo):
    T, D = dxo.shape
    tm = min(TM_MIX, T)
    A = ATTN_W
    C = wo.shape[0] - A

    def body(dxo_ref, w_ref, dyb_ref, da_ref, dc_ref):
        dyb = dxo_ref[...].astype(BF16)
        dyb_ref[...] = dyb
        da_ref[...] = _dot_nt(dyb, w_ref[:A, :]).astype(BF16)
        dc_ref[...] = _dot_nt(dyb, w_ref[A:, :])

    return pl.pallas_call(
        body, name="mixout_bwd", grid=(T // tm,),
        in_specs=[pl.BlockSpec((tm, D), lambda i: (i, 0)), pl.BlockSpec(wo.shape, lambda i: (0, 0))],
        out_specs=[pl.BlockSpec((tm, D), lambda i: (i, 0)), pl.BlockSpec((tm, A), lambda i: (i, 0)),
                   pl.BlockSpec((tm, C), lambda i: (i, 0))],
        out_shape=[jax.ShapeDtypeStruct((T, D), BF16), jax.ShapeDtypeStruct((T, A), BF16),
                   jax.ShapeDtypeStruct((T, C), F32)],
        compiler_params=_cp(1),
    )(dxo, wo)


def _conv_bwd(dconv, ypre, u, w, lg, lb):
    T, CH = dconv.shape
    tm = min(TM, T)
    n = tm + HALO
    hb = tm // HALO
    nt = T // tm
    nchunk = tm // CONV_ROWS

    def body(dc_ref, dcn_ref, yp_ref, ypn_ref, uc_ref, up_ref, w_ref, lg_ref, lb_ref,
             du_ref, dw_ref, dvec_ref, zs_ref, zsh_ref, dy_ref, dysh_ref, dz_ref, dwacc_ref):
        i = pl.program_id(0)

        @pl.when(i == 0)
        def _():
            dwacc_ref[...] = jnp.zeros_like(dwacc_ref)
            dvec_ref[...] = jnp.zeros_like(dvec_ref)

        g, bb = lg_ref[...], lb_ref[...]

        def ln_bwd(dc, yp):
            mu = jnp.mean(yp, axis=-1, keepdims=True)
            d = yp - mu
            rs = lax.rsqrt(jnp.mean(d * d, axis=-1, keepdims=True) + EPS)
            yn = d * rs
            o = yn * g + bb
            sg = _fast_sigmoid(o)
            do = dc * (sg * (1.0 + o * (1.0 - sg)))
            dyn = do * g
            dyp = rs * (dyn - jnp.mean(dyn, axis=-1, keepdims=True)
                        - yn * jnp.mean(dyn * yn, axis=-1, keepdims=True))
            return dyp, do, yn

        dyp, do, yn = ln_bwd(dc_ref[...], yp_ref[...])
        dvec_ref[0:1, :] += jnp.sum(dyp, axis=0, keepdims=True)
        dvec_ref[1:2, :] += jnp.sum(do * yn, axis=0, keepdims=True)
        dvec_ref[2:3, :] += jnp.sum(do, axis=0, keepdims=True)
        dy_ref[0:tm] = dyp
        dyh, _, _ = ln_bwd(dcn_ref[...], ypn_ref[...])
        dy_ref[tm:] = jnp.where(i < nt - 1, dyh, 0.0)
        _shift_copies(dy_ref, dysh_ref, n - 8)
        _fill_z(zs_ref, zsh_ref, uc_ref, up_ref, i, CH, n)

        def chunk(ci, carry):
            c0 = pl.multiple_of(ci * CONV_ROWS, CONV_ROWS)
            acc = jnp.zeros((CONV_ROWS, CH), F32)
            for k in range(CONV_W):
                acc = acc + w_ref[k:k + 1, :] * _tap(dy_ref, dysh_ref, CONV_W - 1 - k, c0)
            dz_ref[pl.ds(c0, CONV_ROWS), :] = acc
            dyc = dy_ref[pl.ds(c0, CONV_ROWS), :]
            for k in range(CONV_W):
                prod = dyc * _tap(zs_ref, zsh_ref, HALO - (CONV_W - 1) + k, c0)
                dwacc_ref[k] += jnp.sum(prod.reshape(CONV_ROWS // 8, 8, CH), axis=0)
            return carry

        lax.fori_loop(0, nchunk, chunk, 0)

        @pl.when(i == nt - 1)
        def _():
            dw_ref[...] = jnp.sum(dwacc_ref[...], axis=1)

        uc = uc_ref[...]
        a = uc[:, :CH]
        sg = _fast_sigmoid(uc[:, CH:])
        dz = dz_ref[...]
        du_ref[:, :CH] = (dz * sg).astype(BF16)
        du_ref[:, CH:] = (dz * a * sg * (1.0 - sg)).astype(BF16)

    cur = lambda c: pl.BlockSpec((tm, c), lambda i: (i, 0))
    nxt = lambda c: pl.BlockSpec((HALO, c), lambda i: (jnp.minimum((i + 1) * hb, T // HALO - 1), 0))
    vec = pl.BlockSpec((1, CH), lambda i: (0, 0))
    return pl.pallas_call(
        body, name="conv_bwd", grid=(nt,),
        in_specs=[cur(CH), nxt(CH), cur(CH), nxt(CH), cur(2 * CH),
                  pl.BlockSpec((HALO, 2 * CH), lambda i: (jnp.maximum(i * hb - 1, 0), 0)),
                  pl.BlockSpec((CONV_W, CH), lambda i: (0, 0)), vec, vec],
        out_specs=[pl.BlockSpec((tm, 2 * CH), lambda i: (i, 0)), pl.BlockSpec((32, CH), lambda i: (0, 0)),
                   pl.BlockSpec((8, CH), lambda i: (0, 0))],
        out_shape=[jax.ShapeDtypeStruct((T, 2 * CH), BF16), jax.ShapeDtypeStruct((32, CH), F32),
                   jax.ShapeDtypeStruct((8, CH), F32)],
        scratch_shapes=[pltpu.VMEM((n, CH), F32), pltpu.VMEM((7, n - 8, CH), F32),
                        pltpu.VMEM((n, CH), F32), pltpu.VMEM((7, n - 8, CH), F32), pltpu.VMEM((tm, CH), F32),
                        pltpu.VMEM((32, 8, CH), F32)],
        compiler_params=_cp(1),
    )(dconv, dconv, ypre, ypre, u, u, w, lg, lb)


def _attn_bwd(sinks, tab, qkv, dattn):
    T = qkv.shape[0]
    nb = T // WINDOW

    def body(sink_ref, tab_ref, q_ref, kvp_ref, kvc_ref, do_ref, dq_ref, dkv_ref, dsk_ref, carry_ref):
        n = pl.program_id(0)

        @pl.when(n == 0)
        def _():
            dsk_ref[...] = jnp.zeros_like(dsk_ref)
            carry_ref[...] = jnp.zeros_like(carry_ref)

        @pl.when(n < nb)
        def _():
            seen = _first_block_mask(n)
            for g in range(N_KV):
                qs = _stack_heads(q_ref, g)
                dos = _stack_heads(do_ref, g)
                k = _band(kvp_ref, kvc_ref, g * HEAD_DIM)
                v = _band(kvp_ref, kvc_ref, KV_W + g * HEAD_DIM)
                p, ps = _attn_probs(qs, k, tab_ref[g], seen, _sink_col(sink_ref, g))
                dp = _dot_nt(dos, v)
                delta = jnp.sum(p * dp, axis=-1, keepdims=True)
                dsb = (p * (dp - delta)).astype(BF16)
                dsink = -ps * delta
                dqs = _dot(dsb, k) * SCALE
                dk = _dot_tn(dsb, qs) * SCALE
                dv = _dot_tn(p.astype(BF16), dos)
                for i in range(GROUP):
                    h = GROUP * g + i
                    dq_ref[:, h * HEAD_DIM:(h + 1) * HEAD_DIM] = dqs[i * WINDOW:(i + 1) * WINDOW].astype(BF16)
                    dsk_ref[h:h + 1, :] += jnp.sum(dsink[i * WINDOW:(i + 1) * WINDOW], axis=0, keepdims=True)
                for off, d in ((g * HEAD_DIM, dk), (KV_W + g * HEAD_DIM, dv)):
                    dkv_ref[:, off:off + HEAD_DIM] = (carry_ref[:, off:off + HEAD_DIM] + d[:WINDOW]).astype(BF16)
                    carry_ref[:, off:off + HEAD_DIM] = d[WINDOW:]

        @pl.when(n == nb)
        def _():
            dkv_ref[...] = carry_ref[...].astype(BF16)

    last = nb - 1
    return pl.pallas_call(
        body, name="attn_bwd", grid=(nb + 1,),
        in_specs=[pl.BlockSpec(memory_space=pltpu.SMEM),
                  pl.BlockSpec(tab.shape, lambda n: (0, 0, 0)),
                  pl.BlockSpec((WINDOW, ATTN_W), lambda n: (jnp.minimum(n, last), 0)),
                  pl.BlockSpec((WINDOW, 2 * KV_W), lambda n: (jnp.clip(n - 1, 0, last), 2)),
                  pl.BlockSpec((WINDOW, 2 * KV_W), lambda n: (jnp.minimum(n, last), 2)),
                  pl.BlockSpec((WINDOW, ATTN_W), lambda n: (jnp.minimum(n, last), 0))],
        out_specs=[pl.BlockSpec((WINDOW, ATTN_W), lambda n: (jnp.minimum(n, last), 0)),
                   pl.BlockSpec((WINDOW, 2 * KV_W), lambda n: (jnp.maximum(n - 1, 0), 0)),
                   pl.BlockSpec((8, LANES), lambda n: (0, 0))],
        out_shape=[jax.ShapeDtypeStruct((T, ATTN_W), BF16), jax.ShapeDtypeStruct((T, 2 * KV_W), BF16),
                   jax.ShapeDtypeStruct((8, LANES), F32)],
        scratch_shapes=[pltpu.VMEM((WINDOW, 2 * KV_W), F32)],
        compiler_params=_cp(1),
    )(sinks, tab, qkv, qkv, qkv, dattn)


def _pack(arrs):
    flat = jnp.concatenate([a.reshape(-1) for a in arrs])
    pad = -flat.shape[0] % (8 * LANES)
    return jnp.pad(flat, (0, pad)).reshape(1, -1, LANES)


def _unpack(packed, like):
    flat = packed.reshape(-1)
    out, off = [], 0
    for a in like:
        out.append(flat[off:off + a.size].reshape(a.shape))
        off += a.size
    return out


def kernel(x, norm_ffn1, w_ffn1_in, w_ffn1_out, norm_mix, w_in, sinks, w_dw, b_dw, conv_ln_g, conv_ln_b, w_out, norm_ffn2, w_ffn2_in, w_ffn2_out, final_norm, loss_target, m_norm_ffn1, m_w_ffn1_in, m_w_ffn1_out, m_norm_mix, m_w_in, m_sinks, m_w_dw, m_b_dw, m_conv_ln_g, m_conv_ln_b, m_w_out, m_norm_ffn2, m_w_ffn2_in, m_w_ffn2_out, m_final_norm, v_norm_ffn1, v_w_ffn1_in, v_w_ffn1_out, v_norm_mix, v_w_in, v_sinks, v_w_dw, v_b_dw, v_conv_ln_g, v_conv_ln_b, v_w_out, v_norm_ffn2, v_w_ffn2_in, v_w_ffn2_out, v_final_norm):
    L, D = norm_ffn1.shape
    T = x.shape[1]
    FB = w_ffn1_in.shape[2]
    CH = b_dw.shape[1]
    QKV = ATTN_W + 2 * KV_W
    xs = x.reshape(T, D)
    tgt = loss_target.reshape(T, D)
    cx, cy, cc = lax.axis_index("x"), lax.axis_index("y"), lax.axis_index("c")
    chip = 2 * cx + cy
    cidx = cc.reshape(1).astype(jnp.int32)
    tr = lambda a_: jnp.transpose(a_, (0, 2, 1))
    big_w = (w_ffn1_in, w_ffn1_out, tr(w_in), w_out, w_ffn2_in, w_ffn2_out)
    big_m = (m_w_ffn1_in, m_w_ffn1_out, tr(m_w_in), m_w_out, m_w_ffn2_in, m_w_ffn2_out)
    big_v = (v_w_ffn1_in, v_w_ffn1_out, tr(v_w_in), v_w_out, v_w_ffn2_in, v_w_ffn2_out)
    NW = len(big_w) + 1

    def own_slot(a, slots=4, idx=chip):
        return lax.dynamic_update_index_in_dim(lax.empty((slots,) + a.shape, a.dtype), a, idx, 0)

    def shards(l, tok):
        return [own_slot((w_[l] + tok[0, 0]).astype(BF16)) for w_ in big_w] + [own_slot(w_dw[l] + tok[0, 0])]

    def gather_start(lands, tok):
        return _xchg_start("gather_start", [], lands, _gather_plan, tok)

    def gather_arrived(started, after, n, taps):
        _, lands, tok = _xchg_wait("gather_wait", started, 0, n, _gather_plan, after)
        return _xchg_start("gshare_start", [], lands[:-1] if taps else lands, _gshare_plan, tok, "sibling3"), lands[-1]

    def shared_weights(shared, after, n):
        _, mats, tok = _xchg_wait("gshare_wait", shared, 0, n, _gshare_plan, after, "sibling3")
        return mats, tok

    row = lambda a, l: a[l].reshape(1, -1)
    tab = _attn_bias_table()
    NB = len(big_w)

    saved, W = [], []
    zero_tok = jnp.zeros((8, LANES), F32)
    src0 = shards(0, zero_tok)
    started = gather_start(src0[:2], zero_tok)
    rest0 = gather_start(src0[2:], started[-1])
    cast = [None] + [shards(l, rest0[-1]) for l in range(1, L)]
    shared, _ = gather_arrived(started, [xs] + [a_ for c_ in cast[1:] for a_ in c_], 2, False)
    after = [shared[-1]]
    for l in range(L):
        mats, tok = shared_weights(shared, after, 2 if l == 0 else NB)
        started = None
        if l + 1 < L:
            started = gather_start(cast[l + 1], tok)
            tok = started[-1]
        x0 = xs
        x1, gu1 = _ffn_fwd(x0, row(norm_ffn1, l) + tok[0, 0], mats[0], mats[1].reshape(2 * FB, D))
        gm_row = row(norm_mix, l)
        if l == 0:
            shared, gdw = gather_arrived(rest0, [x1], NW - 2, True)
            rest, tok = shared_weights(shared, [shared[-1]], NB - 2)
            mats = list(mats) + list(rest)
            gm_row = gm_row + tok[0, 0]
        g1i, g1o, gi, go, g2i, g2o = mats
        w = dict(f1i=g1i, f1o=g1o.reshape(2 * FB, D), f2i=g2i, f2o=g2o.reshape(2 * FB, D),
                 wit=gi.reshape(-1, D), wo=go.reshape(-1, D),
                 wdw=jnp.transpose(gdw, (1, 0, 2)).reshape(CONV_W, CH))
        W.append(w)
        qkv, u = _mixproj_fwd(x1, gm_row, w["wit"])
        attn = _attn_fwd(row(sinks, l), tab, qkv)
        conv, ypre = _conv_fwd(u, w["wdw"], row(b_dw, l), row(conv_ln_g, l), row(conv_ln_b, l))
        x2 = _mixout_fwd(x1, attn, conv, w["wo"])
        g2_row = row(norm_ffn2, l)
        if started is not None and l > 0:
            shared, gdw = gather_arrived(started, [x2], NW, True)
            g2_row = g2_row + shared[-1][0, 0]
        xs, gu2 = _ffn_fwd(x2, g2_row, w["f2i"], w["f2o"])
        if started is not None and l == 0:
            shared, gdw = gather_arrived(started, [xs], NW, True)
        saved.append((x0, gu1, x1, qkv, u, attn, conv, ypre, x2, gu2))
        after = [xs]

    loss_part, dx, d_final = _loss_head(xs, final_norm.reshape(1, D), tgt)
    loss = lax.psum(loss_part[0, 0], ("x", "y", "c"))

    bufs = [[lax.empty(w_.shape, F32) for _ in range(4)] for w_ in big_w]
    d_n1, d_nm, d_n2 = [None] * L, [None] * L, [None] * L
    d_sk, d_bdw, d_lg, d_lb, d_wdw = [None] * L, [None] * L, [None] * L, [None] * L, [None] * L

    me_idx = 4 * cx + 2 * cy + cc

    def reduce_start(gs):
        lands = []
        for g in gs:
            h = g.shape[1] // 2
            mine = lax.dynamic_slice(g, (chip, cc * h, 0), (1, h, g.shape[2]))[0]
            lands.append(own_slot(mine, 8, me_idx))
        return _xchg_start("rs_start", gs, lands, _rs_plan, zero_tok, "all")

    def share_start(rs_started, after, n):
        _, qs, tok = _xchg_wait("rs_wait", rs_started, n, n, _rs_plan, after, "all")
        return _xchg_start("qshare_start", qs, [lax.empty(q.shape, q.dtype) for q in qs], _whole_plan, tok, "sibling")

    def finish(l, shared, after, idxs):
        q_own, q_sib, _ = _xchg_wait("qshare_wait", shared, len(idxs), len(idxs), _whole_plan, after, "sibling")
        for k, t in enumerate(idxs):
            bufs[t] = _adamw_layer(cidx, q_own[k], q_sib[k], big_w[t], big_m[t], big_v[t], bufs[t], l)

    ALL = list(range(NB))
    EARLY, LATE = ALL[2:], ALL[:2]
    rs_list, shares = [], []
    tok = zero_tok
    for l in reversed(range(L)):
        w = W[l]
        x0, gu1, x1, qkv, u, attn, conv, ypre, x2, gu2 = saved[l]
        dx, d_n2[l], hb, dgu, a, dyb = _ffn_bwd(dx, x2, row(norm_ffn2, l), gu2, w["f2i"], w["f2o"], tok)
        g_f2i, g_f2o = _wgrad_ffn_in(hb, dgu, tok), _wgrad_ffn_out(a, dyb, tok)
        lg_row = row(conv_ln_g, l)
        if len(rs_list) >= 2:
            pl_, st_ = rs_list[-2]
            shares.append((pl_, share_start(st_, [g_f2o], NB)))
            lg_row = lg_row + shares[-1][1][-1][0, 0]
        dyb, dattn, dconv = _mixout_bwd(dx, w["wo"])
        g_wo = _wgrad_cat([attn, conv], [dyb]).reshape(4, -1, D)
        du, dwdw, dvec = _conv_bwd(dconv, ypre, u, w["wdw"], lg_row, row(conv_ln_b, l))
        d_wdw[l], d_bdw[l], d_lg[l], d_lb[l] = dwdw[:CONV_W], dvec[0], dvec[1], dvec[2]
        dq, dkv, dsk = _attn_bwd(row(sinks, l), tab, qkv, dattn)
        d_sk[l] = dsk[:, 0]
        dx, d_nm[l], hb = _mix_rms_bwd(dx, x1, row(norm_mix, l), [dq, dkv, du], w["wit"])
        g_wi = _wgrad_cat([dq, dkv, du], [hb]).reshape(4, -1, D)
        if l == 0:
            rs_early = reduce_start([g_wi, g_wo, g_f2i, g_f2o])
            tok = rs_early[-1]
        dx, d_n1[l], hb, dgu, a, dyb = _ffn_bwd(dx, x0, row(norm_ffn1, l), gu1, w["f1i"], w["f1o"], tok)
        g_f1i, g_f1o = _wgrad_ffn_in(hb, dgu, tok), _wgrad_ffn_out(a, dyb, tok)
        rs_started = reduce_start([g_f1i, g_f1o] if l == 0 else [g_f1i, g_f1o, g_wi, g_wo, g_f2i, g_f2o])
        tok = rs_started[-1]
        rs_list.append((l, rs_started))
    grad_x = dx.reshape(x.shape)

    small_g = [jnp.concatenate(d, axis=0) for d in (d_n1, d_nm, d_n2)] + [d_final, jnp.stack(d_sk)] + \
              [jnp.stack(d) for d in (d_bdw, d_lg, d_lb, d_wdw)]
    packed = _pack(small_g)[0]
    small_started = _xchg_start("small_start", [packed], [own_slot(packed, 8, 4 * cx + 2 * cy + cc)], _slot_plan, tok, "all")

    rs_late = rs_list.pop()[1]
    after = [small_started[-1]]
    if len(rs_list) > len(shares):
        pl_, st_ = rs_list[len(shares)]
        shares.append((pl_, share_start(st_, after, NB)))
        after = [shares[-1][1][-1]]
    if shares:
        finish(*shares.pop(0), after, ALL)
        after = [b_[0] for b_ in bufs]
    sh_early = share_start(rs_early, after, len(EARLY))
    after = [sh_early[-1]]
    sh_late = None
    for l, sh in shares:
        finish(l, sh, after, ALL)
        after = [b_[0] for b_ in bufs]
        if sh_late is None:
            sh_late = share_start(rs_late, after, len(LATE))
            after = [sh_late[-1]]
    if sh_late is None:
        sh_late = share_start(rs_late, after, len(LATE))
        after = [sh_late[-1]]
    _, (slots,), _ = _xchg_wait("small_wait", small_started, 1, 1, _slot_plan, after, "all")
    small_sum = _unpack(_sum_slots(slots), small_g)
    g_wdw = lax.dynamic_slice_in_dim(small_sum[8], chip * w_dw.shape[2], w_dw.shape[2], axis=2)
    small_g = [small_sum[0], small_sum[1], small_sum[2], small_sum[3].reshape(D), small_sum[4],
               small_sum[5], small_sum[6], small_sum[7], g_wdw]
    small_w = (norm_ffn1, norm_mix, norm_ffn2, final_norm, sinks, b_dw, conv_ln_g, conv_ln_b, w_dw)
    small_m = (m_norm_ffn1, m_norm_mix, m_norm_ffn2, m_final_norm, m_sinks, m_b_dw, m_conv_ln_g, m_conv_ln_b, m_w_dw)
    small_v = (v_norm_ffn1, v_norm_mix, v_norm_ffn2, v_final_norm, v_sinks, v_b_dw, v_conv_ln_g, v_conv_ln_b, v_w_dw)
    upd = _adamw(_pack(small_g), _pack(small_w), _pack(small_m), _pack(small_v))
    small_upd = [_unpack(u_, small_w) for u_ in upd]
    finish(0, sh_early, [upd[0]], EARLY)
    finish(0, sh_late, [bufs[t][0] for t in EARLY], LATE)

    order = ("norm_ffn1", "w_ffn1_in", "w_ffn1_out", "norm_mix", "w_in", "sinks", "w_dw", "b_dw", "conv_ln_g",
             "conv_ln_b", "w_out", "norm_ffn2", "w_ffn2_in", "w_ffn2_out", "final_norm")
    small_names = ("norm_ffn1", "norm_mix", "norm_ffn2", "final_norm", "sinks", "b_dw", "conv_ln_g", "conv_ln_b", "w_dw")
    big_names = ("w_ffn1_in", "w_ffn1_out", "w_in", "w_out", "w_ffn2_in", "w_ffn2_out")
    grads, deltas, new_m, new_v = {}, {}, {}, {}
    for i, nme in enumerate(small_names):
        grads[nme], deltas[nme], new_m[nme], new_v[nme] = small_g[i], small_upd[0][i], small_upd[1][i], small_upd[2][i]
    for i, nme in enumerate(big_names):
        grads[nme], deltas[nme], new_m[nme], new_v[nme] = [tr(b_) for b_ in bufs[i]] if nme == "w_in" else bufs[i]
    return (loss, grad_x, *[grads[n] for n in order], *[deltas[n] for n in order],
            *[new_m[n] for n in order], *[new_v[n] for n in order])
```

```python
import jax
import jax.numpy as jnp
from jax import lax
from jax.experimental import pallas as pl
from jax.experimental.pallas import tpu as pltpu

F32, BF16 = jnp.float32, jnp.bfloat16
EPS = 1e-6
NEG_INF = -1e30
HEAD_DIM = 64
N_HEADS = 8
N_KV = 2
GROUP = N_HEADS // N_KV
WINDOW = 128
ATTN_W = N_HEADS * HEAD_DIM
KV_W = N_KV * HEAD_DIM
CONV_W = 31
HALO = 32
CONV_ROWS = 32
SCALE = 1.0 / 8.0
ADAM_LR, ADAM_B1, ADAM_B2, ADAM_EPS, ADAM_WD, ADAM_STEP = 0.001, 0.9, 0.999, 1e-08, 0.01, 10
TM = 512
TM_FFN_BWD = 256
TK_WGRAD = 2048
TM_MIX = 1024
LANES = 128
VMEM_LIMIT = 52 * 1024 * 1024
MESH = pl.DeviceIdType.MESH
ANY = pl.BlockSpec(memory_space=pl.ANY)
HBM = pl.BlockSpec(memory_space=pltpu.HBM)
SEM = pl.BlockSpec(memory_space=pltpu.SEMAPHORE)
VMEM = pl.BlockSpec(memory_space=pltpu.VMEM)
EFFECT = pltpu.SideEffectType.DATAFLOW_SIDE_EFFECTING
TOKEN = jax.ShapeDtypeStruct((8, LANES), F32)


def _cp(n):
    return pltpu.CompilerParams(dimension_semantics=("arbitrary",) * n, vmem_limit_bytes=VMEM_LIMIT)


def _dot(a, b):
    return jnp.dot(a, b, preferred_element_type=F32)


def _dot_nt(a, b):
    return lax.dot_general(a, b, (((1,), (1,)), ((), ())), preferred_element_type=F32)


def _dot_tn(a, b):
    return lax.dot_general(a, b, (((0,), (0,)), ((), ())), preferred_element_type=F32)


def _place():
    x, y, c = lax.axis_index("x"), lax.axis_index("y"), lax.axis_index("c")
    chips = [(1 - x, y), (x, 1 - y), (1 - x, 1 - y)]
    return x, y, c, chips


def _rcopy(src, dst, send_sems, recv_sems, k, dev):
    return pltpu.make_async_remote_copy(src_ref=src, dst_ref=dst, send_sem=send_sems.at[k],
                                        recv_sem=recv_sems.at[k], device_id=dev, device_id_type=MESH)


def _hbm(a):
    return pltpu.with_memory_space_constraint(a, pltpu.HBM)


PEERS = {"chips": 3, "sibling": 1, "sibling3": 3, "all": 7}


def _targets(mode):
    x, y, c, chips = _place()
    b = 2 * x + y
    if mode == "chips":
        return b, c, [((px, py, c), 2 * px + py) for px, py in chips]
    if mode == "sibling":
        return b, c, [((x, y, 1 - c), b)]
    if mode == "sibling3":
        return b, c, [((x, y, 1 - c), 2 * px + py) for px, py in chips]
    flip = lambda v, f: 1 - v if f else v
    devs = [(flip(x, k >> 2 & 1), flip(y, k >> 1 & 1), flip(c, k & 1)) for k in range(1, 8)]
    return 4 * x + 2 * y + c, c, [(d, 4 * d[0] + 2 * d[1] + d[2]) for d in devs]


def _xchg_start(name, srcs, lands, plan, dep, mode="chips"):
    ns, nl, npeer = len(srcs), len(lands), PEERS[mode]

    def body(*refs):
        land = refs[ns:ns + nl]
        src = refs[:ns] if ns else land
        send_sems, recv_sems, token = refs[ns + nl + 1], refs[ns + nl + 2], refs[-1]
        me, c, peers = _targets(mode)
        for t in range(nl):
            for j, (dev, tag) in enumerate(peers):
                s, d, _ = plan(src[t], land[t], t, me, c, tag)
                _rcopy(s, d, send_sems, recv_sems, npeer * t + j, dev).start()
        token[...] = jnp.zeros_like(token)

    arrs = list(srcs) + list(lands)
    return pl.pallas_call(
        body, name=name,
        out_shape=(pltpu.SemaphoreType.DMA((npeer * nl,)), pltpu.SemaphoreType.DMA((npeer * nl,)),
                   *[pltpu.HBM(a.shape, a.dtype) for a in arrs], TOKEN),
        in_specs=[HBM] * (ns + nl) + [ANY], out_specs=(SEM, SEM, *[HBM] * (ns + nl), VMEM),
        input_output_aliases={i: 2 + i for i in range(ns + nl)},
        compiler_params=pltpu.CompilerParams(has_side_effects=EFFECT),
    )(*[_hbm(a) for a in arrs], dep)


def _xchg_wait(name, started, ns, nl, plan, after, mode="chips"):
    send_sems, recv_sems, thru = started[0], started[1], started[2:2 + ns + nl]
    npeer = PEERS[mode]

    def body(*refs):
        land = refs[ns:ns + nl]
        src = refs[:ns] if ns else land
        send_sems, recv_sems, token = refs[ns + nl], refs[ns + nl + 1], refs[-1]
        me, c, peers = _targets(mode)
        for t in range(nl):
            for j, (dev, tag) in enumerate(peers):
                s, _, a = plan(src[t], land[t], t, me, c, tag)
                cp = _rcopy(s, a, send_sems, recv_sems, npeer * t + j, dev)
                cp.wait_send()
                cp.wait_recv()
        token[...] = jnp.zeros_like(token)

    out = pl.pallas_call(
        body, name=name,
        out_shape=(*[pltpu.HBM(a.shape, a.dtype) for a in thru], TOKEN),
        in_specs=[HBM] * (ns + nl) + [SEM, SEM] + [ANY] * len(after), out_specs=(*[HBM] * (ns + nl), VMEM),
        input_output_aliases={i: i for i in range(ns + nl)},
        compiler_params=pltpu.CompilerParams(has_side_effects=EFFECT),
    )(*thru, send_sems, recv_sems, *after)
    return out[:ns], out[ns:ns + nl], out[-1]


def _half(ref_rows, which):
    h = ref_rows // 2
    return pl.ds(which * h, h)


def _gather_plan(src, land, t, b, c, pb):
    if land.shape[1] % 2 == 0:
        hs = _half(land.shape[1], c)
        return land.at[b, hs], land.at[b, hs], land.at[pb, hs]
    return land.at[b], land.at[b], land.at[pb]


def _gshare_plan(src, land, t, b, c, pb):
    return land.at[pb, _half(land.shape[1], c)], land.at[pb, _half(land.shape[1], c)], land.at[pb, _half(land.shape[1], 1 - c)]


def _rs_plan(src, land, t, me, c, tag):
    h = src.shape[1] // 2
    return src.at[tag // 2, pl.ds((tag % 2) * h, h), :], land.at[me], land.at[tag]


def _rows_block(h, cap=512):
    for rb in range(min(h, cap) // 16 * 16, 0, -16):
        if h % rb == 0:
            return rb
    return h


def _whole_plan(src, land, t, me, c, tag):
    return src, land, land


def _slot_plan(src, land, t, me, c, tag):
    return src, land.at[me], land.at[tag]


def _adam_update(gg, w, m, v):
    m2 = ADAM_B1 * m + (1.0 - ADAM_B1) * gg
    v2 = ADAM_B2 * v + (1.0 - ADAM_B2) * (gg * gg)
    mh = m2 / (1.0 - ADAM_B1 ** ADAM_STEP)
    vh = v2 / (1.0 - ADAM_B2 ** ADAM_STEP)
    return -ADAM_LR * (mh / (jnp.sqrt(vh) + ADAM_EPS) + ADAM_WD * w), m2, v2


def _adamw_layer(cidx, q_own, q_sib, w, m, v, bufs, l):
    L, R, C = w.shape
    h = R // 2
    rb = _rows_block(h, 256)
    nr = h // rb

    def body(c_ref, qo_ref, qs_ref, w_ref, m_ref, v_ref, *rest):
        g_ref, d_ref, mo_ref, vo_ref = rest[-4:]
        own = pl.program_id(0) == c_ref[0]

        def update(q_ref):
            gg = q_ref[0].astype(F32)
            for s in range(1, 8):
                gg = gg + q_ref[s].astype(F32)
            g_ref[...] = gg
            d_ref[...], mo_ref[...], vo_ref[...] = _adam_update(gg, w_ref[...], m_ref[...], v_ref[...])

        @pl.when(own)
        def _():
            update(qo_ref)

        @pl.when(jnp.logical_not(own))
        def _():
            update(qs_ref)

    q_own_spec = pl.BlockSpec(
        (8, rb, C), lambda hh, i, c: (0, jnp.where(hh == c[0], i, jnp.where(hh < c[0], 0, nr - 1)), 0))
    q_sib_spec = pl.BlockSpec(
        (8, rb, C), lambda hh, i, c: (0, jnp.where(hh != c[0], i, jnp.where(hh < 1 - c[0], 0, nr - 1)), 0))
    wspec = pl.BlockSpec((None, rb, C), lambda hh, i, c: (l, hh * nr + i, 0))
    return pl.pallas_call(
        body, name="adamw_layer", out_shape=[jax.ShapeDtypeStruct(w.shape, F32)] * 4,
        grid_spec=pltpu.PrefetchScalarGridSpec(
            num_scalar_prefetch=1, grid=(2, nr),
            in_specs=[q_own_spec, q_sib_spec, wspec, wspec, wspec] + [ANY] * 4, out_specs=[wspec] * 4),
        input_output_aliases={6 + k: k for k in range(4)},
        compiler_params=_cp(2),
    )(cidx, q_own, q_sib, w, m, v, *bufs)


def _adamw(g, w, m, v):
    L, R, C = g.shape
    rb = _rows_block(R)

    def body(g_ref, w_ref, m_ref, v_ref, d_ref, mo_ref, vo_ref):
        d_ref[...], mo_ref[...], vo_ref[...] = _adam_update(g_ref[...], w_ref[...], m_ref[...], v_ref[...])

    spec = pl.BlockSpec((None, rb, C), lambda l, i: (l, i, 0))
    return pl.pallas_call(
        body, name="adamw", grid=(L, R // rb), in_specs=[spec] * 4, out_specs=[spec] * 3,
        out_shape=[jax.ShapeDtypeStruct(g.shape, F32)] * 3, compiler_params=_cp(2),
    )(g, w, m, v)


def _sum_slots(buf):
    def body(b_ref, o_ref):
        acc = b_ref[0]
        for k in range(1, 8):
            acc = acc + b_ref[k]
        o_ref[...] = acc

    return pl.pallas_call(body, name="sum_slots", in_specs=[VMEM], out_specs=VMEM,
                          out_shape=jax.ShapeDtypeStruct(buf.shape[1:], F32))(buf)


def _rms(xf, g):
    r = lax.rsqrt(jnp.mean(xf * xf, axis=-1, keepdims=True) + EPS)
    return xf * r, r


def _lane_chunks(n):
    lo = (n // LANES + 1) // 2 * LANES
    return ((0, lo), (lo, n - lo))


def _ffn_weight_loads(win_hbm, wout_hbm, win_v, wout_v, sems):
    fb = win_v.shape[2]
    return [[pltpu.make_async_copy(win_hbm.at[b], win_v.at[b], sems.at[3 * b]),
             pltpu.make_async_copy(win_hbm.at[2 + b], win_v.at[2 + b], sems.at[3 * b + 1]),
             pltpu.make_async_copy(wout_hbm.at[pl.ds(b * fb, fb)], wout_v.at[pl.ds(b * fb, fb)], sems.at[3 * b + 2])]
            for b in range(2)]


def _start_all(loads):
    for block in loads:
        for cp in block:
            cp.start()


def _wait_all(block):
    for cp in block:
        cp.wait()


def _fast_sigmoid(v):
    return pl.reciprocal(1.0 + jnp.exp(-v), approx=True)


def _ffn_fwd(x, g, win, wout):
    T, D = x.shape
    FB = win.shape[2]
    tm = min(TM, T)

    def body(x_ref, g_ref, win_hbm, wout_hbm, xo_ref, gu_ref, win_v, wout_v, sems):
        first = pl.program_id(0) == 0
        loads = _ffn_weight_loads(win_hbm, wout_hbm, win_v, wout_v, sems)
        pl.when(first)(lambda: _start_all(loads))

        xf = x_ref[...]
        xh, _ = _rms(xf, None)
        h = (xh * g_ref[...]).astype(BF16)
        acc = jnp.zeros((tm, D), F32)
        for blk in range(2):
            pl.when(first)(lambda blk=blk: _wait_all(loads[blk]))
            for lo, sz in _lane_chunks(FB):
                cols = pl.ds(blk * FB + lo, sz)
                gate = _dot(h, win_v[blk, :, pl.ds(lo, sz)])
                up = _dot(h, win_v[2 + blk, :, pl.ds(lo, sz)])
                gu_ref[0, :, cols] = gate.astype(BF16)
                gu_ref[1, :, cols] = up.astype(BF16)
                a = (gate * _fast_sigmoid(gate) * up).astype(BF16)
                acc = acc + _dot(a, wout_v[cols, :])
        xo_ref[...] = xf + 0.5 * acc

    row = pl.BlockSpec((tm, D), lambda i: (i, 0))
    return pl.pallas_call(
        body, name="ffn_fwd", grid=(T // tm,),
        in_specs=[row, pl.BlockSpec((1, D), lambda i: (0, 0)), ANY, ANY],
        out_specs=[row, pl.BlockSpec((2, tm, 2 * FB), lambda i: (0, i, 0))],
        out_shape=[jax.ShapeDtypeStruct((T, D), F32), jax.ShapeDtypeStruct((2, T, 2 * FB), BF16)],
        scratch_shapes=[pltpu.VMEM(win.shape, BF16), pltpu.VMEM(wout.shape, BF16), pltpu.SemaphoreType.DMA((6,))],
        compiler_params=_cp(1),
    )(x, g, win, wout)


def _mixproj_fwd(x, g, wt):
    T, D = x.shape
    W = wt.shape[0]
    QKV = ATTN_W + 2 * KV_W
    tm = min(TM_MIX, T)

    def body(x_ref, g_ref, w_ref, qkv_ref, u_ref):
        xh, _ = _rms(x_ref[...], None)
        h = (xh * g_ref[...]).astype(BF16)
        qkv_ref[...] = _dot_nt(h, w_ref[:QKV, :]).astype(BF16)
        u_ref[...] = _dot_nt(h, w_ref[QKV:, :])

    return pl.pallas_call(
        body, name="mixproj_fwd", grid=(T // tm,),
        in_specs=[pl.BlockSpec((tm, D), lambda i: (i, 0)), pl.BlockSpec((1, D), lambda i: (0, 0)),
                  pl.BlockSpec((W, D), lambda i: (0, 0))],
        out_specs=[pl.BlockSpec((tm, QKV), lambda i: (i, 0)), pl.BlockSpec((tm, W - QKV), lambda i: (i, 0))],
        out_shape=[jax.ShapeDtypeStruct((T, QKV), BF16), jax.ShapeDtypeStruct((T, W - QKV), F32)],
        compiler_params=_cp(1),
    )(x, g, wt)


def _attn_bias_table():
    rows, cols = GROUP * WINDOW, 2 * WINDOW
    row = lax.broadcasted_iota(jnp.int32, (N_KV, rows, cols), 1)
    col = lax.broadcasted_iota(jnp.int32, (N_KV, rows, cols), 2)
    head = GROUP * lax.broadcasted_iota(jnp.int32, (N_KV, rows, cols), 0) + (row >> 7)
    dist = (row & (WINDOW - 1)) + WINDOW - col
    slope = jnp.exp2(-(head + 1).astype(F32))
    return jnp.where((dist >= 0) & (dist < WINDOW), -slope * dist.astype(F32), NEG_INF)


def _first_block_mask(n):
    col = lax.broadcasted_iota(jnp.int32, (GROUP * WINDOW, 2 * WINDOW), 1)
    return (n > 0) | (col >= WINDOW)


def _sink_col(sink_ref, g):
    hi = lax.broadcasted_iota(jnp.int32, (GROUP * WINDOW, 1), 0) >> 7
    col = jnp.zeros((GROUP * WINDOW, 1), F32)
    for i in range(GROUP):
        col = jnp.where(hi == i, sink_ref[0, GROUP * g + i], col)
    return col


def _stack_heads(ref, g):
    return jnp.concatenate([ref[:, (GROUP * g + i) * HEAD_DIM:(GROUP * g + i + 1) * HEAD_DIM]
                            for i in range(GROUP)], axis=0)


def _band(kvp_ref, kvc_ref, off):
    return jnp.concatenate([kvp_ref[:, off:off + HEAD_DIM], kvc_ref[:, off:off + HEAD_DIM]], axis=0)


def _attn_probs(qs, k, bias, seen, sink):
    s = jnp.where(seen, _dot_nt(qs, k) * SCALE + bias, NEG_INF)
    m = jnp.maximum(jnp.max(s, axis=-1, keepdims=True), sink)
    p = jnp.exp(s - m)
    es = jnp.exp(sink - m)
    inv = 1.0 / (jnp.sum(p, axis=-1, keepdims=True) + es)
    return p * inv, es * inv


def _attn_fwd(sinks, tab, qkv):
    T = qkv.shape[0]
    nb = T // WINDOW

    def body(sink_ref, tab_ref, q_ref, kvp_ref, kvc_ref, o_ref):
        seen = _first_block_mask(pl.program_id(0))
        for g in range(N_KV):
            qs = _stack_heads(q_ref, g)
            k = _band(kvp_ref, kvc_ref, g * HEAD_DIM)
            v = _band(kvp_ref, kvc_ref, KV_W + g * HEAD_DIM)
            p, _ = _attn_probs(qs, k, tab_ref[g], seen, _sink_col(sink_ref, g))
            o = _dot(p.astype(BF16), v)
            for i in range(GROUP):
                h = GROUP * g + i
                o_ref[:, h * HEAD_DIM:(h + 1) * HEAD_DIM] = o[i * WINDOW:(i + 1) * WINDOW].astype(BF16)

    return pl.pallas_call(
        body, name="attn_fwd", grid=(nb,),
        in_specs=[pl.BlockSpec(memory_space=pltpu.SMEM),
                  pl.BlockSpec(tab.shape, lambda n: (0, 0, 0)),
                  pl.BlockSpec((WINDOW, ATTN_W), lambda n: (n, 0)),
                  pl.BlockSpec((WINDOW, 2 * KV_W), lambda n: (jnp.maximum(n - 1, 0), 2)),
                  pl.BlockSpec((WINDOW, 2 * KV_W), lambda n: (n, 2))],
        out_specs=pl.BlockSpec((WINDOW, ATTN_W), lambda n: (n, 0)),
        out_shape=jax.ShapeDtypeStruct((T, ATTN_W), BF16),
        compiler_params=_cp(1),
    )(sinks, tab, qkv, qkv, qkv)


def _shift_copies(src_ref, dst_ref, n):
    for b in range(1, 8):
        dst_ref[b - 1] = src_ref[b:b + n, :]


def _tap(src_ref, sh_ref, s, c0):
    a, b = divmod(s, 8)
    start = pl.multiple_of(c0 + 8 * a, 8)
    if b == 0:
        return src_ref[pl.ds(start, CONV_ROWS), :]
    return sh_ref[b - 1, pl.ds(start, CONV_ROWS), :]


def _glu_rows(u, ch):
    return u[:, :ch] * _fast_sigmoid(u[:, ch:])


def _fill_z(zs_ref, zsh_ref, uc_ref, up_ref, i, ch, n):
    zs_ref[0:HALO] = jnp.where(i > 0, _glu_rows(up_ref[...], ch), 0.0)
    zs_ref[HALO:] = _glu_rows(uc_ref[...], ch)
    _shift_copies(zs_ref, zsh_ref, n - 8)


def _conv_fwd(u, w, b, lg, lb):
    T = u.shape[0]
    CH = u.shape[1] // 2
    tm = min(TM, T)
    n = tm + HALO
    hb = tm // HALO

    def body(uc_ref, up_ref, w_ref, b_ref, lg_ref, lb_ref, conv_ref, ypre_ref, zs_ref, zsh_ref):
        i = pl.program_id(0)
        _fill_z(zs_ref, zsh_ref, uc_ref, up_ref, i, CH, n)
        bias = b_ref[...]

        def chunk(ci, carry):
            c0 = pl.multiple_of(ci * CONV_ROWS, CONV_ROWS)
            acc = jnp.broadcast_to(bias, (CONV_ROWS, CH))
            for k in range(CONV_W):
                acc = acc + w_ref[k:k + 1, :] * _tap(zs_ref, zsh_ref, HALO - (CONV_W - 1) + k, c0)
            ypre_ref[pl.ds(c0, CONV_ROWS), :] = acc
            return carry

        lax.fori_loop(0, tm // CONV_ROWS, chunk, 0)
        y = ypre_ref[...]
        mu = jnp.mean(y, axis=-1, keepdims=True)
        d = y - mu
        var = jnp.mean(d * d, axis=-1, keepdims=True)
        o = d * lax.rsqrt(var + EPS) * lg_ref[...] + lb_ref[...]
        conv_ref[...] = (o * _fast_sigmoid(o)).astype(BF16)

    vec = pl.BlockSpec((1, CH), lambda i: (0, 0))
    return pl.pallas_call(
        body, name="conv_fwd", grid=(T // tm,),
        in_specs=[pl.BlockSpec((tm, 2 * CH), lambda i: (i, 0)),
                  pl.BlockSpec((HALO, 2 * CH), lambda i: (jnp.maximum(i * hb - 1, 0), 0)),
                  pl.BlockSpec((CONV_W, CH), lambda i: (0, 0)), vec, vec, vec],
        out_specs=[pl.BlockSpec((tm, CH), lambda i: (i, 0)), pl.BlockSpec((tm, CH), lambda i: (i, 0))],
        out_shape=[jax.ShapeDtypeStruct((T, CH), BF16), jax.ShapeDtypeStruct((T, CH), F32)],
        scratch_shapes=[pltpu.VMEM((n, CH), F32), pltpu.VMEM((7, n - 8, CH), F32)],
        compiler_params=_cp(1),
    )(u, u, w, b, lg, lb)


def _mixout_fwd(x, attn, conv, wo):
    T, D = x.shape
    tm = min(TM_MIX, T)
    A = attn.shape[1]

    def body(x_ref, a_ref, c_ref, w_ref, xo_ref):
        xo_ref[...] = x_ref[...] + _dot(a_ref[...], w_ref[:A, :]) + _dot(c_ref[...], w_ref[A:, :])

    return pl.pallas_call(
        body, name="mixout_fwd", grid=(T // tm,),
        in_specs=[pl.BlockSpec((tm, D), lambda i: (i, 0)), pl.BlockSpec((tm, A), lambda i: (i, 0)),
                  pl.BlockSpec((tm, conv.shape[1]), lambda i: (i, 0)), pl.BlockSpec(wo.shape, lambda i: (0, 0))],
        out_specs=pl.BlockSpec((tm, D), lambda i: (i, 0)),
        out_shape=jax.ShapeDtypeStruct((T, D), F32),
        compiler_params=_cp(1),
    )(x, attn, conv, wo)


def _rms_bwd_rows(dh, xf, g):
    xh, r = _rms(xf, None)
    dxn = dh * g
    dx = r * (dxn - xh * jnp.mean(dxn * xh, axis=-1, keepdims=True))
    return dx, jnp.sum(dh * xh, axis=0, keepdims=True), xh * g


def _loss_head(x, g, tgt):
    T, D = x.shape
    tm = min(TM, T)

    def body(x_ref, g_ref, t_ref, loss_ref, dx_ref, dg_ref):
        @pl.when(pl.program_id(0) == 0)
        def _():
            loss_ref[...] = jnp.zeros_like(loss_ref)
            dg_ref[...] = jnp.zeros_like(dg_ref)

        xf = x_ref[...]
        g = g_ref[...]
        xh, _ = _rms(xf, None)
        e = xh * g - t_ref[...]
        loss_ref[...] += 0.5 * jnp.sum(jnp.mean(e * e, axis=-1, keepdims=True), axis=0, keepdims=True)
        dx, dg, _ = _rms_bwd_rows(e * (1.0 / D), xf, g)
        dx_ref[...] = dx
        dg_ref[...] += dg

    return pl.pallas_call(
        body, name="loss_head", grid=(T // tm,),
        in_specs=[pl.BlockSpec((tm, D), lambda i: (i, 0)), pl.BlockSpec((1, D), lambda i: (0, 0)),
                  pl.BlockSpec((tm, D), lambda i: (i, 0))],
        out_specs=[pl.BlockSpec((1, 1), lambda i: (0, 0)), pl.BlockSpec((tm, D), lambda i: (i, 0)),
                   pl.BlockSpec((1, D), lambda i: (0, 0))],
        out_shape=[jax.ShapeDtypeStruct((1, 1), F32), jax.ShapeDtypeStruct((T, D), F32),
                   jax.ShapeDtypeStruct((1, D), F32)],
        compiler_params=_cp(1),
    )(x, g, tgt)


def _ffn_bwd(dxo, x, g, gu, win, wout, dep):
    T, D = x.shape
    FB = win.shape[2]
    tm = min(TM_FFN_BWD, T)

    def body(dxo_ref, x_ref, g_ref, gu_ref, win_hbm, wout_hbm, dep_ref,
             dxi_ref, dg_ref, hb_ref, dgu_ref, a_ref, dyb_ref, win_v, wout_v, sems):
        first = pl.program_id(0) == 0
        loads = _ffn_weight_loads(win_hbm, wout_hbm, win_v, wout_v, sems)

        @pl.when(first)
        def _():
            _start_all(loads)
            dg_ref[...] = jnp.zeros_like(dg_ref)

        dyb = (0.5 * dxo_ref[...]).astype(BF16)
        dyb_ref[...] = dyb
        dh = jnp.zeros((tm, D), F32)
        for blk in range(2):
            pl.when(first)(lambda blk=blk: _wait_all(loads[blk]))
            cols = pl.ds(blk * FB, FB)
            da = _dot_nt(dyb, wout_v[cols, :])
            gate = gu_ref[0, :, cols].astype(F32)
            up = gu_ref[1, :, cols].astype(F32)
            sg = _fast_sigmoid(gate)
            s = gate * sg
            a_ref[:, cols] = (s * up).astype(BF16)
            dgate = (da * up * (sg + s * (1.0 - sg))).astype(BF16)
            dup = (da * s).astype(BF16)
            dgu_ref[0, :, cols] = dgate
            dgu_ref[1, :, cols] = dup
            dh = dh + _dot_nt(dgate, win_v[blk]) + _dot_nt(dup, win_v[2 + blk])
        dx, dg, h = _rms_bwd_rows(dh, x_ref[...], g_ref[...])
        dxi_ref[...] = dxo_ref[...] + dx
        dg_ref[...] += dg
        hb_ref[...] = h.astype(BF16)

    row = pl.BlockSpec((tm, D), lambda i: (i, 0))
    act = pl.BlockSpec((2, tm, 2 * FB), lambda i: (0, i, 0))
    return pl.pallas_call(
        body, name="ffn_bwd", grid=(T // tm,),
        in_specs=[row, row, pl.BlockSpec((1, D), lambda i: (0, 0)), act, ANY, ANY, ANY],
        out_specs=[row, pl.BlockSpec((1, D), lambda i: (0, 0)), row, act,
                   pl.BlockSpec((tm, 2 * FB), lambda i: (i, 0)), row],
        out_shape=[jax.ShapeDtypeStruct((T, D), F32), jax.ShapeDtypeStruct((1, D), F32),
                   jax.ShapeDtypeStruct((T, D), BF16), jax.ShapeDtypeStruct((2, T, 2 * FB), BF16),
                   jax.ShapeDtypeStruct((T, 2 * FB), BF16), jax.ShapeDtypeStruct((T, D), BF16)],
        scratch_shapes=[pltpu.VMEM(win.shape, BF16), pltpu.VMEM(wout.shape, BF16), pltpu.SemaphoreType.DMA((6,))],
        compiler_params=_cp(1),
    )(dxo, x, g, gu, win, wout, dep)


def _mix_rms_bwd(dxo, x, g, dzs, wt):
    T, D = x.shape
    tm = min(TM, T)
    npair = len(dzs)

    def body(*refs):
        dxo_ref, x_ref, g_ref = refs[:3]
        dz_refs, w_ref = refs[3:3 + npair], refs[3 + npair]
        dxi_ref, dg_ref, hb_ref = refs[4 + npair:]

        @pl.when(pl.program_id(0) == 0)
        def _():
            dg_ref[...] = jnp.zeros_like(dg_ref)

        dh = jnp.zeros((tm, D), F32)
        k0 = 0
        for dz_ref in dz_refs:
            kp = dz_ref.shape[1]
            dh = dh + _dot(dz_ref[...], w_ref[k0:k0 + kp, :])
            k0 += kp
        dx, dg, h = _rms_bwd_rows(dh, x_ref[...], g_ref[...])
        dxi_ref[...] = dxo_ref[...] + dx
        dg_ref[...] += dg
        hb_ref[...] = h.astype(BF16)

    row = pl.BlockSpec((tm, D), lambda i: (i, 0))
    return pl.pallas_call(
        body, name="mix_rms_bwd", grid=(T // tm,),
        in_specs=[row, row, pl.BlockSpec((1, D), lambda i: (0, 0))]
                 + [pl.BlockSpec((tm, dz.shape[1]), lambda i: (i, 0)) for dz in dzs]
                 + [pl.BlockSpec(wt.shape, lambda i: (0, 0))],
        out_specs=[row, pl.BlockSpec((1, D), lambda i: (0, 0)), row],
        out_shape=[jax.ShapeDtypeStruct((T, D), F32), jax.ShapeDtypeStruct((1, D), F32),
                   jax.ShapeDtypeStruct((T, D), BF16)],
        compiler_params=_cp(1),
    )(dxo, x, g, *dzs, wt)


def _wgrad(name, a, b, a_spec, b_spec, out_shape, out_spec, nblk, dep, acc_shape):
    T = a.shape[0]
    tk = min(TK_WGRAD, T)
    nk = T // tk

    def body(a_ref, b_ref, dep_ref, o_ref, acc_ref):
        k = pl.program_id(1)

        @pl.when(k == 0)
        def _():
            acc_ref[...] = jnp.zeros_like(acc_ref)

        acc_ref[...] += _dot_tn(a_ref[...], b_ref[...])

        @pl.when(k == nk - 1)
        def _():
            o_ref[...] = acc_ref[...].reshape(o_ref.shape).astype(BF16)

    return pl.pallas_call(
        body, name=name, grid=(nblk, nk), in_specs=[a_spec, b_spec, ANY], out_specs=out_spec,
        out_shape=jax.ShapeDtypeStruct(out_shape, BF16), scratch_shapes=[pltpu.VMEM(acc_shape, F32)],
        compiler_params=_cp(2),
    )(a, b, dep)


def _wgrad_ffn_in(hb, dgu, dep):
    T, D = hb.shape
    FB = dgu.shape[2] // 2
    tk = min(TK_WGRAD, T)
    return _wgrad("wgrad_ffn_in", hb, dgu,
                  pl.BlockSpec((tk, D), lambda b, k: (k, 0)),
                  pl.BlockSpec((None, tk, FB), lambda b, k: (b // 2, k, b % 2)),
                  (4, D, FB), pl.BlockSpec((None, D, FB), lambda b, k: (b, 0, 0)), 4, dep, (D, FB))


def _wgrad_ffn_out(a, dyb, dep):
    T, D = dyb.shape
    FB = a.shape[1] // 2
    tk = min(TK_WGRAD, T)
    return _wgrad("wgrad_ffn_out", a, dyb,
                  pl.BlockSpec((tk, FB), lambda b, k: (k, b)),
                  pl.BlockSpec((tk, D), lambda b, k: (k, 0)),
                  (4, FB // 2, D), pl.BlockSpec((2, FB // 2, D), lambda b, k: (b, 0, 0)), 2, dep, (FB, D))


def _wgrad_cat(a_list, b_list):
    T = a_list[0].shape[0]
    tk = min(TK_WGRAD, T)
    nk = T // tk
    na = len(a_list)
    M, N = sum(a.shape[1] for a in a_list), sum(b.shape[1] for b in b_list)

    def body(*refs):
        a_refs, b_refs, o_ref, acc_ref = refs[:na], refs[na:-2], refs[-2], refs[-1]
        k = pl.program_id(0)

        @pl.when(k == 0)
        def _():
            acc_ref[...] = jnp.zeros_like(acc_ref)

        r0 = 0
        for a_ref in a_refs:
            c0 = 0
            for b_ref in b_refs:
                m, n = a_ref.shape[1], b_ref.shape[1]
                acc_ref[r0:r0 + m, c0:c0 + n] += _dot_tn(a_ref[...], b_ref[...])
                c0 += n
            r0 += a_ref.shape[1]

        @pl.when(k == nk - 1)
        def _():
            o_ref[...] = acc_ref[...].astype(BF16)

    return pl.pallas_call(
        body, name="wgrad_cat", grid=(nk,),
        in_specs=[pl.BlockSpec((tk, v.shape[1]), lambda k: (k, 0)) for v in list(a_list) + list(b_list)],
        out_specs=pl.BlockSpec((M, N), lambda k: (0, 0)),
        out_shape=jax.ShapeDtypeStruct((M, N), BF16), scratch_shapes=[pltpu.VMEM((M, N), F32)],
        compiler_params=_cp(1),
    )(*a_list, *b_list)


def _mixout_bwd(dxo, wo):
    T, D = dxo.shape
    tm = min(TM_MIX, T)
    A = ATTN_W
    C = wo.shape[0] - A

    def body(dxo_ref, w_ref, dyb_ref, da_ref, dc_ref):
        dyb = dxo_ref[...].astype(BF16)
        dyb_ref[...] = dyb
        da_ref[...] = _dot_nt(dyb, w_ref[:A, :]).astype(BF16)
        dc_ref[...] = _dot_nt(dyb, w_ref[A:, :])

    return pl.pallas_call(
        body, name="mixout_bwd", grid=(T // tm,),
        in_specs=[pl.BlockSpec((tm, D), lambda i: (i, 0)), pl.BlockSpec(wo.shape, lambda i: (0, 0))],
        out_specs=[pl.BlockSpec((tm, D), lambda i: (i, 0)), pl.BlockSpec((tm, A), lambda i: (i, 0)),
                   pl.BlockSpec((tm, C), lambda i: (i, 0))],
        out_shape=[jax.ShapeDtypeStruct((T, D), BF16), jax.ShapeDtypeStruct((T, A), BF16),
                   jax.ShapeDtypeStruct((T, C), F32)],
        compiler_params=_cp(1),
    )(dxo, wo)


def _conv_bwd(dconv, ypre, u, w, lg, lb):
    T, CH = dconv.shape
    tm = min(TM, T)
    n = tm + HALO
    hb = tm // HALO
    nt = T // tm
    nchunk = tm // CONV_ROWS

    def body(dc_ref, dcn_ref, yp_ref, ypn_ref, uc_ref, up_ref, w_ref, lg_ref, lb_ref,
             du_ref, dw_ref, dvec_ref, zs_ref, zsh_ref, dy_ref, dysh_ref, dz_ref, dwacc_ref):
        i = pl.program_id(0)

        @pl.when(i == 0)
        def _():
            dwacc_ref[...] = jnp.zeros_like(dwacc_ref)
            dvec_ref[...] = jnp.zeros_like(dvec_ref)

        g, bb = lg_ref[...], lb_ref[...]

        def ln_bwd(dc, yp):
            mu = jnp.mean(yp, axis=-1, keepdims=True)
            d = yp - mu
            rs = lax.rsqrt(jnp.mean(d * d, axis=-1, keepdims=True) + EPS)
            yn = d * rs
            o = yn * g + bb
            sg = _fast_sigmoid(o)
            do = dc * (sg * (1.0 + o * (1.0 - sg)))
            dyn = do * g
            dyp = rs * (dyn - jnp.mean(dyn, axis=-1, keepdims=True)
                        - yn * jnp.mean(dyn * yn, axis=-1, keepdims=True))
            return dyp, do, yn

        dyp, do, yn = ln_bwd(dc_ref[...], yp_ref[...])
        dvec_ref[0:1, :] += jnp.sum(dyp, axis=0, keepdims=True)
        dvec_ref[1:2, :] += jnp.sum(do * yn, axis=0, keepdims=True)
        dvec_ref[2:3, :] += jnp.sum(do, axis=0, keepdims=True)
        dy_ref[0:tm] = dyp
        dyh, _, _ = ln_bwd(dcn_ref[...], ypn_ref[...])
        dy_ref[tm:] = jnp.where(i < nt - 1, dyh, 0.0)
        _shift_copies(dy_ref, dysh_ref, n - 8)
        _fill_z(zs_ref, zsh_ref, uc_ref, up_ref, i, CH, n)

        def chunk(ci, carry):
            c0 = pl.multiple_of(ci * CONV_ROWS, CONV_ROWS)
            acc = jnp.zeros((CONV_ROWS, CH), F32)
            for k in range(CONV_W):
                acc = acc + w_ref[k:k + 1, :] * _tap(dy_ref, dysh_ref, CONV_W - 1 - k, c0)
            dz_ref[pl.ds(c0, CONV_ROWS), :] = acc
            dyc = dy_ref[pl.ds(c0, CONV_ROWS), :]
            for k in range(CONV_W):
                prod = dyc * _tap(zs_ref, zsh_ref, HALO - (CONV_W - 1) + k, c0)
                dwacc_ref[k] += jnp.sum(prod.reshape(CONV_ROWS // 8, 8, CH), axis=0)
            return carry

        lax.fori_loop(0, nchunk, chunk, 0)

        @pl.when(i == nt - 1)
        def _():
            dw_ref[...] = jnp.sum(dwacc_ref[...], axis=1)

        uc = uc_ref[...]
        a = uc[:, :CH]
        sg = _fast_sigmoid(uc[:, CH:])
        dz = dz_ref[...]
        du_ref[:, :CH] = (dz * sg).astype(BF16)
        du_ref[:, CH:] = (dz * a * sg * (1.0 - sg)).astype(BF16)

    cur = lambda c: pl.BlockSpec((tm, c), lambda i: (i, 0))
    nxt = lambda c: pl.BlockSpec((HALO, c), lambda i: (jnp.minimum((i + 1) * hb, T // HALO - 1), 0))
    vec = pl.BlockSpec((1, CH), lambda i: (0, 0))
    return pl.pallas_call(
        body, name="conv_bwd", grid=(nt,),
        in_specs=[cur(CH), nxt(CH), cur(CH), nxt(CH), cur(2 * CH),
                  pl.BlockSpec((HALO, 2 * CH), lambda i: (jnp.maximum(i * hb - 1, 0), 0)),
                  pl.BlockSpec((CONV_W, CH), lambda i: (0, 0)), vec, vec],
        out_specs=[pl.BlockSpec((tm, 2 * CH), lambda i: (i, 0)), pl.BlockSpec((32, CH), lambda i: (0, 0)),
                   pl.BlockSpec((8, CH), lambda i: (0, 0))],
        out_shape=[jax.ShapeDtypeStruct((T, 2 * CH), BF16), jax.ShapeDtypeStruct((32, CH), F32),
                   jax.ShapeDtypeStruct((8, CH), F32)],
        scratch_shapes=[pltpu.VMEM((n, CH), F32), pltpu.VMEM((7, n - 8, CH), F32),
                        pltpu.VMEM((n, CH), F32), pltpu.VMEM((7, n - 8, CH), F32), pltpu.VMEM((tm, CH), F32),
                        pltpu.VMEM((32, 8, CH), F32)],
        compiler_params=_cp(1),
    )(dconv, dconv, ypre, ypre, u, u, w, lg, lb)


def _attn_bwd(sinks, tab, qkv, dattn):
    T = qkv.shape[0]
    nb = T // WINDOW

    def body(sink_ref, tab_ref, q_ref, kvp_ref, kvc_ref, do_ref, dq_ref, dkv_ref, dsk_ref, carry_ref):
        n = pl.program_id(0)

        @pl.when(n == 0)
        def _():
            dsk_ref[...] = jnp.zeros_like(dsk_ref)
            carry_ref[...] = jnp.zeros_like(carry_ref)

        @pl.when(n < nb)
        def _():
            seen = _first_block_mask(n)
            for g in range(N_KV):
                qs = _stack_heads(q_ref, g)
                dos = _stack_heads(do_ref, g)
                k = _band(kvp_ref, kvc_ref, g * HEAD_DIM)
                v = _band(kvp_ref, kvc_ref, KV_W + g * HEAD_DIM)
                p, ps = _attn_probs(qs, k, tab_ref[g], seen, _sink_col(sink_ref, g))
                dp = _dot_nt(dos, v)
                delta = jnp.sum(p * dp, axis=-1, keepdims=True)
                dsb = (p * (dp - delta)).astype(BF16)
                dsink = -ps * delta
                dqs = _dot(dsb, k) * SCALE
                dk = _dot_tn(dsb, qs) * SCALE
                dv = _dot_tn(p.astype(BF16), dos)
                for i in range(GROUP):
                    h = GROUP * g + i
                    dq_ref[:, h * HEAD_DIM:(h + 1) * HEAD_DIM] = dqs[i * WINDOW:(i + 1) * WINDOW].astype(BF16)
                    dsk_ref[h:h + 1, :] += jnp.sum(dsink[i * WINDOW:(i + 1) * WINDOW], axis=0, keepdims=True)
                for off, d in ((g * HEAD_DIM, dk), (KV_W + g * HEAD_DIM, dv)):
                    dkv_ref[:, off:off + HEAD_DIM] = (carry_ref[:, off:off + HEAD_DIM] + d[:WINDOW]).astype(BF16)
                    carry_ref[:, off:off + HEAD_DIM] = d[WINDOW:]

        @pl.when(n == nb)
        def _():
            dkv_ref[...] = carry_ref[...].astype(BF16)

    last = nb - 1
    return pl.pallas_call(
        body, name="attn_bwd", grid=(nb + 1,),
        in_specs=[pl.BlockSpec(memory_space=pltpu.SMEM),
                  pl.BlockSpec(tab.shape, lambda n: (0, 0, 0)),
                  pl.BlockSpec((WINDOW, ATTN_W), lambda n: (jnp.minimum(n, last), 0)),
                  pl.BlockSpec((WINDOW, 2 * KV_W), lambda n: (jnp.clip(n - 1, 0, last), 2)),
                  pl.BlockSpec((WINDOW, 2 * KV_W), lambda n: (jnp.minimum(n, last), 2)),
                  pl.BlockSpec((WINDOW, ATTN_W), lambda n: (jnp.minimum(n, last), 0))],
        out_specs=[pl.BlockSpec((WINDOW, ATTN_W), lambda n: (jnp.minimum(n, last), 0)),
                   pl.BlockSpec((WINDOW, 2 * KV_W), lambda n: (jnp.maximum(n - 1, 0), 0)),
                   pl.BlockSpec((8, LANES), lambda n: (0, 0))],
        out_shape=[jax.ShapeDtypeStruct((T, ATTN_W), BF16), jax.ShapeDtypeStruct((T, 2 * KV_W), BF16),
                   jax.ShapeDtypeStruct((8, LANES), F32)],
        scratch_shapes=[pltpu.VMEM((WINDOW, 2 * KV_W), F32)],
        compiler_params=_cp(1),
    )(sinks, tab, qkv, qkv, qkv, dattn)


def _pack(arrs):
    flat = jnp.concatenate([a.reshape(-1) for a in arrs])
    pad = -flat.shape[0] % (8 * LANES)
    return jnp.pad(flat, (0, pad)).reshape(1, -1, LANES)


def _unpack(packed, like):
    flat = packed.reshape(-1)
    out, off = [], 0
    for a in like:
        out.append(flat[off:off + a.size].reshape(a.shape))
        off += a.size
    return out


def kernel(x, norm_ffn1, w_ffn1_in, w_ffn1_out, norm_mix, w_in, sinks, w_dw, b_dw, conv_ln_g, conv_ln_b, w_out, norm_ffn2, w_ffn2_in, w_ffn2_out, final_norm, loss_target, m_norm_ffn1, m_w_ffn1_in, m_w_ffn1_out, m_norm_mix, m_w_in, m_sinks, m_w_dw, m_b_dw, m_conv_ln_g, m_conv_ln_b, m_w_out, m_norm_ffn2, m_w_ffn2_in, m_w_ffn2_out, m_final_norm, v_norm_ffn1, v_w_ffn1_in, v_w_ffn1_out, v_norm_mix, v_w_in, v_sinks, v_w_dw, v_b_dw, v_conv_ln_g, v_conv_ln_b, v_w_out, v_norm_ffn2, v_w_ffn2_in, v_w_ffn2_out, v_final_norm):
    L, D = norm_ffn1.shape
    T = x.shape[1]
    FB = w_ffn1_in.shape[2]
    CH = b_dw.shape[1]
    QKV = ATTN_W + 2 * KV_W
    xs = x.reshape(T, D)
    tgt = loss_target.reshape(T, D)
    cx, cy, cc = lax.axis_index("x"), lax.axis_index("y"), lax.axis_index("c")
    chip = 2 * cx + cy
    cidx = cc.reshape(1).astype(jnp.int32)
    tr = lambda a_: jnp.transpose(a_, (0, 2, 1))
    big_w = (w_ffn1_in, w_ffn1_out, tr(w_in), w_out, w_ffn2_in, w_ffn2_out)
    big_m = (m_w_ffn1_in, m_w_ffn1_out, tr(m_w_in), m_w_out, m_w_ffn2_in, m_w_ffn2_out)
    big_v = (v_w_ffn1_in, v_w_ffn1_out, tr(v_w_in), v_w_out, v_w_ffn2_in, v_w_ffn2_out)
    NW = len(big_w) + 1

    def own_slot(a, slots=4, idx=chip):
        return lax.dynamic_update_index_in_dim(lax.empty((slots,) + a.shape, a.dtype), a, idx, 0)

    def shards(l, tok):
        return [own_slot((w_[l] + tok[0, 0]).astype(BF16)) for w_ in big_w] + [own_slot(w_dw[l] + tok[0, 0])]

    def gather_start(lands, tok):
        return _xchg_start("gather_start", [], lands, _gather_plan, tok)

    def gather_arrived(started, after, n, taps):
        _, lands, tok = _xchg_wait("gather_wait", started, 0, n, _gather_plan, after)
        return _xchg_start("gshare_start", [], lands[:-1] if taps else lands, _gshare_plan, tok, "sibling3"), lands[-1]

    def shared_weights(shared, after, n):
        _, mats, tok = _xchg_wait("gshare_wait", shared, 0, n, _gshare_plan, after, "sibling3")
        return mats, tok

    row = lambda a, l: a[l].reshape(1, -1)
    tab = _attn_bias_table()
    NB = len(big_w)

    saved, W = [], []
    zero_tok = jnp.zeros((8, LANES), F32)
    src0 = shards(0, zero_tok)
    started = gather_start(src0[:2], zero_tok)
    rest0 = gather_start(src0[2:], started[-1])
    cast = [None] + [shards(l, rest0[-1]) for l in range(1, L)]
    shared, _ = gather_arrived(started, [xs] + [a_ for c_ in cast[1:] for a_ in c_], 2, False)
    after = [shared[-1]]
    for l in range(L):
        mats, tok = shared_weights(shared, after, 2 if l == 0 else NB)
        started = None
        if l + 1 < L:
            started = gather_start(cast[l + 1], tok)
            tok = started[-1]
        x0 = xs
        x1, gu1 = _ffn_fwd(x0, row(norm_ffn1, l) + tok[0, 0], mats[0], mats[1].reshape(2 * FB, D))
        gm_row = row(norm_mix, l)
        if l == 0:
            shared, gdw = gather_arrived(rest0, [x1], NW - 2, True)
            rest, tok = shared_weights(shared, [shared[-1]], NB - 2)
            mats = list(mats) + list(rest)
            gm_row = gm_row + tok[0, 0]
        g1i, g1o, gi, go, g2i, g2o = mats
        w = dict(f1i=g1i, f1o=g1o.reshape(2 * FB, D), f2i=g2i, f2o=g2o.reshape(2 * FB, D),
                 wit=gi.reshape(-1, D), wo=go.reshape(-1, D),
                 wdw=jnp.transpose(gdw, (1, 0, 2)).reshape(CONV_W, CH))
        W.append(w)
        qkv, u = _mixproj_fwd(x1, gm_row, w["wit"])
        attn = _attn_fwd(row(sinks, l), tab, qkv)
        conv, ypre = _conv_fwd(u, w["wdw"], row(b_dw, l), row(conv_ln_g, l), row(conv_ln_b, l))
        x2 = _mixout_fwd(x1, attn, conv, w["wo"])
        g2_row = row(norm_ffn2, l)
        if started is not None and l > 0:
            shared, gdw = gather_arrived(started, [x2], NW, True)
            g2_row = g2_row + shared[-1][0, 0]
        xs, gu2 = _ffn_fwd(x2, g2_row, w["f2i"], w["f2o"])
        if started is not None and l == 0:
            shared, gdw = gather_arrived(started, [xs], NW, True)
        saved.append((x0, gu1, x1, qkv, u, attn, conv, ypre, x2, gu2))
        after = [xs]

    loss_part, dx, d_final = _loss_head(xs, final_norm.reshape(1, D), tgt)
    loss = lax.psum(loss_part[0, 0], ("x", "y", "c"))

    bufs = [[lax.empty(w_.shape, F32) for _ in range(4)] for w_ in big_w]
    d_n1, d_nm, d_n2 = [None] * L, [None] * L, [None] * L
    d_sk, d_bdw, d_lg, d_lb, d_wdw = [None] * L, [None] * L, [None] * L, [None] * L, [None] * L

    me_idx = 4 * cx + 2 * cy + cc

    def reduce_start(gs):
        lands = []
        for g in gs:
            h = g.shape[1] // 2
            mine = lax.dynamic_slice(g, (chip, cc * h, 0), (1, h, g.shape[2]))[0]
            lands.append(own_slot(mine, 8, me_idx))
        return _xchg_start("rs_start", gs, lands, _rs_plan, zero_tok, "all")

    def share_start(rs_started, after, n):
        _, qs, tok = _xchg_wait("rs_wait", rs_started, n, n, _rs_plan, after, "all")
        return _xchg_start("qshare_start", qs, [lax.empty(q.shape, q.dtype) for q in qs], _whole_plan, tok, "sibling")

    def finish(l, shared, after, idxs):
        q_own, q_sib, _ = _xchg_wait("qshare_wait", shared, len(idxs), len(idxs), _whole_plan, after, "sibling")
        for k, t in enumerate(idxs):
            bufs[t] = _adamw_layer(cidx, q_own[k], q_sib[k], big_w[t], big_m[t], big_v[t], bufs[t], l)

    ALL = list(range(NB))
    EARLY, LATE = ALL[2:], ALL[:2]
    rs_list, shares = [], []
    tok = zero_tok
    for l in reversed(range(L)):
        w = W[l]
        x0, gu1, x1, qkv, u, attn, conv, ypre, x2, gu2 = saved[l]
        dx, d_n2[l], hb, dgu, a, dyb = _ffn_bwd(dx, x2, row(norm_ffn2, l), gu2, w["f2i"], w["f2o"], tok)
        g_f2i, g_f2o = _wgrad_ffn_in(hb, dgu, tok), _wgrad_ffn_out(a, dyb, tok)
        lg_row = row(conv_ln_g, l)
        if len(rs_list) >= 2:
            pl_, st_ = rs_list[-2]
            shares.append((pl_, share_start(st_, [g_f2o], NB)))
            lg_row = lg_row + shares[-1][1][-1][0, 0]
        dyb, dattn, dconv = _mixout_bwd(dx, w["wo"])
        g_wo = _wgrad_cat([attn, conv], [dyb]).reshape(4, -1, D)
        du, dwdw, dvec = _conv_bwd(dconv, ypre, u, w["wdw"], lg_row, row(conv_ln_b, l))
        d_wdw[l], d_bdw[l], d_lg[l], d_lb[l] = dwdw[:CONV_W], dvec[0], dvec[1], dvec[2]
        dq, dkv, dsk = _attn_bwd(row(sinks, l), tab, qkv, dattn)
        d_sk[l] = dsk[:, 0]
        dx, d_nm[l], hb = _mix_rms_bwd(dx, x1, row(norm_mix, l), [dq, dkv, du], w["wit"])
        g_wi = _wgrad_cat([dq, dkv, du], [hb]).reshape(4, -1, D)
        if l == 0:
            rs_early = reduce_start([g_wi, g_wo, g_f2i, g_f2o])
            tok = rs_early[-1]
        dx, d_n1[l], hb, dgu, a, dyb = _ffn_bwd(dx, x0, row(norm_ffn1, l), gu1, w["f1i"], w["f1o"], tok)
        g_f1i, g_f1o = _wgrad_ffn_in(hb, dgu, tok), _wgrad_ffn_out(a, dyb, tok)
        rs_started = reduce_start([g_f1i, g_f1o] if l == 0 else [g_f1i, g_f1o, g_wi, g_wo, g_f2i, g_f2o])
        tok = rs_started[-1]
        rs_list.append((l, rs_started))
    grad_x = dx.reshape(x.shape)

    small_g = [jnp.concatenate(d, axis=0) for d in (d_n1, d_nm, d_n2)] + [d_final, jnp.stack(d_sk)] + \
              [jnp.stack(d) for d in (d_bdw, d_lg, d_lb, d_wdw)]
    packed = _pack(small_g)[0]
    small_started = _xchg_start("small_start", [packed], [own_slot(packed, 8, 4 * cx + 2 * cy + cc)], _slot_plan, tok, "all")

    rs_late = rs_list.pop()[1]
    after = [small_started[-1]]
    if len(rs_list) > len(shares):
        pl_, st_ = rs_list[len(shares)]
        shares.append((pl_, share_start(st_, after, NB)))
        after = [shares[-1][1][-1]]
    if shares:
        finish(*shares.pop(0), after, ALL)
        after = [b_[0] for b_ in bufs]
    sh_early = share_start(rs_early, after, len(EARLY))
    after = [sh_early[-1]]
    sh_late = None
    for l, sh in shares:
        finish(l, sh, after, ALL)
        after = [b_[0] for b_ in bufs]
        if sh_late is None:
            sh_late = share_start(rs_late, after, len(LATE))
            after = [sh_late[-1]]
    if sh_late is None:
        sh_late = share_start(rs_late, after, len(LATE))
        after = [sh_late[-1]]
    _, (slots,), _ = _xchg_wait("small_wait", small_started, 1, 1, _slot_plan, after, "all")
    small_sum = _unpack(_sum_slots(slots), small_g)
    g_wdw = lax.dynamic_slice_in_dim(small_sum[8], chip * w_dw.shape[2], w_dw.shape[2], axis=2)
    small_g = [small_sum[0], small_sum[1], small_sum[2], small_sum[3].reshape(D), small_sum[4],
               small_sum[5], small_sum[6], small_sum[7], g_wdw]
    small_w = (norm_ffn1, norm_mix, norm_ffn2, final_norm, sinks, b_dw, conv_ln_g, conv_ln_b, w_dw)
    small_m = (m_norm_ffn1, m_norm_mix, m_norm_ffn2, m_final_norm, m_sinks, m_b_dw, m_conv_ln_g, m_conv_ln_b, m_w_dw)
    small_v = (v_norm_ffn1, v_norm_mix, v_norm_ffn2, v_final_norm, v_sinks, v_b_dw, v_conv_ln_g, v_conv_ln_b, v_w_dw)
    upd = _adamw(_pack(small_g), _pack(small_w), _pack(small_m), _pack(small_v))
    small_upd = [_unpack(u_, small_w) for u_ in upd]
    finish(0, sh_early, [upd[0]], EARLY)
    finish(0, sh_late, [bufs[t][0] for t in EARLY], LATE)

    order = ("norm_ffn1", "w_ffn1_in", "w_ffn1_out", "norm_mix", "w_in", "sinks", "w_dw", "b_dw", "conv_ln_g",
             "conv_ln_b", "w_out", "norm_ffn2", "w_ffn2_in", "w_ffn2_out", "final_norm")
    small_names = ("norm_ffn1", "norm_mix", "norm_ffn2", "final_norm", "sinks", "b_dw", "conv_ln_g", "conv_ln_b", "w_dw")
    big_names = ("w_ffn1_in", "w_ffn1_out", "w_in", "w_out", "w_ffn2_in", "w_ffn2_out")
    grads, deltas, new_m, new_v = {}, {}, {}, {}
    for i, nme in enumerate(small_names):
        grads[nme], deltas[nme], new_m[nme], new_v[nme] = small_g[i], small_upd[0][i], small_upd[1][i], small_upd[2][i]
    for i, nme in enumerate(big_names):
        grads[nme], deltas[nme], new_m[nme], new_v[nme] = [tr(b_) for b_ in bufs[i]] if nme == "w_in" else bufs[i]
    return (loss, grad_x, *[grads[n] for n in order], *[deltas[n] for n in order],
            *[new_m[n] for n in order], *[new_v[n] for n in order])
```

```python
import jax
import jax.numpy as jnp
from jax import lax
from jax.experimental import pallas as pl
from jax.experimental.pallas import tpu as pltpu

F32, BF16 = jnp.float32, jnp.bfloat16
EPS = 1e-6
NEG_INF = -1e30
HEAD_DIM = 64
N_HEADS = 8
N_KV = 2
GROUP = N_HEADS // N_KV
WINDOW = 128
ATTN_W = N_HEADS * HEAD_DIM
KV_W = N_KV * HEAD_DIM
CONV_W = 31
HALO = 32
CONV_ROWS = 32
SCALE = 1.0 / 8.0
ADAM_LR, ADAM_B1, ADAM_B2, ADAM_EPS, ADAM_WD, ADAM_STEP = 0.001, 0.9, 0.999, 1e-08, 0.01, 10
TM = 512
TM_FFN_BWD = 256
TK_WGRAD = 2048
TM_MIX = 1024
LANES = 128
VMEM_LIMIT = 52 * 1024 * 1024
MESH = pl.DeviceIdType.MESH
ANY = pl.BlockSpec(memory_space=pl.ANY)
HBM = pl.BlockSpec(memory_space=pltpu.HBM)
SEM = pl.BlockSpec(memory_space=pltpu.SEMAPHORE)
VMEM = pl.BlockSpec(memory_space=pltpu.VMEM)
EFFECT = pltpu.SideEffectType.DATAFLOW_SIDE_EFFECTING
TOKEN = jax.ShapeDtypeStruct((8, LANES), F32)


def _cp(n):
    return pltpu.CompilerParams(dimension_semantics=("arbitrary",) * n, vmem_limit_bytes=VMEM_LIMIT)


def _dot(a, b):
    return jnp.dot(a, b, preferred_element_type=F32)


def _dot_nt(a, b):
    return lax.dot_general(a, b, (((1,), (1,)), ((), ())), preferred_element_type=F32)


def _dot_tn(a, b):
    return lax.dot_general(a, b, (((0,), (0,)), ((), ())), preferred_element_type=F32)


def _place():
    x, y, c = lax.axis_index("x"), lax.axis_index("y"), lax.axis_index("c")
    chips = [(1 - x, y), (x, 1 - y), (1 - x, 1 - y)]
    return x, y, c, chips


def _rcopy(src, dst, send_sems, recv_sems, k, dev):
    return pltpu.make_async_remote_copy(src_ref=src, dst_ref=dst, send_sem=send_sems.at[k],
                                        recv_sem=recv_sems.at[k], device_id=dev, device_id_type=MESH)


def _hbm(a):
    return pltpu.with_memory_space_constraint(a, pltpu.HBM)


PEERS = {"chips": 3, "sibling": 1, "sibling3": 3, "all": 7}


def _targets(mode):
    x, y, c, chips = _place()
    b = 2 * x + y
    if mode == "chips":
        return b, c, [((px, py, c), 2 * px + py) for px, py in chips]
    if mode == "sibling":
        return b, c, [((x, y, 1 - c), b)]
    if mode == "sibling3":
        return b, c, [((x, y, 1 - c), 2 * px + py) for px, py in chips]
    flip = lambda v, f: 1 - v if f else v
    devs = [(flip(x, k >> 2 & 1), flip(y, k >> 1 & 1), flip(c, k & 1)) for k in range(1, 8)]
    return 4 * x + 2 * y + c, c, [(d, 4 * d[0] + 2 * d[1] + d[2]) for d in devs]


def _xchg_start(name, srcs, lands, plan, dep, mode="chips"):
    ns, nl, npeer = len(srcs), len(lands), PEERS[mode]

    def body(*refs):
        land = refs[ns:ns + nl]
        src = refs[:ns] if ns else land
        send_sems, recv_sems, token = refs[ns + nl + 1], refs[ns + nl + 2], refs[-1]
        me, c, peers = _targets(mode)
        for t in range(nl):
            for j, (dev, tag) in enumerate(peers):
                s, d, _ = plan(src[t], land[t], t, me, c, tag)
                _rcopy(s, d, send_sems, recv_sems, npeer * t + j, dev).start()
        token[...] = jnp.zeros_like(token)

    arrs = list(srcs) + list(lands)
    return pl.pallas_call(
        body, name=name,
        out_shape=(pltpu.SemaphoreType.DMA((npeer * nl,)), pltpu.SemaphoreType.DMA((npeer * nl,)),
                   *[pltpu.HBM(a.shape, a.dtype) for a in arrs], TOKEN),
        in_specs=[HBM] * (ns + nl) + [ANY], out_specs=(SEM, SEM, *[HBM] * (ns + nl), VMEM),
        input_output_aliases={i: 2 + i for i in range(ns + nl)},
        compiler_params=pltpu.CompilerParams(has_side_effects=EFFECT),
    )(*[_hbm(a) for a in arrs], dep)


def _xchg_wait(name, started, ns, nl, plan, after, mode="chips"):
    send_sems, recv_sems, thru = started[0], started[1], started[2:2 + ns + nl]
    npeer = PEERS[mode]

    def body(*refs):
        land = refs[ns:ns + nl]
        src = refs[:ns] if ns else land
        send_sems, recv_sems, token = refs[ns + nl], refs[ns + nl + 1], refs[-1]
        me, c, peers = _targets(mode)
        for t in range(nl):
            for j, (dev, tag) in enumerate(peers):
                s, _, a = plan(src[t], land[t], t, me, c, tag)
                cp = _rcopy(s, a, send_sems, recv_sems, npeer * t + j, dev)
                cp.wait_send()
                cp.wait_recv()
        token[...] = jnp.zeros_like(token)

    out = pl.pallas_call(
        body, name=name,
        out_shape=(*[pltpu.HBM(a.shape, a.dtype) for a in thru], TOKEN),
        in_specs=[HBM] * (ns + nl) + [SEM, SEM] + [ANY] * len(after), out_specs=(*[HBM] * (ns + nl), VMEM),
        input_output_aliases={i: i for i in range(ns + nl)},
        compiler_params=pltpu.CompilerParams(has_side_effects=EFFECT),
    )(*thru, send_sems, recv_sems, *after)
    return out[:ns], out[ns:ns + nl], out[-1]


def _half(ref_rows, which):
    h = ref_rows // 2
    return pl.ds(which * h, h)


def _gather_plan(src, land, t, b, c, pb):
    if land.shape[1] % 2 == 0:
        hs = _half(land.shape[1], c)
        return land.at[b, hs], land.at[b, hs], land.at[pb, hs]
    return land.at[b], land.at[b], land.at[pb]


def _gshare_plan(src, land, t, b, c, pb):
    return land.at[pb, _half(land.shape[1], c)], land.at[pb, _half(land.shape[1], c)], land.at[pb, _half(land.shape[1], 1 - c)]


def _rs_plan(src, land, t, me, c, tag):
    h = src.shape[1] // 2
    return src.at[tag // 2, pl.ds((tag % 2) * h, h), :], land.at[me], land.at[tag]


def _rows_block(h, cap=512):
    for rb in range(min(h, cap) // 16 * 16, 0, -16):
        if h % rb == 0:
            return rb
    return h


def _whole_plan(src, land, t, me, c, tag):
    return src, land, land


def _slot_plan(src, land, t, me, c, tag):
    return src, land.at[me], land.at[tag]


def _adam_update(gg, w, m, v):
    m2 = ADAM_B1 * m + (1.0 - ADAM_B1) * gg
    v2 = ADAM_B2 * v + (1.0 - ADAM_B2) * (gg * gg)
    mh = m2 / (1.0 - ADAM_B1 ** ADAM_STEP)
    vh = v2 / (1.0 - ADAM_B2 ** ADAM_STEP)
    return -ADAM_LR * (mh / (jnp.sqrt(vh) + ADAM_EPS) + ADAM_WD * w), m2, v2


def _adamw_layer(cidx, q_own, q_sib, w, m, v, bufs, l):
    L, R, C = w.shape
    h = R // 2
    rb = _rows_block(h, 192)
    nr = h // rb

    def body(c_ref, qo_ref, qs_ref, w_ref, m_ref, v_ref, *rest):
        g_ref, d_ref, mo_ref, vo_ref = rest[-4:]
        own = pl.program_id(0) == c_ref[0]

        def update(q_ref):
            gg = q_ref[0].astype(F32)
            for s in range(1, 8):
                gg = gg + q_ref[s].astype(F32)
            g_ref[...] = gg
            d_ref[...], mo_ref[...], vo_ref[...] = _adam_update(gg, w_ref[...], m_ref[...], v_ref[...])

        @pl.when(own)
        def _():
            update(qo_ref)

        @pl.when(jnp.logical_not(own))
        def _():
            update(qs_ref)

    q_own_spec = pl.BlockSpec(
        (8, rb, C), lambda hh, i, c: (0, jnp.where(hh == c[0], i, jnp.where(hh < c[0], 0, nr - 1)), 0))
    q_sib_spec = pl.BlockSpec(
        (8, rb, C), lambda hh, i, c: (0, jnp.where(hh != c[0], i, jnp.where(hh < 1 - c[0], 0, nr - 1)), 0))
    wspec = pl.BlockSpec((None, rb, C), lambda hh, i, c: (l, hh * nr + i, 0))
    return pl.pallas_call(
        body, name="adamw_layer", out_shape=[jax.ShapeDtypeStruct(w.shape, F32)] * 4,
        grid_spec=pltpu.PrefetchScalarGridSpec(
            num_scalar_prefetch=1, grid=(2, nr),
            in_specs=[q_own_spec, q_sib_spec, wspec, wspec, wspec] + [ANY] * 4, out_specs=[wspec] * 4),
        input_output_aliases={6 + k: k for k in range(4)},
        compiler_params=_cp(2),
    )(cidx, q_own, q_sib, w, m, v, *bufs)


def _adamw(g, w, m, v):
    L, R, C = g.shape
    rb = _rows_block(R)

    def body(g_ref, w_ref, m_ref, v_ref, d_ref, mo_ref, vo_ref):
        d_ref[...], mo_ref[...], vo_ref[...] = _adam_update(g_ref[...], w_ref[...], m_ref[...], v_ref[...])

    spec = pl.BlockSpec((None, rb, C), lambda l, i: (l, i, 0))
    return pl.pallas_call(
        body, name="adamw", grid=(L, R // rb), in_specs=[spec] * 4, out_specs=[spec] * 3,
        out_shape=[jax.ShapeDtypeStruct(g.shape, F32)] * 3, compiler_params=_cp(2),
    )(g, w, m, v)


def _sum_slots(buf):
    def body(b_ref, o_ref):
        acc = b_ref[0]
        for k in range(1, 8):
            acc = acc + b_ref[k]
        o_ref[...] = acc

    return pl.pallas_call(body, name="sum_slots", in_specs=[VMEM], out_specs=VMEM,
                          out_shape=jax.ShapeDtypeStruct(buf.shape[1:], F32))(buf)


def _rms(xf, g):
    r = lax.rsqrt(jnp.mean(xf * xf, axis=-1, keepdims=True) + EPS)
    return xf * r, r


def _lane_chunks(n):
    lo = (n // LANES + 1) // 2 * LANES
    return ((0, lo), (lo, n - lo))


def _load_ffn_weights(win_hbm, wout_hbm, win_v, wout_v, sems):
    fb = win_v.shape[2]
    loads = [pltpu.make_async_copy(win_hbm.at[k], win_v.at[k], sems.at[k]) for k in range(4)]
    loads += [pltpu.make_async_copy(wout_hbm.at[pl.ds(k * fb, fb)], wout_v.at[pl.ds(k * fb, fb)], sems.at[4 + k])
              for k in range(2)]
    for cp in loads:
        cp.start()
    for cp in loads:
        cp.wait()


def _fast_sigmoid(v):
    return pl.reciprocal(1.0 + jnp.exp(-v), approx=True)


def _ffn_fwd(x, g, win, wout):
    T, D = x.shape
    FB = win.shape[2]
    tm = min(TM, T)

    def body(x_ref, g_ref, win_hbm, wout_hbm, xo_ref, gu_ref, win_v, wout_v, sems):
        @pl.when(pl.program_id(0) == 0)
        def _():
            _load_ffn_weights(win_hbm, wout_hbm, win_v, wout_v, sems)

        xf = x_ref[...]
        xh, _ = _rms(xf, None)
        h = (xh * g_ref[...]).astype(BF16)
        acc = jnp.zeros((tm, D), F32)
        for blk in range(2):
            for lo, sz in _lane_chunks(FB):
                cols = pl.ds(blk * FB + lo, sz)
                gate = _dot(h, win_v[blk, :, pl.ds(lo, sz)])
                up = _dot(h, win_v[2 + blk, :, pl.ds(lo, sz)])
                gu_ref[0, :, cols] = gate.astype(BF16)
                gu_ref[1, :, cols] = up.astype(BF16)
                a = (gate * _fast_sigmoid(gate) * up).astype(BF16)
                acc = acc + _dot(a, wout_v[cols, :])
        xo_ref[...] = xf + 0.5 * acc

    row = pl.BlockSpec((tm, D), lambda i: (i, 0))
    return pl.pallas_call(
        body, name="ffn_fwd", grid=(T // tm,),
        in_specs=[row, pl.BlockSpec((1, D), lambda i: (0, 0)), ANY, ANY],
        out_specs=[row, pl.BlockSpec((2, tm, 2 * FB), lambda i: (0, i, 0))],
        out_shape=[jax.ShapeDtypeStruct((T, D), F32), jax.ShapeDtypeStruct((2, T, 2 * FB), BF16)],
        scratch_shapes=[pltpu.VMEM(win.shape, BF16), pltpu.VMEM(wout.shape, BF16), pltpu.SemaphoreType.DMA((6,))],
        compiler_params=_cp(1),
    )(x, g, win, wout)


def _mixproj_fwd(x, g, wt):
    T, D = x.shape
    W = wt.shape[0]
    QKV = ATTN_W + 2 * KV_W
    tm = min(TM_MIX, T)

    def body(x_ref, g_ref, w_ref, qkv_ref, u_ref):
        xh, _ = _rms(x_ref[...], None)
        h = (xh * g_ref[...]).astype(BF16)
        qkv_ref[...] = _dot_nt(h, w_ref[:QKV, :]).astype(BF16)
        u_ref[...] = _dot_nt(h, w_ref[QKV:, :])

    return pl.pallas_call(
        body, name="mixproj_fwd", grid=(T // tm,),
        in_specs=[pl.BlockSpec((tm, D), lambda i: (i, 0)), pl.BlockSpec((1, D), lambda i: (0, 0)),
                  pl.BlockSpec((W, D), lambda i: (0, 0))],
        out_specs=[pl.BlockSpec((tm, QKV), lambda i: (i, 0)), pl.BlockSpec((tm, W - QKV), lambda i: (i, 0))],
        out_shape=[jax.ShapeDtypeStruct((T, QKV), BF16), jax.ShapeDtypeStruct((T, W - QKV), F32)],
        compiler_params=_cp(1),
    )(x, g, wt)


def _attn_bias_table():
    rows, cols = GROUP * WINDOW, 2 * WINDOW
    row = lax.broadcasted_iota(jnp.int32, (N_KV, rows, cols), 1)
    col = lax.broadcasted_iota(jnp.int32, (N_KV, rows, cols), 2)
    head = GROUP * lax.broadcasted_iota(jnp.int32, (N_KV, rows, cols), 0) + (row >> 7)
    dist = (row & (WINDOW - 1)) + WINDOW - col
    slope = jnp.exp2(-(head + 1).astype(F32))
    return jnp.where((dist >= 0) & (dist < WINDOW), -slope * dist.astype(F32), NEG_INF)


def _first_block_mask(n):
    col = lax.broadcasted_iota(jnp.int32, (GROUP * WINDOW, 2 * WINDOW), 1)
    return (n > 0) | (col >= WINDOW)


def _sink_col(sink_ref, g):
    hi = lax.broadcasted_iota(jnp.int32, (GROUP * WINDOW, 1), 0) >> 7
    col = jnp.zeros((GROUP * WINDOW, 1), F32)
    for i in range(GROUP):
        col = jnp.where(hi == i, sink_ref[0, GROUP * g + i], col)
    return col


def _stack_heads(ref, g):
    return jnp.concatenate([ref[:, (GROUP * g + i) * HEAD_DIM:(GROUP * g + i + 1) * HEAD_DIM]
                            for i in range(GROUP)], axis=0)


def _band(kvp_ref, kvc_ref, off):
    return jnp.concatenate([kvp_ref[:, off:off + HEAD_DIM], kvc_ref[:, off:off + HEAD_DIM]], axis=0)


def _attn_probs(qs, k, bias, seen, sink):
    s = jnp.where(seen, _dot_nt(qs, k) * SCALE + bias, NEG_INF)
    m = jnp.maximum(jnp.max(s, axis=-1, keepdims=True), sink)
    p = jnp.exp(s - m)
    es = jnp.exp(sink - m)
    inv = 1.0 / (jnp.sum(p, axis=-1, keepdims=True) + es)
    return p * inv, es * inv


def _attn_fwd(sinks, tab, qkv):
    T = qkv.shape[0]
    nb = T // WINDOW

    def body(sink_ref, tab_ref, q_ref, kvp_ref, kvc_ref, o_ref):
        seen = _first_block_mask(pl.program_id(0))
        for g in range(N_KV):
            qs = _stack_heads(q_ref, g)
            k = _band(kvp_ref, kvc_ref, g * HEAD_DIM)
            v = _band(kvp_ref, kvc_ref, KV_W + g * HEAD_DIM)
            p, _ = _attn_probs(qs, k, tab_ref[g], seen, _sink_col(sink_ref, g))
            o = _dot(p.astype(BF16), v)
            for i in range(GROUP):
                h = GROUP * g + i
                o_ref[:, h * HEAD_DIM:(h + 1) * HEAD_DIM] = o[i * WINDOW:(i + 1) * WINDOW].astype(BF16)

    return pl.pallas_call(
        body, name="attn_fwd", grid=(nb,),
        in_specs=[pl.BlockSpec(memory_space=pltpu.SMEM),
                  pl.BlockSpec(tab.shape, lambda n: (0, 0, 0)),
                  pl.BlockSpec((WINDOW, ATTN_W), lambda n: (n, 0)),
                  pl.BlockSpec((WINDOW, 2 * KV_W), lambda n: (jnp.maximum(n - 1, 0), 2)),
                  pl.BlockSpec((WINDOW, 2 * KV_W), lambda n: (n, 2))],
        out_specs=pl.BlockSpec((WINDOW, ATTN_W), lambda n: (n, 0)),
        out_shape=jax.ShapeDtypeStruct((T, ATTN_W), BF16),
        compiler_params=_cp(1),
    )(sinks, tab, qkv, qkv, qkv)


def _shift_copies(src_ref, dst_ref, n):
    for b in range(1, 8):
        dst_ref[b - 1] = src_ref[b:b + n, :]


def _tap(src_ref, sh_ref, s, c0):
    a, b = divmod(s, 8)
    start = pl.multiple_of(c0 + 8 * a, 8)
    if b == 0:
        return src_ref[pl.ds(start, CONV_ROWS), :]
    return sh_ref[b - 1, pl.ds(start, CONV_ROWS), :]


def _glu_rows(u, ch):
    return u[:, :ch] * _fast_sigmoid(u[:, ch:])


def _fill_z(zs_ref, zsh_ref, uc_ref, up_ref, i, ch, n):
    zs_ref[0:HALO] = jnp.where(i > 0, _glu_rows(up_ref[...], ch), 0.0)
    zs_ref[HALO:] = _glu_rows(uc_ref[...], ch)
    _shift_copies(zs_ref, zsh_ref, n - 8)


def _conv_fwd(u, w, b, lg, lb):
    T = u.shape[0]
    CH = u.shape[1] // 2
    tm = min(TM, T)
    n = tm + HALO
    hb = tm // HALO

    def body(uc_ref, up_ref, w_ref, b_ref, lg_ref, lb_ref, conv_ref, ypre_ref, zs_ref, zsh_ref):
        i = pl.program_id(0)
        _fill_z(zs_ref, zsh_ref, uc_ref, up_ref, i, CH, n)
        bias = b_ref[...]

        def chunk(ci, carry):
            c0 = pl.multiple_of(ci * CONV_ROWS, CONV_ROWS)
            acc = jnp.broadcast_to(bias, (CONV_ROWS, CH))
            for k in range(CONV_W):
                acc = acc + w_ref[k:k + 1, :] * _tap(zs_ref, zsh_ref, HALO - (CONV_W - 1) + k, c0)
            ypre_ref[pl.ds(c0, CONV_ROWS), :] = acc
            return carry

        lax.fori_loop(0, tm // CONV_ROWS, chunk, 0)
        y = ypre_ref[...]
        mu = jnp.mean(y, axis=-1, keepdims=True)
        d = y - mu
        var = jnp.mean(d * d, axis=-1, keepdims=True)
        o = d * lax.rsqrt(var + EPS) * lg_ref[...] + lb_ref[...]
        conv_ref[...] = (o * _fast_sigmoid(o)).astype(BF16)

    vec = pl.BlockSpec((1, CH), lambda i: (0, 0))
    return pl.pallas_call(
        body, name="conv_fwd", grid=(T // tm,),
        in_specs=[pl.BlockSpec((tm, 2 * CH), lambda i: (i, 0)),
                  pl.BlockSpec((HALO, 2 * CH), lambda i: (jnp.maximum(i * hb - 1, 0), 0)),
                  pl.BlockSpec((CONV_W, CH), lambda i: (0, 0)), vec, vec, vec],
        out_specs=[pl.BlockSpec((tm, CH), lambda i: (i, 0)), pl.BlockSpec((tm, CH), lambda i: (i, 0))],
        out_shape=[jax.ShapeDtypeStruct((T, CH), BF16), jax.ShapeDtypeStruct((T, CH), F32)],
        scratch_shapes=[pltpu.VMEM((n, CH), F32), pltpu.VMEM((7, n - 8, CH), F32)],
        compiler_params=_cp(1),
    )(u, u, w, b, lg, lb)


def _mixout_fwd(x, attn, conv, wo):
    T, D = x.shape
    tm = min(TM_MIX, T)
    A = attn.shape[1]

    def body(x_ref, a_ref, c_ref, w_ref, xo_ref):
        xo_ref[...] = x_ref[...] + _dot(a_ref[...], w_ref[:A, :]) + _dot(c_ref[...], w_ref[A:, :])

    return pl.pallas_call(
        body, name="mixout_fwd", grid=(T // tm,),
        in_specs=[pl.BlockSpec((tm, D), lambda i: (i, 0)), pl.BlockSpec((tm, A), lambda i: (i, 0)),
                  pl.BlockSpec((tm, conv.shape[1]), lambda i: (i, 0)), pl.BlockSpec(wo.shape, lambda i: (0, 0))],
        out_specs=pl.BlockSpec((tm, D), lambda i: (i, 0)),
        out_shape=jax.ShapeDtypeStruct((T, D), F32),
        compiler_params=_cp(1),
    )(x, attn, conv, wo)


def _rms_bwd_rows(dh, xf, g):
    xh, r = _rms(xf, None)
    dxn = dh * g
    dx = r * (dxn - xh * jnp.mean(dxn * xh, axis=-1, keepdims=True))
    return dx, jnp.sum(dh * xh, axis=0, keepdims=True), xh * g


def _loss_head(x, g, tgt):
    T, D = x.shape
    tm = min(TM, T)

    def body(x_ref, g_ref, t_ref, loss_ref, dx_ref, dg_ref):
        @pl.when(pl.program_id(0) == 0)
        def _():
            loss_ref[...] = jnp.zeros_like(loss_ref)
            dg_ref[...] = jnp.zeros_like(dg_ref)

        xf = x_ref[...]
        g = g_ref[...]
        xh, _ = _rms(xf, None)
        e = xh * g - t_ref[...]
        loss_ref[...] += 0.5 * jnp.sum(jnp.mean(e * e, axis=-1, keepdims=True), axis=0, keepdims=True)
        dx, dg, _ = _rms_bwd_rows(e * (1.0 / D), xf, g)
        dx_ref[...] = dx
        dg_ref[...] += dg

    return pl.pallas_call(
        body, name="loss_head", grid=(T // tm,),
        in_specs=[pl.BlockSpec((tm, D), lambda i: (i, 0)), pl.BlockSpec((1, D), lambda i: (0, 0)),
                  pl.BlockSpec((tm, D), lambda i: (i, 0))],
        out_specs=[pl.BlockSpec((1, 1), lambda i: (0, 0)), pl.BlockSpec((tm, D), lambda i: (i, 0)),
                   pl.BlockSpec((1, D), lambda i: (0, 0))],
        out_shape=[jax.ShapeDtypeStruct((1, 1), F32), jax.ShapeDtypeStruct((T, D), F32),
                   jax.ShapeDtypeStruct((1, D), F32)],
        compiler_params=_cp(1),
    )(x, g, tgt)


def _ffn_bwd(dxo, x, g, gu, win, wout, dep):
    T, D = x.shape
    FB = win.shape[2]
    tm = min(TM_FFN_BWD, T)

    def body(dxo_ref, x_ref, g_ref, gu_ref, win_hbm, wout_hbm, dep_ref,
             dxi_ref, dg_ref, hb_ref, dgu_ref, a_ref, dyb_ref, win_v, wout_v, sems):
        @pl.when(pl.program_id(0) == 0)
        def _():
            _load_ffn_weights(win_hbm, wout_hbm, win_v, wout_v, sems)
            dg_ref[...] = jnp.zeros_like(dg_ref)

        dyb = (0.5 * dxo_ref[...]).astype(BF16)
        dyb_ref[...] = dyb
        dh = jnp.zeros((tm, D), F32)
        for blk in range(2):
            cols = pl.ds(blk * FB, FB)
            da = _dot_nt(dyb, wout_v[cols, :])
            gate = gu_ref[0, :, cols].astype(F32)
            up = gu_ref[1, :, cols].astype(F32)
            sg = _fast_sigmoid(gate)
            s = gate * sg
            a_ref[:, cols] = (s * up).astype(BF16)
            dgate = (da * up * (sg + s * (1.0 - sg))).astype(BF16)
            dup = (da * s).astype(BF16)
            dgu_ref[0, :, cols] = dgate
            dgu_ref[1, :, cols] = dup
            dh = dh + _dot_nt(dgate, win_v[blk]) + _dot_nt(dup, win_v[2 + blk])
        dx, dg, h = _rms_bwd_rows(dh, x_ref[...], g_ref[...])
        dxi_ref[...] = dxo_ref[...] + dx
        dg_ref[...] += dg
        hb_ref[...] = h.astype(BF16)

    row = pl.BlockSpec((tm, D), lambda i: (i, 0))
    act = pl.BlockSpec((2, tm, 2 * FB), lambda i: (0, i, 0))
    return pl.pallas_call(
        body, name="ffn_bwd", grid=(T // tm,),
        in_specs=[row, row, pl.BlockSpec((1, D), lambda i: (0, 0)), act, ANY, ANY, ANY],
        out_specs=[row, pl.BlockSpec((1, D), lambda i: (0, 0)), row, act,
                   pl.BlockSpec((tm, 2 * FB), lambda i: (i, 0)), row],
        out_shape=[jax.ShapeDtypeStruct((T, D), F32), jax.ShapeDtypeStruct((1, D), F32),
                   jax.ShapeDtypeStruct((T, D), BF16), jax.ShapeDtypeStruct((2, T, 2 * FB), BF16),
                   jax.ShapeDtypeStruct((T, 2 * FB), BF16), jax.ShapeDtypeStruct((T, D), BF16)],
        scratch_shapes=[pltpu.VMEM(win.shape, BF16), pltpu.VMEM(wout.shape, BF16), pltpu.SemaphoreType.DMA((6,))],
        compiler_params=_cp(1),
    )(dxo, x, g, gu, win, wout, dep)


def _mix_rms_bwd(dxo, x, g, dzs, wt):
    T, D = x.shape
    tm = min(TM, T)
    npair = len(dzs)

    def body(*refs):
        dxo_ref, x_ref, g_ref = refs[:3]
        dz_refs, w_ref = refs[3:3 + npair], refs[3 + npair]
        dxi_ref, dg_ref, hb_ref = refs[4 + npair:]

        @pl.when(pl.program_id(0) == 0)
        def _():
            dg_ref[...] = jnp.zeros_like(dg_ref)

        dh = jnp.zeros((tm, D), F32)
        k0 = 0
        for dz_ref in dz_refs:
            kp = dz_ref.shape[1]
            dh = dh + _dot(dz_ref[...], w_ref[k0:k0 + kp, :])
            k0 += kp
        dx, dg, h = _rms_bwd_rows(dh, x_ref[...], g_ref[...])
        dxi_ref[...] = dxo_ref[...] + dx
        dg_ref[...] += dg
        hb_ref[...] = h.astype(BF16)

    row = pl.BlockSpec((tm, D), lambda i: (i, 0))
    return pl.pallas_call(
        body, name="mix_rms_bwd", grid=(T // tm,),
        in_specs=[row, row, pl.BlockSpec((1, D), lambda i: (0, 0))]
                 + [pl.BlockSpec((tm, dz.shape[1]), lambda i: (i, 0)) for dz in dzs]
                 + [pl.BlockSpec(wt.shape, lambda i: (0, 0))],
        out_specs=[row, pl.BlockSpec((1, D), lambda i: (0, 0)), row],
        out_shape=[jax.ShapeDtypeStruct((T, D), F32), jax.ShapeDtypeStruct((1, D), F32),
                   jax.ShapeDtypeStruct((T, D), BF16)],
        compiler_params=_cp(1),
    )(dxo, x, g, *dzs, wt)


def _wgrad(name, a, b, a_spec, b_spec, out_shape, out_spec, nblk, dep, acc_shape):
    T = a.shape[0]
    tk = min(TK_WGRAD, T)
    nk = T // tk

    def body(a_ref, b_ref, dep_ref, o_ref, acc_ref):
        k = pl.program_id(1)

        @pl.when(k == 0)
        def _():
            acc_ref[...] = jnp.zeros_like(acc_ref)

        acc_ref[...] += _dot_tn(a_ref[...], b_ref[...])

        @pl.when(k == nk - 1)
        def _():
            o_ref[...] = acc_ref[...].reshape(o_ref.shape).astype(BF16)

    return pl.pallas_call(
        body, name=name, grid=(nblk, nk), in_specs=[a_spec, b_spec, ANY], out_specs=out_spec,
        out_shape=jax.ShapeDtypeStruct(out_shape, BF16), scratch_shapes=[pltpu.VMEM(acc_shape, F32)],
        compiler_params=_cp(2),
    )(a, b, dep)


def _wgrad_ffn_in(hb, dgu, dep):
    T, D = hb.shape
    FB = dgu.shape[2] // 2
    tk = min(TK_WGRAD, T)
    return _wgrad("wgrad_ffn_in", hb, dgu,
                  pl.BlockSpec((tk, D), lambda b, k: (k, 0)),
                  pl.BlockSpec((None, tk, FB), lambda b, k: (b // 2, k, b % 2)),
                  (4, D, FB), pl.BlockSpec((None, D, FB), lambda b, k: (b, 0, 0)), 4, dep, (D, FB))


def _wgrad_ffn_out(a, dyb, dep):
    T, D = dyb.shape
    FB = a.shape[1] // 2
    tk = min(TK_WGRAD, T)
    return _wgrad("wgrad_ffn_out", a, dyb,
                  pl.BlockSpec((tk, FB), lambda b, k: (k, b)),
                  pl.BlockSpec((tk, D), lambda b, k: (k, 0)),
                  (4, FB // 2, D), pl.BlockSpec((2, FB // 2, D), lambda b, k: (b, 0, 0)), 2, dep, (FB, D))


def _wgrad_cat(a_list, b_list):
    T = a_list[0].shape[0]
    tk = min(TK_WGRAD, T)
    nk = T // tk
    na = len(a_list)
    M, N = sum(a.shape[1] for a in a_list), sum(b.shape[1] for b in b_list)

    def body(*refs):
        a_refs, b_refs, o_ref, acc_ref = refs[:na], refs[na:-2], refs[-2], refs[-1]
        k = pl.program_id(0)

        @pl.when(k == 0)
        def _():
            acc_ref[...] = jnp.zeros_like(acc_ref)

        r0 = 0
        for a_ref in a_refs:
            c0 = 0
            for b_ref in b_refs:
                m, n = a_ref.shape[1], b_ref.shape[1]
                acc_ref[r0:r0 + m, c0:c0 + n] += _dot_tn(a_ref[...], b_ref[...])
                c0 += n
            r0 += a_ref.shape[1]

        @pl.when(k == nk - 1)
        def _():
            o_ref[...] = acc_ref[...].astype(BF16)

    return pl.pallas_call(
        body, name="wgrad_cat", grid=(nk,),
        in_specs=[pl.BlockSpec((tk, v.shape[1]), lambda k: (k, 0)) for v in list(a_list) + list(b_list)],
        out_specs=pl.BlockSpec((M, N), lambda k: (0, 0)),
        out_shape=jax.ShapeDtypeStruct((M, N), BF16), scratch_shapes=[pltpu.VMEM((M, N), F32)],
        compiler_params=_cp(1),
    )(*a_list, *b_list)


def _mixout_bwd(dxo, wo):
    T, D = dxo.shape
    tm = min(TM_MIX, T)
    A = ATTN_W
    C = wo.shape[0] - A

    def body(dxo_ref, w_ref, dyb_ref, da_ref, dc_ref):
        dyb = dxo_ref[...].astype(BF16)
        dyb_ref[...] = dyb
        da_ref[...] = _dot_nt(dyb, w_ref[:A, :]).astype(BF16)
        dc_ref[...] = _dot_nt(dyb, w_ref[A:, :])

    return pl.pallas_call(
        body, name="mixout_bwd", grid=(T // tm,),
        in_specs=[pl.BlockSpec((tm, D), lambda i: (i, 0)), pl.BlockSpec(wo.shape, lambda i: (0, 0))],
        out_specs=[pl.BlockSpec((tm, D), lambda i: (i, 0)), pl.BlockSpec((tm, A), lambda i: (i, 0)),
                   pl.BlockSpec((tm, C), lambda i: (i, 0))],
        out_shape=[jax.ShapeDtypeStruct((T, D), BF16), jax.ShapeDtypeStruct((T, A), BF16),
                   jax.ShapeDtypeStruct((T, C), F32)],
        compiler_params=_cp(1),
    )(dxo, wo)


def _conv_bwd(dconv, ypre, u, w, lg, lb):
    T, CH = dconv.shape
    tm = min(TM, T)
    n = tm + HALO
    hb = tm // HALO
    nt = T // tm
    nchunk = tm // CONV_ROWS

    def body(dc_ref, dcn_ref, yp_ref, ypn_ref, uc_ref, up_ref, w_ref, lg_ref, lb_ref,
             du_ref, dw_ref, dvec_ref, zs_ref, zsh_ref, dy_ref, dysh_ref, dz_ref, dwacc_ref):
        i = pl.program_id(0)

        @pl.when(i == 0)
        def _():
            dwacc_ref[...] = jnp.zeros_like(dwacc_ref)
            dvec_ref[...] = jnp.zeros_like(dvec_ref)

        g, bb = lg_ref[...], lb_ref[...]

        def ln_bwd(dc, yp):
            mu = jnp.mean(yp, axis=-1, keepdims=True)
            d = yp - mu
            rs = lax.rsqrt(jnp.mean(d * d, axis=-1, keepdims=True) + EPS)
            yn = d * rs
            o = yn * g + bb
            sg = _fast_sigmoid(o)
            do = dc * (sg * (1.0 + o * (1.0 - sg)))
            dyn = do * g
            dyp = rs * (dyn - jnp.mean(dyn, axis=-1, keepdims=True)
                        - yn * jnp.mean(dyn * yn, axis=-1, keepdims=True))
            return dyp, do, yn

        dyp, do, yn = ln_bwd(dc_ref[...], yp_ref[...])
        dvec_ref[0:1, :] += jnp.sum(dyp, axis=0, keepdims=True)
        dvec_ref[1:2, :] += jnp.sum(do * yn, axis=0, keepdims=True)
        dvec_ref[2:3, :] += jnp.sum(do, axis=0, keepdims=True)
        dy_ref[0:tm] = dyp
        dyh, _, _ = ln_bwd(dcn_ref[...], ypn_ref[...])
        dy_ref[tm:] = jnp.where(i < nt - 1, dyh, 0.0)
        _shift_copies(dy_ref, dysh_ref, n - 8)
        _fill_z(zs_ref, zsh_ref, uc_ref, up_ref, i, CH, n)

        def chunk(ci, carry):
            c0 = pl.multiple_of(ci * CONV_ROWS, CONV_ROWS)
            acc = jnp.zeros((CONV_ROWS, CH), F32)
            for k in range(CONV_W):
                acc = acc + w_ref[k:k + 1, :] * _tap(dy_ref, dysh_ref, CONV_W - 1 - k, c0)
            dz_ref[pl.ds(c0, CONV_ROWS), :] = acc
            dyc = dy_ref[pl.ds(c0, CONV_ROWS), :]
            for k in range(CONV_W):
                prod = dyc * _tap(zs_ref, zsh_ref, HALO - (CONV_W - 1) + k, c0)
                dwacc_ref[k] += jnp.sum(prod.reshape(CONV_ROWS // 8, 8, CH), axis=0)
            return carry

        lax.fori_loop(0, nchunk, chunk, 0)

        @pl.when(i == nt - 1)
        def _():
            dw_ref[...] = jnp.sum(dwacc_ref[...], axis=1)

        uc = uc_ref[...]
        a = uc[:, :CH]
        sg = _fast_sigmoid(uc[:, CH:])
        dz = dz_ref[...]
        du_ref[:, :CH] = (dz * sg).astype(BF16)
        du_ref[:, CH:] = (dz * a * sg * (1.0 - sg)).astype(BF16)

    cur = lambda c: pl.BlockSpec((tm, c), lambda i: (i, 0))
    nxt = lambda c: pl.BlockSpec((HALO, c), lambda i: (jnp.minimum((i + 1) * hb, T // HALO - 1), 0))
    vec = pl.BlockSpec((1, CH), lambda i: (0, 0))
    return pl.pallas_call(
        body, name="conv_bwd", grid=(nt,),
        in_specs=[cur(CH), nxt(CH), cur(CH), nxt(CH), cur(2 * CH),
                  pl.BlockSpec((HALO, 2 * CH), lambda i: (jnp.maximum(i * hb - 1, 0), 0)),
                  pl.BlockSpec((CONV_W, CH), lambda i: (0, 0)), vec, vec],
        out_specs=[pl.BlockSpec((tm, 2 * CH), lambda i: (i, 0)), pl.BlockSpec((32, CH), lambda i: (0, 0)),
                   pl.BlockSpec((8, CH), lambda i: (0, 0))],
        out_shape=[jax.ShapeDtypeStruct((T, 2 * CH), BF16), jax.ShapeDtypeStruct((32, CH), F32),
                   jax.ShapeDtypeStruct((8, CH), F32)],
        scratch_shapes=[pltpu.VMEM((n, CH), F32), pltpu.VMEM((7, n - 8, CH), F32),
                        pltpu.VMEM((n, CH), F32), pltpu.VMEM((7, n - 8, CH), F32), pltpu.VMEM((tm, CH), F32),
                        pltpu.VMEM((32, 8, CH), F32)],
        compiler_params=_cp(1),
    )(dconv, dconv, ypre, ypre, u, u, w, lg, lb)


def _attn_bwd(sinks, tab, qkv, dattn):
    T = qkv.shape[0]
    nb = T // WINDOW

    def body(sink_ref, tab_ref, q_ref, kvp_ref, kvc_ref, do_ref, dq_ref, dkv_ref, dsk_ref, carry_ref):
        n = pl.program_id(0)

        @pl.when(n == 0)
        def _():
            dsk_ref[...] = jnp.zeros_like(dsk_ref)
            carry_ref[...] = jnp.zeros_like(carry_ref)

        @pl.when(n < nb)
        def _():
            seen = _first_block_mask(n)
            for g in range(N_KV):
                qs = _stack_heads(q_ref, g)
                dos = _stack_heads(do_ref, g)
                k = _band(kvp_ref, kvc_ref, g * HEAD_DIM)
                v = _band(kvp_ref, kvc_ref, KV_W + g * HEAD_DIM)
                p, ps = _attn_probs(qs, k, tab_ref[g], seen, _sink_col(sink_ref, g))
                dp = _dot_nt(dos, v)
                delta = jnp.sum(p * dp, axis=-1, keepdims=True)
                dsb = (p * (dp - delta)).astype(BF16)
                dsink = -ps * delta
                dqs = _dot(dsb, k) * SCALE
                dk = _dot_tn(dsb, qs) * SCALE
                dv = _dot_tn(p.astype(BF16), dos)
                for i in range(GROUP):
                    h = GROUP * g + i
                    dq_ref[:, h * HEAD_DIM:(h + 1) * HEAD_DIM] = dqs[i * WINDOW:(i + 1) * WINDOW].astype(BF16)
                    dsk_ref[h:h + 1, :] += jnp.sum(dsink[i * WINDOW:(i + 1) * WINDOW], axis=0, keepdims=True)
                for off, d in ((g * HEAD_DIM, dk), (KV_W + g * HEAD_DIM, dv)):
                    dkv_ref[:, off:off + HEAD_DIM] = (carry_ref[:, off:off + HEAD_DIM] + d[:WINDOW]).astype(BF16)
                    carry_ref[:, off:off + HEAD_DIM] = d[WINDOW:]

        @pl.when(n == nb)
        def _():
            dkv_ref[...] = carry_ref[...].astype(BF16)

    last = nb - 1
    return pl.pallas_call(
        body, name="attn_bwd", grid=(nb + 1,),
        in_specs=[pl.BlockSpec(memory_space=pltpu.SMEM),
                  pl.BlockSpec(tab.shape, lambda n: (0, 0, 0)),
                  pl.BlockSpec((WINDOW, ATTN_W), lambda n: (jnp.minimum(n, last), 0)),
                  pl.BlockSpec((WINDOW, 2 * KV_W), lambda n: (jnp.clip(n - 1, 0, last), 2)),
                  pl.BlockSpec((WINDOW, 2 * KV_W), lambda n: (jnp.minimum(n, last), 2)),
                  pl.BlockSpec((WINDOW, ATTN_W), lambda n: (jnp.minimum(n, last), 0))],
        out_specs=[pl.BlockSpec((WINDOW, ATTN_W), lambda n: (jnp.minimum(n, last), 0)),
                   pl.BlockSpec((WINDOW, 2 * KV_W), lambda n: (jnp.maximum(n - 1, 0), 0)),
                   pl.BlockSpec((8, LANES), lambda n: (0, 0))],
        out_shape=[jax.ShapeDtypeStruct((T, ATTN_W), BF16), jax.ShapeDtypeStruct((T, 2 * KV_W), BF16),
                   jax.ShapeDtypeStruct((8, LANES), F32)],
        scratch_shapes=[pltpu.VMEM((WINDOW, 2 * KV_W), F32)],
        compiler_params=_cp(1),
    )(sinks, tab, qkv, qkv, qkv, dattn)


def _pack(arrs):
    flat = jnp.concatenate([a.reshape(-1) for a in arrs])
    pad = -flat.shape[0] % (8 * LANES)
    return jnp.pad(flat, (0, pad)).reshape(1, -1, LANES)


def _unpack(packed, like):
    flat = packed.reshape(-1)
    out, off = [], 0
    for a in like:
        out.append(flat[off:off + a.size].reshape(a.shape))
        off += a.size
    return out


def kernel(x, norm_ffn1, w_ffn1_in, w_ffn1_out, norm_mix, w_in, sinks, w_dw, b_dw, conv_ln_g, conv_ln_b, w_out, norm_ffn2, w_ffn2_in, w_ffn2_out, final_norm, loss_target, m_norm_ffn1, m_w_ffn1_in, m_w_ffn1_out, m_norm_mix, m_w_in, m_sinks, m_w_dw, m_b_dw, m_conv_ln_g, m_conv_ln_b, m_w_out, m_norm_ffn2, m_w_ffn2_in, m_w_ffn2_out, m_final_norm, v_norm_ffn1, v_w_ffn1_in, v_w_ffn1_out, v_norm_mix, v_w_in, v_sinks, v_w_dw, v_b_dw, v_conv_ln_g, v_conv_ln_b, v_w_out, v_norm_ffn2, v_w_ffn2_in, v_w_ffn2_out, v_final_norm):
    L, D = norm_ffn1.shape
    T = x.shape[1]
    FB = w_ffn1_in.shape[2]
    CH = b_dw.shape[1]
    QKV = ATTN_W + 2 * KV_W
    xs = x.reshape(T, D)
    tgt = loss_target.reshape(T, D)
    cx, cy, cc = lax.axis_index("x"), lax.axis_index("y"), lax.axis_index("c")
    chip = 2 * cx + cy
    cidx = cc.reshape(1).astype(jnp.int32)
    tr = lambda a_: jnp.transpose(a_, (0, 2, 1))
    big_w = (w_ffn1_in, w_ffn1_out, tr(w_in), w_out, w_ffn2_in, w_ffn2_out)
    big_m = (m_w_ffn1_in, m_w_ffn1_out, tr(m_w_in), m_w_out, m_w_ffn2_in, m_w_ffn2_out)
    big_v = (v_w_ffn1_in, v_w_ffn1_out, tr(v_w_in), v_w_out, v_w_ffn2_in, v_w_ffn2_out)
    NW = len(big_w) + 1

    def own_slot(a, slots=4, idx=chip):
        return lax.dynamic_update_index_in_dim(lax.empty((slots,) + a.shape, a.dtype), a, idx, 0)

    def shards(l, tok):
        return [own_slot((w_[l] + tok[0, 0]).astype(BF16)) for w_ in big_w] + [own_slot(w_dw[l] + tok[0, 0])]

    def gather_start(lands, tok):
        return _xchg_start("gather_start", [], lands, _gather_plan, tok)

    def gather_arrived(started, after, n, taps):
        _, lands, tok = _xchg_wait("gather_wait", started, 0, n, _gather_plan, after)
        return _xchg_start("gshare_start", [], lands[:-1] if taps else lands, _gshare_plan, tok, "sibling3"), lands[-1]

    def shared_weights(shared, after, n):
        _, mats, tok = _xchg_wait("gshare_wait", shared, 0, n, _gshare_plan, after, "sibling3")
        return mats, tok

    row = lambda a, l: a[l].reshape(1, -1)
    tab = _attn_bias_table()
    NB = len(big_w)

    saved, W = [], []
    zero_tok = jnp.zeros((8, LANES), F32)
    src0 = shards(0, zero_tok)
    started = gather_start(src0[:2], zero_tok)
    rest0 = gather_start(src0[2:], started[-1])
    cast = [None] + [shards(l, rest0[-1]) for l in range(1, L)]
    shared, _ = gather_arrived(started, [xs] + [a_ for c_ in cast[1:] for a_ in c_], 2, False)
    after = [shared[-1]]
    for l in range(L):
        mats, tok = shared_weights(shared, after, 2 if l == 0 else NB)
        started = None
        if l + 1 < L:
            started = gather_start(cast[l + 1], tok)
            tok = started[-1]
        x0 = xs
        x1, gu1 = _ffn_fwd(x0, row(norm_ffn1, l) + tok[0, 0], mats[0], mats[1].reshape(2 * FB, D))
        gm_row = row(norm_mix, l)
        if l == 0:
            shared, gdw = gather_arrived(rest0, [x1], NW - 2, True)
            rest, tok = shared_weights(shared, [shared[-1]], NB - 2)
            mats = list(mats) + list(rest)
            gm_row = gm_row + tok[0, 0]
        g1i, g1o, gi, go, g2i, g2o = mats
        w = dict(f1i=g1i, f1o=g1o.reshape(2 * FB, D), f2i=g2i, f2o=g2o.reshape(2 * FB, D),
                 wit=gi.reshape(-1, D), wo=go.reshape(-1, D),
                 wdw=jnp.transpose(gdw, (1, 0, 2)).reshape(CONV_W, CH))
        W.append(w)
        qkv, u = _mixproj_fwd(x1, gm_row, w["wit"])
        attn = _attn_fwd(row(sinks, l), tab, qkv)
        conv, ypre = _conv_fwd(u, w["wdw"], row(b_dw, l), row(conv_ln_g, l), row(conv_ln_b, l))
        x2 = _mixout_fwd(x1, attn, conv, w["wo"])
        g2_row = row(norm_ffn2, l)
        if started is not None and l > 0:
            shared, gdw = gather_arrived(started, [x2], NW, True)
            g2_row = g2_row + shared[-1][0, 0]
        xs, gu2 = _ffn_fwd(x2, g2_row, w["f2i"], w["f2o"])
        if started is not None and l == 0:
            shared, gdw = gather_arrived(started, [xs], NW, True)
        saved.append((x0, gu1, x1, qkv, u, attn, conv, ypre, x2, gu2))
        after = [xs]

    loss_part, dx, d_final = _loss_head(xs, final_norm.reshape(1, D), tgt)
    loss = lax.psum(loss_part[0, 0], ("x", "y", "c"))

    bufs = [[lax.empty(w_.shape, F32) for _ in range(4)] for w_ in big_w]
    d_n1, d_nm, d_n2 = [None] * L, [None] * L, [None] * L
    d_sk, d_bdw, d_lg, d_lb, d_wdw = [None] * L, [None] * L, [None] * L, [None] * L, [None] * L

    me_idx = 4 * cx + 2 * cy + cc

    def reduce_start(gs):
        lands = []
        for g in gs:
            h = g.shape[1] // 2
            mine = lax.dynamic_slice(g, (chip, cc * h, 0), (1, h, g.shape[2]))[0]
            lands.append(own_slot(mine, 8, me_idx))
        return _xchg_start("rs_start", gs, lands, _rs_plan, zero_tok, "all")

    def share_start(rs_started, after, n):
        _, qs, tok = _xchg_wait("rs_wait", rs_started, n, n, _rs_plan, after, "all")
        return _xchg_start("qshare_start", qs, [lax.empty(q.shape, q.dtype) for q in qs], _whole_plan, tok, "sibling")

    def finish(l, shared, after, idxs):
        q_own, q_sib, _ = _xchg_wait("qshare_wait", shared, len(idxs), len(idxs), _whole_plan, after, "sibling")
        for k, t in enumerate(idxs):
            bufs[t] = _adamw_layer(cidx, q_own[k], q_sib[k], big_w[t], big_m[t], big_v[t], bufs[t], l)

    ALL = list(range(NB))
    EARLY, LATE = ALL[2:], ALL[:2]
    rs_list, shares = [], []
    tok = zero_tok
    for l in reversed(range(L)):
        w = W[l]
        x0, gu1, x1, qkv, u, attn, conv, ypre, x2, gu2 = saved[l]
        dx, d_n2[l], hb, dgu, a, dyb = _ffn_bwd(dx, x2, row(norm_ffn2, l), gu2, w["f2i"], w["f2o"], tok)
        g_f2i, g_f2o = _wgrad_ffn_in(hb, dgu, tok), _wgrad_ffn_out(a, dyb, tok)
        lg_row = row(conv_ln_g, l)
        if len(rs_list) >= 2:
            pl_, st_ = rs_list[-2]
            shares.append((pl_, share_start(st_, [g_f2o], NB)))
            lg_row = lg_row + shares[-1][1][-1][0, 0]
        dyb, dattn, dconv = _mixout_bwd(dx, w["wo"])
        g_wo = _wgrad_cat([attn, conv], [dyb]).reshape(4, -1, D)
        du, dwdw, dvec = _conv_bwd(dconv, ypre, u, w["wdw"], lg_row, row(conv_ln_b, l))
        d_wdw[l], d_bdw[l], d_lg[l], d_lb[l] = dwdw[:CONV_W], dvec[0], dvec[1], dvec[2]
        dq, dkv, dsk = _attn_bwd(row(sinks, l), tab, qkv, dattn)
        d_sk[l] = dsk[:, 0]
        dx, d_nm[l], hb = _mix_rms_bwd(dx, x1, row(norm_mix, l), [dq, dkv, du], w["wit"])
        g_wi = _wgrad_cat([dq, dkv, du], [hb]).reshape(4, -1, D)
        if l == 0:
            rs_early = reduce_start([g_wi, g_wo, g_f2i, g_f2o])
            tok = rs_early[-1]
        dx, d_n1[l], hb, dgu, a, dyb = _ffn_bwd(dx, x0, row(norm_ffn1, l), gu1, w["f1i"], w["f1o"], tok)
        g_f1i, g_f1o = _wgrad_ffn_in(hb, dgu, tok), _wgrad_ffn_out(a, dyb, tok)
        rs_started = reduce_start([g_f1i, g_f1o] if l == 0 else [g_f1i, g_f1o, g_wi, g_wo, g_f2i, g_f2o])
        tok = rs_started[-1]
        rs_list.append((l, rs_started))
    grad_x = dx.reshape(x.shape)

    small_g = [jnp.concatenate(d, axis=0) for d in (d_n1, d_nm, d_n2)] + [d_final, jnp.stack(d_sk)] + \
              [jnp.stack(d) for d in (d_bdw, d_lg, d_lb, d_wdw)]
    packed = _pack(small_g)[0]
    small_started = _xchg_start("small_start", [packed], [own_slot(packed, 8, 4 * cx + 2 * cy + cc)], _slot_plan, tok, "all")

    rs_late = rs_list.pop()[1]
    after = [small_started[-1]]
    if len(rs_list) > len(shares):
        pl_, st_ = rs_list[len(shares)]
        shares.append((pl_, share_start(st_, after, NB)))
        after = [shares[-1][1][-1]]
    if shares:
        finish(*shares.pop(0), after, ALL)
        after = [b_[0] for b_ in bufs]
    sh_early = share_start(rs_early, after, len(EARLY))
    after = [sh_early[-1]]
    sh_late = None
    for l, sh in shares:
        finish(l, sh, after, ALL)
        after = [b_[0] for b_ in bufs]
        if sh_late is None:
            sh_late = share_start(rs_late, after, len(LATE))
            after = [sh_late[-1]]
    if sh_late is None:
        sh_late = share_start(rs_late, after, len(LATE))
        after = [sh_late[-1]]
    _, (slots,), _ = _xchg_wait("small_wait", small_started, 1, 1, _slot_plan, after, "all")
    small_sum = _unpack(_sum_slots(slots), small_g)
    g_wdw = lax.dynamic_slice_in_dim(small_sum[8], chip * w_dw.shape[2], w_dw.shape[2], axis=2)
    small_g = [small_sum[0], small_sum[1], small_sum[2], small_sum[3].reshape(D), small_sum[4],
               small_sum[5], small_sum[6], small_sum[7], g_wdw]
    small_w = (norm_ffn1, norm_mix, norm_ffn2, final_norm, sinks, b_dw, conv_ln_g, conv_ln_b, w_dw)
    small_m = (m_norm_ffn1, m_norm_mix, m_norm_ffn2, m_final_norm, m_sinks, m_b_dw, m_conv_ln_g, m_conv_ln_b, m_w_dw)
    small_v = (v_norm_ffn1, v_norm_mix, v_norm_ffn2, v_final_norm, v_sinks, v_b_dw, v_conv_ln_g, v_conv_ln_b, v_w_dw)
    upd = _adamw(_pack(small_g), _pack(small_w), _pack(small_m), _pack(small_v))
    small_upd = [_unpack(u_, small_w) for u_ in upd]
    finish(0, sh_early, [upd[0]], EARLY)
    finish(0, sh_late, [bufs[t][0] for t in EARLY], LATE)

    order = ("norm_ffn1", "w_ffn1_in", "w_ffn1_out", "norm_mix", "w_in", "sinks", "w_dw", "b_dw", "conv_ln_g",
             "conv_ln_b", "w_out", "norm_ffn2", "w_ffn2_in", "w_ffn2_out", "final_norm")
    small_names = ("norm_ffn1", "norm_mix", "norm_ffn2", "final_norm", "sinks", "b_dw", "conv_ln_g", "conv_ln_b", "w_dw")
    big_names = ("w_ffn1_in", "w_ffn1_out", "w_in", "w_out", "w_ffn2_in", "w_ffn2_out")
    grads, deltas, new_m, new_v = {}, {}, {}, {}
    for i, nme in enumerate(small_names):
        grads[nme], deltas[nme], new_m[nme], new_v[nme] = small_g[i], small_upd[0][i], small_upd[1][i], small_upd[2][i]
    for i, nme in enumerate(big_names):
        grads[nme], deltas[nme], new_m[nme], new_v[nme] = [tr(b_) for b_ in bufs[i]] if nme == "w_in" else bufs[i]
    return (loss, grad_x, *[grads[n] for n in order], *[deltas[n] for n in order],
            *[new_m[n] for n in order], *[new_v[n] for n in order])
```

```python
import jax
import jax.numpy as jnp
from jax import lax
from jax.experimental import pallas as pl
from jax.experimental.pallas import tpu as pltpu

F32, BF16 = jnp.float32, jnp.bfloat16
EPS = 1e-6
NEG_INF = -1e30
HEAD_DIM = 64
N_HEADS = 8
N_KV = 2
GROUP = N_HEADS // N_KV
WINDOW = 128
ATTN_W = N_HEADS * HEAD_DIM
KV_W = N_KV * HEAD_DIM
CONV_W = 31
HALO = 32
CONV_ROWS = 32
SCALE = 1.0 / 8.0
ADAM_LR, ADAM_B1, ADAM_B2, ADAM_EPS, ADAM_WD, ADAM_STEP = 0.001, 0.9, 0.999, 1e-08, 0.01, 10
TM = 512
TM_FFN_BWD = 256
TK_WGRAD = 2048
TM_MIX = 1024
LANES = 128
VMEM_LIMIT = 52 * 1024 * 1024
MESH = pl.DeviceIdType.MESH
ANY = pl.BlockSpec(memory_space=pl.ANY)
HBM = pl.BlockSpec(memory_space=pltpu.HBM)
SEM = pl.BlockSpec(memory_space=pltpu.SEMAPHORE)
VMEM = pl.BlockSpec(memory_space=pltpu.VMEM)
EFFECT = pltpu.SideEffectType.DATAFLOW_SIDE_EFFECTING
TOKEN = jax.ShapeDtypeStruct((8, LANES), F32)


def _cp(n):
    return pltpu.CompilerParams(dimension_semantics=("arbitrary",) * n, vmem_limit_bytes=VMEM_LIMIT)


def _dot(a, b):
    return jnp.dot(a, b, preferred_element_type=F32)


def _dot_nt(a, b):
    return lax.dot_general(a, b, (((1,), (1,)), ((), ())), preferred_element_type=F32)


def _dot_tn(a, b):
    return lax.dot_general(a, b, (((0,), (0,)), ((), ())), preferred_element_type=F32)


def _place():
    x, y, c = lax.axis_index("x"), lax.axis_index("y"), lax.axis_index("c")
    chips = [(1 - x, y), (x, 1 - y), (1 - x, 1 - y)]
    return x, y, c, chips


def _rcopy(src, dst, send_sems, recv_sems, k, dev):
    return pltpu.make_async_remote_copy(src_ref=src, dst_ref=dst, send_sem=send_sems.at[k],
                                        recv_sem=recv_sems.at[k], device_id=dev, device_id_type=MESH)


def _hbm(a):
    return pltpu.with_memory_space_constraint(a, pltpu.HBM)


PEERS = {"chips": 3, "sibling": 1, "sibling3": 3, "all": 7}


def _targets(mode):
    x, y, c, chips = _place()
    b = 2 * x + y
    if mode == "chips":
        return b, c, [((px, py, c), 2 * px + py) for px, py in chips]
    if mode == "sibling":
        return b, c, [((x, y, 1 - c), b)]
    if mode == "sibling3":
        return b, c, [((x, y, 1 - c), 2 * px + py) for px, py in chips]
    flip = lambda v, f: 1 - v if f else v
    devs = [(flip(x, k >> 2 & 1), flip(y, k >> 1 & 1), flip(c, k & 1)) for k in range(1, 8)]
    return 4 * x + 2 * y + c, c, [(d, 4 * d[0] + 2 * d[1] + d[2]) for d in devs]


def _xchg_start(name, srcs, lands, plan, dep, mode="chips"):
    ns, nl, npeer = len(srcs), len(lands), PEERS[mode]

    def body(*refs):
        land = refs[ns:ns + nl]
        src = refs[:ns] if ns else land
        send_sems, recv_sems, token = refs[ns + nl + 1], refs[ns + nl + 2], refs[-1]
        me, c, peers = _targets(mode)
        for t in range(nl):
            for j, (dev, tag) in enumerate(peers):
                s, d, _ = plan(src[t], land[t], t, me, c, tag)
                _rcopy(s, d, send_sems, recv_sems, npeer * t + j, dev).start()
        token[...] = jnp.zeros_like(token)

    arrs = list(srcs) + list(lands)
    return pl.pallas_call(
        body, name=name,
        out_shape=(pltpu.SemaphoreType.DMA((npeer * nl,)), pltpu.SemaphoreType.DMA((npeer * nl,)),
                   *[pltpu.HBM(a.shape, a.dtype) for a in arrs], TOKEN),
        in_specs=[HBM] * (ns + nl) + [ANY], out_specs=(SEM, SEM, *[HBM] * (ns + nl), VMEM),
        input_output_aliases={i: 2 + i for i in range(ns + nl)},
        compiler_params=pltpu.CompilerParams(has_side_effects=EFFECT),
    )(*[_hbm(a) for a in arrs], dep)


def _xchg_wait(name, started, ns, nl, plan, after, mode="chips"):
    send_sems, recv_sems, thru = started[0], started[1], started[2:2 + ns + nl]
    npeer = PEERS[mode]

    def body(*refs):
        land = refs[ns:ns + nl]
        src = refs[:ns] if ns else land
        send_sems, recv_sems, token = refs[ns + nl], refs[ns + nl + 1], refs[-1]
        me, c, peers = _targets(mode)
        for t in range(nl):
            for j, (dev, tag) in enumerate(peers):
                s, _, a = plan(src[t], land[t], t, me, c, tag)
                cp = _rcopy(s, a, send_sems, recv_sems, npeer * t + j, dev)
                cp.wait_send()
                cp.wait_recv()
        token[...] = jnp.zeros_like(token)

    out = pl.pallas_call(
        body, name=name,
        out_shape=(*[pltpu.HBM(a.shape, a.dtype) for a in thru], TOKEN),
        in_specs=[HBM] * (ns + nl) + [SEM, SEM] + [ANY] * len(after), out_specs=(*[HBM] * (ns + nl), VMEM),
        input_output_aliases={i: i for i in range(ns + nl)},
        compiler_params=pltpu.CompilerParams(has_side_effects=EFFECT),
    )(*thru, send_sems, recv_sems, *after)
    return out[:ns], out[ns:ns + nl], out[-1]


def _half(ref_rows, which):
    h = ref_rows // 2
    return pl.ds(which * h, h)


def _gather_plan(src, land, t, b, c, pb):
    if land.shape[1] % 2 == 0:
        hs = _half(land.shape[1], c)
        return land.at[b, hs], land.at[b, hs], land.at[pb, hs]
    return land.at[b], land.at[b], land.at[pb]


def _gshare_plan(src, land, t, b, c, pb):
    return land.at[pb, _half(land.shape[1], c)], land.at[pb, _half(land.shape[1], c)], land.at[pb, _half(land.shape[1], 1 - c)]


def _rs_plan(src, land, t, me, c, tag):
    h = src.shape[1] // 2
    return src.at[tag // 2, pl.ds((tag % 2) * h, h), :], land.at[me], land.at[tag]


def _rows_block(h, cap=512):
    for rb in range(min(h, cap) // 16 * 16, 0, -16):
        if h % rb == 0:
            return rb
    return h


def _whole_plan(src, land, t, me, c, tag):
    return src, land, land


def _slot_plan(src, land, t, me, c, tag):
    return src, land.at[me], land.at[tag]


def _adam_update(gg, w, m, v):
    m2 = ADAM_B1 * m + (1.0 - ADAM_B1) * gg
    v2 = ADAM_B2 * v + (1.0 - ADAM_B2) * (gg * gg)
    mh = m2 / (1.0 - ADAM_B1 ** ADAM_STEP)
    vh = v2 / (1.0 - ADAM_B2 ** ADAM_STEP)
    return -ADAM_LR * (mh / (jnp.sqrt(vh) + ADAM_EPS) + ADAM_WD * w), m2, v2


def _adamw_layer(cidx, q_own, q_sib, w, m, v, bufs, l):
    L, R, C = w.shape
    h = R // 2
    rb = _rows_block(h, 256)
    nr = h // rb

    def body(c_ref, qo_ref, qs_ref, w_ref, m_ref, v_ref, *rest):
        g_ref, d_ref, mo_ref, vo_ref = rest[-4:]
        own = pl.program_id(0) == c_ref[0]

        def update(q_ref):
            gg = q_ref[0].astype(F32)
            for s in range(1, 8):
                gg = gg + q_ref[s].astype(F32)
            g_ref[...] = gg
            d_ref[...], mo_ref[...], vo_ref[...] = _adam_update(gg, w_ref[...], m_ref[...], v_ref[...])

        @pl.when(own)
        def _():
            update(qo_ref)

        @pl.when(jnp.logical_not(own))
        def _():
            update(qs_ref)

    q_own_spec = pl.BlockSpec(
        (8, rb, C), lambda hh, i, c: (0, jnp.where(hh == c[0], i, jnp.where(hh < c[0], 0, nr - 1)), 0))
    q_sib_spec = pl.BlockSpec(
        (8, rb, C), lambda hh, i, c: (0, jnp.where(hh != c[0], i, jnp.where(hh < 1 - c[0], 0, nr - 1)), 0))
    wspec = pl.BlockSpec((None, rb, C), lambda hh, i, c: (l, hh * nr + i, 0))
    return pl.pallas_call(
        body, name="adamw_layer", out_shape=[jax.ShapeDtypeStruct(w.shape, F32)] * 4,
        grid_spec=pltpu.PrefetchScalarGridSpec(
            num_scalar_prefetch=1, grid=(2, nr),
            in_specs=[q_own_spec, q_sib_spec, wspec, wspec, wspec] + [ANY] * 4, out_specs=[wspec] * 4),
        input_output_aliases={6 + k: k for k in range(4)},
        compiler_params=_cp(2),
    )(cidx, q_own, q_sib, w, m, v, *bufs)


def _adamw(g, w, m, v):
    L, R, C = g.shape
    rb = _rows_block(R)

    def body(g_ref, w_ref, m_ref, v_ref, d_ref, mo_ref, vo_ref):
        d_ref[...], mo_ref[...], vo_ref[...] = _adam_update(g_ref[...], w_ref[...], m_ref[...], v_ref[...])

    spec = pl.BlockSpec((None, rb, C), lambda l, i: (l, i, 0))
    return pl.pallas_call(
        body, name="adamw", grid=(L, R // rb), in_specs=[spec] * 4, out_specs=[spec] * 3,
        out_shape=[jax.ShapeDtypeStruct(g.shape, F32)] * 3, compiler_params=_cp(2),
    )(g, w, m, v)


def _sum_slots(buf):
    def body(b_ref, o_ref):
        acc = b_ref[0]
        for k in range(1, 8):
            acc = acc + b_ref[k]
        o_ref[...] = acc

    return pl.pallas_call(body, name="sum_slots", in_specs=[VMEM], out_specs=VMEM,
                          out_shape=jax.ShapeDtypeStruct(buf.shape[1:], F32))(buf)


def _rms(xf, g):
    r = lax.rsqrt(jnp.mean(xf * xf, axis=-1, keepdims=True) + EPS)
    return xf * r, r


def _lane_chunks(n):
    lo = (n // LANES + 1) // 2 * LANES
    return ((0, lo), (lo, n - lo))


def _load_ffn_weights(win_hbm, wout_hbm, win_v, wout_v, sems):
    fb = win_v.shape[2]
    loads = [pltpu.make_async_copy(win_hbm.at[k], win_v.at[k], sems.at[k]) for k in range(4)]
    loads += [pltpu.make_async_copy(wout_hbm.at[pl.ds(k * fb, fb)], wout_v.at[pl.ds(k * fb, fb)], sems.at[4 + k])
              for k in range(2)]
    for cp in loads:
        cp.start()
    for cp in loads:
        cp.wait()


def _fast_sigmoid(v):
    return pl.reciprocal(1.0 + jnp.exp(-v), approx=True)


def _ffn_fwd(x, g, win, wout):
    T, D = x.shape
    FB = win.shape[2]
    tm = min(TM, T)

    def body(x_ref, g_ref, win_hbm, wout_hbm, xo_ref, gu_ref, win_v, wout_v, sems):
        @pl.when(pl.program_id(0) == 0)
        def _():
            _load_ffn_weights(win_hbm, wout_hbm, win_v, wout_v, sems)

        xf = x_ref[...]
        xh, _ = _rms(xf, None)
        h = (xh * g_ref[...]).astype(BF16)
        acc = jnp.zeros((tm, D), F32)
        for blk in range(2):
            for lo, sz in _lane_chunks(FB):
                cols = pl.ds(blk * FB + lo, sz)
                gate = _dot(h, win_v[blk, :, pl.ds(lo, sz)])
                up = _dot(h, win_v[2 + blk, :, pl.ds(lo, sz)])
                gu_ref[0, :, cols] = gate.astype(BF16)
                gu_ref[1, :, cols] = up.astype(BF16)
                a = (gate * _fast_sigmoid(gate) * up).astype(BF16)
                acc = acc + _dot(a, wout_v[cols, :])
        xo_ref[...] = xf + 0.5 * acc

    row = pl.BlockSpec((tm, D), lambda i: (i, 0))
    return pl.pallas_call(
        body, name="ffn_fwd", grid=(T // tm,),
        in_specs=[row, pl.BlockSpec((1, D), lambda i: (0, 0)), ANY, ANY],
        out_specs=[row, pl.BlockSpec((2, tm, 2 * FB), lambda i: (0, i, 0))],
        out_shape=[jax.ShapeDtypeStruct((T, D), F32), jax.ShapeDtypeStruct((2, T, 2 * FB), BF16)],
        scratch_shapes=[pltpu.VMEM(win.shape, BF16), pltpu.VMEM(wout.shape, BF16), pltpu.SemaphoreType.DMA((6,))],
        compiler_params=_cp(1),
    )(x, g, win, wout)


def _mixproj_fwd(x, g, wt):
    T, D = x.shape
    W = wt.shape[0]
    QKV = ATTN_W + 2 * KV_W
    tm = min(TM_MIX, T)

    def body(x_ref, g_ref, w_ref, qkv_ref, u_ref):
        xh, _ = _rms(x_ref[...], None)
        h = (xh * g_ref[...]).astype(BF16)
        qkv_ref[...] = _dot_nt(h, w_ref[:QKV, :]).astype(BF16)
        u_ref[...] = _dot_nt(h, w_ref[QKV:, :])

    return pl.pallas_call(
        body, name="mixproj_fwd", grid=(T // tm,),
        in_specs=[pl.BlockSpec((tm, D), lambda i: (i, 0)), pl.BlockSpec((1, D), lambda i: (0, 0)),
                  pl.BlockSpec((W, D), lambda i: (0, 0))],
        out_specs=[pl.BlockSpec((tm, QKV), lambda i: (i, 0)), pl.BlockSpec((tm, W - QKV), lambda i: (i, 0))],
        out_shape=[jax.ShapeDtypeStruct((T, QKV), BF16), jax.ShapeDtypeStruct((T, W - QKV), F32)],
        compiler_params=_cp(1),
    )(x, g, wt)


def _attn_bias_table():
    rows, cols = GROUP * WINDOW, 2 * WINDOW
    row = lax.broadcasted_iota(jnp.int32, (N_KV, rows, cols), 1)
    col = lax.broadcasted_iota(jnp.int32, (N_KV, rows, cols), 2)
    head = GROUP * lax.broadcasted_iota(jnp.int32, (N_KV, rows, cols), 0) + (row >> 7)
    dist = (row & (WINDOW - 1)) + WINDOW - col
    slope = jnp.exp2(-(head + 1).astype(F32))
    return jnp.where((dist >= 0) & (dist < WINDOW), -slope * dist.astype(F32), NEG_INF)


def _first_block_mask(n):
    col = lax.broadcasted_iota(jnp.int32, (GROUP * WINDOW, 2 * WINDOW), 1)
    return (n > 0) | (col >= WINDOW)


def _sink_col(sink_ref, g):
    hi = lax.broadcasted_iota(jnp.int32, (GROUP * WINDOW, 1), 0) >> 7
    col = jnp.zeros((GROUP * WINDOW, 1), F32)
    for i in range(GROUP):
        col = jnp.where(hi == i, sink_ref[0, GROUP * g + i], col)
    return col


def _stack_heads(ref, g):
    return jnp.concatenate([ref[:, (GROUP * g + i) * HEAD_DIM:(GROUP * g + i + 1) * HEAD_DIM]
                            for i in range(GROUP)], axis=0)


def _band(kvp_ref, kvc_ref, off):
    return jnp.concatenate([kvp_ref[:, off:off + HEAD_DIM], kvc_ref[:, off:off + HEAD_DIM]], axis=0)


def _attn_probs(qs, k, bias, seen, sink):
    s = jnp.where(seen, _dot_nt(qs, k) * SCALE + bias, NEG_INF)
    m = jnp.maximum(jnp.max(s, axis=-1, keepdims=True), sink)
    p = jnp.exp(s - m)
    es = jnp.exp(sink - m)
    inv = 1.0 / (jnp.sum(p, axis=-1, keepdims=True) + es)
    return p * inv, es * inv


def _attn_fwd(sinks, tab, qkv):
    T = qkv.shape[0]
    nb = T // WINDOW

    def body(sink_ref, tab_ref, q_ref, kvp_ref, kvc_ref, o_ref):
        seen = _first_block_mask(pl.program_id(0))
        for g in range(N_KV):
            qs = _stack_heads(q_ref, g)
            k = _band(kvp_ref, kvc_ref, g * HEAD_DIM)
            v = _band(kvp_ref, kvc_ref, KV_W + g * HEAD_DIM)
            p, _ = _attn_probs(qs, k, tab_ref[g], seen, _sink_col(sink_ref, g))
            o = _dot(p.astype(BF16), v)
            for i in range(GROUP):
                h = GROUP * g + i
                o_ref[:, h * HEAD_DIM:(h + 1) * HEAD_DIM] = o[i * WINDOW:(i + 1) * WINDOW].astype(BF16)

    return pl.pallas_call(
        body, name="attn_fwd", grid=(nb,),
        in_specs=[pl.BlockSpec(memory_space=pltpu.SMEM),
                  pl.BlockSpec(tab.shape, lambda n: (0, 0, 0)),
                  pl.BlockSpec((WINDOW, ATTN_W), lambda n: (n, 0)),
                  pl.BlockSpec((WINDOW, 2 * KV_W), lambda n: (jnp.maximum(n - 1, 0), 2)),
                  pl.BlockSpec((WINDOW, 2 * KV_W), lambda n: (n, 2))],
        out_specs=pl.BlockSpec((WINDOW, ATTN_W), lambda n: (n, 0)),
        out_shape=jax.ShapeDtypeStruct((T, ATTN_W), BF16),
        compiler_params=_cp(1),
    )(sinks, tab, qkv, qkv, qkv)


def _shift_copies(src_ref, dst_ref, n):
    for b in range(1, 8):
        dst_ref[b - 1] = src_ref[b:b + n, :]


def _tap(src_ref, sh_ref, s, c0):
    a, b = divmod(s, 8)
    start = pl.multiple_of(c0 + 8 * a, 8)
    if b == 0:
        return src_ref[pl.ds(start, CONV_ROWS), :]
    return sh_ref[b - 1, pl.ds(start, CONV_ROWS), :]


def _glu_rows(u, ch):
    return u[:, :ch] * _fast_sigmoid(u[:, ch:])


def _fill_z(zs_ref, zsh_ref, uc_ref, up_ref, i, ch, n):
    zs_ref[0:HALO] = jnp.where(i > 0, _glu_rows(up_ref[...], ch), 0.0)
    zs_ref[HALO:] = _glu_rows(uc_ref[...], ch)
    _shift_copies(zs_ref, zsh_ref, n - 8)


def _conv_fwd(u, w, b, lg, lb):
    T = u.shape[0]
    CH = u.shape[1] // 2
    tm = min(TM, T)
    n = tm + HALO
    hb = tm // HALO

    def body(uc_ref, up_ref, w_ref, b_ref, lg_ref, lb_ref, conv_ref, ypre_ref, zs_ref, zsh_ref):
        i = pl.program_id(0)
        _fill_z(zs_ref, zsh_ref, uc_ref, up_ref, i, CH, n)
        bias = b_ref[...]

        def chunk(ci, carry):
            c0 = pl.multiple_of(ci * CONV_ROWS, CONV_ROWS)
            acc = jnp.broadcast_to(bias, (CONV_ROWS, CH))
            for k in range(CONV_W):
                acc = acc + w_ref[k:k + 1, :] * _tap(zs_ref, zsh_ref, HALO - (CONV_W - 1) + k, c0)
            ypre_ref[pl.ds(c0, CONV_ROWS), :] = acc
            return carry

        lax.fori_loop(0, tm // CONV_ROWS, chunk, 0)
        y = ypre_ref[...]
        mu = jnp.mean(y, axis=-1, keepdims=True)
        d = y - mu
        var = jnp.mean(d * d, axis=-1, keepdims=True)
        o = d * lax.rsqrt(var + EPS) * lg_ref[...] + lb_ref[...]
        conv_ref[...] = (o * _fast_sigmoid(o)).astype(BF16)

    vec = pl.BlockSpec((1, CH), lambda i: (0, 0))
    return pl.pallas_call(
        body, name="conv_fwd", grid=(T // tm,),
        in_specs=[pl.BlockSpec((tm, 2 * CH), lambda i: (i, 0)),
                  pl.BlockSpec((HALO, 2 * CH), lambda i: (jnp.maximum(i * hb - 1, 0), 0)),
                  pl.BlockSpec((CONV_W, CH), lambda i: (0, 0)), vec, vec, vec],
        out_specs=[pl.BlockSpec((tm, CH), lambda i: (i, 0)), pl.BlockSpec((tm, CH), lambda i: (i, 0))],
        out_shape=[jax.ShapeDtypeStruct((T, CH), BF16), jax.ShapeDtypeStruct((T, CH), F32)],
        scratch_shapes=[pltpu.VMEM((n, CH), F32), pltpu.VMEM((7, n - 8, CH), F32)],
        compiler_params=_cp(1),
    )(u, u, w, b, lg, lb)


def _mixout_fwd(x, attn, conv, wo):
    T, D = x.shape
    tm = min(TM_MIX, T)
    A = attn.shape[1]

    def body(x_ref, a_ref, c_ref, w_ref, xo_ref):
        xo_ref[...] = x_ref[...] + _dot(a_ref[...], w_ref[:A, :]) + _dot(c_ref[...], w_ref[A:, :])

    return pl.pallas_call(
        body, name="mixout_fwd", grid=(T // tm,),
        in_specs=[pl.BlockSpec((tm, D), lambda i: (i, 0)), pl.BlockSpec((tm, A), lambda i: (i, 0)),
                  pl.BlockSpec((tm, conv.shape[1]), lambda i: (i, 0)), pl.BlockSpec(wo.shape, lambda i: (0, 0))],
        out_specs=pl.BlockSpec((tm, D), lambda i: (i, 0)),
        out_shape=jax.ShapeDtypeStruct((T, D), F32),
        compiler_params=_cp(1),
    )(x, attn, conv, wo)


def _rms_bwd_rows(dh, xf, g):
    xh, r = _rms(xf, None)
    dxn = dh * g
    dx = r * (dxn - xh * jnp.mean(dxn * xh, axis=-1, keepdims=True))
    return dx, jnp.sum(dh * xh, axis=0, keepdims=True), xh * g


def _loss_head(x, g, tgt):
    T, D = x.shape
    tm = min(TM, T)

    def body(x_ref, g_ref, t_ref, loss_ref, dx_ref, dg_ref):
        @pl.when(pl.program_id(0) == 0)
        def _():
            loss_ref[...] = jnp.zeros_like(loss_ref)
            dg_ref[...] = jnp.zeros_like(dg_ref)

        xf = x_ref[...]
        g = g_ref[...]
        xh, _ = _rms(xf, None)
        e = xh * g - t_ref[...]
        loss_ref[...] += 0.5 * jnp.sum(jnp.mean(e * e, axis=-1, keepdims=True), axis=0, keepdims=True)
        dx, dg, _ = _rms_bwd_rows(e * (1.0 / D), xf, g)
        dx_ref[...] = dx
        dg_ref[...] += dg

    return pl.pallas_call(
        body, name="loss_head", grid=(T // tm,),
        in_specs=[pl.BlockSpec((tm, D), lambda i: (i, 0)), pl.BlockSpec((1, D), lambda i: (0, 0)),
                  pl.BlockSpec((tm, D), lambda i: (i, 0))],
        out_specs=[pl.BlockSpec((1, 1), lambda i: (0, 0)), pl.BlockSpec((tm, D), lambda i: (i, 0)),
                   pl.BlockSpec((1, D), lambda i: (0, 0))],
        out_shape=[jax.ShapeDtypeStruct((1, 1), F32), jax.ShapeDtypeStruct((T, D), F32),
                   jax.ShapeDtypeStruct((1, D), F32)],
        compiler_params=_cp(1),
    )(x, g, tgt)


def _ffn_bwd(dxo, x, g, gu, win, wout, dep):
    T, D = x.shape
    FB = win.shape[2]
    tm = min(TM_FFN_BWD, T)

    def body(dxo_ref, x_ref, g_ref, gu_ref, win_hbm, wout_hbm, dep_ref,
             dxi_ref, dg_ref, hb_ref, dgu_ref, a_ref, dyb_ref, win_v, wout_v, sems):
        @pl.when(pl.program_id(0) == 0)
        def _():
            _load_ffn_weights(win_hbm, wout_hbm, win_v, wout_v, sems)
            dg_ref[...] = jnp.zeros_like(dg_ref)

        dyb = (0.5 * dxo_ref[...]).astype(BF16)
        dyb_ref[...] = dyb
        dh = jnp.zeros((tm, D), F32)
        for blk in range(2):
            cols = pl.ds(blk * FB, FB)
            da = _dot_nt(dyb, wout_v[cols, :])
            gate = gu_ref[0, :, cols].astype(F32)
            up = gu_ref[1, :, cols].astype(F32)
            sg = _fast_sigmoid(gate)
            s = gate * sg
            a_ref[:, cols] = (s * up).astype(BF16)
            dgate = (da * up * (sg + s * (1.0 - sg))).astype(BF16)
            dup = (da * s).astype(BF16)
            dgu_ref[0, :, cols] = dgate
            dgu_ref[1, :, cols] = dup
            dh = dh + _dot_nt(dgate, win_v[blk]) + _dot_nt(dup, win_v[2 + blk])
        dx, dg, h = _rms_bwd_rows(dh, x_ref[...], g_ref[...])
        dxi_ref[...] = dxo_ref[...] + dx
        dg_ref[...] += dg
        hb_ref[...] = h.astype(BF16)

    row = pl.BlockSpec((tm, D), lambda i: (i, 0))
    act = pl.BlockSpec((2, tm, 2 * FB), lambda i: (0, i, 0))
    return pl.pallas_call(
        body, name="ffn_bwd", grid=(T // tm,),
        in_specs=[row, row, pl.BlockSpec((1, D), lambda i: (0, 0)), act, ANY, ANY, ANY],
        out_specs=[row, pl.BlockSpec((1, D), lambda i: (0, 0)), row, act,
                   pl.BlockSpec((tm, 2 * FB), lambda i: (i, 0)), row],
        out_shape=[jax.ShapeDtypeStruct((T, D), F32), jax.ShapeDtypeStruct((1, D), F32),
                   jax.ShapeDtypeStruct((T, D), BF16), jax.ShapeDtypeStruct((2, T, 2 * FB), BF16),
                   jax.ShapeDtypeStruct((T, 2 * FB), BF16), jax.ShapeDtypeStruct((T, D), BF16)],
        scratch_shapes=[pltpu.VMEM(win.shape, BF16), pltpu.VMEM(wout.shape, BF16), pltpu.SemaphoreType.DMA((6,))],
        compiler_params=_cp(1),
    )(dxo, x, g, gu, win, wout, dep)


def _mix_rms_bwd(dxo, x, g, dzs, wt):
    T, D = x.shape
    tm = min(TM, T)
    npair = len(dzs)

    def body(*refs):
        dxo_ref, x_ref, g_ref = refs[:3]
        dz_refs, w_ref = refs[3:3 + npair], refs[3 + npair]
        dxi_ref, dg_ref, hb_ref = refs[4 + npair:]

        @pl.when(pl.program_id(0) == 0)
        def _():
            dg_ref[...] = jnp.zeros_like(dg_ref)

        dh = jnp.zeros((tm, D), F32)
        k0 = 0
        for dz_ref in dz_refs:
            kp = dz_ref.shape[1]
            dh = dh + _dot(dz_ref[...], w_ref[k0:k0 + kp, :])
            k0 += kp
        dx, dg, h = _rms_bwd_rows(dh, x_ref[...], g_ref[...])
        dxi_ref[...] = dxo_ref[...] + dx
        dg_ref[...] += dg
        hb_ref[...] = h.astype(BF16)

    row = pl.BlockSpec((tm, D), lambda i: (i, 0))
    return pl.pallas_call(
        body, name="mix_rms_bwd", grid=(T // tm,),
        in_specs=[row, row, pl.BlockSpec((1, D), lambda i: (0, 0))]
                 + [pl.BlockSpec((tm, dz.shape[1]), lambda i: (i, 0)) for dz in dzs]
                 + [pl.BlockSpec(wt.shape, lambda i: (0, 0))],
        out_specs=[row, pl.BlockSpec((1, D), lambda i: (0, 0)), row],
        out_shape=[jax.ShapeDtypeStruct((T, D), F32), jax.ShapeDtypeStruct((1, D), F32),
                   jax.ShapeDtypeStruct((T, D), BF16)],
        compiler_params=_cp(1),
    )(dxo, x, g, *dzs, wt)


def _wgrad(name, a, b, a_spec, b_spec, out_shape, out_spec, nblk, dep, acc_shape):
    T = a.shape[0]
    tk = min(TK_WGRAD, T)
    nk = T // tk

    def body(a_ref, b_ref, dep_ref, o_ref, acc_ref):
        k = pl.program_id(1)

        @pl.when(k == 0)
        def _():
            acc_ref[...] = jnp.zeros_like(acc_ref)

        acc_ref[...] += _dot_tn(a_ref[...], b_ref[...])

        @pl.when(k == nk - 1)
        def _():
            o_ref[...] = acc_ref[...].reshape(o_ref.shape).astype(BF16)

    return pl.pallas_call(
        body, name=name, grid=(nblk, nk), in_specs=[a_spec, b_spec, ANY], out_specs=out_spec,
        out_shape=jax.ShapeDtypeStruct(out_shape, BF16), scratch_shapes=[pltpu.VMEM(acc_shape, F32)],
        compiler_params=_cp(2),
    )(a, b, dep)


def _back_and_forth(nk):
    return lambda b, k: jnp.where(b % 2 == 0, k, nk - 1 - k)


def _wgrad_ffn_in(hb, dgu, dep):
    T, D = hb.shape
    FB = dgu.shape[2] // 2
    tk = min(TK_WGRAD, T)
    chunk = _back_and_forth(T // tk)
    return _wgrad("wgrad_ffn_in", hb, dgu,
                  pl.BlockSpec((tk, D), lambda b, k: (chunk(b, k), 0)),
                  pl.BlockSpec((None, tk, FB), lambda b, k: (b // 2, chunk(b, k), b % 2)),
                  (4, D, FB), pl.BlockSpec((None, D, FB), lambda b, k: (b, 0, 0)), 4, dep, (D, FB))


def _wgrad_ffn_out(a, dyb, dep):
    T, D = dyb.shape
    FB = a.shape[1] // 2
    tk = min(TK_WGRAD, T)
    chunk = _back_and_forth(T // tk)
    return _wgrad("wgrad_ffn_out", a, dyb,
                  pl.BlockSpec((tk, FB), lambda b, k: (chunk(b, k), b)),
                  pl.BlockSpec((tk, D), lambda b, k: (chunk(b, k), 0)),
                  (4, FB // 2, D), pl.BlockSpec((2, FB // 2, D), lambda b, k: (b, 0, 0)), 2, dep, (FB, D))


def _wgrad_cat(a_list, b_list):
    T = a_list[0].shape[0]
    tk = min(TK_WGRAD, T)
    nk = T // tk
    na = len(a_list)
    M, N = sum(a.shape[1] for a in a_list), sum(b.shape[1] for b in b_list)

    def body(*refs):
        a_refs, b_refs, o_ref, acc_ref = refs[:na], refs[na:-2], refs[-2], refs[-1]
        k = pl.program_id(0)

        @pl.when(k == 0)
        def _():
            acc_ref[...] = jnp.zeros_like(acc_ref)

        r0 = 0
        for a_ref in a_refs:
            c0 = 0
            for b_ref in b_refs:
                m, n = a_ref.shape[1], b_ref.shape[1]
                acc_ref[r0:r0 + m, c0:c0 + n] += _dot_tn(a_ref[...], b_ref[...])
                c0 += n
            r0 += a_ref.shape[1]

        @pl.when(k == nk - 1)
        def _():
            o_ref[...] = acc_ref[...].astype(BF16)

    return pl.pallas_call(
        body, name="wgrad_cat", grid=(nk,),
        in_specs=[pl.BlockSpec((tk, v.shape[1]), lambda k: (k, 0)) for v in list(a_list) + list(b_list)],
        out_specs=pl.BlockSpec((M, N), lambda k: (0, 0)),
        out_shape=jax.ShapeDtypeStruct((M, N), BF16), scratch_shapes=[pltpu.VMEM((M, N), F32)],
        compiler_params=_cp(1),
    )(*a_list, *b_list)


def _mixout_bwd(dxo, wo):
    T, D = dxo.shape
    tm = min(TM_MIX, T)
    A = ATTN_W
    C = wo.shape[0] - A

    def body(dxo_ref, w_ref, dyb_ref, da_ref, dc_ref):
        dyb = dxo_ref[...].astype(BF16)
        dyb_ref[...] = dyb
        da_ref[...] = _dot_nt(dyb, w_ref[:A, :]).astype(BF16)
        dc_ref[...] = _dot_nt(dyb, w_ref[A:, :])

    return pl.pallas_call(
        body, name="mixout_bwd", grid=(T // tm,),
        in_specs=[pl.BlockSpec((tm, D), lambda i: (i, 0)), pl.BlockSpec(wo.shape, lambda i: (0, 0))],
        out_specs=[pl.BlockSpec((tm, D), lambda i: (i, 0)), pl.BlockSpec((tm, A), lambda i: (i, 0)),
                   pl.BlockSpec((tm, C), lambda i: (i, 0))],
        out_shape=[jax.ShapeDtypeStruct((T, D), BF16), jax.ShapeDtypeStruct((T, A), BF16),
                   jax.ShapeDtypeStruct((T, C), F32)],
        compiler_params=_cp(1),
    )(dxo, wo)


def _conv_bwd(dconv, ypre, u, w, lg, lb):
    T, CH = dconv.shape
    tm = min(TM, T)
    n = tm + HALO
    hb = tm // HALO
    nt = T // tm
    nchunk = tm // CONV_ROWS

    def body(dc_ref, dcn_ref, yp_ref, ypn_ref, uc_ref, up_ref, w_ref, lg_ref, lb_ref,
             du_ref, dw_ref, dvec_ref, zs_ref, zsh_ref, dy_ref, dysh_ref, dz_ref, dwacc_ref):
        i = pl.program_id(0)

        @pl.when(i == 0)
        def _():
            dwacc_ref[...] = jnp.zeros_like(dwacc_ref)
            dvec_ref[...] = jnp.zeros_like(dvec_ref)

        g, bb = lg_ref[...], lb_ref[...]

        def ln_bwd(dc, yp):
            mu = jnp.mean(yp, axis=-1, keepdims=True)
            d = yp - mu
            rs = lax.rsqrt(jnp.mean(d * d, axis=-1, keepdims=True) + EPS)
            yn = d * rs
            o = yn * g + bb
            sg = _fast_sigmoid(o)
            do = dc * (sg * (1.0 + o * (1.0 - sg)))
            dyn = do * g
            dyp = rs * (dyn - jnp.mean(dyn, axis=-1, keepdims=True)
                        - yn * jnp.mean(dyn * yn, axis=-1, keepdims=True))
            return dyp, do, yn

        dyp, do, yn = ln_bwd(dc_ref[...], yp_ref[...])
        dvec_ref[0:1, :] += jnp.sum(dyp, axis=0, keepdims=True)
        dvec_ref[1:2, :] += jnp.sum(do * yn, axis=0, keepdims=True)
        dvec_ref[2:3, :] += jnp.sum(do, axis=0, keepdims=True)
        dy_ref[0:tm] = dyp
        dyh, _, _ = ln_bwd(dcn_ref[...], ypn_ref[...])
        dy_ref[tm:] = jnp.where(i < nt - 1, dyh, 0.0)
        _shift_copies(dy_ref, dysh_ref, n - 8)
        _fill_z(zs_ref, zsh_ref, uc_ref, up_ref, i, CH, n)

        def chunk(ci, carry):
            c0 = pl.multiple_of(ci * CONV_ROWS, CONV_ROWS)
            acc = jnp.zeros((CONV_ROWS, CH), F32)
            for k in range(CONV_W):
                acc = acc + w_ref[k:k + 1, :] * _tap(dy_ref, dysh_ref, CONV_W - 1 - k, c0)
            dz_ref[pl.ds(c0, CONV_ROWS), :] = acc
            dyc = dy_ref[pl.ds(c0, CONV_ROWS), :]
            for k in range(CONV_W):
                prod = dyc * _tap(zs_ref, zsh_ref, HALO - (CONV_W - 1) + k, c0)
                dwacc_ref[k] += jnp.sum(prod.reshape(CONV_ROWS // 8, 8, CH), axis=0)
            return carry

        lax.fori_loop(0, nchunk, chunk, 0)

        @pl.when(i == nt - 1)
        def _():
            dw_ref[...] = jnp.sum(dwacc_ref[...], axis=1)

        uc = uc_ref[...]
        a = uc[:, :CH]
        sg = _fast_sigmoid(uc[:, CH:])
        dz = dz_ref[...]
        du_ref[:, :CH] = (dz * sg).astype(BF16)
        du_ref[:, CH:] = (dz * a * sg * (1.0 - sg)).astype(BF16)

    cur = lambda c: pl.BlockSpec((tm, c), lambda i: (i, 0))
    nxt = lambda c: pl.BlockSpec((HALO, c), lambda i: (jnp.minimum((i + 1) * hb, T // HALO - 1), 0))
    vec = pl.BlockSpec((1, CH), lambda i: (0, 0))
    return pl.pallas_call(
        body, name="conv_bwd", grid=(nt,),
        in_specs=[cur(CH), nxt(CH), cur(CH), nxt(CH), cur(2 * CH),
                  pl.BlockSpec((HALO, 2 * CH), lambda i: (jnp.maximum(i * hb - 1, 0), 0)),
                  pl.BlockSpec((CONV_W, CH), lambda i: (0, 0)), vec, vec],
        out_specs=[pl.BlockSpec((tm, 2 * CH), lambda i: (i, 0)), pl.BlockSpec((32, CH), lambda i: (0, 0)),
                   pl.BlockSpec((8, CH), lambda i: (0, 0))],
        out_shape=[jax.ShapeDtypeStruct((T, 2 * CH), BF16), jax.ShapeDtypeStruct((32, CH), F32),
                   jax.ShapeDtypeStruct((8, CH), F32)],
        scratch_shapes=[pltpu.VMEM((n, CH), F32), pltpu.VMEM((7, n - 8, CH), F32),
                        pltpu.VMEM((n, CH), F32), pltpu.VMEM((7, n - 8, CH), F32), pltpu.VMEM((tm, CH), F32),
                        pltpu.VMEM((32, 8, CH), F32)],
        compiler_params=_cp(1),
    )(dconv, dconv, ypre, ypre, u, u, w, lg, lb)


def _attn_bwd(sinks, tab, qkv, dattn):
    T = qkv.shape[0]
    nb = T // WINDOW

    def body(sink_ref, tab_ref, q_ref, kvp_ref, kvc_ref, do_ref, dq_ref, dkv_ref, dsk_ref, carry_ref):
        n = pl.program_id(0)

        @pl.when(n == 0)
        def _():
            dsk_ref[...] = jnp.zeros_like(dsk_ref)
            carry_ref[...] = jnp.zeros_like(carry_ref)

        @pl.when(n < nb)
        def _():
            seen = _first_block_mask(n)
            for g in range(N_KV):
                qs = _stack_heads(q_ref, g)
                dos = _stack_heads(do_ref, g)
                k = _band(kvp_ref, kvc_ref, g * HEAD_DIM)
                v = _band(kvp_ref, kvc_ref, KV_W + g * HEAD_DIM)
                p, ps = _attn_probs(qs, k, tab_ref[g], seen, _sink_col(sink_ref, g))
                dp = _dot_nt(dos, v)
                delta = jnp.sum(p * dp, axis=-1, keepdims=True)
                dsb = (p * (dp - delta)).astype(BF16)
                dsink = -ps * delta
                dqs = _dot(dsb, k) * SCALE
                dk = _dot_tn(dsb, qs) * SCALE
                dv = _dot_tn(p.astype(BF16), dos)
                for i in range(GROUP):
                    h = GROUP * g + i
                    dq_ref[:, h * HEAD_DIM:(h + 1) * HEAD_DIM] = dqs[i * WINDOW:(i + 1) * WINDOW].astype(BF16)
                    dsk_ref[h:h + 1, :] += jnp.sum(dsink[i * WINDOW:(i + 1) * WINDOW], axis=0, keepdims=True)
                for off, d in ((g * HEAD_DIM, dk), (KV_W + g * HEAD_DIM, dv)):
                    dkv_ref[:, off:off + HEAD_DIM] = (carry_ref[:, off:off + HEAD_DIM] + d[:WINDOW]).astype(BF16)
                    carry_ref[:, off:off + HEAD_DIM] = d[WINDOW:]

        @pl.when(n == nb)
        def _():
            dkv_ref[...] = carry_ref[...].astype(BF16)

    last = nb - 1
    return pl.pallas_call(
        body, name="attn_bwd", grid=(nb + 1,),
        in_specs=[pl.BlockSpec(memory_space=pltpu.SMEM),
                  pl.BlockSpec(tab.shape, lambda n: (0, 0, 0)),
                  pl.BlockSpec((WINDOW, ATTN_W), lambda n: (jnp.minimum(n, last), 0)),
                  pl.BlockSpec((WINDOW, 2 * KV_W), lambda n: (jnp.clip(n - 1, 0, last), 2)),
                  pl.BlockSpec((WINDOW, 2 * KV_W), lambda n: (jnp.minimum(n, last), 2)),
                  pl.BlockSpec((WINDOW, ATTN_W), lambda n: (jnp.minimum(n, last), 0))],
        out_specs=[pl.BlockSpec((WINDOW, ATTN_W), lambda n: (jnp.minimum(n, last), 0)),
                   pl.BlockSpec((WINDOW, 2 * KV_W), lambda n: (jnp.maximum(n - 1, 0), 0)),
                   pl.BlockSpec((8, LANES), lambda n: (0, 0))],
        out_shape=[jax.ShapeDtypeStruct((T, ATTN_W), BF16), jax.ShapeDtypeStruct((T, 2 * KV_W), BF16),
                   jax.ShapeDtypeStruct((8, LANES), F32)],
        scratch_shapes=[pltpu.VMEM((WINDOW, 2 * KV_W), F32)],
        compiler_params=_cp(1),
    )(sinks, tab, qkv, qkv, qkv, dattn)


def _pack(arrs):
    flat = jnp.concatenate([a.reshape(-1) for a in arrs])
    pad = -flat.shape[0] % (8 * LANES)
    return jnp.pad(flat, (0, pad)).reshape(1, -1, LANES)


def _unpack(packed, like):
    flat = packed.reshape(-1)
    out, off = [], 0
    for a in like:
        out.append(flat[off:off + a.size].reshape(a.shape))
        off += a.size
    return out


def kernel(x, norm_ffn1, w_ffn1_in, w_ffn1_out, norm_mix, w_in, sinks, w_dw, b_dw, conv_ln_g, conv_ln_b, w_out, norm_ffn2, w_ffn2_in, w_ffn2_out, final_norm, loss_target, m_norm_ffn1, m_w_ffn1_in, m_w_ffn1_out, m_norm_mix, m_w_in, m_sinks, m_w_dw, m_b_dw, m_conv_ln_g, m_conv_ln_b, m_w_out, m_norm_ffn2, m_w_ffn2_in, m_w_ffn2_out, m_final_norm, v_norm_ffn1, v_w_ffn1_in, v_w_ffn1_out, v_norm_mix, v_w_in, v_sinks, v_w_dw, v_b_dw, v_conv_ln_g, v_conv_ln_b, v_w_out, v_norm_ffn2, v_w_ffn2_in, v_w_ffn2_out, v_final_norm):
    L, D = norm_ffn1.shape
    T = x.shape[1]
    FB = w_ffn1_in.shape[2]
    CH = b_dw.shape[1]
    QKV = ATTN_W + 2 * KV_W
    xs = x.reshape(T, D)
    tgt = loss_target.reshape(T, D)
    cx, cy, cc = lax.axis_index("x"), lax.axis_index("y"), lax.axis_index("c")
    chip = 2 * cx + cy
    cidx = cc.reshape(1).astype(jnp.int32)
    tr = lambda a_: jnp.transpose(a_, (0, 2, 1))
    big_w = (w_ffn1_in, w_ffn1_out, tr(w_in), w_out, w_ffn2_in, w_ffn2_out)
    big_m = (m_w_ffn1_in, m_w_ffn1_out, tr(m_w_in), m_w_out, m_w_ffn2_in, m_w_ffn2_out)
    big_v = (v_w_ffn1_in, v_w_ffn1_out, tr(v_w_in), v_w_out, v_w_ffn2_in, v_w_ffn2_out)
    NW = len(big_w) + 1

    def own_slot(a, slots=4, idx=chip):
        return lax.dynamic_update_index_in_dim(lax.empty((slots,) + a.shape, a.dtype), a, idx, 0)

    def shards(l, tok):
        return [own_slot((w_[l] + tok[0, 0]).astype(BF16)) for w_ in big_w] + [own_slot(w_dw[l] + tok[0, 0])]

    def gather_start(lands, tok):
        return _xchg_start("gather_start", [], lands, _gather_plan, tok)

    def gather_arrived(started, after, n, taps):
        _, lands, tok = _xchg_wait("gather_wait", started, 0, n, _gather_plan, after)
        return _xchg_start("gshare_start", [], lands[:-1] if taps else lands, _gshare_plan, tok, "sibling3"), lands[-1]

    def shared_weights(shared, after, n):
        _, mats, tok = _xchg_wait("gshare_wait", shared, 0, n, _gshare_plan, after, "sibling3")
        return mats, tok

    row = lambda a, l: a[l].reshape(1, -1)
    tab = _attn_bias_table()
    NB = len(big_w)

    saved, W = [], []
    zero_tok = jnp.zeros((8, LANES), F32)
    src0 = shards(0, zero_tok)
    started = gather_start(src0[:2], zero_tok)
    rest0 = gather_start(src0[2:], started[-1])
    cast = [None] + [shards(l, rest0[-1]) for l in range(1, L)]
    shared, _ = gather_arrived(started, [xs] + [a_ for c_ in cast[1:] for a_ in c_], 2, False)
    after = [shared[-1]]
    for l in range(L):
        mats, tok = shared_weights(shared, after, 2 if l == 0 else NB)
        started = None
        if l + 1 < L:
            started = gather_start(cast[l + 1], tok)
            tok = started[-1]
        x0 = xs
        x1, gu1 = _ffn_fwd(x0, row(norm_ffn1, l) + tok[0, 0], mats[0], mats[1].reshape(2 * FB, D))
        gm_row = row(norm_mix, l)
        if l == 0:
            shared, gdw = gather_arrived(rest0, [x1], NW - 2, True)
            rest, tok = shared_weights(shared, [shared[-1]], NB - 2)
            mats = list(mats) + list(rest)
            gm_row = gm_row + tok[0, 0]
        g1i, g1o, gi, go, g2i, g2o = mats
        w = dict(f1i=g1i, f1o=g1o.reshape(2 * FB, D), f2i=g2i, f2o=g2o.reshape(2 * FB, D),
                 wit=gi.reshape(-1, D), wo=go.reshape(-1, D),
                 wdw=jnp.transpose(gdw, (1, 0, 2)).reshape(CONV_W, CH))
        W.append(w)
        qkv, u = _mixproj_fwd(x1, gm_row, w["wit"])
        attn = _attn_fwd(row(sinks, l), tab, qkv)
        conv, ypre = _conv_fwd(u, w["wdw"], row(b_dw, l), row(conv_ln_g, l), row(conv_ln_b, l))
        x2 = _mixout_fwd(x1, attn, conv, w["wo"])
        g2_row = row(norm_ffn2, l)
        if started is not None and l > 0:
            shared, gdw = gather_arrived(started, [x2], NW, True)
            g2_row = g2_row + shared[-1][0, 0]
        xs, gu2 = _ffn_fwd(x2, g2_row, w["f2i"], w["f2o"])
        if started is not None and l == 0:
            shared, gdw = gather_arrived(started, [xs], NW, True)
        saved.append((x0, gu1, x1, qkv, u, attn, conv, ypre, x2, gu2))
        after = [xs]

    loss_part, dx, d_final = _loss_head(xs, final_norm.reshape(1, D), tgt)
    loss = lax.psum(loss_part[0, 0], ("x", "y", "c"))

    bufs = [[lax.empty(w_.shape, F32) for _ in range(4)] for w_ in big_w]
    d_n1, d_nm, d_n2 = [None] * L, [None] * L, [None] * L
    d_sk, d_bdw, d_lg, d_lb, d_wdw = [None] * L, [None] * L, [None] * L, [None] * L, [None] * L

    me_idx = 4 * cx + 2 * cy + cc

    def reduce_start(gs):
        lands = []
        for g in gs:
            h = g.shape[1] // 2
            mine = lax.dynamic_slice(g, (chip, cc * h, 0), (1, h, g.shape[2]))[0]
            lands.append(own_slot(mine, 8, me_idx))
        return _xchg_start("rs_start", gs, lands, _rs_plan, zero_tok, "all")

    def share_start(rs_started, after, n):
        _, qs, tok = _xchg_wait("rs_wait", rs_started, n, n, _rs_plan, after, "all")
        return _xchg_start("qshare_start", qs, [lax.empty(q.shape, q.dtype) for q in qs], _whole_plan, tok, "sibling")

    def finish(l, shared, after, idxs):
        q_own, q_sib, _ = _xchg_wait("qshare_wait", shared, len(idxs), len(idxs), _whole_plan, after, "sibling")
        for k, t in enumerate(idxs):
            bufs[t] = _adamw_layer(cidx, q_own[k], q_sib[k], big_w[t], big_m[t], big_v[t], bufs[t], l)

    ALL = list(range(NB))
    EARLY, LATE = ALL[2:], ALL[:2]
    rs_list, shares = [], []
    tok = zero_tok
    for l in reversed(range(L)):
        w = W[l]
        x0, gu1, x1, qkv, u, attn, conv, ypre, x2, gu2 = saved[l]
        dx, d_n2[l], hb, dgu, a, dyb = _ffn_bwd(dx, x2, row(norm_ffn2, l), gu2, w["f2i"], w["f2o"], tok)
        g_f2i, g_f2o = _wgrad_ffn_in(hb, dgu, tok), _wgrad_ffn_out(a, dyb, tok)
        lg_row = row(conv_ln_g, l)
        if len(rs_list) >= 2:
            pl_, st_ = rs_list[-2]
            shares.append((pl_, share_start(st_, [g_f2o], NB)))
            lg_row = lg_row + shares[-1][1][-1][0, 0]
        dyb, dattn, dconv = _mixout_bwd(dx, w["wo"])
        g_wo = _wgrad_cat([attn, conv], [dyb]).reshape(4, -1, D)
        du, dwdw, dvec = _conv_bwd(dconv, ypre, u, w["wdw"], lg_row, row(conv_ln_b, l))
        d_wdw[l], d_bdw[l], d_lg[l], d_lb[l] = dwdw[:CONV_W], dvec[0], dvec[1], dvec[2]
        dq, dkv, dsk = _attn_bwd(row(sinks, l), tab, qkv, dattn)
        d_sk[l] = dsk[:, 0]
        dx, d_nm[l], hb = _mix_rms_bwd(dx, x1, row(norm_mix, l), [dq, dkv, du], w["wit"])
        g_wi = _wgrad_cat([dq, dkv, du], [hb]).reshape(4, -1, D)
        if l == 0:
            rs_early = reduce_start([g_wi, g_wo, g_f2i, g_f2o])
            tok = rs_early[-1]
        dx, d_n1[l], hb, dgu, a, dyb = _ffn_bwd(dx, x0, row(norm_ffn1, l), gu1, w["f1i"], w["f1o"], tok)
        g_f1i, g_f1o = _wgrad_ffn_in(hb, dgu, tok), _wgrad_ffn_out(a, dyb, tok)
        rs_started = reduce_start([g_f1i, g_f1o] if l == 0 else [g_f1i, g_f1o, g_wi, g_wo, g_f2i, g_f2o])
        tok = rs_started[-1]
        rs_list.append((l, rs_started))
    grad_x = dx.reshape(x.shape)

    small_g = [jnp.concatenate(d, axis=0) for d in (d_n1, d_nm, d_n2)] + [d_final, jnp.stack(d_sk)] + \
              [jnp.stack(d) for d in (d_bdw, d_lg, d_lb, d_wdw)]
    packed = _pack(small_g)[0]
    small_started = _xchg_start("small_start", [packed], [own_slot(packed, 8, 4 * cx + 2 * cy + cc)], _slot_plan, tok, "all")

    rs_late = rs_list.pop()[1]
    after = [small_started[-1]]
    if len(rs_list) > len(shares):
        pl_, st_ = rs_list[len(shares)]
        shares.append((pl_, share_start(st_, after, NB)))
        after = [shares[-1][1][-1]]
    if shares:
        finish(*shares.pop(0), after, ALL)
        after = [b_[0] for b_ in bufs]
    sh_early = share_start(rs_early, after, len(EARLY))
    after = [sh_early[-1]]
    sh_late = None
    for l, sh in shares:
        finish(l, sh, after, ALL)
        after = [b_[0] for b_ in bufs]
        if sh_late is None:
            sh_late = share_start(rs_late, after, len(LATE))
            after = [sh_late[-1]]
    if sh_late is None:
        sh_late = share_start(rs_late, after, len(LATE))
        after = [sh_late[-1]]
    _, (slots,), _ = _xchg_wait("small_wait", small_started, 1, 1, _slot_plan, after, "all")
    small_sum = _unpack(_sum_slots(slots), small_g)
    g_wdw = lax.dynamic_slice_in_dim(small_sum[8], chip * w_dw.shape[2], w_dw.shape[2], axis=2)
    small_g = [small_sum[0], small_sum[1], small_sum[2], small_sum[3].reshape(D), small_sum[4],
               small_sum[5], small_sum[6], small_sum[7], g_wdw]
    small_w = (norm_ffn1, norm_mix, norm_ffn2, final_norm, sinks, b_dw, conv_ln_g, conv_ln_b, w_dw)
    small_m = (m_norm_ffn1, m_norm_mix, m_norm_ffn2, m_final_norm, m_sinks, m_b_dw, m_conv_ln_g, m_conv_ln_b, m_w_dw)
    small_v = (v_norm_ffn1, v_norm_mix, v_norm_ffn2, v_final_norm, v_sinks, v_b_dw, v_conv_ln_g, v_conv_ln_b, v_w_dw)
    upd = _adamw(_pack(small_g), _pack(small_w), _pack(small_m), _pack(small_v))
    small_upd = [_unpack(u_, small_w) for u_ in upd]
    finish(0, sh_early, [upd[0]], EARLY)
    finish(0, sh_late, [bufs[t][0] for t in EARLY], LATE)

    order = ("norm_ffn1", "w_ffn1_in", "w_ffn1_out", "norm_mix", "w_in", "sinks", "w_dw", "b_dw", "conv_ln_g",
             "conv_ln_b", "w_out", "norm_ffn2", "w_ffn2_in", "w_ffn2_out", "final_norm")
    small_names = ("norm_ffn1", "norm_mix", "norm_ffn2", "final_norm", "sinks", "b_dw", "conv_ln_g", "conv_ln_b", "w_dw")
    big_names = ("w_ffn1_in", "w_ffn1_out", "w_in", "w_out", "w_ffn2_in", "w_ffn2_out")
    grads, deltas, new_m, new_v = {}, {}, {}, {}
    for i, nme in enumerate(small_names):
        grads[nme], deltas[nme], new_m[nme], new_v[nme] = small_g[i], small_upd[0][i], small_upd[1][i], small_upd[2][i]
    for i, nme in enumerate(big_names):
        grads[nme], deltas[nme], new_m[nme], new_v[nme] = [tr(b_) for b_ in bufs[i]] if nme == "w_in" else bufs[i]
    return (loss, grad_x, *[grads[n] for n in order], *[deltas[n] for n in order],
            *[new_m[n] for n in order], *[new_v[n] for n in order])
```
